```python
import jax
import jax.numpy as jnp
from jax import lax
import numpy as np


D_MODEL = 1024
BATCH = 8
SEQ = 4096
DEPTH = 2

MEM_LEN = 256
A_HEADS = 4
A_DIM = 128
B_HEADS = 4
B_DK = 128
B_DV = 128
CONV_W = 4
CHUNK = 64
C_HEADS = 8
C_DIM = D_MODEL // C_HEADS
Q_BLOCK = 128
X_HEADS = 4
X_DIM = D_MODEL // X_HEADS
D_FF = 4 * D_MODEL
EPS = 1e-6
N_EVEN = (DEPTH + 1) // 2
N_ODD = DEPTH // 2
A_W = A_HEADS * A_DIM
B_KW = B_HEADS * B_DK
B_VW = B_HEADS * B_DV
AB_SIZES = (A_W, A_W, A_W, A_W, A_HEADS, A_HEADS, B_KW, B_KW, B_VW, B_VW)
AB_IN = 4 * A_W + 2 * A_HEADS + 2 * B_KW + 2 * B_VW
C_SIZES = (D_MODEL, D_MODEL, D_MODEL, D_MODEL, C_HEADS)
C_IN = 4 * D_MODEL + C_HEADS

kernel_name = 'hybrid_mlstm_hgrn2_fox_trunk'


def _split(z, sizes):
    offs = [int(o) for o in np.cumsum(sizes)[:-1]]
    return jnp.split(z, offs, axis=-1)


def rmsnorm(x, g):
    xf = x.astype(jnp.float32)
    y = xf * lax.rsqrt(jnp.mean(xf * xf, axis=-1, keepdims=True) + EPS)
    return (y * g.astype(jnp.float32)).astype(x.dtype)


def _heads(a, h):
    return a.reshape(a.shape[0], a.shape[1], h, -1)


def _to_chunks(a):
    b, s, h = a.shape[:3]
    a = a.reshape((b, s // CHUNK, CHUNK, h) + a.shape[3:])
    return jnp.moveaxis(a, (1, 3), (0, 2))


def _from_chunks(a):
    a = jnp.moveaxis(a, (0, 2), (1, 3))
    return a.reshape((a.shape[0], a.shape[1] * a.shape[2]) + a.shape[3:])


def causal_conv_silu(u, w):
    s = u.shape[1]
    up = jnp.pad(u, ((0, 0), (CONV_W - 1, 0), (0, 0)))
    y = up[:, 0:s] * w[0]
    for j in range(1, CONV_W):
        y = y + up[:, j:j + s] * w[j]
    return jax.nn.silu(y)


def mlstm_chunkwise(q, k, v, i_pre, f_pre):
    bn, s, h, d = q.shape
    q = q.astype(jnp.float32)
    k = k.astype(jnp.float32) * (d ** -0.5)
    v = v.astype(jnp.float32)
    logf = jax.nn.log_sigmoid(f_pre)
    xs = (_to_chunks(q), _to_chunks(k), _to_chunks(v), _to_chunks(i_pre), _to_chunks(logf))
    causal = jnp.tril(jnp.ones((CHUNK, CHUNK), dtype=bool))

    def step(carry, inp):
        cmat, nvec, m = carry
        qb, kb, vb, ib, fb = inp
        bcum = jnp.cumsum(fb, axis=-1)
        logd = bcum[..., :, None] - bcum[..., None, :] + ib[..., None, :]
        logd = jnp.where(causal, logd, -jnp.inf)
        inter = bcum + m[..., None]
        m_t = jnp.maximum(inter, jnp.max(logd, axis=-1))
        w_inter = jnp.exp(inter - m_t)
        sc = jnp.einsum('bhtd,bhsd->bhts', qb, kb) * jnp.exp(logd - m_t[..., None])
        num = (jnp.einsum('bhts,bhsd->bhtd', sc, vb)
               + w_inter[..., None] * jnp.einsum('bhtd,bhde->bhte', qb, cmat))
        den = jnp.sum(sc, axis=-1) + w_inter * jnp.einsum('bhtd,bhd->bht', qb, nvec)
        hout = num / jnp.maximum(jnp.abs(den), jnp.exp(-m_t))[..., None]
        b_last = bcum[..., -1]
        log_in = b_last[..., None] - bcum + ib
        m_new = jnp.maximum(b_last + m, jnp.max(log_in, axis=-1))
        w_s = jnp.exp(log_in - m_new[..., None])
        decay = jnp.exp(b_last + m - m_new)
        c_new = decay[..., None, None] * cmat + jnp.einsum('bhs,bhsd,bhse->bhde', w_s, kb, vb)
        n_new = decay[..., None] * nvec + jnp.einsum('bhs,bhsd->bhd', w_s, kb)
        return (c_new, n_new, m_new), hout

    init = (jnp.zeros((bn, h, d, d), jnp.float32), jnp.zeros((bn, h, d), jnp.float32),
            jnp.zeros((bn, h), jnp.float32))
    _, hs = lax.scan(step, init, xs)
    return _from_chunks(hs)


def hgrn2_chunkwise(q, k, v, logf):
    bn, s, h, dk = q.shape
    dv = v.shape[-1]
    xs = (_to_chunks(q), _to_chunks(k), _to_chunks(v), _to_chunks(logf))
    causal = jnp.tril(jnp.ones((CHUNK, CHUNK), dtype=bool))[:, :, None]

    def step(state, inp):
        qb, kb, vb, gb = inp
        bcum = jnp.cumsum(gb, axis=2)
        rel = bcum[:, :, :, None, :] - bcum[:, :, None, :, :]
        rel = jnp.where(causal, rel, -jnp.inf)
        a = jnp.einsum('bhtc,bhsc,bhtsc->bhts', qb, kb, jnp.exp(rel))
        o = (jnp.einsum('bhts,bhse->bhte', a, vb)
             + jnp.einsum('bhtc,bhce->bhte', qb * jnp.exp(bcum), state))
        b_last = bcum[:, :, -1]
        s_new = (jnp.exp(b_last)[..., None] * state
                 + jnp.einsum('bhsc,bhse->bhce', kb * jnp.exp(b_last[:, :, None] - bcum), vb))
        return s_new, o

    _, os_ = lax.scan(step, jnp.zeros((bn, h, dk, dv), jnp.float32), xs)
    return _from_chunks(os_)


def forgetting_attention(q, k, v, logf):
    bn, s, h, d = q.shape
    nb = s // Q_BLOCK
    c = jnp.moveaxis(jnp.cumsum(logf, axis=1), 1, 2)
    qb = q.reshape(bn, nb, Q_BLOCK, h, d).transpose(1, 0, 3, 2, 4)
    cq = c.reshape(bn, h, nb, Q_BLOCK).transpose(2, 0, 1, 3)
    kpos = jnp.arange(s)

    def block(args):
        qi, cqi, bidx = args
        qpos = bidx * Q_BLOCK + jnp.arange(Q_BLOCK)
        logits = (jnp.einsum('bhqd,bshd->bhqs', qi, k).astype(jnp.float32) * (d ** -0.5)
                  + cqi[..., None] - c[:, :, None, :])
        logits = jnp.where(kpos[None, :] <= qpos[:, None], logits, -jnp.inf)
        p = jax.nn.softmax(logits, axis=-1)
        return jnp.einsum('bhqs,bshd->bqhd', p.astype(v.dtype), v)

    out = lax.map(block, (qb, cq, jnp.arange(nb)))
    return out.transpose(1, 0, 2, 3, 4).reshape(bn, s, h, d)


def mixer_ab(u, w_in, conv_w, gate_b, lb, norm_a, norm_b, w_out):
    bn, s, _ = u.shape
    z = u @ w_in
    qa, ka, va, oa, ia, fa, qb, fb, ib, gb = _split(z, AB_SIZES)
    qa, ka = jnp.split(causal_conv_silu(jnp.concatenate([qa, ka], axis=-1), conv_w), 2, axis=-1)
    gates = jnp.concatenate([ia, fa], axis=-1).astype(jnp.float32) + gate_b
    i_pre, f_pre = jnp.split(gates, 2, axis=-1)
    ha = mlstm_chunkwise(_heads(qa, A_HEADS), _heads(ka, A_HEADS), _heads(va, A_HEADS), i_pre, f_pre)
    ha = jax.nn.sigmoid(_heads(oa, A_HEADS).astype(jnp.float32)) * ha
    ha = rmsnorm(ha, norm_a.reshape(A_HEADS, A_DIM)).reshape(bn, s, A_W)
    zf = fb.astype(jnp.float32)
    logf = jnp.log(lb + (1.0 - lb) * jax.nn.sigmoid(zf))
    kf = (1.0 - lb) * jax.nn.sigmoid(-zf)
    hb = hgrn2_chunkwise(_heads(jax.nn.silu(qb.astype(jnp.float32)), B_HEADS), _heads(kf, B_HEADS),
                         _heads(jax.nn.silu(ib.astype(jnp.float32)), B_HEADS), _heads(logf, B_HEADS))
    hb = rmsnorm(hb, norm_b.reshape(B_HEADS, B_DV)) * jax.nn.silu(_heads(gb, B_HEADS).astype(jnp.float32))
    hb = hb.reshape(bn, s, B_VW)
    y = jnp.concatenate([ha, hb], axis=-1).astype(u.dtype)
    return y @ w_out


def mixer_c(u, w_in, f_b, q_norm, k_norm, w_out):
    bn, s, _ = u.shape
    z = u @ w_in
    q, k, v, g, f = _split(z, C_SIZES)
    q = rmsnorm(_heads(q, C_HEADS), q_norm)
    k = rmsnorm(_heads(k, C_HEADS), k_norm)
    logf = jax.nn.log_sigmoid(f.astype(jnp.float32) + f_b)
    o = forgetting_attention(q, k, _heads(v, C_HEADS), logf).reshape(bn, s, D_MODEL)
    o = (o.astype(jnp.float32) * jax.nn.sigmoid(g.astype(jnp.float32))).astype(u.dtype)
    return o @ w_out


def memory_cross_attention(u, mem_k, mem_v, w_q, w_o):
    bn, s, _ = u.shape
    q = _heads(u @ w_q, X_HEADS)
    logits = jnp.einsum('bshd,bmhd->bhsm', q, mem_k).astype(jnp.float32) * (X_DIM ** -0.5)
    p = jax.nn.softmax(logits, axis=-1)
    o = jnp.einsum('bhsm,bmhd->bshd', p.astype(mem_v.dtype), mem_v).reshape(bn, s, D_MODEL)
    return o @ w_o


def squared_relu_mlp(u, w1, w2):
    return jnp.square(jax.nn.relu(u @ w1)) @ w2


def _fwd_setup_inputs(seed: int = 0) -> dict:
    key = jax.random.key(seed)
    ks = jax.random.split(key, 32)

    def nrm(k, shape):
        return jax.random.normal(k, shape, jnp.float32)

    def dense(k, shape, fan_in):
        return nrm(k, shape) * (fan_in ** -0.5)

    def gain(k, shape):
        return 1.0 + 0.02 * nrm(k, shape)

    ab_gate_b = jnp.concatenate(
        [0.1 * nrm(ks[10], (N_EVEN, A_HEADS)),
         jnp.linspace(3.0, 6.0, A_HEADS)[None, :] + 0.1 * nrm(ks[11], (N_EVEN, A_HEADS))], axis=-1)
    c_fgate_b = jnp.linspace(1.0, 5.0, C_HEADS)[None, :] + 0.1 * nrm(ks[17], (N_ODD, C_HEADS))
    return {
        'x': nrm(ks[0], (BATCH, SEQ, D_MODEL)),
        'mem': nrm(ks[1], (BATCH, MEM_LEN, D_MODEL)),
        'norm_mix_g': gain(ks[2], (DEPTH, D_MODEL)),
        'norm_xattn_g': gain(ks[3], (DEPTH, D_MODEL)),
        'norm_mlp_g': gain(ks[4], (DEPTH, D_MODEL)),
        'final_norm_g': gain(ks[5], (D_MODEL,)),
        'ab_w_in': dense(ks[6], (N_EVEN, D_MODEL, AB_IN), D_MODEL),
        'ab_conv_w': dense(ks[7], (N_EVEN, CONV_W, 2 * A_W), CONV_W),
        'ab_gate_b': ab_gate_b,
        'hgrn_lb_logits': 0.1 * nrm(ks[8], (N_EVEN + 1, B_KW)),
        'mlstm_norm_g': gain(ks[9], (N_EVEN, A_W)),
        'hgrn_norm_g': gain(ks[12], (N_EVEN, B_VW)),
        'ab_w_out': dense(ks[13], (N_EVEN, A_W + B_VW, D_MODEL), A_W + B_VW),
        'c_w_in': dense(ks[14], (N_ODD, D_MODEL, C_IN), D_MODEL),
        'c_fgate_b': c_fgate_b,
        'c_qnorm_g': gain(ks[15], (N_ODD, C_DIM)),
        'c_knorm_g': gain(ks[16], (N_ODD, C_DIM)),
        'c_w_out': dense(ks[18], (N_ODD, D_MODEL, D_MODEL), D_MODEL),
        'mem_norm_g': gain(ks[19], (D_MODEL,)),
        'mem_w_kv': dense(ks[20], (D_MODEL, 2 * D_MODEL), D_MODEL),
        'xa_w_q': dense(ks[21], (DEPTH, D_MODEL, D_MODEL), D_MODEL),
        'xa_w_o': dense(ks[22], (DEPTH, D_MODEL, D_MODEL), D_MODEL),
        'mlp_w1': dense(ks[23], (DEPTH, D_MODEL, D_FF), D_MODEL),
        'mlp_w2': dense(ks[24], (DEPTH, D_FF, D_MODEL), D_FF),
    }


def _fwd_reference(x, mem, norm_mix_g, norm_xattn_g, norm_mlp_g, final_norm_g,
              ab_w_in, ab_conv_w, ab_gate_b, hgrn_lb_logits, mlstm_norm_g, hgrn_norm_g, ab_w_out,
              c_w_in, c_fgate_b, c_qnorm_g, c_knorm_g, c_w_out,
              mem_norm_g, mem_w_kv, xa_w_q, xa_w_o, mlp_w1, mlp_w2):
    bn, m_len, _ = mem.shape
    mem_k, mem_v = jnp.split(rmsnorm(mem, mem_norm_g) @ mem_w_kv, 2, axis=-1)
    mem_k = mem_k.reshape(bn, m_len, X_HEADS, X_DIM)
    mem_v = mem_v.reshape(bn, m_len, X_HEADS, X_DIM)
    lb_all = jnp.cumsum(jax.nn.softmax(hgrn_lb_logits.astype(jnp.float32), axis=0), axis=0)
    h = x
    for layer in range(DEPTH):
        u = rmsnorm(h, norm_mix_g[layer])
        if layer % 2 == 0:
            e = layer // 2
            h = h + mixer_ab(u, ab_w_in[e], ab_conv_w[e], ab_gate_b[e], lb_all[e],
                             mlstm_norm_g[e], hgrn_norm_g[e], ab_w_out[e])
        else:
            o = layer // 2
            h = h + mixer_c(u, c_w_in[o], c_fgate_b[o], c_qnorm_g[o], c_knorm_g[o], c_w_out[o])
        h = h + memory_cross_attention(rmsnorm(h, norm_xattn_g[layer]), mem_k, mem_v,
                                       xa_w_q[layer], xa_w_o[layer])
        h = h + squared_relu_mlp(rmsnorm(h, norm_mlp_g[layer]), mlp_w1[layer], mlp_w2[layer])
    return rmsnorm(h, final_norm_g)


import jax as _jax
import jax.numpy as _jnp

TWIN_FORMAT = 'train_step'
FWD_PARAMS = ['x', 'mem', 'norm_mix_g', 'norm_xattn_g', 'norm_mlp_g', 'final_norm_g', 'ab_w_in', 'ab_conv_w', 'ab_gate_b', 'hgrn_lb_logits', 'mlstm_norm_g', 'hgrn_norm_g', 'ab_w_out', 'c_w_in', 'c_fgate_b', 'c_qnorm_g', 'c_knorm_g', 'c_w_out', 'mem_norm_g', 'mem_w_kv', 'xa_w_q', 'xa_w_o', 'mlp_w1', 'mlp_w2']
TWIN_WEIGHTS = ['norm_mix_g', 'norm_xattn_g', 'norm_mlp_g', 'final_norm_g', 'ab_w_in', 'ab_conv_w', 'ab_gate_b', 'hgrn_lb_logits', 'mlstm_norm_g', 'hgrn_norm_g', 'ab_w_out', 'c_w_in', 'c_fgate_b', 'c_qnorm_g', 'c_knorm_g', 'c_w_out', 'mem_norm_g', 'mem_w_kv', 'xa_w_q', 'xa_w_o', 'mlp_w1', 'mlp_w2']
TWIN_DIFF_INPUT = 'x'
TWIN_INPUTS = ['x', 'mem', 'norm_mix_g', 'norm_xattn_g', 'norm_mlp_g', 'final_norm_g', 'ab_w_in', 'ab_conv_w', 'ab_gate_b', 'hgrn_lb_logits', 'mlstm_norm_g', 'hgrn_norm_g', 'ab_w_out', 'c_w_in', 'c_fgate_b', 'c_qnorm_g', 'c_knorm_g', 'c_w_out', 'mem_norm_g', 'mem_w_kv', 'xa_w_q', 'xa_w_o', 'mlp_w1', 'mlp_w2', 'loss_target', 'm_norm_mix_g', 'm_norm_xattn_g', 'm_norm_mlp_g', 'm_final_norm_g', 'm_ab_w_in', 'm_ab_conv_w', 'm_ab_gate_b', 'm_hgrn_lb_logits', 'm_mlstm_norm_g', 'm_hgrn_norm_g', 'm_ab_w_out', 'm_c_w_in', 'm_c_fgate_b', 'm_c_qnorm_g', 'm_c_knorm_g', 'm_c_w_out', 'm_mem_norm_g', 'm_mem_w_kv', 'm_xa_w_q', 'm_xa_w_o', 'm_mlp_w1', 'm_mlp_w2', 'v_norm_mix_g', 'v_norm_xattn_g', 'v_norm_mlp_g', 'v_final_norm_g', 'v_ab_w_in', 'v_ab_conv_w', 'v_ab_gate_b', 'v_hgrn_lb_logits', 'v_mlstm_norm_g', 'v_hgrn_norm_g', 'v_ab_w_out', 'v_c_w_in', 'v_c_fgate_b', 'v_c_qnorm_g', 'v_c_knorm_g', 'v_c_w_out', 'v_mem_norm_g', 'v_mem_w_kv', 'v_xa_w_q', 'v_xa_w_o', 'v_mlp_w1', 'v_mlp_w2']
TWIN_OUTPUTS = ['loss', 'grad_x', 'grad_norm_mix_g', 'grad_norm_xattn_g', 'grad_norm_mlp_g', 'grad_final_norm_g', 'grad_ab_w_in', 'grad_ab_conv_w', 'grad_ab_gate_b', 'grad_hgrn_lb_logits', 'grad_mlstm_norm_g', 'grad_hgrn_norm_g', 'grad_ab_w_out', 'grad_c_w_in', 'grad_c_fgate_b', 'grad_c_qnorm_g', 'grad_c_knorm_g', 'grad_c_w_out', 'grad_mem_norm_g', 'grad_mem_w_kv', 'grad_xa_w_q', 'grad_xa_w_o', 'grad_mlp_w1', 'grad_mlp_w2', 'delta_norm_mix_g', 'delta_norm_xattn_g', 'delta_norm_mlp_g', 'delta_final_norm_g', 'delta_ab_w_in', 'delta_ab_conv_w', 'delta_ab_gate_b', 'delta_hgrn_lb_logits', 'delta_mlstm_norm_g', 'delta_hgrn_norm_g', 'delta_ab_w_out', 'delta_c_w_in', 'delta_c_fgate_b', 'delta_c_qnorm_g', 'delta_c_knorm_g', 'delta_c_w_out', 'delta_mem_norm_g', 'delta_mem_w_kv', 'delta_xa_w_q', 'delta_xa_w_o', 'delta_mlp_w1', 'delta_mlp_w2', 'new_m_norm_mix_g', 'new_m_norm_xattn_g', 'new_m_norm_mlp_g', 'new_m_final_norm_g', 'new_m_ab_w_in', 'new_m_ab_conv_w', 'new_m_ab_gate_b', 'new_m_hgrn_lb_logits', 'new_m_mlstm_norm_g', 'new_m_hgrn_norm_g', 'new_m_ab_w_out', 'new_m_c_w_in', 'new_m_c_fgate_b', 'new_m_c_qnorm_g', 'new_m_c_knorm_g', 'new_m_c_w_out', 'new_m_mem_norm_g', 'new_m_mem_w_kv', 'new_m_xa_w_q', 'new_m_xa_w_o', 'new_m_mlp_w1', 'new_m_mlp_w2', 'new_v_norm_mix_g', 'new_v_norm_xattn_g', 'new_v_norm_mlp_g', 'new_v_final_norm_g', 'new_v_ab_w_in', 'new_v_ab_conv_w', 'new_v_ab_gate_b', 'new_v_hgrn_lb_logits', 'new_v_mlstm_norm_g', 'new_v_hgrn_norm_g', 'new_v_ab_w_out', 'new_v_c_w_in', 'new_v_c_fgate_b', 'new_v_c_qnorm_g', 'new_v_c_knorm_g', 'new_v_c_w_out', 'new_v_mem_norm_g', 'new_v_mem_w_kv', 'new_v_xa_w_q', 'new_v_xa_w_o', 'new_v_mlp_w1', 'new_v_mlp_w2']
TWIN_LEAF_KINDS = {'loss': 'loss', 'grad_x': 'grad_x', 'grad_norm_mix_g': 'grad_w', 'grad_norm_xattn_g': 'grad_w', 'grad_norm_mlp_g': 'grad_w', 'grad_final_norm_g': 'grad_w', 'grad_ab_w_in': 'grad_w', 'grad_ab_conv_w': 'grad_w', 'grad_ab_gate_b': 'grad_w', 'grad_hgrn_lb_logits': 'grad_w', 'grad_mlstm_norm_g': 'grad_w', 'grad_hgrn_norm_g': 'grad_w', 'grad_ab_w_out': 'grad_w', 'grad_c_w_in': 'grad_w', 'grad_c_fgate_b': 'grad_w', 'grad_c_qnorm_g': 'grad_w', 'grad_c_knorm_g': 'grad_w', 'grad_c_w_out': 'grad_w', 'grad_mem_norm_g': 'grad_w', 'grad_mem_w_kv': 'grad_w', 'grad_xa_w_q': 'grad_w', 'grad_xa_w_o': 'grad_w', 'grad_mlp_w1': 'grad_w', 'grad_mlp_w2': 'grad_w', 'delta_norm_mix_g': 'delta_w', 'delta_norm_xattn_g': 'delta_w', 'delta_norm_mlp_g': 'delta_w', 'delta_final_norm_g': 'delta_w', 'delta_ab_w_in': 'delta_w', 'delta_ab_conv_w': 'delta_w', 'delta_ab_gate_b': 'delta_w', 'delta_hgrn_lb_logits': 'delta_w', 'delta_mlstm_norm_g': 'delta_w', 'delta_hgrn_norm_g': 'delta_w', 'delta_ab_w_out': 'delta_w', 'delta_c_w_in': 'delta_w', 'delta_c_fgate_b': 'delta_w', 'delta_c_qnorm_g': 'delta_w', 'delta_c_knorm_g': 'delta_w', 'delta_c_w_out': 'delta_w', 'delta_mem_norm_g': 'delta_w', 'delta_mem_w_kv': 'delta_w', 'delta_xa_w_q': 'delta_w', 'delta_xa_w_o': 'delta_w', 'delta_mlp_w1': 'delta_w', 'delta_mlp_w2': 'delta_w', 'new_m_norm_mix_g': 'new_m', 'new_m_norm_xattn_g': 'new_m', 'new_m_norm_mlp_g': 'new_m', 'new_m_final_norm_g': 'new_m', 'new_m_ab_w_in': 'new_m', 'new_m_ab_conv_w': 'new_m', 'new_m_ab_gate_b': 'new_m', 'new_m_hgrn_lb_logits': 'new_m', 'new_m_mlstm_norm_g': 'new_m', 'new_m_hgrn_norm_g': 'new_m', 'new_m_ab_w_out': 'new_m', 'new_m_c_w_in': 'new_m', 'new_m_c_fgate_b': 'new_m', 'new_m_c_qnorm_g': 'new_m', 'new_m_c_knorm_g': 'new_m', 'new_m_c_w_out': 'new_m', 'new_m_mem_norm_g': 'new_m', 'new_m_mem_w_kv': 'new_m', 'new_m_xa_w_q': 'new_m', 'new_m_xa_w_o': 'new_m', 'new_m_mlp_w1': 'new_m', 'new_m_mlp_w2': 'new_m', 'new_v_norm_mix_g': 'new_v', 'new_v_norm_xattn_g': 'new_v', 'new_v_norm_mlp_g': 'new_v', 'new_v_final_norm_g': 'new_v', 'new_v_ab_w_in': 'new_v', 'new_v_ab_conv_w': 'new_v', 'new_v_ab_gate_b': 'new_v', 'new_v_hgrn_lb_logits': 'new_v', 'new_v_mlstm_norm_g': 'new_v', 'new_v_hgrn_norm_g': 'new_v', 'new_v_ab_w_out': 'new_v', 'new_v_c_w_in': 'new_v', 'new_v_c_fgate_b': 'new_v', 'new_v_c_qnorm_g': 'new_v', 'new_v_c_knorm_g': 'new_v', 'new_v_c_w_out': 'new_v', 'new_v_mem_norm_g': 'new_v', 'new_v_mem_w_kv': 'new_v', 'new_v_xa_w_q': 'new_v', 'new_v_xa_w_o': 'new_v', 'new_v_mlp_w1': 'new_v', 'new_v_mlp_w2': 'new_v'}


def _forward(args):
    return _fwd_reference(*[args[k] for k in FWD_PARAMS])


def _output_shape():
    def fwd():
        inp = _fwd_setup_inputs(0)
        return _fwd_reference(*[inp[k] for k in FWD_PARAMS])
    out = _jax.eval_shape(fwd)
    return out.shape, out.dtype

N_MICROBATCH = 1
ADAM_LR = 0.001
ADAM_B1 = 0.9
ADAM_B2 = 0.999
ADAM_EPS = 1e-08
ADAM_WD = 0.01
ADAM_STEP = 10
PER_EXAMPLE_BATCH_AXIS = {'x': 0, 'mem': 0, 'loss_target': 0}
SHARED_INPUTS = []
_WEIGHT_DTYPES = {'norm_mix_g': _jnp.float32, 'norm_xattn_g': _jnp.float32, 'norm_mlp_g': _jnp.float32, 'final_norm_g': _jnp.float32, 'ab_w_in': _jnp.float32, 'ab_conv_w': _jnp.float32, 'ab_gate_b': _jnp.float32, 'hgrn_lb_logits': _jnp.float32, 'mlstm_norm_g': _jnp.float32, 'hgrn_norm_g': _jnp.float32, 'ab_w_out': _jnp.float32, 'c_w_in': _jnp.float32, 'c_fgate_b': _jnp.float32, 'c_qnorm_g': _jnp.float32, 'c_knorm_g': _jnp.float32, 'c_w_out': _jnp.float32, 'mem_norm_g': _jnp.float32, 'mem_w_kv': _jnp.float32, 'xa_w_q': _jnp.float32, 'xa_w_o': _jnp.float32, 'mlp_w1': _jnp.float32, 'mlp_w2': _jnp.float32}
MOMENT_SCALE = {'norm_mix_g': 1.572657e-01, 'norm_xattn_g': 1.451774e-02, 'norm_mlp_g': 1.358430e-01, 'final_norm_g': 3.263296e+01, 'ab_w_in': 1.069096e-01, 'ab_conv_w': 1.146587e-01, 'ab_gate_b': 7.538835e-01, 'hgrn_lb_logits': 7.717975e-03, 'mlstm_norm_g': 1.613723e-01, 'hgrn_norm_g': 9.602869e-02, 'ab_w_out': 1.320682e-01, 'c_w_in': 2.216371e-02, 'c_fgate_b': 1.707411e-01, 'c_qnorm_g': 3.843902e-02, 'c_knorm_g': 3.818405e-02, 'c_w_out': 3.768258e-02, 'mem_norm_g': 3.081346e-02, 'mem_w_kv': 1.999338e-02, 'xa_w_q': 1.372283e-02, 'xa_w_o': 1.451686e-02, 'mlp_w1': 6.767212e-02, 'mlp_w2': 1.481978e-01}


def _to_microbatches(a, axis):
    t = _jnp.moveaxis(a, axis, 0)
    t = t.reshape((N_MICROBATCH, t.shape[0] // N_MICROBATCH) + t.shape[1:])
    return _jnp.moveaxis(t, 1, axis + 1)


def setup_inputs(seed: int = 0) -> dict:
    inp = _fwd_setup_inputs(seed)
    key = _jax.random.fold_in(_jax.random.key(seed), 7919)
    shape, _ = _output_shape()
    out = dict(inp)
    out["loss_target"] = _jax.random.normal(_jax.random.fold_in(key, 0), shape, _jnp.float32)
    for i, name in enumerate(TWIN_WEIGHTS):
        w = inp[name].astype(_jnp.float32)
        if MOMENT_SCALE is None:
            s = _jnp.sqrt(_jnp.mean(_jnp.square(w)) + 1e-30)
        else:
            s = MOMENT_SCALE[name]
        km, kv = _jax.random.split(_jax.random.fold_in(key, i + 1))
        out[name] = w
        out["m_" + name] = s * _jax.random.normal(km, w.shape, _jnp.float32)
        out["v_" + name] = (s * s) * _jax.random.uniform(kv, w.shape, _jnp.float32, 0.5, 1.5)
    if N_MICROBATCH > 1:
        for name, axis in PER_EXAMPLE_BATCH_AXIS.items():
            out[name] = _to_microbatches(out[name], axis)
    return {'x': out['x'], 'mem': out['mem'], 'norm_mix_g': out['norm_mix_g'], 'norm_xattn_g': out['norm_xattn_g'], 'norm_mlp_g': out['norm_mlp_g'], 'final_norm_g': out['final_norm_g'], 'ab_w_in': out['ab_w_in'], 'ab_conv_w': out['ab_conv_w'], 'ab_gate_b': out['ab_gate_b'], 'hgrn_lb_logits': out['hgrn_lb_logits'], 'mlstm_norm_g': out['mlstm_norm_g'], 'hgrn_norm_g': out['hgrn_norm_g'], 'ab_w_out': out['ab_w_out'], 'c_w_in': out['c_w_in'], 'c_fgate_b': out['c_fgate_b'], 'c_qnorm_g': out['c_qnorm_g'], 'c_knorm_g': out['c_knorm_g'], 'c_w_out': out['c_w_out'], 'mem_norm_g': out['mem_norm_g'], 'mem_w_kv': out['mem_w_kv'], 'xa_w_q': out['xa_w_q'], 'xa_w_o': out['xa_w_o'], 'mlp_w1': out['mlp_w1'], 'mlp_w2': out['mlp_w2'], 'loss_target': out['loss_target'], 'm_norm_mix_g': out['m_norm_mix_g'], 'm_norm_xattn_g': out['m_norm_xattn_g'], 'm_norm_mlp_g': out['m_norm_mlp_g'], 'm_final_norm_g': out['m_final_norm_g'], 'm_ab_w_in': out['m_ab_w_in'], 'm_ab_conv_w': out['m_ab_conv_w'], 'm_ab_gate_b': out['m_ab_gate_b'], 'm_hgrn_lb_logits': out['m_hgrn_lb_logits'], 'm_mlstm_norm_g': out['m_mlstm_norm_g'], 'm_hgrn_norm_g': out['m_hgrn_norm_g'], 'm_ab_w_out': out['m_ab_w_out'], 'm_c_w_in': out['m_c_w_in'], 'm_c_fgate_b': out['m_c_fgate_b'], 'm_c_qnorm_g': out['m_c_qnorm_g'], 'm_c_knorm_g': out['m_c_knorm_g'], 'm_c_w_out': out['m_c_w_out'], 'm_mem_norm_g': out['m_mem_norm_g'], 'm_mem_w_kv': out['m_mem_w_kv'], 'm_xa_w_q': out['m_xa_w_q'], 'm_xa_w_o': out['m_xa_w_o'], 'm_mlp_w1': out['m_mlp_w1'], 'm_mlp_w2': out['m_mlp_w2'], 'v_norm_mix_g': out['v_norm_mix_g'], 'v_norm_xattn_g': out['v_norm_xattn_g'], 'v_norm_mlp_g': out['v_norm_mlp_g'], 'v_final_norm_g': out['v_final_norm_g'], 'v_ab_w_in': out['v_ab_w_in'], 'v_ab_conv_w': out['v_ab_conv_w'], 'v_ab_gate_b': out['v_ab_gate_b'], 'v_hgrn_lb_logits': out['v_hgrn_lb_logits'], 'v_mlstm_norm_g': out['v_mlstm_norm_g'], 'v_hgrn_norm_g': out['v_hgrn_norm_g'], 'v_ab_w_out': out['v_ab_w_out'], 'v_c_w_in': out['v_c_w_in'], 'v_c_fgate_b': out['v_c_fgate_b'], 'v_c_qnorm_g': out['v_c_qnorm_g'], 'v_c_knorm_g': out['v_c_knorm_g'], 'v_c_w_out': out['v_c_w_out'], 'v_mem_norm_g': out['v_mem_norm_g'], 'v_mem_w_kv': out['v_mem_w_kv'], 'v_xa_w_q': out['v_xa_w_q'], 'v_xa_w_o': out['v_xa_w_o'], 'v_mlp_w1': out['v_mlp_w1'], 'v_mlp_w2': out['v_mlp_w2']}


def _loss(weights, diff, rest, loss_target):
    with _jax.named_scope("forward"):
        args = {**rest, TWIN_DIFF_INPUT: diff, **{k: w.astype(_WEIGHT_DTYPES[k]) for k, w in weights.items()}}
        y = _forward(args)
    with _jax.named_scope("loss_head"):
        err = _jnp.square(y.astype(_jnp.float32) - loss_target)
        return 0.5 * _jnp.sum(_jnp.mean(err, axis=-1)) if err.ndim else 0.5 * err


def _adamw(w, g, m, v):
    m = ADAM_B1 * m + (1.0 - ADAM_B1) * g
    v = ADAM_B2 * v + (1.0 - ADAM_B2) * _jnp.square(g)
    m_hat = m / (1.0 - ADAM_B1 ** ADAM_STEP)
    v_hat = v / (1.0 - ADAM_B2 ** ADAM_STEP)
    delta = -ADAM_LR * (m_hat / (_jnp.sqrt(v_hat) + ADAM_EPS) + ADAM_WD * w)
    return delta, m, v


def reference(x, mem, norm_mix_g, norm_xattn_g, norm_mlp_g, final_norm_g, ab_w_in, ab_conv_w, ab_gate_b, hgrn_lb_logits, mlstm_norm_g, hgrn_norm_g, ab_w_out, c_w_in, c_fgate_b, c_qnorm_g, c_knorm_g, c_w_out, mem_norm_g, mem_w_kv, xa_w_q, xa_w_o, mlp_w1, mlp_w2, loss_target, m_norm_mix_g, m_norm_xattn_g, m_norm_mlp_g, m_final_norm_g, m_ab_w_in, m_ab_conv_w, m_ab_gate_b, m_hgrn_lb_logits, m_mlstm_norm_g, m_hgrn_norm_g, m_ab_w_out, m_c_w_in, m_c_fgate_b, m_c_qnorm_g, m_c_knorm_g, m_c_w_out, m_mem_norm_g, m_mem_w_kv, m_xa_w_q, m_xa_w_o, m_mlp_w1, m_mlp_w2, v_norm_mix_g, v_norm_xattn_g, v_norm_mlp_g, v_final_norm_g, v_ab_w_in, v_ab_conv_w, v_ab_gate_b, v_hgrn_lb_logits, v_mlstm_norm_g, v_hgrn_norm_g, v_ab_w_out, v_c_w_in, v_c_fgate_b, v_c_qnorm_g, v_c_knorm_g, v_c_w_out, v_mem_norm_g, v_mem_w_kv, v_xa_w_q, v_xa_w_o, v_mlp_w1, v_mlp_w2):
    given = dict(x=x, mem=mem, norm_mix_g=norm_mix_g, norm_xattn_g=norm_xattn_g, norm_mlp_g=norm_mlp_g, final_norm_g=final_norm_g, ab_w_in=ab_w_in, ab_conv_w=ab_conv_w, ab_gate_b=ab_gate_b, hgrn_lb_logits=hgrn_lb_logits, mlstm_norm_g=mlstm_norm_g, hgrn_norm_g=hgrn_norm_g, ab_w_out=ab_w_out, c_w_in=c_w_in, c_fgate_b=c_fgate_b, c_qnorm_g=c_qnorm_g, c_knorm_g=c_knorm_g, c_w_out=c_w_out, mem_norm_g=mem_norm_g, mem_w_kv=mem_w_kv, xa_w_q=xa_w_q, xa_w_o=xa_w_o, mlp_w1=mlp_w1, mlp_w2=mlp_w2, loss_target=loss_target, m_norm_mix_g=m_norm_mix_g, m_norm_xattn_g=m_norm_xattn_g, m_norm_mlp_g=m_norm_mlp_g, m_final_norm_g=m_final_norm_g, m_ab_w_in=m_ab_w_in, m_ab_conv_w=m_ab_conv_w, m_ab_gate_b=m_ab_gate_b, m_hgrn_lb_logits=m_hgrn_lb_logits, m_mlstm_norm_g=m_mlstm_norm_g, m_hgrn_norm_g=m_hgrn_norm_g, m_ab_w_out=m_ab_w_out, m_c_w_in=m_c_w_in, m_c_fgate_b=m_c_fgate_b, m_c_qnorm_g=m_c_qnorm_g, m_c_knorm_g=m_c_knorm_g, m_c_w_out=m_c_w_out, m_mem_norm_g=m_mem_norm_g, m_mem_w_kv=m_mem_w_kv, m_xa_w_q=m_xa_w_q, m_xa_w_o=m_xa_w_o, m_mlp_w1=m_mlp_w1, m_mlp_w2=m_mlp_w2, v_norm_mix_g=v_norm_mix_g, v_norm_xattn_g=v_norm_xattn_g, v_norm_mlp_g=v_norm_mlp_g, v_final_norm_g=v_final_norm_g, v_ab_w_in=v_ab_w_in, v_ab_conv_w=v_ab_conv_w, v_ab_gate_b=v_ab_gate_b, v_hgrn_lb_logits=v_hgrn_lb_logits, v_mlstm_norm_g=v_mlstm_norm_g, v_hgrn_norm_g=v_hgrn_norm_g, v_ab_w_out=v_ab_w_out, v_c_w_in=v_c_w_in, v_c_fgate_b=v_c_fgate_b, v_c_qnorm_g=v_c_qnorm_g, v_c_knorm_g=v_c_knorm_g, v_c_w_out=v_c_w_out, v_mem_norm_g=v_mem_norm_g, v_mem_w_kv=v_mem_w_kv, v_xa_w_q=v_xa_w_q, v_xa_w_o=v_xa_w_o, v_mlp_w1=v_mlp_w1, v_mlp_w2=v_mlp_w2)
    weights = {n: given[n] for n in TWIN_WEIGHTS}
    shared = {n: given[n] for n in SHARED_INPUTS}
    per_example = {n: given[n] for n in ['x', 'mem']}
    grad_fn = _jax.value_and_grad(_loss, argnums=(0, 1))

    def one_microbatch(ex, loss_target):
        ex = dict(ex)
        diff = ex.pop(TWIN_DIFF_INPUT)
        return grad_fn(weights, diff, {**shared, **ex}, loss_target)

    if N_MICROBATCH == 1:
        loss, (grad_w, grad_x) = one_microbatch(per_example, given["loss_target"])
    else:
        def body(carry, xs):
            loss_sum, grad_sum = carry
            l_k, (gw_k, gx_k) = one_microbatch(xs[0], xs[1])
            with _jax.named_scope("update"):
                return (loss_sum + l_k, _jax.tree.map(_jnp.add, grad_sum, gw_k)), gx_k

        init = (_jnp.zeros((), _jnp.float32), _jax.tree.map(_jnp.zeros_like, weights))
        (loss, grad_w), grad_x = _jax.lax.scan(body, init, (per_example, given["loss_target"]))
    with _jax.named_scope("update"):
        delta_w, new_m, new_v = {}, {}, {}
        for n in TWIN_WEIGHTS:
            delta_w[n], new_m[n], new_v[n] = _adamw(weights[n], grad_w[n], given["m_" + n], given["v_" + n])
    return (loss, grad_x, *[grad_w[n] for n in TWIN_WEIGHTS], *[delta_w[n] for n in TWIN_WEIGHTS],
            *[new_m[n] for n in TWIN_WEIGHTS], *[new_v[n] for n in TWIN_WEIGHTS])
```

```python
import functools

import jax
import jax.numpy as jnp
from jax import lax
from jax.experimental import pallas as pl
from jax.experimental.pallas import tpu as pltpu

F32 = jnp.float32
BF16 = jnp.bfloat16
EPS = 1e-6
D = 1024
CHUNK = 64
HD = 128
XD = 256
NEG = -1e30
VMEM_LIMIT_V7X = 56 * 1024 * 1024
ADAM_LR, ADAM_B1, ADAM_B2, ADAM_EPS, ADAM_WD, ADAM_STEP = 0.001, 0.9, 0.999, 1e-08, 0.01, 10
MESH = pl.DeviceIdType.MESH


def _pc(body, name, grid, in_specs, out_specs, out_shape, scratch=(), **kw):
    return pl.pallas_call(
        body, name=name, grid=grid, in_specs=in_specs, out_specs=out_specs, out_shape=out_shape,
        scratch_shapes=scratch,
        compiler_params=pltpu.CompilerParams(
            dimension_semantics=("arbitrary",) * len(grid), vmem_limit_bytes=VMEM_LIMIT_V7X), **kw)


def _sds(shape, dtype=F32):
    return jax.ShapeDtypeStruct(shape, dtype)


def _blk(n, target):
    return max(b for b in range(128, max(target, 128) + 1, 128) if n % b == 0)


def _dot(a, b, dims):
    return lax.dot_general(a, b, (dims, ((), ())), preferred_element_type=F32)


def _nn(a, b):
    return _dot(a, b, ((1,), (0,)))


def _nt(a, b):
    return _dot(a, b, ((1,), (1,)))


def _tn(a, b):
    return _dot(a, b, ((0,), (0,)))


def _sigmoid(x):
    return 1.0 / (1.0 + jnp.exp(-x))


def _log_sigmoid(x):
    return jnp.minimum(x, 0.0) - jnp.log(1.0 + jnp.exp(-jnp.abs(x)))


def _rstd(x):
    return lax.rsqrt(jnp.mean(x * x, axis=-1, keepdims=True) + EPS)


def _rms_bwd(du, x, g):
    r = _rstd(x)
    xh = x * r
    dxh = du * g
    dx = r * (dxh - xh * jnp.mean(dxh * xh, axis=-1, keepdims=True))
    return dx, du * xh


def _norm_mm(h, g, w, name, bm=512, bn=512):
    t, n = h.shape[0], w.shape[1]
    bm, bn = min(bm, t), _blk(n, 3 * bn)

    def body(h_ref, g_ref, w_ref, z_ref, u_ref):
        @pl.when(pl.program_id(1) == 0)
        def _():
            x = h_ref[...]
            u_ref[...] = (x * _rstd(x) * g_ref[...]).astype(BF16)
        z_ref[...] = _nn(u_ref[...], w_ref[...])

    return _pc(body, name, (t // bm, n // bn),
               [pl.BlockSpec((bm, D), lambda i, j: (i, 0)), pl.BlockSpec((1, D), lambda i, j: (0, 0)),
                pl.BlockSpec((D, bn), lambda i, j: (0, j))],
               [pl.BlockSpec((bm, bn), lambda i, j: (i, j)), pl.BlockSpec((bm, D), lambda i, j: (i, 0))],
               [_sds((t, n)), _sds((t, D), BF16)])(h, g, w)


def _mm_tn(a, b, name, bm=1024, bn=1024, bt=512):
    t, m = a.shape
    n = b.shape[1]
    bm, bn, bt = _blk(m, bm), _blk(n, bn + bn // 2), min(bt, t)

    def body(a_ref, b_ref, o_ref):
        @pl.when(pl.program_id(2) == 0)
        def _():
            o_ref[...] = jnp.zeros_like(o_ref)
        o_ref[...] += _tn(a_ref[...].astype(BF16), b_ref[...].astype(BF16))

    return _pc(body, name, (m // bm, n // bn, t // bt),
               [pl.BlockSpec((bt, bm), lambda i, j, k: (k, i)), pl.BlockSpec((bt, bn), lambda i, j, k: (k, j))],
               pl.BlockSpec((bm, bn), lambda i, j, k: (i, j)), _sds((m, n)))(a, b)


def _bwd_in(dz, w, h, g, dh, name, bm=512, bk=1024):
    t, n = dz.shape
    bm, bk = min(bm, t), _blk(n, bk + bk // 2)
    nk = n // bk

    def body(dz_ref, w_ref, h_ref, g_ref, dh_ref, o_ref, dg_ref, acc):
        i, k = pl.program_id(0), pl.program_id(1)

        @pl.when(k == 0)
        def _():
            acc[...] = jnp.zeros_like(acc)

        @pl.when((i == 0) & (k == 0))
        def _():
            dg_ref[...] = jnp.zeros_like(dg_ref)

        acc[...] += _nt(dz_ref[...], w_ref[...])

        @pl.when(k == nk - 1)
        def _():
            dx, dgr = _rms_bwd(acc[...], h_ref[...], g_ref[...])
            o_ref[...] = dh_ref[...] + dx
            dg_ref[...] += jnp.sum(dgr, axis=0, keepdims=True)

    return _pc(body, name, (t // bm, nk),
               [pl.BlockSpec((bm, bk), lambda i, k: (i, k)), pl.BlockSpec((D, bk), lambda i, k: (0, k)),
                pl.BlockSpec((bm, D), lambda i, k: (i, 0)), pl.BlockSpec((1, D), lambda i, k: (0, 0)),
                pl.BlockSpec((bm, D), lambda i, k: (i, 0))],
               [pl.BlockSpec((bm, D), lambda i, k: (i, 0)), pl.BlockSpec((1, D), lambda i, k: (0, 0))],
               [_sds((t, D)), _sds((1, D))], scratch=[pltpu.VMEM((bm, D), F32)])(dz, w, h, g, dh)


def _mlp_fwd(h, g, w1s, w2, name, bm=512):
    t = h.shape[0]
    bm = min(bm, t)
    nk = w1s.shape[0]

    def body(h_ref, g_ref, w1_ref, w2_ref, o_ref, a_ref, u_ref, acc):
        k = pl.program_id(1)

        @pl.when(k == 0)
        def _():
            x = h_ref[...]
            u_ref[...] = (x * _rstd(x) * g_ref[...]).astype(BF16)
            acc[...] = jnp.zeros_like(acc)

        a = _nn(u_ref[...], w1_ref[...])
        a_ref[...] = a
        r = jnp.square(jnp.maximum(a, 0.0)).astype(BF16)
        acc[...] += _nn(r, w2_ref[...])

        @pl.when(k == nk - 1)
        def _():
            o_ref[...] = h_ref[...] + acc[...]

    return _pc(body, name, (t // bm, nk),
               [pl.BlockSpec((bm, D), lambda i, k: (i, 0)), pl.BlockSpec((1, D), lambda i, k: (0, 0)),
                pl.BlockSpec((None, D, D), lambda i, k: (k, 0, 0)), pl.BlockSpec((D, D), lambda i, k: (k, 0))],
               [pl.BlockSpec((bm, D), lambda i, k: (i, 0)), pl.BlockSpec((bm, D), lambda i, k: (i, k)),
                pl.BlockSpec((bm, D), lambda i, k: (i, 0))],
               [_sds((t, D)), _sds((t, nk * D)), _sds((t, D), BF16)],
               scratch=[pltpu.VMEM((bm, D), F32)])(h, g, w1s, w2)


def _mlp_bwd(dh, a, w1s, w2, h, g, name, bm=512):
    t = h.shape[0]
    bm = min(bm, t)
    nk = w1s.shape[0]

    def body(dh_ref, a_ref, w1_ref, w2_ref, h_ref, g_ref, o_ref, da_ref, r_ref, dg_ref, acc):
        i, k = pl.program_id(0), pl.program_id(1)

        @pl.when(k == 0)
        def _():
            acc[...] = jnp.zeros_like(acc)

        @pl.when((i == 0) & (k == 0))
        def _():
            dg_ref[...] = jnp.zeros_like(dg_ref)

        ap = jnp.maximum(a_ref[...], 0.0)
        r_ref[...] = jnp.square(ap).astype(BF16)
        dr = _nt(dh_ref[...].astype(BF16), w2_ref[...])
        da = (dr * (2.0 * ap)).astype(BF16)
        da_ref[...] = da
        acc[...] += _nt(da, w1_ref[...])

        @pl.when(k == nk - 1)
        def _():
            dx, dgr = _rms_bwd(acc[...], h_ref[...], g_ref[...])
            o_ref[...] = dh_ref[...] + dx
            dg_ref[...] += jnp.sum(dgr, axis=0, keepdims=True)

    return _pc(body, name, (t // bm, nk),
               [pl.BlockSpec((bm, D), lambda i, k: (i, 0)), pl.BlockSpec((bm, D), lambda i, k: (i, k)),
                pl.BlockSpec((None, D, D), lambda i, k: (k, 0, 0)), pl.BlockSpec((D, D), lambda i, k: (k, 0)),
                pl.BlockSpec((bm, D), lambda i, k: (i, 0)), pl.BlockSpec((1, D), lambda i, k: (0, 0))],
               [pl.BlockSpec((bm, D), lambda i, k: (i, 0)), pl.BlockSpec((bm, D), lambda i, k: (i, k)),
                pl.BlockSpec((bm, D), lambda i, k: (i, k)), pl.BlockSpec((1, D), lambda i, k: (0, 0))],
               [_sds((t, D)), _sds((t, nk * D), BF16), _sds((t, nk * D), BF16), _sds((1, D))],
               scratch=[pltpu.VMEM((bm, D), F32)])(dh, a, w1s, w2, h, g)


def _rows_of(x):
    return lax.broadcasted_iota(jnp.int32, x.shape, 0)


def _shift_down(x, s):
    if s == 0:
        return x
    return jnp.where(_rows_of(x) >= s, pltpu.roll(x, s, 0), 0.0)


def _shift_up(x, s):
    if s == 0:
        return x
    n = x.shape[0]
    return jnp.where(_rows_of(x) < n - s, pltpu.roll(x, n - s, 0), 0.0)


def _cumsum_rows(x):
    n, s = x.shape[0], 1
    while s < n:
        x = x + _shift_down(x, s)
        s *= 2
    return x


def _rcumsum_rows(x):
    n, s = x.shape[0], 1
    while s < n:
        x = x + _shift_up(x, s)
        s *= 2
    return x


def _silu(x):
    return x * _sigmoid(x)


def _dsilu(x):
    s = _sigmoid(x)
    return s * (1.0 + x * (1.0 - s))


CONV_W = 4


def _conv_pre(u, w):
    y = _shift_down(u, CONV_W - 1) * w[0:1, :]
    for j in range(1, CONV_W):
        y = y + _shift_down(u, CONV_W - 1 - j) * w[j:j + 1, :]
    return y


def _conv_fwd(z0, cw, name):
    t = z0.shape[0]

    def body(u_ref, w_ref, o_ref):
        o_ref[...] = _silu(_conv_pre(u_ref[...], w_ref[...]))

    return _pc(body, name, (2 * 512 // HD,),
               [pl.BlockSpec((t, HD), lambda c: (0, c)), pl.BlockSpec((CONV_W, HD), lambda c: (0, c))],
               pl.BlockSpec((t, HD), lambda c: (0, c)), _sds((t, 1024)))(z0, cw)


def _conv_bwd(z0, cw, dy, name):
    t = z0.shape[0]

    def body(u_ref, w_ref, dy_ref, du_ref, dw_ref):
        u, w = u_ref[...], w_ref[...]
        dpre = dy_ref[...] * _dsilu(_conv_pre(u, w))
        du = _shift_up(dpre, CONV_W - 1) * w[0:1, :]
        for j in range(1, CONV_W):
            du = du + _shift_up(dpre, CONV_W - 1 - j) * w[j:j + 1, :]
        du_ref[...] = du.astype(BF16)
        for j in range(CONV_W):
            dw_ref[j:j + 1, :] = jnp.sum(dpre * _shift_down(u, CONV_W - 1 - j), axis=0, keepdims=True)

    return _pc(body, name, (2 * 512 // HD,),
               [pl.BlockSpec((t, HD), lambda c: (0, c)), pl.BlockSpec((CONV_W, HD), lambda c: (0, c)),
                pl.BlockSpec((t, HD), lambda c: (0, c))],
               [pl.BlockSpec((t, HD), lambda c: (0, c)), pl.BlockSpec((CONV_W, HD), lambda c: (0, c))],
               [_sds((t, 1024), BF16), _sds((CONV_W, 1024))])(z0, cw, dy)


def _mlstm_gates(gate, bias, m_in):
    L = gate.shape[0]
    r = lax.broadcasted_iota(jnp.int32, (L, L), 0)
    c = lax.broadcasted_iota(jnp.int32, (L, L), 1)
    eye, tril = r == c, c <= r
    i_col = gate[:, 0:1] + bias[:, 0:1]
    f_col = gate[:, 1:2] + bias[:, 1:2]
    logf_col = _log_sigmoid(f_col)
    logf_row = jnp.sum(jnp.where(eye, logf_col, 0.0), axis=0, keepdims=True)
    i_row = jnp.sum(jnp.where(eye, i_col, 0.0), axis=0, keepdims=True)
    b_col = jnp.sum(jnp.where(tril, logf_row, 0.0), axis=1, keepdims=True)
    b_row = jnp.sum(jnp.where(r <= c, logf_col, 0.0), axis=0, keepdims=True)
    logd = jnp.where(tril, b_col - b_row + i_row, NEG)
    inter = b_col + m_in
    m_t = jnp.maximum(inter, jnp.max(logd, axis=1, keepdims=True))
    w_t = jnp.exp(inter - m_t)
    dm = jnp.exp(logd - m_t)
    b_last = b_col[L - 1:L, :]
    log_in = b_last - b_col + i_col
    m_new = jnp.maximum(b_last + m_in, jnp.max(log_in, axis=0, keepdims=True))
    w_col = jnp.exp(log_in - m_new)
    decay = jnp.exp(b_last + m_in - m_new)
    return dict(eye=eye, r=r, c=c, f_col=f_col, m_t=m_t, w_t=w_t, dm=dm, m_new=m_new, w_col=w_col, decay=decay)


def _mlstm_fwd(qk, z0, gates, bias, name):
    t = qk.shape[0]
    nc, nh, L = t // CHUNK, 4, CHUNK
    scale = HD ** -0.5

    def body(q_ref, k_ref, v_ref, g_ref, b_ref, h_ref, cs_ref, ns_ref, ms_ref, c_s, n_s, m_s):
        @pl.when(pl.program_id(1) == 0)
        def _():
            c_s[...] = jnp.zeros_like(c_s)
            n_s[...] = jnp.zeros_like(n_s)
            m_s[...] = jnp.zeros_like(m_s)

        cm, nv, m_in = c_s[...], n_s[...], m_s[...]
        cs_ref[...] = cm
        ns_ref[...] = nv
        ms_ref[...] = jnp.broadcast_to(m_in, ms_ref.shape)
        q, kh, v = q_ref[...], k_ref[...] * scale, v_ref[...]
        G = _mlstm_gates(g_ref[...], b_ref[...], m_in)
        qb, kb, vb = q.astype(BF16), kh.astype(BF16), v.astype(BF16)
        sc = _nt(qb, kb) * G["dm"]
        num = _nn(sc.astype(BF16), vb) + G["w_t"] * _nn(qb, cm.astype(BF16))
        den = jnp.sum(sc, axis=1, keepdims=True) + G["w_t"] * jnp.sum(q * nv, axis=1, keepdims=True)
        h_ref[...] = num / jnp.maximum(jnp.abs(den), jnp.exp(-G["m_t"]))
        wk = G["w_col"] * kh
        c_s[...] = G["decay"] * cm + _tn(wk.astype(BF16), vb)
        n_s[...] = G["decay"] * nv + jnp.sum(wk, axis=0, keepdims=True)
        m_s[...] = G["m_new"]

    hspec = lambda off: pl.BlockSpec((L, HD), lambda h, j: (j, off + h))
    st = lambda r: pl.BlockSpec((None, None, r, HD), lambda h, j: (h, j, 0, 0))
    return _pc(body, name, (nh, nc),
               [hspec(0), hspec(4), hspec(8), pl.BlockSpec((None, L, 2), lambda h, j: (h, j, 0)),
                pl.BlockSpec((None, 1, 2), lambda h, j: (h, 0, 0))],
               [hspec(0), st(HD), st(1), st(1)],
               [_sds((t, 512)), _sds((nh, nc, HD, HD)), _sds((nh, nc, 1, HD)), _sds((nh, nc, 1, HD))],
               scratch=[pltpu.VMEM((HD, HD), F32), pltpu.VMEM((1, HD), F32), pltpu.VMEM((1, 1), F32)])(qk, qk, z0, gates, bias)


def _mlstm_bwd(qk, z0, gates, bias, cs, ns, ms, dh, name):
    t = qk.shape[0]
    nc, nh, L = t // CHUNK, 4, CHUNK
    scale = HD ** -0.5

    def body(q_ref, k_ref, v_ref, g_ref, b_ref, cs_ref, ns_ref, ms_ref, dh_ref, dq_ref, dk_ref, dv_ref, dg_ref, dc_s, dn_s):
        @pl.when(pl.program_id(1) == 0)
        def _():
            dc_s[...] = jnp.zeros_like(dc_s)
            dn_s[...] = jnp.zeros_like(dn_s)

        cm, nv, m_in = cs_ref[...], ns_ref[...], ms_ref[:, 0:1]
        q, kh, v = q_ref[...], k_ref[...] * scale, v_ref[...]
        G = _mlstm_gates(g_ref[...], b_ref[...], m_in)
        w_t, dmat, w_col, decay = G["w_t"], G["dm"], G["w_col"], G["decay"]
        qb, kb, vb, cb = q.astype(BF16), kh.astype(BF16), v.astype(BF16), cm.astype(BF16)
        s = _nt(qb, kb)
        sc = s * dmat
        scb = sc.astype(BF16)
        qc = _nn(qb, cb)
        qn = jnp.sum(q * nv, axis=1, keepdims=True)
        num = _nn(scb, vb) + w_t * qc
        den = jnp.sum(sc, axis=1, keepdims=True) + w_t * qn
        e_m = jnp.exp(-G["m_t"])
        dnm = jnp.maximum(jnp.abs(den), e_m)
        dh_ = dh_ref[...]
        dnum = dh_ / dnm
        dden = jnp.where(jnp.abs(den) > e_m, -jnp.sum(dh_ * num, axis=1, keepdims=True) / (dnm * dnm) * jnp.sign(den), 0.0)
        dnumb = dnum.astype(BF16)
        dsc = _nt(dnumb, vb) + dden
        dv = _tn(scb, dnumb)
        wd = w_t * dnum
        wdb = wd.astype(BF16)
        ds = dsc * dmat
        dsb = ds.astype(BF16)
        dq = _nt(wdb, cb) + (w_t * dden) * nv + _nn(dsb, kb)
        dc_o = _tn(qb, wdb)
        dn_o = jnp.sum(q * (w_t * dden), axis=0, keepdims=True)
        dw = jnp.sum(dnum * qc, axis=1, keepdims=True) + dden * qn
        dkh = _tn(dsb, qb)
        dlogd = ds * s
        db_col = jnp.sum(dlogd, axis=1, keepdims=True) + dw * w_t
        csum = jnp.sum(dlogd, axis=0, keepdims=True)
        dcn, dnn = dc_s[...], dn_s[...]
        dcnb = dcn.astype(BF16)
        kdc = _nn(kb, dcnb)
        dws = jnp.sum(kdc * v, axis=1, keepdims=True) + jnp.sum(kh * dnn, axis=1, keepdims=True)
        dv = dv + w_col * kdc
        dkh = dkh + w_col * (_nt(vb, dcnb) + dnn)
        dlin = dws * w_col
        ddecay = jnp.sum(jnp.sum(dcn * cm, axis=1, keepdims=True), axis=0, keepdims=True) + jnp.sum(dnn * nv, axis=1, keepdims=True)
        dlast = ddecay * decay + jnp.sum(dlin, axis=0, keepdims=True)
        rows = lax.broadcasted_iota(jnp.int32, (L, 1), 0)
        db_col = db_col - dlin + jnp.where(rows == L - 1, dlast, 0.0)
        eye, r, c = G["eye"], G["r"], G["c"]
        di = dlin + jnp.sum(jnp.where(eye, csum, 0.0), axis=1, keepdims=True)
        db_row = jnp.sum(jnp.where(eye, db_col, 0.0), axis=0, keepdims=True) - csum
        dlogf = jnp.sum(jnp.where(c >= r, db_row, 0.0), axis=1, keepdims=True)
        dg_ref[:, 0:1] = di
        dg_ref[:, 1:2] = dlogf * (1.0 - _sigmoid(G["f_col"]))
        dq_ref[...] = dq
        dk_ref[...] = dkh * scale
        dv_ref[...] = dv
        dc_s[...] = decay * dcn + dc_o
        dn_s[...] = decay * dnn + dn_o

    rv = lambda j: nc - 1 - j
    hspec = lambda off: pl.BlockSpec((L, HD), lambda h, j: (rv(j), off + h))
    st = lambda r: pl.BlockSpec((None, None, r, HD), lambda h, j: (h, rv(j), 0, 0))
    gs = pl.BlockSpec((None, L, 2), lambda h, j: (h, rv(j), 0))
    return _pc(body, name, (nh, nc),
               [hspec(0), hspec(4), hspec(8), gs, pl.BlockSpec((None, 1, 2), lambda h, j: (h, 0, 0)),
                st(HD), st(1), st(1), hspec(0)],
               [hspec(0), hspec(0), hspec(0), gs],
               [_sds((t, 512)), _sds((t, 512)), _sds((t, 512)), _sds((nh, t, 2))],
               scratch=[pltpu.VMEM((HD, HD), F32), pltpu.VMEM((1, HD), F32)])(qk, qk, z0, gates, bias, cs, ns, ms, dh)


def _hgrn_act(qb_, fb_, ib_, lg):
    lb = _sigmoid(lg[0:1, :] - lg[1:2, :])
    sg = _sigmoid(fb_)
    f = lb + (1.0 - lb) * sg
    return lb, sg, f, _silu(qb_), (1.0 - lb) * (1.0 - sg), _silu(ib_), _cumsum_rows(jnp.log(f))


def _hgrn_fwd(z0, lbl, name):
    t = z0.shape[0]
    nc, nh, L = t // CHUNK, 4, CHUNK

    def body(q_ref, f_ref, i_ref, l_ref, o_ref, ss_ref, st_s):
        @pl.when(pl.program_id(1) == 0)
        def _():
            st_s[...] = jnp.zeros_like(st_s)

        st = st_s[...]
        ss_ref[...] = st
        _, _, _, q, k, v, b = _hgrn_act(q_ref[...], f_ref[...], i_ref[...], l_ref[...])
        o = _nt((q * jnp.exp(b)).astype(BF16), st.astype(BF16))
        rows = _rows_of(b)
        for dl in range(L):
            e = jnp.exp(jnp.where(rows >= dl, b - pltpu.roll(b, dl, 0) if dl else b - b, NEG))
            a = jnp.sum(q * _shift_down(k, dl) * e, axis=1, keepdims=True)
            o = o + a * _shift_down(v, dl)
        o_ref[...] = o
        bl = b[L - 1:L, :]
        st_s[...] = st * jnp.exp(bl) + _tn(v.astype(BF16), (k * jnp.exp(bl - b)).astype(BF16))

    hspec = lambda off: pl.BlockSpec((L, HD), lambda h, j: (j, off + h))
    return _pc(body, name, (nh, nc),
               [hspec(16), hspec(20), hspec(24), pl.BlockSpec((2, HD), lambda h, j: (0, h))],
               [hspec(0), pl.BlockSpec((None, None, HD, HD), lambda h, j: (h, j, 0, 0))],
               [_sds((t, 512)), _sds((nh, nc, HD, HD))],
               scratch=[pltpu.VMEM((HD, HD), F32)])(z0, z0, z0, lbl)


def _hgrn_bwd(z0, lbl, ss, do, name):
    t = z0.shape[0]
    nc, nh, L = t // CHUNK, 4, CHUNK

    def body(q_ref, f_ref, i_ref, l_ref, ss_ref, do_ref, dq_ref, df_ref, di_ref, dl_ref, dst_s, dlb_s):
        j = pl.program_id(1)

        @pl.when(j == 0)
        def _():
            dst_s[...] = jnp.zeros_like(dst_s)
            dlb_s[...] = jnp.zeros_like(dlb_s)

        st = ss_ref[...]
        qp, fp, ip = q_ref[...], f_ref[...], i_ref[...]
        lb, sg, f, q, k, v, b = _hgrn_act(qp, fp, ip, l_ref[...])
        do_ = do_ref[...]
        dob, stb = do_.astype(BF16), st.astype(BF16)
        eb = jnp.exp(b)
        qe = q * eb
        dqe = _nn(dob, stb)
        dst_o = _tn(dob, qe.astype(BF16))
        dq = dqe * eb
        db = dqe * qe
        dk = jnp.zeros_like(q)
        dv = jnp.zeros_like(q)
        rows = _rows_of(b)
        for dl in range(L):
            up = (L - dl) % L
            kd, vd = _shift_down(k, dl), _shift_down(v, dl)
            e = jnp.exp(jnp.where(rows >= dl, b - pltpu.roll(b, dl, 0) if dl else b - b, NEG))
            a = jnp.sum(q * kd * e, axis=1, keepdims=True)
            p = jnp.sum(do_ * vd, axis=1, keepdims=True) * e
            dq = dq + p * kd
            dkd = p * q
            dbb = dkd * kd
            adv = a * do_
            if dl:
                dv = dv + pltpu.roll(adv, up, 0)
                dk = dk + pltpu.roll(dkd, up, 0)
                db = db + dbb - pltpu.roll(dbb, up, 0)
            else:
                dv = dv + adv
                dk = dk + dkd
        dstn = dst_s[...]
        dstnb = dstn.astype(BF16)
        bl = b[L - 1:L, :]
        ebl = jnp.exp(bl)
        kdec_e = jnp.exp(bl - b)
        kdec = k * kdec_e
        dbl = jnp.sum(dstn * st, axis=0, keepdims=True) * ebl
        dv = dv + _nt(kdec.astype(BF16), dstnb)
        dkdec = _nn(v.astype(BF16), dstnb)
        dk = dk + dkdec * kdec_e
        dx = dkdec * kdec
        dbl = dbl + jnp.sum(dx, axis=0, keepdims=True)
        db = db - dx + jnp.where(rows == L - 1, dbl, 0.0)
        dst_s[...] = dstn * ebl + dst_o
        dg = _rcumsum_rows(db)
        dfk = dg / f - dk
        dq_ref[...] = (dq * _dsilu(qp)).astype(BF16)
        di_ref[...] = (dv * _dsilu(ip)).astype(BF16)
        df_ref[...] = (dfk * (1.0 - lb) * sg * (1.0 - sg)).astype(BF16)
        dlb_s[...] += jnp.sum(dfk * (1.0 - sg), axis=0, keepdims=True)

        @pl.when(j == nc - 1)
        def _():
            dl0 = dlb_s[...] * lb * (1.0 - lb)
            dl_ref[0:1, :] = dl0
            dl_ref[1:2, :] = -dl0

    rv = lambda j: nc - 1 - j
    hspec = lambda off: pl.BlockSpec((L, HD), lambda h, j: (rv(j), off + h))
    return _pc(body, name, (nh, nc),
               [hspec(16), hspec(20), hspec(24), pl.BlockSpec((2, HD), lambda h, j: (0, h)),
                pl.BlockSpec((None, None, HD, HD), lambda h, j: (h, rv(j), 0, 0)), hspec(0)],
               [hspec(0), hspec(0), hspec(0), pl.BlockSpec((2, HD), lambda h, j: (0, h))],
               [_sds((t, 512), BF16), _sds((t, 512), BF16), _sds((t, 512), BF16), _sds((2, 512))],
               scratch=[pltpu.VMEM((HD, HD), F32), pltpu.VMEM((1, HD), F32)])(z0, z0, z0, lbl, ss, do)


def _post0_fwd(hm, hh, z0, na, nb, w, h0, name, bm=512):
    t = h0.shape[0]
    bm = min(bm, t)

    def body(hm_ref, hh_ref, oa_ref, gb_ref, na_ref, nb_ref, w_ref, h_ref, o_ref, y_ref):
        for hd in range(4):
            sl = slice(hd * HD, (hd + 1) * HD)
            pa = _sigmoid(oa_ref[:, sl]) * hm_ref[:, sl]
            y_ref[:, sl] = (pa * _rstd(pa) * na_ref[:, sl]).astype(BF16)
            xb = hh_ref[:, sl]
            y_ref[:, 512 + hd * HD:512 + (hd + 1) * HD] = (xb * _rstd(xb) * nb_ref[:, sl] * _silu(gb_ref[:, sl])).astype(BF16)
        o_ref[...] = h_ref[...] + _nn(y_ref[...], w_ref[...])

    row = lambda wd, c: pl.BlockSpec((bm, wd), lambda i: (i, c))
    vec = lambda wd: pl.BlockSpec((1, wd), lambda i: (0, 0))
    return _pc(body, name, (t // bm,),
               [row(512, 0), row(512, 0), row(512, 3), row(512, 7), vec(512), vec(512),
                pl.BlockSpec((D, D), lambda i: (0, 0)), row(D, 0)],
               [row(D, 0), row(D, 0)], [_sds((t, D)), _sds((t, D), BF16)])(hm, hh, z0, z0, na, nb, w, h0)


def _post0_bwd(dh1, w, hm, hh, z0, na, nb, name, bm=512):
    t = dh1.shape[0]
    bm = min(bm, t)

    def body(dh_ref, w_ref, hm_ref, hh_ref, oa_ref, gb_ref, na_ref, nb_ref, dhm_ref, dhh_ref, doa_ref, dgb_ref, dna_ref, dnb_ref):
        @pl.when(pl.program_id(0) == 0)
        def _():
            dna_ref[...] = jnp.zeros_like(dna_ref)
            dnb_ref[...] = jnp.zeros_like(dnb_ref)

        dy = _nt(dh_ref[...].astype(BF16), w_ref[...])
        for hd in range(4):
            sl = slice(hd * HD, (hd + 1) * HD)
            hm_, oa = hm_ref[:, sl], oa_ref[:, sl]
            sg = _sigmoid(oa)
            dpa, dgr = _rms_bwd(dy[:, sl], sg * hm_, na_ref[:, sl])
            dna_ref[:, sl] += jnp.sum(dgr, axis=0, keepdims=True)
            doa_ref[:, sl] = (dpa * hm_ * sg * (1.0 - sg)).astype(BF16)
            dhm_ref[:, sl] = dpa * sg
            xb, gb, nbv = hh_ref[:, sl], gb_ref[:, sl], nb_ref[:, sl]
            dyb = dy[:, 512 + hd * HD:512 + (hd + 1) * HD]
            dgb_ref[:, sl] = (dyb * (xb * _rstd(xb) * nbv) * _dsilu(gb)).astype(BF16)
            dxb, dgr2 = _rms_bwd(dyb * _silu(gb), xb, nbv)
            dnb_ref[:, sl] += jnp.sum(dgr2, axis=0, keepdims=True)
            dhh_ref[:, sl] = dxb

    row = lambda wd, c: pl.BlockSpec((bm, wd), lambda i: (i, c))
    vec = lambda wd: pl.BlockSpec((1, wd), lambda i: (0, 0))
    return _pc(body, name, (t // bm,),
               [row(D, 0), pl.BlockSpec((D, D), lambda i: (0, 0)), row(512, 0), row(512, 0), row(512, 3), row(512, 7),
                vec(512), vec(512)],
               [row(512, 0), row(512, 0), row(512, 0), row(512, 0), vec(512), vec(512)],
               [_sds((t, 512)), _sds((t, 512)), _sds((t, 512), BF16), _sds((t, 512), BF16), _sds((1, 512)), _sds((1, 512))],
               )(dh1, w, hm, hh, z0, z0, na, nb)


def _memkv_fwd(mem, g, wkv_s, name):
    m = mem.shape[0]

    def body(x_ref, g_ref, w_ref, kv_ref, mn_ref):
        x = x_ref[...]
        mn = (x * _rstd(x) * g_ref[...]).astype(BF16)
        mn_ref[...] = mn
        kv_ref[...] = _nn(mn, w_ref[...])

    return _pc(body, name, (4,),
               [pl.BlockSpec((m, D), lambda k: (0, 0)), pl.BlockSpec((1, D), lambda k: (0, 0)),
                pl.BlockSpec((None, D, 512), lambda k: (k, 0, 0))],
               [pl.BlockSpec((m, 512), lambda k: (0, k)), pl.BlockSpec((m, D), lambda k: (0, 0))],
               [_sds((m, 2048)), _sds((m, D), BF16)])(mem, g, wkv_s)


def _memkv_bwd(dkv, wkv_s, mem, g, name):
    m = mem.shape[0]

    def body(d_ref, w_ref, x_ref, g_ref, dg_ref, acc):
        k = pl.program_id(0)

        @pl.when(k == 0)
        def _():
            acc[...] = jnp.zeros_like(acc)

        acc[...] += _nt(d_ref[...].astype(BF16), w_ref[...])

        @pl.when(k == 3)
        def _():
            _, dgr = _rms_bwd(acc[...], x_ref[...], g_ref[...])
            dg_ref[...] = jnp.sum(dgr, axis=0, keepdims=True)

    return _pc(body, name, (4,),
               [pl.BlockSpec((m, 512), lambda k: (0, k)), pl.BlockSpec((None, D, 512), lambda k: (k, 0, 0)),
                pl.BlockSpec((m, D), lambda k: (0, 0)), pl.BlockSpec((1, D), lambda k: (0, 0))],
               pl.BlockSpec((1, D), lambda k: (0, 0)), _sds((1, D)), scratch=[pltpu.VMEM((m, D), F32)])(dkv, wkv_s, mem, g)


def _xattn_probs(qh, kh):
    s = _nt(qh, kh) * (XD ** -0.5)
    p = jnp.exp(s - jnp.max(s, axis=1, keepdims=True))
    return p / jnp.sum(p, axis=1, keepdims=True)


def _xattn_fwd(q, kv, wo, h1, name, bm=512):
    t, m = q.shape[0], kv.shape[0]
    bm = min(bm, t)

    def body(q_ref, k_ref, v_ref, w_ref, h_ref, out_ref, o_ref):
        for hd in range(D // XD):
            sl = slice(hd * XD, (hd + 1) * XD)
            p = _xattn_probs(q_ref[:, sl].astype(BF16), k_ref[:, sl].astype(BF16))
            o_ref[:, sl] = _nn(p.astype(BF16), v_ref[:, sl].astype(BF16)).astype(BF16)
        out_ref[...] = h_ref[...] + _nn(o_ref[...], w_ref[...])

    row = pl.BlockSpec((bm, D), lambda i: (i, 0))
    return _pc(body, name, (t // bm,),
               [row, pl.BlockSpec((m, D), lambda i: (0, 0)), pl.BlockSpec((m, D), lambda i: (0, 1)),
                pl.BlockSpec((D, D), lambda i: (0, 0)), row],
               [row, row], [_sds((t, D)), _sds((t, D), BF16)])(q, kv, kv, wo, h1)


def _xattn_bwd(dh2, q, kv, wo, name, bm=512):
    t, m = q.shape[0], kv.shape[0]
    bm = min(bm, t)

    def body(dh_ref, q_ref, k_ref, v_ref, w_ref, dq_ref, dkv_ref):
        @pl.when(pl.program_id(0) == 0)
        def _():
            dkv_ref[...] = jnp.zeros_like(dkv_ref)

        d_o = _nt(dh_ref[...].astype(BF16), w_ref[...])
        for hd in range(D // XD):
            sl = slice(hd * XD, (hd + 1) * XD)
            qh, kh, vh = q_ref[:, sl].astype(BF16), k_ref[:, sl].astype(BF16), v_ref[:, sl].astype(BF16)
            p = _xattn_probs(qh, kh)
            dob = d_o[:, sl].astype(BF16)
            dp = _nt(dob, vh)
            dkv_ref[:, D + hd * XD:D + (hd + 1) * XD] += _tn(p.astype(BF16), dob)
            ds = (p * (dp - jnp.sum(dp * p, axis=1, keepdims=True)) * (XD ** -0.5)).astype(BF16)
            dq_ref[:, sl] = _nn(ds, kh).astype(BF16)
            dkv_ref[:, sl] += _tn(ds, qh)

    row = pl.BlockSpec((bm, D), lambda i: (i, 0))
    return _pc(body, name, (t // bm,),
               [row, row, pl.BlockSpec((m, D), lambda i: (0, 0)), pl.BlockSpec((m, D), lambda i: (0, 1)),
                pl.BlockSpec((D, D), lambda i: (0, 0))],
               [row, pl.BlockSpec((m, 2 * D), lambda i: (0, 0))],
               [_sds((t, D), BF16), _sds((m, 2 * D))])(dh2, q, kv, kv, wo)


NH1 = 8
FOX_BM = 512
FOX_BQ = 512
FOX_BK = 512


def _foxprep_fwd(z1, qg, kg, fbp, name):
    t = z1.shape[0]
    bm = min(FOX_BM, t)

    def body(q_ref, k_ref, v_ref, f_ref, qg_ref, kg_ref, fb_ref, qn_ref, kn_ref, vb_ref, c_ref, carry):
        @pl.when(pl.program_id(0) == 0)
        def _():
            carry[...] = jnp.zeros_like(carry)

        for hd in range(NH1):
            sl = slice(hd * HD, (hd + 1) * HD)
            x = q_ref[:, sl]
            qn_ref[:, sl] = (x * _rstd(x) * qg_ref[...]).astype(BF16)
            x = k_ref[:, sl]
            kn_ref[:, sl] = (x * _rstd(x) * kg_ref[...]).astype(BF16)
        vb_ref[...] = v_ref[...].astype(BF16)
        c = carry[...] + _cumsum_rows(_log_sigmoid(f_ref[...] + fb_ref[...]))
        c_ref[...] = c
        carry[...] = c[bm - 1:bm, :]

    row = lambda c: pl.BlockSpec((bm, D), lambda i: (i, c))
    lane = pl.BlockSpec((bm, HD), lambda i: (i, 4 * D // HD))
    vec = pl.BlockSpec((1, HD), lambda i: (0, 0))
    return _pc(body, name, (t // bm,), [row(0), row(1), row(2), lane, vec, vec, vec],
               [row(0), row(0), row(0), pl.BlockSpec((bm, HD), lambda i: (i, 0))],
               [_sds((t, D), BF16), _sds((t, D), BF16), _sds((t, D), BF16), _sds((t, HD))],
               scratch=[pltpu.VMEM((1, HD), F32)])(z1, z1, z1, z1, qg, kg, fbp)


def _foxprep_bwd(dqn, dkn, z1, qg, kg, fbp, dc, name):
    t = z1.shape[0]
    bm = min(FOX_BM, t)
    nb = t // bm

    def body(dqn_ref, dkn_ref, q_ref, k_ref, f_ref, qg_ref, kg_ref, fb_ref, dc_ref,
             dq_ref, dk_ref, df_ref, dqg_ref, dkg_ref, dfb_ref, carry):
        @pl.when(pl.program_id(0) == 0)
        def _():
            carry[...] = jnp.zeros_like(carry)
            dqg_ref[...] = jnp.zeros_like(dqg_ref)
            dkg_ref[...] = jnp.zeros_like(dkg_ref)
            dfb_ref[...] = jnp.zeros_like(dfb_ref)

        for hd in range(NH1):
            sl = slice(hd * HD, (hd + 1) * HD)
            dx, dgr = _rms_bwd(dqn_ref[:, sl], q_ref[:, sl], qg_ref[...])
            dq_ref[:, sl] = dx.astype(BF16)
            dqg_ref[...] += jnp.sum(dgr, axis=0, keepdims=True)
            dx, dgr = _rms_bwd(dkn_ref[:, sl], k_ref[:, sl], kg_ref[...])
            dk_ref[:, sl] = dx.astype(BF16)
            dkg_ref[...] += jnp.sum(dgr, axis=0, keepdims=True)
        dc_ = dc_ref[...]
        dlogf = _rcumsum_rows(dc_) + carry[...]
        carry[...] += jnp.sum(dc_, axis=0, keepdims=True)
        lanes = lax.broadcasted_iota(jnp.int32, dc_.shape, 1)
        df = jnp.where(lanes < NH1, dlogf * (1.0 - _sigmoid(f_ref[...] + fb_ref[...])), 0.0)
        df_ref[...] = df.astype(BF16)
        dfb_ref[...] += jnp.sum(df, axis=0, keepdims=True)

    rv = lambda i: nb - 1 - i
    row = lambda c: pl.BlockSpec((bm, D), lambda i: (rv(i), c))
    lane = lambda c: pl.BlockSpec((bm, HD), lambda i: (rv(i), c))
    vec = pl.BlockSpec((1, HD), lambda i: (0, 0))
    return _pc(body, name, (nb,), [row(0), row(0), row(0), row(1), lane(4 * D // HD), vec, vec, vec, lane(0)],
               [row(0), row(0), lane(0), vec, vec, vec],
               [_sds((t, D), BF16), _sds((t, D), BF16), _sds((t, HD), BF16), _sds((1, HD)), _sds((1, HD)), _sds((1, HD))],
               scratch=[pltpu.VMEM((1, HD), F32)])(dqn, dkn, z1, z1, z1, qg, kg, fbp, dc)


def _fox_scores(q, k, cq, ck, i, j, bq, bk):
    s = _nt(q, k) * (HD ** -0.5) + cq - ck
    rows = i * bq + lax.broadcasted_iota(jnp.int32, s.shape, 0)
    cols = j * bk + lax.broadcasted_iota(jnp.int32, s.shape, 1)
    return s, cols <= rows


def _fox_fwd(qn, kn, vb, c3, crow, name):
    t = qn.shape[0]
    bq, bk = min(FOX_BQ, t), min(FOX_BK, t)
    nq, nk = t // bq, t // bk

    def body(q_ref, k_ref, v_ref, cq_ref, ck_ref, o_ref, lse_ref, m_s, l_s, acc):
        i, j = pl.program_id(1), pl.program_id(2)

        @pl.when(j == 0)
        def _():
            m_s[...] = jnp.full_like(m_s, NEG)
            l_s[...] = jnp.zeros_like(l_s)
            acc[...] = jnp.zeros_like(acc)

        @pl.when(j * bk < (i + 1) * bq)
        def _():
            s, ok = _fox_scores(q_ref[...], k_ref[...], cq_ref[...], ck_ref[...], i, j, bq, bk)
            s = jnp.where(ok, s, NEG)
            m_new = jnp.maximum(m_s[...], jnp.max(s, axis=1, keepdims=True))
            alpha = jnp.exp(m_s[...] - m_new)
            p = jnp.where(ok, jnp.exp(s - m_new), 0.0)
            l_s[...] = alpha * l_s[...] + jnp.sum(p, axis=1, keepdims=True)
            acc[...] = alpha * acc[...] + _nn(p.astype(BF16), v_ref[...])
            m_s[...] = m_new

        @pl.when(j == nk - 1)
        def _():
            o_ref[...] = acc[...] / l_s[...]
            lse_ref[...] = m_s[...] + jnp.log(l_s[...])

    kmap = lambda h, i, j: (jnp.minimum(j, ((i + 1) * bq - 1) // bk), h)
    return _pc(body, name, (NH1, nq, nk),
               [pl.BlockSpec((bq, HD), lambda h, i, j: (i, h)), pl.BlockSpec((bk, HD), kmap), pl.BlockSpec((bk, HD), kmap),
                pl.BlockSpec((None, bq, 1), lambda h, i, j: (h, i, 0)),
                pl.BlockSpec((None, 1, bk), lambda h, i, j: (h, 0, jnp.minimum(j, ((i + 1) * bq - 1) // bk)))],
               [pl.BlockSpec((bq, HD), lambda h, i, j: (i, h)), pl.BlockSpec((None, bq, 1), lambda h, i, j: (h, i, 0))],
               [_sds((t, D)), _sds((NH1, t, 1))],
               scratch=[pltpu.VMEM((bq, 1), F32), pltpu.VMEM((bq, 1), F32), pltpu.VMEM((bq, HD), F32)])(qn, kn, vb, c3, crow)


def _fox_bwd(qn, kn, vb, c3, crow, lse, delta, do, name):
    t = qn.shape[0]
    bq, bk = min(FOX_BQ, t), min(FOX_BK, t)
    nq, nk = t // bq, t // bk

    def body(q_ref, k_ref, v_ref, cq_ref, ck_ref, lse_ref, dl_ref, do_ref, dq_ref, dk_ref, dv_ref, dc_ref, dcq_ref, dk_s, dv_s, dc_s):
        j, i = pl.program_id(1), pl.program_id(2)

        @pl.when(i == 0)
        def _():
            dk_s[...] = jnp.zeros_like(dk_s)
            dv_s[...] = jnp.zeros_like(dv_s)
            dc_s[...] = jnp.zeros_like(dc_s)

        @pl.when((i == 0) & (j == 0))
        def _():
            dq_ref[...] = jnp.zeros_like(dq_ref)
            dcq_ref[...] = jnp.zeros_like(dcq_ref)

        @pl.when(j * bk < (i + 1) * bq)
        def _():
            q, k = q_ref[...], k_ref[...]
            s, ok = _fox_scores(q, k, cq_ref[...], ck_ref[...], i, j, bq, bk)
            p = jnp.where(ok, jnp.exp(s - lse_ref[...]), 0.0)
            dob = do_ref[...]
            dv_s[...] += _tn(p.astype(BF16), dob)
            ds = p * (_nt(dob, v_ref[...]) - dl_ref[...])
            dsb = ds.astype(BF16)
            rows = pl.ds(pl.multiple_of(i * bq, bq), bq)
            dq_ref[rows, :] += _nn(dsb, k) * (HD ** -0.5)
            dk_s[...] += _tn(dsb, q) * (HD ** -0.5)
            dc_s[...] -= jnp.sum(ds, axis=0, keepdims=True)
            dcq_ref[rows, :] += jnp.sum(ds, axis=1, keepdims=True)

        @pl.when(i == nq - 1)
        def _():
            dk_ref[...] = dk_s[...]
            dv_ref[...] = dv_s[...]
            dc_ref[...] = dc_s[...]

    qi = lambda i, j: jnp.maximum(i, (j * bk) // bq)
    qmap = lambda h, j, i: (qi(i, j), h)
    c3map = lambda h, j, i: (h, qi(i, j), 0)
    return _pc(body, name, (NH1, nk, nq),
               [pl.BlockSpec((bq, HD), qmap), pl.BlockSpec((bk, HD), lambda h, j, i: (j, h)),
                pl.BlockSpec((bk, HD), lambda h, j, i: (j, h)), pl.BlockSpec((None, bq, 1), c3map),
                pl.BlockSpec((None, 1, bk), lambda h, j, i: (h, 0, j)), pl.BlockSpec((None, bq, 1), c3map),
                pl.BlockSpec((None, bq, 1), c3map), pl.BlockSpec((bq, HD), qmap)],
               [pl.BlockSpec((t, HD), lambda h, j, i: (0, h)), pl.BlockSpec((bk, HD), lambda h, j, i: (j, h)),
                pl.BlockSpec((bk, HD), lambda h, j, i: (j, h)), pl.BlockSpec((None, 1, bk), lambda h, j, i: (h, 0, j)),
                pl.BlockSpec((None, t, 1), lambda h, j, i: (h, 0, 0))],
               [_sds((t, D)), _sds((t, D)), _sds((t, D)), _sds((NH1, 1, t)), _sds((NH1, t, 1))],
               scratch=[pltpu.VMEM((bk, HD), F32), pltpu.VMEM((bk, HD), F32), pltpu.VMEM((1, bk), F32)],
               )(qn, kn, vb, c3, crow, lse, delta, do)


def _post1_fwd(o, z1, w, h3, name, bm=512):
    t = o.shape[0]
    bm = min(bm, t)

    def body(o_ref, g_ref, w_ref, h_ref, out_ref, og_ref):
        og_ref[...] = (o_ref[...] * _sigmoid(g_ref[...])).astype(BF16)
        out_ref[...] = h_ref[...] + _nn(og_ref[...], w_ref[...])

    row = lambda c: pl.BlockSpec((bm, D), lambda i: (i, c))
    return _pc(body, name, (t // bm,), [row(0), row(3), pl.BlockSpec((D, D), lambda i: (0, 0)), row(0)],
               [row(0), row(0)], [_sds((t, D)), _sds((t, D), BF16)])(o, z1, w, h3)


def _post1_bwd(dh4, w, o, z1, name, bm=512):
    t = o.shape[0]
    bm = min(bm, t)

    def body(dh_ref, w_ref, o_ref, g_ref, do_ref, dg_ref, dl_ref):
        d_og = _nt(dh_ref[...].astype(BF16), w_ref[...])
        o_, sg = o_ref[...], _sigmoid(g_ref[...])
        dob = (d_og * sg).astype(BF16)
        do_ref[...] = dob
        dg_ref[...] = (d_og * o_ * sg * (1.0 - sg)).astype(BF16)
        prod = dob.astype(F32) * o_
        for hd in range(NH1):
            dl_ref[hd] = jnp.sum(prod[:, hd * HD:(hd + 1) * HD], axis=1, keepdims=True)

    row = lambda c: pl.BlockSpec((bm, D), lambda i: (i, c))
    return _pc(body, name, (t // bm,), [row(0), pl.BlockSpec((D, D), lambda i: (0, 0)), row(0), row(3)],
               [row(0), row(0), pl.BlockSpec((NH1, bm, 1), lambda i: (0, i, 0))],
               [_sds((t, D), BF16), _sds((t, D), BF16), _sds((NH1, t, 1))])(dh4, w, o, z1)


def _final(h, g, tgt, name, bm=512):
    t = h.shape[0]
    bm = min(bm, t)

    def body(h_ref, g_ref, t_ref, l_ref, dh_ref, dg_ref):
        @pl.when(pl.program_id(0) == 0)
        def _():
            l_ref[...] = jnp.zeros_like(l_ref)
            dg_ref[...] = jnp.zeros_like(dg_ref)

        x, gv = h_ref[...], g_ref[...]
        r = _rstd(x)
        xh = x * r
        e = xh * gv - t_ref[...]
        l_ref[...] += 0.5 * jnp.sum(jnp.mean(e * e, axis=1, keepdims=True), axis=0, keepdims=True)
        dy = e * (1.0 / D)
        dg_ref[...] += jnp.sum(dy * xh, axis=0, keepdims=True)
        dxh = dy * gv
        dh_ref[...] = r * (dxh - xh * jnp.mean(dxh * xh, axis=1, keepdims=True))

    row = pl.BlockSpec((bm, D), lambda i: (i, 0))
    vec = pl.BlockSpec((1, D), lambda i: (0, 0))
    return _pc(body, name, (t // bm,), [row, vec, row], [pl.BlockSpec((1, HD), lambda i: (0, 0)), row, vec],
               [_sds((1, HD)), _sds((t, D)), _sds((1, D))])(h, g, tgt)


def _adam(w, g, m, v, name):
    r, c = w.shape
    br = min(r, 256)

    def body(w_ref, g_ref, m_ref, v_ref, d_ref, mo_ref, vo_ref):
        gv = g_ref[...]
        mn = ADAM_B1 * m_ref[...] + (1.0 - ADAM_B1) * gv
        vn = ADAM_B2 * v_ref[...] + (1.0 - ADAM_B2) * jnp.square(gv)
        m_hat = mn / (1.0 - ADAM_B1 ** ADAM_STEP)
        v_hat = vn / (1.0 - ADAM_B2 ** ADAM_STEP)
        d_ref[...] = -ADAM_LR * (m_hat / (jnp.sqrt(v_hat) + ADAM_EPS) + ADAM_WD * w_ref[...])
        mo_ref[...] = mn
        vo_ref[...] = vn

    blk = pl.BlockSpec((br, c), lambda i: (i, 0))
    return _pc(body, name, (r // br,), [blk] * 4, [blk] * 3, [_sds((r, c))] * 3)(w, g, m, v)


ZW = 4224
GATE0 = 4096


def _pack_w_in0(w):
    return jnp.concatenate([w[:, :2048], w[:, 2056:], w[:, 2048:2056], jnp.zeros((w.shape[0], ZW - 4104), w.dtype)], axis=1)


def _unpack_w_in0(g):
    return jnp.concatenate([g[:, :2048], g[:, GATE0:GATE0 + 8], g[:, 2048:GATE0]], axis=1)


def _pack_w_in1(w):
    return jnp.concatenate([w, jnp.zeros((w.shape[0], ZW - 4104), w.dtype)], axis=1)


def _unpack_w_in1(g):
    return g[:, :4104]


def _local_step(x, mem, tgt, W, S):
    t = x.shape[0]
    row = lambda v: v.reshape(1, -1)
    G = {}

    kv, mn = _memkv_fwd(mem, row(S["mem_norm_g"]), W["wkv_s"], "memkv_fwd")
    z0, u0 = _norm_mm(x, S["norm_mix_g"][0:1], W["w_in0"], "in0_fwd")
    qk = _conv_fwd(z0, S["conv_w"], "conv_fwd")
    g8 = z0[:, GATE0:GATE0 + 8]
    gates3 = jnp.stack([g8[:, :4].T, g8[:, 4:].T], axis=-1)
    gb = S["gate_b"]
    bias3 = jnp.stack([gb[0, :4], gb[0, 4:]], axis=-1)[:, None, :]
    hm, cs, ns, ms = _mlstm_fwd(qk, z0, gates3, bias3, "mlstm_fwd")
    hh, ss = _hgrn_fwd(z0, S["lb_logits"], "hgrn_fwd")
    h1, y0 = _post0_fwd(hm, hh, z0, S["mlstm_norm_g"], S["hgrn_norm_g"], W["w_out0"], x, "post0_fwd")

    def xattn_mlp_fwd(h, l):
        q, ux = _norm_mm(h, S["norm_xattn_g"][l:l + 1], W["wq"][l], f"xq{l}_fwd")
        h2, ox = _xattn_fwd(q, kv, W["wo"][l], h, f"xattn{l}_fwd")
        h3, a, um = _mlp_fwd(h2, S["norm_mlp_g"][l:l + 1], W["w1s"][l], W["w2"][l], f"mlp{l}_fwd")
        return h3, (h, q, ux, ox, h2, a, um)

    h3, sv0 = xattn_mlp_fwd(h1, 0)
    z1, u1 = _norm_mm(h3, S["norm_mix_g"][1:2], W["w_in1"], "in1_fwd")
    fbp = jnp.pad(S["c_fgate_b"], ((0, 0), (0, HD - NH1)))
    qn, kn, vb, c = _foxprep_fwd(z1, S["c_qnorm_g"], S["c_knorm_g"], fbp, "foxprep_fwd")
    c8t = c[:, :NH1].T
    c3, crow = c8t[:, :, None], c8t[:, None, :]
    o1, lse = _fox_fwd(qn, kn, vb, c3, crow, "fox_fwd")
    h4, og = _post1_fwd(o1, z1, W["w_out1"], h3, "post1_fwd")
    h6, sv1 = xattn_mlp_fwd(h4, 1)
    lossp, dh, G["final_norm_g"] = _final(h6, row(S["final_norm_g"]), tgt, "final")

    dkv = None
    dgx, dgm, dwq, dwo, dw1, dw2 = [None, None], [None, None], [None, None], [None, None], [None, None], [None, None]

    def xattn_mlp_bwd(dh, l, sv):
        nonlocal dkv
        h, q, ux, ox, h2, a, um = sv
        dh2, da, r, dgm[l] = _mlp_bwd(dh, a, W["w1s"][l], W["w2"][l], h2, S["norm_mlp_g"][l:l + 1], f"mlp{l}_bwd")
        dw1[l] = _mm_tn(um, da, f"mlp{l}_dw1")
        dw2[l] = _mm_tn(r, dh, f"mlp{l}_dw2")
        dq, dkv_l = _xattn_bwd(dh2, q, kv, W["wo"][l], f"xattn{l}_bwd")
        dkv = dkv_l if dkv is None else dkv + dkv_l
        dwo[l] = _mm_tn(ox, dh2, f"xattn{l}_dwo")
        dwq[l] = _mm_tn(ux, dq, f"xattn{l}_dwq")
        dh1, dgx[l] = _bwd_in(dq, W["wq"][l], h, S["norm_xattn_g"][l:l + 1], dh2, f"xq{l}_bwd")
        return dh1

    dh4 = xattn_mlp_bwd(dh, 1, sv1)
    do, dgate, delta = _post1_bwd(dh4, W["w_out1"], o1, z1, "post1_bwd")
    G["w_out1"] = _mm_tn(og, dh4, "post1_dw")
    dqn, dkn, dv1, dcrow, dcq = _fox_bwd(qn, kn, vb, c3, crow, lse, delta, do, "fox_bwd")
    dc = jnp.pad((dcrow[:, 0, :] + dcq[:, :, 0]).T, ((0, 0), (0, HD - NH1)))
    dqr, dkr, df1, G["c_qnorm_g"], G["c_knorm_g"], dfb = _foxprep_bwd(
        dqn, dkn, z1, S["c_qnorm_g"], S["c_knorm_g"], fbp, dc, "foxprep_bwd")
    G["c_fgate_b"] = dfb[:, :NH1]
    dz1 = jnp.concatenate([dqr, dkr, dv1.astype(BF16), dgate, df1], axis=1)
    G["w_in1"] = _mm_tn(u1, dz1, "in1_dw")
    dh3, dgmix1 = _bwd_in(dz1, W["w_in1"], h3, S["norm_mix_g"][1:2], dh4, "in1_bwd")
    dh1 = xattn_mlp_bwd(dh3, 0, sv0)

    dhm, dhh, doa, dgb, G["mlstm_norm_g"], G["hgrn_norm_g"] = _post0_bwd(
        dh1, W["w_out0"], hm, hh, z0, S["mlstm_norm_g"], S["hgrn_norm_g"], "post0_bwd")
    G["w_out0"] = _mm_tn(y0, dh1, "post0_dw")
    dqa, dka, dva, dgates3 = _mlstm_bwd(qk, z0, gates3, bias3, cs, ns, ms, dhm, "mlstm_bwd")
    dqb, dfb0, dib, G["lb_logits"] = _hgrn_bwd(z0, S["lb_logits"], ss, dhh, "hgrn_bwd")
    duc, G["conv_w"] = _conv_bwd(z0, S["conv_w"], jnp.concatenate([dqa, dka], axis=1), "conv_bwd")
    dg8 = jnp.concatenate([dgates3[:, :, 0].T, dgates3[:, :, 1].T], axis=1)
    G["gate_b"] = jnp.sum(dg8, axis=0, keepdims=True)
    dz0 = jnp.concatenate([duc, dva.astype(BF16), doa, dqb, dfb0, dib, dgb,
                           jnp.pad(dg8, ((0, 0), (0, HD - 8))).astype(BF16)], axis=1)
    G["w_in0"] = _mm_tn(u0, dz0, "in0_dw")
    dx, dgmix0 = _bwd_in(dz0, W["w_in0"], x, S["norm_mix_g"][0:1], dh1, "in0_bwd")

    G["wkv"] = _mm_tn(mn, dkv, "memkv_dw")
    G["mem_norm_g"] = _memkv_bwd(dkv, W["wkv_s"], mem, row(S["mem_norm_g"]), "memkv_bwd")
    G["norm_mix_g"] = jnp.concatenate([dgmix0, dgmix1], axis=0)
    G["norm_xattn_g"] = jnp.concatenate(dgx, axis=0)
    G["norm_mlp_g"] = jnp.concatenate(dgm, axis=0)
    G["wq"], G["wo"], G["w1"], G["w2"] = jnp.stack(dwq), jnp.stack(dwo), jnp.stack(dw1), jnp.stack(dw2)
    return lossp[0, 0], dx, G


ANY = pl.BlockSpec(memory_space=pl.ANY)
NCHIP = 4
RS_ROWS = 4224
RS_TILE = 384


def _place():
    x, y, c = lax.axis_index("x"), lax.axis_index("y"), lax.axis_index("c")
    return x, y, c, [(1 - x, y), (x, 1 - y), (1 - x, 1 - y)]


def _comm_call(body, name, ins, out_shapes, sems):
    return pl.pallas_call(body, name=name, in_specs=[ANY] * len(ins), out_specs=[ANY] * len(out_shapes),
                          out_shape=out_shapes, scratch_shapes=sems)(*ins)


def _all_gather_chips(arrs, name):
    n = len(arrs)

    def body(*refs):
        ins, outs = refs[:n], refs[n:2 * n]
        send, recv, loc = refs[2 * n:]
        x, y, c, chips = _place()
        me = 2 * x + y
        copies = []
        for a in range(n):
            cp = pltpu.make_async_copy(ins[a], outs[a].at[me], loc.at[a])
            cp.start()
            copies.append(cp)
            for k, (px, py) in enumerate(chips):
                r = pltpu.make_async_remote_copy(src_ref=ins[a], dst_ref=outs[a].at[me], send_sem=send.at[a, k],
                                                 recv_sem=recv.at[a, k], device_id=(px, py, c), device_id_type=MESH)
                r.start()
                copies.append(r)
        for cp in copies:
            cp.wait()

    return _comm_call(body, name, arrs, [_sds((NCHIP,) + a.shape, a.dtype) for a in arrs],
                      [pltpu.SemaphoreType.DMA((n, 3)), pltpu.SemaphoreType.DMA((n, 3)), pltpu.SemaphoreType.DMA((n,))])


def _pair_exchange(gb, name):
    def body(g_ref, o_ref, send, recv):
        x, y, c, _ = _place()
        cp = pltpu.make_async_remote_copy(src_ref=g_ref.at[1 - c], dst_ref=o_ref, send_sem=send, recv_sem=recv,
                                          device_id=(x, y, 1 - c), device_id_type=MESH)
        cp.start()
        cp.wait()

    return _comm_call(body, name, [gb], [_sds(gb.shape[1:], gb.dtype)], [pltpu.SemaphoreType.DMA, pltpu.SemaphoreType.DMA])[0]


def _chip_exchange(p, name):
    def body(p_ref, o_ref, send, recv, loc):
        x, y, c, chips = _place()
        me = 2 * x + y
        own = pltpu.make_async_copy(p_ref.at[me], o_ref.at[me], loc)
        own.start()
        copies = [own]
        for k, (px, py) in enumerate(chips):
            r = pltpu.make_async_remote_copy(src_ref=p_ref.at[2 * px + py], dst_ref=o_ref.at[me], send_sem=send.at[k],
                                             recv_sem=recv.at[k], device_id=(px, py, c), device_id_type=MESH)
            r.start()
            copies.append(r)
        for cp in copies:
            cp.wait()

    return _comm_call(body, name, [p], [_sds(p.shape, p.dtype)],
                      [pltpu.SemaphoreType.DMA((3,)), pltpu.SemaphoreType.DMA((3,)), pltpu.SemaphoreType.DMA])[0]


def _pair_bcast(r, name):
    def body(r_ref, o_ref, send, recv, loc):
        x, y, c, _ = _place()
        own = pltpu.make_async_copy(r_ref, o_ref.at[c], loc)
        own.start()
        cp = pltpu.make_async_remote_copy(src_ref=r_ref, dst_ref=o_ref.at[c], send_sem=send, recv_sem=recv,
                                          device_id=(x, y, 1 - c), device_id_type=MESH)
        cp.start()
        own.wait()
        cp.wait()

    return _comm_call(body, name, [r], [_sds((2,) + r.shape, r.dtype)],
                      [pltpu.SemaphoreType.DMA, pltpu.SemaphoreType.DMA, pltpu.SemaphoreType.DMA])[0]


def _all_gather_devices(v, name):
    def body(v_ref, o_ref, send, recv, loc):
        x, y, c, _ = _place()
        me = 4 * x + 2 * y + c
        own = pltpu.make_async_copy(v_ref, o_ref.at[me], loc)
        own.start()
        copies = [own]
        for k in range(1, 8):
            fx, fy, fc = (k >> 2) & 1, (k >> 1) & 1, k & 1
            peer = (x ^ fx, y ^ fy, c ^ fc)
            r = pltpu.make_async_remote_copy(src_ref=v_ref, dst_ref=o_ref.at[me], send_sem=send.at[k - 1],
                                             recv_sem=recv.at[k - 1], device_id=peer, device_id_type=MESH)
            r.start()
            copies.append(r)
        for cp in copies:
            cp.wait()

    return _comm_call(body, name, [v], [_sds((8,) + v.shape, v.dtype)],
                      [pltpu.SemaphoreType.DMA((7,)), pltpu.SemaphoreType.DMA((7,)), pltpu.SemaphoreType.DMA])[0]


def _sum_slots(a, out_dtype, name, extra=None):
    n, r, w = a.shape
    br = RS_TILE if r % RS_TILE == 0 else r

    def body(*refs):
        a_ref, o_ref = refs[0], refs[-1]
        acc = a_ref[0].astype(F32)
        for s in range(1, n):
            acc = acc + a_ref[s].astype(F32)
        if extra is not None:
            acc = acc + refs[1][...].astype(F32)
        o_ref[...] = acc.astype(out_dtype)

    ins = [a] + ([extra] if extra is not None else [])
    specs = [pl.BlockSpec((n, br, w), lambda i: (0, i, 0))] + ([pl.BlockSpec((br, w), lambda i: (i, 0))] if extra is not None else [])
    return _pc(body, name, (r // br,), specs, pl.BlockSpec((br, w), lambda i: (i, 0)), _sds((r, w), out_dtype))(*ins)


SMALL = ["norm_mix_g", "norm_xattn_g", "norm_mlp_g", "final_norm_g", "mem_norm_g", "hgrn_lb_logits", "mlstm_norm_g",
         "hgrn_norm_g", "c_qnorm_g", "c_knorm_g", "ab_gate_b", "c_fgate_b"]
SMALL_ROWS = 16


def _pack_small(parts):
    flat = jnp.concatenate([p.reshape(-1).astype(F32) for p in parts])
    return jnp.pad(flat, (0, SMALL_ROWS * D - flat.shape[0])).reshape(SMALL_ROWS, D)


def _unpack_small(buf, shapes):
    flat, out, off = buf.reshape(-1), [], 0
    for s in shapes:
        n = 1
        for d in s:
            n *= d
        out.append(flat[off:off + n].reshape(s))
        off += n
    return out


def kernel(x, mem, norm_mix_g, norm_xattn_g, norm_mlp_g, final_norm_g, ab_w_in, ab_conv_w, ab_gate_b, hgrn_lb_logits, mlstm_norm_g, hgrn_norm_g, ab_w_out, c_w_in, c_fgate_b, c_qnorm_g, c_knorm_g, c_w_out, mem_norm_g, mem_w_kv, xa_w_q, xa_w_o, mlp_w1, mlp_w2, loss_target, m_norm_mix_g, m_norm_xattn_g, m_norm_mlp_g, m_final_norm_g, m_ab_w_in, m_ab_conv_w, m_ab_gate_b, m_hgrn_lb_logits, m_mlstm_norm_g, m_hgrn_norm_g, m_ab_w_out, m_c_w_in, m_c_fgate_b, m_c_qnorm_g, m_c_knorm_g, m_c_w_out, m_mem_norm_g, m_mem_w_kv, m_xa_w_q, m_xa_w_o, m_mlp_w1, m_mlp_w2, v_norm_mix_g, v_norm_xattn_g, v_norm_mlp_g, v_final_norm_g, v_ab_w_in, v_ab_conv_w, v_ab_gate_b, v_hgrn_lb_logits, v_mlstm_norm_g, v_hgrn_norm_g, v_ab_w_out, v_c_w_in, v_c_fgate_b, v_c_qnorm_g, v_c_knorm_g, v_c_w_out, v_mem_norm_g, v_mem_w_kv, v_xa_w_q, v_xa_w_o, v_mlp_w1, v_mlp_w2):
    A = dict(locals())
    chip = 2 * lax.axis_index("x") + lax.axis_index("y")

    big = ["ab_w_in", "c_w_in", "ab_w_out", "c_w_out", "mem_w_kv", "xa_w_q", "xa_w_o", "mlp_w1", "mlp_w2"]
    shard2d = {"ab_w_in": (D, 1026), "c_w_in": (D, 1026), "ab_w_out": (256, D), "c_w_out": (256, D), "mem_w_kv": (D, 512),
               "xa_w_q": (512, D), "xa_w_o": (512, D), "mlp_w1": (2 * D, D), "mlp_w2": (2 * D, D)}
    gathered = _all_gather_chips([A[n].reshape(shard2d[n]).astype(BF16) for n in big] + [ab_conv_w[0]], "gather_weights")
    gw = dict(zip(big, gathered[:-1]))
    cols = lambda g: jnp.concatenate([g[k] for k in range(NCHIP)], axis=1)
    per_layer = lambda g: g.reshape(NCHIP, 2, -1, D).transpose(1, 0, 2, 3)
    W = dict(
        w_in0=_pack_w_in0(cols(gw["ab_w_in"])), w_in1=_pack_w_in1(cols(gw["c_w_in"])),
        w_out0=gw["ab_w_out"].reshape(D, D), w_out1=gw["c_w_out"].reshape(D, D), wkv_s=gw["mem_w_kv"],
        wq=per_layer(gw["xa_w_q"]).reshape(2, D, D), wo=per_layer(gw["xa_w_o"]).reshape(2, D, D),
        w1s=per_layer(gw["mlp_w1"]), w2=per_layer(gw["mlp_w2"]).reshape(2, 4 * D, D))
    S = dict(norm_mix_g=norm_mix_g, norm_xattn_g=norm_xattn_g, norm_mlp_g=norm_mlp_g, final_norm_g=final_norm_g,
             conv_w=cols(gathered[-1]), gate_b=ab_gate_b, lb_logits=hgrn_lb_logits, mlstm_norm_g=mlstm_norm_g,
             hgrn_norm_g=hgrn_norm_g, c_fgate_b=c_fgate_b, c_qnorm_g=c_qnorm_g, c_knorm_g=c_knorm_g, mem_norm_g=mem_norm_g)

    lossp, dx, G = _local_step(x[0], mem[0], loss_target[0], W, S)

    gsmall = {"norm_mix_g": G["norm_mix_g"], "norm_xattn_g": G["norm_xattn_g"], "norm_mlp_g": G["norm_mlp_g"],
              "final_norm_g": G["final_norm_g"], "mem_norm_g": G["mem_norm_g"], "hgrn_lb_logits": G["lb_logits"],
              "mlstm_norm_g": G["mlstm_norm_g"], "hgrn_norm_g": G["hgrn_norm_g"], "c_qnorm_g": G["c_qnorm_g"],
              "c_knorm_g": G["c_knorm_g"], "ab_gate_b": G["gate_b"], "c_fgate_b": G["c_fgate_b"]}
    packed = _pack_small([gsmall[n] for n in SMALL] + [G["conv_w"], lossp])
    red = _sum_slots(_all_gather_devices(packed, "gather_small"), F32, "sum_small")
    small_shapes = [A[n].shape for n in SMALL]
    *gs, gconv, loss = _unpack_small(red, small_shapes + [(CONV_W, D), ()])
    gs = dict(zip(SMALL, gs))
    gconv = lax.dynamic_slice_in_dim(gconv, chip * 256, 256, axis=1)[None]

    def by_cols(g, n):
        return g.reshape(g.shape[0], NCHIP, n).transpose(1, 0, 2).reshape(NCHIP, -1)

    def by_rows(g):
        return g.reshape(NCHIP, -1)

    def by_rows_l(g):
        return g.reshape(2, NCHIP, -1).transpose(1, 0, 2).reshape(NCHIP, -1)

    def by_cols_l(g, n):
        return g.reshape(2, g.shape[1], NCHIP, n).transpose(2, 0, 1, 3).reshape(NCHIP, -1)

    parts = [by_cols(_unpack_w_in0(G["w_in0"]), 1026), by_cols(_unpack_w_in1(G["w_in1"]), 1026), by_rows(G["w_out0"]),
             by_rows(G["w_out1"]), by_cols(G["wkv"], 512), by_rows_l(G["wq"]), by_rows_l(G["wo"]), by_cols_l(G["w1"], D),
             by_rows_l(G["w2"])]
    flat = jnp.concatenate(parts, axis=1)
    lp = 2 * RS_ROWS * D
    flat = jnp.pad(flat, ((0, 0), (0, lp - flat.shape[1]))).astype(BF16)
    halves = flat.reshape(NCHIP, 2, RS_ROWS, D).transpose(1, 0, 2, 3)
    core = lax.axis_index("c")
    mine = lax.dynamic_index_in_dim(halves, core, axis=0, keepdims=False)
    theirs = _pair_exchange(halves, "pair_exchange")
    psum_ = _sum_slots(jnp.stack([mine.reshape(NCHIP * RS_ROWS, D), theirs.reshape(NCHIP * RS_ROWS, D)]), BF16, "pair_sum")
    from_chips = _chip_exchange(psum_.reshape(NCHIP, RS_ROWS, D), "chip_exchange")
    rhalf = _sum_slots(from_chips, F32, "chip_sum")
    rfull = _pair_bcast(rhalf, "pair_bcast").reshape(-1)
    gbig, off = {}, 0
    for n in big:
        r, c = shard2d[n]
        gbig[n] = rfull[off:off + r * c].reshape(r, c)
        off += r * c

    out_g, out_d, out_m, out_v = {}, {}, {}, {}
    for n in big:
        d_, m_, v_ = _adam(A[n].reshape(shard2d[n]), gbig[n], A["m_" + n].reshape(shard2d[n]), A["v_" + n].reshape(shard2d[n]), "adam_" + n)
        out_g[n] = gbig[n].reshape(A[n].shape)
        out_d[n], out_m[n], out_v[n] = d_.reshape(A[n].shape), m_.reshape(A[n].shape), v_.reshape(A[n].shape)
    sd, sm, sv = _adam(_pack_small([A[n] for n in SMALL]), _pack_small([gs[n] for n in SMALL]),
                       _pack_small([A["m_" + n] for n in SMALL]), _pack_small([A["v_" + n] for n in SMALL]), "adam_small")
    for n, d_, m_, v_ in zip(SMALL, _unpack_small(sd, small_shapes), _unpack_small(sm, small_shapes), _unpack_small(sv, small_shapes)):
        out_g[n], out_d[n], out_m[n], out_v[n] = gs[n], d_, m_, v_
    cd, cm_, cv = _adam(ab_conv_w[0], gconv[0], m_ab_conv_w[0], v_ab_conv_w[0], "adam_conv")
    out_g["ab_conv_w"], out_d["ab_conv_w"], out_m["ab_conv_w"], out_v["ab_conv_w"] = gconv, cd[None], cm_[None], cv[None]

    order = ["norm_mix_g", "norm_xattn_g", "norm_mlp_g", "final_norm_g", "ab_w_in", "ab_conv_w", "ab_gate_b", "hgrn_lb_logits",
             "mlstm_norm_g", "hgrn_norm_g", "ab_w_out", "c_w_in", "c_fgate_b", "c_qnorm_g", "c_knorm_g", "c_w_out", "mem_norm_g",
             "mem_w_kv", "xa_w_q", "xa_w_o", "mlp_w1", "mlp_w2"]
    return (loss, dx[None], *[out_g[n] for n in order], *[out_d[n] for n in order], *[out_m[n] for n in order],
            *[out_v[n] for n in order])
```

```python
import functools

import jax
import jax.numpy as jnp
from jax import lax
from jax.experimental import pallas as pl
from jax.experimental.pallas import tpu as pltpu

F32 = jnp.float32
BF16 = jnp.bfloat16
EPS = 1e-6
D = 1024
CHUNK = 64
HD = 128
XD = 256
NEG = -1e30
VMEM_LIMIT_V7X = 56 * 1024 * 1024
ADAM_LR, ADAM_B1, ADAM_B2, ADAM_EPS, ADAM_WD, ADAM_STEP = 0.001, 0.9, 0.999, 1e-08, 0.01, 10
MESH = pl.DeviceIdType.MESH


def _pc(body, name, grid, in_specs, out_specs, out_shape, scratch=(), **kw):
    return pl.pallas_call(
        body, name=name, grid=grid, in_specs=in_specs, out_specs=out_specs, out_shape=out_shape,
        scratch_shapes=scratch,
        compiler_params=pltpu.CompilerParams(
            dimension_semantics=("arbitrary",) * len(grid), vmem_limit_bytes=VMEM_LIMIT_V7X), **kw)


def _sds(shape, dtype=F32):
    return jax.ShapeDtypeStruct(shape, dtype)


def _blk(n, target):
    return max(b for b in range(128, max(target, 128) + 1, 128) if n % b == 0)


def _dot(a, b, dims):
    return lax.dot_general(a, b, (dims, ((), ())), preferred_element_type=F32)


def _nn(a, b):
    return _dot(a, b, ((1,), (0,)))


def _nt(a, b):
    return _dot(a, b, ((1,), (1,)))


def _tn(a, b):
    return _dot(a, b, ((0,), (0,)))


def _sigmoid(x):
    return 1.0 / (1.0 + jnp.exp(-x))


def _log_sigmoid(x):
    return jnp.minimum(x, 0.0) - jnp.log(1.0 + jnp.exp(-jnp.abs(x)))


def _rstd(x):
    return lax.rsqrt(jnp.mean(x * x, axis=-1, keepdims=True) + EPS)


def _rms_bwd(du, x, g):
    r = _rstd(x)
    xh = x * r
    dxh = du * g
    dx = r * (dxh - xh * jnp.mean(dxh * xh, axis=-1, keepdims=True))
    return dx, du * xh


def _norm_mm(h, g, w, name, bm=512, bn=512):
    t, n = h.shape[0], w.shape[1]
    bm, bn = min(bm, t), _blk(n, 3 * bn)

    def body(h_ref, g_ref, w_ref, z_ref, u_ref):
        @pl.when(pl.program_id(1) == 0)
        def _():
            x = h_ref[...]
            u_ref[...] = (x * _rstd(x) * g_ref[...]).astype(BF16)
        z_ref[...] = _nn(u_ref[...], w_ref[...])

    return _pc(body, name, (t // bm, n // bn),
               [pl.BlockSpec((bm, D), lambda i, j: (i, 0)), pl.BlockSpec((1, D), lambda i, j: (0, 0)),
                pl.BlockSpec((D, bn), lambda i, j: (0, j))],
               [pl.BlockSpec((bm, bn), lambda i, j: (i, j)), pl.BlockSpec((bm, D), lambda i, j: (i, 0))],
               [_sds((t, n)), _sds((t, D), BF16)])(h, g, w)


def _mm_tn(a, b, name, bm=1024, bn=1024, bt=512):
    t, m = a.shape
    n = b.shape[1]
    bm, bn, bt = _blk(m, bm), _blk(n, bn + bn // 2), min(bt, t)

    def body(a_ref, b_ref, o_ref):
        @pl.when(pl.program_id(2) == 0)
        def _():
            o_ref[...] = jnp.zeros_like(o_ref)
        o_ref[...] += _tn(a_ref[...].astype(BF16), b_ref[...].astype(BF16))

    return _pc(body, name, (m // bm, n // bn, t // bt),
               [pl.BlockSpec((bt, bm), lambda i, j, k: (k, i)), pl.BlockSpec((bt, bn), lambda i, j, k: (k, j))],
               pl.BlockSpec((bm, bn), lambda i, j, k: (i, j)), _sds((m, n)))(a, b)


def _bwd_in(dz, w, h, g, dh, name, bm=512, bk=1024):
    t, n = dz.shape
    bm, bk = min(bm, t), _blk(n, bk + bk // 2)
    nk = n // bk

    def body(dz_ref, w_ref, h_ref, g_ref, dh_ref, o_ref, dg_ref, acc):
        i, k = pl.program_id(0), pl.program_id(1)

        @pl.when(k == 0)
        def _():
            acc[...] = jnp.zeros_like(acc)

        @pl.when((i == 0) & (k == 0))
        def _():
            dg_ref[...] = jnp.zeros_like(dg_ref)

        acc[...] += _nt(dz_ref[...], w_ref[...])

        @pl.when(k == nk - 1)
        def _():
            dx, dgr = _rms_bwd(acc[...], h_ref[...], g_ref[...])
            o_ref[...] = dh_ref[...] + dx
            dg_ref[...] += jnp.sum(dgr, axis=0, keepdims=True)

    return _pc(body, name, (t // bm, nk),
               [pl.BlockSpec((bm, bk), lambda i, k: (i, k)), pl.BlockSpec((D, bk), lambda i, k: (0, k)),
                pl.BlockSpec((bm, D), lambda i, k: (i, 0)), pl.BlockSpec((1, D), lambda i, k: (0, 0)),
                pl.BlockSpec((bm, D), lambda i, k: (i, 0))],
               [pl.BlockSpec((bm, D), lambda i, k: (i, 0)), pl.BlockSpec((1, D), lambda i, k: (0, 0))],
               [_sds((t, D)), _sds((1, D))], scratch=[pltpu.VMEM((bm, D), F32)])(dz, w, h, g, dh)


def _mlp_fwd(h, g, w1s, w2, name, bm=512):
    t = h.shape[0]
    bm = min(bm, t)
    nk = w1s.shape[0]

    def body(h_ref, g_ref, w1_ref, w2_ref, o_ref, a_ref, u_ref, acc):
        k = pl.program_id(1)

        @pl.when(k == 0)
        def _():
            x = h_ref[...]
            u_ref[...] = (x * _rstd(x) * g_ref[...]).astype(BF16)
            acc[...] = jnp.zeros_like(acc)

        a = _nn(u_ref[...], w1_ref[...])
        a_ref[...] = a
        r = jnp.square(jnp.maximum(a, 0.0)).astype(BF16)
        acc[...] += _nn(r, w2_ref[...])

        @pl.when(k == nk - 1)
        def _():
            o_ref[...] = h_ref[...] + acc[...]

    return _pc(body, name, (t // bm, nk),
               [pl.BlockSpec((bm, D), lambda i, k: (i, 0)), pl.BlockSpec((1, D), lambda i, k: (0, 0)),
                pl.BlockSpec((None, D, D), lambda i, k: (k, 0, 0)), pl.BlockSpec((D, D), lambda i, k: (k, 0))],
               [pl.BlockSpec((bm, D), lambda i, k: (i, 0)), pl.BlockSpec((bm, D), lambda i, k: (i, k)),
                pl.BlockSpec((bm, D), lambda i, k: (i, 0))],
               [_sds((t, D)), _sds((t, nk * D)), _sds((t, D), BF16)],
               scratch=[pltpu.VMEM((bm, D), F32)])(h, g, w1s, w2)


def _mlp_bwd(dh, a, w1s, w2, h, g, name, bm=512):
    t = h.shape[0]
    bm = min(bm, t)
    nk = w1s.shape[0]

    def body(dh_ref, a_ref, w1_ref, w2_ref, h_ref, g_ref, o_ref, da_ref, r_ref, dg_ref, acc):
        i, k = pl.program_id(0), pl.program_id(1)

        @pl.when(k == 0)
        def _():
            acc[...] = jnp.zeros_like(acc)

        @pl.when((i == 0) & (k == 0))
        def _():
            dg_ref[...] = jnp.zeros_like(dg_ref)

        ap = jnp.maximum(a_ref[...], 0.0)
        r_ref[...] = jnp.square(ap).astype(BF16)
        dr = _nt(dh_ref[...].astype(BF16), w2_ref[...])
        da = (dr * (2.0 * ap)).astype(BF16)
        da_ref[...] = da
        acc[...] += _nt(da, w1_ref[...])

        @pl.when(k == nk - 1)
        def _():
            dx, dgr = _rms_bwd(acc[...], h_ref[...], g_ref[...])
            o_ref[...] = dh_ref[...] + dx
            dg_ref[...] += jnp.sum(dgr, axis=0, keepdims=True)

    return _pc(body, name, (t // bm, nk),
               [pl.BlockSpec((bm, D), lambda i, k: (i, 0)), pl.BlockSpec((bm, D), lambda i, k: (i, k)),
                pl.BlockSpec((None, D, D), lambda i, k: (k, 0, 0)), pl.BlockSpec((D, D), lambda i, k: (k, 0)),
                pl.BlockSpec((bm, D), lambda i, k: (i, 0)), pl.BlockSpec((1, D), lambda i, k: (0, 0))],
               [pl.BlockSpec((bm, D), lambda i, k: (i, 0)), pl.BlockSpec((bm, D), lambda i, k: (i, k)),
                pl.BlockSpec((bm, D), lambda i, k: (i, k)), pl.BlockSpec((1, D), lambda i, k: (0, 0))],
               [_sds((t, D)), _sds((t, nk * D), BF16), _sds((t, nk * D), BF16), _sds((1, D))],
               scratch=[pltpu.VMEM((bm, D), F32)])(dh, a, w1s, w2, h, g)


def _rows_of(x):
    return lax.broadcasted_iota(jnp.int32, x.shape, 0)


def _shift_down(x, s):
    if s == 0:
        return x
    return jnp.where(_rows_of(x) >= s, pltpu.roll(x, s, 0), 0.0)


def _shift_up(x, s):
    if s == 0:
        return x
    n = x.shape[0]
    return jnp.where(_rows_of(x) < n - s, pltpu.roll(x, n - s, 0), 0.0)


def _cumsum_rows(x):
    n, s = x.shape[0], 1
    while s < n:
        x = x + _shift_down(x, s)
        s *= 2
    return x


def _rcumsum_rows(x):
    n, s = x.shape[0], 1
    while s < n:
        x = x + _shift_up(x, s)
        s *= 2
    return x


def _silu(x):
    return x * _sigmoid(x)


def _dsilu(x):
    s = _sigmoid(x)
    return s * (1.0 + x * (1.0 - s))


CONV_W = 4


def _conv_pre(u, w):
    y = _shift_down(u, CONV_W - 1) * w[0:1, :]
    for j in range(1, CONV_W):
        y = y + _shift_down(u, CONV_W - 1 - j) * w[j:j + 1, :]
    return y


def _conv_fwd(z0, cw, name):
    t = z0.shape[0]

    def body(u_ref, w_ref, o_ref):
        o_ref[...] = _silu(_conv_pre(u_ref[...], w_ref[...]))

    return _pc(body, name, (2 * 512 // HD,),
               [pl.BlockSpec((t, HD), lambda c: (0, c)), pl.BlockSpec((CONV_W, HD), lambda c: (0, c))],
               pl.BlockSpec((t, HD), lambda c: (0, c)), _sds((t, 1024)))(z0, cw)


def _conv_bwd(z0, cw, dy, name):
    t = z0.shape[0]

    def body(u_ref, w_ref, dy_ref, du_ref, dw_ref):
        u, w = u_ref[...], w_ref[...]
        dpre = dy_ref[...] * _dsilu(_conv_pre(u, w))
        du = _shift_up(dpre, CONV_W - 1) * w[0:1, :]
        for j in range(1, CONV_W):
            du = du + _shift_up(dpre, CONV_W - 1 - j) * w[j:j + 1, :]
        du_ref[...] = du.astype(BF16)
        for j in range(CONV_W):
            dw_ref[j:j + 1, :] = jnp.sum(dpre * _shift_down(u, CONV_W - 1 - j), axis=0, keepdims=True)

    return _pc(body, name, (2 * 512 // HD,),
               [pl.BlockSpec((t, HD), lambda c: (0, c)), pl.BlockSpec((CONV_W, HD), lambda c: (0, c)),
                pl.BlockSpec((t, HD), lambda c: (0, c))],
               [pl.BlockSpec((t, HD), lambda c: (0, c)), pl.BlockSpec((CONV_W, HD), lambda c: (0, c))],
               [_sds((t, 1024), BF16), _sds((CONV_W, 1024))])(z0, cw, dy)


def _mlstm_gates(gate, bias, m_in):
    L = gate.shape[0]
    r = lax.broadcasted_iota(jnp.int32, (L, L), 0)
    c = lax.broadcasted_iota(jnp.int32, (L, L), 1)
    eye, tril = r == c, c <= r
    i_col = gate[:, 0:1] + bias[:, 0:1]
    f_col = gate[:, 1:2] + bias[:, 1:2]
    logf_col = _log_sigmoid(f_col)
    logf_row = jnp.sum(jnp.where(eye, logf_col, 0.0), axis=0, keepdims=True)
    i_row = jnp.sum(jnp.where(eye, i_col, 0.0), axis=0, keepdims=True)
    b_col = jnp.sum(jnp.where(tril, logf_row, 0.0), axis=1, keepdims=True)
    b_row = jnp.sum(jnp.where(r <= c, logf_col, 0.0), axis=0, keepdims=True)
    logd = jnp.where(tril, b_col - b_row + i_row, NEG)
    inter = b_col + m_in
    m_t = jnp.maximum(inter, jnp.max(logd, axis=1, keepdims=True))
    w_t = jnp.exp(inter - m_t)
    dm = jnp.exp(logd - m_t)
    b_last = b_col[L - 1:L, :]
    log_in = b_last - b_col + i_col
    m_new = jnp.maximum(b_last + m_in, jnp.max(log_in, axis=0, keepdims=True))
    w_col = jnp.exp(log_in - m_new)
    decay = jnp.exp(b_last + m_in - m_new)
    return dict(eye=eye, r=r, c=c, f_col=f_col, m_t=m_t, w_t=w_t, dm=dm, m_new=m_new, w_col=w_col, decay=decay)


def _mlstm_fwd(qk, z0, gates, bias, name):
    t = qk.shape[0]
    nc, nh, L = t // CHUNK, 4, CHUNK
    scale = HD ** -0.5

    def body(q_ref, k_ref, v_ref, g_ref, b_ref, h_ref, cs_ref, ns_ref, ms_ref, c_s, n_s, m_s):
        @pl.when(pl.program_id(1) == 0)
        def _():
            c_s[...] = jnp.zeros_like(c_s)
            n_s[...] = jnp.zeros_like(n_s)
            m_s[...] = jnp.zeros_like(m_s)

        cm, nv, m_in = c_s[...], n_s[...], m_s[...]
        cs_ref[...] = cm
        ns_ref[...] = nv
        ms_ref[...] = jnp.broadcast_to(m_in, ms_ref.shape)
        q, kh, v = q_ref[...], k_ref[...] * scale, v_ref[...]
        G = _mlstm_gates(g_ref[...], b_ref[...], m_in)
        qb, kb, vb = q.astype(BF16), kh.astype(BF16), v.astype(BF16)
        sc = _nt(qb, kb) * G["dm"]
        num = _nn(sc.astype(BF16), vb) + G["w_t"] * _nn(qb, cm.astype(BF16))
        den = jnp.sum(sc, axis=1, keepdims=True) + G["w_t"] * jnp.sum(q * nv, axis=1, keepdims=True)
        h_ref[...] = num / jnp.maximum(jnp.abs(den), jnp.exp(-G["m_t"]))
        wk = G["w_col"] * kh
        c_s[...] = G["decay"] * cm + _tn(wk.astype(BF16), vb)
        n_s[...] = G["decay"] * nv + jnp.sum(wk, axis=0, keepdims=True)
        m_s[...] = G["m_new"]

    hspec = lambda off: pl.BlockSpec((L, HD), lambda h, j: (j, off + h))
    st = lambda r: pl.BlockSpec((None, None, r, HD), lambda h, j: (h, j, 0, 0))
    return _pc(body, name, (nh, nc),
               [hspec(0), hspec(4), hspec(8), pl.BlockSpec((None, L, 2), lambda h, j: (h, j, 0)),
                pl.BlockSpec((None, 1, 2), lambda h, j: (h, 0, 0))],
               [hspec(0), st(HD), st(1), st(1)],
               [_sds((t, 512)), _sds((nh, nc, HD, HD)), _sds((nh, nc, 1, HD)), _sds((nh, nc, 1, HD))],
               scratch=[pltpu.VMEM((HD, HD), F32), pltpu.VMEM((1, HD), F32), pltpu.VMEM((1, 1), F32)])(qk, qk, z0, gates, bias)


def _mlstm_bwd(qk, z0, gates, bias, cs, ns, ms, dh, name):
    t = qk.shape[0]
    nc, nh, L = t // CHUNK, 4, CHUNK
    scale = HD ** -0.5

    def body(q_ref, k_ref, v_ref, g_ref, b_ref, cs_ref, ns_ref, ms_ref, dh_ref, dq_ref, dk_ref, dv_ref, dg_ref, dc_s, dn_s):
        @pl.when(pl.program_id(1) == 0)
        def _():
            dc_s[...] = jnp.zeros_like(dc_s)
            dn_s[...] = jnp.zeros_like(dn_s)

        cm, nv, m_in = cs_ref[...], ns_ref[...], ms_ref[:, 0:1]
        q, kh, v = q_ref[...], k_ref[...] * scale, v_ref[...]
        G = _mlstm_gates(g_ref[...], b_ref[...], m_in)
        w_t, dmat, w_col, decay = G["w_t"], G["dm"], G["w_col"], G["decay"]
        qb, kb, vb, cb = q.astype(BF16), kh.astype(BF16), v.astype(BF16), cm.astype(BF16)
        s = _nt(qb, kb)
        sc = s * dmat
        scb = sc.astype(BF16)
        qc = _nn(qb, cb)
        qn = jnp.sum(q * nv, axis=1, keepdims=True)
        num = _nn(scb, vb) + w_t * qc
        den = jnp.sum(sc, axis=1, keepdims=True) + w_t * qn
        e_m = jnp.exp(-G["m_t"])
        dnm = jnp.maximum(jnp.abs(den), e_m)
        dh_ = dh_ref[...]
        dnum = dh_ / dnm
        dden = jnp.where(jnp.abs(den) > e_m, -jnp.sum(dh_ * num, axis=1, keepdims=True) / (dnm * dnm) * jnp.sign(den), 0.0)
        dnumb = dnum.astype(BF16)
        dsc = _nt(dnumb, vb) + dden
        dv = _tn(scb, dnumb)
        wd = w_t * dnum
        wdb = wd.astype(BF16)
        ds = dsc * dmat
        dsb = ds.astype(BF16)
        dq = _nt(wdb, cb) + (w_t * dden) * nv + _nn(dsb, kb)
        dc_o = _tn(qb, wdb)
        dn_o = jnp.sum(q * (w_t * dden), axis=0, keepdims=True)
        dw = jnp.sum(dnum * qc, axis=1, keepdims=True) + dden * qn
        dkh = _tn(dsb, qb)
        dlogd = ds * s
        db_col = jnp.sum(dlogd, axis=1, keepdims=True) + dw * w_t
        csum = jnp.sum(dlogd, axis=0, keepdims=True)
        dcn, dnn = dc_s[...], dn_s[...]
        dcnb = dcn.astype(BF16)
        kdc = _nn(kb, dcnb)
        dws = jnp.sum(kdc * v, axis=1, keepdims=True) + jnp.sum(kh * dnn, axis=1, keepdims=True)
        dv = dv + w_col * kdc
        dkh = dkh + w_col * (_nt(vb, dcnb) + dnn)
        dlin = dws * w_col
        ddecay = jnp.sum(jnp.sum(dcn * cm, axis=1, keepdims=True), axis=0, keepdims=True) + jnp.sum(dnn * nv, axis=1, keepdims=True)
        dlast = ddecay * decay + jnp.sum(dlin, axis=0, keepdims=True)
        rows = lax.broadcasted_iota(jnp.int32, (L, 1), 0)
        db_col = db_col - dlin + jnp.where(rows == L - 1, dlast, 0.0)
        eye, r, c = G["eye"], G["r"], G["c"]
        di = dlin + jnp.sum(jnp.where(eye, csum, 0.0), axis=1, keepdims=True)
        db_row = jnp.sum(jnp.where(eye, db_col, 0.0), axis=0, keepdims=True) - csum
        dlogf = jnp.sum(jnp.where(c >= r, db_row, 0.0), axis=1, keepdims=True)
        dg_ref[:, 0:1] = di
        dg_ref[:, 1:2] = dlogf * (1.0 - _sigmoid(G["f_col"]))
        dq_ref[...] = dq
        dk_ref[...] = dkh * scale
        dv_ref[...] = dv
        dc_s[...] = decay * dcn + dc_o
        dn_s[...] = decay * dnn + dn_o

    rv = lambda j: nc - 1 - j
    hspec = lambda off: pl.BlockSpec((L, HD), lambda h, j: (rv(j), off + h))
    st = lambda r: pl.BlockSpec((None, None, r, HD), lambda h, j: (h, rv(j), 0, 0))
    gs = pl.BlockSpec((None, L, 2), lambda h, j: (h, rv(j), 0))
    return _pc(body, name, (nh, nc),
               [hspec(0), hspec(4), hspec(8), gs, pl.BlockSpec((None, 1, 2), lambda h, j: (h, 0, 0)),
                st(HD), st(1), st(1), hspec(0)],
               [hspec(0), hspec(0), hspec(0), gs],
               [_sds((t, 512)), _sds((t, 512)), _sds((t, 512)), _sds((nh, t, 2))],
               scratch=[pltpu.VMEM((HD, HD), F32), pltpu.VMEM((1, HD), F32)])(qk, qk, z0, gates, bias, cs, ns, ms, dh)


def _hgrn_act(qb_, fb_, ib_, lg):
    lb = _sigmoid(lg[0:1, :] - lg[1:2, :])
    sg = _sigmoid(fb_)
    f = lb + (1.0 - lb) * sg
    return lb, sg, f, _silu(qb_), (1.0 - lb) * (1.0 - sg), _silu(ib_), _cumsum_rows(jnp.log(f))


def _hgrn_fwd(z0, lbl, name):
    t = z0.shape[0]
    nc, nh, L = t // CHUNK, 4, CHUNK

    def body(q_ref, f_ref, i_ref, l_ref, o_ref, ss_ref, st_s):
        @pl.when(pl.program_id(1) == 0)
        def _():
            st_s[...] = jnp.zeros_like(st_s)

        st = st_s[...]
        ss_ref[...] = st
        _, _, _, q, k, v, b = _hgrn_act(q_ref[...], f_ref[...], i_ref[...], l_ref[...])
        o = _nt((q * jnp.exp(b)).astype(BF16), st.astype(BF16))
        rows = _rows_of(b)
        for dl in range(L):
            e = jnp.exp(jnp.where(rows >= dl, b - pltpu.roll(b, dl, 0) if dl else b - b, NEG))
            a = jnp.sum(q * _shift_down(k, dl) * e, axis=1, keepdims=True)
            o = o + a * _shift_down(v, dl)
        o_ref[...] = o
        bl = b[L - 1:L, :]
        st_s[...] = st * jnp.exp(bl) + _tn(v.astype(BF16), (k * jnp.exp(bl - b)).astype(BF16))

    hspec = lambda off: pl.BlockSpec((L, HD), lambda h, j: (j, off + h))
    return _pc(body, name, (nh, nc),
               [hspec(16), hspec(20), hspec(24), pl.BlockSpec((2, HD), lambda h, j: (0, h))],
               [hspec(0), pl.BlockSpec((None, None, HD, HD), lambda h, j: (h, j, 0, 0))],
               [_sds((t, 512)), _sds((nh, nc, HD, HD))],
               scratch=[pltpu.VMEM((HD, HD), F32)])(z0, z0, z0, lbl)


def _hgrn_bwd(z0, lbl, ss, do, name):
    t = z0.shape[0]
    nc, nh, L = t // CHUNK, 4, CHUNK

    def body(q_ref, f_ref, i_ref, l_ref, ss_ref, do_ref, dq_ref, df_ref, di_ref, dl_ref, dst_s, dlb_s):
        j = pl.program_id(1)

        @pl.when(j == 0)
        def _():
            dst_s[...] = jnp.zeros_like(dst_s)
            dlb_s[...] = jnp.zeros_like(dlb_s)

        st = ss_ref[...]
        qp, fp, ip = q_ref[...], f_ref[...], i_ref[...]
        lb, sg, f, q, k, v, b = _hgrn_act(qp, fp, ip, l_ref[...])
        do_ = do_ref[...]
        dob, stb = do_.astype(BF16), st.astype(BF16)
        eb = jnp.exp(b)
        qe = q * eb
        dqe = _nn(dob, stb)
        dst_o = _tn(dob, qe.astype(BF16))
        dq = dqe * eb
        db = dqe * qe
        dk = jnp.zeros_like(q)
        dv = jnp.zeros_like(q)
        rows = _rows_of(b)
        for dl in range(L):
            up = (L - dl) % L
            kd, vd = _shift_down(k, dl), _shift_down(v, dl)
            e = jnp.exp(jnp.where(rows >= dl, b - pltpu.roll(b, dl, 0) if dl else b - b, NEG))
            a = jnp.sum(q * kd * e, axis=1, keepdims=True)
            p = jnp.sum(do_ * vd, axis=1, keepdims=True) * e
            dq = dq + p * kd
            dkd = p * q
            dbb = dkd * kd
            adv = a * do_
            if dl:
                dv = dv + pltpu.roll(adv, up, 0)
                dk = dk + pltpu.roll(dkd, up, 0)
                db = db + dbb - pltpu.roll(dbb, up, 0)
            else:
                dv = dv + adv
                dk = dk + dkd
        dstn = dst_s[...]
        dstnb = dstn.astype(BF16)
        bl = b[L - 1:L, :]
        ebl = jnp.exp(bl)
        kdec_e = jnp.exp(bl - b)
        kdec = k * kdec_e
        dbl = jnp.sum(dstn * st, axis=0, keepdims=True) * ebl
        dv = dv + _nt(kdec.astype(BF16), dstnb)
        dkdec = _nn(v.astype(BF16), dstnb)
        dk = dk + dkdec * kdec_e
        dx = dkdec * kdec
        dbl = dbl + jnp.sum(dx, axis=0, keepdims=True)
        db = db - dx + jnp.where(rows == L - 1, dbl, 0.0)
        dst_s[...] = dstn * ebl + dst_o
        dg = _rcumsum_rows(db)
        dfk = dg / f - dk
        dq_ref[...] = (dq * _dsilu(qp)).astype(BF16)
        di_ref[...] = (dv * _dsilu(ip)).astype(BF16)
        df_ref[...] = (dfk * (1.0 - lb) * sg * (1.0 - sg)).astype(BF16)
        dlb_s[...] += jnp.sum(dfk * (1.0 - sg), axis=0, keepdims=True)

        @pl.when(j == nc - 1)
        def _():
            dl0 = dlb_s[...] * lb * (1.0 - lb)
            dl_ref[0:1, :] = dl0
            dl_ref[1:2, :] = -dl0

    rv = lambda j: nc - 1 - j
    hspec = lambda off: pl.BlockSpec((L, HD), lambda h, j: (rv(j), off + h))
    return _pc(body, name, (nh, nc),
               [hspec(16), hspec(20), hspec(24), pl.BlockSpec((2, HD), lambda h, j: (0, h)),
                pl.BlockSpec((None, None, HD, HD), lambda h, j: (h, rv(j), 0, 0)), hspec(0)],
               [hspec(0), hspec(0), hspec(0), pl.BlockSpec((2, HD), lambda h, j: (0, h))],
               [_sds((t, 512), BF16), _sds((t, 512), BF16), _sds((t, 512), BF16), _sds((2, 512))],
               scratch=[pltpu.VMEM((HD, HD), F32), pltpu.VMEM((1, HD), F32)])(z0, z0, z0, lbl, ss, do)


def _post0_fwd(hm, hh, z0, na, nb, w, h0, name, bm=512):
    t = h0.shape[0]
    bm = min(bm, t)

    def body(hm_ref, hh_ref, oa_ref, gb_ref, na_ref, nb_ref, w_ref, h_ref, o_ref, y_ref):
        for hd in range(4):
            sl = slice(hd * HD, (hd + 1) * HD)
            pa = _sigmoid(oa_ref[:, sl]) * hm_ref[:, sl]
            y_ref[:, sl] = (pa * _rstd(pa) * na_ref[:, sl]).astype(BF16)
            xb = hh_ref[:, sl]
            y_ref[:, 512 + hd * HD:512 + (hd + 1) * HD] = (xb * _rstd(xb) * nb_ref[:, sl] * _silu(gb_ref[:, sl])).astype(BF16)
        o_ref[...] = h_ref[...] + _nn(y_ref[...], w_ref[...])

    row = lambda wd, c: pl.BlockSpec((bm, wd), lambda i: (i, c))
    vec = lambda wd: pl.BlockSpec((1, wd), lambda i: (0, 0))
    return _pc(body, name, (t // bm,),
               [row(512, 0), row(512, 0), row(512, 3), row(512, 7), vec(512), vec(512),
                pl.BlockSpec((D, D), lambda i: (0, 0)), row(D, 0)],
               [row(D, 0), row(D, 0)], [_sds((t, D)), _sds((t, D), BF16)])(hm, hh, z0, z0, na, nb, w, h0)


def _post0_bwd(dh1, w, hm, hh, z0, na, nb, name, bm=512):
    t = dh1.shape[0]
    bm = min(bm, t)

    def body(dh_ref, w_ref, hm_ref, hh_ref, oa_ref, gb_ref, na_ref, nb_ref, dhm_ref, dhh_ref, doa_ref, dgb_ref, dna_ref, dnb_ref):
        @pl.when(pl.program_id(0) == 0)
        def _():
            dna_ref[...] = jnp.zeros_like(dna_ref)
            dnb_ref[...] = jnp.zeros_like(dnb_ref)

        dy = _nt(dh_ref[...].astype(BF16), w_ref[...])
        for hd in range(4):
            sl = slice(hd * HD, (hd + 1) * HD)
            hm_, oa = hm_ref[:, sl], oa_ref[:, sl]
            sg = _sigmoid(oa)
            dpa, dgr = _rms_bwd(dy[:, sl], sg * hm_, na_ref[:, sl])
            dna_ref[:, sl] += jnp.sum(dgr, axis=0, keepdims=True)
            doa_ref[:, sl] = (dpa * hm_ * sg * (1.0 - sg)).astype(BF16)
            dhm_ref[:, sl] = dpa * sg
            xb, gb, nbv = hh_ref[:, sl], gb_ref[:, sl], nb_ref[:, sl]
            dyb = dy[:, 512 + hd * HD:512 + (hd + 1) * HD]
            dgb_ref[:, sl] = (dyb * (xb * _rstd(xb) * nbv) * _dsilu(gb)).astype(BF16)
            dxb, dgr2 = _rms_bwd(dyb * _silu(gb), xb, nbv)
            dnb_ref[:, sl] += jnp.sum(dgr2, axis=0, keepdims=True)
            dhh_ref[:, sl] = dxb

    row = lambda wd, c: pl.BlockSpec((bm, wd), lambda i: (i, c))
    vec = lambda wd: pl.BlockSpec((1, wd), lambda i: (0, 0))
    return _pc(body, name, (t // bm,),
               [row(D, 0), pl.BlockSpec((D, D), lambda i: (0, 0)), row(512, 0), row(512, 0), row(512, 3), row(512, 7),
                vec(512), vec(512)],
               [row(512, 0), row(512, 0), row(512, 0), row(512, 0), vec(512), vec(512)],
               [_sds((t, 512)), _sds((t, 512)), _sds((t, 512), BF16), _sds((t, 512), BF16), _sds((1, 512)), _sds((1, 512))],
               )(dh1, w, hm, hh, z0, z0, na, nb)


def _memkv_fwd(mem, g, wkv_s, name):
    m = mem.shape[0]

    def body(x_ref, g_ref, w_ref, kv_ref, mn_ref):
        x = x_ref[...]
        mn = (x * _rstd(x) * g_ref[...]).astype(BF16)
        mn_ref[...] = mn
        kv_ref[...] = _nn(mn, w_ref[...])

    return _pc(body, name, (4,),
               [pl.BlockSpec((m, D), lambda k: (0, 0)), pl.BlockSpec((1, D), lambda k: (0, 0)),
                pl.BlockSpec((None, D, 512), lambda k: (k, 0, 0))],
               [pl.BlockSpec((m, 512), lambda k: (0, k)), pl.BlockSpec((m, D), lambda k: (0, 0))],
               [_sds((m, 2048)), _sds((m, D), BF16)])(mem, g, wkv_s)


def _memkv_bwd(dkv, wkv_s, mem, g, name):
    m = mem.shape[0]

    def body(d_ref, w_ref, x_ref, g_ref, dg_ref, acc):
        k = pl.program_id(0)

        @pl.when(k == 0)
        def _():
            acc[...] = jnp.zeros_like(acc)

        acc[...] += _nt(d_ref[...].astype(BF16), w_ref[...])

        @pl.when(k == 3)
        def _():
            _, dgr = _rms_bwd(acc[...], x_ref[...], g_ref[...])
            dg_ref[...] = jnp.sum(dgr, axis=0, keepdims=True)

    return _pc(body, name, (4,),
               [pl.BlockSpec((m, 512), lambda k: (0, k)), pl.BlockSpec((None, D, 512), lambda k: (k, 0, 0)),
                pl.BlockSpec((m, D), lambda k: (0, 0)), pl.BlockSpec((1, D), lambda k: (0, 0))],
               pl.BlockSpec((1, D), lambda k: (0, 0)), _sds((1, D)), scratch=[pltpu.VMEM((m, D), F32)])(dkv, wkv_s, mem, g)


def _xattn_probs(qh, kh):
    s = _nt(qh, kh) * (XD ** -0.5)
    p = jnp.exp(s - jnp.max(s, axis=1, keepdims=True))
    return p / jnp.sum(p, axis=1, keepdims=True)


def _xattn_fwd(q, kv, wo, h1, name, bm=512):
    t, m = q.shape[0], kv.shape[0]
    bm = min(bm, t)

    def body(q_ref, k_ref, v_ref, w_ref, h_ref, out_ref, o_ref):
        for hd in range(D // XD):
            sl = slice(hd * XD, (hd + 1) * XD)
            p = _xattn_probs(q_ref[:, sl].astype(BF16), k_ref[:, sl].astype(BF16))
            o_ref[:, sl] = _nn(p.astype(BF16), v_ref[:, sl].astype(BF16)).astype(BF16)
        out_ref[...] = h_ref[...] + _nn(o_ref[...], w_ref[...])

    row = pl.BlockSpec((bm, D), lambda i: (i, 0))
    return _pc(body, name, (t // bm,),
               [row, pl.BlockSpec((m, D), lambda i: (0, 0)), pl.BlockSpec((m, D), lambda i: (0, 1)),
                pl.BlockSpec((D, D), lambda i: (0, 0)), row],
               [row, row], [_sds((t, D)), _sds((t, D), BF16)])(q, kv, kv, wo, h1)


def _xattn_bwd(dh2, q, kv, wo, name, bm=512):
    t, m = q.shape[0], kv.shape[0]
    bm = min(bm, t)

    def body(dh_ref, q_ref, k_ref, v_ref, w_ref, dq_ref, dkv_ref):
        @pl.when(pl.program_id(0) == 0)
        def _():
            dkv_ref[...] = jnp.zeros_like(dkv_ref)

        d_o = _nt(dh_ref[...].astype(BF16), w_ref[...])
        for hd in range(D // XD):
            sl = slice(hd * XD, (hd + 1) * XD)
            qh, kh, vh = q_ref[:, sl].astype(BF16), k_ref[:, sl].astype(BF16), v_ref[:, sl].astype(BF16)
            p = _xattn_probs(qh, kh)
            dob = d_o[:, sl].astype(BF16)
            dp = _nt(dob, vh)
            dkv_ref[:, D + hd * XD:D + (hd + 1) * XD] += _tn(p.astype(BF16), dob)
            ds = (p * (dp - jnp.sum(dp * p, axis=1, keepdims=True)) * (XD ** -0.5)).astype(BF16)
            dq_ref[:, sl] = _nn(ds, kh).astype(BF16)
            dkv_ref[:, sl] += _tn(ds, qh)

    row = pl.BlockSpec((bm, D), lambda i: (i, 0))
    return _pc(body, name, (t // bm,),
               [row, row, pl.BlockSpec((m, D), lambda i: (0, 0)), pl.BlockSpec((m, D), lambda i: (0, 1)),
                pl.BlockSpec((D, D), lambda i: (0, 0))],
               [row, pl.BlockSpec((m, 2 * D), lambda i: (0, 0))],
               [_sds((t, D), BF16), _sds((m, 2 * D))])(dh2, q, kv, kv, wo)


NH1 = 8
FOX_BM = 512
FOX_BQ = 512
FOX_BK = 512


def _foxprep_fwd(z1, qg, kg, fbp, name):
    t = z1.shape[0]
    bm = min(FOX_BM, t)

    def body(q_ref, k_ref, v_ref, f_ref, qg_ref, kg_ref, fb_ref, qn_ref, kn_ref, vb_ref, c_ref, carry):
        @pl.when(pl.program_id(0) == 0)
        def _():
            carry[...] = jnp.zeros_like(carry)

        for hd in range(NH1):
            sl = slice(hd * HD, (hd + 1) * HD)
            x = q_ref[:, sl]
            qn_ref[:, sl] = (x * _rstd(x) * qg_ref[...]).astype(BF16)
            x = k_ref[:, sl]
            kn_ref[:, sl] = (x * _rstd(x) * kg_ref[...]).astype(BF16)
        vb_ref[...] = v_ref[...].astype(BF16)
        c = carry[...] + _cumsum_rows(_log_sigmoid(f_ref[...] + fb_ref[...]))
        c_ref[...] = c
        carry[...] = c[bm - 1:bm, :]

    row = lambda c: pl.BlockSpec((bm, D), lambda i: (i, c))
    lane = pl.BlockSpec((bm, HD), lambda i: (i, 4 * D // HD))
    vec = pl.BlockSpec((1, HD), lambda i: (0, 0))
    return _pc(body, name, (t // bm,), [row(0), row(1), row(2), lane, vec, vec, vec],
               [row(0), row(0), row(0), pl.BlockSpec((bm, HD), lambda i: (i, 0))],
               [_sds((t, D), BF16), _sds((t, D), BF16), _sds((t, D), BF16), _sds((t, HD))],
               scratch=[pltpu.VMEM((1, HD), F32)])(z1, z1, z1, z1, qg, kg, fbp)


def _foxprep_bwd(dqn, dkn, z1, qg, kg, fbp, dc, name):
    t = z1.shape[0]
    bm = min(FOX_BM, t)
    nb = t // bm

    def body(dqn_ref, dkn_ref, q_ref, k_ref, f_ref, qg_ref, kg_ref, fb_ref, dc_ref,
             dq_ref, dk_ref, df_ref, dqg_ref, dkg_ref, dfb_ref, carry):
        @pl.when(pl.program_id(0) == 0)
        def _():
            carry[...] = jnp.zeros_like(carry)
            dqg_ref[...] = jnp.zeros_like(dqg_ref)
            dkg_ref[...] = jnp.zeros_like(dkg_ref)
            dfb_ref[...] = jnp.zeros_like(dfb_ref)

        for hd in range(NH1):
            sl = slice(hd * HD, (hd + 1) * HD)
            dx, dgr = _rms_bwd(dqn_ref[:, sl], q_ref[:, sl], qg_ref[...])
            dq_ref[:, sl] = dx.astype(BF16)
            dqg_ref[...] += jnp.sum(dgr, axis=0, keepdims=True)
            dx, dgr = _rms_bwd(dkn_ref[:, sl], k_ref[:, sl], kg_ref[...])
            dk_ref[:, sl] = dx.astype(BF16)
            dkg_ref[...] += jnp.sum(dgr, axis=0, keepdims=True)
        dc_ = dc_ref[...]
        dlogf = _rcumsum_rows(dc_) + carry[...]
        carry[...] += jnp.sum(dc_, axis=0, keepdims=True)
        lanes = lax.broadcasted_iota(jnp.int32, dc_.shape, 1)
        df = jnp.where(lanes < NH1, dlogf * (1.0 - _sigmoid(f_ref[...] + fb_ref[...])), 0.0)
        df_ref[...] = df.astype(BF16)
        dfb_ref[...] += jnp.sum(df, axis=0, keepdims=True)

    rv = lambda i: nb - 1 - i
    row = lambda c: pl.BlockSpec((bm, D), lambda i: (rv(i), c))
    lane = lambda c: pl.BlockSpec((bm, HD), lambda i: (rv(i), c))
    vec = pl.BlockSpec((1, HD), lambda i: (0, 0))
    return _pc(body, name, (nb,), [row(0), row(0), row(0), row(1), lane(4 * D // HD), vec, vec, vec, lane(0)],
               [row(0), row(0), lane(0), vec, vec, vec],
               [_sds((t, D), BF16), _sds((t, D), BF16), _sds((t, HD), BF16), _sds((1, HD)), _sds((1, HD)), _sds((1, HD))],
               scratch=[pltpu.VMEM((1, HD), F32)])(dqn, dkn, z1, z1, z1, qg, kg, fbp, dc)


def _fox_scores(q, k, cq, ck, i, j, bq, bk):
    s = _nt(q, k) * (HD ** -0.5) + cq - ck
    rows = i * bq + lax.broadcasted_iota(jnp.int32, s.shape, 0)
    cols = j * bk + lax.broadcasted_iota(jnp.int32, s.shape, 1)
    return s, cols <= rows


def _fox_fwd(qn, kn, vb, c3, crow, name):
    t = qn.shape[0]
    bq, bk = min(FOX_BQ, t), min(FOX_BK, t)
    nq, nk = t // bq, t // bk

    def body(q_ref, k_ref, v_ref, cq_ref, ck_ref, o_ref, lse_ref, m_s, l_s, acc):
        i, j = pl.program_id(1), pl.program_id(2)

        @pl.when(j == 0)
        def _():
            m_s[...] = jnp.full_like(m_s, NEG)
            l_s[...] = jnp.zeros_like(l_s)
            acc[...] = jnp.zeros_like(acc)

        @pl.when(j * bk < (i + 1) * bq)
        def _():
            s, ok = _fox_scores(q_ref[...], k_ref[...], cq_ref[...], ck_ref[...], i, j, bq, bk)
            s = jnp.where(ok, s, NEG)
            m_new = jnp.maximum(m_s[...], jnp.max(s, axis=1, keepdims=True))
            alpha = jnp.exp(m_s[...] - m_new)
            p = jnp.where(ok, jnp.exp(s - m_new), 0.0)
            l_s[...] = alpha * l_s[...] + jnp.sum(p, axis=1, keepdims=True)
            acc[...] = alpha * acc[...] + _nn(p.astype(BF16), v_ref[...])
            m_s[...] = m_new

        @pl.when(j == nk - 1)
        def _():
            o_ref[...] = acc[...] / l_s[...]
            lse_ref[...] = m_s[...] + jnp.log(l_s[...])

    kmap = lambda h, i, j: (jnp.minimum(j, ((i + 1) * bq - 1) // bk), h)
    return _pc(body, name, (NH1, nq, nk),
               [pl.BlockSpec((bq, HD), lambda h, i, j: (i, h)), pl.BlockSpec((bk, HD), kmap), pl.BlockSpec((bk, HD), kmap),
                pl.BlockSpec((None, bq, 1), lambda h, i, j: (h, i, 0)),
                pl.BlockSpec((None, 1, bk), lambda h, i, j: (h, 0, jnp.minimum(j, ((i + 1) * bq - 1) // bk)))],
               [pl.BlockSpec((bq, HD), lambda h, i, j: (i, h)), pl.BlockSpec((None, bq, 1), lambda h, i, j: (h, i, 0))],
               [_sds((t, D)), _sds((NH1, t, 1))],
               scratch=[pltpu.VMEM((bq, 1), F32), pltpu.VMEM((bq, 1), F32), pltpu.VMEM((bq, HD), F32)])(qn, kn, vb, c3, crow)


def _fox_bwd(qn, kn, vb, c3, crow, lse, delta, do, name):
    t = qn.shape[0]
    bq, bk = min(FOX_BQ, t), min(FOX_BK, t)
    nq, nk = t // bq, t // bk

    def body(q_ref, k_ref, v_ref, cq_ref, ck_ref, lse_ref, dl_ref, do_ref, dq_ref, dk_ref, dv_ref, dc_ref, dcq_ref, dk_s, dv_s, dc_s):
        j, i = pl.program_id(1), pl.program_id(2)

        @pl.when(i == 0)
        def _():
            dk_s[...] = jnp.zeros_like(dk_s)
            dv_s[...] = jnp.zeros_like(dv_s)
            dc_s[...] = jnp.zeros_like(dc_s)

        @pl.when((i == 0) & (j == 0))
        def _():
            dq_ref[...] = jnp.zeros_like(dq_ref)
            dcq_ref[...] = jnp.zeros_like(dcq_ref)

        @pl.when(j * bk < (i + 1) * bq)
        def _():
            q, k = q_ref[...], k_ref[...]
            s, ok = _fox_scores(q, k, cq_ref[...], ck_ref[...], i, j, bq, bk)
            p = jnp.where(ok, jnp.exp(s - lse_ref[...]), 0.0)
            dob = do_ref[...]
            dv_s[...] += _tn(p.astype(BF16), dob)
            ds = p * (_nt(dob, v_ref[...]) - dl_ref[...])
            dsb = ds.astype(BF16)
            rows = pl.ds(pl.multiple_of(i * bq, bq), bq)
            dq_ref[rows, :] += _nn(dsb, k) * (HD ** -0.5)
            dk_s[...] += _tn(dsb, q) * (HD ** -0.5)
            dc_s[...] -= jnp.sum(ds, axis=0, keepdims=True)
            dcq_ref[rows, :] += jnp.sum(ds, axis=1, keepdims=True)

        @pl.when(i == nq - 1)
        def _():
            dk_ref[...] = dk_s[...]
            dv_ref[...] = dv_s[...]
            dc_ref[...] = dc_s[...]

    qi = lambda i, j: jnp.maximum(i, (j * bk) // bq)
    qmap = lambda h, j, i: (qi(i, j), h)
    c3map = lambda h, j, i: (h, qi(i, j), 0)
    return _pc(body, name, (NH1, nk, nq),
               [pl.BlockSpec((bq, HD), qmap), pl.BlockSpec((bk, HD), lambda h, j, i: (j, h)),
                pl.BlockSpec((bk, HD), lambda h, j, i: (j, h)), pl.BlockSpec((None, bq, 1), c3map),
                pl.BlockSpec((None, 1, bk), lambda h, j, i: (h, 0, j)), pl.BlockSpec((None, bq, 1), c3map),
                pl.BlockSpec((None, bq, 1), c3map), pl.BlockSpec((bq, HD), qmap)],
               [pl.BlockSpec((t, HD), lambda h, j, i: (0, h)), pl.BlockSpec((bk, HD), lambda h, j, i: (j, h)),
                pl.BlockSpec((bk, HD), lambda h, j, i: (j, h)), pl.BlockSpec((None, 1, bk), lambda h, j, i: (h, 0, j)),
                pl.BlockSpec((None, t, 1), lambda h, j, i: (h, 0, 0))],
               [_sds((t, D)), _sds((t, D)), _sds((t, D)), _sds((NH1, 1, t)), _sds((NH1, t, 1))],
               scratch=[pltpu.VMEM((bk, HD), F32), pltpu.VMEM((bk, HD), F32), pltpu.VMEM((1, bk), F32)],
               )(qn, kn, vb, c3, crow, lse, delta, do)


def _post1_fwd(o, z1, w, h3, name, bm=512):
    t = o.shape[0]
    bm = min(bm, t)

    def body(o_ref, g_ref, w_ref, h_ref, out_ref, og_ref):
        og_ref[...] = (o_ref[...] * _sigmoid(g_ref[...])).astype(BF16)
        out_ref[...] = h_ref[...] + _nn(og_ref[...], w_ref[...])

    row = lambda c: pl.BlockSpec((bm, D), lambda i: (i, c))
    return _pc(body, name, (t // bm,), [row(0), row(3), pl.BlockSpec((D, D), lambda i: (0, 0)), row(0)],
               [row(0), row(0)], [_sds((t, D)), _sds((t, D), BF16)])(o, z1, w, h3)


def _post1_bwd(dh4, w, o, z1, name, bm=512):
    t = o.shape[0]
    bm = min(bm, t)

    def body(dh_ref, w_ref, o_ref, g_ref, do_ref, dg_ref, dl_ref):
        d_og = _nt(dh_ref[...].astype(BF16), w_ref[...])
        o_, sg = o_ref[...], _sigmoid(g_ref[...])
        dob = (d_og * sg).astype(BF16)
        do_ref[...] = dob
        dg_ref[...] = (d_og * o_ * sg * (1.0 - sg)).astype(BF16)
        prod = dob.astype(F32) * o_
        for hd in range(NH1):
            dl_ref[hd] = jnp.sum(prod[:, hd * HD:(hd + 1) * HD], axis=1, keepdims=True)

    row = lambda c: pl.BlockSpec((bm, D), lambda i: (i, c))
    return _pc(body, name, (t // bm,), [row(0), pl.BlockSpec((D, D), lambda i: (0, 0)), row(0), row(3)],
               [row(0), row(0), pl.BlockSpec((NH1, bm, 1), lambda i: (0, i, 0))],
               [_sds((t, D), BF16), _sds((t, D), BF16), _sds((NH1, t, 1))])(dh4, w, o, z1)


def _final(h, g, tgt, name, bm=512):
    t = h.shape[0]
    bm = min(bm, t)

    def body(h_ref, g_ref, t_ref, l_ref, dh_ref, dg_ref):
        @pl.when(pl.program_id(0) == 0)
        def _():
            l_ref[...] = jnp.zeros_like(l_ref)
            dg_ref[...] = jnp.zeros_like(dg_ref)

        x, gv = h_ref[...], g_ref[...]
        r = _rstd(x)
        xh = x * r
        e = xh * gv - t_ref[...]
        l_ref[...] += 0.5 * jnp.sum(jnp.mean(e * e, axis=1, keepdims=True), axis=0, keepdims=True)
        dy = e * (1.0 / D)
        dg_ref[...] += jnp.sum(dy * xh, axis=0, keepdims=True)
        dxh = dy * gv
        dh_ref[...] = r * (dxh - xh * jnp.mean(dxh * xh, axis=1, keepdims=True))

    row = pl.BlockSpec((bm, D), lambda i: (i, 0))
    vec = pl.BlockSpec((1, D), lambda i: (0, 0))
    return _pc(body, name, (t // bm,), [row, vec, row], [pl.BlockSpec((1, HD), lambda i: (0, 0)), row, vec],
               [_sds((1, HD)), _sds((t, D)), _sds((1, D))])(h, g, tgt)


def _adam(w, g, m, v, name):
    r, c = w.shape
    br = min(r, 256)

    def body(w_ref, g_ref, m_ref, v_ref, d_ref, mo_ref, vo_ref):
        gv = g_ref[...]
        mn = ADAM_B1 * m_ref[...] + (1.0 - ADAM_B1) * gv
        vn = ADAM_B2 * v_ref[...] + (1.0 - ADAM_B2) * jnp.square(gv)
        m_hat = mn / (1.0 - ADAM_B1 ** ADAM_STEP)
        v_hat = vn / (1.0 - ADAM_B2 ** ADAM_STEP)
        d_ref[...] = -ADAM_LR * (m_hat / (jnp.sqrt(v_hat) + ADAM_EPS) + ADAM_WD * w_ref[...])
        mo_ref[...] = mn
        vo_ref[...] = vn

    blk = pl.BlockSpec((br, c), lambda i: (i, 0))
    return _pc(body, name, (r // br,), [blk] * 4, [blk] * 3, [_sds((r, c))] * 3)(w, g, m, v)


ZW = 4224
GATE0 = 4096


def _pack_w_in0(w):
    return jnp.concatenate([w[:, :2048], w[:, 2056:], w[:, 2048:2056], jnp.zeros((w.shape[0], ZW - 4104), w.dtype)], axis=1)


def _unpack_w_in0(g):
    return jnp.concatenate([g[:, :2048], g[:, GATE0:GATE0 + 8], g[:, 2048:GATE0]], axis=1)


def _pack_w_in1(w):
    return jnp.concatenate([w, jnp.zeros((w.shape[0], ZW - 4104), w.dtype)], axis=1)


def _unpack_w_in1(g):
    return g[:, :4104]


def _local_step(x, mem, tgt, W, S):
    t = x.shape[0]
    row = lambda v: v.reshape(1, -1)
    G = {}

    kv, mn = _memkv_fwd(mem, row(S["mem_norm_g"]), W["wkv_s"], "memkv_fwd")
    z0, u0 = _norm_mm(x, S["norm_mix_g"][0:1], W["w_in0"], "in0_fwd")
    qk = _conv_fwd(z0, S["conv_w"], "conv_fwd")
    g8 = z0[:, GATE0:GATE0 + 8]
    gates3 = jnp.stack([g8[:, :4].T, g8[:, 4:].T], axis=-1)
    gb = S["gate_b"]
    bias3 = jnp.stack([gb[0, :4], gb[0, 4:]], axis=-1)[:, None, :]
    hm, cs, ns, ms = _mlstm_fwd(qk, z0, gates3, bias3, "mlstm_fwd")
    hh, ss = _hgrn_fwd(z0, S["lb_logits"], "hgrn_fwd")
    h1, y0 = _post0_fwd(hm, hh, z0, S["mlstm_norm_g"], S["hgrn_norm_g"], W["w_out0"], x, "post0_fwd")

    def xattn_mlp_fwd(h, l):
        q, ux = _norm_mm(h, S["norm_xattn_g"][l:l + 1], W["wq"][l], f"xq{l}_fwd")
        h2, ox = _xattn_fwd(q, kv, W["wo"][l], h, f"xattn{l}_fwd")
        h3, a, um = _mlp_fwd(h2, S["norm_mlp_g"][l:l + 1], W["w1s"][l], W["w2"][l], f"mlp{l}_fwd")
        return h3, (h, q, ux, ox, h2, a, um)

    h3, sv0 = xattn_mlp_fwd(h1, 0)
    z1, u1 = _norm_mm(h3, S["norm_mix_g"][1:2], W["w_in1"], "in1_fwd")
    fbp = jnp.pad(S["c_fgate_b"], ((0, 0), (0, HD - NH1)))
    qn, kn, vb, c = _foxprep_fwd(z1, S["c_qnorm_g"], S["c_knorm_g"], fbp, "foxprep_fwd")
    c8t = c[:, :NH1].T
    c3, crow = c8t[:, :, None], c8t[:, None, :]
    o1, lse = _fox_fwd(qn, kn, vb, c3, crow, "fox_fwd")
    h4, og = _post1_fwd(o1, z1, W["w_out1"], h3, "post1_fwd")
    h6, sv1 = xattn_mlp_fwd(h4, 1)
    lossp, dh, G["final_norm_g"] = _final(h6, row(S["final_norm_g"]), tgt, "final")

    dkv = None
    dgx, dgm, dwq, dwo, dw1, dw2 = [None, None], [None, None], [None, None], [None, None], [None, None], [None, None]

    def xattn_mlp_bwd(dh, l, sv):
        nonlocal dkv
        h, q, ux, ox, h2, a, um = sv
        dh2, da, r, dgm[l] = _mlp_bwd(dh, a, W["w1s"][l], W["w2"][l], h2, S["norm_mlp_g"][l:l + 1], f"mlp{l}_bwd")
        dw1[l] = _mm_tn(um, da, f"mlp{l}_dw1")
        dw2[l] = _mm_tn(r, dh, f"mlp{l}_dw2")
        dq, dkv_l = _xattn_bwd(dh2, q, kv, W["wo"][l], f"xattn{l}_bwd")
        dkv = dkv_l if dkv is None else dkv + dkv_l
        dwo[l] = _mm_tn(ox, dh2, f"xattn{l}_dwo")
        dwq[l] = _mm_tn(ux, dq, f"xattn{l}_dwq")
        dh1, dgx[l] = _bwd_in(dq, W["wq"][l], h, S["norm_xattn_g"][l:l + 1], dh2, f"xq{l}_bwd")
        return dh1

    dh4 = xattn_mlp_bwd(dh, 1, sv1)
    do, dgate, delta = _post1_bwd(dh4, W["w_out1"], o1, z1, "post1_bwd")
    G["w_out1"] = _mm_tn(og, dh4, "post1_dw")
    dqn, dkn, dv1, dcrow, dcq = _fox_bwd(qn, kn, vb, c3, crow, lse, delta, do, "fox_bwd")
    dc = jnp.pad((dcrow[:, 0, :] + dcq[:, :, 0]).T, ((0, 0), (0, HD - NH1)))
    dqr, dkr, df1, G["c_qnorm_g"], G["c_knorm_g"], dfb = _foxprep_bwd(
        dqn, dkn, z1, S["c_qnorm_g"], S["c_knorm_g"], fbp, dc, "foxprep_bwd")
    G["c_fgate_b"] = dfb[:, :NH1]
    dz1 = jnp.concatenate([dqr, dkr, dv1.astype(BF16), dgate, df1], axis=1)
    G["w_in1"] = _mm_tn(u1, dz1, "in1_dw")
    dh3, dgmix1 = _bwd_in(dz1, W["w_in1"], h3, S["norm_mix_g"][1:2], dh4, "in1_bwd")
    dh1 = xattn_mlp_bwd(dh3, 0, sv0)

    dhm, dhh, doa, dgb, G["mlstm_norm_g"], G["hgrn_norm_g"] = _post0_bwd(
        dh1, W["w_out0"], hm, hh, z0, S["mlstm_norm_g"], S["hgrn_norm_g"], "post0_bwd")
    G["w_out0"] = _mm_tn(y0, dh1, "post0_dw")
    dqa, dka, dva, dgates3 = _mlstm_bwd(qk, z0, gates3, bias3, cs, ns, ms, dhm, "mlstm_bwd")
    dqb, dfb0, dib, G["lb_logits"] = _hgrn_bwd(z0, S["lb_logits"], ss, dhh, "hgrn_bwd")
    duc, G["conv_w"] = _conv_bwd(z0, S["conv_w"], jnp.concatenate([dqa, dka], axis=1), "conv_bwd")
    dg8 = jnp.concatenate([dgates3[:, :, 0].T, dgates3[:, :, 1].T], axis=1)
    G["gate_b"] = jnp.sum(dg8, axis=0, keepdims=True)
    dz0 = jnp.concatenate([duc, dva.astype(BF16), doa, dqb, dfb0, dib, dgb,
                           jnp.pad(dg8, ((0, 0), (0, HD - 8))).astype(BF16)], axis=1)
    G["w_in0"] = _mm_tn(u0, dz0, "in0_dw")
    dx, dgmix0 = _bwd_in(dz0, W["w_in0"], x, S["norm_mix_g"][0:1], dh1, "in0_bwd")

    G["wkv"] = _mm_tn(mn, dkv, "memkv_dw")
    G["mem_norm_g"] = _memkv_bwd(dkv, W["wkv_s"], mem, row(S["mem_norm_g"]), "memkv_bwd")
    G["norm_mix_g"] = jnp.concatenate([dgmix0, dgmix1], axis=0)
    G["norm_xattn_g"] = jnp.concatenate(dgx, axis=0)
    G["norm_mlp_g"] = jnp.concatenate(dgm, axis=0)
    G["wq"], G["wo"], G["w1"], G["w2"] = jnp.stack(dwq), jnp.stack(dwo), jnp.stack(dw1), jnp.stack(dw2)
    return lossp[0, 0], dx, G


ANY = pl.BlockSpec(memory_space=pl.ANY)
NCHIP = 4
RS_ROWS = 4224
RS_TILE = 384


def _place():
    x, y, c = lax.axis_index("x"), lax.axis_index("y"), lax.axis_index("c")
    return x, y, c, [(1 - x, y), (x, 1 - y), (1 - x, 1 - y)]


def _comm_call(body, name, ins, out_shapes, sems):
    return pl.pallas_call(body, name=name, in_specs=[ANY] * len(ins), out_specs=[ANY] * len(out_shapes),
                          out_shape=out_shapes, scratch_shapes=sems)(*ins)


def _gather_weights(arrs, name):
    n = len(arrs)

    def body(*refs):
        ins, outs = refs[:n], refs[n:2 * n]
        send_i, recv_i, send_d, recv_d = refs[2 * n:]
        x, y, c, chips = _place()
        me = 2 * x + y

        def half(a, cc):
            h = arrs[a].shape[0] // 2
            return pl.ds(pl.multiple_of(cc * h, h), h)

        def ici(a, k, src_chip, dst_dev):
            return pltpu.make_async_remote_copy(
                src_ref=ins[a].at[half(a, c)], dst_ref=outs[a].at[src_chip, half(a, c)], send_sem=send_i.at[a, k],
                recv_sem=recv_i.at[a, k], device_id=dst_dev, device_id_type=MESH)

        def d2d(a, k, src_chip, cc):
            reg = outs[a].at[src_chip, half(a, cc)]
            return pltpu.make_async_remote_copy(src_ref=reg, dst_ref=reg, send_sem=send_d.at[a, k], recv_sem=recv_d.at[a, k],
                                                device_id=(x, y, 1 - c), device_id_type=MESH)

        for a in range(n):
            for k, (px, py) in enumerate(chips):
                ici(a, k, me, (px, py, c)).start()
        for k, (px, py) in enumerate(chips):
            for a in range(n):
                ici(a, k, 2 * px + py, (px, py, c)).wait_recv()
                d2d(a, k, 2 * px + py, c).start()
        for k, (px, py) in enumerate(chips):
            for a in range(n):
                ici(a, k, me, (px, py, c)).wait_send()
                d2d(a, k, 2 * px + py, c).wait_send()
                d2d(a, k, 2 * px + py, 1 - c).wait_recv()

    sem = lambda: pltpu.SemaphoreType.DMA((n, 3))
    return _comm_call(body, name, arrs, [_sds((NCHIP,) + a.shape, a.dtype) for a in arrs], [sem(), sem(), sem(), sem()])


def _pair_exchange(arrs, name):
    n = len(arrs)

    def body(*refs):
        ins, outs = refs[:n], refs[n:2 * n]
        send, recv = refs[2 * n:]
        x, y, c, _ = _place()
        copies = []
        for a in range(n):
            h = arrs[a].shape[1] // 2
            cp = pltpu.make_async_remote_copy(src_ref=ins[a].at[:, pl.ds(pl.multiple_of((1 - c) * h, h), h)], dst_ref=outs[a],
                                              send_sem=send.at[a], recv_sem=recv.at[a], device_id=(x, y, 1 - c), device_id_type=MESH)
            cp.start()
            copies.append(cp)
        for cp in copies:
            cp.wait()

    return _comm_call(body, name, arrs, [_sds((a.shape[0], a.shape[1] // 2, a.shape[2]), a.dtype) for a in arrs],
                      [pltpu.SemaphoreType.DMA((n,)), pltpu.SemaphoreType.DMA((n,))])


def _chip_exchange(arrs, name):
    n = len(arrs)

    def body(*refs):
        ins, outs = refs[:n], refs[n:2 * n]
        send, recv = refs[2 * n:]
        x, y, c, chips = _place()
        me = 2 * x + y
        copies = []
        for a in range(n):
            for k, (px, py) in enumerate(chips):
                r = pltpu.make_async_remote_copy(src_ref=ins[a].at[2 * px + py], dst_ref=outs[a].at[me], send_sem=send.at[a, k],
                                                 recv_sem=recv.at[a, k], device_id=(px, py, c), device_id_type=MESH)
                r.start()
                copies.append(r)
        for cp in copies:
            cp.wait()

    return _comm_call(body, name, arrs, [_sds(a.shape, a.dtype) for a in arrs],
                      [pltpu.SemaphoreType.DMA((n, 3)), pltpu.SemaphoreType.DMA((n, 3))])


def _pair_swap(arrs, name):
    n = len(arrs)

    def body(*refs):
        ins, outs = refs[:n], refs[n:2 * n]
        send, recv = refs[2 * n:]
        x, y, c, _ = _place()
        copies = []
        for a in range(n):
            cp = pltpu.make_async_remote_copy(src_ref=ins[a], dst_ref=outs[a], send_sem=send.at[a], recv_sem=recv.at[a],
                                              device_id=(x, y, 1 - c), device_id_type=MESH)
            cp.start()
            copies.append(cp)
        for cp in copies:
            cp.wait()

    return _comm_call(body, name, arrs, [_sds(a.shape, a.dtype) for a in arrs],
                      [pltpu.SemaphoreType.DMA((n,)), pltpu.SemaphoreType.DMA((n,))])


def _all_gather_devices(v, name):
    def body(v_ref, o_ref, send, recv, loc):
        x, y, c, _ = _place()
        me = 4 * x + 2 * y + c
        own = pltpu.make_async_copy(v_ref, o_ref.at[me], loc)
        own.start()
        copies = [own]
        for k in range(1, 8):
            fx, fy, fc = (k >> 2) & 1, (k >> 1) & 1, k & 1
            peer = (x ^ fx, y ^ fy, c ^ fc)
            r = pltpu.make_async_remote_copy(src_ref=v_ref, dst_ref=o_ref.at[me], send_sem=send.at[k - 1],
                                             recv_sem=recv.at[k - 1], device_id=peer, device_id_type=MESH)
            r.start()
            copies.append(r)
        for cp in copies:
            cp.wait()

    return _comm_call(body, name, [v], [_sds((8,) + v.shape, v.dtype)],
                      [pltpu.SemaphoreType.DMA((7,)), pltpu.SemaphoreType.DMA((7,)), pltpu.SemaphoreType.DMA])[0]


def _row_tile(r):
    return next((b for b in (512, 384, 256, 128, 64, 32, 16) if r % b == 0), r)


def _add2(a, b, out_dtype, name):
    r, w = a.shape
    br = _row_tile(r)

    def body(a_ref, b_ref, o_ref):
        o_ref[...] = (a_ref[...].astype(F32) + b_ref[...].astype(F32)).astype(out_dtype)

    blk = pl.BlockSpec((br, w), lambda i: (i, 0))
    return _pc(body, name, (r // br,), [blk, blk], blk, _sds((r, w), out_dtype))(a, b)


def _sum_slots(a, out_dtype, name, extra=None):
    n, r, w = a.shape
    br = _row_tile(r)

    def body(*refs):
        a_ref, o_ref = refs[0], refs[-1]
        acc = a_ref[0].astype(F32)
        for s in range(1, n):
            acc = acc + a_ref[s].astype(F32)
        if extra is not None:
            acc = acc + refs[1][...].astype(F32)
        o_ref[...] = acc.astype(out_dtype)

    ins = [a] + ([extra] if extra is not None else [])
    specs = [pl.BlockSpec((n, br, w), lambda i: (0, i, 0))] + ([pl.BlockSpec((br, w), lambda i: (i, 0))] if extra is not None else [])
    return _pc(body, name, (r // br,), specs, pl.BlockSpec((br, w), lambda i: (i, 0)), _sds((r, w), out_dtype))(*ins)


SMALL = ["norm_mix_g", "norm_xattn_g", "norm_mlp_g", "final_norm_g", "mem_norm_g", "hgrn_lb_logits", "mlstm_norm_g",
         "hgrn_norm_g", "c_qnorm_g", "c_knorm_g", "ab_gate_b", "c_fgate_b"]
SMALL_ROWS = 16


def _pack_small(parts):
    flat = jnp.concatenate([p.reshape(-1).astype(F32) for p in parts])
    return jnp.pad(flat, (0, SMALL_ROWS * D - flat.shape[0])).reshape(SMALL_ROWS, D)


def _unpack_small(buf, shapes):
    flat, out, off = buf.reshape(-1), [], 0
    for s in shapes:
        n = 1
        for d in s:
            n *= d
        out.append(flat[off:off + n].reshape(s))
        off += n
    return out


def kernel(x, mem, norm_mix_g, norm_xattn_g, norm_mlp_g, final_norm_g, ab_w_in, ab_conv_w, ab_gate_b, hgrn_lb_logits, mlstm_norm_g, hgrn_norm_g, ab_w_out, c_w_in, c_fgate_b, c_qnorm_g, c_knorm_g, c_w_out, mem_norm_g, mem_w_kv, xa_w_q, xa_w_o, mlp_w1, mlp_w2, loss_target, m_norm_mix_g, m_norm_xattn_g, m_norm_mlp_g, m_final_norm_g, m_ab_w_in, m_ab_conv_w, m_ab_gate_b, m_hgrn_lb_logits, m_mlstm_norm_g, m_hgrn_norm_g, m_ab_w_out, m_c_w_in, m_c_fgate_b, m_c_qnorm_g, m_c_knorm_g, m_c_w_out, m_mem_norm_g, m_mem_w_kv, m_xa_w_q, m_xa_w_o, m_mlp_w1, m_mlp_w2, v_norm_mix_g, v_norm_xattn_g, v_norm_mlp_g, v_final_norm_g, v_ab_w_in, v_ab_conv_w, v_ab_gate_b, v_hgrn_lb_logits, v_mlstm_norm_g, v_hgrn_norm_g, v_ab_w_out, v_c_w_in, v_c_fgate_b, v_c_qnorm_g, v_c_knorm_g, v_c_w_out, v_mem_norm_g, v_mem_w_kv, v_xa_w_q, v_xa_w_o, v_mlp_w1, v_mlp_w2):
    A = dict(locals())
    chip = 2 * lax.axis_index("x") + lax.axis_index("y")

    big = ["ab_w_in", "c_w_in", "ab_w_out", "c_w_out", "mem_w_kv", "xa_w_q", "xa_w_o", "mlp_w1", "mlp_w2"]
    shard2d = {"ab_w_in": (D, 1026), "c_w_in": (D, 1026), "ab_w_out": (256, D), "c_w_out": (256, D), "mem_w_kv": (D, 512),
               "xa_w_q": (512, D), "xa_w_o": (512, D), "mlp_w1": (2 * D, D), "mlp_w2": (2 * D, D)}
    own = [A[n].reshape(shard2d[n]).astype(BF16) for n in big] + [jnp.pad(ab_conv_w[0], ((0, 16 - CONV_W), (0, 0)))]
    gathered = _gather_weights(own, "gather_weights")
    gathered = [lax.dynamic_update_index_in_dim(g, o, chip, 0) for g, o in zip(gathered, own)]
    gathered[-1] = gathered[-1][:, :CONV_W]
    gw = dict(zip(big, gathered[:-1]))
    cols = lambda g: jnp.concatenate([g[k] for k in range(NCHIP)], axis=1)
    per_layer = lambda g: g.reshape(NCHIP, 2, -1, D).transpose(1, 0, 2, 3)
    W = dict(
        w_in0=_pack_w_in0(cols(gw["ab_w_in"])), w_in1=_pack_w_in1(cols(gw["c_w_in"])),
        w_out0=gw["ab_w_out"].reshape(D, D), w_out1=gw["c_w_out"].reshape(D, D), wkv_s=gw["mem_w_kv"],
        wq=per_layer(gw["xa_w_q"]).reshape(2, D, D), wo=per_layer(gw["xa_w_o"]).reshape(2, D, D),
        w1s=per_layer(gw["mlp_w1"]), w2=per_layer(gw["mlp_w2"]).reshape(2, 4 * D, D))
    S = dict(norm_mix_g=norm_mix_g, norm_xattn_g=norm_xattn_g, norm_mlp_g=norm_mlp_g, final_norm_g=final_norm_g,
             conv_w=cols(gathered[-1]), gate_b=ab_gate_b, lb_logits=hgrn_lb_logits, mlstm_norm_g=mlstm_norm_g,
             hgrn_norm_g=hgrn_norm_g, c_fgate_b=c_fgate_b, c_qnorm_g=c_qnorm_g, c_knorm_g=c_knorm_g, mem_norm_g=mem_norm_g)

    lossp, dx, G = _local_step(x[0], mem[0], loss_target[0], W, S)

    gsmall = {"norm_mix_g": G["norm_mix_g"], "norm_xattn_g": G["norm_xattn_g"], "norm_mlp_g": G["norm_mlp_g"],
              "final_norm_g": G["final_norm_g"], "mem_norm_g": G["mem_norm_g"], "hgrn_lb_logits": G["lb_logits"],
              "mlstm_norm_g": G["mlstm_norm_g"], "hgrn_norm_g": G["hgrn_norm_g"], "c_qnorm_g": G["c_qnorm_g"],
              "c_knorm_g": G["c_knorm_g"], "ab_gate_b": G["gate_b"], "c_fgate_b": G["c_fgate_b"]}
    packed = _pack_small([gsmall[n] for n in SMALL] + [G["conv_w"], lossp])
    red = _sum_slots(_all_gather_devices(packed, "gather_small"), F32, "sum_small")
    small_shapes = [A[n].shape for n in SMALL]
    *gs, gconv, loss = _unpack_small(red, small_shapes + [(CONV_W, D), ()])
    gs = dict(zip(SMALL, gs))
    gconv = lax.dynamic_slice_in_dim(gconv, chip * 256, 256, axis=1)[None]

    def by_cols(g, n):
        return g.reshape(g.shape[0], NCHIP, n).transpose(1, 0, 2).reshape(NCHIP, -1)

    def by_rows(g):
        return g.reshape(NCHIP, -1)

    def by_rows_l(g):
        return g.reshape(2, NCHIP, -1).transpose(1, 0, 2).reshape(NCHIP, -1)

    def by_cols_l(g, n):
        return g.reshape(2, g.shape[1], NCHIP, n).transpose(2, 0, 1, 3).reshape(NCHIP, -1)

    def stack_cols(g):
        gb16 = g.astype(BF16)
        return jnp.stack([gb16[:, 1026 * k:1026 * (k + 1)] for k in range(NCHIP)])

    main = ["ab_w_out", "c_w_out", "mem_w_kv", "xa_w_q", "xa_w_o", "mlp_w1", "mlp_w2"]
    parts = [by_rows(G["w_out0"]), by_rows(G["w_out1"]), by_cols(G["wkv"], 512), by_rows_l(G["wq"]), by_rows_l(G["wo"]),
             by_cols_l(G["w1"], D), by_rows_l(G["w2"])]
    flat = jnp.concatenate(parts, axis=1).astype(BF16).reshape(NCHIP, -1, D)
    send = [flat, stack_cols(_unpack_w_in0(G["w_in0"])), stack_cols(_unpack_w_in1(G["w_in1"]))]
    core = lax.axis_index("c")
    theirs = _pair_exchange(send, "pair_exchange")
    psums = []
    for i, (a, th) in enumerate(zip(send, theirs)):
        h = a.shape[1] // 2
        mine = lax.dynamic_slice_in_dim(a, core * h, h, axis=1)
        psums.append(_add2(mine.reshape(-1, a.shape[2]), th.reshape(-1, a.shape[2]), BF16, f"pair_sum{i}").reshape(th.shape))
    from_chips = _chip_exchange(psums, "chip_exchange")
    rhalf = []
    for i, (f, p) in enumerate(zip(from_chips, psums)):
        f = lax.dynamic_update_index_in_dim(f, lax.dynamic_index_in_dim(p, chip, 0, keepdims=False), chip, 0)
        rhalf.append(_sum_slots(f, F32, f"chip_sum{i}"))
    other = _pair_swap(rhalf, "pair_swap")
    rfull = [jnp.where(core == 0, jnp.concatenate([m_, o_], axis=0), jnp.concatenate([o_, m_], axis=0))
             for m_, o_ in zip(rhalf, other)]
    gbig, off = {"ab_w_in": rfull[1], "c_w_in": rfull[2]}, 0
    rmain = rfull[0].reshape(-1)
    for n in main:
        r, c = shard2d[n]
        gbig[n] = rmain[off:off + r * c].reshape(r, c)
        off += r * c

    out_g, out_d, out_m, out_v = {}, {}, {}, {}
    for n in big:
        d_, m_, v_ = _adam(A[n].reshape(shard2d[n]), gbig[n], A["m_" + n].reshape(shard2d[n]), A["v_" + n].reshape(shard2d[n]), "adam_" + n)
        out_g[n] = gbig[n].reshape(A[n].shape)
        out_d[n], out_m[n], out_v[n] = d_.reshape(A[n].shape), m_.reshape(A[n].shape), v_.reshape(A[n].shape)
    sd, sm, sv = _adam(_pack_small([A[n] for n in SMALL]), _pack_small([gs[n] for n in SMALL]),
                       _pack_small([A["m_" + n] for n in SMALL]), _pack_small([A["v_" + n] for n in SMALL]), "adam_small")
    for n, d_, m_, v_ in zip(SMALL, _unpack_small(sd, small_shapes), _unpack_small(sm, small_shapes), _unpack_small(sv, small_shapes)):
        out_g[n], out_d[n], out_m[n], out_v[n] = gs[n], d_, m_, v_
    cd, cm_, cv = _adam(ab_conv_w[0], gconv[0], m_ab_conv_w[0], v_ab_conv_w[0], "adam_conv")
    out_g["ab_conv_w"], out_d["ab_conv_w"], out_m["ab_conv_w"], out_v["ab_conv_w"] = gconv, cd[None], cm_[None], cv[None]

    order = ["norm_mix_g", "norm_xattn_g", "norm_mlp_g", "final_norm_g", "ab_w_in", "ab_conv_w", "ab_gate_b", "hgrn_lb_logits",
             "mlstm_norm_g", "hgrn_norm_g", "ab_w_out", "c_w_in", "c_fgate_b", "c_qnorm_g", "c_knorm_g", "c_w_out", "mem_norm_g",
             "mem_w_kv", "xa_w_q", "xa_w_o", "mlp_w1", "mlp_w2"]
    return (loss, dx[None], *[out_g[n] for n in order], *[out_d[n] for n in order], *[out_m[n] for n in order],
            *[out_v[n] for n in order])
```

```python
import functools

import jax
import jax.numpy as jnp
from jax import lax
from jax.experimental import pallas as pl
from jax.experimental.pallas import tpu as pltpu

F32 = jnp.float32
BF16 = jnp.bfloat16
EPS = 1e-6
D = 1024
CHUNK = 64
HD = 128
XD = 256
NEG = -1e30
VMEM_LIMIT_V7X = 56 * 1024 * 1024
ADAM_LR, ADAM_B1, ADAM_B2, ADAM_EPS, ADAM_WD, ADAM_STEP = 0.001, 0.9, 0.999, 1e-08, 0.01, 10
MESH = pl.DeviceIdType.MESH


def _pc(body, name, grid, in_specs, out_specs, out_shape, scratch=(), **kw):
    return pl.pallas_call(
        body, name=name, grid=grid, in_specs=in_specs, out_specs=out_specs, out_shape=out_shape,
        scratch_shapes=scratch,
        compiler_params=pltpu.CompilerParams(
            dimension_semantics=("arbitrary",) * len(grid), vmem_limit_bytes=VMEM_LIMIT_V7X), **kw)


def _sds(shape, dtype=F32):
    return jax.ShapeDtypeStruct(shape, dtype)


def _blk(n, target):
    return max(b for b in range(128, max(target, 128) + 1, 128) if n % b == 0)


def _dot(a, b, dims):
    return lax.dot_general(a, b, (dims, ((), ())), preferred_element_type=F32)


def _nn(a, b):
    return _dot(a, b, ((1,), (0,)))


def _nt(a, b):
    return _dot(a, b, ((1,), (1,)))


def _tn(a, b):
    return _dot(a, b, ((0,), (0,)))


def _sigmoid(x):
    return 1.0 / (1.0 + jnp.exp(-x))


def _log_sigmoid(x):
    return jnp.minimum(x, 0.0) - jnp.log(1.0 + jnp.exp(-jnp.abs(x)))


def _rstd(x):
    return lax.rsqrt(jnp.mean(x * x, axis=-1, keepdims=True) + EPS)


def _rms_bwd(du, x, g):
    r = _rstd(x)
    xh = x * r
    dxh = du * g
    dx = r * (dxh - xh * jnp.mean(dxh * xh, axis=-1, keepdims=True))
    return dx, du * xh


def _norm_mm(h, g, w, name, bm=512, bn=512):
    t, n = h.shape[0], w.shape[1]
    bm, bn = min(bm, t), _blk(n, 3 * bn)

    def body(h_ref, g_ref, w_ref, z_ref, u_ref):
        @pl.when(pl.program_id(1) == 0)
        def _():
            x = h_ref[...]
            u_ref[...] = (x * _rstd(x) * g_ref[...]).astype(BF16)
        z_ref[...] = _nn(u_ref[...], w_ref[...])

    return _pc(body, name, (t // bm, n // bn),
               [pl.BlockSpec((bm, D), lambda i, j: (i, 0)), pl.BlockSpec((1, D), lambda i, j: (0, 0)),
                pl.BlockSpec((D, bn), lambda i, j: (0, j))],
               [pl.BlockSpec((bm, bn), lambda i, j: (i, j)), pl.BlockSpec((bm, D), lambda i, j: (i, 0))],
               [_sds((t, n)), _sds((t, D), BF16)])(h, g, w)


def _mm_tn(a, b, name, bm=1024, bn=1024, bt=512, col_chips=None):
    t, m = a.shape
    n = b.shape[1]
    bm, bn, bt = _blk(m, bm), (n // col_chips if col_chips else _blk(n, bn + bn // 2)), min(bt, t)
    nt = t // bt

    def body(a_ref, b_ref, o_ref, acc):
        k = pl.program_id(2)

        @pl.when(k == 0)
        def _():
            acc[...] = jnp.zeros_like(acc)

        acc[...] += _tn(a_ref[...].astype(BF16), b_ref[...].astype(BF16))

        @pl.when(k == nt - 1)
        def _():
            o_ref[...] = acc[...].astype(BF16)

    if col_chips:
        out_spec, out_shape = pl.BlockSpec((None, bm, bn), lambda i, j, k: (j, i, 0)), _sds((col_chips, m, bn), BF16)
    else:
        out_spec, out_shape = pl.BlockSpec((bm, bn), lambda i, j, k: (i, j)), _sds((m, n), BF16)
    return _pc(body, name, (m // bm, n // bn, nt),
               [pl.BlockSpec((bt, bm), lambda i, j, k: (k, i)), pl.BlockSpec((bt, bn), lambda i, j, k: (k, j))],
               out_spec, out_shape, scratch=[pltpu.VMEM((bm, bn), F32)])(a, b)


def _bwd_in(dz, w, h, g, dh, name, bm=512, bk=1024):
    t, n = dz.shape
    bm, bk = min(bm, t), _blk(n, bk + bk // 2)
    nk = n // bk

    def body(dz_ref, w_ref, h_ref, g_ref, dh_ref, o_ref, dg_ref, acc):
        i, k = pl.program_id(0), pl.program_id(1)

        @pl.when(k == 0)
        def _():
            acc[...] = jnp.zeros_like(acc)

        @pl.when((i == 0) & (k == 0))
        def _():
            dg_ref[...] = jnp.zeros_like(dg_ref)

        acc[...] += _nt(dz_ref[...], w_ref[...])

        @pl.when(k == nk - 1)
        def _():
            dx, dgr = _rms_bwd(acc[...], h_ref[...], g_ref[...])
            o_ref[...] = dh_ref[...] + dx
            dg_ref[...] += jnp.sum(dgr, axis=0, keepdims=True)

    return _pc(body, name, (t // bm, nk),
               [pl.BlockSpec((bm, bk), lambda i, k: (i, k)), pl.BlockSpec((D, bk), lambda i, k: (0, k)),
                pl.BlockSpec((bm, D), lambda i, k: (i, 0)), pl.BlockSpec((1, D), lambda i, k: (0, 0)),
                pl.BlockSpec((bm, D), lambda i, k: (i, 0))],
               [pl.BlockSpec((bm, D), lambda i, k: (i, 0)), pl.BlockSpec((1, D), lambda i, k: (0, 0))],
               [_sds((t, D)), _sds((1, D))], scratch=[pltpu.VMEM((bm, D), F32)])(dz, w, h, g, dh)


def _mlp_fwd(h, g, w1s, w2, l, name, bm=512):
    t = h.shape[0]
    bm = min(bm, t)
    nk = w1s.shape[0]

    def body(h_ref, g_ref, w1_ref, w2_ref, o_ref, a_ref, u_ref, acc):
        k = pl.program_id(1)

        @pl.when(k == 0)
        def _():
            x = h_ref[...]
            u_ref[...] = (x * _rstd(x) * g_ref[...]).astype(BF16)
            acc[...] = jnp.zeros_like(acc)

        a = _nn(u_ref[...], w1_ref[...])
        a_ref[...] = a
        r = jnp.square(jnp.maximum(a, 0.0)).astype(BF16)
        acc[...] += _nn(r, w2_ref[...])

        @pl.when(k == nk - 1)
        def _():
            o_ref[...] = h_ref[...] + acc[...]

    return _pc(body, name, (t // bm, nk),
               [pl.BlockSpec((bm, D), lambda i, k: (i, 0)), pl.BlockSpec((1, D), lambda i, k: (0, 0)),
                pl.BlockSpec((None, None, D, D), lambda i, k: (k, l, 0, 0)), pl.BlockSpec((None, None, D, D), lambda i, k: (k, l, 0, 0))],
               [pl.BlockSpec((bm, D), lambda i, k: (i, 0)), pl.BlockSpec((bm, D), lambda i, k: (i, k)),
                pl.BlockSpec((bm, D), lambda i, k: (i, 0))],
               [_sds((t, D)), _sds((t, nk * D)), _sds((t, D), BF16)],
               scratch=[pltpu.VMEM((bm, D), F32)])(h, g, w1s, w2)


def _mlp_bwd(dh, a, w1s, w2, l, h, g, name, bm=512):
    t = h.shape[0]
    bm = min(bm, t)
    nk = w1s.shape[0]

    def body(dh_ref, a_ref, w1_ref, w2_ref, h_ref, g_ref, o_ref, da_ref, r_ref, dg_ref, acc):
        i, k = pl.program_id(0), pl.program_id(1)

        @pl.when(k == 0)
        def _():
            acc[...] = jnp.zeros_like(acc)

        @pl.when((i == 0) & (k == 0))
        def _():
            dg_ref[...] = jnp.zeros_like(dg_ref)

        ap = jnp.maximum(a_ref[...], 0.0)
        r_ref[...] = jnp.square(ap).astype(BF16)
        dr = _nt(dh_ref[...].astype(BF16), w2_ref[...])
        da = (dr * (2.0 * ap)).astype(BF16)
        da_ref[...] = da
        acc[...] += _nt(da, w1_ref[...])

        @pl.when(k == nk - 1)
        def _():
            dx, dgr = _rms_bwd(acc[...], h_ref[...], g_ref[...])
            o_ref[...] = dh_ref[...] + dx
            dg_ref[...] += jnp.sum(dgr, axis=0, keepdims=True)

    return _pc(body, name, (t // bm, nk),
               [pl.BlockSpec((bm, D), lambda i, k: (i, 0)), pl.BlockSpec((bm, D), lambda i, k: (i, k)),
                pl.BlockSpec((None, None, D, D), lambda i, k: (k, l, 0, 0)), pl.BlockSpec((None, None, D, D), lambda i, k: (k, l, 0, 0)),
                pl.BlockSpec((bm, D), lambda i, k: (i, 0)), pl.BlockSpec((1, D), lambda i, k: (0, 0))],
               [pl.BlockSpec((bm, D), lambda i, k: (i, 0)), pl.BlockSpec((bm, D), lambda i, k: (i, k)),
                pl.BlockSpec((bm, D), lambda i, k: (i, k)), pl.BlockSpec((1, D), lambda i, k: (0, 0))],
               [_sds((t, D)), _sds((t, nk * D), BF16), _sds((t, nk * D), BF16), _sds((1, D))],
               scratch=[pltpu.VMEM((bm, D), F32)])(dh, a, w1s, w2, h, g)


def _rows_of(x):
    return lax.broadcasted_iota(jnp.int32, x.shape, 0)


def _shift_down(x, s):
    if s == 0:
        return x
    return jnp.where(_rows_of(x) >= s, pltpu.roll(x, s, 0), 0.0)


def _shift_up(x, s):
    if s == 0:
        return x
    n = x.shape[0]
    return jnp.where(_rows_of(x) < n - s, pltpu.roll(x, n - s, 0), 0.0)


def _cumsum_rows(x):
    n, s = x.shape[0], 1
    while s < n:
        x = x + _shift_down(x, s)
        s *= 2
    return x


def _rcumsum_rows(x):
    n, s = x.shape[0], 1
    while s < n:
        x = x + _shift_up(x, s)
        s *= 2
    return x


def _silu(x):
    return x * _sigmoid(x)


def _dsilu(x):
    s = _sigmoid(x)
    return s * (1.0 + x * (1.0 - s))


CONV_W = 4


def _conv_pre(u, w):
    y = _shift_down(u, CONV_W - 1) * w[0:1, :]
    for j in range(1, CONV_W):
        y = y + _shift_down(u, CONV_W - 1 - j) * w[j:j + 1, :]
    return y


def _conv_fwd(z0, cw, name):
    t = z0.shape[0]

    def body(u_ref, w_ref, o_ref):
        o_ref[...] = _silu(_conv_pre(u_ref[...], w_ref[...]))

    return _pc(body, name, (2 * 512 // HD,),
               [pl.BlockSpec((t, HD), lambda c: (0, c)), pl.BlockSpec((CONV_W, HD), lambda c: (0, c))],
               pl.BlockSpec((t, HD), lambda c: (0, c)), _sds((t, 1024)))(z0, cw)


def _conv_bwd(z0, cw, dy, name):
    t = z0.shape[0]

    def body(u_ref, w_ref, dy_ref, du_ref, dw_ref):
        u, w = u_ref[...], w_ref[...]
        dpre = dy_ref[...] * _dsilu(_conv_pre(u, w))
        du = _shift_up(dpre, CONV_W - 1) * w[0:1, :]
        for j in range(1, CONV_W):
            du = du + _shift_up(dpre, CONV_W - 1 - j) * w[j:j + 1, :]
        du_ref[...] = du.astype(BF16)
        for j in range(CONV_W):
            dw_ref[j:j + 1, :] = jnp.sum(dpre * _shift_down(u, CONV_W - 1 - j), axis=0, keepdims=True)

    return _pc(body, name, (2 * 512 // HD,),
               [pl.BlockSpec((t, HD), lambda c: (0, c)), pl.BlockSpec((CONV_W, HD), lambda c: (0, c)),
                pl.BlockSpec((t, HD), lambda c: (0, c))],
               [pl.BlockSpec((t, HD), lambda c: (0, c)), pl.BlockSpec((CONV_W, HD), lambda c: (0, c))],
               [_sds((t, 1024), BF16), _sds((CONV_W, 1024))])(z0, cw, dy)


def _mlstm_gates(gate, bias, m_in):
    L = gate.shape[0]
    r = lax.broadcasted_iota(jnp.int32, (L, L), 0)
    c = lax.broadcasted_iota(jnp.int32, (L, L), 1)
    eye, tril = r == c, c <= r
    i_col = gate[:, 0:1] + bias[:, 0:1]
    f_col = gate[:, 1:2] + bias[:, 1:2]
    logf_col = _log_sigmoid(f_col)
    logf_row = jnp.sum(jnp.where(eye, logf_col, 0.0), axis=0, keepdims=True)
    i_row = jnp.sum(jnp.where(eye, i_col, 0.0), axis=0, keepdims=True)
    b_col = jnp.sum(jnp.where(tril, logf_row, 0.0), axis=1, keepdims=True)
    b_row = jnp.sum(jnp.where(r <= c, logf_col, 0.0), axis=0, keepdims=True)
    logd = jnp.where(tril, b_col - b_row + i_row, NEG)
    inter = b_col + m_in
    m_t = jnp.maximum(inter, jnp.max(logd, axis=1, keepdims=True))
    w_t = jnp.exp(inter - m_t)
    dm = jnp.exp(logd - m_t)
    b_last = b_col[L - 1:L, :]
    log_in = b_last - b_col + i_col
    m_new = jnp.maximum(b_last + m_in, jnp.max(log_in, axis=0, keepdims=True))
    w_col = jnp.exp(log_in - m_new)
    decay = jnp.exp(b_last + m_in - m_new)
    return dict(eye=eye, r=r, c=c, f_col=f_col, m_t=m_t, w_t=w_t, dm=dm, m_new=m_new, w_col=w_col, decay=decay)


def _mlstm_fwd(qk, z0, gates, bias, name):
    t = qk.shape[0]
    nc, nh, L = t // CHUNK, 4, CHUNK
    scale = HD ** -0.5

    def body(q_ref, k_ref, v_ref, g_ref, b_ref, h_ref, cs_ref, ns_ref, ms_ref, c_s, n_s, m_s):
        @pl.when(pl.program_id(0) == 0)
        def _():
            c_s[...] = jnp.zeros_like(c_s)
            n_s[...] = jnp.zeros_like(n_s)
            m_s[...] = jnp.zeros_like(m_s)

        for hd in range(nh):
            sl = slice(hd * HD, (hd + 1) * HD)
            cm, nv, m_in = c_s[hd], n_s[hd], m_s[hd]
            cs_ref[hd] = cm
            ns_ref[hd] = nv
            ms_ref[hd] = jnp.broadcast_to(m_in, (1, HD))
            q, kh, v = q_ref[:, sl], k_ref[:, sl] * scale, v_ref[:, sl]
            G = _mlstm_gates(g_ref[hd], b_ref[hd], m_in)
            qb, kb, vb = q.astype(BF16), kh.astype(BF16), v.astype(BF16)
            sc = _nt(qb, kb) * G["dm"]
            num = _nn(sc.astype(BF16), vb) + G["w_t"] * _nn(qb, cm.astype(BF16))
            den = jnp.sum(sc, axis=1, keepdims=True) + G["w_t"] * jnp.sum(q * nv, axis=1, keepdims=True)
            h_ref[:, sl] = num / jnp.maximum(jnp.abs(den), jnp.exp(-G["m_t"]))
            wk = G["w_col"] * kh
            c_s[hd] = G["decay"] * cm + _tn(wk.astype(BF16), vb)
            n_s[hd] = G["decay"] * nv + jnp.sum(wk, axis=0, keepdims=True)
            m_s[hd] = G["m_new"]

    hspec = lambda blk: pl.BlockSpec((L, 512), lambda j: (j, blk))
    st = lambda r: pl.BlockSpec((nh, None, r, HD), lambda j: (0, j, 0, 0))
    return _pc(body, name, (nc,),
               [hspec(0), hspec(1), hspec(2), pl.BlockSpec((nh, L, 2), lambda j: (0, j, 0)),
                pl.BlockSpec((nh, 1, 2), lambda j: (0, 0, 0))],
               [hspec(0), st(HD), st(1), st(1)],
               [_sds((t, 512)), _sds((nh, nc, HD, HD)), _sds((nh, nc, 1, HD)), _sds((nh, nc, 1, HD))],
               scratch=[pltpu.VMEM((nh, HD, HD), F32), pltpu.VMEM((nh, 1, HD), F32), pltpu.VMEM((nh, 1, 1), F32)])(qk, qk, z0, gates, bias)


def _mlstm_bwd(qk, z0, gates, bias, cs, ns, ms, dh, name):
    t = qk.shape[0]
    nc, nh, L = t // CHUNK, 4, CHUNK
    scale = HD ** -0.5

    def body(q_ref, k_ref, v_ref, g_ref, b_ref, cs_ref, ns_ref, ms_ref, dh_ref, dq_ref, dk_ref, dv_ref, dg_ref, dc_s, dn_s):
        @pl.when(pl.program_id(0) == 0)
        def _():
            dc_s[...] = jnp.zeros_like(dc_s)
            dn_s[...] = jnp.zeros_like(dn_s)

        for hd in range(nh):
            one_head(hd, slice(hd * HD, (hd + 1) * HD), q_ref, k_ref, v_ref, g_ref, b_ref, cs_ref, ns_ref, ms_ref, dh_ref,
                     dq_ref, dk_ref, dv_ref, dg_ref, dc_s, dn_s)

    def one_head(hd, sl, q_ref, k_ref, v_ref, g_ref, b_ref, cs_ref, ns_ref, ms_ref, dh_ref, dq_ref, dk_ref, dv_ref, dg_ref, dc_s, dn_s):
        cm, nv, m_in = cs_ref[hd], ns_ref[hd], ms_ref[hd][:, 0:1]
        q, kh, v = q_ref[:, sl], k_ref[:, sl] * scale, v_ref[:, sl]
        G = _mlstm_gates(g_ref[hd], b_ref[hd], m_in)
        w_t, dmat, w_col, decay = G["w_t"], G["dm"], G["w_col"], G["decay"]
        qb, kb, vb, cb = q.astype(BF16), kh.astype(BF16), v.astype(BF16), cm.astype(BF16)
        s = _nt(qb, kb)
        sc = s * dmat
        scb = sc.astype(BF16)
        qc = _nn(qb, cb)
        qn = jnp.sum(q * nv, axis=1, keepdims=True)
        num = _nn(scb, vb) + w_t * qc
        den = jnp.sum(sc, axis=1, keepdims=True) + w_t * qn
        e_m = jnp.exp(-G["m_t"])
        dnm = jnp.maximum(jnp.abs(den), e_m)
        dh_ = dh_ref[:, sl]
        dnum = dh_ / dnm
        dden = jnp.where(jnp.abs(den) > e_m, -jnp.sum(dh_ * num, axis=1, keepdims=True) / (dnm * dnm) * jnp.sign(den), 0.0)
        dnumb = dnum.astype(BF16)
        dsc = _nt(dnumb, vb) + dden
        dv = _tn(scb, dnumb)
        wd = w_t * dnum
        wdb = wd.astype(BF16)
        ds = dsc * dmat
        dsb = ds.astype(BF16)
        dq = _nt(wdb, cb) + (w_t * dden) * nv + _nn(dsb, kb)
        dc_o = _tn(qb, wdb)
        dn_o = jnp.sum(q * (w_t * dden), axis=0, keepdims=True)
        dw = jnp.sum(dnum * qc, axis=1, keepdims=True) + dden * qn
        dkh = _tn(dsb, qb)
        dlogd = ds * s
        db_col = jnp.sum(dlogd, axis=1, keepdims=True) + dw * w_t
        csum = jnp.sum(dlogd, axis=0, keepdims=True)
        dcn, dnn = dc_s[hd], dn_s[hd]
        dcnb = dcn.astype(BF16)
        kdc = _nn(kb, dcnb)
        dws = jnp.sum(kdc * v, axis=1, keepdims=True) + jnp.sum(kh * dnn, axis=1, keepdims=True)
        dv = dv + w_col * kdc
        dkh = dkh + w_col * (_nt(vb, dcnb) + dnn)
        dlin = dws * w_col
        ddecay = jnp.sum(jnp.sum(dcn * cm, axis=1, keepdims=True), axis=0, keepdims=True) + jnp.sum(dnn * nv, axis=1, keepdims=True)
        dlast = ddecay * decay + jnp.sum(dlin, axis=0, keepdims=True)
        rows = lax.broadcasted_iota(jnp.int32, (L, 1), 0)
        db_col = db_col - dlin + jnp.where(rows == L - 1, dlast, 0.0)
        eye, r, c = G["eye"], G["r"], G["c"]
        di = dlin + jnp.sum(jnp.where(eye, csum, 0.0), axis=1, keepdims=True)
        db_row = jnp.sum(jnp.where(eye, db_col, 0.0), axis=0, keepdims=True) - csum
        dlogf = jnp.sum(jnp.where(c >= r, db_row, 0.0), axis=1, keepdims=True)
        dg_ref[hd, :, 0:1] = di
        dg_ref[hd, :, 1:2] = dlogf * (1.0 - _sigmoid(G["f_col"]))
        dq_ref[:, sl] = dq
        dk_ref[:, sl] = dkh * scale
        dv_ref[:, sl] = dv
        dc_s[hd] = decay * dcn + dc_o
        dn_s[hd] = decay * dnn + dn_o

    rv = lambda j: nc - 1 - j
    hspec = lambda blk: pl.BlockSpec((L, 512), lambda j: (rv(j), blk))
    st = lambda r: pl.BlockSpec((nh, None, r, HD), lambda j: (0, rv(j), 0, 0))
    gs = pl.BlockSpec((nh, L, 2), lambda j: (0, rv(j), 0))
    return _pc(body, name, (nc,),
               [hspec(0), hspec(1), hspec(2), gs, pl.BlockSpec((nh, 1, 2), lambda j: (0, 0, 0)),
                st(HD), st(1), st(1), hspec(0)],
               [hspec(0), hspec(0), hspec(0), gs],
               [_sds((t, 512)), _sds((t, 512)), _sds((t, 512)), _sds((nh, t, 2))],
               scratch=[pltpu.VMEM((nh, HD, HD), F32), pltpu.VMEM((nh, 1, HD), F32)])(qk, qk, z0, gates, bias, cs, ns, ms, dh)


def _hgrn_act(qb_, fb_, ib_, lg):
    lb = _sigmoid(lg[0:1, :] - lg[1:2, :])
    sg = _sigmoid(fb_)
    f = lb + (1.0 - lb) * sg
    return lb, sg, f, _silu(qb_), (1.0 - lb) * (1.0 - sg), _silu(ib_), _cumsum_rows(jnp.log(f))


def _hgrn_fwd(z0, lbl, name):
    t = z0.shape[0]
    nc, nh, L = t // CHUNK, 4, CHUNK

    def body(q_ref, f_ref, i_ref, l_ref, o_ref, ss_ref, st_s):
        @pl.when(pl.program_id(1) == 0)
        def _():
            st_s[...] = jnp.zeros_like(st_s)

        st = st_s[...]
        ss_ref[...] = st
        _, _, _, q, k, v, b = _hgrn_act(q_ref[...], f_ref[...], i_ref[...], l_ref[...])
        o = _nt((q * jnp.exp(b)).astype(BF16), st.astype(BF16))
        rows = _rows_of(b)
        for dl in range(L):
            e = jnp.exp(jnp.where(rows >= dl, b - pltpu.roll(b, dl, 0) if dl else b - b, NEG))
            a = jnp.sum(q * _shift_down(k, dl) * e, axis=1, keepdims=True)
            o = o + a * _shift_down(v, dl)
        o_ref[...] = o
        bl = b[L - 1:L, :]
        st_s[...] = st * jnp.exp(bl) + _tn(v.astype(BF16), (k * jnp.exp(bl - b)).astype(BF16))

    hspec = lambda off: pl.BlockSpec((L, HD), lambda h, j: (j, off + h))
    return _pc(body, name, (nh, nc),
               [hspec(16), hspec(20), hspec(24), pl.BlockSpec((2, HD), lambda h, j: (0, h))],
               [hspec(0), pl.BlockSpec((None, None, HD, HD), lambda h, j: (h, j, 0, 0))],
               [_sds((t, 512)), _sds((nh, nc, HD, HD))],
               scratch=[pltpu.VMEM((HD, HD), F32)])(z0, z0, z0, lbl)


def _hgrn_bwd(z0, lbl, ss, do, name):
    t = z0.shape[0]
    nc, nh, L = t // CHUNK, 4, CHUNK

    def body(q_ref, f_ref, i_ref, l_ref, ss_ref, do_ref, dq_ref, df_ref, di_ref, dl_ref, dst_s, dlb_s):
        j = pl.program_id(1)

        @pl.when(j == 0)
        def _():
            dst_s[...] = jnp.zeros_like(dst_s)
            dlb_s[...] = jnp.zeros_like(dlb_s)

        st = ss_ref[...]
        qp, fp, ip = q_ref[...], f_ref[...], i_ref[...]
        lb, sg, f, q, k, v, b = _hgrn_act(qp, fp, ip, l_ref[...])
        do_ = do_ref[...]
        dob, stb = do_.astype(BF16), st.astype(BF16)
        eb = jnp.exp(b)
        qe = q * eb
        dqe = _nn(dob, stb)
        dst_o = _tn(dob, qe.astype(BF16))
        dq = dqe * eb
        db = dqe * qe
        dk = jnp.zeros_like(q)
        dv = jnp.zeros_like(q)
        rows = _rows_of(b)
        for dl in range(L):
            up = (L - dl) % L
            kd, vd = _shift_down(k, dl), _shift_down(v, dl)
            e = jnp.exp(jnp.where(rows >= dl, b - pltpu.roll(b, dl, 0) if dl else b - b, NEG))
            a = jnp.sum(q * kd * e, axis=1, keepdims=True)
            p = jnp.sum(do_ * vd, axis=1, keepdims=True) * e
            dq = dq + p * kd
            dkd = p * q
            dbb = dkd * kd
            adv = a * do_
            if dl:
                dv = dv + pltpu.roll(adv, up, 0)
                dk = dk + pltpu.roll(dkd, up, 0)
                db = db + dbb - pltpu.roll(dbb, up, 0)
            else:
                dv = dv + adv
                dk = dk + dkd
        dstn = dst_s[...]
        dstnb = dstn.astype(BF16)
        bl = b[L - 1:L, :]
        ebl = jnp.exp(bl)
        kdec_e = jnp.exp(bl - b)
        kdec = k * kdec_e
        dbl = jnp.sum(dstn * st, axis=0, keepdims=True) * ebl
        dv = dv + _nt(kdec.astype(BF16), dstnb)
        dkdec = _nn(v.astype(BF16), dstnb)
        dk = dk + dkdec * kdec_e
        dx = dkdec * kdec
        dbl = dbl + jnp.sum(dx, axis=0, keepdims=True)
        db = db - dx + jnp.where(rows == L - 1, dbl, 0.0)
        dst_s[...] = dstn * ebl + dst_o
        dg = _rcumsum_rows(db)
        dfk = dg / f - dk
        dq_ref[...] = (dq * _dsilu(qp)).astype(BF16)
        di_ref[...] = (dv * _dsilu(ip)).astype(BF16)
        df_ref[...] = (dfk * (1.0 - lb) * sg * (1.0 - sg)).astype(BF16)
        dlb_s[...] += jnp.sum(dfk * (1.0 - sg), axis=0, keepdims=True)

        @pl.when(j == nc - 1)
        def _():
            dl0 = dlb_s[...] * lb * (1.0 - lb)
            dl_ref[0:1, :] = dl0
            dl_ref[1:2, :] = -dl0

    rv = lambda j: nc - 1 - j
    hspec = lambda off: pl.BlockSpec((L, HD), lambda h, j: (rv(j), off + h))
    return _pc(body, name, (nh, nc),
               [hspec(16), hspec(20), hspec(24), pl.BlockSpec((2, HD), lambda h, j: (0, h)),
                pl.BlockSpec((None, None, HD, HD), lambda h, j: (h, rv(j), 0, 0)), hspec(0)],
               [hspec(0), hspec(0), hspec(0), pl.BlockSpec((2, HD), lambda h, j: (0, h))],
               [_sds((t, 512), BF16), _sds((t, 512), BF16), _sds((t, 512), BF16), _sds((2, 512))],
               scratch=[pltpu.VMEM((HD, HD), F32), pltpu.VMEM((1, HD), F32)])(z0, z0, z0, lbl, ss, do)


def _post0_fwd(hm, hh, z0, na, nb, w, h0, name, bm=512):
    t = h0.shape[0]
    bm = min(bm, t)

    def body(hm_ref, hh_ref, oa_ref, gb_ref, na_ref, nb_ref, w_ref, h_ref, o_ref, y_ref):
        for hd in range(4):
            sl = slice(hd * HD, (hd + 1) * HD)
            pa = _sigmoid(oa_ref[:, sl]) * hm_ref[:, sl]
            y_ref[:, sl] = (pa * _rstd(pa) * na_ref[:, sl]).astype(BF16)
            xb = hh_ref[:, sl]
            y_ref[:, 512 + hd * HD:512 + (hd + 1) * HD] = (xb * _rstd(xb) * nb_ref[:, sl] * _silu(gb_ref[:, sl])).astype(BF16)
        o_ref[...] = h_ref[...] + _nn(y_ref[...], w_ref[...])

    row = lambda wd, c: pl.BlockSpec((bm, wd), lambda i: (i, c))
    vec = lambda wd: pl.BlockSpec((1, wd), lambda i: (0, 0))
    return _pc(body, name, (t // bm,),
               [row(512, 0), row(512, 0), row(512, 3), row(512, 7), vec(512), vec(512),
                pl.BlockSpec((D, D), lambda i: (0, 0)), row(D, 0)],
               [row(D, 0), row(D, 0)], [_sds((t, D)), _sds((t, D), BF16)])(hm, hh, z0, z0, na, nb, w, h0)


def _post0_bwd(dh1, w, hm, hh, z0, na, nb, name, bm=512):
    t = dh1.shape[0]
    bm = min(bm, t)

    def body(dh_ref, w_ref, hm_ref, hh_ref, oa_ref, gb_ref, na_ref, nb_ref, dhm_ref, dhh_ref, doa_ref, dgb_ref, dna_ref, dnb_ref):
        @pl.when(pl.program_id(0) == 0)
        def _():
            dna_ref[...] = jnp.zeros_like(dna_ref)
            dnb_ref[...] = jnp.zeros_like(dnb_ref)

        dy = _nt(dh_ref[...].astype(BF16), w_ref[...])
        for hd in range(4):
            sl = slice(hd * HD, (hd + 1) * HD)
            hm_, oa = hm_ref[:, sl], oa_ref[:, sl]
            sg = _sigmoid(oa)
            dpa, dgr = _rms_bwd(dy[:, sl], sg * hm_, na_ref[:, sl])
            dna_ref[:, sl] += jnp.sum(dgr, axis=0, keepdims=True)
            doa_ref[:, sl] = (dpa * hm_ * sg * (1.0 - sg)).astype(BF16)
            dhm_ref[:, sl] = dpa * sg
            xb, gb, nbv = hh_ref[:, sl], gb_ref[:, sl], nb_ref[:, sl]
            dyb = dy[:, 512 + hd * HD:512 + (hd + 1) * HD]
            dgb_ref[:, sl] = (dyb * (xb * _rstd(xb) * nbv) * _dsilu(gb)).astype(BF16)
            dxb, dgr2 = _rms_bwd(dyb * _silu(gb), xb, nbv)
            dnb_ref[:, sl] += jnp.sum(dgr2, axis=0, keepdims=True)
            dhh_ref[:, sl] = dxb

    row = lambda wd, c: pl.BlockSpec((bm, wd), lambda i: (i, c))
    vec = lambda wd: pl.BlockSpec((1, wd), lambda i: (0, 0))
    return _pc(body, name, (t // bm,),
               [row(D, 0), pl.BlockSpec((D, D), lambda i: (0, 0)), row(512, 0), row(512, 0), row(512, 3), row(512, 7),
                vec(512), vec(512)],
               [row(512, 0), row(512, 0), row(512, 0), row(512, 0), vec(512), vec(512)],
               [_sds((t, 512)), _sds((t, 512)), _sds((t, 512), BF16), _sds((t, 512), BF16), _sds((1, 512)), _sds((1, 512))],
               )(dh1, w, hm, hh, z0, z0, na, nb)


def _memkv_fwd(mem, g, wkv_s, name):
    m = mem.shape[0]

    def body(x_ref, g_ref, w_ref, kv_ref, mn_ref):
        x = x_ref[...]
        mn = (x * _rstd(x) * g_ref[...]).astype(BF16)
        mn_ref[...] = mn
        kv_ref[...] = _nn(mn, w_ref[...])

    return _pc(body, name, (4,),
               [pl.BlockSpec((m, D), lambda k: (0, 0)), pl.BlockSpec((1, D), lambda k: (0, 0)),
                pl.BlockSpec((None, D, 512), lambda k: (k, 0, 0))],
               [pl.BlockSpec((m, 512), lambda k: (0, k)), pl.BlockSpec((m, D), lambda k: (0, 0))],
               [_sds((m, 2048)), _sds((m, D), BF16)])(mem, g, wkv_s)


def _memkv_bwd(dkv, wkv_s, mem, g, name):
    m = mem.shape[0]

    def body(d_ref, w_ref, x_ref, g_ref, dg_ref, acc):
        k = pl.program_id(0)

        @pl.when(k == 0)
        def _():
            acc[...] = jnp.zeros_like(acc)

        acc[...] += _nt(d_ref[...].astype(BF16), w_ref[...])

        @pl.when(k == 3)
        def _():
            _, dgr = _rms_bwd(acc[...], x_ref[...], g_ref[...])
            dg_ref[...] = jnp.sum(dgr, axis=0, keepdims=True)

    return _pc(body, name, (4,),
               [pl.BlockSpec((m, 512), lambda k: (0, k)), pl.BlockSpec((None, D, 512), lambda k: (k, 0, 0)),
                pl.BlockSpec((m, D), lambda k: (0, 0)), pl.BlockSpec((1, D), lambda k: (0, 0))],
               pl.BlockSpec((1, D), lambda k: (0, 0)), _sds((1, D)), scratch=[pltpu.VMEM((m, D), F32)])(dkv, wkv_s, mem, g)


def _xattn_probs(qh, kh):
    s = _nt(qh, kh) * (XD ** -0.5)
    p = jnp.exp(s - jnp.max(s, axis=1, keepdims=True))
    return p / jnp.sum(p, axis=1, keepdims=True)


def _xattn_fwd(q, kv, wo, h1, name, bm=512):
    t, m = q.shape[0], kv.shape[0]
    bm = min(bm, t)

    def body(q_ref, k_ref, v_ref, w_ref, h_ref, out_ref, o_ref):
        for hd in range(D // XD):
            sl = slice(hd * XD, (hd + 1) * XD)
            p = _xattn_probs(q_ref[:, sl].astype(BF16), k_ref[:, sl].astype(BF16))
            o_ref[:, sl] = _nn(p.astype(BF16), v_ref[:, sl].astype(BF16)).astype(BF16)
        out_ref[...] = h_ref[...] + _nn(o_ref[...], w_ref[...])

    row = pl.BlockSpec((bm, D), lambda i: (i, 0))
    return _pc(body, name, (t // bm,),
               [row, pl.BlockSpec((m, D), lambda i: (0, 0)), pl.BlockSpec((m, D), lambda i: (0, 1)),
                pl.BlockSpec((D, D), lambda i: (0, 0)), row],
               [row, row], [_sds((t, D)), _sds((t, D), BF16)])(q, kv, kv, wo, h1)


def _xattn_bwd(dh2, q, kv, wo, name, bm=512):
    t, m = q.shape[0], kv.shape[0]
    bm = min(bm, t)

    def body(dh_ref, q_ref, k_ref, v_ref, w_ref, dq_ref, dkv_ref):
        @pl.when(pl.program_id(0) == 0)
        def _():
            dkv_ref[...] = jnp.zeros_like(dkv_ref)

        d_o = _nt(dh_ref[...].astype(BF16), w_ref[...])
        for hd in range(D // XD):
            sl = slice(hd * XD, (hd + 1) * XD)
            qh, kh, vh = q_ref[:, sl].astype(BF16), k_ref[:, sl].astype(BF16), v_ref[:, sl].astype(BF16)
            p = _xattn_probs(qh, kh)
            dob = d_o[:, sl].astype(BF16)
            dp = _nt(dob, vh)
            dkv_ref[:, D + hd * XD:D + (hd + 1) * XD] += _tn(p.astype(BF16), dob)
            ds = (p * (dp - jnp.sum(dp * p, axis=1, keepdims=True)) * (XD ** -0.5)).astype(BF16)
            dq_ref[:, sl] = _nn(ds, kh).astype(BF16)
            dkv_ref[:, sl] += _tn(ds, qh)

    row = pl.BlockSpec((bm, D), lambda i: (i, 0))
    return _pc(body, name, (t // bm,),
               [row, row, pl.BlockSpec((m, D), lambda i: (0, 0)), pl.BlockSpec((m, D), lambda i: (0, 1)),
                pl.BlockSpec((D, D), lambda i: (0, 0))],
               [row, pl.BlockSpec((m, 2 * D), lambda i: (0, 0))],
               [_sds((t, D), BF16), _sds((m, 2 * D))])(dh2, q, kv, kv, wo)


NH1 = 8
FOX_BM = 512
FOX_BQ = 512
FOX_BK = 512


def _foxprep_fwd(z1, qg, kg, fbp, name):
    t = z1.shape[0]
    bm = min(FOX_BM, t)

    def body(q_ref, k_ref, v_ref, f_ref, qg_ref, kg_ref, fb_ref, qn_ref, kn_ref, vb_ref, c_ref, carry):
        @pl.when(pl.program_id(0) == 0)
        def _():
            carry[...] = jnp.zeros_like(carry)

        for hd in range(NH1):
            sl = slice(hd * HD, (hd + 1) * HD)
            x = q_ref[:, sl]
            qn_ref[:, sl] = (x * _rstd(x) * qg_ref[...] * FOX_QSCALE).astype(BF16)
            x = k_ref[:, sl]
            kn_ref[:, sl] = (x * _rstd(x) * kg_ref[...]).astype(BF16)
        vb_ref[...] = v_ref[...].astype(BF16)
        c = carry[...] + _cumsum_rows(_log_sigmoid(f_ref[...] + fb_ref[...]))
        c_ref[...] = c
        carry[...] = c[bm - 1:bm, :]

    row = lambda c: pl.BlockSpec((bm, D), lambda i: (i, c))
    lane = pl.BlockSpec((bm, HD), lambda i: (i, 4 * D // HD))
    vec = pl.BlockSpec((1, HD), lambda i: (0, 0))
    return _pc(body, name, (t // bm,), [row(0), row(1), row(2), lane, vec, vec, vec],
               [row(0), row(0), row(0), pl.BlockSpec((bm, HD), lambda i: (i, 0))],
               [_sds((t, D), BF16), _sds((t, D), BF16), _sds((t, D), BF16), _sds((t, HD))],
               scratch=[pltpu.VMEM((1, HD), F32)])(z1, z1, z1, z1, qg, kg, fbp)


def _foxprep_bwd(dqn, dkn, z1, qg, kg, fbp, dc, name):
    t = z1.shape[0]
    bm = min(FOX_BM, t)
    nb = t // bm

    def body(dqn_ref, dkn_ref, q_ref, k_ref, f_ref, qg_ref, kg_ref, fb_ref, dc_ref,
             dq_ref, dk_ref, df_ref, dqg_ref, dkg_ref, dfb_ref, carry):
        @pl.when(pl.program_id(0) == 0)
        def _():
            carry[...] = jnp.zeros_like(carry)
            dqg_ref[...] = jnp.zeros_like(dqg_ref)
            dkg_ref[...] = jnp.zeros_like(dkg_ref)
            dfb_ref[...] = jnp.zeros_like(dfb_ref)

        for hd in range(NH1):
            sl = slice(hd * HD, (hd + 1) * HD)
            dx, dgr = _rms_bwd(dqn_ref[:, sl] * (HD ** -0.5), q_ref[:, sl], qg_ref[...])
            dq_ref[:, sl] = dx.astype(BF16)
            dqg_ref[...] += jnp.sum(dgr, axis=0, keepdims=True)
            dx, dgr = _rms_bwd(dkn_ref[:, sl], k_ref[:, sl], kg_ref[...])
            dk_ref[:, sl] = dx.astype(BF16)
            dkg_ref[...] += jnp.sum(dgr, axis=0, keepdims=True)
        dc_ = dc_ref[...]
        dlogf = _rcumsum_rows(dc_) + carry[...]
        carry[...] += jnp.sum(dc_, axis=0, keepdims=True)
        lanes = lax.broadcasted_iota(jnp.int32, dc_.shape, 1)
        df = jnp.where(lanes < NH1, dlogf * (1.0 - _sigmoid(f_ref[...] + fb_ref[...])), 0.0)
        df_ref[...] = df.astype(BF16)
        dfb_ref[...] += jnp.sum(df, axis=0, keepdims=True)

    rv = lambda i: nb - 1 - i
    row = lambda c: pl.BlockSpec((bm, D), lambda i: (rv(i), c))
    lane = lambda c: pl.BlockSpec((bm, HD), lambda i: (rv(i), c))
    vec = pl.BlockSpec((1, HD), lambda i: (0, 0))
    return _pc(body, name, (nb,), [row(0), row(0), row(0), row(1), lane(4 * D // HD), vec, vec, vec, lane(0)],
               [row(0), row(0), lane(0), vec, vec, vec],
               [_sds((t, D), BF16), _sds((t, D), BF16), _sds((t, HD), BF16), _sds((1, HD)), _sds((1, HD)), _sds((1, HD))],
               scratch=[pltpu.VMEM((1, HD), F32)])(dqn, dkn, z1, z1, z1, qg, kg, fbp, dc)


LOG2E = 1.4426950408889634
FOX_QSCALE = HD ** -0.5 * LOG2E


def _fox_scores(q, k, ck, i, j, bq, bk, masked):
    s = _nt(q, k) - ck
    if not masked:
        return s, None
    rows = i * bq + lax.broadcasted_iota(jnp.int32, s.shape, 0)
    cols = j * bk + lax.broadcasted_iota(jnp.int32, s.shape, 1)
    return s, cols <= rows


def _fox_block_kind(i, j, bq, bk):
    active = j * bk < (i + 1) * bq
    full = (j + 1) * bk <= i * bq + 1
    return full, active & jnp.logical_not(full)


def _fox_fwd(qn, kn, vb, crow, name):
    t = qn.shape[0]
    bq, bk = min(FOX_BQ, t), min(FOX_BK, t)
    nq, nk = t // bq, t // bk

    def body(q_ref, k_ref, v_ref, ck_ref, o_ref, lse_ref, m_s, l_s, acc):
        i, j = pl.program_id(1), pl.program_id(2)

        @pl.when(j == 0)
        def _():
            m_s[...] = jnp.full_like(m_s, NEG)
            l_s[...] = jnp.zeros_like(l_s)
            acc[...] = jnp.zeros_like(acc)

        def step(masked):
            s, ok = _fox_scores(q_ref[...], k_ref[...], ck_ref[...], i, j, bq, bk, masked)
            if masked:
                s = jnp.where(ok, s, NEG)
            m_new = jnp.maximum(m_s[...], jnp.max(s, axis=1, keepdims=True))
            alpha = jnp.exp2(m_s[...] - m_new)
            p = jnp.exp2(s - m_new)
            l_s[...] = alpha * l_s[...] + jnp.sum(p, axis=1, keepdims=True)
            acc[...] = alpha * acc[...] + _nn(p.astype(BF16), v_ref[...])
            m_s[...] = m_new

        full, part = _fox_block_kind(i, j, bq, bk)
        pl.when(full)(lambda: step(False))
        pl.when(part)(lambda: step(True))

        @pl.when(j == nk - 1)
        def _():
            o_ref[...] = acc[...] / l_s[...]
            lse_ref[...] = m_s[...] + jnp.log2(l_s[...])

    kmap = lambda h, i, j: (jnp.minimum(j, ((i + 1) * bq - 1) // bk), h)
    return _pc(body, name, (NH1, nq, nk),
               [pl.BlockSpec((bq, HD), lambda h, i, j: (i, h)), pl.BlockSpec((bk, HD), kmap), pl.BlockSpec((bk, HD), kmap),
                pl.BlockSpec((None, 1, bk), lambda h, i, j: (h, 0, jnp.minimum(j, ((i + 1) * bq - 1) // bk)))],
               [pl.BlockSpec((bq, HD), lambda h, i, j: (i, h)), pl.BlockSpec((None, bq, 1), lambda h, i, j: (h, i, 0))],
               [_sds((t, D)), _sds((NH1, t, 1))],
               scratch=[pltpu.VMEM((bq, 1), F32), pltpu.VMEM((bq, 1), F32), pltpu.VMEM((bq, HD), F32)])(qn, kn, vb, crow)


def _fox_bwd(qn, kn, vb, crow, lse, delta, do, name):
    t = qn.shape[0]
    bq, bk = min(FOX_BQ, t), min(FOX_BK, t)
    nq, nk = t // bq, t // bk

    def body(q_ref, k_ref, v_ref, ck_ref, lse_ref, dl_ref, do_ref, dq_ref, dk_ref, dv_ref, dc_ref, dcq_ref, dk_s, dv_s, dc_s):
        j, i = pl.program_id(1), pl.program_id(2)

        @pl.when(i == 0)
        def _():
            dk_s[...] = jnp.zeros_like(dk_s)
            dv_s[...] = jnp.zeros_like(dv_s)
            dc_s[...] = jnp.zeros_like(dc_s)

        @pl.when((i == 0) & (j == 0))
        def _():
            dq_ref[...] = jnp.zeros_like(dq_ref)
            dcq_ref[...] = jnp.zeros_like(dcq_ref)

        def step(masked):
            q, k = q_ref[...], k_ref[...]
            s, ok = _fox_scores(q, k, ck_ref[...], i, j, bq, bk, masked)
            if masked:
                s = jnp.where(ok, s, NEG)
            p = jnp.exp2(s - lse_ref[...])
            dob = do_ref[...]
            dv_s[...] += _tn(p.astype(BF16), dob)
            ds = p * (_nt(dob, v_ref[...]) - dl_ref[...])
            dsb = ds.astype(BF16)
            rows = pl.ds(pl.multiple_of(i * bq, bq), bq)
            dq_ref[rows, :] += _nn(dsb, k)
            dk_s[...] += _tn(dsb, q)
            dc_s[...] -= jnp.sum(ds, axis=0, keepdims=True)
            dcq_ref[rows, :] += jnp.sum(ds, axis=1, keepdims=True)

        full, part = _fox_block_kind(i, j, bq, bk)
        pl.when(full)(lambda: step(False))
        pl.when(part)(lambda: step(True))

        @pl.when(i == nq - 1)
        def _():
            dk_ref[...] = dk_s[...] * (1.0 / LOG2E)
            dv_ref[...] = dv_s[...]
            dc_ref[...] = dc_s[...]

    qi = lambda i, j: jnp.maximum(i, (j * bk) // bq)
    qmap = lambda h, j, i: (qi(i, j), h)
    c3map = lambda h, j, i: (h, qi(i, j), 0)
    return _pc(body, name, (NH1, nk, nq),
               [pl.BlockSpec((bq, HD), qmap), pl.BlockSpec((bk, HD), lambda h, j, i: (j, h)),
                pl.BlockSpec((bk, HD), lambda h, j, i: (j, h)),
                pl.BlockSpec((None, 1, bk), lambda h, j, i: (h, 0, j)), pl.BlockSpec((None, bq, 1), c3map),
                pl.BlockSpec((None, bq, 1), c3map), pl.BlockSpec((bq, HD), qmap)],
               [pl.BlockSpec((t, HD), lambda h, j, i: (0, h)), pl.BlockSpec((bk, HD), lambda h, j, i: (j, h)),
                pl.BlockSpec((bk, HD), lambda h, j, i: (j, h)), pl.BlockSpec((None, 1, bk), lambda h, j, i: (h, 0, j)),
                pl.BlockSpec((None, t, 1), lambda h, j, i: (h, 0, 0))],
               [_sds((t, D)), _sds((t, D)), _sds((t, D)), _sds((NH1, 1, t)), _sds((NH1, t, 1))],
               scratch=[pltpu.VMEM((bk, HD), F32), pltpu.VMEM((bk, HD), F32), pltpu.VMEM((1, bk), F32)],
               )(qn, kn, vb, crow, lse, delta, do)


def _post1_fwd(o, z1, w, h3, name, bm=512):
    t = o.shape[0]
    bm = min(bm, t)

    def body(o_ref, g_ref, w_ref, h_ref, out_ref, og_ref):
        og_ref[...] = (o_ref[...] * _sigmoid(g_ref[...])).astype(BF16)
        out_ref[...] = h_ref[...] + _nn(og_ref[...], w_ref[...])

    row = lambda c: pl.BlockSpec((bm, D), lambda i: (i, c))
    return _pc(body, name, (t // bm,), [row(0), row(3), pl.BlockSpec((D, D), lambda i: (0, 0)), row(0)],
               [row(0), row(0)], [_sds((t, D)), _sds((t, D), BF16)])(o, z1, w, h3)


def _post1_bwd(dh4, w, o, z1, name, bm=512):
    t = o.shape[0]
    bm = min(bm, t)

    def body(dh_ref, w_ref, o_ref, g_ref, do_ref, dg_ref, dl_ref):
        d_og = _nt(dh_ref[...].astype(BF16), w_ref[...])
        o_, sg = o_ref[...], _sigmoid(g_ref[...])
        dob = (d_og * sg).astype(BF16)
        do_ref[...] = dob
        dg_ref[...] = (d_og * o_ * sg * (1.0 - sg)).astype(BF16)
        prod = dob.astype(F32) * o_
        for hd in range(NH1):
            dl_ref[hd] = jnp.sum(prod[:, hd * HD:(hd + 1) * HD], axis=1, keepdims=True)

    row = lambda c: pl.BlockSpec((bm, D), lambda i: (i, c))
    return _pc(body, name, (t // bm,), [row(0), pl.BlockSpec((D, D), lambda i: (0, 0)), row(0), row(3)],
               [row(0), row(0), pl.BlockSpec((NH1, bm, 1), lambda i: (0, i, 0))],
               [_sds((t, D), BF16), _sds((t, D), BF16), _sds((NH1, t, 1))])(dh4, w, o, z1)


def _final(h, g, tgt, name, bm=512):
    t = h.shape[0]
    bm = min(bm, t)

    def body(h_ref, g_ref, t_ref, l_ref, dh_ref, dg_ref):
        @pl.when(pl.program_id(0) == 0)
        def _():
            l_ref[...] = jnp.zeros_like(l_ref)
            dg_ref[...] = jnp.zeros_like(dg_ref)

        x, gv = h_ref[...], g_ref[...]
        r = _rstd(x)
        xh = x * r
        e = xh * gv - t_ref[...]
        l_ref[...] += 0.5 * jnp.sum(jnp.mean(e * e, axis=1, keepdims=True), axis=0, keepdims=True)
        dy = e * (1.0 / D)
        dg_ref[...] += jnp.sum(dy * xh, axis=0, keepdims=True)
        dxh = dy * gv
        dh_ref[...] = r * (dxh - xh * jnp.mean(dxh * xh, axis=1, keepdims=True))

    row = pl.BlockSpec((bm, D), lambda i: (i, 0))
    vec = pl.BlockSpec((1, D), lambda i: (0, 0))
    return _pc(body, name, (t // bm,), [row, vec, row], [pl.BlockSpec((1, HD), lambda i: (0, 0)), row, vec],
               [_sds((1, HD)), _sds((t, D)), _sds((1, D))])(h, g, tgt)


def _adam(w, g, m, v, name):
    r, c = w.shape
    br = min(r, 256)

    def body(w_ref, g_ref, m_ref, v_ref, d_ref, mo_ref, vo_ref):
        gv = g_ref[...]
        mn = ADAM_B1 * m_ref[...] + (1.0 - ADAM_B1) * gv
        vn = ADAM_B2 * v_ref[...] + (1.0 - ADAM_B2) * jnp.square(gv)
        m_hat = mn / (1.0 - ADAM_B1 ** ADAM_STEP)
        v_hat = vn / (1.0 - ADAM_B2 ** ADAM_STEP)
        d_ref[...] = -ADAM_LR * (m_hat / (jnp.sqrt(v_hat) + ADAM_EPS) + ADAM_WD * w_ref[...])
        mo_ref[...] = mn
        vo_ref[...] = vn

    blk = pl.BlockSpec((br, c), lambda i: (i, 0))
    return _pc(body, name, (r // br,), [blk] * 4, [blk] * 3, [_sds((r, c))] * 3)(w, g, m, v)


ZW = 4224
GATE0 = 4096


def _pack_w_in0(w):
    return jnp.concatenate([w[:, :2048], w[:, 2056:], w[:, 2048:2056], jnp.zeros((w.shape[0], ZW - 4104), w.dtype)], axis=1)


def _unpack_w_in0(g):
    return jnp.concatenate([g[:, :2048], g[:, GATE0:GATE0 + 8], g[:, 2048:GATE0]], axis=1)


def _pack_w_in1(w):
    return jnp.concatenate([w, jnp.zeros((w.shape[0], ZW - 4104), w.dtype)], axis=1)


def _unpack_w_in1(g):
    return g[:, :4104]


def _local_step(x, mem, tgt, W, S):
    t = x.shape[0]
    row = lambda v: v.reshape(1, -1)
    G = {}

    kv, mn = _memkv_fwd(mem, row(S["mem_norm_g"]), W["wkv_s"], "memkv_fwd")
    z0, u0 = _norm_mm(x, S["norm_mix_g"][0:1], W["w_in0"], "in0_fwd")
    qk = _conv_fwd(z0, S["conv_w"], "conv_fwd")
    g8 = z0[:, GATE0:GATE0 + 8]
    gates3 = jnp.stack([g8[:, :4].T, g8[:, 4:].T], axis=-1)
    gb = S["gate_b"]
    bias3 = jnp.stack([gb[0, :4], gb[0, 4:]], axis=-1)[:, None, :]
    hm, cs, ns, ms = _mlstm_fwd(qk, z0, gates3, bias3, "mlstm_fwd")
    hh, ss = _hgrn_fwd(z0, S["lb_logits"], "hgrn_fwd")
    h1, y0 = _post0_fwd(hm, hh, z0, S["mlstm_norm_g"], S["hgrn_norm_g"], W["w_out0"], x, "post0_fwd")

    def xattn_mlp_fwd(h, l):
        q, ux = _norm_mm(h, S["norm_xattn_g"][l:l + 1], W["wq"][l], f"xq{l}_fwd")
        h2, ox = _xattn_fwd(q, kv, W["wo"][l], h, f"xattn{l}_fwd")
        h3, a, um = _mlp_fwd(h2, S["norm_mlp_g"][l:l + 1], W["w1s"], W["w2"], l, f"mlp{l}_fwd")
        return h3, (h, q, ux, ox, h2, a, um)

    h3, sv0 = xattn_mlp_fwd(h1, 0)
    z1, u1 = _norm_mm(h3, S["norm_mix_g"][1:2], W["w_in1"], "in1_fwd")
    fbp = jnp.pad(S["c_fgate_b"], ((0, 0), (0, HD - NH1)))
    qn, kn, vb, c = _foxprep_fwd(z1, S["c_qnorm_g"], S["c_knorm_g"], fbp, "foxprep_fwd")
    crow = (c[:, :NH1] * LOG2E).T[:, None, :]
    o1, lse = _fox_fwd(qn, kn, vb, crow, "fox_fwd")
    h4, og = _post1_fwd(o1, z1, W["w_out1"], h3, "post1_fwd")
    h6, sv1 = xattn_mlp_fwd(h4, 1)
    lossp, dh, G["final_norm_g"] = _final(h6, row(S["final_norm_g"]), tgt, "final")

    dkv = None
    dgx, dgm, dwq, dwo, dw1, dw2 = [None, None], [None, None], [None, None], [None, None], [None, None], [None, None]

    def xattn_mlp_bwd(dh, l, sv):
        nonlocal dkv
        h, q, ux, ox, h2, a, um = sv
        dh2, da, r, dgm[l] = _mlp_bwd(dh, a, W["w1s"], W["w2"], l, h2, S["norm_mlp_g"][l:l + 1], f"mlp{l}_bwd")
        dw1[l] = _mm_tn(um, da, f"mlp{l}_dw1", col_chips=NCHIP)
        dw2[l] = _mm_tn(r, dh, f"mlp{l}_dw2")
        dq, dkv_l = _xattn_bwd(dh2, q, kv, W["wo"][l], f"xattn{l}_bwd")
        dkv = dkv_l if dkv is None else dkv + dkv_l
        dwo[l] = _mm_tn(ox, dh2, f"xattn{l}_dwo")
        dwq[l] = _mm_tn(ux, dq, f"xattn{l}_dwq")
        dh1, dgx[l] = _bwd_in(dq, W["wq"][l], h, S["norm_xattn_g"][l:l + 1], dh2, f"xq{l}_bwd")
        return dh1

    dh4 = xattn_mlp_bwd(dh, 1, sv1)
    do, dgate, delta = _post1_bwd(dh4, W["w_out1"], o1, z1, "post1_bwd")
    G["w_out1"] = _mm_tn(og, dh4, "post1_dw")
    dqn, dkn, dv1, dcrow, dcq = _fox_bwd(qn, kn, vb, crow, lse, delta, do, "fox_bwd")
    dc = jnp.pad((dcrow[:, 0, :] + dcq[:, :, 0]).T, ((0, 0), (0, HD - NH1)))
    dqr, dkr, df1, G["c_qnorm_g"], G["c_knorm_g"], dfb = _foxprep_bwd(
        dqn, dkn, z1, S["c_qnorm_g"], S["c_knorm_g"], fbp, dc, "foxprep_bwd")
    G["c_fgate_b"] = dfb[:, :NH1]
    dz1 = jnp.concatenate([dqr, dkr, dv1.astype(BF16), dgate, df1], axis=1)
    G["w_in1"] = _mm_tn(u1, dz1, "in1_dw")
    dh3, dgmix1 = _bwd_in(dz1, W["w_in1"], h3, S["norm_mix_g"][1:2], dh4, "in1_bwd")
    dh1 = xattn_mlp_bwd(dh3, 0, sv0)

    dhm, dhh, doa, dgb, G["mlstm_norm_g"], G["hgrn_norm_g"] = _post0_bwd(
        dh1, W["w_out0"], hm, hh, z0, S["mlstm_norm_g"], S["hgrn_norm_g"], "post0_bwd")
    G["w_out0"] = _mm_tn(y0, dh1, "post0_dw")
    dqa, dka, dva, dgates3 = _mlstm_bwd(qk, z0, gates3, bias3, cs, ns, ms, dhm, "mlstm_bwd")
    dqb, dfb0, dib, G["lb_logits"] = _hgrn_bwd(z0, S["lb_logits"], ss, dhh, "hgrn_bwd")
    duc, G["conv_w"] = _conv_bwd(z0, S["conv_w"], jnp.concatenate([dqa, dka], axis=1), "conv_bwd")
    dg8 = jnp.concatenate([dgates3[:, :, 0].T, dgates3[:, :, 1].T], axis=1)
    G["gate_b"] = jnp.sum(dg8, axis=0, keepdims=True)
    dz0 = jnp.concatenate([duc, dva.astype(BF16), doa, dqb, dfb0, dib, dgb,
                           jnp.pad(dg8, ((0, 0), (0, HD - 8))).astype(BF16)], axis=1)
    G["w_in0"] = _mm_tn(u0, dz0, "in0_dw")
    dx, dgmix0 = _bwd_in(dz0, W["w_in0"], x, S["norm_mix_g"][0:1], dh1, "in0_bwd")

    G["wkv"] = _mm_tn(mn, dkv, "memkv_dw", col_chips=NCHIP)
    G["mem_norm_g"] = _memkv_bwd(dkv, W["wkv_s"], mem, row(S["mem_norm_g"]), "memkv_bwd")
    G["norm_mix_g"] = jnp.concatenate([dgmix0, dgmix1], axis=0)
    G["norm_xattn_g"] = jnp.concatenate(dgx, axis=0)
    G["norm_mlp_g"] = jnp.concatenate(dgm, axis=0)
    G["wq"], G["wo"], G["w1"], G["w2"] = dwq, dwo, dw1, dw2
    return lossp[0, 0], dx, G


ANY = pl.BlockSpec(memory_space=pl.ANY)
NCHIP = 4
RS_ROWS = 4224
RS_TILE = 384


def _place():
    x, y, c = lax.axis_index("x"), lax.axis_index("y"), lax.axis_index("c")
    return x, y, c, [(1 - x, y), (x, 1 - y), (1 - x, 1 - y)]


def _comm_call(body, name, ins, out_shapes, sems):
    return pl.pallas_call(body, name=name, in_specs=[ANY] * len(ins), out_specs=[ANY] * len(out_shapes),
                          out_shape=out_shapes, scratch_shapes=sems)(*ins)


def _gather_weights(arrs, name):
    n = len(arrs)

    def body(*refs):
        ins, outs = refs[:n], refs[n:2 * n]
        send_i, recv_i, send_d, recv_d = refs[2 * n:]
        x, y, c, chips = _place()
        me = 2 * x + y

        def half(a, cc):
            h = arrs[a].shape[0] // 2
            return pl.ds(pl.multiple_of(cc * h, h), h)

        def ici(a, k, src_chip, dst_dev):
            return pltpu.make_async_remote_copy(
                src_ref=ins[a].at[half(a, c)], dst_ref=outs[a].at[src_chip, half(a, c)], send_sem=send_i.at[a, k],
                recv_sem=recv_i.at[a, k], device_id=dst_dev, device_id_type=MESH)

        def d2d(a, k, src_chip, cc):
            reg = outs[a].at[src_chip, half(a, cc)]
            return pltpu.make_async_remote_copy(src_ref=reg, dst_ref=reg, send_sem=send_d.at[a, k], recv_sem=recv_d.at[a, k],
                                                device_id=(x, y, 1 - c), device_id_type=MESH)

        for a in range(n):
            for k, (px, py) in enumerate(chips):
                ici(a, k, me, (px, py, c)).start()
        for k, (px, py) in enumerate(chips):
            for a in range(n):
                ici(a, k, 2 * px + py, (px, py, c)).wait_recv()
                d2d(a, k, 2 * px + py, c).start()
        for k, (px, py) in enumerate(chips):
            for a in range(n):
                ici(a, k, me, (px, py, c)).wait_send()
                d2d(a, k, 2 * px + py, c).wait_send()
                d2d(a, k, 2 * px + py, 1 - c).wait_recv()

    sem = lambda: pltpu.SemaphoreType.DMA((n, 3))
    return _comm_call(body, name, arrs, [_sds((NCHIP,) + a.shape, a.dtype) for a in arrs], [sem(), sem(), sem(), sem()])


def _pair_exchange(arrs, name):
    n = len(arrs)

    def body(*refs):
        ins, outs = refs[:n], refs[n:2 * n]
        send, recv = refs[2 * n:]
        x, y, c, _ = _place()
        copies = []
        for a in range(n):
            h = arrs[a].shape[1] // 2
            cp = pltpu.make_async_remote_copy(src_ref=ins[a].at[:, pl.ds(pl.multiple_of((1 - c) * h, h), h)], dst_ref=outs[a],
                                              send_sem=send.at[a], recv_sem=recv.at[a], device_id=(x, y, 1 - c), device_id_type=MESH)
            cp.start()
            copies.append(cp)
        for cp in copies:
            cp.wait()

    return _comm_call(body, name, arrs, [_sds((a.shape[0], a.shape[1] // 2, a.shape[2]), a.dtype) for a in arrs],
                      [pltpu.SemaphoreType.DMA((n,)), pltpu.SemaphoreType.DMA((n,))])


def _chip_exchange(arrs, name):
    n = len(arrs)

    def body(*refs):
        ins, outs = refs[:n], refs[n:2 * n]
        send, recv = refs[2 * n:]
        x, y, c, chips = _place()
        me = 2 * x + y
        copies = []
        for a in range(n):
            for k, (px, py) in enumerate(chips):
                r = pltpu.make_async_remote_copy(src_ref=ins[a].at[2 * px + py], dst_ref=outs[a].at[me], send_sem=send.at[a, k],
                                                 recv_sem=recv.at[a, k], device_id=(px, py, c), device_id_type=MESH)
                r.start()
                copies.append(r)
        for cp in copies:
            cp.wait()

    return _comm_call(body, name, arrs, [_sds(a.shape, a.dtype) for a in arrs],
                      [pltpu.SemaphoreType.DMA((n, 3)), pltpu.SemaphoreType.DMA((n, 3))])


def _pair_swap(arrs, name):
    n = len(arrs)

    def body(*refs):
        ins, outs = refs[:n], refs[n:2 * n]
        send, recv = refs[2 * n:]
        x, y, c, _ = _place()
        copies = []
        for a in range(n):
            cp = pltpu.make_async_remote_copy(src_ref=ins[a], dst_ref=outs[a], send_sem=send.at[a], recv_sem=recv.at[a],
                                              device_id=(x, y, 1 - c), device_id_type=MESH)
            cp.start()
            copies.append(cp)
        for cp in copies:
            cp.wait()

    return _comm_call(body, name, arrs, [_sds(a.shape, a.dtype) for a in arrs],
                      [pltpu.SemaphoreType.DMA((n,)), pltpu.SemaphoreType.DMA((n,))])


def _all_gather_devices(v, name):
    def body(v_ref, o_ref, send, recv, loc):
        x, y, c, _ = _place()
        me = 4 * x + 2 * y + c
        own = pltpu.make_async_copy(v_ref, o_ref.at[me], loc)
        own.start()
        copies = [own]
        for k in range(1, 8):
            fx, fy, fc = (k >> 2) & 1, (k >> 1) & 1, k & 1
            peer = (x ^ fx, y ^ fy, c ^ fc)
            r = pltpu.make_async_remote_copy(src_ref=v_ref, dst_ref=o_ref.at[me], send_sem=send.at[k - 1],
                                             recv_sem=recv.at[k - 1], device_id=peer, device_id_type=MESH)
            r.start()
            copies.append(r)
        for cp in copies:
            cp.wait()

    return _comm_call(body, name, [v], [_sds((8,) + v.shape, v.dtype)],
                      [pltpu.SemaphoreType.DMA((7,)), pltpu.SemaphoreType.DMA((7,)), pltpu.SemaphoreType.DMA])[0]


def _row_tile(r):
    return next((b for b in (512, 384, 256, 128, 64, 32, 16) if r % b == 0), r)


def _add2(a, b, out_dtype, name):
    r, w = a.shape
    br = _row_tile(r)

    def body(a_ref, b_ref, o_ref):
        o_ref[...] = (a_ref[...].astype(F32) + b_ref[...].astype(F32)).astype(out_dtype)

    blk = pl.BlockSpec((br, w), lambda i: (i, 0))
    return _pc(body, name, (r // br,), [blk, blk], blk, _sds((r, w), out_dtype))(a, b)


def _sum_slots(a, out_dtype, name, extra=None):
    n, r, w = a.shape
    br = _row_tile(r)

    def body(*refs):
        a_ref, o_ref = refs[0], refs[-1]
        acc = a_ref[0].astype(F32)
        for s in range(1, n):
            acc = acc + a_ref[s].astype(F32)
        if extra is not None:
            acc = acc + refs[1][...].astype(F32)
        o_ref[...] = acc.astype(out_dtype)

    ins = [a] + ([extra] if extra is not None else [])
    specs = [pl.BlockSpec((n, br, w), lambda i: (0, i, 0))] + ([pl.BlockSpec((br, w), lambda i: (i, 0))] if extra is not None else [])
    return _pc(body, name, (r // br,), specs, pl.BlockSpec((br, w), lambda i: (i, 0)), _sds((r, w), out_dtype))(*ins)


SMALL = ["norm_mix_g", "norm_xattn_g", "norm_mlp_g", "final_norm_g", "mem_norm_g", "hgrn_lb_logits", "mlstm_norm_g",
         "hgrn_norm_g", "c_qnorm_g", "c_knorm_g", "ab_gate_b", "c_fgate_b"]
SMALL_ROWS = 16


def _pack_small(parts):
    flat = jnp.concatenate([p.reshape(-1).astype(F32) for p in parts])
    return jnp.pad(flat, (0, SMALL_ROWS * D - flat.shape[0])).reshape(SMALL_ROWS, D)


def _unpack_small(buf, shapes):
    flat, out, off = buf.reshape(-1), [], 0
    for s in shapes:
        n = 1
        for d in s:
            n *= d
        out.append(flat[off:off + n].reshape(s))
        off += n
    return out


def kernel(x, mem, norm_mix_g, norm_xattn_g, norm_mlp_g, final_norm_g, ab_w_in, ab_conv_w, ab_gate_b, hgrn_lb_logits, mlstm_norm_g, hgrn_norm_g, ab_w_out, c_w_in, c_fgate_b, c_qnorm_g, c_knorm_g, c_w_out, mem_norm_g, mem_w_kv, xa_w_q, xa_w_o, mlp_w1, mlp_w2, loss_target, m_norm_mix_g, m_norm_xattn_g, m_norm_mlp_g, m_final_norm_g, m_ab_w_in, m_ab_conv_w, m_ab_gate_b, m_hgrn_lb_logits, m_mlstm_norm_g, m_hgrn_norm_g, m_ab_w_out, m_c_w_in, m_c_fgate_b, m_c_qnorm_g, m_c_knorm_g, m_c_w_out, m_mem_norm_g, m_mem_w_kv, m_xa_w_q, m_xa_w_o, m_mlp_w1, m_mlp_w2, v_norm_mix_g, v_norm_xattn_g, v_norm_mlp_g, v_final_norm_g, v_ab_w_in, v_ab_conv_w, v_ab_gate_b, v_hgrn_lb_logits, v_mlstm_norm_g, v_hgrn_norm_g, v_ab_w_out, v_c_w_in, v_c_fgate_b, v_c_qnorm_g, v_c_knorm_g, v_c_w_out, v_mem_norm_g, v_mem_w_kv, v_xa_w_q, v_xa_w_o, v_mlp_w1, v_mlp_w2):
    A = dict(locals())
    chip = 2 * lax.axis_index("x") + lax.axis_index("y")

    big = ["ab_w_in", "c_w_in", "ab_w_out", "c_w_out", "mem_w_kv", "xa_w_q", "xa_w_o", "mlp_w1", "mlp_w2"]
    shard2d = {"ab_w_in": (D, 1026), "c_w_in": (D, 1026), "ab_w_out": (256, D), "c_w_out": (256, D), "mem_w_kv": (D, 512),
               "xa_w_q": (512, D), "xa_w_o": (512, D), "mlp_w1": (2 * D, D), "mlp_w2": (2 * D, D)}
    own = [A[n].reshape(shard2d[n]).astype(BF16) for n in big] + [jnp.pad(ab_conv_w[0], ((0, 16 - CONV_W), (0, 0)))]
    gathered = _gather_weights(own, "gather_weights")
    gathered = [lax.dynamic_update_index_in_dim(g, o, chip, 0) for g, o in zip(gathered, own)]
    gathered[-1] = gathered[-1][:, :CONV_W]
    gw = dict(zip(big, gathered[:-1]))
    cols = lambda g: jnp.concatenate([g[k] for k in range(NCHIP)], axis=1)
    per_layer = lambda g: g.reshape(NCHIP, 2, -1, D).transpose(1, 0, 2, 3)
    W = dict(
        w_in0=_pack_w_in0(cols(gw["ab_w_in"])), w_in1=_pack_w_in1(cols(gw["c_w_in"])),
        w_out0=gw["ab_w_out"].reshape(D, D), w_out1=gw["c_w_out"].reshape(D, D), wkv_s=gw["mem_w_kv"],
        wq=per_layer(gw["xa_w_q"]).reshape(2, D, D), wo=per_layer(gw["xa_w_o"]).reshape(2, D, D),
        w1s=gw["mlp_w1"].reshape(NCHIP, 2, D, D), w2=gw["mlp_w2"].reshape(NCHIP, 2, D, D))
    S = dict(norm_mix_g=norm_mix_g, norm_xattn_g=norm_xattn_g, norm_mlp_g=norm_mlp_g, final_norm_g=final_norm_g,
             conv_w=cols(gathered[-1]), gate_b=ab_gate_b, lb_logits=hgrn_lb_logits, mlstm_norm_g=mlstm_norm_g,
             hgrn_norm_g=hgrn_norm_g, c_fgate_b=c_fgate_b, c_qnorm_g=c_qnorm_g, c_knorm_g=c_knorm_g, mem_norm_g=mem_norm_g)

    lossp, dx, G = _local_step(x[0], mem[0], loss_target[0], W, S)

    gsmall = {"norm_mix_g": G["norm_mix_g"], "norm_xattn_g": G["norm_xattn_g"], "norm_mlp_g": G["norm_mlp_g"],
              "final_norm_g": G["final_norm_g"], "mem_norm_g": G["mem_norm_g"], "hgrn_lb_logits": G["lb_logits"],
              "mlstm_norm_g": G["mlstm_norm_g"], "hgrn_norm_g": G["hgrn_norm_g"], "c_qnorm_g": G["c_qnorm_g"],
              "c_knorm_g": G["c_knorm_g"], "ab_gate_b": G["gate_b"], "c_fgate_b": G["c_fgate_b"]}
    packed = _pack_small([gsmall[n] for n in SMALL] + [G["conv_w"], lossp])
    red = _sum_slots(_all_gather_devices(packed, "gather_small"), F32, "sum_small")
    small_shapes = [A[n].shape for n in SMALL]
    *gs, gconv, loss = _unpack_small(red, small_shapes + [(CONV_W, D), ()])
    gs = dict(zip(SMALL, gs))
    gconv = lax.dynamic_slice_in_dim(gconv, chip * 256, 256, axis=1)[None]

    def stack_cols(g):
        return jnp.stack([g[:, 1026 * k:1026 * (k + 1)] for k in range(NCHIP)])

    by_rows = lambda g: g.reshape(NCHIP, -1, D)
    main = ["ab_w_out", "c_w_out", "xa_w_q", "xa_w_o", "mlp_w1", "mlp_w2"]
    flat = jnp.concatenate([by_rows(G["w_out0"]), by_rows(G["w_out1"]), by_rows(G["wq"][0]), by_rows(G["wq"][1]),
                            by_rows(G["wo"][0]), by_rows(G["wo"][1]), G["w1"][0], G["w1"][1], by_rows(G["w2"][0]),
                            by_rows(G["w2"][1])], axis=1)
    send = [flat, stack_cols(_unpack_w_in0(G["w_in0"])), stack_cols(_unpack_w_in1(G["w_in1"])), G["wkv"]]
    core = lax.axis_index("c")
    theirs = _pair_exchange(send, "pair_exchange")
    psums = []
    for i, (a, th) in enumerate(zip(send, theirs)):
        h = a.shape[1] // 2
        mine = lax.dynamic_slice_in_dim(a, core * h, h, axis=1)
        psums.append(_add2(mine.reshape(-1, a.shape[2]), th.reshape(-1, a.shape[2]), BF16, f"pair_sum{i}").reshape(th.shape))
    from_chips = _chip_exchange(psums, "chip_exchange")
    rhalf = []
    for i, (f, p) in enumerate(zip(from_chips, psums)):
        f = lax.dynamic_update_index_in_dim(f, lax.dynamic_index_in_dim(p, chip, 0, keepdims=False), chip, 0)
        rhalf.append(_sum_slots(f, F32, f"chip_sum{i}"))
    other = _pair_swap(rhalf, "pair_swap")
    rfull = [jnp.where(core == 0, jnp.concatenate([m_, o_], axis=0), jnp.concatenate([o_, m_], axis=0))
             for m_, o_ in zip(rhalf, other)]
    gbig, off = {"ab_w_in": rfull[1], "c_w_in": rfull[2], "mem_w_kv": rfull[3]}, 0
    rmain = rfull[0].reshape(-1)
    for n in main:
        r, c = shard2d[n]
        gbig[n] = rmain[off:off + r * c].reshape(r, c)
        off += r * c

    out_g, out_d, out_m, out_v = {}, {}, {}, {}
    for n in big:
        d_, m_, v_ = _adam(A[n].reshape(shard2d[n]), gbig[n], A["m_" + n].reshape(shard2d[n]), A["v_" + n].reshape(shard2d[n]), "adam_" + n)
        out_g[n] = gbig[n].reshape(A[n].shape)
        out_d[n], out_m[n], out_v[n] = d_.reshape(A[n].shape), m_.reshape(A[n].shape), v_.reshape(A[n].shape)
    sd, sm, sv = _adam(_pack_small([A[n] for n in SMALL]), _pack_small([gs[n] for n in SMALL]),
                       _pack_small([A["m_" + n] for n in SMALL]), _pack_small([A["v_" + n] for n in SMALL]), "adam_small")
    for n, d_, m_, v_ in zip(SMALL, _unpack_small(sd, small_shapes), _unpack_small(sm, small_shapes), _unpack_small(sv, small_shapes)):
        out_g[n], out_d[n], out_m[n], out_v[n] = gs[n], d_, m_, v_
    cd, cm_, cv = _adam(ab_conv_w[0], gconv[0], m_ab_conv_w[0], v_ab_conv_w[0], "adam_conv")
    out_g["ab_conv_w"], out_d["ab_conv_w"], out_m["ab_conv_w"], out_v["ab_conv_w"] = gconv, cd[None], cm_[None], cv[None]

    order = ["norm_mix_g", "norm_xattn_g", "norm_mlp_g", "final_norm_g", "ab_w_in", "ab_conv_w", "ab_gate_b", "hgrn_lb_logits",
             "mlstm_norm_g", "hgrn_norm_g", "ab_w_out", "c_w_in", "c_fgate_b", "c_qnorm_g", "c_knorm_g", "c_w_out", "mem_norm_g",
             "mem_w_kv", "xa_w_q", "xa_w_o", "mlp_w1", "mlp_w2"]
    return (loss, dx[None], *[out_g[n] for n in order], *[out_d[n] for n in order], *[out_m[n] for n in order],
            *[out_v[n] for n in order])
```

```python
import functools

import jax
import jax.numpy as jnp
from jax import lax
from jax.experimental import pallas as pl
from jax.experimental.pallas import tpu as pltpu

F32 = jnp.float32
BF16 = jnp.bfloat16
EPS = 1e-6
D = 1024
CHUNK = 64
HD = 128
XD = 256
NEG = -1e30
VMEM_LIMIT_V7X = 56 * 1024 * 1024
ADAM_LR, ADAM_B1, ADAM_B2, ADAM_EPS, ADAM_WD, ADAM_STEP = 0.001, 0.9, 0.999, 1e-08, 0.01, 10
MESH = pl.DeviceIdType.MESH


def _pc(body, name, grid, in_specs, out_specs, out_shape, scratch=(), **kw):
    return pl.pallas_call(
        body, name=name, grid=grid, in_specs=in_specs, out_specs=out_specs, out_shape=out_shape,
        scratch_shapes=scratch,
        compiler_params=pltpu.CompilerParams(
            dimension_semantics=("arbitrary",) * len(grid), vmem_limit_bytes=VMEM_LIMIT_V7X), **kw)


def _sds(shape, dtype=F32):
    return jax.ShapeDtypeStruct(shape, dtype)


def _blk(n, target):
    return max(b for b in range(128, max(target, 128) + 1, 128) if n % b == 0)


def _dot(a, b, dims):
    return lax.dot_general(a, b, (dims, ((), ())), preferred_element_type=F32)


def _nn(a, b):
    return _dot(a, b, ((1,), (0,)))


def _nt(a, b):
    return _dot(a, b, ((1,), (1,)))


def _tn(a, b):
    return _dot(a, b, ((0,), (0,)))


def _sigmoid(x):
    return 1.0 / (1.0 + jnp.exp(-x))


def _log_sigmoid(x):
    return jnp.minimum(x, 0.0) - jnp.log(1.0 + jnp.exp(-jnp.abs(x)))


def _rstd(x):
    return lax.rsqrt(jnp.mean(x * x, axis=-1, keepdims=True) + EPS)


def _rms_bwd(du, x, g):
    r = _rstd(x)
    xh = x * r
    dxh = du * g
    dx = r * (dxh - xh * jnp.mean(dxh * xh, axis=-1, keepdims=True))
    return dx, du * xh


def _norm_mm(h, g, w, name, bm=512, bn=512):
    t, n = h.shape[0], w.shape[1]
    bm, bn = min(bm, t), _blk(n, 3 * bn)

    def body(h_ref, g_ref, w_ref, z_ref, u_ref):
        @pl.when(pl.program_id(1) == 0)
        def _():
            x = h_ref[...]
            u_ref[...] = (x * _rstd(x) * g_ref[...]).astype(BF16)
        z_ref[...] = _nn(u_ref[...], w_ref[...])

    return _pc(body, name, (t // bm, n // bn),
               [pl.BlockSpec((bm, D), lambda i, j: (i, 0)), pl.BlockSpec((1, D), lambda i, j: (0, 0)),
                pl.BlockSpec((D, bn), lambda i, j: (0, j))],
               [pl.BlockSpec((bm, bn), lambda i, j: (i, j)), pl.BlockSpec((bm, D), lambda i, j: (i, 0))],
               [_sds((t, n)), _sds((t, D), BF16)])(h, g, w)


def _mm_tn(a, b, name, bm=1024, bn=1024, bt=512, col_chips=None):
    t, m = a.shape
    n = b.shape[1]
    bm, bn, bt = _blk(m, bm), (n // col_chips if col_chips else _blk(n, bn + bn // 2)), min(bt, t)
    nt = t // bt

    def body(a_ref, b_ref, o_ref, acc):
        k = pl.program_id(2)

        @pl.when(k == 0)
        def _():
            acc[...] = jnp.zeros_like(acc)

        acc[...] += _tn(a_ref[...].astype(BF16), b_ref[...].astype(BF16))

        @pl.when(k == nt - 1)
        def _():
            o_ref[...] = acc[...].astype(BF16)

    if col_chips:
        out_spec, out_shape = pl.BlockSpec((None, bm, bn), lambda i, j, k: (j, i, 0)), _sds((col_chips, m, bn), BF16)
    else:
        out_spec, out_shape = pl.BlockSpec((bm, bn), lambda i, j, k: (i, j)), _sds((m, n), BF16)
    return _pc(body, name, (m // bm, n // bn, nt),
               [pl.BlockSpec((bt, bm), lambda i, j, k: (k, i)), pl.BlockSpec((bt, bn), lambda i, j, k: (k, j))],
               out_spec, out_shape, scratch=[pltpu.VMEM((bm, bn), F32)])(a, b)


def _bwd_in(dz, w, h, g, dh, name, bm=512, bk=1024):
    t, n = dz.shape
    bm, bk = min(bm, t), _blk(n, bk + bk // 2)
    nk = n // bk

    def body(dz_ref, w_ref, h_ref, g_ref, dh_ref, o_ref, dg_ref, acc):
        i, k = pl.program_id(0), pl.program_id(1)

        @pl.when(k == 0)
        def _():
            acc[...] = jnp.zeros_like(acc)

        @pl.when((i == 0) & (k == 0))
        def _():
            dg_ref[...] = jnp.zeros_like(dg_ref)

        acc[...] += _nt(dz_ref[...], w_ref[...])

        @pl.when(k == nk - 1)
        def _():
            dx, dgr = _rms_bwd(acc[...], h_ref[...], g_ref[...])
            o_ref[...] = dh_ref[...] + dx
            dg_ref[...] += jnp.sum(dgr, axis=0, keepdims=True)

    return _pc(body, name, (t // bm, nk),
               [pl.BlockSpec((bm, bk), lambda i, k: (i, k)), pl.BlockSpec((D, bk), lambda i, k: (0, k)),
                pl.BlockSpec((bm, D), lambda i, k: (i, 0)), pl.BlockSpec((1, D), lambda i, k: (0, 0)),
                pl.BlockSpec((bm, D), lambda i, k: (i, 0))],
               [pl.BlockSpec((bm, D), lambda i, k: (i, 0)), pl.BlockSpec((1, D), lambda i, k: (0, 0))],
               [_sds((t, D)), _sds((1, D))], scratch=[pltpu.VMEM((bm, D), F32)])(dz, w, h, g, dh)


def _mlp_fwd(h, g, w1s, w2, l, name, bm=512):
    t = h.shape[0]
    bm = min(bm, t)
    nk = w1s.shape[0]

    def body(h_ref, g_ref, w1_ref, w2_ref, o_ref, a_ref, u_ref, acc):
        k = pl.program_id(1)

        @pl.when(k == 0)
        def _():
            x = h_ref[...]
            u_ref[...] = (x * _rstd(x) * g_ref[...]).astype(BF16)
            acc[...] = jnp.zeros_like(acc)

        a = _nn(u_ref[...], w1_ref[...])
        a_ref[...] = a
        r = jnp.square(jnp.maximum(a, 0.0)).astype(BF16)
        acc[...] += _nn(r, w2_ref[...])

        @pl.when(k == nk - 1)
        def _():
            o_ref[...] = h_ref[...] + acc[...]

    return _pc(body, name, (t // bm, nk),
               [pl.BlockSpec((bm, D), lambda i, k: (i, 0)), pl.BlockSpec((1, D), lambda i, k: (0, 0)),
                pl.BlockSpec((None, None, D, D), lambda i, k: (k, l, 0, 0)), pl.BlockSpec((None, None, D, D), lambda i, k: (k, l, 0, 0))],
               [pl.BlockSpec((bm, D), lambda i, k: (i, 0)), pl.BlockSpec((bm, D), lambda i, k: (i, k)),
                pl.BlockSpec((bm, D), lambda i, k: (i, 0))],
               [_sds((t, D)), _sds((t, nk * D)), _sds((t, D), BF16)],
               scratch=[pltpu.VMEM((bm, D), F32)])(h, g, w1s, w2)


def _mlp_bwd(dh, a, w1s, w2, l, h, g, name, bm=512):
    t = h.shape[0]
    bm = min(bm, t)
    nk = w1s.shape[0]

    def body(dh_ref, a_ref, w1_ref, w2_ref, h_ref, g_ref, o_ref, da_ref, r_ref, dg_ref, acc):
        i, k = pl.program_id(0), pl.program_id(1)

        @pl.when(k == 0)
        def _():
            acc[...] = jnp.zeros_like(acc)

        @pl.when((i == 0) & (k == 0))
        def _():
            dg_ref[...] = jnp.zeros_like(dg_ref)

        ap = jnp.maximum(a_ref[...], 0.0)
        r_ref[...] = jnp.square(ap).astype(BF16)
        dr = _nt(dh_ref[...].astype(BF16), w2_ref[...])
        da = (dr * (2.0 * ap)).astype(BF16)
        da_ref[...] = da
        acc[...] += _nt(da, w1_ref[...])

        @pl.when(k == nk - 1)
        def _():
            dx, dgr = _rms_bwd(acc[...], h_ref[...], g_ref[...])
            o_ref[...] = dh_ref[...] + dx
            dg_ref[...] += jnp.sum(dgr, axis=0, keepdims=True)

    return _pc(body, name, (t // bm, nk),
               [pl.BlockSpec((bm, D), lambda i, k: (i, 0)), pl.BlockSpec((bm, D), lambda i, k: (i, k)),
                pl.BlockSpec((None, None, D, D), lambda i, k: (k, l, 0, 0)), pl.BlockSpec((None, None, D, D), lambda i, k: (k, l, 0, 0)),
                pl.BlockSpec((bm, D), lambda i, k: (i, 0)), pl.BlockSpec((1, D), lambda i, k: (0, 0))],
               [pl.BlockSpec((bm, D), lambda i, k: (i, 0)), pl.BlockSpec((bm, D), lambda i, k: (i, k)),
                pl.BlockSpec((bm, D), lambda i, k: (i, k)), pl.BlockSpec((1, D), lambda i, k: (0, 0))],
               [_sds((t, D)), _sds((t, nk * D), BF16), _sds((t, nk * D), BF16), _sds((1, D))],
               scratch=[pltpu.VMEM((bm, D), F32)])(dh, a, w1s, w2, h, g)


def _rows_of(x):
    return lax.broadcasted_iota(jnp.int32, x.shape, 0)


def _shift_down(x, s):
    if s == 0:
        return x
    return jnp.where(_rows_of(x) >= s, pltpu.roll(x, s, 0), 0.0)


def _shift_up(x, s):
    if s == 0:
        return x
    n = x.shape[0]
    return jnp.where(_rows_of(x) < n - s, pltpu.roll(x, n - s, 0), 0.0)


def _cumsum_rows(x):
    n, s = x.shape[0], 1
    while s < n:
        x = x + _shift_down(x, s)
        s *= 2
    return x


def _rcumsum_rows(x):
    n, s = x.shape[0], 1
    while s < n:
        x = x + _shift_up(x, s)
        s *= 2
    return x


def _silu(x):
    return x * _sigmoid(x)


def _dsilu(x):
    s = _sigmoid(x)
    return s * (1.0 + x * (1.0 - s))


CONV_W = 4


def _conv_pre(u, w):
    y = _shift_down(u, CONV_W - 1) * w[0:1, :]
    for j in range(1, CONV_W):
        y = y + _shift_down(u, CONV_W - 1 - j) * w[j:j + 1, :]
    return y


def _conv_fwd(z0, cw, name):
    t = z0.shape[0]

    def body(u_ref, w_ref, o_ref):
        o_ref[...] = _silu(_conv_pre(u_ref[...], w_ref[...]))

    return _pc(body, name, (2 * 512 // HD,),
               [pl.BlockSpec((t, HD), lambda c: (0, c)), pl.BlockSpec((CONV_W, HD), lambda c: (0, c))],
               pl.BlockSpec((t, HD), lambda c: (0, c)), _sds((t, 1024)))(z0, cw)


def _conv_bwd(z0, cw, dy, name):
    t = z0.shape[0]

    def body(u_ref, w_ref, dy_ref, du_ref, dw_ref):
        u, w = u_ref[...], w_ref[...]
        dpre = dy_ref[...] * _dsilu(_conv_pre(u, w))
        du = _shift_up(dpre, CONV_W - 1) * w[0:1, :]
        for j in range(1, CONV_W):
            du = du + _shift_up(dpre, CONV_W - 1 - j) * w[j:j + 1, :]
        du_ref[...] = du.astype(BF16)
        for j in range(CONV_W):
            dw_ref[j:j + 1, :] = jnp.sum(dpre * _shift_down(u, CONV_W - 1 - j), axis=0, keepdims=True)

    return _pc(body, name, (2 * 512 // HD,),
               [pl.BlockSpec((t, HD), lambda c: (0, c)), pl.BlockSpec((CONV_W, HD), lambda c: (0, c)),
                pl.BlockSpec((t, HD), lambda c: (0, c))],
               [pl.BlockSpec((t, HD), lambda c: (0, c)), pl.BlockSpec((CONV_W, HD), lambda c: (0, c))],
               [_sds((t, 1024), BF16), _sds((CONV_W, 1024))])(z0, cw, dy)


def _mlstm_gates(gate, bias, m_in):
    L = gate.shape[0]
    r = lax.broadcasted_iota(jnp.int32, (L, L), 0)
    c = lax.broadcasted_iota(jnp.int32, (L, L), 1)
    eye, tril = r == c, c <= r
    i_col = gate[:, 0:1] + bias[:, 0:1]
    f_col = gate[:, 1:2] + bias[:, 1:2]
    logf_col = _log_sigmoid(f_col)
    logf_row = jnp.sum(jnp.where(eye, logf_col, 0.0), axis=0, keepdims=True)
    i_row = jnp.sum(jnp.where(eye, i_col, 0.0), axis=0, keepdims=True)
    b_col = jnp.sum(jnp.where(tril, logf_row, 0.0), axis=1, keepdims=True)
    b_row = jnp.sum(jnp.where(r <= c, logf_col, 0.0), axis=0, keepdims=True)
    logd = jnp.where(tril, b_col - b_row + i_row, NEG)
    inter = b_col + m_in
    m_t = jnp.maximum(inter, jnp.max(logd, axis=1, keepdims=True))
    w_t = jnp.exp(inter - m_t)
    dm = jnp.exp(logd - m_t)
    b_last = b_col[L - 1:L, :]
    log_in = b_last - b_col + i_col
    m_new = jnp.maximum(b_last + m_in, jnp.max(log_in, axis=0, keepdims=True))
    w_col = jnp.exp(log_in - m_new)
    decay = jnp.exp(b_last + m_in - m_new)
    return dict(eye=eye, r=r, c=c, f_col=f_col, m_t=m_t, w_t=w_t, dm=dm, m_new=m_new, w_col=w_col, decay=decay)


def _mlstm_fwd(qk, z0, gates, bias, name):
    t = qk.shape[0]
    nc, nh, L = t // CHUNK, 4, CHUNK
    scale = HD ** -0.5

    def body(q_ref, k_ref, v_ref, g_ref, b_ref, h_ref, cs_ref, ns_ref, ms_ref, c_s, n_s, m_s):
        @pl.when(pl.program_id(0) == 0)
        def _():
            c_s[...] = jnp.zeros_like(c_s)
            n_s[...] = jnp.zeros_like(n_s)
            m_s[...] = jnp.zeros_like(m_s)

        for hd in range(nh):
            sl = slice(hd * HD, (hd + 1) * HD)
            cm, nv, m_in = c_s[hd], n_s[hd], m_s[hd]
            cs_ref[hd] = cm
            ns_ref[hd] = nv
            ms_ref[hd] = jnp.broadcast_to(m_in, (1, HD))
            q, kh, v = q_ref[:, sl], k_ref[:, sl] * scale, v_ref[:, sl]
            G = _mlstm_gates(g_ref[hd], b_ref[hd], m_in)
            qb, kb, vb = q.astype(BF16), kh.astype(BF16), v.astype(BF16)
            sc = _nt(qb, kb) * G["dm"]
            num = _nn(sc.astype(BF16), vb) + G["w_t"] * _nn(qb, cm.astype(BF16))
            den = jnp.sum(sc, axis=1, keepdims=True) + G["w_t"] * jnp.sum(q * nv, axis=1, keepdims=True)
            h_ref[:, sl] = num / jnp.maximum(jnp.abs(den), jnp.exp(-G["m_t"]))
            wk = G["w_col"] * kh
            c_s[hd] = G["decay"] * cm + _tn(wk.astype(BF16), vb)
            n_s[hd] = G["decay"] * nv + jnp.sum(wk, axis=0, keepdims=True)
            m_s[hd] = G["m_new"]

    hspec = lambda blk: pl.BlockSpec((L, 512), lambda j: (j, blk))
    st = lambda r: pl.BlockSpec((nh, None, r, HD), lambda j: (0, j, 0, 0))
    return _pc(body, name, (nc,),
               [hspec(0), hspec(1), hspec(2), pl.BlockSpec((nh, L, 2), lambda j: (0, j, 0)),
                pl.BlockSpec((nh, 1, 2), lambda j: (0, 0, 0))],
               [hspec(0), st(HD), st(1), st(1)],
               [_sds((t, 512)), _sds((nh, nc, HD, HD)), _sds((nh, nc, 1, HD)), _sds((nh, nc, 1, HD))],
               scratch=[pltpu.VMEM((nh, HD, HD), F32), pltpu.VMEM((nh, 1, HD), F32), pltpu.VMEM((nh, 1, 1), F32)])(qk, qk, z0, gates, bias)


def _mlstm_bwd(qk, z0, gates, bias, cs, ns, ms, dh, name):
    t = qk.shape[0]
    nc, nh, L = t // CHUNK, 4, CHUNK
    scale = HD ** -0.5

    def body(q_ref, k_ref, v_ref, g_ref, b_ref, cs_ref, ns_ref, ms_ref, dh_ref, dq_ref, dk_ref, dv_ref, dg_ref, dc_s, dn_s):
        @pl.when(pl.program_id(0) == 0)
        def _():
            dc_s[...] = jnp.zeros_like(dc_s)
            dn_s[...] = jnp.zeros_like(dn_s)

        for hd in range(nh):
            one_head(hd, slice(hd * HD, (hd + 1) * HD), q_ref, k_ref, v_ref, g_ref, b_ref, cs_ref, ns_ref, ms_ref, dh_ref,
                     dq_ref, dk_ref, dv_ref, dg_ref, dc_s, dn_s)

    def one_head(hd, sl, q_ref, k_ref, v_ref, g_ref, b_ref, cs_ref, ns_ref, ms_ref, dh_ref, dq_ref, dk_ref, dv_ref, dg_ref, dc_s, dn_s):
        cm, nv, m_in = cs_ref[hd], ns_ref[hd], ms_ref[hd][:, 0:1]
        q, kh, v = q_ref[:, sl], k_ref[:, sl] * scale, v_ref[:, sl]
        G = _mlstm_gates(g_ref[hd], b_ref[hd], m_in)
        w_t, dmat, w_col, decay = G["w_t"], G["dm"], G["w_col"], G["decay"]
        qb, kb, vb, cb = q.astype(BF16), kh.astype(BF16), v.astype(BF16), cm.astype(BF16)
        s = _nt(qb, kb)
        sc = s * dmat
        scb = sc.astype(BF16)
        qc = _nn(qb, cb)
        qn = jnp.sum(q * nv, axis=1, keepdims=True)
        num = _nn(scb, vb) + w_t * qc
        den = jnp.sum(sc, axis=1, keepdims=True) + w_t * qn
        e_m = jnp.exp(-G["m_t"])
        dnm = jnp.maximum(jnp.abs(den), e_m)
        dh_ = dh_ref[:, sl]
        dnum = dh_ / dnm
        dden = jnp.where(jnp.abs(den) > e_m, -jnp.sum(dh_ * num, axis=1, keepdims=True) / (dnm * dnm) * jnp.sign(den), 0.0)
        dnumb = dnum.astype(BF16)
        dsc = _nt(dnumb, vb) + dden
        dv = _tn(scb, dnumb)
        wd = w_t * dnum
        wdb = wd.astype(BF16)
        ds = dsc * dmat
        dsb = ds.astype(BF16)
        dq = _nt(wdb, cb) + (w_t * dden) * nv + _nn(dsb, kb)
        dc_o = _tn(qb, wdb)
        dn_o = jnp.sum(q * (w_t * dden), axis=0, keepdims=True)
        dw = jnp.sum(dnum * qc, axis=1, keepdims=True) + dden * qn
        dkh = _tn(dsb, qb)
        dlogd = ds * s
        db_col = jnp.sum(dlogd, axis=1, keepdims=True) + dw * w_t
        csum = jnp.sum(dlogd, axis=0, keepdims=True)
        dcn, dnn = dc_s[hd], dn_s[hd]
        dcnb = dcn.astype(BF16)
        kdc = _nn(kb, dcnb)
        dws = jnp.sum(kdc * v, axis=1, keepdims=True) + jnp.sum(kh * dnn, axis=1, keepdims=True)
        dv = dv + w_col * kdc
        dkh = dkh + w_col * (_nt(vb, dcnb) + dnn)
        dlin = dws * w_col
        ddecay = jnp.sum(jnp.sum(dcn * cm, axis=1, keepdims=True), axis=0, keepdims=True) + jnp.sum(dnn * nv, axis=1, keepdims=True)
        dlast = ddecay * decay + jnp.sum(dlin, axis=0, keepdims=True)
        rows = lax.broadcasted_iota(jnp.int32, (L, 1), 0)
        db_col = db_col - dlin + jnp.where(rows == L - 1, dlast, 0.0)
        eye, r, c = G["eye"], G["r"], G["c"]
        di = dlin + jnp.sum(jnp.where(eye, csum, 0.0), axis=1, keepdims=True)
        db_row = jnp.sum(jnp.where(eye, db_col, 0.0), axis=0, keepdims=True) - csum
        dlogf = jnp.sum(jnp.where(c >= r, db_row, 0.0), axis=1, keepdims=True)
        dg_ref[hd, :, 0:1] = di
        dg_ref[hd, :, 1:2] = dlogf * (1.0 - _sigmoid(G["f_col"]))
        dq_ref[:, sl] = dq
        dk_ref[:, sl] = dkh * scale
        dv_ref[:, sl] = dv
        dc_s[hd] = decay * dcn + dc_o
        dn_s[hd] = decay * dnn + dn_o

    rv = lambda j: nc - 1 - j
    hspec = lambda blk: pl.BlockSpec((L, 512), lambda j: (rv(j), blk))
    st = lambda r: pl.BlockSpec((nh, None, r, HD), lambda j: (0, rv(j), 0, 0))
    gs = pl.BlockSpec((nh, L, 2), lambda j: (0, rv(j), 0))
    return _pc(body, name, (nc,),
               [hspec(0), hspec(1), hspec(2), gs, pl.BlockSpec((nh, 1, 2), lambda j: (0, 0, 0)),
                st(HD), st(1), st(1), hspec(0)],
               [hspec(0), hspec(0), hspec(0), gs],
               [_sds((t, 512)), _sds((t, 512)), _sds((t, 512)), _sds((nh, t, 2))],
               scratch=[pltpu.VMEM((nh, HD, HD), F32), pltpu.VMEM((nh, 1, HD), F32)])(qk, qk, z0, gates, bias, cs, ns, ms, dh)


def _hgrn_act(qb_, fb_, ib_, lg):
    lb = _sigmoid(lg[0:1, :] - lg[1:2, :])
    sg = _sigmoid(fb_)
    f = lb + (1.0 - lb) * sg
    return lb, sg, f, _silu(qb_), (1.0 - lb) * (1.0 - sg), _silu(ib_), _cumsum_rows(jnp.log(f))


HG_SUB = 16


def _hgrn_offdiag(q, k, b, r0):
    beta = b[r0 - 1:r0, :]
    e1 = jnp.exp(b[r0:r0 + HG_SUB, :] - beta)
    e2 = jnp.where(_rows_of(b) < r0, jnp.exp(jnp.minimum(beta - b, 0.0)), 0.0)
    return q[r0:r0 + HG_SUB, :] * e1, k * e2, e1, e2


def _hgrn_fwd(z0, lbl, name):
    t = z0.shape[0]
    nc, nh, L = t // CHUNK, 4, CHUNK

    def body(q_ref, f_ref, i_ref, l_ref, o_ref, ss_ref, st_s):
        @pl.when(pl.program_id(1) == 0)
        def _():
            st_s[...] = jnp.zeros_like(st_s)

        st = st_s[...]
        ss_ref[...] = st
        _, _, _, q, k, v, b = _hgrn_act(q_ref[...], f_ref[...], i_ref[...], l_ref[...])
        o = _nt((q * jnp.exp(b)).astype(BF16), st.astype(BF16))
        sub = _rows_of(b) & (HG_SUB - 1)
        o = o + jnp.sum(q * k, axis=1, keepdims=True) * v
        for dl in range(1, HG_SUB):
            e = jnp.exp(jnp.where(sub >= dl, b - pltpu.roll(b, dl, 0), NEG))
            a = jnp.sum(q * pltpu.roll(k, dl, 0) * e, axis=1, keepdims=True)
            o = o + a * pltpu.roll(v, dl, 0)
        o_ref[...] = o
        vb = v.astype(BF16)
        for i in range(1, L // HG_SUB):
            r0 = i * HG_SUB
            qt, kt, _, _ = _hgrn_offdiag(q, k, b, r0)
            a = _nt(qt.astype(BF16), kt.astype(BF16))
            o_ref[r0:r0 + HG_SUB, :] += _nn(a.astype(BF16), vb)
        bl = b[L - 1:L, :]
        st_s[...] = st * jnp.exp(bl) + _tn(v.astype(BF16), (k * jnp.exp(bl - b)).astype(BF16))

    hspec = lambda off: pl.BlockSpec((L, HD), lambda h, j: (j, off + h))
    return _pc(body, name, (nh, nc),
               [hspec(16), hspec(20), hspec(24), pl.BlockSpec((2, HD), lambda h, j: (0, h))],
               [hspec(0), pl.BlockSpec((None, None, HD, HD), lambda h, j: (h, j, 0, 0))],
               [_sds((t, 512)), _sds((nh, nc, HD, HD))],
               scratch=[pltpu.VMEM((HD, HD), F32)])(z0, z0, z0, lbl)


def _hgrn_bwd(z0, lbl, ss, do, name):
    t = z0.shape[0]
    nc, nh, L = t // CHUNK, 4, CHUNK

    def body(q_ref, f_ref, i_ref, l_ref, ss_ref, do_ref, dq_ref, df_ref, di_ref, dl_ref, dst_s, dlb_s, dq_a, dk_a, dv_a, db_a):
        j = pl.program_id(1)

        @pl.when(j == 0)
        def _():
            dst_s[...] = jnp.zeros_like(dst_s)
            dlb_s[...] = jnp.zeros_like(dlb_s)

        st = ss_ref[...]
        qp, fp, ip = q_ref[...], f_ref[...], i_ref[...]
        lb, sg, f, q, k, v, b = _hgrn_act(qp, fp, ip, l_ref[...])
        do_ = do_ref[...]
        dob, stb = do_.astype(BF16), st.astype(BF16)
        eb = jnp.exp(b)
        qe = q * eb
        dqe = _nn(dob, stb)
        dst_o = _tn(dob, qe.astype(BF16))
        dq = dqe * eb
        db = dqe * qe
        rows = _rows_of(b)
        sub = rows & (HG_SUB - 1)
        p0 = jnp.sum(do_ * v, axis=1, keepdims=True)
        dq = dq + p0 * k
        dk = p0 * q
        dv = jnp.sum(q * k, axis=1, keepdims=True) * do_
        for dl in range(1, HG_SUB):
            up = L - dl
            kd, vd = pltpu.roll(k, dl, 0), pltpu.roll(v, dl, 0)
            e = jnp.exp(jnp.where(sub >= dl, b - pltpu.roll(b, dl, 0), NEG))
            a = jnp.sum(q * kd * e, axis=1, keepdims=True)
            p = jnp.sum(do_ * vd, axis=1, keepdims=True) * e
            dq = dq + p * kd
            dkd = p * q
            dbb = dkd * kd
            dv = dv + pltpu.roll(a * do_, up, 0)
            dk = dk + pltpu.roll(dkd, up, 0)
            db = db + dbb - pltpu.roll(dbb, up, 0)
        dq_a[...], dk_a[...], dv_a[...], db_a[...] = dq, dk, dv, db
        vb = v.astype(BF16)
        for i in range(1, L // HG_SUB):
            r0 = i * HG_SUB
            blk = slice(r0, r0 + HG_SUB)
            qt, kt, e1, e2 = _hgrn_offdiag(q, k, b, r0)
            qtb, ktb, dob_i = qt.astype(BF16), kt.astype(BF16), dob[blk, :]
            a = _nt(qtb, ktb).astype(BF16)
            da = _nt(dob_i, vb).astype(BF16)
            dv_a[...] += _tn(a, dob_i)
            dqt = _nn(da, ktb)
            dkt = _tn(da, qtb)
            dq_a[blk, :] += dqt * e1
            t1, t2 = dqt * qt, dkt * kt
            db_a[blk, :] += t1
            dk_a[...] += dkt * e2
            db_a[...] -= t2
            db_a[r0 - 1:r0, :] += jnp.sum(t2, axis=0, keepdims=True) - jnp.sum(t1, axis=0, keepdims=True)
        dq, dk, dv, db = dq_a[...], dk_a[...], dv_a[...], db_a[...]
        dstn = dst_s[...]
        dstnb = dstn.astype(BF16)
        bl = b[L - 1:L, :]
        ebl = jnp.exp(bl)
        kdec_e = jnp.exp(bl - b)
        kdec = k * kdec_e
        dbl = jnp.sum(dstn * st, axis=0, keepdims=True) * ebl
        dv = dv + _nt(kdec.astype(BF16), dstnb)
        dkdec = _nn(v.astype(BF16), dstnb)
        dk = dk + dkdec * kdec_e
        dx = dkdec * kdec
        dbl = dbl + jnp.sum(dx, axis=0, keepdims=True)
        db = db - dx + jnp.where(rows == L - 1, dbl, 0.0)
        dst_s[...] = dstn * ebl + dst_o
        dg = _rcumsum_rows(db)
        dfk = dg / f - dk
        dq_ref[...] = (dq * _dsilu(qp)).astype(BF16)
        di_ref[...] = (dv * _dsilu(ip)).astype(BF16)
        df_ref[...] = (dfk * (1.0 - lb) * sg * (1.0 - sg)).astype(BF16)
        dlb_s[...] += jnp.sum(dfk * (1.0 - sg), axis=0, keepdims=True)

        @pl.when(j == nc - 1)
        def _():
            dl0 = dlb_s[...] * lb * (1.0 - lb)
            dl_ref[0:1, :] = dl0
            dl_ref[1:2, :] = -dl0

    rv = lambda j: nc - 1 - j
    hspec = lambda off: pl.BlockSpec((L, HD), lambda h, j: (rv(j), off + h))
    return _pc(body, name, (nh, nc),
               [hspec(16), hspec(20), hspec(24), pl.BlockSpec((2, HD), lambda h, j: (0, h)),
                pl.BlockSpec((None, None, HD, HD), lambda h, j: (h, rv(j), 0, 0)), hspec(0)],
               [hspec(0), hspec(0), hspec(0), pl.BlockSpec((2, HD), lambda h, j: (0, h))],
               [_sds((t, 512), BF16), _sds((t, 512), BF16), _sds((t, 512), BF16), _sds((2, 512))],
               scratch=[pltpu.VMEM((HD, HD), F32), pltpu.VMEM((1, HD), F32)] + [pltpu.VMEM((L, HD), F32)] * 4)(z0, z0, z0, lbl, ss, do)


def _post0_fwd(hm, hh, z0, na, nb, w, h0, name, bm=512):
    t = h0.shape[0]
    bm = min(bm, t)

    def body(hm_ref, hh_ref, oa_ref, gb_ref, na_ref, nb_ref, w_ref, h_ref, o_ref, y_ref):
        for hd in range(4):
            sl = slice(hd * HD, (hd + 1) * HD)
            pa = _sigmoid(oa_ref[:, sl]) * hm_ref[:, sl]
            y_ref[:, sl] = (pa * _rstd(pa) * na_ref[:, sl]).astype(BF16)
            xb = hh_ref[:, sl]
            y_ref[:, 512 + hd * HD:512 + (hd + 1) * HD] = (xb * _rstd(xb) * nb_ref[:, sl] * _silu(gb_ref[:, sl])).astype(BF16)
        o_ref[...] = h_ref[...] + _nn(y_ref[...], w_ref[...])

    row = lambda wd, c: pl.BlockSpec((bm, wd), lambda i: (i, c))
    vec = lambda wd: pl.BlockSpec((1, wd), lambda i: (0, 0))
    return _pc(body, name, (t // bm,),
               [row(512, 0), row(512, 0), row(512, 3), row(512, 7), vec(512), vec(512),
                pl.BlockSpec((D, D), lambda i: (0, 0)), row(D, 0)],
               [row(D, 0), row(D, 0)], [_sds((t, D)), _sds((t, D), BF16)])(hm, hh, z0, z0, na, nb, w, h0)


def _post0_bwd(dh1, w, hm, hh, z0, na, nb, name, bm=512):
    t = dh1.shape[0]
    bm = min(bm, t)

    def body(dh_ref, w_ref, hm_ref, hh_ref, oa_ref, gb_ref, na_ref, nb_ref, dhm_ref, dhh_ref, doa_ref, dgb_ref, dna_ref, dnb_ref):
        @pl.when(pl.program_id(0) == 0)
        def _():
            dna_ref[...] = jnp.zeros_like(dna_ref)
            dnb_ref[...] = jnp.zeros_like(dnb_ref)

        dy = _nt(dh_ref[...].astype(BF16), w_ref[...])
        for hd in range(4):
            sl = slice(hd * HD, (hd + 1) * HD)
            hm_, oa = hm_ref[:, sl], oa_ref[:, sl]
            sg = _sigmoid(oa)
            dpa, dgr = _rms_bwd(dy[:, sl], sg * hm_, na_ref[:, sl])
            dna_ref[:, sl] += jnp.sum(dgr, axis=0, keepdims=True)
            doa_ref[:, sl] = (dpa * hm_ * sg * (1.0 - sg)).astype(BF16)
            dhm_ref[:, sl] = dpa * sg
            xb, gb, nbv = hh_ref[:, sl], gb_ref[:, sl], nb_ref[:, sl]
            dyb = dy[:, 512 + hd * HD:512 + (hd + 1) * HD]
            dgb_ref[:, sl] = (dyb * (xb * _rstd(xb) * nbv) * _dsilu(gb)).astype(BF16)
            dxb, dgr2 = _rms_bwd(dyb * _silu(gb), xb, nbv)
            dnb_ref[:, sl] += jnp.sum(dgr2, axis=0, keepdims=True)
            dhh_ref[:, sl] = dxb

    row = lambda wd, c: pl.BlockSpec((bm, wd), lambda i: (i, c))
    vec = lambda wd: pl.BlockSpec((1, wd), lambda i: (0, 0))
    return _pc(body, name, (t // bm,),
               [row(D, 0), pl.BlockSpec((D, D), lambda i: (0, 0)), row(512, 0), row(512, 0), row(512, 3), row(512, 7),
                vec(512), vec(512)],
               [row(512, 0), row(512, 0), row(512, 0), row(512, 0), vec(512), vec(512)],
               [_sds((t, 512)), _sds((t, 512)), _sds((t, 512), BF16), _sds((t, 512), BF16), _sds((1, 512)), _sds((1, 512))],
               )(dh1, w, hm, hh, z0, z0, na, nb)


def _memkv_fwd(mem, g, wkv_s, name):
    m = mem.shape[0]

    def body(x_ref, g_ref, w_ref, kv_ref, mn_ref):
        x = x_ref[...]
        mn = (x * _rstd(x) * g_ref[...]).astype(BF16)
        mn_ref[...] = mn
        kv_ref[...] = _nn(mn, w_ref[...])

    return _pc(body, name, (4,),
               [pl.BlockSpec((m, D), lambda k: (0, 0)), pl.BlockSpec((1, D), lambda k: (0, 0)),
                pl.BlockSpec((None, D, 512), lambda k: (k, 0, 0))],
               [pl.BlockSpec((m, 512), lambda k: (0, k)), pl.BlockSpec((m, D), lambda k: (0, 0))],
               [_sds((m, 2048)), _sds((m, D), BF16)])(mem, g, wkv_s)


def _memkv_bwd(dkv, wkv_s, mem, g, name):
    m = mem.shape[0]

    def body(d_ref, w_ref, x_ref, g_ref, dg_ref, acc):
        k = pl.program_id(0)

        @pl.when(k == 0)
        def _():
            acc[...] = jnp.zeros_like(acc)

        acc[...] += _nt(d_ref[...].astype(BF16), w_ref[...])

        @pl.when(k == 3)
        def _():
            _, dgr = _rms_bwd(acc[...], x_ref[...], g_ref[...])
            dg_ref[...] = jnp.sum(dgr, axis=0, keepdims=True)

    return _pc(body, name, (4,),
               [pl.BlockSpec((m, 512), lambda k: (0, k)), pl.BlockSpec((None, D, 512), lambda k: (k, 0, 0)),
                pl.BlockSpec((m, D), lambda k: (0, 0)), pl.BlockSpec((1, D), lambda k: (0, 0))],
               pl.BlockSpec((1, D), lambda k: (0, 0)), _sds((1, D)), scratch=[pltpu.VMEM((m, D), F32)])(dkv, wkv_s, mem, g)


def _xattn_probs(qh, kh):
    s = _nt(qh, kh) * (XD ** -0.5)
    p = jnp.exp(s - jnp.max(s, axis=1, keepdims=True))
    return p / jnp.sum(p, axis=1, keepdims=True)


def _xattn_fwd(q, kv, wo, h1, name, bm=512):
    t, m = q.shape[0], kv.shape[0]
    bm = min(bm, t)

    def body(q_ref, k_ref, v_ref, w_ref, h_ref, out_ref, o_ref):
        for hd in range(D // XD):
            sl = slice(hd * XD, (hd + 1) * XD)
            p = _xattn_probs(q_ref[:, sl].astype(BF16), k_ref[:, sl].astype(BF16))
            o_ref[:, sl] = _nn(p.astype(BF16), v_ref[:, sl].astype(BF16)).astype(BF16)
        out_ref[...] = h_ref[...] + _nn(o_ref[...], w_ref[...])

    row = pl.BlockSpec((bm, D), lambda i: (i, 0))
    return _pc(body, name, (t // bm,),
               [row, pl.BlockSpec((m, D), lambda i: (0, 0)), pl.BlockSpec((m, D), lambda i: (0, 1)),
                pl.BlockSpec((D, D), lambda i: (0, 0)), row],
               [row, row], [_sds((t, D)), _sds((t, D), BF16)])(q, kv, kv, wo, h1)


def _xattn_bwd(dh2, q, kv, wo, name, bm=512):
    t, m = q.shape[0], kv.shape[0]
    bm = min(bm, t)

    def body(dh_ref, q_ref, k_ref, v_ref, w_ref, dq_ref, dkv_ref):
        @pl.when(pl.program_id(0) == 0)
        def _():
            dkv_ref[...] = jnp.zeros_like(dkv_ref)

        d_o = _nt(dh_ref[...].astype(BF16), w_ref[...])
        for hd in range(D // XD):
            sl = slice(hd * XD, (hd + 1) * XD)
            qh, kh, vh = q_ref[:, sl].astype(BF16), k_ref[:, sl].astype(BF16), v_ref[:, sl].astype(BF16)
            p = _xattn_probs(qh, kh)
            dob = d_o[:, sl].astype(BF16)
            dp = _nt(dob, vh)
            dkv_ref[:, D + hd * XD:D + (hd + 1) * XD] += _tn(p.astype(BF16), dob)
            ds = (p * (dp - jnp.sum(dp * p, axis=1, keepdims=True)) * (XD ** -0.5)).astype(BF16)
            dq_ref[:, sl] = _nn(ds, kh).astype(BF16)
            dkv_ref[:, sl] += _tn(ds, qh)

    row = pl.BlockSpec((bm, D), lambda i: (i, 0))
    return _pc(body, name, (t // bm,),
               [row, row, pl.BlockSpec((m, D), lambda i: (0, 0)), pl.BlockSpec((m, D), lambda i: (0, 1)),
                pl.BlockSpec((D, D), lambda i: (0, 0))],
               [row, pl.BlockSpec((m, 2 * D), lambda i: (0, 0))],
               [_sds((t, D), BF16), _sds((m, 2 * D))])(dh2, q, kv, kv, wo)


NH1 = 8
FOX_BM = 512
FOX_BQ = 512
FOX_BK = 512
FOX_HEADS_PER_STEP = 2


def _foxprep_fwd(z1, qg, kg, fbp, name):
    t = z1.shape[0]
    bm = min(FOX_BM, t)

    def body(q_ref, k_ref, v_ref, f_ref, qg_ref, kg_ref, fb_ref, qn_ref, kn_ref, vb_ref, c_ref, carry):
        @pl.when(pl.program_id(0) == 0)
        def _():
            carry[...] = jnp.zeros_like(carry)

        for hd in range(NH1):
            sl = slice(hd * HD, (hd + 1) * HD)
            x = q_ref[:, sl]
            qn_ref[:, sl] = (x * _rstd(x) * qg_ref[...] * FOX_QSCALE).astype(BF16)
            x = k_ref[:, sl]
            kn_ref[:, sl] = (x * _rstd(x) * kg_ref[...]).astype(BF16)
        vb_ref[...] = v_ref[...].astype(BF16)
        c = carry[...] + _cumsum_rows(_log_sigmoid(f_ref[...] + fb_ref[...]))
        c_ref[...] = c
        carry[...] = c[bm - 1:bm, :]

    row = lambda c: pl.BlockSpec((bm, D), lambda i: (i, c))
    lane = pl.BlockSpec((bm, HD), lambda i: (i, 4 * D // HD))
    vec = pl.BlockSpec((1, HD), lambda i: (0, 0))
    return _pc(body, name, (t // bm,), [row(0), row(1), row(2), lane, vec, vec, vec],
               [row(0), row(0), row(0), pl.BlockSpec((bm, HD), lambda i: (i, 0))],
               [_sds((t, D), BF16), _sds((t, D), BF16), _sds((t, D), BF16), _sds((t, HD))],
               scratch=[pltpu.VMEM((1, HD), F32)])(z1, z1, z1, z1, qg, kg, fbp)


def _foxprep_bwd(dqn, dkn, z1, qg, kg, fbp, dc, name):
    t = z1.shape[0]
    bm = min(FOX_BM, t)
    nb = t // bm

    def body(dqn_ref, dkn_ref, q_ref, k_ref, f_ref, qg_ref, kg_ref, fb_ref, dc_ref,
             dq_ref, dk_ref, df_ref, dqg_ref, dkg_ref, dfb_ref, carry):
        @pl.when(pl.program_id(0) == 0)
        def _():
            carry[...] = jnp.zeros_like(carry)
            dqg_ref[...] = jnp.zeros_like(dqg_ref)
            dkg_ref[...] = jnp.zeros_like(dkg_ref)
            dfb_ref[...] = jnp.zeros_like(dfb_ref)

        for hd in range(NH1):
            sl = slice(hd * HD, (hd + 1) * HD)
            dx, dgr = _rms_bwd(dqn_ref[:, sl] * (HD ** -0.5), q_ref[:, sl], qg_ref[...])
            dq_ref[:, sl] = dx.astype(BF16)
            dqg_ref[...] += jnp.sum(dgr, axis=0, keepdims=True)
            dx, dgr = _rms_bwd(dkn_ref[:, sl], k_ref[:, sl], kg_ref[...])
            dk_ref[:, sl] = dx.astype(BF16)
            dkg_ref[...] += jnp.sum(dgr, axis=0, keepdims=True)
        dc_ = dc_ref[...]
        dlogf = _rcumsum_rows(dc_) + carry[...]
        carry[...] += jnp.sum(dc_, axis=0, keepdims=True)
        lanes = lax.broadcasted_iota(jnp.int32, dc_.shape, 1)
        df = jnp.where(lanes < NH1, dlogf * (1.0 - _sigmoid(f_ref[...] + fb_ref[...])), 0.0)
        df_ref[...] = df.astype(BF16)
        dfb_ref[...] += jnp.sum(df, axis=0, keepdims=True)

    rv = lambda i: nb - 1 - i
    row = lambda c: pl.BlockSpec((bm, D), lambda i: (rv(i), c))
    lane = lambda c: pl.BlockSpec((bm, HD), lambda i: (rv(i), c))
    vec = pl.BlockSpec((1, HD), lambda i: (0, 0))
    return _pc(body, name, (nb,), [row(0), row(0), row(0), row(1), lane(4 * D // HD), vec, vec, vec, lane(0)],
               [row(0), row(0), lane(0), vec, vec, vec],
               [_sds((t, D), BF16), _sds((t, D), BF16), _sds((t, HD), BF16), _sds((1, HD)), _sds((1, HD)), _sds((1, HD))],
               scratch=[pltpu.VMEM((1, HD), F32)])(dqn, dkn, z1, z1, z1, qg, kg, fbp, dc)


LOG2E = 1.4426950408889634
FOX_QSCALE = HD ** -0.5 * LOG2E


def _fox_scores(q, k, ck, i, j, bq, bk, masked):
    s = _nt(q, k) - ck
    if not masked:
        return s, None
    rows = i * bq + lax.broadcasted_iota(jnp.int32, s.shape, 0)
    cols = j * bk + lax.broadcasted_iota(jnp.int32, s.shape, 1)
    return s, cols <= rows


def _fox_block_kind(i, j, bq, bk):
    active = j * bk < (i + 1) * bq
    full = (j + 1) * bk <= i * bq + 1
    return full, active & jnp.logical_not(full)


def _fox_fwd(qn, kn, vb, crow, name):
    t = qn.shape[0]
    bq, bk, G = min(FOX_BQ, t), min(FOX_BK, t), FOX_HEADS_PER_STEP
    nq, nk = t // bq, t // bk

    def body(q_ref, k_ref, v_ref, ck_ref, o_ref, lse_ref, m_s, l_s, acc):
        i, j = pl.program_id(1), pl.program_id(2)

        @pl.when(j == 0)
        def _():
            m_s[...] = jnp.full_like(m_s, NEG)
            l_s[...] = jnp.zeros_like(l_s)
            acc[...] = jnp.zeros_like(acc)

        def step(masked):
            for g in range(G):
                sl = slice(g * HD, (g + 1) * HD)
                s, ok = _fox_scores(q_ref[:, sl], k_ref[:, sl], ck_ref[g], i, j, bq, bk, masked)
                if masked:
                    s = jnp.where(ok, s, NEG)
                m_new = jnp.maximum(m_s[g], jnp.max(s, axis=1, keepdims=True))
                alpha = jnp.exp2(m_s[g] - m_new)
                p = jnp.exp2(s - m_new)
                l_s[g] = alpha * l_s[g] + jnp.sum(p, axis=1, keepdims=True)
                acc[:, sl] = alpha * acc[:, sl] + _nn(p.astype(BF16), v_ref[:, sl])
                m_s[g] = m_new

        full, part = _fox_block_kind(i, j, bq, bk)
        pl.when(full)(lambda: step(False))
        pl.when(part)(lambda: step(True))

        @pl.when(j == nk - 1)
        def _():
            for g in range(G):
                sl = slice(g * HD, (g + 1) * HD)
                o_ref[:, sl] = acc[:, sl] / l_s[g]
                lse_ref[g] = m_s[g] + jnp.log2(l_s[g])

    kj = lambda i, j: jnp.minimum(j, ((i + 1) * bq - 1) // bk)
    kmap = lambda h, i, j: (kj(i, j), h)
    return _pc(body, name, (NH1 // G, nq, nk),
               [pl.BlockSpec((bq, G * HD), lambda h, i, j: (i, h)), pl.BlockSpec((bk, G * HD), kmap),
                pl.BlockSpec((bk, G * HD), kmap), pl.BlockSpec((G, 1, bk), lambda h, i, j: (h, 0, kj(i, j)))],
               [pl.BlockSpec((bq, G * HD), lambda h, i, j: (i, h)), pl.BlockSpec((G, bq, 1), lambda h, i, j: (h, i, 0))],
               [_sds((t, D)), _sds((NH1, t, 1))],
               scratch=[pltpu.VMEM((G, bq, 1), F32), pltpu.VMEM((G, bq, 1), F32), pltpu.VMEM((bq, G * HD), F32)])(qn, kn, vb, crow)


def _fox_bwd(qn, kn, vb, crow, lse, delta, do, name):
    t = qn.shape[0]
    bq, bk, G = min(FOX_BQ, t), min(FOX_BK, t), FOX_HEADS_PER_STEP
    nq, nk = t // bq, t // bk

    def body(q_ref, k_ref, v_ref, ck_ref, lse_ref, dl_ref, do_ref, dq_ref, dk_ref, dv_ref, dc_ref, dcq_ref, dk_s, dv_s, dc_s):
        j, i = pl.program_id(1), pl.program_id(2)

        @pl.when(i == 0)
        def _():
            dk_s[...] = jnp.zeros_like(dk_s)
            dv_s[...] = jnp.zeros_like(dv_s)
            dc_s[...] = jnp.zeros_like(dc_s)

        @pl.when((i == 0) & (j == 0))
        def _():
            dq_ref[...] = jnp.zeros_like(dq_ref)
            dcq_ref[...] = jnp.zeros_like(dcq_ref)

        def step(masked):
            rows = pl.ds(pl.multiple_of(i * bq, bq), bq)
            for g in range(G):
                sl = slice(g * HD, (g + 1) * HD)
                q, k = q_ref[:, sl], k_ref[:, sl]
                s, ok = _fox_scores(q, k, ck_ref[g], i, j, bq, bk, masked)
                if masked:
                    s = jnp.where(ok, s, NEG)
                p = jnp.exp2(s - lse_ref[g])
                dob = do_ref[:, sl]
                dv_s[:, sl] += _tn(p.astype(BF16), dob)
                ds = p * (_nt(dob, v_ref[:, sl]) - dl_ref[g])
                dsb = ds.astype(BF16)
                dq_ref[rows, sl] += _nn(dsb, k)
                dk_s[:, sl] += _tn(dsb, q)
                dc_s[g] -= jnp.sum(ds, axis=0, keepdims=True)
                dcq_ref[g, rows, :] += jnp.sum(ds, axis=1, keepdims=True)

        full, part = _fox_block_kind(i, j, bq, bk)
        pl.when(full)(lambda: step(False))
        pl.when(part)(lambda: step(True))

        @pl.when(i == nq - 1)
        def _():
            dk_ref[...] = dk_s[...] * (1.0 / LOG2E)
            dv_ref[...] = dv_s[...]
            dc_ref[...] = dc_s[...]

    qi = lambda i, j: jnp.maximum(i, (j * bk) // bq)
    qmap = lambda h, j, i: (qi(i, j), h)
    c3map = lambda h, j, i: (h, qi(i, j), 0)
    kspec = pl.BlockSpec((bk, G * HD), lambda h, j, i: (j, h))
    return _pc(body, name, (NH1 // G, nk, nq),
               [pl.BlockSpec((bq, G * HD), qmap), kspec, kspec,
                pl.BlockSpec((G, 1, bk), lambda h, j, i: (h, 0, j)), pl.BlockSpec((G, bq, 1), c3map),
                pl.BlockSpec((G, bq, 1), c3map), pl.BlockSpec((bq, G * HD), qmap)],
               [pl.BlockSpec((t, G * HD), lambda h, j, i: (0, h)), kspec, kspec, pl.BlockSpec((G, 1, bk), lambda h, j, i: (h, 0, j)),
                pl.BlockSpec((G, t, 1), lambda h, j, i: (h, 0, 0))],
               [_sds((t, D)), _sds((t, D)), _sds((t, D)), _sds((NH1, 1, t)), _sds((NH1, t, 1))],
               scratch=[pltpu.VMEM((bk, G * HD), F32), pltpu.VMEM((bk, G * HD), F32), pltpu.VMEM((G, 1, bk), F32)],
               )(qn, kn, vb, crow, lse, delta, do)


def _post1_fwd(o, z1, w, h3, name, bm=512):
    t = o.shape[0]
    bm = min(bm, t)

    def body(o_ref, g_ref, w_ref, h_ref, out_ref, og_ref):
        og_ref[...] = (o_ref[...] * _sigmoid(g_ref[...])).astype(BF16)
        out_ref[...] = h_ref[...] + _nn(og_ref[...], w_ref[...])

    row = lambda c: pl.BlockSpec((bm, D), lambda i: (i, c))
    return _pc(body, name, (t // bm,), [row(0), row(3), pl.BlockSpec((D, D), lambda i: (0, 0)), row(0)],
               [row(0), row(0)], [_sds((t, D)), _sds((t, D), BF16)])(o, z1, w, h3)


def _post1_bwd(dh4, w, o, z1, name, bm=512):
    t = o.shape[0]
    bm = min(bm, t)

    def body(dh_ref, w_ref, o_ref, g_ref, do_ref, dg_ref, dl_ref):
        d_og = _nt(dh_ref[...].astype(BF16), w_ref[...])
        o_, sg = o_ref[...], _sigmoid(g_ref[...])
        dob = (d_og * sg).astype(BF16)
        do_ref[...] = dob
        dg_ref[...] = (d_og * o_ * sg * (1.0 - sg)).astype(BF16)
        prod = dob.astype(F32) * o_
        for hd in range(NH1):
            dl_ref[hd] = jnp.sum(prod[:, hd * HD:(hd + 1) * HD], axis=1, keepdims=True)

    row = lambda c: pl.BlockSpec((bm, D), lambda i: (i, c))
    return _pc(body, name, (t // bm,), [row(0), pl.BlockSpec((D, D), lambda i: (0, 0)), row(0), row(3)],
               [row(0), row(0), pl.BlockSpec((NH1, bm, 1), lambda i: (0, i, 0))],
               [_sds((t, D), BF16), _sds((t, D), BF16), _sds((NH1, t, 1))])(dh4, w, o, z1)


def _final(h, g, tgt, name, bm=512):
    t = h.shape[0]
    bm = min(bm, t)

    def body(h_ref, g_ref, t_ref, l_ref, dh_ref, dg_ref):
        @pl.when(pl.program_id(0) == 0)
        def _():
            l_ref[...] = jnp.zeros_like(l_ref)
            dg_ref[...] = jnp.zeros_like(dg_ref)

        x, gv = h_ref[...], g_ref[...]
        r = _rstd(x)
        xh = x * r
        e = xh * gv - t_ref[...]
        l_ref[...] += 0.5 * jnp.sum(jnp.mean(e * e, axis=1, keepdims=True), axis=0, keepdims=True)
        dy = e * (1.0 / D)
        dg_ref[...] += jnp.sum(dy * xh, axis=0, keepdims=True)
        dxh = dy * gv
        dh_ref[...] = r * (dxh - xh * jnp.mean(dxh * xh, axis=1, keepdims=True))

    row = pl.BlockSpec((bm, D), lambda i: (i, 0))
    vec = pl.BlockSpec((1, D), lambda i: (0, 0))
    return _pc(body, name, (t // bm,), [row, vec, row], [pl.BlockSpec((1, HD), lambda i: (0, 0)), row, vec],
               [_sds((1, HD)), _sds((t, D)), _sds((1, D))])(h, g, tgt)


def _adam(w, g, m, v, name):
    r, c = w.shape
    br = min(r, 256)

    def body(w_ref, g_ref, m_ref, v_ref, d_ref, mo_ref, vo_ref):
        gv = g_ref[...]
        mn = ADAM_B1 * m_ref[...] + (1.0 - ADAM_B1) * gv
        vn = ADAM_B2 * v_ref[...] + (1.0 - ADAM_B2) * jnp.square(gv)
        m_hat = mn / (1.0 - ADAM_B1 ** ADAM_STEP)
        v_hat = vn / (1.0 - ADAM_B2 ** ADAM_STEP)
        d_ref[...] = -ADAM_LR * (m_hat / (jnp.sqrt(v_hat) + ADAM_EPS) + ADAM_WD * w_ref[...])
        mo_ref[...] = mn
        vo_ref[...] = vn

    blk = pl.BlockSpec((br, c), lambda i: (i, 0))
    return _pc(body, name, (r // br,), [blk] * 4, [blk] * 3, [_sds((r, c))] * 3)(w, g, m, v)


ZW = 4224
GATE0 = 4096


def _pack_w_in0(w):
    return jnp.concatenate([w[:, :2048], w[:, 2056:], w[:, 2048:2056], jnp.zeros((w.shape[0], ZW - 4104), w.dtype)], axis=1)


def _unpack_w_in0(g):
    return jnp.concatenate([g[:, :2048], g[:, GATE0:GATE0 + 8], g[:, 2048:GATE0]], axis=1)


def _pack_w_in1(w):
    return jnp.concatenate([w, jnp.zeros((w.shape[0], ZW - 4104), w.dtype)], axis=1)


def _unpack_w_in1(g):
    return g[:, :4104]


def _local_step(x, mem, tgt, W, S):
    t = x.shape[0]
    row = lambda v: v.reshape(1, -1)
    G = {}

    kv, mn = _memkv_fwd(mem, row(S["mem_norm_g"]), W["wkv_s"], "memkv_fwd")
    z0, u0 = _norm_mm(x, S["norm_mix_g"][0:1], W["w_in0"], "in0_fwd")
    qk = _conv_fwd(z0, S["conv_w"], "conv_fwd")
    g8 = z0[:, GATE0:GATE0 + 8]
    gates3 = jnp.stack([g8[:, :4].T, g8[:, 4:].T], axis=-1)
    gb = S["gate_b"]
    bias3 = jnp.stack([gb[0, :4], gb[0, 4:]], axis=-1)[:, None, :]
    hm, cs, ns, ms = _mlstm_fwd(qk, z0, gates3, bias3, "mlstm_fwd")
    hh, ss = _hgrn_fwd(z0, S["lb_logits"], "hgrn_fwd")
    h1, y0 = _post0_fwd(hm, hh, z0, S["mlstm_norm_g"], S["hgrn_norm_g"], W["w_out0"], x, "post0_fwd")

    def xattn_mlp_fwd(h, l):
        q, ux = _norm_mm(h, S["norm_xattn_g"][l:l + 1], W["wq"][l], f"xq{l}_fwd")
        h2, ox = _xattn_fwd(q, kv, W["wo"][l], h, f"xattn{l}_fwd")
        h3, a, um = _mlp_fwd(h2, S["norm_mlp_g"][l:l + 1], W["w1s"], W["w2"], l, f"mlp{l}_fwd")
        return h3, (h, q, ux, ox, h2, a, um)

    h3, sv0 = xattn_mlp_fwd(h1, 0)
    z1, u1 = _norm_mm(h3, S["norm_mix_g"][1:2], W["w_in1"], "in1_fwd")
    fbp = jnp.pad(S["c_fgate_b"], ((0, 0), (0, HD - NH1)))
    qn, kn, vb, c = _foxprep_fwd(z1, S["c_qnorm_g"], S["c_knorm_g"], fbp, "foxprep_fwd")
    crow = (c[:, :NH1] * LOG2E).T[:, None, :]
    o1, lse = _fox_fwd(qn, kn, vb, crow, "fox_fwd")
    h4, og = _post1_fwd(o1, z1, W["w_out1"], h3, "post1_fwd")
    h6, sv1 = xattn_mlp_fwd(h4, 1)
    lossp, dh, G["final_norm_g"] = _final(h6, row(S["final_norm_g"]), tgt, "final")

    dkv = None
    dgx, dgm, dwq, dwo, dw1, dw2 = [None, None], [None, None], [None, None], [None, None], [None, None], [None, None]

    def xattn_mlp_bwd(dh, l, sv):
        nonlocal dkv
        h, q, ux, ox, h2, a, um = sv
        dh2, da, r, dgm[l] = _mlp_bwd(dh, a, W["w1s"], W["w2"], l, h2, S["norm_mlp_g"][l:l + 1], f"mlp{l}_bwd")
        dw1[l] = _mm_tn(um, da, f"mlp{l}_dw1", col_chips=NCHIP)
        dw2[l] = _mm_tn(r, dh, f"mlp{l}_dw2")
        dq, dkv_l = _xattn_bwd(dh2, q, kv, W["wo"][l], f"xattn{l}_bwd")
        dkv = dkv_l if dkv is None else dkv + dkv_l
        dwo[l] = _mm_tn(ox, dh2, f"xattn{l}_dwo")
        dwq[l] = _mm_tn(ux, dq, f"xattn{l}_dwq")
        dh1, dgx[l] = _bwd_in(dq, W["wq"][l], h, S["norm_xattn_g"][l:l + 1], dh2, f"xq{l}_bwd")
        return dh1

    dh4 = xattn_mlp_bwd(dh, 1, sv1)
    do, dgate, delta = _post1_bwd(dh4, W["w_out1"], o1, z1, "post1_bwd")
    G["w_out1"] = _mm_tn(og, dh4, "post1_dw")
    dqn, dkn, dv1, dcrow, dcq = _fox_bwd(qn, kn, vb, crow, lse, delta, do, "fox_bwd")
    dc = jnp.pad((dcrow[:, 0, :] + dcq[:, :, 0]).T, ((0, 0), (0, HD - NH1)))
    dqr, dkr, df1, G["c_qnorm_g"], G["c_knorm_g"], dfb = _foxprep_bwd(
        dqn, dkn, z1, S["c_qnorm_g"], S["c_knorm_g"], fbp, dc, "foxprep_bwd")
    G["c_fgate_b"] = dfb[:, :NH1]
    dz1 = jnp.concatenate([dqr, dkr, dv1.astype(BF16), dgate, df1], axis=1)
    G["w_in1"] = _mm_tn(u1, dz1, "in1_dw")
    dh3, dgmix1 = _bwd_in(dz1, W["w_in1"], h3, S["norm_mix_g"][1:2], dh4, "in1_bwd")
    dh1 = xattn_mlp_bwd(dh3, 0, sv0)

    dhm, dhh, doa, dgb, G["mlstm_norm_g"], G["hgrn_norm_g"] = _post0_bwd(
        dh1, W["w_out0"], hm, hh, z0, S["mlstm_norm_g"], S["hgrn_norm_g"], "post0_bwd")
    G["w_out0"] = _mm_tn(y0, dh1, "post0_dw")
    dqa, dka, dva, dgates3 = _mlstm_bwd(qk, z0, gates3, bias3, cs, ns, ms, dhm, "mlstm_bwd")
    dqb, dfb0, dib, G["lb_logits"] = _hgrn_bwd(z0, S["lb_logits"], ss, dhh, "hgrn_bwd")
    duc, G["conv_w"] = _conv_bwd(z0, S["conv_w"], jnp.concatenate([dqa, dka], axis=1), "conv_bwd")
    dg8 = jnp.concatenate([dgates3[:, :, 0].T, dgates3[:, :, 1].T], axis=1)
    G["gate_b"] = jnp.sum(dg8, axis=0, keepdims=True)
    dz0 = jnp.concatenate([duc, dva.astype(BF16), doa, dqb, dfb0, dib, dgb,
                           jnp.pad(dg8, ((0, 0), (0, HD - 8))).astype(BF16)], axis=1)
    G["w_in0"] = _mm_tn(u0, dz0, "in0_dw")
    dx, dgmix0 = _bwd_in(dz0, W["w_in0"], x, S["norm_mix_g"][0:1], dh1, "in0_bwd")

    G["wkv"] = _mm_tn(mn, dkv, "memkv_dw", col_chips=NCHIP)
    G["mem_norm_g"] = _memkv_bwd(dkv, W["wkv_s"], mem, row(S["mem_norm_g"]), "memkv_bwd")
    G["norm_mix_g"] = jnp.concatenate([dgmix0, dgmix1], axis=0)
    G["norm_xattn_g"] = jnp.concatenate(dgx, axis=0)
    G["norm_mlp_g"] = jnp.concatenate(dgm, axis=0)
    G["wq"], G["wo"], G["w1"], G["w2"] = dwq, dwo, dw1, dw2
    return lossp[0, 0], dx, G


ANY = pl.BlockSpec(memory_space=pl.ANY)
NCHIP = 4
RS_ROWS = 4224
RS_TILE = 384


def _place():
    x, y, c = lax.axis_index("x"), lax.axis_index("y"), lax.axis_index("c")
    return x, y, c, [(1 - x, y), (x, 1 - y), (1 - x, 1 - y)]


def _comm_call(body, name, ins, out_shapes, sems):
    return pl.pallas_call(body, name=name, in_specs=[ANY] * len(ins), out_specs=[ANY] * len(out_shapes),
                          out_shape=out_shapes, scratch_shapes=sems)(*ins)


def _gather_weights(arrs, name):
    n = len(arrs)

    def body(*refs):
        ins, outs = refs[:n], refs[n:2 * n]
        send_i, recv_i, send_d, recv_d = refs[2 * n:]
        x, y, c, chips = _place()
        me = 2 * x + y

        def half(a, cc):
            h = arrs[a].shape[0] // 2
            return pl.ds(pl.multiple_of(cc * h, h), h)

        def ici(a, k, src_chip, dst_dev):
            return pltpu.make_async_remote_copy(
                src_ref=ins[a].at[half(a, c)], dst_ref=outs[a].at[src_chip, half(a, c)], send_sem=send_i.at[a, k],
                recv_sem=recv_i.at[a, k], device_id=dst_dev, device_id_type=MESH)

        def d2d(a, k, src_chip, cc):
            reg = outs[a].at[src_chip, half(a, cc)]
            return pltpu.make_async_remote_copy(src_ref=reg, dst_ref=reg, send_sem=send_d.at[a, k], recv_sem=recv_d.at[a, k],
                                                device_id=(x, y, 1 - c), device_id_type=MESH)

        for a in range(n):
            for k, (px, py) in enumerate(chips):
                ici(a, k, me, (px, py, c)).start()
        for k, (px, py) in enumerate(chips):
            for a in range(n):
                ici(a, k, 2 * px + py, (px, py, c)).wait_recv()
                d2d(a, k, 2 * px + py, c).start()
        for k, (px, py) in enumerate(chips):
            for a in range(n):
                ici(a, k, me, (px, py, c)).wait_send()
                d2d(a, k, 2 * px + py, c).wait_send()
                d2d(a, k, 2 * px + py, 1 - c).wait_recv()

    sem = lambda: pltpu.SemaphoreType.DMA((n, 3))
    return _comm_call(body, name, arrs, [_sds((NCHIP,) + a.shape, a.dtype) for a in arrs], [sem(), sem(), sem(), sem()])


def _pair_exchange(arrs, name):
    n = len(arrs)

    def body(*refs):
        ins, outs = refs[:n], refs[n:2 * n]
        send, recv = refs[2 * n:]
        x, y, c, _ = _place()
        copies = []
        for a in range(n):
            h = arrs[a].shape[1] // 2
            cp = pltpu.make_async_remote_copy(src_ref=ins[a].at[:, pl.ds(pl.multiple_of((1 - c) * h, h), h)], dst_ref=outs[a],
                                              send_sem=send.at[a], recv_sem=recv.at[a], device_id=(x, y, 1 - c), device_id_type=MESH)
            cp.start()
            copies.append(cp)
        for cp in copies:
            cp.wait()

    return _comm_call(body, name, arrs, [_sds((a.shape[0], a.shape[1] // 2, a.shape[2]), a.dtype) for a in arrs],
                      [pltpu.SemaphoreType.DMA((n,)), pltpu.SemaphoreType.DMA((n,))])


def _chip_exchange(arrs, name):
    n = len(arrs)

    def body(*refs):
        ins, outs = refs[:n], refs[n:2 * n]
        send, recv = refs[2 * n:]
        x, y, c, chips = _place()
        me = 2 * x + y
        copies = []
        for a in range(n):
            for k, (px, py) in enumerate(chips):
                r = pltpu.make_async_remote_copy(src_ref=ins[a].at[2 * px + py], dst_ref=outs[a].at[me], send_sem=send.at[a, k],
                                                 recv_sem=recv.at[a, k], device_id=(px, py, c), device_id_type=MESH)
                r.start()
                copies.append(r)
        for cp in copies:
            cp.wait()

    return _comm_call(body, name, arrs, [_sds(a.shape, a.dtype) for a in arrs],
                      [pltpu.SemaphoreType.DMA((n, 3)), pltpu.SemaphoreType.DMA((n, 3))])


def _pair_swap(arrs, name):
    n = len(arrs)

    def body(*refs):
        ins, outs = refs[:n], refs[n:2 * n]
        send, recv = refs[2 * n:]
        x, y, c, _ = _place()
        copies = []
        for a in range(n):
            cp = pltpu.make_async_remote_copy(src_ref=ins[a], dst_ref=outs[a], send_sem=send.at[a], recv_sem=recv.at[a],
                                              device_id=(x, y, 1 - c), device_id_type=MESH)
            cp.start()
            copies.append(cp)
        for cp in copies:
            cp.wait()

    return _comm_call(body, name, arrs, [_sds(a.shape, a.dtype) for a in arrs],
                      [pltpu.SemaphoreType.DMA((n,)), pltpu.SemaphoreType.DMA((n,))])


def _all_gather_devices(v, name):
    def body(v_ref, o_ref, send, recv, loc):
        x, y, c, _ = _place()
        me = 4 * x + 2 * y + c
        own = pltpu.make_async_copy(v_ref, o_ref.at[me], loc)
        own.start()
        copies = [own]
        for k in range(1, 8):
            fx, fy, fc = (k >> 2) & 1, (k >> 1) & 1, k & 1
            peer = (x ^ fx, y ^ fy, c ^ fc)
            r = pltpu.make_async_remote_copy(src_ref=v_ref, dst_ref=o_ref.at[me], send_sem=send.at[k - 1],
                                             recv_sem=recv.at[k - 1], device_id=peer, device_id_type=MESH)
            r.start()
            copies.append(r)
        for cp in copies:
            cp.wait()

    return _comm_call(body, name, [v], [_sds((8,) + v.shape, v.dtype)],
                      [pltpu.SemaphoreType.DMA((7,)), pltpu.SemaphoreType.DMA((7,)), pltpu.SemaphoreType.DMA])[0]


def _row_tile(r):
    return next((b for b in (512, 384, 256, 128, 64, 32, 16) if r % b == 0), r)


def _add2(a, b, out_dtype, name):
    r, w = a.shape
    br = _row_tile(r)

    def body(a_ref, b_ref, o_ref):
        o_ref[...] = (a_ref[...].astype(F32) + b_ref[...].astype(F32)).astype(out_dtype)

    blk = pl.BlockSpec((br, w), lambda i: (i, 0))
    return _pc(body, name, (r // br,), [blk, blk], blk, _sds((r, w), out_dtype))(a, b)


def _sum_slots(a, out_dtype, name, extra=None):
    n, r, w = a.shape
    br = _row_tile(r)

    def body(*refs):
        a_ref, o_ref = refs[0], refs[-1]
        acc = a_ref[0].astype(F32)
        for s in range(1, n):
            acc = acc + a_ref[s].astype(F32)
        if extra is not None:
            acc = acc + refs[1][...].astype(F32)
        o_ref[...] = acc.astype(out_dtype)

    ins = [a] + ([extra] if extra is not None else [])
    specs = [pl.BlockSpec((n, br, w), lambda i: (0, i, 0))] + ([pl.BlockSpec((br, w), lambda i: (i, 0))] if extra is not None else [])
    return _pc(body, name, (r // br,), specs, pl.BlockSpec((br, w), lambda i: (i, 0)), _sds((r, w), out_dtype))(*ins)


SMALL = ["norm_mix_g", "norm_xattn_g", "norm_mlp_g", "final_norm_g", "mem_norm_g", "hgrn_lb_logits", "mlstm_norm_g",
         "hgrn_norm_g", "c_qnorm_g", "c_knorm_g", "ab_gate_b", "c_fgate_b"]
SMALL_ROWS = 16


def _pack_small(parts):
    flat = jnp.concatenate([p.reshape(-1).astype(F32) for p in parts])
    return jnp.pad(flat, (0, SMALL_ROWS * D - flat.shape[0])).reshape(SMALL_ROWS, D)


def _unpack_small(buf, shapes):
    flat, out, off = buf.reshape(-1), [], 0
    for s in shapes:
        n = 1
        for d in s:
            n *= d
        out.append(flat[off:off + n].reshape(s))
        off += n
    return out


def kernel(x, mem, norm_mix_g, norm_xattn_g, norm_mlp_g, final_norm_g, ab_w_in, ab_conv_w, ab_gate_b, hgrn_lb_logits, mlstm_norm_g, hgrn_norm_g, ab_w_out, c_w_in, c_fgate_b, c_qnorm_g, c_knorm_g, c_w_out, mem_norm_g, mem_w_kv, xa_w_q, xa_w_o, mlp_w1, mlp_w2, loss_target, m_norm_mix_g, m_norm_xattn_g, m_norm_mlp_g, m_final_norm_g, m_ab_w_in, m_ab_conv_w, m_ab_gate_b, m_hgrn_lb_logits, m_mlstm_norm_g, m_hgrn_norm_g, m_ab_w_out, m_c_w_in, m_c_fgate_b, m_c_qnorm_g, m_c_knorm_g, m_c_w_out, m_mem_norm_g, m_mem_w_kv, m_xa_w_q, m_xa_w_o, m_mlp_w1, m_mlp_w2, v_norm_mix_g, v_norm_xattn_g, v_norm_mlp_g, v_final_norm_g, v_ab_w_in, v_ab_conv_w, v_ab_gate_b, v_hgrn_lb_logits, v_mlstm_norm_g, v_hgrn_norm_g, v_ab_w_out, v_c_w_in, v_c_fgate_b, v_c_qnorm_g, v_c_knorm_g, v_c_w_out, v_mem_norm_g, v_mem_w_kv, v_xa_w_q, v_xa_w_o, v_mlp_w1, v_mlp_w2):
    A = dict(locals())
    chip = 2 * lax.axis_index("x") + lax.axis_index("y")

    big = ["ab_w_in", "c_w_in", "ab_w_out", "c_w_out", "mem_w_kv", "xa_w_q", "xa_w_o", "mlp_w1", "mlp_w2"]
    shard2d = {"ab_w_in": (D, 1026), "c_w_in": (D, 1026), "ab_w_out": (256, D), "c_w_out": (256, D), "mem_w_kv": (D, 512),
               "xa_w_q": (512, D), "xa_w_o": (512, D), "mlp_w1": (2 * D, D), "mlp_w2": (2 * D, D)}
    own = [A[n].reshape(shard2d[n]).astype(BF16) for n in big] + [jnp.pad(ab_conv_w[0], ((0, 16 - CONV_W), (0, 0)))]
    gathered = _gather_weights(own, "gather_weights")
    gathered = [lax.dynamic_update_index_in_dim(g, o, chip, 0) for g, o in zip(gathered, own)]
    gathered[-1] = gathered[-1][:, :CONV_W]
    gw = dict(zip(big, gathered[:-1]))
    cols = lambda g: jnp.concatenate([g[k] for k in range(NCHIP)], axis=1)
    per_layer = lambda g: g.reshape(NCHIP, 2, -1, D).transpose(1, 0, 2, 3)
    W = dict(
        w_in0=_pack_w_in0(cols(gw["ab_w_in"])), w_in1=_pack_w_in1(cols(gw["c_w_in"])),
        w_out0=gw["ab_w_out"].reshape(D, D), w_out1=gw["c_w_out"].reshape(D, D), wkv_s=gw["mem_w_kv"],
        wq=per_layer(gw["xa_w_q"]).reshape(2, D, D), wo=per_layer(gw["xa_w_o"]).reshape(2, D, D),
        w1s=gw["mlp_w1"].reshape(NCHIP, 2, D, D), w2=gw["mlp_w2"].reshape(NCHIP, 2, D, D))
    S = dict(norm_mix_g=norm_mix_g, norm_xattn_g=norm_xattn_g, norm_mlp_g=norm_mlp_g, final_norm_g=final_norm_g,
             conv_w=cols(gathered[-1]), gate_b=ab_gate_b, lb_logits=hgrn_lb_logits, mlstm_norm_g=mlstm_norm_g,
             hgrn_norm_g=hgrn_norm_g, c_fgate_b=c_fgate_b, c_qnorm_g=c_qnorm_g, c_knorm_g=c_knorm_g, mem_norm_g=mem_norm_g)

    lossp, dx, G = _local_step(x[0], mem[0], loss_target[0], W, S)

    gsmall = {"norm_mix_g": G["norm_mix_g"], "norm_xattn_g": G["norm_xattn_g"], "norm_mlp_g": G["norm_mlp_g"],
              "final_norm_g": G["final_norm_g"], "mem_norm_g": G["mem_norm_g"], "hgrn_lb_logits": G["lb_logits"],
              "mlstm_norm_g": G["mlstm_norm_g"], "hgrn_norm_g": G["hgrn_norm_g"], "c_qnorm_g": G["c_qnorm_g"],
              "c_knorm_g": G["c_knorm_g"], "ab_gate_b": G["gate_b"], "c_fgate_b": G["c_fgate_b"]}
    packed = _pack_small([gsmall[n] for n in SMALL] + [G["conv_w"], lossp])
    red = _sum_slots(_all_gather_devices(packed, "gather_small"), F32, "sum_small")
    small_shapes = [A[n].shape for n in SMALL]
    *gs, gconv, loss = _unpack_small(red, small_shapes + [(CONV_W, D), ()])
    gs = dict(zip(SMALL, gs))
    gconv = lax.dynamic_slice_in_dim(gconv, chip * 256, 256, axis=1)[None]

    def stack_cols(g):
        return jnp.stack([g[:, 1026 * k:1026 * (k + 1)] for k in range(NCHIP)])

    by_rows = lambda g: g.reshape(NCHIP, -1, D)
    main = ["ab_w_out", "c_w_out", "xa_w_q", "xa_w_o", "mlp_w1", "mlp_w2"]
    flat = jnp.concatenate([by_rows(G["w_out0"]), by_rows(G["w_out1"]), by_rows(G["wq"][0]), by_rows(G["wq"][1]),
                            by_rows(G["wo"][0]), by_rows(G["wo"][1]), G["w1"][0], G["w1"][1], by_rows(G["w2"][0]),
                            by_rows(G["w2"][1])], axis=1)
    send = [flat, stack_cols(_unpack_w_in0(G["w_in0"])), stack_cols(_unpack_w_in1(G["w_in1"])), G["wkv"]]
    core = lax.axis_index("c")
    theirs = _pair_exchange(send, "pair_exchange")
    psums = []
    for i, (a, th) in enumerate(zip(send, theirs)):
        h = a.shape[1] // 2
        mine = lax.dynamic_slice_in_dim(a, core * h, h, axis=1)
        psums.append(_add2(mine.reshape(-1, a.shape[2]), th.reshape(-1, a.shape[2]), BF16, f"pair_sum{i}").reshape(th.shape))
    from_chips = _chip_exchange(psums, "chip_exchange")
    rhalf = []
    for i, (f, p) in enumerate(zip(from_chips, psums)):
        f = lax.dynamic_update_index_in_dim(f, lax.dynamic_index_in_dim(p, chip, 0, keepdims=False), chip, 0)
        rhalf.append(_sum_slots(f, F32, f"chip_sum{i}"))
    other = _pair_swap(rhalf, "pair_swap")
    rfull = [jnp.where(core == 0, jnp.concatenate([m_, o_], axis=0), jnp.concatenate([o_, m_], axis=0))
             for m_, o_ in zip(rhalf, other)]
    gbig, off = {"ab_w_in": rfull[1], "c_w_in": rfull[2], "mem_w_kv": rfull[3]}, 0
    rmain = rfull[0].reshape(-1)
    for n in main:
        r, c = shard2d[n]
        gbig[n] = rmain[off:off + r * c].reshape(r, c)
        off += r * c

    out_g, out_d, out_m, out_v = {}, {}, {}, {}
    for n in big:
        d_, m_, v_ = _adam(A[n].reshape(shard2d[n]), gbig[n], A["m_" + n].reshape(shard2d[n]), A["v_" + n].reshape(shard2d[n]), "adam_" + n)
        out_g[n] = gbig[n].reshape(A[n].shape)
        out_d[n], out_m[n], out_v[n] = d_.reshape(A[n].shape), m_.reshape(A[n].shape), v_.reshape(A[n].shape)
    sd, sm, sv = _adam(_pack_small([A[n] for n in SMALL]), _pack_small([gs[n] for n in SMALL]),
                       _pack_small([A["m_" + n] for n in SMALL]), _pack_small([A["v_" + n] for n in SMALL]), "adam_small")
    for n, d_, m_, v_ in zip(SMALL, _unpack_small(sd, small_shapes), _unpack_small(sm, small_shapes), _unpack_small(sv, small_shapes)):
        out_g[n], out_d[n], out_m[n], out_v[n] = gs[n], d_, m_, v_
    cd, cm_, cv = _adam(ab_conv_w[0], gconv[0], m_ab_conv_w[0], v_ab_conv_w[0], "adam_conv")
    out_g["ab_conv_w"], out_d["ab_conv_w"], out_m["ab_conv_w"], out_v["ab_conv_w"] = gconv, cd[None], cm_[None], cv[None]

    order = ["norm_mix_g", "norm_xattn_g", "norm_mlp_g", "final_norm_g", "ab_w_in", "ab_conv_w", "ab_gate_b", "hgrn_lb_logits",
             "mlstm_norm_g", "hgrn_norm_g", "ab_w_out", "c_w_in", "c_fgate_b", "c_qnorm_g", "c_knorm_g", "c_w_out", "mem_norm_g",
             "mem_w_kv", "xa_w_q", "xa_w_o", "mlp_w1", "mlp_w2"]
    return (loss, dx[None], *[out_g[n] for n in order], *[out_d[n] for n in order], *[out_m[n] for n in order],
            *[out_v[n] for n in order])
```

```python
import functools

import jax
import jax.numpy as jnp
from jax import lax
from jax.experimental import pallas as pl
from jax.experimental.pallas import tpu as pltpu

F32 = jnp.float32
BF16 = jnp.bfloat16
EPS = 1e-6
D = 1024
CHUNK = 64
HD = 128
XD = 256
NEG = -1e30
VMEM_LIMIT_V7X = 56 * 1024 * 1024
ADAM_LR, ADAM_B1, ADAM_B2, ADAM_EPS, ADAM_WD, ADAM_STEP = 0.001, 0.9, 0.999, 1e-08, 0.01, 10
MESH = pl.DeviceIdType.MESH


def _pc(body, name, grid, in_specs, out_specs, out_shape, scratch=(), **kw):
    return pl.pallas_call(
        body, name=name, grid=grid, in_specs=in_specs, out_specs=out_specs, out_shape=out_shape,
        scratch_shapes=scratch,
        compiler_params=pltpu.CompilerParams(
            dimension_semantics=("arbitrary",) * len(grid), vmem_limit_bytes=VMEM_LIMIT_V7X), **kw)


def _sds(shape, dtype=F32):
    return jax.ShapeDtypeStruct(shape, dtype)


def _blk(n, target):
    return max(b for b in range(128, max(target, 128) + 1, 128) if n % b == 0)


def _dot(a, b, dims):
    return lax.dot_general(a, b, (dims, ((), ())), preferred_element_type=F32)


def _nn(a, b):
    return _dot(a, b, ((1,), (0,)))


def _nt(a, b):
    return _dot(a, b, ((1,), (1,)))


def _tn(a, b):
    return _dot(a, b, ((0,), (0,)))


def _sigmoid(x):
    return 1.0 / (1.0 + jnp.exp(-x))


def _log_sigmoid(x):
    return jnp.minimum(x, 0.0) - jnp.log(1.0 + jnp.exp(-jnp.abs(x)))


def _rstd(x):
    return lax.rsqrt(jnp.mean(x * x, axis=-1, keepdims=True) + EPS)


def _rms_bwd(du, x, g):
    r = _rstd(x)
    xh = x * r
    dxh = du * g
    dx = r * (dxh - xh * jnp.mean(dxh * xh, axis=-1, keepdims=True))
    return dx, du * xh


def _norm_mm(h, g, w, name, bm=512, bn=512):
    t, n = h.shape[0], w.shape[1]
    bm, bn = min(bm, t), _blk(n, 3 * bn)

    def body(h_ref, g_ref, w_ref, z_ref, u_ref):
        @pl.when(pl.program_id(1) == 0)
        def _():
            x = h_ref[...]
            u_ref[...] = (x * _rstd(x) * g_ref[...]).astype(BF16)
        z_ref[...] = _nn(u_ref[...], w_ref[...])

    return _pc(body, name, (t // bm, n // bn),
               [pl.BlockSpec((bm, D), lambda i, j: (i, 0)), pl.BlockSpec((1, D), lambda i, j: (0, 0)),
                pl.BlockSpec((D, bn), lambda i, j: (0, j))],
               [pl.BlockSpec((bm, bn), lambda i, j: (i, j)), pl.BlockSpec((bm, D), lambda i, j: (i, 0))],
               [_sds((t, n)), _sds((t, D), BF16)])(h, g, w)


def _mm_tn(a, b, name, bm=1024, bn=1024, bt=512, col_chips=None):
    t, m = a.shape
    n = b.shape[1]
    bm, bn, bt = _blk(m, bm), (n // col_chips if col_chips else _blk(n, bn + bn // 2)), min(bt, t)
    nt = t // bt

    def body(a_ref, b_ref, o_ref, acc):
        k = pl.program_id(2)

        @pl.when(k == 0)
        def _():
            acc[...] = jnp.zeros_like(acc)

        acc[...] += _tn(a_ref[...].astype(BF16), b_ref[...].astype(BF16))

        @pl.when(k == nt - 1)
        def _():
            o_ref[...] = acc[...].astype(BF16)

    if col_chips:
        out_spec, out_shape = pl.BlockSpec((None, bm, bn), lambda i, j, k: (j, i, 0)), _sds((col_chips, m, bn), BF16)
    else:
        out_spec, out_shape = pl.BlockSpec((bm, bn), lambda i, j, k: (i, j)), _sds((m, n), BF16)
    return _pc(body, name, (m // bm, n // bn, nt),
               [pl.BlockSpec((bt, bm), lambda i, j, k: (k, i)), pl.BlockSpec((bt, bn), lambda i, j, k: (k, j))],
               out_spec, out_shape, scratch=[pltpu.VMEM((bm, bn), F32)])(a, b)


def _bwd_in(dz, w, h, g, dh, name, bm=512, bk=1024):
    t, n = dz.shape
    bm, bk = min(bm, t), _blk(n, bk + bk // 2)
    nk = n // bk

    def body(dz_ref, w_ref, h_ref, g_ref, dh_ref, o_ref, dg_ref, acc):
        i, k = pl.program_id(0), pl.program_id(1)

        @pl.when(k == 0)
        def _():
            acc[...] = jnp.zeros_like(acc)

        @pl.when((i == 0) & (k == 0))
        def _():
            dg_ref[...] = jnp.zeros_like(dg_ref)

        acc[...] += _nt(dz_ref[...], w_ref[...])

        @pl.when(k == nk - 1)
        def _():
            dx, dgr = _rms_bwd(acc[...], h_ref[...], g_ref[...])
            o_ref[...] = dh_ref[...] + dx
            dg_ref[...] += jnp.sum(dgr, axis=0, keepdims=True)

    return _pc(body, name, (t // bm, nk),
               [pl.BlockSpec((bm, bk), lambda i, k: (i, k)), pl.BlockSpec((D, bk), lambda i, k: (0, k)),
                pl.BlockSpec((bm, D), lambda i, k: (i, 0)), pl.BlockSpec((1, D), lambda i, k: (0, 0)),
                pl.BlockSpec((bm, D), lambda i, k: (i, 0))],
               [pl.BlockSpec((bm, D), lambda i, k: (i, 0)), pl.BlockSpec((1, D), lambda i, k: (0, 0))],
               [_sds((t, D)), _sds((1, D))], scratch=[pltpu.VMEM((bm, D), F32)])(dz, w, h, g, dh)


def _mlp_fwd(h, g, w1s, w2, l, name, bm=512):
    t = h.shape[0]
    bm = min(bm, t)
    nk = w1s.shape[0]

    def body(h_ref, g_ref, w1_ref, w2_ref, o_ref, a_ref, u_ref, acc):
        k = pl.program_id(1)

        @pl.when(k == 0)
        def _():
            x = h_ref[...]
            u_ref[...] = (x * _rstd(x) * g_ref[...]).astype(BF16)
            acc[...] = jnp.zeros_like(acc)

        a = _nn(u_ref[...], w1_ref[...])
        a_ref[...] = a
        r = jnp.square(jnp.maximum(a, 0.0)).astype(BF16)
        acc[...] += _nn(r, w2_ref[...])

        @pl.when(k == nk - 1)
        def _():
            o_ref[...] = h_ref[...] + acc[...]

    return _pc(body, name, (t // bm, nk),
               [pl.BlockSpec((bm, D), lambda i, k: (i, 0)), pl.BlockSpec((1, D), lambda i, k: (0, 0)),
                pl.BlockSpec((None, None, D, D), lambda i, k: (k, l, 0, 0)), pl.BlockSpec((None, None, D, D), lambda i, k: (k, l, 0, 0))],
               [pl.BlockSpec((bm, D), lambda i, k: (i, 0)), pl.BlockSpec((bm, D), lambda i, k: (i, k)),
                pl.BlockSpec((bm, D), lambda i, k: (i, 0))],
               [_sds((t, D)), _sds((t, nk * D)), _sds((t, D), BF16)],
               scratch=[pltpu.VMEM((bm, D), F32)])(h, g, w1s, w2)


def _mlp_bwd(dh, a, w1s, w2, l, h, g, name, bm=512):
    t = h.shape[0]
    bm = min(bm, t)
    nk = w1s.shape[0]

    def body(dh_ref, a_ref, w1_ref, w2_ref, h_ref, g_ref, o_ref, da_ref, r_ref, dg_ref, acc):
        i, k = pl.program_id(0), pl.program_id(1)

        @pl.when(k == 0)
        def _():
            acc[...] = jnp.zeros_like(acc)

        @pl.when((i == 0) & (k == 0))
        def _():
            dg_ref[...] = jnp.zeros_like(dg_ref)

        ap = jnp.maximum(a_ref[...], 0.0)
        r_ref[...] = jnp.square(ap).astype(BF16)
        dr = _nt(dh_ref[...].astype(BF16), w2_ref[...])
        da = (dr * (2.0 * ap)).astype(BF16)
        da_ref[...] = da
        acc[...] += _nt(da, w1_ref[...])

        @pl.when(k == nk - 1)
        def _():
            dx, dgr = _rms_bwd(acc[...], h_ref[...], g_ref[...])
            o_ref[...] = dh_ref[...] + dx
            dg_ref[...] += jnp.sum(dgr, axis=0, keepdims=True)

    return _pc(body, name, (t // bm, nk),
               [pl.BlockSpec((bm, D), lambda i, k: (i, 0)), pl.BlockSpec((bm, D), lambda i, k: (i, k)),
                pl.BlockSpec((None, None, D, D), lambda i, k: (k, l, 0, 0)), pl.BlockSpec((None, None, D, D), lambda i, k: (k, l, 0, 0)),
                pl.BlockSpec((bm, D), lambda i, k: (i, 0)), pl.BlockSpec((1, D), lambda i, k: (0, 0))],
               [pl.BlockSpec((bm, D), lambda i, k: (i, 0)), pl.BlockSpec((bm, D), lambda i, k: (i, k)),
                pl.BlockSpec((bm, D), lambda i, k: (i, k)), pl.BlockSpec((1, D), lambda i, k: (0, 0))],
               [_sds((t, D)), _sds((t, nk * D), BF16), _sds((t, nk * D), BF16), _sds((1, D))],
               scratch=[pltpu.VMEM((bm, D), F32)])(dh, a, w1s, w2, h, g)


def _rows_of(x):
    return lax.broadcasted_iota(jnp.int32, x.shape, 0)


def _shift_down(x, s):
    if s == 0:
        return x
    return jnp.where(_rows_of(x) >= s, pltpu.roll(x, s, 0), 0.0)


def _shift_up(x, s):
    if s == 0:
        return x
    n = x.shape[0]
    return jnp.where(_rows_of(x) < n - s, pltpu.roll(x, n - s, 0), 0.0)


def _cumsum_rows(x):
    n, s = x.shape[0], 1
    while s < n:
        x = x + _shift_down(x, s)
        s *= 2
    return x


def _rcumsum_rows(x):
    n, s = x.shape[0], 1
    while s < n:
        x = x + _shift_up(x, s)
        s *= 2
    return x


def _silu(x):
    return x * _sigmoid(x)


def _dsilu(x):
    s = _sigmoid(x)
    return s * (1.0 + x * (1.0 - s))


CONV_W = 4


def _conv_pre(u, w):
    y = _shift_down(u, CONV_W - 1) * w[0:1, :]
    for j in range(1, CONV_W):
        y = y + _shift_down(u, CONV_W - 1 - j) * w[j:j + 1, :]
    return y


def _conv_fwd(z0, cw, name):
    t = z0.shape[0]

    def body(u_ref, w_ref, o_ref):
        o_ref[...] = _silu(_conv_pre(u_ref[...], w_ref[...]))

    return _pc(body, name, (2 * 512 // HD,),
               [pl.BlockSpec((t, HD), lambda c: (0, c)), pl.BlockSpec((CONV_W, HD), lambda c: (0, c))],
               pl.BlockSpec((t, HD), lambda c: (0, c)), _sds((t, 1024)))(z0, cw)


def _conv_bwd(z0, cw, dy, name):
    t = z0.shape[0]

    def body(u_ref, w_ref, dy_ref, du_ref, dw_ref):
        u, w = u_ref[...], w_ref[...]
        dpre = dy_ref[...] * _dsilu(_conv_pre(u, w))
        du = _shift_up(dpre, CONV_W - 1) * w[0:1, :]
        for j in range(1, CONV_W):
            du = du + _shift_up(dpre, CONV_W - 1 - j) * w[j:j + 1, :]
        du_ref[...] = du.astype(BF16)
        for j in range(CONV_W):
            dw_ref[j:j + 1, :] = jnp.sum(dpre * _shift_down(u, CONV_W - 1 - j), axis=0, keepdims=True)

    return _pc(body, name, (2 * 512 // HD,),
               [pl.BlockSpec((t, HD), lambda c: (0, c)), pl.BlockSpec((CONV_W, HD), lambda c: (0, c)),
                pl.BlockSpec((t, HD), lambda c: (0, c))],
               [pl.BlockSpec((t, HD), lambda c: (0, c)), pl.BlockSpec((CONV_W, HD), lambda c: (0, c))],
               [_sds((t, 1024), BF16), _sds((CONV_W, 1024))])(z0, cw, dy)


def _mlstm_gates(gate, bias, m_in):
    L = gate.shape[0]
    r = lax.broadcasted_iota(jnp.int32, (L, L), 0)
    c = lax.broadcasted_iota(jnp.int32, (L, L), 1)
    eye, tril = r == c, c <= r
    i_col = gate[:, 0:1] + bias[:, 0:1]
    f_col = gate[:, 1:2] + bias[:, 1:2]
    logf_col = _log_sigmoid(f_col)
    logf_row = jnp.sum(jnp.where(eye, logf_col, 0.0), axis=0, keepdims=True)
    i_row = jnp.sum(jnp.where(eye, i_col, 0.0), axis=0, keepdims=True)
    b_col = jnp.sum(jnp.where(tril, logf_row, 0.0), axis=1, keepdims=True)
    b_row = jnp.sum(jnp.where(r <= c, logf_col, 0.0), axis=0, keepdims=True)
    logd = jnp.where(tril, b_col - b_row + i_row, NEG)
    inter = b_col + m_in
    m_t = jnp.maximum(inter, jnp.max(logd, axis=1, keepdims=True))
    w_t = jnp.exp(inter - m_t)
    dm = jnp.exp(logd - m_t)
    b_last = b_col[L - 1:L, :]
    log_in = b_last - b_col + i_col
    m_new = jnp.maximum(b_last + m_in, jnp.max(log_in, axis=0, keepdims=True))
    w_col = jnp.exp(log_in - m_new)
    decay = jnp.exp(b_last + m_in - m_new)
    return dict(eye=eye, r=r, c=c, f_col=f_col, m_t=m_t, w_t=w_t, dm=dm, m_new=m_new, w_col=w_col, decay=decay)


def _mlstm_fwd(qk, z0, gates, bias, name):
    t = qk.shape[0]
    nc, nh, L = t // CHUNK, 4, CHUNK
    scale = HD ** -0.5

    def body(q_ref, k_ref, v_ref, g_ref, b_ref, h_ref, cs_ref, ns_ref, ms_ref, c_s, n_s, m_s):
        @pl.when(pl.program_id(0) == 0)
        def _():
            c_s[...] = jnp.zeros_like(c_s)
            n_s[...] = jnp.zeros_like(n_s)
            m_s[...] = jnp.zeros_like(m_s)

        for hd in range(nh):
            sl = slice(hd * HD, (hd + 1) * HD)
            cm, nv, m_in = c_s[hd], n_s[hd], m_s[hd]
            cs_ref[hd] = cm
            ns_ref[hd] = nv
            ms_ref[hd] = jnp.broadcast_to(m_in, (1, HD))
            q, kh, v = q_ref[:, sl], k_ref[:, sl] * scale, v_ref[:, sl]
            G = _mlstm_gates(g_ref[hd], b_ref[hd], m_in)
            qb, kb, vb = q.astype(BF16), kh.astype(BF16), v.astype(BF16)
            sc = _nt(qb, kb) * G["dm"]
            num = _nn(sc.astype(BF16), vb) + G["w_t"] * _nn(qb, cm.astype(BF16))
            den = jnp.sum(sc, axis=1, keepdims=True) + G["w_t"] * jnp.sum(q * nv, axis=1, keepdims=True)
            h_ref[:, sl] = num / jnp.maximum(jnp.abs(den), jnp.exp(-G["m_t"]))
            wk = G["w_col"] * kh
            c_s[hd] = G["decay"] * cm + _tn(wk.astype(BF16), vb)
            n_s[hd] = G["decay"] * nv + jnp.sum(wk, axis=0, keepdims=True)
            m_s[hd] = G["m_new"]

    hspec = lambda blk: pl.BlockSpec((L, 512), lambda j: (j, blk))
    st = lambda r: pl.BlockSpec((nh, None, r, HD), lambda j: (0, j, 0, 0))
    return _pc(body, name, (nc,),
               [hspec(0), hspec(1), hspec(2), pl.BlockSpec((nh, L, 2), lambda j: (0, j, 0)),
                pl.BlockSpec((nh, 1, 2), lambda j: (0, 0, 0))],
               [hspec(0), st(HD), st(1), st(1)],
               [_sds((t, 512)), _sds((nh, nc, HD, HD)), _sds((nh, nc, 1, HD)), _sds((nh, nc, 1, HD))],
               scratch=[pltpu.VMEM((nh, HD, HD), F32), pltpu.VMEM((nh, 1, HD), F32), pltpu.VMEM((nh, 1, 1), F32)])(qk, qk, z0, gates, bias)


def _mlstm_bwd(qk, z0, gates, bias, cs, ns, ms, dh, name):
    t = qk.shape[0]
    nc, nh, L = t // CHUNK, 4, CHUNK
    scale = HD ** -0.5

    def body(q_ref, k_ref, v_ref, g_ref, b_ref, cs_ref, ns_ref, ms_ref, dh_ref, dq_ref, dk_ref, dv_ref, dg_ref, dc_s, dn_s):
        @pl.when(pl.program_id(0) == 0)
        def _():
            dc_s[...] = jnp.zeros_like(dc_s)
            dn_s[...] = jnp.zeros_like(dn_s)

        for hd in range(nh):
            one_head(hd, slice(hd * HD, (hd + 1) * HD), q_ref, k_ref, v_ref, g_ref, b_ref, cs_ref, ns_ref, ms_ref, dh_ref,
                     dq_ref, dk_ref, dv_ref, dg_ref, dc_s, dn_s)

    def one_head(hd, sl, q_ref, k_ref, v_ref, g_ref, b_ref, cs_ref, ns_ref, ms_ref, dh_ref, dq_ref, dk_ref, dv_ref, dg_ref, dc_s, dn_s):
        cm, nv, m_in = cs_ref[hd], ns_ref[hd], ms_ref[hd][:, 0:1]
        q, kh, v = q_ref[:, sl], k_ref[:, sl] * scale, v_ref[:, sl]
        G = _mlstm_gates(g_ref[hd], b_ref[hd], m_in)
        w_t, dmat, w_col, decay = G["w_t"], G["dm"], G["w_col"], G["decay"]
        qb, kb, vb, cb = q.astype(BF16), kh.astype(BF16), v.astype(BF16), cm.astype(BF16)
        s = _nt(qb, kb)
        sc = s * dmat
        scb = sc.astype(BF16)
        qc = _nn(qb, cb)
        qn = jnp.sum(q * nv, axis=1, keepdims=True)
        num = _nn(scb, vb) + w_t * qc
        den = jnp.sum(sc, axis=1, keepdims=True) + w_t * qn
        e_m = jnp.exp(-G["m_t"])
        dnm = jnp.maximum(jnp.abs(den), e_m)
        dh_ = dh_ref[:, sl]
        dnum = dh_ / dnm
        dden = jnp.where(jnp.abs(den) > e_m, -jnp.sum(dh_ * num, axis=1, keepdims=True) / (dnm * dnm) * jnp.sign(den), 0.0)
        dnumb = dnum.astype(BF16)
        dsc = _nt(dnumb, vb) + dden
        dv = _tn(scb, dnumb)
        wd = w_t * dnum
        wdb = wd.astype(BF16)
        ds = dsc * dmat
        dsb = ds.astype(BF16)
        dq = _nt(wdb, cb) + (w_t * dden) * nv + _nn(dsb, kb)
        dc_o = _tn(qb, wdb)
        dn_o = jnp.sum(q * (w_t * dden), axis=0, keepdims=True)
        dw = jnp.sum(dnum * qc, axis=1, keepdims=True) + dden * qn
        dkh = _tn(dsb, qb)
        dlogd = ds * s
        db_col = jnp.sum(dlogd, axis=1, keepdims=True) + dw * w_t
        csum = jnp.sum(dlogd, axis=0, keepdims=True)
        dcn, dnn = dc_s[hd], dn_s[hd]
        dcnb = dcn.astype(BF16)
        kdc = _nn(kb, dcnb)
        dws = jnp.sum(kdc * v, axis=1, keepdims=True) + jnp.sum(kh * dnn, axis=1, keepdims=True)
        dv = dv + w_col * kdc
        dkh = dkh + w_col * (_nt(vb, dcnb) + dnn)
        dlin = dws * w_col
        ddecay = jnp.sum(jnp.sum(dcn * cm, axis=1, keepdims=True), axis=0, keepdims=True) + jnp.sum(dnn * nv, axis=1, keepdims=True)
        dlast = ddecay * decay + jnp.sum(dlin, axis=0, keepdims=True)
        rows = lax.broadcasted_iota(jnp.int32, (L, 1), 0)
        db_col = db_col - dlin + jnp.where(rows == L - 1, dlast, 0.0)
        eye, r, c = G["eye"], G["r"], G["c"]
        di = dlin + jnp.sum(jnp.where(eye, csum, 0.0), axis=1, keepdims=True)
        db_row = jnp.sum(jnp.where(eye, db_col, 0.0), axis=0, keepdims=True) - csum
        dlogf = jnp.sum(jnp.where(c >= r, db_row, 0.0), axis=1, keepdims=True)
        dg_ref[hd, :, 0:1] = di
        dg_ref[hd, :, 1:2] = dlogf * (1.0 - _sigmoid(G["f_col"]))
        dq_ref[:, sl] = dq
        dk_ref[:, sl] = dkh * scale
        dv_ref[:, sl] = dv
        dc_s[hd] = decay * dcn + dc_o
        dn_s[hd] = decay * dnn + dn_o

    rv = lambda j: nc - 1 - j
    hspec = lambda blk: pl.BlockSpec((L, 512), lambda j: (rv(j), blk))
    st = lambda r: pl.BlockSpec((nh, None, r, HD), lambda j: (0, rv(j), 0, 0))
    gs = pl.BlockSpec((nh, L, 2), lambda j: (0, rv(j), 0))
    return _pc(body, name, (nc,),
               [hspec(0), hspec(1), hspec(2), gs, pl.BlockSpec((nh, 1, 2), lambda j: (0, 0, 0)),
                st(HD), st(1), st(1), hspec(0)],
               [hspec(0), hspec(0), hspec(0), gs],
               [_sds((t, 512)), _sds((t, 512)), _sds((t, 512)), _sds((nh, t, 2))],
               scratch=[pltpu.VMEM((nh, HD, HD), F32), pltpu.VMEM((nh, 1, HD), F32)])(qk, qk, z0, gates, bias, cs, ns, ms, dh)


def _hgrn_act(qb_, fb_, ib_, lg):
    lb = _sigmoid(lg[0:1, :] - lg[1:2, :])
    sg = _sigmoid(fb_)
    f = lb + (1.0 - lb) * sg
    return lb, sg, f, _silu(qb_), (1.0 - lb) * (1.0 - sg), _silu(ib_), _cumsum_rows(jnp.log(f))


HG_SUB = 16


def _hgrn_offdiag(q, k, b, r0):
    beta = b[r0 - 1:r0, :]
    e1 = jnp.exp(b[r0:r0 + HG_SUB, :] - beta)
    e2 = jnp.where(_rows_of(b) < r0, jnp.exp(jnp.minimum(beta - b, 0.0)), 0.0)
    return q[r0:r0 + HG_SUB, :] * e1, k * e2, e1, e2


def _hgrn_fwd(z0, lbl, name):
    t = z0.shape[0]
    nc, nh, L = t // CHUNK, 4, CHUNK

    def body(q_ref, f_ref, i_ref, l_ref, o_ref, ss_ref, st_s):
        @pl.when(pl.program_id(0) == 0)
        def _():
            st_s[...] = jnp.zeros_like(st_s)

        for hd in range(nh):
            sl = slice(hd * HD, (hd + 1) * HD)
            st = st_s[hd]
            ss_ref[hd] = st
            _, _, _, q, k, v, b = _hgrn_act(q_ref[:, sl], f_ref[:, sl], i_ref[:, sl], l_ref[:, sl])
            o = _nt((q * jnp.exp(b)).astype(BF16), st.astype(BF16))
            sub = _rows_of(b) & (HG_SUB - 1)
            o = o + jnp.sum(q * k, axis=1, keepdims=True) * v
            for dl in range(1, HG_SUB):
                e = jnp.exp(jnp.where(sub >= dl, b - pltpu.roll(b, dl, 0), NEG))
                a = jnp.sum(q * pltpu.roll(k, dl, 0) * e, axis=1, keepdims=True)
                o = o + a * pltpu.roll(v, dl, 0)
            o_ref[:, sl] = o
            vb = v.astype(BF16)
            for i in range(1, L // HG_SUB):
                r0 = i * HG_SUB
                qt, kt, _, _ = _hgrn_offdiag(q, k, b, r0)
                a = _nt(qt.astype(BF16), kt.astype(BF16))
                o_ref[r0:r0 + HG_SUB, sl] += _nn(a.astype(BF16), vb)
            bl = b[L - 1:L, :]
            st_s[hd] = st * jnp.exp(bl) + _tn(v.astype(BF16), (k * jnp.exp(bl - b)).astype(BF16))

    hspec = lambda blk: pl.BlockSpec((L, 512), lambda j: (j, blk))
    return _pc(body, name, (nc,),
               [hspec(4), hspec(5), hspec(6), pl.BlockSpec((2, 512), lambda j: (0, 0))],
               [hspec(0), pl.BlockSpec((nh, None, HD, HD), lambda j: (0, j, 0, 0))],
               [_sds((t, 512)), _sds((nh, nc, HD, HD))],
               scratch=[pltpu.VMEM((nh, HD, HD), F32)])(z0, z0, z0, lbl)


def _hgrn_bwd(z0, lbl, ss, do, name):
    t = z0.shape[0]
    nc, nh, L = t // CHUNK, 4, CHUNK

    def body(q_ref, f_ref, i_ref, l_ref, ss_ref, do_ref, dq_ref, df_ref, di_ref, dl_ref, dst_s, dlb_s, dq_a, dk_a, dv_a, db_a):
        @pl.when(pl.program_id(0) == 0)
        def _():
            dst_s[...] = jnp.zeros_like(dst_s)
            dlb_s[...] = jnp.zeros_like(dlb_s)

        for hd in range(nh):
            one_head(hd, slice(hd * HD, (hd + 1) * HD), q_ref, f_ref, i_ref, l_ref, ss_ref, do_ref, dq_ref, df_ref, di_ref, dl_ref,
                     dst_s, dlb_s, dq_a.at[hd], dk_a.at[hd], dv_a.at[hd], db_a.at[hd])

    def one_head(hd, sl, q_ref, f_ref, i_ref, l_ref, ss_ref, do_ref, dq_ref, df_ref, di_ref, dl_ref, dst_s, dlb_s, dq_a, dk_a, dv_a, db_a):
        st = ss_ref[hd]
        qp, fp, ip = q_ref[:, sl], f_ref[:, sl], i_ref[:, sl]
        lb, sg, f, q, k, v, b = _hgrn_act(qp, fp, ip, l_ref[:, sl])
        do_ = do_ref[:, sl]
        dob, stb = do_.astype(BF16), st.astype(BF16)
        eb = jnp.exp(b)
        qe = q * eb
        dqe = _nn(dob, stb)
        dst_o = _tn(dob, qe.astype(BF16))
        dq = dqe * eb
        db = dqe * qe
        rows = _rows_of(b)
        sub = rows & (HG_SUB - 1)
        p0 = jnp.sum(do_ * v, axis=1, keepdims=True)
        dq = dq + p0 * k
        dk = p0 * q
        dv = jnp.sum(q * k, axis=1, keepdims=True) * do_
        for dl in range(1, HG_SUB):
            up = L - dl
            kd, vd = pltpu.roll(k, dl, 0), pltpu.roll(v, dl, 0)
            e = jnp.exp(jnp.where(sub >= dl, b - pltpu.roll(b, dl, 0), NEG))
            a = jnp.sum(q * kd * e, axis=1, keepdims=True)
            p = jnp.sum(do_ * vd, axis=1, keepdims=True) * e
            dq = dq + p * kd
            dkd = p * q
            dbb = dkd * kd
            dv = dv + pltpu.roll(a * do_, up, 0)
            dk = dk + pltpu.roll(dkd, up, 0)
            db = db + dbb - pltpu.roll(dbb, up, 0)
        dq_a[...], dk_a[...], dv_a[...], db_a[...] = dq, dk, dv, db
        vb = v.astype(BF16)
        for i in range(1, L // HG_SUB):
            r0 = i * HG_SUB
            blk = slice(r0, r0 + HG_SUB)
            qt, kt, e1, e2 = _hgrn_offdiag(q, k, b, r0)
            qtb, ktb, dob_i = qt.astype(BF16), kt.astype(BF16), dob[blk, :]
            a = _nt(qtb, ktb).astype(BF16)
            da = _nt(dob_i, vb).astype(BF16)
            dv_a[...] += _tn(a, dob_i)
            dqt = _nn(da, ktb)
            dkt = _tn(da, qtb)
            dq_a[blk, :] += dqt * e1
            t1, t2 = dqt * qt, dkt * kt
            db_a[blk, :] += t1
            dk_a[...] += dkt * e2
            db_a[...] -= t2
            db_a[r0 - 1:r0, :] += jnp.sum(t2, axis=0, keepdims=True) - jnp.sum(t1, axis=0, keepdims=True)
        dq, dk, dv, db = dq_a[...], dk_a[...], dv_a[...], db_a[...]
        dstn = dst_s[hd]
        dstnb = dstn.astype(BF16)
        bl = b[L - 1:L, :]
        ebl = jnp.exp(bl)
        kdec_e = jnp.exp(bl - b)
        kdec = k * kdec_e
        dbl = jnp.sum(dstn * st, axis=0, keepdims=True) * ebl
        dv = dv + _nt(kdec.astype(BF16), dstnb)
        dkdec = _nn(v.astype(BF16), dstnb)
        dk = dk + dkdec * kdec_e
        dx = dkdec * kdec
        dbl = dbl + jnp.sum(dx, axis=0, keepdims=True)
        db = db - dx + jnp.where(rows == L - 1, dbl, 0.0)
        dst_s[hd] = dstn * ebl + dst_o
        dg = _rcumsum_rows(db)
        dfk = dg / f - dk
        dq_ref[:, sl] = (dq * _dsilu(qp)).astype(BF16)
        di_ref[:, sl] = (dv * _dsilu(ip)).astype(BF16)
        df_ref[:, sl] = (dfk * (1.0 - lb) * sg * (1.0 - sg)).astype(BF16)
        dlb_s[hd] += jnp.sum(dfk * (1.0 - sg), axis=0, keepdims=True)

        @pl.when(pl.program_id(0) == nc - 1)
        def _():
            dl0 = dlb_s[hd] * lb * (1.0 - lb)
            dl_ref[0:1, sl] = dl0
            dl_ref[1:2, sl] = -dl0

    rv = lambda j: nc - 1 - j
    hspec = lambda blk: pl.BlockSpec((L, 512), lambda j: (rv(j), blk))
    return _pc(body, name, (nc,),
               [hspec(4), hspec(5), hspec(6), pl.BlockSpec((2, 512), lambda j: (0, 0)),
                pl.BlockSpec((nh, None, HD, HD), lambda j: (0, rv(j), 0, 0)), hspec(0)],
               [hspec(0), hspec(0), hspec(0), pl.BlockSpec((2, 512), lambda j: (0, 0))],
               [_sds((t, 512), BF16), _sds((t, 512), BF16), _sds((t, 512), BF16), _sds((2, 512))],
               scratch=[pltpu.VMEM((nh, HD, HD), F32), pltpu.VMEM((nh, 1, HD), F32)] + [pltpu.VMEM((nh, L, HD), F32)] * 4)(z0, z0, z0, lbl, ss, do)


def _post0_fwd(hm, hh, z0, na, nb, w, h0, name, bm=512):
    t = h0.shape[0]
    bm = min(bm, t)

    def body(hm_ref, hh_ref, oa_ref, gb_ref, na_ref, nb_ref, w_ref, h_ref, o_ref, y_ref):
        for hd in range(4):
            sl = slice(hd * HD, (hd + 1) * HD)
            pa = _sigmoid(oa_ref[:, sl]) * hm_ref[:, sl]
            y_ref[:, sl] = (pa * _rstd(pa) * na_ref[:, sl]).astype(BF16)
            xb = hh_ref[:, sl]
            y_ref[:, 512 + hd * HD:512 + (hd + 1) * HD] = (xb * _rstd(xb) * nb_ref[:, sl] * _silu(gb_ref[:, sl])).astype(BF16)
        o_ref[...] = h_ref[...] + _nn(y_ref[...], w_ref[...])

    row = lambda wd, c: pl.BlockSpec((bm, wd), lambda i: (i, c))
    vec = lambda wd: pl.BlockSpec((1, wd), lambda i: (0, 0))
    return _pc(body, name, (t // bm,),
               [row(512, 0), row(512, 0), row(512, 3), row(512, 7), vec(512), vec(512),
                pl.BlockSpec((D, D), lambda i: (0, 0)), row(D, 0)],
               [row(D, 0), row(D, 0)], [_sds((t, D)), _sds((t, D), BF16)])(hm, hh, z0, z0, na, nb, w, h0)


def _post0_bwd(dh1, w, hm, hh, z0, na, nb, name, bm=512):
    t = dh1.shape[0]
    bm = min(bm, t)

    def body(dh_ref, w_ref, hm_ref, hh_ref, oa_ref, gb_ref, na_ref, nb_ref, dhm_ref, dhh_ref, doa_ref, dgb_ref, dna_ref, dnb_ref):
        @pl.when(pl.program_id(0) == 0)
        def _():
            dna_ref[...] = jnp.zeros_like(dna_ref)
            dnb_ref[...] = jnp.zeros_like(dnb_ref)

        dy = _nt(dh_ref[...].astype(BF16), w_ref[...])
        for hd in range(4):
            sl = slice(hd * HD, (hd + 1) * HD)
            hm_, oa = hm_ref[:, sl], oa_ref[:, sl]
            sg = _sigmoid(oa)
            dpa, dgr = _rms_bwd(dy[:, sl], sg * hm_, na_ref[:, sl])
            dna_ref[:, sl] += jnp.sum(dgr, axis=0, keepdims=True)
            doa_ref[:, sl] = (dpa * hm_ * sg * (1.0 - sg)).astype(BF16)
            dhm_ref[:, sl] = dpa * sg
            xb, gb, nbv = hh_ref[:, sl], gb_ref[:, sl], nb_ref[:, sl]
            dyb = dy[:, 512 + hd * HD:512 + (hd + 1) * HD]
            dgb_ref[:, sl] = (dyb * (xb * _rstd(xb) * nbv) * _dsilu(gb)).astype(BF16)
            dxb, dgr2 = _rms_bwd(dyb * _silu(gb), xb, nbv)
            dnb_ref[:, sl] += jnp.sum(dgr2, axis=0, keepdims=True)
            dhh_ref[:, sl] = dxb

    row = lambda wd, c: pl.BlockSpec((bm, wd), lambda i: (i, c))
    vec = lambda wd: pl.BlockSpec((1, wd), lambda i: (0, 0))
    return _pc(body, name, (t // bm,),
               [row(D, 0), pl.BlockSpec((D, D), lambda i: (0, 0)), row(512, 0), row(512, 0), row(512, 3), row(512, 7),
                vec(512), vec(512)],
               [row(512, 0), row(512, 0), row(512, 0), row(512, 0), vec(512), vec(512)],
               [_sds((t, 512)), _sds((t, 512)), _sds((t, 512), BF16), _sds((t, 512), BF16), _sds((1, 512)), _sds((1, 512))],
               )(dh1, w, hm, hh, z0, z0, na, nb)


def _memkv_fwd(mem, g, wkv_s, name):
    m = mem.shape[0]

    def body(x_ref, g_ref, w_ref, kv_ref, mn_ref):
        x = x_ref[...]
        mn = (x * _rstd(x) * g_ref[...]).astype(BF16)
        mn_ref[...] = mn
        kv_ref[...] = _nn(mn, w_ref[...])

    return _pc(body, name, (4,),
               [pl.BlockSpec((m, D), lambda k: (0, 0)), pl.BlockSpec((1, D), lambda k: (0, 0)),
                pl.BlockSpec((None, D, 512), lambda k: (k, 0, 0))],
               [pl.BlockSpec((m, 512), lambda k: (0, k)), pl.BlockSpec((m, D), lambda k: (0, 0))],
               [_sds((m, 2048)), _sds((m, D), BF16)])(mem, g, wkv_s)


def _memkv_bwd(dkv, wkv_s, mem, g, name):
    m = mem.shape[0]

    def body(d_ref, w_ref, x_ref, g_ref, dg_ref, acc):
        k = pl.program_id(0)

        @pl.when(k == 0)
        def _():
            acc[...] = jnp.zeros_like(acc)

        acc[...] += _nt(d_ref[...].astype(BF16), w_ref[...])

        @pl.when(k == 3)
        def _():
            _, dgr = _rms_bwd(acc[...], x_ref[...], g_ref[...])
            dg_ref[...] = jnp.sum(dgr, axis=0, keepdims=True)

    return _pc(body, name, (4,),
               [pl.BlockSpec((m, 512), lambda k: (0, k)), pl.BlockSpec((None, D, 512), lambda k: (k, 0, 0)),
                pl.BlockSpec((m, D), lambda k: (0, 0)), pl.BlockSpec((1, D), lambda k: (0, 0))],
               pl.BlockSpec((1, D), lambda k: (0, 0)), _sds((1, D)), scratch=[pltpu.VMEM((m, D), F32)])(dkv, wkv_s, mem, g)


def _xattn_probs(qh, kh):
    s = _nt(qh, kh) * (XD ** -0.5)
    p = jnp.exp(s - jnp.max(s, axis=1, keepdims=True))
    return p / jnp.sum(p, axis=1, keepdims=True)


def _xattn_fwd(q, kv, wo, h1, name, bm=512):
    t, m = q.shape[0], kv.shape[0]
    bm = min(bm, t)

    def body(q_ref, k_ref, v_ref, w_ref, h_ref, out_ref, o_ref):
        for hd in range(D // XD):
            sl = slice(hd * XD, (hd + 1) * XD)
            p = _xattn_probs(q_ref[:, sl].astype(BF16), k_ref[:, sl].astype(BF16))
            o_ref[:, sl] = _nn(p.astype(BF16), v_ref[:, sl].astype(BF16)).astype(BF16)
        out_ref[...] = h_ref[...] + _nn(o_ref[...], w_ref[...])

    row = pl.BlockSpec((bm, D), lambda i: (i, 0))
    return _pc(body, name, (t // bm,),
               [row, pl.BlockSpec((m, D), lambda i: (0, 0)), pl.BlockSpec((m, D), lambda i: (0, 1)),
                pl.BlockSpec((D, D), lambda i: (0, 0)), row],
               [row, row], [_sds((t, D)), _sds((t, D), BF16)])(q, kv, kv, wo, h1)


def _xattn_bwd(dh2, q, kv, wo, name, bm=512):
    t, m = q.shape[0], kv.shape[0]
    bm = min(bm, t)

    def body(dh_ref, q_ref, k_ref, v_ref, w_ref, dq_ref, dkv_ref):
        @pl.when(pl.program_id(0) == 0)
        def _():
            dkv_ref[...] = jnp.zeros_like(dkv_ref)

        d_o = _nt(dh_ref[...].astype(BF16), w_ref[...])
        for hd in range(D // XD):
            sl = slice(hd * XD, (hd + 1) * XD)
            qh, kh, vh = q_ref[:, sl].astype(BF16), k_ref[:, sl].astype(BF16), v_ref[:, sl].astype(BF16)
            p = _xattn_probs(qh, kh)
            dob = d_o[:, sl].astype(BF16)
            dp = _nt(dob, vh)
            dkv_ref[:, D + hd * XD:D + (hd + 1) * XD] += _tn(p.astype(BF16), dob)
            ds = (p * (dp - jnp.sum(dp * p, axis=1, keepdims=True)) * (XD ** -0.5)).astype(BF16)
            dq_ref[:, sl] = _nn(ds, kh).astype(BF16)
            dkv_ref[:, sl] += _tn(ds, qh)

    row = pl.BlockSpec((bm, D), lambda i: (i, 0))
    return _pc(body, name, (t // bm,),
               [row, row, pl.BlockSpec((m, D), lambda i: (0, 0)), pl.BlockSpec((m, D), lambda i: (0, 1)),
                pl.BlockSpec((D, D), lambda i: (0, 0))],
               [row, pl.BlockSpec((m, 2 * D), lambda i: (0, 0))],
               [_sds((t, D), BF16), _sds((m, 2 * D))])(dh2, q, kv, kv, wo)


NH1 = 8
FOX_BM = 512
FOX_BQ = 512
FOX_BK = 512
FOX_HEADS_PER_STEP = 2


def _foxprep_fwd(z1, qg, kg, fbp, name):
    t = z1.shape[0]
    bm = min(FOX_BM, t)

    def body(q_ref, k_ref, v_ref, f_ref, qg_ref, kg_ref, fb_ref, qn_ref, kn_ref, vb_ref, c_ref, carry):
        @pl.when(pl.program_id(0) == 0)
        def _():
            carry[...] = jnp.zeros_like(carry)

        for hd in range(NH1):
            sl = slice(hd * HD, (hd + 1) * HD)
            x = q_ref[:, sl]
            qn_ref[:, sl] = (x * _rstd(x) * qg_ref[...] * FOX_QSCALE).astype(BF16)
            x = k_ref[:, sl]
            kn_ref[:, sl] = (x * _rstd(x) * kg_ref[...]).astype(BF16)
        vb_ref[...] = v_ref[...].astype(BF16)
        c = carry[...] + _cumsum_rows(_log_sigmoid(f_ref[...] + fb_ref[...]))
        c_ref[...] = c
        carry[...] = c[bm - 1:bm, :]

    row = lambda c: pl.BlockSpec((bm, D), lambda i: (i, c))
    lane = pl.BlockSpec((bm, HD), lambda i: (i, 4 * D // HD))
    vec = pl.BlockSpec((1, HD), lambda i: (0, 0))
    return _pc(body, name, (t // bm,), [row(0), row(1), row(2), lane, vec, vec, vec],
               [row(0), row(0), row(0), pl.BlockSpec((bm, HD), lambda i: (i, 0))],
               [_sds((t, D), BF16), _sds((t, D), BF16), _sds((t, D), BF16), _sds((t, HD))],
               scratch=[pltpu.VMEM((1, HD), F32)])(z1, z1, z1, z1, qg, kg, fbp)


def _foxprep_bwd(dqn, dkn, z1, qg, kg, fbp, dc, name):
    t = z1.shape[0]
    bm = min(FOX_BM, t)
    nb = t // bm

    def body(dqn_ref, dkn_ref, q_ref, k_ref, f_ref, qg_ref, kg_ref, fb_ref, dc_ref,
             dq_ref, dk_ref, df_ref, dqg_ref, dkg_ref, dfb_ref, carry):
        @pl.when(pl.program_id(0) == 0)
        def _():
            carry[...] = jnp.zeros_like(carry)
            dqg_ref[...] = jnp.zeros_like(dqg_ref)
            dkg_ref[...] = jnp.zeros_like(dkg_ref)
            dfb_ref[...] = jnp.zeros_like(dfb_ref)

        for hd in range(NH1):
            sl = slice(hd * HD, (hd + 1) * HD)
            dx, dgr = _rms_bwd(dqn_ref[:, sl] * (HD ** -0.5), q_ref[:, sl], qg_ref[...])
            dq_ref[:, sl] = dx.astype(BF16)
            dqg_ref[...] += jnp.sum(dgr, axis=0, keepdims=True)
            dx, dgr = _rms_bwd(dkn_ref[:, sl], k_ref[:, sl], kg_ref[...])
            dk_ref[:, sl] = dx.astype(BF16)
            dkg_ref[...] += jnp.sum(dgr, axis=0, keepdims=True)
        dc_ = dc_ref[...]
        dlogf = _rcumsum_rows(dc_) + carry[...]
        carry[...] += jnp.sum(dc_, axis=0, keepdims=True)
        lanes = lax.broadcasted_iota(jnp.int32, dc_.shape, 1)
        df = jnp.where(lanes < NH1, dlogf * (1.0 - _sigmoid(f_ref[...] + fb_ref[...])), 0.0)
        df_ref[...] = df.astype(BF16)
        dfb_ref[...] += jnp.sum(df, axis=0, keepdims=True)

    rv = lambda i: nb - 1 - i
    row = lambda c: pl.BlockSpec((bm, D), lambda i: (rv(i), c))
    lane = lambda c: pl.BlockSpec((bm, HD), lambda i: (rv(i), c))
    vec = pl.BlockSpec((1, HD), lambda i: (0, 0))
    return _pc(body, name, (nb,), [row(0), row(0), row(0), row(1), lane(4 * D // HD), vec, vec, vec, lane(0)],
               [row(0), row(0), lane(0), vec, vec, vec],
               [_sds((t, D), BF16), _sds((t, D), BF16), _sds((t, HD), BF16), _sds((1, HD)), _sds((1, HD)), _sds((1, HD))],
               scratch=[pltpu.VMEM((1, HD), F32)])(dqn, dkn, z1, z1, z1, qg, kg, fbp, dc)


LOG2E = 1.4426950408889634
FOX_QSCALE = HD ** -0.5 * LOG2E


def _fox_scores(q, k, ck, i, j, bq, bk, masked):
    s = _nt(q, k) - ck
    if not masked:
        return s, None
    rows = i * bq + lax.broadcasted_iota(jnp.int32, s.shape, 0)
    cols = j * bk + lax.broadcasted_iota(jnp.int32, s.shape, 1)
    return s, cols <= rows


def _fox_block_kind(i, j, bq, bk):
    active = j * bk < (i + 1) * bq
    full = (j + 1) * bk <= i * bq + 1
    return full, active & jnp.logical_not(full)


def _fox_fwd(qn, kn, vb, crow, name):
    t = qn.shape[0]
    bq, bk, G = min(FOX_BQ, t), min(FOX_BK, t), FOX_HEADS_PER_STEP
    nq, nk = t // bq, t // bk

    def body(q_ref, k_ref, v_ref, ck_ref, o_ref, lse_ref, m_s, l_s, acc):
        i, j = pl.program_id(1), pl.program_id(2)

        @pl.when(j == 0)
        def _():
            m_s[...] = jnp.full_like(m_s, NEG)
            l_s[...] = jnp.zeros_like(l_s)
            acc[...] = jnp.zeros_like(acc)

        def step(masked):
            for g in range(G):
                sl = slice(g * HD, (g + 1) * HD)
                s, ok = _fox_scores(q_ref[:, sl], k_ref[:, sl], ck_ref[g], i, j, bq, bk, masked)
                if masked:
                    s = jnp.where(ok, s, NEG)
                m_new = jnp.maximum(m_s[g], jnp.max(s, axis=1, keepdims=True))
                alpha = jnp.exp2(m_s[g] - m_new)
                p = jnp.exp2(s - m_new)
                l_s[g] = alpha * l_s[g] + jnp.sum(p, axis=1, keepdims=True)
                acc[:, sl] = alpha * acc[:, sl] + _nn(p.astype(BF16), v_ref[:, sl])
                m_s[g] = m_new

        full, part = _fox_block_kind(i, j, bq, bk)
        pl.when(full)(lambda: step(False))
        pl.when(part)(lambda: step(True))

        @pl.when(j == nk - 1)
        def _():
            for g in range(G):
                sl = slice(g * HD, (g + 1) * HD)
                o_ref[:, sl] = acc[:, sl] / l_s[g]
                lse_ref[g] = m_s[g] + jnp.log2(l_s[g])

    kj = lambda i, j: jnp.minimum(j, ((i + 1) * bq - 1) // bk)
    kmap = lambda h, i, j: (kj(i, j), h)
    return _pc(body, name, (NH1 // G, nq, nk),
               [pl.BlockSpec((bq, G * HD), lambda h, i, j: (i, h)), pl.BlockSpec((bk, G * HD), kmap),
                pl.BlockSpec((bk, G * HD), kmap), pl.BlockSpec((G, 1, bk), lambda h, i, j: (h, 0, kj(i, j)))],
               [pl.BlockSpec((bq, G * HD), lambda h, i, j: (i, h)), pl.BlockSpec((G, bq, 1), lambda h, i, j: (h, i, 0))],
               [_sds((t, D)), _sds((NH1, t, 1))],
               scratch=[pltpu.VMEM((G, bq, 1), F32), pltpu.VMEM((G, bq, 1), F32), pltpu.VMEM((bq, G * HD), F32)])(qn, kn, vb, crow)


def _fox_bwd(qn, kn, vb, crow, lse, delta, do, name):
    t = qn.shape[0]
    bq, bk, G = min(FOX_BQ, t), min(FOX_BK, t), FOX_HEADS_PER_STEP
    nq, nk = t // bq, t // bk

    def body(q_ref, k_ref, v_ref, ck_ref, lse_ref, dl_ref, do_ref, dq_ref, dk_ref, dv_ref, dc_ref, dcq_ref, dk_s, dv_s, dc_s):
        j, i = pl.program_id(1), pl.program_id(2)

        @pl.when(i == 0)
        def _():
            dk_s[...] = jnp.zeros_like(dk_s)
            dv_s[...] = jnp.zeros_like(dv_s)
            dc_s[...] = jnp.zeros_like(dc_s)

        @pl.when((i == 0) & (j == 0))
        def _():
            dq_ref[...] = jnp.zeros_like(dq_ref)
            dcq_ref[...] = jnp.zeros_like(dcq_ref)

        def step(masked):
            rows = pl.ds(pl.multiple_of(i * bq, bq), bq)
            for g in range(G):
                sl = slice(g * HD, (g + 1) * HD)
                q, k = q_ref[:, sl], k_ref[:, sl]
                s, ok = _fox_scores(q, k, ck_ref[g], i, j, bq, bk, masked)
                if masked:
                    s = jnp.where(ok, s, NEG)
                p = jnp.exp2(s - lse_ref[g])
                dob = do_ref[:, sl]
                dv_s[:, sl] += _tn(p.astype(BF16), dob)
                ds = p * (_nt(dob, v_ref[:, sl]) - dl_ref[g])
                dsb = ds.astype(BF16)
                dq_ref[rows, sl] += _nn(dsb, k)
                dk_s[:, sl] += _tn(dsb, q)
                dc_s[g] -= jnp.sum(ds, axis=0, keepdims=True)
                dcq_ref[g, rows, :] += jnp.sum(ds, axis=1, keepdims=True)

        full, part = _fox_block_kind(i, j, bq, bk)
        pl.when(full)(lambda: step(False))
        pl.when(part)(lambda: step(True))

        @pl.when(i == nq - 1)
        def _():
            dk_ref[...] = dk_s[...] * (1.0 / LOG2E)
            dv_ref[...] = dv_s[...]
            dc_ref[...] = dc_s[...]

    qi = lambda i, j: jnp.maximum(i, (j * bk) // bq)
    qmap = lambda h, j, i: (qi(i, j), h)
    c3map = lambda h, j, i: (h, qi(i, j), 0)
    kspec = pl.BlockSpec((bk, G * HD), lambda h, j, i: (j, h))
    return _pc(body, name, (NH1 // G, nk, nq),
               [pl.BlockSpec((bq, G * HD), qmap), kspec, kspec,
                pl.BlockSpec((G, 1, bk), lambda h, j, i: (h, 0, j)), pl.BlockSpec((G, bq, 1), c3map),
                pl.BlockSpec((G, bq, 1), c3map), pl.BlockSpec((bq, G * HD), qmap)],
               [pl.BlockSpec((t, G * HD), lambda h, j, i: (0, h)), kspec, kspec, pl.BlockSpec((G, 1, bk), lambda h, j, i: (h, 0, j)),
                pl.BlockSpec((G, t, 1), lambda h, j, i: (h, 0, 0))],
               [_sds((t, D)), _sds((t, D)), _sds((t, D)), _sds((NH1, 1, t)), _sds((NH1, t, 1))],
               scratch=[pltpu.VMEM((bk, G * HD), F32), pltpu.VMEM((bk, G * HD), F32), pltpu.VMEM((G, 1, bk), F32)],
               )(qn, kn, vb, crow, lse, delta, do)


def _post1_fwd(o, z1, w, h3, name, bm=512):
    t = o.shape[0]
    bm = min(bm, t)

    def body(o_ref, g_ref, w_ref, h_ref, out_ref, og_ref):
        og_ref[...] = (o_ref[...] * _sigmoid(g_ref[...])).astype(BF16)
        out_ref[...] = h_ref[...] + _nn(og_ref[...], w_ref[...])

    row = lambda c: pl.BlockSpec((bm, D), lambda i: (i, c))
    return _pc(body, name, (t // bm,), [row(0), row(3), pl.BlockSpec((D, D), lambda i: (0, 0)), row(0)],
               [row(0), row(0)], [_sds((t, D)), _sds((t, D), BF16)])(o, z1, w, h3)


def _post1_bwd(dh4, w, o, z1, name, bm=512):
    t = o.shape[0]
    bm = min(bm, t)

    def body(dh_ref, w_ref, o_ref, g_ref, do_ref, dg_ref, dl_ref):
        d_og = _nt(dh_ref[...].astype(BF16), w_ref[...])
        o_, sg = o_ref[...], _sigmoid(g_ref[...])
        dob = (d_og * sg).astype(BF16)
        do_ref[...] = dob
        dg_ref[...] = (d_og * o_ * sg * (1.0 - sg)).astype(BF16)
        prod = dob.astype(F32) * o_
        for hd in range(NH1):
            dl_ref[hd] = jnp.sum(prod[:, hd * HD:(hd + 1) * HD], axis=1, keepdims=True)

    row = lambda c: pl.BlockSpec((bm, D), lambda i: (i, c))
    return _pc(body, name, (t // bm,), [row(0), pl.BlockSpec((D, D), lambda i: (0, 0)), row(0), row(3)],
               [row(0), row(0), pl.BlockSpec((NH1, bm, 1), lambda i: (0, i, 0))],
               [_sds((t, D), BF16), _sds((t, D), BF16), _sds((NH1, t, 1))])(dh4, w, o, z1)


def _final(h, g, tgt, name, bm=512):
    t = h.shape[0]
    bm = min(bm, t)

    def body(h_ref, g_ref, t_ref, l_ref, dh_ref, dg_ref):
        @pl.when(pl.program_id(0) == 0)
        def _():
            l_ref[...] = jnp.zeros_like(l_ref)
            dg_ref[...] = jnp.zeros_like(dg_ref)

        x, gv = h_ref[...], g_ref[...]
        r = _rstd(x)
        xh = x * r
        e = xh * gv - t_ref[...]
        l_ref[...] += 0.5 * jnp.sum(jnp.mean(e * e, axis=1, keepdims=True), axis=0, keepdims=True)
        dy = e * (1.0 / D)
        dg_ref[...] += jnp.sum(dy * xh, axis=0, keepdims=True)
        dxh = dy * gv
        dh_ref[...] = r * (dxh - xh * jnp.mean(dxh * xh, axis=1, keepdims=True))

    row = pl.BlockSpec((bm, D), lambda i: (i, 0))
    vec = pl.BlockSpec((1, D), lambda i: (0, 0))
    return _pc(body, name, (t // bm,), [row, vec, row], [pl.BlockSpec((1, HD), lambda i: (0, 0)), row, vec],
               [_sds((1, HD)), _sds((t, D)), _sds((1, D))])(h, g, tgt)


def _adam(w, g, m, v, name):
    r, c = w.shape
    br = min(r, 256)

    def body(w_ref, g_ref, m_ref, v_ref, d_ref, mo_ref, vo_ref):
        gv = g_ref[...]
        mn = ADAM_B1 * m_ref[...] + (1.0 - ADAM_B1) * gv
        vn = ADAM_B2 * v_ref[...] + (1.0 - ADAM_B2) * jnp.square(gv)
        m_hat = mn / (1.0 - ADAM_B1 ** ADAM_STEP)
        v_hat = vn / (1.0 - ADAM_B2 ** ADAM_STEP)
        d_ref[...] = -ADAM_LR * (m_hat / (jnp.sqrt(v_hat) + ADAM_EPS) + ADAM_WD * w_ref[...])
        mo_ref[...] = mn
        vo_ref[...] = vn

    blk = pl.BlockSpec((br, c), lambda i: (i, 0))
    return _pc(body, name, (r // br,), [blk] * 4, [blk] * 3, [_sds((r, c))] * 3)(w, g, m, v)


ZW = 4224
GATE0 = 4096


def _pack_w_in0(w):
    return jnp.concatenate([w[:, :2048], w[:, 2056:], w[:, 2048:2056], jnp.zeros((w.shape[0], ZW - 4104), w.dtype)], axis=1)


def _unpack_w_in0(g):
    return jnp.concatenate([g[:, :2048], g[:, GATE0:GATE0 + 8], g[:, 2048:GATE0]], axis=1)


def _pack_w_in1(w):
    return jnp.concatenate([w, jnp.zeros((w.shape[0], ZW - 4104), w.dtype)], axis=1)


def _unpack_w_in1(g):
    return g[:, :4104]


def _local_step(x, mem, tgt, W, S, late_weights=None):
    t = x.shape[0]
    row = lambda v: v.reshape(1, -1)
    G = {}

    kv, mn = _memkv_fwd(mem, row(S["mem_norm_g"]), W["wkv_s"], "memkv_fwd")
    z0, u0 = _norm_mm(x, S["norm_mix_g"][0:1], W["w_in0"], "in0_fwd")
    qk = _conv_fwd(z0, S["conv_w"], "conv_fwd")
    g8 = z0[:, GATE0:GATE0 + 8]
    gates3 = jnp.stack([g8[:, :4].T, g8[:, 4:].T], axis=-1)
    gb = S["gate_b"]
    bias3 = jnp.stack([gb[0, :4], gb[0, 4:]], axis=-1)[:, None, :]
    hm, cs, ns, ms = _mlstm_fwd(qk, z0, gates3, bias3, "mlstm_fwd")
    hh, ss = _hgrn_fwd(z0, S["lb_logits"], "hgrn_fwd")
    if late_weights is not None:
        W = {**W, **late_weights(hh)}
    h1, y0 = _post0_fwd(hm, hh, z0, S["mlstm_norm_g"], S["hgrn_norm_g"], W["w_out0"], x, "post0_fwd")

    def xattn_mlp_fwd(h, l):
        q, ux = _norm_mm(h, S["norm_xattn_g"][l:l + 1], W["wq"][l], f"xq{l}_fwd")
        h2, ox = _xattn_fwd(q, kv, W["wo"][l], h, f"xattn{l}_fwd")
        h3, a, um = _mlp_fwd(h2, S["norm_mlp_g"][l:l + 1], W["w1s"], W["w2"], l, f"mlp{l}_fwd")
        return h3, (h, q, ux, ox, h2, a, um)

    h3, sv0 = xattn_mlp_fwd(h1, 0)
    z1, u1 = _norm_mm(h3, S["norm_mix_g"][1:2], W["w_in1"], "in1_fwd")
    fbp = jnp.pad(S["c_fgate_b"], ((0, 0), (0, HD - NH1)))
    qn, kn, vb, c = _foxprep_fwd(z1, S["c_qnorm_g"], S["c_knorm_g"], fbp, "foxprep_fwd")
    crow = (c[:, :NH1] * LOG2E).T[:, None, :]
    o1, lse = _fox_fwd(qn, kn, vb, crow, "fox_fwd")
    h4, og = _post1_fwd(o1, z1, W["w_out1"], h3, "post1_fwd")
    h6, sv1 = xattn_mlp_fwd(h4, 1)
    lossp, dh, G["final_norm_g"] = _final(h6, row(S["final_norm_g"]), tgt, "final")

    dkv = None
    dgx, dgm, dwq, dwo, dw1, dw2 = [None, None], [None, None], [None, None], [None, None], [None, None], [None, None]

    def xattn_mlp_bwd(dh, l, sv):
        nonlocal dkv
        h, q, ux, ox, h2, a, um = sv
        dh2, da, r, dgm[l] = _mlp_bwd(dh, a, W["w1s"], W["w2"], l, h2, S["norm_mlp_g"][l:l + 1], f"mlp{l}_bwd")
        dw1[l] = _mm_tn(um, da, f"mlp{l}_dw1", col_chips=NCHIP)
        dw2[l] = _mm_tn(r, dh, f"mlp{l}_dw2")
        dq, dkv_l = _xattn_bwd(dh2, q, kv, W["wo"][l], f"xattn{l}_bwd")
        dkv = dkv_l if dkv is None else dkv + dkv_l
        dwo[l] = _mm_tn(ox, dh2, f"xattn{l}_dwo")
        dwq[l] = _mm_tn(ux, dq, f"xattn{l}_dwq")
        dh1, dgx[l] = _bwd_in(dq, W["wq"][l], h, S["norm_xattn_g"][l:l + 1], dh2, f"xq{l}_bwd")
        return dh1

    dh4 = xattn_mlp_bwd(dh, 1, sv1)
    do, dgate, delta = _post1_bwd(dh4, W["w_out1"], o1, z1, "post1_bwd")
    G["w_out1"] = _mm_tn(og, dh4, "post1_dw")
    dqn, dkn, dv1, dcrow, dcq = _fox_bwd(qn, kn, vb, crow, lse, delta, do, "fox_bwd")
    dc = jnp.pad((dcrow[:, 0, :] + dcq[:, :, 0]).T, ((0, 0), (0, HD - NH1)))
    dqr, dkr, df1, G["c_qnorm_g"], G["c_knorm_g"], dfb = _foxprep_bwd(
        dqn, dkn, z1, S["c_qnorm_g"], S["c_knorm_g"], fbp, dc, "foxprep_bwd")
    G["c_fgate_b"] = dfb[:, :NH1]
    dz1 = jnp.concatenate([dqr, dkr, dv1.astype(BF16), dgate, df1], axis=1)
    G["w_in1"] = _mm_tn(u1, dz1, "in1_dw")
    dh3, dgmix1 = _bwd_in(dz1, W["w_in1"], h3, S["norm_mix_g"][1:2], dh4, "in1_bwd")
    dh1 = xattn_mlp_bwd(dh3, 0, sv0)

    dhm, dhh, doa, dgb, G["mlstm_norm_g"], G["hgrn_norm_g"] = _post0_bwd(
        dh1, W["w_out0"], hm, hh, z0, S["mlstm_norm_g"], S["hgrn_norm_g"], "post0_bwd")
    G["w_out0"] = _mm_tn(y0, dh1, "post0_dw")
    dqa, dka, dva, dgates3 = _mlstm_bwd(qk, z0, gates3, bias3, cs, ns, ms, dhm, "mlstm_bwd")
    dqb, dfb0, dib, G["lb_logits"] = _hgrn_bwd(z0, S["lb_logits"], ss, dhh, "hgrn_bwd")
    duc, G["conv_w"] = _conv_bwd(z0, S["conv_w"], jnp.concatenate([dqa, dka], axis=1), "conv_bwd")
    dg8 = jnp.concatenate([dgates3[:, :, 0].T, dgates3[:, :, 1].T], axis=1)
    G["gate_b"] = jnp.sum(dg8, axis=0, keepdims=True)
    dz0 = jnp.concatenate([duc, dva.astype(BF16), doa, dqb, dfb0, dib, dgb,
                           jnp.pad(dg8, ((0, 0), (0, HD - 8))).astype(BF16)], axis=1)
    G["w_in0"] = _mm_tn(u0, dz0, "in0_dw")
    dx, dgmix0 = _bwd_in(dz0, W["w_in0"], x, S["norm_mix_g"][0:1], dh1, "in0_bwd")

    G["wkv"] = _mm_tn(mn, dkv, "memkv_dw", col_chips=NCHIP)
    G["mem_norm_g"] = _memkv_bwd(dkv, W["wkv_s"], mem, row(S["mem_norm_g"]), "memkv_bwd")
    G["norm_mix_g"] = jnp.concatenate([dgmix0, dgmix1], axis=0)
    G["norm_xattn_g"] = jnp.concatenate(dgx, axis=0)
    G["norm_mlp_g"] = jnp.concatenate(dgm, axis=0)
    G["wq"], G["wo"], G["w1"], G["w2"] = dwq, dwo, dw1, dw2
    return lossp[0, 0], dx, G


ANY = pl.BlockSpec(memory_space=pl.ANY)
NCHIP = 4
RS_ROWS = 4224
RS_TILE = 384


def _place():
    x, y, c = lax.axis_index("x"), lax.axis_index("y"), lax.axis_index("c")
    return x, y, c, [(1 - x, y), (x, 1 - y), (1 - x, 1 - y)]


def _comm_call(body, name, ins, out_shapes, sems):
    return pl.pallas_call(body, name=name, in_specs=[ANY] * len(ins), out_specs=[ANY] * len(out_shapes),
                          out_shape=out_shapes, scratch_shapes=sems)(*ins)


def _gather_weights(arrs, name):
    n = len(arrs)

    def body(*refs):
        ins, outs = refs[:n], refs[n:2 * n]
        send_i, recv_i, send_d, recv_d = refs[2 * n:]
        x, y, c, chips = _place()
        me = 2 * x + y

        def half(a, cc):
            h = arrs[a].shape[0] // 2
            return pl.ds(pl.multiple_of(cc * h, h), h)

        def ici(a, k, src_chip, dst_dev):
            return pltpu.make_async_remote_copy(
                src_ref=ins[a].at[half(a, c)], dst_ref=outs[a].at[src_chip, half(a, c)], send_sem=send_i.at[a, k],
                recv_sem=recv_i.at[a, k], device_id=dst_dev, device_id_type=MESH)

        def d2d(a, k, src_chip, cc):
            reg = outs[a].at[src_chip, half(a, cc)]
            return pltpu.make_async_remote_copy(src_ref=reg, dst_ref=reg, send_sem=send_d.at[a, k], recv_sem=recv_d.at[a, k],
                                                device_id=(x, y, 1 - c), device_id_type=MESH)

        for a in range(n):
            for k, (px, py) in enumerate(chips):
                ici(a, k, me, (px, py, c)).start()
        for k, (px, py) in enumerate(chips):
            for a in range(n):
                ici(a, k, 2 * px + py, (px, py, c)).wait_recv()
                d2d(a, k, 2 * px + py, c).start()
        for k, (px, py) in enumerate(chips):
            for a in range(n):
                ici(a, k, me, (px, py, c)).wait_send()
                d2d(a, k, 2 * px + py, c).wait_send()
                d2d(a, k, 2 * px + py, 1 - c).wait_recv()

    sem = lambda: pltpu.SemaphoreType.DMA((n, 3))
    return _comm_call(body, name, arrs, [_sds((NCHIP,) + a.shape, a.dtype) for a in arrs], [sem(), sem(), sem(), sem()])


HBM = pl.BlockSpec(memory_space=pltpu.HBM)
SEM = pl.BlockSpec(memory_space=pltpu.SEMAPHORE)
DATAFLOW = pltpu.SideEffectType.DATAFLOW_SIDE_EFFECTING


def _half_rows(r, cc):
    return pl.ds(pl.multiple_of(cc * (r // 2), r // 2), r // 2)


def _gather_start(arrs, after, name):
    n = len(arrs)

    def body(*refs):
        ins, lands = refs[:n], refs[n:2 * n]
        send, recv, token = refs[2 * n + 1], refs[2 * n + 2], refs[-1]
        x, y, c, chips = _place()
        me = 2 * x + y
        for a in range(n):
            rows = _half_rows(arrs[a].shape[0], c)
            for k, (px, py) in enumerate(chips):
                pltpu.make_async_remote_copy(src_ref=ins[a].at[rows], dst_ref=lands[a].at[me, rows], send_sem=send.at[3 * a + k],
                                             recv_sem=recv.at[3 * a + k], device_id=(px, py, c), device_id_type=MESH).start()
        token[...] = jnp.zeros_like(token)

    hbm = lambda v: pltpu.with_memory_space_constraint(v, pltpu.HBM)
    land_shapes = [((NCHIP,) + a.shape, a.dtype) for a in arrs]
    out = pl.pallas_call(
        body, name=name,
        out_shape=(pltpu.SemaphoreType.DMA((3 * n,)), pltpu.SemaphoreType.DMA((3 * n,)), *[pltpu.HBM(a.shape, a.dtype) for a in arrs],
                   *[pltpu.HBM(s, d) for s, d in land_shapes], _sds((8, HD))),
        in_specs=[HBM] * (2 * n) + [ANY], out_specs=(SEM, SEM, *[HBM] * (2 * n), pl.BlockSpec(memory_space=pltpu.VMEM)),
        input_output_aliases={i: 2 + i for i in range(2 * n)},
        compiler_params=pltpu.CompilerParams(has_side_effects=DATAFLOW),
    )(*[hbm(a) for a in arrs], *[hbm(lax.empty(s, d)) for s, d in land_shapes], after)
    return out[0], out[1], list(out[2:2 + n]), list(out[2 + n:2 + 2 * n]), out[-1]


def _gather_wait(send, recv, srcs, lands, after, name):
    n = len(srcs)

    def body(*refs):
        ins, lands_ = refs[:n], refs[n:2 * n]
        send_, recv_ = refs[2 * n], refs[2 * n + 1]
        x, y, c, chips = _place()
        for a in range(n):
            rows = _half_rows(srcs[a].shape[0], c)
            for k, (px, py) in enumerate(chips):
                cp = pltpu.make_async_remote_copy(src_ref=ins[a].at[rows], dst_ref=lands_[a].at[2 * px + py, rows], send_sem=send_.at[3 * a + k],
                                                  recv_sem=recv_.at[3 * a + k], device_id=(px, py, c), device_id_type=MESH)
                cp.wait_send()
                cp.wait_recv()

    out = pl.pallas_call(
        body, name=name, out_shape=[pltpu.HBM(v.shape, v.dtype) for v in list(srcs) + list(lands)],
        in_specs=[HBM] * (2 * n) + [SEM, SEM, ANY], out_specs=[HBM] * (2 * n), input_output_aliases={i: i for i in range(2 * n)},
        compiler_params=pltpu.CompilerParams(has_side_effects=DATAFLOW),
    )(*srcs, *lands, send, recv, after)
    return list(out[n:])


def _pair_forward(lands, name):
    n = len(lands)

    def body(*refs):
        ins, outs = refs[:n], refs[n:2 * n]
        send, recv = refs[2 * n:]
        x, y, c, chips = _place()
        copies = []
        for a in range(n):
            r = lands[a].shape[1]
            for k, (px, py) in enumerate(chips):
                cp = pltpu.make_async_remote_copy(
                    src_ref=ins[a].at[2 * px + py, _half_rows(r, c)], dst_ref=outs[a].at[2 * px + py, _half_rows(r, c)],
                    send_sem=send.at[a, k], recv_sem=recv.at[a, k], device_id=(x, y, 1 - c), device_id_type=MESH)
                cp.start()
                copies.append(cp)
        for a in range(n):
            r = lands[a].shape[1]
            for k, (px, py) in enumerate(chips):
                pltpu.make_async_remote_copy(
                    src_ref=ins[a].at[2 * px + py, _half_rows(r, c)], dst_ref=outs[a].at[2 * px + py, _half_rows(r, 1 - c)],
                    send_sem=send.at[a, k], recv_sem=recv.at[a, k], device_id=(x, y, 1 - c), device_id_type=MESH).wait_recv()
        for cp in copies:
            cp.wait_send()

    return pl.pallas_call(body, name=name, in_specs=[ANY] * n, out_specs=[ANY] * n, out_shape=[_sds(v.shape, v.dtype) for v in lands],
                          scratch_shapes=[pltpu.SemaphoreType.DMA((n, 3)), pltpu.SemaphoreType.DMA((n, 3))],
                          input_output_aliases={i: i for i in range(n)})(*lands)


def _pair_exchange(arrs, name):
    n = len(arrs)

    def body(*refs):
        ins, outs = refs[:n], refs[n:2 * n]
        send, recv = refs[2 * n:]
        x, y, c, _ = _place()
        copies = []
        for a in range(n):
            h = arrs[a].shape[1] // 2
            cp = pltpu.make_async_remote_copy(src_ref=ins[a].at[:, pl.ds(pl.multiple_of((1 - c) * h, h), h)], dst_ref=outs[a],
                                              send_sem=send.at[a], recv_sem=recv.at[a], device_id=(x, y, 1 - c), device_id_type=MESH)
            cp.start()
            copies.append(cp)
        for cp in copies:
            cp.wait()

    return _comm_call(body, name, arrs, [_sds((a.shape[0], a.shape[1] // 2, a.shape[2]), a.dtype) for a in arrs],
                      [pltpu.SemaphoreType.DMA((n,)), pltpu.SemaphoreType.DMA((n,))])


def _chip_exchange(arrs, name):
    n = len(arrs)

    def body(*refs):
        ins, outs = refs[:n], refs[n:2 * n]
        send, recv = refs[2 * n:]
        x, y, c, chips = _place()
        me = 2 * x + y
        copies = []
        for a in range(n):
            for k, (px, py) in enumerate(chips):
                r = pltpu.make_async_remote_copy(src_ref=ins[a].at[2 * px + py], dst_ref=outs[a].at[me], send_sem=send.at[a, k],
                                                 recv_sem=recv.at[a, k], device_id=(px, py, c), device_id_type=MESH)
                r.start()
                copies.append(r)
        for cp in copies:
            cp.wait()

    return _comm_call(body, name, arrs, [_sds(a.shape, a.dtype) for a in arrs],
                      [pltpu.SemaphoreType.DMA((n, 3)), pltpu.SemaphoreType.DMA((n, 3))])


def _pair_swap(arrs, name):
    n = len(arrs)

    def body(*refs):
        ins, outs = refs[:n], refs[n:2 * n]
        send, recv = refs[2 * n:]
        x, y, c, _ = _place()
        copies = []
        for a in range(n):
            cp = pltpu.make_async_remote_copy(src_ref=ins[a], dst_ref=outs[a], send_sem=send.at[a], recv_sem=recv.at[a],
                                              device_id=(x, y, 1 - c), device_id_type=MESH)
            cp.start()
            copies.append(cp)
        for cp in copies:
            cp.wait()

    return _comm_call(body, name, arrs, [_sds(a.shape, a.dtype) for a in arrs],
                      [pltpu.SemaphoreType.DMA((n,)), pltpu.SemaphoreType.DMA((n,))])


def _all_gather_devices(v, name):
    def body(v_ref, o_ref, send, recv, loc):
        x, y, c, _ = _place()
        me = 4 * x + 2 * y + c
        own = pltpu.make_async_copy(v_ref, o_ref.at[me], loc)
        own.start()
        copies = [own]
        for k in range(1, 8):
            fx, fy, fc = (k >> 2) & 1, (k >> 1) & 1, k & 1
            peer = (x ^ fx, y ^ fy, c ^ fc)
            r = pltpu.make_async_remote_copy(src_ref=v_ref, dst_ref=o_ref.at[me], send_sem=send.at[k - 1],
                                             recv_sem=recv.at[k - 1], device_id=peer, device_id_type=MESH)
            r.start()
            copies.append(r)
        for cp in copies:
            cp.wait()

    return _comm_call(body, name, [v], [_sds((8,) + v.shape, v.dtype)],
                      [pltpu.SemaphoreType.DMA((7,)), pltpu.SemaphoreType.DMA((7,)), pltpu.SemaphoreType.DMA])[0]


def _row_tile(r):
    return next((b for b in (512, 384, 256, 128, 64, 32, 16) if r % b == 0), r)


def _add2(a, b, out_dtype, name):
    r, w = a.shape
    br = _row_tile(r)

    def body(a_ref, b_ref, o_ref):
        o_ref[...] = (a_ref[...].astype(F32) + b_ref[...].astype(F32)).astype(out_dtype)

    blk = pl.BlockSpec((br, w), lambda i: (i, 0))
    return _pc(body, name, (r // br,), [blk, blk], blk, _sds((r, w), out_dtype))(a, b)


def _sum_slots(a, out_dtype, name, extra=None):
    n, r, w = a.shape
    br = _row_tile(r)

    def body(*refs):
        a_ref, o_ref = refs[0], refs[-1]
        acc = a_ref[0].astype(F32)
        for s in range(1, n):
            acc = acc + a_ref[s].astype(F32)
        if extra is not None:
            acc = acc + refs[1][...].astype(F32)
        o_ref[...] = acc.astype(out_dtype)

    ins = [a] + ([extra] if extra is not None else [])
    specs = [pl.BlockSpec((n, br, w), lambda i: (0, i, 0))] + ([pl.BlockSpec((br, w), lambda i: (i, 0))] if extra is not None else [])
    return _pc(body, name, (r // br,), specs, pl.BlockSpec((br, w), lambda i: (i, 0)), _sds((r, w), out_dtype))(*ins)


SMALL = ["norm_mix_g", "norm_xattn_g", "norm_mlp_g", "final_norm_g", "mem_norm_g", "hgrn_lb_logits", "mlstm_norm_g",
         "hgrn_norm_g", "c_qnorm_g", "c_knorm_g", "ab_gate_b", "c_fgate_b"]
SMALL_ROWS = 16


def _pack_small(parts):
    flat = jnp.concatenate([p.reshape(-1).astype(F32) for p in parts])
    return jnp.pad(flat, (0, SMALL_ROWS * D - flat.shape[0])).reshape(SMALL_ROWS, D)


def _unpack_small(buf, shapes):
    flat, out, off = buf.reshape(-1), [], 0
    for s in shapes:
        n = 1
        for d in s:
            n *= d
        out.append(flat[off:off + n].reshape(s))
        off += n
    return out


def kernel(x, mem, norm_mix_g, norm_xattn_g, norm_mlp_g, final_norm_g, ab_w_in, ab_conv_w, ab_gate_b, hgrn_lb_logits, mlstm_norm_g, hgrn_norm_g, ab_w_out, c_w_in, c_fgate_b, c_qnorm_g, c_knorm_g, c_w_out, mem_norm_g, mem_w_kv, xa_w_q, xa_w_o, mlp_w1, mlp_w2, loss_target, m_norm_mix_g, m_norm_xattn_g, m_norm_mlp_g, m_final_norm_g, m_ab_w_in, m_ab_conv_w, m_ab_gate_b, m_hgrn_lb_logits, m_mlstm_norm_g, m_hgrn_norm_g, m_ab_w_out, m_c_w_in, m_c_fgate_b, m_c_qnorm_g, m_c_knorm_g, m_c_w_out, m_mem_norm_g, m_mem_w_kv, m_xa_w_q, m_xa_w_o, m_mlp_w1, m_mlp_w2, v_norm_mix_g, v_norm_xattn_g, v_norm_mlp_g, v_final_norm_g, v_ab_w_in, v_ab_conv_w, v_ab_gate_b, v_hgrn_lb_logits, v_mlstm_norm_g, v_hgrn_norm_g, v_ab_w_out, v_c_w_in, v_c_fgate_b, v_c_qnorm_g, v_c_knorm_g, v_c_w_out, v_mem_norm_g, v_mem_w_kv, v_xa_w_q, v_xa_w_o, v_mlp_w1, v_mlp_w2):
    A = dict(locals())
    chip = 2 * lax.axis_index("x") + lax.axis_index("y")

    big = ["ab_w_in", "c_w_in", "ab_w_out", "c_w_out", "mem_w_kv", "xa_w_q", "xa_w_o", "mlp_w1", "mlp_w2"]
    shard2d = {"ab_w_in": (D, 1026), "c_w_in": (D, 1026), "ab_w_out": (256, D), "c_w_out": (256, D), "mem_w_kv": (D, 512),
               "xa_w_q": (512, D), "xa_w_o": (512, D), "mlp_w1": (2 * D, D), "mlp_w2": (2 * D, D)}
    shard16 = lambda n: A[n].reshape(shard2d[n]).astype(BF16)
    own_slot = lambda gs, os: [lax.dynamic_update_index_in_dim(g, o, chip, 0) for g, o in zip(gs, os)]
    cols = lambda g: jnp.concatenate([g[k] for k in range(NCHIP)], axis=1)
    per_layer = lambda g: g.reshape(NCHIP, 2, -1, D).transpose(1, 0, 2, 3)
    first = [shard16("ab_w_in"), shard16("mem_w_kv"), jnp.pad(ab_conv_w[0], ((0, 16 - CONV_W), (0, 0)))]
    g_in0, g_kv, g_conv = own_slot(_gather_weights(first, "gather_first"), first)
    W = dict(w_in0=_pack_w_in0(cols(g_in0)), wkv_s=g_kv)
    rest_names = ["c_w_in", "ab_w_out", "c_w_out", "xa_w_q", "xa_w_o", "mlp_w1", "mlp_w2"]
    rest = [shard16(n) for n in rest_names]
    send_s, recv_s, srcs, lands, token = _gather_start(rest, g_conv, "gather_rest_start")

    def late_weights(after):
        got = _pair_forward(_gather_wait(send_s, recv_s, srcs, lands, after, "gather_rest_wait"), "gather_rest_forward")
        gw = dict(zip(rest_names, own_slot(got, rest)))
        return dict(w_in1=_pack_w_in1(cols(gw["c_w_in"])), w_out0=gw["ab_w_out"].reshape(D, D), w_out1=gw["c_w_out"].reshape(D, D),
                    wq=per_layer(gw["xa_w_q"]).reshape(2, D, D), wo=per_layer(gw["xa_w_o"]).reshape(2, D, D),
                    w1s=gw["mlp_w1"].reshape(NCHIP, 2, D, D), w2=gw["mlp_w2"].reshape(NCHIP, 2, D, D))

    S = dict(norm_mix_g=norm_mix_g + token[0, 0], norm_xattn_g=norm_xattn_g, norm_mlp_g=norm_mlp_g, final_norm_g=final_norm_g,
             conv_w=cols(g_conv[:, :CONV_W]), gate_b=ab_gate_b, lb_logits=hgrn_lb_logits, mlstm_norm_g=mlstm_norm_g,
             hgrn_norm_g=hgrn_norm_g, c_fgate_b=c_fgate_b, c_qnorm_g=c_qnorm_g, c_knorm_g=c_knorm_g, mem_norm_g=mem_norm_g)

    lossp, dx, G = _local_step(x[0], mem[0], loss_target[0], W, S, late_weights)

    gsmall = {"norm_mix_g": G["norm_mix_g"], "norm_xattn_g": G["norm_xattn_g"], "norm_mlp_g": G["norm_mlp_g"],
              "final_norm_g": G["final_norm_g"], "mem_norm_g": G["mem_norm_g"], "hgrn_lb_logits": G["lb_logits"],
              "mlstm_norm_g": G["mlstm_norm_g"], "hgrn_norm_g": G["hgrn_norm_g"], "c_qnorm_g": G["c_qnorm_g"],
              "c_knorm_g": G["c_knorm_g"], "ab_gate_b": G["gate_b"], "c_fgate_b": G["c_fgate_b"]}
    packed = _pack_small([gsmall[n] for n in SMALL] + [G["conv_w"], lossp])
    red = _sum_slots(_all_gather_devices(packed, "gather_small"), F32, "sum_small")
    small_shapes = [A[n].shape for n in SMALL]
    *gs, gconv, loss = _unpack_small(red, small_shapes + [(CONV_W, D), ()])
    gs = dict(zip(SMALL, gs))
    gconv = lax.dynamic_slice_in_dim(gconv, chip * 256, 256, axis=1)[None]

    def stack_cols(g):
        return jnp.stack([g[:, 1026 * k:1026 * (k + 1)] for k in range(NCHIP)])

    by_rows = lambda g: g.reshape(NCHIP, -1, D)
    main = ["ab_w_out", "c_w_out", "xa_w_q", "xa_w_o", "mlp_w1", "mlp_w2"]
    flat = jnp.concatenate([by_rows(G["w_out0"]), by_rows(G["w_out1"]), by_rows(G["wq"][0]), by_rows(G["wq"][1]),
                            by_rows(G["wo"][0]), by_rows(G["wo"][1]), G["w1"][0], G["w1"][1], by_rows(G["w2"][0]),
                            by_rows(G["w2"][1])], axis=1)
    send = [flat, stack_cols(_unpack_w_in0(G["w_in0"])), stack_cols(_unpack_w_in1(G["w_in1"])), G["wkv"]]
    core = lax.axis_index("c")
    theirs = _pair_exchange(send, "pair_exchange")
    psums = []
    for i, (a, th) in enumerate(zip(send, theirs)):
        h = a.shape[1] // 2
        mine = lax.dynamic_slice_in_dim(a, core * h, h, axis=1)
        psums.append(_add2(mine.reshape(-1, a.shape[2]), th.reshape(-1, a.shape[2]), BF16, f"pair_sum{i}").reshape(th.shape))
    from_chips = _chip_exchange(psums, "chip_exchange")
    rhalf = []
    for i, (f, p) in enumerate(zip(from_chips, psums)):
        f = lax.dynamic_update_index_in_dim(f, lax.dynamic_index_in_dim(p, chip, 0, keepdims=False), chip, 0)
        rhalf.append(_sum_slots(f, F32, f"chip_sum{i}"))
    other = _pair_swap(rhalf, "pair_swap")
    rfull = [jnp.where(core == 0, jnp.concatenate([m_, o_], axis=0), jnp.concatenate([o_, m_], axis=0))
             for m_, o_ in zip(rhalf, other)]
    gbig, off = {"ab_w_in": rfull[1], "c_w_in": rfull[2], "mem_w_kv": rfull[3]}, 0
    rmain = rfull[0].reshape(-1)
    for n in main:
        r, c = shard2d[n]
        gbig[n] = rmain[off:off + r * c].reshape(r, c)
        off += r * c

    out_g, out_d, out_m, out_v = {}, {}, {}, {}
    for n in big:
        d_, m_, v_ = _adam(A[n].reshape(shard2d[n]), gbig[n], A["m_" + n].reshape(shard2d[n]), A["v_" + n].reshape(shard2d[n]), "adam_" + n)
        out_g[n] = gbig[n].reshape(A[n].shape)
        out_d[n], out_m[n], out_v[n] = d_.reshape(A[n].shape), m_.reshape(A[n].shape), v_.reshape(A[n].shape)
    sd, sm, sv = _adam(_pack_small([A[n] for n in SMALL]), _pack_small([gs[n] for n in SMALL]),
                       _pack_small([A["m_" + n] for n in SMALL]), _pack_small([A["v_" + n] for n in SMALL]), "adam_small")
    for n, d_, m_, v_ in zip(SMALL, _unpack_small(sd, small_shapes), _unpack_small(sm, small_shapes), _unpack_small(sv, small_shapes)):
        out_g[n], out_d[n], out_m[n], out_v[n] = gs[n], d_, m_, v_
    cd, cm_, cv = _adam(ab_conv_w[0], gconv[0], m_ab_conv_w[0], v_ab_conv_w[0], "adam_conv")
    out_g["ab_conv_w"], out_d["ab_conv_w"], out_m["ab_conv_w"], out_v["ab_conv_w"] = gconv, cd[None], cm_[None], cv[None]

    order = ["norm_mix_g", "norm_xattn_g", "norm_mlp_g", "final_norm_g", "ab_w_in", "ab_conv_w", "ab_gate_b", "hgrn_lb_logits",
             "mlstm_norm_g", "hgrn_norm_g", "ab_w_out", "c_w_in", "c_fgate_b", "c_qnorm_g", "c_knorm_g", "c_w_out", "mem_norm_g",
             "mem_w_kv", "xa_w_q", "xa_w_o", "mlp_w1", "mlp_w2"]
    return (loss, dx[None], *[out_g[n] for n in order], *[out_d[n] for n in order], *[out_m[n] for n in order],
            *[out_v[n] for n in order])
```

```python
import functools

import jax
import jax.numpy as jnp
from jax import lax
from jax.experimental import pallas as pl
from jax.experimental.pallas import tpu as pltpu

F32 = jnp.float32
BF16 = jnp.bfloat16
EPS = 1e-6
D = 1024
CHUNK = 64
HD = 128
XD = 256
NEG = -1e30
VMEM_LIMIT_V7X = 56 * 1024 * 1024
ADAM_LR, ADAM_B1, ADAM_B2, ADAM_EPS, ADAM_WD, ADAM_STEP = 0.001, 0.9, 0.999, 1e-08, 0.01, 10
MESH = pl.DeviceIdType.MESH


def _pc(body, name, grid, in_specs, out_specs, out_shape, scratch=(), **kw):
    return pl.pallas_call(
        body, name=name, grid=grid, in_specs=in_specs, out_specs=out_specs, out_shape=out_shape,
        scratch_shapes=scratch,
        compiler_params=pltpu.CompilerParams(
            dimension_semantics=("arbitrary",) * len(grid), vmem_limit_bytes=VMEM_LIMIT_V7X), **kw)


def _sds(shape, dtype=F32):
    return jax.ShapeDtypeStruct(shape, dtype)


def _blk(n, target):
    return max(b for b in range(128, max(target, 128) + 1, 128) if n % b == 0)


def _dot(a, b, dims):
    return lax.dot_general(a, b, (dims, ((), ())), preferred_element_type=F32)


def _nn(a, b):
    return _dot(a, b, ((1,), (0,)))


def _nt(a, b):
    return _dot(a, b, ((1,), (1,)))


def _tn(a, b):
    return _dot(a, b, ((0,), (0,)))


def _sigmoid(x):
    return 1.0 / (1.0 + jnp.exp(-x))


def _log_sigmoid(x):
    return jnp.minimum(x, 0.0) - jnp.log(1.0 + jnp.exp(-jnp.abs(x)))


def _rstd(x):
    return lax.rsqrt(jnp.mean(x * x, axis=-1, keepdims=True) + EPS)


def _rms_bwd(du, x, g):
    r = _rstd(x)
    xh = x * r
    dxh = du * g
    dx = r * (dxh - xh * jnp.mean(dxh * xh, axis=-1, keepdims=True))
    return dx, du * xh


def _norm_mm(h, g, w, name, bm=512, bn=512):
    t, n = h.shape[0], w.shape[1]
    bm, bn = min(bm, t), _blk(n, 3 * bn)

    def body(h_ref, g_ref, w_ref, z_ref, u_ref):
        @pl.when(pl.program_id(1) == 0)
        def _():
            x = h_ref[...]
            u_ref[...] = (x * _rstd(x) * g_ref[...]).astype(BF16)
        z_ref[...] = _nn(u_ref[...], w_ref[...])

    return _pc(body, name, (t // bm, n // bn),
               [pl.BlockSpec((bm, D), lambda i, j: (i, 0)), pl.BlockSpec((1, D), lambda i, j: (0, 0)),
                pl.BlockSpec((D, bn), lambda i, j: (0, j))],
               [pl.BlockSpec((bm, bn), lambda i, j: (i, j)), pl.BlockSpec((bm, D), lambda i, j: (i, 0))],
               [_sds((t, n)), _sds((t, D), BF16)])(h, g, w)


def _mm_tn(a, b, name, bm=1024, bn=1024, bt=512, col_chips=None):
    t, m = a.shape
    n = b.shape[1]
    bm, bn, bt = _blk(m, bm), (n // col_chips if col_chips else _blk(n, bn + bn // 2)), min(bt, t)
    nt = t // bt

    def body(a_ref, b_ref, o_ref, acc):
        k = pl.program_id(2)

        @pl.when(k == 0)
        def _():
            acc[...] = jnp.zeros_like(acc)

        acc[...] += _tn(a_ref[...].astype(BF16), b_ref[...].astype(BF16))

        @pl.when(k == nt - 1)
        def _():
            o_ref[...] = acc[...].astype(BF16)

    if col_chips:
        out_spec, out_shape = pl.BlockSpec((None, bm, bn), lambda i, j, k: (j, i, 0)), _sds((col_chips, m, bn), BF16)
    else:
        out_spec, out_shape = pl.BlockSpec((bm, bn), lambda i, j, k: (i, j)), _sds((m, n), BF16)
    return _pc(body, name, (m // bm, n // bn, nt),
               [pl.BlockSpec((bt, bm), lambda i, j, k: (k, i)), pl.BlockSpec((bt, bn), lambda i, j, k: (k, j))],
               out_spec, out_shape, scratch=[pltpu.VMEM((bm, bn), F32)])(a, b)


def _bwd_in(dz, w, h, g, dh, name, bm=512, bk=1024):
    t, n = dz.shape
    bm, bk = min(bm, t), _blk(n, bk + bk // 2)
    nk = n // bk

    def body(dz_ref, w_ref, h_ref, g_ref, dh_ref, o_ref, dg_ref, acc):
        i, k = pl.program_id(0), pl.program_id(1)

        @pl.when(k == 0)
        def _():
            acc[...] = jnp.zeros_like(acc)

        @pl.when((i == 0) & (k == 0))
        def _():
            dg_ref[...] = jnp.zeros_like(dg_ref)

        acc[...] += _nt(dz_ref[...], w_ref[...])

        @pl.when(k == nk - 1)
        def _():
            dx, dgr = _rms_bwd(acc[...], h_ref[...], g_ref[...])
            o_ref[...] = dh_ref[...] + dx
            dg_ref[...] += jnp.sum(dgr, axis=0, keepdims=True)

    return _pc(body, name, (t // bm, nk),
               [pl.BlockSpec((bm, bk), lambda i, k: (i, k)), pl.BlockSpec((D, bk), lambda i, k: (0, k)),
                pl.BlockSpec((bm, D), lambda i, k: (i, 0)), pl.BlockSpec((1, D), lambda i, k: (0, 0)),
                pl.BlockSpec((bm, D), lambda i, k: (i, 0))],
               [pl.BlockSpec((bm, D), lambda i, k: (i, 0)), pl.BlockSpec((1, D), lambda i, k: (0, 0))],
               [_sds((t, D)), _sds((1, D))], scratch=[pltpu.VMEM((bm, D), F32)])(dz, w, h, g, dh)


def _mlp_fwd(h, g, w1s, w2, l, name, bm=512):
    t = h.shape[0]
    bm = min(bm, t)
    nk = w1s.shape[0]

    def body(h_ref, g_ref, w1_ref, w2_ref, o_ref, a_ref, u_ref, acc):
        k = pl.program_id(1)

        @pl.when(k == 0)
        def _():
            x = h_ref[...]
            u_ref[...] = (x * _rstd(x) * g_ref[...]).astype(BF16)
            acc[...] = jnp.zeros_like(acc)

        a = _nn(u_ref[...], w1_ref[...])
        a_ref[...] = a
        r = jnp.square(jnp.maximum(a, 0.0)).astype(BF16)
        acc[...] += _nn(r, w2_ref[...])

        @pl.when(k == nk - 1)
        def _():
            o_ref[...] = h_ref[...] + acc[...]

    return _pc(body, name, (t // bm, nk),
               [pl.BlockSpec((bm, D), lambda i, k: (i, 0)), pl.BlockSpec((1, D), lambda i, k: (0, 0)),
                pl.BlockSpec((None, None, D, D), lambda i, k: (k, l, 0, 0)), pl.BlockSpec((None, None, D, D), lambda i, k: (k, l, 0, 0))],
               [pl.BlockSpec((bm, D), lambda i, k: (i, 0)), pl.BlockSpec((bm, D), lambda i, k: (i, k)),
                pl.BlockSpec((bm, D), lambda i, k: (i, 0))],
               [_sds((t, D)), _sds((t, nk * D)), _sds((t, D), BF16)],
               scratch=[pltpu.VMEM((bm, D), F32)])(h, g, w1s, w2)


def _mlp_bwd(dh, a, w1s, w2, l, h, g, name, bm=512):
    t = h.shape[0]
    bm = min(bm, t)
    nk = w1s.shape[0]

    def body(dh_ref, a_ref, w1_ref, w2_ref, h_ref, g_ref, o_ref, da_ref, r_ref, dg_ref, acc):
        i, k = pl.program_id(0), pl.program_id(1)

        @pl.when(k == 0)
        def _():
            acc[...] = jnp.zeros_like(acc)

        @pl.when((i == 0) & (k == 0))
        def _():
            dg_ref[...] = jnp.zeros_like(dg_ref)

        ap = jnp.maximum(a_ref[...], 0.0)
        r_ref[...] = jnp.square(ap).astype(BF16)
        dr = _nt(dh_ref[...].astype(BF16), w2_ref[...])
        da = (dr * (2.0 * ap)).astype(BF16)
        da_ref[...] = da
        acc[...] += _nt(da, w1_ref[...])

        @pl.when(k == nk - 1)
        def _():
            dx, dgr = _rms_bwd(acc[...], h_ref[...], g_ref[...])
            o_ref[...] = dh_ref[...] + dx
            dg_ref[...] += jnp.sum(dgr, axis=0, keepdims=True)

    return _pc(body, name, (t // bm, nk),
               [pl.BlockSpec((bm, D), lambda i, k: (i, 0)), pl.BlockSpec((bm, D), lambda i, k: (i, k)),
                pl.BlockSpec((None, None, D, D), lambda i, k: (k, l, 0, 0)), pl.BlockSpec((None, None, D, D), lambda i, k: (k, l, 0, 0)),
                pl.BlockSpec((bm, D), lambda i, k: (i, 0)), pl.BlockSpec((1, D), lambda i, k: (0, 0))],
               [pl.BlockSpec((bm, D), lambda i, k: (i, 0)), pl.BlockSpec((bm, D), lambda i, k: (i, k)),
                pl.BlockSpec((bm, D), lambda i, k: (i, k)), pl.BlockSpec((1, D), lambda i, k: (0, 0))],
               [_sds((t, D)), _sds((t, nk * D), BF16), _sds((t, nk * D), BF16), _sds((1, D))],
               scratch=[pltpu.VMEM((bm, D), F32)])(dh, a, w1s, w2, h, g)


def _rows_of(x):
    return lax.broadcasted_iota(jnp.int32, x.shape, 0)


def _shift_down(x, s):
    if s == 0:
        return x
    return jnp.where(_rows_of(x) >= s, pltpu.roll(x, s, 0), 0.0)


def _shift_up(x, s):
    if s == 0:
        return x
    n = x.shape[0]
    return jnp.where(_rows_of(x) < n - s, pltpu.roll(x, n - s, 0), 0.0)


def _cumsum_rows(x):
    n, s = x.shape[0], 1
    while s < n:
        x = x + _shift_down(x, s)
        s *= 2
    return x


def _rcumsum_rows(x):
    n, s = x.shape[0], 1
    while s < n:
        x = x + _shift_up(x, s)
        s *= 2
    return x


def _silu(x):
    return x * _sigmoid(x)


def _dsilu(x):
    s = _sigmoid(x)
    return s * (1.0 + x * (1.0 - s))


CONV_W = 4


def _conv_pre(u, w):
    y = _shift_down(u, CONV_W - 1) * w[0:1, :]
    for j in range(1, CONV_W):
        y = y + _shift_down(u, CONV_W - 1 - j) * w[j:j + 1, :]
    return y


def _conv_fwd(z0, cw, name):
    t = z0.shape[0]

    def body(u_ref, w_ref, o_ref):
        o_ref[...] = _silu(_conv_pre(u_ref[...], w_ref[...]))

    return _pc(body, name, (2 * 512 // HD,),
               [pl.BlockSpec((t, HD), lambda c: (0, c)), pl.BlockSpec((CONV_W, HD), lambda c: (0, c))],
               pl.BlockSpec((t, HD), lambda c: (0, c)), _sds((t, 1024)))(z0, cw)


def _conv_bwd(z0, cw, dy, name):
    t = z0.shape[0]

    def body(u_ref, w_ref, dy_ref, du_ref, dw_ref):
        u, w = u_ref[...], w_ref[...]
        dpre = dy_ref[...] * _dsilu(_conv_pre(u, w))
        du = _shift_up(dpre, CONV_W - 1) * w[0:1, :]
        for j in range(1, CONV_W):
            du = du + _shift_up(dpre, CONV_W - 1 - j) * w[j:j + 1, :]
        du_ref[...] = du.astype(BF16)
        for j in range(CONV_W):
            dw_ref[j:j + 1, :] = jnp.sum(dpre * _shift_down(u, CONV_W - 1 - j), axis=0, keepdims=True)

    return _pc(body, name, (2 * 512 // HD,),
               [pl.BlockSpec((t, HD), lambda c: (0, c)), pl.BlockSpec((CONV_W, HD), lambda c: (0, c)),
                pl.BlockSpec((t, HD), lambda c: (0, c))],
               [pl.BlockSpec((t, HD), lambda c: (0, c)), pl.BlockSpec((CONV_W, HD), lambda c: (0, c))],
               [_sds((t, 1024), BF16), _sds((CONV_W, 1024))])(z0, cw, dy)


def _mlstm_gates(gate, bias, m_in):
    L = gate.shape[0]
    r = lax.broadcasted_iota(jnp.int32, (L, L), 0)
    c = lax.broadcasted_iota(jnp.int32, (L, L), 1)
    eye, tril = r == c, c <= r
    i_col = gate[:, 0:1] + bias[:, 0:1]
    f_col = gate[:, 1:2] + bias[:, 1:2]
    logf_col = _log_sigmoid(f_col)
    logf_row = jnp.sum(jnp.where(eye, logf_col, 0.0), axis=0, keepdims=True)
    i_row = jnp.sum(jnp.where(eye, i_col, 0.0), axis=0, keepdims=True)
    b_col = jnp.sum(jnp.where(tril, logf_row, 0.0), axis=1, keepdims=True)
    b_row = jnp.sum(jnp.where(r <= c, logf_col, 0.0), axis=0, keepdims=True)
    logd = jnp.where(tril, b_col - b_row + i_row, NEG)
    inter = b_col + m_in
    m_t = jnp.maximum(inter, jnp.max(logd, axis=1, keepdims=True))
    w_t = jnp.exp(inter - m_t)
    dm = jnp.exp(logd - m_t)
    b_last = b_col[L - 1:L, :]
    log_in = b_last - b_col + i_col
    m_new = jnp.maximum(b_last + m_in, jnp.max(log_in, axis=0, keepdims=True))
    w_col = jnp.exp(log_in - m_new)
    decay = jnp.exp(b_last + m_in - m_new)
    return dict(eye=eye, r=r, c=c, f_col=f_col, m_t=m_t, w_t=w_t, dm=dm, m_new=m_new, w_col=w_col, decay=decay)


def _mlstm_fwd(qk, z0, gates, bias, name):
    t = qk.shape[0]
    nc, nh, L = t // CHUNK, 4, CHUNK
    scale = HD ** -0.5

    def body(q_ref, k_ref, v_ref, g_ref, b_ref, h_ref, cs_ref, ns_ref, ms_ref, c_s, n_s, m_s):
        @pl.when(pl.program_id(0) == 0)
        def _():
            c_s[...] = jnp.zeros_like(c_s)
            n_s[...] = jnp.zeros_like(n_s)
            m_s[...] = jnp.zeros_like(m_s)

        for hd in range(nh):
            sl = slice(hd * HD, (hd + 1) * HD)
            cm, nv, m_in = c_s[hd], n_s[hd], m_s[hd]
            cs_ref[hd] = cm
            ns_ref[hd] = nv
            ms_ref[hd] = jnp.broadcast_to(m_in, (1, HD))
            q, kh, v = q_ref[:, sl], k_ref[:, sl] * scale, v_ref[:, sl]
            G = _mlstm_gates(g_ref[hd], b_ref[hd], m_in)
            qb, kb, vb = q.astype(BF16), kh.astype(BF16), v.astype(BF16)
            sc = _nt(qb, kb) * G["dm"]
            num = _nn(sc.astype(BF16), vb) + G["w_t"] * _nn(qb, cm.astype(BF16))
            den = jnp.sum(sc, axis=1, keepdims=True) + G["w_t"] * jnp.sum(q * nv, axis=1, keepdims=True)
            h_ref[:, sl] = num / jnp.maximum(jnp.abs(den), jnp.exp(-G["m_t"]))
            wk = G["w_col"] * kh
            c_s[hd] = G["decay"] * cm + _tn(wk.astype(BF16), vb)
            n_s[hd] = G["decay"] * nv + jnp.sum(wk, axis=0, keepdims=True)
            m_s[hd] = G["m_new"]

    hspec = lambda blk: pl.BlockSpec((L, 512), lambda j: (j, blk))
    st = lambda r: pl.BlockSpec((nh, None, r, HD), lambda j: (0, j, 0, 0))
    return _pc(body, name, (nc,),
               [hspec(0), hspec(1), hspec(2), pl.BlockSpec((nh, L, 2), lambda j: (0, j, 0)),
                pl.BlockSpec((nh, 1, 2), lambda j: (0, 0, 0))],
               [hspec(0), st(HD), st(1), st(1)],
               [_sds((t, 512)), _sds((nh, nc, HD, HD)), _sds((nh, nc, 1, HD)), _sds((nh, nc, 1, HD))],
               scratch=[pltpu.VMEM((nh, HD, HD), F32), pltpu.VMEM((nh, 1, HD), F32), pltpu.VMEM((nh, 1, 1), F32)])(qk, qk, z0, gates, bias)


def _mlstm_bwd(qk, z0, gates, bias, cs, ns, ms, dh, name):
    t = qk.shape[0]
    nc, nh, L = t // CHUNK, 4, CHUNK
    scale = HD ** -0.5

    def body(q_ref, k_ref, v_ref, g_ref, b_ref, cs_ref, ns_ref, ms_ref, dh_ref, dq_ref, dk_ref, dv_ref, dg_ref, dc_s, dn_s):
        @pl.when(pl.program_id(0) == 0)
        def _():
            dc_s[...] = jnp.zeros_like(dc_s)
            dn_s[...] = jnp.zeros_like(dn_s)

        for hd in range(nh):
            one_head(hd, slice(hd * HD, (hd + 1) * HD), q_ref, k_ref, v_ref, g_ref, b_ref, cs_ref, ns_ref, ms_ref, dh_ref,
                     dq_ref, dk_ref, dv_ref, dg_ref, dc_s, dn_s)

    def one_head(hd, sl, q_ref, k_ref, v_ref, g_ref, b_ref, cs_ref, ns_ref, ms_ref, dh_ref, dq_ref, dk_ref, dv_ref, dg_ref, dc_s, dn_s):
        cm, nv, m_in = cs_ref[hd], ns_ref[hd], ms_ref[hd][:, 0:1]
        q, kh, v = q_ref[:, sl], k_ref[:, sl] * scale, v_ref[:, sl]
        G = _mlstm_gates(g_ref[hd], b_ref[hd], m_in)
        w_t, dmat, w_col, decay = G["w_t"], G["dm"], G["w_col"], G["decay"]
        qb, kb, vb, cb = q.astype(BF16), kh.astype(BF16), v.astype(BF16), cm.astype(BF16)
        s = _nt(qb, kb)
        sc = s * dmat
        scb = sc.astype(BF16)
        qc = _nn(qb, cb)
        qn = jnp.sum(q * nv, axis=1, keepdims=True)
        num = _nn(scb, vb) + w_t * qc
        den = jnp.sum(sc, axis=1, keepdims=True) + w_t * qn
        e_m = jnp.exp(-G["m_t"])
        dnm = jnp.maximum(jnp.abs(den), e_m)
        dh_ = dh_ref[:, sl]
        dnum = dh_ / dnm
        dden = jnp.where(jnp.abs(den) > e_m, -jnp.sum(dh_ * num, axis=1, keepdims=True) / (dnm * dnm) * jnp.sign(den), 0.0)
        dnumb = dnum.astype(BF16)
        dsc = _nt(dnumb, vb) + dden
        dv = _tn(scb, dnumb)
        wd = w_t * dnum
        wdb = wd.astype(BF16)
        ds = dsc * dmat
        dsb = ds.astype(BF16)
        dq = _nt(wdb, cb) + (w_t * dden) * nv + _nn(dsb, kb)
        dc_o = _tn(qb, wdb)
        dn_o = jnp.sum(q * (w_t * dden), axis=0, keepdims=True)
        dw = jnp.sum(dnum * qc, axis=1, keepdims=True) + dden * qn
        dkh = _tn(dsb, qb)
        dlogd = ds * s
        db_col = jnp.sum(dlogd, axis=1, keepdims=True) + dw * w_t
        csum = jnp.sum(dlogd, axis=0, keepdims=True)
        dcn, dnn = dc_s[hd], dn_s[hd]
        dcnb = dcn.astype(BF16)
        kdc = _nn(kb, dcnb)
        dws = jnp.sum(kdc * v, axis=1, keepdims=True) + jnp.sum(kh * dnn, axis=1, keepdims=True)
        dv = dv + w_col * kdc
        dkh = dkh + w_col * (_nt(vb, dcnb) + dnn)
        dlin = dws * w_col
        ddecay = jnp.sum(jnp.sum(dcn * cm, axis=1, keepdims=True), axis=0, keepdims=True) + jnp.sum(dnn * nv, axis=1, keepdims=True)
        dlast = ddecay * decay + jnp.sum(dlin, axis=0, keepdims=True)
        rows = lax.broadcasted_iota(jnp.int32, (L, 1), 0)
        db_col = db_col - dlin + jnp.where(rows == L - 1, dlast, 0.0)
        eye, r, c = G["eye"], G["r"], G["c"]
        di = dlin + jnp.sum(jnp.where(eye, csum, 0.0), axis=1, keepdims=True)
        db_row = jnp.sum(jnp.where(eye, db_col, 0.0), axis=0, keepdims=True) - csum
        dlogf = jnp.sum(jnp.where(c >= r, db_row, 0.0), axis=1, keepdims=True)
        dg_ref[hd, :, 0:1] = di
        dg_ref[hd, :, 1:2] = dlogf * (1.0 - _sigmoid(G["f_col"]))
        dq_ref[:, sl] = dq
        dk_ref[:, sl] = dkh * scale
        dv_ref[:, sl] = dv
        dc_s[hd] = decay * dcn + dc_o
        dn_s[hd] = decay * dnn + dn_o

    rv = lambda j: nc - 1 - j
    hspec = lambda blk: pl.BlockSpec((L, 512), lambda j: (rv(j), blk))
    st = lambda r: pl.BlockSpec((nh, None, r, HD), lambda j: (0, rv(j), 0, 0))
    gs = pl.BlockSpec((nh, L, 2), lambda j: (0, rv(j), 0))
    return _pc(body, name, (nc,),
               [hspec(0), hspec(1), hspec(2), gs, pl.BlockSpec((nh, 1, 2), lambda j: (0, 0, 0)),
                st(HD), st(1), st(1), hspec(0)],
               [hspec(0), hspec(0), hspec(0), gs],
               [_sds((t, 512)), _sds((t, 512)), _sds((t, 512)), _sds((nh, t, 2))],
               scratch=[pltpu.VMEM((nh, HD, HD), F32), pltpu.VMEM((nh, 1, HD), F32)])(qk, qk, z0, gates, bias, cs, ns, ms, dh)


def _hgrn_act(qb_, fb_, ib_, lg):
    lb = _sigmoid(lg[0:1, :] - lg[1:2, :])
    sg = _sigmoid(fb_)
    f = lb + (1.0 - lb) * sg
    return lb, sg, f, _silu(qb_), (1.0 - lb) * (1.0 - sg), _silu(ib_), _cumsum_rows(jnp.log(f))


HG_SUB = 16


def _hgrn_offdiag(q, k, b, r0):
    beta = b[r0 - 1:r0, :]
    e1 = jnp.exp(b[r0:r0 + HG_SUB, :] - beta)
    e2 = jnp.where(_rows_of(b) < r0, jnp.exp(jnp.minimum(beta - b, 0.0)), 0.0)
    return q[r0:r0 + HG_SUB, :] * e1, k * e2, e1, e2


def _hgrn_fwd(z0, lbl, name):
    t = z0.shape[0]
    nc, nh, L = t // CHUNK, 4, CHUNK

    def body(q_ref, f_ref, i_ref, l_ref, o_ref, ss_ref, st_s):
        @pl.when(pl.program_id(0) == 0)
        def _():
            st_s[...] = jnp.zeros_like(st_s)

        for hd in range(nh):
            sl = slice(hd * HD, (hd + 1) * HD)
            st = st_s[hd]
            ss_ref[hd] = st
            _, _, _, q, k, v, b = _hgrn_act(q_ref[:, sl], f_ref[:, sl], i_ref[:, sl], l_ref[:, sl])
            o = _nt((q * jnp.exp(b)).astype(BF16), st.astype(BF16))
            sub = _rows_of(b) & (HG_SUB - 1)
            o = o + jnp.sum(q * k, axis=1, keepdims=True) * v
            for dl in range(1, HG_SUB):
                e = jnp.exp(jnp.where(sub >= dl, b - pltpu.roll(b, dl, 0), NEG))
                a = jnp.sum(q * pltpu.roll(k, dl, 0) * e, axis=1, keepdims=True)
                o = o + a * pltpu.roll(v, dl, 0)
            o_ref[:, sl] = o
            vb = v.astype(BF16)
            for i in range(1, L // HG_SUB):
                r0 = i * HG_SUB
                qt, kt, _, _ = _hgrn_offdiag(q, k, b, r0)
                a = _nt(qt.astype(BF16), kt.astype(BF16))
                o_ref[r0:r0 + HG_SUB, sl] += _nn(a.astype(BF16), vb)
            bl = b[L - 1:L, :]
            st_s[hd] = st * jnp.exp(bl) + _tn(v.astype(BF16), (k * jnp.exp(bl - b)).astype(BF16))

    hspec = lambda blk: pl.BlockSpec((L, 512), lambda j: (j, blk))
    return _pc(body, name, (nc,),
               [hspec(4), hspec(5), hspec(6), pl.BlockSpec((2, 512), lambda j: (0, 0))],
               [hspec(0), pl.BlockSpec((nh, None, HD, HD), lambda j: (0, j, 0, 0))],
               [_sds((t, 512)), _sds((nh, nc, HD, HD))],
               scratch=[pltpu.VMEM((nh, HD, HD), F32)])(z0, z0, z0, lbl)


def _hgrn_bwd(z0, lbl, ss, do, name):
    t = z0.shape[0]
    nc, nh, L = t // CHUNK, 4, CHUNK

    def body(q_ref, f_ref, i_ref, l_ref, ss_ref, do_ref, dq_ref, df_ref, di_ref, dl_ref, dst_s, dlb_s, dq_a, dk_a, dv_a, db_a):
        @pl.when(pl.program_id(0) == 0)
        def _():
            dst_s[...] = jnp.zeros_like(dst_s)
            dlb_s[...] = jnp.zeros_like(dlb_s)

        for hd in range(nh):
            one_head(hd, slice(hd * HD, (hd + 1) * HD), q_ref, f_ref, i_ref, l_ref, ss_ref, do_ref, dq_ref, df_ref, di_ref, dl_ref,
                     dst_s, dlb_s, dq_a.at[hd], dk_a.at[hd], dv_a.at[hd], db_a.at[hd])

    def one_head(hd, sl, q_ref, f_ref, i_ref, l_ref, ss_ref, do_ref, dq_ref, df_ref, di_ref, dl_ref, dst_s, dlb_s, dq_a, dk_a, dv_a, db_a):
        st = ss_ref[hd]
        qp, fp, ip = q_ref[:, sl], f_ref[:, sl], i_ref[:, sl]
        lb, sg, f, q, k, v, b = _hgrn_act(qp, fp, ip, l_ref[:, sl])
        do_ = do_ref[:, sl]
        dob, stb = do_.astype(BF16), st.astype(BF16)
        eb = jnp.exp(b)
        qe = q * eb
        dqe = _nn(dob, stb)
        dst_o = _tn(dob, qe.astype(BF16))
        dq = dqe * eb
        db = dqe * qe
        rows = _rows_of(b)
        sub = rows & (HG_SUB - 1)
        p0 = jnp.sum(do_ * v, axis=1, keepdims=True)
        dq = dq + p0 * k
        dk = p0 * q
        dv = jnp.sum(q * k, axis=1, keepdims=True) * do_
        for dl in range(1, HG_SUB):
            up = L - dl
            kd, vd = pltpu.roll(k, dl, 0), pltpu.roll(v, dl, 0)
            e = jnp.exp(jnp.where(sub >= dl, b - pltpu.roll(b, dl, 0), NEG))
            a = jnp.sum(q * kd * e, axis=1, keepdims=True)
            p = jnp.sum(do_ * vd, axis=1, keepdims=True) * e
            dq = dq + p * kd
            dkd = p * q
            dbb = dkd * kd
            dv = dv + pltpu.roll(a * do_, up, 0)
            dk = dk + pltpu.roll(dkd, up, 0)
            db = db + dbb - pltpu.roll(dbb, up, 0)
        dq_a[...], dk_a[...], dv_a[...], db_a[...] = dq, dk, dv, db
        vb = v.astype(BF16)
        for i in range(1, L // HG_SUB):
            r0 = i * HG_SUB
            blk = slice(r0, r0 + HG_SUB)
            qt, kt, e1, e2 = _hgrn_offdiag(q, k, b, r0)
            qtb, ktb, dob_i = qt.astype(BF16), kt.astype(BF16), dob[blk, :]
            a = _nt(qtb, ktb).astype(BF16)
            da = _nt(dob_i, vb).astype(BF16)
            dv_a[...] += _tn(a, dob_i)
            dqt = _nn(da, ktb)
            dkt = _tn(da, qtb)
            dq_a[blk, :] += dqt * e1
            t1, t2 = dqt * qt, dkt * kt
            db_a[blk, :] += t1
            dk_a[...] += dkt * e2
            db_a[...] -= t2
            db_a[r0 - 1:r0, :] += jnp.sum(t2, axis=0, keepdims=True) - jnp.sum(t1, axis=0, keepdims=True)
        dq, dk, dv, db = dq_a[...], dk_a[...], dv_a[...], db_a[...]
        dstn = dst_s[hd]
        dstnb = dstn.astype(BF16)
        bl = b[L - 1:L, :]
        ebl = jnp.exp(bl)
        kdec_e = jnp.exp(bl - b)
        kdec = k * kdec_e
        dbl = jnp.sum(dstn * st, axis=0, keepdims=True) * ebl
        dv = dv + _nt(kdec.astype(BF16), dstnb)
        dkdec = _nn(v.astype(BF16), dstnb)
        dk = dk + dkdec * kdec_e
        dx = dkdec * kdec
        dbl = dbl + jnp.sum(dx, axis=0, keepdims=True)
        db = db - dx + jnp.where(rows == L - 1, dbl, 0.0)
        dst_s[hd] = dstn * ebl + dst_o
        dg = _rcumsum_rows(db)
        dfk = dg / f - dk
        dq_ref[:, sl] = (dq * _dsilu(qp)).astype(BF16)
        di_ref[:, sl] = (dv * _dsilu(ip)).astype(BF16)
        df_ref[:, sl] = (dfk * (1.0 - lb) * sg * (1.0 - sg)).astype(BF16)
        dlb_s[hd] += jnp.sum(dfk * (1.0 - sg), axis=0, keepdims=True)

        @pl.when(pl.program_id(0) == nc - 1)
        def _():
            dl0 = dlb_s[hd] * lb * (1.0 - lb)
            dl_ref[0:1, sl] = dl0
            dl_ref[1:2, sl] = -dl0

    rv = lambda j: nc - 1 - j
    hspec = lambda blk: pl.BlockSpec((L, 512), lambda j: (rv(j), blk))
    return _pc(body, name, (nc,),
               [hspec(4), hspec(5), hspec(6), pl.BlockSpec((2, 512), lambda j: (0, 0)),
                pl.BlockSpec((nh, None, HD, HD), lambda j: (0, rv(j), 0, 0)), hspec(0)],
               [hspec(0), hspec(0), hspec(0), pl.BlockSpec((2, 512), lambda j: (0, 0))],
               [_sds((t, 512), BF16), _sds((t, 512), BF16), _sds((t, 512), BF16), _sds((2, 512))],
               scratch=[pltpu.VMEM((nh, HD, HD), F32), pltpu.VMEM((nh, 1, HD), F32)] + [pltpu.VMEM((nh, L, HD), F32)] * 4)(z0, z0, z0, lbl, ss, do)


def _post0_fwd(hm, hh, z0, na, nb, w, h0, name, bm=512):
    t = h0.shape[0]
    bm = min(bm, t)

    def body(hm_ref, hh_ref, oa_ref, gb_ref, na_ref, nb_ref, w_ref, h_ref, o_ref, y_ref):
        for hd in range(4):
            sl = slice(hd * HD, (hd + 1) * HD)
            pa = _sigmoid(oa_ref[:, sl]) * hm_ref[:, sl]
            y_ref[:, sl] = (pa * _rstd(pa) * na_ref[:, sl]).astype(BF16)
            xb = hh_ref[:, sl]
            y_ref[:, 512 + hd * HD:512 + (hd + 1) * HD] = (xb * _rstd(xb) * nb_ref[:, sl] * _silu(gb_ref[:, sl])).astype(BF16)
        o_ref[...] = h_ref[...] + _nn(y_ref[...], w_ref[...])

    row = lambda wd, c: pl.BlockSpec((bm, wd), lambda i: (i, c))
    vec = lambda wd: pl.BlockSpec((1, wd), lambda i: (0, 0))
    return _pc(body, name, (t // bm,),
               [row(512, 0), row(512, 0), row(512, 3), row(512, 7), vec(512), vec(512),
                pl.BlockSpec((D, D), lambda i: (0, 0)), row(D, 0)],
               [row(D, 0), row(D, 0)], [_sds((t, D)), _sds((t, D), BF16)])(hm, hh, z0, z0, na, nb, w, h0)


def _post0_bwd(dh1, w, hm, hh, z0, na, nb, name, bm=512):
    t = dh1.shape[0]
    bm = min(bm, t)

    def body(dh_ref, w_ref, hm_ref, hh_ref, oa_ref, gb_ref, na_ref, nb_ref, dhm_ref, dhh_ref, doa_ref, dgb_ref, dna_ref, dnb_ref):
        @pl.when(pl.program_id(0) == 0)
        def _():
            dna_ref[...] = jnp.zeros_like(dna_ref)
            dnb_ref[...] = jnp.zeros_like(dnb_ref)

        dy = _nt(dh_ref[...].astype(BF16), w_ref[...])
        for hd in range(4):
            sl = slice(hd * HD, (hd + 1) * HD)
            hm_, oa = hm_ref[:, sl], oa_ref[:, sl]
            sg = _sigmoid(oa)
            dpa, dgr = _rms_bwd(dy[:, sl], sg * hm_, na_ref[:, sl])
            dna_ref[:, sl] += jnp.sum(dgr, axis=0, keepdims=True)
            doa_ref[:, sl] = (dpa * hm_ * sg * (1.0 - sg)).astype(BF16)
            dhm_ref[:, sl] = dpa * sg
            xb, gb, nbv = hh_ref[:, sl], gb_ref[:, sl], nb_ref[:, sl]
            dyb = dy[:, 512 + hd * HD:512 + (hd + 1) * HD]
            dgb_ref[:, sl] = (dyb * (xb * _rstd(xb) * nbv) * _dsilu(gb)).astype(BF16)
            dxb, dgr2 = _rms_bwd(dyb * _silu(gb), xb, nbv)
            dnb_ref[:, sl] += jnp.sum(dgr2, axis=0, keepdims=True)
            dhh_ref[:, sl] = dxb

    row = lambda wd, c: pl.BlockSpec((bm, wd), lambda i: (i, c))
    vec = lambda wd: pl.BlockSpec((1, wd), lambda i: (0, 0))
    return _pc(body, name, (t // bm,),
               [row(D, 0), pl.BlockSpec((D, D), lambda i: (0, 0)), row(512, 0), row(512, 0), row(512, 3), row(512, 7),
                vec(512), vec(512)],
               [row(512, 0), row(512, 0), row(512, 0), row(512, 0), vec(512), vec(512)],
               [_sds((t, 512)), _sds((t, 512)), _sds((t, 512), BF16), _sds((t, 512), BF16), _sds((1, 512)), _sds((1, 512))],
               )(dh1, w, hm, hh, z0, z0, na, nb)


def _memkv_fwd(mem, g, wkv_s, name):
    m = mem.shape[0]

    def body(x_ref, g_ref, w_ref, kv_ref, mn_ref):
        x = x_ref[...]
        mn = (x * _rstd(x) * g_ref[...]).astype(BF16)
        mn_ref[...] = mn
        kv_ref[...] = _nn(mn, w_ref[...])

    return _pc(body, name, (4,),
               [pl.BlockSpec((m, D), lambda k: (0, 0)), pl.BlockSpec((1, D), lambda k: (0, 0)),
                pl.BlockSpec((None, D, 512), lambda k: (k, 0, 0))],
               [pl.BlockSpec((m, 512), lambda k: (0, k)), pl.BlockSpec((m, D), lambda k: (0, 0))],
               [_sds((m, 2048)), _sds((m, D), BF16)])(mem, g, wkv_s)


def _memkv_bwd(dkv, wkv_s, mem, g, name):
    m = mem.shape[0]

    def body(d_ref, w_ref, x_ref, g_ref, dg_ref, acc):
        k = pl.program_id(0)

        @pl.when(k == 0)
        def _():
            acc[...] = jnp.zeros_like(acc)

        acc[...] += _nt(d_ref[...].astype(BF16), w_ref[...])

        @pl.when(k == 3)
        def _():
            _, dgr = _rms_bwd(acc[...], x_ref[...], g_ref[...])
            dg_ref[...] = jnp.sum(dgr, axis=0, keepdims=True)

    return _pc(body, name, (4,),
               [pl.BlockSpec((m, 512), lambda k: (0, k)), pl.BlockSpec((None, D, 512), lambda k: (k, 0, 0)),
                pl.BlockSpec((m, D), lambda k: (0, 0)), pl.BlockSpec((1, D), lambda k: (0, 0))],
               pl.BlockSpec((1, D), lambda k: (0, 0)), _sds((1, D)), scratch=[pltpu.VMEM((m, D), F32)])(dkv, wkv_s, mem, g)


def _xattn_probs(qh, kh):
    s = _nt(qh, kh) * (XD ** -0.5)
    p = jnp.exp(s - jnp.max(s, axis=1, keepdims=True))
    return p / jnp.sum(p, axis=1, keepdims=True)


def _xattn_fwd(q, kv, wo, h1, name, bm=512):
    t, m = q.shape[0], kv.shape[0]
    bm = min(bm, t)

    def body(q_ref, k_ref, v_ref, w_ref, h_ref, out_ref, o_ref):
        for hd in range(D // XD):
            sl = slice(hd * XD, (hd + 1) * XD)
            p = _xattn_probs(q_ref[:, sl].astype(BF16), k_ref[:, sl].astype(BF16))
            o_ref[:, sl] = _nn(p.astype(BF16), v_ref[:, sl].astype(BF16)).astype(BF16)
        out_ref[...] = h_ref[...] + _nn(o_ref[...], w_ref[...])

    row = pl.BlockSpec((bm, D), lambda i: (i, 0))
    return _pc(body, name, (t // bm,),
               [row, pl.BlockSpec((m, D), lambda i: (0, 0)), pl.BlockSpec((m, D), lambda i: (0, 1)),
                pl.BlockSpec((D, D), lambda i: (0, 0)), row],
               [row, row], [_sds((t, D)), _sds((t, D), BF16)])(q, kv, kv, wo, h1)


def _xattn_bwd(dh2, q, kv, wo, name, bm=512):
    t, m = q.shape[0], kv.shape[0]
    bm = min(bm, t)

    def body(dh_ref, q_ref, k_ref, v_ref, w_ref, dq_ref, dkv_ref):
        @pl.when(pl.program_id(0) == 0)
        def _():
            dkv_ref[...] = jnp.zeros_like(dkv_ref)

        d_o = _nt(dh_ref[...].astype(BF16), w_ref[...])
        for hd in range(D // XD):
            sl = slice(hd * XD, (hd + 1) * XD)
            qh, kh, vh = q_ref[:, sl].astype(BF16), k_ref[:, sl].astype(BF16), v_ref[:, sl].astype(BF16)
            p = _xattn_probs(qh, kh)
            dob = d_o[:, sl].astype(BF16)
            dp = _nt(dob, vh)
            dkv_ref[:, D + hd * XD:D + (hd + 1) * XD] += _tn(p.astype(BF16), dob)
            ds = (p * (dp - jnp.sum(dp * p, axis=1, keepdims=True)) * (XD ** -0.5)).astype(BF16)
            dq_ref[:, sl] = _nn(ds, kh).astype(BF16)
            dkv_ref[:, sl] += _tn(ds, qh)

    row = pl.BlockSpec((bm, D), lambda i: (i, 0))
    return _pc(body, name, (t // bm,),
               [row, row, pl.BlockSpec((m, D), lambda i: (0, 0)), pl.BlockSpec((m, D), lambda i: (0, 1)),
                pl.BlockSpec((D, D), lambda i: (0, 0))],
               [row, pl.BlockSpec((m, 2 * D), lambda i: (0, 0))],
               [_sds((t, D), BF16), _sds((m, 2 * D))])(dh2, q, kv, kv, wo)


NH1 = 8
FOX_BM = 512
FOX_BQ = 512
FOX_BK = 512
FOX_HEADS_PER_STEP = 2


def _foxprep_fwd(z1, qg, kg, fbp, name):
    t = z1.shape[0]
    bm = min(FOX_BM, t)

    def body(q_ref, k_ref, v_ref, f_ref, qg_ref, kg_ref, fb_ref, qn_ref, kn_ref, vb_ref, c_ref, carry):
        @pl.when(pl.program_id(0) == 0)
        def _():
            carry[...] = jnp.zeros_like(carry)

        for hd in range(NH1):
            sl = slice(hd * HD, (hd + 1) * HD)
            x = q_ref[:, sl]
            qn_ref[:, sl] = (x * _rstd(x) * qg_ref[...] * FOX_QSCALE).astype(BF16)
            x = k_ref[:, sl]
            kn_ref[:, sl] = (x * _rstd(x) * kg_ref[...]).astype(BF16)
        vb_ref[...] = v_ref[...].astype(BF16)
        c = carry[...] + _cumsum_rows(_log_sigmoid(f_ref[...] + fb_ref[...]))
        c_ref[...] = c
        carry[...] = c[bm - 1:bm, :]

    row = lambda c: pl.BlockSpec((bm, D), lambda i: (i, c))
    lane = pl.BlockSpec((bm, HD), lambda i: (i, 4 * D // HD))
    vec = pl.BlockSpec((1, HD), lambda i: (0, 0))
    return _pc(body, name, (t // bm,), [row(0), row(1), row(2), lane, vec, vec, vec],
               [row(0), row(0), row(0), pl.BlockSpec((bm, HD), lambda i: (i, 0))],
               [_sds((t, D), BF16), _sds((t, D), BF16), _sds((t, D), BF16), _sds((t, HD))],
               scratch=[pltpu.VMEM((1, HD), F32)])(z1, z1, z1, z1, qg, kg, fbp)


def _foxprep_bwd(dqn, dkn, z1, qg, kg, fbp, dc, name):
    t = z1.shape[0]
    bm = min(FOX_BM, t)
    nb = t // bm

    def body(dqn_ref, dkn_ref, q_ref, k_ref, f_ref, qg_ref, kg_ref, fb_ref, dc_ref,
             dq_ref, dk_ref, df_ref, dqg_ref, dkg_ref, dfb_ref, carry):
        @pl.when(pl.program_id(0) == 0)
        def _():
            carry[...] = jnp.zeros_like(carry)
            dqg_ref[...] = jnp.zeros_like(dqg_ref)
            dkg_ref[...] = jnp.zeros_like(dkg_ref)
            dfb_ref[...] = jnp.zeros_like(dfb_ref)

        for hd in range(NH1):
            sl = slice(hd * HD, (hd + 1) * HD)
            dx, dgr = _rms_bwd(dqn_ref[:, sl] * (HD ** -0.5), q_ref[:, sl], qg_ref[...])
            dq_ref[:, sl] = dx.astype(BF16)
            dqg_ref[...] += jnp.sum(dgr, axis=0, keepdims=True)
            dx, dgr = _rms_bwd(dkn_ref[:, sl], k_ref[:, sl], kg_ref[...])
            dk_ref[:, sl] = dx.astype(BF16)
            dkg_ref[...] += jnp.sum(dgr, axis=0, keepdims=True)
        dc_ = dc_ref[...]
        dlogf = _rcumsum_rows(dc_) + carry[...]
        carry[...] += jnp.sum(dc_, axis=0, keepdims=True)
        lanes = lax.broadcasted_iota(jnp.int32, dc_.shape, 1)
        df = jnp.where(lanes < NH1, dlogf * (1.0 - _sigmoid(f_ref[...] + fb_ref[...])), 0.0)
        df_ref[...] = df.astype(BF16)
        dfb_ref[...] += jnp.sum(df, axis=0, keepdims=True)

    rv = lambda i: nb - 1 - i
    row = lambda c: pl.BlockSpec((bm, D), lambda i: (rv(i), c))
    lane = lambda c: pl.BlockSpec((bm, HD), lambda i: (rv(i), c))
    vec = pl.BlockSpec((1, HD), lambda i: (0, 0))
    return _pc(body, name, (nb,), [row(0), row(0), row(0), row(1), lane(4 * D // HD), vec, vec, vec, lane(0)],
               [row(0), row(0), lane(0), vec, vec, vec],
               [_sds((t, D), BF16), _sds((t, D), BF16), _sds((t, HD), BF16), _sds((1, HD)), _sds((1, HD)), _sds((1, HD))],
               scratch=[pltpu.VMEM((1, HD), F32)])(dqn, dkn, z1, z1, z1, qg, kg, fbp, dc)


LOG2E = 1.4426950408889634
FOX_QSCALE = HD ** -0.5 * LOG2E


def _fox_scores(q, k, ck, i, j, bq, bk, masked):
    s = _nt(q, k) - ck
    if not masked:
        return s, None
    rows = i * bq + lax.broadcasted_iota(jnp.int32, s.shape, 0)
    cols = j * bk + lax.broadcasted_iota(jnp.int32, s.shape, 1)
    return s, cols <= rows


def _fox_block_kind(i, j, bq, bk):
    active = j * bk < (i + 1) * bq
    full = (j + 1) * bk <= i * bq + 1
    return full, active & jnp.logical_not(full)


def _fox_fwd(qn, kn, vb, crow, name):
    t = qn.shape[0]
    bq, bk, G = min(FOX_BQ, t), min(FOX_BK, t), FOX_HEADS_PER_STEP
    nq, nk = t // bq, t // bk

    def body(q_ref, k_ref, v_ref, ck_ref, o_ref, lse_ref, m_s, l_s, acc):
        i, j = pl.program_id(1), pl.program_id(2)

        @pl.when(j == 0)
        def _():
            m_s[...] = jnp.full_like(m_s, NEG)
            l_s[...] = jnp.zeros_like(l_s)
            acc[...] = jnp.zeros_like(acc)

        def step(masked):
            for g in range(G):
                sl = slice(g * HD, (g + 1) * HD)
                s, ok = _fox_scores(q_ref[:, sl], k_ref[:, sl], ck_ref[g], i, j, bq, bk, masked)
                if masked:
                    s = jnp.where(ok, s, NEG)
                m_new = jnp.maximum(m_s[g], jnp.max(s, axis=1, keepdims=True))
                alpha = jnp.exp2(m_s[g] - m_new)
                p = jnp.exp2(s - m_new)
                l_s[g] = alpha * l_s[g] + jnp.sum(p, axis=1, keepdims=True)
                acc[:, sl] = alpha * acc[:, sl] + _nn(p.astype(BF16), v_ref[:, sl])
                m_s[g] = m_new

        full, part = _fox_block_kind(i, j, bq, bk)
        pl.when(full)(lambda: step(False))
        pl.when(part)(lambda: step(True))

        @pl.when(j == nk - 1)
        def _():
            for g in range(G):
                sl = slice(g * HD, (g + 1) * HD)
                o_ref[:, sl] = acc[:, sl] / l_s[g]
                lse_ref[g] = m_s[g] + jnp.log2(l_s[g])

    kj = lambda i, j: jnp.minimum(j, ((i + 1) * bq - 1) // bk)
    kmap = lambda h, i, j: (kj(i, j), h)
    return _pc(body, name, (NH1 // G, nq, nk),
               [pl.BlockSpec((bq, G * HD), lambda h, i, j: (i, h)), pl.BlockSpec((bk, G * HD), kmap),
                pl.BlockSpec((bk, G * HD), kmap), pl.BlockSpec((G, 1, bk), lambda h, i, j: (h, 0, kj(i, j)))],
               [pl.BlockSpec((bq, G * HD), lambda h, i, j: (i, h)), pl.BlockSpec((G, bq, 1), lambda h, i, j: (h, i, 0))],
               [_sds((t, D)), _sds((NH1, t, 1))],
               scratch=[pltpu.VMEM((G, bq, 1), F32), pltpu.VMEM((G, bq, 1), F32), pltpu.VMEM((bq, G * HD), F32)])(qn, kn, vb, crow)


def _fox_bwd(qn, kn, vb, crow, lse, delta, do, name):
    t = qn.shape[0]
    bq, bk, G = min(FOX_BQ, t), min(FOX_BK, t), FOX_HEADS_PER_STEP
    nq, nk = t // bq, t // bk

    def body(q_ref, k_ref, v_ref, ck_ref, lse_ref, dl_ref, do_ref, dq_ref, dk_ref, dv_ref, dc_ref, dcq_ref, dk_s, dv_s, dc_s):
        j, i = pl.program_id(1), pl.program_id(2)

        @pl.when(i == 0)
        def _():
            dk_s[...] = jnp.zeros_like(dk_s)
            dv_s[...] = jnp.zeros_like(dv_s)
            dc_s[...] = jnp.zeros_like(dc_s)

        @pl.when((i == 0) & (j == 0))
        def _():
            dq_ref[...] = jnp.zeros_like(dq_ref)
            dcq_ref[...] = jnp.zeros_like(dcq_ref)

        def step(masked):
            rows = pl.ds(pl.multiple_of(i * bq, bq), bq)
            for g in range(G):
                sl = slice(g * HD, (g + 1) * HD)
                q, k = q_ref[:, sl], k_ref[:, sl]
                s, ok = _fox_scores(q, k, ck_ref[g], i, j, bq, bk, masked)
                if masked:
                    s = jnp.where(ok, s, NEG)
                p = jnp.exp2(s - lse_ref[g])
                dob = do_ref[:, sl]
                dv_s[:, sl] += _tn(p.astype(BF16), dob)
                ds = p * (_nt(dob, v_ref[:, sl]) - dl_ref[g])
                dsb = ds.astype(BF16)
                dq_ref[rows, sl] += _nn(dsb, k)
                dk_s[:, sl] += _tn(dsb, q)
                dc_s[g] -= jnp.sum(ds, axis=0, keepdims=True)
                dcq_ref[g, rows, :] += jnp.sum(ds, axis=1, keepdims=True)

        full, part = _fox_block_kind(i, j, bq, bk)
        pl.when(full)(lambda: step(False))
        pl.when(part)(lambda: step(True))

        @pl.when(i == nq - 1)
        def _():
            dk_ref[...] = dk_s[...] * (1.0 / LOG2E)
            dv_ref[...] = dv_s[...]
            dc_ref[...] = dc_s[...]

    qi = lambda i, j: jnp.maximum(i, (j * bk) // bq)
    qmap = lambda h, j, i: (qi(i, j), h)
    c3map = lambda h, j, i: (h, qi(i, j), 0)
    kspec = pl.BlockSpec((bk, G * HD), lambda h, j, i: (j, h))
    return _pc(body, name, (NH1 // G, nk, nq),
               [pl.BlockSpec((bq, G * HD), qmap), kspec, kspec,
                pl.BlockSpec((G, 1, bk), lambda h, j, i: (h, 0, j)), pl.BlockSpec((G, bq, 1), c3map),
                pl.BlockSpec((G, bq, 1), c3map), pl.BlockSpec((bq, G * HD), qmap)],
               [pl.BlockSpec((t, G * HD), lambda h, j, i: (0, h)), kspec, kspec, pl.BlockSpec((G, 1, bk), lambda h, j, i: (h, 0, j)),
                pl.BlockSpec((G, t, 1), lambda h, j, i: (h, 0, 0))],
               [_sds((t, D)), _sds((t, D)), _sds((t, D)), _sds((NH1, 1, t)), _sds((NH1, t, 1))],
               scratch=[pltpu.VMEM((bk, G * HD), F32), pltpu.VMEM((bk, G * HD), F32), pltpu.VMEM((G, 1, bk), F32)],
               )(qn, kn, vb, crow, lse, delta, do)


def _post1_fwd(o, z1, w, h3, name, bm=512):
    t = o.shape[0]
    bm = min(bm, t)

    def body(o_ref, g_ref, w_ref, h_ref, out_ref, og_ref):
        og_ref[...] = (o_ref[...] * _sigmoid(g_ref[...])).astype(BF16)
        out_ref[...] = h_ref[...] + _nn(og_ref[...], w_ref[...])

    row = lambda c: pl.BlockSpec((bm, D), lambda i: (i, c))
    return _pc(body, name, (t // bm,), [row(0), row(3), pl.BlockSpec((D, D), lambda i: (0, 0)), row(0)],
               [row(0), row(0)], [_sds((t, D)), _sds((t, D), BF16)])(o, z1, w, h3)


def _post1_bwd(dh4, w, o, z1, name, bm=512):
    t = o.shape[0]
    bm = min(bm, t)

    def body(dh_ref, w_ref, o_ref, g_ref, do_ref, dg_ref, dl_ref):
        d_og = _nt(dh_ref[...].astype(BF16), w_ref[...])
        o_, sg = o_ref[...], _sigmoid(g_ref[...])
        dob = (d_og * sg).astype(BF16)
        do_ref[...] = dob
        dg_ref[...] = (d_og * o_ * sg * (1.0 - sg)).astype(BF16)
        prod = dob.astype(F32) * o_
        for hd in range(NH1):
            dl_ref[hd] = jnp.sum(prod[:, hd * HD:(hd + 1) * HD], axis=1, keepdims=True)

    row = lambda c: pl.BlockSpec((bm, D), lambda i: (i, c))
    return _pc(body, name, (t // bm,), [row(0), pl.BlockSpec((D, D), lambda i: (0, 0)), row(0), row(3)],
               [row(0), row(0), pl.BlockSpec((NH1, bm, 1), lambda i: (0, i, 0))],
               [_sds((t, D), BF16), _sds((t, D), BF16), _sds((NH1, t, 1))])(dh4, w, o, z1)


def _final(h, g, tgt, name, bm=512):
    t = h.shape[0]
    bm = min(bm, t)

    def body(h_ref, g_ref, t_ref, l_ref, dh_ref, dg_ref):
        @pl.when(pl.program_id(0) == 0)
        def _():
            l_ref[...] = jnp.zeros_like(l_ref)
            dg_ref[...] = jnp.zeros_like(dg_ref)

        x, gv = h_ref[...], g_ref[...]
        r = _rstd(x)
        xh = x * r
        e = xh * gv - t_ref[...]
        l_ref[...] += 0.5 * jnp.sum(jnp.mean(e * e, axis=1, keepdims=True), axis=0, keepdims=True)
        dy = e * (1.0 / D)
        dg_ref[...] += jnp.sum(dy * xh, axis=0, keepdims=True)
        dxh = dy * gv
        dh_ref[...] = r * (dxh - xh * jnp.mean(dxh * xh, axis=1, keepdims=True))

    row = pl.BlockSpec((bm, D), lambda i: (i, 0))
    vec = pl.BlockSpec((1, D), lambda i: (0, 0))
    return _pc(body, name, (t // bm,), [row, vec, row], [pl.BlockSpec((1, HD), lambda i: (0, 0)), row, vec],
               [_sds((1, HD)), _sds((t, D)), _sds((1, D))])(h, g, tgt)


def _adam(w, g, m, v, name):
    r, c = w.shape
    br = min(r, 256)

    def body(w_ref, g_ref, m_ref, v_ref, d_ref, mo_ref, vo_ref):
        gv = g_ref[...]
        mn = ADAM_B1 * m_ref[...] + (1.0 - ADAM_B1) * gv
        vn = ADAM_B2 * v_ref[...] + (1.0 - ADAM_B2) * jnp.square(gv)
        m_hat = mn / (1.0 - ADAM_B1 ** ADAM_STEP)
        v_hat = vn / (1.0 - ADAM_B2 ** ADAM_STEP)
        d_ref[...] = -ADAM_LR * (m_hat / (jnp.sqrt(v_hat) + ADAM_EPS) + ADAM_WD * w_ref[...])
        mo_ref[...] = mn
        vo_ref[...] = vn

    blk = pl.BlockSpec((br, c), lambda i: (i, 0))
    return _pc(body, name, (r // br,), [blk] * 4, [blk] * 3, [_sds((r, c))] * 3)(w, g, m, v)


ZW = 4224
GATE0 = 4096


def _pack_w_in0(w):
    return jnp.concatenate([w[:, :2048], w[:, 2056:], w[:, 2048:2056], jnp.zeros((w.shape[0], ZW - 4104), w.dtype)], axis=1)


def _unpack_w_in0(g):
    return jnp.concatenate([g[:, :2048], g[:, GATE0:GATE0 + 8], g[:, 2048:GATE0]], axis=1)


def _pack_w_in1(w):
    return jnp.concatenate([w, jnp.zeros((w.shape[0], ZW - 4104), w.dtype)], axis=1)


def _unpack_w_in1(g):
    return g[:, :4104]


def _local_step(x, mem, tgt, W, S, late_weights=None, grads_hook=None):
    t = x.shape[0]
    row = lambda v: v.reshape(1, -1)
    G = {}

    kv, mn = _memkv_fwd(mem, row(S["mem_norm_g"]), W["wkv_s"], "memkv_fwd")
    z0, u0 = _norm_mm(x, S["norm_mix_g"][0:1], W["w_in0"], "in0_fwd")
    qk = _conv_fwd(z0, S["conv_w"], "conv_fwd")
    g8 = z0[:, GATE0:GATE0 + 8]
    gates3 = jnp.stack([g8[:, :4].T, g8[:, 4:].T], axis=-1)
    gb = S["gate_b"]
    bias3 = jnp.stack([gb[0, :4], gb[0, 4:]], axis=-1)[:, None, :]
    hm, cs, ns, ms = _mlstm_fwd(qk, z0, gates3, bias3, "mlstm_fwd")
    hh, ss = _hgrn_fwd(z0, S["lb_logits"], "hgrn_fwd")
    if late_weights is not None:
        W = {**W, **late_weights(hh)}
    h1, y0 = _post0_fwd(hm, hh, z0, S["mlstm_norm_g"], S["hgrn_norm_g"], W["w_out0"], x, "post0_fwd")

    def xattn_mlp_fwd(h, l):
        q, ux = _norm_mm(h, S["norm_xattn_g"][l:l + 1], W["wq"][l], f"xq{l}_fwd")
        h2, ox = _xattn_fwd(q, kv, W["wo"][l], h, f"xattn{l}_fwd")
        h3, a, um = _mlp_fwd(h2, S["norm_mlp_g"][l:l + 1], W["w1s"], W["w2"], l, f"mlp{l}_fwd")
        return h3, (h, q, ux, ox, h2, a, um)

    h3, sv0 = xattn_mlp_fwd(h1, 0)
    z1, u1 = _norm_mm(h3, S["norm_mix_g"][1:2], W["w_in1"], "in1_fwd")
    fbp = jnp.pad(S["c_fgate_b"], ((0, 0), (0, HD - NH1)))
    qn, kn, vb, c = _foxprep_fwd(z1, S["c_qnorm_g"], S["c_knorm_g"], fbp, "foxprep_fwd")
    crow = (c[:, :NH1] * LOG2E).T[:, None, :]
    o1, lse = _fox_fwd(qn, kn, vb, crow, "fox_fwd")
    h4, og = _post1_fwd(o1, z1, W["w_out1"], h3, "post1_fwd")
    h6, sv1 = xattn_mlp_fwd(h4, 1)
    lossp, dh, G["final_norm_g"] = _final(h6, row(S["final_norm_g"]), tgt, "final")

    grads_ready = grads_hook if grads_hook is not None else (lambda stage, grads: 0.0)
    dkv = None
    dgx, dgm, dwq, dwo, dw1, dw2 = [None, None], [None, None], [None, None], [None, None], [None, None], [None, None]

    def xattn_mlp_bwd(dh, l, sv):
        nonlocal dkv
        h, q, ux, ox, h2, a, um = sv
        dh2, da, r, dgm[l] = _mlp_bwd(dh, a, W["w1s"], W["w2"], l, h2, S["norm_mlp_g"][l:l + 1], f"mlp{l}_bwd")
        dw1[l] = _mm_tn(um, da, f"mlp{l}_dw1", col_chips=NCHIP)
        dw2[l] = _mm_tn(r, dh, f"mlp{l}_dw2")
        dq, dkv_l = _xattn_bwd(dh2, q, kv, W["wo"][l], f"xattn{l}_bwd")
        dkv = dkv_l if dkv is None else dkv + dkv_l
        dwo[l] = _mm_tn(ox, dh2, f"xattn{l}_dwo")
        dwq[l] = _mm_tn(ux, dq, f"xattn{l}_dwq")
        tok = grads_ready("layer0_mlp_xattn", dict(wq=dwq[0], wo=dwo[0], w1=dw1[0], w2=dw2[0])) if l == 0 else 0.0
        dh1, dgx[l] = _bwd_in(dq, W["wq"][l], h, S["norm_xattn_g"][l:l + 1] + tok, dh2, f"xq{l}_bwd")
        return dh1

    dh4 = xattn_mlp_bwd(dh, 1, sv1)
    do, dgate, delta = _post1_bwd(dh4, W["w_out1"], o1, z1, "post1_bwd")
    G["w_out1"] = _mm_tn(og, dh4, "post1_dw")
    dqn, dkn, dv1, dcrow, dcq = _fox_bwd(qn, kn, vb, crow, lse, delta, do, "fox_bwd")
    dc = jnp.pad((dcrow[:, 0, :] + dcq[:, :, 0]).T, ((0, 0), (0, HD - NH1)))
    dqr, dkr, df1, G["c_qnorm_g"], G["c_knorm_g"], dfb = _foxprep_bwd(
        dqn, dkn, z1, S["c_qnorm_g"], S["c_knorm_g"], fbp, dc, "foxprep_bwd")
    G["c_fgate_b"] = dfb[:, :NH1]
    dz1 = jnp.concatenate([dqr, dkr, dv1.astype(BF16), dgate, df1], axis=1)
    G["w_in1"] = _mm_tn(u1, dz1, "in1_dw")
    tok = grads_ready("layer1", dict(w_out=G["w_out1"], w_in=G["w_in1"], wq=dwq[1], wo=dwo[1], w1=dw1[1], w2=dw2[1]))
    dh3, dgmix1 = _bwd_in(dz1, W["w_in1"], h3, S["norm_mix_g"][1:2] + tok, dh4, "in1_bwd")
    dh1 = xattn_mlp_bwd(dh3, 0, sv0)

    dhm, dhh, doa, dgb, G["mlstm_norm_g"], G["hgrn_norm_g"] = _post0_bwd(
        dh1, W["w_out0"], hm, hh, z0, S["mlstm_norm_g"], S["hgrn_norm_g"], "post0_bwd")
    G["w_out0"] = _mm_tn(y0, dh1, "post0_dw")
    dqa, dka, dva, dgates3 = _mlstm_bwd(qk, z0, gates3, bias3, cs, ns, ms, dhm, "mlstm_bwd")
    dqb, dfb0, dib, G["lb_logits"] = _hgrn_bwd(z0, S["lb_logits"], ss, dhh, "hgrn_bwd")
    duc, G["conv_w"] = _conv_bwd(z0, S["conv_w"], jnp.concatenate([dqa, dka], axis=1), "conv_bwd")
    dg8 = jnp.concatenate([dgates3[:, :, 0].T, dgates3[:, :, 1].T], axis=1)
    G["gate_b"] = jnp.sum(dg8, axis=0, keepdims=True)
    dz0 = jnp.concatenate([duc, dva.astype(BF16), doa, dqb, dfb0, dib, dgb,
                           jnp.pad(dg8, ((0, 0), (0, HD - 8))).astype(BF16)], axis=1)
    G["w_in0"] = _mm_tn(u0, dz0, "in0_dw")
    dx, dgmix0 = _bwd_in(dz0, W["w_in0"], x, S["norm_mix_g"][0:1], dh1, "in0_bwd")

    G["wkv"] = _mm_tn(mn, dkv, "memkv_dw", col_chips=NCHIP)
    G["mem_norm_g"] = _memkv_bwd(dkv, W["wkv_s"], mem, row(S["mem_norm_g"]), "memkv_bwd")
    G["norm_mix_g"] = jnp.concatenate([dgmix0, dgmix1], axis=0)
    G["norm_xattn_g"] = jnp.concatenate(dgx, axis=0)
    G["norm_mlp_g"] = jnp.concatenate(dgm, axis=0)
    G["wq"], G["wo"], G["w1"], G["w2"] = dwq, dwo, dw1, dw2
    return lossp[0, 0], dx, G


ANY = pl.BlockSpec(memory_space=pl.ANY)
NCHIP = 4
RS_ROWS = 4224
RS_TILE = 384


def _place():
    x, y, c = lax.axis_index("x"), lax.axis_index("y"), lax.axis_index("c")
    return x, y, c, [(1 - x, y), (x, 1 - y), (1 - x, 1 - y)]


def _comm_call(body, name, ins, out_shapes, sems):
    return pl.pallas_call(body, name=name, in_specs=[ANY] * len(ins), out_specs=[ANY] * len(out_shapes),
                          out_shape=out_shapes, scratch_shapes=sems)(*ins)


def _gather_weights(arrs, name):
    n = len(arrs)

    def body(*refs):
        ins, outs = refs[:n], refs[n:2 * n]
        send_i, recv_i, send_d, recv_d = refs[2 * n:]
        x, y, c, chips = _place()
        me = 2 * x + y

        def half(a, cc):
            h = arrs[a].shape[0] // 2
            return pl.ds(pl.multiple_of(cc * h, h), h)

        def ici(a, k, src_chip, dst_dev):
            return pltpu.make_async_remote_copy(
                src_ref=ins[a].at[half(a, c)], dst_ref=outs[a].at[src_chip, half(a, c)], send_sem=send_i.at[a, k],
                recv_sem=recv_i.at[a, k], device_id=dst_dev, device_id_type=MESH)

        def d2d(a, k, src_chip, cc):
            reg = outs[a].at[src_chip, half(a, cc)]
            return pltpu.make_async_remote_copy(src_ref=reg, dst_ref=reg, send_sem=send_d.at[a, k], recv_sem=recv_d.at[a, k],
                                                device_id=(x, y, 1 - c), device_id_type=MESH)

        for a in range(n):
            for k, (px, py) in enumerate(chips):
                ici(a, k, me, (px, py, c)).start()
        for k, (px, py) in enumerate(chips):
            for a in range(n):
                ici(a, k, 2 * px + py, (px, py, c)).wait_recv()
                d2d(a, k, 2 * px + py, c).start()
        for k, (px, py) in enumerate(chips):
            for a in range(n):
                ici(a, k, me, (px, py, c)).wait_send()
                d2d(a, k, 2 * px + py, c).wait_send()
                d2d(a, k, 2 * px + py, 1 - c).wait_recv()

    sem = lambda: pltpu.SemaphoreType.DMA((n, 3))
    return _comm_call(body, name, arrs, [_sds((NCHIP,) + a.shape, a.dtype) for a in arrs], [sem(), sem(), sem(), sem()])


HBM = pl.BlockSpec(memory_space=pltpu.HBM)
SEM = pl.BlockSpec(memory_space=pltpu.SEMAPHORE)
DATAFLOW = pltpu.SideEffectType.DATAFLOW_SIDE_EFFECTING


def _half_rows(r, cc):
    return pl.ds(pl.multiple_of(cc * (r // 2), r // 2), r // 2)


def _gather_start(arrs, after, name):
    n = len(arrs)

    def body(*refs):
        ins, lands = refs[:n], refs[n:2 * n]
        send, recv, token = refs[2 * n + 1], refs[2 * n + 2], refs[-1]
        x, y, c, chips = _place()
        me = 2 * x + y
        for a in range(n):
            rows = _half_rows(arrs[a].shape[0], c)
            for k, (px, py) in enumerate(chips):
                pltpu.make_async_remote_copy(src_ref=ins[a].at[rows], dst_ref=lands[a].at[me, rows], send_sem=send.at[3 * a + k],
                                             recv_sem=recv.at[3 * a + k], device_id=(px, py, c), device_id_type=MESH).start()
        token[...] = jnp.zeros_like(token)

    hbm = lambda v: pltpu.with_memory_space_constraint(v, pltpu.HBM)
    land_shapes = [((NCHIP,) + a.shape, a.dtype) for a in arrs]
    out = pl.pallas_call(
        body, name=name,
        out_shape=(pltpu.SemaphoreType.DMA((3 * n,)), pltpu.SemaphoreType.DMA((3 * n,)), *[pltpu.HBM(a.shape, a.dtype) for a in arrs],
                   *[pltpu.HBM(s, d) for s, d in land_shapes], _sds((8, HD))),
        in_specs=[HBM] * (2 * n) + [ANY], out_specs=(SEM, SEM, *[HBM] * (2 * n), pl.BlockSpec(memory_space=pltpu.VMEM)),
        input_output_aliases={i: 2 + i for i in range(2 * n)},
        compiler_params=pltpu.CompilerParams(has_side_effects=DATAFLOW),
    )(*[hbm(a) for a in arrs], *[hbm(lax.empty(s, d)) for s, d in land_shapes], after)
    return out[0], out[1], list(out[2:2 + n]), list(out[2 + n:2 + 2 * n]), out[-1]


def _gather_wait(send, recv, srcs, lands, after, name):
    n = len(srcs)

    def body(*refs):
        ins, lands_ = refs[:n], refs[n:2 * n]
        send_, recv_ = refs[2 * n], refs[2 * n + 1]
        x, y, c, chips = _place()
        for a in range(n):
            rows = _half_rows(srcs[a].shape[0], c)
            for k, (px, py) in enumerate(chips):
                cp = pltpu.make_async_remote_copy(src_ref=ins[a].at[rows], dst_ref=lands_[a].at[2 * px + py, rows], send_sem=send_.at[3 * a + k],
                                                  recv_sem=recv_.at[3 * a + k], device_id=(px, py, c), device_id_type=MESH)
                cp.wait_send()
                cp.wait_recv()

    out = pl.pallas_call(
        body, name=name, out_shape=[pltpu.HBM(v.shape, v.dtype) for v in list(srcs) + list(lands)],
        in_specs=[HBM] * (2 * n) + [SEM, SEM, ANY], out_specs=[HBM] * (2 * n), input_output_aliases={i: i for i in range(2 * n)},
        compiler_params=pltpu.CompilerParams(has_side_effects=DATAFLOW),
    )(*srcs, *lands, send, recv, after)
    return list(out[n:])


def _pair_forward(lands, name):
    n = len(lands)

    def body(*refs):
        ins, outs = refs[:n], refs[n:2 * n]
        send, recv = refs[2 * n:]
        x, y, c, chips = _place()
        copies = []
        for a in range(n):
            r = lands[a].shape[1]
            for k, (px, py) in enumerate(chips):
                cp = pltpu.make_async_remote_copy(
                    src_ref=ins[a].at[2 * px + py, _half_rows(r, c)], dst_ref=outs[a].at[2 * px + py, _half_rows(r, c)],
                    send_sem=send.at[a, k], recv_sem=recv.at[a, k], device_id=(x, y, 1 - c), device_id_type=MESH)
                cp.start()
                copies.append(cp)
        for a in range(n):
            r = lands[a].shape[1]
            for k, (px, py) in enumerate(chips):
                pltpu.make_async_remote_copy(
                    src_ref=ins[a].at[2 * px + py, _half_rows(r, c)], dst_ref=outs[a].at[2 * px + py, _half_rows(r, 1 - c)],
                    send_sem=send.at[a, k], recv_sem=recv.at[a, k], device_id=(x, y, 1 - c), device_id_type=MESH).wait_recv()
        for cp in copies:
            cp.wait_send()

    return pl.pallas_call(body, name=name, in_specs=[ANY] * n, out_specs=[ANY] * n, out_shape=[_sds(v.shape, v.dtype) for v in lands],
                          scratch_shapes=[pltpu.SemaphoreType.DMA((n, 3)), pltpu.SemaphoreType.DMA((n, 3))],
                          input_output_aliases={i: i for i in range(n)})(*lands)


def _pair_exchange(arrs, name):
    n = len(arrs)

    def body(*refs):
        ins, outs = refs[:n], refs[n:2 * n]
        send, recv = refs[2 * n:]
        x, y, c, _ = _place()
        copies = []
        for a in range(n):
            h = arrs[a].shape[1] // 2
            cp = pltpu.make_async_remote_copy(src_ref=ins[a].at[:, pl.ds(pl.multiple_of((1 - c) * h, h), h)], dst_ref=outs[a],
                                              send_sem=send.at[a], recv_sem=recv.at[a], device_id=(x, y, 1 - c), device_id_type=MESH)
            cp.start()
            copies.append(cp)
        for cp in copies:
            cp.wait()

    return _comm_call(body, name, arrs, [_sds((a.shape[0], a.shape[1] // 2, a.shape[2]), a.dtype) for a in arrs],
                      [pltpu.SemaphoreType.DMA((n,)), pltpu.SemaphoreType.DMA((n,))])


def _chip_exchange(arrs, name):
    n = len(arrs)

    def body(*refs):
        ins, outs = refs[:n], refs[n:2 * n]
        send, recv = refs[2 * n:]
        x, y, c, chips = _place()
        me = 2 * x + y
        copies = []
        for a in range(n):
            for k, (px, py) in enumerate(chips):
                r = pltpu.make_async_remote_copy(src_ref=ins[a].at[2 * px + py], dst_ref=outs[a].at[me], send_sem=send.at[a, k],
                                                 recv_sem=recv.at[a, k], device_id=(px, py, c), device_id_type=MESH)
                r.start()
                copies.append(r)
        for cp in copies:
            cp.wait()

    return _comm_call(body, name, arrs, [_sds(a.shape, a.dtype) for a in arrs],
                      [pltpu.SemaphoreType.DMA((n, 3)), pltpu.SemaphoreType.DMA((n, 3))])


def _chip_exchange_start(arrs, name):
    n = len(arrs)

    def body(*refs):
        ins, lands = refs[:n], refs[n:2 * n]
        send, recv, token = refs[2 * n], refs[2 * n + 1], refs[-1]
        x, y, c, chips = _place()
        me = 2 * x + y
        for a in range(n):
            for k, (px, py) in enumerate(chips):
                pltpu.make_async_remote_copy(src_ref=ins[a].at[2 * px + py], dst_ref=lands[a].at[me], send_sem=send.at[3 * a + k],
                                             recv_sem=recv.at[3 * a + k], device_id=(px, py, c), device_id_type=MESH).start()
        token[...] = jnp.zeros_like(token)

    hbm = lambda v: pltpu.with_memory_space_constraint(v, pltpu.HBM)
    out = pl.pallas_call(
        body, name=name,
        out_shape=(pltpu.SemaphoreType.DMA((3 * n,)), pltpu.SemaphoreType.DMA((3 * n,)), *[pltpu.HBM(a.shape, a.dtype) for a in arrs],
                   *[pltpu.HBM(a.shape, a.dtype) for a in arrs], _sds((8, HD))),
        in_specs=[HBM] * (2 * n), out_specs=(SEM, SEM, *[HBM] * (2 * n), pl.BlockSpec(memory_space=pltpu.VMEM)),
        input_output_aliases={i: 2 + i for i in range(2 * n)},
        compiler_params=pltpu.CompilerParams(has_side_effects=DATAFLOW),
    )(*[hbm(a) for a in arrs], *[hbm(lax.empty(a.shape, a.dtype)) for a in arrs])
    return out[0], out[1], list(out[2:2 + n]), list(out[2 + n:2 + 2 * n]), out[-1]


def _chip_exchange_wait(send, recv, srcs, lands, after, name):
    n = len(srcs)

    def body(*refs):
        ins, lands_ = refs[:n], refs[n:2 * n]
        send_, recv_ = refs[2 * n], refs[2 * n + 1]
        x, y, c, chips = _place()
        for a in range(n):
            for k, (px, py) in enumerate(chips):
                cp = pltpu.make_async_remote_copy(src_ref=ins[a].at[2 * px + py], dst_ref=lands_[a].at[2 * px + py], send_sem=send_.at[3 * a + k],
                                                  recv_sem=recv_.at[3 * a + k], device_id=(px, py, c), device_id_type=MESH)
                cp.wait_send()
                cp.wait_recv()

    out = pl.pallas_call(
        body, name=name, out_shape=[pltpu.HBM(v.shape, v.dtype) for v in list(srcs) + list(lands)],
        in_specs=[HBM] * (2 * n) + [SEM, SEM, ANY], out_specs=[HBM] * (2 * n), input_output_aliases={i: i for i in range(2 * n)},
        compiler_params=pltpu.CompilerParams(has_side_effects=DATAFLOW),
    )(*srcs, *lands, send, recv, after)
    return list(out[n:])


def _pair_swap(arrs, name):
    n = len(arrs)

    def body(*refs):
        ins, outs = refs[:n], refs[n:2 * n]
        send, recv = refs[2 * n:]
        x, y, c, _ = _place()
        copies = []
        for a in range(n):
            cp = pltpu.make_async_remote_copy(src_ref=ins[a], dst_ref=outs[a], send_sem=send.at[a], recv_sem=recv.at[a],
                                              device_id=(x, y, 1 - c), device_id_type=MESH)
            cp.start()
            copies.append(cp)
        for cp in copies:
            cp.wait()

    return _comm_call(body, name, arrs, [_sds(a.shape, a.dtype) for a in arrs],
                      [pltpu.SemaphoreType.DMA((n,)), pltpu.SemaphoreType.DMA((n,))])


def _all_gather_devices(v, name):
    def body(v_ref, o_ref, send, recv, loc):
        x, y, c, _ = _place()
        me = 4 * x + 2 * y + c
        own = pltpu.make_async_copy(v_ref, o_ref.at[me], loc)
        own.start()
        copies = [own]
        for k in range(1, 8):
            fx, fy, fc = (k >> 2) & 1, (k >> 1) & 1, k & 1
            peer = (x ^ fx, y ^ fy, c ^ fc)
            r = pltpu.make_async_remote_copy(src_ref=v_ref, dst_ref=o_ref.at[me], send_sem=send.at[k - 1],
                                             recv_sem=recv.at[k - 1], device_id=peer, device_id_type=MESH)
            r.start()
            copies.append(r)
        for cp in copies:
            cp.wait()

    return _comm_call(body, name, [v], [_sds((8,) + v.shape, v.dtype)],
                      [pltpu.SemaphoreType.DMA((7,)), pltpu.SemaphoreType.DMA((7,)), pltpu.SemaphoreType.DMA])[0]


def _row_tile(r):
    return next((b for b in (512, 384, 256, 128, 64, 32, 16) if r % b == 0), r)


def _add2(a, b, out_dtype, name):
    r, w = a.shape
    br = _row_tile(r)

    def body(a_ref, b_ref, o_ref):
        o_ref[...] = (a_ref[...].astype(F32) + b_ref[...].astype(F32)).astype(out_dtype)

    blk = pl.BlockSpec((br, w), lambda i: (i, 0))
    return _pc(body, name, (r // br,), [blk, blk], blk, _sds((r, w), out_dtype))(a, b)


def _sum_slots(a, out_dtype, name, extra=None):
    n, r, w = a.shape
    br = _row_tile(r)

    def body(*refs):
        a_ref, o_ref = refs[0], refs[-1]
        acc = a_ref[0].astype(F32)
        for s in range(1, n):
            acc = acc + a_ref[s].astype(F32)
        if extra is not None:
            acc = acc + refs[1][...].astype(F32)
        o_ref[...] = acc.astype(out_dtype)

    ins = [a] + ([extra] if extra is not None else [])
    specs = [pl.BlockSpec((n, br, w), lambda i: (0, i, 0))] + ([pl.BlockSpec((br, w), lambda i: (i, 0))] if extra is not None else [])
    return _pc(body, name, (r // br,), specs, pl.BlockSpec((br, w), lambda i: (i, 0)), _sds((r, w), out_dtype))(*ins)


SMALL = ["norm_mix_g", "norm_xattn_g", "norm_mlp_g", "final_norm_g", "mem_norm_g", "hgrn_lb_logits", "mlstm_norm_g",
         "hgrn_norm_g", "c_qnorm_g", "c_knorm_g", "ab_gate_b", "c_fgate_b"]
SMALL_ROWS = 16


def _pack_small(parts):
    flat = jnp.concatenate([p.reshape(-1).astype(F32) for p in parts])
    return jnp.pad(flat, (0, SMALL_ROWS * D - flat.shape[0])).reshape(SMALL_ROWS, D)


def _unpack_small(buf, shapes):
    flat, out, off = buf.reshape(-1), [], 0
    for s in shapes:
        n = 1
        for d in s:
            n *= d
        out.append(flat[off:off + n].reshape(s))
        off += n
    return out


def kernel(x, mem, norm_mix_g, norm_xattn_g, norm_mlp_g, final_norm_g, ab_w_in, ab_conv_w, ab_gate_b, hgrn_lb_logits, mlstm_norm_g, hgrn_norm_g, ab_w_out, c_w_in, c_fgate_b, c_qnorm_g, c_knorm_g, c_w_out, mem_norm_g, mem_w_kv, xa_w_q, xa_w_o, mlp_w1, mlp_w2, loss_target, m_norm_mix_g, m_norm_xattn_g, m_norm_mlp_g, m_final_norm_g, m_ab_w_in, m_ab_conv_w, m_ab_gate_b, m_hgrn_lb_logits, m_mlstm_norm_g, m_hgrn_norm_g, m_ab_w_out, m_c_w_in, m_c_fgate_b, m_c_qnorm_g, m_c_knorm_g, m_c_w_out, m_mem_norm_g, m_mem_w_kv, m_xa_w_q, m_xa_w_o, m_mlp_w1, m_mlp_w2, v_norm_mix_g, v_norm_xattn_g, v_norm_mlp_g, v_final_norm_g, v_ab_w_in, v_ab_conv_w, v_ab_gate_b, v_hgrn_lb_logits, v_mlstm_norm_g, v_hgrn_norm_g, v_ab_w_out, v_c_w_in, v_c_fgate_b, v_c_qnorm_g, v_c_knorm_g, v_c_w_out, v_mem_norm_g, v_mem_w_kv, v_xa_w_q, v_xa_w_o, v_mlp_w1, v_mlp_w2):
    A = dict(locals())
    chip = 2 * lax.axis_index("x") + lax.axis_index("y")

    big = ["ab_w_in", "c_w_in", "ab_w_out", "c_w_out", "mem_w_kv", "xa_w_q", "xa_w_o", "mlp_w1", "mlp_w2"]
    shard2d = {"ab_w_in": (D, 1026), "c_w_in": (D, 1026), "ab_w_out": (256, D), "c_w_out": (256, D), "mem_w_kv": (D, 512),
               "xa_w_q": (512, D), "xa_w_o": (512, D), "mlp_w1": (2 * D, D), "mlp_w2": (2 * D, D)}
    shard16 = lambda n: A[n].reshape(shard2d[n]).astype(BF16)
    own_slot = lambda gs, os: [lax.dynamic_update_index_in_dim(g, o, chip, 0) for g, o in zip(gs, os)]
    cols = lambda g: jnp.concatenate([g[k] for k in range(NCHIP)], axis=1)
    per_layer = lambda g: g.reshape(NCHIP, 2, -1, D).transpose(1, 0, 2, 3)
    first = [shard16("ab_w_in"), shard16("mem_w_kv"), jnp.pad(ab_conv_w[0], ((0, 16 - CONV_W), (0, 0)))]
    g_in0, g_kv, g_conv = own_slot(_gather_weights(first, "gather_first"), first)
    W = dict(w_in0=_pack_w_in0(cols(g_in0)), wkv_s=g_kv)
    rest_names = ["c_w_in", "ab_w_out", "c_w_out", "xa_w_q", "xa_w_o", "mlp_w1", "mlp_w2"]
    rest = [shard16(n) for n in rest_names]
    send_s, recv_s, srcs, lands, token = _gather_start(rest, g_conv, "gather_rest_start")

    def late_weights(after):
        got = _pair_forward(_gather_wait(send_s, recv_s, srcs, lands, after, "gather_rest_wait"), "gather_rest_forward")
        gw = dict(zip(rest_names, own_slot(got, rest)))
        return dict(w_in1=_pack_w_in1(cols(gw["c_w_in"])), w_out0=gw["ab_w_out"].reshape(D, D), w_out1=gw["c_w_out"].reshape(D, D),
                    wq=per_layer(gw["xa_w_q"]).reshape(2, D, D), wo=per_layer(gw["xa_w_o"]).reshape(2, D, D),
                    w1s=gw["mlp_w1"].reshape(NCHIP, 2, D, D), w2=gw["mlp_w2"].reshape(NCHIP, 2, D, D))

    S = dict(norm_mix_g=norm_mix_g + token[0, 0], norm_xattn_g=norm_xattn_g, norm_mlp_g=norm_mlp_g, final_norm_g=final_norm_g,
             conv_w=cols(g_conv[:, :CONV_W]), gate_b=ab_gate_b, lb_logits=hgrn_lb_logits, mlstm_norm_g=mlstm_norm_g,
             hgrn_norm_g=hgrn_norm_g, c_fgate_b=c_fgate_b, c_qnorm_g=c_qnorm_g, c_knorm_g=c_knorm_g, mem_norm_g=mem_norm_g)

    core = lax.axis_index("c")
    by_rows = lambda g: g.reshape(NCHIP, -1, D)

    def stack_cols(g):
        return jnp.stack([g[:, 1026 * k:1026 * (k + 1)] for k in range(NCHIP)])

    def pair_sums(arrs, tag):
        theirs = _pair_exchange(arrs, f"pair_exchange_{tag}")
        out = []
        for i, (a, th) in enumerate(zip(arrs, theirs)):
            h = a.shape[1] // 2
            mine = lax.dynamic_slice_in_dim(a, core * h, h, axis=1)
            out.append(_add2(mine.reshape(-1, a.shape[2]), th.reshape(-1, a.shape[2]), BF16, f"pair_sum_{tag}{i}").reshape(th.shape))
        return out

    def chip_sums(psums, from_chips, tag):
        out = []
        for i, (f, p) in enumerate(zip(from_chips, psums)):
            f = lax.dynamic_update_index_in_dim(f, lax.dynamic_index_in_dim(p, chip, 0, keepdims=False), chip, 0)
            out.append(_sum_slots(f, F32, f"chip_sum_{tag}{i}"))
        return out

    started = {}

    def grads_hook(stage, g):
        if stage == "layer1":
            arrs = [jnp.concatenate([by_rows(g["w_out"]), by_rows(g["wq"]), by_rows(g["wo"]), g["w1"], by_rows(g["w2"])], axis=1),
                    stack_cols(_unpack_w_in1(g["w_in"]))]
        else:
            arrs = [jnp.concatenate([by_rows(g["wq"]), by_rows(g["wo"]), g["w1"], by_rows(g["w2"])], axis=1)]
        psums = pair_sums(arrs, stage)
        *handles, token = _chip_exchange_start(psums, f"chip_exchange_start_{stage}")
        started[stage] = (psums, handles)
        return token[0, 0]

    lossp, dx, G = _local_step(x[0], mem[0], loss_target[0], W, S, late_weights, grads_hook)

    gsmall = {"norm_mix_g": G["norm_mix_g"], "norm_xattn_g": G["norm_xattn_g"], "norm_mlp_g": G["norm_mlp_g"],
              "final_norm_g": G["final_norm_g"], "mem_norm_g": G["mem_norm_g"], "hgrn_lb_logits": G["lb_logits"],
              "mlstm_norm_g": G["mlstm_norm_g"], "hgrn_norm_g": G["hgrn_norm_g"], "c_qnorm_g": G["c_qnorm_g"],
              "c_knorm_g": G["c_knorm_g"], "ab_gate_b": G["gate_b"], "c_fgate_b": G["c_fgate_b"]}
    packed = _pack_small([gsmall[n] for n in SMALL] + [G["conv_w"], lossp])
    red = _sum_slots(_all_gather_devices(packed, "gather_small"), F32, "sum_small")
    small_shapes = [A[n].shape for n in SMALL]
    *gs, gconv, loss = _unpack_small(red, small_shapes + [(CONV_W, D), ()])
    gs = dict(zip(SMALL, gs))
    gconv = lax.dynamic_slice_in_dim(gconv, chip * 256, 256, axis=1)[None]

    last = pair_sums([by_rows(G["w_out0"]), stack_cols(_unpack_w_in0(G["w_in0"])), G["wkv"]], "last")
    rhalf = chip_sums(last, _chip_exchange(last, "chip_exchange_last"), "last")
    for stage in ("layer1", "layer0_mlp_xattn"):
        psums, handles = started[stage]
        rhalf += chip_sums(psums, _chip_exchange_wait(*handles, dx, f"chip_exchange_wait_{stage}"), stage)
    other = _pair_swap(rhalf, "pair_swap")
    r_out0, r_in0, r_kv, r_l1, r_in1, r_l0 = [
        jnp.where(core == 0, jnp.concatenate([m_, o_], axis=0), jnp.concatenate([o_, m_], axis=0)) for m_, o_ in zip(rhalf, other)]
    gbig = {"ab_w_in": r_in0, "c_w_in": r_in1, "mem_w_kv": r_kv, "ab_w_out": r_out0, "c_w_out": r_l1[0:256],
            "xa_w_q": jnp.concatenate([r_l0[0:256], r_l1[256:512]], axis=0),
            "xa_w_o": jnp.concatenate([r_l0[256:512], r_l1[512:768]], axis=0),
            "mlp_w1": jnp.concatenate([r_l0[512:1536], r_l1[768:1792]], axis=0),
            "mlp_w2": jnp.concatenate([r_l0[1536:2560], r_l1[1792:2816]], axis=0)}

    out_g, out_d, out_m, out_v = {}, {}, {}, {}
    for n in big:
        d_, m_, v_ = _adam(A[n].reshape(shard2d[n]), gbig[n], A["m_" + n].reshape(shard2d[n]), A["v_" + n].reshape(shard2d[n]), "adam_" + n)
        out_g[n] = gbig[n].reshape(A[n].shape)
        out_d[n], out_m[n], out_v[n] = d_.reshape(A[n].shape), m_.reshape(A[n].shape), v_.reshape(A[n].shape)
    sd, sm, sv = _adam(_pack_small([A[n] for n in SMALL]), _pack_small([gs[n] for n in SMALL]),
                       _pack_small([A["m_" + n] for n in SMALL]), _pack_small([A["v_" + n] for n in SMALL]), "adam_small")
    for n, d_, m_, v_ in zip(SMALL, _unpack_small(sd, small_shapes), _unpack_small(sm, small_shapes), _unpack_small(sv, small_shapes)):
        out_g[n], out_d[n], out_m[n], out_v[n] = gs[n], d_, m_, v_
    cd, cm_, cv = _adam(ab_conv_w[0], gconv[0], m_ab_conv_w[0], v_ab_conv_w[0], "adam_conv")
    out_g["ab_conv_w"], out_d["ab_conv_w"], out_m["ab_conv_w"], out_v["ab_conv_w"] = gconv, cd[None], cm_[None], cv[None]

    order = ["norm_mix_g", "norm_xattn_g", "norm_mlp_g", "final_norm_g", "ab_w_in", "ab_conv_w", "ab_gate_b", "hgrn_lb_logits",
             "mlstm_norm_g", "hgrn_norm_g", "ab_w_out", "c_w_in", "c_fgate_b", "c_qnorm_g", "c_knorm_g", "c_w_out", "mem_norm_g",
             "mem_w_kv", "xa_w_q", "xa_w_o", "mlp_w1", "mlp_w2"]
    return (loss, dx[None], *[out_g[n] for n in order], *[out_d[n] for n in order], *[out_m[n] for n in order],
            *[out_v[n] for n in order])
```

```python
import functools

import jax
import jax.numpy as jnp
from jax import lax
from jax.experimental import pallas as pl
from jax.experimental.pallas import tpu as pltpu

F32 = jnp.float32
BF16 = jnp.bfloat16
EPS = 1e-6
D = 1024
CHUNK = 64
HD = 128
XD = 256
NEG = -1e30
VMEM_LIMIT_V7X = 56 * 1024 * 1024
ADAM_LR, ADAM_B1, ADAM_B2, ADAM_EPS, ADAM_WD, ADAM_STEP = 0.001, 0.9, 0.999, 1e-08, 0.01, 10
MESH = pl.DeviceIdType.MESH


def _pc(body, name, grid, in_specs, out_specs, out_shape, scratch=(), **kw):
    return pl.pallas_call(
        body, name=name, grid=grid, in_specs=in_specs, out_specs=out_specs, out_shape=out_shape,
        scratch_shapes=scratch,
        compiler_params=pltpu.CompilerParams(
            dimension_semantics=("arbitrary",) * len(grid), vmem_limit_bytes=VMEM_LIMIT_V7X), **kw)


def _sds(shape, dtype=F32):
    return jax.ShapeDtypeStruct(shape, dtype)


def _blk(n, target):
    return max(b for b in range(128, max(target, 128) + 1, 128) if n % b == 0)


def _dot(a, b, dims):
    return lax.dot_general(a, b, (dims, ((), ())), preferred_element_type=F32)


def _nn(a, b):
    return _dot(a, b, ((1,), (0,)))


def _nt(a, b):
    return _dot(a, b, ((1,), (1,)))


def _tn(a, b):
    return _dot(a, b, ((0,), (0,)))


def _sigmoid(x):
    return 1.0 / (1.0 + jnp.exp(-x))


def _log_sigmoid(x):
    return jnp.minimum(x, 0.0) - jnp.log(1.0 + jnp.exp(-jnp.abs(x)))


def _rstd(x):
    return lax.rsqrt(jnp.mean(x * x, axis=-1, keepdims=True) + EPS)


def _rms_bwd(du, x, g):
    r = _rstd(x)
    xh = x * r
    dxh = du * g
    dx = r * (dxh - xh * jnp.mean(dxh * xh, axis=-1, keepdims=True))
    return dx, du * xh


def _norm_mm(h, g, w, name, bm=1024, bn=512):
    t, n = h.shape[0], w.shape[1]
    bm, bn = min(bm, t), _blk(n, 3 * bn)

    def body(h_ref, g_ref, w_ref, z_ref, u_ref):
        @pl.when(pl.program_id(1) == 0)
        def _():
            x = h_ref[...]
            u_ref[...] = (x * _rstd(x) * g_ref[...]).astype(BF16)
        z_ref[...] = _nn(u_ref[...], w_ref[...])

    return _pc(body, name, (t // bm, n // bn),
               [pl.BlockSpec((bm, D), lambda i, j: (i, 0)), pl.BlockSpec((1, D), lambda i, j: (0, 0)),
                pl.BlockSpec((D, bn), lambda i, j: (0, j))],
               [pl.BlockSpec((bm, bn), lambda i, j: (i, j)), pl.BlockSpec((bm, D), lambda i, j: (i, 0))],
               [_sds((t, n)), _sds((t, D), BF16)])(h, g, w)


def _mm_tn(a, b, name, bm=1024, bn=1024, bt=2048, col_chips=None):
    t, m = a.shape
    n = b.shape[1]
    bm, bn, bt = _blk(m, bm), (n // col_chips if col_chips else _blk(n, bn + bn // 2)), min(bt, t)
    nt = t // bt

    def body(a_ref, b_ref, o_ref, acc):
        k = pl.program_id(2)

        @pl.when(k == 0)
        def _():
            acc[...] = jnp.zeros_like(acc)

        acc[...] += _tn(a_ref[...].astype(BF16), b_ref[...].astype(BF16))

        @pl.when(k == nt - 1)
        def _():
            o_ref[...] = acc[...].astype(BF16)

    if col_chips:
        out_spec, out_shape = pl.BlockSpec((None, bm, bn), lambda i, j, k: (j, i, 0)), _sds((col_chips, m, bn), BF16)
    else:
        out_spec, out_shape = pl.BlockSpec((bm, bn), lambda i, j, k: (i, j)), _sds((m, n), BF16)
    return _pc(body, name, (m // bm, n // bn, nt),
               [pl.BlockSpec((bt, bm), lambda i, j, k: (k, i)), pl.BlockSpec((bt, bn), lambda i, j, k: (k, j))],
               out_spec, out_shape, scratch=[pltpu.VMEM((bm, bn), F32)])(a, b)


def _bwd_in(dz, w, h, g, dh, name, bm=1024, bk=1024):
    t, n = dz.shape
    bm, bk = min(bm, t), _blk(n, bk + bk // 2)
    nk = n // bk

    def body(dz_ref, w_ref, h_ref, g_ref, dh_ref, o_ref, dg_ref, acc):
        i, k = pl.program_id(0), pl.program_id(1)

        @pl.when(k == 0)
        def _():
            acc[...] = jnp.zeros_like(acc)

        @pl.when((i == 0) & (k == 0))
        def _():
            dg_ref[...] = jnp.zeros_like(dg_ref)

        acc[...] += _nt(dz_ref[...], w_ref[...])

        @pl.when(k == nk - 1)
        def _():
            dx, dgr = _rms_bwd(acc[...], h_ref[...], g_ref[...])
            o_ref[...] = dh_ref[...] + dx
            dg_ref[...] += jnp.sum(dgr, axis=0, keepdims=True)

    return _pc(body, name, (t // bm, nk),
               [pl.BlockSpec((bm, bk), lambda i, k: (i, k)), pl.BlockSpec((D, bk), lambda i, k: (0, k)),
                pl.BlockSpec((bm, D), lambda i, k: (i, 0)), pl.BlockSpec((1, D), lambda i, k: (0, 0)),
                pl.BlockSpec((bm, D), lambda i, k: (i, 0))],
               [pl.BlockSpec((bm, D), lambda i, k: (i, 0)), pl.BlockSpec((1, D), lambda i, k: (0, 0))],
               [_sds((t, D)), _sds((1, D))], scratch=[pltpu.VMEM((bm, D), F32)])(dz, w, h, g, dh)


def _mlp_fwd(h, g, w1s, w2, l, name, bm=1024):
    t = h.shape[0]
    bm = min(bm, t)
    nk = w1s.shape[0]

    def body(h_ref, g_ref, w1_ref, w2_ref, o_ref, a_ref, u_ref, acc):
        k = pl.program_id(1)

        @pl.when(k == 0)
        def _():
            x = h_ref[...]
            u_ref[...] = (x * _rstd(x) * g_ref[...]).astype(BF16)
            acc[...] = jnp.zeros_like(acc)

        a = _nn(u_ref[...], w1_ref[...])
        a_ref[...] = a
        r = jnp.square(jnp.maximum(a, 0.0)).astype(BF16)
        acc[...] += _nn(r, w2_ref[...])

        @pl.when(k == nk - 1)
        def _():
            o_ref[...] = h_ref[...] + acc[...]

    return _pc(body, name, (t // bm, nk),
               [pl.BlockSpec((bm, D), lambda i, k: (i, 0)), pl.BlockSpec((1, D), lambda i, k: (0, 0)),
                pl.BlockSpec((None, None, D, D), lambda i, k: (k, l, 0, 0)), pl.BlockSpec((None, None, D, D), lambda i, k: (k, l, 0, 0))],
               [pl.BlockSpec((bm, D), lambda i, k: (i, 0)), pl.BlockSpec((bm, D), lambda i, k: (i, k)),
                pl.BlockSpec((bm, D), lambda i, k: (i, 0))],
               [_sds((t, D)), _sds((t, nk * D)), _sds((t, D), BF16)],
               scratch=[pltpu.VMEM((bm, D), F32)])(h, g, w1s, w2)


def _mlp_bwd(dh, a, w1s, w2, l, h, g, name, bm=512):
    t = h.shape[0]
    bm = min(bm, t)
    nk = w1s.shape[0]

    def body(dh_ref, a_ref, w1_ref, w2_ref, h_ref, g_ref, o_ref, da_ref, r_ref, dg_ref, acc):
        i, k = pl.program_id(0), pl.program_id(1)

        @pl.when(k == 0)
        def _():
            acc[...] = jnp.zeros_like(acc)

        @pl.when((i == 0) & (k == 0))
        def _():
            dg_ref[...] = jnp.zeros_like(dg_ref)

        ap = jnp.maximum(a_ref[...], 0.0)
        r_ref[...] = jnp.square(ap).astype(BF16)
        dr = _nt(dh_ref[...].astype(BF16), w2_ref[...])
        da = (dr * (2.0 * ap)).astype(BF16)
        da_ref[...] = da
        acc[...] += _nt(da, w1_ref[...])

        @pl.when(k == nk - 1)
        def _():
            dx, dgr = _rms_bwd(acc[...], h_ref[...], g_ref[...])
            o_ref[...] = dh_ref[...] + dx
            dg_ref[...] += jnp.sum(dgr, axis=0, keepdims=True)

    return _pc(body, name, (t // bm, nk),
               [pl.BlockSpec((bm, D), lambda i, k: (i, 0)), pl.BlockSpec((bm, D), lambda i, k: (i, k)),
                pl.BlockSpec((None, None, D, D), lambda i, k: (k, l, 0, 0)), pl.BlockSpec((None, None, D, D), lambda i, k: (k, l, 0, 0)),
                pl.BlockSpec((bm, D), lambda i, k: (i, 0)), pl.BlockSpec((1, D), lambda i, k: (0, 0))],
               [pl.BlockSpec((bm, D), lambda i, k: (i, 0)), pl.BlockSpec((bm, D), lambda i, k: (i, k)),
                pl.BlockSpec((bm, D), lambda i, k: (i, k)), pl.BlockSpec((1, D), lambda i, k: (0, 0))],
               [_sds((t, D)), _sds((t, nk * D), BF16), _sds((t, nk * D), BF16), _sds((1, D))],
               scratch=[pltpu.VMEM((bm, D), F32)])(dh, a, w1s, w2, h, g)


def _rows_of(x):
    return lax.broadcasted_iota(jnp.int32, x.shape, 0)


def _shift_down(x, s):
    if s == 0:
        return x
    return jnp.where(_rows_of(x) >= s, pltpu.roll(x, s, 0), 0.0)


def _shift_up(x, s):
    if s == 0:
        return x
    n = x.shape[0]
    return jnp.where(_rows_of(x) < n - s, pltpu.roll(x, n - s, 0), 0.0)


def _cumsum_rows(x):
    n, s = x.shape[0], 1
    while s < n:
        x = x + _shift_down(x, s)
        s *= 2
    return x


def _rcumsum_rows(x):
    n, s = x.shape[0], 1
    while s < n:
        x = x + _shift_up(x, s)
        s *= 2
    return x


def _silu(x):
    return x * _sigmoid(x)


def _dsilu(x):
    s = _sigmoid(x)
    return s * (1.0 + x * (1.0 - s))


CONV_W = 4


def _conv_pre(u, w):
    y = _shift_down(u, CONV_W - 1) * w[0:1, :]
    for j in range(1, CONV_W):
        y = y + _shift_down(u, CONV_W - 1 - j) * w[j:j + 1, :]
    return y


def _conv_fwd(z0, cw, name):
    t = z0.shape[0]

    def body(u_ref, w_ref, o_ref):
        o_ref[...] = _silu(_conv_pre(u_ref[...], w_ref[...]))

    return _pc(body, name, (2 * 512 // HD,),
               [pl.BlockSpec((t, HD), lambda c: (0, c)), pl.BlockSpec((CONV_W, HD), lambda c: (0, c))],
               pl.BlockSpec((t, HD), lambda c: (0, c)), _sds((t, 1024)))(z0, cw)


def _conv_bwd(z0, cw, dy, name):
    t = z0.shape[0]

    def body(u_ref, w_ref, dy_ref, du_ref, dw_ref):
        u, w = u_ref[...], w_ref[...]
        dpre = dy_ref[...] * _dsilu(_conv_pre(u, w))
        du = _shift_up(dpre, CONV_W - 1) * w[0:1, :]
        for j in range(1, CONV_W):
            du = du + _shift_up(dpre, CONV_W - 1 - j) * w[j:j + 1, :]
        du_ref[...] = du.astype(BF16)
        for j in range(CONV_W):
            dw_ref[j:j + 1, :] = jnp.sum(dpre * _shift_down(u, CONV_W - 1 - j), axis=0, keepdims=True)

    return _pc(body, name, (2 * 512 // HD,),
               [pl.BlockSpec((t, HD), lambda c: (0, c)), pl.BlockSpec((CONV_W, HD), lambda c: (0, c)),
                pl.BlockSpec((t, HD), lambda c: (0, c))],
               [pl.BlockSpec((t, HD), lambda c: (0, c)), pl.BlockSpec((CONV_W, HD), lambda c: (0, c))],
               [_sds((t, 1024), BF16), _sds((CONV_W, 1024))])(z0, cw, dy)


def _mlstm_gates(gate, bias, m_in):
    L = gate.shape[0]
    r = lax.broadcasted_iota(jnp.int32, (L, L), 0)
    c = lax.broadcasted_iota(jnp.int32, (L, L), 1)
    eye, tril = r == c, c <= r
    i_col = gate[:, 0:1] + bias[:, 0:1]
    f_col = gate[:, 1:2] + bias[:, 1:2]
    logf_col = _log_sigmoid(f_col)
    logf_row = jnp.sum(jnp.where(eye, logf_col, 0.0), axis=0, keepdims=True)
    i_row = jnp.sum(jnp.where(eye, i_col, 0.0), axis=0, keepdims=True)
    b_col = jnp.sum(jnp.where(tril, logf_row, 0.0), axis=1, keepdims=True)
    b_row = jnp.sum(jnp.where(r <= c, logf_col, 0.0), axis=0, keepdims=True)
    logd = jnp.where(tril, b_col - b_row + i_row, NEG)
    inter = b_col + m_in
    m_t = jnp.maximum(inter, jnp.max(logd, axis=1, keepdims=True))
    w_t = jnp.exp(inter - m_t)
    dm = jnp.exp(logd - m_t)
    b_last = b_col[L - 1:L, :]
    log_in = b_last - b_col + i_col
    m_new = jnp.maximum(b_last + m_in, jnp.max(log_in, axis=0, keepdims=True))
    w_col = jnp.exp(log_in - m_new)
    decay = jnp.exp(b_last + m_in - m_new)
    return dict(eye=eye, r=r, c=c, f_col=f_col, m_t=m_t, w_t=w_t, dm=dm, m_new=m_new, w_col=w_col, decay=decay)


def _mlstm_fwd(qk, z0, gates, bias, name):
    t = qk.shape[0]
    nc, nh, L = t // CHUNK, 4, CHUNK
    scale = HD ** -0.5

    def body(q_ref, k_ref, v_ref, g_ref, b_ref, h_ref, cs_ref, ns_ref, ms_ref, c_s, n_s, m_s):
        @pl.when(pl.program_id(0) == 0)
        def _():
            c_s[...] = jnp.zeros_like(c_s)
            n_s[...] = jnp.zeros_like(n_s)
            m_s[...] = jnp.zeros_like(m_s)

        for hd in range(nh):
            sl = slice(hd * HD, (hd + 1) * HD)
            cm, nv, m_in = c_s[hd], n_s[hd], m_s[hd]
            cs_ref[hd] = cm
            ns_ref[hd] = nv
            ms_ref[hd] = jnp.broadcast_to(m_in, (1, HD))
            q, kh, v = q_ref[:, sl], k_ref[:, sl] * scale, v_ref[:, sl]
            G = _mlstm_gates(g_ref[hd], b_ref[hd], m_in)
            qb, kb, vb = q.astype(BF16), kh.astype(BF16), v.astype(BF16)
            sc = _nt(qb, kb) * G["dm"]
            num = _nn(sc.astype(BF16), vb) + G["w_t"] * _nn(qb, cm.astype(BF16))
            den = jnp.sum(sc, axis=1, keepdims=True) + G["w_t"] * jnp.sum(q * nv, axis=1, keepdims=True)
            h_ref[:, sl] = num / jnp.maximum(jnp.abs(den), jnp.exp(-G["m_t"]))
            wk = G["w_col"] * kh
            c_s[hd] = G["decay"] * cm + _tn(wk.astype(BF16), vb)
            n_s[hd] = G["decay"] * nv + jnp.sum(wk, axis=0, keepdims=True)
            m_s[hd] = G["m_new"]

    hspec = lambda blk: pl.BlockSpec((L, 512), lambda j: (j, blk))
    st = lambda r: pl.BlockSpec((nh, None, r, HD), lambda j: (0, j, 0, 0))
    return _pc(body, name, (nc,),
               [hspec(0), hspec(1), hspec(2), pl.BlockSpec((nh, L, 2), lambda j: (0, j, 0)),
                pl.BlockSpec((nh, 1, 2), lambda j: (0, 0, 0))],
               [hspec(0), st(HD), st(1), st(1)],
               [_sds((t, 512)), _sds((nh, nc, HD, HD)), _sds((nh, nc, 1, HD)), _sds((nh, nc, 1, HD))],
               scratch=[pltpu.VMEM((nh, HD, HD), F32), pltpu.VMEM((nh, 1, HD), F32), pltpu.VMEM((nh, 1, 1), F32)])(qk, qk, z0, gates, bias)


def _mlstm_bwd(qk, z0, gates, bias, cs, ns, ms, dh, name):
    t = qk.shape[0]
    nc, nh, L = t // CHUNK, 4, CHUNK
    scale = HD ** -0.5

    def body(q_ref, k_ref, v_ref, g_ref, b_ref, cs_ref, ns_ref, ms_ref, dh_ref, dq_ref, dk_ref, dv_ref, dg_ref, dc_s, dn_s):
        @pl.when(pl.program_id(0) == 0)
        def _():
            dc_s[...] = jnp.zeros_like(dc_s)
            dn_s[...] = jnp.zeros_like(dn_s)

        for hd in range(nh):
            one_head(hd, slice(hd * HD, (hd + 1) * HD), q_ref, k_ref, v_ref, g_ref, b_ref, cs_ref, ns_ref, ms_ref, dh_ref,
                     dq_ref, dk_ref, dv_ref, dg_ref, dc_s, dn_s)

    def one_head(hd, sl, q_ref, k_ref, v_ref, g_ref, b_ref, cs_ref, ns_ref, ms_ref, dh_ref, dq_ref, dk_ref, dv_ref, dg_ref, dc_s, dn_s):
        cm, nv, m_in = cs_ref[hd], ns_ref[hd], ms_ref[hd][:, 0:1]
        q, kh, v = q_ref[:, sl], k_ref[:, sl] * scale, v_ref[:, sl]
        G = _mlstm_gates(g_ref[hd], b_ref[hd], m_in)
        w_t, dmat, w_col, decay = G["w_t"], G["dm"], G["w_col"], G["decay"]
        qb, kb, vb, cb = q.astype(BF16), kh.astype(BF16), v.astype(BF16), cm.astype(BF16)
        s = _nt(qb, kb)
        sc = s * dmat
        scb = sc.astype(BF16)
        qc = _nn(qb, cb)
        qn = jnp.sum(q * nv, axis=1, keepdims=True)
        num = _nn(scb, vb) + w_t * qc
        den = jnp.sum(sc, axis=1, keepdims=True) + w_t * qn
        e_m = jnp.exp(-G["m_t"])
        dnm = jnp.maximum(jnp.abs(den), e_m)
        dh_ = dh_ref[:, sl]
        dnum = dh_ / dnm
        dden = jnp.where(jnp.abs(den) > e_m, -jnp.sum(dh_ * num, axis=1, keepdims=True) / (dnm * dnm) * jnp.sign(den), 0.0)
        dnumb = dnum.astype(BF16)
        dsc = _nt(dnumb, vb) + dden
        dv = _tn(scb, dnumb)
        wd = w_t * dnum
        wdb = wd.astype(BF16)
        ds = dsc * dmat
        dsb = ds.astype(BF16)
        dq = _nt(wdb, cb) + (w_t * dden) * nv + _nn(dsb, kb)
        dc_o = _tn(qb, wdb)
        dn_o = jnp.sum(q * (w_t * dden), axis=0, keepdims=True)
        dw = jnp.sum(dnum * qc, axis=1, keepdims=True) + dden * qn
        dkh = _tn(dsb, qb)
        dlogd = ds * s
        db_col = jnp.sum(dlogd, axis=1, keepdims=True) + dw * w_t
        csum = jnp.sum(dlogd, axis=0, keepdims=True)
        dcn, dnn = dc_s[hd], dn_s[hd]
        dcnb = dcn.astype(BF16)
        kdc = _nn(kb, dcnb)
        dws = jnp.sum(kdc * v, axis=1, keepdims=True) + jnp.sum(kh * dnn, axis=1, keepdims=True)
        dv = dv + w_col * kdc
        dkh = dkh + w_col * (_nt(vb, dcnb) + dnn)
        dlin = dws * w_col
        ddecay = jnp.sum(jnp.sum(dcn * cm, axis=1, keepdims=True), axis=0, keepdims=True) + jnp.sum(dnn * nv, axis=1, keepdims=True)
        dlast = ddecay * decay + jnp.sum(dlin, axis=0, keepdims=True)
        rows = lax.broadcasted_iota(jnp.int32, (L, 1), 0)
        db_col = db_col - dlin + jnp.where(rows == L - 1, dlast, 0.0)
        eye, r, c = G["eye"], G["r"], G["c"]
        di = dlin + jnp.sum(jnp.where(eye, csum, 0.0), axis=1, keepdims=True)
        db_row = jnp.sum(jnp.where(eye, db_col, 0.0), axis=0, keepdims=True) - csum
        dlogf = jnp.sum(jnp.where(c >= r, db_row, 0.0), axis=1, keepdims=True)
        dg_ref[hd, :, 0:1] = di
        dg_ref[hd, :, 1:2] = dlogf * (1.0 - _sigmoid(G["f_col"]))
        dq_ref[:, sl] = dq
        dk_ref[:, sl] = dkh * scale
        dv_ref[:, sl] = dv
        dc_s[hd] = decay * dcn + dc_o
        dn_s[hd] = decay * dnn + dn_o

    rv = lambda j: nc - 1 - j
    hspec = lambda blk: pl.BlockSpec((L, 512), lambda j: (rv(j), blk))
    st = lambda r: pl.BlockSpec((nh, None, r, HD), lambda j: (0, rv(j), 0, 0))
    gs = pl.BlockSpec((nh, L, 2), lambda j: (0, rv(j), 0))
    return _pc(body, name, (nc,),
               [hspec(0), hspec(1), hspec(2), gs, pl.BlockSpec((nh, 1, 2), lambda j: (0, 0, 0)),
                st(HD), st(1), st(1), hspec(0)],
               [hspec(0), hspec(0), hspec(0), gs],
               [_sds((t, 512)), _sds((t, 512)), _sds((t, 512)), _sds((nh, t, 2))],
               scratch=[pltpu.VMEM((nh, HD, HD), F32), pltpu.VMEM((nh, 1, HD), F32)])(qk, qk, z0, gates, bias, cs, ns, ms, dh)


def _hgrn_act(qb_, fb_, ib_, lg):
    lb = _sigmoid(lg[0:1, :] - lg[1:2, :])
    sg = _sigmoid(fb_)
    f = lb + (1.0 - lb) * sg
    return lb, sg, f, _silu(qb_), (1.0 - lb) * (1.0 - sg), _silu(ib_), _cumsum_rows(jnp.log(f))


HG_SUB = 16


def _hgrn_offdiag(q, k, b, r0):
    beta = b[r0 - 1:r0, :]
    e1 = jnp.exp(b[r0:r0 + HG_SUB, :] - beta)
    e2 = jnp.where(_rows_of(b) < r0, jnp.exp(jnp.minimum(beta - b, 0.0)), 0.0)
    return q[r0:r0 + HG_SUB, :] * e1, k * e2, e1, e2


def _hgrn_fwd(z0, lbl, name):
    t = z0.shape[0]
    nc, nh, L = t // CHUNK, 4, CHUNK

    def body(q_ref, f_ref, i_ref, l_ref, o_ref, ss_ref, st_s):
        @pl.when(pl.program_id(0) == 0)
        def _():
            st_s[...] = jnp.zeros_like(st_s)

        for hd in range(nh):
            sl = slice(hd * HD, (hd + 1) * HD)
            st = st_s[hd]
            ss_ref[hd] = st
            _, _, _, q, k, v, b = _hgrn_act(q_ref[:, sl], f_ref[:, sl], i_ref[:, sl], l_ref[:, sl])
            o = _nt((q * jnp.exp(b)).astype(BF16), st.astype(BF16))
            sub = _rows_of(b) & (HG_SUB - 1)
            o = o + jnp.sum(q * k, axis=1, keepdims=True) * v
            for dl in range(1, HG_SUB):
                e = jnp.exp(jnp.where(sub >= dl, b - pltpu.roll(b, dl, 0), NEG))
                a = jnp.sum(q * pltpu.roll(k, dl, 0) * e, axis=1, keepdims=True)
                o = o + a * pltpu.roll(v, dl, 0)
            o_ref[:, sl] = o
            vb = v.astype(BF16)
            for i in range(1, L // HG_SUB):
                r0 = i * HG_SUB
                qt, kt, _, _ = _hgrn_offdiag(q, k, b, r0)
                a = _nt(qt.astype(BF16), kt.astype(BF16))
                o_ref[r0:r0 + HG_SUB, sl] += _nn(a.astype(BF16), vb)
            bl = b[L - 1:L, :]
            st_s[hd] = st * jnp.exp(bl) + _tn(v.astype(BF16), (k * jnp.exp(bl - b)).astype(BF16))

    hspec = lambda blk: pl.BlockSpec((L, 512), lambda j: (j, blk))
    return _pc(body, name, (nc,),
               [hspec(4), hspec(5), hspec(6), pl.BlockSpec((2, 512), lambda j: (0, 0))],
               [hspec(0), pl.BlockSpec((nh, None, HD, HD), lambda j: (0, j, 0, 0))],
               [_sds((t, 512)), _sds((nh, nc, HD, HD))],
               scratch=[pltpu.VMEM((nh, HD, HD), F32)])(z0, z0, z0, lbl)


def _hgrn_bwd(z0, lbl, ss, do, name):
    t = z0.shape[0]
    nc, nh, L = t // CHUNK, 4, CHUNK

    def body(q_ref, f_ref, i_ref, l_ref, ss_ref, do_ref, dq_ref, df_ref, di_ref, dl_ref, dst_s, dlb_s, dq_a, dk_a, dv_a, db_a):
        @pl.when(pl.program_id(0) == 0)
        def _():
            dst_s[...] = jnp.zeros_like(dst_s)
            dlb_s[...] = jnp.zeros_like(dlb_s)

        for hd in range(nh):
            one_head(hd, slice(hd * HD, (hd + 1) * HD), q_ref, f_ref, i_ref, l_ref, ss_ref, do_ref, dq_ref, df_ref, di_ref, dl_ref,
                     dst_s, dlb_s, dq_a.at[hd], dk_a.at[hd], dv_a.at[hd], db_a.at[hd])

    def one_head(hd, sl, q_ref, f_ref, i_ref, l_ref, ss_ref, do_ref, dq_ref, df_ref, di_ref, dl_ref, dst_s, dlb_s, dq_a, dk_a, dv_a, db_a):
        st = ss_ref[hd]
        qp, fp, ip = q_ref[:, sl], f_ref[:, sl], i_ref[:, sl]
        lb, sg, f, q, k, v, b = _hgrn_act(qp, fp, ip, l_ref[:, sl])
        do_ = do_ref[:, sl]
        dob, stb = do_.astype(BF16), st.astype(BF16)
        eb = jnp.exp(b)
        qe = q * eb
        dqe = _nn(dob, stb)
        dst_o = _tn(dob, qe.astype(BF16))
        dq = dqe * eb
        db = dqe * qe
        rows = _rows_of(b)
        sub = rows & (HG_SUB - 1)
        p0 = jnp.sum(do_ * v, axis=1, keepdims=True)
        dq = dq + p0 * k
        dk = p0 * q
        dv = jnp.sum(q * k, axis=1, keepdims=True) * do_
        for dl in range(1, HG_SUB):
            up = L - dl
            kd, vd = pltpu.roll(k, dl, 0), pltpu.roll(v, dl, 0)
            e = jnp.exp(jnp.where(sub >= dl, b - pltpu.roll(b, dl, 0), NEG))
            a = jnp.sum(q * kd * e, axis=1, keepdims=True)
            p = jnp.sum(do_ * vd, axis=1, keepdims=True) * e
            dq = dq + p * kd
            dkd = p * q
            dbb = dkd * kd
            dv = dv + pltpu.roll(a * do_, up, 0)
            dk = dk + pltpu.roll(dkd, up, 0)
            db = db + dbb - pltpu.roll(dbb, up, 0)
        dq_a[...], dk_a[...], dv_a[...], db_a[...] = dq, dk, dv, db
        vb = v.astype(BF16)
        for i in range(1, L // HG_SUB):
            r0 = i * HG_SUB
            blk = slice(r0, r0 + HG_SUB)
            qt, kt, e1, e2 = _hgrn_offdiag(q, k, b, r0)
            qtb, ktb, dob_i = qt.astype(BF16), kt.astype(BF16), dob[blk, :]
            a = _nt(qtb, ktb).astype(BF16)
            da = _nt(dob_i, vb).astype(BF16)
            dv_a[...] += _tn(a, dob_i)
            dqt = _nn(da, ktb)
            dkt = _tn(da, qtb)
            dq_a[blk, :] += dqt * e1
            t1, t2 = dqt * qt, dkt * kt
            db_a[blk, :] += t1
            dk_a[...] += dkt * e2
            db_a[...] -= t2
            db_a[r0 - 1:r0, :] += jnp.sum(t2, axis=0, keepdims=True) - jnp.sum(t1, axis=0, keepdims=True)
        dq, dk, dv, db = dq_a[...], dk_a[...], dv_a[...], db_a[...]
        dstn = dst_s[hd]
        dstnb = dstn.astype(BF16)
        bl = b[L - 1:L, :]
        ebl = jnp.exp(bl)
        kdec_e = jnp.exp(bl - b)
        kdec = k * kdec_e
        dbl = jnp.sum(dstn * st, axis=0, keepdims=True) * ebl
        dv = dv + _nt(kdec.astype(BF16), dstnb)
        dkdec = _nn(v.astype(BF16), dstnb)
        dk = dk + dkdec * kdec_e
        dx = dkdec * kdec
        dbl = dbl + jnp.sum(dx, axis=0, keepdims=True)
        db = db - dx + jnp.where(rows == L - 1, dbl, 0.0)
        dst_s[hd] = dstn * ebl + dst_o
        dg = _rcumsum_rows(db)
        dfk = dg / f - dk
        dq_ref[:, sl] = (dq * _dsilu(qp)).astype(BF16)
        di_ref[:, sl] = (dv * _dsilu(ip)).astype(BF16)
        df_ref[:, sl] = (dfk * (1.0 - lb) * sg * (1.0 - sg)).astype(BF16)
        dlb_s[hd] += jnp.sum(dfk * (1.0 - sg), axis=0, keepdims=True)

        @pl.when(pl.program_id(0) == nc - 1)
        def _():
            dl0 = dlb_s[hd] * lb * (1.0 - lb)
            dl_ref[0:1, sl] = dl0
            dl_ref[1:2, sl] = -dl0

    rv = lambda j: nc - 1 - j
    hspec = lambda blk: pl.BlockSpec((L, 512), lambda j: (rv(j), blk))
    return _pc(body, name, (nc,),
               [hspec(4), hspec(5), hspec(6), pl.BlockSpec((2, 512), lambda j: (0, 0)),
                pl.BlockSpec((nh, None, HD, HD), lambda j: (0, rv(j), 0, 0)), hspec(0)],
               [hspec(0), hspec(0), hspec(0), pl.BlockSpec((2, 512), lambda j: (0, 0))],
               [_sds((t, 512), BF16), _sds((t, 512), BF16), _sds((t, 512), BF16), _sds((2, 512))],
               scratch=[pltpu.VMEM((nh, HD, HD), F32), pltpu.VMEM((nh, 1, HD), F32)] + [pltpu.VMEM((nh, L, HD), F32)] * 4)(z0, z0, z0, lbl, ss, do)


def _post0_fwd(hm, hh, z0, na, nb, w, h0, name, bm=512):
    t = h0.shape[0]
    bm = min(bm, t)

    def body(hm_ref, hh_ref, oa_ref, gb_ref, na_ref, nb_ref, w_ref, h_ref, o_ref, y_ref):
        for hd in range(4):
            sl = slice(hd * HD, (hd + 1) * HD)
            pa = _sigmoid(oa_ref[:, sl]) * hm_ref[:, sl]
            y_ref[:, sl] = (pa * _rstd(pa) * na_ref[:, sl]).astype(BF16)
            xb = hh_ref[:, sl]
            y_ref[:, 512 + hd * HD:512 + (hd + 1) * HD] = (xb * _rstd(xb) * nb_ref[:, sl] * _silu(gb_ref[:, sl])).astype(BF16)
        o_ref[...] = h_ref[...] + _nn(y_ref[...], w_ref[...])

    row = lambda wd, c: pl.BlockSpec((bm, wd), lambda i: (i, c))
    vec = lambda wd: pl.BlockSpec((1, wd), lambda i: (0, 0))
    return _pc(body, name, (t // bm,),
               [row(512, 0), row(512, 0), row(512, 3), row(512, 7), vec(512), vec(512),
                pl.BlockSpec((D, D), lambda i: (0, 0)), row(D, 0)],
               [row(D, 0), row(D, 0)], [_sds((t, D)), _sds((t, D), BF16)])(hm, hh, z0, z0, na, nb, w, h0)


def _post0_bwd(dh1, w, hm, hh, z0, na, nb, name, bm=512):
    t = dh1.shape[0]
    bm = min(bm, t)

    def body(dh_ref, w_ref, hm_ref, hh_ref, oa_ref, gb_ref, na_ref, nb_ref, dhm_ref, dhh_ref, doa_ref, dgb_ref, dna_ref, dnb_ref):
        @pl.when(pl.program_id(0) == 0)
        def _():
            dna_ref[...] = jnp.zeros_like(dna_ref)
            dnb_ref[...] = jnp.zeros_like(dnb_ref)

        dy = _nt(dh_ref[...].astype(BF16), w_ref[...])
        for hd in range(4):
            sl = slice(hd * HD, (hd + 1) * HD)
            hm_, oa = hm_ref[:, sl], oa_ref[:, sl]
            sg = _sigmoid(oa)
            dpa, dgr = _rms_bwd(dy[:, sl], sg * hm_, na_ref[:, sl])
            dna_ref[:, sl] += jnp.sum(dgr, axis=0, keepdims=True)
            doa_ref[:, sl] = (dpa * hm_ * sg * (1.0 - sg)).astype(BF16)
            dhm_ref[:, sl] = dpa * sg
            xb, gb, nbv = hh_ref[:, sl], gb_ref[:, sl], nb_ref[:, sl]
            dyb = dy[:, 512 + hd * HD:512 + (hd + 1) * HD]
            dgb_ref[:, sl] = (dyb * (xb * _rstd(xb) * nbv) * _dsilu(gb)).astype(BF16)
            dxb, dgr2 = _rms_bwd(dyb * _silu(gb), xb, nbv)
            dnb_ref[:, sl] += jnp.sum(dgr2, axis=0, keepdims=True)
            dhh_ref[:, sl] = dxb

    row = lambda wd, c: pl.BlockSpec((bm, wd), lambda i: (i, c))
    vec = lambda wd: pl.BlockSpec((1, wd), lambda i: (0, 0))
    return _pc(body, name, (t // bm,),
               [row(D, 0), pl.BlockSpec((D, D), lambda i: (0, 0)), row(512, 0), row(512, 0), row(512, 3), row(512, 7),
                vec(512), vec(512)],
               [row(512, 0), row(512, 0), row(512, 0), row(512, 0), vec(512), vec(512)],
               [_sds((t, 512)), _sds((t, 512)), _sds((t, 512), BF16), _sds((t, 512), BF16), _sds((1, 512)), _sds((1, 512))],
               )(dh1, w, hm, hh, z0, z0, na, nb)


def _memkv_fwd(mem, g, wkv_s, name):
    m = mem.shape[0]

    def body(x_ref, g_ref, w_ref, kv_ref, mn_ref):
        x = x_ref[...]
        mn = (x * _rstd(x) * g_ref[...]).astype(BF16)
        mn_ref[...] = mn
        kv_ref[...] = _nn(mn, w_ref[...])

    return _pc(body, name, (4,),
               [pl.BlockSpec((m, D), lambda k: (0, 0)), pl.BlockSpec((1, D), lambda k: (0, 0)),
                pl.BlockSpec((None, D, 512), lambda k: (k, 0, 0))],
               [pl.BlockSpec((m, 512), lambda k: (0, k)), pl.BlockSpec((m, D), lambda k: (0, 0))],
               [_sds((m, 2048)), _sds((m, D), BF16)])(mem, g, wkv_s)


def _memkv_bwd(dkv, wkv_s, mem, g, name):
    m = mem.shape[0]

    def body(d_ref, w_ref, x_ref, g_ref, dg_ref, acc):
        k = pl.program_id(0)

        @pl.when(k == 0)
        def _():
            acc[...] = jnp.zeros_like(acc)

        acc[...] += _nt(d_ref[...].astype(BF16), w_ref[...])

        @pl.when(k == 3)
        def _():
            _, dgr = _rms_bwd(acc[...], x_ref[...], g_ref[...])
            dg_ref[...] = jnp.sum(dgr, axis=0, keepdims=True)

    return _pc(body, name, (4,),
               [pl.BlockSpec((m, 512), lambda k: (0, k)), pl.BlockSpec((None, D, 512), lambda k: (k, 0, 0)),
                pl.BlockSpec((m, D), lambda k: (0, 0)), pl.BlockSpec((1, D), lambda k: (0, 0))],
               pl.BlockSpec((1, D), lambda k: (0, 0)), _sds((1, D)), scratch=[pltpu.VMEM((m, D), F32)])(dkv, wkv_s, mem, g)


def _xattn_probs(qh, kh):
    s = _nt(qh, kh) * (XD ** -0.5)
    p = jnp.exp(s - jnp.max(s, axis=1, keepdims=True))
    return p / jnp.sum(p, axis=1, keepdims=True)


def _xattn_fwd(q, kv, wo, h1, name, bm=512):
    t, m = q.shape[0], kv.shape[0]
    bm = min(bm, t)

    def body(q_ref, k_ref, v_ref, w_ref, h_ref, out_ref, o_ref):
        for hd in range(D // XD):
            sl = slice(hd * XD, (hd + 1) * XD)
            p = _xattn_probs(q_ref[:, sl].astype(BF16), k_ref[:, sl].astype(BF16))
            o_ref[:, sl] = _nn(p.astype(BF16), v_ref[:, sl].astype(BF16)).astype(BF16)
        out_ref[...] = h_ref[...] + _nn(o_ref[...], w_ref[...])

    row = pl.BlockSpec((bm, D), lambda i: (i, 0))
    return _pc(body, name, (t // bm,),
               [row, pl.BlockSpec((m, D), lambda i: (0, 0)), pl.BlockSpec((m, D), lambda i: (0, 1)),
                pl.BlockSpec((D, D), lambda i: (0, 0)), row],
               [row, row], [_sds((t, D)), _sds((t, D), BF16)])(q, kv, kv, wo, h1)


def _xattn_bwd(dh2, q, kv, wo, name, bm=512):
    t, m = q.shape[0], kv.shape[0]
    bm = min(bm, t)

    def body(dh_ref, q_ref, k_ref, v_ref, w_ref, dq_ref, dkv_ref):
        @pl.when(pl.program_id(0) == 0)
        def _():
            dkv_ref[...] = jnp.zeros_like(dkv_ref)

        d_o = _nt(dh_ref[...].astype(BF16), w_ref[...])
        for hd in range(D // XD):
            sl = slice(hd * XD, (hd + 1) * XD)
            qh, kh, vh = q_ref[:, sl].astype(BF16), k_ref[:, sl].astype(BF16), v_ref[:, sl].astype(BF16)
            p = _xattn_probs(qh, kh)
            dob = d_o[:, sl].astype(BF16)
            dp = _nt(dob, vh)
            dkv_ref[:, D + hd * XD:D + (hd + 1) * XD] += _tn(p.astype(BF16), dob)
            ds = (p * (dp - jnp.sum(dp * p, axis=1, keepdims=True)) * (XD ** -0.5)).astype(BF16)
            dq_ref[:, sl] = _nn(ds, kh).astype(BF16)
            dkv_ref[:, sl] += _tn(ds, qh)

    row = pl.BlockSpec((bm, D), lambda i: (i, 0))
    return _pc(body, name, (t // bm,),
               [row, row, pl.BlockSpec((m, D), lambda i: (0, 0)), pl.BlockSpec((m, D), lambda i: (0, 1)),
                pl.BlockSpec((D, D), lambda i: (0, 0))],
               [row, pl.BlockSpec((m, 2 * D), lambda i: (0, 0))],
               [_sds((t, D), BF16), _sds((m, 2 * D))])(dh2, q, kv, kv, wo)


NH1 = 8
FOX_BM = 512
FOX_BQ = 512
FOX_BK = 512
FOX_HEADS_PER_STEP = 2
FOX_SUB = 512


def _foxprep_fwd(z1, qg, kg, fbp, name):
    t = z1.shape[0]
    bm = min(FOX_BM, t)

    def body(q_ref, k_ref, v_ref, f_ref, qg_ref, kg_ref, fb_ref, qn_ref, kn_ref, vb_ref, c_ref, carry):
        @pl.when(pl.program_id(0) == 0)
        def _():
            carry[...] = jnp.zeros_like(carry)

        for hd in range(NH1):
            sl = slice(hd * HD, (hd + 1) * HD)
            x = q_ref[:, sl]
            qn_ref[:, sl] = (x * _rstd(x) * qg_ref[...] * FOX_QSCALE).astype(BF16)
            x = k_ref[:, sl]
            kn_ref[:, sl] = (x * _rstd(x) * kg_ref[...]).astype(BF16)
        vb_ref[...] = v_ref[...].astype(BF16)
        c = carry[...] + _cumsum_rows(_log_sigmoid(f_ref[...] + fb_ref[...]))
        c_ref[...] = c
        carry[...] = c[bm - 1:bm, :]

    row = lambda c: pl.BlockSpec((bm, D), lambda i: (i, c))
    lane = pl.BlockSpec((bm, HD), lambda i: (i, 4 * D // HD))
    vec = pl.BlockSpec((1, HD), lambda i: (0, 0))
    return _pc(body, name, (t // bm,), [row(0), row(1), row(2), lane, vec, vec, vec],
               [row(0), row(0), row(0), pl.BlockSpec((bm, HD), lambda i: (i, 0))],
               [_sds((t, D), BF16), _sds((t, D), BF16), _sds((t, D), BF16), _sds((t, HD))],
               scratch=[pltpu.VMEM((1, HD), F32)])(z1, z1, z1, z1, qg, kg, fbp)


def _foxprep_bwd(dqn, dkn, z1, qg, kg, fbp, dc, name):
    t = z1.shape[0]
    bm = min(FOX_BM, t)
    nb = t // bm

    def body(dqn_ref, dkn_ref, q_ref, k_ref, f_ref, qg_ref, kg_ref, fb_ref, dc_ref,
             dq_ref, dk_ref, df_ref, dqg_ref, dkg_ref, dfb_ref, carry):
        @pl.when(pl.program_id(0) == 0)
        def _():
            carry[...] = jnp.zeros_like(carry)
            dqg_ref[...] = jnp.zeros_like(dqg_ref)
            dkg_ref[...] = jnp.zeros_like(dkg_ref)
            dfb_ref[...] = jnp.zeros_like(dfb_ref)

        for hd in range(NH1):
            sl = slice(hd * HD, (hd + 1) * HD)
            dx, dgr = _rms_bwd(dqn_ref[:, sl] * (HD ** -0.5), q_ref[:, sl], qg_ref[...])
            dq_ref[:, sl] = dx.astype(BF16)
            dqg_ref[...] += jnp.sum(dgr, axis=0, keepdims=True)
            dx, dgr = _rms_bwd(dkn_ref[:, sl], k_ref[:, sl], kg_ref[...])
            dk_ref[:, sl] = dx.astype(BF16)
            dkg_ref[...] += jnp.sum(dgr, axis=0, keepdims=True)
        dc_ = dc_ref[...]
        dlogf = _rcumsum_rows(dc_) + carry[...]
        carry[...] += jnp.sum(dc_, axis=0, keepdims=True)
        lanes = lax.broadcasted_iota(jnp.int32, dc_.shape, 1)
        df = jnp.where(lanes < NH1, dlogf * (1.0 - _sigmoid(f_ref[...] + fb_ref[...])), 0.0)
        df_ref[...] = df.astype(BF16)
        dfb_ref[...] += jnp.sum(df, axis=0, keepdims=True)

    rv = lambda i: nb - 1 - i
    row = lambda c: pl.BlockSpec((bm, D), lambda i: (rv(i), c))
    lane = lambda c: pl.BlockSpec((bm, HD), lambda i: (rv(i), c))
    vec = pl.BlockSpec((1, HD), lambda i: (0, 0))
    return _pc(body, name, (nb,), [row(0), row(0), row(0), row(1), lane(4 * D // HD), vec, vec, vec, lane(0)],
               [row(0), row(0), lane(0), vec, vec, vec],
               [_sds((t, D), BF16), _sds((t, D), BF16), _sds((t, HD), BF16), _sds((1, HD)), _sds((1, HD)), _sds((1, HD))],
               scratch=[pltpu.VMEM((1, HD), F32)])(dqn, dkn, z1, z1, z1, qg, kg, fbp, dc)


LOG2E = 1.4426950408889634
FOX_QSCALE = HD ** -0.5 * LOG2E


def _fox_scores(q, k, ck, i, j, bq, bk, masked):
    s = _nt(q, k) - ck
    if not masked:
        return s, None
    rows = i * bq + lax.broadcasted_iota(jnp.int32, s.shape, 0)
    cols = j * bk + lax.broadcasted_iota(jnp.int32, s.shape, 1)
    return s, cols <= rows


def _fox_block_kind(i, j, bq, bk):
    active = j * bk < (i + 1) * bq
    full = (j + 1) * bk <= i * bq + 1
    return full, active & jnp.logical_not(full)


def _fox_fwd(qn, kn, vb, crow, name):
    t = qn.shape[0]
    bq, bk, G = min(FOX_BQ, t), min(FOX_BK, t), FOX_HEADS_PER_STEP
    nq, nk = t // bq, t // bk

    def body(q_ref, k_ref, v_ref, ck_ref, o_ref, lse_ref, *scr):
        i, j = pl.program_id(1), pl.program_id(2)
        m_s, l_s, acc = scr[:G], scr[G:2 * G], scr[2 * G:]

        @pl.when(j == 0)
        def _():
            for g in range(G):
                m_s[g][...] = jnp.full_like(m_s[g], NEG)
                l_s[g][...] = jnp.zeros_like(l_s[g])
                acc[g][...] = jnp.zeros_like(acc[g])

        def step(masked):
            for g in range(G):
                sl = slice(g * HD, (g + 1) * HD)
                k, v, ck = k_ref[:, sl], v_ref[:, sl], ck_ref[g]
                for r0 in range(0, bq, FOX_SUB):
                    rs = slice(r0, r0 + FOX_SUB)
                    s = _nt(q_ref[rs, sl], k) - ck
                    if masked:
                        rows = i * bq + r0 + lax.broadcasted_iota(jnp.int32, s.shape, 0)
                        cols = j * bk + lax.broadcasted_iota(jnp.int32, s.shape, 1)
                        s = jnp.where(cols <= rows, s, NEG)
                    m_old = m_s[g][rs, :]
                    m_new = jnp.maximum(m_old, jnp.max(s, axis=1, keepdims=True))
                    alpha = jnp.exp2(m_old - m_new)
                    p = jnp.exp2(s - m_new)
                    l_s[g][rs, :] = alpha * l_s[g][rs, :] + jnp.sum(p, axis=1, keepdims=True)
                    acc[g][rs, :] = alpha * acc[g][rs, :] + _nn(p.astype(BF16), v)
                    m_s[g][rs, :] = m_new

        full, part = _fox_block_kind(i, j, bq, bk)
        pl.when(full)(lambda: step(False))
        pl.when(part)(lambda: step(True))

        @pl.when(j == nk - 1)
        def _():
            for g in range(G):
                l = l_s[g][...]
                o_ref[:, g * HD:(g + 1) * HD] = acc[g][...] / l
                lse_ref[g] = m_s[g][...] + jnp.log2(l)

    kj = lambda i, j: jnp.minimum(j, ((i + 1) * bq - 1) // bk)
    kmap = lambda h, i, j: (kj(i, j), h)
    return _pc(body, name, (NH1 // G, nq, nk),
               [pl.BlockSpec((bq, G * HD), lambda h, i, j: (i, h)), pl.BlockSpec((bk, G * HD), kmap),
                pl.BlockSpec((bk, G * HD), kmap), pl.BlockSpec((G, 1, bk), lambda h, i, j: (h, 0, kj(i, j)))],
               [pl.BlockSpec((bq, G * HD), lambda h, i, j: (i, h)), pl.BlockSpec((G, bq, 1), lambda h, i, j: (h, i, 0))],
               [_sds((t, D)), _sds((NH1, t, 1))],
               scratch=[pltpu.VMEM((bq, 1), F32)] * (2 * G) + [pltpu.VMEM((bq, HD), F32)] * G)(qn, kn, vb, crow)


def _fox_bwd(qn, kn, vb, crow, lse, delta, do, name):
    t = qn.shape[0]
    bq, bk, G = min(FOX_BQ, t), min(FOX_BK, t), FOX_HEADS_PER_STEP
    nq, nk = t // bq, t // bk

    def body(q_ref, k_ref, v_ref, ck_ref, lse_ref, dl_ref, do_ref, dq_ref, dk_ref, dv_ref, dc_ref, dcq_ref, dk_s, dv_s, dc_s):
        j, i = pl.program_id(1), pl.program_id(2)

        @pl.when(i == 0)
        def _():
            dk_s[...] = jnp.zeros_like(dk_s)
            dv_s[...] = jnp.zeros_like(dv_s)
            dc_s[...] = jnp.zeros_like(dc_s)

        @pl.when((i == 0) & (j == 0))
        def _():
            dq_ref[...] = jnp.zeros_like(dq_ref)
            dcq_ref[...] = jnp.zeros_like(dcq_ref)

        def step(masked):
            rows = pl.ds(pl.multiple_of(i * bq, bq), bq)
            for g in range(G):
                sl = slice(g * HD, (g + 1) * HD)
                q, k = q_ref[:, sl], k_ref[:, sl]
                s, ok = _fox_scores(q, k, ck_ref[g], i, j, bq, bk, masked)
                if masked:
                    s = jnp.where(ok, s, NEG)
                p = jnp.exp2(s - lse_ref[g])
                dob = do_ref[:, sl]
                dv_s[:, sl] += _tn(p.astype(BF16), dob)
                ds = p * (_nt(dob, v_ref[:, sl]) - dl_ref[g])
                dsb = ds.astype(BF16)
                dq_ref[rows, sl] += _nn(dsb, k)
                dk_s[:, sl] += _tn(dsb, q)
                dc_s[g] -= jnp.sum(ds, axis=0, keepdims=True)
                dcq_ref[g, rows, :] += jnp.sum(ds, axis=1, keepdims=True)

        full, part = _fox_block_kind(i, j, bq, bk)
        pl.when(full)(lambda: step(False))
        pl.when(part)(lambda: step(True))

        @pl.when(i == nq - 1)
        def _():
            dk_ref[...] = dk_s[...] * (1.0 / LOG2E)
            dv_ref[...] = dv_s[...]
            dc_ref[...] = dc_s[...]

    qi = lambda i, j: jnp.maximum(i, (j * bk) // bq)
    qmap = lambda h, j, i: (qi(i, j), h)
    c3map = lambda h, j, i: (h, qi(i, j), 0)
    kspec = pl.BlockSpec((bk, G * HD), lambda h, j, i: (j, h))
    return _pc(body, name, (NH1 // G, nk, nq),
               [pl.BlockSpec((bq, G * HD), qmap), kspec, kspec,
                pl.BlockSpec((G, 1, bk), lambda h, j, i: (h, 0, j)), pl.BlockSpec((G, bq, 1), c3map),
                pl.BlockSpec((G, bq, 1), c3map), pl.BlockSpec((bq, G * HD), qmap)],
               [pl.BlockSpec((t, G * HD), lambda h, j, i: (0, h)), kspec, kspec, pl.BlockSpec((G, 1, bk), lambda h, j, i: (h, 0, j)),
                pl.BlockSpec((G, t, 1), lambda h, j, i: (h, 0, 0))],
               [_sds((t, D)), _sds((t, D)), _sds((t, D)), _sds((NH1, 1, t)), _sds((NH1, t, 1))],
               scratch=[pltpu.VMEM((bk, G * HD), F32), pltpu.VMEM((bk, G * HD), F32), pltpu.VMEM((G, 1, bk), F32)],
               )(qn, kn, vb, crow, lse, delta, do)


def _post1_fwd(o, z1, w, h3, name, bm=512):
    t = o.shape[0]
    bm = min(bm, t)

    def body(o_ref, g_ref, w_ref, h_ref, out_ref, og_ref):
        og_ref[...] = (o_ref[...] * _sigmoid(g_ref[...])).astype(BF16)
        out_ref[...] = h_ref[...] + _nn(og_ref[...], w_ref[...])

    row = lambda c: pl.BlockSpec((bm, D), lambda i: (i, c))
    return _pc(body, name, (t // bm,), [row(0), row(3), pl.BlockSpec((D, D), lambda i: (0, 0)), row(0)],
               [row(0), row(0)], [_sds((t, D)), _sds((t, D), BF16)])(o, z1, w, h3)


def _post1_bwd(dh4, w, o, z1, name, bm=512):
    t = o.shape[0]
    bm = min(bm, t)

    def body(dh_ref, w_ref, o_ref, g_ref, do_ref, dg_ref, dl_ref):
        d_og = _nt(dh_ref[...].astype(BF16), w_ref[...])
        o_, sg = o_ref[...], _sigmoid(g_ref[...])
        dob = (d_og * sg).astype(BF16)
        do_ref[...] = dob
        dg_ref[...] = (d_og * o_ * sg * (1.0 - sg)).astype(BF16)
        prod = dob.astype(F32) * o_
        for hd in range(NH1):
            dl_ref[hd] = jnp.sum(prod[:, hd * HD:(hd + 1) * HD], axis=1, keepdims=True)

    row = lambda c: pl.BlockSpec((bm, D), lambda i: (i, c))
    return _pc(body, name, (t // bm,), [row(0), pl.BlockSpec((D, D), lambda i: (0, 0)), row(0), row(3)],
               [row(0), row(0), pl.BlockSpec((NH1, bm, 1), lambda i: (0, i, 0))],
               [_sds((t, D), BF16), _sds((t, D), BF16), _sds((NH1, t, 1))])(dh4, w, o, z1)


def _final(h, g, tgt, name, bm=512):
    t = h.shape[0]
    bm = min(bm, t)

    def body(h_ref, g_ref, t_ref, l_ref, dh_ref, dg_ref):
        @pl.when(pl.program_id(0) == 0)
        def _():
            l_ref[...] = jnp.zeros_like(l_ref)
            dg_ref[...] = jnp.zeros_like(dg_ref)

        x, gv = h_ref[...], g_ref[...]
        r = _rstd(x)
        xh = x * r
        e = xh * gv - t_ref[...]
        l_ref[...] += 0.5 * jnp.sum(jnp.mean(e * e, axis=1, keepdims=True), axis=0, keepdims=True)
        dy = e * (1.0 / D)
        dg_ref[...] += jnp.sum(dy * xh, axis=0, keepdims=True)
        dxh = dy * gv
        dh_ref[...] = r * (dxh - xh * jnp.mean(dxh * xh, axis=1, keepdims=True))

    row = pl.BlockSpec((bm, D), lambda i: (i, 0))
    vec = pl.BlockSpec((1, D), lambda i: (0, 0))
    return _pc(body, name, (t // bm,), [row, vec, row], [pl.BlockSpec((1, HD), lambda i: (0, 0)), row, vec],
               [_sds((1, HD)), _sds((t, D)), _sds((1, D))])(h, g, tgt)


def _adam(w, g, m, v, name):
    r, c = w.shape
    br = min(r, 256)

    def body(w_ref, g_ref, m_ref, v_ref, d_ref, mo_ref, vo_ref):
        gv = g_ref[...]
        mn = ADAM_B1 * m_ref[...] + (1.0 - ADAM_B1) * gv
        vn = ADAM_B2 * v_ref[...] + (1.0 - ADAM_B2) * jnp.square(gv)
        m_hat = mn / (1.0 - ADAM_B1 ** ADAM_STEP)
        v_hat = vn / (1.0 - ADAM_B2 ** ADAM_STEP)
        d_ref[...] = -ADAM_LR * (m_hat / (jnp.sqrt(v_hat) + ADAM_EPS) + ADAM_WD * w_ref[...])
        mo_ref[...] = mn
        vo_ref[...] = vn

    blk = pl.BlockSpec((br, c), lambda i: (i, 0))
    return _pc(body, name, (r // br,), [blk] * 4, [blk] * 3, [_sds((r, c))] * 3)(w, g, m, v)


ZW = 4224
GATE0 = 4096


def _pack_w_in0(w):
    return jnp.concatenate([w[:, :2048], w[:, 2056:], w[:, 2048:2056], jnp.zeros((w.shape[0], ZW - 4104), w.dtype)], axis=1)


def _unpack_w_in0(g):
    return jnp.concatenate([g[:, :2048], g[:, GATE0:GATE0 + 8], g[:, 2048:GATE0]], axis=1)


def _pack_w_in1(w):
    return jnp.concatenate([w, jnp.zeros((w.shape[0], ZW - 4104), w.dtype)], axis=1)


def _unpack_w_in1(g):
    return g[:, :4104]


def _local_step(x, mem, tgt, W, S, late_weights=None, grads_hook=None):
    t = x.shape[0]
    row = lambda v: v.reshape(1, -1)
    G = {}

    z0, u0 = _norm_mm(x, S["norm_mix_g"][0:1], W["w_in0"], "in0_fwd")
    qk = _conv_fwd(z0, S["conv_w"], "conv_fwd")
    g8 = z0[:, GATE0:GATE0 + 8]
    gates3 = jnp.stack([g8[:, :4].T, g8[:, 4:].T], axis=-1)
    gb = S["gate_b"]
    bias3 = jnp.stack([gb[0, :4], gb[0, 4:]], axis=-1)[:, None, :]
    hm, cs, ns, ms = _mlstm_fwd(qk, z0, gates3, bias3, "mlstm_fwd")
    hh, ss = _hgrn_fwd(z0, S["lb_logits"], "hgrn_fwd")
    if late_weights is not None:
        W = {**W, **late_weights(hh)}
    kv, mn = _memkv_fwd(mem, row(S["mem_norm_g"]), W["wkv_s"], "memkv_fwd")
    h1, y0 = _post0_fwd(hm, hh, z0, S["mlstm_norm_g"], S["hgrn_norm_g"], W["w_out0"], x, "post0_fwd")

    def xattn_mlp_fwd(h, l):
        q, ux = _norm_mm(h, S["norm_xattn_g"][l:l + 1], W["wq"][l], f"xq{l}_fwd")
        h2, ox = _xattn_fwd(q, kv, W["wo"][l], h, f"xattn{l}_fwd")
        h3, a, um = _mlp_fwd(h2, S["norm_mlp_g"][l:l + 1], W["w1s"], W["w2"], l, f"mlp{l}_fwd")
        return h3, (h, q, ux, ox, h2, a, um)

    h3, sv0 = xattn_mlp_fwd(h1, 0)
    z1, u1 = _norm_mm(h3, S["norm_mix_g"][1:2], W["w_in1"], "in1_fwd")
    fbp = jnp.pad(S["c_fgate_b"], ((0, 0), (0, HD - NH1)))
    qn, kn, vb, c = _foxprep_fwd(z1, S["c_qnorm_g"], S["c_knorm_g"], fbp, "foxprep_fwd")
    crow = (c[:, :NH1] * LOG2E).T[:, None, :]
    o1, lse = _fox_fwd(qn, kn, vb, crow, "fox_fwd")
    h4, og = _post1_fwd(o1, z1, W["w_out1"], h3, "post1_fwd")
    h6, sv1 = xattn_mlp_fwd(h4, 1)
    lossp, dh, G["final_norm_g"] = _final(h6, row(S["final_norm_g"]), tgt, "final")

    grads_ready = grads_hook if grads_hook is not None else (lambda stage, grads: 0.0)
    dkv = None
    dgx, dgm, dwq, dwo, dw1, dw2 = [None, None], [None, None], [None, None], [None, None], [None, None], [None, None]

    def xattn_mlp_bwd(dh, l, sv):
        nonlocal dkv
        h, q, ux, ox, h2, a, um = sv
        dh2, da, r, dgm[l] = _mlp_bwd(dh, a, W["w1s"], W["w2"], l, h2, S["norm_mlp_g"][l:l + 1], f"mlp{l}_bwd")
        dw1[l] = _mm_tn(um, da, f"mlp{l}_dw1", col_chips=NCHIP)
        dw2[l] = _mm_tn(r, dh, f"mlp{l}_dw2")
        dq, dkv_l = _xattn_bwd(dh2, q, kv, W["wo"][l], f"xattn{l}_bwd")
        dkv = dkv_l if dkv is None else dkv + dkv_l
        dwo[l] = _mm_tn(ox, dh2, f"xattn{l}_dwo")
        dwq[l] = _mm_tn(ux, dq, f"xattn{l}_dwq")
        tok = grads_ready("layer0_mlp_xattn", dict(wq=dwq[0], wo=dwo[0], w1=dw1[0], w2=dw2[0])) if l == 0 else 0.0
        dh1, dgx[l] = _bwd_in(dq, W["wq"][l], h, S["norm_xattn_g"][l:l + 1] + tok, dh2, f"xq{l}_bwd")
        return dh1

    dh4 = xattn_mlp_bwd(dh, 1, sv1)
    do, dgate, delta = _post1_bwd(dh4, W["w_out1"], o1, z1, "post1_bwd")
    G["w_out1"] = _mm_tn(og, dh4, "post1_dw")
    dqn, dkn, dv1, dcrow, dcq = _fox_bwd(qn, kn, vb, crow, lse, delta, do, "fox_bwd")
    dc = jnp.pad((dcrow[:, 0, :] + dcq[:, :, 0]).T, ((0, 0), (0, HD - NH1)))
    dqr, dkr, df1, G["c_qnorm_g"], G["c_knorm_g"], dfb = _foxprep_bwd(
        dqn, dkn, z1, S["c_qnorm_g"], S["c_knorm_g"], fbp, dc, "foxprep_bwd")
    G["c_fgate_b"] = dfb[:, :NH1]
    dz1 = jnp.concatenate([dqr, dkr, dv1.astype(BF16), dgate, df1], axis=1)
    G["w_in1"] = _mm_tn(u1, dz1, "in1_dw")
    tok = grads_ready("layer1", dict(w_out=G["w_out1"], w_in=G["w_in1"], wq=dwq[1], wo=dwo[1], w1=dw1[1], w2=dw2[1]))
    dh3, dgmix1 = _bwd_in(dz1, W["w_in1"], h3, S["norm_mix_g"][1:2] + tok, dh4, "in1_bwd")
    dh1 = xattn_mlp_bwd(dh3, 0, sv0)

    dhm, dhh, doa, dgb, G["mlstm_norm_g"], G["hgrn_norm_g"] = _post0_bwd(
        dh1, W["w_out0"], hm, hh, z0, S["mlstm_norm_g"], S["hgrn_norm_g"], "post0_bwd")
    G["w_out0"] = _mm_tn(y0, dh1, "post0_dw")
    dqa, dka, dva, dgates3 = _mlstm_bwd(qk, z0, gates3, bias3, cs, ns, ms, dhm, "mlstm_bwd")
    dqb, dfb0, dib, G["lb_logits"] = _hgrn_bwd(z0, S["lb_logits"], ss, dhh, "hgrn_bwd")
    duc, G["conv_w"] = _conv_bwd(z0, S["conv_w"], jnp.concatenate([dqa, dka], axis=1), "conv_bwd")
    dg8 = jnp.concatenate([dgates3[:, :, 0].T, dgates3[:, :, 1].T], axis=1)
    G["gate_b"] = jnp.sum(dg8, axis=0, keepdims=True)
    dz0 = jnp.concatenate([duc, dva.astype(BF16), doa, dqb, dfb0, dib, dgb,
                           jnp.pad(dg8, ((0, 0), (0, HD - 8))).astype(BF16)], axis=1)
    G["w_in0"] = _mm_tn(u0, dz0, "in0_dw")
    dx, dgmix0 = _bwd_in(dz0, W["w_in0"], x, S["norm_mix_g"][0:1], dh1, "in0_bwd")

    G["wkv"] = _mm_tn(mn, dkv, "memkv_dw", col_chips=NCHIP)
    G["mem_norm_g"] = _memkv_bwd(dkv, W["wkv_s"], mem, row(S["mem_norm_g"]), "memkv_bwd")
    G["norm_mix_g"] = jnp.concatenate([dgmix0, dgmix1], axis=0)
    G["norm_xattn_g"] = jnp.concatenate(dgx, axis=0)
    G["norm_mlp_g"] = jnp.concatenate(dgm, axis=0)
    G["wq"], G["wo"], G["w1"], G["w2"] = dwq, dwo, dw1, dw2
    return lossp[0, 0], dx, G


ANY = pl.BlockSpec(memory_space=pl.ANY)
NCHIP = 4


def _place():
    x, y, c = lax.axis_index("x"), lax.axis_index("y"), lax.axis_index("c")
    return x, y, c, [(1 - x, y), (x, 1 - y), (1 - x, 1 - y)]


def _comm_call(body, name, ins, out_shapes, sems):
    return pl.pallas_call(body, name=name, in_specs=[ANY] * len(ins), out_specs=[ANY] * len(out_shapes),
                          out_shape=out_shapes, scratch_shapes=sems)(*ins)


def _gather_weights(arrs, name):
    n = len(arrs)

    def body(*refs):
        ins, outs = refs[:n], refs[n:2 * n]
        send_i, recv_i, send_d, recv_d = refs[2 * n:]
        x, y, c, chips = _place()
        me = 2 * x + y

        def half(a, cc):
            h = arrs[a].shape[0] // 2
            return pl.ds(pl.multiple_of(cc * h, h), h)

        def ici(a, k, src_chip, dst_dev):
            return pltpu.make_async_remote_copy(
                src_ref=ins[a].at[half(a, c)], dst_ref=outs[a].at[src_chip, half(a, c)], send_sem=send_i.at[a, k],
                recv_sem=recv_i.at[a, k], device_id=dst_dev, device_id_type=MESH)

        def d2d(a, k, src_chip, cc):
            reg = outs[a].at[src_chip, half(a, cc)]
            return pltpu.make_async_remote_copy(src_ref=reg, dst_ref=reg, send_sem=send_d.at[a, k], recv_sem=recv_d.at[a, k],
                                                device_id=(x, y, 1 - c), device_id_type=MESH)

        for a in range(n):
            for k, (px, py) in enumerate(chips):
                ici(a, k, me, (px, py, c)).start()
        for k, (px, py) in enumerate(chips):
            for a in range(n):
                ici(a, k, 2 * px + py, (px, py, c)).wait_recv()
                d2d(a, k, 2 * px + py, c).start()
        for k, (px, py) in enumerate(chips):
            for a in range(n):
                ici(a, k, me, (px, py, c)).wait_send()
                d2d(a, k, 2 * px + py, c).wait_send()
                d2d(a, k, 2 * px + py, 1 - c).wait_recv()

    sem = lambda: pltpu.SemaphoreType.DMA((n, 3))
    return _comm_call(body, name, arrs, [_sds((NCHIP,) + a.shape, a.dtype) for a in arrs], [sem(), sem(), sem(), sem()])


HBM = pl.BlockSpec(memory_space=pltpu.HBM)
SEM = pl.BlockSpec(memory_space=pltpu.SEMAPHORE)
DATAFLOW = pltpu.SideEffectType.DATAFLOW_SIDE_EFFECTING


def _half_rows(r, cc):
    return pl.ds(pl.multiple_of(cc * (r // 2), r // 2), r // 2)


def _gather_start(arrs, after, name):
    n = len(arrs)

    def body(*refs):
        ins, lands = refs[:n], refs[n:2 * n]
        send, recv, token = refs[2 * n + 1], refs[2 * n + 2], refs[-1]
        x, y, c, chips = _place()
        me = 2 * x + y
        for a in range(n):
            rows = _half_rows(arrs[a].shape[0], c)
            for k, (px, py) in enumerate(chips):
                pltpu.make_async_remote_copy(src_ref=ins[a].at[rows], dst_ref=lands[a].at[me, rows], send_sem=send.at[3 * a + k],
                                             recv_sem=recv.at[3 * a + k], device_id=(px, py, c), device_id_type=MESH).start()
        token[...] = jnp.zeros_like(token)

    hbm = lambda v: pltpu.with_memory_space_constraint(v, pltpu.HBM)
    land_shapes = [((NCHIP,) + a.shape, a.dtype) for a in arrs]
    out = pl.pallas_call(
        body, name=name,
        out_shape=(pltpu.SemaphoreType.DMA((3 * n,)), pltpu.SemaphoreType.DMA((3 * n,)), *[pltpu.HBM(a.shape, a.dtype) for a in arrs],
                   *[pltpu.HBM(s, d) for s, d in land_shapes], _sds((8, HD))),
        in_specs=[HBM] * (2 * n) + [ANY], out_specs=(SEM, SEM, *[HBM] * (2 * n), pl.BlockSpec(memory_space=pltpu.VMEM)),
        input_output_aliases={i: 2 + i for i in range(2 * n)},
        compiler_params=pltpu.CompilerParams(has_side_effects=DATAFLOW),
    )(*[hbm(a) for a in arrs], *[hbm(lax.empty(s, d)) for s, d in land_shapes], after)
    return out[0], out[1], list(out[2:2 + n]), list(out[2 + n:2 + 2 * n]), out[-1]


def _gather_wait(send, recv, srcs, lands, after, name):
    n = len(srcs)

    def body(*refs):
        ins, lands_ = refs[:n], refs[n:2 * n]
        send_, recv_ = refs[2 * n], refs[2 * n + 1]
        x, y, c, chips = _place()
        for a in range(n):
            rows = _half_rows(srcs[a].shape[0], c)
            for k, (px, py) in enumerate(chips):
                cp = pltpu.make_async_remote_copy(src_ref=ins[a].at[rows], dst_ref=lands_[a].at[2 * px + py, rows], send_sem=send_.at[3 * a + k],
                                                  recv_sem=recv_.at[3 * a + k], device_id=(px, py, c), device_id_type=MESH)
                cp.wait_send()
                cp.wait_recv()

    out = pl.pallas_call(
        body, name=name, out_shape=[pltpu.HBM(v.shape, v.dtype) for v in list(srcs) + list(lands)],
        in_specs=[HBM] * (2 * n) + [SEM, SEM, ANY], out_specs=[HBM] * (2 * n), input_output_aliases={i: i for i in range(2 * n)},
        compiler_params=pltpu.CompilerParams(has_side_effects=DATAFLOW),
    )(*srcs, *lands, send, recv, after)
    return list(out[n:])


def _pair_forward(lands, name):
    n = len(lands)

    def body(*refs):
        ins, outs = refs[:n], refs[n:2 * n]
        send, recv = refs[2 * n:]
        x, y, c, chips = _place()
        copies = []
        for a in range(n):
            r = lands[a].shape[1]
            for k, (px, py) in enumerate(chips):
                cp = pltpu.make_async_remote_copy(
                    src_ref=ins[a].at[2 * px + py, _half_rows(r, c)], dst_ref=outs[a].at[2 * px + py, _half_rows(r, c)],
                    send_sem=send.at[a, k], recv_sem=recv.at[a, k], device_id=(x, y, 1 - c), device_id_type=MESH)
                cp.start()
                copies.append(cp)
        for a in range(n):
            r = lands[a].shape[1]
            for k, (px, py) in enumerate(chips):
                pltpu.make_async_remote_copy(
                    src_ref=ins[a].at[2 * px + py, _half_rows(r, c)], dst_ref=outs[a].at[2 * px + py, _half_rows(r, 1 - c)],
                    send_sem=send.at[a, k], recv_sem=recv.at[a, k], device_id=(x, y, 1 - c), device_id_type=MESH).wait_recv()
        for cp in copies:
            cp.wait_send()

    return pl.pallas_call(body, name=name, in_specs=[ANY] * n, out_specs=[ANY] * n, out_shape=[_sds(v.shape, v.dtype) for v in lands],
                          scratch_shapes=[pltpu.SemaphoreType.DMA((n, 3)), pltpu.SemaphoreType.DMA((n, 3))],
                          input_output_aliases={i: i for i in range(n)})(*lands)


def _pair_exchange(arrs, name):
    n = len(arrs)

    def body(*refs):
        ins, outs = refs[:n], refs[n:2 * n]
        send, recv = refs[2 * n:]
        x, y, c, _ = _place()
        copies = []
        for a in range(n):
            h = arrs[a].shape[1] // 2
            cp = pltpu.make_async_remote_copy(src_ref=ins[a].at[:, pl.ds(pl.multiple_of((1 - c) * h, h), h)], dst_ref=outs[a],
                                              send_sem=send.at[a], recv_sem=recv.at[a], device_id=(x, y, 1 - c), device_id_type=MESH)
            cp.start()
            copies.append(cp)
        for cp in copies:
            cp.wait()

    return _comm_call(body, name, arrs, [_sds((a.shape[0], a.shape[1] // 2, a.shape[2]), a.dtype) for a in arrs],
                      [pltpu.SemaphoreType.DMA((n,)), pltpu.SemaphoreType.DMA((n,))])


def _chip_exchange(arrs, name):
    n = len(arrs)

    def body(*refs):
        ins, outs = refs[:n], refs[n:2 * n]
        send, recv = refs[2 * n:]
        x, y, c, chips = _place()
        me = 2 * x + y
        copies = []
        for a in range(n):
            for k, (px, py) in enumerate(chips):
                r = pltpu.make_async_remote_copy(src_ref=ins[a].at[2 * px + py], dst_ref=outs[a].at[me], send_sem=send.at[a, k],
                                                 recv_sem=recv.at[a, k], device_id=(px, py, c), device_id_type=MESH)
                r.start()
                copies.append(r)
        for cp in copies:
            cp.wait()

    return _comm_call(body, name, arrs, [_sds(a.shape, a.dtype) for a in arrs],
                      [pltpu.SemaphoreType.DMA((n, 3)), pltpu.SemaphoreType.DMA((n, 3))])


def _chip_exchange_start(arrs, name):
    n = len(arrs)

    def body(*refs):
        ins, lands = refs[:n], refs[n:2 * n]
        send, recv, token = refs[2 * n], refs[2 * n + 1], refs[-1]
        x, y, c, chips = _place()
        me = 2 * x + y
        for a in range(n):
            for k, (px, py) in enumerate(chips):
                pltpu.make_async_remote_copy(src_ref=ins[a].at[2 * px + py], dst_ref=lands[a].at[me], send_sem=send.at[3 * a + k],
                                             recv_sem=recv.at[3 * a + k], device_id=(px, py, c), device_id_type=MESH).start()
        token[...] = jnp.zeros_like(token)

    hbm = lambda v: pltpu.with_memory_space_constraint(v, pltpu.HBM)
    out = pl.pallas_call(
        body, name=name,
        out_shape=(pltpu.SemaphoreType.DMA((3 * n,)), pltpu.SemaphoreType.DMA((3 * n,)), *[pltpu.HBM(a.shape, a.dtype) for a in arrs],
                   *[pltpu.HBM(a.shape, a.dtype) for a in arrs], _sds((8, HD))),
        in_specs=[HBM] * (2 * n), out_specs=(SEM, SEM, *[HBM] * (2 * n), pl.BlockSpec(memory_space=pltpu.VMEM)),
        input_output_aliases={i: 2 + i for i in range(2 * n)},
        compiler_params=pltpu.CompilerParams(has_side_effects=DATAFLOW),
    )(*[hbm(a) for a in arrs], *[hbm(lax.empty(a.shape, a.dtype)) for a in arrs])
    return out[0], out[1], list(out[2:2 + n]), list(out[2 + n:2 + 2 * n]), out[-1]


def _chip_exchange_wait(send, recv, srcs, lands, after, name):
    n = len(srcs)

    def body(*refs):
        ins, lands_ = refs[:n], refs[n:2 * n]
        send_, recv_ = refs[2 * n], refs[2 * n + 1]
        x, y, c, chips = _place()
        for a in range(n):
            for k, (px, py) in enumerate(chips):
                cp = pltpu.make_async_remote_copy(src_ref=ins[a].at[2 * px + py], dst_ref=lands_[a].at[2 * px + py], send_sem=send_.at[3 * a + k],
                                                  recv_sem=recv_.at[3 * a + k], device_id=(px, py, c), device_id_type=MESH)
                cp.wait_send()
                cp.wait_recv()

    out = pl.pallas_call(
        body, name=name, out_shape=[pltpu.HBM(v.shape, v.dtype) for v in list(srcs) + list(lands)],
        in_specs=[HBM] * (2 * n) + [SEM, SEM, ANY], out_specs=[HBM] * (2 * n), input_output_aliases={i: i for i in range(2 * n)},
        compiler_params=pltpu.CompilerParams(has_side_effects=DATAFLOW),
    )(*srcs, *lands, send, recv, after)
    return list(out[n:])


def _pair_swap(arrs, name):
    n = len(arrs)

    def body(*refs):
        ins, outs = refs[:n], refs[n:2 * n]
        send, recv = refs[2 * n:]
        x, y, c, _ = _place()
        copies = []
        for a in range(n):
            cp = pltpu.make_async_remote_copy(src_ref=ins[a], dst_ref=outs[a], send_sem=send.at[a], recv_sem=recv.at[a],
                                              device_id=(x, y, 1 - c), device_id_type=MESH)
            cp.start()
            copies.append(cp)
        for cp in copies:
            cp.wait()

    return _comm_call(body, name, arrs, [_sds(a.shape, a.dtype) for a in arrs],
                      [pltpu.SemaphoreType.DMA((n,)), pltpu.SemaphoreType.DMA((n,))])


def _all_gather_devices(v, name):
    def body(v_ref, o_ref, send, recv, loc):
        x, y, c, _ = _place()
        me = 4 * x + 2 * y + c
        own = pltpu.make_async_copy(v_ref, o_ref.at[me], loc)
        own.start()
        copies = [own]
        for k in range(1, 8):
            fx, fy, fc = (k >> 2) & 1, (k >> 1) & 1, k & 1
            peer = (x ^ fx, y ^ fy, c ^ fc)
            r = pltpu.make_async_remote_copy(src_ref=v_ref, dst_ref=o_ref.at[me], send_sem=send.at[k - 1],
                                             recv_sem=recv.at[k - 1], device_id=peer, device_id_type=MESH)
            r.start()
            copies.append(r)
        for cp in copies:
            cp.wait()

    return _comm_call(body, name, [v], [_sds((8,) + v.shape, v.dtype)],
                      [pltpu.SemaphoreType.DMA((7,)), pltpu.SemaphoreType.DMA((7,)), pltpu.SemaphoreType.DMA])[0]


def _row_tile(r):
    return next((b for b in (512, 384, 256, 128, 64, 32, 16) if r % b == 0), r)


def _add2(a, b, out_dtype, name):
    r, w = a.shape
    br = _row_tile(r)

    def body(a_ref, b_ref, o_ref):
        o_ref[...] = (a_ref[...].astype(F32) + b_ref[...].astype(F32)).astype(out_dtype)

    blk = pl.BlockSpec((br, w), lambda i: (i, 0))
    return _pc(body, name, (r // br,), [blk, blk], blk, _sds((r, w), out_dtype))(a, b)


def _sum_slots(a, out_dtype, name, extra=None):
    n, r, w = a.shape
    br = _row_tile(r)

    def body(*refs):
        a_ref, o_ref = refs[0], refs[-1]
        acc = a_ref[0].astype(F32)
        for s in range(1, n):
            acc = acc + a_ref[s].astype(F32)
        if extra is not None:
            acc = acc + refs[1][...].astype(F32)
        o_ref[...] = acc.astype(out_dtype)

    ins = [a] + ([extra] if extra is not None else [])
    specs = [pl.BlockSpec((n, br, w), lambda i: (0, i, 0))] + ([pl.BlockSpec((br, w), lambda i: (i, 0))] if extra is not None else [])
    return _pc(body, name, (r // br,), specs, pl.BlockSpec((br, w), lambda i: (i, 0)), _sds((r, w), out_dtype))(*ins)


SMALL = ["norm_mix_g", "norm_xattn_g", "norm_mlp_g", "final_norm_g", "mem_norm_g", "hgrn_lb_logits", "mlstm_norm_g",
         "hgrn_norm_g", "c_qnorm_g", "c_knorm_g", "ab_gate_b", "c_fgate_b"]
SMALL_ROWS = 16


def _pack_small(parts):
    flat = jnp.concatenate([p.reshape(-1).astype(F32) for p in parts])
    return jnp.pad(flat, (0, SMALL_ROWS * D - flat.shape[0])).reshape(SMALL_ROWS, D)


def _unpack_small(buf, shapes):
    flat, out, off = buf.reshape(-1), [], 0
    for s in shapes:
        n = 1
        for d in s:
            n *= d
        out.append(flat[off:off + n].reshape(s))
        off += n
    return out


def kernel(x, mem, norm_mix_g, norm_xattn_g, norm_mlp_g, final_norm_g, ab_w_in, ab_conv_w, ab_gate_b, hgrn_lb_logits, mlstm_norm_g, hgrn_norm_g, ab_w_out, c_w_in, c_fgate_b, c_qnorm_g, c_knorm_g, c_w_out, mem_norm_g, mem_w_kv, xa_w_q, xa_w_o, mlp_w1, mlp_w2, loss_target, m_norm_mix_g, m_norm_xattn_g, m_norm_mlp_g, m_final_norm_g, m_ab_w_in, m_ab_conv_w, m_ab_gate_b, m_hgrn_lb_logits, m_mlstm_norm_g, m_hgrn_norm_g, m_ab_w_out, m_c_w_in, m_c_fgate_b, m_c_qnorm_g, m_c_knorm_g, m_c_w_out, m_mem_norm_g, m_mem_w_kv, m_xa_w_q, m_xa_w_o, m_mlp_w1, m_mlp_w2, v_norm_mix_g, v_norm_xattn_g, v_norm_mlp_g, v_final_norm_g, v_ab_w_in, v_ab_conv_w, v_ab_gate_b, v_hgrn_lb_logits, v_mlstm_norm_g, v_hgrn_norm_g, v_ab_w_out, v_c_w_in, v_c_fgate_b, v_c_qnorm_g, v_c_knorm_g, v_c_w_out, v_mem_norm_g, v_mem_w_kv, v_xa_w_q, v_xa_w_o, v_mlp_w1, v_mlp_w2):
    A = dict(locals())
    chip = 2 * lax.axis_index("x") + lax.axis_index("y")

    big = ["ab_w_in", "c_w_in", "ab_w_out", "c_w_out", "mem_w_kv", "xa_w_q", "xa_w_o", "mlp_w1", "mlp_w2"]
    shard2d = {"ab_w_in": (D, 1026), "c_w_in": (D, 1026), "ab_w_out": (256, D), "c_w_out": (256, D), "mem_w_kv": (D, 512),
               "xa_w_q": (512, D), "xa_w_o": (512, D), "mlp_w1": (2 * D, D), "mlp_w2": (2 * D, D)}
    shard16 = lambda n: A[n].reshape(shard2d[n]).astype(BF16)
    own_slot = lambda gs, os: [lax.dynamic_update_index_in_dim(g, o, chip, 0) for g, o in zip(gs, os)]
    cols = lambda g: jnp.concatenate([g[k] for k in range(NCHIP)], axis=1)
    per_layer = lambda g: g.reshape(NCHIP, 2, -1, D).transpose(1, 0, 2, 3)
    first = [shard16("ab_w_in"), jnp.pad(ab_conv_w[0], ((0, 16 - CONV_W), (0, 0)))]
    g_in0, g_conv = own_slot(_gather_weights(first, "gather_first"), first)
    W = dict(w_in0=_pack_w_in0(cols(g_in0)))
    rest_names = ["c_w_in", "ab_w_out", "c_w_out", "xa_w_q", "xa_w_o", "mlp_w1", "mlp_w2", "mem_w_kv"]
    rest = [shard16(n) for n in rest_names]
    send_s, recv_s, srcs, lands, token = _gather_start(rest, g_conv, "gather_rest_start")

    def late_weights(after):
        got = _pair_forward(_gather_wait(send_s, recv_s, srcs, lands, after, "gather_rest_wait"), "gather_rest_forward")
        gw = dict(zip(rest_names, own_slot(got, rest)))
        return dict(w_in1=_pack_w_in1(cols(gw["c_w_in"])), w_out0=gw["ab_w_out"].reshape(D, D), w_out1=gw["c_w_out"].reshape(D, D),
                    wkv_s=gw["mem_w_kv"],
                    wq=per_layer(gw["xa_w_q"]).reshape(2, D, D), wo=per_layer(gw["xa_w_o"]).reshape(2, D, D),
                    w1s=gw["mlp_w1"].reshape(NCHIP, 2, D, D), w2=gw["mlp_w2"].reshape(NCHIP, 2, D, D))

    S = dict(norm_mix_g=norm_mix_g + token[0, 0], norm_xattn_g=norm_xattn_g, norm_mlp_g=norm_mlp_g, final_norm_g=final_norm_g,
             conv_w=cols(g_conv[:, :CONV_W]), gate_b=ab_gate_b, lb_logits=hgrn_lb_logits, mlstm_norm_g=mlstm_norm_g,
             hgrn_norm_g=hgrn_norm_g, c_fgate_b=c_fgate_b, c_qnorm_g=c_qnorm_g, c_knorm_g=c_knorm_g, mem_norm_g=mem_norm_g)

    core = lax.axis_index("c")
    by_rows = lambda g: g.reshape(NCHIP, -1, D)

    def stack_cols(g):
        return jnp.stack([g[:, 1026 * k:1026 * (k + 1)] for k in range(NCHIP)])

    def pair_sums(arrs, tag):
        theirs = _pair_exchange(arrs, f"pair_exchange_{tag}")
        out = []
        for i, (a, th) in enumerate(zip(arrs, theirs)):
            h = a.shape[1] // 2
            mine = lax.dynamic_slice_in_dim(a, core * h, h, axis=1)
            out.append(_add2(mine.reshape(-1, a.shape[2]), th.reshape(-1, a.shape[2]), BF16, f"pair_sum_{tag}{i}").reshape(th.shape))
        return out

    def chip_sums(psums, from_chips, tag):
        out = []
        for i, (f, p) in enumerate(zip(from_chips, psums)):
            f = lax.dynamic_update_index_in_dim(f, lax.dynamic_index_in_dim(p, chip, 0, keepdims=False), chip, 0)
            out.append(_sum_slots(f, F32, f"chip_sum_{tag}{i}"))
        return out

    started = {}

    def grads_hook(stage, g):
        if stage == "layer1":
            arrs = [jnp.concatenate([by_rows(g["w_out"]), by_rows(g["wq"]), by_rows(g["wo"]), g["w1"], by_rows(g["w2"])], axis=1),
                    stack_cols(_unpack_w_in1(g["w_in"]))]
        else:
            arrs = [jnp.concatenate([by_rows(g["wq"]), by_rows(g["wo"]), g["w1"], by_rows(g["w2"])], axis=1)]
        psums = pair_sums(arrs, stage)
        *handles, token = _chip_exchange_start(psums, f"chip_exchange_start_{stage}")
        started[stage] = (psums, handles)
        return token[0, 0]

    lossp, dx, G = _local_step(x[0], mem[0], loss_target[0], W, S, late_weights, grads_hook)

    gsmall = {"norm_mix_g": G["norm_mix_g"], "norm_xattn_g": G["norm_xattn_g"], "norm_mlp_g": G["norm_mlp_g"],
              "final_norm_g": G["final_norm_g"], "mem_norm_g": G["mem_norm_g"], "hgrn_lb_logits": G["lb_logits"],
              "mlstm_norm_g": G["mlstm_norm_g"], "hgrn_norm_g": G["hgrn_norm_g"], "c_qnorm_g": G["c_qnorm_g"],
              "c_knorm_g": G["c_knorm_g"], "ab_gate_b": G["gate_b"], "c_fgate_b": G["c_fgate_b"]}
    packed = _pack_small([gsmall[n] for n in SMALL] + [G["conv_w"], lossp])
    red = _sum_slots(_all_gather_devices(packed, "gather_small"), F32, "sum_small")
    small_shapes = [A[n].shape for n in SMALL]
    *gs, gconv, loss = _unpack_small(red, small_shapes + [(CONV_W, D), ()])
    gs = dict(zip(SMALL, gs))
    gconv = lax.dynamic_slice_in_dim(gconv, chip * 256, 256, axis=1)[None]

    last = pair_sums([by_rows(G["w_out0"]), stack_cols(_unpack_w_in0(G["w_in0"])), G["wkv"]], "last")
    rhalf = chip_sums(last, _chip_exchange(last, "chip_exchange_last"), "last")
    for stage in ("layer1", "layer0_mlp_xattn"):
        psums, handles = started[stage]
        rhalf += chip_sums(psums, _chip_exchange_wait(*handles, dx, f"chip_exchange_wait_{stage}"), stage)
    other = _pair_swap(rhalf, "pair_swap")
    r_out0, r_in0, r_kv, r_l1, r_in1, r_l0 = [
        jnp.where(core == 0, jnp.concatenate([m_, o_], axis=0), jnp.concatenate([o_, m_], axis=0)) for m_, o_ in zip(rhalf, other)]
    gbig = {"ab_w_in": r_in0, "c_w_in": r_in1, "mem_w_kv": r_kv, "ab_w_out": r_out0, "c_w_out": r_l1[0:256],
            "xa_w_q": jnp.concatenate([r_l0[0:256], r_l1[256:512]], axis=0),
            "xa_w_o": jnp.concatenate([r_l0[256:512], r_l1[512:768]], axis=0),
            "mlp_w1": jnp.concatenate([r_l0[512:1536], r_l1[768:1792]], axis=0),
            "mlp_w2": jnp.concatenate([r_l0[1536:2560], r_l1[1792:2816]], axis=0)}

    out_g, out_d, out_m, out_v = {}, {}, {}, {}
    for n in big:
        d_, m_, v_ = _adam(A[n].reshape(shard2d[n]), gbig[n], A["m_" + n].reshape(shard2d[n]), A["v_" + n].reshape(shard2d[n]), "adam_" + n)
        out_g[n] = gbig[n].reshape(A[n].shape)
        out_d[n], out_m[n], out_v[n] = d_.reshape(A[n].shape), m_.reshape(A[n].shape), v_.reshape(A[n].shape)
    sd, sm, sv = _adam(_pack_small([A[n] for n in SMALL]), _pack_small([gs[n] for n in SMALL]),
                       _pack_small([A["m_" + n] for n in SMALL]), _pack_small([A["v_" + n] for n in SMALL]), "adam_small")
    for n, d_, m_, v_ in zip(SMALL, _unpack_small(sd, small_shapes), _unpack_small(sm, small_shapes), _unpack_small(sv, small_shapes)):
        out_g[n], out_d[n], out_m[n], out_v[n] = gs[n], d_, m_, v_
    cd, cm_, cv = _adam(ab_conv_w[0], gconv[0], m_ab_conv_w[0], v_ab_conv_w[0], "adam_conv")
    out_g["ab_conv_w"], out_d["ab_conv_w"], out_m["ab_conv_w"], out_v["ab_conv_w"] = gconv, cd[None], cm_[None], cv[None]

    order = ["norm_mix_g", "norm_xattn_g", "norm_mlp_g", "final_norm_g", "ab_w_in", "ab_conv_w", "ab_gate_b", "hgrn_lb_logits",
             "mlstm_norm_g", "hgrn_norm_g", "ab_w_out", "c_w_in", "c_fgate_b", "c_qnorm_g", "c_knorm_g", "c_w_out", "mem_norm_g",
             "mem_w_kv", "xa_w_q", "xa_w_o", "mlp_w1", "mlp_w2"]
    return (loss, dx[None], *[out_g[n] for n in order], *[out_d[n] for n in order], *[out_m[n] for n in order],
            *[out_v[n] for n in order])
```

```python
import functools

import jax
import jax.numpy as jnp
from jax import lax
from jax.experimental import pallas as pl
from jax.experimental.pallas import tpu as pltpu

F32 = jnp.float32
BF16 = jnp.bfloat16
EPS = 1e-6
D = 1024
CHUNK = 64
HD = 128
XD = 256
NEG = -1e30
VMEM_LIMIT_V7X = 56 * 1024 * 1024
ADAM_LR, ADAM_B1, ADAM_B2, ADAM_EPS, ADAM_WD, ADAM_STEP = 0.001, 0.9, 0.999, 1e-08, 0.01, 10
MESH = pl.DeviceIdType.MESH


def _pc(body, name, grid, in_specs, out_specs, out_shape, scratch=(), **kw):
    return pl.pallas_call(
        body, name=name, grid=grid, in_specs=in_specs, out_specs=out_specs, out_shape=out_shape,
        scratch_shapes=scratch,
        compiler_params=pltpu.CompilerParams(
            dimension_semantics=("arbitrary",) * len(grid), vmem_limit_bytes=VMEM_LIMIT_V7X), **kw)


def _sds(shape, dtype=F32):
    return jax.ShapeDtypeStruct(shape, dtype)


def _blk(n, target):
    return max(b for b in range(128, max(target, 128) + 1, 128) if n % b == 0)


def _dot(a, b, dims):
    return lax.dot_general(a, b, (dims, ((), ())), preferred_element_type=F32)


def _nn(a, b):
    return _dot(a, b, ((1,), (0,)))


def _nt(a, b):
    return _dot(a, b, ((1,), (1,)))


def _tn(a, b):
    return _dot(a, b, ((0,), (0,)))


def _sigmoid(x):
    return 1.0 / (1.0 + jnp.exp(-x))


def _log_sigmoid(x):
    return jnp.minimum(x, 0.0) - jnp.log(1.0 + jnp.exp(-jnp.abs(x)))


def _rstd(x):
    return lax.rsqrt(jnp.mean(x * x, axis=-1, keepdims=True) + EPS)


def _rms_bwd(du, x, g):
    r = _rstd(x)
    xh = x * r
    dxh = du * g
    dx = r * (dxh - xh * jnp.mean(dxh * xh, axis=-1, keepdims=True))
    return dx, du * xh


def _norm_mm(h, g, w, name, bm=1024, bn=512):
    t, n = h.shape[0], w.shape[1]
    bm, bn = min(bm, t), _blk(n, 3 * bn)

    def body(h_ref, g_ref, w_ref, z_ref, u_ref):
        @pl.when(pl.program_id(1) == 0)
        def _():
            x = h_ref[...]
            u_ref[...] = (x * _rstd(x) * g_ref[...]).astype(BF16)
        z_ref[...] = _nn(u_ref[...], w_ref[...])

    return _pc(body, name, (t // bm, n // bn),
               [pl.BlockSpec((bm, D), lambda i, j: (i, 0)), pl.BlockSpec((1, D), lambda i, j: (0, 0)),
                pl.BlockSpec((D, bn), lambda i, j: (0, j))],
               [pl.BlockSpec((bm, bn), lambda i, j: (i, j)), pl.BlockSpec((bm, D), lambda i, j: (i, 0))],
               [_sds((t, n)), _sds((t, D), BF16)])(h, g, w)


def _mm_tn(a, b, name, bm=1024, bn=1024, bt=2048, col_chips=None):
    t, m = a.shape
    n = b.shape[1]
    bm, bn, bt = _blk(m, bm), (n // col_chips if col_chips else _blk(n, bn + bn // 2)), min(bt, t)
    nt = t // bt

    def body(a_ref, b_ref, o_ref, acc):
        k = pl.program_id(2)

        @pl.when(k == 0)
        def _():
            acc[...] = jnp.zeros_like(acc)

        acc[...] += _tn(a_ref[...].astype(BF16), b_ref[...].astype(BF16))

        @pl.when(k == nt - 1)
        def _():
            o_ref[...] = acc[...].astype(BF16)

    if col_chips:
        out_spec, out_shape = pl.BlockSpec((None, bm, bn), lambda i, j, k: (j, i, 0)), _sds((col_chips, m, bn), BF16)
    else:
        out_spec, out_shape = pl.BlockSpec((bm, bn), lambda i, j, k: (i, j)), _sds((m, n), BF16)
    return _pc(body, name, (m // bm, n // bn, nt),
               [pl.BlockSpec((bt, bm), lambda i, j, k: (k, i)), pl.BlockSpec((bt, bn), lambda i, j, k: (k, j))],
               out_spec, out_shape, scratch=[pltpu.VMEM((bm, bn), F32)])(a, b)


def _bwd_in(dz, w, h, g, dh, name, bm=1024, bk=1024):
    t, n = dz.shape
    bm, bk = min(bm, t), _blk(n, bk + bk // 2)
    nk = n // bk

    def body(dz_ref, w_ref, h_ref, g_ref, dh_ref, o_ref, dg_ref, acc):
        i, k = pl.program_id(0), pl.program_id(1)

        @pl.when(k == 0)
        def _():
            acc[...] = jnp.zeros_like(acc)

        @pl.when((i == 0) & (k == 0))
        def _():
            dg_ref[...] = jnp.zeros_like(dg_ref)

        acc[...] += _nt(dz_ref[...], w_ref[...])

        @pl.when(k == nk - 1)
        def _():
            dx, dgr = _rms_bwd(acc[...], h_ref[...], g_ref[...])
            o_ref[...] = dh_ref[...] + dx
            dg_ref[...] += jnp.sum(dgr, axis=0, keepdims=True)

    return _pc(body, name, (t // bm, nk),
               [pl.BlockSpec((bm, bk), lambda i, k: (i, k)), pl.BlockSpec((D, bk), lambda i, k: (0, k)),
                pl.BlockSpec((bm, D), lambda i, k: (i, 0)), pl.BlockSpec((1, D), lambda i, k: (0, 0)),
                pl.BlockSpec((bm, D), lambda i, k: (i, 0))],
               [pl.BlockSpec((bm, D), lambda i, k: (i, 0)), pl.BlockSpec((1, D), lambda i, k: (0, 0))],
               [_sds((t, D)), _sds((1, D))], scratch=[pltpu.VMEM((bm, D), F32)])(dz, w, h, g, dh)


def _mlp_fwd(h, g, w1s, w2, l, name, bm=1024):
    t = h.shape[0]
    bm = min(bm, t)
    nk = w1s.shape[0]

    def body(h_ref, g_ref, w1_ref, w2_ref, o_ref, a_ref, u_ref, acc):
        k = pl.program_id(1)

        @pl.when(k == 0)
        def _():
            x = h_ref[...]
            u_ref[...] = (x * _rstd(x) * g_ref[...]).astype(BF16)
            acc[...] = jnp.zeros_like(acc)

        a = _nn(u_ref[...], w1_ref[...])
        a_ref[...] = a
        r = jnp.square(jnp.maximum(a, 0.0)).astype(BF16)
        acc[...] += _nn(r, w2_ref[...])

        @pl.when(k == nk - 1)
        def _():
            o_ref[...] = h_ref[...] + acc[...]

    return _pc(body, name, (t // bm, nk),
               [pl.BlockSpec((bm, D), lambda i, k: (i, 0)), pl.BlockSpec((1, D), lambda i, k: (0, 0)),
                pl.BlockSpec((None, None, D, D), lambda i, k: (k, l, 0, 0)), pl.BlockSpec((None, None, D, D), lambda i, k: (k, l, 0, 0))],
               [pl.BlockSpec((bm, D), lambda i, k: (i, 0)), pl.BlockSpec((bm, D), lambda i, k: (i, k)),
                pl.BlockSpec((bm, D), lambda i, k: (i, 0))],
               [_sds((t, D)), _sds((t, nk * D)), _sds((t, D), BF16)],
               scratch=[pltpu.VMEM((bm, D), F32)])(h, g, w1s, w2)


def _mlp_bwd(dh, a, w1s, w2, l, h, g, name, bm=512):
    t = h.shape[0]
    bm = min(bm, t)
    nk = w1s.shape[0]

    def body(dh_ref, a_ref, w1_ref, w2_ref, h_ref, g_ref, o_ref, da_ref, r_ref, dg_ref, acc):
        i, k = pl.program_id(0), pl.program_id(1)

        @pl.when(k == 0)
        def _():
            acc[...] = jnp.zeros_like(acc)

        @pl.when((i == 0) & (k == 0))
        def _():
            dg_ref[...] = jnp.zeros_like(dg_ref)

        ap = jnp.maximum(a_ref[...], 0.0)
        r_ref[...] = jnp.square(ap).astype(BF16)
        dr = _nt(dh_ref[...].astype(BF16), w2_ref[...])
        da = (dr * (2.0 * ap)).astype(BF16)
        da_ref[...] = da
        acc[...] += _nt(da, w1_ref[...])

        @pl.when(k == nk - 1)
        def _():
            dx, dgr = _rms_bwd(acc[...], h_ref[...], g_ref[...])
            o_ref[...] = dh_ref[...] + dx
            dg_ref[...] += jnp.sum(dgr, axis=0, keepdims=True)

    return _pc(body, name, (t // bm, nk),
               [pl.BlockSpec((bm, D), lambda i, k: (i, 0)), pl.BlockSpec((bm, D), lambda i, k: (i, k)),
                pl.BlockSpec((None, None, D, D), lambda i, k: (k, l, 0, 0)), pl.BlockSpec((None, None, D, D), lambda i, k: (k, l, 0, 0)),
                pl.BlockSpec((bm, D), lambda i, k: (i, 0)), pl.BlockSpec((1, D), lambda i, k: (0, 0))],
               [pl.BlockSpec((bm, D), lambda i, k: (i, 0)), pl.BlockSpec((bm, D), lambda i, k: (i, k)),
                pl.BlockSpec((bm, D), lambda i, k: (i, k)), pl.BlockSpec((1, D), lambda i, k: (0, 0))],
               [_sds((t, D)), _sds((t, nk * D), BF16), _sds((t, nk * D), BF16), _sds((1, D))],
               scratch=[pltpu.VMEM((bm, D), F32)])(dh, a, w1s, w2, h, g)


def _rows_of(x):
    return lax.broadcasted_iota(jnp.int32, x.shape, 0)


def _shift_down(x, s):
    if s == 0:
        return x
    return jnp.where(_rows_of(x) >= s, pltpu.roll(x, s, 0), 0.0)


def _shift_up(x, s):
    if s == 0:
        return x
    n = x.shape[0]
    return jnp.where(_rows_of(x) < n - s, pltpu.roll(x, n - s, 0), 0.0)


def _cumsum_rows(x):
    n, s = x.shape[0], 1
    while s < n:
        x = x + _shift_down(x, s)
        s *= 2
    return x


def _rcumsum_rows(x):
    n, s = x.shape[0], 1
    while s < n:
        x = x + _shift_up(x, s)
        s *= 2
    return x


def _silu(x):
    return x * _sigmoid(x)


def _dsilu(x):
    s = _sigmoid(x)
    return s * (1.0 + x * (1.0 - s))


CONV_W = 4


def _conv_pre(u, w):
    y = _shift_down(u, CONV_W - 1) * w[0:1, :]
    for j in range(1, CONV_W):
        y = y + _shift_down(u, CONV_W - 1 - j) * w[j:j + 1, :]
    return y


def _conv_fwd(z0, cw, name):
    t = z0.shape[0]

    def body(u_ref, w_ref, o_ref):
        o_ref[...] = _silu(_conv_pre(u_ref[...], w_ref[...]))

    return _pc(body, name, (2 * 512 // HD,),
               [pl.BlockSpec((t, HD), lambda c: (0, c)), pl.BlockSpec((CONV_W, HD), lambda c: (0, c))],
               pl.BlockSpec((t, HD), lambda c: (0, c)), _sds((t, 1024)))(z0, cw)


def _conv_bwd(z0, cw, dy, name):
    t = z0.shape[0]

    def body(u_ref, w_ref, dy_ref, du_ref, dw_ref):
        u, w = u_ref[...], w_ref[...]
        dpre = dy_ref[...] * _dsilu(_conv_pre(u, w))
        du = _shift_up(dpre, CONV_W - 1) * w[0:1, :]
        for j in range(1, CONV_W):
            du = du + _shift_up(dpre, CONV_W - 1 - j) * w[j:j + 1, :]
        du_ref[...] = du.astype(BF16)
        for j in range(CONV_W):
            dw_ref[j:j + 1, :] = jnp.sum(dpre * _shift_down(u, CONV_W - 1 - j), axis=0, keepdims=True)

    return _pc(body, name, (2 * 512 // HD,),
               [pl.BlockSpec((t, HD), lambda c: (0, c)), pl.BlockSpec((CONV_W, HD), lambda c: (0, c)),
                pl.BlockSpec((t, HD), lambda c: (0, c))],
               [pl.BlockSpec((t, HD), lambda c: (0, c)), pl.BlockSpec((CONV_W, HD), lambda c: (0, c))],
               [_sds((t, 1024), BF16), _sds((CONV_W, 1024))])(z0, cw, dy)


def _mlstm_gates(gate, bias, m_in):
    L = gate.shape[0]
    r = lax.broadcasted_iota(jnp.int32, (L, L), 0)
    c = lax.broadcasted_iota(jnp.int32, (L, L), 1)
    eye, tril = r == c, c <= r
    i_col = gate[:, 0:1] + bias[:, 0:1]
    f_col = gate[:, 1:2] + bias[:, 1:2]
    logf_col = _log_sigmoid(f_col)
    logf_row = jnp.sum(jnp.where(eye, logf_col, 0.0), axis=0, keepdims=True)
    i_row = jnp.sum(jnp.where(eye, i_col, 0.0), axis=0, keepdims=True)
    b_col = jnp.sum(jnp.where(tril, logf_row, 0.0), axis=1, keepdims=True)
    b_row = jnp.sum(jnp.where(r <= c, logf_col, 0.0), axis=0, keepdims=True)
    logd = jnp.where(tril, b_col - b_row + i_row, NEG)
    inter = b_col + m_in
    m_t = jnp.maximum(inter, jnp.max(logd, axis=1, keepdims=True))
    w_t = jnp.exp(inter - m_t)
    dm = jnp.exp(logd - m_t)
    b_last = b_col[L - 1:L, :]
    log_in = b_last - b_col + i_col
    m_new = jnp.maximum(b_last + m_in, jnp.max(log_in, axis=0, keepdims=True))
    w_col = jnp.exp(log_in - m_new)
    decay = jnp.exp(b_last + m_in - m_new)
    return dict(eye=eye, r=r, c=c, f_col=f_col, m_t=m_t, w_t=w_t, dm=dm, m_new=m_new, w_col=w_col, decay=decay)


def _mlstm_fwd(qk, z0, gates, bias, name):
    t = qk.shape[0]
    nc, nh, L = t // CHUNK, 4, CHUNK
    scale = HD ** -0.5

    def body(q_ref, k_ref, v_ref, g_ref, b_ref, h_ref, cs_ref, ns_ref, ms_ref, c_s, n_s, m_s):
        @pl.when(pl.program_id(0) == 0)
        def _():
            c_s[...] = jnp.zeros_like(c_s)
            n_s[...] = jnp.zeros_like(n_s)
            m_s[...] = jnp.zeros_like(m_s)

        for hd in range(nh):
            sl = slice(hd * HD, (hd + 1) * HD)
            cm, nv, m_in = c_s[hd], n_s[hd], m_s[hd]
            cs_ref[hd] = cm
            ns_ref[hd] = nv
            ms_ref[hd] = jnp.broadcast_to(m_in, (1, HD))
            q, kh, v = q_ref[:, sl], k_ref[:, sl] * scale, v_ref[:, sl]
            G = _mlstm_gates(g_ref[hd], b_ref[hd], m_in)
            qb, kb, vb = q.astype(BF16), kh.astype(BF16), v.astype(BF16)
            sc = _nt(qb, kb) * G["dm"]
            num = _nn(sc.astype(BF16), vb) + G["w_t"] * _nn(qb, cm.astype(BF16))
            den = jnp.sum(sc, axis=1, keepdims=True) + G["w_t"] * jnp.sum(q * nv, axis=1, keepdims=True)
            h_ref[:, sl] = num / jnp.maximum(jnp.abs(den), jnp.exp(-G["m_t"]))
            wk = G["w_col"] * kh
            c_s[hd] = G["decay"] * cm + _tn(wk.astype(BF16), vb)
            n_s[hd] = G["decay"] * nv + jnp.sum(wk, axis=0, keepdims=True)
            m_s[hd] = G["m_new"]

    hspec = lambda blk: pl.BlockSpec((L, 512), lambda j: (j, blk))
    st = lambda r: pl.BlockSpec((nh, None, r, HD), lambda j: (0, j, 0, 0))
    return _pc(body, name, (nc,),
               [hspec(0), hspec(1), hspec(2), pl.BlockSpec((nh, L, 2), lambda j: (0, j, 0)),
                pl.BlockSpec((nh, 1, 2), lambda j: (0, 0, 0))],
               [hspec(0), st(HD), st(1), st(1)],
               [_sds((t, 512)), _sds((nh, nc, HD, HD)), _sds((nh, nc, 1, HD)), _sds((nh, nc, 1, HD))],
               scratch=[pltpu.VMEM((nh, HD, HD), F32), pltpu.VMEM((nh, 1, HD), F32), pltpu.VMEM((nh, 1, 1), F32)])(qk, qk, z0, gates, bias)


def _mlstm_bwd(qk, z0, gates, bias, cs, ns, ms, dh, name):
    t = qk.shape[0]
    nc, nh, L = t // CHUNK, 4, CHUNK
    scale = HD ** -0.5

    def body(q_ref, k_ref, v_ref, g_ref, b_ref, cs_ref, ns_ref, ms_ref, dh_ref, dq_ref, dk_ref, dv_ref, dg_ref, dc_s, dn_s):
        @pl.when(pl.program_id(0) == 0)
        def _():
            dc_s[...] = jnp.zeros_like(dc_s)
            dn_s[...] = jnp.zeros_like(dn_s)

        for hd in range(nh):
            one_head(hd, slice(hd * HD, (hd + 1) * HD), q_ref, k_ref, v_ref, g_ref, b_ref, cs_ref, ns_ref, ms_ref, dh_ref,
                     dq_ref, dk_ref, dv_ref, dg_ref, dc_s, dn_s)

    def one_head(hd, sl, q_ref, k_ref, v_ref, g_ref, b_ref, cs_ref, ns_ref, ms_ref, dh_ref, dq_ref, dk_ref, dv_ref, dg_ref, dc_s, dn_s):
        cm, nv, m_in = cs_ref[hd], ns_ref[hd], ms_ref[hd][:, 0:1]
        q, kh, v = q_ref[:, sl], k_ref[:, sl] * scale, v_ref[:, sl]
        G = _mlstm_gates(g_ref[hd], b_ref[hd], m_in)
        w_t, dmat, w_col, decay = G["w_t"], G["dm"], G["w_col"], G["decay"]
        qb, kb, vb, cb = q.astype(BF16), kh.astype(BF16), v.astype(BF16), cm.astype(BF16)
        s = _nt(qb, kb)
        sc = s * dmat
        scb = sc.astype(BF16)
        qc = _nn(qb, cb)
        qn = jnp.sum(q * nv, axis=1, keepdims=True)
        num = _nn(scb, vb) + w_t * qc
        den = jnp.sum(sc, axis=1, keepdims=True) + w_t * qn
        e_m = jnp.exp(-G["m_t"])
        dnm = jnp.maximum(jnp.abs(den), e_m)
        dh_ = dh_ref[:, sl]
        dnum = dh_ / dnm
        dden = jnp.where(jnp.abs(den) > e_m, -jnp.sum(dh_ * num, axis=1, keepdims=True) / (dnm * dnm) * jnp.sign(den), 0.0)
        dnumb = dnum.astype(BF16)
        dsc = _nt(dnumb, vb) + dden
        dv = _tn(scb, dnumb)
        wd = w_t * dnum
        wdb = wd.astype(BF16)
        ds = dsc * dmat
        dsb = ds.astype(BF16)
        dq = _nt(wdb, cb) + (w_t * dden) * nv + _nn(dsb, kb)
        dc_o = _tn(qb, wdb)
        dn_o = jnp.sum(q * (w_t * dden), axis=0, keepdims=True)
        dw = jnp.sum(dnum * qc, axis=1, keepdims=True) + dden * qn
        dkh = _tn(dsb, qb)
        dlogd = ds * s
        db_col = jnp.sum(dlogd, axis=1, keepdims=True) + dw * w_t
        csum = jnp.sum(dlogd, axis=0, keepdims=True)
        dcn, dnn = dc_s[hd], dn_s[hd]
        dcnb = dcn.astype(BF16)
        kdc = _nn(kb, dcnb)
        dws = jnp.sum(kdc * v, axis=1, keepdims=True) + jnp.sum(kh * dnn, axis=1, keepdims=True)
        dv = dv + w_col * kdc
        dkh = dkh + w_col * (_nt(vb, dcnb) + dnn)
        dlin = dws * w_col
        ddecay = jnp.sum(jnp.sum(dcn * cm, axis=1, keepdims=True), axis=0, keepdims=True) + jnp.sum(dnn * nv, axis=1, keepdims=True)
        dlast = ddecay * decay + jnp.sum(dlin, axis=0, keepdims=True)
        rows = lax.broadcasted_iota(jnp.int32, (L, 1), 0)
        db_col = db_col - dlin + jnp.where(rows == L - 1, dlast, 0.0)
        eye, r, c = G["eye"], G["r"], G["c"]
        di = dlin + jnp.sum(jnp.where(eye, csum, 0.0), axis=1, keepdims=True)
        db_row = jnp.sum(jnp.where(eye, db_col, 0.0), axis=0, keepdims=True) - csum
        dlogf = jnp.sum(jnp.where(c >= r, db_row, 0.0), axis=1, keepdims=True)
        dg_ref[hd, :, 0:1] = di
        dg_ref[hd, :, 1:2] = dlogf * (1.0 - _sigmoid(G["f_col"]))
        dq_ref[:, sl] = dq
        dk_ref[:, sl] = dkh * scale
        dv_ref[:, sl] = dv
        dc_s[hd] = decay * dcn + dc_o
        dn_s[hd] = decay * dnn + dn_o

    rv = lambda j: nc - 1 - j
    hspec = lambda blk: pl.BlockSpec((L, 512), lambda j: (rv(j), blk))
    st = lambda r: pl.BlockSpec((nh, None, r, HD), lambda j: (0, rv(j), 0, 0))
    gs = pl.BlockSpec((nh, L, 2), lambda j: (0, rv(j), 0))
    return _pc(body, name, (nc,),
               [hspec(0), hspec(1), hspec(2), gs, pl.BlockSpec((nh, 1, 2), lambda j: (0, 0, 0)),
                st(HD), st(1), st(1), hspec(0)],
               [hspec(0), hspec(0), hspec(0), gs],
               [_sds((t, 512)), _sds((t, 512)), _sds((t, 512)), _sds((nh, t, 2))],
               scratch=[pltpu.VMEM((nh, HD, HD), F32), pltpu.VMEM((nh, 1, HD), F32)])(qk, qk, z0, gates, bias, cs, ns, ms, dh)


def _hgrn_act(qb_, fb_, ib_, lg):
    lb = _sigmoid(lg[0:1, :] - lg[1:2, :])
    sg = _sigmoid(fb_)
    f = lb + (1.0 - lb) * sg
    return lb, sg, f, _silu(qb_), (1.0 - lb) * (1.0 - sg), _silu(ib_), _cumsum_rows(jnp.log(f))


HG_SUB = 16


def _hgrn_offdiag(q, k, b, r0):
    beta = b[r0 - 1:r0, :]
    e1 = jnp.exp(b[r0:r0 + HG_SUB, :] - beta)
    e2 = jnp.where(_rows_of(b) < r0, jnp.exp(jnp.minimum(beta - b, 0.0)), 0.0)
    return q[r0:r0 + HG_SUB, :] * e1, k * e2, e1, e2


def _hgrn_fwd(z0, lbl, name):
    t = z0.shape[0]
    nc, nh, L = t // CHUNK, 4, CHUNK

    def body(q_ref, f_ref, i_ref, l_ref, o_ref, ss_ref, st_s):
        @pl.when(pl.program_id(0) == 0)
        def _():
            st_s[...] = jnp.zeros_like(st_s)

        for hd in range(nh):
            sl = slice(hd * HD, (hd + 1) * HD)
            st = st_s[hd]
            ss_ref[hd] = st
            _, _, _, q, k, v, b = _hgrn_act(q_ref[:, sl], f_ref[:, sl], i_ref[:, sl], l_ref[:, sl])
            o = _nt((q * jnp.exp(b)).astype(BF16), st.astype(BF16))
            sub = _rows_of(b) & (HG_SUB - 1)
            o = o + jnp.sum(q * k, axis=1, keepdims=True) * v
            for dl in range(1, HG_SUB):
                e = jnp.exp(jnp.where(sub >= dl, b - pltpu.roll(b, dl, 0), NEG))
                a = jnp.sum(q * pltpu.roll(k, dl, 0) * e, axis=1, keepdims=True)
                o = o + a * pltpu.roll(v, dl, 0)
            o_ref[:, sl] = o
            vb = v.astype(BF16)
            for i in range(1, L // HG_SUB):
                r0 = i * HG_SUB
                qt, kt, _, _ = _hgrn_offdiag(q, k, b, r0)
                a = _nt(qt.astype(BF16), kt.astype(BF16))
                o_ref[r0:r0 + HG_SUB, sl] += _nn(a.astype(BF16), vb)
            bl = b[L - 1:L, :]
            st_s[hd] = st * jnp.exp(bl) + _tn(v.astype(BF16), (k * jnp.exp(bl - b)).astype(BF16))

    hspec = lambda blk: pl.BlockSpec((L, 512), lambda j: (j, blk))
    return _pc(body, name, (nc,),
               [hspec(4), hspec(5), hspec(6), pl.BlockSpec((2, 512), lambda j: (0, 0))],
               [hspec(0), pl.BlockSpec((nh, None, HD, HD), lambda j: (0, j, 0, 0))],
               [_sds((t, 512)), _sds((nh, nc, HD, HD))],
               scratch=[pltpu.VMEM((nh, HD, HD), F32)])(z0, z0, z0, lbl)


def _hgrn_bwd(z0, lbl, ss, do, name):
    t = z0.shape[0]
    nc, nh, L = t // CHUNK, 4, CHUNK

    def body(q_ref, f_ref, i_ref, l_ref, ss_ref, do_ref, dq_ref, df_ref, di_ref, dl_ref, dst_s, dlb_s, dq_a, dk_a, dv_a, db_a):
        @pl.when(pl.program_id(0) == 0)
        def _():
            dst_s[...] = jnp.zeros_like(dst_s)
            dlb_s[...] = jnp.zeros_like(dlb_s)

        for hd in range(nh):
            one_head(hd, slice(hd * HD, (hd + 1) * HD), q_ref, f_ref, i_ref, l_ref, ss_ref, do_ref, dq_ref, df_ref, di_ref, dl_ref,
                     dst_s, dlb_s, dq_a.at[hd], dk_a.at[hd], dv_a.at[hd], db_a.at[hd])

    def one_head(hd, sl, q_ref, f_ref, i_ref, l_ref, ss_ref, do_ref, dq_ref, df_ref, di_ref, dl_ref, dst_s, dlb_s, dq_a, dk_a, dv_a, db_a):
        st = ss_ref[hd]
        qp, fp, ip = q_ref[:, sl], f_ref[:, sl], i_ref[:, sl]
        lb, sg, f, q, k, v, b = _hgrn_act(qp, fp, ip, l_ref[:, sl])
        do_ = do_ref[:, sl]
        dob, stb = do_.astype(BF16), st.astype(BF16)
        eb = jnp.exp(b)
        qe = q * eb
        dqe = _nn(dob, stb)
        dst_o = _tn(dob, qe.astype(BF16))
        dq = dqe * eb
        db = dqe * qe
        rows = _rows_of(b)
        sub = rows & (HG_SUB - 1)
        p0 = jnp.sum(do_ * v, axis=1, keepdims=True)
        dq = dq + p0 * k
        dk = p0 * q
        dv = jnp.sum(q * k, axis=1, keepdims=True) * do_
        for dl in range(1, HG_SUB):
            up = L - dl
            kd, vd = pltpu.roll(k, dl, 0), pltpu.roll(v, dl, 0)
            e = jnp.exp(jnp.where(sub >= dl, b - pltpu.roll(b, dl, 0), NEG))
            a = jnp.sum(q * kd * e, axis=1, keepdims=True)
            p = jnp.sum(do_ * vd, axis=1, keepdims=True) * e
            dq = dq + p * kd
            dkd = p * q
            dbb = dkd * kd
            dv = dv + pltpu.roll(a * do_, up, 0)
            dk = dk + pltpu.roll(dkd, up, 0)
            db = db + dbb - pltpu.roll(dbb, up, 0)
        dq_a[...], dk_a[...], dv_a[...], db_a[...] = dq, dk, dv, db
        vb = v.astype(BF16)
        for i in range(1, L // HG_SUB):
            r0 = i * HG_SUB
            blk = slice(r0, r0 + HG_SUB)
            qt, kt, e1, e2 = _hgrn_offdiag(q, k, b, r0)
            qtb, ktb, dob_i = qt.astype(BF16), kt.astype(BF16), do_[blk, :].astype(BF16)
            a = _nt(qtb, ktb).astype(BF16)
            da = _nt(dob_i, vb).astype(BF16)
            dv_a[...] += _tn(a, dob_i)
            dqt = _nn(da, ktb)
            dkt = _tn(da, qtb)
            dq_a[blk, :] += dqt * e1
            t1, t2 = dqt * qt, dkt * kt
            db_a[blk, :] += t1
            dk_a[...] += dkt * e2
            db_a[...] -= t2
            db_a[r0 - 1:r0, :] += jnp.sum(t2, axis=0, keepdims=True) - jnp.sum(t1, axis=0, keepdims=True)
        dq, dk, dv, db = dq_a[...], dk_a[...], dv_a[...], db_a[...]
        dstn = dst_s[hd]
        dstnb = dstn.astype(BF16)
        bl = b[L - 1:L, :]
        ebl = jnp.exp(bl)
        kdec_e = jnp.exp(bl - b)
        kdec = k * kdec_e
        dbl = jnp.sum(dstn * st, axis=0, keepdims=True) * ebl
        dv = dv + _nt(kdec.astype(BF16), dstnb)
        dkdec = _nn(v.astype(BF16), dstnb)
        dk = dk + dkdec * kdec_e
        dx = dkdec * kdec
        dbl = dbl + jnp.sum(dx, axis=0, keepdims=True)
        db = db - dx + jnp.where(rows == L - 1, dbl, 0.0)
        dst_s[hd] = dstn * ebl + dst_o
        dg = _rcumsum_rows(db)
        dfk = dg / f - dk
        dq_ref[:, sl] = (dq * _dsilu(qp)).astype(BF16)
        di_ref[:, sl] = (dv * _dsilu(ip)).astype(BF16)
        df_ref[:, sl] = (dfk * (1.0 - lb) * sg * (1.0 - sg)).astype(BF16)
        dlb_s[hd] += jnp.sum(dfk * (1.0 - sg), axis=0, keepdims=True)

        @pl.when(pl.program_id(0) == nc - 1)
        def _():
            dl0 = dlb_s[hd] * lb * (1.0 - lb)
            dl_ref[0:1, sl] = dl0
            dl_ref[1:2, sl] = -dl0

    rv = lambda j: nc - 1 - j
    hspec = lambda blk: pl.BlockSpec((L, 512), lambda j: (rv(j), blk))
    return _pc(body, name, (nc,),
               [hspec(4), hspec(5), hspec(6), pl.BlockSpec((2, 512), lambda j: (0, 0)),
                pl.BlockSpec((nh, None, HD, HD), lambda j: (0, rv(j), 0, 0)), hspec(0)],
               [hspec(0), hspec(0), hspec(0), pl.BlockSpec((2, 512), lambda j: (0, 0))],
               [_sds((t, 512), BF16), _sds((t, 512), BF16), _sds((t, 512), BF16), _sds((2, 512))],
               scratch=[pltpu.VMEM((nh, HD, HD), F32), pltpu.VMEM((nh, 1, HD), F32)] + [pltpu.VMEM((nh, L, HD), F32)] * 4)(z0, z0, z0, lbl, ss, do)


def _post0_fwd(hm, hh, z0, na, nb, w, h0, name, bm=512):
    t = h0.shape[0]
    bm = min(bm, t)

    def body(hm_ref, hh_ref, oa_ref, gb_ref, na_ref, nb_ref, w_ref, h_ref, o_ref, y_ref):
        for hd in range(4):
            sl = slice(hd * HD, (hd + 1) * HD)
            pa = _sigmoid(oa_ref[:, sl]) * hm_ref[:, sl]
            y_ref[:, sl] = (pa * _rstd(pa) * na_ref[:, sl]).astype(BF16)
            xb = hh_ref[:, sl]
            y_ref[:, 512 + hd * HD:512 + (hd + 1) * HD] = (xb * _rstd(xb) * nb_ref[:, sl] * _silu(gb_ref[:, sl])).astype(BF16)
        o_ref[...] = h_ref[...] + _nn(y_ref[...], w_ref[...])

    row = lambda wd, c: pl.BlockSpec((bm, wd), lambda i: (i, c))
    vec = lambda wd: pl.BlockSpec((1, wd), lambda i: (0, 0))
    return _pc(body, name, (t // bm,),
               [row(512, 0), row(512, 0), row(512, 3), row(512, 7), vec(512), vec(512),
                pl.BlockSpec((D, D), lambda i: (0, 0)), row(D, 0)],
               [row(D, 0), row(D, 0)], [_sds((t, D)), _sds((t, D), BF16)])(hm, hh, z0, z0, na, nb, w, h0)


def _post0_bwd(dh1, w, hm, hh, z0, na, nb, name, bm=512):
    t = dh1.shape[0]
    bm = min(bm, t)

    def body(dh_ref, w_ref, hm_ref, hh_ref, oa_ref, gb_ref, na_ref, nb_ref, dhm_ref, dhh_ref, doa_ref, dgb_ref, dna_ref, dnb_ref):
        @pl.when(pl.program_id(0) == 0)
        def _():
            dna_ref[...] = jnp.zeros_like(dna_ref)
            dnb_ref[...] = jnp.zeros_like(dnb_ref)

        dy = _nt(dh_ref[...].astype(BF16), w_ref[...])
        for hd in range(4):
            sl = slice(hd * HD, (hd + 1) * HD)
            hm_, oa = hm_ref[:, sl], oa_ref[:, sl]
            sg = _sigmoid(oa)
            dpa, dgr = _rms_bwd(dy[:, sl], sg * hm_, na_ref[:, sl])
            dna_ref[:, sl] += jnp.sum(dgr, axis=0, keepdims=True)
            doa_ref[:, sl] = (dpa * hm_ * sg * (1.0 - sg)).astype(BF16)
            dhm_ref[:, sl] = dpa * sg
            xb, gb, nbv = hh_ref[:, sl], gb_ref[:, sl], nb_ref[:, sl]
            dyb = dy[:, 512 + hd * HD:512 + (hd + 1) * HD]
            dgb_ref[:, sl] = (dyb * (xb * _rstd(xb) * nbv) * _dsilu(gb)).astype(BF16)
            dxb, dgr2 = _rms_bwd(dyb * _silu(gb), xb, nbv)
            dnb_ref[:, sl] += jnp.sum(dgr2, axis=0, keepdims=True)
            dhh_ref[:, sl] = dxb

    row = lambda wd, c: pl.BlockSpec((bm, wd), lambda i: (i, c))
    vec = lambda wd: pl.BlockSpec((1, wd), lambda i: (0, 0))
    return _pc(body, name, (t // bm,),
               [row(D, 0), pl.BlockSpec((D, D), lambda i: (0, 0)), row(512, 0), row(512, 0), row(512, 3), row(512, 7),
                vec(512), vec(512)],
               [row(512, 0), row(512, 0), row(512, 0), row(512, 0), vec(512), vec(512)],
               [_sds((t, 512)), _sds((t, 512)), _sds((t, 512), BF16), _sds((t, 512), BF16), _sds((1, 512)), _sds((1, 512))],
               )(dh1, w, hm, hh, z0, z0, na, nb)


def _memkv_fwd(mem, g, wkv_s, name):
    m = mem.shape[0]

    def body(x_ref, g_ref, w_ref, kv_ref, mn_ref):
        x = x_ref[...]
        mn = (x * _rstd(x) * g_ref[...]).astype(BF16)
        mn_ref[...] = mn
        kv_ref[...] = _nn(mn, w_ref[...])

    return _pc(body, name, (4,),
               [pl.BlockSpec((m, D), lambda k: (0, 0)), pl.BlockSpec((1, D), lambda k: (0, 0)),
                pl.BlockSpec((None, D, 512), lambda k: (k, 0, 0))],
               [pl.BlockSpec((m, 512), lambda k: (0, k)), pl.BlockSpec((m, D), lambda k: (0, 0))],
               [_sds((m, 2048)), _sds((m, D), BF16)])(mem, g, wkv_s)


def _memkv_bwd(dkv, wkv_s, mem, g, name):
    m = mem.shape[0]

    def body(d_ref, w_ref, x_ref, g_ref, dg_ref, acc):
        k = pl.program_id(0)

        @pl.when(k == 0)
        def _():
            acc[...] = jnp.zeros_like(acc)

        acc[...] += _nt(d_ref[...].astype(BF16), w_ref[...])

        @pl.when(k == 3)
        def _():
            _, dgr = _rms_bwd(acc[...], x_ref[...], g_ref[...])
            dg_ref[...] = jnp.sum(dgr, axis=0, keepdims=True)

    return _pc(body, name, (4,),
               [pl.BlockSpec((m, 512), lambda k: (0, k)), pl.BlockSpec((None, D, 512), lambda k: (k, 0, 0)),
                pl.BlockSpec((m, D), lambda k: (0, 0)), pl.BlockSpec((1, D), lambda k: (0, 0))],
               pl.BlockSpec((1, D), lambda k: (0, 0)), _sds((1, D)), scratch=[pltpu.VMEM((m, D), F32)])(dkv, wkv_s, mem, g)


def _xattn_probs(qh, kh):
    s = _nt(qh, kh) * (XD ** -0.5)
    p = jnp.exp(s - jnp.max(s, axis=1, keepdims=True))
    return p / jnp.sum(p, axis=1, keepdims=True)


def _xattn_fwd(q, kv, wo, h1, name, bm=512):
    t, m = q.shape[0], kv.shape[0]
    bm = min(bm, t)

    def body(q_ref, k_ref, v_ref, w_ref, h_ref, out_ref, o_ref):
        for hd in range(D // XD):
            sl = slice(hd * XD, (hd + 1) * XD)
            p = _xattn_probs(q_ref[:, sl].astype(BF16), k_ref[:, sl].astype(BF16))
            o_ref[:, sl] = _nn(p.astype(BF16), v_ref[:, sl].astype(BF16)).astype(BF16)
        out_ref[...] = h_ref[...] + _nn(o_ref[...], w_ref[...])

    row = pl.BlockSpec((bm, D), lambda i: (i, 0))
    return _pc(body, name, (t // bm,),
               [row, pl.BlockSpec((m, D), lambda i: (0, 0)), pl.BlockSpec((m, D), lambda i: (0, 1)),
                pl.BlockSpec((D, D), lambda i: (0, 0)), row],
               [row, row], [_sds((t, D)), _sds((t, D), BF16)])(q, kv, kv, wo, h1)


def _xattn_bwd(dh2, q, kv, wo, name, bm=512):
    t, m = q.shape[0], kv.shape[0]
    bm = min(bm, t)

    def body(dh_ref, q_ref, k_ref, v_ref, w_ref, dq_ref, dkv_ref):
        @pl.when(pl.program_id(0) == 0)
        def _():
            dkv_ref[...] = jnp.zeros_like(dkv_ref)

        d_o = _nt(dh_ref[...].astype(BF16), w_ref[...])
        for hd in range(D // XD):
            sl = slice(hd * XD, (hd + 1) * XD)
            qh, kh, vh = q_ref[:, sl].astype(BF16), k_ref[:, sl].astype(BF16), v_ref[:, sl].astype(BF16)
            p = _xattn_probs(qh, kh)
            dob = d_o[:, sl].astype(BF16)
            dp = _nt(dob, vh)
            dkv_ref[:, D + hd * XD:D + (hd + 1) * XD] += _tn(p.astype(BF16), dob)
            ds = (p * (dp - jnp.sum(dp * p, axis=1, keepdims=True)) * (XD ** -0.5)).astype(BF16)
            dq_ref[:, sl] = _nn(ds, kh).astype(BF16)
            dkv_ref[:, sl] += _tn(ds, qh)

    row = pl.BlockSpec((bm, D), lambda i: (i, 0))
    return _pc(body, name, (t // bm,),
               [row, row, pl.BlockSpec((m, D), lambda i: (0, 0)), pl.BlockSpec((m, D), lambda i: (0, 1)),
                pl.BlockSpec((D, D), lambda i: (0, 0))],
               [row, pl.BlockSpec((m, 2 * D), lambda i: (0, 0))],
               [_sds((t, D), BF16), _sds((m, 2 * D))])(dh2, q, kv, kv, wo)


NH1 = 8
FOX_BM = 512
FOX_BQ = 512
FOX_BK = 512
FOX_HEADS_PER_STEP = 2


def _foxprep_fwd(z1, qg, kg, fbp, name):
    t = z1.shape[0]
    bm = min(FOX_BM, t)

    def body(q_ref, k_ref, v_ref, f_ref, qg_ref, kg_ref, fb_ref, qn_ref, kn_ref, vb_ref, c_ref, carry):
        @pl.when(pl.program_id(0) == 0)
        def _():
            carry[...] = jnp.zeros_like(carry)

        for hd in range(NH1):
            sl = slice(hd * HD, (hd + 1) * HD)
            x = q_ref[:, sl]
            qn_ref[:, sl] = (x * _rstd(x) * qg_ref[...] * FOX_QSCALE).astype(BF16)
            x = k_ref[:, sl]
            kn_ref[:, sl] = (x * _rstd(x) * kg_ref[...]).astype(BF16)
        vb_ref[...] = v_ref[...].astype(BF16)
        c = carry[...] + _cumsum_rows(_log_sigmoid(f_ref[...] + fb_ref[...]))
        c_ref[...] = c
        carry[...] = c[bm - 1:bm, :]

    row = lambda c: pl.BlockSpec((bm, D), lambda i: (i, c))
    lane = pl.BlockSpec((bm, HD), lambda i: (i, 4 * D // HD))
    vec = pl.BlockSpec((1, HD), lambda i: (0, 0))
    return _pc(body, name, (t // bm,), [row(0), row(1), row(2), lane, vec, vec, vec],
               [row(0), row(0), row(0), pl.BlockSpec((bm, HD), lambda i: (i, 0))],
               [_sds((t, D), BF16), _sds((t, D), BF16), _sds((t, D), BF16), _sds((t, HD))],
               scratch=[pltpu.VMEM((1, HD), F32)])(z1, z1, z1, z1, qg, kg, fbp)


def _foxprep_bwd(dqn, dkn, z1, qg, kg, fbp, dc, name):
    t = z1.shape[0]
    bm = min(FOX_BM, t)
    nb = t // bm

    def body(dqn_ref, dkn_ref, q_ref, k_ref, f_ref, qg_ref, kg_ref, fb_ref, dc_ref,
             dq_ref, dk_ref, df_ref, dqg_ref, dkg_ref, dfb_ref, carry):
        @pl.when(pl.program_id(0) == 0)
        def _():
            carry[...] = jnp.zeros_like(carry)
            dqg_ref[...] = jnp.zeros_like(dqg_ref)
            dkg_ref[...] = jnp.zeros_like(dkg_ref)
            dfb_ref[...] = jnp.zeros_like(dfb_ref)

        for hd in range(NH1):
            sl = slice(hd * HD, (hd + 1) * HD)
            dx, dgr = _rms_bwd(dqn_ref[:, sl] * (HD ** -0.5), q_ref[:, sl], qg_ref[...])
            dq_ref[:, sl] = dx.astype(BF16)
            dqg_ref[...] += jnp.sum(dgr, axis=0, keepdims=True)
            dx, dgr = _rms_bwd(dkn_ref[:, sl], k_ref[:, sl], kg_ref[...])
            dk_ref[:, sl] = dx.astype(BF16)
            dkg_ref[...] += jnp.sum(dgr, axis=0, keepdims=True)
        dc_ = dc_ref[...]
        dlogf = _rcumsum_rows(dc_) + carry[...]
        carry[...] += jnp.sum(dc_, axis=0, keepdims=True)
        lanes = lax.broadcasted_iota(jnp.int32, dc_.shape, 1)
        df = jnp.where(lanes < NH1, dlogf * (1.0 - _sigmoid(f_ref[...] + fb_ref[...])), 0.0)
        df_ref[...] = df.astype(BF16)
        dfb_ref[...] += jnp.sum(df, axis=0, keepdims=True)

    rv = lambda i: nb - 1 - i
    row = lambda c: pl.BlockSpec((bm, D), lambda i: (rv(i), c))
    lane = lambda c: pl.BlockSpec((bm, HD), lambda i: (rv(i), c))
    vec = pl.BlockSpec((1, HD), lambda i: (0, 0))
    return _pc(body, name, (nb,), [row(0), row(0), row(0), row(1), lane(4 * D // HD), vec, vec, vec, lane(0)],
               [row(0), row(0), lane(0), vec, vec, vec],
               [_sds((t, D), BF16), _sds((t, D), BF16), _sds((t, HD), BF16), _sds((1, HD)), _sds((1, HD)), _sds((1, HD))],
               scratch=[pltpu.VMEM((1, HD), F32)])(dqn, dkn, z1, z1, z1, qg, kg, fbp, dc)


LOG2E = 1.4426950408889634
FOX_QSCALE = HD ** -0.5 * LOG2E


def _fox_block_kind(i, j, bq, bk):
    active = j * bk < (i + 1) * bq
    full = (j + 1) * bk <= i * bq + 1
    return full, active & jnp.logical_not(full)


def _fox_lane_tiles(x):
    return [x[:, c0:c0 + HD] for c0 in range(0, x.shape[1], HD)]


def _fox_masked_scores(q, k, ck, i, j, bq, bk, masked):
    s = _nt(q, k) - ck
    if masked:
        rows = i * bq + lax.broadcasted_iota(jnp.int32, s.shape, 0)
        cols = j * bk + lax.broadcasted_iota(jnp.int32, s.shape, 1)
        s = jnp.where(cols <= rows, s, NEG)
    return s


def _fox_specs(t, bq, bk, G):
    kj = lambda i, j: jnp.minimum(j, ((i + 1) * bq - 1) // bk)
    qspec = pl.BlockSpec((bq, G * HD), lambda h, i, j: (i, h))
    kspec = pl.BlockSpec((bk, G * HD), lambda h, i, j: (kj(i, j), h))
    cspec = pl.BlockSpec((G, 1, bk), lambda h, i, j: (h, 0, kj(i, j)))
    colspec = pl.BlockSpec((G, bq, 1), lambda h, i, j: (h, i, 0))
    return qspec, kspec, cspec, colspec


def _fox_rowmax(qn, kn, crow, name):
    t = qn.shape[0]
    bq, bk, G = min(FOX_BQ, t), min(FOX_BK, t), FOX_HEADS_PER_STEP
    nq, nk = t // bq, t // bk

    def body(q_ref, k_ref, ck_ref, m_ref, *mp):
        i, j = pl.program_id(1), pl.program_id(2)

        @pl.when(j == 0)
        def _():
            for g in range(G):
                mp[g][...] = jnp.full_like(mp[g], NEG)

        def step(masked):
            for g in range(G):
                sl = slice(g * HD, (g + 1) * HD)
                s = _fox_masked_scores(q_ref[:, sl], k_ref[:, sl], ck_ref[g], i, j, bq, bk, masked)
                m = mp[g][...]
                for tile in _fox_lane_tiles(s):
                    m = jnp.maximum(m, tile)
                mp[g][...] = m

        full, part = _fox_block_kind(i, j, bq, bk)
        pl.when(full)(lambda: step(False))
        pl.when(part)(lambda: step(True))

        @pl.when(j == nk - 1)
        def _():
            for g in range(G):
                m_ref[g] = jnp.max(mp[g][...], axis=1, keepdims=True)

    qspec, kspec, cspec, colspec = _fox_specs(t, bq, bk, G)
    return _pc(body, name, (NH1 // G, nq, nk), [qspec, kspec, cspec], colspec, _sds((NH1, t, 1)),
               scratch=[pltpu.VMEM((bq, HD), F32)] * G)(qn, kn, crow)


def _fox_fwd(qn, kn, vb, crow, m, name):
    t = qn.shape[0]
    bq, bk, G = min(FOX_BQ, t), min(FOX_BK, t), FOX_HEADS_PER_STEP
    nq, nk = t // bq, t // bk

    def body(q_ref, k_ref, v_ref, ck_ref, m_ref, o_ref, lse_ref, *scr):
        i, j = pl.program_id(1), pl.program_id(2)
        lp, acc = scr[:G], scr[G:]

        @pl.when(j == 0)
        def _():
            for g in range(G):
                lp[g][...] = jnp.zeros_like(lp[g])
                acc[g][...] = jnp.zeros_like(acc[g])

        def step(masked):
            for g in range(G):
                sl = slice(g * HD, (g + 1) * HD)
                s = _fox_masked_scores(q_ref[:, sl], k_ref[:, sl], ck_ref[g], i, j, bq, bk, masked)
                p = jnp.exp2(s - m_ref[g])
                l = lp[g][...]
                for tile in _fox_lane_tiles(p):
                    l = l + tile
                lp[g][...] = l
                acc[g][...] += _nn(p.astype(BF16), v_ref[:, sl])

        full, part = _fox_block_kind(i, j, bq, bk)
        pl.when(full)(lambda: step(False))
        pl.when(part)(lambda: step(True))

        @pl.when(j == nk - 1)
        def _():
            for g in range(G):
                l = jnp.sum(lp[g][...], axis=1, keepdims=True)
                o_ref[:, g * HD:(g + 1) * HD] = acc[g][...] / l
                lse_ref[g] = m_ref[g] + jnp.log2(l)

    qspec, kspec, cspec, colspec = _fox_specs(t, bq, bk, G)
    return _pc(body, name, (NH1 // G, nq, nk), [qspec, kspec, kspec, cspec, colspec], [qspec, colspec],
               [_sds((t, D)), _sds((NH1, t, 1))], scratch=[pltpu.VMEM((bq, HD), F32)] * (2 * G))(qn, kn, vb, crow, m)


def _fox_bwd(qn, kn, vb, crow, lse, delta, do, name):
    t = qn.shape[0]
    bq, bk, G = min(FOX_BQ, t), min(FOX_BK, t), FOX_HEADS_PER_STEP
    nq, nk = t // bq, t // bk

    def body(q_ref, k_ref, v_ref, ck_ref, lse_ref, dl_ref, do_ref, dq_ref, dk_ref, dv_ref, dc_ref, dcq_ref, dk_s, dv_s, dc_s):
        j, i = pl.program_id(1), pl.program_id(2)

        @pl.when(i == 0)
        def _():
            dk_s[...] = jnp.zeros_like(dk_s)
            dv_s[...] = jnp.zeros_like(dv_s)
            dc_s[...] = jnp.zeros_like(dc_s)

        @pl.when((i == 0) & (j == 0))
        def _():
            dq_ref[...] = jnp.zeros_like(dq_ref)
            dcq_ref[...] = jnp.zeros_like(dcq_ref)

        def step(masked):
            rows = pl.ds(pl.multiple_of(i * bq, bq), bq)
            for g in range(G):
                sl = slice(g * HD, (g + 1) * HD)
                q, k = q_ref[:, sl], k_ref[:, sl]
                s = _fox_masked_scores(q, k, ck_ref[g], i, j, bq, bk, masked)
                p = jnp.exp2(s - lse_ref[g])
                dob = do_ref[:, sl]
                dv_s[:, sl] += _tn(p.astype(BF16), dob)
                ds = p * (_nt(dob, v_ref[:, sl]) - dl_ref[g])
                dsb = ds.astype(BF16)
                dq_ref[rows, sl] += _nn(dsb, k)
                dk_s[:, sl] += _tn(dsb, q)
                dc_s[g] -= jnp.sum(ds, axis=0, keepdims=True)
                part_sum = dcq_ref[g, rows, :]
                for tile in _fox_lane_tiles(ds):
                    part_sum = part_sum + tile
                dcq_ref[g, rows, :] = part_sum

        full, part = _fox_block_kind(i, j, bq, bk)
        pl.when(full)(lambda: step(False))
        pl.when(part)(lambda: step(True))

        @pl.when(i == nq - 1)
        def _():
            dk_ref[...] = dk_s[...] * (1.0 / LOG2E)
            dv_ref[...] = dv_s[...]
            dc_ref[...] = dc_s[...]

    qi = lambda i, j: jnp.maximum(i, (j * bk) // bq)
    qmap = lambda h, j, i: (qi(i, j), h)
    c3map = lambda h, j, i: (h, qi(i, j), 0)
    kspec = pl.BlockSpec((bk, G * HD), lambda h, j, i: (j, h))
    return _pc(body, name, (NH1 // G, nk, nq),
               [pl.BlockSpec((bq, G * HD), qmap), kspec, kspec,
                pl.BlockSpec((G, 1, bk), lambda h, j, i: (h, 0, j)), pl.BlockSpec((G, bq, 1), c3map),
                pl.BlockSpec((G, bq, 1), c3map), pl.BlockSpec((bq, G * HD), qmap)],
               [pl.BlockSpec((t, G * HD), lambda h, j, i: (0, h)), kspec, kspec, pl.BlockSpec((G, 1, bk), lambda h, j, i: (h, 0, j)),
                pl.BlockSpec((G, t, HD), lambda h, j, i: (h, 0, 0))],
               [_sds((t, D)), _sds((t, D)), _sds((t, D)), _sds((NH1, 1, t)), _sds((NH1, t, HD))],
               scratch=[pltpu.VMEM((bk, G * HD), F32), pltpu.VMEM((bk, G * HD), F32), pltpu.VMEM((G, 1, bk), F32)],
               )(qn, kn, vb, crow, lse, delta, do)


def _post1_fwd(o, z1, w, h3, name, bm=512):
    t = o.shape[0]
    bm = min(bm, t)

    def body(o_ref, g_ref, w_ref, h_ref, out_ref, og_ref):
        og_ref[...] = (o_ref[...] * _sigmoid(g_ref[...])).astype(BF16)
        out_ref[...] = h_ref[...] + _nn(og_ref[...], w_ref[...])

    row = lambda c: pl.BlockSpec((bm, D), lambda i: (i, c))
    return _pc(body, name, (t // bm,), [row(0), row(3), pl.BlockSpec((D, D), lambda i: (0, 0)), row(0)],
               [row(0), row(0)], [_sds((t, D)), _sds((t, D), BF16)])(o, z1, w, h3)


def _post1_bwd(dh4, w, o, z1, name, bm=512):
    t = o.shape[0]
    bm = min(bm, t)

    def body(dh_ref, w_ref, o_ref, g_ref, do_ref, dg_ref, dl_ref):
        d_og = _nt(dh_ref[...].astype(BF16), w_ref[...])
        o_, sg = o_ref[...], _sigmoid(g_ref[...])
        dob = (d_og * sg).astype(BF16)
        do_ref[...] = dob
        dg_ref[...] = (d_og * o_ * sg * (1.0 - sg)).astype(BF16)
        prod = dob.astype(F32) * o_
        for hd in range(NH1):
            dl_ref[hd] = jnp.sum(prod[:, hd * HD:(hd + 1) * HD], axis=1, keepdims=True)

    row = lambda c: pl.BlockSpec((bm, D), lambda i: (i, c))
    return _pc(body, name, (t // bm,), [row(0), pl.BlockSpec((D, D), lambda i: (0, 0)), row(0), row(3)],
               [row(0), row(0), pl.BlockSpec((NH1, bm, 1), lambda i: (0, i, 0))],
               [_sds((t, D), BF16), _sds((t, D), BF16), _sds((NH1, t, 1))])(dh4, w, o, z1)


def _final(h, g, tgt, name, bm=512):
    t = h.shape[0]
    bm = min(bm, t)

    def body(h_ref, g_ref, t_ref, l_ref, dh_ref, dg_ref):
        @pl.when(pl.program_id(0) == 0)
        def _():
            l_ref[...] = jnp.zeros_like(l_ref)
            dg_ref[...] = jnp.zeros_like(dg_ref)

        x, gv = h_ref[...], g_ref[...]
        r = _rstd(x)
        xh = x * r
        e = xh * gv - t_ref[...]
        l_ref[...] += 0.5 * jnp.sum(jnp.mean(e * e, axis=1, keepdims=True), axis=0, keepdims=True)
        dy = e * (1.0 / D)
        dg_ref[...] += jnp.sum(dy * xh, axis=0, keepdims=True)
        dxh = dy * gv
        dh_ref[...] = r * (dxh - xh * jnp.mean(dxh * xh, axis=1, keepdims=True))

    row = pl.BlockSpec((bm, D), lambda i: (i, 0))
    vec = pl.BlockSpec((1, D), lambda i: (0, 0))
    return _pc(body, name, (t // bm,), [row, vec, row], [pl.BlockSpec((1, HD), lambda i: (0, 0)), row, vec],
               [_sds((1, HD)), _sds((t, D)), _sds((1, D))])(h, g, tgt)


def _adam(w, g, m, v, name):
    r, c = w.shape
    br = min(r, 256)

    def body(w_ref, g_ref, m_ref, v_ref, d_ref, mo_ref, vo_ref):
        gv = g_ref[...]
        mn = ADAM_B1 * m_ref[...] + (1.0 - ADAM_B1) * gv
        vn = ADAM_B2 * v_ref[...] + (1.0 - ADAM_B2) * jnp.square(gv)
        m_hat = mn / (1.0 - ADAM_B1 ** ADAM_STEP)
        v_hat = vn / (1.0 - ADAM_B2 ** ADAM_STEP)
        d_ref[...] = -ADAM_LR * (m_hat / (jnp.sqrt(v_hat) + ADAM_EPS) + ADAM_WD * w_ref[...])
        mo_ref[...] = mn
        vo_ref[...] = vn

    blk = pl.BlockSpec((br, c), lambda i: (i, 0))
    return _pc(body, name, (r // br,), [blk] * 4, [blk] * 3, [_sds((r, c))] * 3)(w, g, m, v)


ZW = 4224
GATE0 = 4096


def _pack_w_in0(w):
    return jnp.concatenate([w[:, :2048], w[:, 2056:], w[:, 2048:2056], jnp.zeros((w.shape[0], ZW - 4104), w.dtype)], axis=1)


def _unpack_w_in0(g):
    return jnp.concatenate([g[:, :2048], g[:, GATE0:GATE0 + 8], g[:, 2048:GATE0]], axis=1)


def _pack_w_in1(w):
    return jnp.concatenate([w, jnp.zeros((w.shape[0], ZW - 4104), w.dtype)], axis=1)


def _unpack_w_in1(g):
    return g[:, :4104]


def _local_step(x, mem, tgt, W, S, late_weights=None, grads_hook=None):
    t = x.shape[0]
    row = lambda v: v.reshape(1, -1)
    G = {}

    z0, u0 = _norm_mm(x, S["norm_mix_g"][0:1], W["w_in0"], "in0_fwd")
    qk = _conv_fwd(z0, S["conv_w"], "conv_fwd")
    g8 = z0[:, GATE0:GATE0 + 8]
    gates3 = jnp.stack([g8[:, :4].T, g8[:, 4:].T], axis=-1)
    gb = S["gate_b"]
    bias3 = jnp.stack([gb[0, :4], gb[0, 4:]], axis=-1)[:, None, :]
    hm, cs, ns, ms = _mlstm_fwd(qk, z0, gates3, bias3, "mlstm_fwd")
    hh, ss = _hgrn_fwd(z0, S["lb_logits"], "hgrn_fwd")
    if late_weights is not None:
        W = {**W, **late_weights(hh)}
    kv, mn = _memkv_fwd(mem, row(S["mem_norm_g"]), W["wkv_s"], "memkv_fwd")
    h1, y0 = _post0_fwd(hm, hh, z0, S["mlstm_norm_g"], S["hgrn_norm_g"], W["w_out0"], x, "post0_fwd")

    def xattn_mlp_fwd(h, l):
        q, ux = _norm_mm(h, S["norm_xattn_g"][l:l + 1], W["wq"][l], f"xq{l}_fwd")
        h2, ox = _xattn_fwd(q, kv, W["wo"][l], h, f"xattn{l}_fwd")
        h3, a, um = _mlp_fwd(h2, S["norm_mlp_g"][l:l + 1], W["w1s"], W["w2"], l, f"mlp{l}_fwd")
        return h3, (h, q, ux, ox, h2, a, um)

    h3, sv0 = xattn_mlp_fwd(h1, 0)
    z1, u1 = _norm_mm(h3, S["norm_mix_g"][1:2], W["w_in1"], "in1_fwd")
    fbp = jnp.pad(S["c_fgate_b"], ((0, 0), (0, HD - NH1)))
    qn, kn, vb, c = _foxprep_fwd(z1, S["c_qnorm_g"], S["c_knorm_g"], fbp, "foxprep_fwd")
    crow = (c[:, :NH1] * LOG2E).T[:, None, :]
    o1, lse = _fox_fwd(qn, kn, vb, crow, _fox_rowmax(qn, kn, crow, "fox_rowmax"), "fox_fwd")
    h4, og = _post1_fwd(o1, z1, W["w_out1"], h3, "post1_fwd")
    h6, sv1 = xattn_mlp_fwd(h4, 1)
    lossp, dh, G["final_norm_g"] = _final(h6, row(S["final_norm_g"]), tgt, "final")

    grads_ready = grads_hook if grads_hook is not None else (lambda stage, grads: 0.0)
    dkv = None
    dgx, dgm, dwq, dwo, dw1, dw2 = [None, None], [None, None], [None, None], [None, None], [None, None], [None, None]

    def xattn_mlp_bwd(dh, l, sv):
        nonlocal dkv
        h, q, ux, ox, h2, a, um = sv
        dh2, da, r, dgm[l] = _mlp_bwd(dh, a, W["w1s"], W["w2"], l, h2, S["norm_mlp_g"][l:l + 1], f"mlp{l}_bwd")
        dw1[l] = _mm_tn(um, da, f"mlp{l}_dw1", col_chips=NCHIP)
        dw2[l] = _mm_tn(r, dh, f"mlp{l}_dw2")
        dq, dkv_l = _xattn_bwd(dh2, q, kv, W["wo"][l], f"xattn{l}_bwd")
        dkv = dkv_l if dkv is None else dkv + dkv_l
        dwo[l] = _mm_tn(ox, dh2, f"xattn{l}_dwo")
        dwq[l] = _mm_tn(ux, dq, f"xattn{l}_dwq")
        tok = grads_ready("layer0_mlp_xattn", dict(wq=dwq[0], wo=dwo[0], w1=dw1[0], w2=dw2[0])) if l == 0 else 0.0
        dh1, dgx[l] = _bwd_in(dq, W["wq"][l], h, S["norm_xattn_g"][l:l + 1] + tok, dh2, f"xq{l}_bwd")
        return dh1

    dh4 = xattn_mlp_bwd(dh, 1, sv1)
    do, dgate, delta = _post1_bwd(dh4, W["w_out1"], o1, z1, "post1_bwd")
    G["w_out1"] = _mm_tn(og, dh4, "post1_dw")
    dqn, dkn, dv1, dcrow, dcq = _fox_bwd(qn, kn, vb, crow, lse, delta, do, "fox_bwd")
    dc = jnp.pad((dcrow[:, 0, :] + jnp.sum(dcq, axis=-1)).T, ((0, 0), (0, HD - NH1)))
    dqr, dkr, df1, G["c_qnorm_g"], G["c_knorm_g"], dfb = _foxprep_bwd(
        dqn, dkn, z1, S["c_qnorm_g"], S["c_knorm_g"], fbp, dc, "foxprep_bwd")
    G["c_fgate_b"] = dfb[:, :NH1]
    dz1 = jnp.concatenate([dqr, dkr, dv1.astype(BF16), dgate, df1], axis=1)
    G["w_in1"] = _mm_tn(u1, dz1, "in1_dw")
    tok = grads_ready("layer1", dict(w_out=G["w_out1"], w_in=G["w_in1"], wq=dwq[1], wo=dwo[1], w1=dw1[1], w2=dw2[1]))
    dh3, dgmix1 = _bwd_in(dz1, W["w_in1"], h3, S["norm_mix_g"][1:2] + tok, dh4, "in1_bwd")
    dh1 = xattn_mlp_bwd(dh3, 0, sv0)

    dhm, dhh, doa, dgb, G["mlstm_norm_g"], G["hgrn_norm_g"] = _post0_bwd(
        dh1, W["w_out0"], hm, hh, z0, S["mlstm_norm_g"], S["hgrn_norm_g"], "post0_bwd")
    G["w_out0"] = _mm_tn(y0, dh1, "post0_dw")
    dqa, dka, dva, dgates3 = _mlstm_bwd(qk, z0, gates3, bias3, cs, ns, ms, dhm, "mlstm_bwd")
    dqb, dfb0, dib, G["lb_logits"] = _hgrn_bwd(z0, S["lb_logits"], ss, dhh, "hgrn_bwd")
    duc, G["conv_w"] = _conv_bwd(z0, S["conv_w"], jnp.concatenate([dqa, dka], axis=1), "conv_bwd")
    dg8 = jnp.concatenate([dgates3[:, :, 0].T, dgates3[:, :, 1].T], axis=1)
    G["gate_b"] = jnp.sum(dg8, axis=0, keepdims=True)
    dz0 = jnp.concatenate([duc, dva.astype(BF16), doa, dqb, dfb0, dib, dgb,
                           jnp.pad(dg8, ((0, 0), (0, HD - 8))).astype(BF16)], axis=1)
    G["w_in0"] = _mm_tn(u0, dz0, "in0_dw")
    dx, dgmix0 = _bwd_in(dz0, W["w_in0"], x, S["norm_mix_g"][0:1], dh1, "in0_bwd")

    G["wkv"] = _mm_tn(mn, dkv, "memkv_dw", col_chips=NCHIP)
    G["mem_norm_g"] = _memkv_bwd(dkv, W["wkv_s"], mem, row(S["mem_norm_g"]), "memkv_bwd")
    G["norm_mix_g"] = jnp.concatenate([dgmix0, dgmix1], axis=0)
    G["norm_xattn_g"] = jnp.concatenate(dgx, axis=0)
    G["norm_mlp_g"] = jnp.concatenate(dgm, axis=0)
    G["wq"], G["wo"], G["w1"], G["w2"] = dwq, dwo, dw1, dw2
    return lossp[0, 0], dx, G


ANY = pl.BlockSpec(memory_space=pl.ANY)
NCHIP = 4


def _place():
    x, y, c = lax.axis_index("x"), lax.axis_index("y"), lax.axis_index("c")
    return x, y, c, [(1 - x, y), (x, 1 - y), (1 - x, 1 - y)]


def _comm_call(body, name, ins, out_shapes, sems):
    return pl.pallas_call(body, name=name, in_specs=[ANY] * len(ins), out_specs=[ANY] * len(out_shapes),
                          out_shape=out_shapes, scratch_shapes=sems)(*ins)


def _gather_weights(arrs, name):
    n = len(arrs)

    def body(*refs):
        ins, outs = refs[:n], refs[n:2 * n]
        send_i, recv_i, send_d, recv_d = refs[2 * n:]
        x, y, c, chips = _place()
        me = 2 * x + y

        def half(a, cc):
            h = arrs[a].shape[0] // 2
            return pl.ds(pl.multiple_of(cc * h, h), h)

        def ici(a, k, src_chip, dst_dev):
            return pltpu.make_async_remote_copy(
                src_ref=ins[a].at[half(a, c)], dst_ref=outs[a].at[src_chip, half(a, c)], send_sem=send_i.at[a, k],
                recv_sem=recv_i.at[a, k], device_id=dst_dev, device_id_type=MESH)

        def d2d(a, k, src_chip, cc):
            reg = outs[a].at[src_chip, half(a, cc)]
            return pltpu.make_async_remote_copy(src_ref=reg, dst_ref=reg, send_sem=send_d.at[a, k], recv_sem=recv_d.at[a, k],
                                                device_id=(x, y, 1 - c), device_id_type=MESH)

        for a in range(n):
            for k, (px, py) in enumerate(chips):
                ici(a, k, me, (px, py, c)).start()
        for k, (px, py) in enumerate(chips):
            for a in range(n):
                ici(a, k, 2 * px + py, (px, py, c)).wait_recv()
                d2d(a, k, 2 * px + py, c).start()
        for k, (px, py) in enumerate(chips):
            for a in range(n):
                ici(a, k, me, (px, py, c)).wait_send()
                d2d(a, k, 2 * px + py, c).wait_send()
                d2d(a, k, 2 * px + py, 1 - c).wait_recv()

    sem = lambda: pltpu.SemaphoreType.DMA((n, 3))
    return _comm_call(body, name, arrs, [_sds((NCHIP,) + a.shape, a.dtype) for a in arrs], [sem(), sem(), sem(), sem()])


HBM = pl.BlockSpec(memory_space=pltpu.HBM)
SEM = pl.BlockSpec(memory_space=pltpu.SEMAPHORE)
DATAFLOW = pltpu.SideEffectType.DATAFLOW_SIDE_EFFECTING


def _half_rows(r, cc):
    return pl.ds(pl.multiple_of(cc * (r // 2), r // 2), r // 2)


def _gather_start(arrs, after, name):
    n = len(arrs)

    def body(*refs):
        ins, lands = refs[:n], refs[n:2 * n]
        send, recv, token = refs[2 * n + 1], refs[2 * n + 2], refs[-1]
        x, y, c, chips = _place()
        me = 2 * x + y
        for a in range(n):
            rows = _half_rows(arrs[a].shape[0], c)
            for k, (px, py) in enumerate(chips):
                pltpu.make_async_remote_copy(src_ref=ins[a].at[rows], dst_ref=lands[a].at[me, rows], send_sem=send.at[3 * a + k],
                                             recv_sem=recv.at[3 * a + k], device_id=(px, py, c), device_id_type=MESH).start()
        token[...] = jnp.zeros_like(token)

    hbm = lambda v: pltpu.with_memory_space_constraint(v, pltpu.HBM)
    land_shapes = [((NCHIP,) + a.shape, a.dtype) for a in arrs]
    out = pl.pallas_call(
        body, name=name,
        out_shape=(pltpu.SemaphoreType.DMA((3 * n,)), pltpu.SemaphoreType.DMA((3 * n,)), *[pltpu.HBM(a.shape, a.dtype) for a in arrs],
                   *[pltpu.HBM(s, d) for s, d in land_shapes], _sds((8, HD))),
        in_specs=[HBM] * (2 * n) + [ANY], out_specs=(SEM, SEM, *[HBM] * (2 * n), pl.BlockSpec(memory_space=pltpu.VMEM)),
        input_output_aliases={i: 2 + i for i in range(2 * n)},
        compiler_params=pltpu.CompilerParams(has_side_effects=DATAFLOW),
    )(*[hbm(a) for a in arrs], *[hbm(lax.empty(s, d)) for s, d in land_shapes], after)
    return out[0], out[1], list(out[2:2 + n]), list(out[2 + n:2 + 2 * n]), out[-1]


def _gather_wait(send, recv, srcs, lands, after, name):
    n = len(srcs)

    def body(*refs):
        ins, lands_ = refs[:n], refs[n:2 * n]
        send_, recv_ = refs[2 * n], refs[2 * n + 1]
        x, y, c, chips = _place()
        for a in range(n):
            rows = _half_rows(srcs[a].shape[0], c)
            for k, (px, py) in enumerate(chips):
                cp = pltpu.make_async_remote_copy(src_ref=ins[a].at[rows], dst_ref=lands_[a].at[2 * px + py, rows], send_sem=send_.at[3 * a + k],
                                                  recv_sem=recv_.at[3 * a + k], device_id=(px, py, c), device_id_type=MESH)
                cp.wait_send()
                cp.wait_recv()

    out = pl.pallas_call(
        body, name=name, out_shape=[pltpu.HBM(v.shape, v.dtype) for v in list(srcs) + list(lands)],
        in_specs=[HBM] * (2 * n) + [SEM, SEM, ANY], out_specs=[HBM] * (2 * n), input_output_aliases={i: i for i in range(2 * n)},
        compiler_params=pltpu.CompilerParams(has_side_effects=DATAFLOW),
    )(*srcs, *lands, send, recv, after)
    return list(out[n:])


def _pair_forward(lands, name):
    n = len(lands)

    def body(*refs):
        ins, outs = refs[:n], refs[n:2 * n]
        send, recv = refs[2 * n:]
        x, y, c, chips = _place()
        copies = []
        for a in range(n):
            r = lands[a].shape[1]
            for k, (px, py) in enumerate(chips):
                cp = pltpu.make_async_remote_copy(
                    src_ref=ins[a].at[2 * px + py, _half_rows(r, c)], dst_ref=outs[a].at[2 * px + py, _half_rows(r, c)],
                    send_sem=send.at[a, k], recv_sem=recv.at[a, k], device_id=(x, y, 1 - c), device_id_type=MESH)
                cp.start()
                copies.append(cp)
        for a in range(n):
            r = lands[a].shape[1]
            for k, (px, py) in enumerate(chips):
                pltpu.make_async_remote_copy(
                    src_ref=ins[a].at[2 * px + py, _half_rows(r, c)], dst_ref=outs[a].at[2 * px + py, _half_rows(r, 1 - c)],
                    send_sem=send.at[a, k], recv_sem=recv.at[a, k], device_id=(x, y, 1 - c), device_id_type=MESH).wait_recv()
        for cp in copies:
            cp.wait_send()

    return pl.pallas_call(body, name=name, in_specs=[ANY] * n, out_specs=[ANY] * n, out_shape=[_sds(v.shape, v.dtype) for v in lands],
                          scratch_shapes=[pltpu.SemaphoreType.DMA((n, 3)), pltpu.SemaphoreType.DMA((n, 3))],
                          input_output_aliases={i: i for i in range(n)})(*lands)


def _pair_exchange(arrs, name):
    n = len(arrs)

    def body(*refs):
        ins, outs = refs[:n], refs[n:2 * n]
        send, recv = refs[2 * n:]
        x, y, c, _ = _place()
        copies = []
        for a in range(n):
            h = arrs[a].shape[1] // 2
            cp = pltpu.make_async_remote_copy(src_ref=ins[a].at[:, pl.ds(pl.multiple_of((1 - c) * h, h), h)], dst_ref=outs[a],
                                              send_sem=send.at[a], recv_sem=recv.at[a], device_id=(x, y, 1 - c), device_id_type=MESH)
            cp.start()
            copies.append(cp)
        for cp in copies:
            cp.wait()

    return _comm_call(body, name, arrs, [_sds((a.shape[0], a.shape[1] // 2, a.shape[2]), a.dtype) for a in arrs],
                      [pltpu.SemaphoreType.DMA((n,)), pltpu.SemaphoreType.DMA((n,))])


def _chip_exchange(arrs, name):
    n = len(arrs)

    def body(*refs):
        ins, outs = refs[:n], refs[n:2 * n]
        send, recv = refs[2 * n:]
        x, y, c, chips = _place()
        me = 2 * x + y
        copies = []
        for a in range(n):
            for k, (px, py) in enumerate(chips):
                r = pltpu.make_async_remote_copy(src_ref=ins[a].at[2 * px + py], dst_ref=outs[a].at[me], send_sem=send.at[a, k],
                                                 recv_sem=recv.at[a, k], device_id=(px, py, c), device_id_type=MESH)
                r.start()
                copies.append(r)
        for cp in copies:
            cp.wait()

    return _comm_call(body, name, arrs, [_sds(a.shape, a.dtype) for a in arrs],
                      [pltpu.SemaphoreType.DMA((n, 3)), pltpu.SemaphoreType.DMA((n, 3))])


def _chip_exchange_start(arrs, name):
    n = len(arrs)

    def body(*refs):
        ins, lands = refs[:n], refs[n:2 * n]
        send, recv, token = refs[2 * n], refs[2 * n + 1], refs[-1]
        x, y, c, chips = _place()
        me = 2 * x + y
        for a in range(n):
            for k, (px, py) in enumerate(chips):
                pltpu.make_async_remote_copy(src_ref=ins[a].at[2 * px + py], dst_ref=lands[a].at[me], send_sem=send.at[3 * a + k],
                                             recv_sem=recv.at[3 * a + k], device_id=(px, py, c), device_id_type=MESH).start()
        token[...] = jnp.zeros_like(token)

    hbm = lambda v: pltpu.with_memory_space_constraint(v, pltpu.HBM)
    out = pl.pallas_call(
        body, name=name,
        out_shape=(pltpu.SemaphoreType.DMA((3 * n,)), pltpu.SemaphoreType.DMA((3 * n,)), *[pltpu.HBM(a.shape, a.dtype) for a in arrs],
                   *[pltpu.HBM(a.shape, a.dtype) for a in arrs], _sds((8, HD))),
        in_specs=[HBM] * (2 * n), out_specs=(SEM, SEM, *[HBM] * (2 * n), pl.BlockSpec(memory_space=pltpu.VMEM)),
        input_output_aliases={i: 2 + i for i in range(2 * n)},
        compiler_params=pltpu.CompilerParams(has_side_effects=DATAFLOW),
    )(*[hbm(a) for a in arrs], *[hbm(lax.empty(a.shape, a.dtype)) for a in arrs])
    return out[0], out[1], list(out[2:2 + n]), list(out[2 + n:2 + 2 * n]), out[-1]


def _chip_exchange_wait(send, recv, srcs, lands, after, name):
    n = len(srcs)

    def body(*refs):
        ins, lands_ = refs[:n], refs[n:2 * n]
        send_, recv_ = refs[2 * n], refs[2 * n + 1]
        x, y, c, chips = _place()
        for a in range(n):
            for k, (px, py) in enumerate(chips):
                cp = pltpu.make_async_remote_copy(src_ref=ins[a].at[2 * px + py], dst_ref=lands_[a].at[2 * px + py], send_sem=send_.at[3 * a + k],
                                                  recv_sem=recv_.at[3 * a + k], device_id=(px, py, c), device_id_type=MESH)
                cp.wait_send()
                cp.wait_recv()

    out = pl.pallas_call(
        body, name=name, out_shape=[pltpu.HBM(v.shape, v.dtype) for v in list(srcs) + list(lands)],
        in_specs=[HBM] * (2 * n) + [SEM, SEM, ANY], out_specs=[HBM] * (2 * n), input_output_aliases={i: i for i in range(2 * n)},
        compiler_params=pltpu.CompilerParams(has_side_effects=DATAFLOW),
    )(*srcs, *lands, send, recv, after)
    return list(out[n:])


def _pair_swap(arrs, name):
    n = len(arrs)

    def body(*refs):
        ins, outs = refs[:n], refs[n:2 * n]
        send, recv = refs[2 * n:]
        x, y, c, _ = _place()
        copies = []
        for a in range(n):
            cp = pltpu.make_async_remote_copy(src_ref=ins[a], dst_ref=outs[a], send_sem=send.at[a], recv_sem=recv.at[a],
                                              device_id=(x, y, 1 - c), device_id_type=MESH)
            cp.start()
            copies.append(cp)
        for cp in copies:
            cp.wait()

    return _comm_call(body, name, arrs, [_sds(a.shape, a.dtype) for a in arrs],
                      [pltpu.SemaphoreType.DMA((n,)), pltpu.SemaphoreType.DMA((n,))])


def _all_gather_devices(v, name):
    def body(v_ref, o_ref, send, recv, loc):
        x, y, c, _ = _place()
        me = 4 * x + 2 * y + c
        own = pltpu.make_async_copy(v_ref, o_ref.at[me], loc)
        own.start()
        copies = [own]
        for k in range(1, 8):
            fx, fy, fc = (k >> 2) & 1, (k >> 1) & 1, k & 1
            peer = (x ^ fx, y ^ fy, c ^ fc)
            r = pltpu.make_async_remote_copy(src_ref=v_ref, dst_ref=o_ref.at[me], send_sem=send.at[k - 1],
                                             recv_sem=recv.at[k - 1], device_id=peer, device_id_type=MESH)
            r.start()
            copies.append(r)
        for cp in copies:
            cp.wait()

    return _comm_call(body, name, [v], [_sds((8,) + v.shape, v.dtype)],
                      [pltpu.SemaphoreType.DMA((7,)), pltpu.SemaphoreType.DMA((7,)), pltpu.SemaphoreType.DMA])[0]


def _row_tile(r):
    return next((b for b in (512, 384, 256, 128, 64, 32, 16) if r % b == 0), r)


def _add2(a, b, out_dtype, name):
    r, w = a.shape
    br = _row_tile(r)

    def body(a_ref, b_ref, o_ref):
        o_ref[...] = (a_ref[...].astype(F32) + b_ref[...].astype(F32)).astype(out_dtype)

    blk = pl.BlockSpec((br, w), lambda i: (i, 0))
    return _pc(body, name, (r // br,), [blk, blk], blk, _sds((r, w), out_dtype))(a, b)


def _sum_slots(a, out_dtype, name, extra=None):
    n, r, w = a.shape
    br = _row_tile(r)

    def body(*refs):
        a_ref, o_ref = refs[0], refs[-1]
        acc = a_ref[0].astype(F32)
        for s in range(1, n):
            acc = acc + a_ref[s].astype(F32)
        if extra is not None:
            acc = acc + refs[1][...].astype(F32)
        o_ref[...] = acc.astype(out_dtype)

    ins = [a] + ([extra] if extra is not None else [])
    specs = [pl.BlockSpec((n, br, w), lambda i: (0, i, 0))] + ([pl.BlockSpec((br, w), lambda i: (i, 0))] if extra is not None else [])
    return _pc(body, name, (r // br,), specs, pl.BlockSpec((br, w), lambda i: (i, 0)), _sds((r, w), out_dtype))(*ins)


SMALL = ["norm_mix_g", "norm_xattn_g", "norm_mlp_g", "final_norm_g", "mem_norm_g", "hgrn_lb_logits", "mlstm_norm_g",
         "hgrn_norm_g", "c_qnorm_g", "c_knorm_g", "ab_gate_b", "c_fgate_b"]
SMALL_ROWS = 16


def _pack_small(parts):
    flat = jnp.concatenate([p.reshape(-1).astype(F32) for p in parts])
    return jnp.pad(flat, (0, SMALL_ROWS * D - flat.shape[0])).reshape(SMALL_ROWS, D)


def _unpack_small(buf, shapes):
    flat, out, off = buf.reshape(-1), [], 0
    for s in shapes:
        n = 1
        for d in s:
            n *= d
        out.append(flat[off:off + n].reshape(s))
        off += n
    return out


def kernel(x, mem, norm_mix_g, norm_xattn_g, norm_mlp_g, final_norm_g, ab_w_in, ab_conv_w, ab_gate_b, hgrn_lb_logits, mlstm_norm_g, hgrn_norm_g, ab_w_out, c_w_in, c_fgate_b, c_qnorm_g, c_knorm_g, c_w_out, mem_norm_g, mem_w_kv, xa_w_q, xa_w_o, mlp_w1, mlp_w2, loss_target, m_norm_mix_g, m_norm_xattn_g, m_norm_mlp_g, m_final_norm_g, m_ab_w_in, m_ab_conv_w, m_ab_gate_b, m_hgrn_lb_logits, m_mlstm_norm_g, m_hgrn_norm_g, m_ab_w_out, m_c_w_in, m_c_fgate_b, m_c_qnorm_g, m_c_knorm_g, m_c_w_out, m_mem_norm_g, m_mem_w_kv, m_xa_w_q, m_xa_w_o, m_mlp_w1, m_mlp_w2, v_norm_mix_g, v_norm_xattn_g, v_norm_mlp_g, v_final_norm_g, v_ab_w_in, v_ab_conv_w, v_ab_gate_b, v_hgrn_lb_logits, v_mlstm_norm_g, v_hgrn_norm_g, v_ab_w_out, v_c_w_in, v_c_fgate_b, v_c_qnorm_g, v_c_knorm_g, v_c_w_out, v_mem_norm_g, v_mem_w_kv, v_xa_w_q, v_xa_w_o, v_mlp_w1, v_mlp_w2):
    A = dict(locals())
    chip = 2 * lax.axis_index("x") + lax.axis_index("y")

    big = ["ab_w_in", "c_w_in", "ab_w_out", "c_w_out", "mem_w_kv", "xa_w_q", "xa_w_o", "mlp_w1", "mlp_w2"]
    shard2d = {"ab_w_in": (D, 1026), "c_w_in": (D, 1026), "ab_w_out": (256, D), "c_w_out": (256, D), "mem_w_kv": (D, 512),
               "xa_w_q": (512, D), "xa_w_o": (512, D), "mlp_w1": (2 * D, D), "mlp_w2": (2 * D, D)}
    shard16 = lambda n: A[n].reshape(shard2d[n]).astype(BF16)
    own_slot = lambda gs, os: [lax.dynamic_update_index_in_dim(g, o, chip, 0) for g, o in zip(gs, os)]
    cols = lambda g: jnp.concatenate([g[k] for k in range(NCHIP)], axis=1)
    per_layer = lambda g: g.reshape(NCHIP, 2, -1, D).transpose(1, 0, 2, 3)
    first = [shard16("ab_w_in"), jnp.pad(ab_conv_w[0], ((0, 16 - CONV_W), (0, 0)))]
    g_in0, g_conv = own_slot(_gather_weights(first, "gather_first"), first)
    W = dict(w_in0=_pack_w_in0(cols(g_in0)))
    rest_names = ["c_w_in", "ab_w_out", "c_w_out", "xa_w_q", "xa_w_o", "mlp_w1", "mlp_w2", "mem_w_kv"]
    rest = [shard16(n) for n in rest_names]
    send_s, recv_s, srcs, lands, token = _gather_start(rest, g_conv, "gather_rest_start")

    def late_weights(after):
        got = _pair_forward(_gather_wait(send_s, recv_s, srcs, lands, after, "gather_rest_wait"), "gather_rest_forward")
        gw = dict(zip(rest_names, own_slot(got, rest)))
        return dict(w_in1=_pack_w_in1(cols(gw["c_w_in"])), w_out0=gw["ab_w_out"].reshape(D, D), w_out1=gw["c_w_out"].reshape(D, D),
                    wkv_s=gw["mem_w_kv"],
                    wq=per_layer(gw["xa_w_q"]).reshape(2, D, D), wo=per_layer(gw["xa_w_o"]).reshape(2, D, D),
                    w1s=gw["mlp_w1"].reshape(NCHIP, 2, D, D), w2=gw["mlp_w2"].reshape(NCHIP, 2, D, D))

    S = dict(norm_mix_g=norm_mix_g + token[0, 0], norm_xattn_g=norm_xattn_g, norm_mlp_g=norm_mlp_g, final_norm_g=final_norm_g,
             conv_w=cols(g_conv[:, :CONV_W]), gate_b=ab_gate_b, lb_logits=hgrn_lb_logits, mlstm_norm_g=mlstm_norm_g,
             hgrn_norm_g=hgrn_norm_g, c_fgate_b=c_fgate_b, c_qnorm_g=c_qnorm_g, c_knorm_g=c_knorm_g, mem_norm_g=mem_norm_g)

    core = lax.axis_index("c")
    by_rows = lambda g: g.reshape(NCHIP, -1, D)

    def stack_cols(g):
        return jnp.stack([g[:, 1026 * k:1026 * (k + 1)] for k in range(NCHIP)])

    def pair_sums(arrs, tag):
        theirs = _pair_exchange(arrs, f"pair_exchange_{tag}")
        out = []
        for i, (a, th) in enumerate(zip(arrs, theirs)):
            h = a.shape[1] // 2
            mine = lax.dynamic_slice_in_dim(a, core * h, h, axis=1)
            out.append(_add2(mine.reshape(-1, a.shape[2]), th.reshape(-1, a.shape[2]), BF16, f"pair_sum_{tag}{i}").reshape(th.shape))
        return out

    def chip_sums(psums, from_chips, tag):
        out = []
        for i, (f, p) in enumerate(zip(from_chips, psums)):
            f = lax.dynamic_update_index_in_dim(f, lax.dynamic_index_in_dim(p, chip, 0, keepdims=False), chip, 0)
            out.append(_sum_slots(f, F32, f"chip_sum_{tag}{i}"))
        return out

    started = {}

    def grads_hook(stage, g):
        if stage == "layer1":
            arrs = [jnp.concatenate([by_rows(g["w_out"]), by_rows(g["wq"]), by_rows(g["wo"]), g["w1"], by_rows(g["w2"])], axis=1),
                    stack_cols(_unpack_w_in1(g["w_in"]))]
        else:
            arrs = [jnp.concatenate([by_rows(g["wq"]), by_rows(g["wo"]), g["w1"], by_rows(g["w2"])], axis=1)]
        psums = pair_sums(arrs, stage)
        *handles, token = _chip_exchange_start(psums, f"chip_exchange_start_{stage}")
        started[stage] = (psums, handles)
        return token[0, 0]

    lossp, dx, G = _local_step(x[0], mem[0], loss_target[0], W, S, late_weights, grads_hook)

    gsmall = {"norm_mix_g": G["norm_mix_g"], "norm_xattn_g": G["norm_xattn_g"], "norm_mlp_g": G["norm_mlp_g"],
              "final_norm_g": G["final_norm_g"], "mem_norm_g": G["mem_norm_g"], "hgrn_lb_logits": G["lb_logits"],
              "mlstm_norm_g": G["mlstm_norm_g"], "hgrn_norm_g": G["hgrn_norm_g"], "c_qnorm_g": G["c_qnorm_g"],
              "c_knorm_g": G["c_knorm_g"], "ab_gate_b": G["gate_b"], "c_fgate_b": G["c_fgate_b"]}
    packed = _pack_small([gsmall[n] for n in SMALL] + [G["conv_w"], lossp])
    red = _sum_slots(_all_gather_devices(packed, "gather_small"), F32, "sum_small")
    small_shapes = [A[n].shape for n in SMALL]
    *gs, gconv, loss = _unpack_small(red, small_shapes + [(CONV_W, D), ()])
    gs = dict(zip(SMALL, gs))
    gconv = lax.dynamic_slice_in_dim(gconv, chip * 256, 256, axis=1)[None]

    last = pair_sums([by_rows(G["w_out0"]), stack_cols(_unpack_w_in0(G["w_in0"])), G["wkv"]], "last")
    rhalf = chip_sums(last, _chip_exchange(last, "chip_exchange_last"), "last")
    for stage in ("layer1", "layer0_mlp_xattn"):
        psums, handles = started[stage]
        rhalf += chip_sums(psums, _chip_exchange_wait(*handles, dx, f"chip_exchange_wait_{stage}"), stage)
    other = _pair_swap(rhalf, "pair_swap")
    r_out0, r_in0, r_kv, r_l1, r_in1, r_l0 = [
        jnp.where(core == 0, jnp.concatenate([m_, o_], axis=0), jnp.concatenate([o_, m_], axis=0)) for m_, o_ in zip(rhalf, other)]
    gbig = {"ab_w_in": r_in0, "c_w_in": r_in1, "mem_w_kv": r_kv, "ab_w_out": r_out0, "c_w_out": r_l1[0:256],
            "xa_w_q": jnp.concatenate([r_l0[0:256], r_l1[256:512]], axis=0),
            "xa_w_o": jnp.concatenate([r_l0[256:512], r_l1[512:768]], axis=0),
            "mlp_w1": jnp.concatenate([r_l0[512:1536], r_l1[768:1792]], axis=0),
            "mlp_w2": jnp.concatenate([r_l0[1536:2560], r_l1[1792:2816]], axis=0)}

    out_g, out_d, out_m, out_v = {}, {}, {}, {}
    for n in big:
        d_, m_, v_ = _adam(A[n].reshape(shard2d[n]), gbig[n], A["m_" + n].reshape(shard2d[n]), A["v_" + n].reshape(shard2d[n]), "adam_" + n)
        out_g[n] = gbig[n].reshape(A[n].shape)
        out_d[n], out_m[n], out_v[n] = d_.reshape(A[n].shape), m_.reshape(A[n].shape), v_.reshape(A[n].shape)
    sd, sm, sv = _adam(_pack_small([A[n] for n in SMALL]), _pack_small([gs[n] for n in SMALL]),
                       _pack_small([A["m_" + n] for n in SMALL]), _pack_small([A["v_" + n] for n in SMALL]), "adam_small")
    for n, d_, m_, v_ in zip(SMALL, _unpack_small(sd, small_shapes), _unpack_small(sm, small_shapes), _unpack_small(sv, small_shapes)):
        out_g[n], out_d[n], out_m[n], out_v[n] = gs[n], d_, m_, v_
    cd, cm_, cv = _adam(ab_conv_w[0], gconv[0], m_ab_conv_w[0], v_ab_conv_w[0], "adam_conv")
    out_g["ab_conv_w"], out_d["ab_conv_w"], out_m["ab_conv_w"], out_v["ab_conv_w"] = gconv, cd[None], cm_[None], cv[None]

    order = ["norm_mix_g", "norm_xattn_g", "norm_mlp_g", "final_norm_g", "ab_w_in", "ab_conv_w", "ab_gate_b", "hgrn_lb_logits",
             "mlstm_norm_g", "hgrn_norm_g", "ab_w_out", "c_w_in", "c_fgate_b", "c_qnorm_g", "c_knorm_g", "c_w_out", "mem_norm_g",
             "mem_w_kv", "xa_w_q", "xa_w_o", "mlp_w1", "mlp_w2"]
    return (loss, dx[None], *[out_g[n] for n in order], *[out_d[n] for n in order], *[out_m[n] for n in order],
            *[out_v[n] for n in order])
```

```python
import functools

import jax
import jax.numpy as jnp
from jax import lax
from jax.experimental import pallas as pl
from jax.experimental.pallas import tpu as pltpu

F32 = jnp.float32
BF16 = jnp.bfloat16
EPS = 1e-6
D = 1024
CHUNK = 64
HD = 128
XD = 256
NEG = -1e30
VMEM_LIMIT_V7X = 56 * 1024 * 1024
ADAM_LR, ADAM_B1, ADAM_B2, ADAM_EPS, ADAM_WD, ADAM_STEP = 0.001, 0.9, 0.999, 1e-08, 0.01, 10
MESH = pl.DeviceIdType.MESH


def _pc(body, name, grid, in_specs, out_specs, out_shape, scratch=(), **kw):
    return pl.pallas_call(
        body, name=name, grid=grid, in_specs=in_specs, out_specs=out_specs, out_shape=out_shape,
        scratch_shapes=scratch,
        compiler_params=pltpu.CompilerParams(
            dimension_semantics=("arbitrary",) * len(grid), vmem_limit_bytes=VMEM_LIMIT_V7X), **kw)


def _sds(shape, dtype=F32):
    return jax.ShapeDtypeStruct(shape, dtype)


def _blk(n, target):
    return max(b for b in range(128, max(target, 128) + 1, 128) if n % b == 0)


def _dot(a, b, dims):
    return lax.dot_general(a, b, (dims, ((), ())), preferred_element_type=F32)


def _nn(a, b):
    return _dot(a, b, ((1,), (0,)))


def _nt(a, b):
    return _dot(a, b, ((1,), (1,)))


def _tn(a, b):
    return _dot(a, b, ((0,), (0,)))


def _sigmoid(x):
    return 1.0 / (1.0 + jnp.exp(-x))


def _log_sigmoid(x):
    return jnp.minimum(x, 0.0) - jnp.log(1.0 + jnp.exp(-jnp.abs(x)))


def _rstd(x):
    return lax.rsqrt(jnp.mean(x * x, axis=-1, keepdims=True) + EPS)


def _rms_bwd(du, x, g):
    r = _rstd(x)
    xh = x * r
    dxh = du * g
    dx = r * (dxh - xh * jnp.mean(dxh * xh, axis=-1, keepdims=True))
    return dx, du * xh


def _norm_mm(h, g, w, name, bm=1024, bn=512):
    t, n = h.shape[0], w.shape[1]
    bm, bn = min(bm, t), _blk(n, 3 * bn)

    def body(h_ref, g_ref, w_ref, z_ref, u_ref):
        @pl.when(pl.program_id(1) == 0)
        def _():
            x = h_ref[...]
            u_ref[...] = (x * _rstd(x) * g_ref[...]).astype(BF16)
        z_ref[...] = _nn(u_ref[...], w_ref[...])

    return _pc(body, name, (t // bm, n // bn),
               [pl.BlockSpec((bm, D), lambda i, j: (i, 0)), pl.BlockSpec((1, D), lambda i, j: (0, 0)),
                pl.BlockSpec((D, bn), lambda i, j: (0, j))],
               [pl.BlockSpec((bm, bn), lambda i, j: (i, j)), pl.BlockSpec((bm, D), lambda i, j: (i, 0))],
               [_sds((t, n)), _sds((t, D), BF16)])(h, g, w)


def _mm_tn(a, b, name, bm=1024, bn=1024, bt=2048, col_chips=None):
    t, m = a.shape
    n = b.shape[1]
    bm, bn, bt = _blk(m, bm), (n // col_chips if col_chips else _blk(n, bn + bn // 2)), min(bt, t)
    nt = t // bt

    def body(a_ref, b_ref, o_ref, acc):
        k = pl.program_id(2)

        @pl.when(k == 0)
        def _():
            acc[...] = jnp.zeros_like(acc)

        acc[...] += _tn(a_ref[...].astype(BF16), b_ref[...].astype(BF16))

        @pl.when(k == nt - 1)
        def _():
            o_ref[...] = acc[...].astype(BF16)

    if col_chips:
        out_spec, out_shape = pl.BlockSpec((None, bm, bn), lambda i, j, k: (j, i, 0)), _sds((col_chips, m, bn), BF16)
    else:
        out_spec, out_shape = pl.BlockSpec((bm, bn), lambda i, j, k: (i, j)), _sds((m, n), BF16)
    return _pc(body, name, (m // bm, n // bn, nt),
               [pl.BlockSpec((bt, bm), lambda i, j, k: (k, i)), pl.BlockSpec((bt, bn), lambda i, j, k: (k, j))],
               out_spec, out_shape, scratch=[pltpu.VMEM((bm, bn), F32)])(a, b)


def _bwd_in(dz, w, h, g, dh, name, bm=1024, bk=1024):
    t, n = dz.shape
    bm, bk = min(bm, t), _blk(n, bk + bk // 2)
    nk = n // bk

    def body(dz_ref, w_ref, h_ref, g_ref, dh_ref, o_ref, dg_ref, acc):
        i, k = pl.program_id(0), pl.program_id(1)

        @pl.when(k == 0)
        def _():
            acc[...] = jnp.zeros_like(acc)

        @pl.when((i == 0) & (k == 0))
        def _():
            dg_ref[...] = jnp.zeros_like(dg_ref)

        acc[...] += _nt(dz_ref[...], w_ref[...])

        @pl.when(k == nk - 1)
        def _():
            dx, dgr = _rms_bwd(acc[...], h_ref[...], g_ref[...])
            o_ref[...] = dh_ref[...] + dx
            dg_ref[...] += jnp.sum(dgr, axis=0, keepdims=True)

    return _pc(body, name, (t // bm, nk),
               [pl.BlockSpec((bm, bk), lambda i, k: (i, k)), pl.BlockSpec((D, bk), lambda i, k: (0, k)),
                pl.BlockSpec((bm, D), lambda i, k: (i, 0)), pl.BlockSpec((1, D), lambda i, k: (0, 0)),
                pl.BlockSpec((bm, D), lambda i, k: (i, 0))],
               [pl.BlockSpec((bm, D), lambda i, k: (i, 0)), pl.BlockSpec((1, D), lambda i, k: (0, 0))],
               [_sds((t, D)), _sds((1, D))], scratch=[pltpu.VMEM((bm, D), F32)])(dz, w, h, g, dh)


def _mlp_fwd(h, g, w1s, w2, l, name, bm=1024):
    t = h.shape[0]
    bm = min(bm, t)
    nk = w1s.shape[0]

    def body(h_ref, g_ref, w1_ref, w2_ref, o_ref, a_ref, u_ref, acc):
        k = pl.program_id(1)

        @pl.when(k == 0)
        def _():
            x = h_ref[...]
            u_ref[...] = (x * _rstd(x) * g_ref[...]).astype(BF16)
            acc[...] = jnp.zeros_like(acc)

        a = _nn(u_ref[...], w1_ref[...])
        a_ref[...] = a
        r = jnp.square(jnp.maximum(a, 0.0)).astype(BF16)
        acc[...] += _nn(r, w2_ref[...])

        @pl.when(k == nk - 1)
        def _():
            o_ref[...] = h_ref[...] + acc[...]

    return _pc(body, name, (t // bm, nk),
               [pl.BlockSpec((bm, D), lambda i, k: (i, 0)), pl.BlockSpec((1, D), lambda i, k: (0, 0)),
                pl.BlockSpec((None, None, D, D), lambda i, k: (k, l, 0, 0)), pl.BlockSpec((None, None, D, D), lambda i, k: (k, l, 0, 0))],
               [pl.BlockSpec((bm, D), lambda i, k: (i, 0)), pl.BlockSpec((bm, D), lambda i, k: (i, k)),
                pl.BlockSpec((bm, D), lambda i, k: (i, 0))],
               [_sds((t, D)), _sds((t, nk * D)), _sds((t, D), BF16)],
               scratch=[pltpu.VMEM((bm, D), F32)])(h, g, w1s, w2)


def _mlp_bwd(dh, a, w1s, w2, l, h, g, name, bm=512):
    t = h.shape[0]
    bm = min(bm, t)
    nk = w1s.shape[0]

    def body(dh_ref, a_ref, w1_ref, w2_ref, h_ref, g_ref, o_ref, da_ref, r_ref, dg_ref, acc):
        i, k = pl.program_id(0), pl.program_id(1)

        @pl.when(k == 0)
        def _():
            acc[...] = jnp.zeros_like(acc)

        @pl.when((i == 0) & (k == 0))
        def _():
            dg_ref[...] = jnp.zeros_like(dg_ref)

        ap = jnp.maximum(a_ref[...], 0.0)
        r_ref[...] = jnp.square(ap).astype(BF16)
        dr = _nt(dh_ref[...].astype(BF16), w2_ref[...])
        da = (dr * (2.0 * ap)).astype(BF16)
        da_ref[...] = da
        acc[...] += _nt(da, w1_ref[...])

        @pl.when(k == nk - 1)
        def _():
            dx, dgr = _rms_bwd(acc[...], h_ref[...], g_ref[...])
            o_ref[...] = dh_ref[...] + dx
            dg_ref[...] += jnp.sum(dgr, axis=0, keepdims=True)

    return _pc(body, name, (t // bm, nk),
               [pl.BlockSpec((bm, D), lambda i, k: (i, 0)), pl.BlockSpec((bm, D), lambda i, k: (i, k)),
                pl.BlockSpec((None, None, D, D), lambda i, k: (k, l, 0, 0)), pl.BlockSpec((None, None, D, D), lambda i, k: (k, l, 0, 0)),
                pl.BlockSpec((bm, D), lambda i, k: (i, 0)), pl.BlockSpec((1, D), lambda i, k: (0, 0))],
               [pl.BlockSpec((bm, D), lambda i, k: (i, 0)), pl.BlockSpec((bm, D), lambda i, k: (i, k)),
                pl.BlockSpec((bm, D), lambda i, k: (i, k)), pl.BlockSpec((1, D), lambda i, k: (0, 0))],
               [_sds((t, D)), _sds((t, nk * D), BF16), _sds((t, nk * D), BF16), _sds((1, D))],
               scratch=[pltpu.VMEM((bm, D), F32)])(dh, a, w1s, w2, h, g)


def _rows_of(x):
    return lax.broadcasted_iota(jnp.int32, x.shape, 0)


def _shift_down(x, s):
    if s == 0:
        return x
    return jnp.where(_rows_of(x) >= s, pltpu.roll(x, s, 0), 0.0)


def _shift_up(x, s):
    if s == 0:
        return x
    n = x.shape[0]
    return jnp.where(_rows_of(x) < n - s, pltpu.roll(x, n - s, 0), 0.0)


def _cumsum_rows(x):
    n, s = x.shape[0], 1
    while s < n:
        x = x + _shift_down(x, s)
        s *= 2
    return x


def _rcumsum_rows(x):
    n, s = x.shape[0], 1
    while s < n:
        x = x + _shift_up(x, s)
        s *= 2
    return x


def _silu(x):
    return x * _sigmoid(x)


def _dsilu(x):
    s = _sigmoid(x)
    return s * (1.0 + x * (1.0 - s))


CONV_W = 4


def _conv_pre(u, w):
    y = _shift_down(u, CONV_W - 1) * w[0:1, :]
    for j in range(1, CONV_W):
        y = y + _shift_down(u, CONV_W - 1 - j) * w[j:j + 1, :]
    return y


def _conv_fwd(z0, cw, name):
    t = z0.shape[0]

    def body(u_ref, w_ref, o_ref):
        o_ref[...] = _silu(_conv_pre(u_ref[...], w_ref[...]))

    return _pc(body, name, (2 * 512 // HD,),
               [pl.BlockSpec((t, HD), lambda c: (0, c)), pl.BlockSpec((CONV_W, HD), lambda c: (0, c))],
               pl.BlockSpec((t, HD), lambda c: (0, c)), _sds((t, 1024)))(z0, cw)


def _conv_bwd(z0, cw, dy, name):
    t = z0.shape[0]

    def body(u_ref, w_ref, dy_ref, du_ref, dw_ref):
        u, w = u_ref[...], w_ref[...]
        dpre = dy_ref[...] * _dsilu(_conv_pre(u, w))
        du = _shift_up(dpre, CONV_W - 1) * w[0:1, :]
        for j in range(1, CONV_W):
            du = du + _shift_up(dpre, CONV_W - 1 - j) * w[j:j + 1, :]
        du_ref[...] = du.astype(BF16)
        for j in range(CONV_W):
            dw_ref[j:j + 1, :] = jnp.sum(dpre * _shift_down(u, CONV_W - 1 - j), axis=0, keepdims=True)

    return _pc(body, name, (2 * 512 // HD,),
               [pl.BlockSpec((t, HD), lambda c: (0, c)), pl.BlockSpec((CONV_W, HD), lambda c: (0, c)),
                pl.BlockSpec((t, HD), lambda c: (0, c))],
               [pl.BlockSpec((t, HD), lambda c: (0, c)), pl.BlockSpec((CONV_W, HD), lambda c: (0, c))],
               [_sds((t, 1024), BF16), _sds((CONV_W, 1024))])(z0, cw, dy)


def _mlstm_gates(gate, bias, m_in):
    L = gate.shape[0]
    r = lax.broadcasted_iota(jnp.int32, (L, L), 0)
    c = lax.broadcasted_iota(jnp.int32, (L, L), 1)
    eye, tril = r == c, c <= r
    i_col = gate[:, 0:1] + bias[:, 0:1]
    f_col = gate[:, 1:2] + bias[:, 1:2]
    logf_col = _log_sigmoid(f_col)
    logf_row = jnp.sum(jnp.where(eye, logf_col, 0.0), axis=0, keepdims=True)
    i_row = jnp.sum(jnp.where(eye, i_col, 0.0), axis=0, keepdims=True)
    b_col = jnp.sum(jnp.where(tril, logf_row, 0.0), axis=1, keepdims=True)
    b_row = jnp.sum(jnp.where(r <= c, logf_col, 0.0), axis=0, keepdims=True)
    logd = jnp.where(tril, b_col - b_row + i_row, NEG)
    inter = b_col + m_in
    m_t = jnp.maximum(inter, jnp.max(logd, axis=1, keepdims=True))
    w_t = jnp.exp(inter - m_t)
    dm = jnp.exp(logd - m_t)
    b_last = b_col[L - 1:L, :]
    log_in = b_last - b_col + i_col
    m_new = jnp.maximum(b_last + m_in, jnp.max(log_in, axis=0, keepdims=True))
    w_col = jnp.exp(log_in - m_new)
    decay = jnp.exp(b_last + m_in - m_new)
    return dict(eye=eye, r=r, c=c, f_col=f_col, m_t=m_t, w_t=w_t, dm=dm, m_new=m_new, w_col=w_col, decay=decay)


def _mlstm_fwd(qk, z0, gates, bias, name):
    t = qk.shape[0]
    nc, nh, L = t // CHUNK, 4, CHUNK
    scale = HD ** -0.5

    def body(q_ref, k_ref, v_ref, g_ref, b_ref, h_ref, cs_ref, ns_ref, ms_ref, c_s, n_s, m_s):
        @pl.when(pl.program_id(0) == 0)
        def _():
            c_s[...] = jnp.zeros_like(c_s)
            n_s[...] = jnp.zeros_like(n_s)
            m_s[...] = jnp.zeros_like(m_s)

        for hd in range(nh):
            sl = slice(hd * HD, (hd + 1) * HD)
            cm, nv, m_in = c_s[hd], n_s[hd], m_s[hd]
            cs_ref[hd] = cm
            ns_ref[hd] = nv
            ms_ref[hd] = jnp.broadcast_to(m_in, (1, HD))
            q, kh, v = q_ref[:, sl], k_ref[:, sl] * scale, v_ref[:, sl]
            G = _mlstm_gates(g_ref[hd], b_ref[hd], m_in)
            qb, kb, vb = q.astype(BF16), kh.astype(BF16), v.astype(BF16)
            sc = _nt(qb, kb) * G["dm"]
            num = _nn(sc.astype(BF16), vb) + G["w_t"] * _nn(qb, cm.astype(BF16))
            den = jnp.sum(sc, axis=1, keepdims=True) + G["w_t"] * jnp.sum(q * nv, axis=1, keepdims=True)
            h_ref[:, sl] = num / jnp.maximum(jnp.abs(den), jnp.exp(-G["m_t"]))
            wk = G["w_col"] * kh
            c_s[hd] = G["decay"] * cm + _tn(wk.astype(BF16), vb)
            n_s[hd] = G["decay"] * nv + jnp.sum(wk, axis=0, keepdims=True)
            m_s[hd] = G["m_new"]

    hspec = lambda blk: pl.BlockSpec((L, 512), lambda j: (j, blk))
    st = lambda r: pl.BlockSpec((nh, None, r, HD), lambda j: (0, j, 0, 0))
    return _pc(body, name, (nc,),
               [hspec(0), hspec(1), hspec(2), pl.BlockSpec((nh, L, 2), lambda j: (0, j, 0)),
                pl.BlockSpec((nh, 1, 2), lambda j: (0, 0, 0))],
               [hspec(0), st(HD), st(1), st(1)],
               [_sds((t, 512)), _sds((nh, nc, HD, HD)), _sds((nh, nc, 1, HD)), _sds((nh, nc, 1, HD))],
               scratch=[pltpu.VMEM((nh, HD, HD), F32), pltpu.VMEM((nh, 1, HD), F32), pltpu.VMEM((nh, 1, 1), F32)])(qk, qk, z0, gates, bias)


def _mlstm_bwd(qk, z0, gates, bias, cs, ns, ms, dh, name):
    t = qk.shape[0]
    nc, nh, L = t // CHUNK, 4, CHUNK
    scale = HD ** -0.5

    def body(q_ref, k_ref, v_ref, g_ref, b_ref, cs_ref, ns_ref, ms_ref, dh_ref, dq_ref, dk_ref, dv_ref, dg_ref, dc_s, dn_s):
        @pl.when(pl.program_id(0) == 0)
        def _():
            dc_s[...] = jnp.zeros_like(dc_s)
            dn_s[...] = jnp.zeros_like(dn_s)

        for hd in range(nh):
            one_head(hd, slice(hd * HD, (hd + 1) * HD), q_ref, k_ref, v_ref, g_ref, b_ref, cs_ref, ns_ref, ms_ref, dh_ref,
                     dq_ref, dk_ref, dv_ref, dg_ref, dc_s, dn_s)

    def one_head(hd, sl, q_ref, k_ref, v_ref, g_ref, b_ref, cs_ref, ns_ref, ms_ref, dh_ref, dq_ref, dk_ref, dv_ref, dg_ref, dc_s, dn_s):
        cm, nv, m_in = cs_ref[hd], ns_ref[hd], ms_ref[hd][:, 0:1]
        q, kh, v = q_ref[:, sl], k_ref[:, sl] * scale, v_ref[:, sl]
        G = _mlstm_gates(g_ref[hd], b_ref[hd], m_in)
        w_t, dmat, w_col, decay = G["w_t"], G["dm"], G["w_col"], G["decay"]
        qb, kb, vb, cb = q.astype(BF16), kh.astype(BF16), v.astype(BF16), cm.astype(BF16)
        s = _nt(qb, kb)
        sc = s * dmat
        scb = sc.astype(BF16)
        qc = _nn(qb, cb)
        qn = jnp.sum(q * nv, axis=1, keepdims=True)
        num = _nn(scb, vb) + w_t * qc
        den = jnp.sum(sc, axis=1, keepdims=True) + w_t * qn
        e_m = jnp.exp(-G["m_t"])
        dnm = jnp.maximum(jnp.abs(den), e_m)
        dh_ = dh_ref[:, sl]
        dnum = dh_ / dnm
        dden = jnp.where(jnp.abs(den) > e_m, -jnp.sum(dh_ * num, axis=1, keepdims=True) / (dnm * dnm) * jnp.sign(den), 0.0)
        dnumb = dnum.astype(BF16)
        dsc = _nt(dnumb, vb) + dden
        dv = _tn(scb, dnumb)
        wd = w_t * dnum
        wdb = wd.astype(BF16)
        ds = dsc * dmat
        dsb = ds.astype(BF16)
        dq = _nt(wdb, cb) + (w_t * dden) * nv + _nn(dsb, kb)
        dc_o = _tn(qb, wdb)
        dn_o = jnp.sum(q * (w_t * dden), axis=0, keepdims=True)
        dw = jnp.sum(dnum * qc, axis=1, keepdims=True) + dden * qn
        dkh = _tn(dsb, qb)
        dlogd = ds * s
        db_col = jnp.sum(dlogd, axis=1, keepdims=True) + dw * w_t
        csum = jnp.sum(dlogd, axis=0, keepdims=True)
        dcn, dnn = dc_s[hd], dn_s[hd]
        dcnb = dcn.astype(BF16)
        kdc = _nn(kb, dcnb)
        dws = jnp.sum(kdc * v, axis=1, keepdims=True) + jnp.sum(kh * dnn, axis=1, keepdims=True)
        dv = dv + w_col * kdc
        dkh = dkh + w_col * (_nt(vb, dcnb) + dnn)
        dlin = dws * w_col
        ddecay = jnp.sum(jnp.sum(dcn * cm, axis=1, keepdims=True), axis=0, keepdims=True) + jnp.sum(dnn * nv, axis=1, keepdims=True)
        dlast = ddecay * decay + jnp.sum(dlin, axis=0, keepdims=True)
        rows = lax.broadcasted_iota(jnp.int32, (L, 1), 0)
        db_col = db_col - dlin + jnp.where(rows == L - 1, dlast, 0.0)
        eye, r, c = G["eye"], G["r"], G["c"]
        di = dlin + jnp.sum(jnp.where(eye, csum, 0.0), axis=1, keepdims=True)
        db_row = jnp.sum(jnp.where(eye, db_col, 0.0), axis=0, keepdims=True) - csum
        dlogf = jnp.sum(jnp.where(c >= r, db_row, 0.0), axis=1, keepdims=True)
        dg_ref[hd, :, 0:1] = di
        dg_ref[hd, :, 1:2] = dlogf * (1.0 - _sigmoid(G["f_col"]))
        dq_ref[:, sl] = dq
        dk_ref[:, sl] = dkh * scale
        dv_ref[:, sl] = dv
        dc_s[hd] = decay * dcn + dc_o
        dn_s[hd] = decay * dnn + dn_o

    rv = lambda j: nc - 1 - j
    hspec = lambda blk: pl.BlockSpec((L, 512), lambda j: (rv(j), blk))
    st = lambda r: pl.BlockSpec((nh, None, r, HD), lambda j: (0, rv(j), 0, 0))
    gs = pl.BlockSpec((nh, L, 2), lambda j: (0, rv(j), 0))
    return _pc(body, name, (nc,),
               [hspec(0), hspec(1), hspec(2), gs, pl.BlockSpec((nh, 1, 2), lambda j: (0, 0, 0)),
                st(HD), st(1), st(1), hspec(0)],
               [hspec(0), hspec(0), hspec(0), gs],
               [_sds((t, 512)), _sds((t, 512)), _sds((t, 512)), _sds((nh, t, 2))],
               scratch=[pltpu.VMEM((nh, HD, HD), F32), pltpu.VMEM((nh, 1, HD), F32)])(qk, qk, z0, gates, bias, cs, ns, ms, dh)


def _hgrn_act(qb_, fb_, ib_, lg):
    lb = _sigmoid(lg[0:1, :] - lg[1:2, :])
    sg = _sigmoid(fb_)
    f = lb + (1.0 - lb) * sg
    return lb, sg, f, _silu(qb_), (1.0 - lb) * (1.0 - sg), _silu(ib_), _cumsum_rows(jnp.log(f))


HG_SUB = 16


def _hgrn_offdiag(q, k, b, r0):
    beta = b[r0 - 1:r0, :]
    e1 = jnp.exp(b[r0:r0 + HG_SUB, :] - beta)
    e2 = jnp.where(_rows_of(b) < r0, jnp.exp(jnp.minimum(beta - b, 0.0)), 0.0)
    return q[r0:r0 + HG_SUB, :] * e1, k * e2, e1, e2


def _hgrn_fwd(z0, lbl, name):
    t = z0.shape[0]
    nc, nh, L = t // CHUNK, 4, CHUNK

    def body(q_ref, f_ref, i_ref, l_ref, o_ref, ss_ref, st_s):
        @pl.when(pl.program_id(0) == 0)
        def _():
            st_s[...] = jnp.zeros_like(st_s)

        for hd in range(nh):
            sl = slice(hd * HD, (hd + 1) * HD)
            st = st_s[hd]
            ss_ref[hd] = st
            _, _, _, q, k, v, b = _hgrn_act(q_ref[:, sl], f_ref[:, sl], i_ref[:, sl], l_ref[:, sl])
            o = _nt((q * jnp.exp(b)).astype(BF16), st.astype(BF16))
            sub = _rows_of(b) & (HG_SUB - 1)
            o = o + jnp.sum(q * k, axis=1, keepdims=True) * v
            for dl in range(1, HG_SUB):
                e = jnp.exp(jnp.where(sub >= dl, b - pltpu.roll(b, dl, 0), NEG))
                a = jnp.sum(q * pltpu.roll(k, dl, 0) * e, axis=1, keepdims=True)
                o = o + a * pltpu.roll(v, dl, 0)
            o_ref[:, sl] = o
            vb = v.astype(BF16)
            for i in range(1, L // HG_SUB):
                r0 = i * HG_SUB
                qt, kt, _, _ = _hgrn_offdiag(q, k, b, r0)
                a = _nt(qt.astype(BF16), kt.astype(BF16))
                o_ref[r0:r0 + HG_SUB, sl] += _nn(a.astype(BF16), vb)
            bl = b[L - 1:L, :]
            st_s[hd] = st * jnp.exp(bl) + _tn(v.astype(BF16), (k * jnp.exp(bl - b)).astype(BF16))

    hspec = lambda blk: pl.BlockSpec((L, 512), lambda j: (j, blk))
    return _pc(body, name, (nc,),
               [hspec(4), hspec(5), hspec(6), pl.BlockSpec((2, 512), lambda j: (0, 0))],
               [hspec(0), pl.BlockSpec((nh, None, HD, HD), lambda j: (0, j, 0, 0))],
               [_sds((t, 512)), _sds((nh, nc, HD, HD))],
               scratch=[pltpu.VMEM((nh, HD, HD), F32)])(z0, z0, z0, lbl)


def _hgrn_bwd(z0, lbl, ss, do, name):
    t = z0.shape[0]
    nc, nh, L = t // CHUNK, 4, CHUNK

    def body(q_ref, f_ref, i_ref, l_ref, ss_ref, do_ref, dq_ref, df_ref, di_ref, dl_ref, dst_s, dlb_s, dq_a, dk_a, dv_a, db_a):
        @pl.when(pl.program_id(0) == 0)
        def _():
            dst_s[...] = jnp.zeros_like(dst_s)
            dlb_s[...] = jnp.zeros_like(dlb_s)

        for hd in range(nh):
            one_head(hd, slice(hd * HD, (hd + 1) * HD), q_ref, f_ref, i_ref, l_ref, ss_ref, do_ref, dq_ref, df_ref, di_ref, dl_ref,
                     dst_s, dlb_s, dq_a.at[hd], dk_a.at[hd], dv_a.at[hd], db_a.at[hd])

    def one_head(hd, sl, q_ref, f_ref, i_ref, l_ref, ss_ref, do_ref, dq_ref, df_ref, di_ref, dl_ref, dst_s, dlb_s, dq_a, dk_a, dv_a, db_a):
        st = ss_ref[hd]
        qp, fp, ip = q_ref[:, sl], f_ref[:, sl], i_ref[:, sl]
        lb, sg, f, q, k, v, b = _hgrn_act(qp, fp, ip, l_ref[:, sl])
        do_ = do_ref[:, sl]
        dob, stb = do_.astype(BF16), st.astype(BF16)
        eb = jnp.exp(b)
        qe = q * eb
        dqe = _nn(dob, stb)
        dst_o = _tn(dob, qe.astype(BF16))
        dq = dqe * eb
        db = dqe * qe
        rows = _rows_of(b)
        sub = rows & (HG_SUB - 1)
        p0 = jnp.sum(do_ * v, axis=1, keepdims=True)
        dq = dq + p0 * k
        dk = p0 * q
        dv = jnp.sum(q * k, axis=1, keepdims=True) * do_
        for dl in range(1, HG_SUB):
            up = L - dl
            kd, vd = pltpu.roll(k, dl, 0), pltpu.roll(v, dl, 0)
            e = jnp.exp(jnp.where(sub >= dl, b - pltpu.roll(b, dl, 0), NEG))
            a = jnp.sum(q * kd * e, axis=1, keepdims=True)
            p = jnp.sum(do_ * vd, axis=1, keepdims=True) * e
            dq = dq + p * kd
            dkd = p * q
            dbb = dkd * kd
            dv = dv + pltpu.roll(a * do_, up, 0)
            dk = dk + pltpu.roll(dkd, up, 0)
            db = db + dbb - pltpu.roll(dbb, up, 0)
        dq_a[...], dk_a[...], dv_a[...], db_a[...] = dq, dk, dv, db
        vb = v.astype(BF16)
        for i in range(1, L // HG_SUB):
            r0 = i * HG_SUB
            blk = slice(r0, r0 + HG_SUB)
            qt, kt, e1, e2 = _hgrn_offdiag(q, k, b, r0)
            qtb, ktb, dob_i = qt.astype(BF16), kt.astype(BF16), do_[blk, :].astype(BF16)
            a = _nt(qtb, ktb).astype(BF16)
            da = _nt(dob_i, vb).astype(BF16)
            dv_a[...] += _tn(a, dob_i)
            dqt = _nn(da, ktb)
            dkt = _tn(da, qtb)
            dq_a[blk, :] += dqt * e1
            t1, t2 = dqt * qt, dkt * kt
            db_a[blk, :] += t1
            dk_a[...] += dkt * e2
            db_a[...] -= t2
            db_a[r0 - 1:r0, :] += jnp.sum(t2, axis=0, keepdims=True) - jnp.sum(t1, axis=0, keepdims=True)
        dq, dk, dv, db = dq_a[...], dk_a[...], dv_a[...], db_a[...]
        dstn = dst_s[hd]
        dstnb = dstn.astype(BF16)
        bl = b[L - 1:L, :]
        ebl = jnp.exp(bl)
        kdec_e = jnp.exp(bl - b)
        kdec = k * kdec_e
        dbl = jnp.sum(dstn * st, axis=0, keepdims=True) * ebl
        dv = dv + _nt(kdec.astype(BF16), dstnb)
        dkdec = _nn(v.astype(BF16), dstnb)
        dk = dk + dkdec * kdec_e
        dx = dkdec * kdec
        dbl = dbl + jnp.sum(dx, axis=0, keepdims=True)
        db = db - dx + jnp.where(rows == L - 1, dbl, 0.0)
        dst_s[hd] = dstn * ebl + dst_o
        dg = _rcumsum_rows(db)
        dfk = dg / f - dk
        dq_ref[:, sl] = (dq * _dsilu(qp)).astype(BF16)
        di_ref[:, sl] = (dv * _dsilu(ip)).astype(BF16)
        df_ref[:, sl] = (dfk * (1.0 - lb) * sg * (1.0 - sg)).astype(BF16)
        dlb_s[hd] += jnp.sum(dfk * (1.0 - sg), axis=0, keepdims=True)

        @pl.when(pl.program_id(0) == nc - 1)
        def _():
            dl0 = dlb_s[hd] * lb * (1.0 - lb)
            dl_ref[0:1, sl] = dl0
            dl_ref[1:2, sl] = -dl0

    rv = lambda j: nc - 1 - j
    hspec = lambda blk: pl.BlockSpec((L, 512), lambda j: (rv(j), blk))
    return _pc(body, name, (nc,),
               [hspec(4), hspec(5), hspec(6), pl.BlockSpec((2, 512), lambda j: (0, 0)),
                pl.BlockSpec((nh, None, HD, HD), lambda j: (0, rv(j), 0, 0)), hspec(0)],
               [hspec(0), hspec(0), hspec(0), pl.BlockSpec((2, 512), lambda j: (0, 0))],
               [_sds((t, 512), BF16), _sds((t, 512), BF16), _sds((t, 512), BF16), _sds((2, 512))],
               scratch=[pltpu.VMEM((nh, HD, HD), F32), pltpu.VMEM((nh, 1, HD), F32)] + [pltpu.VMEM((nh, L, HD), F32)] * 4)(z0, z0, z0, lbl, ss, do)


def _post0_fwd(hm, hh, z0, na, nb, w, h0, name, bm=512):
    t = h0.shape[0]
    bm = min(bm, t)

    def body(hm_ref, hh_ref, oa_ref, gb_ref, na_ref, nb_ref, w_ref, h_ref, o_ref, y_ref):
        for hd in range(4):
            sl = slice(hd * HD, (hd + 1) * HD)
            pa = _sigmoid(oa_ref[:, sl]) * hm_ref[:, sl]
            y_ref[:, sl] = (pa * _rstd(pa) * na_ref[:, sl]).astype(BF16)
            xb = hh_ref[:, sl]
            y_ref[:, 512 + hd * HD:512 + (hd + 1) * HD] = (xb * _rstd(xb) * nb_ref[:, sl] * _silu(gb_ref[:, sl])).astype(BF16)
        o_ref[...] = h_ref[...] + _nn(y_ref[...], w_ref[...])

    row = lambda wd, c: pl.BlockSpec((bm, wd), lambda i: (i, c))
    vec = lambda wd: pl.BlockSpec((1, wd), lambda i: (0, 0))
    return _pc(body, name, (t // bm,),
               [row(512, 0), row(512, 0), row(512, 3), row(512, 7), vec(512), vec(512),
                pl.BlockSpec((D, D), lambda i: (0, 0)), row(D, 0)],
               [row(D, 0), row(D, 0)], [_sds((t, D)), _sds((t, D), BF16)])(hm, hh, z0, z0, na, nb, w, h0)


def _post0_bwd(dh1, w, hm, hh, z0, na, nb, name, bm=512):
    t = dh1.shape[0]
    bm = min(bm, t)

    def body(dh_ref, w_ref, hm_ref, hh_ref, oa_ref, gb_ref, na_ref, nb_ref, dhm_ref, dhh_ref, doa_ref, dgb_ref, dna_ref, dnb_ref):
        @pl.when(pl.program_id(0) == 0)
        def _():
            dna_ref[...] = jnp.zeros_like(dna_ref)
            dnb_ref[...] = jnp.zeros_like(dnb_ref)

        dy = _nt(dh_ref[...].astype(BF16), w_ref[...])
        for hd in range(4):
            sl = slice(hd * HD, (hd + 1) * HD)
            hm_, oa = hm_ref[:, sl], oa_ref[:, sl]
            sg = _sigmoid(oa)
            dpa, dgr = _rms_bwd(dy[:, sl], sg * hm_, na_ref[:, sl])
            dna_ref[:, sl] += jnp.sum(dgr, axis=0, keepdims=True)
            doa_ref[:, sl] = (dpa * hm_ * sg * (1.0 - sg)).astype(BF16)
            dhm_ref[:, sl] = dpa * sg
            xb, gb, nbv = hh_ref[:, sl], gb_ref[:, sl], nb_ref[:, sl]
            dyb = dy[:, 512 + hd * HD:512 + (hd + 1) * HD]
            dgb_ref[:, sl] = (dyb * (xb * _rstd(xb) * nbv) * _dsilu(gb)).astype(BF16)
            dxb, dgr2 = _rms_bwd(dyb * _silu(gb), xb, nbv)
            dnb_ref[:, sl] += jnp.sum(dgr2, axis=0, keepdims=True)
            dhh_ref[:, sl] = dxb

    row = lambda wd, c: pl.BlockSpec((bm, wd), lambda i: (i, c))
    vec = lambda wd: pl.BlockSpec((1, wd), lambda i: (0, 0))
    return _pc(body, name, (t // bm,),
               [row(D, 0), pl.BlockSpec((D, D), lambda i: (0, 0)), row(512, 0), row(512, 0), row(512, 3), row(512, 7),
                vec(512), vec(512)],
               [row(512, 0), row(512, 0), row(512, 0), row(512, 0), vec(512), vec(512)],
               [_sds((t, 512)), _sds((t, 512)), _sds((t, 512), BF16), _sds((t, 512), BF16), _sds((1, 512)), _sds((1, 512))],
               )(dh1, w, hm, hh, z0, z0, na, nb)


def _memkv_fwd(mem, g, wkv_s, name):
    m = mem.shape[0]

    def body(x_ref, g_ref, w_ref, kv_ref, mn_ref):
        x = x_ref[...]
        mn = (x * _rstd(x) * g_ref[...]).astype(BF16)
        mn_ref[...] = mn
        kv_ref[...] = _nn(mn, w_ref[...])

    return _pc(body, name, (4,),
               [pl.BlockSpec((m, D), lambda k: (0, 0)), pl.BlockSpec((1, D), lambda k: (0, 0)),
                pl.BlockSpec((None, D, 512), lambda k: (k, 0, 0))],
               [pl.BlockSpec((m, 512), lambda k: (0, k)), pl.BlockSpec((m, D), lambda k: (0, 0))],
               [_sds((m, 2048)), _sds((m, D), BF16)])(mem, g, wkv_s)


def _memkv_bwd(dkv, wkv_s, mem, g, name):
    m = mem.shape[0]

    def body(d_ref, w_ref, x_ref, g_ref, dg_ref, acc):
        k = pl.program_id(0)

        @pl.when(k == 0)
        def _():
            acc[...] = jnp.zeros_like(acc)

        acc[...] += _nt(d_ref[...].astype(BF16), w_ref[...])

        @pl.when(k == 3)
        def _():
            _, dgr = _rms_bwd(acc[...], x_ref[...], g_ref[...])
            dg_ref[...] = jnp.sum(dgr, axis=0, keepdims=True)

    return _pc(body, name, (4,),
               [pl.BlockSpec((m, 512), lambda k: (0, k)), pl.BlockSpec((None, D, 512), lambda k: (k, 0, 0)),
                pl.BlockSpec((m, D), lambda k: (0, 0)), pl.BlockSpec((1, D), lambda k: (0, 0))],
               pl.BlockSpec((1, D), lambda k: (0, 0)), _sds((1, D)), scratch=[pltpu.VMEM((m, D), F32)])(dkv, wkv_s, mem, g)


def _xattn_probs(qh, kh):
    s = _nt(qh, kh) * (XD ** -0.5)
    p = jnp.exp(s - jnp.max(s, axis=1, keepdims=True))
    return p / jnp.sum(p, axis=1, keepdims=True)


def _xattn_fwd(q, kv, wo, h1, name, bm=512):
    t, m = q.shape[0], kv.shape[0]
    bm = min(bm, t)

    def body(q_ref, k_ref, v_ref, w_ref, h_ref, out_ref, o_ref):
        for hd in range(D // XD):
            sl = slice(hd * XD, (hd + 1) * XD)
            p = _xattn_probs(q_ref[:, sl].astype(BF16), k_ref[:, sl].astype(BF16))
            o_ref[:, sl] = _nn(p.astype(BF16), v_ref[:, sl].astype(BF16)).astype(BF16)
        out_ref[...] = h_ref[...] + _nn(o_ref[...], w_ref[...])

    row = pl.BlockSpec((bm, D), lambda i: (i, 0))
    return _pc(body, name, (t // bm,),
               [row, pl.BlockSpec((m, D), lambda i: (0, 0)), pl.BlockSpec((m, D), lambda i: (0, 1)),
                pl.BlockSpec((D, D), lambda i: (0, 0)), row],
               [row, row], [_sds((t, D)), _sds((t, D), BF16)])(q, kv, kv, wo, h1)


def _xattn_bwd(dh2, q, kv, wo, name, bm=512):
    t, m = q.shape[0], kv.shape[0]
    bm = min(bm, t)

    def body(dh_ref, q_ref, k_ref, v_ref, w_ref, dq_ref, dkv_ref):
        @pl.when(pl.program_id(0) == 0)
        def _():
            dkv_ref[...] = jnp.zeros_like(dkv_ref)

        d_o = _nt(dh_ref[...].astype(BF16), w_ref[...])
        for hd in range(D // XD):
            sl = slice(hd * XD, (hd + 1) * XD)
            qh, kh, vh = q_ref[:, sl].astype(BF16), k_ref[:, sl].astype(BF16), v_ref[:, sl].astype(BF16)
            p = _xattn_probs(qh, kh)
            dob = d_o[:, sl].astype(BF16)
            dp = _nt(dob, vh)
            dkv_ref[:, D + hd * XD:D + (hd + 1) * XD] += _tn(p.astype(BF16), dob)
            ds = (p * (dp - jnp.sum(dp * p, axis=1, keepdims=True)) * (XD ** -0.5)).astype(BF16)
            dq_ref[:, sl] = _nn(ds, kh).astype(BF16)
            dkv_ref[:, sl] += _tn(ds, qh)

    row = pl.BlockSpec((bm, D), lambda i: (i, 0))
    return _pc(body, name, (t // bm,),
               [row, row, pl.BlockSpec((m, D), lambda i: (0, 0)), pl.BlockSpec((m, D), lambda i: (0, 1)),
                pl.BlockSpec((D, D), lambda i: (0, 0))],
               [row, pl.BlockSpec((m, 2 * D), lambda i: (0, 0))],
               [_sds((t, D), BF16), _sds((m, 2 * D))])(dh2, q, kv, kv, wo)


NH1 = 8
FOX_BM = 512
FOX_BQ = 512
FOX_BK = 512
FOX_HEADS_PER_STEP = 2


def _foxprep_fwd(z1, qg, kg, fbp, name):
    t = z1.shape[0]
    bm = min(FOX_BM, t)

    def body(q_ref, k_ref, v_ref, f_ref, qg_ref, kg_ref, fb_ref, qn_ref, kn_ref, vb_ref, c_ref, carry):
        @pl.when(pl.program_id(0) == 0)
        def _():
            carry[...] = jnp.zeros_like(carry)

        for hd in range(NH1):
            sl = slice(hd * HD, (hd + 1) * HD)
            x = q_ref[:, sl]
            qn_ref[:, sl] = (x * _rstd(x) * qg_ref[...] * FOX_QSCALE).astype(BF16)
            x = k_ref[:, sl]
            kn_ref[:, sl] = (x * _rstd(x) * kg_ref[...]).astype(BF16)
        vb_ref[...] = v_ref[...].astype(BF16)
        c = carry[...] + _cumsum_rows(_log_sigmoid(f_ref[...] + fb_ref[...]))
        c_ref[...] = c
        carry[...] = c[bm - 1:bm, :]

    row = lambda c: pl.BlockSpec((bm, D), lambda i: (i, c))
    lane = pl.BlockSpec((bm, HD), lambda i: (i, 4 * D // HD))
    vec = pl.BlockSpec((1, HD), lambda i: (0, 0))
    return _pc(body, name, (t // bm,), [row(0), row(1), row(2), lane, vec, vec, vec],
               [row(0), row(0), row(0), pl.BlockSpec((bm, HD), lambda i: (i, 0))],
               [_sds((t, D), BF16), _sds((t, D), BF16), _sds((t, D), BF16), _sds((t, HD))],
               scratch=[pltpu.VMEM((1, HD), F32)])(z1, z1, z1, z1, qg, kg, fbp)


def _foxprep_bwd(dqn, dkn, z1, qg, kg, fbp, dc, name):
    t = z1.shape[0]
    bm = min(FOX_BM, t)
    nb = t // bm

    def body(dqn_ref, dkn_ref, q_ref, k_ref, f_ref, qg_ref, kg_ref, fb_ref, dc_ref,
             dq_ref, dk_ref, df_ref, dqg_ref, dkg_ref, dfb_ref, carry):
        @pl.when(pl.program_id(0) == 0)
        def _():
            carry[...] = jnp.zeros_like(carry)
            dqg_ref[...] = jnp.zeros_like(dqg_ref)
            dkg_ref[...] = jnp.zeros_like(dkg_ref)
            dfb_ref[...] = jnp.zeros_like(dfb_ref)

        for hd in range(NH1):
            sl = slice(hd * HD, (hd + 1) * HD)
            dx, dgr = _rms_bwd(dqn_ref[:, sl] * (HD ** -0.5), q_ref[:, sl], qg_ref[...])
            dq_ref[:, sl] = dx.astype(BF16)
            dqg_ref[...] += jnp.sum(dgr, axis=0, keepdims=True)
            dx, dgr = _rms_bwd(dkn_ref[:, sl], k_ref[:, sl], kg_ref[...])
            dk_ref[:, sl] = dx.astype(BF16)
            dkg_ref[...] += jnp.sum(dgr, axis=0, keepdims=True)
        dc_ = dc_ref[...]
        dlogf = _rcumsum_rows(dc_) + carry[...]
        carry[...] += jnp.sum(dc_, axis=0, keepdims=True)
        lanes = lax.broadcasted_iota(jnp.int32, dc_.shape, 1)
        df = jnp.where(lanes < NH1, dlogf * (1.0 - _sigmoid(f_ref[...] + fb_ref[...])), 0.0)
        df_ref[...] = df.astype(BF16)
        dfb_ref[...] += jnp.sum(df, axis=0, keepdims=True)

    rv = lambda i: nb - 1 - i
    row = lambda c: pl.BlockSpec((bm, D), lambda i: (rv(i), c))
    lane = lambda c: pl.BlockSpec((bm, HD), lambda i: (rv(i), c))
    vec = pl.BlockSpec((1, HD), lambda i: (0, 0))
    return _pc(body, name, (nb,), [row(0), row(0), row(0), row(1), lane(4 * D // HD), vec, vec, vec, lane(0)],
               [row(0), row(0), lane(0), vec, vec, vec],
               [_sds((t, D), BF16), _sds((t, D), BF16), _sds((t, HD), BF16), _sds((1, HD)), _sds((1, HD)), _sds((1, HD))],
               scratch=[pltpu.VMEM((1, HD), F32)])(dqn, dkn, z1, z1, z1, qg, kg, fbp, dc)


LOG2E = 1.4426950408889634
FOX_QSCALE = HD ** -0.5 * LOG2E


def _fox_steps(t, bq, bk, k_major):
    nq, nk = t // bq, t // bk
    pairs = [(i, j) for i in range(nq) for j in range(nk) if j * bk < (i + 1) * bq]
    if k_major:
        pairs.sort(key=lambda p: (p[1], p[0]))
    outer = [p[1] if k_major else p[0] for p in pairs]
    n = len(pairs)
    flags = [(n_ == 0 or outer[n_] != outer[n_ - 1]) + 2 * (n_ == n - 1 or outer[n_] != outer[n_ + 1])
             + 4 * (not (j + 1) * bk <= i * bq + 1) for n_, (i, j) in enumerate(pairs)]
    as_i32 = lambda v: jnp.asarray(v, jnp.int32)
    return as_i32([p[0] for p in pairs]), as_i32([p[1] for p in pairs]), as_i32(flags)


def _fox_step_info(qi_ref, kj_ref, fl_ref):
    s = pl.program_id(1)
    fl = fl_ref[s]
    return qi_ref[s], kj_ref[s], (fl & 1) != 0, (fl & 2) != 0, (fl & 4) != 0


def _fox_call(body, name, tables, in_specs, out_specs, out_shape, scratch):
    grid_spec = pltpu.PrefetchScalarGridSpec(num_scalar_prefetch=3, grid=(NH1 // FOX_HEADS_PER_STEP, tables[0].shape[0]),
                                             in_specs=in_specs, out_specs=out_specs, scratch_shapes=scratch)
    return pl.pallas_call(body, name=name, grid_spec=grid_spec, out_shape=out_shape,
                          compiler_params=pltpu.CompilerParams(dimension_semantics=("arbitrary", "arbitrary"),
                                                               vmem_limit_bytes=VMEM_LIMIT_V7X))


def _fox_lane_tiles(x):
    return [x[:, c0:c0 + HD] for c0 in range(0, x.shape[1], HD)]


def _fox_masked_scores(q, k, ck, i, j, bq, bk, masked):
    s = _nt(q, k) - ck
    if masked:
        rows = i * bq + lax.broadcasted_iota(jnp.int32, s.shape, 0)
        cols = j * bk + lax.broadcasted_iota(jnp.int32, s.shape, 1)
        s = jnp.where(cols <= rows, s, NEG)
    return s


def _fox_specs(bq, bk, G):
    qspec = pl.BlockSpec((bq, G * HD), lambda h, s, qi, kj, fl: (qi[s], h))
    kspec = pl.BlockSpec((bk, G * HD), lambda h, s, qi, kj, fl: (kj[s], h))
    cspec = pl.BlockSpec((G, 1, bk), lambda h, s, qi, kj, fl: (h, 0, kj[s]))
    colspec = pl.BlockSpec((G, bq, 1), lambda h, s, qi, kj, fl: (h, qi[s], 0))
    return qspec, kspec, cspec, colspec


def _fox_rowmax(qn, kn, crow, name):
    t = qn.shape[0]
    bq, bk, G = min(FOX_BQ, t), min(FOX_BK, t), FOX_HEADS_PER_STEP
    tables = _fox_steps(t, bq, bk, k_major=False)

    def body(qi_ref, kj_ref, fl_ref, q_ref, k_ref, ck_ref, m_ref, *mp):
        i, j, first, last, diag = _fox_step_info(qi_ref, kj_ref, fl_ref)

        @pl.when(first)
        def _():
            for g in range(G):
                mp[g][...] = jnp.full_like(mp[g], NEG)

        def step(masked):
            for g in range(G):
                sl = slice(g * HD, (g + 1) * HD)
                s = _fox_masked_scores(q_ref[:, sl], k_ref[:, sl], ck_ref[g], i, j, bq, bk, masked)
                m = mp[g][...]
                for tile in _fox_lane_tiles(s):
                    m = jnp.maximum(m, tile)
                mp[g][...] = m

        pl.when(jnp.logical_not(diag))(lambda: step(False))
        pl.when(diag)(lambda: step(True))

        @pl.when(last)
        def _():
            for g in range(G):
                m_ref[g] = jnp.max(mp[g][...], axis=1, keepdims=True)

    qspec, kspec, cspec, colspec = _fox_specs(bq, bk, G)
    return _fox_call(body, name, tables, [qspec, kspec, cspec], colspec, _sds((NH1, t, 1)),
                     [pltpu.VMEM((bq, HD), F32)] * G)(*tables, qn, kn, crow)


def _fox_fwd(qn, kn, vb, crow, m, name):
    t = qn.shape[0]
    bq, bk, G = min(FOX_BQ, t), min(FOX_BK, t), FOX_HEADS_PER_STEP
    tables = _fox_steps(t, bq, bk, k_major=False)

    def body(qi_ref, kj_ref, fl_ref, q_ref, k_ref, v_ref, ck_ref, m_ref, o_ref, lse_ref, *scr):
        i, j, first, last, diag = _fox_step_info(qi_ref, kj_ref, fl_ref)
        lp, acc = scr[:G], scr[G:]

        @pl.when(first)
        def _():
            for g in range(G):
                lp[g][...] = jnp.zeros_like(lp[g])
                acc[g][...] = jnp.zeros_like(acc[g])

        def step(masked):
            for g in range(G):
                sl = slice(g * HD, (g + 1) * HD)
                s = _fox_masked_scores(q_ref[:, sl], k_ref[:, sl], ck_ref[g], i, j, bq, bk, masked)
                p = jnp.exp2(s - m_ref[g])
                l = lp[g][...]
                for tile in _fox_lane_tiles(p):
                    l = l + tile
                lp[g][...] = l
                acc[g][...] += _nn(p.astype(BF16), v_ref[:, sl])

        pl.when(jnp.logical_not(diag))(lambda: step(False))
        pl.when(diag)(lambda: step(True))

        @pl.when(last)
        def _():
            for g in range(G):
                l = jnp.sum(lp[g][...], axis=1, keepdims=True)
                o_ref[:, g * HD:(g + 1) * HD] = acc[g][...] / l
                lse_ref[g] = m_ref[g] + jnp.log2(l)

    qspec, kspec, cspec, colspec = _fox_specs(bq, bk, G)
    return _fox_call(body, name, tables, [qspec, kspec, kspec, cspec, colspec], [qspec, colspec],
                     [_sds((t, D)), _sds((NH1, t, 1))], [pltpu.VMEM((bq, HD), F32)] * (2 * G))(*tables, qn, kn, vb, crow, m)


def _fox_bwd(qn, kn, vb, crow, lse, delta, do, name):
    t = qn.shape[0]
    bq, bk, G = min(FOX_BQ, t), min(FOX_BK, t), FOX_HEADS_PER_STEP
    tables = _fox_steps(t, bq, bk, k_major=True)

    def body(qi_ref, kj_ref, fl_ref, q_ref, k_ref, v_ref, ck_ref, lse_ref, dl_ref, do_ref, dq_ref, dk_ref, dv_ref, dc_ref, dcq_ref,
             dk_s, dv_s, dc_s):
        i, j, first, last, diag = _fox_step_info(qi_ref, kj_ref, fl_ref)

        @pl.when(first)
        def _():
            dk_s[...] = jnp.zeros_like(dk_s)
            dv_s[...] = jnp.zeros_like(dv_s)
            dc_s[...] = jnp.zeros_like(dc_s)

        @pl.when(pl.program_id(1) == 0)
        def _():
            dq_ref[...] = jnp.zeros_like(dq_ref)
            dcq_ref[...] = jnp.zeros_like(dcq_ref)

        def step(masked):
            rows = pl.ds(pl.multiple_of(i * bq, bq), bq)
            for g in range(G):
                sl = slice(g * HD, (g + 1) * HD)
                q, k = q_ref[:, sl], k_ref[:, sl]
                s = _fox_masked_scores(q, k, ck_ref[g], i, j, bq, bk, masked)
                p = jnp.exp2(s - lse_ref[g])
                dob = do_ref[:, sl]
                dv_s[:, sl] += _tn(p.astype(BF16), dob)
                ds = p * (_nt(dob, v_ref[:, sl]) - dl_ref[g])
                dsb = ds.astype(BF16)
                dq_ref[rows, sl] += _nn(dsb, k)
                dk_s[:, sl] += _tn(dsb, q)
                dc_s[g] -= jnp.sum(ds, axis=0, keepdims=True)
                part_sum = dcq_ref[g, rows, :]
                for tile in _fox_lane_tiles(ds):
                    part_sum = part_sum + tile
                dcq_ref[g, rows, :] = part_sum

        pl.when(jnp.logical_not(diag))(lambda: step(False))
        pl.when(diag)(lambda: step(True))

        @pl.when(last)
        def _():
            dk_ref[...] = dk_s[...] * (1.0 / LOG2E)
            dv_ref[...] = dv_s[...]
            dc_ref[...] = dc_s[...]

    qspec, kspec, cspec, colspec = _fox_specs(bq, bk, G)
    return _fox_call(
        body, name, tables, [qspec, kspec, kspec, cspec, colspec, colspec, qspec],
        [pl.BlockSpec((t, G * HD), lambda h, s, qi, kj, fl: (0, h)), kspec, kspec, cspec,
         pl.BlockSpec((G, t, HD), lambda h, s, qi, kj, fl: (h, 0, 0))],
        [_sds((t, D)), _sds((t, D)), _sds((t, D)), _sds((NH1, 1, t)), _sds((NH1, t, HD))],
        [pltpu.VMEM((bk, G * HD), F32), pltpu.VMEM((bk, G * HD), F32), pltpu.VMEM((G, 1, bk), F32)],
    )(*tables, qn, kn, vb, crow, lse, delta, do)


def _post1_fwd(o, z1, w, h3, name, bm=512):
    t = o.shape[0]
    bm = min(bm, t)

    def body(o_ref, g_ref, w_ref, h_ref, out_ref, og_ref):
        og_ref[...] = (o_ref[...] * _sigmoid(g_ref[...])).astype(BF16)
        out_ref[...] = h_ref[...] + _nn(og_ref[...], w_ref[...])

    row = lambda c: pl.BlockSpec((bm, D), lambda i: (i, c))
    return _pc(body, name, (t // bm,), [row(0), row(3), pl.BlockSpec((D, D), lambda i: (0, 0)), row(0)],
               [row(0), row(0)], [_sds((t, D)), _sds((t, D), BF16)])(o, z1, w, h3)


def _post1_bwd(dh4, w, o, z1, name, bm=512):
    t = o.shape[0]
    bm = min(bm, t)

    def body(dh_ref, w_ref, o_ref, g_ref, do_ref, dg_ref, dl_ref):
        d_og = _nt(dh_ref[...].astype(BF16), w_ref[...])
        o_, sg = o_ref[...], _sigmoid(g_ref[...])
        dob = (d_og * sg).astype(BF16)
        do_ref[...] = dob
        dg_ref[...] = (d_og * o_ * sg * (1.0 - sg)).astype(BF16)
        prod = dob.astype(F32) * o_
        for hd in range(NH1):
            dl_ref[hd] = jnp.sum(prod[:, hd * HD:(hd + 1) * HD], axis=1, keepdims=True)

    row = lambda c: pl.BlockSpec((bm, D), lambda i: (i, c))
    return _pc(body, name, (t // bm,), [row(0), pl.BlockSpec((D, D), lambda i: (0, 0)), row(0), row(3)],
               [row(0), row(0), pl.BlockSpec((NH1, bm, 1), lambda i: (0, i, 0))],
               [_sds((t, D), BF16), _sds((t, D), BF16), _sds((NH1, t, 1))])(dh4, w, o, z1)


def _final(h, g, tgt, name, bm=512):
    t = h.shape[0]
    bm = min(bm, t)

    def body(h_ref, g_ref, t_ref, l_ref, dh_ref, dg_ref):
        @pl.when(pl.program_id(0) == 0)
        def _():
            l_ref[...] = jnp.zeros_like(l_ref)
            dg_ref[...] = jnp.zeros_like(dg_ref)

        x, gv = h_ref[...], g_ref[...]
        r = _rstd(x)
        xh = x * r
        e = xh * gv - t_ref[...]
        l_ref[...] += 0.5 * jnp.sum(jnp.mean(e * e, axis=1, keepdims=True), axis=0, keepdims=True)
        dy = e * (1.0 / D)
        dg_ref[...] += jnp.sum(dy * xh, axis=0, keepdims=True)
        dxh = dy * gv
        dh_ref[...] = r * (dxh - xh * jnp.mean(dxh * xh, axis=1, keepdims=True))

    row = pl.BlockSpec((bm, D), lambda i: (i, 0))
    vec = pl.BlockSpec((1, D), lambda i: (0, 0))
    return _pc(body, name, (t // bm,), [row, vec, row], [pl.BlockSpec((1, HD), lambda i: (0, 0)), row, vec],
               [_sds((1, HD)), _sds((t, D)), _sds((1, D))])(h, g, tgt)


def _adam(w, g, m, v, name):
    r, c = w.shape
    br = min(r, 256)

    def body(w_ref, g_ref, m_ref, v_ref, d_ref, mo_ref, vo_ref):
        gv = g_ref[...]
        mn = ADAM_B1 * m_ref[...] + (1.0 - ADAM_B1) * gv
        vn = ADAM_B2 * v_ref[...] + (1.0 - ADAM_B2) * jnp.square(gv)
        m_hat = mn / (1.0 - ADAM_B1 ** ADAM_STEP)
        v_hat = vn / (1.0 - ADAM_B2 ** ADAM_STEP)
        d_ref[...] = -ADAM_LR * (m_hat / (jnp.sqrt(v_hat) + ADAM_EPS) + ADAM_WD * w_ref[...])
        mo_ref[...] = mn
        vo_ref[...] = vn

    blk = pl.BlockSpec((br, c), lambda i: (i, 0))
    return _pc(body, name, (r // br,), [blk] * 4, [blk] * 3, [_sds((r, c))] * 3)(w, g, m, v)


ZW = 4224
GATE0 = 4096


def _pack_w_in0(w):
    return jnp.concatenate([w[:, :2048], w[:, 2056:], w[:, 2048:2056], jnp.zeros((w.shape[0], ZW - 4104), w.dtype)], axis=1)


def _unpack_w_in0(g):
    return jnp.concatenate([g[:, :2048], g[:, GATE0:GATE0 + 8], g[:, 2048:GATE0]], axis=1)


def _pack_w_in1(w):
    return jnp.concatenate([w, jnp.zeros((w.shape[0], ZW - 4104), w.dtype)], axis=1)


def _unpack_w_in1(g):
    return g[:, :4104]


def _local_step(x, mem, tgt, W, S, late_weights=None, grads_hook=None):
    t = x.shape[0]
    row = lambda v: v.reshape(1, -1)
    G = {}

    z0, u0 = _norm_mm(x, S["norm_mix_g"][0:1], W["w_in0"], "in0_fwd")
    qk = _conv_fwd(z0, S["conv_w"], "conv_fwd")
    g8 = z0[:, GATE0:GATE0 + 8]
    gates3 = jnp.stack([g8[:, :4].T, g8[:, 4:].T], axis=-1)
    gb = S["gate_b"]
    bias3 = jnp.stack([gb[0, :4], gb[0, 4:]], axis=-1)[:, None, :]
    hm, cs, ns, ms = _mlstm_fwd(qk, z0, gates3, bias3, "mlstm_fwd")
    hh, ss = _hgrn_fwd(z0, S["lb_logits"], "hgrn_fwd")
    if late_weights is not None:
        W = {**W, **late_weights(hh)}
    kv, mn = _memkv_fwd(mem, row(S["mem_norm_g"]), W["wkv_s"], "memkv_fwd")
    h1, y0 = _post0_fwd(hm, hh, z0, S["mlstm_norm_g"], S["hgrn_norm_g"], W["w_out0"], x, "post0_fwd")

    def xattn_mlp_fwd(h, l):
        q, ux = _norm_mm(h, S["norm_xattn_g"][l:l + 1], W["wq"][l], f"xq{l}_fwd")
        h2, ox = _xattn_fwd(q, kv, W["wo"][l], h, f"xattn{l}_fwd")
        h3, a, um = _mlp_fwd(h2, S["norm_mlp_g"][l:l + 1], W["w1s"], W["w2"], l, f"mlp{l}_fwd")
        return h3, (h, q, ux, ox, h2, a, um)

    h3, sv0 = xattn_mlp_fwd(h1, 0)
    z1, u1 = _norm_mm(h3, S["norm_mix_g"][1:2], W["w_in1"], "in1_fwd")
    fbp = jnp.pad(S["c_fgate_b"], ((0, 0), (0, HD - NH1)))
    qn, kn, vb, c = _foxprep_fwd(z1, S["c_qnorm_g"], S["c_knorm_g"], fbp, "foxprep_fwd")
    crow = (c[:, :NH1] * LOG2E).T[:, None, :]
    o1, lse = _fox_fwd(qn, kn, vb, crow, _fox_rowmax(qn, kn, crow, "fox_rowmax"), "fox_fwd")
    h4, og = _post1_fwd(o1, z1, W["w_out1"], h3, "post1_fwd")
    h6, sv1 = xattn_mlp_fwd(h4, 1)
    lossp, dh, G["final_norm_g"] = _final(h6, row(S["final_norm_g"]), tgt, "final")

    grads_ready = grads_hook if grads_hook is not None else (lambda stage, grads: 0.0)
    dkv = None
    dgx, dgm, dwq, dwo, dw1, dw2 = [None, None], [None, None], [None, None], [None, None], [None, None], [None, None]

    def xattn_mlp_bwd(dh, l, sv):
        nonlocal dkv
        h, q, ux, ox, h2, a, um = sv
        dh2, da, r, dgm[l] = _mlp_bwd(dh, a, W["w1s"], W["w2"], l, h2, S["norm_mlp_g"][l:l + 1], f"mlp{l}_bwd")
        dw1[l] = _mm_tn(um, da, f"mlp{l}_dw1", col_chips=NCHIP)
        dw2[l] = _mm_tn(r, dh, f"mlp{l}_dw2")
        dq, dkv_l = _xattn_bwd(dh2, q, kv, W["wo"][l], f"xattn{l}_bwd")
        dkv = dkv_l if dkv is None else dkv + dkv_l
        dwo[l] = _mm_tn(ox, dh2, f"xattn{l}_dwo")
        dwq[l] = _mm_tn(ux, dq, f"xattn{l}_dwq")
        tok = grads_ready("layer0_mlp_xattn", dict(wq=dwq[0], wo=dwo[0], w1=dw1[0], w2=dw2[0])) if l == 0 else 0.0
        dh1, dgx[l] = _bwd_in(dq, W["wq"][l], h, S["norm_xattn_g"][l:l + 1] + tok, dh2, f"xq{l}_bwd")
        return dh1

    dh4 = xattn_mlp_bwd(dh, 1, sv1)
    do, dgate, delta = _post1_bwd(dh4, W["w_out1"], o1, z1, "post1_bwd")
    G["w_out1"] = _mm_tn(og, dh4, "post1_dw")
    dqn, dkn, dv1, dcrow, dcq = _fox_bwd(qn, kn, vb, crow, lse, delta, do, "fox_bwd")
    dc = jnp.pad((dcrow[:, 0, :] + jnp.sum(dcq, axis=-1)).T, ((0, 0), (0, HD - NH1)))
    dqr, dkr, df1, G["c_qnorm_g"], G["c_knorm_g"], dfb = _foxprep_bwd(
        dqn, dkn, z1, S["c_qnorm_g"], S["c_knorm_g"], fbp, dc, "foxprep_bwd")
    G["c_fgate_b"] = dfb[:, :NH1]
    dz1 = jnp.concatenate([dqr, dkr, dv1.astype(BF16), dgate, df1], axis=1)
    G["w_in1"] = _mm_tn(u1, dz1, "in1_dw")
    tok = grads_ready("layer1", dict(w_out=G["w_out1"], w_in=G["w_in1"], wq=dwq[1], wo=dwo[1], w1=dw1[1], w2=dw2[1]))
    dh3, dgmix1 = _bwd_in(dz1, W["w_in1"], h3, S["norm_mix_g"][1:2] + tok, dh4, "in1_bwd")
    dh1 = xattn_mlp_bwd(dh3, 0, sv0)

    dhm, dhh, doa, dgb, G["mlstm_norm_g"], G["hgrn_norm_g"] = _post0_bwd(
        dh1, W["w_out0"], hm, hh, z0, S["mlstm_norm_g"], S["hgrn_norm_g"], "post0_bwd")
    G["w_out0"] = _mm_tn(y0, dh1, "post0_dw")
    dqa, dka, dva, dgates3 = _mlstm_bwd(qk, z0, gates3, bias3, cs, ns, ms, dhm, "mlstm_bwd")
    dqb, dfb0, dib, G["lb_logits"] = _hgrn_bwd(z0, S["lb_logits"], ss, dhh, "hgrn_bwd")
    duc, G["conv_w"] = _conv_bwd(z0, S["conv_w"], jnp.concatenate([dqa, dka], axis=1), "conv_bwd")
    dg8 = jnp.concatenate([dgates3[:, :, 0].T, dgates3[:, :, 1].T], axis=1)
    G["gate_b"] = jnp.sum(dg8, axis=0, keepdims=True)
    dz0 = jnp.concatenate([duc, dva.astype(BF16), doa, dqb, dfb0, dib, dgb,
                           jnp.pad(dg8, ((0, 0), (0, HD - 8))).astype(BF16)], axis=1)
    G["w_in0"] = _mm_tn(u0, dz0, "in0_dw")
    dx, dgmix0 = _bwd_in(dz0, W["w_in0"], x, S["norm_mix_g"][0:1], dh1, "in0_bwd")

    G["wkv"] = _mm_tn(mn, dkv, "memkv_dw", col_chips=NCHIP)
    G["mem_norm_g"] = _memkv_bwd(dkv, W["wkv_s"], mem, row(S["mem_norm_g"]), "memkv_bwd")
    G["norm_mix_g"] = jnp.concatenate([dgmix0, dgmix1], axis=0)
    G["norm_xattn_g"] = jnp.concatenate(dgx, axis=0)
    G["norm_mlp_g"] = jnp.concatenate(dgm, axis=0)
    G["wq"], G["wo"], G["w1"], G["w2"] = dwq, dwo, dw1, dw2
    return lossp[0, 0], dx, G


ANY = pl.BlockSpec(memory_space=pl.ANY)
NCHIP = 4


def _place():
    x, y, c = lax.axis_index("x"), lax.axis_index("y"), lax.axis_index("c")
    return x, y, c, [(1 - x, y), (x, 1 - y), (1 - x, 1 - y)]


def _comm_call(body, name, ins, out_shapes, sems):
    return pl.pallas_call(body, name=name, in_specs=[ANY] * len(ins), out_specs=[ANY] * len(out_shapes),
                          out_shape=out_shapes, scratch_shapes=sems)(*ins)


def _gather_weights(arrs, name):
    n = len(arrs)

    def body(*refs):
        ins, outs = refs[:n], refs[n:2 * n]
        send_i, recv_i, send_d, recv_d = refs[2 * n:]
        x, y, c, chips = _place()
        me = 2 * x + y

        def half(a, cc):
            h = arrs[a].shape[0] // 2
            return pl.ds(pl.multiple_of(cc * h, h), h)

        def ici(a, k, src_chip, dst_dev):
            return pltpu.make_async_remote_copy(
                src_ref=ins[a].at[half(a, c)], dst_ref=outs[a].at[src_chip, half(a, c)], send_sem=send_i.at[a, k],
                recv_sem=recv_i.at[a, k], device_id=dst_dev, device_id_type=MESH)

        def d2d(a, k, src_chip, cc):
            reg = outs[a].at[src_chip, half(a, cc)]
            return pltpu.make_async_remote_copy(src_ref=reg, dst_ref=reg, send_sem=send_d.at[a, k], recv_sem=recv_d.at[a, k],
                                                device_id=(x, y, 1 - c), device_id_type=MESH)

        for a in range(n):
            for k, (px, py) in enumerate(chips):
                ici(a, k, me, (px, py, c)).start()
        for k, (px, py) in enumerate(chips):
            for a in range(n):
                ici(a, k, 2 * px + py, (px, py, c)).wait_recv()
                d2d(a, k, 2 * px + py, c).start()
        for k, (px, py) in enumerate(chips):
            for a in range(n):
                ici(a, k, me, (px, py, c)).wait_send()
                d2d(a, k, 2 * px + py, c).wait_send()
                d2d(a, k, 2 * px + py, 1 - c).wait_recv()

    sem = lambda: pltpu.SemaphoreType.DMA((n, 3))
    return _comm_call(body, name, arrs, [_sds((NCHIP,) + a.shape, a.dtype) for a in arrs], [sem(), sem(), sem(), sem()])


HBM = pl.BlockSpec(memory_space=pltpu.HBM)
SEM = pl.BlockSpec(memory_space=pltpu.SEMAPHORE)
DATAFLOW = pltpu.SideEffectType.DATAFLOW_SIDE_EFFECTING


def _half_rows(r, cc):
    return pl.ds(pl.multiple_of(cc * (r // 2), r // 2), r // 2)


def _gather_start(arrs, after, name):
    n = len(arrs)

    def body(*refs):
        ins, lands = refs[:n], refs[n:2 * n]
        send, recv, token = refs[2 * n + 1], refs[2 * n + 2], refs[-1]
        x, y, c, chips = _place()
        me = 2 * x + y
        for a in range(n):
            rows = _half_rows(arrs[a].shape[0], c)
            for k, (px, py) in enumerate(chips):
                pltpu.make_async_remote_copy(src_ref=ins[a].at[rows], dst_ref=lands[a].at[me, rows], send_sem=send.at[3 * a + k],
                                             recv_sem=recv.at[3 * a + k], device_id=(px, py, c), device_id_type=MESH).start()
        token[...] = jnp.zeros_like(token)

    hbm = lambda v: pltpu.with_memory_space_constraint(v, pltpu.HBM)
    land_shapes = [((NCHIP,) + a.shape, a.dtype) for a in arrs]
    out = pl.pallas_call(
        body, name=name,
        out_shape=(pltpu.SemaphoreType.DMA((3 * n,)), pltpu.SemaphoreType.DMA((3 * n,)), *[pltpu.HBM(a.shape, a.dtype) for a in arrs],
                   *[pltpu.HBM(s, d) for s, d in land_shapes], _sds((8, HD))),
        in_specs=[HBM] * (2 * n) + [ANY], out_specs=(SEM, SEM, *[HBM] * (2 * n), pl.BlockSpec(memory_space=pltpu.VMEM)),
        input_output_aliases={i: 2 + i for i in range(2 * n)},
        compiler_params=pltpu.CompilerParams(has_side_effects=DATAFLOW),
    )(*[hbm(a) for a in arrs], *[hbm(lax.empty(s, d)) for s, d in land_shapes], after)
    return out[0], out[1], list(out[2:2 + n]), list(out[2 + n:2 + 2 * n]), out[-1]


def _gather_wait(send, recv, srcs, lands, after, name):
    n = len(srcs)

    def body(*refs):
        ins, lands_ = refs[:n], refs[n:2 * n]
        send_, recv_ = refs[2 * n], refs[2 * n + 1]
        x, y, c, chips = _place()
        for a in range(n):
            rows = _half_rows(srcs[a].shape[0], c)
            for k, (px, py) in enumerate(chips):
                cp = pltpu.make_async_remote_copy(src_ref=ins[a].at[rows], dst_ref=lands_[a].at[2 * px + py, rows], send_sem=send_.at[3 * a + k],
                                                  recv_sem=recv_.at[3 * a + k], device_id=(px, py, c), device_id_type=MESH)
                cp.wait_send()
                cp.wait_recv()

    out = pl.pallas_call(
        body, name=name, out_shape=[pltpu.HBM(v.shape, v.dtype) for v in list(srcs) + list(lands)],
        in_specs=[HBM] * (2 * n) + [SEM, SEM, ANY], out_specs=[HBM] * (2 * n), input_output_aliases={i: i for i in range(2 * n)},
        compiler_params=pltpu.CompilerParams(has_side_effects=DATAFLOW),
    )(*srcs, *lands, send, recv, after)
    return list(out[n:])


def _pair_forward(lands, name):
    n = len(lands)

    def body(*refs):
        ins, outs = refs[:n], refs[n:2 * n]
        send, recv = refs[2 * n:]
        x, y, c, chips = _place()
        copies = []
        for a in range(n):
            r = lands[a].shape[1]
            for k, (px, py) in enumerate(chips):
                cp = pltpu.make_async_remote_copy(
                    src_ref=ins[a].at[2 * px + py, _half_rows(r, c)], dst_ref=outs[a].at[2 * px + py, _half_rows(r, c)],
                    send_sem=send.at[a, k], recv_sem=recv.at[a, k], device_id=(x, y, 1 - c), device_id_type=MESH)
                cp.start()
                copies.append(cp)
        for a in range(n):
            r = lands[a].shape[1]
            for k, (px, py) in enumerate(chips):
                pltpu.make_async_remote_copy(
                    src_ref=ins[a].at[2 * px + py, _half_rows(r, c)], dst_ref=outs[a].at[2 * px + py, _half_rows(r, 1 - c)],
                    send_sem=send.at[a, k], recv_sem=recv.at[a, k], device_id=(x, y, 1 - c), device_id_type=MESH).wait_recv()
        for cp in copies:
            cp.wait_send()

    return pl.pallas_call(body, name=name, in_specs=[ANY] * n, out_specs=[ANY] * n, out_shape=[_sds(v.shape, v.dtype) for v in lands],
                          scratch_shapes=[pltpu.SemaphoreType.DMA((n, 3)), pltpu.SemaphoreType.DMA((n, 3))],
                          input_output_aliases={i: i for i in range(n)})(*lands)


def _pair_exchange(arrs, name):
    n = len(arrs)

    def body(*refs):
        ins, outs = refs[:n], refs[n:2 * n]
        send, recv = refs[2 * n:]
        x, y, c, _ = _place()
        copies = []
        for a in range(n):
            h = arrs[a].shape[1] // 2
            cp = pltpu.make_async_remote_copy(src_ref=ins[a].at[:, pl.ds(pl.multiple_of((1 - c) * h, h), h)], dst_ref=outs[a],
                                              send_sem=send.at[a], recv_sem=recv.at[a], device_id=(x, y, 1 - c), device_id_type=MESH)
            cp.start()
            copies.append(cp)
        for cp in copies:
            cp.wait()

    return _comm_call(body, name, arrs, [_sds((a.shape[0], a.shape[1] // 2, a.shape[2]), a.dtype) for a in arrs],
                      [pltpu.SemaphoreType.DMA((n,)), pltpu.SemaphoreType.DMA((n,))])


def _chip_exchange(arrs, name):
    n = len(arrs)

    def body(*refs):
        ins, outs = refs[:n], refs[n:2 * n]
        send, recv = refs[2 * n:]
        x, y, c, chips = _place()
        me = 2 * x + y
        copies = []
        for a in range(n):
            for k, (px, py) in enumerate(chips):
                r = pltpu.make_async_remote_copy(src_ref=ins[a].at[2 * px + py], dst_ref=outs[a].at[me], send_sem=send.at[a, k],
                                                 recv_sem=recv.at[a, k], device_id=(px, py, c), device_id_type=MESH)
                r.start()
                copies.append(r)
        for cp in copies:
            cp.wait()

    return _comm_call(body, name, arrs, [_sds(a.shape, a.dtype) for a in arrs],
                      [pltpu.SemaphoreType.DMA((n, 3)), pltpu.SemaphoreType.DMA((n, 3))])


def _chip_exchange_start(arrs, name):
    n = len(arrs)

    def body(*refs):
        ins, lands = refs[:n], refs[n:2 * n]
        send, recv, token = refs[2 * n], refs[2 * n + 1], refs[-1]
        x, y, c, chips = _place()
        me = 2 * x + y
        for a in range(n):
            for k, (px, py) in enumerate(chips):
                pltpu.make_async_remote_copy(src_ref=ins[a].at[2 * px + py], dst_ref=lands[a].at[me], send_sem=send.at[3 * a + k],
                                             recv_sem=recv.at[3 * a + k], device_id=(px, py, c), device_id_type=MESH).start()
        token[...] = jnp.zeros_like(token)

    hbm = lambda v: pltpu.with_memory_space_constraint(v, pltpu.HBM)
    out = pl.pallas_call(
        body, name=name,
        out_shape=(pltpu.SemaphoreType.DMA((3 * n,)), pltpu.SemaphoreType.DMA((3 * n,)), *[pltpu.HBM(a.shape, a.dtype) for a in arrs],
                   *[pltpu.HBM(a.shape, a.dtype) for a in arrs], _sds((8, HD))),
        in_specs=[HBM] * (2 * n), out_specs=(SEM, SEM, *[HBM] * (2 * n), pl.BlockSpec(memory_space=pltpu.VMEM)),
        input_output_aliases={i: 2 + i for i in range(2 * n)},
        compiler_params=pltpu.CompilerParams(has_side_effects=DATAFLOW),
    )(*[hbm(a) for a in arrs], *[hbm(lax.empty(a.shape, a.dtype)) for a in arrs])
    return out[0], out[1], list(out[2:2 + n]), list(out[2 + n:2 + 2 * n]), out[-1]


def _chip_exchange_wait(send, recv, srcs, lands, after, name):
    n = len(srcs)

    def body(*refs):
        ins, lands_ = refs[:n], refs[n:2 * n]
        send_, recv_ = refs[2 * n], refs[2 * n + 1]
        x, y, c, chips = _place()
        for a in range(n):
            for k, (px, py) in enumerate(chips):
                cp = pltpu.make_async_remote_copy(src_ref=ins[a].at[2 * px + py], dst_ref=lands_[a].at[2 * px + py], send_sem=send_.at[3 * a + k],
                                                  recv_sem=recv_.at[3 * a + k], device_id=(px, py, c), device_id_type=MESH)
                cp.wait_send()
                cp.wait_recv()

    out = pl.pallas_call(
        body, name=name, out_shape=[pltpu.HBM(v.shape, v.dtype) for v in list(srcs) + list(lands)],
        in_specs=[HBM] * (2 * n) + [SEM, SEM, ANY], out_specs=[HBM] * (2 * n), input_output_aliases={i: i for i in range(2 * n)},
        compiler_params=pltpu.CompilerParams(has_side_effects=DATAFLOW),
    )(*srcs, *lands, send, recv, after)
    return list(out[n:])


def _pair_swap(arrs, name):
    n = len(arrs)

    def body(*refs):
        ins, outs = refs[:n], refs[n:2 * n]
        send, recv = refs[2 * n:]
        x, y, c, _ = _place()
        copies = []
        for a in range(n):
            cp = pltpu.make_async_remote_copy(src_ref=ins[a], dst_ref=outs[a], send_sem=send.at[a], recv_sem=recv.at[a],
                                              device_id=(x, y, 1 - c), device_id_type=MESH)
            cp.start()
            copies.append(cp)
        for cp in copies:
            cp.wait()

    return _comm_call(body, name, arrs, [_sds(a.shape, a.dtype) for a in arrs],
                      [pltpu.SemaphoreType.DMA((n,)), pltpu.SemaphoreType.DMA((n,))])


def _all_gather_devices(v, name):
    def body(v_ref, o_ref, send, recv, loc):
        x, y, c, _ = _place()
        me = 4 * x + 2 * y + c
        own = pltpu.make_async_copy(v_ref, o_ref.at[me], loc)
        own.start()
        copies = [own]
        for k in range(1, 8):
            fx, fy, fc = (k >> 2) & 1, (k >> 1) & 1, k & 1
            peer = (x ^ fx, y ^ fy, c ^ fc)
            r = pltpu.make_async_remote_copy(src_ref=v_ref, dst_ref=o_ref.at[me], send_sem=send.at[k - 1],
                                             recv_sem=recv.at[k - 1], device_id=peer, device_id_type=MESH)
            r.start()
            copies.append(r)
        for cp in copies:
            cp.wait()

    return _comm_call(body, name, [v], [_sds((8,) + v.shape, v.dtype)],
                      [pltpu.SemaphoreType.DMA((7,)), pltpu.SemaphoreType.DMA((7,)), pltpu.SemaphoreType.DMA])[0]


def _row_tile(r):
    return next((b for b in (512, 384, 256, 128, 64, 32, 16) if r % b == 0), r)


def _add2(a, b, out_dtype, name):
    r, w = a.shape
    br = _row_tile(r)

    def body(a_ref, b_ref, o_ref):
        o_ref[...] = (a_ref[...].astype(F32) + b_ref[...].astype(F32)).astype(out_dtype)

    blk = pl.BlockSpec((br, w), lambda i: (i, 0))
    return _pc(body, name, (r // br,), [blk, blk], blk, _sds((r, w), out_dtype))(a, b)


def _sum_slots(a, out_dtype, name, extra=None):
    n, r, w = a.shape
    br = _row_tile(r)

    def body(*refs):
        a_ref, o_ref = refs[0], refs[-1]
        acc = a_ref[0].astype(F32)
        for s in range(1, n):
            acc = acc + a_ref[s].astype(F32)
        if extra is not None:
            acc = acc + refs[1][...].astype(F32)
        o_ref[...] = acc.astype(out_dtype)

    ins = [a] + ([extra] if extra is not None else [])
    specs = [pl.BlockSpec((n, br, w), lambda i: (0, i, 0))] + ([pl.BlockSpec((br, w), lambda i: (i, 0))] if extra is not None else [])
    return _pc(body, name, (r // br,), specs, pl.BlockSpec((br, w), lambda i: (i, 0)), _sds((r, w), out_dtype))(*ins)


SMALL = ["norm_mix_g", "norm_xattn_g", "norm_mlp_g", "final_norm_g", "mem_norm_g", "hgrn_lb_logits", "mlstm_norm_g",
         "hgrn_norm_g", "c_qnorm_g", "c_knorm_g", "ab_gate_b", "c_fgate_b"]
SMALL_ROWS = 16


def _pack_small(parts):
    flat = jnp.concatenate([p.reshape(-1).astype(F32) for p in parts])
    return jnp.pad(flat, (0, SMALL_ROWS * D - flat.shape[0])).reshape(SMALL_ROWS, D)


def _unpack_small(buf, shapes):
    flat, out, off = buf.reshape(-1), [], 0
    for s in shapes:
        n = 1
        for d in s:
            n *= d
        out.append(flat[off:off + n].reshape(s))
        off += n
    return out


def kernel(x, mem, norm_mix_g, norm_xattn_g, norm_mlp_g, final_norm_g, ab_w_in, ab_conv_w, ab_gate_b, hgrn_lb_logits, mlstm_norm_g, hgrn_norm_g, ab_w_out, c_w_in, c_fgate_b, c_qnorm_g, c_knorm_g, c_w_out, mem_norm_g, mem_w_kv, xa_w_q, xa_w_o, mlp_w1, mlp_w2, loss_target, m_norm_mix_g, m_norm_xattn_g, m_norm_mlp_g, m_final_norm_g, m_ab_w_in, m_ab_conv_w, m_ab_gate_b, m_hgrn_lb_logits, m_mlstm_norm_g, m_hgrn_norm_g, m_ab_w_out, m_c_w_in, m_c_fgate_b, m_c_qnorm_g, m_c_knorm_g, m_c_w_out, m_mem_norm_g, m_mem_w_kv, m_xa_w_q, m_xa_w_o, m_mlp_w1, m_mlp_w2, v_norm_mix_g, v_norm_xattn_g, v_norm_mlp_g, v_final_norm_g, v_ab_w_in, v_ab_conv_w, v_ab_gate_b, v_hgrn_lb_logits, v_mlstm_norm_g, v_hgrn_norm_g, v_ab_w_out, v_c_w_in, v_c_fgate_b, v_c_qnorm_g, v_c_knorm_g, v_c_w_out, v_mem_norm_g, v_mem_w_kv, v_xa_w_q, v_xa_w_o, v_mlp_w1, v_mlp_w2):
    A = dict(locals())
    chip = 2 * lax.axis_index("x") + lax.axis_index("y")

    big = ["ab_w_in", "c_w_in", "ab_w_out", "c_w_out", "mem_w_kv", "xa_w_q", "xa_w_o", "mlp_w1", "mlp_w2"]
    shard2d = {"ab_w_in": (D, 1026), "c_w_in": (D, 1026), "ab_w_out": (256, D), "c_w_out": (256, D), "mem_w_kv": (D, 512),
               "xa_w_q": (512, D), "xa_w_o": (512, D), "mlp_w1": (2 * D, D), "mlp_w2": (2 * D, D)}
    shard16 = lambda n: A[n].reshape(shard2d[n]).astype(BF16)
    own_slot = lambda gs, os: [lax.dynamic_update_index_in_dim(g, o, chip, 0) for g, o in zip(gs, os)]
    cols = lambda g: jnp.concatenate([g[k] for k in range(NCHIP)], axis=1)
    per_layer = lambda g: g.reshape(NCHIP, 2, -1, D).transpose(1, 0, 2, 3)
    first = [shard16("ab_w_in"), jnp.pad(ab_conv_w[0], ((0, 16 - CONV_W), (0, 0)))]
    g_in0, g_conv = own_slot(_gather_weights(first, "gather_first"), first)
    W = dict(w_in0=_pack_w_in0(cols(g_in0)))
    rest_names = ["c_w_in", "ab_w_out", "c_w_out", "xa_w_q", "xa_w_o", "mlp_w1", "mlp_w2", "mem_w_kv"]
    rest = [shard16(n) for n in rest_names]
    send_s, recv_s, srcs, lands, token = _gather_start(rest, g_conv, "gather_rest_start")

    def late_weights(after):
        got = _pair_forward(_gather_wait(send_s, recv_s, srcs, lands, after, "gather_rest_wait"), "gather_rest_forward")
        gw = dict(zip(rest_names, own_slot(got, rest)))
        return dict(w_in1=_pack_w_in1(cols(gw["c_w_in"])), w_out0=gw["ab_w_out"].reshape(D, D), w_out1=gw["c_w_out"].reshape(D, D),
                    wkv_s=gw["mem_w_kv"],
                    wq=per_layer(gw["xa_w_q"]).reshape(2, D, D), wo=per_layer(gw["xa_w_o"]).reshape(2, D, D),
                    w1s=gw["mlp_w1"].reshape(NCHIP, 2, D, D), w2=gw["mlp_w2"].reshape(NCHIP, 2, D, D))

    S = dict(norm_mix_g=norm_mix_g + token[0, 0], norm_xattn_g=norm_xattn_g, norm_mlp_g=norm_mlp_g, final_norm_g=final_norm_g,
             conv_w=cols(g_conv[:, :CONV_W]), gate_b=ab_gate_b, lb_logits=hgrn_lb_logits, mlstm_norm_g=mlstm_norm_g,
             hgrn_norm_g=hgrn_norm_g, c_fgate_b=c_fgate_b, c_qnorm_g=c_qnorm_g, c_knorm_g=c_knorm_g, mem_norm_g=mem_norm_g)

    core = lax.axis_index("c")
    by_rows = lambda g: g.reshape(NCHIP, -1, D)

    def stack_cols(g):
        return jnp.stack([g[:, 1026 * k:1026 * (k + 1)] for k in range(NCHIP)])

    def pair_sums(arrs, tag):
        theirs = _pair_exchange(arrs, f"pair_exchange_{tag}")
        out = []
        for i, (a, th) in enumerate(zip(arrs, theirs)):
            h = a.shape[1] // 2
            mine = lax.dynamic_slice_in_dim(a, core * h, h, axis=1)
            out.append(_add2(mine.reshape(-1, a.shape[2]), th.reshape(-1, a.shape[2]), BF16, f"pair_sum_{tag}{i}").reshape(th.shape))
        return out

    def chip_sums(psums, from_chips, tag):
        out = []
        for i, (f, p) in enumerate(zip(from_chips, psums)):
            f = lax.dynamic_update_index_in_dim(f, lax.dynamic_index_in_dim(p, chip, 0, keepdims=False), chip, 0)
            out.append(_sum_slots(f, F32, f"chip_sum_{tag}{i}"))
        return out

    started = {}

    def grads_hook(stage, g):
        if stage == "layer1":
            arrs = [jnp.concatenate([by_rows(g["w_out"]), by_rows(g["wq"]), by_rows(g["wo"]), g["w1"], by_rows(g["w2"])], axis=1),
                    stack_cols(_unpack_w_in1(g["w_in"]))]
        else:
            arrs = [jnp.concatenate([by_rows(g["wq"]), by_rows(g["wo"]), g["w1"], by_rows(g["w2"])], axis=1)]
        psums = pair_sums(arrs, stage)
        *handles, token = _chip_exchange_start(psums, f"chip_exchange_start_{stage}")
        started[stage] = (psums, handles)
        return token[0, 0]

    lossp, dx, G = _local_step(x[0], mem[0], loss_target[0], W, S, late_weights, grads_hook)

    gsmall = {"norm_mix_g": G["norm_mix_g"], "norm_xattn_g": G["norm_xattn_g"], "norm_mlp_g": G["norm_mlp_g"],
              "final_norm_g": G["final_norm_g"], "mem_norm_g": G["mem_norm_g"], "hgrn_lb_logits": G["lb_logits"],
              "mlstm_norm_g": G["mlstm_norm_g"], "hgrn_norm_g": G["hgrn_norm_g"], "c_qnorm_g": G["c_qnorm_g"],
              "c_knorm_g": G["c_knorm_g"], "ab_gate_b": G["gate_b"], "c_fgate_b": G["c_fgate_b"]}
    packed = _pack_small([gsmall[n] for n in SMALL] + [G["conv_w"], lossp])
    red = _sum_slots(_all_gather_devices(packed, "gather_small"), F32, "sum_small")
    small_shapes = [A[n].shape for n in SMALL]
    *gs, gconv, loss = _unpack_small(red, small_shapes + [(CONV_W, D), ()])
    gs = dict(zip(SMALL, gs))
    gconv = lax.dynamic_slice_in_dim(gconv, chip * 256, 256, axis=1)[None]

    last = pair_sums([by_rows(G["w_out0"]), stack_cols(_unpack_w_in0(G["w_in0"])), G["wkv"]], "last")
    rhalf = chip_sums(last, _chip_exchange(last, "chip_exchange_last"), "last")
    for stage in ("layer1", "layer0_mlp_xattn"):
        psums, handles = started[stage]
        rhalf += chip_sums(psums, _chip_exchange_wait(*handles, dx, f"chip_exchange_wait_{stage}"), stage)
    other = _pair_swap(rhalf, "pair_swap")
    r_out0, r_in0, r_kv, r_l1, r_in1, r_l0 = [
        jnp.where(core == 0, jnp.concatenate([m_, o_], axis=0), jnp.concatenate([o_, m_], axis=0)) for m_, o_ in zip(rhalf, other)]
    gbig = {"ab_w_in": r_in0, "c_w_in": r_in1, "mem_w_kv": r_kv, "ab_w_out": r_out0, "c_w_out": r_l1[0:256],
            "xa_w_q": jnp.concatenate([r_l0[0:256], r_l1[256:512]], axis=0),
            "xa_w_o": jnp.concatenate([r_l0[256:512], r_l1[512:768]], axis=0),
            "mlp_w1": jnp.concatenate([r_l0[512:1536], r_l1[768:1792]], axis=0),
            "mlp_w2": jnp.concatenate([r_l0[1536:2560], r_l1[1792:2816]], axis=0)}

    out_g, out_d, out_m, out_v = {}, {}, {}, {}
    for n in big:
        d_, m_, v_ = _adam(A[n].reshape(shard2d[n]), gbig[n], A["m_" + n].reshape(shard2d[n]), A["v_" + n].reshape(shard2d[n]), "adam_" + n)
        out_g[n] = gbig[n].reshape(A[n].shape)
        out_d[n], out_m[n], out_v[n] = d_.reshape(A[n].shape), m_.reshape(A[n].shape), v_.reshape(A[n].shape)
    sd, sm, sv = _adam(_pack_small([A[n] for n in SMALL]), _pack_small([gs[n] for n in SMALL]),
                       _pack_small([A["m_" + n] for n in SMALL]), _pack_small([A["v_" + n] for n in SMALL]), "adam_small")
    for n, d_, m_, v_ in zip(SMALL, _unpack_small(sd, small_shapes), _unpack_small(sm, small_shapes), _unpack_small(sv, small_shapes)):
        out_g[n], out_d[n], out_m[n], out_v[n] = gs[n], d_, m_, v_
    cd, cm_, cv = _adam(ab_conv_w[0], gconv[0], m_ab_conv_w[0], v_ab_conv_w[0], "adam_conv")
    out_g["ab_conv_w"], out_d["ab_conv_w"], out_m["ab_conv_w"], out_v["ab_conv_w"] = gconv, cd[None], cm_[None], cv[None]

    order = ["norm_mix_g", "norm_xattn_g", "norm_mlp_g", "final_norm_g", "ab_w_in", "ab_conv_w", "ab_gate_b", "hgrn_lb_logits",
             "mlstm_norm_g", "hgrn_norm_g", "ab_w_out", "c_w_in", "c_fgate_b", "c_qnorm_g", "c_knorm_g", "c_w_out", "mem_norm_g",
             "mem_w_kv", "xa_w_q", "xa_w_o", "mlp_w1", "mlp_w2"]
    return (loss, dx[None], *[out_g[n] for n in order], *[out_d[n] for n in order], *[out_m[n] for n in order],
            *[out_v[n] for n in order])
```

```python
import functools

import jax
import jax.numpy as jnp
from jax import lax
from jax.experimental import pallas as pl
from jax.experimental.pallas import tpu as pltpu

F32 = jnp.float32
BF16 = jnp.bfloat16
EPS = 1e-6
D = 1024
CHUNK = 64
REC_CHUNKS = 4
HD = 128
XD = 256
NEG = -1e30
VMEM_LIMIT_V7X = 56 * 1024 * 1024
ADAM_LR, ADAM_B1, ADAM_B2, ADAM_EPS, ADAM_WD, ADAM_STEP = 0.001, 0.9, 0.999, 1e-08, 0.01, 10
MESH = pl.DeviceIdType.MESH


def _pc(body, name, grid, in_specs, out_specs, out_shape, scratch=(), **kw):
    return pl.pallas_call(
        body, name=name, grid=grid, in_specs=in_specs, out_specs=out_specs, out_shape=out_shape,
        scratch_shapes=scratch,
        compiler_params=pltpu.CompilerParams(
            dimension_semantics=("arbitrary",) * len(grid), vmem_limit_bytes=VMEM_LIMIT_V7X), **kw)


def _sds(shape, dtype=F32):
    return jax.ShapeDtypeStruct(shape, dtype)


def _blk(n, target):
    return max(b for b in range(128, max(target, 128) + 1, 128) if n % b == 0)


def _dot(a, b, dims):
    return lax.dot_general(a, b, (dims, ((), ())), preferred_element_type=F32)


def _nn(a, b):
    return _dot(a, b, ((1,), (0,)))


def _nt(a, b):
    return _dot(a, b, ((1,), (1,)))


def _tn(a, b):
    return _dot(a, b, ((0,), (0,)))


def _sigmoid(x):
    return 1.0 / (1.0 + jnp.exp(-x))


def _log_sigmoid(x):
    return jnp.minimum(x, 0.0) - jnp.log(1.0 + jnp.exp(-jnp.abs(x)))


def _rstd(x):
    return lax.rsqrt(jnp.mean(x * x, axis=-1, keepdims=True) + EPS)


def _rms_bwd(du, x, g):
    r = _rstd(x)
    xh = x * r
    dxh = du * g
    dx = r * (dxh - xh * jnp.mean(dxh * xh, axis=-1, keepdims=True))
    return dx, du * xh


def _norm_mm(h, g, w, name, bm=1024, bn=512):
    t, n = h.shape[0], w.shape[1]
    bm, bn = min(bm, t), _blk(n, 3 * bn)

    def body(h_ref, g_ref, w_ref, z_ref, u_ref):
        @pl.when(pl.program_id(1) == 0)
        def _():
            x = h_ref[...]
            u_ref[...] = (x * _rstd(x) * g_ref[...]).astype(BF16)
        z_ref[...] = _nn(u_ref[...], w_ref[...])

    return _pc(body, name, (t // bm, n // bn),
               [pl.BlockSpec((bm, D), lambda i, j: (i, 0)), pl.BlockSpec((1, D), lambda i, j: (0, 0)),
                pl.BlockSpec((D, bn), lambda i, j: (0, j))],
               [pl.BlockSpec((bm, bn), lambda i, j: (i, j)), pl.BlockSpec((bm, D), lambda i, j: (i, 0))],
               [_sds((t, n)), _sds((t, D), BF16)])(h, g, w)


def _mm_tn(a, b, name, bm=1024, bn=1024, bt=2048, col_chips=None):
    t, m = a.shape
    n = b.shape[1]
    bm, bn, bt = _blk(m, bm), (n // col_chips if col_chips else _blk(n, bn + bn // 2)), min(bt, t)
    nt = t // bt

    def body(a_ref, b_ref, o_ref, acc):
        k = pl.program_id(2)

        @pl.when(k == 0)
        def _():
            acc[...] = jnp.zeros_like(acc)

        acc[...] += _tn(a_ref[...].astype(BF16), b_ref[...].astype(BF16))

        @pl.when(k == nt - 1)
        def _():
            o_ref[...] = acc[...].astype(BF16)

    if col_chips:
        out_spec, out_shape = pl.BlockSpec((None, bm, bn), lambda i, j, k: (j, i, 0)), _sds((col_chips, m, bn), BF16)
    else:
        out_spec, out_shape = pl.BlockSpec((bm, bn), lambda i, j, k: (i, j)), _sds((m, n), BF16)
    return _pc(body, name, (m // bm, n // bn, nt),
               [pl.BlockSpec((bt, bm), lambda i, j, k: (k, i)), pl.BlockSpec((bt, bn), lambda i, j, k: (k, j))],
               out_spec, out_shape, scratch=[pltpu.VMEM((bm, bn), F32)])(a, b)


def _bwd_in(dz, w, h, g, dh, name, bm=1024, bk=1024):
    t, n = dz.shape
    bm, bk = min(bm, t), _blk(n, bk + bk // 2)
    nk = n // bk

    def body(dz_ref, w_ref, h_ref, g_ref, dh_ref, o_ref, dg_ref, acc):
        i, k = pl.program_id(0), pl.program_id(1)

        @pl.when(k == 0)
        def _():
            acc[...] = jnp.zeros_like(acc)

        @pl.when((i == 0) & (k == 0))
        def _():
            dg_ref[...] = jnp.zeros_like(dg_ref)

        acc[...] += _nt(dz_ref[...], w_ref[...])

        @pl.when(k == nk - 1)
        def _():
            dx, dgr = _rms_bwd(acc[...], h_ref[...], g_ref[...])
            o_ref[...] = dh_ref[...] + dx
            dg_ref[...] += jnp.sum(dgr, axis=0, keepdims=True)

    return _pc(body, name, (t // bm, nk),
               [pl.BlockSpec((bm, bk), lambda i, k: (i, k)), pl.BlockSpec((D, bk), lambda i, k: (0, k)),
                pl.BlockSpec((bm, D), lambda i, k: (i, 0)), pl.BlockSpec((1, D), lambda i, k: (0, 0)),
                pl.BlockSpec((bm, D), lambda i, k: (i, 0))],
               [pl.BlockSpec((bm, D), lambda i, k: (i, 0)), pl.BlockSpec((1, D), lambda i, k: (0, 0))],
               [_sds((t, D)), _sds((1, D))], scratch=[pltpu.VMEM((bm, D), F32)])(dz, w, h, g, dh)


def _mlp_fwd(h, g, w1s, w2, l, name, bm=1024):
    t = h.shape[0]
    bm = min(bm, t)
    nk = w1s.shape[0]

    def body(h_ref, g_ref, w1_ref, w2_ref, o_ref, a_ref, u_ref, acc):
        k = pl.program_id(1)

        @pl.when(k == 0)
        def _():
            x = h_ref[...]
            u_ref[...] = (x * _rstd(x) * g_ref[...]).astype(BF16)
            acc[...] = jnp.zeros_like(acc)

        a = _nn(u_ref[...], w1_ref[...])
        a_ref[...] = a
        r = jnp.square(jnp.maximum(a, 0.0)).astype(BF16)
        acc[...] += _nn(r, w2_ref[...])

        @pl.when(k == nk - 1)
        def _():
            o_ref[...] = h_ref[...] + acc[...]

    return _pc(body, name, (t // bm, nk),
               [pl.BlockSpec((bm, D), lambda i, k: (i, 0)), pl.BlockSpec((1, D), lambda i, k: (0, 0)),
                pl.BlockSpec((None, None, D, D), lambda i, k: (k, l, 0, 0)), pl.BlockSpec((None, None, D, D), lambda i, k: (k, l, 0, 0))],
               [pl.BlockSpec((bm, D), lambda i, k: (i, 0)), pl.BlockSpec((bm, D), lambda i, k: (i, k)),
                pl.BlockSpec((bm, D), lambda i, k: (i, 0))],
               [_sds((t, D)), _sds((t, nk * D)), _sds((t, D), BF16)],
               scratch=[pltpu.VMEM((bm, D), F32)])(h, g, w1s, w2)


def _mlp_bwd(dh, a, w1s, w2, l, h, g, name, bm=512):
    t = h.shape[0]
    bm = min(bm, t)
    nk = w1s.shape[0]

    def body(dh_ref, a_ref, w1_ref, w2_ref, h_ref, g_ref, o_ref, da_ref, r_ref, dg_ref, acc):
        i, k = pl.program_id(0), pl.program_id(1)

        @pl.when(k == 0)
        def _():
            acc[...] = jnp.zeros_like(acc)

        @pl.when((i == 0) & (k == 0))
        def _():
            dg_ref[...] = jnp.zeros_like(dg_ref)

        ap = jnp.maximum(a_ref[...], 0.0)
        r_ref[...] = jnp.square(ap).astype(BF16)
        dr = _nt(dh_ref[...].astype(BF16), w2_ref[...])
        da = (dr * (2.0 * ap)).astype(BF16)
        da_ref[...] = da
        acc[...] += _nt(da, w1_ref[...])

        @pl.when(k == nk - 1)
        def _():
            dx, dgr = _rms_bwd(acc[...], h_ref[...], g_ref[...])
            o_ref[...] = dh_ref[...] + dx
            dg_ref[...] += jnp.sum(dgr, axis=0, keepdims=True)

    return _pc(body, name, (t // bm, nk),
               [pl.BlockSpec((bm, D), lambda i, k: (i, 0)), pl.BlockSpec((bm, D), lambda i, k: (i, k)),
                pl.BlockSpec((None, None, D, D), lambda i, k: (k, l, 0, 0)), pl.BlockSpec((None, None, D, D), lambda i, k: (k, l, 0, 0)),
                pl.BlockSpec((bm, D), lambda i, k: (i, 0)), pl.BlockSpec((1, D), lambda i, k: (0, 0))],
               [pl.BlockSpec((bm, D), lambda i, k: (i, 0)), pl.BlockSpec((bm, D), lambda i, k: (i, k)),
                pl.BlockSpec((bm, D), lambda i, k: (i, k)), pl.BlockSpec((1, D), lambda i, k: (0, 0))],
               [_sds((t, D)), _sds((t, nk * D), BF16), _sds((t, nk * D), BF16), _sds((1, D))],
               scratch=[pltpu.VMEM((bm, D), F32)])(dh, a, w1s, w2, h, g)


def _rows_of(x):
    return lax.broadcasted_iota(jnp.int32, x.shape, 0)


def _shift_down(x, s):
    if s == 0:
        return x
    return jnp.where(_rows_of(x) >= s, pltpu.roll(x, s, 0), 0.0)


def _shift_up(x, s):
    if s == 0:
        return x
    n = x.shape[0]
    return jnp.where(_rows_of(x) < n - s, pltpu.roll(x, n - s, 0), 0.0)


def _cumsum_rows(x):
    n, s = x.shape[0], 1
    while s < n:
        x = x + _shift_down(x, s)
        s *= 2
    return x


def _rcumsum_rows(x):
    n, s = x.shape[0], 1
    while s < n:
        x = x + _shift_up(x, s)
        s *= 2
    return x


def _silu(x):
    return x * _sigmoid(x)


def _dsilu(x):
    s = _sigmoid(x)
    return s * (1.0 + x * (1.0 - s))


CONV_W = 4


def _conv_pre(u, w):
    y = _shift_down(u, CONV_W - 1) * w[0:1, :]
    for j in range(1, CONV_W):
        y = y + _shift_down(u, CONV_W - 1 - j) * w[j:j + 1, :]
    return y


def _conv_fwd(z0, cw, name):
    t = z0.shape[0]

    def body(u_ref, w_ref, o_ref):
        o_ref[...] = _silu(_conv_pre(u_ref[...], w_ref[...]))

    return _pc(body, name, (2 * 512 // HD,),
               [pl.BlockSpec((t, HD), lambda c: (0, c)), pl.BlockSpec((CONV_W, HD), lambda c: (0, c))],
               pl.BlockSpec((t, HD), lambda c: (0, c)), _sds((t, 1024)))(z0, cw)


def _conv_bwd(z0, cw, dy, name):
    t = z0.shape[0]

    def body(u_ref, w_ref, dy_ref, du_ref, dw_ref):
        u, w = u_ref[...], w_ref[...]
        dpre = dy_ref[...] * _dsilu(_conv_pre(u, w))
        du = _shift_up(dpre, CONV_W - 1) * w[0:1, :]
        for j in range(1, CONV_W):
            du = du + _shift_up(dpre, CONV_W - 1 - j) * w[j:j + 1, :]
        du_ref[...] = du.astype(BF16)
        for j in range(CONV_W):
            dw_ref[j:j + 1, :] = jnp.sum(dpre * _shift_down(u, CONV_W - 1 - j), axis=0, keepdims=True)

    return _pc(body, name, (2 * 512 // HD,),
               [pl.BlockSpec((t, HD), lambda c: (0, c)), pl.BlockSpec((CONV_W, HD), lambda c: (0, c)),
                pl.BlockSpec((t, HD), lambda c: (0, c))],
               [pl.BlockSpec((t, HD), lambda c: (0, c)), pl.BlockSpec((CONV_W, HD), lambda c: (0, c))],
               [_sds((t, 1024), BF16), _sds((CONV_W, 1024))])(z0, cw, dy)


def _mlstm_gates(gate, bias, m_in):
    L = gate.shape[0]
    r = lax.broadcasted_iota(jnp.int32, (L, L), 0)
    c = lax.broadcasted_iota(jnp.int32, (L, L), 1)
    eye, tril = r == c, c <= r
    i_col = gate[:, 0:1] + bias[:, 0:1]
    f_col = gate[:, 1:2] + bias[:, 1:2]
    logf_col = _log_sigmoid(f_col)
    logf_row = jnp.sum(jnp.where(eye, logf_col, 0.0), axis=0, keepdims=True)
    i_row = jnp.sum(jnp.where(eye, i_col, 0.0), axis=0, keepdims=True)
    b_col = jnp.sum(jnp.where(tril, logf_row, 0.0), axis=1, keepdims=True)
    b_row = jnp.sum(jnp.where(r <= c, logf_col, 0.0), axis=0, keepdims=True)
    logd = jnp.where(tril, b_col - b_row + i_row, NEG)
    inter = b_col + m_in
    m_t = jnp.maximum(inter, jnp.max(logd, axis=1, keepdims=True))
    w_t = jnp.exp(inter - m_t)
    dm = jnp.exp(logd - m_t)
    b_last = b_col[L - 1:L, :]
    log_in = b_last - b_col + i_col
    m_new = jnp.maximum(b_last + m_in, jnp.max(log_in, axis=0, keepdims=True))
    w_col = jnp.exp(log_in - m_new)
    decay = jnp.exp(b_last + m_in - m_new)
    return dict(eye=eye, r=r, c=c, f_col=f_col, m_t=m_t, w_t=w_t, dm=dm, m_new=m_new, w_col=w_col, decay=decay)


def _mlstm_fwd(qk, z0, gates, bias, name):
    t = qk.shape[0]
    nc, nh, L = t // CHUNK, 4, CHUNK
    scale = HD ** -0.5

    def body(q_ref, k_ref, v_ref, g_ref, b_ref, h_ref, cs_ref, ns_ref, ms_ref, c_s, n_s, m_s):
        @pl.when(pl.program_id(0) == 0)
        def _():
            c_s[...] = jnp.zeros_like(c_s)
            n_s[...] = jnp.zeros_like(n_s)
            m_s[...] = jnp.zeros_like(m_s)

        for hd in range(nh):
            sl = slice(hd * HD, (hd + 1) * HD)
            cm, nv, m_in = c_s[hd], n_s[hd], m_s[hd]
            for ck in range(cps):
                rows = slice(ck * L, (ck + 1) * L)
                cs_ref[hd, ck] = cm
                ns_ref[hd, ck] = nv
                ms_ref[hd, ck] = jnp.broadcast_to(m_in, (1, HD))
                q, kh, v = q_ref[rows, sl], k_ref[rows, sl] * scale, v_ref[rows, sl]
                G = _mlstm_gates(g_ref[hd, rows, :], b_ref[hd], m_in)
                qb, kb, vb = q.astype(BF16), kh.astype(BF16), v.astype(BF16)
                sc = _nt(qb, kb) * G["dm"]
                num = _nn(sc.astype(BF16), vb) + G["w_t"] * _nn(qb, cm.astype(BF16))
                den = jnp.sum(sc, axis=1, keepdims=True) + G["w_t"] * jnp.sum(q * nv, axis=1, keepdims=True)
                h_ref[rows, sl] = num / jnp.maximum(jnp.abs(den), jnp.exp(-G["m_t"]))
                wk = G["w_col"] * kh
                cm = G["decay"] * cm + _tn(wk.astype(BF16), vb)
                nv = G["decay"] * nv + jnp.sum(wk, axis=0, keepdims=True)
                m_in = G["m_new"]
            c_s[hd], n_s[hd], m_s[hd] = cm, nv, m_in

    cps = REC_CHUNKS
    hspec = lambda blk: pl.BlockSpec((cps * L, 512), lambda j: (j, blk))
    st = lambda r: pl.BlockSpec((nh, cps, r, HD), lambda j: (0, j, 0, 0))
    return _pc(body, name, (nc // cps,),
               [hspec(0), hspec(1), hspec(2), pl.BlockSpec((nh, cps * L, 2), lambda j: (0, j, 0)),
                pl.BlockSpec((nh, 1, 2), lambda j: (0, 0, 0))],
               [hspec(0), st(HD), st(1), st(1)],
               [_sds((t, 512)), _sds((nh, nc, HD, HD)), _sds((nh, nc, 1, HD)), _sds((nh, nc, 1, HD))],
               scratch=[pltpu.VMEM((nh, HD, HD), F32), pltpu.VMEM((nh, 1, HD), F32), pltpu.VMEM((nh, 1, 1), F32)])(qk, qk, z0, gates, bias)


def _mlstm_bwd(qk, z0, gates, bias, cs, ns, ms, dh, name):
    t = qk.shape[0]
    nc, nh, L = t // CHUNK, 4, CHUNK
    scale = HD ** -0.5

    def body(q_ref, k_ref, v_ref, g_ref, b_ref, cs_ref, ns_ref, ms_ref, dh_ref, dq_ref, dk_ref, dv_ref, dg_ref, dc_s, dn_s):
        @pl.when(pl.program_id(0) == 0)
        def _():
            dc_s[...] = jnp.zeros_like(dc_s)
            dn_s[...] = jnp.zeros_like(dn_s)

        for ck in reversed(range(cps)):
            for hd in range(nh):
                one_head(hd, ck, slice(hd * HD, (hd + 1) * HD), slice(ck * L, (ck + 1) * L), q_ref, k_ref, v_ref, g_ref, b_ref,
                         cs_ref, ns_ref, ms_ref, dh_ref, dq_ref, dk_ref, dv_ref, dg_ref, dc_s, dn_s)

    def one_head(hd, ck, sl, rows, q_ref, k_ref, v_ref, g_ref, b_ref, cs_ref, ns_ref, ms_ref, dh_ref, dq_ref, dk_ref, dv_ref, dg_ref,
                 dc_s, dn_s):
        cm, nv, m_in = cs_ref[hd, ck], ns_ref[hd, ck], ms_ref[hd, ck][:, 0:1]
        q, kh, v = q_ref[rows, sl], k_ref[rows, sl] * scale, v_ref[rows, sl]
        G = _mlstm_gates(g_ref[hd, rows, :], b_ref[hd], m_in)
        w_t, dmat, w_col, decay = G["w_t"], G["dm"], G["w_col"], G["decay"]
        qb, kb, vb, cb = q.astype(BF16), kh.astype(BF16), v.astype(BF16), cm.astype(BF16)
        s = _nt(qb, kb)
        sc = s * dmat
        scb = sc.astype(BF16)
        qc = _nn(qb, cb)
        qn = jnp.sum(q * nv, axis=1, keepdims=True)
        num = _nn(scb, vb) + w_t * qc
        den = jnp.sum(sc, axis=1, keepdims=True) + w_t * qn
        e_m = jnp.exp(-G["m_t"])
        dnm = jnp.maximum(jnp.abs(den), e_m)
        dh_ = dh_ref[rows, sl]
        dnum = dh_ / dnm
        dden = jnp.where(jnp.abs(den) > e_m, -jnp.sum(dh_ * num, axis=1, keepdims=True) / (dnm * dnm) * jnp.sign(den), 0.0)
        dnumb = dnum.astype(BF16)
        dsc = _nt(dnumb, vb) + dden
        dv = _tn(scb, dnumb)
        wd = w_t * dnum
        wdb = wd.astype(BF16)
        ds = dsc * dmat
        dsb = ds.astype(BF16)
        dq = _nt(wdb, cb) + (w_t * dden) * nv + _nn(dsb, kb)
        dc_o = _tn(qb, wdb)
        dn_o = jnp.sum(q * (w_t * dden), axis=0, keepdims=True)
        dw = jnp.sum(dnum * qc, axis=1, keepdims=True) + dden * qn
        dkh = _tn(dsb, qb)
        dlogd = ds * s
        db_col = jnp.sum(dlogd, axis=1, keepdims=True) + dw * w_t
        csum = jnp.sum(dlogd, axis=0, keepdims=True)
        dcn, dnn = dc_s[hd], dn_s[hd]
        dcnb = dcn.astype(BF16)
        kdc = _nn(kb, dcnb)
        dws = jnp.sum(kdc * v, axis=1, keepdims=True) + jnp.sum(kh * dnn, axis=1, keepdims=True)
        dv = dv + w_col * kdc
        dkh = dkh + w_col * (_nt(vb, dcnb) + dnn)
        dlin = dws * w_col
        ddecay = jnp.sum(jnp.sum(dcn * cm, axis=1, keepdims=True), axis=0, keepdims=True) + jnp.sum(dnn * nv, axis=1, keepdims=True)
        dlast = ddecay * decay + jnp.sum(dlin, axis=0, keepdims=True)
        row_id = lax.broadcasted_iota(jnp.int32, (L, 1), 0)
        db_col = db_col - dlin + jnp.where(row_id == L - 1, dlast, 0.0)
        eye, r, c = G["eye"], G["r"], G["c"]
        di = dlin + jnp.sum(jnp.where(eye, csum, 0.0), axis=1, keepdims=True)
        db_row = jnp.sum(jnp.where(eye, db_col, 0.0), axis=0, keepdims=True) - csum
        dlogf = jnp.sum(jnp.where(c >= r, db_row, 0.0), axis=1, keepdims=True)
        dg_ref[hd, rows, 0:1] = di
        dg_ref[hd, rows, 1:2] = dlogf * (1.0 - _sigmoid(G["f_col"]))
        dq_ref[rows, sl] = dq
        dk_ref[rows, sl] = dkh * scale
        dv_ref[rows, sl] = dv
        dc_s[hd] = decay * dcn + dc_o
        dn_s[hd] = decay * dnn + dn_o

    cps = REC_CHUNKS
    rv = lambda j: nc // cps - 1 - j
    hspec = lambda blk: pl.BlockSpec((cps * L, 512), lambda j: (rv(j), blk))
    st = lambda r: pl.BlockSpec((nh, cps, r, HD), lambda j: (0, rv(j), 0, 0))
    gs = pl.BlockSpec((nh, cps * L, 2), lambda j: (0, rv(j), 0))
    return _pc(body, name, (nc // cps,),
               [hspec(0), hspec(1), hspec(2), gs, pl.BlockSpec((nh, 1, 2), lambda j: (0, 0, 0)),
                st(HD), st(1), st(1), hspec(0)],
               [hspec(0), hspec(0), hspec(0), gs],
               [_sds((t, 512)), _sds((t, 512)), _sds((t, 512)), _sds((nh, t, 2))],
               scratch=[pltpu.VMEM((nh, HD, HD), F32), pltpu.VMEM((nh, 1, HD), F32)])(qk, qk, z0, gates, bias, cs, ns, ms, dh)


def _hgrn_act(qb_, fb_, ib_, lg):
    lb = _sigmoid(lg[0:1, :] - lg[1:2, :])
    sg = _sigmoid(fb_)
    f = lb + (1.0 - lb) * sg
    return lb, sg, f, _silu(qb_), (1.0 - lb) * (1.0 - sg), _silu(ib_), _cumsum_rows(jnp.log(f))


HG_SUB = 16


def _hgrn_offdiag(q, k, b, r0):
    beta = b[r0 - 1:r0, :]
    e1 = jnp.exp(b[r0:r0 + HG_SUB, :] - beta)
    e2 = jnp.where(_rows_of(b) < r0, jnp.exp(jnp.minimum(beta - b, 0.0)), 0.0)
    return q[r0:r0 + HG_SUB, :] * e1, k * e2, e1, e2


def _hgrn_fwd(z0, lbl, name):
    t = z0.shape[0]
    nc, nh, L = t // CHUNK, 4, CHUNK

    def body(q_ref, f_ref, i_ref, l_ref, o_ref, ss_ref, st_s):
        @pl.when(pl.program_id(0) == 0)
        def _():
            st_s[...] = jnp.zeros_like(st_s)

        for hd in range(nh):
            sl = slice(hd * HD, (hd + 1) * HD)
            st = st_s[hd]
            for ck in range(cps):
                rows = slice(ck * L, (ck + 1) * L)
                ss_ref[hd, ck] = st
                _, _, _, q, k, v, b = _hgrn_act(q_ref[rows, sl], f_ref[rows, sl], i_ref[rows, sl], l_ref[:, sl])
                o = _nt((q * jnp.exp(b)).astype(BF16), st.astype(BF16))
                sub = _rows_of(b) & (HG_SUB - 1)
                o = o + jnp.sum(q * k, axis=1, keepdims=True) * v
                for dl in range(1, HG_SUB):
                    e = jnp.exp(jnp.where(sub >= dl, b - pltpu.roll(b, dl, 0), NEG))
                    a = jnp.sum(q * pltpu.roll(k, dl, 0) * e, axis=1, keepdims=True)
                    o = o + a * pltpu.roll(v, dl, 0)
                o_ref[rows, sl] = o
                vb = v.astype(BF16)
                for i in range(1, L // HG_SUB):
                    r0 = i * HG_SUB
                    qt, kt, _, _ = _hgrn_offdiag(q, k, b, r0)
                    a = _nt(qt.astype(BF16), kt.astype(BF16))
                    o_ref[ck * L + r0:ck * L + r0 + HG_SUB, sl] += _nn(a.astype(BF16), vb)
                bl = b[L - 1:L, :]
                st = st * jnp.exp(bl) + _tn(v.astype(BF16), (k * jnp.exp(bl - b)).astype(BF16))
            st_s[hd] = st

    cps = REC_CHUNKS
    hspec = lambda blk: pl.BlockSpec((cps * L, 512), lambda j: (j, blk))
    return _pc(body, name, (nc // cps,),
               [hspec(4), hspec(5), hspec(6), pl.BlockSpec((2, 512), lambda j: (0, 0))],
               [hspec(0), pl.BlockSpec((nh, cps, HD, HD), lambda j: (0, j, 0, 0))],
               [_sds((t, 512)), _sds((nh, nc, HD, HD))],
               scratch=[pltpu.VMEM((nh, HD, HD), F32)])(z0, z0, z0, lbl)


def _hgrn_bwd(z0, lbl, ss, do, name):
    t = z0.shape[0]
    nc, nh, L = t // CHUNK, 4, CHUNK

    def body(q_ref, f_ref, i_ref, l_ref, ss_ref, do_ref, dq_ref, df_ref, di_ref, dl_ref, dst_s, dlb_s, dq_a, dk_a, dv_a, db_a):
        @pl.when(pl.program_id(0) == 0)
        def _():
            dst_s[...] = jnp.zeros_like(dst_s)
            dlb_s[...] = jnp.zeros_like(dlb_s)

        for ck in reversed(range(cps)):
            for hd in range(nh):
                one_head(hd, ck, slice(hd * HD, (hd + 1) * HD), slice(ck * L, (ck + 1) * L), q_ref, f_ref, i_ref, l_ref, ss_ref, do_ref,
                         dq_ref, df_ref, di_ref, dl_ref, dst_s, dlb_s, dq_a.at[hd], dk_a.at[hd], dv_a.at[hd], db_a.at[hd])

    def one_head(hd, ck, sl, rs, q_ref, f_ref, i_ref, l_ref, ss_ref, do_ref, dq_ref, df_ref, di_ref, dl_ref, dst_s, dlb_s,
                 dq_a, dk_a, dv_a, db_a):
        st = ss_ref[hd, ck]
        qp, fp, ip = q_ref[rs, sl], f_ref[rs, sl], i_ref[rs, sl]
        lb, sg, f, q, k, v, b = _hgrn_act(qp, fp, ip, l_ref[:, sl])
        do_ = do_ref[rs, sl]
        dob, stb = do_.astype(BF16), st.astype(BF16)
        eb = jnp.exp(b)
        qe = q * eb
        dqe = _nn(dob, stb)
        dst_o = _tn(dob, qe.astype(BF16))
        dq = dqe * eb
        db = dqe * qe
        rows = _rows_of(b)
        sub = rows & (HG_SUB - 1)
        p0 = jnp.sum(do_ * v, axis=1, keepdims=True)
        dq = dq + p0 * k
        dk = p0 * q
        dv = jnp.sum(q * k, axis=1, keepdims=True) * do_
        for dl in range(1, HG_SUB):
            up = L - dl
            kd, vd = pltpu.roll(k, dl, 0), pltpu.roll(v, dl, 0)
            e = jnp.exp(jnp.where(sub >= dl, b - pltpu.roll(b, dl, 0), NEG))
            a = jnp.sum(q * kd * e, axis=1, keepdims=True)
            p = jnp.sum(do_ * vd, axis=1, keepdims=True) * e
            dq = dq + p * kd
            dkd = p * q
            dbb = dkd * kd
            dv = dv + pltpu.roll(a * do_, up, 0)
            dk = dk + pltpu.roll(dkd, up, 0)
            db = db + dbb - pltpu.roll(dbb, up, 0)
        dq_a[...], dk_a[...], dv_a[...], db_a[...] = dq, dk, dv, db
        vb = v.astype(BF16)
        for i in range(1, L // HG_SUB):
            r0 = i * HG_SUB
            blk = slice(r0, r0 + HG_SUB)
            qt, kt, e1, e2 = _hgrn_offdiag(q, k, b, r0)
            qtb, ktb, dob_i = qt.astype(BF16), kt.astype(BF16), do_[blk, :].astype(BF16)
            a = _nt(qtb, ktb).astype(BF16)
            da = _nt(dob_i, vb).astype(BF16)
            dv_a[...] += _tn(a, dob_i)
            dqt = _nn(da, ktb)
            dkt = _tn(da, qtb)
            dq_a[blk, :] += dqt * e1
            t1, t2 = dqt * qt, dkt * kt
            db_a[blk, :] += t1
            dk_a[...] += dkt * e2
            db_a[...] -= t2
            db_a[r0 - 1:r0, :] += jnp.sum(t2, axis=0, keepdims=True) - jnp.sum(t1, axis=0, keepdims=True)
        dq, dk, dv, db = dq_a[...], dk_a[...], dv_a[...], db_a[...]
        dstn = dst_s[hd]
        dstnb = dstn.astype(BF16)
        bl = b[L - 1:L, :]
        ebl = jnp.exp(bl)
        kdec_e = jnp.exp(bl - b)
        kdec = k * kdec_e
        dbl = jnp.sum(dstn * st, axis=0, keepdims=True) * ebl
        dv = dv + _nt(kdec.astype(BF16), dstnb)
        dkdec = _nn(v.astype(BF16), dstnb)
        dk = dk + dkdec * kdec_e
        dx = dkdec * kdec
        dbl = dbl + jnp.sum(dx, axis=0, keepdims=True)
        db = db - dx + jnp.where(rows == L - 1, dbl, 0.0)
        dst_s[hd] = dstn * ebl + dst_o
        dg = _rcumsum_rows(db)
        dfk = dg / f - dk
        dq_ref[rs, sl] = (dq * _dsilu(qp)).astype(BF16)
        di_ref[rs, sl] = (dv * _dsilu(ip)).astype(BF16)
        df_ref[rs, sl] = (dfk * (1.0 - lb) * sg * (1.0 - sg)).astype(BF16)
        dlb_s[hd] += jnp.sum(dfk * (1.0 - sg), axis=0, keepdims=True)

        if ck == 0:
            @pl.when(pl.program_id(0) == nc // cps - 1)
            def _():
                dl0 = dlb_s[hd] * lb * (1.0 - lb)
                dl_ref[0:1, sl] = dl0
                dl_ref[1:2, sl] = -dl0

    cps = REC_CHUNKS
    rv = lambda j: nc // cps - 1 - j
    hspec = lambda blk: pl.BlockSpec((cps * L, 512), lambda j: (rv(j), blk))
    return _pc(body, name, (nc // cps,),
               [hspec(4), hspec(5), hspec(6), pl.BlockSpec((2, 512), lambda j: (0, 0)),
                pl.BlockSpec((nh, cps, HD, HD), lambda j: (0, rv(j), 0, 0)), hspec(0)],
               [hspec(0), hspec(0), hspec(0), pl.BlockSpec((2, 512), lambda j: (0, 0))],
               [_sds((t, 512), BF16), _sds((t, 512), BF16), _sds((t, 512), BF16), _sds((2, 512))],
               scratch=[pltpu.VMEM((nh, HD, HD), F32), pltpu.VMEM((nh, 1, HD), F32)] + [pltpu.VMEM((nh, L, HD), F32)] * 4)(z0, z0, z0, lbl, ss, do)


def _post0_fwd(hm, hh, z0, na, nb, w, h0, name, bm=512):
    t = h0.shape[0]
    bm = min(bm, t)

    def body(hm_ref, hh_ref, oa_ref, gb_ref, na_ref, nb_ref, w_ref, h_ref, o_ref, y_ref):
        for hd in range(4):
            sl = slice(hd * HD, (hd + 1) * HD)
            pa = _sigmoid(oa_ref[:, sl]) * hm_ref[:, sl]
            y_ref[:, sl] = (pa * _rstd(pa) * na_ref[:, sl]).astype(BF16)
            xb = hh_ref[:, sl]
            y_ref[:, 512 + hd * HD:512 + (hd + 1) * HD] = (xb * _rstd(xb) * nb_ref[:, sl] * _silu(gb_ref[:, sl])).astype(BF16)
        o_ref[...] = h_ref[...] + _nn(y_ref[...], w_ref[...])

    row = lambda wd, c: pl.BlockSpec((bm, wd), lambda i: (i, c))
    vec = lambda wd: pl.BlockSpec((1, wd), lambda i: (0, 0))
    return _pc(body, name, (t // bm,),
               [row(512, 0), row(512, 0), row(512, 3), row(512, 7), vec(512), vec(512),
                pl.BlockSpec((D, D), lambda i: (0, 0)), row(D, 0)],
               [row(D, 0), row(D, 0)], [_sds((t, D)), _sds((t, D), BF16)])(hm, hh, z0, z0, na, nb, w, h0)


def _post0_bwd(dh1, w, hm, hh, z0, na, nb, name, bm=512):
    t = dh1.shape[0]
    bm = min(bm, t)

    def body(dh_ref, w_ref, hm_ref, hh_ref, oa_ref, gb_ref, na_ref, nb_ref, dhm_ref, dhh_ref, doa_ref, dgb_ref, dna_ref, dnb_ref):
        @pl.when(pl.program_id(0) == 0)
        def _():
            dna_ref[...] = jnp.zeros_like(dna_ref)
            dnb_ref[...] = jnp.zeros_like(dnb_ref)

        dy = _nt(dh_ref[...].astype(BF16), w_ref[...])
        for hd in range(4):
            sl = slice(hd * HD, (hd + 1) * HD)
            hm_, oa = hm_ref[:, sl], oa_ref[:, sl]
            sg = _sigmoid(oa)
            dpa, dgr = _rms_bwd(dy[:, sl], sg * hm_, na_ref[:, sl])
            dna_ref[:, sl] += jnp.sum(dgr, axis=0, keepdims=True)
            doa_ref[:, sl] = (dpa * hm_ * sg * (1.0 - sg)).astype(BF16)
            dhm_ref[:, sl] = dpa * sg
            xb, gb, nbv = hh_ref[:, sl], gb_ref[:, sl], nb_ref[:, sl]
            dyb = dy[:, 512 + hd * HD:512 + (hd + 1) * HD]
            dgb_ref[:, sl] = (dyb * (xb * _rstd(xb) * nbv) * _dsilu(gb)).astype(BF16)
            dxb, dgr2 = _rms_bwd(dyb * _silu(gb), xb, nbv)
            dnb_ref[:, sl] += jnp.sum(dgr2, axis=0, keepdims=True)
            dhh_ref[:, sl] = dxb

    row = lambda wd, c: pl.BlockSpec((bm, wd), lambda i: (i, c))
    vec = lambda wd: pl.BlockSpec((1, wd), lambda i: (0, 0))
    return _pc(body, name, (t // bm,),
               [row(D, 0), pl.BlockSpec((D, D), lambda i: (0, 0)), row(512, 0), row(512, 0), row(512, 3), row(512, 7),
                vec(512), vec(512)],
               [row(512, 0), row(512, 0), row(512, 0), row(512, 0), vec(512), vec(512)],
               [_sds((t, 512)), _sds((t, 512)), _sds((t, 512), BF16), _sds((t, 512), BF16), _sds((1, 512)), _sds((1, 512))],
               )(dh1, w, hm, hh, z0, z0, na, nb)


def _memkv_fwd(mem, g, wkv_s, name):
    m = mem.shape[0]

    def body(x_ref, g_ref, w_ref, kv_ref, mn_ref):
        x = x_ref[...]
        mn = (x * _rstd(x) * g_ref[...]).astype(BF16)
        mn_ref[...] = mn
        kv_ref[...] = _nn(mn, w_ref[...])

    return _pc(body, name, (4,),
               [pl.BlockSpec((m, D), lambda k: (0, 0)), pl.BlockSpec((1, D), lambda k: (0, 0)),
                pl.BlockSpec((None, D, 512), lambda k: (k, 0, 0))],
               [pl.BlockSpec((m, 512), lambda k: (0, k)), pl.BlockSpec((m, D), lambda k: (0, 0))],
               [_sds((m, 2048)), _sds((m, D), BF16)])(mem, g, wkv_s)


def _memkv_bwd(dkv, wkv_s, mem, g, name):
    m = mem.shape[0]

    def body(d_ref, w_ref, x_ref, g_ref, dg_ref, acc):
        k = pl.program_id(0)

        @pl.when(k == 0)
        def _():
            acc[...] = jnp.zeros_like(acc)

        acc[...] += _nt(d_ref[...].astype(BF16), w_ref[...])

        @pl.when(k == 3)
        def _():
            _, dgr = _rms_bwd(acc[...], x_ref[...], g_ref[...])
            dg_ref[...] = jnp.sum(dgr, axis=0, keepdims=True)

    return _pc(body, name, (4,),
               [pl.BlockSpec((m, 512), lambda k: (0, k)), pl.BlockSpec((None, D, 512), lambda k: (k, 0, 0)),
                pl.BlockSpec((m, D), lambda k: (0, 0)), pl.BlockSpec((1, D), lambda k: (0, 0))],
               pl.BlockSpec((1, D), lambda k: (0, 0)), _sds((1, D)), scratch=[pltpu.VMEM((m, D), F32)])(dkv, wkv_s, mem, g)


def _xattn_probs(qh, kh):
    s = _nt(qh, kh) * (XD ** -0.5)
    p = jnp.exp(s - jnp.max(s, axis=1, keepdims=True))
    return p / jnp.sum(p, axis=1, keepdims=True)


def _xattn_fwd(q, kv, wo, h1, name, bm=512):
    t, m = q.shape[0], kv.shape[0]
    bm = min(bm, t)

    def body(q_ref, k_ref, v_ref, w_ref, h_ref, out_ref, o_ref):
        for hd in range(D // XD):
            sl = slice(hd * XD, (hd + 1) * XD)
            p = _xattn_probs(q_ref[:, sl].astype(BF16), k_ref[:, sl].astype(BF16))
            o_ref[:, sl] = _nn(p.astype(BF16), v_ref[:, sl].astype(BF16)).astype(BF16)
        out_ref[...] = h_ref[...] + _nn(o_ref[...], w_ref[...])

    row = pl.BlockSpec((bm, D), lambda i: (i, 0))
    return _pc(body, name, (t // bm,),
               [row, pl.BlockSpec((m, D), lambda i: (0, 0)), pl.BlockSpec((m, D), lambda i: (0, 1)),
                pl.BlockSpec((D, D), lambda i: (0, 0)), row],
               [row, row], [_sds((t, D)), _sds((t, D), BF16)])(q, kv, kv, wo, h1)


def _xattn_bwd(dh2, q, kv, wo, name, bm=512):
    t, m = q.shape[0], kv.shape[0]
    bm = min(bm, t)

    def body(dh_ref, q_ref, k_ref, v_ref, w_ref, dq_ref, dkv_ref):
        @pl.when(pl.program_id(0) == 0)
        def _():
            dkv_ref[...] = jnp.zeros_like(dkv_ref)

        d_o = _nt(dh_ref[...].astype(BF16), w_ref[...])
        for hd in range(D // XD):
            sl = slice(hd * XD, (hd + 1) * XD)
            qh, kh, vh = q_ref[:, sl].astype(BF16), k_ref[:, sl].astype(BF16), v_ref[:, sl].astype(BF16)
            p = _xattn_probs(qh, kh)
            dob = d_o[:, sl].astype(BF16)
            dp = _nt(dob, vh)
            dkv_ref[:, D + hd * XD:D + (hd + 1) * XD] += _tn(p.astype(BF16), dob)
            ds = (p * (dp - jnp.sum(dp * p, axis=1, keepdims=True)) * (XD ** -0.5)).astype(BF16)
            dq_ref[:, sl] = _nn(ds, kh).astype(BF16)
            dkv_ref[:, sl] += _tn(ds, qh)

    row = pl.BlockSpec((bm, D), lambda i: (i, 0))
    return _pc(body, name, (t // bm,),
               [row, row, pl.BlockSpec((m, D), lambda i: (0, 0)), pl.BlockSpec((m, D), lambda i: (0, 1)),
                pl.BlockSpec((D, D), lambda i: (0, 0))],
               [row, pl.BlockSpec((m, 2 * D), lambda i: (0, 0))],
               [_sds((t, D), BF16), _sds((m, 2 * D))])(dh2, q, kv, kv, wo)


NH1 = 8
FOX_BM = 512
FOX_BQ = 512
FOX_BK = 512
FOX_HEADS_PER_STEP = 2


def _foxprep_fwd(z1, qg, kg, fbp, name):
    t = z1.shape[0]
    bm = min(FOX_BM, t)

    def body(q_ref, k_ref, v_ref, f_ref, qg_ref, kg_ref, fb_ref, qn_ref, kn_ref, vb_ref, c_ref, carry):
        @pl.when(pl.program_id(0) == 0)
        def _():
            carry[...] = jnp.zeros_like(carry)

        for hd in range(NH1):
            sl = slice(hd * HD, (hd + 1) * HD)
            x = q_ref[:, sl]
            qn_ref[:, sl] = (x * _rstd(x) * qg_ref[...] * FOX_QSCALE).astype(BF16)
            x = k_ref[:, sl]
            kn_ref[:, sl] = (x * _rstd(x) * kg_ref[...]).astype(BF16)
        vb_ref[...] = v_ref[...].astype(BF16)
        c = carry[...] + _cumsum_rows(_log_sigmoid(f_ref[...] + fb_ref[...]))
        c_ref[...] = c
        carry[...] = c[bm - 1:bm, :]

    row = lambda c: pl.BlockSpec((bm, D), lambda i: (i, c))
    lane = pl.BlockSpec((bm, HD), lambda i: (i, 4 * D // HD))
    vec = pl.BlockSpec((1, HD), lambda i: (0, 0))
    return _pc(body, name, (t // bm,), [row(0), row(1), row(2), lane, vec, vec, vec],
               [row(0), row(0), row(0), pl.BlockSpec((bm, HD), lambda i: (i, 0))],
               [_sds((t, D), BF16), _sds((t, D), BF16), _sds((t, D), BF16), _sds((t, HD))],
               scratch=[pltpu.VMEM((1, HD), F32)])(z1, z1, z1, z1, qg, kg, fbp)


def _foxprep_bwd(dqn, dkn, z1, qg, kg, fbp, dc, name):
    t = z1.shape[0]
    bm = min(FOX_BM, t)
    nb = t // bm

    def body(dqn_ref, dkn_ref, q_ref, k_ref, f_ref, qg_ref, kg_ref, fb_ref, dc_ref,
             dq_ref, dk_ref, df_ref, dqg_ref, dkg_ref, dfb_ref, carry):
        @pl.when(pl.program_id(0) == 0)
        def _():
            carry[...] = jnp.zeros_like(carry)
            dqg_ref[...] = jnp.zeros_like(dqg_ref)
            dkg_ref[...] = jnp.zeros_like(dkg_ref)
            dfb_ref[...] = jnp.zeros_like(dfb_ref)

        for hd in range(NH1):
            sl = slice(hd * HD, (hd + 1) * HD)
            dx, dgr = _rms_bwd(dqn_ref[:, sl] * (HD ** -0.5), q_ref[:, sl], qg_ref[...])
            dq_ref[:, sl] = dx.astype(BF16)
            dqg_ref[...] += jnp.sum(dgr, axis=0, keepdims=True)
            dx, dgr = _rms_bwd(dkn_ref[:, sl], k_ref[:, sl], kg_ref[...])
            dk_ref[:, sl] = dx.astype(BF16)
            dkg_ref[...] += jnp.sum(dgr, axis=0, keepdims=True)
        dc_ = dc_ref[...]
        dlogf = _rcumsum_rows(dc_) + carry[...]
        carry[...] += jnp.sum(dc_, axis=0, keepdims=True)
        lanes = lax.broadcasted_iota(jnp.int32, dc_.shape, 1)
        df = jnp.where(lanes < NH1, dlogf * (1.0 - _sigmoid(f_ref[...] + fb_ref[...])), 0.0)
        df_ref[...] = df.astype(BF16)
        dfb_ref[...] += jnp.sum(df, axis=0, keepdims=True)

    rv = lambda i: nb - 1 - i
    row = lambda c: pl.BlockSpec((bm, D), lambda i: (rv(i), c))
    lane = lambda c: pl.BlockSpec((bm, HD), lambda i: (rv(i), c))
    vec = pl.BlockSpec((1, HD), lambda i: (0, 0))
    return _pc(body, name, (nb,), [row(0), row(0), row(0), row(1), lane(4 * D // HD), vec, vec, vec, lane(0)],
               [row(0), row(0), lane(0), vec, vec, vec],
               [_sds((t, D), BF16), _sds((t, D), BF16), _sds((t, HD), BF16), _sds((1, HD)), _sds((1, HD)), _sds((1, HD))],
               scratch=[pltpu.VMEM((1, HD), F32)])(dqn, dkn, z1, z1, z1, qg, kg, fbp, dc)


LOG2E = 1.4426950408889634
FOX_QSCALE = HD ** -0.5 * LOG2E


def _fox_steps(t, bq, bk, k_major):
    nq, nk = t // bq, t // bk
    pairs = [(i, j) for i in range(nq) for j in range(nk) if j * bk < (i + 1) * bq]
    if k_major:
        pairs.sort(key=lambda p: (p[1], p[0]))
    outer = [p[1] if k_major else p[0] for p in pairs]
    n = len(pairs)
    flags = [(n_ == 0 or outer[n_] != outer[n_ - 1]) + 2 * (n_ == n - 1 or outer[n_] != outer[n_ + 1])
             + 4 * (not (j + 1) * bk <= i * bq + 1) for n_, (i, j) in enumerate(pairs)]
    as_i32 = lambda v: jnp.asarray(v, jnp.int32)
    return as_i32([p[0] for p in pairs]), as_i32([p[1] for p in pairs]), as_i32(flags)


def _fox_step_info(qi_ref, kj_ref, fl_ref):
    s = pl.program_id(1)
    fl = fl_ref[s]
    return qi_ref[s], kj_ref[s], (fl & 1) != 0, (fl & 2) != 0, (fl & 4) != 0


def _fox_call(body, name, tables, in_specs, out_specs, out_shape, scratch):
    grid_spec = pltpu.PrefetchScalarGridSpec(num_scalar_prefetch=3, grid=(NH1 // FOX_HEADS_PER_STEP, tables[0].shape[0]),
                                             in_specs=in_specs, out_specs=out_specs, scratch_shapes=scratch)
    return pl.pallas_call(body, name=name, grid_spec=grid_spec, out_shape=out_shape,
                          compiler_params=pltpu.CompilerParams(dimension_semantics=("arbitrary", "arbitrary"),
                                                               vmem_limit_bytes=VMEM_LIMIT_V7X))


def _fox_lane_tiles(x):
    return [x[:, c0:c0 + HD] for c0 in range(0, x.shape[1], HD)]


def _fox_masked_scores(q, k, ck, i, j, bq, bk, masked):
    s = _nt(q, k) - ck
    if masked:
        rows = i * bq + lax.broadcasted_iota(jnp.int32, s.shape, 0)
        cols = j * bk + lax.broadcasted_iota(jnp.int32, s.shape, 1)
        s = jnp.where(cols <= rows, s, NEG)
    return s


def _fox_specs(bq, bk, G):
    qspec = pl.BlockSpec((bq, G * HD), lambda h, s, qi, kj, fl: (qi[s], h))
    kspec = pl.BlockSpec((bk, G * HD), lambda h, s, qi, kj, fl: (kj[s], h))
    cspec = pl.BlockSpec((G, 1, bk), lambda h, s, qi, kj, fl: (h, 0, kj[s]))
    colspec = pl.BlockSpec((G, bq, 1), lambda h, s, qi, kj, fl: (h, qi[s], 0))
    return qspec, kspec, cspec, colspec


def _fox_rowmax(qn, kn, crow, name):
    t = qn.shape[0]
    bq, bk, G = min(FOX_BQ, t), min(2 * FOX_BK, t), FOX_HEADS_PER_STEP
    tables = _fox_steps(t, bq, bk, k_major=False)

    def body(qi_ref, kj_ref, fl_ref, q_ref, k_ref, ck_ref, m_ref, *mp):
        i, j, first, last, diag = _fox_step_info(qi_ref, kj_ref, fl_ref)

        @pl.when(first)
        def _():
            for g in range(G):
                mp[g][...] = jnp.full_like(mp[g], NEG)

        def step(masked):
            for g in range(G):
                sl = slice(g * HD, (g + 1) * HD)
                s = _fox_masked_scores(q_ref[:, sl], k_ref[:, sl], ck_ref[g], i, j, bq, bk, masked)
                m = mp[g][...]
                for tile in _fox_lane_tiles(s):
                    m = jnp.maximum(m, tile)
                mp[g][...] = m

        pl.when(jnp.logical_not(diag))(lambda: step(False))
        pl.when(diag)(lambda: step(True))

        @pl.when(last)
        def _():
            for g in range(G):
                m_ref[g] = jnp.max(mp[g][...], axis=1, keepdims=True)

    qspec, kspec, cspec, colspec = _fox_specs(bq, bk, G)
    return _fox_call(body, name, tables, [qspec, kspec, cspec], colspec, _sds((NH1, t, 1)),
                     [pltpu.VMEM((bq, HD), F32)] * G)(*tables, qn, kn, crow)


def _fox_fwd(qn, kn, vb, crow, m, name):
    t = qn.shape[0]
    bq, bk, G = min(FOX_BQ, t), min(FOX_BK, t), FOX_HEADS_PER_STEP
    tables = _fox_steps(t, bq, bk, k_major=False)

    def body(qi_ref, kj_ref, fl_ref, q_ref, k_ref, v_ref, ck_ref, m_ref, o_ref, lse_ref, *scr):
        i, j, first, last, diag = _fox_step_info(qi_ref, kj_ref, fl_ref)
        lp, acc = scr[:G], scr[G:]

        @pl.when(first)
        def _():
            for g in range(G):
                lp[g][...] = jnp.zeros_like(lp[g])
                acc[g][...] = jnp.zeros_like(acc[g])

        def step(masked):
            for g in range(G):
                sl = slice(g * HD, (g + 1) * HD)
                s = _fox_masked_scores(q_ref[:, sl], k_ref[:, sl], ck_ref[g], i, j, bq, bk, masked)
                p = jnp.exp2(s - m_ref[g])
                l = lp[g][...]
                for tile in _fox_lane_tiles(p):
                    l = l + tile
                lp[g][...] = l
                acc[g][...] += _nn(p.astype(BF16), v_ref[:, sl])

        pl.when(jnp.logical_not(diag))(lambda: step(False))
        pl.when(diag)(lambda: step(True))

        @pl.when(last)
        def _():
            for g in range(G):
                l = jnp.sum(lp[g][...], axis=1, keepdims=True)
                o_ref[:, g * HD:(g + 1) * HD] = acc[g][...] / l
                lse_ref[g] = m_ref[g] + jnp.log2(l)

    qspec, kspec, cspec, colspec = _fox_specs(bq, bk, G)
    return _fox_call(body, name, tables, [qspec, kspec, kspec, cspec, colspec], [qspec, colspec],
                     [_sds((t, D)), _sds((NH1, t, 1))], [pltpu.VMEM((bq, HD), F32)] * (2 * G))(*tables, qn, kn, vb, crow, m)


def _fox_bwd(qn, kn, vb, crow, lse, delta, do, name):
    t = qn.shape[0]
    bq, bk, G = min(FOX_BQ, t), min(FOX_BK, t), FOX_HEADS_PER_STEP
    tables = _fox_steps(t, bq, bk, k_major=True)

    def body(qi_ref, kj_ref, fl_ref, q_ref, k_ref, v_ref, ck_ref, lse_ref, dl_ref, do_ref, dq_ref, dk_ref, dv_ref, dc_ref, dcq_ref,
             dk_s, dv_s, dc_s):
        i, j, first, last, diag = _fox_step_info(qi_ref, kj_ref, fl_ref)

        @pl.when(first)
        def _():
            dk_s[...] = jnp.zeros_like(dk_s)
            dv_s[...] = jnp.zeros_like(dv_s)
            dc_s[...] = jnp.zeros_like(dc_s)

        @pl.when(pl.program_id(1) == 0)
        def _():
            dq_ref[...] = jnp.zeros_like(dq_ref)
            dcq_ref[...] = jnp.zeros_like(dcq_ref)

        def step(masked):
            rows = pl.ds(pl.multiple_of(i * bq, bq), bq)
            for g in range(G):
                sl = slice(g * HD, (g + 1) * HD)
                q, k = q_ref[:, sl], k_ref[:, sl]
                s = _fox_masked_scores(q, k, ck_ref[g], i, j, bq, bk, masked)
                p = jnp.exp2(s - lse_ref[g])
                dob = do_ref[:, sl]
                dv_s[:, sl] += _tn(p.astype(BF16), dob)
                ds = p * (_nt(dob, v_ref[:, sl]) - dl_ref[g])
                dsb = ds.astype(BF16)
                dq_ref[rows, sl] += _nn(dsb, k)
                dk_s[:, sl] += _tn(dsb, q)
                dc_s[g] -= jnp.sum(ds, axis=0, keepdims=True)
                part_sum = dcq_ref[g, rows, :]
                for tile in _fox_lane_tiles(ds):
                    part_sum = part_sum + tile
                dcq_ref[g, rows, :] = part_sum

        pl.when(jnp.logical_not(diag))(lambda: step(False))
        pl.when(diag)(lambda: step(True))

        @pl.when(last)
        def _():
            dk_ref[...] = dk_s[...] * (1.0 / LOG2E)
            dv_ref[...] = dv_s[...]
            dc_ref[...] = dc_s[...]

    qspec, kspec, cspec, colspec = _fox_specs(bq, bk, G)
    return _fox_call(
        body, name, tables, [qspec, kspec, kspec, cspec, colspec, colspec, qspec],
        [pl.BlockSpec((t, G * HD), lambda h, s, qi, kj, fl: (0, h)), kspec, kspec, cspec,
         pl.BlockSpec((G, t, HD), lambda h, s, qi, kj, fl: (h, 0, 0))],
        [_sds((t, D)), _sds((t, D)), _sds((t, D)), _sds((NH1, 1, t)), _sds((NH1, t, HD))],
        [pltpu.VMEM((bk, G * HD), F32), pltpu.VMEM((bk, G * HD), F32), pltpu.VMEM((G, 1, bk), F32)],
    )(*tables, qn, kn, vb, crow, lse, delta, do)


def _post1_fwd(o, z1, w, h3, name, bm=512):
    t = o.shape[0]
    bm = min(bm, t)

    def body(o_ref, g_ref, w_ref, h_ref, out_ref, og_ref):
        og_ref[...] = (o_ref[...] * _sigmoid(g_ref[...])).astype(BF16)
        out_ref[...] = h_ref[...] + _nn(og_ref[...], w_ref[...])

    row = lambda c: pl.BlockSpec((bm, D), lambda i: (i, c))
    return _pc(body, name, (t // bm,), [row(0), row(3), pl.BlockSpec((D, D), lambda i: (0, 0)), row(0)],
               [row(0), row(0)], [_sds((t, D)), _sds((t, D), BF16)])(o, z1, w, h3)


def _post1_bwd(dh4, w, o, z1, name, bm=512):
    t = o.shape[0]
    bm = min(bm, t)

    def body(dh_ref, w_ref, o_ref, g_ref, do_ref, dg_ref, dl_ref):
        d_og = _nt(dh_ref[...].astype(BF16), w_ref[...])
        o_, sg = o_ref[...], _sigmoid(g_ref[...])
        dob = (d_og * sg).astype(BF16)
        do_ref[...] = dob
        dg_ref[...] = (d_og * o_ * sg * (1.0 - sg)).astype(BF16)
        prod = dob.astype(F32) * o_
        for hd in range(NH1):
            dl_ref[hd] = jnp.sum(prod[:, hd * HD:(hd + 1) * HD], axis=1, keepdims=True)

    row = lambda c: pl.BlockSpec((bm, D), lambda i: (i, c))
    return _pc(body, name, (t // bm,), [row(0), pl.BlockSpec((D, D), lambda i: (0, 0)), row(0), row(3)],
               [row(0), row(0), pl.BlockSpec((NH1, bm, 1), lambda i: (0, i, 0))],
               [_sds((t, D), BF16), _sds((t, D), BF16), _sds((NH1, t, 1))])(dh4, w, o, z1)


def _final(h, g, tgt, name, bm=512):
    t = h.shape[0]
    bm = min(bm, t)

    def body(h_ref, g_ref, t_ref, l_ref, dh_ref, dg_ref):
        @pl.when(pl.program_id(0) == 0)
        def _():
            l_ref[...] = jnp.zeros_like(l_ref)
            dg_ref[...] = jnp.zeros_like(dg_ref)

        x, gv = h_ref[...], g_ref[...]
        r = _rstd(x)
        xh = x * r
        e = xh * gv - t_ref[...]
        l_ref[...] += 0.5 * jnp.sum(jnp.mean(e * e, axis=1, keepdims=True), axis=0, keepdims=True)
        dy = e * (1.0 / D)
        dg_ref[...] += jnp.sum(dy * xh, axis=0, keepdims=True)
        dxh = dy * gv
        dh_ref[...] = r * (dxh - xh * jnp.mean(dxh * xh, axis=1, keepdims=True))

    row = pl.BlockSpec((bm, D), lambda i: (i, 0))
    vec = pl.BlockSpec((1, D), lambda i: (0, 0))
    return _pc(body, name, (t // bm,), [row, vec, row], [pl.BlockSpec((1, HD), lambda i: (0, 0)), row, vec],
               [_sds((1, HD)), _sds((t, D)), _sds((1, D))])(h, g, tgt)


def _adam(w, g, m, v, name):
    r, c = w.shape
    br = min(r, 256)

    def body(w_ref, g_ref, m_ref, v_ref, d_ref, mo_ref, vo_ref):
        gv = g_ref[...]
        mn = ADAM_B1 * m_ref[...] + (1.0 - ADAM_B1) * gv
        vn = ADAM_B2 * v_ref[...] + (1.0 - ADAM_B2) * jnp.square(gv)
        m_hat = mn / (1.0 - ADAM_B1 ** ADAM_STEP)
        v_hat = vn / (1.0 - ADAM_B2 ** ADAM_STEP)
        d_ref[...] = -ADAM_LR * (m_hat / (jnp.sqrt(v_hat) + ADAM_EPS) + ADAM_WD * w_ref[...])
        mo_ref[...] = mn
        vo_ref[...] = vn

    blk = pl.BlockSpec((br, c), lambda i: (i, 0))
    return _pc(body, name, (r // br,), [blk] * 4, [blk] * 3, [_sds((r, c))] * 3)(w, g, m, v)


ZW = 4224
GATE0 = 4096


def _pack_w_in0(w):
    return jnp.concatenate([w[:, :2048], w[:, 2056:], w[:, 2048:2056], jnp.zeros((w.shape[0], ZW - 4104), w.dtype)], axis=1)


def _unpack_w_in0(g):
    return jnp.concatenate([g[:, :2048], g[:, GATE0:GATE0 + 8], g[:, 2048:GATE0]], axis=1)


def _pack_w_in1(w):
    return jnp.concatenate([w, jnp.zeros((w.shape[0], ZW - 4104), w.dtype)], axis=1)


def _unpack_w_in1(g):
    return g[:, :4104]


def _local_step(x, mem, tgt, W, S, late_weights=None, grads_hook=None):
    t = x.shape[0]
    row = lambda v: v.reshape(1, -1)
    G = {}

    z0, u0 = _norm_mm(x, S["norm_mix_g"][0:1], W["w_in0"], "in0_fwd")
    qk = _conv_fwd(z0, S["conv_w"], "conv_fwd")
    g8 = z0[:, GATE0:GATE0 + 8]
    gates3 = jnp.stack([g8[:, :4].T, g8[:, 4:].T], axis=-1)
    gb = S["gate_b"]
    bias3 = jnp.stack([gb[0, :4], gb[0, 4:]], axis=-1)[:, None, :]
    hm, cs, ns, ms = _mlstm_fwd(qk, z0, gates3, bias3, "mlstm_fwd")
    hh, ss = _hgrn_fwd(z0, S["lb_logits"], "hgrn_fwd")
    if late_weights is not None:
        W = {**W, **late_weights(hh)}
    kv, mn = _memkv_fwd(mem, row(S["mem_norm_g"]), W["wkv_s"], "memkv_fwd")
    h1, y0 = _post0_fwd(hm, hh, z0, S["mlstm_norm_g"], S["hgrn_norm_g"], W["w_out0"], x, "post0_fwd")

    def xattn_mlp_fwd(h, l):
        q, ux = _norm_mm(h, S["norm_xattn_g"][l:l + 1], W["wq"][l], f"xq{l}_fwd")
        h2, ox = _xattn_fwd(q, kv, W["wo"][l], h, f"xattn{l}_fwd")
        h3, a, um = _mlp_fwd(h2, S["norm_mlp_g"][l:l + 1], W["w1s"], W["w2"], l, f"mlp{l}_fwd")
        return h3, (h, q, ux, ox, h2, a, um)

    h3, sv0 = xattn_mlp_fwd(h1, 0)
    z1, u1 = _norm_mm(h3, S["norm_mix_g"][1:2], W["w_in1"], "in1_fwd")
    fbp = jnp.pad(S["c_fgate_b"], ((0, 0), (0, HD - NH1)))
    qn, kn, vb, c = _foxprep_fwd(z1, S["c_qnorm_g"], S["c_knorm_g"], fbp, "foxprep_fwd")
    crow = (c[:, :NH1] * LOG2E).T[:, None, :]
    o1, lse = _fox_fwd(qn, kn, vb, crow, _fox_rowmax(qn, kn, crow, "fox_rowmax"), "fox_fwd")
    h4, og = _post1_fwd(o1, z1, W["w_out1"], h3, "post1_fwd")
    h6, sv1 = xattn_mlp_fwd(h4, 1)
    lossp, dh, G["final_norm_g"] = _final(h6, row(S["final_norm_g"]), tgt, "final")

    grads_ready = grads_hook if grads_hook is not None else (lambda stage, grads: 0.0)
    dkv = None
    dgx, dgm, dwq, dwo, dw1, dw2 = [None, None], [None, None], [None, None], [None, None], [None, None], [None, None]

    def xattn_mlp_bwd(dh, l, sv):
        nonlocal dkv
        h, q, ux, ox, h2, a, um = sv
        dh2, da, r, dgm[l] = _mlp_bwd(dh, a, W["w1s"], W["w2"], l, h2, S["norm_mlp_g"][l:l + 1], f"mlp{l}_bwd")
        dw1[l] = _mm_tn(um, da, f"mlp{l}_dw1", col_chips=NCHIP)
        dw2[l] = _mm_tn(r, dh, f"mlp{l}_dw2")
        dq, dkv_l = _xattn_bwd(dh2, q, kv, W["wo"][l], f"xattn{l}_bwd")
        dkv = dkv_l if dkv is None else dkv + dkv_l
        dwo[l] = _mm_tn(ox, dh2, f"xattn{l}_dwo")
        dwq[l] = _mm_tn(ux, dq, f"xattn{l}_dwq")
        tok = grads_ready("layer0_mlp_xattn", dict(wq=dwq[0], wo=dwo[0], w1=dw1[0], w2=dw2[0])) if l == 0 else 0.0
        dh1, dgx[l] = _bwd_in(dq, W["wq"][l], h, S["norm_xattn_g"][l:l + 1] + tok, dh2, f"xq{l}_bwd")
        return dh1

    dh4 = xattn_mlp_bwd(dh, 1, sv1)
    do, dgate, delta = _post1_bwd(dh4, W["w_out1"], o1, z1, "post1_bwd")
    G["w_out1"] = _mm_tn(og, dh4, "post1_dw")
    dqn, dkn, dv1, dcrow, dcq = _fox_bwd(qn, kn, vb, crow, lse, delta, do, "fox_bwd")
    dc = jnp.pad((dcrow[:, 0, :] + jnp.sum(dcq, axis=-1)).T, ((0, 0), (0, HD - NH1)))
    dqr, dkr, df1, G["c_qnorm_g"], G["c_knorm_g"], dfb = _foxprep_bwd(
        dqn, dkn, z1, S["c_qnorm_g"], S["c_knorm_g"], fbp, dc, "foxprep_bwd")
    G["c_fgate_b"] = dfb[:, :NH1]
    dz1 = jnp.concatenate([dqr, dkr, dv1.astype(BF16), dgate, df1], axis=1)
    G["w_in1"] = _mm_tn(u1, dz1, "in1_dw")
    tok = grads_ready("layer1", dict(w_out=G["w_out1"], w_in=G["w_in1"], wq=dwq[1], wo=dwo[1], w1=dw1[1], w2=dw2[1]))
    dh3, dgmix1 = _bwd_in(dz1, W["w_in1"], h3, S["norm_mix_g"][1:2] + tok, dh4, "in1_bwd")
    dh1 = xattn_mlp_bwd(dh3, 0, sv0)

    dhm, dhh, doa, dgb, G["mlstm_norm_g"], G["hgrn_norm_g"] = _post0_bwd(
        dh1, W["w_out0"], hm, hh, z0, S["mlstm_norm_g"], S["hgrn_norm_g"], "post0_bwd")
    G["w_out0"] = _mm_tn(y0, dh1, "post0_dw")
    dqa, dka, dva, dgates3 = _mlstm_bwd(qk, z0, gates3, bias3, cs, ns, ms, dhm, "mlstm_bwd")
    dqb, dfb0, dib, G["lb_logits"] = _hgrn_bwd(z0, S["lb_logits"], ss, dhh, "hgrn_bwd")
    duc, G["conv_w"] = _conv_bwd(z0, S["conv_w"], jnp.concatenate([dqa, dka], axis=1), "conv_bwd")
    dg8 = jnp.concatenate([dgates3[:, :, 0].T, dgates3[:, :, 1].T], axis=1)
    G["gate_b"] = jnp.sum(dg8, axis=0, keepdims=True)
    dz0 = jnp.concatenate([duc, dva.astype(BF16), doa, dqb, dfb0, dib, dgb,
                           jnp.pad(dg8, ((0, 0), (0, HD - 8))).astype(BF16)], axis=1)
    G["w_in0"] = _mm_tn(u0, dz0, "in0_dw")
    dx, dgmix0 = _bwd_in(dz0, W["w_in0"], x, S["norm_mix_g"][0:1], dh1, "in0_bwd")

    G["wkv"] = _mm_tn(mn, dkv, "memkv_dw", col_chips=NCHIP)
    G["mem_norm_g"] = _memkv_bwd(dkv, W["wkv_s"], mem, row(S["mem_norm_g"]), "memkv_bwd")
    G["norm_mix_g"] = jnp.concatenate([dgmix0, dgmix1], axis=0)
    G["norm_xattn_g"] = jnp.concatenate(dgx, axis=0)
    G["norm_mlp_g"] = jnp.concatenate(dgm, axis=0)
    G["wq"], G["wo"], G["w1"], G["w2"] = dwq, dwo, dw1, dw2
    return lossp[0, 0], dx, G


ANY = pl.BlockSpec(memory_space=pl.ANY)
NCHIP = 4


def _place():
    x, y, c = lax.axis_index("x"), lax.axis_index("y"), lax.axis_index("c")
    return x, y, c, [(1 - x, y), (x, 1 - y), (1 - x, 1 - y)]


def _comm_call(body, name, ins, out_shapes, sems):
    return pl.pallas_call(body, name=name, in_specs=[ANY] * len(ins), out_specs=[ANY] * len(out_shapes),
                          out_shape=out_shapes, scratch_shapes=sems)(*ins)


def _gather_weights(arrs, name):
    n = len(arrs)

    def body(*refs):
        ins, outs = refs[:n], refs[n:2 * n]
        send_i, recv_i, send_d, recv_d = refs[2 * n:]
        x, y, c, chips = _place()
        me = 2 * x + y

        def half(a, cc):
            h = arrs[a].shape[0] // 2
            return pl.ds(pl.multiple_of(cc * h, h), h)

        def ici(a, k, src_chip, dst_dev):
            return pltpu.make_async_remote_copy(
                src_ref=ins[a].at[half(a, c)], dst_ref=outs[a].at[src_chip, half(a, c)], send_sem=send_i.at[a, k],
                recv_sem=recv_i.at[a, k], device_id=dst_dev, device_id_type=MESH)

        def d2d(a, k, src_chip, cc):
            reg = outs[a].at[src_chip, half(a, cc)]
            return pltpu.make_async_remote_copy(src_ref=reg, dst_ref=reg, send_sem=send_d.at[a, k], recv_sem=recv_d.at[a, k],
                                                device_id=(x, y, 1 - c), device_id_type=MESH)

        for a in range(n):
            for k, (px, py) in enumerate(chips):
                ici(a, k, me, (px, py, c)).start()
        for k, (px, py) in enumerate(chips):
            for a in range(n):
                ici(a, k, 2 * px + py, (px, py, c)).wait_recv()
                d2d(a, k, 2 * px + py, c).start()
        for k, (px, py) in enumerate(chips):
            for a in range(n):
                ici(a, k, me, (px, py, c)).wait_send()
                d2d(a, k, 2 * px + py, c).wait_send()
                d2d(a, k, 2 * px + py, 1 - c).wait_recv()

    sem = lambda: pltpu.SemaphoreType.DMA((n, 3))
    return _comm_call(body, name, arrs, [_sds((NCHIP,) + a.shape, a.dtype) for a in arrs], [sem(), sem(), sem(), sem()])


HBM = pl.BlockSpec(memory_space=pltpu.HBM)
SEM = pl.BlockSpec(memory_space=pltpu.SEMAPHORE)
DATAFLOW = pltpu.SideEffectType.DATAFLOW_SIDE_EFFECTING


def _half_rows(r, cc):
    return pl.ds(pl.multiple_of(cc * (r // 2), r // 2), r // 2)


def _gather_start(arrs, after, name):
    n = len(arrs)

    def body(*refs):
        ins, lands = refs[:n], refs[n:2 * n]
        send, recv, token = refs[2 * n + 1], refs[2 * n + 2], refs[-1]
        x, y, c, chips = _place()
        me = 2 * x + y
        for a in range(n):
            rows = _half_rows(arrs[a].shape[0], c)
            for k, (px, py) in enumerate(chips):
                pltpu.make_async_remote_copy(src_ref=ins[a].at[rows], dst_ref=lands[a].at[me, rows], send_sem=send.at[3 * a + k],
                                             recv_sem=recv.at[3 * a + k], device_id=(px, py, c), device_id_type=MESH).start()
        token[...] = jnp.zeros_like(token)

    hbm = lambda v: pltpu.with_memory_space_constraint(v, pltpu.HBM)
    land_shapes = [((NCHIP,) + a.shape, a.dtype) for a in arrs]
    out = pl.pallas_call(
        body, name=name,
        out_shape=(pltpu.SemaphoreType.DMA((3 * n,)), pltpu.SemaphoreType.DMA((3 * n,)), *[pltpu.HBM(a.shape, a.dtype) for a in arrs],
                   *[pltpu.HBM(s, d) for s, d in land_shapes], _sds((8, HD))),
        in_specs=[HBM] * (2 * n) + [ANY], out_specs=(SEM, SEM, *[HBM] * (2 * n), pl.BlockSpec(memory_space=pltpu.VMEM)),
        input_output_aliases={i: 2 + i for i in range(2 * n)},
        compiler_params=pltpu.CompilerParams(has_side_effects=DATAFLOW),
    )(*[hbm(a) for a in arrs], *[hbm(lax.empty(s, d)) for s, d in land_shapes], after)
    return out[0], out[1], list(out[2:2 + n]), list(out[2 + n:2 + 2 * n]), out[-1]


def _gather_wait(send, recv, srcs, lands, after, name):
    n = len(srcs)

    def body(*refs):
        ins, lands_ = refs[:n], refs[n:2 * n]
        send_, recv_ = refs[2 * n], refs[2 * n + 1]
        x, y, c, chips = _place()
        for a in range(n):
            rows = _half_rows(srcs[a].shape[0], c)
            for k, (px, py) in enumerate(chips):
                cp = pltpu.make_async_remote_copy(src_ref=ins[a].at[rows], dst_ref=lands_[a].at[2 * px + py, rows], send_sem=send_.at[3 * a + k],
                                                  recv_sem=recv_.at[3 * a + k], device_id=(px, py, c), device_id_type=MESH)
                cp.wait_send()
                cp.wait_recv()

    out = pl.pallas_call(
        body, name=name, out_shape=[pltpu.HBM(v.shape, v.dtype) for v in list(srcs) + list(lands)],
        in_specs=[HBM] * (2 * n) + [SEM, SEM, ANY], out_specs=[HBM] * (2 * n), input_output_aliases={i: i for i in range(2 * n)},
        compiler_params=pltpu.CompilerParams(has_side_effects=DATAFLOW),
    )(*srcs, *lands, send, recv, after)
    return list(out[n:])


def _pair_forward(lands, name):
    n = len(lands)

    def body(*refs):
        ins, outs = refs[:n], refs[n:2 * n]
        send, recv = refs[2 * n:]
        x, y, c, chips = _place()
        copies = []
        for a in range(n):
            r = lands[a].shape[1]
            for k, (px, py) in enumerate(chips):
                cp = pltpu.make_async_remote_copy(
                    src_ref=ins[a].at[2 * px + py, _half_rows(r, c)], dst_ref=outs[a].at[2 * px + py, _half_rows(r, c)],
                    send_sem=send.at[a, k], recv_sem=recv.at[a, k], device_id=(x, y, 1 - c), device_id_type=MESH)
                cp.start()
                copies.append(cp)
        for a in range(n):
            r = lands[a].shape[1]
            for k, (px, py) in enumerate(chips):
                pltpu.make_async_remote_copy(
                    src_ref=ins[a].at[2 * px + py, _half_rows(r, c)], dst_ref=outs[a].at[2 * px + py, _half_rows(r, 1 - c)],
                    send_sem=send.at[a, k], recv_sem=recv.at[a, k], device_id=(x, y, 1 - c), device_id_type=MESH).wait_recv()
        for cp in copies:
            cp.wait_send()

    return pl.pallas_call(body, name=name, in_specs=[ANY] * n, out_specs=[ANY] * n, out_shape=[_sds(v.shape, v.dtype) for v in lands],
                          scratch_shapes=[pltpu.SemaphoreType.DMA((n, 3)), pltpu.SemaphoreType.DMA((n, 3))],
                          input_output_aliases={i: i for i in range(n)})(*lands)


def _pair_exchange(arrs, name):
    n = len(arrs)

    def body(*refs):
        ins, outs = refs[:n], refs[n:2 * n]
        send, recv = refs[2 * n:]
        x, y, c, _ = _place()
        copies = []
        for a in range(n):
            h = arrs[a].shape[1] // 2
            cp = pltpu.make_async_remote_copy(src_ref=ins[a].at[:, pl.ds(pl.multiple_of((1 - c) * h, h), h)], dst_ref=outs[a],
                                              send_sem=send.at[a], recv_sem=recv.at[a], device_id=(x, y, 1 - c), device_id_type=MESH)
            cp.start()
            copies.append(cp)
        for cp in copies:
            cp.wait()

    return _comm_call(body, name, arrs, [_sds((a.shape[0], a.shape[1] // 2, a.shape[2]), a.dtype) for a in arrs],
                      [pltpu.SemaphoreType.DMA((n,)), pltpu.SemaphoreType.DMA((n,))])


def _chip_exchange(arrs, name):
    n = len(arrs)

    def body(*refs):
        ins, outs = refs[:n], refs[n:2 * n]
        send, recv = refs[2 * n:]
        x, y, c, chips = _place()
        me = 2 * x + y
        copies = []
        for a in range(n):
            for k, (px, py) in enumerate(chips):
                r = pltpu.make_async_remote_copy(src_ref=ins[a].at[2 * px + py], dst_ref=outs[a].at[me], send_sem=send.at[a, k],
                                                 recv_sem=recv.at[a, k], device_id=(px, py, c), device_id_type=MESH)
                r.start()
                copies.append(r)
        for cp in copies:
            cp.wait()

    return _comm_call(body, name, arrs, [_sds(a.shape, a.dtype) for a in arrs],
                      [pltpu.SemaphoreType.DMA((n, 3)), pltpu.SemaphoreType.DMA((n, 3))])


def _chip_exchange_start(arrs, name):
    n = len(arrs)

    def body(*refs):
        ins, lands = refs[:n], refs[n:2 * n]
        send, recv, token = refs[2 * n], refs[2 * n + 1], refs[-1]
        x, y, c, chips = _place()
        me = 2 * x + y
        for a in range(n):
            for k, (px, py) in enumerate(chips):
                pltpu.make_async_remote_copy(src_ref=ins[a].at[2 * px + py], dst_ref=lands[a].at[me], send_sem=send.at[3 * a + k],
                                             recv_sem=recv.at[3 * a + k], device_id=(px, py, c), device_id_type=MESH).start()
        token[...] = jnp.zeros_like(token)

    hbm = lambda v: pltpu.with_memory_space_constraint(v, pltpu.HBM)
    out = pl.pallas_call(
        body, name=name,
        out_shape=(pltpu.SemaphoreType.DMA((3 * n,)), pltpu.SemaphoreType.DMA((3 * n,)), *[pltpu.HBM(a.shape, a.dtype) for a in arrs],
                   *[pltpu.HBM(a.shape, a.dtype) for a in arrs], _sds((8, HD))),
        in_specs=[HBM] * (2 * n), out_specs=(SEM, SEM, *[HBM] * (2 * n), pl.BlockSpec(memory_space=pltpu.VMEM)),
        input_output_aliases={i: 2 + i for i in range(2 * n)},
        compiler_params=pltpu.CompilerParams(has_side_effects=DATAFLOW),
    )(*[hbm(a) for a in arrs], *[hbm(lax.empty(a.shape, a.dtype)) for a in arrs])
    return out[0], out[1], list(out[2:2 + n]), list(out[2 + n:2 + 2 * n]), out[-1]


def _chip_exchange_wait(send, recv, srcs, lands, after, name):
    n = len(srcs)

    def body(*refs):
        ins, lands_ = refs[:n], refs[n:2 * n]
        send_, recv_ = refs[2 * n], refs[2 * n + 1]
        x, y, c, chips = _place()
        for a in range(n):
            for k, (px, py) in enumerate(chips):
                cp = pltpu.make_async_remote_copy(src_ref=ins[a].at[2 * px + py], dst_ref=lands_[a].at[2 * px + py], send_sem=send_.at[3 * a + k],
                                                  recv_sem=recv_.at[3 * a + k], device_id=(px, py, c), device_id_type=MESH)
                cp.wait_send()
                cp.wait_recv()

    out = pl.pallas_call(
        body, name=name, out_shape=[pltpu.HBM(v.shape, v.dtype) for v in list(srcs) + list(lands)],
        in_specs=[HBM] * (2 * n) + [SEM, SEM, ANY], out_specs=[HBM] * (2 * n), input_output_aliases={i: i for i in range(2 * n)},
        compiler_params=pltpu.CompilerParams(has_side_effects=DATAFLOW),
    )(*srcs, *lands, send, recv, after)
    return list(out[n:])


def _pair_swap(arrs, name):
    n = len(arrs)

    def body(*refs):
        ins, outs = refs[:n], refs[n:2 * n]
        send, recv = refs[2 * n:]
        x, y, c, _ = _place()
        copies = []
        for a in range(n):
            cp = pltpu.make_async_remote_copy(src_ref=ins[a], dst_ref=outs[a], send_sem=send.at[a], recv_sem=recv.at[a],
                                              device_id=(x, y, 1 - c), device_id_type=MESH)
            cp.start()
            copies.append(cp)
        for cp in copies:
            cp.wait()

    return _comm_call(body, name, arrs, [_sds(a.shape, a.dtype) for a in arrs],
                      [pltpu.SemaphoreType.DMA((n,)), pltpu.SemaphoreType.DMA((n,))])


def _all_gather_devices(v, name):
    def body(v_ref, o_ref, send, recv, loc):
        x, y, c, _ = _place()
        me = 4 * x + 2 * y + c
        own = pltpu.make_async_copy(v_ref, o_ref.at[me], loc)
        own.start()
        copies = [own]
        for k in range(1, 8):
            fx, fy, fc = (k >> 2) & 1, (k >> 1) & 1, k & 1
            peer = (x ^ fx, y ^ fy, c ^ fc)
            r = pltpu.make_async_remote_copy(src_ref=v_ref, dst_ref=o_ref.at[me], send_sem=send.at[k - 1],
                                             recv_sem=recv.at[k - 1], device_id=peer, device_id_type=MESH)
            r.start()
            copies.append(r)
        for cp in copies:
            cp.wait()

    return _comm_call(body, name, [v], [_sds((8,) + v.shape, v.dtype)],
                      [pltpu.SemaphoreType.DMA((7,)), pltpu.SemaphoreType.DMA((7,)), pltpu.SemaphoreType.DMA])[0]


def _row_tile(r):
    return next((b for b in (512, 384, 256, 128, 64, 32, 16) if r % b == 0), r)


def _add2(a, b, out_dtype, name):
    r, w = a.shape
    br = _row_tile(r)

    def body(a_ref, b_ref, o_ref):
        o_ref[...] = (a_ref[...].astype(F32) + b_ref[...].astype(F32)).astype(out_dtype)

    blk = pl.BlockSpec((br, w), lambda i: (i, 0))
    return _pc(body, name, (r // br,), [blk, blk], blk, _sds((r, w), out_dtype))(a, b)


def _sum_slots(a, out_dtype, name, extra=None):
    n, r, w = a.shape
    br = _row_tile(r)

    def body(*refs):
        a_ref, o_ref = refs[0], refs[-1]
        acc = a_ref[0].astype(F32)
        for s in range(1, n):
            acc = acc + a_ref[s].astype(F32)
        if extra is not None:
            acc = acc + refs[1][...].astype(F32)
        o_ref[...] = acc.astype(out_dtype)

    ins = [a] + ([extra] if extra is not None else [])
    specs = [pl.BlockSpec((n, br, w), lambda i: (0, i, 0))] + ([pl.BlockSpec((br, w), lambda i: (i, 0))] if extra is not None else [])
    return _pc(body, name, (r // br,), specs, pl.BlockSpec((br, w), lambda i: (i, 0)), _sds((r, w), out_dtype))(*ins)


SMALL = ["norm_mix_g", "norm_xattn_g", "norm_mlp_g", "final_norm_g", "mem_norm_g", "hgrn_lb_logits", "mlstm_norm_g",
         "hgrn_norm_g", "c_qnorm_g", "c_knorm_g", "ab_gate_b", "c_fgate_b"]
SMALL_ROWS = 16


def _pack_small(parts):
    flat = jnp.concatenate([p.reshape(-1).astype(F32) for p in parts])
    return jnp.pad(flat, (0, SMALL_ROWS * D - flat.shape[0])).reshape(SMALL_ROWS, D)


def _unpack_small(buf, shapes):
    flat, out, off = buf.reshape(-1), [], 0
    for s in shapes:
        n = 1
        for d in s:
            n *= d
        out.append(flat[off:off + n].reshape(s))
        off += n
    return out


def kernel(x, mem, norm_mix_g, norm_xattn_g, norm_mlp_g, final_norm_g, ab_w_in, ab_conv_w, ab_gate_b, hgrn_lb_logits, mlstm_norm_g, hgrn_norm_g, ab_w_out, c_w_in, c_fgate_b, c_qnorm_g, c_knorm_g, c_w_out, mem_norm_g, mem_w_kv, xa_w_q, xa_w_o, mlp_w1, mlp_w2, loss_target, m_norm_mix_g, m_norm_xattn_g, m_norm_mlp_g, m_final_norm_g, m_ab_w_in, m_ab_conv_w, m_ab_gate_b, m_hgrn_lb_logits, m_mlstm_norm_g, m_hgrn_norm_g, m_ab_w_out, m_c_w_in, m_c_fgate_b, m_c_qnorm_g, m_c_knorm_g, m_c_w_out, m_mem_norm_g, m_mem_w_kv, m_xa_w_q, m_xa_w_o, m_mlp_w1, m_mlp_w2, v_norm_mix_g, v_norm_xattn_g, v_norm_mlp_g, v_final_norm_g, v_ab_w_in, v_ab_conv_w, v_ab_gate_b, v_hgrn_lb_logits, v_mlstm_norm_g, v_hgrn_norm_g, v_ab_w_out, v_c_w_in, v_c_fgate_b, v_c_qnorm_g, v_c_knorm_g, v_c_w_out, v_mem_norm_g, v_mem_w_kv, v_xa_w_q, v_xa_w_o, v_mlp_w1, v_mlp_w2):
    A = dict(locals())
    chip = 2 * lax.axis_index("x") + lax.axis_index("y")

    big = ["ab_w_in", "c_w_in", "ab_w_out", "c_w_out", "mem_w_kv", "xa_w_q", "xa_w_o", "mlp_w1", "mlp_w2"]
    shard2d = {"ab_w_in": (D, 1026), "c_w_in": (D, 1026), "ab_w_out": (256, D), "c_w_out": (256, D), "mem_w_kv": (D, 512),
               "xa_w_q": (512, D), "xa_w_o": (512, D), "mlp_w1": (2 * D, D), "mlp_w2": (2 * D, D)}
    shard16 = lambda n: A[n].reshape(shard2d[n]).astype(BF16)
    own_slot = lambda gs, os: [lax.dynamic_update_index_in_dim(g, o, chip, 0) for g, o in zip(gs, os)]
    cols = lambda g: jnp.concatenate([g[k] for k in range(NCHIP)], axis=1)
    per_layer = lambda g: g.reshape(NCHIP, 2, -1, D).transpose(1, 0, 2, 3)
    first = [shard16("ab_w_in"), jnp.pad(ab_conv_w[0], ((0, 16 - CONV_W), (0, 0)))]
    g_in0, g_conv = own_slot(_gather_weights(first, "gather_first"), first)
    W = dict(w_in0=_pack_w_in0(cols(g_in0)))
    rest_names = ["c_w_in", "ab_w_out", "c_w_out", "xa_w_q", "xa_w_o", "mlp_w1", "mlp_w2", "mem_w_kv"]
    rest = [shard16(n) for n in rest_names]
    send_s, recv_s, srcs, lands, token = _gather_start(rest, g_conv, "gather_rest_start")

    def late_weights(after):
        got = _pair_forward(_gather_wait(send_s, recv_s, srcs, lands, after, "gather_rest_wait"), "gather_rest_forward")
        gw = dict(zip(rest_names, own_slot(got, rest)))
        return dict(w_in1=_pack_w_in1(cols(gw["c_w_in"])), w_out0=gw["ab_w_out"].reshape(D, D), w_out1=gw["c_w_out"].reshape(D, D),
                    wkv_s=gw["mem_w_kv"],
                    wq=per_layer(gw["xa_w_q"]).reshape(2, D, D), wo=per_layer(gw["xa_w_o"]).reshape(2, D, D),
                    w1s=gw["mlp_w1"].reshape(NCHIP, 2, D, D), w2=gw["mlp_w2"].reshape(NCHIP, 2, D, D))

    S = dict(norm_mix_g=norm_mix_g + token[0, 0], norm_xattn_g=norm_xattn_g, norm_mlp_g=norm_mlp_g, final_norm_g=final_norm_g,
             conv_w=cols(g_conv[:, :CONV_W]), gate_b=ab_gate_b, lb_logits=hgrn_lb_logits, mlstm_norm_g=mlstm_norm_g,
             hgrn_norm_g=hgrn_norm_g, c_fgate_b=c_fgate_b, c_qnorm_g=c_qnorm_g, c_knorm_g=c_knorm_g, mem_norm_g=mem_norm_g)

    core = lax.axis_index("c")
    by_rows = lambda g: g.reshape(NCHIP, -1, D)

    def stack_cols(g):
        return jnp.stack([g[:, 1026 * k:1026 * (k + 1)] for k in range(NCHIP)])

    def pair_sums(arrs, tag):
        theirs = _pair_exchange(arrs, f"pair_exchange_{tag}")
        out = []
        for i, (a, th) in enumerate(zip(arrs, theirs)):
            h = a.shape[1] // 2
            mine = lax.dynamic_slice_in_dim(a, core * h, h, axis=1)
            out.append(_add2(mine.reshape(-1, a.shape[2]), th.reshape(-1, a.shape[2]), BF16, f"pair_sum_{tag}{i}").reshape(th.shape))
        return out

    def chip_sums(psums, from_chips, tag):
        out = []
        for i, (f, p) in enumerate(zip(from_chips, psums)):
            f = lax.dynamic_update_index_in_dim(f, lax.dynamic_index_in_dim(p, chip, 0, keepdims=False), chip, 0)
            out.append(_sum_slots(f, F32, f"chip_sum_{tag}{i}"))
        return out

    started = {}

    def grads_hook(stage, g):
        if stage == "layer1":
            arrs = [jnp.concatenate([by_rows(g["w_out"]), by_rows(g["wq"]), by_rows(g["wo"]), g["w1"], by_rows(g["w2"])], axis=1),
                    stack_cols(_unpack_w_in1(g["w_in"]))]
        else:
            arrs = [jnp.concatenate([by_rows(g["wq"]), by_rows(g["wo"]), g["w1"], by_rows(g["w2"])], axis=1)]
        psums = pair_sums(arrs, stage)
        *handles, token = _chip_exchange_start(psums, f"chip_exchange_start_{stage}")
        started[stage] = (psums, handles)
        return token[0, 0]

    lossp, dx, G = _local_step(x[0], mem[0], loss_target[0], W, S, late_weights, grads_hook)

    gsmall = {"norm_mix_g": G["norm_mix_g"], "norm_xattn_g": G["norm_xattn_g"], "norm_mlp_g": G["norm_mlp_g"],
              "final_norm_g": G["final_norm_g"], "mem_norm_g": G["mem_norm_g"], "hgrn_lb_logits": G["lb_logits"],
              "mlstm_norm_g": G["mlstm_norm_g"], "hgrn_norm_g": G["hgrn_norm_g"], "c_qnorm_g": G["c_qnorm_g"],
              "c_knorm_g": G["c_knorm_g"], "ab_gate_b": G["gate_b"], "c_fgate_b": G["c_fgate_b"]}
    packed = _pack_small([gsmall[n] for n in SMALL] + [G["conv_w"], lossp])
    red = _sum_slots(_all_gather_devices(packed, "gather_small"), F32, "sum_small")
    small_shapes = [A[n].shape for n in SMALL]
    *gs, gconv, loss = _unpack_small(red, small_shapes + [(CONV_W, D), ()])
    gs = dict(zip(SMALL, gs))
    gconv = lax.dynamic_slice_in_dim(gconv, chip * 256, 256, axis=1)[None]

    last = pair_sums([by_rows(G["w_out0"]), stack_cols(_unpack_w_in0(G["w_in0"])), G["wkv"]], "last")
    rhalf = chip_sums(last, _chip_exchange(last, "chip_exchange_last"), "last")
    for stage in ("layer1", "layer0_mlp_xattn"):
        psums, handles = started[stage]
        rhalf += chip_sums(psums, _chip_exchange_wait(*handles, dx, f"chip_exchange_wait_{stage}"), stage)
    other = _pair_swap(rhalf, "pair_swap")
    r_out0, r_in0, r_kv, r_l1, r_in1, r_l0 = [
        jnp.where(core == 0, jnp.concatenate([m_, o_], axis=0), jnp.concatenate([o_, m_], axis=0)) for m_, o_ in zip(rhalf, other)]
    gbig = {"ab_w_in": r_in0, "c_w_in": r_in1, "mem_w_kv": r_kv, "ab_w_out": r_out0, "c_w_out": r_l1[0:256],
            "xa_w_q": jnp.concatenate([r_l0[0:256], r_l1[256:512]], axis=0),
            "xa_w_o": jnp.concatenate([r_l0[256:512], r_l1[512:768]], axis=0),
            "mlp_w1": jnp.concatenate([r_l0[512:1536], r_l1[768:1792]], axis=0),
            "mlp_w2": jnp.concatenate([r_l0[1536:2560], r_l1[1792:2816]], axis=0)}

    out_g, out_d, out_m, out_v = {}, {}, {}, {}
    for n in big:
        d_, m_, v_ = _adam(A[n].reshape(shard2d[n]), gbig[n], A["m_" + n].reshape(shard2d[n]), A["v_" + n].reshape(shard2d[n]), "adam_" + n)
        out_g[n] = gbig[n].reshape(A[n].shape)
        out_d[n], out_m[n], out_v[n] = d_.reshape(A[n].shape), m_.reshape(A[n].shape), v_.reshape(A[n].shape)
    sd, sm, sv = _adam(_pack_small([A[n] for n in SMALL]), _pack_small([gs[n] for n in SMALL]),
                       _pack_small([A["m_" + n] for n in SMALL]), _pack_small([A["v_" + n] for n in SMALL]), "adam_small")
    for n, d_, m_, v_ in zip(SMALL, _unpack_small(sd, small_shapes), _unpack_small(sm, small_shapes), _unpack_small(sv, small_shapes)):
        out_g[n], out_d[n], out_m[n], out_v[n] = gs[n], d_, m_, v_
    cd, cm_, cv = _adam(ab_conv_w[0], gconv[0], m_ab_conv_w[0], v_ab_conv_w[0], "adam_conv")
    out_g["ab_conv_w"], out_d["ab_conv_w"], out_m["ab_conv_w"], out_v["ab_conv_w"] = gconv, cd[None], cm_[None], cv[None]

    order = ["norm_mix_g", "norm_xattn_g", "norm_mlp_g", "final_norm_g", "ab_w_in", "ab_conv_w", "ab_gate_b", "hgrn_lb_logits",
             "mlstm_norm_g", "hgrn_norm_g", "ab_w_out", "c_w_in", "c_fgate_b", "c_qnorm_g", "c_knorm_g", "c_w_out", "mem_norm_g",
             "mem_w_kv", "xa_w_q", "xa_w_o", "mlp_w1", "mlp_w2"]
    return (loss, dx[None], *[out_g[n] for n in order], *[out_d[n] for n in order], *[out_m[n] for n in order],
            *[out_v[n] for n in order])
```

```python
import functools

import jax
import jax.numpy as jnp
from jax import lax
from jax.experimental import pallas as pl
from jax.experimental.pallas import tpu as pltpu

F32 = jnp.float32
BF16 = jnp.bfloat16
EPS = 1e-6
D = 1024
CHUNK = 64
REC_CHUNKS = 4
HD = 128
XD = 256
NEG = -1e30
VMEM_LIMIT_V7X = 56 * 1024 * 1024
ADAM_LR, ADAM_B1, ADAM_B2, ADAM_EPS, ADAM_WD, ADAM_STEP = 0.001, 0.9, 0.999, 1e-08, 0.01, 10
MESH = pl.DeviceIdType.MESH


def _pc(body, name, grid, in_specs, out_specs, out_shape, scratch=(), **kw):
    return pl.pallas_call(
        body, name=name, grid=grid, in_specs=in_specs, out_specs=out_specs, out_shape=out_shape,
        scratch_shapes=scratch,
        compiler_params=pltpu.CompilerParams(
            dimension_semantics=("arbitrary",) * len(grid), vmem_limit_bytes=VMEM_LIMIT_V7X), **kw)


def _sds(shape, dtype=F32):
    return jax.ShapeDtypeStruct(shape, dtype)


def _blk(n, target):
    return max(b for b in range(128, max(target, 128) + 1, 128) if n % b == 0)


def _dot(a, b, dims):
    return lax.dot_general(a, b, (dims, ((), ())), preferred_element_type=F32)


def _nn(a, b):
    return _dot(a, b, ((1,), (0,)))


def _nt(a, b):
    return _dot(a, b, ((1,), (1,)))


def _tn(a, b):
    return _dot(a, b, ((0,), (0,)))


def _sigmoid(x):
    return 1.0 / (1.0 + jnp.exp(-x))


def _log_sigmoid(x):
    return jnp.minimum(x, 0.0) - jnp.log(1.0 + jnp.exp(-jnp.abs(x)))


def _rstd(x):
    return lax.rsqrt(jnp.mean(x * x, axis=-1, keepdims=True) + EPS)


def _rms_bwd(du, x, g):
    r = _rstd(x)
    xh = x * r
    dxh = du * g
    dx = r * (dxh - xh * jnp.mean(dxh * xh, axis=-1, keepdims=True))
    return dx, du * xh


def _norm_mm(h, g, w, name, bm=1024, bn=512):
    t, n = h.shape[0], w.shape[1]
    bm, bn = min(bm, t), _blk(n, 3 * bn)

    def body(h_ref, g_ref, w_ref, z_ref, u_ref):
        @pl.when(pl.program_id(1) == 0)
        def _():
            x = h_ref[...]
            u_ref[...] = (x * _rstd(x) * g_ref[...]).astype(BF16)
        z_ref[...] = _nn(u_ref[...], w_ref[...])

    return _pc(body, name, (t // bm, n // bn),
               [pl.BlockSpec((bm, D), lambda i, j: (i, 0)), pl.BlockSpec((1, D), lambda i, j: (0, 0)),
                pl.BlockSpec((D, bn), lambda i, j: (0, j))],
               [pl.BlockSpec((bm, bn), lambda i, j: (i, j)), pl.BlockSpec((bm, D), lambda i, j: (i, 0))],
               [_sds((t, n)), _sds((t, D), BF16)])(h, g, w)


def _mm_tn(a, b, name, bm=1024, bn=1024, bt=2048, col_chips=None):
    t, m = a.shape
    n = b.shape[1]
    bm, bn, bt = _blk(m, bm), (n // col_chips if col_chips else _blk(n, bn + bn // 2)), min(bt, t)
    nt = t // bt

    def body(a_ref, b_ref, o_ref, acc):
        k = pl.program_id(2)

        @pl.when(k == 0)
        def _():
            acc[...] = jnp.zeros_like(acc)

        acc[...] += _tn(a_ref[...].astype(BF16), b_ref[...].astype(BF16))

        @pl.when(k == nt - 1)
        def _():
            o_ref[...] = acc[...].astype(BF16)

    if col_chips:
        out_spec, out_shape = pl.BlockSpec((None, bm, bn), lambda i, j, k: (j, i, 0)), _sds((col_chips, m, bn), BF16)
    else:
        out_spec, out_shape = pl.BlockSpec((bm, bn), lambda i, j, k: (i, j)), _sds((m, n), BF16)
    return _pc(body, name, (m // bm, n // bn, nt),
               [pl.BlockSpec((bt, bm), lambda i, j, k: (k, i)), pl.BlockSpec((bt, bn), lambda i, j, k: (k, j))],
               out_spec, out_shape, scratch=[pltpu.VMEM((bm, bn), F32)])(a, b)


def _bwd_in(dz, w, h, g, dh, name, bm=1024, bk=1024):
    t, n = dz.shape
    bm, bk = min(bm, t), _blk(n, bk + bk // 2)
    nk = n // bk

    def body(dz_ref, w_ref, h_ref, g_ref, dh_ref, o_ref, dg_ref, acc):
        i, k = pl.program_id(0), pl.program_id(1)

        @pl.when(k == 0)
        def _():
            acc[...] = jnp.zeros_like(acc)

        @pl.when((i == 0) & (k == 0))
        def _():
            dg_ref[...] = jnp.zeros_like(dg_ref)

        acc[...] += _nt(dz_ref[...], w_ref[...])

        @pl.when(k == nk - 1)
        def _():
            dx, dgr = _rms_bwd(acc[...], h_ref[...], g_ref[...])
            o_ref[...] = dh_ref[...] + dx
            dg_ref[...] += jnp.sum(dgr, axis=0, keepdims=True)

    return _pc(body, name, (t // bm, nk),
               [pl.BlockSpec((bm, bk), lambda i, k: (i, k)), pl.BlockSpec((D, bk), lambda i, k: (0, k)),
                pl.BlockSpec((bm, D), lambda i, k: (i, 0)), pl.BlockSpec((1, D), lambda i, k: (0, 0)),
                pl.BlockSpec((bm, D), lambda i, k: (i, 0))],
               [pl.BlockSpec((bm, D), lambda i, k: (i, 0)), pl.BlockSpec((1, D), lambda i, k: (0, 0))],
               [_sds((t, D)), _sds((1, D))], scratch=[pltpu.VMEM((bm, D), F32)])(dz, w, h, g, dh)


def _mlp_fwd(h, g, w1s, w2, l, name, bm=1024):
    t = h.shape[0]
    bm = min(bm, t)
    nk = w1s.shape[0]

    def body(h_ref, g_ref, w1_ref, w2_ref, o_ref, a_ref, u_ref, acc):
        k = pl.program_id(1)

        @pl.when(k == 0)
        def _():
            x = h_ref[...]
            u_ref[...] = (x * _rstd(x) * g_ref[...]).astype(BF16)
            acc[...] = jnp.zeros_like(acc)

        a = _nn(u_ref[...], w1_ref[...])
        a_ref[...] = a
        r = jnp.square(jnp.maximum(a, 0.0)).astype(BF16)
        acc[...] += _nn(r, w2_ref[...])

        @pl.when(k == nk - 1)
        def _():
            o_ref[...] = h_ref[...] + acc[...]

    return _pc(body, name, (t // bm, nk),
               [pl.BlockSpec((bm, D), lambda i, k: (i, 0)), pl.BlockSpec((1, D), lambda i, k: (0, 0)),
                pl.BlockSpec((None, None, D, D), lambda i, k: (k, l, 0, 0)), pl.BlockSpec((None, None, D, D), lambda i, k: (k, l, 0, 0))],
               [pl.BlockSpec((bm, D), lambda i, k: (i, 0)), pl.BlockSpec((bm, D), lambda i, k: (i, k)),
                pl.BlockSpec((bm, D), lambda i, k: (i, 0))],
               [_sds((t, D)), _sds((t, nk * D)), _sds((t, D), BF16)],
               scratch=[pltpu.VMEM((bm, D), F32)])(h, g, w1s, w2)


def _mlp_bwd(dh, a, w1s, w2, l, h, g, name, bm=512):
    t = h.shape[0]
    bm = min(bm, t)
    nk = w1s.shape[0]

    def body(dh_ref, a_ref, w1_ref, w2_ref, h_ref, g_ref, o_ref, da_ref, r_ref, dg_ref, acc):
        i, k = pl.program_id(0), pl.program_id(1)

        @pl.when(k == 0)
        def _():
            acc[...] = jnp.zeros_like(acc)

        @pl.when((i == 0) & (k == 0))
        def _():
            dg_ref[...] = jnp.zeros_like(dg_ref)

        ap = jnp.maximum(a_ref[...], 0.0)
        r_ref[...] = jnp.square(ap).astype(BF16)
        dr = _nt(dh_ref[...].astype(BF16), w2_ref[...])
        da = (dr * (2.0 * ap)).astype(BF16)
        da_ref[...] = da
        acc[...] += _nt(da, w1_ref[...])

        @pl.when(k == nk - 1)
        def _():
            dx, dgr = _rms_bwd(acc[...], h_ref[...], g_ref[...])
            o_ref[...] = dh_ref[...] + dx
            dg_ref[...] += jnp.sum(dgr, axis=0, keepdims=True)

    return _pc(body, name, (t // bm, nk),
               [pl.BlockSpec((bm, D), lambda i, k: (i, 0)), pl.BlockSpec((bm, D), lambda i, k: (i, k)),
                pl.BlockSpec((None, None, D, D), lambda i, k: (k, l, 0, 0)), pl.BlockSpec((None, None, D, D), lambda i, k: (k, l, 0, 0)),
                pl.BlockSpec((bm, D), lambda i, k: (i, 0)), pl.BlockSpec((1, D), lambda i, k: (0, 0))],
               [pl.BlockSpec((bm, D), lambda i, k: (i, 0)), pl.BlockSpec((bm, D), lambda i, k: (i, k)),
                pl.BlockSpec((bm, D), lambda i, k: (i, k)), pl.BlockSpec((1, D), lambda i, k: (0, 0))],
               [_sds((t, D)), _sds((t, nk * D), BF16), _sds((t, nk * D), BF16), _sds((1, D))],
               scratch=[pltpu.VMEM((bm, D), F32)])(dh, a, w1s, w2, h, g)


def _rows_of(x):
    return lax.broadcasted_iota(jnp.int32, x.shape, 0)


def _shift_down(x, s):
    if s == 0:
        return x
    return jnp.where(_rows_of(x) >= s, pltpu.roll(x, s, 0), 0.0)


def _shift_up(x, s):
    if s == 0:
        return x
    n = x.shape[0]
    return jnp.where(_rows_of(x) < n - s, pltpu.roll(x, n - s, 0), 0.0)


def _cumsum_rows(x):
    n, s = x.shape[0], 1
    while s < n:
        x = x + _shift_down(x, s)
        s *= 2
    return x


def _rcumsum_rows(x):
    n, s = x.shape[0], 1
    while s < n:
        x = x + _shift_up(x, s)
        s *= 2
    return x


def _silu(x):
    return x * _sigmoid(x)


def _dsilu(x):
    s = _sigmoid(x)
    return s * (1.0 + x * (1.0 - s))


CONV_W = 4


def _conv_pre(u, w):
    y = _shift_down(u, CONV_W - 1) * w[0:1, :]
    for j in range(1, CONV_W):
        y = y + _shift_down(u, CONV_W - 1 - j) * w[j:j + 1, :]
    return y


def _conv_fwd(z0, cw, name):
    t = z0.shape[0]

    def body(u_ref, w_ref, o_ref):
        o_ref[...] = _silu(_conv_pre(u_ref[...], w_ref[...]))

    return _pc(body, name, (2 * 512 // HD,),
               [pl.BlockSpec((t, HD), lambda c: (0, c)), pl.BlockSpec((CONV_W, HD), lambda c: (0, c))],
               pl.BlockSpec((t, HD), lambda c: (0, c)), _sds((t, 1024)))(z0, cw)


def _conv_bwd(z0, cw, dy, name):
    t = z0.shape[0]

    def body(u_ref, w_ref, dy_ref, du_ref, dw_ref):
        u, w = u_ref[...], w_ref[...]
        dpre = dy_ref[...] * _dsilu(_conv_pre(u, w))
        du = _shift_up(dpre, CONV_W - 1) * w[0:1, :]
        for j in range(1, CONV_W):
            du = du + _shift_up(dpre, CONV_W - 1 - j) * w[j:j + 1, :]
        du_ref[...] = du.astype(BF16)
        for j in range(CONV_W):
            dw_ref[j:j + 1, :] = jnp.sum(dpre * _shift_down(u, CONV_W - 1 - j), axis=0, keepdims=True)

    return _pc(body, name, (2 * 512 // HD,),
               [pl.BlockSpec((t, HD), lambda c: (0, c)), pl.BlockSpec((CONV_W, HD), lambda c: (0, c)),
                pl.BlockSpec((t, HD), lambda c: (0, c))],
               [pl.BlockSpec((t, HD), lambda c: (0, c)), pl.BlockSpec((CONV_W, HD), lambda c: (0, c))],
               [_sds((t, 1024), BF16), _sds((CONV_W, 1024))])(z0, cw, dy)


def _mlstm_gates(gate, bias, m_in):
    L = gate.shape[0]
    r = lax.broadcasted_iota(jnp.int32, (L, L), 0)
    c = lax.broadcasted_iota(jnp.int32, (L, L), 1)
    eye, tril = r == c, c <= r
    i_col = gate[:, 0:1] + bias[:, 0:1]
    f_col = gate[:, 1:2] + bias[:, 1:2]
    logf_col = _log_sigmoid(f_col)
    logf_row = jnp.sum(jnp.where(eye, logf_col, 0.0), axis=0, keepdims=True)
    i_row = jnp.sum(jnp.where(eye, i_col, 0.0), axis=0, keepdims=True)
    b_col = jnp.sum(jnp.where(tril, logf_row, 0.0), axis=1, keepdims=True)
    b_row = jnp.sum(jnp.where(r <= c, logf_col, 0.0), axis=0, keepdims=True)
    logd = jnp.where(tril, b_col - b_row + i_row, NEG)
    inter = b_col + m_in
    m_t = jnp.maximum(inter, jnp.max(logd, axis=1, keepdims=True))
    w_t = jnp.exp(inter - m_t)
    dm = jnp.exp(logd - m_t)
    b_last = b_col[L - 1:L, :]
    log_in = b_last - b_col + i_col
    m_new = jnp.maximum(b_last + m_in, jnp.max(log_in, axis=0, keepdims=True))
    w_col = jnp.exp(log_in - m_new)
    decay = jnp.exp(b_last + m_in - m_new)
    return dict(eye=eye, r=r, c=c, f_col=f_col, m_t=m_t, w_t=w_t, dm=dm, m_new=m_new, w_col=w_col, decay=decay)


def _mlstm_fwd(qk, z0, gates, bias, name):
    t = qk.shape[0]
    nc, nh, L = t // CHUNK, 4, CHUNK
    scale = HD ** -0.5

    def body(q_ref, k_ref, v_ref, g_ref, b_ref, h_ref, cs_ref, ns_ref, ms_ref, c_s, n_s, m_s):
        @pl.when(pl.program_id(0) == 0)
        def _():
            c_s[...] = jnp.zeros_like(c_s)
            n_s[...] = jnp.zeros_like(n_s)
            m_s[...] = jnp.zeros_like(m_s)

        for hd in range(nh):
            sl = slice(hd * HD, (hd + 1) * HD)
            cm, nv, m_in = c_s[hd], n_s[hd], m_s[hd]
            for ck in range(cps):
                rows = slice(ck * L, (ck + 1) * L)
                cs_ref[hd, ck] = cm
                ns_ref[hd, ck] = nv
                ms_ref[hd, ck] = jnp.broadcast_to(m_in, (1, HD))
                q, kh, v = q_ref[rows, sl], k_ref[rows, sl] * scale, v_ref[rows, sl]
                G = _mlstm_gates(g_ref[hd, rows, :], b_ref[hd], m_in)
                qb, kb, vb = q.astype(BF16), kh.astype(BF16), v.astype(BF16)
                sc = _nt(qb, kb) * G["dm"]
                num = _nn(sc.astype(BF16), vb) + G["w_t"] * _nn(qb, cm.astype(BF16))
                den = jnp.sum(sc, axis=1, keepdims=True) + G["w_t"] * jnp.sum(q * nv, axis=1, keepdims=True)
                h_ref[rows, sl] = num / jnp.maximum(jnp.abs(den), jnp.exp(-G["m_t"]))
                wk = G["w_col"] * kh
                cm = G["decay"] * cm + _tn(wk.astype(BF16), vb)
                nv = G["decay"] * nv + jnp.sum(wk, axis=0, keepdims=True)
                m_in = G["m_new"]
            c_s[hd], n_s[hd], m_s[hd] = cm, nv, m_in

    cps = REC_CHUNKS
    hspec = lambda blk: pl.BlockSpec((cps * L, 512), lambda j: (j, blk))
    st = lambda r: pl.BlockSpec((nh, cps, r, HD), lambda j: (0, j, 0, 0))
    return _pc(body, name, (nc // cps,),
               [hspec(0), hspec(1), hspec(2), pl.BlockSpec((nh, cps * L, 2), lambda j: (0, j, 0)),
                pl.BlockSpec((nh, 1, 2), lambda j: (0, 0, 0))],
               [hspec(0), st(HD), st(1), st(1)],
               [_sds((t, 512)), _sds((nh, nc, HD, HD)), _sds((nh, nc, 1, HD)), _sds((nh, nc, 1, HD))],
               scratch=[pltpu.VMEM((nh, HD, HD), F32), pltpu.VMEM((nh, 1, HD), F32), pltpu.VMEM((nh, 1, 1), F32)])(qk, qk, z0, gates, bias)


def _mlstm_bwd(qk, z0, gates, bias, cs, ns, ms, dh, name):
    t = qk.shape[0]
    nc, nh, L = t // CHUNK, 4, CHUNK
    scale = HD ** -0.5

    def body(q_ref, k_ref, v_ref, g_ref, b_ref, cs_ref, ns_ref, ms_ref, dh_ref, dq_ref, dk_ref, dv_ref, dg_ref, dc_s, dn_s):
        @pl.when(pl.program_id(0) == 0)
        def _():
            dc_s[...] = jnp.zeros_like(dc_s)
            dn_s[...] = jnp.zeros_like(dn_s)

        for ck in reversed(range(cps)):
            for hd in range(nh):
                one_head(hd, ck, slice(hd * HD, (hd + 1) * HD), slice(ck * L, (ck + 1) * L), q_ref, k_ref, v_ref, g_ref, b_ref,
                         cs_ref, ns_ref, ms_ref, dh_ref, dq_ref, dk_ref, dv_ref, dg_ref, dc_s, dn_s)

    def one_head(hd, ck, sl, rows, q_ref, k_ref, v_ref, g_ref, b_ref, cs_ref, ns_ref, ms_ref, dh_ref, dq_ref, dk_ref, dv_ref, dg_ref,
                 dc_s, dn_s):
        cm, nv, m_in = cs_ref[hd, ck], ns_ref[hd, ck], ms_ref[hd, ck][:, 0:1]
        q, kh, v = q_ref[rows, sl], k_ref[rows, sl] * scale, v_ref[rows, sl]
        G = _mlstm_gates(g_ref[hd, rows, :], b_ref[hd], m_in)
        w_t, dmat, w_col, decay = G["w_t"], G["dm"], G["w_col"], G["decay"]
        qb, kb, vb, cb = q.astype(BF16), kh.astype(BF16), v.astype(BF16), cm.astype(BF16)
        s = _nt(qb, kb)
        sc = s * dmat
        scb = sc.astype(BF16)
        qc = _nn(qb, cb)
        qn = jnp.sum(q * nv, axis=1, keepdims=True)
        num = _nn(scb, vb) + w_t * qc
        den = jnp.sum(sc, axis=1, keepdims=True) + w_t * qn
        e_m = jnp.exp(-G["m_t"])
        dnm = jnp.maximum(jnp.abs(den), e_m)
        dh_ = dh_ref[rows, sl]
        dnum = dh_ / dnm
        dden = jnp.where(jnp.abs(den) > e_m, -jnp.sum(dh_ * num, axis=1, keepdims=True) / (dnm * dnm) * jnp.sign(den), 0.0)
        dnumb = dnum.astype(BF16)
        dsc = _nt(dnumb, vb) + dden
        dv = _tn(scb, dnumb)
        wd = w_t * dnum
        wdb = wd.astype(BF16)
        ds = dsc * dmat
        dsb = ds.astype(BF16)
        dq = _nt(wdb, cb) + (w_t * dden) * nv + _nn(dsb, kb)
        dc_o = _tn(qb, wdb)
        dn_o = jnp.sum(q * (w_t * dden), axis=0, keepdims=True)
        dw = jnp.sum(dnum * qc, axis=1, keepdims=True) + dden * qn
        dkh = _tn(dsb, qb)
        dlogd = ds * s
        db_col = jnp.sum(dlogd, axis=1, keepdims=True) + dw * w_t
        csum = jnp.sum(dlogd, axis=0, keepdims=True)
        dcn, dnn = dc_s[hd], dn_s[hd]
        dcnb = dcn.astype(BF16)
        kdc = _nn(kb, dcnb)
        dws = jnp.sum(kdc * v, axis=1, keepdims=True) + jnp.sum(kh * dnn, axis=1, keepdims=True)
        dv = dv + w_col * kdc
        dkh = dkh + w_col * (_nt(vb, dcnb) + dnn)
        dlin = dws * w_col
        ddecay = jnp.sum(jnp.sum(dcn * cm, axis=1, keepdims=True), axis=0, keepdims=True) + jnp.sum(dnn * nv, axis=1, keepdims=True)
        dlast = ddecay * decay + jnp.sum(dlin, axis=0, keepdims=True)
        row_id = lax.broadcasted_iota(jnp.int32, (L, 1), 0)
        db_col = db_col - dlin + jnp.where(row_id == L - 1, dlast, 0.0)
        eye, r, c = G["eye"], G["r"], G["c"]
        di = dlin + jnp.sum(jnp.where(eye, csum, 0.0), axis=1, keepdims=True)
        db_row = jnp.sum(jnp.where(eye, db_col, 0.0), axis=0, keepdims=True) - csum
        dlogf = jnp.sum(jnp.where(c >= r, db_row, 0.0), axis=1, keepdims=True)
        dg_ref[hd, rows, 0:1] = di
        dg_ref[hd, rows, 1:2] = dlogf * (1.0 - _sigmoid(G["f_col"]))
        dq_ref[rows, sl] = dq
        dk_ref[rows, sl] = dkh * scale
        dv_ref[rows, sl] = dv
        dc_s[hd] = decay * dcn + dc_o
        dn_s[hd] = decay * dnn + dn_o

    cps = REC_CHUNKS
    rv = lambda j: nc // cps - 1 - j
    hspec = lambda blk: pl.BlockSpec((cps * L, 512), lambda j: (rv(j), blk))
    st = lambda r: pl.BlockSpec((nh, cps, r, HD), lambda j: (0, rv(j), 0, 0))
    gs = pl.BlockSpec((nh, cps * L, 2), lambda j: (0, rv(j), 0))
    return _pc(body, name, (nc // cps,),
               [hspec(0), hspec(1), hspec(2), gs, pl.BlockSpec((nh, 1, 2), lambda j: (0, 0, 0)),
                st(HD), st(1), st(1), hspec(0)],
               [hspec(0), hspec(0), hspec(0), gs],
               [_sds((t, 512)), _sds((t, 512)), _sds((t, 512)), _sds((nh, t, 2))],
               scratch=[pltpu.VMEM((nh, HD, HD), F32), pltpu.VMEM((nh, 1, HD), F32)])(qk, qk, z0, gates, bias, cs, ns, ms, dh)


def _hgrn_act(qb_, fb_, ib_, lg):
    lb = _sigmoid(lg[0:1, :] - lg[1:2, :])
    sg = _sigmoid(fb_)
    f = lb + (1.0 - lb) * sg
    return lb, sg, f, _silu(qb_), (1.0 - lb) * (1.0 - sg), _silu(ib_), _cumsum_rows(jnp.log(f))


HG_SUB = 16


def _hgrn_offdiag(q, k, b, r0):
    beta = b[r0 - 1:r0, :]
    e1 = jnp.exp(b[r0:r0 + HG_SUB, :] - beta)
    e2 = jnp.where(_rows_of(b) < r0, jnp.exp(jnp.minimum(beta - b, 0.0)), 0.0)
    return q[r0:r0 + HG_SUB, :] * e1, k * e2, e1, e2


def _hgrn_fwd(z0, lbl, name):
    t = z0.shape[0]
    nc, nh, L = t // CHUNK, 4, CHUNK

    def body(q_ref, f_ref, i_ref, l_ref, o_ref, ss_ref, st_s):
        @pl.when(pl.program_id(0) == 0)
        def _():
            st_s[...] = jnp.zeros_like(st_s)

        for hd in range(nh):
            sl = slice(hd * HD, (hd + 1) * HD)
            st = st_s[hd]
            for ck in range(cps):
                rows = slice(ck * L, (ck + 1) * L)
                ss_ref[hd, ck] = st
                _, _, _, q, k, v, b = _hgrn_act(q_ref[rows, sl], f_ref[rows, sl], i_ref[rows, sl], l_ref[:, sl])
                o = _nt((q * jnp.exp(b)).astype(BF16), st.astype(BF16))
                sub = _rows_of(b) & (HG_SUB - 1)
                o = o + jnp.sum(q * k, axis=1, keepdims=True) * v
                for dl in range(1, HG_SUB):
                    e = jnp.exp(jnp.where(sub >= dl, b - pltpu.roll(b, dl, 0), NEG))
                    a = jnp.sum(q * pltpu.roll(k, dl, 0) * e, axis=1, keepdims=True)
                    o = o + a * pltpu.roll(v, dl, 0)
                o_ref[rows, sl] = o
                vb = v.astype(BF16)
                for i in range(1, L // HG_SUB):
                    r0 = i * HG_SUB
                    qt, kt, _, _ = _hgrn_offdiag(q, k, b, r0)
                    a = _nt(qt.astype(BF16), kt.astype(BF16))
                    o_ref[ck * L + r0:ck * L + r0 + HG_SUB, sl] += _nn(a.astype(BF16), vb)
                bl = b[L - 1:L, :]
                st = st * jnp.exp(bl) + _tn(v.astype(BF16), (k * jnp.exp(bl - b)).astype(BF16))
            st_s[hd] = st

    cps = REC_CHUNKS
    hspec = lambda blk: pl.BlockSpec((cps * L, 512), lambda j: (j, blk))
    return _pc(body, name, (nc // cps,),
               [hspec(4), hspec(5), hspec(6), pl.BlockSpec((2, 512), lambda j: (0, 0))],
               [hspec(0), pl.BlockSpec((nh, cps, HD, HD), lambda j: (0, j, 0, 0))],
               [_sds((t, 512)), _sds((nh, nc, HD, HD))],
               scratch=[pltpu.VMEM((nh, HD, HD), F32)])(z0, z0, z0, lbl)


def _hgrn_bwd(z0, lbl, ss, do, name):
    t = z0.shape[0]
    nc, nh, L = t // CHUNK, 4, CHUNK

    def body(q_ref, f_ref, i_ref, l_ref, ss_ref, do_ref, dq_ref, df_ref, di_ref, dl_ref, dst_s, dlb_s, dq_a, dk_a, dv_a, db_a):
        @pl.when(pl.program_id(0) == 0)
        def _():
            dst_s[...] = jnp.zeros_like(dst_s)
            dlb_s[...] = jnp.zeros_like(dlb_s)

        for ck in reversed(range(cps)):
            for hd in range(nh):
                one_head(hd, ck, slice(hd * HD, (hd + 1) * HD), slice(ck * L, (ck + 1) * L), q_ref, f_ref, i_ref, l_ref, ss_ref, do_ref,
                         dq_ref, df_ref, di_ref, dl_ref, dst_s, dlb_s, dq_a.at[hd], dk_a.at[hd], dv_a.at[hd], db_a.at[hd])

    def one_head(hd, ck, sl, rs, q_ref, f_ref, i_ref, l_ref, ss_ref, do_ref, dq_ref, df_ref, di_ref, dl_ref, dst_s, dlb_s,
                 dq_a, dk_a, dv_a, db_a):
        st = ss_ref[hd, ck]
        qp, fp, ip = q_ref[rs, sl], f_ref[rs, sl], i_ref[rs, sl]
        lb, sg, f, q, k, v, b = _hgrn_act(qp, fp, ip, l_ref[:, sl])
        do_ = do_ref[rs, sl]
        dob, stb = do_.astype(BF16), st.astype(BF16)
        eb = jnp.exp(b)
        qe = q * eb
        dqe = _nn(dob, stb)
        dst_o = _tn(dob, qe.astype(BF16))
        dq = dqe * eb
        db = dqe * qe
        rows = _rows_of(b)
        sub = rows & (HG_SUB - 1)
        p0 = jnp.sum(do_ * v, axis=1, keepdims=True)
        dq = dq + p0 * k
        dk = p0 * q
        dv = jnp.sum(q * k, axis=1, keepdims=True) * do_
        for dl in range(1, HG_SUB):
            up = L - dl
            kd, vd = pltpu.roll(k, dl, 0), pltpu.roll(v, dl, 0)
            e = jnp.exp(jnp.where(sub >= dl, b - pltpu.roll(b, dl, 0), NEG))
            a = jnp.sum(q * kd * e, axis=1, keepdims=True)
            p = jnp.sum(do_ * vd, axis=1, keepdims=True) * e
            dq = dq + p * kd
            dkd = p * q
            dbb = dkd * kd
            dv = dv + pltpu.roll(a * do_, up, 0)
            dk = dk + pltpu.roll(dkd, up, 0)
            db = db + dbb - pltpu.roll(dbb, up, 0)
        dq_a[...], dk_a[...], dv_a[...], db_a[...] = dq, dk, dv, db
        vb = v.astype(BF16)
        for i in range(1, L // HG_SUB):
            r0 = i * HG_SUB
            blk = slice(r0, r0 + HG_SUB)
            qt, kt, e1, e2 = _hgrn_offdiag(q, k, b, r0)
            qtb, ktb, dob_i = qt.astype(BF16), kt.astype(BF16), do_[blk, :].astype(BF16)
            a = _nt(qtb, ktb).astype(BF16)
            da = _nt(dob_i, vb).astype(BF16)
            dv_a[...] += _tn(a, dob_i)
            dqt = _nn(da, ktb)
            dkt = _tn(da, qtb)
            dq_a[blk, :] += dqt * e1
            t1, t2 = dqt * qt, dkt * kt
            db_a[blk, :] += t1
            dk_a[...] += dkt * e2
            db_a[...] -= t2
            db_a[r0 - 1:r0, :] += jnp.sum(t2, axis=0, keepdims=True) - jnp.sum(t1, axis=0, keepdims=True)
        dq, dk, dv, db = dq_a[...], dk_a[...], dv_a[...], db_a[...]
        dstn = dst_s[hd]
        dstnb = dstn.astype(BF16)
        bl = b[L - 1:L, :]
        ebl = jnp.exp(bl)
        kdec_e = jnp.exp(bl - b)
        kdec = k * kdec_e
        dbl = jnp.sum(dstn * st, axis=0, keepdims=True) * ebl
        dv = dv + _nt(kdec.astype(BF16), dstnb)
        dkdec = _nn(v.astype(BF16), dstnb)
        dk = dk + dkdec * kdec_e
        dx = dkdec * kdec
        dbl = dbl + jnp.sum(dx, axis=0, keepdims=True)
        db = db - dx + jnp.where(rows == L - 1, dbl, 0.0)
        dst_s[hd] = dstn * ebl + dst_o
        dg = _rcumsum_rows(db)
        dfk = dg / f - dk
        dq_ref[rs, sl] = (dq * _dsilu(qp)).astype(BF16)
        di_ref[rs, sl] = (dv * _dsilu(ip)).astype(BF16)
        df_ref[rs, sl] = (dfk * (1.0 - lb) * sg * (1.0 - sg)).astype(BF16)
        dlb_s[hd] += jnp.sum(dfk * (1.0 - sg), axis=0, keepdims=True)

        if ck == 0:
            @pl.when(pl.program_id(0) == nc // cps - 1)
            def _():
                dl0 = dlb_s[hd] * lb * (1.0 - lb)
                dl_ref[0:1, sl] = dl0
                dl_ref[1:2, sl] = -dl0

    cps = REC_CHUNKS
    rv = lambda j: nc // cps - 1 - j
    hspec = lambda blk: pl.BlockSpec((cps * L, 512), lambda j: (rv(j), blk))
    return _pc(body, name, (nc // cps,),
               [hspec(4), hspec(5), hspec(6), pl.BlockSpec((2, 512), lambda j: (0, 0)),
                pl.BlockSpec((nh, cps, HD, HD), lambda j: (0, rv(j), 0, 0)), hspec(0)],
               [hspec(0), hspec(0), hspec(0), pl.BlockSpec((2, 512), lambda j: (0, 0))],
               [_sds((t, 512), BF16), _sds((t, 512), BF16), _sds((t, 512), BF16), _sds((2, 512))],
               scratch=[pltpu.VMEM((nh, HD, HD), F32), pltpu.VMEM((nh, 1, HD), F32)] + [pltpu.VMEM((nh, L, HD), F32)] * 4)(z0, z0, z0, lbl, ss, do)


def _post0_fwd(hm, hh, z0, na, nb, w, h0, name, bm=512):
    t = h0.shape[0]
    bm = min(bm, t)

    def body(hm_ref, hh_ref, oa_ref, gb_ref, na_ref, nb_ref, w_ref, h_ref, o_ref, y_ref):
        for hd in range(4):
            sl = slice(hd * HD, (hd + 1) * HD)
            pa = _sigmoid(oa_ref[:, sl]) * hm_ref[:, sl]
            y_ref[:, sl] = (pa * _rstd(pa) * na_ref[:, sl]).astype(BF16)
            xb = hh_ref[:, sl]
            y_ref[:, 512 + hd * HD:512 + (hd + 1) * HD] = (xb * _rstd(xb) * nb_ref[:, sl] * _silu(gb_ref[:, sl])).astype(BF16)
        o_ref[...] = h_ref[...] + _nn(y_ref[...], w_ref[...])

    row = lambda wd, c: pl.BlockSpec((bm, wd), lambda i: (i, c))
    vec = lambda wd: pl.BlockSpec((1, wd), lambda i: (0, 0))
    return _pc(body, name, (t // bm,),
               [row(512, 0), row(512, 0), row(512, 3), row(512, 7), vec(512), vec(512),
                pl.BlockSpec((D, D), lambda i: (0, 0)), row(D, 0)],
               [row(D, 0), row(D, 0)], [_sds((t, D)), _sds((t, D), BF16)])(hm, hh, z0, z0, na, nb, w, h0)


def _post0_bwd(dh1, w, hm, hh, z0, na, nb, name, bm=512):
    t = dh1.shape[0]
    bm = min(bm, t)

    def body(dh_ref, w_ref, hm_ref, hh_ref, oa_ref, gb_ref, na_ref, nb_ref, dhm_ref, dhh_ref, doa_ref, dgb_ref, dna_ref, dnb_ref):
        @pl.when(pl.program_id(0) == 0)
        def _():
            dna_ref[...] = jnp.zeros_like(dna_ref)
            dnb_ref[...] = jnp.zeros_like(dnb_ref)

        dy = _nt(dh_ref[...].astype(BF16), w_ref[...])
        for hd in range(4):
            sl = slice(hd * HD, (hd + 1) * HD)
            hm_, oa = hm_ref[:, sl], oa_ref[:, sl]
            sg = _sigmoid(oa)
            dpa, dgr = _rms_bwd(dy[:, sl], sg * hm_, na_ref[:, sl])
            dna_ref[:, sl] += jnp.sum(dgr, axis=0, keepdims=True)
            doa_ref[:, sl] = (dpa * hm_ * sg * (1.0 - sg)).astype(BF16)
            dhm_ref[:, sl] = dpa * sg
            xb, gb, nbv = hh_ref[:, sl], gb_ref[:, sl], nb_ref[:, sl]
            dyb = dy[:, 512 + hd * HD:512 + (hd + 1) * HD]
            dgb_ref[:, sl] = (dyb * (xb * _rstd(xb) * nbv) * _dsilu(gb)).astype(BF16)
            dxb, dgr2 = _rms_bwd(dyb * _silu(gb), xb, nbv)
            dnb_ref[:, sl] += jnp.sum(dgr2, axis=0, keepdims=True)
            dhh_ref[:, sl] = dxb

    row = lambda wd, c: pl.BlockSpec((bm, wd), lambda i: (i, c))
    vec = lambda wd: pl.BlockSpec((1, wd), lambda i: (0, 0))
    return _pc(body, name, (t // bm,),
               [row(D, 0), pl.BlockSpec((D, D), lambda i: (0, 0)), row(512, 0), row(512, 0), row(512, 3), row(512, 7),
                vec(512), vec(512)],
               [row(512, 0), row(512, 0), row(512, 0), row(512, 0), vec(512), vec(512)],
               [_sds((t, 512)), _sds((t, 512)), _sds((t, 512), BF16), _sds((t, 512), BF16), _sds((1, 512)), _sds((1, 512))],
               )(dh1, w, hm, hh, z0, z0, na, nb)


def _memkv_fwd(mem, g, wkv_s, name):
    m = mem.shape[0]

    def body(x_ref, g_ref, w_ref, kv_ref, mn_ref):
        x = x_ref[...]
        mn = (x * _rstd(x) * g_ref[...]).astype(BF16)
        mn_ref[...] = mn
        kv_ref[...] = _nn(mn, w_ref[...])

    return _pc(body, name, (4,),
               [pl.BlockSpec((m, D), lambda k: (0, 0)), pl.BlockSpec((1, D), lambda k: (0, 0)),
                pl.BlockSpec((None, D, 512), lambda k: (k, 0, 0))],
               [pl.BlockSpec((m, 512), lambda k: (0, k)), pl.BlockSpec((m, D), lambda k: (0, 0))],
               [_sds((m, 2048)), _sds((m, D), BF16)])(mem, g, wkv_s)


def _memkv_bwd(dkv, wkv_s, mem, g, name):
    m = mem.shape[0]

    def body(d_ref, w_ref, x_ref, g_ref, dg_ref, acc):
        k = pl.program_id(0)

        @pl.when(k == 0)
        def _():
            acc[...] = jnp.zeros_like(acc)

        acc[...] += _nt(d_ref[...].astype(BF16), w_ref[...])

        @pl.when(k == 3)
        def _():
            _, dgr = _rms_bwd(acc[...], x_ref[...], g_ref[...])
            dg_ref[...] = jnp.sum(dgr, axis=0, keepdims=True)

    return _pc(body, name, (4,),
               [pl.BlockSpec((m, 512), lambda k: (0, k)), pl.BlockSpec((None, D, 512), lambda k: (k, 0, 0)),
                pl.BlockSpec((m, D), lambda k: (0, 0)), pl.BlockSpec((1, D), lambda k: (0, 0))],
               pl.BlockSpec((1, D), lambda k: (0, 0)), _sds((1, D)), scratch=[pltpu.VMEM((m, D), F32)])(dkv, wkv_s, mem, g)


def _xattn_probs(qh, kh):
    s = _nt(qh, kh) * (XD ** -0.5)
    p = jnp.exp(s - jnp.max(s, axis=1, keepdims=True))
    return p / jnp.sum(p, axis=1, keepdims=True)


def _xattn_fwd(q, kv, wo, h1, name, bm=512):
    t, m = q.shape[0], kv.shape[0]
    bm = min(bm, t)

    def body(q_ref, k_ref, v_ref, w_ref, h_ref, out_ref, o_ref):
        for hd in range(D // XD):
            sl = slice(hd * XD, (hd + 1) * XD)
            p = _xattn_probs(q_ref[:, sl].astype(BF16), k_ref[:, sl].astype(BF16))
            o_ref[:, sl] = _nn(p.astype(BF16), v_ref[:, sl].astype(BF16)).astype(BF16)
        out_ref[...] = h_ref[...] + _nn(o_ref[...], w_ref[...])

    row = pl.BlockSpec((bm, D), lambda i: (i, 0))
    return _pc(body, name, (t // bm,),
               [row, pl.BlockSpec((m, D), lambda i: (0, 0)), pl.BlockSpec((m, D), lambda i: (0, 1)),
                pl.BlockSpec((D, D), lambda i: (0, 0)), row],
               [row, row], [_sds((t, D)), _sds((t, D), BF16)])(q, kv, kv, wo, h1)


def _xattn_bwd(dh2, q, kv, wo, name, bm=512):
    t, m = q.shape[0], kv.shape[0]
    bm = min(bm, t)

    def body(dh_ref, q_ref, k_ref, v_ref, w_ref, dq_ref, dkv_ref):
        @pl.when(pl.program_id(0) == 0)
        def _():
            dkv_ref[...] = jnp.zeros_like(dkv_ref)

        d_o = _nt(dh_ref[...].astype(BF16), w_ref[...])
        for hd in range(D // XD):
            sl = slice(hd * XD, (hd + 1) * XD)
            qh, kh, vh = q_ref[:, sl].astype(BF16), k_ref[:, sl].astype(BF16), v_ref[:, sl].astype(BF16)
            p = _xattn_probs(qh, kh)
            dob = d_o[:, sl].astype(BF16)
            dp = _nt(dob, vh)
            dkv_ref[:, D + hd * XD:D + (hd + 1) * XD] += _tn(p.astype(BF16), dob)
            ds = (p * (dp - jnp.sum(dp * p, axis=1, keepdims=True)) * (XD ** -0.5)).astype(BF16)
            dq_ref[:, sl] = _nn(ds, kh).astype(BF16)
            dkv_ref[:, sl] += _tn(ds, qh)

    row = pl.BlockSpec((bm, D), lambda i: (i, 0))
    return _pc(body, name, (t // bm,),
               [row, row, pl.BlockSpec((m, D), lambda i: (0, 0)), pl.BlockSpec((m, D), lambda i: (0, 1)),
                pl.BlockSpec((D, D), lambda i: (0, 0))],
               [row, pl.BlockSpec((m, 2 * D), lambda i: (0, 0))],
               [_sds((t, D), BF16), _sds((m, 2 * D))])(dh2, q, kv, kv, wo)


NH1 = 8
FOX_BM = 512
FOX_BQ = 512
FOX_BK = 512
FOX_HEADS_PER_STEP = 4


def _foxprep_fwd(z1, qg, kg, fbp, name):
    t = z1.shape[0]
    bm = min(FOX_BM, t)

    def body(q_ref, k_ref, v_ref, f_ref, qg_ref, kg_ref, fb_ref, qn_ref, kn_ref, vb_ref, c_ref, carry):
        @pl.when(pl.program_id(0) == 0)
        def _():
            carry[...] = jnp.zeros_like(carry)

        for hd in range(NH1):
            sl = slice(hd * HD, (hd + 1) * HD)
            x = q_ref[:, sl]
            qn_ref[:, sl] = (x * _rstd(x) * qg_ref[...] * FOX_QSCALE).astype(BF16)
            x = k_ref[:, sl]
            kn_ref[:, sl] = (x * _rstd(x) * kg_ref[...]).astype(BF16)
        vb_ref[...] = v_ref[...].astype(BF16)
        c = carry[...] + _cumsum_rows(_log_sigmoid(f_ref[...] + fb_ref[...]))
        c_ref[...] = c
        carry[...] = c[bm - 1:bm, :]

    row = lambda c: pl.BlockSpec((bm, D), lambda i: (i, c))
    lane = pl.BlockSpec((bm, HD), lambda i: (i, 4 * D // HD))
    vec = pl.BlockSpec((1, HD), lambda i: (0, 0))
    return _pc(body, name, (t // bm,), [row(0), row(1), row(2), lane, vec, vec, vec],
               [row(0), row(0), row(0), pl.BlockSpec((bm, HD), lambda i: (i, 0))],
               [_sds((t, D), BF16), _sds((t, D), BF16), _sds((t, D), BF16), _sds((t, HD))],
               scratch=[pltpu.VMEM((1, HD), F32)])(z1, z1, z1, z1, qg, kg, fbp)


def _foxprep_bwd(dqn, dkn, z1, qg, kg, fbp, dc, name):
    t = z1.shape[0]
    bm = min(FOX_BM, t)
    nb = t // bm

    def body(dqn_ref, dkn_ref, q_ref, k_ref, f_ref, qg_ref, kg_ref, fb_ref, dc_ref,
             dq_ref, dk_ref, df_ref, dqg_ref, dkg_ref, dfb_ref, carry):
        @pl.when(pl.program_id(0) == 0)
        def _():
            carry[...] = jnp.zeros_like(carry)
            dqg_ref[...] = jnp.zeros_like(dqg_ref)
            dkg_ref[...] = jnp.zeros_like(dkg_ref)
            dfb_ref[...] = jnp.zeros_like(dfb_ref)

        for hd in range(NH1):
            sl = slice(hd * HD, (hd + 1) * HD)
            dx, dgr = _rms_bwd(dqn_ref[:, sl] * (HD ** -0.5), q_ref[:, sl], qg_ref[...])
            dq_ref[:, sl] = dx.astype(BF16)
            dqg_ref[...] += jnp.sum(dgr, axis=0, keepdims=True)
            dx, dgr = _rms_bwd(dkn_ref[:, sl], k_ref[:, sl], kg_ref[...])
            dk_ref[:, sl] = dx.astype(BF16)
            dkg_ref[...] += jnp.sum(dgr, axis=0, keepdims=True)
        dc_ = dc_ref[...]
        dlogf = _rcumsum_rows(dc_) + carry[...]
        carry[...] += jnp.sum(dc_, axis=0, keepdims=True)
        lanes = lax.broadcasted_iota(jnp.int32, dc_.shape, 1)
        df = jnp.where(lanes < NH1, dlogf * (1.0 - _sigmoid(f_ref[...] + fb_ref[...])), 0.0)
        df_ref[...] = df.astype(BF16)
        dfb_ref[...] += jnp.sum(df, axis=0, keepdims=True)

    rv = lambda i: nb - 1 - i
    row = lambda c: pl.BlockSpec((bm, D), lambda i: (rv(i), c))
    lane = lambda c: pl.BlockSpec((bm, HD), lambda i: (rv(i), c))
    vec = pl.BlockSpec((1, HD), lambda i: (0, 0))
    return _pc(body, name, (nb,), [row(0), row(0), row(0), row(1), lane(4 * D // HD), vec, vec, vec, lane(0)],
               [row(0), row(0), lane(0), vec, vec, vec],
               [_sds((t, D), BF16), _sds((t, D), BF16), _sds((t, HD), BF16), _sds((1, HD)), _sds((1, HD)), _sds((1, HD))],
               scratch=[pltpu.VMEM((1, HD), F32)])(dqn, dkn, z1, z1, z1, qg, kg, fbp, dc)


LOG2E = 1.4426950408889634
FOX_QSCALE = HD ** -0.5 * LOG2E


def _fox_steps(t, bq, bk, k_major):
    nq, nk = t // bq, t // bk
    pairs = [(i, j) for i in range(nq) for j in range(nk) if j * bk < (i + 1) * bq]
    if k_major:
        pairs.sort(key=lambda p: (p[1], p[0]))
    outer = [p[1] if k_major else p[0] for p in pairs]
    n = len(pairs)
    flags = [(n_ == 0 or outer[n_] != outer[n_ - 1]) + 2 * (n_ == n - 1 or outer[n_] != outer[n_ + 1])
             + 4 * (not (j + 1) * bk <= i * bq + 1) for n_, (i, j) in enumerate(pairs)]
    as_i32 = lambda v: jnp.asarray(v, jnp.int32)
    return as_i32([p[0] for p in pairs]), as_i32([p[1] for p in pairs]), as_i32(flags)


def _fox_step_info(qi_ref, kj_ref, fl_ref):
    s = pl.program_id(1)
    fl = fl_ref[s]
    return qi_ref[s], kj_ref[s], (fl & 1) != 0, (fl & 2) != 0, (fl & 4) != 0


def _fox_call(body, name, tables, in_specs, out_specs, out_shape, scratch):
    grid_spec = pltpu.PrefetchScalarGridSpec(num_scalar_prefetch=3, grid=(NH1 // FOX_HEADS_PER_STEP, tables[0].shape[0]),
                                             in_specs=in_specs, out_specs=out_specs, scratch_shapes=scratch)
    return pl.pallas_call(body, name=name, grid_spec=grid_spec, out_shape=out_shape,
                          compiler_params=pltpu.CompilerParams(dimension_semantics=("arbitrary", "arbitrary"),
                                                               vmem_limit_bytes=VMEM_LIMIT_V7X))


def _fox_lane_tiles(x):
    return [x[:, c0:c0 + HD] for c0 in range(0, x.shape[1], HD)]


def _fox_masked_scores(q, k, ck, i, j, bq, bk, masked):
    s = _nt(q, k) - ck
    if masked:
        rows = i * bq + lax.broadcasted_iota(jnp.int32, s.shape, 0)
        cols = j * bk + lax.broadcasted_iota(jnp.int32, s.shape, 1)
        s = jnp.where(cols <= rows, s, NEG)
    return s


def _fox_specs(bq, bk, G):
    qspec = pl.BlockSpec((bq, G * HD), lambda h, s, qi, kj, fl: (qi[s], h))
    kspec = pl.BlockSpec((bk, G * HD), lambda h, s, qi, kj, fl: (kj[s], h))
    cspec = pl.BlockSpec((G, 1, bk), lambda h, s, qi, kj, fl: (h, 0, kj[s]))
    colspec = pl.BlockSpec((G, bq, 1), lambda h, s, qi, kj, fl: (h, qi[s], 0))
    return qspec, kspec, cspec, colspec


def _fox_rowmax(qn, kn, crow, name):
    t = qn.shape[0]
    bq, bk, G = min(FOX_BQ, t), min(2 * FOX_BK, t), FOX_HEADS_PER_STEP
    tables = _fox_steps(t, bq, bk, k_major=False)

    def body(qi_ref, kj_ref, fl_ref, q_ref, k_ref, ck_ref, m_ref, *mp):
        i, j, first, last, diag = _fox_step_info(qi_ref, kj_ref, fl_ref)

        @pl.when(first)
        def _():
            for g in range(G):
                mp[g][...] = jnp.full_like(mp[g], NEG)

        def step(masked):
            for g in range(G):
                sl = slice(g * HD, (g + 1) * HD)
                s = _fox_masked_scores(q_ref[:, sl], k_ref[:, sl], ck_ref[g], i, j, bq, bk, masked)
                m = mp[g][...]
                for tile in _fox_lane_tiles(s):
                    m = jnp.maximum(m, tile)
                mp[g][...] = m

        pl.when(jnp.logical_not(diag))(lambda: step(False))
        pl.when(diag)(lambda: step(True))

        @pl.when(last)
        def _():
            for g in range(G):
                m_ref[g] = jnp.max(mp[g][...], axis=1, keepdims=True)

    qspec, kspec, cspec, colspec = _fox_specs(bq, bk, G)
    return _fox_call(body, name, tables, [qspec, kspec, cspec], colspec, _sds((NH1, t, 1)),
                     [pltpu.VMEM((bq, HD), F32)] * G)(*tables, qn, kn, crow)


def _fox_fwd(qn, kn, vb, crow, m, name):
    t = qn.shape[0]
    bq, bk, G = min(FOX_BQ, t), min(FOX_BK, t), FOX_HEADS_PER_STEP
    tables = _fox_steps(t, bq, bk, k_major=False)

    def body(qi_ref, kj_ref, fl_ref, q_ref, k_ref, v_ref, ck_ref, m_ref, o_ref, lse_ref, *scr):
        i, j, first, last, diag = _fox_step_info(qi_ref, kj_ref, fl_ref)
        lp, acc = scr[:G], scr[G:]

        @pl.when(first)
        def _():
            for g in range(G):
                lp[g][...] = jnp.zeros_like(lp[g])
                acc[g][...] = jnp.zeros_like(acc[g])

        def step(masked):
            for g in range(G):
                sl = slice(g * HD, (g + 1) * HD)
                s = _fox_masked_scores(q_ref[:, sl], k_ref[:, sl], ck_ref[g], i, j, bq, bk, masked)
                p = jnp.exp2(s - m_ref[g])
                l = lp[g][...]
                for tile in _fox_lane_tiles(p):
                    l = l + tile
                lp[g][...] = l
                acc[g][...] += _nn(p.astype(BF16), v_ref[:, sl])

        pl.when(jnp.logical_not(diag))(lambda: step(False))
        pl.when(diag)(lambda: step(True))

        @pl.when(last)
        def _():
            for g in range(G):
                l = jnp.sum(lp[g][...], axis=1, keepdims=True)
                o_ref[:, g * HD:(g + 1) * HD] = acc[g][...] / l
                lse_ref[g] = m_ref[g] + jnp.log2(l)

    qspec, kspec, cspec, colspec = _fox_specs(bq, bk, G)
    return _fox_call(body, name, tables, [qspec, kspec, kspec, cspec, colspec], [qspec, colspec],
                     [_sds((t, D)), _sds((NH1, t, 1))], [pltpu.VMEM((bq, HD), F32)] * (2 * G))(*tables, qn, kn, vb, crow, m)


def _fox_bwd(qn, kn, vb, crow, lse, delta, do, name):
    t = qn.shape[0]
    bq, bk, G = min(FOX_BQ, t), min(FOX_BK, t), FOX_HEADS_PER_STEP
    tables = _fox_steps(t, bq, bk, k_major=True)

    def body(qi_ref, kj_ref, fl_ref, q_ref, k_ref, v_ref, ck_ref, lse_ref, dl_ref, do_ref, dq_ref, dk_ref, dv_ref, dc_ref, dcq_ref,
             dk_s, dv_s, dc_s):
        i, j, first, last, diag = _fox_step_info(qi_ref, kj_ref, fl_ref)

        @pl.when(first)
        def _():
            dk_s[...] = jnp.zeros_like(dk_s)
            dv_s[...] = jnp.zeros_like(dv_s)
            dc_s[...] = jnp.zeros_like(dc_s)

        @pl.when(pl.program_id(1) == 0)
        def _():
            dq_ref[...] = jnp.zeros_like(dq_ref)
            dcq_ref[...] = jnp.zeros_like(dcq_ref)

        def step(masked):
            rows = pl.ds(pl.multiple_of(i * bq, bq), bq)
            for g in range(G):
                sl = slice(g * HD, (g + 1) * HD)
                q, k = q_ref[:, sl], k_ref[:, sl]
                s = _fox_masked_scores(q, k, ck_ref[g], i, j, bq, bk, masked)
                p = jnp.exp2(s - lse_ref[g])
                dob = do_ref[:, sl]
                dv_s[:, sl] += _tn(p.astype(BF16), dob)
                ds = p * (_nt(dob, v_ref[:, sl]) - dl_ref[g])
                dsb = ds.astype(BF16)
                dq_ref[rows, sl] += _nn(dsb, k)
                dk_s[:, sl] += _tn(dsb, q)
                dc_s[g] -= jnp.sum(ds, axis=0, keepdims=True)
                part_sum = dcq_ref[g, rows, :]
                for tile in _fox_lane_tiles(ds):
                    part_sum = part_sum + tile
                dcq_ref[g, rows, :] = part_sum

        pl.when(jnp.logical_not(diag))(lambda: step(False))
        pl.when(diag)(lambda: step(True))

        @pl.when(last)
        def _():
            dk_ref[...] = dk_s[...] * (1.0 / LOG2E)
            dv_ref[...] = dv_s[...]
            dc_ref[...] = dc_s[...]

    qspec, kspec, cspec, colspec = _fox_specs(bq, bk, G)
    return _fox_call(
        body, name, tables, [qspec, kspec, kspec, cspec, colspec, colspec, qspec],
        [pl.BlockSpec((t, G * HD), lambda h, s, qi, kj, fl: (0, h)), kspec, kspec, cspec,
         pl.BlockSpec((G, t, HD), lambda h, s, qi, kj, fl: (h, 0, 0))],
        [_sds((t, D)), _sds((t, D)), _sds((t, D)), _sds((NH1, 1, t)), _sds((NH1, t, HD))],
        [pltpu.VMEM((bk, G * HD), F32), pltpu.VMEM((bk, G * HD), F32), pltpu.VMEM((G, 1, bk), F32)],
    )(*tables, qn, kn, vb, crow, lse, delta, do)


def _post1_fwd(o, z1, w, h3, name, bm=512):
    t = o.shape[0]
    bm = min(bm, t)

    def body(o_ref, g_ref, w_ref, h_ref, out_ref, og_ref):
        og_ref[...] = (o_ref[...] * _sigmoid(g_ref[...])).astype(BF16)
        out_ref[...] = h_ref[...] + _nn(og_ref[...], w_ref[...])

    row = lambda c: pl.BlockSpec((bm, D), lambda i: (i, c))
    return _pc(body, name, (t // bm,), [row(0), row(3), pl.BlockSpec((D, D), lambda i: (0, 0)), row(0)],
               [row(0), row(0)], [_sds((t, D)), _sds((t, D), BF16)])(o, z1, w, h3)


def _post1_bwd(dh4, w, o, z1, name, bm=512):
    t = o.shape[0]
    bm = min(bm, t)

    def body(dh_ref, w_ref, o_ref, g_ref, do_ref, dg_ref, dl_ref):
        d_og = _nt(dh_ref[...].astype(BF16), w_ref[...])
        o_, sg = o_ref[...], _sigmoid(g_ref[...])
        dob = (d_og * sg).astype(BF16)
        do_ref[...] = dob
        dg_ref[...] = (d_og * o_ * sg * (1.0 - sg)).astype(BF16)
        prod = dob.astype(F32) * o_
        for hd in range(NH1):
            dl_ref[hd] = jnp.sum(prod[:, hd * HD:(hd + 1) * HD], axis=1, keepdims=True)

    row = lambda c: pl.BlockSpec((bm, D), lambda i: (i, c))
    return _pc(body, name, (t // bm,), [row(0), pl.BlockSpec((D, D), lambda i: (0, 0)), row(0), row(3)],
               [row(0), row(0), pl.BlockSpec((NH1, bm, 1), lambda i: (0, i, 0))],
               [_sds((t, D), BF16), _sds((t, D), BF16), _sds((NH1, t, 1))])(dh4, w, o, z1)


def _final(h, g, tgt, name, bm=512):
    t = h.shape[0]
    bm = min(bm, t)

    def body(h_ref, g_ref, t_ref, l_ref, dh_ref, dg_ref):
        @pl.when(pl.program_id(0) == 0)
        def _():
            l_ref[...] = jnp.zeros_like(l_ref)
            dg_ref[...] = jnp.zeros_like(dg_ref)

        x, gv = h_ref[...], g_ref[...]
        r = _rstd(x)
        xh = x * r
        e = xh * gv - t_ref[...]
        l_ref[...] += 0.5 * jnp.sum(jnp.mean(e * e, axis=1, keepdims=True), axis=0, keepdims=True)
        dy = e * (1.0 / D)
        dg_ref[...] += jnp.sum(dy * xh, axis=0, keepdims=True)
        dxh = dy * gv
        dh_ref[...] = r * (dxh - xh * jnp.mean(dxh * xh, axis=1, keepdims=True))

    row = pl.BlockSpec((bm, D), lambda i: (i, 0))
    vec = pl.BlockSpec((1, D), lambda i: (0, 0))
    return _pc(body, name, (t // bm,), [row, vec, row], [pl.BlockSpec((1, HD), lambda i: (0, 0)), row, vec],
               [_sds((1, HD)), _sds((t, D)), _sds((1, D))])(h, g, tgt)


def _adam(w, g, m, v, name):
    r, c = w.shape
    br = min(r, 256)

    def body(w_ref, g_ref, m_ref, v_ref, d_ref, mo_ref, vo_ref):
        gv = g_ref[...]
        mn = ADAM_B1 * m_ref[...] + (1.0 - ADAM_B1) * gv
        vn = ADAM_B2 * v_ref[...] + (1.0 - ADAM_B2) * jnp.square(gv)
        m_hat = mn / (1.0 - ADAM_B1 ** ADAM_STEP)
        v_hat = vn / (1.0 - ADAM_B2 ** ADAM_STEP)
        d_ref[...] = -ADAM_LR * (m_hat / (jnp.sqrt(v_hat) + ADAM_EPS) + ADAM_WD * w_ref[...])
        mo_ref[...] = mn
        vo_ref[...] = vn

    blk = pl.BlockSpec((br, c), lambda i: (i, 0))
    return _pc(body, name, (r // br,), [blk] * 4, [blk] * 3, [_sds((r, c))] * 3)(w, g, m, v)


ZW = 4224
GATE0 = 4096


def _pack_w_in0(w):
    return jnp.concatenate([w[:, :2048], w[:, 2056:], w[:, 2048:2056], jnp.zeros((w.shape[0], ZW - 4104), w.dtype)], axis=1)


def _unpack_w_in0(g):
    return jnp.concatenate([g[:, :2048], g[:, GATE0:GATE0 + 8], g[:, 2048:GATE0]], axis=1)


def _pack_w_in1(w):
    return jnp.concatenate([w, jnp.zeros((w.shape[0], ZW - 4104), w.dtype)], axis=1)


def _unpack_w_in1(g):
    return g[:, :4104]


def _local_step(x, mem, tgt, W, S, late_weights=None, grads_hook=None):
    t = x.shape[0]
    row = lambda v: v.reshape(1, -1)
    G = {}

    z0, u0 = _norm_mm(x, S["norm_mix_g"][0:1], W["w_in0"], "in0_fwd")
    qk = _conv_fwd(z0, S["conv_w"], "conv_fwd")
    g8 = z0[:, GATE0:GATE0 + 8]
    gates3 = jnp.stack([g8[:, :4].T, g8[:, 4:].T], axis=-1)
    gb = S["gate_b"]
    bias3 = jnp.stack([gb[0, :4], gb[0, 4:]], axis=-1)[:, None, :]
    hm, cs, ns, ms = _mlstm_fwd(qk, z0, gates3, bias3, "mlstm_fwd")
    hh, ss = _hgrn_fwd(z0, S["lb_logits"], "hgrn_fwd")
    if late_weights is not None:
        W = {**W, **late_weights(hh)}
    kv, mn = _memkv_fwd(mem, row(S["mem_norm_g"]), W["wkv_s"], "memkv_fwd")
    h1, y0 = _post0_fwd(hm, hh, z0, S["mlstm_norm_g"], S["hgrn_norm_g"], W["w_out0"], x, "post0_fwd")

    def xattn_mlp_fwd(h, l):
        q, ux = _norm_mm(h, S["norm_xattn_g"][l:l + 1], W["wq"][l], f"xq{l}_fwd")
        h2, ox = _xattn_fwd(q, kv, W["wo"][l], h, f"xattn{l}_fwd")
        h3, a, um = _mlp_fwd(h2, S["norm_mlp_g"][l:l + 1], W["w1s"], W["w2"], l, f"mlp{l}_fwd")
        return h3, (h, q, ux, ox, h2, a, um)

    h3, sv0 = xattn_mlp_fwd(h1, 0)
    z1, u1 = _norm_mm(h3, S["norm_mix_g"][1:2], W["w_in1"], "in1_fwd")
    fbp = jnp.pad(S["c_fgate_b"], ((0, 0), (0, HD - NH1)))
    qn, kn, vb, c = _foxprep_fwd(z1, S["c_qnorm_g"], S["c_knorm_g"], fbp, "foxprep_fwd")
    crow = (c[:, :NH1] * LOG2E).T[:, None, :]
    o1, lse = _fox_fwd(qn, kn, vb, crow, _fox_rowmax(qn, kn, crow, "fox_rowmax"), "fox_fwd")
    h4, og = _post1_fwd(o1, z1, W["w_out1"], h3, "post1_fwd")
    h6, sv1 = xattn_mlp_fwd(h4, 1)
    lossp, dh, G["final_norm_g"] = _final(h6, row(S["final_norm_g"]), tgt, "final")

    grads_ready = grads_hook if grads_hook is not None else (lambda stage, grads: 0.0)
    dkv = None
    dgx, dgm, dwq, dwo, dw1, dw2 = [None, None], [None, None], [None, None], [None, None], [None, None], [None, None]

    def xattn_mlp_bwd(dh, l, sv):
        nonlocal dkv
        h, q, ux, ox, h2, a, um = sv
        dh2, da, r, dgm[l] = _mlp_bwd(dh, a, W["w1s"], W["w2"], l, h2, S["norm_mlp_g"][l:l + 1], f"mlp{l}_bwd")
        dw1[l] = _mm_tn(um, da, f"mlp{l}_dw1", col_chips=NCHIP)
        dw2[l] = _mm_tn(r, dh, f"mlp{l}_dw2")
        dq, dkv_l = _xattn_bwd(dh2, q, kv, W["wo"][l], f"xattn{l}_bwd")
        dkv = dkv_l if dkv is None else dkv + dkv_l
        dwo[l] = _mm_tn(ox, dh2, f"xattn{l}_dwo")
        dwq[l] = _mm_tn(ux, dq, f"xattn{l}_dwq")
        tok = 0.0
        if l == 0:
            G["wkv"] = _mm_tn(mn, dkv, "memkv_dw", col_chips=NCHIP)
            G["mem_norm_g"] = _memkv_bwd(dkv, W["wkv_s"], mem, row(S["mem_norm_g"]), "memkv_bwd")
            tok = grads_ready("layer0_mlp_xattn", dict(wq=dwq[0], wo=dwo[0], w1=dw1[0], w2=dw2[0], wkv=G["wkv"]))
        dh1, dgx[l] = _bwd_in(dq, W["wq"][l], h, S["norm_xattn_g"][l:l + 1] + tok, dh2, f"xq{l}_bwd")
        return dh1

    dh4 = xattn_mlp_bwd(dh, 1, sv1)
    do, dgate, delta = _post1_bwd(dh4, W["w_out1"], o1, z1, "post1_bwd")
    G["w_out1"] = _mm_tn(og, dh4, "post1_dw")
    dqn, dkn, dv1, dcrow, dcq = _fox_bwd(qn, kn, vb, crow, lse, delta, do, "fox_bwd")
    dc = jnp.pad((dcrow[:, 0, :] + jnp.sum(dcq, axis=-1)).T, ((0, 0), (0, HD - NH1)))
    dqr, dkr, df1, G["c_qnorm_g"], G["c_knorm_g"], dfb = _foxprep_bwd(
        dqn, dkn, z1, S["c_qnorm_g"], S["c_knorm_g"], fbp, dc, "foxprep_bwd")
    G["c_fgate_b"] = dfb[:, :NH1]
    dz1 = jnp.concatenate([dqr, dkr, dv1.astype(BF16), dgate, df1], axis=1)
    G["w_in1"] = _mm_tn(u1, dz1, "in1_dw")
    tok = grads_ready("layer1", dict(w_out=G["w_out1"], w_in=G["w_in1"], wq=dwq[1], wo=dwo[1], w1=dw1[1], w2=dw2[1]))
    dh3, dgmix1 = _bwd_in(dz1, W["w_in1"], h3, S["norm_mix_g"][1:2] + tok, dh4, "in1_bwd")
    dh1 = xattn_mlp_bwd(dh3, 0, sv0)

    dhm, dhh, doa, dgb, G["mlstm_norm_g"], G["hgrn_norm_g"] = _post0_bwd(
        dh1, W["w_out0"], hm, hh, z0, S["mlstm_norm_g"], S["hgrn_norm_g"], "post0_bwd")
    G["w_out0"] = _mm_tn(y0, dh1, "post0_dw")
    dqa, dka, dva, dgates3 = _mlstm_bwd(qk, z0, gates3, bias3, cs, ns, ms, dhm, "mlstm_bwd")
    dqb, dfb0, dib, G["lb_logits"] = _hgrn_bwd(z0, S["lb_logits"], ss, dhh, "hgrn_bwd")
    duc, G["conv_w"] = _conv_bwd(z0, S["conv_w"], jnp.concatenate([dqa, dka], axis=1), "conv_bwd")
    dg8 = jnp.concatenate([dgates3[:, :, 0].T, dgates3[:, :, 1].T], axis=1)
    G["gate_b"] = jnp.sum(dg8, axis=0, keepdims=True)
    dz0 = jnp.concatenate([duc, dva.astype(BF16), doa, dqb, dfb0, dib, dgb,
                           jnp.pad(dg8, ((0, 0), (0, HD - 8))).astype(BF16)], axis=1)
    G["w_in0"] = _mm_tn(u0, dz0, "in0_dw")
    dx, dgmix0 = _bwd_in(dz0, W["w_in0"], x, S["norm_mix_g"][0:1], dh1, "in0_bwd")

    G["norm_mix_g"] = jnp.concatenate([dgmix0, dgmix1], axis=0)
    G["norm_xattn_g"] = jnp.concatenate(dgx, axis=0)
    G["norm_mlp_g"] = jnp.concatenate(dgm, axis=0)
    G["wq"], G["wo"], G["w1"], G["w2"] = dwq, dwo, dw1, dw2
    return lossp[0, 0], dx, G


ANY = pl.BlockSpec(memory_space=pl.ANY)
NCHIP = 4


def _place():
    x, y, c = lax.axis_index("x"), lax.axis_index("y"), lax.axis_index("c")
    return x, y, c, [(1 - x, y), (x, 1 - y), (1 - x, 1 - y)]


def _comm_call(body, name, ins, out_shapes, sems):
    return pl.pallas_call(body, name=name, in_specs=[ANY] * len(ins), out_specs=[ANY] * len(out_shapes),
                          out_shape=out_shapes, scratch_shapes=sems)(*ins)


def _gather_weights(arrs, name):
    n = len(arrs)

    def body(*refs):
        ins, outs = refs[:n], refs[n:2 * n]
        send_i, recv_i, send_d, recv_d = refs[2 * n:]
        x, y, c, chips = _place()
        me = 2 * x + y

        def half(a, cc):
            h = arrs[a].shape[0] // 2
            return pl.ds(pl.multiple_of(cc * h, h), h)

        def ici(a, k, src_chip, dst_dev):
            return pltpu.make_async_remote_copy(
                src_ref=ins[a].at[half(a, c)], dst_ref=outs[a].at[src_chip, half(a, c)], send_sem=send_i.at[a, k],
                recv_sem=recv_i.at[a, k], device_id=dst_dev, device_id_type=MESH)

        def d2d(a, k, src_chip, cc):
            reg = outs[a].at[src_chip, half(a, cc)]
            return pltpu.make_async_remote_copy(src_ref=reg, dst_ref=reg, send_sem=send_d.at[a, k], recv_sem=recv_d.at[a, k],
                                                device_id=(x, y, 1 - c), device_id_type=MESH)

        for a in range(n):
            for k, (px, py) in enumerate(chips):
                ici(a, k, me, (px, py, c)).start()
        for k, (px, py) in enumerate(chips):
            for a in range(n):
                ici(a, k, 2 * px + py, (px, py, c)).wait_recv()
                d2d(a, k, 2 * px + py, c).start()
        for k, (px, py) in enumerate(chips):
            for a in range(n):
                ici(a, k, me, (px, py, c)).wait_send()
                d2d(a, k, 2 * px + py, c).wait_send()
                d2d(a, k, 2 * px + py, 1 - c).wait_recv()

    sem = lambda: pltpu.SemaphoreType.DMA((n, 3))
    return _comm_call(body, name, arrs, [_sds((NCHIP,) + a.shape, a.dtype) for a in arrs], [sem(), sem(), sem(), sem()])


HBM = pl.BlockSpec(memory_space=pltpu.HBM)
SEM = pl.BlockSpec(memory_space=pltpu.SEMAPHORE)
DATAFLOW = pltpu.SideEffectType.DATAFLOW_SIDE_EFFECTING


def _half_rows(r, cc):
    return pl.ds(pl.multiple_of(cc * (r // 2), r // 2), r // 2)


def _gather_start(arrs, after, name):
    n = len(arrs)

    def body(*refs):
        ins, lands = refs[:n], refs[n:2 * n]
        send, recv, token = refs[2 * n + 1], refs[2 * n + 2], refs[-1]
        x, y, c, chips = _place()
        me = 2 * x + y
        for a in range(n):
            rows = _half_rows(arrs[a].shape[0], c)
            for k, (px, py) in enumerate(chips):
                pltpu.make_async_remote_copy(src_ref=ins[a].at[rows], dst_ref=lands[a].at[me, rows], send_sem=send.at[3 * a + k],
                                             recv_sem=recv.at[3 * a + k], device_id=(px, py, c), device_id_type=MESH).start()
        token[...] = jnp.zeros_like(token)

    hbm = lambda v: pltpu.with_memory_space_constraint(v, pltpu.HBM)
    land_shapes = [((NCHIP,) + a.shape, a.dtype) for a in arrs]
    out = pl.pallas_call(
        body, name=name,
        out_shape=(pltpu.SemaphoreType.DMA((3 * n,)), pltpu.SemaphoreType.DMA((3 * n,)), *[pltpu.HBM(a.shape, a.dtype) for a in arrs],
                   *[pltpu.HBM(s, d) for s, d in land_shapes], _sds((8, HD))),
        in_specs=[HBM] * (2 * n) + [ANY], out_specs=(SEM, SEM, *[HBM] * (2 * n), pl.BlockSpec(memory_space=pltpu.VMEM)),
        input_output_aliases={i: 2 + i for i in range(2 * n)},
        compiler_params=pltpu.CompilerParams(has_side_effects=DATAFLOW),
    )(*[hbm(a) for a in arrs], *[hbm(lax.empty(s, d)) for s, d in land_shapes], after)
    return out[0], out[1], list(out[2:2 + n]), list(out[2 + n:2 + 2 * n]), out[-1]


def _gather_wait(send, recv, srcs, lands, after, name):
    n = len(srcs)

    def body(*refs):
        ins, lands_ = refs[:n], refs[n:2 * n]
        send_, recv_ = refs[2 * n], refs[2 * n + 1]
        x, y, c, chips = _place()
        for a in range(n):
            rows = _half_rows(srcs[a].shape[0], c)
            for k, (px, py) in enumerate(chips):
                cp = pltpu.make_async_remote_copy(src_ref=ins[a].at[rows], dst_ref=lands_[a].at[2 * px + py, rows], send_sem=send_.at[3 * a + k],
                                                  recv_sem=recv_.at[3 * a + k], device_id=(px, py, c), device_id_type=MESH)
                cp.wait_send()
                cp.wait_recv()

    out = pl.pallas_call(
        body, name=name, out_shape=[pltpu.HBM(v.shape, v.dtype) for v in list(srcs) + list(lands)],
        in_specs=[HBM] * (2 * n) + [SEM, SEM, ANY], out_specs=[HBM] * (2 * n), input_output_aliases={i: i for i in range(2 * n)},
        compiler_params=pltpu.CompilerParams(has_side_effects=DATAFLOW),
    )(*srcs, *lands, send, recv, after)
    return list(out[n:])


def _pair_forward(lands, name):
    n = len(lands)

    def body(*refs):
        ins, outs = refs[:n], refs[n:2 * n]
        send, recv = refs[2 * n:]
        x, y, c, chips = _place()
        copies = []
        for a in range(n):
            r = lands[a].shape[1]
            for k, (px, py) in enumerate(chips):
                cp = pltpu.make_async_remote_copy(
                    src_ref=ins[a].at[2 * px + py, _half_rows(r, c)], dst_ref=outs[a].at[2 * px + py, _half_rows(r, c)],
                    send_sem=send.at[a, k], recv_sem=recv.at[a, k], device_id=(x, y, 1 - c), device_id_type=MESH)
                cp.start()
                copies.append(cp)
        for a in range(n):
            r = lands[a].shape[1]
            for k, (px, py) in enumerate(chips):
                pltpu.make_async_remote_copy(
                    src_ref=ins[a].at[2 * px + py, _half_rows(r, c)], dst_ref=outs[a].at[2 * px + py, _half_rows(r, 1 - c)],
                    send_sem=send.at[a, k], recv_sem=recv.at[a, k], device_id=(x, y, 1 - c), device_id_type=MESH).wait_recv()
        for cp in copies:
            cp.wait_send()

    return pl.pallas_call(body, name=name, in_specs=[ANY] * n, out_specs=[ANY] * n, out_shape=[_sds(v.shape, v.dtype) for v in lands],
                          scratch_shapes=[pltpu.SemaphoreType.DMA((n, 3)), pltpu.SemaphoreType.DMA((n, 3))],
                          input_output_aliases={i: i for i in range(n)})(*lands)


def _pair_exchange(arrs, name):
    n = len(arrs)

    def body(*refs):
        ins, outs = refs[:n], refs[n:2 * n]
        send, recv = refs[2 * n:]
        x, y, c, _ = _place()
        copies = []
        for a in range(n):
            h = arrs[a].shape[1] // 2
            cp = pltpu.make_async_remote_copy(src_ref=ins[a].at[:, pl.ds(pl.multiple_of((1 - c) * h, h), h)], dst_ref=outs[a],
                                              send_sem=send.at[a], recv_sem=recv.at[a], device_id=(x, y, 1 - c), device_id_type=MESH)
            cp.start()
            copies.append(cp)
        for cp in copies:
            cp.wait()

    return _comm_call(body, name, arrs, [_sds((a.shape[0], a.shape[1] // 2, a.shape[2]), a.dtype) for a in arrs],
                      [pltpu.SemaphoreType.DMA((n,)), pltpu.SemaphoreType.DMA((n,))])


def _chip_exchange(arrs, name):
    n = len(arrs)

    def body(*refs):
        ins, outs = refs[:n], refs[n:2 * n]
        send, recv = refs[2 * n:]
        x, y, c, chips = _place()
        me = 2 * x + y
        copies = []
        for a in range(n):
            for k, (px, py) in enumerate(chips):
                r = pltpu.make_async_remote_copy(src_ref=ins[a].at[2 * px + py], dst_ref=outs[a].at[me], send_sem=send.at[a, k],
                                                 recv_sem=recv.at[a, k], device_id=(px, py, c), device_id_type=MESH)
                r.start()
                copies.append(r)
        for cp in copies:
            cp.wait()

    return _comm_call(body, name, arrs, [_sds(a.shape, a.dtype) for a in arrs],
                      [pltpu.SemaphoreType.DMA((n, 3)), pltpu.SemaphoreType.DMA((n, 3))])


def _chip_exchange_start(arrs, name):
    n = len(arrs)

    def body(*refs):
        ins, lands = refs[:n], refs[n:2 * n]
        send, recv, token = refs[2 * n], refs[2 * n + 1], refs[-1]
        x, y, c, chips = _place()
        me = 2 * x + y
        for a in range(n):
            for k, (px, py) in enumerate(chips):
                pltpu.make_async_remote_copy(src_ref=ins[a].at[2 * px + py], dst_ref=lands[a].at[me], send_sem=send.at[3 * a + k],
                                             recv_sem=recv.at[3 * a + k], device_id=(px, py, c), device_id_type=MESH).start()
        token[...] = jnp.zeros_like(token)

    hbm = lambda v: pltpu.with_memory_space_constraint(v, pltpu.HBM)
    out = pl.pallas_call(
        body, name=name,
        out_shape=(pltpu.SemaphoreType.DMA((3 * n,)), pltpu.SemaphoreType.DMA((3 * n,)), *[pltpu.HBM(a.shape, a.dtype) for a in arrs],
                   *[pltpu.HBM(a.shape, a.dtype) for a in arrs], _sds((8, HD))),
        in_specs=[HBM] * (2 * n), out_specs=(SEM, SEM, *[HBM] * (2 * n), pl.BlockSpec(memory_space=pltpu.VMEM)),
        input_output_aliases={i: 2 + i for i in range(2 * n)},
        compiler_params=pltpu.CompilerParams(has_side_effects=DATAFLOW),
    )(*[hbm(a) for a in arrs], *[hbm(lax.empty(a.shape, a.dtype)) for a in arrs])
    return out[0], out[1], list(out[2:2 + n]), list(out[2 + n:2 + 2 * n]), out[-1]


def _chip_exchange_wait(send, recv, srcs, lands, after, name):
    n = len(srcs)

    def body(*refs):
        ins, lands_ = refs[:n], refs[n:2 * n]
        send_, recv_ = refs[2 * n], refs[2 * n + 1]
        x, y, c, chips = _place()
        for a in range(n):
            for k, (px, py) in enumerate(chips):
                cp = pltpu.make_async_remote_copy(src_ref=ins[a].at[2 * px + py], dst_ref=lands_[a].at[2 * px + py], send_sem=send_.at[3 * a + k],
                                                  recv_sem=recv_.at[3 * a + k], device_id=(px, py, c), device_id_type=MESH)
                cp.wait_send()
                cp.wait_recv()

    out = pl.pallas_call(
        body, name=name, out_shape=[pltpu.HBM(v.shape, v.dtype) for v in list(srcs) + list(lands)],
        in_specs=[HBM] * (2 * n) + [SEM, SEM, ANY], out_specs=[HBM] * (2 * n), input_output_aliases={i: i for i in range(2 * n)},
        compiler_params=pltpu.CompilerParams(has_side_effects=DATAFLOW),
    )(*srcs, *lands, send, recv, after)
    return list(out[n:])


def _pair_swap(arrs, name):
    n = len(arrs)

    def body(*refs):
        ins, outs = refs[:n], refs[n:2 * n]
        send, recv = refs[2 * n:]
        x, y, c, _ = _place()
        copies = []
        for a in range(n):
            cp = pltpu.make_async_remote_copy(src_ref=ins[a], dst_ref=outs[a], send_sem=send.at[a], recv_sem=recv.at[a],
                                              device_id=(x, y, 1 - c), device_id_type=MESH)
            cp.start()
            copies.append(cp)
        for cp in copies:
            cp.wait()

    return _comm_call(body, name, arrs, [_sds(a.shape, a.dtype) for a in arrs],
                      [pltpu.SemaphoreType.DMA((n,)), pltpu.SemaphoreType.DMA((n,))])


def _all_gather_devices(v, name):
    def body(v_ref, o_ref, send, recv, loc):
        x, y, c, _ = _place()
        me = 4 * x + 2 * y + c
        own = pltpu.make_async_copy(v_ref, o_ref.at[me], loc)
        own.start()
        copies = [own]
        for k in range(1, 8):
            fx, fy, fc = (k >> 2) & 1, (k >> 1) & 1, k & 1
            peer = (x ^ fx, y ^ fy, c ^ fc)
            r = pltpu.make_async_remote_copy(src_ref=v_ref, dst_ref=o_ref.at[me], send_sem=send.at[k - 1],
                                             recv_sem=recv.at[k - 1], device_id=peer, device_id_type=MESH)
            r.start()
            copies.append(r)
        for cp in copies:
            cp.wait()

    return _comm_call(body, name, [v], [_sds((8,) + v.shape, v.dtype)],
                      [pltpu.SemaphoreType.DMA((7,)), pltpu.SemaphoreType.DMA((7,)), pltpu.SemaphoreType.DMA])[0]


def _row_tile(r):
    return next((b for b in (512, 384, 256, 128, 64, 32, 16) if r % b == 0), r)


def _add2(a, b, out_dtype, name):
    r, w = a.shape
    br = _row_tile(r)

    def body(a_ref, b_ref, o_ref):
        o_ref[...] = (a_ref[...].astype(F32) + b_ref[...].astype(F32)).astype(out_dtype)

    blk = pl.BlockSpec((br, w), lambda i: (i, 0))
    return _pc(body, name, (r // br,), [blk, blk], blk, _sds((r, w), out_dtype))(a, b)


def _sum_slots(a, out_dtype, name, extra=None):
    n, r, w = a.shape
    br = _row_tile(r)

    def body(*refs):
        a_ref, o_ref = refs[0], refs[-1]
        acc = a_ref[0].astype(F32)
        for s in range(1, n):
            acc = acc + a_ref[s].astype(F32)
        if extra is not None:
            acc = acc + refs[1][...].astype(F32)
        o_ref[...] = acc.astype(out_dtype)

    ins = [a] + ([extra] if extra is not None else [])
    specs = [pl.BlockSpec((n, br, w), lambda i: (0, i, 0))] + ([pl.BlockSpec((br, w), lambda i: (i, 0))] if extra is not None else [])
    return _pc(body, name, (r // br,), specs, pl.BlockSpec((br, w), lambda i: (i, 0)), _sds((r, w), out_dtype))(*ins)


SMALL = ["norm_mix_g", "norm_xattn_g", "norm_mlp_g", "final_norm_g", "mem_norm_g", "hgrn_lb_logits", "mlstm_norm_g",
         "hgrn_norm_g", "c_qnorm_g", "c_knorm_g", "ab_gate_b", "c_fgate_b"]
SMALL_ROWS = 16


def _pack_small(parts):
    flat = jnp.concatenate([p.reshape(-1).astype(F32) for p in parts])
    return jnp.pad(flat, (0, SMALL_ROWS * D - flat.shape[0])).reshape(SMALL_ROWS, D)


def _unpack_small(buf, shapes):
    flat, out, off = buf.reshape(-1), [], 0
    for s in shapes:
        n = 1
        for d in s:
            n *= d
        out.append(flat[off:off + n].reshape(s))
        off += n
    return out


def kernel(x, mem, norm_mix_g, norm_xattn_g, norm_mlp_g, final_norm_g, ab_w_in, ab_conv_w, ab_gate_b, hgrn_lb_logits, mlstm_norm_g, hgrn_norm_g, ab_w_out, c_w_in, c_fgate_b, c_qnorm_g, c_knorm_g, c_w_out, mem_norm_g, mem_w_kv, xa_w_q, xa_w_o, mlp_w1, mlp_w2, loss_target, m_norm_mix_g, m_norm_xattn_g, m_norm_mlp_g, m_final_norm_g, m_ab_w_in, m_ab_conv_w, m_ab_gate_b, m_hgrn_lb_logits, m_mlstm_norm_g, m_hgrn_norm_g, m_ab_w_out, m_c_w_in, m_c_fgate_b, m_c_qnorm_g, m_c_knorm_g, m_c_w_out, m_mem_norm_g, m_mem_w_kv, m_xa_w_q, m_xa_w_o, m_mlp_w1, m_mlp_w2, v_norm_mix_g, v_norm_xattn_g, v_norm_mlp_g, v_final_norm_g, v_ab_w_in, v_ab_conv_w, v_ab_gate_b, v_hgrn_lb_logits, v_mlstm_norm_g, v_hgrn_norm_g, v_ab_w_out, v_c_w_in, v_c_fgate_b, v_c_qnorm_g, v_c_knorm_g, v_c_w_out, v_mem_norm_g, v_mem_w_kv, v_xa_w_q, v_xa_w_o, v_mlp_w1, v_mlp_w2):
    A = dict(locals())
    chip = 2 * lax.axis_index("x") + lax.axis_index("y")

    big = ["ab_w_in", "c_w_in", "ab_w_out", "c_w_out", "mem_w_kv", "xa_w_q", "xa_w_o", "mlp_w1", "mlp_w2"]
    shard2d = {"ab_w_in": (D, 1026), "c_w_in": (D, 1026), "ab_w_out": (256, D), "c_w_out": (256, D), "mem_w_kv": (D, 512),
               "xa_w_q": (512, D), "xa_w_o": (512, D), "mlp_w1": (2 * D, D), "mlp_w2": (2 * D, D)}
    shard16 = lambda n: A[n].reshape(shard2d[n]).astype(BF16)
    own_slot = lambda gs, os: [lax.dynamic_update_index_in_dim(g, o, chip, 0) for g, o in zip(gs, os)]
    cols = lambda g: jnp.concatenate([g[k] for k in range(NCHIP)], axis=1)
    per_layer = lambda g: g.reshape(NCHIP, 2, -1, D).transpose(1, 0, 2, 3)
    first = [shard16("ab_w_in"), jnp.pad(ab_conv_w[0], ((0, 16 - CONV_W), (0, 0)))]
    g_in0, g_conv = own_slot(_gather_weights(first, "gather_first"), first)
    W = dict(w_in0=_pack_w_in0(cols(g_in0)))
    rest_names = ["c_w_in", "ab_w_out", "c_w_out", "xa_w_q", "xa_w_o", "mlp_w1", "mlp_w2", "mem_w_kv"]
    rest = [shard16(n) for n in rest_names]
    send_s, recv_s, srcs, lands, token = _gather_start(rest, g_conv, "gather_rest_start")

    def late_weights(after):
        got = _pair_forward(_gather_wait(send_s, recv_s, srcs, lands, after, "gather_rest_wait"), "gather_rest_forward")
        gw = dict(zip(rest_names, own_slot(got, rest)))
        return dict(w_in1=_pack_w_in1(cols(gw["c_w_in"])), w_out0=gw["ab_w_out"].reshape(D, D), w_out1=gw["c_w_out"].reshape(D, D),
                    wkv_s=gw["mem_w_kv"],
                    wq=per_layer(gw["xa_w_q"]).reshape(2, D, D), wo=per_layer(gw["xa_w_o"]).reshape(2, D, D),
                    w1s=gw["mlp_w1"].reshape(NCHIP, 2, D, D), w2=gw["mlp_w2"].reshape(NCHIP, 2, D, D))

    S = dict(norm_mix_g=norm_mix_g + token[0, 0], norm_xattn_g=norm_xattn_g, norm_mlp_g=norm_mlp_g, final_norm_g=final_norm_g,
             conv_w=cols(g_conv[:, :CONV_W]), gate_b=ab_gate_b, lb_logits=hgrn_lb_logits, mlstm_norm_g=mlstm_norm_g,
             hgrn_norm_g=hgrn_norm_g, c_fgate_b=c_fgate_b, c_qnorm_g=c_qnorm_g, c_knorm_g=c_knorm_g, mem_norm_g=mem_norm_g)

    core = lax.axis_index("c")
    by_rows = lambda g: g.reshape(NCHIP, -1, D)

    def stack_cols(g):
        return jnp.stack([g[:, 1026 * k:1026 * (k + 1)] for k in range(NCHIP)])

    def pair_sums(arrs, tag):
        theirs = _pair_exchange(arrs, f"pair_exchange_{tag}")
        out = []
        for i, (a, th) in enumerate(zip(arrs, theirs)):
            h = a.shape[1] // 2
            mine = lax.dynamic_slice_in_dim(a, core * h, h, axis=1)
            out.append(_add2(mine.reshape(-1, a.shape[2]), th.reshape(-1, a.shape[2]), BF16, f"pair_sum_{tag}{i}").reshape(th.shape))
        return out

    def chip_sums(psums, from_chips, tag):
        out = []
        for i, (f, p) in enumerate(zip(from_chips, psums)):
            f = lax.dynamic_update_index_in_dim(f, lax.dynamic_index_in_dim(p, chip, 0, keepdims=False), chip, 0)
            out.append(_sum_slots(f, F32, f"chip_sum_{tag}{i}"))
        return out

    started = {}

    def grads_hook(stage, g):
        if stage == "layer1":
            arrs = [jnp.concatenate([by_rows(g["w_out"]), by_rows(g["wq"]), by_rows(g["wo"]), g["w1"], by_rows(g["w2"])], axis=1),
                    stack_cols(_unpack_w_in1(g["w_in"]))]
        else:
            arrs = [jnp.concatenate([by_rows(g["wq"]), by_rows(g["wo"]), g["w1"], by_rows(g["w2"])], axis=1), g["wkv"]]
        psums = pair_sums(arrs, stage)
        *handles, token = _chip_exchange_start(psums, f"chip_exchange_start_{stage}")
        started[stage] = (psums, handles)
        return token[0, 0]

    lossp, dx, G = _local_step(x[0], mem[0], loss_target[0], W, S, late_weights, grads_hook)

    gsmall = {"norm_mix_g": G["norm_mix_g"], "norm_xattn_g": G["norm_xattn_g"], "norm_mlp_g": G["norm_mlp_g"],
              "final_norm_g": G["final_norm_g"], "mem_norm_g": G["mem_norm_g"], "hgrn_lb_logits": G["lb_logits"],
              "mlstm_norm_g": G["mlstm_norm_g"], "hgrn_norm_g": G["hgrn_norm_g"], "c_qnorm_g": G["c_qnorm_g"],
              "c_knorm_g": G["c_knorm_g"], "ab_gate_b": G["gate_b"], "c_fgate_b": G["c_fgate_b"]}
    packed = _pack_small([gsmall[n] for n in SMALL] + [G["conv_w"], lossp])
    red = _sum_slots(_all_gather_devices(packed, "gather_small"), F32, "sum_small")
    small_shapes = [A[n].shape for n in SMALL]
    *gs, gconv, loss = _unpack_small(red, small_shapes + [(CONV_W, D), ()])
    gs = dict(zip(SMALL, gs))
    gconv = lax.dynamic_slice_in_dim(gconv, chip * 256, 256, axis=1)[None]

    last = pair_sums([by_rows(G["w_out0"]), stack_cols(_unpack_w_in0(G["w_in0"]))], "last")
    rhalf = chip_sums(last, _chip_exchange(last, "chip_exchange_last"), "last")
    for stage in ("layer1", "layer0_mlp_xattn"):
        psums, handles = started[stage]
        rhalf += chip_sums(psums, _chip_exchange_wait(*handles, dx, f"chip_exchange_wait_{stage}"), stage)
    other = _pair_swap(rhalf, "pair_swap")
    r_out0, r_in0, r_l1, r_in1, r_l0, r_kv = [
        jnp.where(core == 0, jnp.concatenate([m_, o_], axis=0), jnp.concatenate([o_, m_], axis=0)) for m_, o_ in zip(rhalf, other)]
    gbig = {"ab_w_in": r_in0, "c_w_in": r_in1, "mem_w_kv": r_kv, "ab_w_out": r_out0, "c_w_out": r_l1[0:256],
            "xa_w_q": jnp.concatenate([r_l0[0:256], r_l1[256:512]], axis=0),
            "xa_w_o": jnp.concatenate([r_l0[256:512], r_l1[512:768]], axis=0),
            "mlp_w1": jnp.concatenate([r_l0[512:1536], r_l1[768:1792]], axis=0),
            "mlp_w2": jnp.concatenate([r_l0[1536:2560], r_l1[1792:2816]], axis=0)}

    out_g, out_d, out_m, out_v = {}, {}, {}, {}
    for n in big:
        d_, m_, v_ = _adam(A[n].reshape(shard2d[n]), gbig[n], A["m_" + n].reshape(shard2d[n]), A["v_" + n].reshape(shard2d[n]), "adam_" + n)
        out_g[n] = gbig[n].reshape(A[n].shape)
        out_d[n], out_m[n], out_v[n] = d_.reshape(A[n].shape), m_.reshape(A[n].shape), v_.reshape(A[n].shape)
    sd, sm, sv = _adam(_pack_small([A[n] for n in SMALL]), _pack_small([gs[n] for n in SMALL]),
                       _pack_small([A["m_" + n] for n in SMALL]), _pack_small([A["v_" + n] for n in SMALL]), "adam_small")
    for n, d_, m_, v_ in zip(SMALL, _unpack_small(sd, small_shapes), _unpack_small(sm, small_shapes), _unpack_small(sv, small_shapes)):
        out_g[n], out_d[n], out_m[n], out_v[n] = gs[n], d_, m_, v_
    cd, cm_, cv = _adam(ab_conv_w[0], gconv[0], m_ab_conv_w[0], v_ab_conv_w[0], "adam_conv")
    out_g["ab_conv_w"], out_d["ab_conv_w"], out_m["ab_conv_w"], out_v["ab_conv_w"] = gconv, cd[None], cm_[None], cv[None]

    order = ["norm_mix_g", "norm_xattn_g", "norm_mlp_g", "final_norm_g", "ab_w_in", "ab_conv_w", "ab_gate_b", "hgrn_lb_logits",
             "mlstm_norm_g", "hgrn_norm_g", "ab_w_out", "c_w_in", "c_fgate_b", "c_qnorm_g", "c_knorm_g", "c_w_out", "mem_norm_g",
             "mem_w_kv", "xa_w_q", "xa_w_o", "mlp_w1", "mlp_w2"]
    return (loss, dx[None], *[out_g[n] for n in order], *[out_d[n] for n in order], *[out_m[n] for n in order],
            *[out_v[n] for n in order])
```

```python
import functools

import jax
import jax.numpy as jnp
from jax import lax
from jax.experimental import pallas as pl
from jax.experimental.pallas import tpu as pltpu

F32 = jnp.float32
BF16 = jnp.bfloat16
EPS = 1e-6
D = 1024
CHUNK = 64
REC_CHUNKS = 4
HD = 128
XD = 256
NEG = -1e30
VMEM_LIMIT_V7X = 56 * 1024 * 1024
ADAM_LR, ADAM_B1, ADAM_B2, ADAM_EPS, ADAM_WD, ADAM_STEP = 0.001, 0.9, 0.999, 1e-08, 0.01, 10
MESH = pl.DeviceIdType.MESH


def _pc(body, name, grid, in_specs, out_specs, out_shape, scratch=(), **kw):
    return pl.pallas_call(
        body, name=name, grid=grid, in_specs=in_specs, out_specs=out_specs, out_shape=out_shape,
        scratch_shapes=scratch,
        compiler_params=pltpu.CompilerParams(
            dimension_semantics=("arbitrary",) * len(grid), vmem_limit_bytes=VMEM_LIMIT_V7X), **kw)


def _sds(shape, dtype=F32):
    return jax.ShapeDtypeStruct(shape, dtype)


def _blk(n, target):
    return max(b for b in range(128, max(target, 128) + 1, 128) if n % b == 0)


def _dot(a, b, dims):
    return lax.dot_general(a, b, (dims, ((), ())), preferred_element_type=F32)


def _nn(a, b):
    return _dot(a, b, ((1,), (0,)))


def _nt(a, b):
    return _dot(a, b, ((1,), (1,)))


def _tn(a, b):
    return _dot(a, b, ((0,), (0,)))


def _sigmoid(x):
    return 1.0 / (1.0 + jnp.exp(-x))


def _log_sigmoid(x):
    return jnp.minimum(x, 0.0) - jnp.log(1.0 + jnp.exp(-jnp.abs(x)))


def _rstd(x):
    return lax.rsqrt(jnp.mean(x * x, axis=-1, keepdims=True) + EPS)


def _rms_bwd(du, x, g):
    r = _rstd(x)
    xh = x * r
    dxh = du * g
    dx = r * (dxh - xh * jnp.mean(dxh * xh, axis=-1, keepdims=True))
    return dx, du * xh


def _norm_mm(h, g, w, name, bm=1024, bn=512):
    t, n = h.shape[0], w.shape[1]
    bm, bn = min(bm, t), _blk(n, 3 * bn)

    def body(h_ref, g_ref, w_ref, z_ref, u_ref):
        @pl.when(pl.program_id(1) == 0)
        def _():
            x = h_ref[...]
            u_ref[...] = (x * _rstd(x) * g_ref[...]).astype(BF16)
        z_ref[...] = _nn(u_ref[...], w_ref[...])

    return _pc(body, name, (t // bm, n // bn),
               [pl.BlockSpec((bm, D), lambda i, j: (i, 0)), pl.BlockSpec((1, D), lambda i, j: (0, 0)),
                pl.BlockSpec((D, bn), lambda i, j: (0, j))],
               [pl.BlockSpec((bm, bn), lambda i, j: (i, j)), pl.BlockSpec((bm, D), lambda i, j: (i, 0))],
               [_sds((t, n)), _sds((t, D), BF16)])(h, g, w)


def _mm_tn(a, b, name, bm=1024, bn=1024, bt=4096, col_chips=None):
    t, m = a.shape
    n = b.shape[1]
    bm, bn, bt = _blk(m, bm), (n // col_chips if col_chips else _blk(n, bn + bn // 2)), min(bt, t)
    nt = t // bt

    def body(a_ref, b_ref, o_ref, acc):
        k = pl.program_id(2)

        @pl.when(k == 0)
        def _():
            acc[...] = jnp.zeros_like(acc)

        acc[...] += _tn(a_ref[...].astype(BF16), b_ref[...].astype(BF16))

        @pl.when(k == nt - 1)
        def _():
            o_ref[...] = acc[...].astype(BF16)

    if col_chips:
        out_spec, out_shape = pl.BlockSpec((None, bm, bn), lambda i, j, k: (j, i, 0)), _sds((col_chips, m, bn), BF16)
    else:
        out_spec, out_shape = pl.BlockSpec((bm, bn), lambda i, j, k: (i, j)), _sds((m, n), BF16)
    return _pc(body, name, (m // bm, n // bn, nt),
               [pl.BlockSpec((bt, bm), lambda i, j, k: (k, i)), pl.BlockSpec((bt, bn), lambda i, j, k: (k, j))],
               out_spec, out_shape, scratch=[pltpu.VMEM((bm, bn), F32)])(a, b)


def _bwd_in(dz, w, h, g, dh, name, bm=1024, bk=1024):
    t, n = dz.shape
    if n > 2 * bk:
        bm, bk = min(bm // 2, t), n
    else:
        bm, bk = min(bm, t), _blk(n, bk + bk // 2)
    nk = n // bk

    def body(dz_ref, w_ref, h_ref, g_ref, dh_ref, o_ref, dg_ref, acc):
        i, k = pl.program_id(0), pl.program_id(1)

        @pl.when(k == 0)
        def _():
            acc[...] = jnp.zeros_like(acc)

        @pl.when((i == 0) & (k == 0))
        def _():
            dg_ref[...] = jnp.zeros_like(dg_ref)

        acc[...] += _nt(dz_ref[...], w_ref[...])

        @pl.when(k == nk - 1)
        def _():
            dx, dgr = _rms_bwd(acc[...], h_ref[...], g_ref[...])
            o_ref[...] = dh_ref[...] + dx
            dg_ref[...] += jnp.sum(dgr, axis=0, keepdims=True)

    return _pc(body, name, (t // bm, nk),
               [pl.BlockSpec((bm, bk), lambda i, k: (i, k)), pl.BlockSpec((D, bk), lambda i, k: (0, k)),
                pl.BlockSpec((bm, D), lambda i, k: (i, 0)), pl.BlockSpec((1, D), lambda i, k: (0, 0)),
                pl.BlockSpec((bm, D), lambda i, k: (i, 0))],
               [pl.BlockSpec((bm, D), lambda i, k: (i, 0)), pl.BlockSpec((1, D), lambda i, k: (0, 0))],
               [_sds((t, D)), _sds((1, D))], scratch=[pltpu.VMEM((bm, D), F32)])(dz, w, h, g, dh)


def _mlp_fwd(h, g, w1s, w2, l, name, bm=1024):
    t = h.shape[0]
    bm = min(bm, t)
    nk = w1s.shape[0]

    def body(h_ref, g_ref, w1_ref, w2_ref, o_ref, a_ref, u_ref, acc):
        k = pl.program_id(1)

        @pl.when(k == 0)
        def _():
            x = h_ref[...]
            u_ref[...] = (x * _rstd(x) * g_ref[...]).astype(BF16)
            acc[...] = jnp.zeros_like(acc)

        a = _nn(u_ref[...], w1_ref[...])
        a_ref[...] = a
        r = jnp.square(jnp.maximum(a, 0.0)).astype(BF16)
        acc[...] += _nn(r, w2_ref[...])

        @pl.when(k == nk - 1)
        def _():
            o_ref[...] = h_ref[...] + acc[...]

    return _pc(body, name, (t // bm, nk),
               [pl.BlockSpec((bm, D), lambda i, k: (i, 0)), pl.BlockSpec((1, D), lambda i, k: (0, 0)),
                pl.BlockSpec((None, None, D, D), lambda i, k: (k, l, 0, 0)), pl.BlockSpec((None, None, D, D), lambda i, k: (k, l, 0, 0))],
               [pl.BlockSpec((bm, D), lambda i, k: (i, 0)), pl.BlockSpec((bm, D), lambda i, k: (i, k)),
                pl.BlockSpec((bm, D), lambda i, k: (i, 0))],
               [_sds((t, D)), _sds((t, nk * D)), _sds((t, D), BF16)],
               scratch=[pltpu.VMEM((bm, D), F32)])(h, g, w1s, w2)


def _mlp_bwd(dh, a, w1s, w2, l, h, g, name, bm=512):
    t = h.shape[0]
    bm = min(bm, t)
    nk = w1s.shape[0]

    def body(dh_ref, a_ref, w1_ref, w2_ref, h_ref, g_ref, o_ref, da_ref, r_ref, dg_ref, acc):
        i, k = pl.program_id(0), pl.program_id(1)

        @pl.when(k == 0)
        def _():
            acc[...] = jnp.zeros_like(acc)

        @pl.when((i == 0) & (k == 0))
        def _():
            dg_ref[...] = jnp.zeros_like(dg_ref)

        ap = jnp.maximum(a_ref[...], 0.0)
        r_ref[...] = jnp.square(ap).astype(BF16)
        dr = _nt(dh_ref[...].astype(BF16), w2_ref[...])
        da = (dr * (2.0 * ap)).astype(BF16)
        da_ref[...] = da
        acc[...] += _nt(da, w1_ref[...])

        @pl.when(k == nk - 1)
        def _():
            dx, dgr = _rms_bwd(acc[...], h_ref[...], g_ref[...])
            o_ref[...] = dh_ref[...] + dx
            dg_ref[...] += jnp.sum(dgr, axis=0, keepdims=True)

    return _pc(body, name, (t // bm, nk),
               [pl.BlockSpec((bm, D), lambda i, k: (i, 0)), pl.BlockSpec((bm, D), lambda i, k: (i, k)),
                pl.BlockSpec((None, None, D, D), lambda i, k: (k, l, 0, 0)), pl.BlockSpec((None, None, D, D), lambda i, k: (k, l, 0, 0)),
                pl.BlockSpec((bm, D), lambda i, k: (i, 0)), pl.BlockSpec((1, D), lambda i, k: (0, 0))],
               [pl.BlockSpec((bm, D), lambda i, k: (i, 0)), pl.BlockSpec((bm, D), lambda i, k: (i, k)),
                pl.BlockSpec((bm, D), lambda i, k: (i, k)), pl.BlockSpec((1, D), lambda i, k: (0, 0))],
               [_sds((t, D)), _sds((t, nk * D), BF16), _sds((t, nk * D), BF16), _sds((1, D))],
               scratch=[pltpu.VMEM((bm, D), F32)])(dh, a, w1s, w2, h, g)


def _rows_of(x):
    return lax.broadcasted_iota(jnp.int32, x.shape, 0)


def _shift_down(x, s):
    if s == 0:
        return x
    return jnp.where(_rows_of(x) >= s, pltpu.roll(x, s, 0), 0.0)


def _shift_up(x, s):
    if s == 0:
        return x
    n = x.shape[0]
    return jnp.where(_rows_of(x) < n - s, pltpu.roll(x, n - s, 0), 0.0)


def _cumsum_rows(x):
    n, s = x.shape[0], 1
    while s < n:
        x = x + _shift_down(x, s)
        s *= 2
    return x


def _rcumsum_rows(x):
    n, s = x.shape[0], 1
    while s < n:
        x = x + _shift_up(x, s)
        s *= 2
    return x


def _silu(x):
    return x * _sigmoid(x)


def _dsilu(x):
    s = _sigmoid(x)
    return s * (1.0 + x * (1.0 - s))


CONV_W = 4


def _conv_pre(u, w):
    y = _shift_down(u, CONV_W - 1) * w[0:1, :]
    for j in range(1, CONV_W):
        y = y + _shift_down(u, CONV_W - 1 - j) * w[j:j + 1, :]
    return y


def _conv_fwd(z0, cw, name):
    t = z0.shape[0]

    def body(u_ref, w_ref, o_ref):
        o_ref[...] = _silu(_conv_pre(u_ref[...], w_ref[...]))

    return _pc(body, name, (2 * 512 // HD,),
               [pl.BlockSpec((t, HD), lambda c: (0, c)), pl.BlockSpec((CONV_W, HD), lambda c: (0, c))],
               pl.BlockSpec((t, HD), lambda c: (0, c)), _sds((t, 1024)))(z0, cw)


def _conv_bwd(z0, cw, dy, name):
    t = z0.shape[0]

    def body(u_ref, w_ref, dy_ref, du_ref, dw_ref):
        u, w = u_ref[...], w_ref[...]
        dpre = dy_ref[...] * _dsilu(_conv_pre(u, w))
        du = _shift_up(dpre, CONV_W - 1) * w[0:1, :]
        for j in range(1, CONV_W):
            du = du + _shift_up(dpre, CONV_W - 1 - j) * w[j:j + 1, :]
        du_ref[...] = du.astype(BF16)
        for j in range(CONV_W):
            dw_ref[j:j + 1, :] = jnp.sum(dpre * _shift_down(u, CONV_W - 1 - j), axis=0, keepdims=True)

    return _pc(body, name, (2 * 512 // HD,),
               [pl.BlockSpec((t, HD), lambda c: (0, c)), pl.BlockSpec((CONV_W, HD), lambda c: (0, c)),
                pl.BlockSpec((t, HD), lambda c: (0, c))],
               [pl.BlockSpec((t, HD), lambda c: (0, c)), pl.BlockSpec((CONV_W, HD), lambda c: (0, c))],
               [_sds((t, 1024), BF16), _sds((CONV_W, 1024))])(z0, cw, dy)


def _mlstm_gates(gate, bias, m_in):
    L = gate.shape[0]
    r = lax.broadcasted_iota(jnp.int32, (L, L), 0)
    c = lax.broadcasted_iota(jnp.int32, (L, L), 1)
    eye, tril = r == c, c <= r
    i_col = gate[:, 0:1] + bias[:, 0:1]
    f_col = gate[:, 1:2] + bias[:, 1:2]
    logf_col = _log_sigmoid(f_col)
    logf_row = jnp.sum(jnp.where(eye, logf_col, 0.0), axis=0, keepdims=True)
    i_row = jnp.sum(jnp.where(eye, i_col, 0.0), axis=0, keepdims=True)
    b_col = jnp.sum(jnp.where(tril, logf_row, 0.0), axis=1, keepdims=True)
    b_row = jnp.sum(jnp.where(r <= c, logf_col, 0.0), axis=0, keepdims=True)
    logd = jnp.where(tril, b_col - b_row + i_row, NEG)
    inter = b_col + m_in
    m_t = jnp.maximum(inter, jnp.max(logd, axis=1, keepdims=True))
    w_t = jnp.exp(inter - m_t)
    dm = jnp.exp(logd - m_t)
    b_last = b_col[L - 1:L, :]
    log_in = b_last - b_col + i_col
    m_new = jnp.maximum(b_last + m_in, jnp.max(log_in, axis=0, keepdims=True))
    w_col = jnp.exp(log_in - m_new)
    decay = jnp.exp(b_last + m_in - m_new)
    return dict(eye=eye, r=r, c=c, f_col=f_col, m_t=m_t, w_t=w_t, dm=dm, m_new=m_new, w_col=w_col, decay=decay)


def _mlstm_fwd(qk, z0, gates, bias, name):
    t = qk.shape[0]
    nc, nh, L = t // CHUNK, 4, CHUNK
    scale = HD ** -0.5

    def body(q_ref, k_ref, v_ref, g_ref, b_ref, h_ref, cs_ref, ns_ref, ms_ref, c_s, n_s, m_s):
        @pl.when(pl.program_id(0) == 0)
        def _():
            c_s[...] = jnp.zeros_like(c_s)
            n_s[...] = jnp.zeros_like(n_s)
            m_s[...] = jnp.zeros_like(m_s)

        for hd in range(nh):
            sl = slice(hd * HD, (hd + 1) * HD)
            cm, nv, m_in = c_s[hd], n_s[hd], m_s[hd]
            for ck in range(cps):
                rows = slice(ck * L, (ck + 1) * L)
                cs_ref[hd, ck] = cm
                ns_ref[hd, ck] = nv
                ms_ref[hd, ck] = jnp.broadcast_to(m_in, (1, HD))
                q, kh, v = q_ref[rows, sl], k_ref[rows, sl] * scale, v_ref[rows, sl]
                G = _mlstm_gates(g_ref[hd, rows, :], b_ref[hd], m_in)
                qb, kb, vb = q.astype(BF16), kh.astype(BF16), v.astype(BF16)
                sc = _nt(qb, kb) * G["dm"]
                num = _nn(sc.astype(BF16), vb) + G["w_t"] * _nn(qb, cm.astype(BF16))
                den = jnp.sum(sc, axis=1, keepdims=True) + G["w_t"] * jnp.sum(q * nv, axis=1, keepdims=True)
                h_ref[rows, sl] = num / jnp.maximum(jnp.abs(den), jnp.exp(-G["m_t"]))
                wk = G["w_col"] * kh
                cm = G["decay"] * cm + _tn(wk.astype(BF16), vb)
                nv = G["decay"] * nv + jnp.sum(wk, axis=0, keepdims=True)
                m_in = G["m_new"]
            c_s[hd], n_s[hd], m_s[hd] = cm, nv, m_in

    cps = REC_CHUNKS
    hspec = lambda blk: pl.BlockSpec((cps * L, 512), lambda j: (j, blk))
    st = lambda r: pl.BlockSpec((nh, cps, r, HD), lambda j: (0, j, 0, 0))
    return _pc(body, name, (nc // cps,),
               [hspec(0), hspec(1), hspec(2), pl.BlockSpec((nh, cps * L, 2), lambda j: (0, j, 0)),
                pl.BlockSpec((nh, 1, 2), lambda j: (0, 0, 0))],
               [hspec(0), st(HD), st(1), st(1)],
               [_sds((t, 512)), _sds((nh, nc, HD, HD)), _sds((nh, nc, 1, HD)), _sds((nh, nc, 1, HD))],
               scratch=[pltpu.VMEM((nh, HD, HD), F32), pltpu.VMEM((nh, 1, HD), F32), pltpu.VMEM((nh, 1, 1), F32)])(qk, qk, z0, gates, bias)


def _mlstm_bwd(qk, z0, gates, bias, cs, ns, ms, dh, name):
    t = qk.shape[0]
    nc, nh, L = t // CHUNK, 4, CHUNK
    scale = HD ** -0.5

    def body(q_ref, k_ref, v_ref, g_ref, b_ref, cs_ref, ns_ref, ms_ref, dh_ref, dq_ref, dk_ref, dv_ref, dg_ref, dc_s, dn_s):
        @pl.when(pl.program_id(0) == 0)
        def _():
            dc_s[...] = jnp.zeros_like(dc_s)
            dn_s[...] = jnp.zeros_like(dn_s)

        for ck in reversed(range(cps)):
            for hd in range(nh):
                one_head(hd, ck, slice(hd * HD, (hd + 1) * HD), slice(ck * L, (ck + 1) * L), q_ref, k_ref, v_ref, g_ref, b_ref,
                         cs_ref, ns_ref, ms_ref, dh_ref, dq_ref, dk_ref, dv_ref, dg_ref, dc_s, dn_s)

    def one_head(hd, ck, sl, rows, q_ref, k_ref, v_ref, g_ref, b_ref, cs_ref, ns_ref, ms_ref, dh_ref, dq_ref, dk_ref, dv_ref, dg_ref,
                 dc_s, dn_s):
        cm, nv, m_in = cs_ref[hd, ck], ns_ref[hd, ck], ms_ref[hd, ck][:, 0:1]
        q, kh, v = q_ref[rows, sl], k_ref[rows, sl] * scale, v_ref[rows, sl]
        G = _mlstm_gates(g_ref[hd, rows, :], b_ref[hd], m_in)
        w_t, dmat, w_col, decay = G["w_t"], G["dm"], G["w_col"], G["decay"]
        qb, kb, vb, cb = q.astype(BF16), kh.astype(BF16), v.astype(BF16), cm.astype(BF16)
        s = _nt(qb, kb)
        sc = s * dmat
        scb = sc.astype(BF16)
        qc = _nn(qb, cb)
        qn = jnp.sum(q * nv, axis=1, keepdims=True)
        num = _nn(scb, vb) + w_t * qc
        den = jnp.sum(sc, axis=1, keepdims=True) + w_t * qn
        e_m = jnp.exp(-G["m_t"])
        dnm = jnp.maximum(jnp.abs(den), e_m)
        dh_ = dh_ref[rows, sl]
        dnum = dh_ / dnm
        dden = jnp.where(jnp.abs(den) > e_m, -jnp.sum(dh_ * num, axis=1, keepdims=True) / (dnm * dnm) * jnp.sign(den), 0.0)
        dnumb = dnum.astype(BF16)
        dsc = _nt(dnumb, vb) + dden
        dv = _tn(scb, dnumb)
        wd = w_t * dnum
        wdb = wd.astype(BF16)
        ds = dsc * dmat
        dsb = ds.astype(BF16)
        dq = _nt(wdb, cb) + (w_t * dden) * nv + _nn(dsb, kb)
        dc_o = _tn(qb, wdb)
        dn_o = jnp.sum(q * (w_t * dden), axis=0, keepdims=True)
        dw = jnp.sum(dnum * qc, axis=1, keepdims=True) + dden * qn
        dkh = _tn(dsb, qb)
        dlogd = ds * s
        db_col = jnp.sum(dlogd, axis=1, keepdims=True) + dw * w_t
        csum = jnp.sum(dlogd, axis=0, keepdims=True)
        dcn, dnn = dc_s[hd], dn_s[hd]
        dcnb = dcn.astype(BF16)
        kdc = _nn(kb, dcnb)
        dws = jnp.sum(kdc * v, axis=1, keepdims=True) + jnp.sum(kh * dnn, axis=1, keepdims=True)
        dv = dv + w_col * kdc
        dkh = dkh + w_col * (_nt(vb, dcnb) + dnn)
        dlin = dws * w_col
        ddecay = jnp.sum(jnp.sum(dcn * cm, axis=1, keepdims=True), axis=0, keepdims=True) + jnp.sum(dnn * nv, axis=1, keepdims=True)
        dlast = ddecay * decay + jnp.sum(dlin, axis=0, keepdims=True)
        row_id = lax.broadcasted_iota(jnp.int32, (L, 1), 0)
        db_col = db_col - dlin + jnp.where(row_id == L - 1, dlast, 0.0)
        eye, r, c = G["eye"], G["r"], G["c"]
        di = dlin + jnp.sum(jnp.where(eye, csum, 0.0), axis=1, keepdims=True)
        db_row = jnp.sum(jnp.where(eye, db_col, 0.0), axis=0, keepdims=True) - csum
        dlogf = jnp.sum(jnp.where(c >= r, db_row, 0.0), axis=1, keepdims=True)
        dg_ref[hd, rows, 0:1] = di
        dg_ref[hd, rows, 1:2] = dlogf * (1.0 - _sigmoid(G["f_col"]))
        dq_ref[rows, sl] = dq
        dk_ref[rows, sl] = dkh * scale
        dv_ref[rows, sl] = dv
        dc_s[hd] = decay * dcn + dc_o
        dn_s[hd] = decay * dnn + dn_o

    cps = REC_CHUNKS
    rv = lambda j: nc // cps - 1 - j
    hspec = lambda blk: pl.BlockSpec((cps * L, 512), lambda j: (rv(j), blk))
    st = lambda r: pl.BlockSpec((nh, cps, r, HD), lambda j: (0, rv(j), 0, 0))
    gs = pl.BlockSpec((nh, cps * L, 2), lambda j: (0, rv(j), 0))
    return _pc(body, name, (nc // cps,),
               [hspec(0), hspec(1), hspec(2), gs, pl.BlockSpec((nh, 1, 2), lambda j: (0, 0, 0)),
                st(HD), st(1), st(1), hspec(0)],
               [hspec(0), hspec(0), hspec(0), gs],
               [_sds((t, 512)), _sds((t, 512)), _sds((t, 512)), _sds((nh, t, 2))],
               scratch=[pltpu.VMEM((nh, HD, HD), F32), pltpu.VMEM((nh, 1, HD), F32)])(qk, qk, z0, gates, bias, cs, ns, ms, dh)


def _hgrn_act(qb_, fb_, ib_, lg):
    lb = _sigmoid(lg[0:1, :] - lg[1:2, :])
    sg = _sigmoid(fb_)
    f = lb + (1.0 - lb) * sg
    return lb, sg, f, _silu(qb_), (1.0 - lb) * (1.0 - sg), _silu(ib_), _cumsum_rows(jnp.log(f))


HG_SUB = 16


def _hgrn_offdiag(q, k, b, r0):
    beta = b[r0 - 1:r0, :]
    e1 = jnp.exp(b[r0:r0 + HG_SUB, :] - beta)
    e2 = jnp.where(_rows_of(b) < r0, jnp.exp(jnp.minimum(beta - b, 0.0)), 0.0)
    return q[r0:r0 + HG_SUB, :] * e1, k * e2, e1, e2


def _hgrn_fwd(z0, lbl, name):
    t = z0.shape[0]
    nc, nh, L = t // CHUNK, 4, CHUNK

    def body(q_ref, f_ref, i_ref, l_ref, o_ref, ss_ref, st_s):
        @pl.when(pl.program_id(0) == 0)
        def _():
            st_s[...] = jnp.zeros_like(st_s)

        for hd in range(nh):
            sl = slice(hd * HD, (hd + 1) * HD)
            st = st_s[hd]
            for ck in range(cps):
                rows = slice(ck * L, (ck + 1) * L)
                ss_ref[hd, ck] = st
                _, _, _, q, k, v, b = _hgrn_act(q_ref[rows, sl], f_ref[rows, sl], i_ref[rows, sl], l_ref[:, sl])
                o = _nt((q * jnp.exp(b)).astype(BF16), st.astype(BF16))
                sub = _rows_of(b) & (HG_SUB - 1)
                o = o + jnp.sum(q * k, axis=1, keepdims=True) * v
                for dl in range(1, HG_SUB):
                    e = jnp.exp(jnp.where(sub >= dl, b - pltpu.roll(b, dl, 0), NEG))
                    a = jnp.sum(q * pltpu.roll(k, dl, 0) * e, axis=1, keepdims=True)
                    o = o + a * pltpu.roll(v, dl, 0)
                o_ref[rows, sl] = o
                vb = v.astype(BF16)
                for i in range(1, L // HG_SUB):
                    r0 = i * HG_SUB
                    qt, kt, _, _ = _hgrn_offdiag(q, k, b, r0)
                    a = _nt(qt.astype(BF16), kt.astype(BF16))
                    o_ref[ck * L + r0:ck * L + r0 + HG_SUB, sl] += _nn(a.astype(BF16), vb)
                bl = b[L - 1:L, :]
                st = st * jnp.exp(bl) + _tn(v.astype(BF16), (k * jnp.exp(bl - b)).astype(BF16))
            st_s[hd] = st

    cps = REC_CHUNKS
    hspec = lambda blk: pl.BlockSpec((cps * L, 512), lambda j: (j, blk))
    return _pc(body, name, (nc // cps,),
               [hspec(4), hspec(5), hspec(6), pl.BlockSpec((2, 512), lambda j: (0, 0))],
               [hspec(0), pl.BlockSpec((nh, cps, HD, HD), lambda j: (0, j, 0, 0))],
               [_sds((t, 512)), _sds((nh, nc, HD, HD))],
               scratch=[pltpu.VMEM((nh, HD, HD), F32)])(z0, z0, z0, lbl)


def _hgrn_bwd(z0, lbl, ss, do, name):
    t = z0.shape[0]
    nc, nh, L = t // CHUNK, 4, CHUNK

    def body(q_ref, f_ref, i_ref, l_ref, ss_ref, do_ref, dq_ref, df_ref, di_ref, dl_ref, dst_s, dlb_s, dq_a, dk_a, dv_a, db_a):
        @pl.when(pl.program_id(0) == 0)
        def _():
            dst_s[...] = jnp.zeros_like(dst_s)
            dlb_s[...] = jnp.zeros_like(dlb_s)

        for ck in reversed(range(cps)):
            for hd in range(nh):
                one_head(hd, ck, slice(hd * HD, (hd + 1) * HD), slice(ck * L, (ck + 1) * L), q_ref, f_ref, i_ref, l_ref, ss_ref, do_ref,
                         dq_ref, df_ref, di_ref, dl_ref, dst_s, dlb_s, dq_a.at[hd], dk_a.at[hd], dv_a.at[hd], db_a.at[hd])

    def one_head(hd, ck, sl, rs, q_ref, f_ref, i_ref, l_ref, ss_ref, do_ref, dq_ref, df_ref, di_ref, dl_ref, dst_s, dlb_s,
                 dq_a, dk_a, dv_a, db_a):
        st = ss_ref[hd, ck]
        qp, fp, ip = q_ref[rs, sl], f_ref[rs, sl], i_ref[rs, sl]
        lb, sg, f, q, k, v, b = _hgrn_act(qp, fp, ip, l_ref[:, sl])
        do_ = do_ref[rs, sl]
        dob, stb = do_.astype(BF16), st.astype(BF16)
        eb = jnp.exp(b)
        qe = q * eb
        dqe = _nn(dob, stb)
        dst_o = _tn(dob, qe.astype(BF16))
        dq = dqe * eb
        db = dqe * qe
        rows = _rows_of(b)
        sub = rows & (HG_SUB - 1)
        p0 = jnp.sum(do_ * v, axis=1, keepdims=True)
        dq = dq + p0 * k
        dk = p0 * q
        dv = jnp.sum(q * k, axis=1, keepdims=True) * do_
        for dl in range(1, HG_SUB):
            up = L - dl
            kd, vd = pltpu.roll(k, dl, 0), pltpu.roll(v, dl, 0)
            e = jnp.exp(jnp.where(sub >= dl, b - pltpu.roll(b, dl, 0), NEG))
            a = jnp.sum(q * kd * e, axis=1, keepdims=True)
            p = jnp.sum(do_ * vd, axis=1, keepdims=True) * e
            dq = dq + p * kd
            dkd = p * q
            dbb = dkd * kd
            dv = dv + pltpu.roll(a * do_, up, 0)
            dk = dk + pltpu.roll(dkd, up, 0)
            db = db + dbb - pltpu.roll(dbb, up, 0)
        dq_a[...], dk_a[...], dv_a[...], db_a[...] = dq, dk, dv, db
        vb = v.astype(BF16)
        for i in range(1, L // HG_SUB):
            r0 = i * HG_SUB
            blk = slice(r0, r0 + HG_SUB)
            qt, kt, e1, e2 = _hgrn_offdiag(q, k, b, r0)
            qtb, ktb, dob_i = qt.astype(BF16), kt.astype(BF16), do_[blk, :].astype(BF16)
            a = _nt(qtb, ktb).astype(BF16)
            da = _nt(dob_i, vb).astype(BF16)
            dv_a[...] += _tn(a, dob_i)
            dqt = _nn(da, ktb)
            dkt = _tn(da, qtb)
            dq_a[blk, :] += dqt * e1
            t1, t2 = dqt * qt, dkt * kt
            db_a[blk, :] += t1
            dk_a[...] += dkt * e2
            db_a[...] -= t2
            db_a[r0 - 1:r0, :] += jnp.sum(t2, axis=0, keepdims=True) - jnp.sum(t1, axis=0, keepdims=True)
        dq, dk, dv, db = dq_a[...], dk_a[...], dv_a[...], db_a[...]
        dstn = dst_s[hd]
        dstnb = dstn.astype(BF16)
        bl = b[L - 1:L, :]
        ebl = jnp.exp(bl)
        kdec_e = jnp.exp(bl - b)
        kdec = k * kdec_e
        dbl = jnp.sum(dstn * st, axis=0, keepdims=True) * ebl
        dv = dv + _nt(kdec.astype(BF16), dstnb)
        dkdec = _nn(v.astype(BF16), dstnb)
        dk = dk + dkdec * kdec_e
        dx = dkdec * kdec
        dbl = dbl + jnp.sum(dx, axis=0, keepdims=True)
        db = db - dx + jnp.where(rows == L - 1, dbl, 0.0)
        dst_s[hd] = dstn * ebl + dst_o
        dg = _rcumsum_rows(db)
        dfk = dg / f - dk
        dq_ref[rs, sl] = (dq * _dsilu(qp)).astype(BF16)
        di_ref[rs, sl] = (dv * _dsilu(ip)).astype(BF16)
        df_ref[rs, sl] = (dfk * (1.0 - lb) * sg * (1.0 - sg)).astype(BF16)
        dlb_s[hd] += jnp.sum(dfk * (1.0 - sg), axis=0, keepdims=True)

        if ck == 0:
            @pl.when(pl.program_id(0) == nc // cps - 1)
            def _():
                dl0 = dlb_s[hd] * lb * (1.0 - lb)
                dl_ref[0:1, sl] = dl0
                dl_ref[1:2, sl] = -dl0

    cps = REC_CHUNKS
    rv = lambda j: nc // cps - 1 - j
    hspec = lambda blk: pl.BlockSpec((cps * L, 512), lambda j: (rv(j), blk))
    return _pc(body, name, (nc // cps,),
               [hspec(4), hspec(5), hspec(6), pl.BlockSpec((2, 512), lambda j: (0, 0)),
                pl.BlockSpec((nh, cps, HD, HD), lambda j: (0, rv(j), 0, 0)), hspec(0)],
               [hspec(0), hspec(0), hspec(0), pl.BlockSpec((2, 512), lambda j: (0, 0))],
               [_sds((t, 512), BF16), _sds((t, 512), BF16), _sds((t, 512), BF16), _sds((2, 512))],
               scratch=[pltpu.VMEM((nh, HD, HD), F32), pltpu.VMEM((nh, 1, HD), F32)] + [pltpu.VMEM((nh, L, HD), F32)] * 4)(z0, z0, z0, lbl, ss, do)


def _post0_fwd(hm, hh, z0, na, nb, w, h0, name, bm=512):
    t = h0.shape[0]
    bm = min(bm, t)

    def body(hm_ref, hh_ref, oa_ref, gb_ref, na_ref, nb_ref, w_ref, h_ref, o_ref, y_ref):
        for hd in range(4):
            sl = slice(hd * HD, (hd + 1) * HD)
            pa = _sigmoid(oa_ref[:, sl]) * hm_ref[:, sl]
            y_ref[:, sl] = (pa * _rstd(pa) * na_ref[:, sl]).astype(BF16)
            xb = hh_ref[:, sl]
            y_ref[:, 512 + hd * HD:512 + (hd + 1) * HD] = (xb * _rstd(xb) * nb_ref[:, sl] * _silu(gb_ref[:, sl])).astype(BF16)
        o_ref[...] = h_ref[...] + _nn(y_ref[...], w_ref[...])

    row = lambda wd, c: pl.BlockSpec((bm, wd), lambda i: (i, c))
    vec = lambda wd: pl.BlockSpec((1, wd), lambda i: (0, 0))
    return _pc(body, name, (t // bm,),
               [row(512, 0), row(512, 0), row(512, 3), row(512, 7), vec(512), vec(512),
                pl.BlockSpec((D, D), lambda i: (0, 0)), row(D, 0)],
               [row(D, 0), row(D, 0)], [_sds((t, D)), _sds((t, D), BF16)])(hm, hh, z0, z0, na, nb, w, h0)


def _post0_bwd(dh1, w, hm, hh, z0, na, nb, name, bm=512):
    t = dh1.shape[0]
    bm = min(bm, t)

    def body(dh_ref, w_ref, hm_ref, hh_ref, oa_ref, gb_ref, na_ref, nb_ref, dhm_ref, dhh_ref, doa_ref, dgb_ref, dna_ref, dnb_ref):
        @pl.when(pl.program_id(0) == 0)
        def _():
            dna_ref[...] = jnp.zeros_like(dna_ref)
            dnb_ref[...] = jnp.zeros_like(dnb_ref)

        dy = _nt(dh_ref[...].astype(BF16), w_ref[...])
        for hd in range(4):
            sl = slice(hd * HD, (hd + 1) * HD)
            hm_, oa = hm_ref[:, sl], oa_ref[:, sl]
            sg = _sigmoid(oa)
            dpa, dgr = _rms_bwd(dy[:, sl], sg * hm_, na_ref[:, sl])
            dna_ref[:, sl] += jnp.sum(dgr, axis=0, keepdims=True)
            doa_ref[:, sl] = (dpa * hm_ * sg * (1.0 - sg)).astype(BF16)
            dhm_ref[:, sl] = dpa * sg
            xb, gb, nbv = hh_ref[:, sl], gb_ref[:, sl], nb_ref[:, sl]
            dyb = dy[:, 512 + hd * HD:512 + (hd + 1) * HD]
            dgb_ref[:, sl] = (dyb * (xb * _rstd(xb) * nbv) * _dsilu(gb)).astype(BF16)
            dxb, dgr2 = _rms_bwd(dyb * _silu(gb), xb, nbv)
            dnb_ref[:, sl] += jnp.sum(dgr2, axis=0, keepdims=True)
            dhh_ref[:, sl] = dxb

    row = lambda wd, c: pl.BlockSpec((bm, wd), lambda i: (i, c))
    vec = lambda wd: pl.BlockSpec((1, wd), lambda i: (0, 0))
    return _pc(body, name, (t // bm,),
               [row(D, 0), pl.BlockSpec((D, D), lambda i: (0, 0)), row(512, 0), row(512, 0), row(512, 3), row(512, 7),
                vec(512), vec(512)],
               [row(512, 0), row(512, 0), row(512, 0), row(512, 0), vec(512), vec(512)],
               [_sds((t, 512)), _sds((t, 512)), _sds((t, 512), BF16), _sds((t, 512), BF16), _sds((1, 512)), _sds((1, 512))],
               )(dh1, w, hm, hh, z0, z0, na, nb)


def _memkv_fwd(mem, g, wkv_s, name):
    m = mem.shape[0]

    def body(x_ref, g_ref, w_ref, kv_ref, mn_ref):
        x = x_ref[...]
        mn = (x * _rstd(x) * g_ref[...]).astype(BF16)
        mn_ref[...] = mn
        kv_ref[...] = _nn(mn, w_ref[...])

    return _pc(body, name, (4,),
               [pl.BlockSpec((m, D), lambda k: (0, 0)), pl.BlockSpec((1, D), lambda k: (0, 0)),
                pl.BlockSpec((None, D, 512), lambda k: (k, 0, 0))],
               [pl.BlockSpec((m, 512), lambda k: (0, k)), pl.BlockSpec((m, D), lambda k: (0, 0))],
               [_sds((m, 2048)), _sds((m, D), BF16)])(mem, g, wkv_s)


def _memkv_bwd(dkv, wkv_s, mem, g, name):
    m = mem.shape[0]

    def body(d_ref, w_ref, x_ref, g_ref, dg_ref, acc):
        k = pl.program_id(0)

        @pl.when(k == 0)
        def _():
            acc[...] = jnp.zeros_like(acc)

        acc[...] += _nt(d_ref[...].astype(BF16), w_ref[...])

        @pl.when(k == 3)
        def _():
            _, dgr = _rms_bwd(acc[...], x_ref[...], g_ref[...])
            dg_ref[...] = jnp.sum(dgr, axis=0, keepdims=True)

    return _pc(body, name, (4,),
               [pl.BlockSpec((m, 512), lambda k: (0, k)), pl.BlockSpec((None, D, 512), lambda k: (k, 0, 0)),
                pl.BlockSpec((m, D), lambda k: (0, 0)), pl.BlockSpec((1, D), lambda k: (0, 0))],
               pl.BlockSpec((1, D), lambda k: (0, 0)), _sds((1, D)), scratch=[pltpu.VMEM((m, D), F32)])(dkv, wkv_s, mem, g)


def _xattn_probs(qh, kh):
    s = _nt(qh, kh) * (XD ** -0.5)
    p = jnp.exp(s - jnp.max(s, axis=1, keepdims=True))
    return p / jnp.sum(p, axis=1, keepdims=True)


def _xattn_fwd(q, kv, wo, h1, name, bm=512):
    t, m = q.shape[0], kv.shape[0]
    bm = min(bm, t)

    def body(q_ref, k_ref, v_ref, w_ref, h_ref, out_ref, o_ref):
        for hd in range(D // XD):
            sl = slice(hd * XD, (hd + 1) * XD)
            p = _xattn_probs(q_ref[:, sl].astype(BF16), k_ref[:, sl].astype(BF16))
            o_ref[:, sl] = _nn(p.astype(BF16), v_ref[:, sl].astype(BF16)).astype(BF16)
        out_ref[...] = h_ref[...] + _nn(o_ref[...], w_ref[...])

    row = pl.BlockSpec((bm, D), lambda i: (i, 0))
    return _pc(body, name, (t // bm,),
               [row, pl.BlockSpec((m, D), lambda i: (0, 0)), pl.BlockSpec((m, D), lambda i: (0, 1)),
                pl.BlockSpec((D, D), lambda i: (0, 0)), row],
               [row, row], [_sds((t, D)), _sds((t, D), BF16)])(q, kv, kv, wo, h1)


def _xattn_bwd(dh2, q, kv, wo, name, bm=512):
    t, m = q.shape[0], kv.shape[0]
    bm = min(bm, t)

    def body(dh_ref, q_ref, k_ref, v_ref, w_ref, dq_ref, dkv_ref):
        @pl.when(pl.program_id(0) == 0)
        def _():
            dkv_ref[...] = jnp.zeros_like(dkv_ref)

        d_o = _nt(dh_ref[...].astype(BF16), w_ref[...])
        for hd in range(D // XD):
            sl = slice(hd * XD, (hd + 1) * XD)
            qh, kh, vh = q_ref[:, sl].astype(BF16), k_ref[:, sl].astype(BF16), v_ref[:, sl].astype(BF16)
            p = _xattn_probs(qh, kh)
            dob = d_o[:, sl].astype(BF16)
            dp = _nt(dob, vh)
            dkv_ref[:, D + hd * XD:D + (hd + 1) * XD] += _tn(p.astype(BF16), dob)
            ds = (p * (dp - jnp.sum(dp * p, axis=1, keepdims=True)) * (XD ** -0.5)).astype(BF16)
            dq_ref[:, sl] = _nn(ds, kh).astype(BF16)
            dkv_ref[:, sl] += _tn(ds, qh)

    row = pl.BlockSpec((bm, D), lambda i: (i, 0))
    return _pc(body, name, (t // bm,),
               [row, row, pl.BlockSpec((m, D), lambda i: (0, 0)), pl.BlockSpec((m, D), lambda i: (0, 1)),
                pl.BlockSpec((D, D), lambda i: (0, 0))],
               [row, pl.BlockSpec((m, 2 * D), lambda i: (0, 0))],
               [_sds((t, D), BF16), _sds((m, 2 * D))])(dh2, q, kv, kv, wo)


NH1 = 8
FOX_BM = 512
FOX_BQ = 512
FOX_BK = 512
FOX_HEADS_PER_STEP = 4


def _foxprep_fwd(z1, qg, kg, fbp, name):
    t = z1.shape[0]
    bm = min(FOX_BM, t)

    def body(q_ref, k_ref, v_ref, f_ref, qg_ref, kg_ref, fb_ref, qn_ref, kn_ref, vb_ref, c_ref, carry):
        @pl.when(pl.program_id(0) == 0)
        def _():
            carry[...] = jnp.zeros_like(carry)

        for hd in range(NH1):
            sl = slice(hd * HD, (hd + 1) * HD)
            x = q_ref[:, sl]
            qn_ref[:, sl] = (x * _rstd(x) * qg_ref[...] * FOX_QSCALE).astype(BF16)
            x = k_ref[:, sl]
            kn_ref[:, sl] = (x * _rstd(x) * kg_ref[...]).astype(BF16)
        vb_ref[...] = v_ref[...].astype(BF16)
        c = carry[...] + _cumsum_rows(_log_sigmoid(f_ref[...] + fb_ref[...]))
        c_ref[...] = c
        carry[...] = c[bm - 1:bm, :]

    row = lambda c: pl.BlockSpec((bm, D), lambda i: (i, c))
    lane = pl.BlockSpec((bm, HD), lambda i: (i, 4 * D // HD))
    vec = pl.BlockSpec((1, HD), lambda i: (0, 0))
    return _pc(body, name, (t // bm,), [row(0), row(1), row(2), lane, vec, vec, vec],
               [row(0), row(0), row(0), pl.BlockSpec((bm, HD), lambda i: (i, 0))],
               [_sds((t, D), BF16), _sds((t, D), BF16), _sds((t, D), BF16), _sds((t, HD))],
               scratch=[pltpu.VMEM((1, HD), F32)])(z1, z1, z1, z1, qg, kg, fbp)


def _foxprep_bwd(dqn, dkn, z1, qg, kg, fbp, dc, name):
    t = z1.shape[0]
    bm = min(FOX_BM, t)
    nb = t // bm

    def body(dqn_ref, dkn_ref, q_ref, k_ref, f_ref, qg_ref, kg_ref, fb_ref, dc_ref,
             dq_ref, dk_ref, df_ref, dqg_ref, dkg_ref, dfb_ref, carry):
        @pl.when(pl.program_id(0) == 0)
        def _():
            carry[...] = jnp.zeros_like(carry)
            dqg_ref[...] = jnp.zeros_like(dqg_ref)
            dkg_ref[...] = jnp.zeros_like(dkg_ref)
            dfb_ref[...] = jnp.zeros_like(dfb_ref)

        for hd in range(NH1):
            sl = slice(hd * HD, (hd + 1) * HD)
            dx, dgr = _rms_bwd(dqn_ref[:, sl] * (HD ** -0.5), q_ref[:, sl], qg_ref[...])
            dq_ref[:, sl] = dx.astype(BF16)
            dqg_ref[...] += jnp.sum(dgr, axis=0, keepdims=True)
            dx, dgr = _rms_bwd(dkn_ref[:, sl], k_ref[:, sl], kg_ref[...])
            dk_ref[:, sl] = dx.astype(BF16)
            dkg_ref[...] += jnp.sum(dgr, axis=0, keepdims=True)
        dc_ = dc_ref[...]
        dlogf = _rcumsum_rows(dc_) + carry[...]
        carry[...] += jnp.sum(dc_, axis=0, keepdims=True)
        lanes = lax.broadcasted_iota(jnp.int32, dc_.shape, 1)
        df = jnp.where(lanes < NH1, dlogf * (1.0 - _sigmoid(f_ref[...] + fb_ref[...])), 0.0)
        df_ref[...] = df.astype(BF16)
        dfb_ref[...] += jnp.sum(df, axis=0, keepdims=True)

    rv = lambda i: nb - 1 - i
    row = lambda c: pl.BlockSpec((bm, D), lambda i: (rv(i), c))
    lane = lambda c: pl.BlockSpec((bm, HD), lambda i: (rv(i), c))
    vec = pl.BlockSpec((1, HD), lambda i: (0, 0))
    return _pc(body, name, (nb,), [row(0), row(0), row(0), row(1), lane(4 * D // HD), vec, vec, vec, lane(0)],
               [row(0), row(0), lane(0), vec, vec, vec],
               [_sds((t, D), BF16), _sds((t, D), BF16), _sds((t, HD), BF16), _sds((1, HD)), _sds((1, HD)), _sds((1, HD))],
               scratch=[pltpu.VMEM((1, HD), F32)])(dqn, dkn, z1, z1, z1, qg, kg, fbp, dc)


LOG2E = 1.4426950408889634
FOX_QSCALE = HD ** -0.5 * LOG2E


def _fox_steps(t, bq, bk, k_major):
    nq, nk = t // bq, t // bk
    pairs = [(i, j) for i in range(nq) for j in range(nk) if j * bk < (i + 1) * bq]
    if k_major:
        pairs.sort(key=lambda p: (p[1], p[0]))
    outer = [p[1] if k_major else p[0] for p in pairs]
    n = len(pairs)
    flags = [(n_ == 0 or outer[n_] != outer[n_ - 1]) + 2 * (n_ == n - 1 or outer[n_] != outer[n_ + 1])
             + 4 * (not (j + 1) * bk <= i * bq + 1) for n_, (i, j) in enumerate(pairs)]
    as_i32 = lambda v: jnp.asarray(v, jnp.int32)
    return as_i32([p[0] for p in pairs]), as_i32([p[1] for p in pairs]), as_i32(flags)


def _fox_step_info(qi_ref, kj_ref, fl_ref):
    s = pl.program_id(1)
    fl = fl_ref[s]
    return qi_ref[s], kj_ref[s], (fl & 1) != 0, (fl & 2) != 0, (fl & 4) != 0


def _fox_call(body, name, tables, in_specs, out_specs, out_shape, scratch):
    grid_spec = pltpu.PrefetchScalarGridSpec(num_scalar_prefetch=3, grid=(NH1 // FOX_HEADS_PER_STEP, tables[0].shape[0]),
                                             in_specs=in_specs, out_specs=out_specs, scratch_shapes=scratch)
    return pl.pallas_call(body, name=name, grid_spec=grid_spec, out_shape=out_shape,
                          compiler_params=pltpu.CompilerParams(dimension_semantics=("arbitrary", "arbitrary"),
                                                               vmem_limit_bytes=VMEM_LIMIT_V7X))


def _fox_lane_tiles(x):
    return [x[:, c0:c0 + HD] for c0 in range(0, x.shape[1], HD)]


def _fox_masked_scores(q, k, ck, i, j, bq, bk, masked):
    s = _nt(q, k) - ck
    if masked:
        rows = i * bq + lax.broadcasted_iota(jnp.int32, s.shape, 0)
        cols = j * bk + lax.broadcasted_iota(jnp.int32, s.shape, 1)
        s = jnp.where(cols <= rows, s, NEG)
    return s


def _fox_specs(bq, bk, G):
    qspec = pl.BlockSpec((bq, G * HD), lambda h, s, qi, kj, fl: (qi[s], h))
    kspec = pl.BlockSpec((bk, G * HD), lambda h, s, qi, kj, fl: (kj[s], h))
    cspec = pl.BlockSpec((G, 1, bk), lambda h, s, qi, kj, fl: (h, 0, kj[s]))
    colspec = pl.BlockSpec((G, bq, 1), lambda h, s, qi, kj, fl: (h, qi[s], 0))
    return qspec, kspec, cspec, colspec


def _fox_rowmax(qn, kn, crow, name):
    t = qn.shape[0]
    bq, bk, G = min(FOX_BQ, t), min(2 * FOX_BK, t), FOX_HEADS_PER_STEP
    tables = _fox_steps(t, bq, bk, k_major=False)

    def body(qi_ref, kj_ref, fl_ref, q_ref, k_ref, ck_ref, m_ref, *mp):
        i, j, first, last, diag = _fox_step_info(qi_ref, kj_ref, fl_ref)

        @pl.when(first)
        def _():
            for g in range(G):
                mp[g][...] = jnp.full_like(mp[g], NEG)

        def step(masked):
            for g in range(G):
                sl = slice(g * HD, (g + 1) * HD)
                s = _fox_masked_scores(q_ref[:, sl], k_ref[:, sl], ck_ref[g], i, j, bq, bk, masked)
                m = mp[g][...]
                for tile in _fox_lane_tiles(s):
                    m = jnp.maximum(m, tile)
                mp[g][...] = m

        pl.when(jnp.logical_not(diag))(lambda: step(False))
        pl.when(diag)(lambda: step(True))

        @pl.when(last)
        def _():
            for g in range(G):
                m_ref[g] = jnp.max(mp[g][...], axis=1, keepdims=True)

    qspec, kspec, cspec, colspec = _fox_specs(bq, bk, G)
    return _fox_call(body, name, tables, [qspec, kspec, cspec], colspec, _sds((NH1, t, 1)),
                     [pltpu.VMEM((bq, HD), F32)] * G)(*tables, qn, kn, crow)


def _fox_fwd(qn, kn, vb, crow, m, name):
    t = qn.shape[0]
    bq, bk, G = min(FOX_BQ, t), min(FOX_BK, t), FOX_HEADS_PER_STEP
    tables = _fox_steps(t, bq, bk, k_major=False)

    def body(qi_ref, kj_ref, fl_ref, q_ref, k_ref, v_ref, ck_ref, m_ref, o_ref, lse_ref, *scr):
        i, j, first, last, diag = _fox_step_info(qi_ref, kj_ref, fl_ref)
        lp, acc = scr[:G], scr[G:]

        @pl.when(first)
        def _():
            for g in range(G):
                lp[g][...] = jnp.zeros_like(lp[g])
                acc[g][...] = jnp.zeros_like(acc[g])

        def step(masked):
            for g in range(G):
                sl = slice(g * HD, (g + 1) * HD)
                s = _fox_masked_scores(q_ref[:, sl], k_ref[:, sl], ck_ref[g], i, j, bq, bk, masked)
                p = jnp.exp2(s - m_ref[g])
                l = lp[g][...]
                for tile in _fox_lane_tiles(p):
                    l = l + tile
                lp[g][...] = l
                acc[g][...] += _nn(p.astype(BF16), v_ref[:, sl])

        pl.when(jnp.logical_not(diag))(lambda: step(False))
        pl.when(diag)(lambda: step(True))

        @pl.when(last)
        def _():
            for g in range(G):
                l = jnp.sum(lp[g][...], axis=1, keepdims=True)
                o_ref[:, g * HD:(g + 1) * HD] = acc[g][...] / l
                lse_ref[g] = m_ref[g] + jnp.log2(l)

    qspec, kspec, cspec, colspec = _fox_specs(bq, bk, G)
    return _fox_call(body, name, tables, [qspec, kspec, kspec, cspec, colspec], [qspec, colspec],
                     [_sds((t, D)), _sds((NH1, t, 1))], [pltpu.VMEM((bq, HD), F32)] * (2 * G))(*tables, qn, kn, vb, crow, m)


def _fox_bwd(qn, kn, vb, crow, lse, delta, do, name):
    t = qn.shape[0]
    bq, bk, G = min(FOX_BQ, t), min(FOX_BK, t), FOX_HEADS_PER_STEP
    tables = _fox_steps(t, bq, bk, k_major=True)

    def body(qi_ref, kj_ref, fl_ref, q_ref, k_ref, v_ref, ck_ref, lse_ref, dl_ref, do_ref, dq_ref, dk_ref, dv_ref, dc_ref, dcq_ref,
             dk_s, dv_s, dc_s):
        i, j, first, last, diag = _fox_step_info(qi_ref, kj_ref, fl_ref)

        @pl.when(first)
        def _():
            dk_s[...] = jnp.zeros_like(dk_s)
            dv_s[...] = jnp.zeros_like(dv_s)
            dc_s[...] = jnp.zeros_like(dc_s)

        @pl.when(pl.program_id(1) == 0)
        def _():
            dq_ref[...] = jnp.zeros_like(dq_ref)
            dcq_ref[...] = jnp.zeros_like(dcq_ref)

        def step(masked):
            rows = pl.ds(pl.multiple_of(i * bq, bq), bq)
            for g in range(G):
                sl = slice(g * HD, (g + 1) * HD)
                q, k = q_ref[:, sl], k_ref[:, sl]
                s = _fox_masked_scores(q, k, ck_ref[g], i, j, bq, bk, masked)
                p = jnp.exp2(s - lse_ref[g])
                dob = do_ref[:, sl]
                dv_s[:, sl] += _tn(p.astype(BF16), dob)
                ds = p * (_nt(dob, v_ref[:, sl]) - dl_ref[g])
                dsb = ds.astype(BF16)
                dq_ref[rows, sl] += _nn(dsb, k)
                dk_s[:, sl] += _tn(dsb, q)
                dc_s[g] -= jnp.sum(ds, axis=0, keepdims=True)
                part_sum = dcq_ref[g, rows, :]
                for tile in _fox_lane_tiles(ds):
                    part_sum = part_sum + tile
                dcq_ref[g, rows, :] = part_sum

        pl.when(jnp.logical_not(diag))(lambda: step(False))
        pl.when(diag)(lambda: step(True))

        @pl.when(last)
        def _():
            dk_ref[...] = dk_s[...] * (1.0 / LOG2E)
            dv_ref[...] = dv_s[...]
            dc_ref[...] = dc_s[...]

    qspec, kspec, cspec, colspec = _fox_specs(bq, bk, G)
    return _fox_call(
        body, name, tables, [qspec, kspec, kspec, cspec, colspec, colspec, qspec],
        [pl.BlockSpec((t, G * HD), lambda h, s, qi, kj, fl: (0, h)), kspec, kspec, cspec,
         pl.BlockSpec((G, t, HD), lambda h, s, qi, kj, fl: (h, 0, 0))],
        [_sds((t, D)), _sds((t, D)), _sds((t, D)), _sds((NH1, 1, t)), _sds((NH1, t, HD))],
        [pltpu.VMEM((bk, G * HD), F32), pltpu.VMEM((bk, G * HD), F32), pltpu.VMEM((G, 1, bk), F32)],
    )(*tables, qn, kn, vb, crow, lse, delta, do)


def _post1_fwd(o, z1, w, h3, name, bm=512):
    t = o.shape[0]
    bm = min(bm, t)

    def body(o_ref, g_ref, w_ref, h_ref, out_ref, og_ref):
        og_ref[...] = (o_ref[...] * _sigmoid(g_ref[...])).astype(BF16)
        out_ref[...] = h_ref[...] + _nn(og_ref[...], w_ref[...])

    row = lambda c: pl.BlockSpec((bm, D), lambda i: (i, c))
    return _pc(body, name, (t // bm,), [row(0), row(3), pl.BlockSpec((D, D), lambda i: (0, 0)), row(0)],
               [row(0), row(0)], [_sds((t, D)), _sds((t, D), BF16)])(o, z1, w, h3)


def _post1_bwd(dh4, w, o, z1, name, bm=512):
    t = o.shape[0]
    bm = min(bm, t)

    def body(dh_ref, w_ref, o_ref, g_ref, do_ref, dg_ref, dl_ref):
        d_og = _nt(dh_ref[...].astype(BF16), w_ref[...])
        o_, sg = o_ref[...], _sigmoid(g_ref[...])
        dob = (d_og * sg).astype(BF16)
        do_ref[...] = dob
        dg_ref[...] = (d_og * o_ * sg * (1.0 - sg)).astype(BF16)
        prod = dob.astype(F32) * o_
        for hd in range(NH1):
            dl_ref[hd] = jnp.sum(prod[:, hd * HD:(hd + 1) * HD], axis=1, keepdims=True)

    row = lambda c: pl.BlockSpec((bm, D), lambda i: (i, c))
    return _pc(body, name, (t // bm,), [row(0), pl.BlockSpec((D, D), lambda i: (0, 0)), row(0), row(3)],
               [row(0), row(0), pl.BlockSpec((NH1, bm, 1), lambda i: (0, i, 0))],
               [_sds((t, D), BF16), _sds((t, D), BF16), _sds((NH1, t, 1))])(dh4, w, o, z1)


def _final(h, g, tgt, name, bm=512):
    t = h.shape[0]
    bm = min(bm, t)

    def body(h_ref, g_ref, t_ref, l_ref, dh_ref, dg_ref):
        @pl.when(pl.program_id(0) == 0)
        def _():
            l_ref[...] = jnp.zeros_like(l_ref)
            dg_ref[...] = jnp.zeros_like(dg_ref)

        x, gv = h_ref[...], g_ref[...]
        r = _rstd(x)
        xh = x * r
        e = xh * gv - t_ref[...]
        l_ref[...] += 0.5 * jnp.sum(jnp.mean(e * e, axis=1, keepdims=True), axis=0, keepdims=True)
        dy = e * (1.0 / D)
        dg_ref[...] += jnp.sum(dy * xh, axis=0, keepdims=True)
        dxh = dy * gv
        dh_ref[...] = r * (dxh - xh * jnp.mean(dxh * xh, axis=1, keepdims=True))

    row = pl.BlockSpec((bm, D), lambda i: (i, 0))
    vec = pl.BlockSpec((1, D), lambda i: (0, 0))
    return _pc(body, name, (t // bm,), [row, vec, row], [pl.BlockSpec((1, HD), lambda i: (0, 0)), row, vec],
               [_sds((1, HD)), _sds((t, D)), _sds((1, D))])(h, g, tgt)


def _adam(w, g, m, v, name):
    r, c = w.shape
    br = min(r, 256)

    def body(w_ref, g_ref, m_ref, v_ref, d_ref, mo_ref, vo_ref):
        gv = g_ref[...]
        mn = ADAM_B1 * m_ref[...] + (1.0 - ADAM_B1) * gv
        vn = ADAM_B2 * v_ref[...] + (1.0 - ADAM_B2) * jnp.square(gv)
        m_hat = mn / (1.0 - ADAM_B1 ** ADAM_STEP)
        v_hat = vn / (1.0 - ADAM_B2 ** ADAM_STEP)
        d_ref[...] = -ADAM_LR * (m_hat / (jnp.sqrt(v_hat) + ADAM_EPS) + ADAM_WD * w_ref[...])
        mo_ref[...] = mn
        vo_ref[...] = vn

    blk = pl.BlockSpec((br, c), lambda i: (i, 0))
    return _pc(body, name, (r // br,), [blk] * 4, [blk] * 3, [_sds((r, c))] * 3)(w, g, m, v)


ZW = 4224
GATE0 = 4096


def _pack_w_in0(w):
    return jnp.concatenate([w[:, :2048], w[:, 2056:], w[:, 2048:2056], jnp.zeros((w.shape[0], ZW - 4104), w.dtype)], axis=1)


def _unpack_w_in0(g):
    return jnp.concatenate([g[:, :2048], g[:, GATE0:GATE0 + 8], g[:, 2048:GATE0]], axis=1)


def _pack_w_in1(w):
    return jnp.concatenate([w, jnp.zeros((w.shape[0], ZW - 4104), w.dtype)], axis=1)


def _unpack_w_in1(g):
    return g[:, :4104]


def _local_step(x, mem, tgt, W, S, late_weights=None, grads_hook=None):
    t = x.shape[0]
    row = lambda v: v.reshape(1, -1)
    G = {}

    z0, u0 = _norm_mm(x, S["norm_mix_g"][0:1], W["w_in0"], "in0_fwd")
    qk = _conv_fwd(z0, S["conv_w"], "conv_fwd")
    g8 = z0[:, GATE0:GATE0 + 8]
    gates3 = jnp.stack([g8[:, :4].T, g8[:, 4:].T], axis=-1)
    gb = S["gate_b"]
    bias3 = jnp.stack([gb[0, :4], gb[0, 4:]], axis=-1)[:, None, :]
    hm, cs, ns, ms = _mlstm_fwd(qk, z0, gates3, bias3, "mlstm_fwd")
    hh, ss = _hgrn_fwd(z0, S["lb_logits"], "hgrn_fwd")
    if late_weights is not None:
        W = {**W, **late_weights(hh)}
    kv, mn = _memkv_fwd(mem, row(S["mem_norm_g"]), W["wkv_s"], "memkv_fwd")
    h1, y0 = _post0_fwd(hm, hh, z0, S["mlstm_norm_g"], S["hgrn_norm_g"], W["w_out0"], x, "post0_fwd")

    def xattn_mlp_fwd(h, l):
        q, ux = _norm_mm(h, S["norm_xattn_g"][l:l + 1], W["wq"][l], f"xq{l}_fwd")
        h2, ox = _xattn_fwd(q, kv, W["wo"][l], h, f"xattn{l}_fwd")
        h3, a, um = _mlp_fwd(h2, S["norm_mlp_g"][l:l + 1], W["w1s"], W["w2"], l, f"mlp{l}_fwd")
        return h3, (h, q, ux, ox, h2, a, um)

    h3, sv0 = xattn_mlp_fwd(h1, 0)
    z1, u1 = _norm_mm(h3, S["norm_mix_g"][1:2], W["w_in1"], "in1_fwd")
    fbp = jnp.pad(S["c_fgate_b"], ((0, 0), (0, HD - NH1)))
    qn, kn, vb, c = _foxprep_fwd(z1, S["c_qnorm_g"], S["c_knorm_g"], fbp, "foxprep_fwd")
    crow = (c[:, :NH1] * LOG2E).T[:, None, :]
    o1, lse = _fox_fwd(qn, kn, vb, crow, _fox_rowmax(qn, kn, crow, "fox_rowmax"), "fox_fwd")
    h4, og = _post1_fwd(o1, z1, W["w_out1"], h3, "post1_fwd")
    h6, sv1 = xattn_mlp_fwd(h4, 1)
    lossp, dh, G["final_norm_g"] = _final(h6, row(S["final_norm_g"]), tgt, "final")

    grads_ready = grads_hook if grads_hook is not None else (lambda stage, grads: 0.0)
    dkv = None
    dgx, dgm, dwq, dwo, dw1, dw2 = [None, None], [None, None], [None, None], [None, None], [None, None], [None, None]

    def xattn_mlp_bwd(dh, l, sv):
        nonlocal dkv
        h, q, ux, ox, h2, a, um = sv
        dh2, da, r, dgm[l] = _mlp_bwd(dh, a, W["w1s"], W["w2"], l, h2, S["norm_mlp_g"][l:l + 1], f"mlp{l}_bwd")
        dw1[l] = _mm_tn(um, da, f"mlp{l}_dw1", col_chips=NCHIP)
        dw2[l] = _mm_tn(r, dh, f"mlp{l}_dw2")
        dq, dkv_l = _xattn_bwd(dh2, q, kv, W["wo"][l], f"xattn{l}_bwd")
        dkv = dkv_l if dkv is None else dkv + dkv_l
        dwo[l] = _mm_tn(ox, dh2, f"xattn{l}_dwo")
        dwq[l] = _mm_tn(ux, dq, f"xattn{l}_dwq")
        tok = 0.0
        if l == 0:
            G["wkv"] = _mm_tn(mn, dkv, "memkv_dw", col_chips=NCHIP)
            G["mem_norm_g"] = _memkv_bwd(dkv, W["wkv_s"], mem, row(S["mem_norm_g"]), "memkv_bwd")
            tok = grads_ready("layer0_mlp_xattn", dict(wq=dwq[0], wo=dwo[0], w1=dw1[0], w2=dw2[0], wkv=G["wkv"]))
        dh1, dgx[l] = _bwd_in(dq, W["wq"][l], h, S["norm_xattn_g"][l:l + 1] + tok, dh2, f"xq{l}_bwd")
        return dh1

    dh4 = xattn_mlp_bwd(dh, 1, sv1)
    do, dgate, delta = _post1_bwd(dh4, W["w_out1"], o1, z1, "post1_bwd")
    G["w_out1"] = _mm_tn(og, dh4, "post1_dw")
    dqn, dkn, dv1, dcrow, dcq = _fox_bwd(qn, kn, vb, crow, lse, delta, do, "fox_bwd")
    dc = jnp.pad((dcrow[:, 0, :] + jnp.sum(dcq, axis=-1)).T, ((0, 0), (0, HD - NH1)))
    dqr, dkr, df1, G["c_qnorm_g"], G["c_knorm_g"], dfb = _foxprep_bwd(
        dqn, dkn, z1, S["c_qnorm_g"], S["c_knorm_g"], fbp, dc, "foxprep_bwd")
    G["c_fgate_b"] = dfb[:, :NH1]
    dz1 = jnp.concatenate([dqr, dkr, dv1.astype(BF16), dgate, df1], axis=1)
    G["w_in1"] = _mm_tn(u1, dz1, "in1_dw")
    tok = grads_ready("layer1", dict(w_out=G["w_out1"], w_in=G["w_in1"], wq=dwq[1], wo=dwo[1], w1=dw1[1], w2=dw2[1]))
    dh3, dgmix1 = _bwd_in(dz1, W["w_in1"], h3, S["norm_mix_g"][1:2] + tok, dh4, "in1_bwd")
    dh1 = xattn_mlp_bwd(dh3, 0, sv0)

    dhm, dhh, doa, dgb, G["mlstm_norm_g"], G["hgrn_norm_g"] = _post0_bwd(
        dh1, W["w_out0"], hm, hh, z0, S["mlstm_norm_g"], S["hgrn_norm_g"], "post0_bwd")
    G["w_out0"] = _mm_tn(y0, dh1, "post0_dw")
    dqa, dka, dva, dgates3 = _mlstm_bwd(qk, z0, gates3, bias3, cs, ns, ms, dhm, "mlstm_bwd")
    dqb, dfb0, dib, G["lb_logits"] = _hgrn_bwd(z0, S["lb_logits"], ss, dhh, "hgrn_bwd")
    duc, G["conv_w"] = _conv_bwd(z0, S["conv_w"], jnp.concatenate([dqa, dka], axis=1), "conv_bwd")
    dg8 = jnp.concatenate([dgates3[:, :, 0].T, dgates3[:, :, 1].T], axis=1)
    G["gate_b"] = jnp.sum(dg8, axis=0, keepdims=True)
    dz0 = jnp.concatenate([duc, dva.astype(BF16), doa, dqb, dfb0, dib, dgb,
                           jnp.pad(dg8, ((0, 0), (0, HD - 8))).astype(BF16)], axis=1)
    G["w_in0"] = _mm_tn(u0, dz0, "in0_dw")
    dx, dgmix0 = _bwd_in(dz0, W["w_in0"], x, S["norm_mix_g"][0:1], dh1, "in0_bwd")

    G["norm_mix_g"] = jnp.concatenate([dgmix0, dgmix1], axis=0)
    G["norm_xattn_g"] = jnp.concatenate(dgx, axis=0)
    G["norm_mlp_g"] = jnp.concatenate(dgm, axis=0)
    G["wq"], G["wo"], G["w1"], G["w2"] = dwq, dwo, dw1, dw2
    return lossp[0, 0], dx, G


ANY = pl.BlockSpec(memory_space=pl.ANY)
NCHIP = 4


def _place():
    x, y, c = lax.axis_index("x"), lax.axis_index("y"), lax.axis_index("c")
    return x, y, c, [(1 - x, y), (x, 1 - y), (1 - x, 1 - y)]


def _comm_call(body, name, ins, out_shapes, sems):
    return pl.pallas_call(body, name=name, in_specs=[ANY] * len(ins), out_specs=[ANY] * len(out_shapes),
                          out_shape=out_shapes, scratch_shapes=sems)(*ins)


def _gather_weights(arrs, name):
    n = len(arrs)

    def body(*refs):
        ins, outs = refs[:n], refs[n:2 * n]
        send_i, recv_i, send_d, recv_d = refs[2 * n:]
        x, y, c, chips = _place()
        me = 2 * x + y

        def half(a, cc):
            h = arrs[a].shape[0] // 2
            return pl.ds(pl.multiple_of(cc * h, h), h)

        def ici(a, k, src_chip, dst_dev):
            return pltpu.make_async_remote_copy(
                src_ref=ins[a].at[half(a, c)], dst_ref=outs[a].at[src_chip, half(a, c)], send_sem=send_i.at[a, k],
                recv_sem=recv_i.at[a, k], device_id=dst_dev, device_id_type=MESH)

        def d2d(a, k, src_chip, cc):
            reg = outs[a].at[src_chip, half(a, cc)]
            return pltpu.make_async_remote_copy(src_ref=reg, dst_ref=reg, send_sem=send_d.at[a, k], recv_sem=recv_d.at[a, k],
                                                device_id=(x, y, 1 - c), device_id_type=MESH)

        for a in range(n):
            for k, (px, py) in enumerate(chips):
                ici(a, k, me, (px, py, c)).start()
        for k, (px, py) in enumerate(chips):
            for a in range(n):
                ici(a, k, 2 * px + py, (px, py, c)).wait_recv()
                d2d(a, k, 2 * px + py, c).start()
        for k, (px, py) in enumerate(chips):
            for a in range(n):
                ici(a, k, me, (px, py, c)).wait_send()
                d2d(a, k, 2 * px + py, c).wait_send()
                d2d(a, k, 2 * px + py, 1 - c).wait_recv()

    sem = lambda: pltpu.SemaphoreType.DMA((n, 3))
    return _comm_call(body, name, arrs, [_sds((NCHIP,) + a.shape, a.dtype) for a in arrs], [sem(), sem(), sem(), sem()])


HBM = pl.BlockSpec(memory_space=pltpu.HBM)
SEM = pl.BlockSpec(memory_space=pltpu.SEMAPHORE)
DATAFLOW = pltpu.SideEffectType.DATAFLOW_SIDE_EFFECTING


def _half_rows(r, cc):
    return pl.ds(pl.multiple_of(cc * (r // 2), r // 2), r // 2)


def _gather_start(arrs, after, name):
    n = len(arrs)

    def body(*refs):
        ins, lands = refs[:n], refs[n:2 * n]
        send, recv, token = refs[2 * n + 1], refs[2 * n + 2], refs[-1]
        x, y, c, chips = _place()
        me = 2 * x + y
        for a in range(n):
            rows = _half_rows(arrs[a].shape[0], c)
            for k, (px, py) in enumerate(chips):
                pltpu.make_async_remote_copy(src_ref=ins[a].at[rows], dst_ref=lands[a].at[me, rows], send_sem=send.at[3 * a + k],
                                             recv_sem=recv.at[3 * a + k], device_id=(px, py, c), device_id_type=MESH).start()
        token[...] = jnp.zeros_like(token)

    hbm = lambda v: pltpu.with_memory_space_constraint(v, pltpu.HBM)
    land_shapes = [((NCHIP,) + a.shape, a.dtype) for a in arrs]
    out = pl.pallas_call(
        body, name=name,
        out_shape=(pltpu.SemaphoreType.DMA((3 * n,)), pltpu.SemaphoreType.DMA((3 * n,)), *[pltpu.HBM(a.shape, a.dtype) for a in arrs],
                   *[pltpu.HBM(s, d) for s, d in land_shapes], _sds((8, HD))),
        in_specs=[HBM] * (2 * n) + [ANY], out_specs=(SEM, SEM, *[HBM] * (2 * n), pl.BlockSpec(memory_space=pltpu.VMEM)),
        input_output_aliases={i: 2 + i for i in range(2 * n)},
        compiler_params=pltpu.CompilerParams(has_side_effects=DATAFLOW),
    )(*[hbm(a) for a in arrs], *[hbm(lax.empty(s, d)) for s, d in land_shapes], after)
    return out[0], out[1], list(out[2:2 + n]), list(out[2 + n:2 + 2 * n]), out[-1]


def _gather_wait(send, recv, srcs, lands, after, name):
    n = len(srcs)

    def body(*refs):
        ins, lands_ = refs[:n], refs[n:2 * n]
        send_, recv_ = refs[2 * n], refs[2 * n + 1]
        x, y, c, chips = _place()
        for a in range(n):
            rows = _half_rows(srcs[a].shape[0], c)
            for k, (px, py) in enumerate(chips):
                cp = pltpu.make_async_remote_copy(src_ref=ins[a].at[rows], dst_ref=lands_[a].at[2 * px + py, rows], send_sem=send_.at[3 * a + k],
                                                  recv_sem=recv_.at[3 * a + k], device_id=(px, py, c), device_id_type=MESH)
                cp.wait_send()
                cp.wait_recv()

    out = pl.pallas_call(
        body, name=name, out_shape=[pltpu.HBM(v.shape, v.dtype) for v in list(srcs) + list(lands)],
        in_specs=[HBM] * (2 * n) + [SEM, SEM, ANY], out_specs=[HBM] * (2 * n), input_output_aliases={i: i for i in range(2 * n)},
        compiler_params=pltpu.CompilerParams(has_side_effects=DATAFLOW),
    )(*srcs, *lands, send, recv, after)
    return list(out[n:])


def _pair_forward(lands, name):
    n = len(lands)

    def body(*refs):
        ins, outs = refs[:n], refs[n:2 * n]
        send, recv = refs[2 * n:]
        x, y, c, chips = _place()
        copies = []
        for a in range(n):
            r = lands[a].shape[1]
            for k, (px, py) in enumerate(chips):
                cp = pltpu.make_async_remote_copy(
                    src_ref=ins[a].at[2 * px + py, _half_rows(r, c)], dst_ref=outs[a].at[2 * px + py, _half_rows(r, c)],
                    send_sem=send.at[a, k], recv_sem=recv.at[a, k], device_id=(x, y, 1 - c), device_id_type=MESH)
                cp.start()
                copies.append(cp)
        for a in range(n):
            r = lands[a].shape[1]
            for k, (px, py) in enumerate(chips):
                pltpu.make_async_remote_copy(
                    src_ref=ins[a].at[2 * px + py, _half_rows(r, c)], dst_ref=outs[a].at[2 * px + py, _half_rows(r, 1 - c)],
                    send_sem=send.at[a, k], recv_sem=recv.at[a, k], device_id=(x, y, 1 - c), device_id_type=MESH).wait_recv()
        for cp in copies:
            cp.wait_send()

    return pl.pallas_call(body, name=name, in_specs=[ANY] * n, out_specs=[ANY] * n, out_shape=[_sds(v.shape, v.dtype) for v in lands],
                          scratch_shapes=[pltpu.SemaphoreType.DMA((n, 3)), pltpu.SemaphoreType.DMA((n, 3))],
                          input_output_aliases={i: i for i in range(n)})(*lands)


def _pair_exchange(arrs, name):
    n = len(arrs)

    def body(*refs):
        ins, outs = refs[:n], refs[n:2 * n]
        send, recv = refs[2 * n:]
        x, y, c, _ = _place()
        copies = []
        for a in range(n):
            h = arrs[a].shape[1] // 2
            cp = pltpu.make_async_remote_copy(src_ref=ins[a].at[:, pl.ds(pl.multiple_of((1 - c) * h, h), h)], dst_ref=outs[a],
                                              send_sem=send.at[a], recv_sem=recv.at[a], device_id=(x, y, 1 - c), device_id_type=MESH)
            cp.start()
            copies.append(cp)
        for cp in copies:
            cp.wait()

    return _comm_call(body, name, arrs, [_sds((a.shape[0], a.shape[1] // 2, a.shape[2]), a.dtype) for a in arrs],
                      [pltpu.SemaphoreType.DMA((n,)), pltpu.SemaphoreType.DMA((n,))])


def _chip_exchange(arrs, name):
    n = len(arrs)

    def body(*refs):
        ins, outs = refs[:n], refs[n:2 * n]
        send, recv = refs[2 * n:]
        x, y, c, chips = _place()
        me = 2 * x + y
        copies = []
        for a in range(n):
            for k, (px, py) in enumerate(chips):
                r = pltpu.make_async_remote_copy(src_ref=ins[a].at[2 * px + py], dst_ref=outs[a].at[me], send_sem=send.at[a, k],
                                                 recv_sem=recv.at[a, k], device_id=(px, py, c), device_id_type=MESH)
                r.start()
                copies.append(r)
        for cp in copies:
            cp.wait()

    return _comm_call(body, name, arrs, [_sds(a.shape, a.dtype) for a in arrs],
                      [pltpu.SemaphoreType.DMA((n, 3)), pltpu.SemaphoreType.DMA((n, 3))])


def _chip_exchange_start(arrs, name):
    n = len(arrs)

    def body(*refs):
        ins, lands = refs[:n], refs[n:2 * n]
        send, recv, token = refs[2 * n], refs[2 * n + 1], refs[-1]
        x, y, c, chips = _place()
        me = 2 * x + y
        for a in range(n):
            for k, (px, py) in enumerate(chips):
                pltpu.make_async_remote_copy(src_ref=ins[a].at[2 * px + py], dst_ref=lands[a].at[me], send_sem=send.at[3 * a + k],
                                             recv_sem=recv.at[3 * a + k], device_id=(px, py, c), device_id_type=MESH).start()
        token[...] = jnp.zeros_like(token)

    hbm = lambda v: pltpu.with_memory_space_constraint(v, pltpu.HBM)
    out = pl.pallas_call(
        body, name=name,
        out_shape=(pltpu.SemaphoreType.DMA((3 * n,)), pltpu.SemaphoreType.DMA((3 * n,)), *[pltpu.HBM(a.shape, a.dtype) for a in arrs],
                   *[pltpu.HBM(a.shape, a.dtype) for a in arrs], _sds((8, HD))),
        in_specs=[HBM] * (2 * n), out_specs=(SEM, SEM, *[HBM] * (2 * n), pl.BlockSpec(memory_space=pltpu.VMEM)),
        input_output_aliases={i: 2 + i for i in range(2 * n)},
        compiler_params=pltpu.CompilerParams(has_side_effects=DATAFLOW),
    )(*[hbm(a) for a in arrs], *[hbm(lax.empty(a.shape, a.dtype)) for a in arrs])
    return out[0], out[1], list(out[2:2 + n]), list(out[2 + n:2 + 2 * n]), out[-1]


def _chip_exchange_wait(send, recv, srcs, lands, after, name):
    n = len(srcs)

    def body(*refs):
        ins, lands_ = refs[:n], refs[n:2 * n]
        send_, recv_ = refs[2 * n], refs[2 * n + 1]
        x, y, c, chips = _place()
        for a in range(n):
            for k, (px, py) in enumerate(chips):
                cp = pltpu.make_async_remote_copy(src_ref=ins[a].at[2 * px + py], dst_ref=lands_[a].at[2 * px + py], send_sem=send_.at[3 * a + k],
                                                  recv_sem=recv_.at[3 * a + k], device_id=(px, py, c), device_id_type=MESH)
                cp.wait_send()
                cp.wait_recv()

    out = pl.pallas_call(
        body, name=name, out_shape=[pltpu.HBM(v.shape, v.dtype) for v in list(srcs) + list(lands)],
        in_specs=[HBM] * (2 * n) + [SEM, SEM, ANY], out_specs=[HBM] * (2 * n), input_output_aliases={i: i for i in range(2 * n)},
        compiler_params=pltpu.CompilerParams(has_side_effects=DATAFLOW),
    )(*srcs, *lands, send, recv, after)
    return list(out[n:])


def _pair_swap(arrs, name):
    n = len(arrs)

    def body(*refs):
        ins, outs = refs[:n], refs[n:2 * n]
        send, recv = refs[2 * n:]
        x, y, c, _ = _place()
        copies = []
        for a in range(n):
            cp = pltpu.make_async_remote_copy(src_ref=ins[a], dst_ref=outs[a], send_sem=send.at[a], recv_sem=recv.at[a],
                                              device_id=(x, y, 1 - c), device_id_type=MESH)
            cp.start()
            copies.append(cp)
        for cp in copies:
            cp.wait()

    return _comm_call(body, name, arrs, [_sds(a.shape, a.dtype) for a in arrs],
                      [pltpu.SemaphoreType.DMA((n,)), pltpu.SemaphoreType.DMA((n,))])


def _all_gather_devices(v, name):
    def body(v_ref, o_ref, send, recv, loc):
        x, y, c, _ = _place()
        me = 4 * x + 2 * y + c
        own = pltpu.make_async_copy(v_ref, o_ref.at[me], loc)
        own.start()
        copies = [own]
        for k in range(1, 8):
            fx, fy, fc = (k >> 2) & 1, (k >> 1) & 1, k & 1
            peer = (x ^ fx, y ^ fy, c ^ fc)
            r = pltpu.make_async_remote_copy(src_ref=v_ref, dst_ref=o_ref.at[me], send_sem=send.at[k - 1],
                                             recv_sem=recv.at[k - 1], device_id=peer, device_id_type=MESH)
            r.start()
            copies.append(r)
        for cp in copies:
            cp.wait()

    return _comm_call(body, name, [v], [_sds((8,) + v.shape, v.dtype)],
                      [pltpu.SemaphoreType.DMA((7,)), pltpu.SemaphoreType.DMA((7,)), pltpu.SemaphoreType.DMA])[0]


def _row_tile(r):
    return next((b for b in (512, 384, 256, 128, 64, 32, 16) if r % b == 0), r)


def _add2(a, b, out_dtype, name):
    r, w = a.shape
    br = _row_tile(r)

    def body(a_ref, b_ref, o_ref):
        o_ref[...] = (a_ref[...].astype(F32) + b_ref[...].astype(F32)).astype(out_dtype)

    blk = pl.BlockSpec((br, w), lambda i: (i, 0))
    return _pc(body, name, (r // br,), [blk, blk], blk, _sds((r, w), out_dtype))(a, b)


def _sum_slots(a, out_dtype, name, extra=None):
    n, r, w = a.shape
    br = _row_tile(r)

    def body(*refs):
        a_ref, o_ref = refs[0], refs[-1]
        acc = a_ref[0].astype(F32)
        for s in range(1, n):
            acc = acc + a_ref[s].astype(F32)
        if extra is not None:
            acc = acc + refs[1][...].astype(F32)
        o_ref[...] = acc.astype(out_dtype)

    ins = [a] + ([extra] if extra is not None else [])
    specs = [pl.BlockSpec((n, br, w), lambda i: (0, i, 0))] + ([pl.BlockSpec((br, w), lambda i: (i, 0))] if extra is not None else [])
    return _pc(body, name, (r // br,), specs, pl.BlockSpec((br, w), lambda i: (i, 0)), _sds((r, w), out_dtype))(*ins)


SMALL = ["norm_mix_g", "norm_xattn_g", "norm_mlp_g", "final_norm_g", "mem_norm_g", "hgrn_lb_logits", "mlstm_norm_g",
         "hgrn_norm_g", "c_qnorm_g", "c_knorm_g", "ab_gate_b", "c_fgate_b"]
SMALL_ROWS = 16


def _pack_small(parts):
    flat = jnp.concatenate([p.reshape(-1).astype(F32) for p in parts])
    return jnp.pad(flat, (0, SMALL_ROWS * D - flat.shape[0])).reshape(SMALL_ROWS, D)


def _unpack_small(buf, shapes):
    flat, out, off = buf.reshape(-1), [], 0
    for s in shapes:
        n = 1
        for d in s:
            n *= d
        out.append(flat[off:off + n].reshape(s))
        off += n
    return out


def kernel(x, mem, norm_mix_g, norm_xattn_g, norm_mlp_g, final_norm_g, ab_w_in, ab_conv_w, ab_gate_b, hgrn_lb_logits, mlstm_norm_g, hgrn_norm_g, ab_w_out, c_w_in, c_fgate_b, c_qnorm_g, c_knorm_g, c_w_out, mem_norm_g, mem_w_kv, xa_w_q, xa_w_o, mlp_w1, mlp_w2, loss_target, m_norm_mix_g, m_norm_xattn_g, m_norm_mlp_g, m_final_norm_g, m_ab_w_in, m_ab_conv_w, m_ab_gate_b, m_hgrn_lb_logits, m_mlstm_norm_g, m_hgrn_norm_g, m_ab_w_out, m_c_w_in, m_c_fgate_b, m_c_qnorm_g, m_c_knorm_g, m_c_w_out, m_mem_norm_g, m_mem_w_kv, m_xa_w_q, m_xa_w_o, m_mlp_w1, m_mlp_w2, v_norm_mix_g, v_norm_xattn_g, v_norm_mlp_g, v_final_norm_g, v_ab_w_in, v_ab_conv_w, v_ab_gate_b, v_hgrn_lb_logits, v_mlstm_norm_g, v_hgrn_norm_g, v_ab_w_out, v_c_w_in, v_c_fgate_b, v_c_qnorm_g, v_c_knorm_g, v_c_w_out, v_mem_norm_g, v_mem_w_kv, v_xa_w_q, v_xa_w_o, v_mlp_w1, v_mlp_w2):
    A = dict(locals())
    chip = 2 * lax.axis_index("x") + lax.axis_index("y")

    big = ["ab_w_in", "c_w_in", "ab_w_out", "c_w_out", "mem_w_kv", "xa_w_q", "xa_w_o", "mlp_w1", "mlp_w2"]
    shard2d = {"ab_w_in": (D, 1026), "c_w_in": (D, 1026), "ab_w_out": (256, D), "c_w_out": (256, D), "mem_w_kv": (D, 512),
               "xa_w_q": (512, D), "xa_w_o": (512, D), "mlp_w1": (2 * D, D), "mlp_w2": (2 * D, D)}
    shard16 = lambda n: A[n].reshape(shard2d[n]).astype(BF16)
    own_slot = lambda gs, os: [lax.dynamic_update_index_in_dim(g, o, chip, 0) for g, o in zip(gs, os)]
    cols = lambda g: jnp.concatenate([g[k] for k in range(NCHIP)], axis=1)
    per_layer = lambda g: g.reshape(NCHIP, 2, -1, D).transpose(1, 0, 2, 3)
    first = [shard16("ab_w_in"), jnp.pad(ab_conv_w[0], ((0, 16 - CONV_W), (0, 0)))]
    g_in0, g_conv = own_slot(_gather_weights(first, "gather_first"), first)
    W = dict(w_in0=_pack_w_in0(cols(g_in0)))
    rest_names = ["c_w_in", "ab_w_out", "c_w_out", "xa_w_q", "xa_w_o", "mlp_w1", "mlp_w2", "mem_w_kv"]
    rest = [shard16(n) for n in rest_names]
    send_s, recv_s, srcs, lands, token = _gather_start(rest, g_conv, "gather_rest_start")

    def late_weights(after):
        got = _pair_forward(_gather_wait(send_s, recv_s, srcs, lands, after, "gather_rest_wait"), "gather_rest_forward")
        gw = dict(zip(rest_names, own_slot(got, rest)))
        return dict(w_in1=_pack_w_in1(cols(gw["c_w_in"])), w_out0=gw["ab_w_out"].reshape(D, D), w_out1=gw["c_w_out"].reshape(D, D),
                    wkv_s=gw["mem_w_kv"],
                    wq=per_layer(gw["xa_w_q"]).reshape(2, D, D), wo=per_layer(gw["xa_w_o"]).reshape(2, D, D),
                    w1s=gw["mlp_w1"].reshape(NCHIP, 2, D, D), w2=gw["mlp_w2"].reshape(NCHIP, 2, D, D))

    S = dict(norm_mix_g=norm_mix_g + token[0, 0], norm_xattn_g=norm_xattn_g, norm_mlp_g=norm_mlp_g, final_norm_g=final_norm_g,
             conv_w=cols(g_conv[:, :CONV_W]), gate_b=ab_gate_b, lb_logits=hgrn_lb_logits, mlstm_norm_g=mlstm_norm_g,
             hgrn_norm_g=hgrn_norm_g, c_fgate_b=c_fgate_b, c_qnorm_g=c_qnorm_g, c_knorm_g=c_knorm_g, mem_norm_g=mem_norm_g)

    core = lax.axis_index("c")
    by_rows = lambda g: g.reshape(NCHIP, -1, D)

    def stack_cols(g):
        return jnp.stack([g[:, 1026 * k:1026 * (k + 1)] for k in range(NCHIP)])

    def pair_sums(arrs, tag):
        theirs = _pair_exchange(arrs, f"pair_exchange_{tag}")
        out = []
        for i, (a, th) in enumerate(zip(arrs, theirs)):
            h = a.shape[1] // 2
            mine = lax.dynamic_slice_in_dim(a, core * h, h, axis=1)
            out.append(_add2(mine.reshape(-1, a.shape[2]), th.reshape(-1, a.shape[2]), BF16, f"pair_sum_{tag}{i}").reshape(th.shape))
        return out

    def chip_sums(psums, from_chips, tag):
        out = []
        for i, (f, p) in enumerate(zip(from_chips, psums)):
            f = lax.dynamic_update_index_in_dim(f, lax.dynamic_index_in_dim(p, chip, 0, keepdims=False), chip, 0)
            out.append(_sum_slots(f, F32, f"chip_sum_{tag}{i}"))
        return out

    started = {}

    def grads_hook(stage, g):
        if stage == "layer1":
            arrs = [jnp.concatenate([by_rows(g["w_out"]), by_rows(g["wq"]), by_rows(g["wo"]), g["w1"], by_rows(g["w2"])], axis=1),
                    stack_cols(_unpack_w_in1(g["w_in"]))]
        else:
            arrs = [jnp.concatenate([by_rows(g["wq"]), by_rows(g["wo"]), g["w1"], by_rows(g["w2"])], axis=1), g["wkv"]]
        psums = pair_sums(arrs, stage)
        *handles, token = _chip_exchange_start(psums, f"chip_exchange_start_{stage}")
        started[stage] = (psums, handles)
        return token[0, 0]

    lossp, dx, G = _local_step(x[0], mem[0], loss_target[0], W, S, late_weights, grads_hook)

    gsmall = {"norm_mix_g": G["norm_mix_g"], "norm_xattn_g": G["norm_xattn_g"], "norm_mlp_g": G["norm_mlp_g"],
              "final_norm_g": G["final_norm_g"], "mem_norm_g": G["mem_norm_g"], "hgrn_lb_logits": G["lb_logits"],
              "mlstm_norm_g": G["mlstm_norm_g"], "hgrn_norm_g": G["hgrn_norm_g"], "c_qnorm_g": G["c_qnorm_g"],
              "c_knorm_g": G["c_knorm_g"], "ab_gate_b": G["gate_b"], "c_fgate_b": G["c_fgate_b"]}
    packed = _pack_small([gsmall[n] for n in SMALL] + [G["conv_w"], lossp])
    red = _sum_slots(_all_gather_devices(packed, "gather_small"), F32, "sum_small")
    small_shapes = [A[n].shape for n in SMALL]
    *gs, gconv, loss = _unpack_small(red, small_shapes + [(CONV_W, D), ()])
    gs = dict(zip(SMALL, gs))
    gconv = lax.dynamic_slice_in_dim(gconv, chip * 256, 256, axis=1)[None]

    last = pair_sums([by_rows(G["w_out0"]), stack_cols(_unpack_w_in0(G["w_in0"]))], "last")
    rhalf = chip_sums(last, _chip_exchange(last, "chip_exchange_last"), "last")
    for stage in ("layer1", "layer0_mlp_xattn"):
        psums, handles = started[stage]
        rhalf += chip_sums(psums, _chip_exchange_wait(*handles, dx, f"chip_exchange_wait_{stage}"), stage)
    other = _pair_swap(rhalf, "pair_swap")
    r_out0, r_in0, r_l1, r_in1, r_l0, r_kv = [
        jnp.where(core == 0, jnp.concatenate([m_, o_], axis=0), jnp.concatenate([o_, m_], axis=0)) for m_, o_ in zip(rhalf, other)]
    gbig = {"ab_w_in": r_in0, "c_w_in": r_in1, "mem_w_kv": r_kv, "ab_w_out": r_out0, "c_w_out": r_l1[0:256],
            "xa_w_q": jnp.concatenate([r_l0[0:256], r_l1[256:512]], axis=0),
            "xa_w_o": jnp.concatenate([r_l0[256:512], r_l1[512:768]], axis=0),
            "mlp_w1": jnp.concatenate([r_l0[512:1536], r_l1[768:1792]], axis=0),
            "mlp_w2": jnp.concatenate([r_l0[1536:2560], r_l1[1792:2816]], axis=0)}

    out_g, out_d, out_m, out_v = {}, {}, {}, {}
    for n in big:
        d_, m_, v_ = _adam(A[n].reshape(shard2d[n]), gbig[n], A["m_" + n].reshape(shard2d[n]), A["v_" + n].reshape(shard2d[n]), "adam_" + n)
        out_g[n] = gbig[n].reshape(A[n].shape)
        out_d[n], out_m[n], out_v[n] = d_.reshape(A[n].shape), m_.reshape(A[n].shape), v_.reshape(A[n].shape)
    sd, sm, sv = _adam(_pack_small([A[n] for n in SMALL]), _pack_small([gs[n] for n in SMALL]),
                       _pack_small([A["m_" + n] for n in SMALL]), _pack_small([A["v_" + n] for n in SMALL]), "adam_small")
    for n, d_, m_, v_ in zip(SMALL, _unpack_small(sd, small_shapes), _unpack_small(sm, small_shapes), _unpack_small(sv, small_shapes)):
        out_g[n], out_d[n], out_m[n], out_v[n] = gs[n], d_, m_, v_
    cd, cm_, cv = _adam(ab_conv_w[0], gconv[0], m_ab_conv_w[0], v_ab_conv_w[0], "adam_conv")
    out_g["ab_conv_w"], out_d["ab_conv_w"], out_m["ab_conv_w"], out_v["ab_conv_w"] = gconv, cd[None], cm_[None], cv[None]

    order = ["norm_mix_g", "norm_xattn_g", "norm_mlp_g", "final_norm_g", "ab_w_in", "ab_conv_w", "ab_gate_b", "hgrn_lb_logits",
             "mlstm_norm_g", "hgrn_norm_g", "ab_w_out", "c_w_in", "c_fgate_b", "c_qnorm_g", "c_knorm_g", "c_w_out", "mem_norm_g",
             "mem_w_kv", "xa_w_q", "xa_w_o", "mlp_w1", "mlp_w2"]
    return (loss, dx[None], *[out_g[n] for n in order], *[out_d[n] for n in order], *[out_m[n] for n in order],
            *[out_v[n] for n in order])
```

```python
import functools

import jax
import jax.numpy as jnp
from jax import lax
from jax.experimental import pallas as pl
from jax.experimental.pallas import tpu as pltpu

F32 = jnp.float32
BF16 = jnp.bfloat16
EPS = 1e-6
D = 1024
CHUNK = 64
REC_CHUNKS = 4
HD = 128
XD = 256
NEG = -1e30
VMEM_LIMIT_V7X = 56 * 1024 * 1024
ADAM_LR, ADAM_B1, ADAM_B2, ADAM_EPS, ADAM_WD, ADAM_STEP = 0.001, 0.9, 0.999, 1e-08, 0.01, 10
MESH = pl.DeviceIdType.MESH


def _pc(body, name, grid, in_specs, out_specs, out_shape, scratch=(), **kw):
    return pl.pallas_call(
        body, name=name, grid=grid, in_specs=in_specs, out_specs=out_specs, out_shape=out_shape,
        scratch_shapes=scratch,
        compiler_params=pltpu.CompilerParams(
            dimension_semantics=("arbitrary",) * len(grid), vmem_limit_bytes=VMEM_LIMIT_V7X), **kw)


def _sds(shape, dtype=F32):
    return jax.ShapeDtypeStruct(shape, dtype)


def _blk(n, target):
    return max(b for b in range(128, max(target, 128) + 1, 128) if n % b == 0)


def _dot(a, b, dims):
    return lax.dot_general(a, b, (dims, ((), ())), preferred_element_type=F32)


def _nn(a, b):
    return _dot(a, b, ((1,), (0,)))


def _nt(a, b):
    return _dot(a, b, ((1,), (1,)))


def _tn(a, b):
    return _dot(a, b, ((0,), (0,)))


def _sigmoid(x):
    return 1.0 / (1.0 + jnp.exp(-x))


def _log_sigmoid(x):
    return jnp.minimum(x, 0.0) - jnp.log(1.0 + jnp.exp(-jnp.abs(x)))


def _rstd(x):
    return lax.rsqrt(jnp.mean(x * x, axis=-1, keepdims=True) + EPS)


def _rms_bwd(du, x, g):
    r = _rstd(x)
    xh = x * r
    dxh = du * g
    dx = r * (dxh - xh * jnp.mean(dxh * xh, axis=-1, keepdims=True))
    return dx, du * xh


def _norm_mm(h, g, w, name, bm=1024, bn=512):
    t, n = h.shape[0], w.shape[1]
    bm, bn = min(bm, t), _blk(n, 3 * bn)

    def body(h_ref, g_ref, w_ref, z_ref, u_ref):
        @pl.when(pl.program_id(1) == 0)
        def _():
            x = h_ref[...]
            u_ref[...] = (x * _rstd(x) * g_ref[...]).astype(BF16)
        z_ref[...] = _nn(u_ref[...], w_ref[...])

    return _pc(body, name, (t // bm, n // bn),
               [pl.BlockSpec((bm, D), lambda i, j: (i, 0)), pl.BlockSpec((1, D), lambda i, j: (0, 0)),
                pl.BlockSpec((D, bn), lambda i, j: (0, j))],
               [pl.BlockSpec((bm, bn), lambda i, j: (i, j)), pl.BlockSpec((bm, D), lambda i, j: (i, 0))],
               [_sds((t, n)), _sds((t, D), BF16)])(h, g, w)


def _mm_tn(a, b, name, bm=1024, bn=1024, bt=4096, col_chips=None):
    t, m = a.shape
    n = b.shape[1]
    bm, bn, bt = _blk(m, bm), (n // col_chips if col_chips else _blk(n, bn + bn // 2)), min(bt, t)
    if (m // bm) * (n // bn) == 1 and bt >= 1024:
        bt //= 4
    nt = t // bt

    def body(a_ref, b_ref, o_ref, acc):
        k = pl.program_id(2)

        @pl.when(k == 0)
        def _():
            acc[...] = jnp.zeros_like(acc)

        acc[...] += _tn(a_ref[...].astype(BF16), b_ref[...].astype(BF16))

        @pl.when(k == nt - 1)
        def _():
            o_ref[...] = acc[...].astype(BF16)

    if col_chips:
        out_spec, out_shape = pl.BlockSpec((None, bm, bn), lambda i, j, k: (j, i, 0)), _sds((col_chips, m, bn), BF16)
    else:
        out_spec, out_shape = pl.BlockSpec((bm, bn), lambda i, j, k: (i, j)), _sds((m, n), BF16)
    return _pc(body, name, (m // bm, n // bn, nt),
               [pl.BlockSpec((bt, bm), lambda i, j, k: (k, i)), pl.BlockSpec((bt, bn), lambda i, j, k: (k, j))],
               out_spec, out_shape, scratch=[pltpu.VMEM((bm, bn), F32)])(a, b)


def _bwd_in(dz, w, h, g, dh, name, bm=1024, bk=1024):
    t, n = dz.shape
    if n > 2 * bk:
        bm, bk = min(bm // 2, t), n
    else:
        bm, bk = min(bm, t), _blk(n, bk + bk // 2)
    nk = n // bk

    def body(dz_ref, w_ref, h_ref, g_ref, dh_ref, o_ref, dg_ref, acc):
        i, k = pl.program_id(0), pl.program_id(1)

        @pl.when(k == 0)
        def _():
            acc[...] = jnp.zeros_like(acc)

        @pl.when((i == 0) & (k == 0))
        def _():
            dg_ref[...] = jnp.zeros_like(dg_ref)

        acc[...] += _nt(dz_ref[...], w_ref[...])

        @pl.when(k == nk - 1)
        def _():
            dx, dgr = _rms_bwd(acc[...], h_ref[...], g_ref[...])
            o_ref[...] = dh_ref[...] + dx
            dg_ref[...] += jnp.sum(dgr, axis=0, keepdims=True)

    return _pc(body, name, (t // bm, nk),
               [pl.BlockSpec((bm, bk), lambda i, k: (i, k)), pl.BlockSpec((D, bk), lambda i, k: (0, k)),
                pl.BlockSpec((bm, D), lambda i, k: (i, 0)), pl.BlockSpec((1, D), lambda i, k: (0, 0)),
                pl.BlockSpec((bm, D), lambda i, k: (i, 0))],
               [pl.BlockSpec((bm, D), lambda i, k: (i, 0)), pl.BlockSpec((1, D), lambda i, k: (0, 0))],
               [_sds((t, D)), _sds((1, D))], scratch=[pltpu.VMEM((bm, D), F32)])(dz, w, h, g, dh)


def _mlp_fwd(h, g, w1s, w2, l, name, bm=1024):
    t = h.shape[0]
    bm = min(bm, t)
    nk = w1s.shape[0]

    def body(h_ref, g_ref, w1_ref, w2_ref, o_ref, a_ref, u_ref, acc):
        k = pl.program_id(1)

        @pl.when(k == 0)
        def _():
            x = h_ref[...]
            u_ref[...] = (x * _rstd(x) * g_ref[...]).astype(BF16)
            acc[...] = jnp.zeros_like(acc)

        a = _nn(u_ref[...], w1_ref[...])
        a_ref[...] = a
        r = jnp.square(jnp.maximum(a, 0.0)).astype(BF16)
        acc[...] += _nn(r, w2_ref[...])

        @pl.when(k == nk - 1)
        def _():
            o_ref[...] = h_ref[...] + acc[...]

    return _pc(body, name, (t // bm, nk),
               [pl.BlockSpec((bm, D), lambda i, k: (i, 0)), pl.BlockSpec((1, D), lambda i, k: (0, 0)),
                pl.BlockSpec((None, None, D, D), lambda i, k: (k, l, 0, 0)), pl.BlockSpec((None, None, D, D), lambda i, k: (k, l, 0, 0))],
               [pl.BlockSpec((bm, D), lambda i, k: (i, 0)), pl.BlockSpec((bm, D), lambda i, k: (i, k)),
                pl.BlockSpec((bm, D), lambda i, k: (i, 0))],
               [_sds((t, D)), _sds((t, nk * D)), _sds((t, D), BF16)],
               scratch=[pltpu.VMEM((bm, D), F32)])(h, g, w1s, w2)


def _mlp_bwd(dh, a, w1s, w2, l, h, g, name, bm=512):
    t = h.shape[0]
    bm = min(bm, t)
    nk = w1s.shape[0]

    def body(dh_ref, a_ref, w1_ref, w2_ref, h_ref, g_ref, o_ref, da_ref, r_ref, dg_ref, acc):
        i, k = pl.program_id(0), pl.program_id(1)

        @pl.when(k == 0)
        def _():
            acc[...] = jnp.zeros_like(acc)

        @pl.when((i == 0) & (k == 0))
        def _():
            dg_ref[...] = jnp.zeros_like(dg_ref)

        ap = jnp.maximum(a_ref[...], 0.0)
        r_ref[...] = jnp.square(ap).astype(BF16)
        dr = _nt(dh_ref[...].astype(BF16), w2_ref[...])
        da = (dr * (2.0 * ap)).astype(BF16)
        da_ref[...] = da
        acc[...] += _nt(da, w1_ref[...])

        @pl.when(k == nk - 1)
        def _():
            dx, dgr = _rms_bwd(acc[...], h_ref[...], g_ref[...])
            o_ref[...] = dh_ref[...] + dx
            dg_ref[...] += jnp.sum(dgr, axis=0, keepdims=True)

    return _pc(body, name, (t // bm, nk),
               [pl.BlockSpec((bm, D), lambda i, k: (i, 0)), pl.BlockSpec((bm, D), lambda i, k: (i, k)),
                pl.BlockSpec((None, None, D, D), lambda i, k: (k, l, 0, 0)), pl.BlockSpec((None, None, D, D), lambda i, k: (k, l, 0, 0)),
                pl.BlockSpec((bm, D), lambda i, k: (i, 0)), pl.BlockSpec((1, D), lambda i, k: (0, 0))],
               [pl.BlockSpec((bm, D), lambda i, k: (i, 0)), pl.BlockSpec((bm, D), lambda i, k: (i, k)),
                pl.BlockSpec((bm, D), lambda i, k: (i, k)), pl.BlockSpec((1, D), lambda i, k: (0, 0))],
               [_sds((t, D)), _sds((t, nk * D), BF16), _sds((t, nk * D), BF16), _sds((1, D))],
               scratch=[pltpu.VMEM((bm, D), F32)])(dh, a, w1s, w2, h, g)


def _rows_of(x):
    return lax.broadcasted_iota(jnp.int32, x.shape, 0)


def _shift_down(x, s):
    if s == 0:
        return x
    return jnp.where(_rows_of(x) >= s, pltpu.roll(x, s, 0), 0.0)


def _shift_up(x, s):
    if s == 0:
        return x
    n = x.shape[0]
    return jnp.where(_rows_of(x) < n - s, pltpu.roll(x, n - s, 0), 0.0)


def _cumsum_rows(x):
    n, s = x.shape[0], 1
    while s < n:
        x = x + _shift_down(x, s)
        s *= 2
    return x


def _rcumsum_rows(x):
    n, s = x.shape[0], 1
    while s < n:
        x = x + _shift_up(x, s)
        s *= 2
    return x


def _silu(x):
    return x * _sigmoid(x)


def _dsilu(x):
    s = _sigmoid(x)
    return s * (1.0 + x * (1.0 - s))


CONV_W = 4


def _conv_pre(u, w):
    y = _shift_down(u, CONV_W - 1) * w[0:1, :]
    for j in range(1, CONV_W):
        y = y + _shift_down(u, CONV_W - 1 - j) * w[j:j + 1, :]
    return y


def _conv_fwd(z0, cw, name):
    t = z0.shape[0]

    def body(u_ref, w_ref, o_ref):
        o_ref[...] = _silu(_conv_pre(u_ref[...], w_ref[...]))

    return _pc(body, name, (2 * 512 // HD,),
               [pl.BlockSpec((t, HD), lambda c: (0, c)), pl.BlockSpec((CONV_W, HD), lambda c: (0, c))],
               pl.BlockSpec((t, HD), lambda c: (0, c)), _sds((t, 1024)))(z0, cw)


def _conv_bwd(z0, cw, dy, name):
    t = z0.shape[0]

    def body(u_ref, w_ref, dy_ref, du_ref, dw_ref):
        u, w = u_ref[...], w_ref[...]
        dpre = dy_ref[...] * _dsilu(_conv_pre(u, w))
        du = _shift_up(dpre, CONV_W - 1) * w[0:1, :]
        for j in range(1, CONV_W):
            du = du + _shift_up(dpre, CONV_W - 1 - j) * w[j:j + 1, :]
        du_ref[...] = du.astype(BF16)
        for j in range(CONV_W):
            dw_ref[j:j + 1, :] = jnp.sum(dpre * _shift_down(u, CONV_W - 1 - j), axis=0, keepdims=True)

    return _pc(body, name, (2 * 512 // HD,),
               [pl.BlockSpec((t, HD), lambda c: (0, c)), pl.BlockSpec((CONV_W, HD), lambda c: (0, c)),
                pl.BlockSpec((t, HD), lambda c: (0, c))],
               [pl.BlockSpec((t, HD), lambda c: (0, c)), pl.BlockSpec((CONV_W, HD), lambda c: (0, c))],
               [_sds((t, 1024), BF16), _sds((CONV_W, 1024))])(z0, cw, dy)


def _mlstm_gates(gate, bias, m_in):
    L = gate.shape[0]
    r = lax.broadcasted_iota(jnp.int32, (L, L), 0)
    c = lax.broadcasted_iota(jnp.int32, (L, L), 1)
    eye, tril = r == c, c <= r
    i_col = gate[:, 0:1] + bias[:, 0:1]
    f_col = gate[:, 1:2] + bias[:, 1:2]
    logf_col = _log_sigmoid(f_col)
    logf_row = jnp.sum(jnp.where(eye, logf_col, 0.0), axis=0, keepdims=True)
    i_row = jnp.sum(jnp.where(eye, i_col, 0.0), axis=0, keepdims=True)
    b_col = jnp.sum(jnp.where(tril, logf_row, 0.0), axis=1, keepdims=True)
    b_row = jnp.sum(jnp.where(r <= c, logf_col, 0.0), axis=0, keepdims=True)
    logd = jnp.where(tril, b_col - b_row + i_row, NEG)
    inter = b_col + m_in
    m_t = jnp.maximum(inter, jnp.max(logd, axis=1, keepdims=True))
    w_t = jnp.exp(inter - m_t)
    dm = jnp.exp(logd - m_t)
    b_last = b_col[L - 1:L, :]
    log_in = b_last - b_col + i_col
    m_new = jnp.maximum(b_last + m_in, jnp.max(log_in, axis=0, keepdims=True))
    w_col = jnp.exp(log_in - m_new)
    decay = jnp.exp(b_last + m_in - m_new)
    return dict(eye=eye, r=r, c=c, f_col=f_col, m_t=m_t, w_t=w_t, dm=dm, m_new=m_new, w_col=w_col, decay=decay)


def _mlstm_fwd(qk, z0, gates, bias, name):
    t = qk.shape[0]
    nc, nh, L = t // CHUNK, 4, CHUNK
    scale = HD ** -0.5

    def body(q_ref, k_ref, v_ref, g_ref, b_ref, h_ref, cs_ref, ns_ref, ms_ref, c_s, n_s, m_s):
        @pl.when(pl.program_id(0) == 0)
        def _():
            c_s[...] = jnp.zeros_like(c_s)
            n_s[...] = jnp.zeros_like(n_s)
            m_s[...] = jnp.zeros_like(m_s)

        for hd in range(nh):
            sl = slice(hd * HD, (hd + 1) * HD)
            cm, nv, m_in = c_s[hd], n_s[hd], m_s[hd]
            for ck in range(cps):
                rows = slice(ck * L, (ck + 1) * L)
                cs_ref[hd, ck] = cm
                ns_ref[hd, ck] = nv
                ms_ref[hd, ck] = jnp.broadcast_to(m_in, (1, HD))
                q, kh, v = q_ref[rows, sl], k_ref[rows, sl] * scale, v_ref[rows, sl]
                G = _mlstm_gates(g_ref[hd, rows, :], b_ref[hd], m_in)
                qb, kb, vb = q.astype(BF16), kh.astype(BF16), v.astype(BF16)
                sc = _nt(qb, kb) * G["dm"]
                num = _nn(sc.astype(BF16), vb) + G["w_t"] * _nn(qb, cm.astype(BF16))
                den = jnp.sum(sc, axis=1, keepdims=True) + G["w_t"] * jnp.sum(q * nv, axis=1, keepdims=True)
                h_ref[rows, sl] = num / jnp.maximum(jnp.abs(den), jnp.exp(-G["m_t"]))
                wk = G["w_col"] * kh
                cm = G["decay"] * cm + _tn(wk.astype(BF16), vb)
                nv = G["decay"] * nv + jnp.sum(wk, axis=0, keepdims=True)
                m_in = G["m_new"]
            c_s[hd], n_s[hd], m_s[hd] = cm, nv, m_in

    cps = REC_CHUNKS
    hspec = lambda blk: pl.BlockSpec((cps * L, 512), lambda j: (j, blk))
    st = lambda r: pl.BlockSpec((nh, cps, r, HD), lambda j: (0, j, 0, 0))
    return _pc(body, name, (nc // cps,),
               [hspec(0), hspec(1), hspec(2), pl.BlockSpec((nh, cps * L, 2), lambda j: (0, j, 0)),
                pl.BlockSpec((nh, 1, 2), lambda j: (0, 0, 0))],
               [hspec(0), st(HD), st(1), st(1)],
               [_sds((t, 512)), _sds((nh, nc, HD, HD)), _sds((nh, nc, 1, HD)), _sds((nh, nc, 1, HD))],
               scratch=[pltpu.VMEM((nh, HD, HD), F32), pltpu.VMEM((nh, 1, HD), F32), pltpu.VMEM((nh, 1, 1), F32)])(qk, qk, z0, gates, bias)


def _mlstm_bwd(qk, z0, gates, bias, cs, ns, ms, dh, name):
    t = qk.shape[0]
    nc, nh, L = t // CHUNK, 4, CHUNK
    scale = HD ** -0.5

    def body(q_ref, k_ref, v_ref, g_ref, b_ref, cs_ref, ns_ref, ms_ref, dh_ref, dq_ref, dk_ref, dv_ref, dg_ref, dc_s, dn_s):
        @pl.when(pl.program_id(0) == 0)
        def _():
            dc_s[...] = jnp.zeros_like(dc_s)
            dn_s[...] = jnp.zeros_like(dn_s)

        for ck in reversed(range(cps)):
            for hd in range(nh):
                one_head(hd, ck, slice(hd * HD, (hd + 1) * HD), slice(ck * L, (ck + 1) * L), q_ref, k_ref, v_ref, g_ref, b_ref,
                         cs_ref, ns_ref, ms_ref, dh_ref, dq_ref, dk_ref, dv_ref, dg_ref, dc_s, dn_s)

    def one_head(hd, ck, sl, rows, q_ref, k_ref, v_ref, g_ref, b_ref, cs_ref, ns_ref, ms_ref, dh_ref, dq_ref, dk_ref, dv_ref, dg_ref,
                 dc_s, dn_s):
        cm, nv, m_in = cs_ref[hd, ck], ns_ref[hd, ck], ms_ref[hd, ck][:, 0:1]
        q, kh, v = q_ref[rows, sl], k_ref[rows, sl] * scale, v_ref[rows, sl]
        G = _mlstm_gates(g_ref[hd, rows, :], b_ref[hd], m_in)
        w_t, dmat, w_col, decay = G["w_t"], G["dm"], G["w_col"], G["decay"]
        qb, kb, vb, cb = q.astype(BF16), kh.astype(BF16), v.astype(BF16), cm.astype(BF16)
        s = _nt(qb, kb)
        sc = s * dmat
        scb = sc.astype(BF16)
        qc = _nn(qb, cb)
        qn = jnp.sum(q * nv, axis=1, keepdims=True)
        num = _nn(scb, vb) + w_t * qc
        den = jnp.sum(sc, axis=1, keepdims=True) + w_t * qn
        e_m = jnp.exp(-G["m_t"])
        dnm = jnp.maximum(jnp.abs(den), e_m)
        dh_ = dh_ref[rows, sl]
        dnum = dh_ / dnm
        dden = jnp.where(jnp.abs(den) > e_m, -jnp.sum(dh_ * num, axis=1, keepdims=True) / (dnm * dnm) * jnp.sign(den), 0.0)
        dnumb = dnum.astype(BF16)
        dsc = _nt(dnumb, vb) + dden
        dv = _tn(scb, dnumb)
        wd = w_t * dnum
        wdb = wd.astype(BF16)
        ds = dsc * dmat
        dsb = ds.astype(BF16)
        dq = _nt(wdb, cb) + (w_t * dden) * nv + _nn(dsb, kb)
        dc_o = _tn(qb, wdb)
        dn_o = jnp.sum(q * (w_t * dden), axis=0, keepdims=True)
        dw = jnp.sum(dnum * qc, axis=1, keepdims=True) + dden * qn
        dkh = _tn(dsb, qb)
        dlogd = ds * s
        db_col = jnp.sum(dlogd, axis=1, keepdims=True) + dw * w_t
        csum = jnp.sum(dlogd, axis=0, keepdims=True)
        dcn, dnn = dc_s[hd], dn_s[hd]
        dcnb = dcn.astype(BF16)
        kdc = _nn(kb, dcnb)
        dws = jnp.sum(kdc * v, axis=1, keepdims=True) + jnp.sum(kh * dnn, axis=1, keepdims=True)
        dv = dv + w_col * kdc
        dkh = dkh + w_col * (_nt(vb, dcnb) + dnn)
        dlin = dws * w_col
        ddecay = jnp.sum(jnp.sum(dcn * cm, axis=1, keepdims=True), axis=0, keepdims=True) + jnp.sum(dnn * nv, axis=1, keepdims=True)
        dlast = ddecay * decay + jnp.sum(dlin, axis=0, keepdims=True)
        row_id = lax.broadcasted_iota(jnp.int32, (L, 1), 0)
        db_col = db_col - dlin + jnp.where(row_id == L - 1, dlast, 0.0)
        eye, r, c = G["eye"], G["r"], G["c"]
        di = dlin + jnp.sum(jnp.where(eye, csum, 0.0), axis=1, keepdims=True)
        db_row = jnp.sum(jnp.where(eye, db_col, 0.0), axis=0, keepdims=True) - csum
        dlogf = jnp.sum(jnp.where(c >= r, db_row, 0.0), axis=1, keepdims=True)
        dg_ref[hd, rows, 0:1] = di
        dg_ref[hd, rows, 1:2] = dlogf * (1.0 - _sigmoid(G["f_col"]))
        dq_ref[rows, sl] = dq
        dk_ref[rows, sl] = dkh * scale
        dv_ref[rows, sl] = dv
        dc_s[hd] = decay * dcn + dc_o
        dn_s[hd] = decay * dnn + dn_o

    cps = REC_CHUNKS
    rv = lambda j: nc // cps - 1 - j
    hspec = lambda blk: pl.BlockSpec((cps * L, 512), lambda j: (rv(j), blk))
    st = lambda r: pl.BlockSpec((nh, cps, r, HD), lambda j: (0, rv(j), 0, 0))
    gs = pl.BlockSpec((nh, cps * L, 2), lambda j: (0, rv(j), 0))
    return _pc(body, name, (nc // cps,),
               [hspec(0), hspec(1), hspec(2), gs, pl.BlockSpec((nh, 1, 2), lambda j: (0, 0, 0)),
                st(HD), st(1), st(1), hspec(0)],
               [hspec(0), hspec(0), hspec(0), gs],
               [_sds((t, 512)), _sds((t, 512)), _sds((t, 512)), _sds((nh, t, 2))],
               scratch=[pltpu.VMEM((nh, HD, HD), F32), pltpu.VMEM((nh, 1, HD), F32)])(qk, qk, z0, gates, bias, cs, ns, ms, dh)


def _hgrn_act(qb_, fb_, ib_, lg):
    lb = _sigmoid(lg[0:1, :] - lg[1:2, :])
    sg = _sigmoid(fb_)
    f = lb + (1.0 - lb) * sg
    return lb, sg, f, _silu(qb_), (1.0 - lb) * (1.0 - sg), _silu(ib_), _cumsum_rows(jnp.log(f))


HG_SUB = 16


def _hgrn_offdiag(q, k, b, r0):
    beta = b[r0 - 1:r0, :]
    e1 = jnp.exp(b[r0:r0 + HG_SUB, :] - beta)
    e2 = jnp.where(_rows_of(b) < r0, jnp.exp(jnp.minimum(beta - b, 0.0)), 0.0)
    return q[r0:r0 + HG_SUB, :] * e1, k * e2, e1, e2


def _hgrn_fwd(z0, lbl, name):
    t = z0.shape[0]
    nc, nh, L = t // CHUNK, 4, CHUNK

    def body(q_ref, f_ref, i_ref, l_ref, o_ref, ss_ref, st_s):
        @pl.when(pl.program_id(0) == 0)
        def _():
            st_s[...] = jnp.zeros_like(st_s)

        for hd in range(nh):
            sl = slice(hd * HD, (hd + 1) * HD)
            st = st_s[hd]
            for ck in range(cps):
                rows = slice(ck * L, (ck + 1) * L)
                ss_ref[hd, ck] = st
                _, _, _, q, k, v, b = _hgrn_act(q_ref[rows, sl], f_ref[rows, sl], i_ref[rows, sl], l_ref[:, sl])
                o = _nt((q * jnp.exp(b)).astype(BF16), st.astype(BF16))
                sub = _rows_of(b) & (HG_SUB - 1)
                o = o + jnp.sum(q * k, axis=1, keepdims=True) * v
                for dl in range(1, HG_SUB):
                    e = jnp.exp(jnp.where(sub >= dl, b - pltpu.roll(b, dl, 0), NEG))
                    a = jnp.sum(q * pltpu.roll(k, dl, 0) * e, axis=1, keepdims=True)
                    o = o + a * pltpu.roll(v, dl, 0)
                o_ref[rows, sl] = o
                vb = v.astype(BF16)
                for i in range(1, L // HG_SUB):
                    r0 = i * HG_SUB
                    qt, kt, _, _ = _hgrn_offdiag(q, k, b, r0)
                    a = _nt(qt.astype(BF16), kt.astype(BF16))
                    o_ref[ck * L + r0:ck * L + r0 + HG_SUB, sl] += _nn(a.astype(BF16), vb)
                bl = b[L - 1:L, :]
                st = st * jnp.exp(bl) + _tn(v.astype(BF16), (k * jnp.exp(bl - b)).astype(BF16))
            st_s[hd] = st

    cps = REC_CHUNKS
    hspec = lambda blk: pl.BlockSpec((cps * L, 512), lambda j: (j, blk))
    return _pc(body, name, (nc // cps,),
               [hspec(4), hspec(5), hspec(6), pl.BlockSpec((2, 512), lambda j: (0, 0))],
               [hspec(0), pl.BlockSpec((nh, cps, HD, HD), lambda j: (0, j, 0, 0))],
               [_sds((t, 512)), _sds((nh, nc, HD, HD))],
               scratch=[pltpu.VMEM((nh, HD, HD), F32)])(z0, z0, z0, lbl)


def _hgrn_bwd(z0, lbl, ss, do, name):
    t = z0.shape[0]
    nc, nh, L = t // CHUNK, 4, CHUNK

    def body(q_ref, f_ref, i_ref, l_ref, ss_ref, do_ref, dq_ref, df_ref, di_ref, dl_ref, dst_s, dlb_s, dq_a, dk_a, dv_a, db_a):
        @pl.when(pl.program_id(0) == 0)
        def _():
            dst_s[...] = jnp.zeros_like(dst_s)
            dlb_s[...] = jnp.zeros_like(dlb_s)

        for ck in reversed(range(cps)):
            for hd in range(nh):
                one_head(hd, ck, slice(hd * HD, (hd + 1) * HD), slice(ck * L, (ck + 1) * L), q_ref, f_ref, i_ref, l_ref, ss_ref, do_ref,
                         dq_ref, df_ref, di_ref, dl_ref, dst_s, dlb_s, dq_a.at[hd], dk_a.at[hd], dv_a.at[hd], db_a.at[hd])

    def one_head(hd, ck, sl, rs, q_ref, f_ref, i_ref, l_ref, ss_ref, do_ref, dq_ref, df_ref, di_ref, dl_ref, dst_s, dlb_s,
                 dq_a, dk_a, dv_a, db_a):
        st = ss_ref[hd, ck]
        qp, fp, ip = q_ref[rs, sl], f_ref[rs, sl], i_ref[rs, sl]
        lb, sg, f, q, k, v, b = _hgrn_act(qp, fp, ip, l_ref[:, sl])
        do_ = do_ref[rs, sl]
        dob, stb = do_.astype(BF16), st.astype(BF16)
        eb = jnp.exp(b)
        qe = q * eb
        dqe = _nn(dob, stb)
        dst_o = _tn(dob, qe.astype(BF16))
        dq = dqe * eb
        db = dqe * qe
        rows = _rows_of(b)
        sub = rows & (HG_SUB - 1)
        p0 = jnp.sum(do_ * v, axis=1, keepdims=True)
        dq = dq + p0 * k
        dk = p0 * q
        dv = jnp.sum(q * k, axis=1, keepdims=True) * do_
        for dl in range(1, HG_SUB):
            up = L - dl
            kd, vd = pltpu.roll(k, dl, 0), pltpu.roll(v, dl, 0)
            e = jnp.exp(jnp.where(sub >= dl, b - pltpu.roll(b, dl, 0), NEG))
            a = jnp.sum(q * kd * e, axis=1, keepdims=True)
            p = jnp.sum(do_ * vd, axis=1, keepdims=True) * e
            dq = dq + p * kd
            dkd = p * q
            dbb = dkd * kd
            dv = dv + pltpu.roll(a * do_, up, 0)
            dk = dk + pltpu.roll(dkd, up, 0)
            db = db + dbb - pltpu.roll(dbb, up, 0)
        dq_a[...], dk_a[...], dv_a[...], db_a[...] = dq, dk, dv, db
        vb = v.astype(BF16)
        for i in range(1, L // HG_SUB):
            r0 = i * HG_SUB
            blk = slice(r0, r0 + HG_SUB)
            qt, kt, e1, e2 = _hgrn_offdiag(q, k, b, r0)
            qtb, ktb, dob_i = qt.astype(BF16), kt.astype(BF16), do_[blk, :].astype(BF16)
            a = _nt(qtb, ktb).astype(BF16)
            da = _nt(dob_i, vb).astype(BF16)
            dv_a[...] += _tn(a, dob_i)
            dqt = _nn(da, ktb)
            dkt = _tn(da, qtb)
            dq_a[blk, :] += dqt * e1
            t1, t2 = dqt * qt, dkt * kt
            db_a[blk, :] += t1
            dk_a[...] += dkt * e2
            db_a[...] -= t2
            db_a[r0 - 1:r0, :] += jnp.sum(t2, axis=0, keepdims=True) - jnp.sum(t1, axis=0, keepdims=True)
        dq, dk, dv, db = dq_a[...], dk_a[...], dv_a[...], db_a[...]
        dstn = dst_s[hd]
        dstnb = dstn.astype(BF16)
        bl = b[L - 1:L, :]
        ebl = jnp.exp(bl)
        kdec_e = jnp.exp(bl - b)
        kdec = k * kdec_e
        dbl = jnp.sum(dstn * st, axis=0, keepdims=True) * ebl
        dv = dv + _nt(kdec.astype(BF16), dstnb)
        dkdec = _nn(v.astype(BF16), dstnb)
        dk = dk + dkdec * kdec_e
        dx = dkdec * kdec
        dbl = dbl + jnp.sum(dx, axis=0, keepdims=True)
        db = db - dx + jnp.where(rows == L - 1, dbl, 0.0)
        dst_s[hd] = dstn * ebl + dst_o
        dg = _rcumsum_rows(db)
        dfk = dg / f - dk
        dq_ref[rs, sl] = (dq * _dsilu(qp)).astype(BF16)
        di_ref[rs, sl] = (dv * _dsilu(ip)).astype(BF16)
        df_ref[rs, sl] = (dfk * (1.0 - lb) * sg * (1.0 - sg)).astype(BF16)
        dlb_s[hd] += jnp.sum(dfk * (1.0 - sg), axis=0, keepdims=True)

        if ck == 0:
            @pl.when(pl.program_id(0) == nc // cps - 1)
            def _():
                dl0 = dlb_s[hd] * lb * (1.0 - lb)
                dl_ref[0:1, sl] = dl0
                dl_ref[1:2, sl] = -dl0

    cps = REC_CHUNKS
    rv = lambda j: nc // cps - 1 - j
    hspec = lambda blk: pl.BlockSpec((cps * L, 512), lambda j: (rv(j), blk))
    return _pc(body, name, (nc // cps,),
               [hspec(4), hspec(5), hspec(6), pl.BlockSpec((2, 512), lambda j: (0, 0)),
                pl.BlockSpec((nh, cps, HD, HD), lambda j: (0, rv(j), 0, 0)), hspec(0)],
               [hspec(0), hspec(0), hspec(0), pl.BlockSpec((2, 512), lambda j: (0, 0))],
               [_sds((t, 512), BF16), _sds((t, 512), BF16), _sds((t, 512), BF16), _sds((2, 512))],
               scratch=[pltpu.VMEM((nh, HD, HD), F32), pltpu.VMEM((nh, 1, HD), F32)] + [pltpu.VMEM((nh, L, HD), F32)] * 4)(z0, z0, z0, lbl, ss, do)


def _post0_fwd(hm, hh, z0, na, nb, w, h0, name, bm=512):
    t = h0.shape[0]
    bm = min(bm, t)

    def body(hm_ref, hh_ref, oa_ref, gb_ref, na_ref, nb_ref, w_ref, h_ref, o_ref, y_ref):
        for hd in range(4):
            sl = slice(hd * HD, (hd + 1) * HD)
            pa = _sigmoid(oa_ref[:, sl]) * hm_ref[:, sl]
            y_ref[:, sl] = (pa * _rstd(pa) * na_ref[:, sl]).astype(BF16)
            xb = hh_ref[:, sl]
            y_ref[:, 512 + hd * HD:512 + (hd + 1) * HD] = (xb * _rstd(xb) * nb_ref[:, sl] * _silu(gb_ref[:, sl])).astype(BF16)
        o_ref[...] = h_ref[...] + _nn(y_ref[...], w_ref[...])

    row = lambda wd, c: pl.BlockSpec((bm, wd), lambda i: (i, c))
    vec = lambda wd: pl.BlockSpec((1, wd), lambda i: (0, 0))
    return _pc(body, name, (t // bm,),
               [row(512, 0), row(512, 0), row(512, 3), row(512, 7), vec(512), vec(512),
                pl.BlockSpec((D, D), lambda i: (0, 0)), row(D, 0)],
               [row(D, 0), row(D, 0)], [_sds((t, D)), _sds((t, D), BF16)])(hm, hh, z0, z0, na, nb, w, h0)


def _post0_bwd(dh1, w, hm, hh, z0, na, nb, name, bm=512):
    t = dh1.shape[0]
    bm = min(bm, t)

    def body(dh_ref, w_ref, hm_ref, hh_ref, oa_ref, gb_ref, na_ref, nb_ref, dhm_ref, dhh_ref, doa_ref, dgb_ref, dna_ref, dnb_ref):
        @pl.when(pl.program_id(0) == 0)
        def _():
            dna_ref[...] = jnp.zeros_like(dna_ref)
            dnb_ref[...] = jnp.zeros_like(dnb_ref)

        dy = _nt(dh_ref[...].astype(BF16), w_ref[...])
        for hd in range(4):
            sl = slice(hd * HD, (hd + 1) * HD)
            hm_, oa = hm_ref[:, sl], oa_ref[:, sl]
            sg = _sigmoid(oa)
            dpa, dgr = _rms_bwd(dy[:, sl], sg * hm_, na_ref[:, sl])
            dna_ref[:, sl] += jnp.sum(dgr, axis=0, keepdims=True)
            doa_ref[:, sl] = (dpa * hm_ * sg * (1.0 - sg)).astype(BF16)
            dhm_ref[:, sl] = dpa * sg
            xb, gb, nbv = hh_ref[:, sl], gb_ref[:, sl], nb_ref[:, sl]
            dyb = dy[:, 512 + hd * HD:512 + (hd + 1) * HD]
            dgb_ref[:, sl] = (dyb * (xb * _rstd(xb) * nbv) * _dsilu(gb)).astype(BF16)
            dxb, dgr2 = _rms_bwd(dyb * _silu(gb), xb, nbv)
            dnb_ref[:, sl] += jnp.sum(dgr2, axis=0, keepdims=True)
            dhh_ref[:, sl] = dxb

    row = lambda wd, c: pl.BlockSpec((bm, wd), lambda i: (i, c))
    vec = lambda wd: pl.BlockSpec((1, wd), lambda i: (0, 0))
    return _pc(body, name, (t // bm,),
               [row(D, 0), pl.BlockSpec((D, D), lambda i: (0, 0)), row(512, 0), row(512, 0), row(512, 3), row(512, 7),
                vec(512), vec(512)],
               [row(512, 0), row(512, 0), row(512, 0), row(512, 0), vec(512), vec(512)],
               [_sds((t, 512)), _sds((t, 512)), _sds((t, 512), BF16), _sds((t, 512), BF16), _sds((1, 512)), _sds((1, 512))],
               )(dh1, w, hm, hh, z0, z0, na, nb)


def _memkv_fwd(mem, g, wkv_s, name):
    m = mem.shape[0]

    def body(x_ref, g_ref, w_ref, kv_ref, mn_ref):
        x = x_ref[...]
        mn = (x * _rstd(x) * g_ref[...]).astype(BF16)
        mn_ref[...] = mn
        kv_ref[...] = _nn(mn, w_ref[...])

    return _pc(body, name, (4,),
               [pl.BlockSpec((m, D), lambda k: (0, 0)), pl.BlockSpec((1, D), lambda k: (0, 0)),
                pl.BlockSpec((None, D, 512), lambda k: (k, 0, 0))],
               [pl.BlockSpec((m, 512), lambda k: (0, k)), pl.BlockSpec((m, D), lambda k: (0, 0))],
               [_sds((m, 2048)), _sds((m, D), BF16)])(mem, g, wkv_s)


def _memkv_bwd(dkv, wkv_s, mem, g, name):
    m = mem.shape[0]

    def body(d_ref, w_ref, x_ref, g_ref, dg_ref, acc):
        k = pl.program_id(0)

        @pl.when(k == 0)
        def _():
            acc[...] = jnp.zeros_like(acc)

        acc[...] += _nt(d_ref[...].astype(BF16), w_ref[...])

        @pl.when(k == 3)
        def _():
            _, dgr = _rms_bwd(acc[...], x_ref[...], g_ref[...])
            dg_ref[...] = jnp.sum(dgr, axis=0, keepdims=True)

    return _pc(body, name, (4,),
               [pl.BlockSpec((m, 512), lambda k: (0, k)), pl.BlockSpec((None, D, 512), lambda k: (k, 0, 0)),
                pl.BlockSpec((m, D), lambda k: (0, 0)), pl.BlockSpec((1, D), lambda k: (0, 0))],
               pl.BlockSpec((1, D), lambda k: (0, 0)), _sds((1, D)), scratch=[pltpu.VMEM((m, D), F32)])(dkv, wkv_s, mem, g)


def _xattn_probs(qh, kh):
    s = _nt(qh, kh) * (XD ** -0.5)
    p = jnp.exp(s - jnp.max(s, axis=1, keepdims=True))
    return p / jnp.sum(p, axis=1, keepdims=True)


def _xattn_fwd(q, kv, wo, h1, name, bm=512):
    t, m = q.shape[0], kv.shape[0]
    bm = min(bm, t)

    def body(q_ref, k_ref, v_ref, w_ref, h_ref, out_ref, o_ref):
        for hd in range(D // XD):
            sl = slice(hd * XD, (hd + 1) * XD)
            p = _xattn_probs(q_ref[:, sl].astype(BF16), k_ref[:, sl].astype(BF16))
            o_ref[:, sl] = _nn(p.astype(BF16), v_ref[:, sl].astype(BF16)).astype(BF16)
        out_ref[...] = h_ref[...] + _nn(o_ref[...], w_ref[...])

    row = pl.BlockSpec((bm, D), lambda i: (i, 0))
    return _pc(body, name, (t // bm,),
               [row, pl.BlockSpec((m, D), lambda i: (0, 0)), pl.BlockSpec((m, D), lambda i: (0, 1)),
                pl.BlockSpec((D, D), lambda i: (0, 0)), row],
               [row, row], [_sds((t, D)), _sds((t, D), BF16)])(q, kv, kv, wo, h1)


def _xattn_bwd(dh2, q, kv, wo, name, bm=512):
    t, m = q.shape[0], kv.shape[0]
    bm = min(bm, t)

    def body(dh_ref, q_ref, k_ref, v_ref, w_ref, dq_ref, dkv_ref):
        @pl.when(pl.program_id(0) == 0)
        def _():
            dkv_ref[...] = jnp.zeros_like(dkv_ref)

        d_o = _nt(dh_ref[...].astype(BF16), w_ref[...])
        for hd in range(D // XD):
            sl = slice(hd * XD, (hd + 1) * XD)
            qh, kh, vh = q_ref[:, sl].astype(BF16), k_ref[:, sl].astype(BF16), v_ref[:, sl].astype(BF16)
            p = _xattn_probs(qh, kh)
            dob = d_o[:, sl].astype(BF16)
            dp = _nt(dob, vh)
            dkv_ref[:, D + hd * XD:D + (hd + 1) * XD] += _tn(p.astype(BF16), dob)
            ds = (p * (dp - jnp.sum(dp * p, axis=1, keepdims=True)) * (XD ** -0.5)).astype(BF16)
            dq_ref[:, sl] = _nn(ds, kh).astype(BF16)
            dkv_ref[:, sl] += _tn(ds, qh)

    row = pl.BlockSpec((bm, D), lambda i: (i, 0))
    return _pc(body, name, (t // bm,),
               [row, row, pl.BlockSpec((m, D), lambda i: (0, 0)), pl.BlockSpec((m, D), lambda i: (0, 1)),
                pl.BlockSpec((D, D), lambda i: (0, 0))],
               [row, pl.BlockSpec((m, 2 * D), lambda i: (0, 0))],
               [_sds((t, D), BF16), _sds((m, 2 * D))])(dh2, q, kv, kv, wo)


NH1 = 8
FOX_BM = 512
FOX_BQ = 512
FOX_BK = 512
FOX_HEADS_PER_STEP = 4


def _foxprep_fwd(z1, qg, kg, fbp, name):
    t = z1.shape[0]
    bm = min(FOX_BM, t)

    def body(q_ref, k_ref, v_ref, f_ref, qg_ref, kg_ref, fb_ref, qn_ref, kn_ref, vb_ref, c_ref, carry):
        @pl.when(pl.program_id(0) == 0)
        def _():
            carry[...] = jnp.zeros_like(carry)

        for hd in range(NH1):
            sl = slice(hd * HD, (hd + 1) * HD)
            x = q_ref[:, sl]
            qn_ref[:, sl] = (x * _rstd(x) * qg_ref[...] * FOX_QSCALE).astype(BF16)
            x = k_ref[:, sl]
            kn_ref[:, sl] = (x * _rstd(x) * kg_ref[...]).astype(BF16)
        vb_ref[...] = v_ref[...].astype(BF16)
        c = carry[...] + _cumsum_rows(_log_sigmoid(f_ref[...] + fb_ref[...]))
        c_ref[...] = c
        carry[...] = c[bm - 1:bm, :]

    row = lambda c: pl.BlockSpec((bm, D), lambda i: (i, c))
    lane = pl.BlockSpec((bm, HD), lambda i: (i, 4 * D // HD))
    vec = pl.BlockSpec((1, HD), lambda i: (0, 0))
    return _pc(body, name, (t // bm,), [row(0), row(1), row(2), lane, vec, vec, vec],
               [row(0), row(0), row(0), pl.BlockSpec((bm, HD), lambda i: (i, 0))],
               [_sds((t, D), BF16), _sds((t, D), BF16), _sds((t, D), BF16), _sds((t, HD))],
               scratch=[pltpu.VMEM((1, HD), F32)])(z1, z1, z1, z1, qg, kg, fbp)


def _foxprep_bwd(dqn, dkn, z1, qg, kg, fbp, dc, name):
    t = z1.shape[0]
    bm = min(FOX_BM, t)
    nb = t // bm

    def body(dqn_ref, dkn_ref, q_ref, k_ref, f_ref, qg_ref, kg_ref, fb_ref, dc_ref,
             dq_ref, dk_ref, df_ref, dqg_ref, dkg_ref, dfb_ref, carry):
        @pl.when(pl.program_id(0) == 0)
        def _():
            carry[...] = jnp.zeros_like(carry)
            dqg_ref[...] = jnp.zeros_like(dqg_ref)
            dkg_ref[...] = jnp.zeros_like(dkg_ref)
            dfb_ref[...] = jnp.zeros_like(dfb_ref)

        for hd in range(NH1):
            sl = slice(hd * HD, (hd + 1) * HD)
            dx, dgr = _rms_bwd(dqn_ref[:, sl] * (HD ** -0.5), q_ref[:, sl], qg_ref[...])
            dq_ref[:, sl] = dx.astype(BF16)
            dqg_ref[...] += jnp.sum(dgr, axis=0, keepdims=True)
            dx, dgr = _rms_bwd(dkn_ref[:, sl], k_ref[:, sl], kg_ref[...])
            dk_ref[:, sl] = dx.astype(BF16)
            dkg_ref[...] += jnp.sum(dgr, axis=0, keepdims=True)
        dc_ = dc_ref[...]
        dlogf = _rcumsum_rows(dc_) + carry[...]
        carry[...] += jnp.sum(dc_, axis=0, keepdims=True)
        lanes = lax.broadcasted_iota(jnp.int32, dc_.shape, 1)
        df = jnp.where(lanes < NH1, dlogf * (1.0 - _sigmoid(f_ref[...] + fb_ref[...])), 0.0)
        df_ref[...] = df.astype(BF16)
        dfb_ref[...] += jnp.sum(df, axis=0, keepdims=True)

    rv = lambda i: nb - 1 - i
    row = lambda c: pl.BlockSpec((bm, D), lambda i: (rv(i), c))
    lane = lambda c: pl.BlockSpec((bm, HD), lambda i: (rv(i), c))
    vec = pl.BlockSpec((1, HD), lambda i: (0, 0))
    return _pc(body, name, (nb,), [row(0), row(0), row(0), row(1), lane(4 * D // HD), vec, vec, vec, lane(0)],
               [row(0), row(0), lane(0), vec, vec, vec],
               [_sds((t, D), BF16), _sds((t, D), BF16), _sds((t, HD), BF16), _sds((1, HD)), _sds((1, HD)), _sds((1, HD))],
               scratch=[pltpu.VMEM((1, HD), F32)])(dqn, dkn, z1, z1, z1, qg, kg, fbp, dc)


LOG2E = 1.4426950408889634
FOX_QSCALE = HD ** -0.5 * LOG2E


def _fox_steps(t, bq, bk, k_major):
    nq, nk = t // bq, t // bk
    pairs = [(i, j) for i in range(nq) for j in range(nk) if j * bk < (i + 1) * bq]
    if k_major:
        pairs.sort(key=lambda p: (p[1], p[0]))
    outer = [p[1] if k_major else p[0] for p in pairs]
    n = len(pairs)
    flags = [(n_ == 0 or outer[n_] != outer[n_ - 1]) + 2 * (n_ == n - 1 or outer[n_] != outer[n_ + 1])
             + 4 * (not (j + 1) * bk <= i * bq + 1) for n_, (i, j) in enumerate(pairs)]
    as_i32 = lambda v: jnp.asarray(v, jnp.int32)
    return as_i32([p[0] for p in pairs]), as_i32([p[1] for p in pairs]), as_i32(flags)


def _fox_step_info(qi_ref, kj_ref, fl_ref):
    s = pl.program_id(1)
    fl = fl_ref[s]
    return qi_ref[s], kj_ref[s], (fl & 1) != 0, (fl & 2) != 0, (fl & 4) != 0


def _fox_call(body, name, tables, in_specs, out_specs, out_shape, scratch):
    grid_spec = pltpu.PrefetchScalarGridSpec(num_scalar_prefetch=3, grid=(NH1 // FOX_HEADS_PER_STEP, tables[0].shape[0]),
                                             in_specs=in_specs, out_specs=out_specs, scratch_shapes=scratch)
    return pl.pallas_call(body, name=name, grid_spec=grid_spec, out_shape=out_shape,
                          compiler_params=pltpu.CompilerParams(dimension_semantics=("arbitrary", "arbitrary"),
                                                               vmem_limit_bytes=VMEM_LIMIT_V7X))


def _fox_lane_tiles(x):
    return [x[:, c0:c0 + HD] for c0 in range(0, x.shape[1], HD)]


def _fox_masked_scores(q, k, ck, i, j, bq, bk, masked):
    s = _nt(q, k) - ck
    if masked:
        rows = i * bq + lax.broadcasted_iota(jnp.int32, s.shape, 0)
        cols = j * bk + lax.broadcasted_iota(jnp.int32, s.shape, 1)
        s = jnp.where(cols <= rows, s, NEG)
    return s


def _fox_specs(bq, bk, G):
    qspec = pl.BlockSpec((bq, G * HD), lambda h, s, qi, kj, fl: (qi[s], h))
    kspec = pl.BlockSpec((bk, G * HD), lambda h, s, qi, kj, fl: (kj[s], h))
    cspec = pl.BlockSpec((G, 1, bk), lambda h, s, qi, kj, fl: (h, 0, kj[s]))
    colspec = pl.BlockSpec((G, bq, 1), lambda h, s, qi, kj, fl: (h, qi[s], 0))
    return qspec, kspec, cspec, colspec


def _fox_rowmax(qn, kn, crow, name):
    t = qn.shape[0]
    bq, bk, G = min(FOX_BQ, t), min(2 * FOX_BK, t), FOX_HEADS_PER_STEP
    tables = _fox_steps(t, bq, bk, k_major=False)

    def body(qi_ref, kj_ref, fl_ref, q_ref, k_ref, ck_ref, m_ref, *mp):
        i, j, first, last, diag = _fox_step_info(qi_ref, kj_ref, fl_ref)

        @pl.when(first)
        def _():
            for g in range(G):
                mp[g][...] = jnp.full_like(mp[g], NEG)

        def step(masked):
            for g in range(G):
                sl = slice(g * HD, (g + 1) * HD)
                s = _fox_masked_scores(q_ref[:, sl], k_ref[:, sl], ck_ref[g], i, j, bq, bk, masked)
                m = mp[g][...]
                for tile in _fox_lane_tiles(s):
                    m = jnp.maximum(m, tile)
                mp[g][...] = m

        pl.when(jnp.logical_not(diag))(lambda: step(False))
        pl.when(diag)(lambda: step(True))

        @pl.when(last)
        def _():
            for g in range(G):
                m_ref[g] = jnp.max(mp[g][...], axis=1, keepdims=True)

    qspec, kspec, cspec, colspec = _fox_specs(bq, bk, G)
    return _fox_call(body, name, tables, [qspec, kspec, cspec], colspec, _sds((NH1, t, 1)),
                     [pltpu.VMEM((bq, HD), F32)] * G)(*tables, qn, kn, crow)


def _fox_fwd(qn, kn, vb, crow, m, name):
    t = qn.shape[0]
    bq, bk, G = min(FOX_BQ, t), min(FOX_BK, t), FOX_HEADS_PER_STEP
    tables = _fox_steps(t, bq, bk, k_major=False)

    def body(qi_ref, kj_ref, fl_ref, q_ref, k_ref, v_ref, ck_ref, m_ref, o_ref, lse_ref, *scr):
        i, j, first, last, diag = _fox_step_info(qi_ref, kj_ref, fl_ref)
        lp, acc = scr[:G], scr[G:]

        @pl.when(first)
        def _():
            for g in range(G):
                lp[g][...] = jnp.zeros_like(lp[g])
                acc[g][...] = jnp.zeros_like(acc[g])

        def step(masked):
            for g in range(G):
                sl = slice(g * HD, (g + 1) * HD)
                s = _fox_masked_scores(q_ref[:, sl], k_ref[:, sl], ck_ref[g], i, j, bq, bk, masked)
                p = jnp.exp2(s - m_ref[g])
                l = lp[g][...]
                for tile in _fox_lane_tiles(p):
                    l = l + tile
                lp[g][...] = l
                acc[g][...] += _nn(p.astype(BF16), v_ref[:, sl])

        pl.when(jnp.logical_not(diag))(lambda: step(False))
        pl.when(diag)(lambda: step(True))

        @pl.when(last)
        def _():
            for g in range(G):
                l = jnp.sum(lp[g][...], axis=1, keepdims=True)
                o_ref[:, g * HD:(g + 1) * HD] = acc[g][...] / l
                lse_ref[g] = m_ref[g] + jnp.log2(l)

    qspec, kspec, cspec, colspec = _fox_specs(bq, bk, G)
    return _fox_call(body, name, tables, [qspec, kspec, kspec, cspec, colspec], [qspec, colspec],
                     [_sds((t, D)), _sds((NH1, t, 1))], [pltpu.VMEM((bq, HD), F32)] * (2 * G))(*tables, qn, kn, vb, crow, m)


def _fox_bwd(qn, kn, vb, crow, lse, delta, do, name):
    t = qn.shape[0]
    bq, bk, G = min(FOX_BQ, t), min(FOX_BK, t), FOX_HEADS_PER_STEP
    tables = _fox_steps(t, bq, bk, k_major=True)

    def body(qi_ref, kj_ref, fl_ref, q_ref, k_ref, v_ref, ck_ref, lse_ref, dl_ref, do_ref, dq_ref, dk_ref, dv_ref, dc_ref, dcq_ref,
             dk_s, dv_s, dc_s):
        i, j, first, last, diag = _fox_step_info(qi_ref, kj_ref, fl_ref)

        @pl.when(first)
        def _():
            dk_s[...] = jnp.zeros_like(dk_s)
            dv_s[...] = jnp.zeros_like(dv_s)
            dc_s[...] = jnp.zeros_like(dc_s)

        @pl.when(pl.program_id(1) == 0)
        def _():
            dq_ref[...] = jnp.zeros_like(dq_ref)
            dcq_ref[...] = jnp.zeros_like(dcq_ref)

        def step(masked):
            rows = pl.ds(pl.multiple_of(i * bq, bq), bq)
            for g in range(G):
                sl = slice(g * HD, (g + 1) * HD)
                q, k = q_ref[:, sl], k_ref[:, sl]
                s = _fox_masked_scores(q, k, ck_ref[g], i, j, bq, bk, masked)
                p = jnp.exp2(s - lse_ref[g])
                dob = do_ref[:, sl]
                dv_s[:, sl] += _tn(p.astype(BF16), dob)
                ds = p * (_nt(dob, v_ref[:, sl]) - dl_ref[g])
                dsb = ds.astype(BF16)
                dq_ref[rows, sl] += _nn(dsb, k)
                dk_s[:, sl] += _tn(dsb, q)
                dc_s[g] -= jnp.sum(ds, axis=0, keepdims=True)
                part_sum = dcq_ref[g, rows, :]
                for tile in _fox_lane_tiles(ds):
                    part_sum = part_sum + tile
                dcq_ref[g, rows, :] = part_sum

        pl.when(jnp.logical_not(diag))(lambda: step(False))
        pl.when(diag)(lambda: step(True))

        @pl.when(last)
        def _():
            dk_ref[...] = dk_s[...] * (1.0 / LOG2E)
            dv_ref[...] = dv_s[...]
            dc_ref[...] = dc_s[...]

    qspec, kspec, cspec, colspec = _fox_specs(bq, bk, G)
    return _fox_call(
        body, name, tables, [qspec, kspec, kspec, cspec, colspec, colspec, qspec],
        [pl.BlockSpec((t, G * HD), lambda h, s, qi, kj, fl: (0, h)), kspec, kspec, cspec,
         pl.BlockSpec((G, t, HD), lambda h, s, qi, kj, fl: (h, 0, 0))],
        [_sds((t, D)), _sds((t, D)), _sds((t, D)), _sds((NH1, 1, t)), _sds((NH1, t, HD))],
        [pltpu.VMEM((bk, G * HD), F32), pltpu.VMEM((bk, G * HD), F32), pltpu.VMEM((G, 1, bk), F32)],
    )(*tables, qn, kn, vb, crow, lse, delta, do)


def _post1_fwd(o, z1, w, h3, name, bm=512):
    t = o.shape[0]
    bm = min(bm, t)

    def body(o_ref, g_ref, w_ref, h_ref, out_ref, og_ref):
        og_ref[...] = (o_ref[...] * _sigmoid(g_ref[...])).astype(BF16)
        out_ref[...] = h_ref[...] + _nn(og_ref[...], w_ref[...])

    row = lambda c: pl.BlockSpec((bm, D), lambda i: (i, c))
    return _pc(body, name, (t // bm,), [row(0), row(3), pl.BlockSpec((D, D), lambda i: (0, 0)), row(0)],
               [row(0), row(0)], [_sds((t, D)), _sds((t, D), BF16)])(o, z1, w, h3)


def _post1_bwd(dh4, w, o, z1, name, bm=512):
    t = o.shape[0]
    bm = min(bm, t)

    def body(dh_ref, w_ref, o_ref, g_ref, do_ref, dg_ref, dl_ref):
        d_og = _nt(dh_ref[...].astype(BF16), w_ref[...])
        o_, sg = o_ref[...], _sigmoid(g_ref[...])
        dob = (d_og * sg).astype(BF16)
        do_ref[...] = dob
        dg_ref[...] = (d_og * o_ * sg * (1.0 - sg)).astype(BF16)
        prod = dob.astype(F32) * o_
        for hd in range(NH1):
            dl_ref[hd] = jnp.sum(prod[:, hd * HD:(hd + 1) * HD], axis=1, keepdims=True)

    row = lambda c: pl.BlockSpec((bm, D), lambda i: (i, c))
    return _pc(body, name, (t // bm,), [row(0), pl.BlockSpec((D, D), lambda i: (0, 0)), row(0), row(3)],
               [row(0), row(0), pl.BlockSpec((NH1, bm, 1), lambda i: (0, i, 0))],
               [_sds((t, D), BF16), _sds((t, D), BF16), _sds((NH1, t, 1))])(dh4, w, o, z1)


def _final(h, g, tgt, name, bm=512):
    t = h.shape[0]
    bm = min(bm, t)

    def body(h_ref, g_ref, t_ref, l_ref, dh_ref, dg_ref):
        @pl.when(pl.program_id(0) == 0)
        def _():
            l_ref[...] = jnp.zeros_like(l_ref)
            dg_ref[...] = jnp.zeros_like(dg_ref)

        x, gv = h_ref[...], g_ref[...]
        r = _rstd(x)
        xh = x * r
        e = xh * gv - t_ref[...]
        l_ref[...] += 0.5 * jnp.sum(jnp.mean(e * e, axis=1, keepdims=True), axis=0, keepdims=True)
        dy = e * (1.0 / D)
        dg_ref[...] += jnp.sum(dy * xh, axis=0, keepdims=True)
        dxh = dy * gv
        dh_ref[...] = r * (dxh - xh * jnp.mean(dxh * xh, axis=1, keepdims=True))

    row = pl.BlockSpec((bm, D), lambda i: (i, 0))
    vec = pl.BlockSpec((1, D), lambda i: (0, 0))
    return _pc(body, name, (t // bm,), [row, vec, row], [pl.BlockSpec((1, HD), lambda i: (0, 0)), row, vec],
               [_sds((1, HD)), _sds((t, D)), _sds((1, D))])(h, g, tgt)


def _adam(w, g, m, v, name):
    r, c = w.shape
    br = min(r, 256)

    def body(w_ref, g_ref, m_ref, v_ref, d_ref, mo_ref, vo_ref):
        gv = g_ref[...]
        mn = ADAM_B1 * m_ref[...] + (1.0 - ADAM_B1) * gv
        vn = ADAM_B2 * v_ref[...] + (1.0 - ADAM_B2) * jnp.square(gv)
        m_hat = mn / (1.0 - ADAM_B1 ** ADAM_STEP)
        v_hat = vn / (1.0 - ADAM_B2 ** ADAM_STEP)
        d_ref[...] = -ADAM_LR * (m_hat / (jnp.sqrt(v_hat) + ADAM_EPS) + ADAM_WD * w_ref[...])
        mo_ref[...] = mn
        vo_ref[...] = vn

    blk = pl.BlockSpec((br, c), lambda i: (i, 0))
    return _pc(body, name, (r // br,), [blk] * 4, [blk] * 3, [_sds((r, c))] * 3)(w, g, m, v)


ZW = 4224
GATE0 = 4096


def _pack_w_in0(w):
    return jnp.concatenate([w[:, :2048], w[:, 2056:], w[:, 2048:2056], jnp.zeros((w.shape[0], ZW - 4104), w.dtype)], axis=1)


def _unpack_w_in0(g):
    return jnp.concatenate([g[:, :2048], g[:, GATE0:GATE0 + 8], g[:, 2048:GATE0]], axis=1)


def _pack_w_in1(w):
    return jnp.concatenate([w, jnp.zeros((w.shape[0], ZW - 4104), w.dtype)], axis=1)


def _unpack_w_in1(g):
    return g[:, :4104]


def _local_step(x, mem, tgt, W, S, late_weights=None, grads_hook=None):
    t = x.shape[0]
    row = lambda v: v.reshape(1, -1)
    G = {}

    z0, u0 = _norm_mm(x, S["norm_mix_g"][0:1], W["w_in0"], "in0_fwd")
    qk = _conv_fwd(z0, S["conv_w"], "conv_fwd")
    g8 = z0[:, GATE0:GATE0 + 8]
    gates3 = jnp.stack([g8[:, :4].T, g8[:, 4:].T], axis=-1)
    gb = S["gate_b"]
    bias3 = jnp.stack([gb[0, :4], gb[0, 4:]], axis=-1)[:, None, :]
    hm, cs, ns, ms = _mlstm_fwd(qk, z0, gates3, bias3, "mlstm_fwd")
    hh, ss = _hgrn_fwd(z0, S["lb_logits"], "hgrn_fwd")
    if late_weights is not None:
        W = {**W, **late_weights(hh)}
    kv, mn = _memkv_fwd(mem, row(S["mem_norm_g"]), W["wkv_s"], "memkv_fwd")
    h1, y0 = _post0_fwd(hm, hh, z0, S["mlstm_norm_g"], S["hgrn_norm_g"], W["w_out0"], x, "post0_fwd")

    def xattn_mlp_fwd(h, l):
        q, ux = _norm_mm(h, S["norm_xattn_g"][l:l + 1], W["wq"][l], f"xq{l}_fwd")
        h2, ox = _xattn_fwd(q, kv, W["wo"][l], h, f"xattn{l}_fwd")
        h3, a, um = _mlp_fwd(h2, S["norm_mlp_g"][l:l + 1], W["w1s"], W["w2"], l, f"mlp{l}_fwd")
        return h3, (h, q, ux, ox, h2, a, um)

    h3, sv0 = xattn_mlp_fwd(h1, 0)
    z1, u1 = _norm_mm(h3, S["norm_mix_g"][1:2], W["w_in1"], "in1_fwd")
    fbp = jnp.pad(S["c_fgate_b"], ((0, 0), (0, HD - NH1)))
    qn, kn, vb, c = _foxprep_fwd(z1, S["c_qnorm_g"], S["c_knorm_g"], fbp, "foxprep_fwd")
    crow = (c[:, :NH1] * LOG2E).T[:, None, :]
    o1, lse = _fox_fwd(qn, kn, vb, crow, _fox_rowmax(qn, kn, crow, "fox_rowmax"), "fox_fwd")
    h4, og = _post1_fwd(o1, z1, W["w_out1"], h3, "post1_fwd")
    h6, sv1 = xattn_mlp_fwd(h4, 1)
    lossp, dh, G["final_norm_g"] = _final(h6, row(S["final_norm_g"]), tgt, "final")

    grads_ready = grads_hook if grads_hook is not None else (lambda stage, grads: 0.0)
    dkv = None
    dgx, dgm, dwq, dwo, dw1, dw2 = [None, None], [None, None], [None, None], [None, None], [None, None], [None, None]

    def xattn_mlp_bwd(dh, l, sv):
        nonlocal dkv
        h, q, ux, ox, h2, a, um = sv
        dh2, da, r, dgm[l] = _mlp_bwd(dh, a, W["w1s"], W["w2"], l, h2, S["norm_mlp_g"][l:l + 1], f"mlp{l}_bwd")
        dw1[l] = _mm_tn(um, da, f"mlp{l}_dw1", col_chips=NCHIP)
        dw2[l] = _mm_tn(r, dh, f"mlp{l}_dw2")
        dq, dkv_l = _xattn_bwd(dh2, q, kv, W["wo"][l], f"xattn{l}_bwd")
        dkv = dkv_l if dkv is None else dkv + dkv_l
        dwo[l] = _mm_tn(ox, dh2, f"xattn{l}_dwo")
        dwq[l] = _mm_tn(ux, dq, f"xattn{l}_dwq")
        tok = 0.0
        if l == 0:
            G["wkv"] = _mm_tn(mn, dkv, "memkv_dw", col_chips=NCHIP)
            G["mem_norm_g"] = _memkv_bwd(dkv, W["wkv_s"], mem, row(S["mem_norm_g"]), "memkv_bwd")
            tok = grads_ready("layer0_mlp_xattn", dict(wq=dwq[0], wo=dwo[0], w1=dw1[0], w2=dw2[0], wkv=G["wkv"]))
        dh1, dgx[l] = _bwd_in(dq, W["wq"][l], h, S["norm_xattn_g"][l:l + 1] + tok, dh2, f"xq{l}_bwd")
        return dh1

    dh4 = xattn_mlp_bwd(dh, 1, sv1)
    do, dgate, delta = _post1_bwd(dh4, W["w_out1"], o1, z1, "post1_bwd")
    G["w_out1"] = _mm_tn(og, dh4, "post1_dw")
    dqn, dkn, dv1, dcrow, dcq = _fox_bwd(qn, kn, vb, crow, lse, delta, do, "fox_bwd")
    dc = jnp.pad((dcrow[:, 0, :] + jnp.sum(dcq, axis=-1)).T, ((0, 0), (0, HD - NH1)))
    dqr, dkr, df1, G["c_qnorm_g"], G["c_knorm_g"], dfb = _foxprep_bwd(
        dqn, dkn, z1, S["c_qnorm_g"], S["c_knorm_g"], fbp, dc, "foxprep_bwd")
    G["c_fgate_b"] = dfb[:, :NH1]
    dz1 = jnp.concatenate([dqr, dkr, dv1.astype(BF16), dgate, df1], axis=1)
    G["w_in1"] = _mm_tn(u1, dz1, "in1_dw")
    tok = grads_ready("layer1", dict(w_out=G["w_out1"], w_in=G["w_in1"], wq=dwq[1], wo=dwo[1], w1=dw1[1], w2=dw2[1]))
    dh3, dgmix1 = _bwd_in(dz1, W["w_in1"], h3, S["norm_mix_g"][1:2] + tok, dh4, "in1_bwd")
    dh1 = xattn_mlp_bwd(dh3, 0, sv0)

    dhm, dhh, doa, dgb, G["mlstm_norm_g"], G["hgrn_norm_g"] = _post0_bwd(
        dh1, W["w_out0"], hm, hh, z0, S["mlstm_norm_g"], S["hgrn_norm_g"], "post0_bwd")
    G["w_out0"] = _mm_tn(y0, dh1, "post0_dw")
    dqa, dka, dva, dgates3 = _mlstm_bwd(qk, z0, gates3, bias3, cs, ns, ms, dhm, "mlstm_bwd")
    dqb, dfb0, dib, G["lb_logits"] = _hgrn_bwd(z0, S["lb_logits"], ss, dhh, "hgrn_bwd")
    duc, G["conv_w"] = _conv_bwd(z0, S["conv_w"], jnp.concatenate([dqa, dka], axis=1), "conv_bwd")
    dg8 = jnp.concatenate([dgates3[:, :, 0].T, dgates3[:, :, 1].T], axis=1)
    G["gate_b"] = jnp.sum(dg8, axis=0, keepdims=True)
    dz0 = jnp.concatenate([duc, dva.astype(BF16), doa, dqb, dfb0, dib, dgb,
                           jnp.pad(dg8, ((0, 0), (0, HD - 8))).astype(BF16)], axis=1)
    G["w_in0"] = _mm_tn(u0, dz0, "in0_dw")
    dx, dgmix0 = _bwd_in(dz0, W["w_in0"], x, S["norm_mix_g"][0:1], dh1, "in0_bwd")

    G["norm_mix_g"] = jnp.concatenate([dgmix0, dgmix1], axis=0)
    G["norm_xattn_g"] = jnp.concatenate(dgx, axis=0)
    G["norm_mlp_g"] = jnp.concatenate(dgm, axis=0)
    G["wq"], G["wo"], G["w1"], G["w2"] = dwq, dwo, dw1, dw2
    return lossp[0, 0], dx, G


ANY = pl.BlockSpec(memory_space=pl.ANY)
NCHIP = 4


def _place():
    x, y, c = lax.axis_index("x"), lax.axis_index("y"), lax.axis_index("c")
    return x, y, c, [(1 - x, y), (x, 1 - y), (1 - x, 1 - y)]


def _comm_call(body, name, ins, out_shapes, sems):
    return pl.pallas_call(body, name=name, in_specs=[ANY] * len(ins), out_specs=[ANY] * len(out_shapes),
                          out_shape=out_shapes, scratch_shapes=sems)(*ins)


def _gather_weights(arrs, name):
    n = len(arrs)

    def body(*refs):
        ins, outs = refs[:n], refs[n:2 * n]
        send_i, recv_i, send_d, recv_d = refs[2 * n:]
        x, y, c, chips = _place()
        me = 2 * x + y

        def half(a, cc):
            h = arrs[a].shape[0] // 2
            return pl.ds(pl.multiple_of(cc * h, h), h)

        def ici(a, k, src_chip, dst_dev):
            return pltpu.make_async_remote_copy(
                src_ref=ins[a].at[half(a, c)], dst_ref=outs[a].at[src_chip, half(a, c)], send_sem=send_i.at[a, k],
                recv_sem=recv_i.at[a, k], device_id=dst_dev, device_id_type=MESH)

        def d2d(a, k, src_chip, cc):
            reg = outs[a].at[src_chip, half(a, cc)]
            return pltpu.make_async_remote_copy(src_ref=reg, dst_ref=reg, send_sem=send_d.at[a, k], recv_sem=recv_d.at[a, k],
                                                device_id=(x, y, 1 - c), device_id_type=MESH)

        for a in range(n):
            for k, (px, py) in enumerate(chips):
                ici(a, k, me, (px, py, c)).start()
        for k, (px, py) in enumerate(chips):
            for a in range(n):
                ici(a, k, 2 * px + py, (px, py, c)).wait_recv()
                d2d(a, k, 2 * px + py, c).start()
        for k, (px, py) in enumerate(chips):
            for a in range(n):
                ici(a, k, me, (px, py, c)).wait_send()
                d2d(a, k, 2 * px + py, c).wait_send()
                d2d(a, k, 2 * px + py, 1 - c).wait_recv()

    sem = lambda: pltpu.SemaphoreType.DMA((n, 3))
    return _comm_call(body, name, arrs, [_sds((NCHIP,) + a.shape, a.dtype) for a in arrs], [sem(), sem(), sem(), sem()])


HBM = pl.BlockSpec(memory_space=pltpu.HBM)
SEM = pl.BlockSpec(memory_space=pltpu.SEMAPHORE)
DATAFLOW = pltpu.SideEffectType.DATAFLOW_SIDE_EFFECTING


def _half_rows(r, cc):
    return pl.ds(pl.multiple_of(cc * (r // 2), r // 2), r // 2)


def _gather_start(arrs, after, name):
    n = len(arrs)

    def body(*refs):
        ins, lands = refs[:n], refs[n:2 * n]
        send, recv, token = refs[2 * n + 1], refs[2 * n + 2], refs[-1]
        x, y, c, chips = _place()
        me = 2 * x + y
        for a in range(n):
            rows = _half_rows(arrs[a].shape[0], c)
            for k, (px, py) in enumerate(chips):
                pltpu.make_async_remote_copy(src_ref=ins[a].at[rows], dst_ref=lands[a].at[me, rows], send_sem=send.at[3 * a + k],
                                             recv_sem=recv.at[3 * a + k], device_id=(px, py, c), device_id_type=MESH).start()
        token[...] = jnp.zeros_like(token)

    hbm = lambda v: pltpu.with_memory_space_constraint(v, pltpu.HBM)
    land_shapes = [((NCHIP,) + a.shape, a.dtype) for a in arrs]
    out = pl.pallas_call(
        body, name=name,
        out_shape=(pltpu.SemaphoreType.DMA((3 * n,)), pltpu.SemaphoreType.DMA((3 * n,)), *[pltpu.HBM(a.shape, a.dtype) for a in arrs],
                   *[pltpu.HBM(s, d) for s, d in land_shapes], _sds((8, HD))),
        in_specs=[HBM] * (2 * n) + [ANY], out_specs=(SEM, SEM, *[HBM] * (2 * n), pl.BlockSpec(memory_space=pltpu.VMEM)),
        input_output_aliases={i: 2 + i for i in range(2 * n)},
        compiler_params=pltpu.CompilerParams(has_side_effects=DATAFLOW),
    )(*[hbm(a) for a in arrs], *[hbm(lax.empty(s, d)) for s, d in land_shapes], after)
    return out[0], out[1], list(out[2:2 + n]), list(out[2 + n:2 + 2 * n]), out[-1]


def _gather_wait(send, recv, srcs, lands, after, name):
    n = len(srcs)

    def body(*refs):
        ins, lands_ = refs[:n], refs[n:2 * n]
        send_, recv_ = refs[2 * n], refs[2 * n + 1]
        x, y, c, chips = _place()
        for a in range(n):
            rows = _half_rows(srcs[a].shape[0], c)
            for k, (px, py) in enumerate(chips):
                cp = pltpu.make_async_remote_copy(src_ref=ins[a].at[rows], dst_ref=lands_[a].at[2 * px + py, rows], send_sem=send_.at[3 * a + k],
                                                  recv_sem=recv_.at[3 * a + k], device_id=(px, py, c), device_id_type=MESH)
                cp.wait_send()
                cp.wait_recv()

    out = pl.pallas_call(
        body, name=name, out_shape=[pltpu.HBM(v.shape, v.dtype) for v in list(srcs) + list(lands)],
        in_specs=[HBM] * (2 * n) + [SEM, SEM, ANY], out_specs=[HBM] * (2 * n), input_output_aliases={i: i for i in range(2 * n)},
        compiler_params=pltpu.CompilerParams(has_side_effects=DATAFLOW),
    )(*srcs, *lands, send, recv, after)
    return list(out[n:])


def _pair_forward(lands, name):
    n = len(lands)

    def body(*refs):
        ins, outs = refs[:n], refs[n:2 * n]
        send, recv = refs[2 * n:]
        x, y, c, chips = _place()
        copies = []
        for a in range(n):
            r = lands[a].shape[1]
            for k, (px, py) in enumerate(chips):
                cp = pltpu.make_async_remote_copy(
                    src_ref=ins[a].at[2 * px + py, _half_rows(r, c)], dst_ref=outs[a].at[2 * px + py, _half_rows(r, c)],
                    send_sem=send.at[a, k], recv_sem=recv.at[a, k], device_id=(x, y, 1 - c), device_id_type=MESH)
                cp.start()
                copies.append(cp)
        for a in range(n):
            r = lands[a].shape[1]
            for k, (px, py) in enumerate(chips):
                pltpu.make_async_remote_copy(
                    src_ref=ins[a].at[2 * px + py, _half_rows(r, c)], dst_ref=outs[a].at[2 * px + py, _half_rows(r, 1 - c)],
                    send_sem=send.at[a, k], recv_sem=recv.at[a, k], device_id=(x, y, 1 - c), device_id_type=MESH).wait_recv()
        for cp in copies:
            cp.wait_send()

    return pl.pallas_call(body, name=name, in_specs=[ANY] * n, out_specs=[ANY] * n, out_shape=[_sds(v.shape, v.dtype) for v in lands],
                          scratch_shapes=[pltpu.SemaphoreType.DMA((n, 3)), pltpu.SemaphoreType.DMA((n, 3))],
                          input_output_aliases={i: i for i in range(n)})(*lands)


def _pair_exchange(arrs, name):
    n = len(arrs)

    def body(*refs):
        ins, outs = refs[:n], refs[n:2 * n]
        send, recv = refs[2 * n:]
        x, y, c, _ = _place()
        copies = []
        for a in range(n):
            h = arrs[a].shape[1] // 2
            cp = pltpu.make_async_remote_copy(src_ref=ins[a].at[:, pl.ds(pl.multiple_of((1 - c) * h, h), h)], dst_ref=outs[a],
                                              send_sem=send.at[a], recv_sem=recv.at[a], device_id=(x, y, 1 - c), device_id_type=MESH)
            cp.start()
            copies.append(cp)
        for cp in copies:
            cp.wait()

    return _comm_call(body, name, arrs, [_sds((a.shape[0], a.shape[1] // 2, a.shape[2]), a.dtype) for a in arrs],
                      [pltpu.SemaphoreType.DMA((n,)), pltpu.SemaphoreType.DMA((n,))])


def _chip_exchange(arrs, name):
    n = len(arrs)

    def body(*refs):
        ins, outs = refs[:n], refs[n:2 * n]
        send, recv = refs[2 * n:]
        x, y, c, chips = _place()
        me = 2 * x + y
        copies = []
        for a in range(n):
            for k, (px, py) in enumerate(chips):
                r = pltpu.make_async_remote_copy(src_ref=ins[a].at[2 * px + py], dst_ref=outs[a].at[me], send_sem=send.at[a, k],
                                                 recv_sem=recv.at[a, k], device_id=(px, py, c), device_id_type=MESH)
                r.start()
                copies.append(r)
        for cp in copies:
            cp.wait()

    return _comm_call(body, name, arrs, [_sds(a.shape, a.dtype) for a in arrs],
                      [pltpu.SemaphoreType.DMA((n, 3)), pltpu.SemaphoreType.DMA((n, 3))])


def _chip_exchange_start(arrs, name):
    n = len(arrs)

    def body(*refs):
        ins, lands = refs[:n], refs[n:2 * n]
        send, recv, token = refs[2 * n], refs[2 * n + 1], refs[-1]
        x, y, c, chips = _place()
        me = 2 * x + y
        for a in range(n):
            for k, (px, py) in enumerate(chips):
                pltpu.make_async_remote_copy(src_ref=ins[a].at[2 * px + py], dst_ref=lands[a].at[me], send_sem=send.at[3 * a + k],
                                             recv_sem=recv.at[3 * a + k], device_id=(px, py, c), device_id_type=MESH).start()
        token[...] = jnp.zeros_like(token)

    hbm = lambda v: pltpu.with_memory_space_constraint(v, pltpu.HBM)
    out = pl.pallas_call(
        body, name=name,
        out_shape=(pltpu.SemaphoreType.DMA((3 * n,)), pltpu.SemaphoreType.DMA((3 * n,)), *[pltpu.HBM(a.shape, a.dtype) for a in arrs],
                   *[pltpu.HBM(a.shape, a.dtype) for a in arrs], _sds((8, HD))),
        in_specs=[HBM] * (2 * n), out_specs=(SEM, SEM, *[HBM] * (2 * n), pl.BlockSpec(memory_space=pltpu.VMEM)),
        input_output_aliases={i: 2 + i for i in range(2 * n)},
        compiler_params=pltpu.CompilerParams(has_side_effects=DATAFLOW),
    )(*[hbm(a) for a in arrs], *[hbm(lax.empty(a.shape, a.dtype)) for a in arrs])
    return out[0], out[1], list(out[2:2 + n]), list(out[2 + n:2 + 2 * n]), out[-1]


def _chip_exchange_wait(send, recv, srcs, lands, after, name):
    n = len(srcs)

    def body(*refs):
        ins, lands_ = refs[:n], refs[n:2 * n]
        send_, recv_ = refs[2 * n], refs[2 * n + 1]
        x, y, c, chips = _place()
        for a in range(n):
            for k, (px, py) in enumerate(chips):
                cp = pltpu.make_async_remote_copy(src_ref=ins[a].at[2 * px + py], dst_ref=lands_[a].at[2 * px + py], send_sem=send_.at[3 * a + k],
                                                  recv_sem=recv_.at[3 * a + k], device_id=(px, py, c), device_id_type=MESH)
                cp.wait_send()
                cp.wait_recv()

    out = pl.pallas_call(
        body, name=name, out_shape=[pltpu.HBM(v.shape, v.dtype) for v in list(srcs) + list(lands)],
        in_specs=[HBM] * (2 * n) + [SEM, SEM, ANY], out_specs=[HBM] * (2 * n), input_output_aliases={i: i for i in range(2 * n)},
        compiler_params=pltpu.CompilerParams(has_side_effects=DATAFLOW),
    )(*srcs, *lands, send, recv, after)
    return list(out[n:])


def _pair_swap(arrs, name):
    n = len(arrs)

    def body(*refs):
        ins, outs = refs[:n], refs[n:2 * n]
        send, recv = refs[2 * n:]
        x, y, c, _ = _place()
        copies = []
        for a in range(n):
            cp = pltpu.make_async_remote_copy(src_ref=ins[a], dst_ref=outs[a], send_sem=send.at[a], recv_sem=recv.at[a],
                                              device_id=(x, y, 1 - c), device_id_type=MESH)
            cp.start()
            copies.append(cp)
        for cp in copies:
            cp.wait()

    return _comm_call(body, name, arrs, [_sds(a.shape, a.dtype) for a in arrs],
                      [pltpu.SemaphoreType.DMA((n,)), pltpu.SemaphoreType.DMA((n,))])


def _all_gather_devices(v, name):
    def body(v_ref, o_ref, send, recv, loc):
        x, y, c, _ = _place()
        me = 4 * x + 2 * y + c
        own = pltpu.make_async_copy(v_ref, o_ref.at[me], loc)
        own.start()
        copies = [own]
        for k in range(1, 8):
            fx, fy, fc = (k >> 2) & 1, (k >> 1) & 1, k & 1
            peer = (x ^ fx, y ^ fy, c ^ fc)
            r = pltpu.make_async_remote_copy(src_ref=v_ref, dst_ref=o_ref.at[me], send_sem=send.at[k - 1],
                                             recv_sem=recv.at[k - 1], device_id=peer, device_id_type=MESH)
            r.start()
            copies.append(r)
        for cp in copies:
            cp.wait()

    return _comm_call(body, name, [v], [_sds((8,) + v.shape, v.dtype)],
                      [pltpu.SemaphoreType.DMA((7,)), pltpu.SemaphoreType.DMA((7,)), pltpu.SemaphoreType.DMA])[0]


def _row_tile(r):
    return next((b for b in (512, 384, 256, 128, 64, 32, 16) if r % b == 0), r)


def _add2(a, b, out_dtype, name):
    r, w = a.shape
    br = _row_tile(r)

    def body(a_ref, b_ref, o_ref):
        o_ref[...] = (a_ref[...].astype(F32) + b_ref[...].astype(F32)).astype(out_dtype)

    blk = pl.BlockSpec((br, w), lambda i: (i, 0))
    return _pc(body, name, (r // br,), [blk, blk], blk, _sds((r, w), out_dtype))(a, b)


def _sum_slots(a, out_dtype, name, extra=None):
    n, r, w = a.shape
    br = _row_tile(r)

    def body(*refs):
        a_ref, o_ref = refs[0], refs[-1]
        acc = a_ref[0].astype(F32)
        for s in range(1, n):
            acc = acc + a_ref[s].astype(F32)
        if extra is not None:
            acc = acc + refs[1][...].astype(F32)
        o_ref[...] = acc.astype(out_dtype)

    ins = [a] + ([extra] if extra is not None else [])
    specs = [pl.BlockSpec((n, br, w), lambda i: (0, i, 0))] + ([pl.BlockSpec((br, w), lambda i: (i, 0))] if extra is not None else [])
    return _pc(body, name, (r // br,), specs, pl.BlockSpec((br, w), lambda i: (i, 0)), _sds((r, w), out_dtype))(*ins)


SMALL = ["norm_mix_g", "norm_xattn_g", "norm_mlp_g", "final_norm_g", "mem_norm_g", "hgrn_lb_logits", "mlstm_norm_g",
         "hgrn_norm_g", "c_qnorm_g", "c_knorm_g", "ab_gate_b", "c_fgate_b"]
SMALL_ROWS = 16


def _pack_small(parts):
    flat = jnp.concatenate([p.reshape(-1).astype(F32) for p in parts])
    return jnp.pad(flat, (0, SMALL_ROWS * D - flat.shape[0])).reshape(SMALL_ROWS, D)


def _unpack_small(buf, shapes):
    flat, out, off = buf.reshape(-1), [], 0
    for s in shapes:
        n = 1
        for d in s:
            n *= d
        out.append(flat[off:off + n].reshape(s))
        off += n
    return out


def kernel(x, mem, norm_mix_g, norm_xattn_g, norm_mlp_g, final_norm_g, ab_w_in, ab_conv_w, ab_gate_b, hgrn_lb_logits, mlstm_norm_g, hgrn_norm_g, ab_w_out, c_w_in, c_fgate_b, c_qnorm_g, c_knorm_g, c_w_out, mem_norm_g, mem_w_kv, xa_w_q, xa_w_o, mlp_w1, mlp_w2, loss_target, m_norm_mix_g, m_norm_xattn_g, m_norm_mlp_g, m_final_norm_g, m_ab_w_in, m_ab_conv_w, m_ab_gate_b, m_hgrn_lb_logits, m_mlstm_norm_g, m_hgrn_norm_g, m_ab_w_out, m_c_w_in, m_c_fgate_b, m_c_qnorm_g, m_c_knorm_g, m_c_w_out, m_mem_norm_g, m_mem_w_kv, m_xa_w_q, m_xa_w_o, m_mlp_w1, m_mlp_w2, v_norm_mix_g, v_norm_xattn_g, v_norm_mlp_g, v_final_norm_g, v_ab_w_in, v_ab_conv_w, v_ab_gate_b, v_hgrn_lb_logits, v_mlstm_norm_g, v_hgrn_norm_g, v_ab_w_out, v_c_w_in, v_c_fgate_b, v_c_qnorm_g, v_c_knorm_g, v_c_w_out, v_mem_norm_g, v_mem_w_kv, v_xa_w_q, v_xa_w_o, v_mlp_w1, v_mlp_w2):
    A = dict(locals())
    chip = 2 * lax.axis_index("x") + lax.axis_index("y")

    big = ["ab_w_in", "c_w_in", "ab_w_out", "c_w_out", "mem_w_kv", "xa_w_q", "xa_w_o", "mlp_w1", "mlp_w2"]
    shard2d = {"ab_w_in": (D, 1026), "c_w_in": (D, 1026), "ab_w_out": (256, D), "c_w_out": (256, D), "mem_w_kv": (D, 512),
               "xa_w_q": (512, D), "xa_w_o": (512, D), "mlp_w1": (2 * D, D), "mlp_w2": (2 * D, D)}
    shard16 = lambda n: A[n].reshape(shard2d[n]).astype(BF16)
    own_slot = lambda gs, os: [lax.dynamic_update_index_in_dim(g, o, chip, 0) for g, o in zip(gs, os)]
    cols = lambda g: jnp.concatenate([g[k] for k in range(NCHIP)], axis=1)
    per_layer = lambda g: g.reshape(NCHIP, 2, -1, D).transpose(1, 0, 2, 3)
    first = [shard16("ab_w_in"), jnp.pad(ab_conv_w[0], ((0, 16 - CONV_W), (0, 0)))]
    g_in0, g_conv = own_slot(_gather_weights(first, "gather_first"), first)
    W = dict(w_in0=_pack_w_in0(cols(g_in0)))
    rest_names = ["c_w_in", "ab_w_out", "c_w_out", "xa_w_q", "xa_w_o", "mlp_w1", "mlp_w2", "mem_w_kv"]
    rest = [shard16(n) for n in rest_names]
    send_s, recv_s, srcs, lands, token = _gather_start(rest, g_conv, "gather_rest_start")

    def late_weights(after):
        got = _pair_forward(_gather_wait(send_s, recv_s, srcs, lands, after, "gather_rest_wait"), "gather_rest_forward")
        gw = dict(zip(rest_names, own_slot(got, rest)))
        return dict(w_in1=_pack_w_in1(cols(gw["c_w_in"])), w_out0=gw["ab_w_out"].reshape(D, D), w_out1=gw["c_w_out"].reshape(D, D),
                    wkv_s=gw["mem_w_kv"],
                    wq=per_layer(gw["xa_w_q"]).reshape(2, D, D), wo=per_layer(gw["xa_w_o"]).reshape(2, D, D),
                    w1s=gw["mlp_w1"].reshape(NCHIP, 2, D, D), w2=gw["mlp_w2"].reshape(NCHIP, 2, D, D))

    S = dict(norm_mix_g=norm_mix_g + token[0, 0], norm_xattn_g=norm_xattn_g, norm_mlp_g=norm_mlp_g, final_norm_g=final_norm_g,
             conv_w=cols(g_conv[:, :CONV_W]), gate_b=ab_gate_b, lb_logits=hgrn_lb_logits, mlstm_norm_g=mlstm_norm_g,
             hgrn_norm_g=hgrn_norm_g, c_fgate_b=c_fgate_b, c_qnorm_g=c_qnorm_g, c_knorm_g=c_knorm_g, mem_norm_g=mem_norm_g)

    core = lax.axis_index("c")
    by_rows = lambda g: g.reshape(NCHIP, -1, D)

    def stack_cols(g):
        return jnp.stack([g[:, 1026 * k:1026 * (k + 1)] for k in range(NCHIP)])

    def pair_sums(arrs, tag):
        theirs = _pair_exchange(arrs, f"pair_exchange_{tag}")
        out = []
        for i, (a, th) in enumerate(zip(arrs, theirs)):
            h = a.shape[1] // 2
            mine = lax.dynamic_slice_in_dim(a, core * h, h, axis=1)
            out.append(_add2(mine.reshape(-1, a.shape[2]), th.reshape(-1, a.shape[2]), BF16, f"pair_sum_{tag}{i}").reshape(th.shape))
        return out

    def chip_sums(psums, from_chips, tag):
        out = []
        for i, (f, p) in enumerate(zip(from_chips, psums)):
            f = lax.dynamic_update_index_in_dim(f, lax.dynamic_index_in_dim(p, chip, 0, keepdims=False), chip, 0)
            out.append(_sum_slots(f, F32, f"chip_sum_{tag}{i}"))
        return out

    started = {}

    def grads_hook(stage, g):
        if stage == "layer1":
            arrs = [jnp.concatenate([by_rows(g["w_out"]), by_rows(g["wq"]), by_rows(g["wo"]), g["w1"], by_rows(g["w2"])], axis=1),
                    stack_cols(_unpack_w_in1(g["w_in"]))]
        else:
            arrs = [jnp.concatenate([by_rows(g["wq"]), by_rows(g["wo"]), g["w1"], by_rows(g["w2"])], axis=1), g["wkv"]]
        psums = pair_sums(arrs, stage)
        *handles, token = _chip_exchange_start(psums, f"chip_exchange_start_{stage}")
        started[stage] = (psums, handles)
        return token[0, 0]

    lossp, dx, G = _local_step(x[0], mem[0], loss_target[0], W, S, late_weights, grads_hook)

    gsmall = {"norm_mix_g": G["norm_mix_g"], "norm_xattn_g": G["norm_xattn_g"], "norm_mlp_g": G["norm_mlp_g"],
              "final_norm_g": G["final_norm_g"], "mem_norm_g": G["mem_norm_g"], "hgrn_lb_logits": G["lb_logits"],
              "mlstm_norm_g": G["mlstm_norm_g"], "hgrn_norm_g": G["hgrn_norm_g"], "c_qnorm_g": G["c_qnorm_g"],
              "c_knorm_g": G["c_knorm_g"], "ab_gate_b": G["gate_b"], "c_fgate_b": G["c_fgate_b"]}
    packed = _pack_small([gsmall[n] for n in SMALL] + [G["conv_w"], lossp])
    red = _sum_slots(_all_gather_devices(packed, "gather_small"), F32, "sum_small")
    small_shapes = [A[n].shape for n in SMALL]
    *gs, gconv, loss = _unpack_small(red, small_shapes + [(CONV_W, D), ()])
    gs = dict(zip(SMALL, gs))
    gconv = lax.dynamic_slice_in_dim(gconv, chip * 256, 256, axis=1)[None]

    last = pair_sums([by_rows(G["w_out0"]), stack_cols(_unpack_w_in0(G["w_in0"]))], "last")
    rhalf = chip_sums(last, _chip_exchange(last, "chip_exchange_last"), "last")
    for stage in ("layer1", "layer0_mlp_xattn"):
        psums, handles = started[stage]
        rhalf += chip_sums(psums, _chip_exchange_wait(*handles, dx, f"chip_exchange_wait_{stage}"), stage)
    other = _pair_swap(rhalf, "pair_swap")
    r_out0, r_in0, r_l1, r_in1, r_l0, r_kv = [
        jnp.where(core == 0, jnp.concatenate([m_, o_], axis=0), jnp.concatenate([o_, m_], axis=0)) for m_, o_ in zip(rhalf, other)]
    gbig = {"ab_w_in": r_in0, "c_w_in": r_in1, "mem_w_kv": r_kv, "ab_w_out": r_out0, "c_w_out": r_l1[0:256],
            "xa_w_q": jnp.concatenate([r_l0[0:256], r_l1[256:512]], axis=0),
            "xa_w_o": jnp.concatenate([r_l0[256:512], r_l1[512:768]], axis=0),
            "mlp_w1": jnp.concatenate([r_l0[512:1536], r_l1[768:1792]], axis=0),
            "mlp_w2": jnp.concatenate([r_l0[1536:2560], r_l1[1792:2816]], axis=0)}

    out_g, out_d, out_m, out_v = {}, {}, {}, {}
    for n in big:
        d_, m_, v_ = _adam(A[n].reshape(shard2d[n]), gbig[n], A["m_" + n].reshape(shard2d[n]), A["v_" + n].reshape(shard2d[n]), "adam_" + n)
        out_g[n] = gbig[n].reshape(A[n].shape)
        out_d[n], out_m[n], out_v[n] = d_.reshape(A[n].shape), m_.reshape(A[n].shape), v_.reshape(A[n].shape)
    sd, sm, sv = _adam(_pack_small([A[n] for n in SMALL]), _pack_small([gs[n] for n in SMALL]),
                       _pack_small([A["m_" + n] for n in SMALL]), _pack_small([A["v_" + n] for n in SMALL]), "adam_small")
    for n, d_, m_, v_ in zip(SMALL, _unpack_small(sd, small_shapes), _unpack_small(sm, small_shapes), _unpack_small(sv, small_shapes)):
        out_g[n], out_d[n], out_m[n], out_v[n] = gs[n], d_, m_, v_
    cd, cm_, cv = _adam(ab_conv_w[0], gconv[0], m_ab_conv_w[0], v_ab_conv_w[0], "adam_conv")
    out_g["ab_conv_w"], out_d["ab_conv_w"], out_m["ab_conv_w"], out_v["ab_conv_w"] = gconv, cd[None], cm_[None], cv[None]

    order = ["norm_mix_g", "norm_xattn_g", "norm_mlp_g", "final_norm_g", "ab_w_in", "ab_conv_w", "ab_gate_b", "hgrn_lb_logits",
             "mlstm_norm_g", "hgrn_norm_g", "ab_w_out", "c_w_in", "c_fgate_b", "c_qnorm_g", "c_knorm_g", "c_w_out", "mem_norm_g",
             "mem_w_kv", "xa_w_q", "xa_w_o", "mlp_w1", "mlp_w2"]
    return (loss, dx[None], *[out_g[n] for n in order], *[out_d[n] for n in order], *[out_m[n] for n in order],
            *[out_v[n] for n in order])
```

```python
import functools

import jax
import jax.numpy as jnp
from jax import lax
from jax.experimental import pallas as pl
from jax.experimental.pallas import tpu as pltpu

F32 = jnp.float32
BF16 = jnp.bfloat16
EPS = 1e-6
D = 1024
CHUNK = 64
REC_CHUNKS = 4
HD = 128
XD = 256
NEG = -1e30
VMEM_LIMIT_V7X = 56 * 1024 * 1024
ADAM_LR, ADAM_B1, ADAM_B2, ADAM_EPS, ADAM_WD, ADAM_STEP = 0.001, 0.9, 0.999, 1e-08, 0.01, 10
MESH = pl.DeviceIdType.MESH


def _pc(body, name, grid, in_specs, out_specs, out_shape, scratch=(), **kw):
    return pl.pallas_call(
        body, name=name, grid=grid, in_specs=in_specs, out_specs=out_specs, out_shape=out_shape,
        scratch_shapes=scratch,
        compiler_params=pltpu.CompilerParams(
            dimension_semantics=("arbitrary",) * len(grid), vmem_limit_bytes=VMEM_LIMIT_V7X), **kw)


def _sds(shape, dtype=F32):
    return jax.ShapeDtypeStruct(shape, dtype)


def _blk(n, target):
    return max(b for b in range(128, max(target, 128) + 1, 128) if n % b == 0)


def _dot(a, b, dims):
    return lax.dot_general(a, b, (dims, ((), ())), preferred_element_type=F32)


def _nn(a, b):
    return _dot(a, b, ((1,), (0,)))


def _nt(a, b):
    return _dot(a, b, ((1,), (1,)))


def _tn(a, b):
    return _dot(a, b, ((0,), (0,)))


def _sigmoid(x):
    return 1.0 / (1.0 + jnp.exp(-x))


def _log_sigmoid(x):
    return jnp.minimum(x, 0.0) - jnp.log(1.0 + jnp.exp(-jnp.abs(x)))


def _rstd(x):
    return lax.rsqrt(jnp.mean(x * x, axis=-1, keepdims=True) + EPS)


def _rms_bwd(du, x, g):
    r = _rstd(x)
    xh = x * r
    dxh = du * g
    dx = r * (dxh - xh * jnp.mean(dxh * xh, axis=-1, keepdims=True))
    return dx, du * xh


def _norm_mm(h, g, w, name, bm=1024, bn=512):
    t, n = h.shape[0], w.shape[1]
    bm, bn = min(bm, t), _blk(n, 3 * bn)

    def body(h_ref, g_ref, w_ref, z_ref, u_ref):
        @pl.when(pl.program_id(1) == 0)
        def _():
            x = h_ref[...]
            u_ref[...] = (x * _rstd(x) * g_ref[...]).astype(BF16)
        z_ref[...] = _nn(u_ref[...], w_ref[...])

    return _pc(body, name, (t // bm, n // bn),
               [pl.BlockSpec((bm, D), lambda i, j: (i, 0)), pl.BlockSpec((1, D), lambda i, j: (0, 0)),
                pl.BlockSpec((D, bn), lambda i, j: (0, j))],
               [pl.BlockSpec((bm, bn), lambda i, j: (i, j)), pl.BlockSpec((bm, D), lambda i, j: (i, 0))],
               [_sds((t, n)), _sds((t, D), BF16)])(h, g, w)


def _mm_tn(a, b, name, bm=1024, bn=1024, bt=4096, col_chips=None):
    t, m = a.shape
    n = b.shape[1]
    bm, bn, bt = _blk(m, bm), (n // col_chips if col_chips else _blk(n, bn + bn // 2)), min(bt, t)
    if (m // bm) * (n // bn) == 1 and bt >= 1024:
        bt //= 4
    nt = t // bt

    def body(a_ref, b_ref, o_ref, acc):
        k = pl.program_id(2)

        @pl.when(k == 0)
        def _():
            acc[...] = jnp.zeros_like(acc)

        acc[...] += _tn(a_ref[...].astype(BF16), b_ref[...].astype(BF16))

        @pl.when(k == nt - 1)
        def _():
            o_ref[...] = acc[...].astype(BF16)

    if col_chips:
        out_spec, out_shape = pl.BlockSpec((None, bm, bn), lambda i, j, k: (j, i, 0)), _sds((col_chips, m, bn), BF16)
    else:
        out_spec, out_shape = pl.BlockSpec((bm, bn), lambda i, j, k: (i, j)), _sds((m, n), BF16)
    return _pc(body, name, (m // bm, n // bn, nt),
               [pl.BlockSpec((bt, bm), lambda i, j, k: (k, i)), pl.BlockSpec((bt, bn), lambda i, j, k: (k, j))],
               out_spec, out_shape, scratch=[pltpu.VMEM((bm, bn), F32)])(a, b)


def _bwd_in(dz, w, h, g, dh, name, bm=1024, bk=1024):
    t, n = dz.shape
    if n > 2 * bk:
        bm, bk = min(bm // 2, t), n
    else:
        bm, bk = min(bm, t), _blk(n, bk + bk // 2)
    nk = n // bk

    def body(dz_ref, w_ref, h_ref, g_ref, dh_ref, o_ref, dg_ref, acc):
        i, k = pl.program_id(0), pl.program_id(1)

        @pl.when(k == 0)
        def _():
            acc[...] = jnp.zeros_like(acc)

        @pl.when((i == 0) & (k == 0))
        def _():
            dg_ref[...] = jnp.zeros_like(dg_ref)

        acc[...] += _nt(dz_ref[...], w_ref[...])

        @pl.when(k == nk - 1)
        def _():
            dx, dgr = _rms_bwd(acc[...], h_ref[...], g_ref[...])
            o_ref[...] = dh_ref[...] + dx
            dg_ref[...] += jnp.sum(dgr, axis=0, keepdims=True)

    return _pc(body, name, (t // bm, nk),
               [pl.BlockSpec((bm, bk), lambda i, k: (i, k)), pl.BlockSpec((D, bk), lambda i, k: (0, k)),
                pl.BlockSpec((bm, D), lambda i, k: (i, 0)), pl.BlockSpec((1, D), lambda i, k: (0, 0)),
                pl.BlockSpec((bm, D), lambda i, k: (i, 0))],
               [pl.BlockSpec((bm, D), lambda i, k: (i, 0)), pl.BlockSpec((1, D), lambda i, k: (0, 0))],
               [_sds((t, D)), _sds((1, D))], scratch=[pltpu.VMEM((bm, D), F32)])(dz, w, h, g, dh)


def _mlp_fwd(h, g, w1s, w2, l, name, bm=1024):
    t = h.shape[0]
    bm = min(bm, t)
    nk = w1s.shape[0]

    def body(h_ref, g_ref, w1_ref, w2_ref, o_ref, a_ref, u_ref, acc):
        k = pl.program_id(1)

        @pl.when(k == 0)
        def _():
            x = h_ref[...]
            u_ref[...] = (x * _rstd(x) * g_ref[...]).astype(BF16)
            acc[...] = jnp.zeros_like(acc)

        a = _nn(u_ref[...], w1_ref[...])
        a_ref[...] = a
        r = jnp.square(jnp.maximum(a, 0.0)).astype(BF16)
        acc[...] += _nn(r, w2_ref[...])

        @pl.when(k == nk - 1)
        def _():
            o_ref[...] = h_ref[...] + acc[...]

    return _pc(body, name, (t // bm, nk),
               [pl.BlockSpec((bm, D), lambda i, k: (i, 0)), pl.BlockSpec((1, D), lambda i, k: (0, 0)),
                pl.BlockSpec((None, None, D, D), lambda i, k: (k, l, 0, 0)), pl.BlockSpec((None, None, D, D), lambda i, k: (k, l, 0, 0))],
               [pl.BlockSpec((bm, D), lambda i, k: (i, 0)), pl.BlockSpec((bm, D), lambda i, k: (i, k)),
                pl.BlockSpec((bm, D), lambda i, k: (i, 0))],
               [_sds((t, D)), _sds((t, nk * D)), _sds((t, D), BF16)],
               scratch=[pltpu.VMEM((bm, D), F32)])(h, g, w1s, w2)


def _mlp_bwd(dh, a, w1s, w2, l, h, g, name, bm=512):
    t = h.shape[0]
    bm = min(bm, t)
    nk = w1s.shape[0]

    def body(dh_ref, a_ref, w1_ref, w2_ref, h_ref, g_ref, o_ref, da_ref, r_ref, dg_ref, acc):
        i, k = pl.program_id(0), pl.program_id(1)

        @pl.when(k == 0)
        def _():
            acc[...] = jnp.zeros_like(acc)

        @pl.when((i == 0) & (k == 0))
        def _():
            dg_ref[...] = jnp.zeros_like(dg_ref)

        ap = jnp.maximum(a_ref[...], 0.0)
        r_ref[...] = jnp.square(ap).astype(BF16)
        dr = _nt(dh_ref[...].astype(BF16), w2_ref[...])
        da = (dr * (2.0 * ap)).astype(BF16)
        da_ref[...] = da
        acc[...] += _nt(da, w1_ref[...])

        @pl.when(k == nk - 1)
        def _():
            dx, dgr = _rms_bwd(acc[...], h_ref[...], g_ref[...])
            o_ref[...] = dh_ref[...] + dx
            dg_ref[...] += jnp.sum(dgr, axis=0, keepdims=True)

    return _pc(body, name, (t // bm, nk),
               [pl.BlockSpec((bm, D), lambda i, k: (i, 0)), pl.BlockSpec((bm, D), lambda i, k: (i, k)),
                pl.BlockSpec((None, None, D, D), lambda i, k: (k, l, 0, 0)), pl.BlockSpec((None, None, D, D), lambda i, k: (k, l, 0, 0)),
                pl.BlockSpec((bm, D), lambda i, k: (i, 0)), pl.BlockSpec((1, D), lambda i, k: (0, 0))],
               [pl.BlockSpec((bm, D), lambda i, k: (i, 0)), pl.BlockSpec((bm, D), lambda i, k: (i, k)),
                pl.BlockSpec((bm, D), lambda i, k: (i, k)), pl.BlockSpec((1, D), lambda i, k: (0, 0))],
               [_sds((t, D)), _sds((t, nk * D), BF16), _sds((t, nk * D), BF16), _sds((1, D))],
               scratch=[pltpu.VMEM((bm, D), F32)])(dh, a, w1s, w2, h, g)


def _rows_of(x):
    return lax.broadcasted_iota(jnp.int32, x.shape, 0)


def _shift_down(x, s):
    if s == 0:
        return x
    return jnp.where(_rows_of(x) >= s, pltpu.roll(x, s, 0), 0.0)


def _shift_up(x, s):
    if s == 0:
        return x
    n = x.shape[0]
    return jnp.where(_rows_of(x) < n - s, pltpu.roll(x, n - s, 0), 0.0)


def _cumsum_rows(x):
    n, s = x.shape[0], 1
    while s < n:
        x = x + _shift_down(x, s)
        s *= 2
    return x


def _rcumsum_rows(x):
    n, s = x.shape[0], 1
    while s < n:
        x = x + _shift_up(x, s)
        s *= 2
    return x


def _silu(x):
    return x * _sigmoid(x)


def _dsilu(x):
    s = _sigmoid(x)
    return s * (1.0 + x * (1.0 - s))


CONV_W = 4


def _conv_pre(u, w):
    y = _shift_down(u, CONV_W - 1) * w[0:1, :]
    for j in range(1, CONV_W):
        y = y + _shift_down(u, CONV_W - 1 - j) * w[j:j + 1, :]
    return y


def _conv_fwd(z0, cw, name):
    t = z0.shape[0]

    def body(u_ref, w_ref, o_ref):
        o_ref[...] = _silu(_conv_pre(u_ref[...], w_ref[...]))

    return _pc(body, name, (2 * 512 // HD,),
               [pl.BlockSpec((t, HD), lambda c: (0, c)), pl.BlockSpec((CONV_W, HD), lambda c: (0, c))],
               pl.BlockSpec((t, HD), lambda c: (0, c)), _sds((t, 1024)))(z0, cw)


def _conv_bwd(z0, cw, dy, name):
    t = z0.shape[0]

    def body(u_ref, w_ref, dy_ref, du_ref, dw_ref):
        u, w = u_ref[...], w_ref[...]
        dpre = dy_ref[...] * _dsilu(_conv_pre(u, w))
        du = _shift_up(dpre, CONV_W - 1) * w[0:1, :]
        for j in range(1, CONV_W):
            du = du + _shift_up(dpre, CONV_W - 1 - j) * w[j:j + 1, :]
        du_ref[...] = du.astype(BF16)
        for j in range(CONV_W):
            dw_ref[j:j + 1, :] = jnp.sum(dpre * _shift_down(u, CONV_W - 1 - j), axis=0, keepdims=True)

    return _pc(body, name, (2 * 512 // HD,),
               [pl.BlockSpec((t, HD), lambda c: (0, c)), pl.BlockSpec((CONV_W, HD), lambda c: (0, c)),
                pl.BlockSpec((t, HD), lambda c: (0, c))],
               [pl.BlockSpec((t, HD), lambda c: (0, c)), pl.BlockSpec((CONV_W, HD), lambda c: (0, c))],
               [_sds((t, 1024), BF16), _sds((CONV_W, 1024))])(z0, cw, dy)


def _mlstm_gates(gate, bias, m_in):
    L = gate.shape[0]
    r = lax.broadcasted_iota(jnp.int32, (L, L), 0)
    c = lax.broadcasted_iota(jnp.int32, (L, L), 1)
    eye, tril = r == c, c <= r
    i_col = gate[:, 0:1] + bias[:, 0:1]
    f_col = gate[:, 1:2] + bias[:, 1:2]
    logf_col = _log_sigmoid(f_col)
    logf_row = jnp.sum(jnp.where(eye, logf_col, 0.0), axis=0, keepdims=True)
    i_row = jnp.sum(jnp.where(eye, i_col, 0.0), axis=0, keepdims=True)
    b_col = jnp.sum(jnp.where(tril, logf_row, 0.0), axis=1, keepdims=True)
    b_row = jnp.sum(jnp.where(r <= c, logf_col, 0.0), axis=0, keepdims=True)
    logd = jnp.where(tril, b_col - b_row + i_row, NEG)
    inter = b_col + m_in
    m_t = jnp.maximum(inter, jnp.max(logd, axis=1, keepdims=True))
    w_t = jnp.exp(inter - m_t)
    dm = jnp.exp(logd - m_t)
    b_last = b_col[L - 1:L, :]
    log_in = b_last - b_col + i_col
    m_new = jnp.maximum(b_last + m_in, jnp.max(log_in, axis=0, keepdims=True))
    w_col = jnp.exp(log_in - m_new)
    decay = jnp.exp(b_last + m_in - m_new)
    return dict(eye=eye, r=r, c=c, f_col=f_col, m_t=m_t, w_t=w_t, dm=dm, m_new=m_new, w_col=w_col, decay=decay)


def _mlstm_fwd(qk, z0, gates, bias, name):
    t = qk.shape[0]
    nc, nh, L = t // CHUNK, 4, CHUNK
    scale = HD ** -0.5

    def body(q_ref, k_ref, v_ref, g_ref, b_ref, h_ref, cs_ref, ns_ref, ms_ref, c_s, n_s, m_s):
        @pl.when(pl.program_id(0) == 0)
        def _():
            c_s[...] = jnp.zeros_like(c_s)
            n_s[...] = jnp.zeros_like(n_s)
            m_s[...] = jnp.zeros_like(m_s)

        for hd in range(nh):
            sl = slice(hd * HD, (hd + 1) * HD)
            cm, nv, m_in = c_s[hd], n_s[hd], m_s[hd]
            for ck in range(cps):
                rows = slice(ck * L, (ck + 1) * L)
                cs_ref[hd, ck] = cm
                ns_ref[hd, ck] = nv
                ms_ref[hd, ck] = jnp.broadcast_to(m_in, (1, HD))
                q, kh, v = q_ref[rows, sl], k_ref[rows, sl] * scale, v_ref[rows, sl]
                G = _mlstm_gates(g_ref[hd, rows, :], b_ref[hd], m_in)
                qb, kb, vb = q.astype(BF16), kh.astype(BF16), v.astype(BF16)
                sc = _nt(qb, kb) * G["dm"]
                num = _nn(sc.astype(BF16), vb) + G["w_t"] * _nn(qb, cm.astype(BF16))
                den = jnp.sum(sc, axis=1, keepdims=True) + G["w_t"] * jnp.sum(q * nv, axis=1, keepdims=True)
                h_ref[rows, sl] = num / jnp.maximum(jnp.abs(den), jnp.exp(-G["m_t"]))
                wk = G["w_col"] * kh
                cm = G["decay"] * cm + _tn(wk.astype(BF16), vb)
                nv = G["decay"] * nv + jnp.sum(wk, axis=0, keepdims=True)
                m_in = G["m_new"]
            c_s[hd], n_s[hd], m_s[hd] = cm, nv, m_in

    cps = REC_CHUNKS
    hspec = lambda blk: pl.BlockSpec((cps * L, 512), lambda j: (j, blk))
    st = lambda r: pl.BlockSpec((nh, cps, r, HD), lambda j: (0, j, 0, 0))
    return _pc(body, name, (nc // cps,),
               [hspec(0), hspec(1), hspec(2), pl.BlockSpec((nh, cps * L, 2), lambda j: (0, j, 0)),
                pl.BlockSpec((nh, 1, 2), lambda j: (0, 0, 0))],
               [hspec(0), st(HD), st(1), st(1)],
               [_sds((t, 512)), _sds((nh, nc, HD, HD)), _sds((nh, nc, 1, HD)), _sds((nh, nc, 1, HD))],
               scratch=[pltpu.VMEM((nh, HD, HD), F32), pltpu.VMEM((nh, 1, HD), F32), pltpu.VMEM((nh, 1, 1), F32)])(qk, qk, z0, gates, bias)


def _mlstm_bwd(qk, z0, gates, bias, cs, ns, ms, dh, name):
    t = qk.shape[0]
    nc, nh, L = t // CHUNK, 4, CHUNK
    scale = HD ** -0.5

    def body(q_ref, k_ref, v_ref, g_ref, b_ref, cs_ref, ns_ref, ms_ref, dh_ref, dq_ref, dk_ref, dv_ref, dg_ref, dc_s, dn_s):
        @pl.when(pl.program_id(0) == 0)
        def _():
            dc_s[...] = jnp.zeros_like(dc_s)
            dn_s[...] = jnp.zeros_like(dn_s)

        for ck in reversed(range(cps)):
            for hd in range(nh):
                one_head(hd, ck, slice(hd * HD, (hd + 1) * HD), slice(ck * L, (ck + 1) * L), q_ref, k_ref, v_ref, g_ref, b_ref,
                         cs_ref, ns_ref, ms_ref, dh_ref, dq_ref, dk_ref, dv_ref, dg_ref, dc_s, dn_s)

    def one_head(hd, ck, sl, rows, q_ref, k_ref, v_ref, g_ref, b_ref, cs_ref, ns_ref, ms_ref, dh_ref, dq_ref, dk_ref, dv_ref, dg_ref,
                 dc_s, dn_s):
        cm, nv, m_in = cs_ref[hd, ck], ns_ref[hd, ck], ms_ref[hd, ck][:, 0:1]
        q, kh, v = q_ref[rows, sl], k_ref[rows, sl] * scale, v_ref[rows, sl]
        G = _mlstm_gates(g_ref[hd, rows, :], b_ref[hd], m_in)
        w_t, dmat, w_col, decay = G["w_t"], G["dm"], G["w_col"], G["decay"]
        qb, kb, vb, cb = q.astype(BF16), kh.astype(BF16), v.astype(BF16), cm.astype(BF16)
        s = _nt(qb, kb)
        sc = s * dmat
        scb = sc.astype(BF16)
        qc = _nn(qb, cb)
        qn = jnp.sum(q * nv, axis=1, keepdims=True)
        num = _nn(scb, vb) + w_t * qc
        den = jnp.sum(sc, axis=1, keepdims=True) + w_t * qn
        e_m = jnp.exp(-G["m_t"])
        dnm = jnp.maximum(jnp.abs(den), e_m)
        dh_ = dh_ref[rows, sl]
        dnum = dh_ / dnm
        dden = jnp.where(jnp.abs(den) > e_m, -jnp.sum(dh_ * num, axis=1, keepdims=True) / (dnm * dnm) * jnp.sign(den), 0.0)
        dnumb = dnum.astype(BF16)
        dsc = _nt(dnumb, vb) + dden
        dv = _tn(scb, dnumb)
        wd = w_t * dnum
        wdb = wd.astype(BF16)
        ds = dsc * dmat
        dsb = ds.astype(BF16)
        dq = _nt(wdb, cb) + (w_t * dden) * nv + _nn(dsb, kb)
        dc_o = _tn(qb, wdb)
        dn_o = jnp.sum(q * (w_t * dden), axis=0, keepdims=True)
        dw = jnp.sum(dnum * qc, axis=1, keepdims=True) + dden * qn
        dkh = _tn(dsb, qb)
        dlogd = ds * s
        db_col = jnp.sum(dlogd, axis=1, keepdims=True) + dw * w_t
        csum = jnp.sum(dlogd, axis=0, keepdims=True)
        dcn, dnn = dc_s[hd], dn_s[hd]
        dcnb = dcn.astype(BF16)
        kdc = _nn(kb, dcnb)
        dws = jnp.sum(kdc * v, axis=1, keepdims=True) + jnp.sum(kh * dnn, axis=1, keepdims=True)
        dv = dv + w_col * kdc
        dkh = dkh + w_col * (_nt(vb, dcnb) + dnn)
        dlin = dws * w_col
        ddecay = jnp.sum(jnp.sum(dcn * cm, axis=1, keepdims=True), axis=0, keepdims=True) + jnp.sum(dnn * nv, axis=1, keepdims=True)
        dlast = ddecay * decay + jnp.sum(dlin, axis=0, keepdims=True)
        row_id = lax.broadcasted_iota(jnp.int32, (L, 1), 0)
        db_col = db_col - dlin + jnp.where(row_id == L - 1, dlast, 0.0)
        eye, r, c = G["eye"], G["r"], G["c"]
        di = dlin + jnp.sum(jnp.where(eye, csum, 0.0), axis=1, keepdims=True)
        db_row = jnp.sum(jnp.where(eye, db_col, 0.0), axis=0, keepdims=True) - csum
        dlogf = jnp.sum(jnp.where(c >= r, db_row, 0.0), axis=1, keepdims=True)
        dg_ref[hd, rows, 0:1] = di
        dg_ref[hd, rows, 1:2] = dlogf * (1.0 - _sigmoid(G["f_col"]))
        dq_ref[rows, sl] = dq
        dk_ref[rows, sl] = dkh * scale
        dv_ref[rows, sl] = dv
        dc_s[hd] = decay * dcn + dc_o
        dn_s[hd] = decay * dnn + dn_o

    cps = REC_CHUNKS
    rv = lambda j: nc // cps - 1 - j
    hspec = lambda blk: pl.BlockSpec((cps * L, 512), lambda j: (rv(j), blk))
    st = lambda r: pl.BlockSpec((nh, cps, r, HD), lambda j: (0, rv(j), 0, 0))
    gs = pl.BlockSpec((nh, cps * L, 2), lambda j: (0, rv(j), 0))
    return _pc(body, name, (nc // cps,),
               [hspec(0), hspec(1), hspec(2), gs, pl.BlockSpec((nh, 1, 2), lambda j: (0, 0, 0)),
                st(HD), st(1), st(1), hspec(0)],
               [hspec(0), hspec(0), hspec(0), gs],
               [_sds((t, 512)), _sds((t, 512)), _sds((t, 512)), _sds((nh, t, 2))],
               scratch=[pltpu.VMEM((nh, HD, HD), F32), pltpu.VMEM((nh, 1, HD), F32)])(qk, qk, z0, gates, bias, cs, ns, ms, dh)


def _hgrn_act(qb_, fb_, ib_, lg):
    lb = _sigmoid(lg[0:1, :] - lg[1:2, :])
    sg = _sigmoid(fb_)
    f = lb + (1.0 - lb) * sg
    return lb, sg, f, _silu(qb_), (1.0 - lb) * (1.0 - sg), _silu(ib_), _cumsum_rows(jnp.log(f))


HG_SUB = 16


def _hgrn_offdiag(q, k, b, r0):
    beta = b[r0 - 1:r0, :]
    e1 = jnp.exp(b[r0:r0 + HG_SUB, :] - beta)
    e2 = jnp.where(_rows_of(b) < r0, jnp.exp(jnp.minimum(beta - b, 0.0)), 0.0)
    return q[r0:r0 + HG_SUB, :] * e1, k * e2, e1, e2


def _hgrn_fwd(z0, lbl, name):
    t = z0.shape[0]
    nc, nh, L = t // CHUNK, 4, CHUNK

    def body(q_ref, f_ref, i_ref, l_ref, o_ref, ss_ref, st_s):
        @pl.when(pl.program_id(0) == 0)
        def _():
            st_s[...] = jnp.zeros_like(st_s)

        for hd in range(nh):
            sl = slice(hd * HD, (hd + 1) * HD)
            st = st_s[hd]
            for ck in range(cps):
                rows = slice(ck * L, (ck + 1) * L)
                ss_ref[hd, ck] = st
                _, _, _, q, k, v, b = _hgrn_act(q_ref[rows, sl], f_ref[rows, sl], i_ref[rows, sl], l_ref[:, sl])
                o = _nt((q * jnp.exp(b)).astype(BF16), st.astype(BF16))
                sub = _rows_of(b) & (HG_SUB - 1)
                o = o + jnp.sum(q * k, axis=1, keepdims=True) * v
                for dl in range(1, HG_SUB):
                    e = jnp.exp(jnp.where(sub >= dl, b - pltpu.roll(b, dl, 0), NEG))
                    a = jnp.sum(q * pltpu.roll(k, dl, 0) * e, axis=1, keepdims=True)
                    o = o + a * pltpu.roll(v, dl, 0)
                o_ref[rows, sl] = o
                vb = v.astype(BF16)
                for i in range(1, L // HG_SUB):
                    r0 = i * HG_SUB
                    qt, kt, _, _ = _hgrn_offdiag(q, k, b, r0)
                    a = _nt(qt.astype(BF16), kt.astype(BF16))
                    o_ref[ck * L + r0:ck * L + r0 + HG_SUB, sl] += _nn(a.astype(BF16), vb)
                bl = b[L - 1:L, :]
                st = st * jnp.exp(bl) + _tn(v.astype(BF16), (k * jnp.exp(bl - b)).astype(BF16))
            st_s[hd] = st

    cps = REC_CHUNKS
    hspec = lambda blk: pl.BlockSpec((cps * L, 512), lambda j: (j, blk))
    return _pc(body, name, (nc // cps,),
               [hspec(4), hspec(5), hspec(6), pl.BlockSpec((2, 512), lambda j: (0, 0))],
               [hspec(0), pl.BlockSpec((nh, cps, HD, HD), lambda j: (0, j, 0, 0))],
               [_sds((t, 512)), _sds((nh, nc, HD, HD))],
               scratch=[pltpu.VMEM((nh, HD, HD), F32)])(z0, z0, z0, lbl)


def _hgrn_bwd(z0, lbl, ss, do, name):
    t = z0.shape[0]
    nc, nh, L = t // CHUNK, 4, CHUNK

    def body(q_ref, f_ref, i_ref, l_ref, ss_ref, do_ref, dq_ref, df_ref, di_ref, dl_ref, dst_s, dlb_s, dq_a, dk_a, dv_a, db_a):
        @pl.when(pl.program_id(0) == 0)
        def _():
            dst_s[...] = jnp.zeros_like(dst_s)
            dlb_s[...] = jnp.zeros_like(dlb_s)

        for ck in reversed(range(cps)):
            for hd in range(nh):
                one_head(hd, ck, slice(hd * HD, (hd + 1) * HD), slice(ck * L, (ck + 1) * L), q_ref, f_ref, i_ref, l_ref, ss_ref, do_ref,
                         dq_ref, df_ref, di_ref, dl_ref, dst_s, dlb_s, dq_a.at[hd], dk_a.at[hd], dv_a.at[hd], db_a.at[hd])

    def one_head(hd, ck, sl, rs, q_ref, f_ref, i_ref, l_ref, ss_ref, do_ref, dq_ref, df_ref, di_ref, dl_ref, dst_s, dlb_s,
                 dq_a, dk_a, dv_a, db_a):
        st = ss_ref[hd, ck]
        qp, fp, ip = q_ref[rs, sl], f_ref[rs, sl], i_ref[rs, sl]
        lb, sg, f, q, k, v, b = _hgrn_act(qp, fp, ip, l_ref[:, sl])
        do_ = do_ref[rs, sl]
        dob, stb = do_.astype(BF16), st.astype(BF16)
        eb = jnp.exp(b)
        qe = q * eb
        dqe = _nn(dob, stb)
        dst_o = _tn(dob, qe.astype(BF16))
        dq = dqe * eb
        db = dqe * qe
        rows = _rows_of(b)
        sub = rows & (HG_SUB - 1)
        p0 = jnp.sum(do_ * v, axis=1, keepdims=True)
        dq = dq + p0 * k
        dk = p0 * q
        dv = jnp.sum(q * k, axis=1, keepdims=True) * do_
        for dl in range(1, HG_SUB):
            up = L - dl
            kd, vd = pltpu.roll(k, dl, 0), pltpu.roll(v, dl, 0)
            e = jnp.exp(jnp.where(sub >= dl, b - pltpu.roll(b, dl, 0), NEG))
            a = jnp.sum(q * kd * e, axis=1, keepdims=True)
            p = jnp.sum(do_ * vd, axis=1, keepdims=True) * e
            dq = dq + p * kd
            dkd = p * q
            dbb = dkd * kd
            dv = dv + pltpu.roll(a * do_, up, 0)
            dk = dk + pltpu.roll(dkd, up, 0)
            db = db + dbb - pltpu.roll(dbb, up, 0)
        dq_a[...], dk_a[...], dv_a[...], db_a[...] = dq, dk, dv, db
        vb = v.astype(BF16)
        for i in range(1, L // HG_SUB):
            r0 = i * HG_SUB
            blk = slice(r0, r0 + HG_SUB)
            qt, kt, e1, e2 = _hgrn_offdiag(q, k, b, r0)
            qtb, ktb, dob_i = qt.astype(BF16), kt.astype(BF16), do_[blk, :].astype(BF16)
            a = _nt(qtb, ktb).astype(BF16)
            da = _nt(dob_i, vb).astype(BF16)
            dv_a[...] += _tn(a, dob_i)
            dqt = _nn(da, ktb)
            dkt = _tn(da, qtb)
            dq_a[blk, :] += dqt * e1
            t1, t2 = dqt * qt, dkt * kt
            db_a[blk, :] += t1
            dk_a[...] += dkt * e2
            db_a[...] -= t2
            db_a[r0 - 1:r0, :] += jnp.sum(t2, axis=0, keepdims=True) - jnp.sum(t1, axis=0, keepdims=True)
        dq, dk, dv, db = dq_a[...], dk_a[...], dv_a[...], db_a[...]
        dstn = dst_s[hd]
        dstnb = dstn.astype(BF16)
        bl = b[L - 1:L, :]
        ebl = jnp.exp(bl)
        kdec_e = jnp.exp(bl - b)
        kdec = k * kdec_e
        dbl = jnp.sum(dstn * st, axis=0, keepdims=True) * ebl
        dv = dv + _nt(kdec.astype(BF16), dstnb)
        dkdec = _nn(v.astype(BF16), dstnb)
        dk = dk + dkdec * kdec_e
        dx = dkdec * kdec
        dbl = dbl + jnp.sum(dx, axis=0, keepdims=True)
        db = db - dx + jnp.where(rows == L - 1, dbl, 0.0)
        dst_s[hd] = dstn * ebl + dst_o
        dg = _rcumsum_rows(db)
        dfk = dg / f - dk
        dq_ref[rs, sl] = (dq * _dsilu(qp)).astype(BF16)
        di_ref[rs, sl] = (dv * _dsilu(ip)).astype(BF16)
        df_ref[rs, sl] = (dfk * (1.0 - lb) * sg * (1.0 - sg)).astype(BF16)
        dlb_s[hd] += jnp.sum(dfk * (1.0 - sg), axis=0, keepdims=True)

        if ck == 0:
            @pl.when(pl.program_id(0) == nc // cps - 1)
            def _():
                dl0 = dlb_s[hd] * lb * (1.0 - lb)
                dl_ref[0:1, sl] = dl0
                dl_ref[1:2, sl] = -dl0

    cps = REC_CHUNKS
    rv = lambda j: nc // cps - 1 - j
    hspec = lambda blk: pl.BlockSpec((cps * L, 512), lambda j: (rv(j), blk))
    return _pc(body, name, (nc // cps,),
               [hspec(4), hspec(5), hspec(6), pl.BlockSpec((2, 512), lambda j: (0, 0)),
                pl.BlockSpec((nh, cps, HD, HD), lambda j: (0, rv(j), 0, 0)), hspec(0)],
               [hspec(0), hspec(0), hspec(0), pl.BlockSpec((2, 512), lambda j: (0, 0))],
               [_sds((t, 512), BF16), _sds((t, 512), BF16), _sds((t, 512), BF16), _sds((2, 512))],
               scratch=[pltpu.VMEM((nh, HD, HD), F32), pltpu.VMEM((nh, 1, HD), F32)] + [pltpu.VMEM((nh, L, HD), F32)] * 4)(z0, z0, z0, lbl, ss, do)


def _post0_fwd(hm, hh, z0, na, nb, w, h0, name, bm=512):
    t = h0.shape[0]
    bm = min(bm, t)

    def body(hm_ref, hh_ref, oa_ref, gb_ref, na_ref, nb_ref, w_ref, h_ref, o_ref, y_ref):
        for hd in range(4):
            sl = slice(hd * HD, (hd + 1) * HD)
            pa = _sigmoid(oa_ref[:, sl]) * hm_ref[:, sl]
            y_ref[:, sl] = (pa * _rstd(pa) * na_ref[:, sl]).astype(BF16)
            xb = hh_ref[:, sl]
            y_ref[:, 512 + hd * HD:512 + (hd + 1) * HD] = (xb * _rstd(xb) * nb_ref[:, sl] * _silu(gb_ref[:, sl])).astype(BF16)
        o_ref[...] = h_ref[...] + _nn(y_ref[...], w_ref[...])

    row = lambda wd, c: pl.BlockSpec((bm, wd), lambda i: (i, c))
    vec = lambda wd: pl.BlockSpec((1, wd), lambda i: (0, 0))
    return _pc(body, name, (t // bm,),
               [row(512, 0), row(512, 0), row(512, 3), row(512, 7), vec(512), vec(512),
                pl.BlockSpec((D, D), lambda i: (0, 0)), row(D, 0)],
               [row(D, 0), row(D, 0)], [_sds((t, D)), _sds((t, D), BF16)])(hm, hh, z0, z0, na, nb, w, h0)


def _post0_bwd(dh1, w, hm, hh, z0, na, nb, name, bm=512):
    t = dh1.shape[0]
    bm = min(bm, t)

    def body(dh_ref, w_ref, hm_ref, hh_ref, oa_ref, gb_ref, na_ref, nb_ref, dhm_ref, dhh_ref, doa_ref, dgb_ref, dna_ref, dnb_ref):
        @pl.when(pl.program_id(0) == 0)
        def _():
            dna_ref[...] = jnp.zeros_like(dna_ref)
            dnb_ref[...] = jnp.zeros_like(dnb_ref)

        dy = _nt(dh_ref[...].astype(BF16), w_ref[...])
        for hd in range(4):
            sl = slice(hd * HD, (hd + 1) * HD)
            hm_, oa = hm_ref[:, sl], oa_ref[:, sl]
            sg = _sigmoid(oa)
            dpa, dgr = _rms_bwd(dy[:, sl], sg * hm_, na_ref[:, sl])
            dna_ref[:, sl] += jnp.sum(dgr, axis=0, keepdims=True)
            doa_ref[:, sl] = (dpa * hm_ * sg * (1.0 - sg)).astype(BF16)
            dhm_ref[:, sl] = dpa * sg
            xb, gb, nbv = hh_ref[:, sl], gb_ref[:, sl], nb_ref[:, sl]
            dyb = dy[:, 512 + hd * HD:512 + (hd + 1) * HD]
            dgb_ref[:, sl] = (dyb * (xb * _rstd(xb) * nbv) * _dsilu(gb)).astype(BF16)
            dxb, dgr2 = _rms_bwd(dyb * _silu(gb), xb, nbv)
            dnb_ref[:, sl] += jnp.sum(dgr2, axis=0, keepdims=True)
            dhh_ref[:, sl] = dxb

    row = lambda wd, c: pl.BlockSpec((bm, wd), lambda i: (i, c))
    vec = lambda wd: pl.BlockSpec((1, wd), lambda i: (0, 0))
    return _pc(body, name, (t // bm,),
               [row(D, 0), pl.BlockSpec((D, D), lambda i: (0, 0)), row(512, 0), row(512, 0), row(512, 3), row(512, 7),
                vec(512), vec(512)],
               [row(512, 0), row(512, 0), row(512, 0), row(512, 0), vec(512), vec(512)],
               [_sds((t, 512)), _sds((t, 512)), _sds((t, 512), BF16), _sds((t, 512), BF16), _sds((1, 512)), _sds((1, 512))],
               )(dh1, w, hm, hh, z0, z0, na, nb)


def _memkv_fwd(mem, g, wkv_s, name):
    m = mem.shape[0]

    def body(x_ref, g_ref, w_ref, kv_ref, mn_ref):
        x = x_ref[...]
        mn = (x * _rstd(x) * g_ref[...]).astype(BF16)
        mn_ref[...] = mn
        kv_ref[...] = _nn(mn, w_ref[...])

    return _pc(body, name, (4,),
               [pl.BlockSpec((m, D), lambda k: (0, 0)), pl.BlockSpec((1, D), lambda k: (0, 0)),
                pl.BlockSpec((None, D, 512), lambda k: (k, 0, 0))],
               [pl.BlockSpec((m, 512), lambda k: (0, k)), pl.BlockSpec((m, D), lambda k: (0, 0))],
               [_sds((m, 2048)), _sds((m, D), BF16)])(mem, g, wkv_s)


def _memkv_bwd(dkv, wkv_s, mem, g, name):
    m = mem.shape[0]

    def body(d_ref, w_ref, x_ref, g_ref, dg_ref, acc):
        k = pl.program_id(0)

        @pl.when(k == 0)
        def _():
            acc[...] = jnp.zeros_like(acc)

        acc[...] += _nt(d_ref[...].astype(BF16), w_ref[...])

        @pl.when(k == 3)
        def _():
            _, dgr = _rms_bwd(acc[...], x_ref[...], g_ref[...])
            dg_ref[...] = jnp.sum(dgr, axis=0, keepdims=True)

    return _pc(body, name, (4,),
               [pl.BlockSpec((m, 512), lambda k: (0, k)), pl.BlockSpec((None, D, 512), lambda k: (k, 0, 0)),
                pl.BlockSpec((m, D), lambda k: (0, 0)), pl.BlockSpec((1, D), lambda k: (0, 0))],
               pl.BlockSpec((1, D), lambda k: (0, 0)), _sds((1, D)), scratch=[pltpu.VMEM((m, D), F32)])(dkv, wkv_s, mem, g)


def _xattn_probs(qh, kh):
    s = _nt(qh, kh) * (XD ** -0.5)
    p = jnp.exp(s - jnp.max(s, axis=1, keepdims=True))
    return p / jnp.sum(p, axis=1, keepdims=True)


def _xattn_fwd(q, kv, wo, h1, name, bm=512):
    t, m = q.shape[0], kv.shape[0]
    bm = min(bm, t)

    def body(q_ref, k_ref, v_ref, w_ref, h_ref, out_ref, o_ref):
        for hd in range(D // XD):
            sl = slice(hd * XD, (hd + 1) * XD)
            p = _xattn_probs(q_ref[:, sl].astype(BF16), k_ref[:, sl].astype(BF16))
            o_ref[:, sl] = _nn(p.astype(BF16), v_ref[:, sl].astype(BF16)).astype(BF16)
        out_ref[...] = h_ref[...] + _nn(o_ref[...], w_ref[...])

    row = pl.BlockSpec((bm, D), lambda i: (i, 0))
    return _pc(body, name, (t // bm,),
               [row, pl.BlockSpec((m, D), lambda i: (0, 0)), pl.BlockSpec((m, D), lambda i: (0, 1)),
                pl.BlockSpec((D, D), lambda i: (0, 0)), row],
               [row, row], [_sds((t, D)), _sds((t, D), BF16)])(q, kv, kv, wo, h1)


def _xattn_bwd(dh2, q, kv, wo, name, bm=512):
    t, m = q.shape[0], kv.shape[0]
    bm = min(bm, t)

    def body(dh_ref, q_ref, k_ref, v_ref, w_ref, dq_ref, dkv_ref):
        @pl.when(pl.program_id(0) == 0)
        def _():
            dkv_ref[...] = jnp.zeros_like(dkv_ref)

        d_o = _nt(dh_ref[...].astype(BF16), w_ref[...])
        for hd in range(D // XD):
            sl = slice(hd * XD, (hd + 1) * XD)
            qh, kh, vh = q_ref[:, sl].astype(BF16), k_ref[:, sl].astype(BF16), v_ref[:, sl].astype(BF16)
            p = _xattn_probs(qh, kh)
            dob = d_o[:, sl].astype(BF16)
            dp = _nt(dob, vh)
            dkv_ref[:, D + hd * XD:D + (hd + 1) * XD] += _tn(p.astype(BF16), dob)
            ds = (p * (dp - jnp.sum(dp * p, axis=1, keepdims=True)) * (XD ** -0.5)).astype(BF16)
            dq_ref[:, sl] = _nn(ds, kh).astype(BF16)
            dkv_ref[:, sl] += _tn(ds, qh)

    row = pl.BlockSpec((bm, D), lambda i: (i, 0))
    return _pc(body, name, (t // bm,),
               [row, row, pl.BlockSpec((m, D), lambda i: (0, 0)), pl.BlockSpec((m, D), lambda i: (0, 1)),
                pl.BlockSpec((D, D), lambda i: (0, 0))],
               [row, pl.BlockSpec((m, 2 * D), lambda i: (0, 0))],
               [_sds((t, D), BF16), _sds((m, 2 * D))])(dh2, q, kv, kv, wo)


NH1 = 8
FOX_BM = 512
FOX_BQ = 512
FOX_BK = 512
FOX_HEADS_PER_STEP = 4


def _foxprep_fwd(z1, qg, kg, fbp, name):
    t = z1.shape[0]
    bm = min(FOX_BM, t)

    def body(q_ref, k_ref, v_ref, f_ref, qg_ref, kg_ref, fb_ref, qn_ref, kn_ref, vb_ref, c_ref, carry):
        @pl.when(pl.program_id(0) == 0)
        def _():
            carry[...] = jnp.zeros_like(carry)

        for hd in range(NH1):
            sl = slice(hd * HD, (hd + 1) * HD)
            x = q_ref[:, sl]
            qn_ref[:, sl] = (x * _rstd(x) * qg_ref[...] * FOX_QSCALE).astype(BF16)
            x = k_ref[:, sl]
            kn_ref[:, sl] = (x * _rstd(x) * kg_ref[...]).astype(BF16)
        vb_ref[...] = v_ref[...].astype(BF16)
        c = carry[...] + _cumsum_rows(_log_sigmoid(f_ref[...] + fb_ref[...]))
        c_ref[...] = c
        carry[...] = c[bm - 1:bm, :]

    row = lambda c: pl.BlockSpec((bm, D), lambda i: (i, c))
    lane = pl.BlockSpec((bm, HD), lambda i: (i, 4 * D // HD))
    vec = pl.BlockSpec((1, HD), lambda i: (0, 0))
    return _pc(body, name, (t // bm,), [row(0), row(1), row(2), lane, vec, vec, vec],
               [row(0), row(0), row(0), pl.BlockSpec((bm, HD), lambda i: (i, 0))],
               [_sds((t, D), BF16), _sds((t, D), BF16), _sds((t, D), BF16), _sds((t, HD))],
               scratch=[pltpu.VMEM((1, HD), F32)])(z1, z1, z1, z1, qg, kg, fbp)


def _foxprep_bwd(dqn, dkn, z1, qg, kg, fbp, dc, name):
    t = z1.shape[0]
    bm = min(FOX_BM, t)
    nb = t // bm

    def body(dqn_ref, dkn_ref, q_ref, k_ref, f_ref, qg_ref, kg_ref, fb_ref, dc_ref,
             dq_ref, dk_ref, df_ref, dqg_ref, dkg_ref, dfb_ref, carry):
        @pl.when(pl.program_id(0) == 0)
        def _():
            carry[...] = jnp.zeros_like(carry)
            dqg_ref[...] = jnp.zeros_like(dqg_ref)
            dkg_ref[...] = jnp.zeros_like(dkg_ref)
            dfb_ref[...] = jnp.zeros_like(dfb_ref)

        for hd in range(NH1):
            sl = slice(hd * HD, (hd + 1) * HD)
            dx, dgr = _rms_bwd(dqn_ref[:, sl] * (HD ** -0.5), q_ref[:, sl], qg_ref[...])
            dq_ref[:, sl] = dx.astype(BF16)
            dqg_ref[...] += jnp.sum(dgr, axis=0, keepdims=True)
            dx, dgr = _rms_bwd(dkn_ref[:, sl], k_ref[:, sl], kg_ref[...])
            dk_ref[:, sl] = dx.astype(BF16)
            dkg_ref[...] += jnp.sum(dgr, axis=0, keepdims=True)
        dc_ = dc_ref[...]
        dlogf = _rcumsum_rows(dc_) + carry[...]
        carry[...] += jnp.sum(dc_, axis=0, keepdims=True)
        lanes = lax.broadcasted_iota(jnp.int32, dc_.shape, 1)
        df = jnp.where(lanes < NH1, dlogf * (1.0 - _sigmoid(f_ref[...] + fb_ref[...])), 0.0)
        df_ref[...] = df.astype(BF16)
        dfb_ref[...] += jnp.sum(df, axis=0, keepdims=True)

    rv = lambda i: nb - 1 - i
    row = lambda c: pl.BlockSpec((bm, D), lambda i: (rv(i), c))
    lane = lambda c: pl.BlockSpec((bm, HD), lambda i: (rv(i), c))
    vec = pl.BlockSpec((1, HD), lambda i: (0, 0))
    return _pc(body, name, (nb,), [row(0), row(0), row(0), row(1), lane(4 * D // HD), vec, vec, vec, lane(0)],
               [row(0), row(0), lane(0), vec, vec, vec],
               [_sds((t, D), BF16), _sds((t, D), BF16), _sds((t, HD), BF16), _sds((1, HD)), _sds((1, HD)), _sds((1, HD))],
               scratch=[pltpu.VMEM((1, HD), F32)])(dqn, dkn, z1, z1, z1, qg, kg, fbp, dc)


LOG2E = 1.4426950408889634
FOX_QSCALE = HD ** -0.5 * LOG2E


def _fox_steps(t, bq, bk, k_major):
    nq, nk = t // bq, t // bk
    pairs = [(i, j) for i in range(nq) for j in range(nk) if j * bk < (i + 1) * bq]
    if k_major:
        pairs.sort(key=lambda p: (p[1], p[0]))
    outer = [p[1] if k_major else p[0] for p in pairs]
    n = len(pairs)
    flags = [(n_ == 0 or outer[n_] != outer[n_ - 1]) + 2 * (n_ == n - 1 or outer[n_] != outer[n_ + 1])
             + 4 * (not (j + 1) * bk <= i * bq + 1) for n_, (i, j) in enumerate(pairs)]
    as_i32 = lambda v: jnp.asarray(v, jnp.int32)
    return as_i32([p[0] for p in pairs]), as_i32([p[1] for p in pairs]), as_i32(flags)


def _fox_step_info(qi_ref, kj_ref, fl_ref):
    s = pl.program_id(1)
    fl = fl_ref[s]
    return qi_ref[s], kj_ref[s], (fl & 1) != 0, (fl & 2) != 0, (fl & 4) != 0


def _fox_call(body, name, tables, in_specs, out_specs, out_shape, scratch):
    grid_spec = pltpu.PrefetchScalarGridSpec(num_scalar_prefetch=3, grid=(NH1 // FOX_HEADS_PER_STEP, tables[0].shape[0]),
                                             in_specs=in_specs, out_specs=out_specs, scratch_shapes=scratch)
    return pl.pallas_call(body, name=name, grid_spec=grid_spec, out_shape=out_shape,
                          compiler_params=pltpu.CompilerParams(dimension_semantics=("arbitrary", "arbitrary"),
                                                               vmem_limit_bytes=VMEM_LIMIT_V7X))


def _fox_lane_tiles(x):
    return [x[:, c0:c0 + HD] for c0 in range(0, x.shape[1], HD)]


def _fox_masked_scores(q, k, ck, i, j, bq, bk, masked):
    s = _nt(q, k) - ck
    if masked:
        rows = i * bq + lax.broadcasted_iota(jnp.int32, s.shape, 0)
        cols = j * bk + lax.broadcasted_iota(jnp.int32, s.shape, 1)
        s = jnp.where(cols <= rows, s, NEG)
    return s


def _fox_specs(bq, bk, G):
    qspec = pl.BlockSpec((bq, G * HD), lambda h, s, qi, kj, fl: (qi[s], h))
    kspec = pl.BlockSpec((bk, G * HD), lambda h, s, qi, kj, fl: (kj[s], h))
    cspec = pl.BlockSpec((G, 1, bk), lambda h, s, qi, kj, fl: (h, 0, kj[s]))
    colspec = pl.BlockSpec((G, bq, 1), lambda h, s, qi, kj, fl: (h, qi[s], 0))
    return qspec, kspec, cspec, colspec


def _fox_rowmax(qn, kn, crow, name):
    t = qn.shape[0]
    bq, bk, G = min(FOX_BQ, t), min(2 * FOX_BK, t), FOX_HEADS_PER_STEP
    tables = _fox_steps(t, bq, bk, k_major=False)

    def body(qi_ref, kj_ref, fl_ref, q_ref, k_ref, ck_ref, m_ref, *mp):
        i, j, first, last, diag = _fox_step_info(qi_ref, kj_ref, fl_ref)

        @pl.when(first)
        def _():
            for g in range(G):
                mp[g][...] = jnp.full_like(mp[g], NEG)

        def step(masked):
            for g in range(G):
                sl = slice(g * HD, (g + 1) * HD)
                s = _fox_masked_scores(q_ref[:, sl], k_ref[:, sl], ck_ref[g], i, j, bq, bk, masked)
                m = mp[g][...]
                for tile in _fox_lane_tiles(s):
                    m = jnp.maximum(m, tile)
                mp[g][...] = m

        pl.when(jnp.logical_not(diag))(lambda: step(False))
        pl.when(diag)(lambda: step(True))

        @pl.when(last)
        def _():
            for g in range(G):
                m_ref[g] = jnp.max(mp[g][...], axis=1, keepdims=True)

    qspec, kspec, cspec, colspec = _fox_specs(bq, bk, G)
    return _fox_call(body, name, tables, [qspec, kspec, cspec], colspec, _sds((NH1, t, 1)),
                     [pltpu.VMEM((bq, HD), F32)] * G)(*tables, qn, kn, crow)


def _fox_fwd(qn, kn, vb, crow, m, name):
    t = qn.shape[0]
    bq, bk, G = min(FOX_BQ, t), min(FOX_BK, t), FOX_HEADS_PER_STEP
    tables = _fox_steps(t, bq, bk, k_major=False)

    def body(qi_ref, kj_ref, fl_ref, q_ref, k_ref, v_ref, ck_ref, m_ref, o_ref, lse_ref, *scr):
        i, j, first, last, diag = _fox_step_info(qi_ref, kj_ref, fl_ref)
        lp, acc = scr[:G], scr[G:]

        @pl.when(first)
        def _():
            for g in range(G):
                lp[g][...] = jnp.zeros_like(lp[g])
                acc[g][...] = jnp.zeros_like(acc[g])

        def step(masked):
            for g in range(G):
                sl = slice(g * HD, (g + 1) * HD)
                s = _fox_masked_scores(q_ref[:, sl], k_ref[:, sl], ck_ref[g], i, j, bq, bk, masked)
                p = jnp.exp2(s - m_ref[g])
                l = lp[g][...]
                for tile in _fox_lane_tiles(p):
                    l = l + tile
                lp[g][...] = l
                acc[g][...] += _nn(p.astype(BF16), v_ref[:, sl])

        pl.when(jnp.logical_not(diag))(lambda: step(False))
        pl.when(diag)(lambda: step(True))

        @pl.when(last)
        def _():
            for g in range(G):
                l = jnp.sum(lp[g][...], axis=1, keepdims=True)
                o_ref[:, g * HD:(g + 1) * HD] = acc[g][...] / l
                lse_ref[g] = m_ref[g] + jnp.log2(l)

    qspec, kspec, cspec, colspec = _fox_specs(bq, bk, G)
    return _fox_call(body, name, tables, [qspec, kspec, kspec, cspec, colspec], [qspec, colspec],
                     [_sds((t, D)), _sds((NH1, t, 1))], [pltpu.VMEM((bq, HD), F32)] * (2 * G))(*tables, qn, kn, vb, crow, m)


def _fox_bwd(qn, kn, vb, crow, lse, delta, do, name):
    t = qn.shape[0]
    bq, bk, G = min(FOX_BQ, t), min(FOX_BK, t), FOX_HEADS_PER_STEP
    tables = _fox_steps(t, bq, bk, k_major=True)

    def body(qi_ref, kj_ref, fl_ref, q_ref, k_ref, v_ref, ck_ref, lse_ref, dl_ref, do_ref, dq_ref, dk_ref, dv_ref, dc_ref, dcq_ref,
             dk_s, dv_s, dc_s):
        i, j, first, last, diag = _fox_step_info(qi_ref, kj_ref, fl_ref)

        @pl.when(first)
        def _():
            dk_s[...] = jnp.zeros_like(dk_s)
            dv_s[...] = jnp.zeros_like(dv_s)
            dc_s[...] = jnp.zeros_like(dc_s)

        @pl.when(pl.program_id(1) == 0)
        def _():
            dq_ref[...] = jnp.zeros_like(dq_ref)
            dcq_ref[...] = jnp.zeros_like(dcq_ref)

        def step(masked):
            rows = pl.ds(pl.multiple_of(i * bq, bq), bq)
            for g in range(G):
                sl = slice(g * HD, (g + 1) * HD)
                q, k = q_ref[:, sl], k_ref[:, sl]
                s = _fox_masked_scores(q, k, ck_ref[g], i, j, bq, bk, masked)
                p = jnp.exp2(s - lse_ref[g])
                dob = do_ref[:, sl]
                dv_s[:, sl] += _tn(p.astype(BF16), dob)
                ds = p * (_nt(dob, v_ref[:, sl]) - dl_ref[g])
                dsb = ds.astype(BF16)
                dq_ref[rows, sl] += _nn(dsb, k)
                dk_s[:, sl] += _tn(dsb, q)
                dc_s[g] -= jnp.sum(ds, axis=0, keepdims=True)
                part_sum = dcq_ref[g, rows, :]
                for tile in _fox_lane_tiles(ds):
                    part_sum = part_sum + tile
                dcq_ref[g, rows, :] = part_sum

        pl.when(jnp.logical_not(diag))(lambda: step(False))
        pl.when(diag)(lambda: step(True))

        @pl.when(last)
        def _():
            dk_ref[...] = dk_s[...] * (1.0 / LOG2E)
            dv_ref[...] = dv_s[...]
            dc_ref[...] = dc_s[...]

    qspec, kspec, cspec, colspec = _fox_specs(bq, bk, G)
    return _fox_call(
        body, name, tables, [qspec, kspec, kspec, cspec, colspec, colspec, qspec],
        [pl.BlockSpec((t, G * HD), lambda h, s, qi, kj, fl: (0, h)), kspec, kspec, cspec,
         pl.BlockSpec((G, t, HD), lambda h, s, qi, kj, fl: (h, 0, 0))],
        [_sds((t, D)), _sds((t, D)), _sds((t, D)), _sds((NH1, 1, t)), _sds((NH1, t, HD))],
        [pltpu.VMEM((bk, G * HD), F32), pltpu.VMEM((bk, G * HD), F32), pltpu.VMEM((G, 1, bk), F32)],
    )(*tables, qn, kn, vb, crow, lse, delta, do)


def _post1_fwd(o, z1, w, h3, name, bm=512):
    t = o.shape[0]
    bm = min(bm, t)

    def body(o_ref, g_ref, w_ref, h_ref, out_ref, og_ref):
        og_ref[...] = (o_ref[...] * _sigmoid(g_ref[...])).astype(BF16)
        out_ref[...] = h_ref[...] + _nn(og_ref[...], w_ref[...])

    row = lambda c: pl.BlockSpec((bm, D), lambda i: (i, c))
    return _pc(body, name, (t // bm,), [row(0), row(3), pl.BlockSpec((D, D), lambda i: (0, 0)), row(0)],
               [row(0), row(0)], [_sds((t, D)), _sds((t, D), BF16)])(o, z1, w, h3)


def _post1_bwd(dh4, w, o, z1, name, bm=512):
    t = o.shape[0]
    bm = min(bm, t)

    def body(dh_ref, w_ref, o_ref, g_ref, do_ref, dg_ref, dl_ref):
        d_og = _nt(dh_ref[...].astype(BF16), w_ref[...])
        o_, sg = o_ref[...], _sigmoid(g_ref[...])
        dob = (d_og * sg).astype(BF16)
        do_ref[...] = dob
        dg_ref[...] = (d_og * o_ * sg * (1.0 - sg)).astype(BF16)
        prod = dob.astype(F32) * o_
        for hd in range(NH1):
            dl_ref[hd] = jnp.sum(prod[:, hd * HD:(hd + 1) * HD], axis=1, keepdims=True)

    row = lambda c: pl.BlockSpec((bm, D), lambda i: (i, c))
    return _pc(body, name, (t // bm,), [row(0), pl.BlockSpec((D, D), lambda i: (0, 0)), row(0), row(3)],
               [row(0), row(0), pl.BlockSpec((NH1, bm, 1), lambda i: (0, i, 0))],
               [_sds((t, D), BF16), _sds((t, D), BF16), _sds((NH1, t, 1))])(dh4, w, o, z1)


def _final(h, g, tgt, name, bm=512):
    t = h.shape[0]
    bm = min(bm, t)

    def body(h_ref, g_ref, t_ref, l_ref, dh_ref, dg_ref):
        @pl.when(pl.program_id(0) == 0)
        def _():
            l_ref[...] = jnp.zeros_like(l_ref)
            dg_ref[...] = jnp.zeros_like(dg_ref)

        x, gv = h_ref[...], g_ref[...]
        r = _rstd(x)
        xh = x * r
        e = xh * gv - t_ref[...]
        l_ref[...] += 0.5 * jnp.sum(jnp.mean(e * e, axis=1, keepdims=True), axis=0, keepdims=True)
        dy = e * (1.0 / D)
        dg_ref[...] += jnp.sum(dy * xh, axis=0, keepdims=True)
        dxh = dy * gv
        dh_ref[...] = r * (dxh - xh * jnp.mean(dxh * xh, axis=1, keepdims=True))

    row = pl.BlockSpec((bm, D), lambda i: (i, 0))
    vec = pl.BlockSpec((1, D), lambda i: (0, 0))
    return _pc(body, name, (t // bm,), [row, vec, row], [pl.BlockSpec((1, HD), lambda i: (0, 0)), row, vec],
               [_sds((1, HD)), _sds((t, D)), _sds((1, D))])(h, g, tgt)


def _adam(w, g, m, v, name):
    r, c = w.shape
    br = min(r, 256)

    def body(w_ref, g_ref, m_ref, v_ref, d_ref, mo_ref, vo_ref):
        gv = g_ref[...]
        mn = ADAM_B1 * m_ref[...] + (1.0 - ADAM_B1) * gv
        vn = ADAM_B2 * v_ref[...] + (1.0 - ADAM_B2) * jnp.square(gv)
        m_hat = mn / (1.0 - ADAM_B1 ** ADAM_STEP)
        v_hat = vn / (1.0 - ADAM_B2 ** ADAM_STEP)
        d_ref[...] = -ADAM_LR * (m_hat / (jnp.sqrt(v_hat) + ADAM_EPS) + ADAM_WD * w_ref[...])
        mo_ref[...] = mn
        vo_ref[...] = vn

    blk = pl.BlockSpec((br, c), lambda i: (i, 0))
    return _pc(body, name, (r // br,), [blk] * 4, [blk] * 3, [_sds((r, c))] * 3)(w, g, m, v)


ZW = 4224
GATE0 = 4096


def _pack_w_in0(w):
    return jnp.concatenate([w[:, :2048], w[:, 2056:], w[:, 2048:2056], jnp.zeros((w.shape[0], ZW - 4104), w.dtype)], axis=1)


def _unpack_w_in0(g):
    return jnp.concatenate([g[:, :2048], g[:, GATE0:GATE0 + 8], g[:, 2048:GATE0]], axis=1)


def _pack_w_in1(w):
    return jnp.concatenate([w, jnp.zeros((w.shape[0], ZW - 4104), w.dtype)], axis=1)


def _unpack_w_in1(g):
    return g[:, :4104]


def _local_step(x, mem, tgt, W, S, late_weights=None, grads_hook=None):
    t = x.shape[0]
    row = lambda v: v.reshape(1, -1)
    G = {}

    z0, u0 = _norm_mm(x, S["norm_mix_g"][0:1], W["w_in0"], "in0_fwd")
    qk = _conv_fwd(z0, S["conv_w"], "conv_fwd")
    g8 = z0[:, GATE0:GATE0 + 8]
    gates3 = jnp.stack([g8[:, :4].T, g8[:, 4:].T], axis=-1)
    gb = S["gate_b"]
    bias3 = jnp.stack([gb[0, :4], gb[0, 4:]], axis=-1)[:, None, :]
    hm, cs, ns, ms = _mlstm_fwd(qk, z0, gates3, bias3, "mlstm_fwd")
    hh, ss = _hgrn_fwd(z0, S["lb_logits"], "hgrn_fwd")
    if late_weights is not None:
        W = {**W, **late_weights(hh)}
    kv, mn = _memkv_fwd(mem, row(S["mem_norm_g"]), W["wkv_s"], "memkv_fwd")
    h1, y0 = _post0_fwd(hm, hh, z0, S["mlstm_norm_g"], S["hgrn_norm_g"], W["w_out0"], x, "post0_fwd")

    def xattn_mlp_fwd(h, l):
        q, ux = _norm_mm(h, S["norm_xattn_g"][l:l + 1], W["wq"][l], f"xq{l}_fwd")
        h2, ox = _xattn_fwd(q, kv, W["wo"][l], h, f"xattn{l}_fwd")
        h3, a, um = _mlp_fwd(h2, S["norm_mlp_g"][l:l + 1], W["w1s"], W["w2"], l, f"mlp{l}_fwd")
        return h3, (h, q, ux, ox, h2, a, um)

    h3, sv0 = xattn_mlp_fwd(h1, 0)
    z1, u1 = _norm_mm(h3, S["norm_mix_g"][1:2], W["w_in1"], "in1_fwd")
    fbp = jnp.pad(S["c_fgate_b"], ((0, 0), (0, HD - NH1)))
    qn, kn, vb, c = _foxprep_fwd(z1, S["c_qnorm_g"], S["c_knorm_g"], fbp, "foxprep_fwd")
    crow = (c[:, :NH1] * LOG2E).T[:, None, :]
    o1, lse = _fox_fwd(qn, kn, vb, crow, _fox_rowmax(qn, kn, crow, "fox_rowmax"), "fox_fwd")
    h4, og = _post1_fwd(o1, z1, W["w_out1"], h3, "post1_fwd")
    h6, sv1 = xattn_mlp_fwd(h4, 1)
    lossp, dh, G["final_norm_g"] = _final(h6, row(S["final_norm_g"]), tgt, "final")

    grads_ready = grads_hook if grads_hook is not None else (lambda stage, grads: 0.0)
    dkv = None
    dgx, dgm, dwq, dwo, dw1, dw2 = [None, None], [None, None], [None, None], [None, None], [None, None], [None, None]

    def xattn_mlp_bwd(dh, l, sv):
        nonlocal dkv
        h, q, ux, ox, h2, a, um = sv
        dh2, da, r, dgm[l] = _mlp_bwd(dh, a, W["w1s"], W["w2"], l, h2, S["norm_mlp_g"][l:l + 1], f"mlp{l}_bwd")
        dw1[l] = _mm_tn(um, da, f"mlp{l}_dw1", col_chips=NCHIP)
        dw2[l] = _mm_tn(r, dh, f"mlp{l}_dw2")
        dq, dkv_l = _xattn_bwd(dh2, q, kv, W["wo"][l], f"xattn{l}_bwd")
        dkv = dkv_l if dkv is None else dkv + dkv_l
        dwo[l] = _mm_tn(ox, dh2, f"xattn{l}_dwo")
        dwq[l] = _mm_tn(ux, dq, f"xattn{l}_dwq")
        dh1, dgx[l] = _bwd_in(dq, W["wq"][l], h, S["norm_xattn_g"][l:l + 1], dh2, f"xq{l}_bwd")
        return dh1

    dh4 = xattn_mlp_bwd(dh, 1, sv1)
    do, dgate, delta = _post1_bwd(dh4, W["w_out1"], o1, z1, "post1_bwd")
    G["w_out1"] = _mm_tn(og, dh4, "post1_dw")
    dqn, dkn, dv1, dcrow, dcq = _fox_bwd(qn, kn, vb, crow, lse, delta, do, "fox_bwd")
    dc = jnp.pad((dcrow[:, 0, :] + jnp.sum(dcq, axis=-1)).T, ((0, 0), (0, HD - NH1)))
    dqr, dkr, df1, G["c_qnorm_g"], G["c_knorm_g"], dfb = _foxprep_bwd(
        dqn, dkn, z1, S["c_qnorm_g"], S["c_knorm_g"], fbp, dc, "foxprep_bwd")
    G["c_fgate_b"] = dfb[:, :NH1]
    dz1 = jnp.concatenate([dqr, dkr, dv1.astype(BF16), dgate, df1], axis=1)
    G["w_in1"] = _mm_tn(u1, dz1, "in1_dw")
    tok = grads_ready("layer1", dict(w_out=G["w_out1"], w_in=G["w_in1"], wq=dwq[1], wo=dwo[1], w1=dw1[1], w2=dw2[1]))
    dh3, dgmix1 = _bwd_in(dz1, W["w_in1"], h3, S["norm_mix_g"][1:2] + tok, dh4, "in1_bwd")
    dh1 = xattn_mlp_bwd(dh3, 0, sv0)

    G["wkv"] = _mm_tn(mn, dkv, "memkv_dw", col_chips=NCHIP)
    G["mem_norm_g"] = _memkv_bwd(dkv, W["wkv_s"], mem, row(S["mem_norm_g"]), "memkv_bwd")
    G["w_out0"] = _mm_tn(y0, dh1, "post0_dw")
    tok = grads_ready("layer0", dict(wq=dwq[0], wo=dwo[0], w1=dw1[0], w2=dw2[0], wkv=G["wkv"], w_out=G["w_out0"]))
    dhm, dhh, doa, dgb, G["mlstm_norm_g"], G["hgrn_norm_g"] = _post0_bwd(
        dh1, W["w_out0"], hm, hh, z0, S["mlstm_norm_g"] + tok, S["hgrn_norm_g"], "post0_bwd")
    dqa, dka, dva, dgates3 = _mlstm_bwd(qk, z0, gates3, bias3, cs, ns, ms, dhm, "mlstm_bwd")
    dqb, dfb0, dib, G["lb_logits"] = _hgrn_bwd(z0, S["lb_logits"], ss, dhh, "hgrn_bwd")
    duc, G["conv_w"] = _conv_bwd(z0, S["conv_w"], jnp.concatenate([dqa, dka], axis=1), "conv_bwd")
    dg8 = jnp.concatenate([dgates3[:, :, 0].T, dgates3[:, :, 1].T], axis=1)
    G["gate_b"] = jnp.sum(dg8, axis=0, keepdims=True)
    dz0 = jnp.concatenate([duc, dva.astype(BF16), doa, dqb, dfb0, dib, dgb,
                           jnp.pad(dg8, ((0, 0), (0, HD - 8))).astype(BF16)], axis=1)
    G["w_in0"] = _mm_tn(u0, dz0, "in0_dw")
    tok = grads_ready("in0", dict(w_in=G["w_in0"]))
    dx, dgmix0 = _bwd_in(dz0, W["w_in0"], x, S["norm_mix_g"][0:1] + tok, dh1, "in0_bwd")

    G["norm_mix_g"] = jnp.concatenate([dgmix0, dgmix1], axis=0)
    G["norm_xattn_g"] = jnp.concatenate(dgx, axis=0)
    G["norm_mlp_g"] = jnp.concatenate(dgm, axis=0)
    G["wq"], G["wo"], G["w1"], G["w2"] = dwq, dwo, dw1, dw2
    return lossp[0, 0], dx, G


ANY = pl.BlockSpec(memory_space=pl.ANY)
NCHIP = 4


def _place():
    x, y, c = lax.axis_index("x"), lax.axis_index("y"), lax.axis_index("c")
    return x, y, c, [(1 - x, y), (x, 1 - y), (1 - x, 1 - y)]


def _comm_call(body, name, ins, out_shapes, sems):
    return pl.pallas_call(body, name=name, in_specs=[ANY] * len(ins), out_specs=[ANY] * len(out_shapes),
                          out_shape=out_shapes, scratch_shapes=sems)(*ins)


def _gather_weights(arrs, name):
    n = len(arrs)

    def body(*refs):
        ins, outs = refs[:n], refs[n:2 * n]
        send_i, recv_i, send_d, recv_d = refs[2 * n:]
        x, y, c, chips = _place()
        me = 2 * x + y

        def half(a, cc):
            h = arrs[a].shape[0] // 2
            return pl.ds(pl.multiple_of(cc * h, h), h)

        def ici(a, k, src_chip, dst_dev):
            return pltpu.make_async_remote_copy(
                src_ref=ins[a].at[half(a, c)], dst_ref=outs[a].at[src_chip, half(a, c)], send_sem=send_i.at[a, k],
                recv_sem=recv_i.at[a, k], device_id=dst_dev, device_id_type=MESH)

        def d2d(a, k, src_chip, cc):
            reg = outs[a].at[src_chip, half(a, cc)]
            return pltpu.make_async_remote_copy(src_ref=reg, dst_ref=reg, send_sem=send_d.at[a, k], recv_sem=recv_d.at[a, k],
                                                device_id=(x, y, 1 - c), device_id_type=MESH)

        for a in range(n):
            for k, (px, py) in enumerate(chips):
                ici(a, k, me, (px, py, c)).start()
        for k, (px, py) in enumerate(chips):
            for a in range(n):
                ici(a, k, 2 * px + py, (px, py, c)).wait_recv()
                d2d(a, k, 2 * px + py, c).start()
        for k, (px, py) in enumerate(chips):
            for a in range(n):
                ici(a, k, me, (px, py, c)).wait_send()
                d2d(a, k, 2 * px + py, c).wait_send()
                d2d(a, k, 2 * px + py, 1 - c).wait_recv()

    sem = lambda: pltpu.SemaphoreType.DMA((n, 3))
    return _comm_call(body, name, arrs, [_sds((NCHIP,) + a.shape, a.dtype) for a in arrs], [sem(), sem(), sem(), sem()])


HBM = pl.BlockSpec(memory_space=pltpu.HBM)
SEM = pl.BlockSpec(memory_space=pltpu.SEMAPHORE)
DATAFLOW = pltpu.SideEffectType.DATAFLOW_SIDE_EFFECTING


def _half_rows(r, cc):
    return pl.ds(pl.multiple_of(cc * (r // 2), r // 2), r // 2)


def _gather_start(arrs, after, name):
    n = len(arrs)

    def body(*refs):
        ins, lands = refs[:n], refs[n:2 * n]
        send, recv, token = refs[2 * n + 1], refs[2 * n + 2], refs[-1]
        x, y, c, chips = _place()
        me = 2 * x + y
        for a in range(n):
            rows = _half_rows(arrs[a].shape[0], c)
            for k, (px, py) in enumerate(chips):
                pltpu.make_async_remote_copy(src_ref=ins[a].at[rows], dst_ref=lands[a].at[me, rows], send_sem=send.at[3 * a + k],
                                             recv_sem=recv.at[3 * a + k], device_id=(px, py, c), device_id_type=MESH).start()
        token[...] = jnp.zeros_like(token)

    hbm = lambda v: pltpu.with_memory_space_constraint(v, pltpu.HBM)
    land_shapes = [((NCHIP,) + a.shape, a.dtype) for a in arrs]
    out = pl.pallas_call(
        body, name=name,
        out_shape=(pltpu.SemaphoreType.DMA((3 * n,)), pltpu.SemaphoreType.DMA((3 * n,)), *[pltpu.HBM(a.shape, a.dtype) for a in arrs],
                   *[pltpu.HBM(s, d) for s, d in land_shapes], _sds((8, HD))),
        in_specs=[HBM] * (2 * n) + [ANY], out_specs=(SEM, SEM, *[HBM] * (2 * n), pl.BlockSpec(memory_space=pltpu.VMEM)),
        input_output_aliases={i: 2 + i for i in range(2 * n)},
        compiler_params=pltpu.CompilerParams(has_side_effects=DATAFLOW),
    )(*[hbm(a) for a in arrs], *[hbm(lax.empty(s, d)) for s, d in land_shapes], after)
    return out[0], out[1], list(out[2:2 + n]), list(out[2 + n:2 + 2 * n]), out[-1]


def _gather_wait(send, recv, srcs, lands, after, name):
    n = len(srcs)

    def body(*refs):
        ins, lands_ = refs[:n], refs[n:2 * n]
        send_, recv_ = refs[2 * n], refs[2 * n + 1]
        x, y, c, chips = _place()
        for a in range(n):
            rows = _half_rows(srcs[a].shape[0], c)
            for k, (px, py) in enumerate(chips):
                cp = pltpu.make_async_remote_copy(src_ref=ins[a].at[rows], dst_ref=lands_[a].at[2 * px + py, rows], send_sem=send_.at[3 * a + k],
                                                  recv_sem=recv_.at[3 * a + k], device_id=(px, py, c), device_id_type=MESH)
                cp.wait_send()
                cp.wait_recv()

    out = pl.pallas_call(
        body, name=name, out_shape=[pltpu.HBM(v.shape, v.dtype) for v in list(srcs) + list(lands)],
        in_specs=[HBM] * (2 * n) + [SEM, SEM, ANY], out_specs=[HBM] * (2 * n), input_output_aliases={i: i for i in range(2 * n)},
        compiler_params=pltpu.CompilerParams(has_side_effects=DATAFLOW),
    )(*srcs, *lands, send, recv, after)
    return list(out[n:])


def _pair_forward(lands, name):
    n = len(lands)

    def body(*refs):
        ins, outs = refs[:n], refs[n:2 * n]
        send, recv = refs[2 * n:]
        x, y, c, chips = _place()
        copies = []
        for a in range(n):
            r = lands[a].shape[1]
            for k, (px, py) in enumerate(chips):
                cp = pltpu.make_async_remote_copy(
                    src_ref=ins[a].at[2 * px + py, _half_rows(r, c)], dst_ref=outs[a].at[2 * px + py, _half_rows(r, c)],
                    send_sem=send.at[a, k], recv_sem=recv.at[a, k], device_id=(x, y, 1 - c), device_id_type=MESH)
                cp.start()
                copies.append(cp)
        for a in range(n):
            r = lands[a].shape[1]
            for k, (px, py) in enumerate(chips):
                pltpu.make_async_remote_copy(
                    src_ref=ins[a].at[2 * px + py, _half_rows(r, c)], dst_ref=outs[a].at[2 * px + py, _half_rows(r, 1 - c)],
                    send_sem=send.at[a, k], recv_sem=recv.at[a, k], device_id=(x, y, 1 - c), device_id_type=MESH).wait_recv()
        for cp in copies:
            cp.wait_send()

    return pl.pallas_call(body, name=name, in_specs=[ANY] * n, out_specs=[ANY] * n, out_shape=[_sds(v.shape, v.dtype) for v in lands],
                          scratch_shapes=[pltpu.SemaphoreType.DMA((n, 3)), pltpu.SemaphoreType.DMA((n, 3))],
                          input_output_aliases={i: i for i in range(n)})(*lands)


def _pair_exchange(arrs, name):
    n = len(arrs)

    def body(*refs):
        ins, outs = refs[:n], refs[n:2 * n]
        send, recv = refs[2 * n:]
        x, y, c, _ = _place()
        copies = []
        for a in range(n):
            h = arrs[a].shape[1] // 2
            cp = pltpu.make_async_remote_copy(src_ref=ins[a].at[:, pl.ds(pl.multiple_of((1 - c) * h, h), h)], dst_ref=outs[a],
                                              send_sem=send.at[a], recv_sem=recv.at[a], device_id=(x, y, 1 - c), device_id_type=MESH)
            cp.start()
            copies.append(cp)
        for cp in copies:
            cp.wait()

    return _comm_call(body, name, arrs, [_sds((a.shape[0], a.shape[1] // 2, a.shape[2]), a.dtype) for a in arrs],
                      [pltpu.SemaphoreType.DMA((n,)), pltpu.SemaphoreType.DMA((n,))])


def _chip_exchange_start(arrs, name):
    n = len(arrs)

    def body(*refs):
        ins, lands = refs[:n], refs[n:2 * n]
        send, recv, token = refs[2 * n], refs[2 * n + 1], refs[-1]
        x, y, c, chips = _place()
        me = 2 * x + y
        for a in range(n):
            for k, (px, py) in enumerate(chips):
                pltpu.make_async_remote_copy(src_ref=ins[a].at[2 * px + py], dst_ref=lands[a].at[me], send_sem=send.at[3 * a + k],
                                             recv_sem=recv.at[3 * a + k], device_id=(px, py, c), device_id_type=MESH).start()
        token[...] = jnp.zeros_like(token)

    hbm = lambda v: pltpu.with_memory_space_constraint(v, pltpu.HBM)
    out = pl.pallas_call(
        body, name=name,
        out_shape=(pltpu.SemaphoreType.DMA((3 * n,)), pltpu.SemaphoreType.DMA((3 * n,)), *[pltpu.HBM(a.shape, a.dtype) for a in arrs],
                   *[pltpu.HBM(a.shape, a.dtype) for a in arrs], _sds((8, HD))),
        in_specs=[HBM] * (2 * n), out_specs=(SEM, SEM, *[HBM] * (2 * n), pl.BlockSpec(memory_space=pltpu.VMEM)),
        input_output_aliases={i: 2 + i for i in range(2 * n)},
        compiler_params=pltpu.CompilerParams(has_side_effects=DATAFLOW),
    )(*[hbm(a) for a in arrs], *[hbm(lax.empty(a.shape, a.dtype)) for a in arrs])
    return out[0], out[1], list(out[2:2 + n]), list(out[2 + n:2 + 2 * n]), out[-1]


def _chip_exchange_wait(send, recv, srcs, lands, after, name):
    n = len(srcs)

    def body(*refs):
        ins, lands_ = refs[:n], refs[n:2 * n]
        send_, recv_ = refs[2 * n], refs[2 * n + 1]
        x, y, c, chips = _place()
        for a in range(n):
            for k, (px, py) in enumerate(chips):
                cp = pltpu.make_async_remote_copy(src_ref=ins[a].at[2 * px + py], dst_ref=lands_[a].at[2 * px + py], send_sem=send_.at[3 * a + k],
                                                  recv_sem=recv_.at[3 * a + k], device_id=(px, py, c), device_id_type=MESH)
                cp.wait_send()
                cp.wait_recv()

    out = pl.pallas_call(
        body, name=name, out_shape=[pltpu.HBM(v.shape, v.dtype) for v in list(srcs) + list(lands)],
        in_specs=[HBM] * (2 * n) + [SEM, SEM, ANY], out_specs=[HBM] * (2 * n), input_output_aliases={i: i for i in range(2 * n)},
        compiler_params=pltpu.CompilerParams(has_side_effects=DATAFLOW),
    )(*srcs, *lands, send, recv, after)
    return list(out[n:])


def _pair_swap(arrs, name):
    n = len(arrs)

    def body(*refs):
        ins, outs = refs[:n], refs[n:2 * n]
        send, recv = refs[2 * n:]
        x, y, c, _ = _place()
        copies = []
        for a in range(n):
            cp = pltpu.make_async_remote_copy(src_ref=ins[a], dst_ref=outs[a], send_sem=send.at[a], recv_sem=recv.at[a],
                                              device_id=(x, y, 1 - c), device_id_type=MESH)
            cp.start()
            copies.append(cp)
        for cp in copies:
            cp.wait()

    return _comm_call(body, name, arrs, [_sds(a.shape, a.dtype) for a in arrs],
                      [pltpu.SemaphoreType.DMA((n,)), pltpu.SemaphoreType.DMA((n,))])


def _all_gather_devices(v, name):
    def body(v_ref, o_ref, send, recv, loc):
        x, y, c, _ = _place()
        me = 4 * x + 2 * y + c
        own = pltpu.make_async_copy(v_ref, o_ref.at[me], loc)
        own.start()
        copies = [own]
        for k in range(1, 8):
            fx, fy, fc = (k >> 2) & 1, (k >> 1) & 1, k & 1
            peer = (x ^ fx, y ^ fy, c ^ fc)
            r = pltpu.make_async_remote_copy(src_ref=v_ref, dst_ref=o_ref.at[me], send_sem=send.at[k - 1],
                                             recv_sem=recv.at[k - 1], device_id=peer, device_id_type=MESH)
            r.start()
            copies.append(r)
        for cp in copies:
            cp.wait()

    return _comm_call(body, name, [v], [_sds((8,) + v.shape, v.dtype)],
                      [pltpu.SemaphoreType.DMA((7,)), pltpu.SemaphoreType.DMA((7,)), pltpu.SemaphoreType.DMA])[0]


def _row_tile(r):
    return next((b for b in (512, 384, 256, 128, 64, 32, 16) if r % b == 0), r)


def _add2(a, b, out_dtype, name):
    r, w = a.shape
    br = _row_tile(r)

    def body(a_ref, b_ref, o_ref):
        o_ref[...] = (a_ref[...].astype(F32) + b_ref[...].astype(F32)).astype(out_dtype)

    blk = pl.BlockSpec((br, w), lambda i: (i, 0))
    return _pc(body, name, (r // br,), [blk, blk], blk, _sds((r, w), out_dtype))(a, b)


def _sum_slots(a, out_dtype, name, extra=None):
    n, r, w = a.shape
    br = _row_tile(r)

    def body(*refs):
        a_ref, o_ref = refs[0], refs[-1]
        acc = a_ref[0].astype(F32)
        for s in range(1, n):
            acc = acc + a_ref[s].astype(F32)
        if extra is not None:
            acc = acc + refs[1][...].astype(F32)
        o_ref[...] = acc.astype(out_dtype)

    ins = [a] + ([extra] if extra is not None else [])
    specs = [pl.BlockSpec((n, br, w), lambda i: (0, i, 0))] + ([pl.BlockSpec((br, w), lambda i: (i, 0))] if extra is not None else [])
    return _pc(body, name, (r // br,), specs, pl.BlockSpec((br, w), lambda i: (i, 0)), _sds((r, w), out_dtype))(*ins)


SMALL = ["norm_mix_g", "norm_xattn_g", "norm_mlp_g", "final_norm_g", "mem_norm_g", "hgrn_lb_logits", "mlstm_norm_g",
         "hgrn_norm_g", "c_qnorm_g", "c_knorm_g", "ab_gate_b", "c_fgate_b"]
SMALL_ROWS = 16


def _pack_small(parts):
    flat = jnp.concatenate([p.reshape(-1).astype(F32) for p in parts])
    return jnp.pad(flat, (0, SMALL_ROWS * D - flat.shape[0])).reshape(SMALL_ROWS, D)


def _unpack_small(buf, shapes):
    flat, out, off = buf.reshape(-1), [], 0
    for s in shapes:
        n = 1
        for d in s:
            n *= d
        out.append(flat[off:off + n].reshape(s))
        off += n
    return out


def kernel(x, mem, norm_mix_g, norm_xattn_g, norm_mlp_g, final_norm_g, ab_w_in, ab_conv_w, ab_gate_b, hgrn_lb_logits, mlstm_norm_g, hgrn_norm_g, ab_w_out, c_w_in, c_fgate_b, c_qnorm_g, c_knorm_g, c_w_out, mem_norm_g, mem_w_kv, xa_w_q, xa_w_o, mlp_w1, mlp_w2, loss_target, m_norm_mix_g, m_norm_xattn_g, m_norm_mlp_g, m_final_norm_g, m_ab_w_in, m_ab_conv_w, m_ab_gate_b, m_hgrn_lb_logits, m_mlstm_norm_g, m_hgrn_norm_g, m_ab_w_out, m_c_w_in, m_c_fgate_b, m_c_qnorm_g, m_c_knorm_g, m_c_w_out, m_mem_norm_g, m_mem_w_kv, m_xa_w_q, m_xa_w_o, m_mlp_w1, m_mlp_w2, v_norm_mix_g, v_norm_xattn_g, v_norm_mlp_g, v_final_norm_g, v_ab_w_in, v_ab_conv_w, v_ab_gate_b, v_hgrn_lb_logits, v_mlstm_norm_g, v_hgrn_norm_g, v_ab_w_out, v_c_w_in, v_c_fgate_b, v_c_qnorm_g, v_c_knorm_g, v_c_w_out, v_mem_norm_g, v_mem_w_kv, v_xa_w_q, v_xa_w_o, v_mlp_w1, v_mlp_w2):
    A = dict(locals())
    chip = 2 * lax.axis_index("x") + lax.axis_index("y")

    big = ["ab_w_in", "c_w_in", "ab_w_out", "c_w_out", "mem_w_kv", "xa_w_q", "xa_w_o", "mlp_w1", "mlp_w2"]
    shard2d = {"ab_w_in": (D, 1026), "c_w_in": (D, 1026), "ab_w_out": (256, D), "c_w_out": (256, D), "mem_w_kv": (D, 512),
               "xa_w_q": (512, D), "xa_w_o": (512, D), "mlp_w1": (2 * D, D), "mlp_w2": (2 * D, D)}
    shard16 = lambda n: A[n].reshape(shard2d[n]).astype(BF16)
    own_slot = lambda gs, os: [lax.dynamic_update_index_in_dim(g, o, chip, 0) for g, o in zip(gs, os)]
    cols = lambda g: jnp.concatenate([g[k] for k in range(NCHIP)], axis=1)
    per_layer = lambda g: g.reshape(NCHIP, 2, -1, D).transpose(1, 0, 2, 3)
    first = [shard16("ab_w_in"), jnp.pad(ab_conv_w[0], ((0, 16 - CONV_W), (0, 0)))]
    g_in0, g_conv = own_slot(_gather_weights(first, "gather_first"), first)
    W = dict(w_in0=_pack_w_in0(cols(g_in0)))
    rest_names = ["c_w_in", "ab_w_out", "c_w_out", "xa_w_q", "xa_w_o", "mlp_w1", "mlp_w2", "mem_w_kv"]
    rest = [shard16(n) for n in rest_names]
    send_s, recv_s, srcs, lands, token = _gather_start(rest, g_conv, "gather_rest_start")

    def late_weights(after):
        got = _pair_forward(_gather_wait(send_s, recv_s, srcs, lands, after, "gather_rest_wait"), "gather_rest_forward")
        gw = dict(zip(rest_names, own_slot(got, rest)))
        return dict(w_in1=_pack_w_in1(cols(gw["c_w_in"])), w_out0=gw["ab_w_out"].reshape(D, D), w_out1=gw["c_w_out"].reshape(D, D),
                    wkv_s=gw["mem_w_kv"],
                    wq=per_layer(gw["xa_w_q"]).reshape(2, D, D), wo=per_layer(gw["xa_w_o"]).reshape(2, D, D),
                    w1s=gw["mlp_w1"].reshape(NCHIP, 2, D, D), w2=gw["mlp_w2"].reshape(NCHIP, 2, D, D))

    S = dict(norm_mix_g=norm_mix_g + token[0, 0], norm_xattn_g=norm_xattn_g, norm_mlp_g=norm_mlp_g, final_norm_g=final_norm_g,
             conv_w=cols(g_conv[:, :CONV_W]), gate_b=ab_gate_b, lb_logits=hgrn_lb_logits, mlstm_norm_g=mlstm_norm_g,
             hgrn_norm_g=hgrn_norm_g, c_fgate_b=c_fgate_b, c_qnorm_g=c_qnorm_g, c_knorm_g=c_knorm_g, mem_norm_g=mem_norm_g)

    core = lax.axis_index("c")
    by_rows = lambda g: g.reshape(NCHIP, -1, D)

    def stack_cols(g):
        return jnp.stack([g[:, 1026 * k:1026 * (k + 1)] for k in range(NCHIP)])

    def pair_sums(arrs, tag):
        theirs = _pair_exchange(arrs, f"pair_exchange_{tag}")
        out = []
        for i, (a, th) in enumerate(zip(arrs, theirs)):
            h = a.shape[1] // 2
            mine = lax.dynamic_slice_in_dim(a, core * h, h, axis=1)
            out.append(_add2(mine.reshape(-1, a.shape[2]), th.reshape(-1, a.shape[2]), BF16, f"pair_sum_{tag}{i}").reshape(th.shape))
        return out

    def chip_sums(psums, from_chips, tag):
        out = []
        for i, (f, p) in enumerate(zip(from_chips, psums)):
            f = lax.dynamic_update_index_in_dim(f, lax.dynamic_index_in_dim(p, chip, 0, keepdims=False), chip, 0)
            out.append(_sum_slots(f, F32, f"chip_sum_{tag}{i}"))
        return out

    started = {}

    def grads_hook(stage, g):
        if stage == "in0":
            arrs = [stack_cols(_unpack_w_in0(g["w_in"]))]
        else:
            arrs = [jnp.concatenate([by_rows(g["w_out"]), by_rows(g["wq"]), by_rows(g["wo"]), g["w1"], by_rows(g["w2"])], axis=1),
                    stack_cols(_unpack_w_in1(g["w_in"])) if stage == "layer1" else g["wkv"]]
        psums = pair_sums(arrs, stage)
        *handles, token = _chip_exchange_start(psums, f"chip_exchange_start_{stage}")
        started[stage] = (psums, handles)
        return token[0, 0]

    lossp, dx, G = _local_step(x[0], mem[0], loss_target[0], W, S, late_weights, grads_hook)

    gsmall = {"norm_mix_g": G["norm_mix_g"], "norm_xattn_g": G["norm_xattn_g"], "norm_mlp_g": G["norm_mlp_g"],
              "final_norm_g": G["final_norm_g"], "mem_norm_g": G["mem_norm_g"], "hgrn_lb_logits": G["lb_logits"],
              "mlstm_norm_g": G["mlstm_norm_g"], "hgrn_norm_g": G["hgrn_norm_g"], "c_qnorm_g": G["c_qnorm_g"],
              "c_knorm_g": G["c_knorm_g"], "ab_gate_b": G["gate_b"], "c_fgate_b": G["c_fgate_b"]}
    packed = _pack_small([gsmall[n] for n in SMALL] + [G["conv_w"], lossp])
    red = _sum_slots(_all_gather_devices(packed, "gather_small"), F32, "sum_small")
    small_shapes = [A[n].shape for n in SMALL]
    *gs, gconv, loss = _unpack_small(red, small_shapes + [(CONV_W, D), ()])
    gs = dict(zip(SMALL, gs))
    gconv = lax.dynamic_slice_in_dim(gconv, chip * 256, 256, axis=1)[None]

    rhalf = []
    for stage in ("layer1", "layer0", "in0"):
        psums, handles = started[stage]
        rhalf += chip_sums(psums, _chip_exchange_wait(*handles, dx, f"chip_exchange_wait_{stage}"), stage)
    other = _pair_swap(rhalf, "pair_swap")
    r_l1, r_in1, r_l0, r_kv, r_in0 = [
        jnp.where(core == 0, jnp.concatenate([m_, o_], axis=0), jnp.concatenate([o_, m_], axis=0)) for m_, o_ in zip(rhalf, other)]
    both = lambda lo, hi: jnp.concatenate([r_l0[lo:hi], r_l1[lo:hi]], axis=0)
    gbig = {"ab_w_in": r_in0, "c_w_in": r_in1, "mem_w_kv": r_kv, "ab_w_out": r_l0[0:256], "c_w_out": r_l1[0:256],
            "xa_w_q": both(256, 512), "xa_w_o": both(512, 768), "mlp_w1": both(768, 1792), "mlp_w2": both(1792, 2816)}

    out_g, out_d, out_m, out_v = {}, {}, {}, {}
    for n in big:
        d_, m_, v_ = _adam(A[n].reshape(shard2d[n]), gbig[n], A["m_" + n].reshape(shard2d[n]), A["v_" + n].reshape(shard2d[n]), "adam_" + n)
        out_g[n] = gbig[n].reshape(A[n].shape)
        out_d[n], out_m[n], out_v[n] = d_.reshape(A[n].shape), m_.reshape(A[n].shape), v_.reshape(A[n].shape)
    sd, sm, sv = _adam(_pack_small([A[n] for n in SMALL]), _pack_small([gs[n] for n in SMALL]),
                       _pack_small([A["m_" + n] for n in SMALL]), _pack_small([A["v_" + n] for n in SMALL]), "adam_small")
    for n, d_, m_, v_ in zip(SMALL, _unpack_small(sd, small_shapes), _unpack_small(sm, small_shapes), _unpack_small(sv, small_shapes)):
        out_g[n], out_d[n], out_m[n], out_v[n] = gs[n], d_, m_, v_
    cd, cm_, cv = _adam(ab_conv_w[0], gconv[0], m_ab_conv_w[0], v_ab_conv_w[0], "adam_conv")
    out_g["ab_conv_w"], out_d["ab_conv_w"], out_m["ab_conv_w"], out_v["ab_conv_w"] = gconv, cd[None], cm_[None], cv[None]

    order = ["norm_mix_g", "norm_xattn_g", "norm_mlp_g", "final_norm_g", "ab_w_in", "ab_conv_w", "ab_gate_b", "hgrn_lb_logits",
             "mlstm_norm_g", "hgrn_norm_g", "ab_w_out", "c_w_in", "c_fgate_b", "c_qnorm_g", "c_knorm_g", "c_w_out", "mem_norm_g",
             "mem_w_kv", "xa_w_q", "xa_w_o", "mlp_w1", "mlp_w2"]
    return (loss, dx[None], *[out_g[n] for n in order], *[out_d[n] for n in order], *[out_m[n] for n in order],
            *[out_v[n] for n in order])
```

```python
import jax
import jax.numpy as jnp
from jax import lax
from jax.experimental import pallas as pl
from jax.experimental.pallas import tpu as pltpu

F32 = jnp.float32
BF16 = jnp.bfloat16
EPS = 1e-6
D = 1024
CHUNK = 64
REC_CHUNKS = 4
HD = 128
XD = 256
NEG = -1e30
VMEM_LIMIT_V7X = 56 * 1024 * 1024
ADAM_LR, ADAM_B1, ADAM_B2, ADAM_EPS, ADAM_WD, ADAM_STEP = 0.001, 0.9, 0.999, 1e-08, 0.01, 10
MESH = pl.DeviceIdType.MESH


def _pc(body, name, grid, in_specs, out_specs, out_shape, scratch=(), **kw):
    return pl.pallas_call(
        body, name=name, grid=grid, in_specs=in_specs, out_specs=out_specs, out_shape=out_shape,
        scratch_shapes=scratch,
        compiler_params=pltpu.CompilerParams(
            dimension_semantics=("arbitrary",) * len(grid), vmem_limit_bytes=VMEM_LIMIT_V7X), **kw)


def _sds(shape, dtype=F32):
    return jax.ShapeDtypeStruct(shape, dtype)


def _blk(n, target):
    return max(b for b in range(128, max(target, 128) + 1, 128) if n % b == 0)


def _dot(a, b, dims):
    return lax.dot_general(a, b, (dims, ((), ())), preferred_element_type=F32)


def _nn(a, b):
    return _dot(a, b, ((1,), (0,)))


def _nt(a, b):
    return _dot(a, b, ((1,), (1,)))


def _tn(a, b):
    return _dot(a, b, ((0,), (0,)))


def _sigmoid(x):
    return 1.0 / (1.0 + jnp.exp(-x))


def _log_sigmoid(x):
    return jnp.minimum(x, 0.0) - jnp.log(1.0 + jnp.exp(-jnp.abs(x)))


def _rstd(x):
    return lax.rsqrt(jnp.mean(x * x, axis=-1, keepdims=True) + EPS)


def _rms_bwd(du, x, g):
    r = _rstd(x)
    xh = x * r
    dxh = du * g
    dx = r * (dxh - xh * jnp.mean(dxh * xh, axis=-1, keepdims=True))
    return dx, du * xh


def _norm_mm(h, g, w, name, bm=1024, bn=512):
    t, n = h.shape[0], w.shape[1]
    bm, bn = min(bm, t), _blk(n, 3 * bn)

    def body(h_ref, g_ref, w_ref, z_ref, u_ref):
        @pl.when(pl.program_id(1) == 0)
        def _():
            x = h_ref[...]
            u_ref[...] = (x * _rstd(x) * g_ref[...]).astype(BF16)
        z_ref[...] = _nn(u_ref[...], w_ref[...])

    return _pc(body, name, (t // bm, n // bn),
               [pl.BlockSpec((bm, D), lambda i, j: (i, 0)), pl.BlockSpec((1, D), lambda i, j: (0, 0)),
                pl.BlockSpec((D, bn), lambda i, j: (0, j))],
               [pl.BlockSpec((bm, bn), lambda i, j: (i, j)), pl.BlockSpec((bm, D), lambda i, j: (i, 0))],
               [_sds((t, n)), _sds((t, D), BF16)])(h, g, w)


def _mm_tn(a, b, name, bm=1024, bn=1024, bt=4096, col_chips=None):
    t, m = a.shape
    n = b.shape[1]
    bm, bn, bt = _blk(m, bm), (n // col_chips if col_chips else _blk(n, bn + bn // 2)), min(bt, t)
    if (m // bm) * (n // bn) == 1 and bt >= 1024:
        bt //= 4
    nt = t // bt

    def body(a_ref, b_ref, o_ref, acc):
        k = pl.program_id(2)

        @pl.when(k == 0)
        def _():
            acc[...] = jnp.zeros_like(acc)

        acc[...] += _tn(a_ref[...].astype(BF16), b_ref[...].astype(BF16))

        @pl.when(k == nt - 1)
        def _():
            o_ref[...] = acc[...].astype(BF16)

    if col_chips:
        out_spec, out_shape = pl.BlockSpec((None, bm, bn), lambda i, j, k: (j, i, 0)), _sds((col_chips, m, bn), BF16)
    else:
        out_spec, out_shape = pl.BlockSpec((bm, bn), lambda i, j, k: (i, j)), _sds((m, n), BF16)
    return _pc(body, name, (m // bm, n // bn, nt),
               [pl.BlockSpec((bt, bm), lambda i, j, k: (k, i)), pl.BlockSpec((bt, bn), lambda i, j, k: (k, j))],
               out_spec, out_shape, scratch=[pltpu.VMEM((bm, bn), F32)])(a, b)


def _bwd_in(dz, w, h, g, dh, name, bm=1024, bk=1024):
    t, n = dz.shape
    if n > 2 * bk:
        bm, bk = min(bm // 2, t), n
    else:
        bm, bk = min(bm, t), _blk(n, bk + bk // 2)
    nk = n // bk

    def body(dz_ref, w_ref, h_ref, g_ref, dh_ref, o_ref, dg_ref, acc):
        i, k = pl.program_id(0), pl.program_id(1)

        @pl.when(k == 0)
        def _():
            acc[...] = jnp.zeros_like(acc)

        @pl.when((i == 0) & (k == 0))
        def _():
            dg_ref[...] = jnp.zeros_like(dg_ref)

        acc[...] += _nt(dz_ref[...], w_ref[...])

        @pl.when(k == nk - 1)
        def _():
            dx, dgr = _rms_bwd(acc[...], h_ref[...], g_ref[...])
            o_ref[...] = dh_ref[...] + dx
            dg_ref[...] += jnp.sum(dgr, axis=0, keepdims=True)

    return _pc(body, name, (t // bm, nk),
               [pl.BlockSpec((bm, bk), lambda i, k: (i, k)), pl.BlockSpec((D, bk), lambda i, k: (0, k)),
                pl.BlockSpec((bm, D), lambda i, k: (i, 0)), pl.BlockSpec((1, D), lambda i, k: (0, 0)),
                pl.BlockSpec((bm, D), lambda i, k: (i, 0))],
               [pl.BlockSpec((bm, D), lambda i, k: (i, 0)), pl.BlockSpec((1, D), lambda i, k: (0, 0))],
               [_sds((t, D)), _sds((1, D))], scratch=[pltpu.VMEM((bm, D), F32)])(dz, w, h, g, dh)


def _mlp_fwd(h, g, w1s, w2, l, name, bm=1024):
    t = h.shape[0]
    bm = min(bm, t)
    nk = w1s.shape[0]

    def body(h_ref, g_ref, w1_ref, w2_ref, o_ref, a_ref, u_ref, acc):
        k = pl.program_id(1)

        @pl.when(k == 0)
        def _():
            x = h_ref[...]
            u_ref[...] = (x * _rstd(x) * g_ref[...]).astype(BF16)
            acc[...] = jnp.zeros_like(acc)

        a = _nn(u_ref[...], w1_ref[...])
        a_ref[...] = a
        r = jnp.square(jnp.maximum(a, 0.0)).astype(BF16)
        acc[...] += _nn(r, w2_ref[...])

        @pl.when(k == nk - 1)
        def _():
            o_ref[...] = h_ref[...] + acc[...]

    return _pc(body, name, (t // bm, nk),
               [pl.BlockSpec((bm, D), lambda i, k: (i, 0)), pl.BlockSpec((1, D), lambda i, k: (0, 0)),
                pl.BlockSpec((None, None, D, D), lambda i, k: (k, l, 0, 0)), pl.BlockSpec((None, None, D, D), lambda i, k: (k, l, 0, 0))],
               [pl.BlockSpec((bm, D), lambda i, k: (i, 0)), pl.BlockSpec((bm, D), lambda i, k: (i, k)),
                pl.BlockSpec((bm, D), lambda i, k: (i, 0))],
               [_sds((t, D)), _sds((t, nk * D)), _sds((t, D), BF16)],
               scratch=[pltpu.VMEM((bm, D), F32)])(h, g, w1s, w2)


def _mlp_bwd(dh, a, w1s, w2, l, h, g, name, bm=512):
    t = h.shape[0]
    bm = min(bm, t)
    nk = w1s.shape[0]

    def body(dh_ref, a_ref, w1_ref, w2_ref, h_ref, g_ref, o_ref, da_ref, r_ref, dg_ref, acc):
        i, k = pl.program_id(0), pl.program_id(1)

        @pl.when(k == 0)
        def _():
            acc[...] = jnp.zeros_like(acc)

        @pl.when((i == 0) & (k == 0))
        def _():
            dg_ref[...] = jnp.zeros_like(dg_ref)

        ap = jnp.maximum(a_ref[...], 0.0)
        r_ref[...] = jnp.square(ap).astype(BF16)
        dr = _nt(dh_ref[...].astype(BF16), w2_ref[...])
        da = (dr * (2.0 * ap)).astype(BF16)
        da_ref[...] = da
        acc[...] += _nt(da, w1_ref[...])

        @pl.when(k == nk - 1)
        def _():
            dx, dgr = _rms_bwd(acc[...], h_ref[...], g_ref[...])
            o_ref[...] = dh_ref[...] + dx
            dg_ref[...] += jnp.sum(dgr, axis=0, keepdims=True)

    return _pc(body, name, (t // bm, nk),
               [pl.BlockSpec((bm, D), lambda i, k: (i, 0)), pl.BlockSpec((bm, D), lambda i, k: (i, k)),
                pl.BlockSpec((None, None, D, D), lambda i, k: (k, l, 0, 0)), pl.BlockSpec((None, None, D, D), lambda i, k: (k, l, 0, 0)),
                pl.BlockSpec((bm, D), lambda i, k: (i, 0)), pl.BlockSpec((1, D), lambda i, k: (0, 0))],
               [pl.BlockSpec((bm, D), lambda i, k: (i, 0)), pl.BlockSpec((bm, D), lambda i, k: (i, k)),
                pl.BlockSpec((bm, D), lambda i, k: (i, k)), pl.BlockSpec((1, D), lambda i, k: (0, 0))],
               [_sds((t, D)), _sds((t, nk * D), BF16), _sds((t, nk * D), BF16), _sds((1, D))],
               scratch=[pltpu.VMEM((bm, D), F32)])(dh, a, w1s, w2, h, g)


def _rows_of(x):
    return lax.broadcasted_iota(jnp.int32, x.shape, 0)


def _shift_down(x, s):
    if s == 0:
        return x
    return jnp.where(_rows_of(x) >= s, pltpu.roll(x, s, 0), 0.0)


def _shift_up(x, s):
    if s == 0:
        return x
    n = x.shape[0]
    return jnp.where(_rows_of(x) < n - s, pltpu.roll(x, n - s, 0), 0.0)


def _cumsum_rows(x):
    n, s = x.shape[0], 1
    while s < n:
        x = x + _shift_down(x, s)
        s *= 2
    return x


def _rcumsum_rows(x):
    n, s = x.shape[0], 1
    while s < n:
        x = x + _shift_up(x, s)
        s *= 2
    return x


def _silu(x):
    return x * _sigmoid(x)


def _dsilu(x):
    s = _sigmoid(x)
    return s * (1.0 + x * (1.0 - s))


CONV_W = 4


def _conv_pre(u, w):
    y = _shift_down(u, CONV_W - 1) * w[0:1, :]
    for j in range(1, CONV_W):
        y = y + _shift_down(u, CONV_W - 1 - j) * w[j:j + 1, :]
    return y


def _conv_fwd(z0, cw, name):
    t = z0.shape[0]

    def body(u_ref, w_ref, o_ref):
        o_ref[...] = _silu(_conv_pre(u_ref[...], w_ref[...]))

    return _pc(body, name, (2 * 512 // HD,),
               [pl.BlockSpec((t, HD), lambda c: (0, c)), pl.BlockSpec((CONV_W, HD), lambda c: (0, c))],
               pl.BlockSpec((t, HD), lambda c: (0, c)), _sds((t, 1024)))(z0, cw)


def _conv_bwd(z0, cw, dy, name):
    t = z0.shape[0]

    def body(u_ref, w_ref, dy_ref, du_ref, dw_ref):
        u, w = u_ref[...], w_ref[...]
        dpre = dy_ref[...] * _dsilu(_conv_pre(u, w))
        du = _shift_up(dpre, CONV_W - 1) * w[0:1, :]
        for j in range(1, CONV_W):
            du = du + _shift_up(dpre, CONV_W - 1 - j) * w[j:j + 1, :]
        du_ref[...] = du.astype(BF16)
        for j in range(CONV_W):
            dw_ref[j:j + 1, :] = jnp.sum(dpre * _shift_down(u, CONV_W - 1 - j), axis=0, keepdims=True)

    return _pc(body, name, (2 * 512 // HD,),
               [pl.BlockSpec((t, HD), lambda c: (0, c)), pl.BlockSpec((CONV_W, HD), lambda c: (0, c)),
                pl.BlockSpec((t, HD), lambda c: (0, c))],
               [pl.BlockSpec((t, HD), lambda c: (0, c)), pl.BlockSpec((CONV_W, HD), lambda c: (0, c))],
               [_sds((t, 1024), BF16), _sds((CONV_W, 1024))])(z0, cw, dy)


def _mlstm_gates(gate, bias, m_in):
    L = gate.shape[0]
    r = lax.broadcasted_iota(jnp.int32, (L, L), 0)
    c = lax.broadcasted_iota(jnp.int32, (L, L), 1)
    eye, tril = r == c, c <= r
    i_col = gate[:, 0:1] + bias[:, 0:1]
    f_col = gate[:, 1:2] + bias[:, 1:2]
    logf_col = _log_sigmoid(f_col)
    logf_row = jnp.sum(jnp.where(eye, logf_col, 0.0), axis=0, keepdims=True)
    i_row = jnp.sum(jnp.where(eye, i_col, 0.0), axis=0, keepdims=True)
    b_col = jnp.sum(jnp.where(tril, logf_row, 0.0), axis=1, keepdims=True)
    b_row = jnp.sum(jnp.where(r <= c, logf_col, 0.0), axis=0, keepdims=True)
    logd = jnp.where(tril, b_col - b_row + i_row, NEG)
    inter = b_col + m_in
    m_t = jnp.maximum(inter, jnp.max(logd, axis=1, keepdims=True))
    w_t = jnp.exp(inter - m_t)
    dm = jnp.exp(logd - m_t)
    b_last = b_col[L - 1:L, :]
    log_in = b_last - b_col + i_col
    m_new = jnp.maximum(b_last + m_in, jnp.max(log_in, axis=0, keepdims=True))
    w_col = jnp.exp(log_in - m_new)
    decay = jnp.exp(b_last + m_in - m_new)
    return dict(eye=eye, r=r, c=c, f_col=f_col, m_t=m_t, w_t=w_t, dm=dm, m_new=m_new, w_col=w_col, decay=decay)


def _mlstm_fwd(qk, z0, gates, bias, name):
    t = qk.shape[0]
    nc, nh, L = t // CHUNK, 4, CHUNK
    scale = HD ** -0.5

    def body(q_ref, k_ref, v_ref, g_ref, b_ref, h_ref, cs_ref, ns_ref, ms_ref, c_s, n_s, m_s):
        @pl.when(pl.program_id(0) == 0)
        def _():
            c_s[...] = jnp.zeros_like(c_s)
            n_s[...] = jnp.zeros_like(n_s)
            m_s[...] = jnp.zeros_like(m_s)

        for hd in range(nh):
            sl = slice(hd * HD, (hd + 1) * HD)
            cm, nv, m_in = c_s[hd], n_s[hd], m_s[hd]
            for ck in range(cps):
                rows = slice(ck * L, (ck + 1) * L)
                cs_ref[hd, ck] = cm
                ns_ref[hd, ck] = nv
                ms_ref[hd, ck] = jnp.broadcast_to(m_in, (1, HD))
                q, kh, v = q_ref[rows, sl], k_ref[rows, sl] * scale, v_ref[rows, sl]
                G = _mlstm_gates(g_ref[hd, rows, :], b_ref[hd], m_in)
                qb, kb, vb = q.astype(BF16), kh.astype(BF16), v.astype(BF16)
                sc = _nt(qb, kb) * G["dm"]
                num = _nn(sc.astype(BF16), vb) + G["w_t"] * _nn(qb, cm.astype(BF16))
                den = jnp.sum(sc, axis=1, keepdims=True) + G["w_t"] * jnp.sum(q * nv, axis=1, keepdims=True)
                h_ref[rows, sl] = num / jnp.maximum(jnp.abs(den), jnp.exp(-G["m_t"]))
                wk = G["w_col"] * kh
                cm = G["decay"] * cm + _tn(wk.astype(BF16), vb)
                nv = G["decay"] * nv + jnp.sum(wk, axis=0, keepdims=True)
                m_in = G["m_new"]
            c_s[hd], n_s[hd], m_s[hd] = cm, nv, m_in

    cps = REC_CHUNKS
    hspec = lambda blk: pl.BlockSpec((cps * L, 512), lambda j: (j, blk))
    st = lambda r: pl.BlockSpec((nh, cps, r, HD), lambda j: (0, j, 0, 0))
    return _pc(body, name, (nc // cps,),
               [hspec(0), hspec(1), hspec(2), pl.BlockSpec((nh, cps * L, 2), lambda j: (0, j, 0)),
                pl.BlockSpec((nh, 1, 2), lambda j: (0, 0, 0))],
               [hspec(0), st(HD), st(1), st(1)],
               [_sds((t, 512)), _sds((nh, nc, HD, HD)), _sds((nh, nc, 1, HD)), _sds((nh, nc, 1, HD))],
               scratch=[pltpu.VMEM((nh, HD, HD), F32), pltpu.VMEM((nh, 1, HD), F32), pltpu.VMEM((nh, 1, 1), F32)])(qk, qk, z0, gates, bias)


def _mlstm_bwd(qk, z0, gates, bias, cs, ns, ms, dh, name):
    t = qk.shape[0]
    nc, nh, L = t // CHUNK, 4, CHUNK
    scale = HD ** -0.5

    def body(q_ref, k_ref, v_ref, g_ref, b_ref, cs_ref, ns_ref, ms_ref, dh_ref, dqk_ref, dv_ref, dg_ref, dc_s, dn_s):
        @pl.when(pl.program_id(0) == 0)
        def _():
            dc_s[...] = jnp.zeros_like(dc_s)
            dn_s[...] = jnp.zeros_like(dn_s)

        for ck in reversed(range(cps)):
            for hd in range(nh):
                one_head(hd, ck, slice(hd * HD, (hd + 1) * HD), slice(ck * L, (ck + 1) * L), q_ref, k_ref, v_ref, g_ref, b_ref,
                         cs_ref, ns_ref, ms_ref, dh_ref, dqk_ref, dv_ref, dg_ref, dc_s, dn_s)

    def one_head(hd, ck, sl, rows, q_ref, k_ref, v_ref, g_ref, b_ref, cs_ref, ns_ref, ms_ref, dh_ref, dqk_ref, dv_ref, dg_ref,
                 dc_s, dn_s):
        cm, nv, m_in = cs_ref[hd, ck], ns_ref[hd, ck], ms_ref[hd, ck][:, 0:1]
        q, kh, v = q_ref[rows, sl], k_ref[rows, sl] * scale, v_ref[rows, sl]
        G = _mlstm_gates(g_ref[hd, rows, :], b_ref[hd], m_in)
        w_t, dmat, w_col, decay = G["w_t"], G["dm"], G["w_col"], G["decay"]
        qb, kb, vb, cb = q.astype(BF16), kh.astype(BF16), v.astype(BF16), cm.astype(BF16)
        s = _nt(qb, kb)
        sc = s * dmat
        scb = sc.astype(BF16)
        qc = _nn(qb, cb)
        qn = jnp.sum(q * nv, axis=1, keepdims=True)
        num = _nn(scb, vb) + w_t * qc
        den = jnp.sum(sc, axis=1, keepdims=True) + w_t * qn
        e_m = jnp.exp(-G["m_t"])
        dnm = jnp.maximum(jnp.abs(den), e_m)
        dh_ = dh_ref[rows, sl]
        dnum = dh_ / dnm
        dden = jnp.where(jnp.abs(den) > e_m, -jnp.sum(dh_ * num, axis=1, keepdims=True) / (dnm * dnm) * jnp.sign(den), 0.0)
        dnumb = dnum.astype(BF16)
        dsc = _nt(dnumb, vb) + dden
        dv = _tn(scb, dnumb)
        wd = w_t * dnum
        wdb = wd.astype(BF16)
        ds = dsc * dmat
        dsb = ds.astype(BF16)
        dq = _nt(wdb, cb) + (w_t * dden) * nv + _nn(dsb, kb)
        dc_o = _tn(qb, wdb)
        dn_o = jnp.sum(q * (w_t * dden), axis=0, keepdims=True)
        dw = jnp.sum(dnum * qc, axis=1, keepdims=True) + dden * qn
        dkh = _tn(dsb, qb)
        dlogd = ds * s
        db_col = jnp.sum(dlogd, axis=1, keepdims=True) + dw * w_t
        csum = jnp.sum(dlogd, axis=0, keepdims=True)
        dcn, dnn = dc_s[hd], dn_s[hd]
        dcnb = dcn.astype(BF16)
        kdc = _nn(kb, dcnb)
        dws = jnp.sum(kdc * v, axis=1, keepdims=True) + jnp.sum(kh * dnn, axis=1, keepdims=True)
        dv = dv + w_col * kdc
        dkh = dkh + w_col * (_nt(vb, dcnb) + dnn)
        dlin = dws * w_col
        ddecay = jnp.sum(jnp.sum(dcn * cm, axis=1, keepdims=True), axis=0, keepdims=True) + jnp.sum(dnn * nv, axis=1, keepdims=True)
        dlast = ddecay * decay + jnp.sum(dlin, axis=0, keepdims=True)
        row_id = lax.broadcasted_iota(jnp.int32, (L, 1), 0)
        db_col = db_col - dlin + jnp.where(row_id == L - 1, dlast, 0.0)
        eye, r, c = G["eye"], G["r"], G["c"]
        di = dlin + jnp.sum(jnp.where(eye, csum, 0.0), axis=1, keepdims=True)
        db_row = jnp.sum(jnp.where(eye, db_col, 0.0), axis=0, keepdims=True) - csum
        dlogf = jnp.sum(jnp.where(c >= r, db_row, 0.0), axis=1, keepdims=True)
        dg_ref[hd, rows, 0:1] = di
        dg_ref[hd, rows, 1:2] = dlogf * (1.0 - _sigmoid(G["f_col"]))
        dqk_ref[rows, sl] = dq
        dqk_ref[rows, 512 + hd * HD:512 + (hd + 1) * HD] = dkh * scale
        dv_ref[rows, sl] = dv
        dc_s[hd] = decay * dcn + dc_o
        dn_s[hd] = decay * dnn + dn_o

    cps = REC_CHUNKS
    rv = lambda j: nc // cps - 1 - j
    hspec = lambda blk: pl.BlockSpec((cps * L, 512), lambda j: (rv(j), blk))
    st = lambda r: pl.BlockSpec((nh, cps, r, HD), lambda j: (0, rv(j), 0, 0))
    gs = pl.BlockSpec((nh, cps * L, 2), lambda j: (0, rv(j), 0))
    return _pc(body, name, (nc // cps,),
               [hspec(0), hspec(1), hspec(2), gs, pl.BlockSpec((nh, 1, 2), lambda j: (0, 0, 0)),
                st(HD), st(1), st(1), hspec(0)],
               [pl.BlockSpec((cps * L, 1024), lambda j: (rv(j), 0)), hspec(0), gs],
               [_sds((t, 1024)), _sds((t, 512)), _sds((nh, t, 2))],
               scratch=[pltpu.VMEM((nh, HD, HD), F32), pltpu.VMEM((nh, 1, HD), F32)])(qk, qk, z0, gates, bias, cs, ns, ms, dh)


def _hgrn_act(qb_, fb_, ib_, lg):
    lb = _sigmoid(lg[0:1, :] - lg[1:2, :])
    sg = _sigmoid(fb_)
    f = lb + (1.0 - lb) * sg
    return lb, sg, f, _silu(qb_), (1.0 - lb) * (1.0 - sg), _silu(ib_), _cumsum_rows(jnp.log(f))


HG_SUB = 16


def _hgrn_offdiag(q, k, b, r0):
    beta = b[r0 - 1:r0, :]
    e1 = jnp.exp(b[r0:r0 + HG_SUB, :] - beta)
    e2 = jnp.where(_rows_of(b) < r0, jnp.exp(jnp.minimum(beta - b, 0.0)), 0.0)
    return q[r0:r0 + HG_SUB, :] * e1, k * e2, e1, e2


def _hgrn_fwd(z0, lbl, name):
    t = z0.shape[0]
    nc, nh, L = t // CHUNK, 4, CHUNK

    def body(q_ref, f_ref, i_ref, l_ref, o_ref, ss_ref, st_s):
        @pl.when(pl.program_id(0) == 0)
        def _():
            st_s[...] = jnp.zeros_like(st_s)

        for hd in range(nh):
            sl = slice(hd * HD, (hd + 1) * HD)
            st = st_s[hd]
            for ck in range(cps):
                rows = slice(ck * L, (ck + 1) * L)
                ss_ref[hd, ck] = st
                _, _, _, q, k, v, b = _hgrn_act(q_ref[rows, sl], f_ref[rows, sl], i_ref[rows, sl], l_ref[:, sl])
                o = _nt((q * jnp.exp(b)).astype(BF16), st.astype(BF16))
                sub = _rows_of(b) & (HG_SUB - 1)
                o = o + jnp.sum(q * k, axis=1, keepdims=True) * v
                for dl in range(1, HG_SUB):
                    e = jnp.exp(jnp.where(sub >= dl, b - pltpu.roll(b, dl, 0), NEG))
                    a = jnp.sum(q * pltpu.roll(k, dl, 0) * e, axis=1, keepdims=True)
                    o = o + a * pltpu.roll(v, dl, 0)
                o_ref[rows, sl] = o
                vb = v.astype(BF16)
                for i in range(1, L // HG_SUB):
                    r0 = i * HG_SUB
                    qt, kt, _, _ = _hgrn_offdiag(q, k, b, r0)
                    a = _nt(qt.astype(BF16), kt.astype(BF16))
                    o_ref[ck * L + r0:ck * L + r0 + HG_SUB, sl] += _nn(a.astype(BF16), vb)
                bl = b[L - 1:L, :]
                st = st * jnp.exp(bl) + _tn(v.astype(BF16), (k * jnp.exp(bl - b)).astype(BF16))
            st_s[hd] = st

    cps = REC_CHUNKS
    hspec = lambda blk: pl.BlockSpec((cps * L, 512), lambda j: (j, blk))
    return _pc(body, name, (nc // cps,),
               [hspec(4), hspec(5), hspec(6), pl.BlockSpec((2, 512), lambda j: (0, 0))],
               [hspec(0), pl.BlockSpec((nh, cps, HD, HD), lambda j: (0, j, 0, 0))],
               [_sds((t, 512)), _sds((nh, nc, HD, HD))],
               scratch=[pltpu.VMEM((nh, HD, HD), F32)])(z0, z0, z0, lbl)


def _hgrn_bwd(z0, lbl, ss, do, name):
    t = z0.shape[0]
    nc, nh, L = t // CHUNK, 4, CHUNK

    def body(q_ref, f_ref, i_ref, l_ref, ss_ref, do_ref, dq_ref, df_ref, di_ref, dl_ref, dst_s, dlb_s, dq_a, dk_a, dv_a, db_a):
        @pl.when(pl.program_id(0) == 0)
        def _():
            dst_s[...] = jnp.zeros_like(dst_s)
            dlb_s[...] = jnp.zeros_like(dlb_s)

        for ck in reversed(range(cps)):
            for hd in range(nh):
                one_head(hd, ck, slice(hd * HD, (hd + 1) * HD), slice(ck * L, (ck + 1) * L), q_ref, f_ref, i_ref, l_ref, ss_ref, do_ref,
                         dq_ref, df_ref, di_ref, dl_ref, dst_s, dlb_s, dq_a.at[hd], dk_a.at[hd], dv_a.at[hd], db_a.at[hd])

    def one_head(hd, ck, sl, rs, q_ref, f_ref, i_ref, l_ref, ss_ref, do_ref, dq_ref, df_ref, di_ref, dl_ref, dst_s, dlb_s,
                 dq_a, dk_a, dv_a, db_a):
        st = ss_ref[hd, ck]
        qp, fp, ip = q_ref[rs, sl], f_ref[rs, sl], i_ref[rs, sl]
        lb, sg, f, q, k, v, b = _hgrn_act(qp, fp, ip, l_ref[:, sl])
        do_ = do_ref[rs, sl]
        dob, stb = do_.astype(BF16), st.astype(BF16)
        eb = jnp.exp(b)
        qe = q * eb
        dqe = _nn(dob, stb)
        dst_o = _tn(dob, qe.astype(BF16))
        dq = dqe * eb
        db = dqe * qe
        rows = _rows_of(b)
        sub = rows & (HG_SUB - 1)
        p0 = jnp.sum(do_ * v, axis=1, keepdims=True)
        dq = dq + p0 * k
        dk = p0 * q
        dv = jnp.sum(q * k, axis=1, keepdims=True) * do_
        for dl in range(1, HG_SUB):
            up = L - dl
            kd, vd = pltpu.roll(k, dl, 0), pltpu.roll(v, dl, 0)
            e = jnp.exp(jnp.where(sub >= dl, b - pltpu.roll(b, dl, 0), NEG))
            a = jnp.sum(q * kd * e, axis=1, keepdims=True)
            p = jnp.sum(do_ * vd, axis=1, keepdims=True) * e
            dq = dq + p * kd
            dkd = p * q
            dbb = dkd * kd
            dv = dv + pltpu.roll(a * do_, up, 0)
            dk = dk + pltpu.roll(dkd, up, 0)
            db = db + dbb - pltpu.roll(dbb, up, 0)
        dq_a[...], dk_a[...], dv_a[...], db_a[...] = dq, dk, dv, db
        vb = v.astype(BF16)
        for i in range(1, L // HG_SUB):
            r0 = i * HG_SUB
            blk = slice(r0, r0 + HG_SUB)
            qt, kt, e1, e2 = _hgrn_offdiag(q, k, b, r0)
            qtb, ktb, dob_i = qt.astype(BF16), kt.astype(BF16), do_[blk, :].astype(BF16)
            a = _nt(qtb, ktb).astype(BF16)
            da = _nt(dob_i, vb).astype(BF16)
            dv_a[...] += _tn(a, dob_i)
            dqt = _nn(da, ktb)
            dkt = _tn(da, qtb)
            dq_a[blk, :] += dqt * e1
            t1, t2 = dqt * qt, dkt * kt
            db_a[blk, :] += t1
            dk_a[...] += dkt * e2
            db_a[...] -= t2
            db_a[r0 - 1:r0, :] += jnp.sum(t2, axis=0, keepdims=True) - jnp.sum(t1, axis=0, keepdims=True)
        dq, dk, dv, db = dq_a[...], dk_a[...], dv_a[...], db_a[...]
        dstn = dst_s[hd]
        dstnb = dstn.astype(BF16)
        bl = b[L - 1:L, :]
        ebl = jnp.exp(bl)
        kdec_e = jnp.exp(bl - b)
        kdec = k * kdec_e
        dbl = jnp.sum(dstn * st, axis=0, keepdims=True) * ebl
        dv = dv + _nt(kdec.astype(BF16), dstnb)
        dkdec = _nn(v.astype(BF16), dstnb)
        dk = dk + dkdec * kdec_e
        dx = dkdec * kdec
        dbl = dbl + jnp.sum(dx, axis=0, keepdims=True)
        db = db - dx + jnp.where(rows == L - 1, dbl, 0.0)
        dst_s[hd] = dstn * ebl + dst_o
        dg = _rcumsum_rows(db)
        dfk = dg / f - dk
        dq_ref[rs, sl] = (dq * _dsilu(qp)).astype(BF16)
        di_ref[rs, sl] = (dv * _dsilu(ip)).astype(BF16)
        df_ref[rs, sl] = (dfk * (1.0 - lb) * sg * (1.0 - sg)).astype(BF16)
        dlb_s[hd] += jnp.sum(dfk * (1.0 - sg), axis=0, keepdims=True)

        if ck == 0:
            @pl.when(pl.program_id(0) == nc // cps - 1)
            def _():
                dl0 = dlb_s[hd] * lb * (1.0 - lb)
                dl_ref[0:1, sl] = dl0
                dl_ref[1:2, sl] = -dl0

    cps = REC_CHUNKS
    rv = lambda j: nc // cps - 1 - j
    hspec = lambda blk: pl.BlockSpec((cps * L, 512), lambda j: (rv(j), blk))
    return _pc(body, name, (nc // cps,),
               [hspec(4), hspec(5), hspec(6), pl.BlockSpec((2, 512), lambda j: (0, 0)),
                pl.BlockSpec((nh, cps, HD, HD), lambda j: (0, rv(j), 0, 0)), hspec(0)],
               [hspec(0), hspec(0), hspec(0), pl.BlockSpec((2, 512), lambda j: (0, 0))],
               [_sds((t, 512), BF16), _sds((t, 512), BF16), _sds((t, 512), BF16), _sds((2, 512))],
               scratch=[pltpu.VMEM((nh, HD, HD), F32), pltpu.VMEM((nh, 1, HD), F32)] + [pltpu.VMEM((nh, L, HD), F32)] * 4)(z0, z0, z0, lbl, ss, do)


def _post0_fwd(hm, hh, z0, na, nb, w, h0, name, bm=512):
    t = h0.shape[0]
    bm = min(bm, t)

    def body(hm_ref, hh_ref, oa_ref, gb_ref, na_ref, nb_ref, w_ref, h_ref, o_ref, y_ref):
        for hd in range(4):
            sl = slice(hd * HD, (hd + 1) * HD)
            pa = _sigmoid(oa_ref[:, sl]) * hm_ref[:, sl]
            y_ref[:, sl] = (pa * _rstd(pa) * na_ref[:, sl]).astype(BF16)
            xb = hh_ref[:, sl]
            y_ref[:, 512 + hd * HD:512 + (hd + 1) * HD] = (xb * _rstd(xb) * nb_ref[:, sl] * _silu(gb_ref[:, sl])).astype(BF16)
        o_ref[...] = h_ref[...] + _nn(y_ref[...], w_ref[...])

    row = lambda wd, c: pl.BlockSpec((bm, wd), lambda i: (i, c))
    vec = lambda wd: pl.BlockSpec((1, wd), lambda i: (0, 0))
    return _pc(body, name, (t // bm,),
               [row(512, 0), row(512, 0), row(512, 3), row(512, 7), vec(512), vec(512),
                pl.BlockSpec((D, D), lambda i: (0, 0)), row(D, 0)],
               [row(D, 0), row(D, 0)], [_sds((t, D)), _sds((t, D), BF16)])(hm, hh, z0, z0, na, nb, w, h0)


def _post0_bwd(dh1, w, hm, hh, z0, na, nb, name, bm=512):
    t = dh1.shape[0]
    bm = min(bm, t)

    def body(dh_ref, w_ref, hm_ref, hh_ref, oa_ref, gb_ref, na_ref, nb_ref, dhm_ref, dhh_ref, doa_ref, dgb_ref, dna_ref, dnb_ref):
        @pl.when(pl.program_id(0) == 0)
        def _():
            dna_ref[...] = jnp.zeros_like(dna_ref)
            dnb_ref[...] = jnp.zeros_like(dnb_ref)

        dy = _nt(dh_ref[...].astype(BF16), w_ref[...])
        for hd in range(4):
            sl = slice(hd * HD, (hd + 1) * HD)
            hm_, oa = hm_ref[:, sl], oa_ref[:, sl]
            sg = _sigmoid(oa)
            dpa, dgr = _rms_bwd(dy[:, sl], sg * hm_, na_ref[:, sl])
            dna_ref[:, sl] += jnp.sum(dgr, axis=0, keepdims=True)
            doa_ref[:, sl] = (dpa * hm_ * sg * (1.0 - sg)).astype(BF16)
            dhm_ref[:, sl] = dpa * sg
            xb, gb, nbv = hh_ref[:, sl], gb_ref[:, sl], nb_ref[:, sl]
            dyb = dy[:, 512 + hd * HD:512 + (hd + 1) * HD]
            dgb_ref[:, sl] = (dyb * (xb * _rstd(xb) * nbv) * _dsilu(gb)).astype(BF16)
            dxb, dgr2 = _rms_bwd(dyb * _silu(gb), xb, nbv)
            dnb_ref[:, sl] += jnp.sum(dgr2, axis=0, keepdims=True)
            dhh_ref[:, sl] = dxb

    row = lambda wd, c: pl.BlockSpec((bm, wd), lambda i: (i, c))
    vec = lambda wd: pl.BlockSpec((1, wd), lambda i: (0, 0))
    return _pc(body, name, (t // bm,),
               [row(D, 0), pl.BlockSpec((D, D), lambda i: (0, 0)), row(512, 0), row(512, 0), row(512, 3), row(512, 7),
                vec(512), vec(512)],
               [row(512, 0), row(512, 0), row(512, 0), row(512, 0), vec(512), vec(512)],
               [_sds((t, 512)), _sds((t, 512)), _sds((t, 512), BF16), _sds((t, 512), BF16), _sds((1, 512)), _sds((1, 512))],
               )(dh1, w, hm, hh, z0, z0, na, nb)


def _memkv_fwd(mem, g, wkv_s, name):
    m = mem.shape[0]

    def body(x_ref, g_ref, w_ref, kv_ref, mn_ref):
        x = x_ref[...]
        mn = (x * _rstd(x) * g_ref[...]).astype(BF16)
        mn_ref[...] = mn
        kv_ref[...] = _nn(mn, w_ref[...])

    return _pc(body, name, (4,),
               [pl.BlockSpec((m, D), lambda k: (0, 0)), pl.BlockSpec((1, D), lambda k: (0, 0)),
                pl.BlockSpec((None, D, 512), lambda k: (k, 0, 0))],
               [pl.BlockSpec((m, 512), lambda k: (0, k)), pl.BlockSpec((m, D), lambda k: (0, 0))],
               [_sds((m, 2048)), _sds((m, D), BF16)])(mem, g, wkv_s)


def _memkv_bwd(dkv, wkv_s, mem, g, name):
    m = mem.shape[0]

    def body(d_ref, w_ref, x_ref, g_ref, dg_ref, acc):
        k = pl.program_id(0)

        @pl.when(k == 0)
        def _():
            acc[...] = jnp.zeros_like(acc)

        acc[...] += _nt(d_ref[...].astype(BF16), w_ref[...])

        @pl.when(k == 3)
        def _():
            _, dgr = _rms_bwd(acc[...], x_ref[...], g_ref[...])
            dg_ref[...] = jnp.sum(dgr, axis=0, keepdims=True)

    return _pc(body, name, (4,),
               [pl.BlockSpec((m, 512), lambda k: (0, k)), pl.BlockSpec((None, D, 512), lambda k: (k, 0, 0)),
                pl.BlockSpec((m, D), lambda k: (0, 0)), pl.BlockSpec((1, D), lambda k: (0, 0))],
               pl.BlockSpec((1, D), lambda k: (0, 0)), _sds((1, D)), scratch=[pltpu.VMEM((m, D), F32)])(dkv, wkv_s, mem, g)


def _xattn_probs(qh, kh):
    s = _nt(qh, kh) * (XD ** -0.5)
    p = jnp.exp(s - jnp.max(s, axis=1, keepdims=True))
    return p / jnp.sum(p, axis=1, keepdims=True)


def _xattn_fwd(q, kv, wo, h1, name, bm=512):
    t, m = q.shape[0], kv.shape[0]
    bm = min(bm, t)

    def body(q_ref, k_ref, v_ref, w_ref, h_ref, out_ref, o_ref):
        for hd in range(D // XD):
            sl = slice(hd * XD, (hd + 1) * XD)
            p = _xattn_probs(q_ref[:, sl].astype(BF16), k_ref[:, sl].astype(BF16))
            o_ref[:, sl] = _nn(p.astype(BF16), v_ref[:, sl].astype(BF16)).astype(BF16)
        out_ref[...] = h_ref[...] + _nn(o_ref[...], w_ref[...])

    row = pl.BlockSpec((bm, D), lambda i: (i, 0))
    return _pc(body, name, (t // bm,),
               [row, pl.BlockSpec((m, D), lambda i: (0, 0)), pl.BlockSpec((m, D), lambda i: (0, 1)),
                pl.BlockSpec((D, D), lambda i: (0, 0)), row],
               [row, row], [_sds((t, D)), _sds((t, D), BF16)])(q, kv, kv, wo, h1)


def _xattn_bwd(dh2, q, kv, wo, name, bm=512):
    t, m = q.shape[0], kv.shape[0]
    bm = min(bm, t)

    def body(dh_ref, q_ref, k_ref, v_ref, w_ref, dq_ref, dkv_ref):
        @pl.when(pl.program_id(0) == 0)
        def _():
            dkv_ref[...] = jnp.zeros_like(dkv_ref)

        d_o = _nt(dh_ref[...].astype(BF16), w_ref[...])
        for hd in range(D // XD):
            sl = slice(hd * XD, (hd + 1) * XD)
            qh, kh, vh = q_ref[:, sl].astype(BF16), k_ref[:, sl].astype(BF16), v_ref[:, sl].astype(BF16)
            p = _xattn_probs(qh, kh)
            dob = d_o[:, sl].astype(BF16)
            dp = _nt(dob, vh)
            dkv_ref[:, D + hd * XD:D + (hd + 1) * XD] += _tn(p.astype(BF16), dob)
            ds = (p * (dp - jnp.sum(dp * p, axis=1, keepdims=True)) * (XD ** -0.5)).astype(BF16)
            dq_ref[:, sl] = _nn(ds, kh).astype(BF16)
            dkv_ref[:, sl] += _tn(ds, qh)

    row = pl.BlockSpec((bm, D), lambda i: (i, 0))
    return _pc(body, name, (t // bm,),
               [row, row, pl.BlockSpec((m, D), lambda i: (0, 0)), pl.BlockSpec((m, D), lambda i: (0, 1)),
                pl.BlockSpec((D, D), lambda i: (0, 0))],
               [row, pl.BlockSpec((m, 2 * D), lambda i: (0, 0))],
               [_sds((t, D), BF16), _sds((m, 2 * D))])(dh2, q, kv, kv, wo)


NH1 = 8
FOX_BM = 512
FOX_BQ = 512
FOX_BK = 512
FOX_HEADS_PER_STEP = 4


def _foxprep_fwd(z1, qg, kg, fbp, name):
    t = z1.shape[0]
    bm = min(FOX_BM, t)

    def body(q_ref, k_ref, v_ref, f_ref, qg_ref, kg_ref, fb_ref, qn_ref, kn_ref, vb_ref, c_ref, carry):
        @pl.when(pl.program_id(0) == 0)
        def _():
            carry[...] = jnp.zeros_like(carry)

        for hd in range(NH1):
            sl = slice(hd * HD, (hd + 1) * HD)
            x = q_ref[:, sl]
            qn_ref[:, sl] = (x * _rstd(x) * qg_ref[...] * FOX_QSCALE).astype(BF16)
            x = k_ref[:, sl]
            kn_ref[:, sl] = (x * _rstd(x) * kg_ref[...]).astype(BF16)
        vb_ref[...] = v_ref[...].astype(BF16)
        c = carry[...] + _cumsum_rows(_log_sigmoid(f_ref[...] + fb_ref[...]))
        c_ref[...] = c
        carry[...] = c[bm - 1:bm, :]

    row = lambda c: pl.BlockSpec((bm, D), lambda i: (i, c))
    lane = pl.BlockSpec((bm, HD), lambda i: (i, 4 * D // HD))
    vec = pl.BlockSpec((1, HD), lambda i: (0, 0))
    return _pc(body, name, (t // bm,), [row(0), row(1), row(2), lane, vec, vec, vec],
               [row(0), row(0), row(0), pl.BlockSpec((bm, HD), lambda i: (i, 0))],
               [_sds((t, D), BF16), _sds((t, D), BF16), _sds((t, D), BF16), _sds((t, HD))],
               scratch=[pltpu.VMEM((1, HD), F32)])(z1, z1, z1, z1, qg, kg, fbp)


def _foxprep_bwd(dqn, dkn, dv, dgate, z1, qg, kg, fbp, dc, name):
    t = z1.shape[0]
    bm = min(FOX_BM, t)
    nb = t // bm

    def body(dqn_ref, dkn_ref, dv_ref, dgt_ref, q_ref, k_ref, f_ref, qg_ref, kg_ref, fb_ref, dc_ref,
             dz_ref, dqg_ref, dkg_ref, dfb_ref, carry):
        @pl.when(pl.program_id(0) == 0)
        def _():
            carry[...] = jnp.zeros_like(carry)
            dqg_ref[...] = jnp.zeros_like(dqg_ref)
            dkg_ref[...] = jnp.zeros_like(dkg_ref)
            dfb_ref[...] = jnp.zeros_like(dfb_ref)

        for hd in range(NH1):
            sl = slice(hd * HD, (hd + 1) * HD)
            dx, dgr = _rms_bwd(dqn_ref[:, sl] * (HD ** -0.5), q_ref[:, sl], qg_ref[...])
            dz_ref[:, sl] = dx.astype(BF16)
            dqg_ref[...] += jnp.sum(dgr, axis=0, keepdims=True)
            dx, dgr = _rms_bwd(dkn_ref[:, sl], k_ref[:, sl], kg_ref[...])
            dz_ref[:, D + hd * HD:D + (hd + 1) * HD] = dx.astype(BF16)
            dkg_ref[...] += jnp.sum(dgr, axis=0, keepdims=True)
        dz_ref[:, 2 * D:3 * D] = dv_ref[...].astype(BF16)
        dz_ref[:, 3 * D:4 * D] = dgt_ref[...]
        dc_ = dc_ref[...]
        dlogf = _rcumsum_rows(dc_) + carry[...]
        carry[...] += jnp.sum(dc_, axis=0, keepdims=True)
        lanes = lax.broadcasted_iota(jnp.int32, dc_.shape, 1)
        df = jnp.where(lanes < NH1, dlogf * (1.0 - _sigmoid(f_ref[...] + fb_ref[...])), 0.0)
        dz_ref[:, GATE0:GATE0 + HD] = df.astype(BF16)
        dfb_ref[...] += jnp.sum(df, axis=0, keepdims=True)

    rv = lambda i: nb - 1 - i
    row = lambda c: pl.BlockSpec((bm, D), lambda i: (rv(i), c))
    lane = lambda c: pl.BlockSpec((bm, HD), lambda i: (rv(i), c))
    vec = pl.BlockSpec((1, HD), lambda i: (0, 0))
    return _pc(body, name, (nb,), [row(0), row(0), row(0), row(0), row(0), row(1), lane(4 * D // HD), vec, vec, vec, lane(0)],
               [pl.BlockSpec((bm, ZW), lambda i: (rv(i), 0)), vec, vec, vec],
               [_sds((t, ZW), BF16), _sds((1, HD)), _sds((1, HD)), _sds((1, HD))],
               scratch=[pltpu.VMEM((1, HD), F32)])(dqn, dkn, dv, dgate, z1, z1, z1, qg, kg, fbp, dc)


LOG2E = 1.4426950408889634
FOX_QSCALE = HD ** -0.5 * LOG2E


def _fox_steps(t, bq, bk, k_major):
    nq, nk = t // bq, t // bk
    pairs = [(i, j) for i in range(nq) for j in range(nk) if j * bk < (i + 1) * bq]
    if k_major:
        pairs.sort(key=lambda p: (p[1], p[0]))
    outer = [p[1] if k_major else p[0] for p in pairs]
    n = len(pairs)
    flags = [(n_ == 0 or outer[n_] != outer[n_ - 1]) + 2 * (n_ == n - 1 or outer[n_] != outer[n_ + 1])
             + 4 * (not (j + 1) * bk <= i * bq + 1) for n_, (i, j) in enumerate(pairs)]
    as_i32 = lambda v: jnp.asarray(v, jnp.int32)
    return as_i32([p[0] for p in pairs]), as_i32([p[1] for p in pairs]), as_i32(flags)


def _fox_step_info(qi_ref, kj_ref, fl_ref):
    s = pl.program_id(1)
    fl = fl_ref[s]
    return qi_ref[s], kj_ref[s], (fl & 1) != 0, (fl & 2) != 0, (fl & 4) != 0


def _fox_call(body, name, tables, in_specs, out_specs, out_shape, scratch):
    grid_spec = pltpu.PrefetchScalarGridSpec(num_scalar_prefetch=3, grid=(NH1 // FOX_HEADS_PER_STEP, tables[0].shape[0]),
                                             in_specs=in_specs, out_specs=out_specs, scratch_shapes=scratch)
    return pl.pallas_call(body, name=name, grid_spec=grid_spec, out_shape=out_shape,
                          compiler_params=pltpu.CompilerParams(dimension_semantics=("arbitrary", "arbitrary"),
                                                               vmem_limit_bytes=VMEM_LIMIT_V7X))


def _fox_lane_tiles(x):
    return [x[:, c0:c0 + HD] for c0 in range(0, x.shape[1], HD)]


def _fox_masked_scores(q, k, ck, i, j, bq, bk, masked):
    s = _nt(q, k) - ck
    if masked:
        rows = i * bq + lax.broadcasted_iota(jnp.int32, s.shape, 0)
        cols = j * bk + lax.broadcasted_iota(jnp.int32, s.shape, 1)
        s = jnp.where(cols <= rows, s, NEG)
    return s


def _fox_specs(bq, bk, G):
    qspec = pl.BlockSpec((bq, G * HD), lambda h, s, qi, kj, fl: (qi[s], h))
    kspec = pl.BlockSpec((bk, G * HD), lambda h, s, qi, kj, fl: (kj[s], h))
    cspec = pl.BlockSpec((G, 1, bk), lambda h, s, qi, kj, fl: (h, 0, kj[s]))
    colspec = pl.BlockSpec((G, bq, 1), lambda h, s, qi, kj, fl: (h, qi[s], 0))
    return qspec, kspec, cspec, colspec


def _fox_rowmax(qn, kn, crow, name):
    t = qn.shape[0]
    bq, bk, G = min(FOX_BQ, t), min(2 * FOX_BK, t), FOX_HEADS_PER_STEP
    tables = _fox_steps(t, bq, bk, k_major=False)

    def body(qi_ref, kj_ref, fl_ref, q_ref, k_ref, ck_ref, m_ref, *mp):
        i, j, first, last, diag = _fox_step_info(qi_ref, kj_ref, fl_ref)

        @pl.when(first)
        def _():
            for g in range(G):
                mp[g][...] = jnp.full_like(mp[g], NEG)

        def step(masked):
            for g in range(G):
                sl = slice(g * HD, (g + 1) * HD)
                s = _fox_masked_scores(q_ref[:, sl], k_ref[:, sl], ck_ref[g], i, j, bq, bk, masked)
                m = mp[g][...]
                for tile in _fox_lane_tiles(s):
                    m = jnp.maximum(m, tile)
                mp[g][...] = m

        pl.when(jnp.logical_not(diag))(lambda: step(False))
        pl.when(diag)(lambda: step(True))

        @pl.when(last)
        def _():
            for g in range(G):
                m_ref[g] = jnp.max(mp[g][...], axis=1, keepdims=True)

    qspec, kspec, cspec, colspec = _fox_specs(bq, bk, G)
    return _fox_call(body, name, tables, [qspec, kspec, cspec], colspec, _sds((NH1, t, 1)),
                     [pltpu.VMEM((bq, HD), F32)] * G)(*tables, qn, kn, crow)


def _fox_fwd(qn, kn, vb, crow, m, name):
    t = qn.shape[0]
    bq, bk, G = min(FOX_BQ, t), min(FOX_BK, t), FOX_HEADS_PER_STEP
    tables = _fox_steps(t, bq, bk, k_major=False)

    def body(qi_ref, kj_ref, fl_ref, q_ref, k_ref, v_ref, ck_ref, m_ref, o_ref, lse_ref, *scr):
        i, j, first, last, diag = _fox_step_info(qi_ref, kj_ref, fl_ref)
        lp, acc = scr[:G], scr[G:]

        @pl.when(first)
        def _():
            for g in range(G):
                lp[g][...] = jnp.zeros_like(lp[g])
                acc[g][...] = jnp.zeros_like(acc[g])

        def step(masked):
            for g in range(G):
                sl = slice(g * HD, (g + 1) * HD)
                s = _fox_masked_scores(q_ref[:, sl], k_ref[:, sl], ck_ref[g], i, j, bq, bk, masked)
                p = jnp.exp2(s - m_ref[g])
                l = lp[g][...]
                for tile in _fox_lane_tiles(p):
                    l = l + tile
                lp[g][...] = l
                acc[g][...] += _nn(p.astype(BF16), v_ref[:, sl])

        pl.when(jnp.logical_not(diag))(lambda: step(False))
        pl.when(diag)(lambda: step(True))

        @pl.when(last)
        def _():
            for g in range(G):
                l = jnp.sum(lp[g][...], axis=1, keepdims=True)
                o_ref[:, g * HD:(g + 1) * HD] = acc[g][...] / l
                lse_ref[g] = m_ref[g] + jnp.log2(l)

    qspec, kspec, cspec, colspec = _fox_specs(bq, bk, G)
    return _fox_call(body, name, tables, [qspec, kspec, kspec, cspec, colspec], [qspec, colspec],
                     [_sds((t, D)), _sds((NH1, t, 1))], [pltpu.VMEM((bq, HD), F32)] * (2 * G))(*tables, qn, kn, vb, crow, m)


def _fox_bwd(qn, kn, vb, crow, lse, delta, do, name):
    t = qn.shape[0]
    bq, bk, G = min(FOX_BQ, t), min(FOX_BK, t), FOX_HEADS_PER_STEP
    tables = _fox_steps(t, bq, bk, k_major=True)

    def body(qi_ref, kj_ref, fl_ref, q_ref, k_ref, v_ref, ck_ref, lse_ref, dl_ref, do_ref, dq_ref, dk_ref, dv_ref, dc_ref, dcq_ref,
             dk_s, dv_s, dc_s):
        i, j, first, last, diag = _fox_step_info(qi_ref, kj_ref, fl_ref)

        @pl.when(first)
        def _():
            dk_s[...] = jnp.zeros_like(dk_s)
            dv_s[...] = jnp.zeros_like(dv_s)
            dc_s[...] = jnp.zeros_like(dc_s)

        @pl.when(pl.program_id(1) == 0)
        def _():
            dq_ref[...] = jnp.zeros_like(dq_ref)
            dcq_ref[...] = jnp.zeros_like(dcq_ref)

        def step(masked):
            rows = pl.ds(pl.multiple_of(i * bq, bq), bq)
            for g in range(G):
                sl = slice(g * HD, (g + 1) * HD)
                q, k = q_ref[:, sl], k_ref[:, sl]
                s = _fox_masked_scores(q, k, ck_ref[g], i, j, bq, bk, masked)
                p = jnp.exp2(s - lse_ref[g])
                dob = do_ref[:, sl]
                dv_s[:, sl] += _tn(p.astype(BF16), dob)
                ds = p * (_nt(dob, v_ref[:, sl]) - dl_ref[g])
                dsb = ds.astype(BF16)
                dq_ref[rows, sl] += _nn(dsb, k)
                dk_s[:, sl] += _tn(dsb, q)
                dc_s[g] -= jnp.sum(ds, axis=0, keepdims=True)
                part_sum = dcq_ref[g, rows, :]
                for tile in _fox_lane_tiles(ds):
                    part_sum = part_sum + tile
                dcq_ref[g, rows, :] = part_sum

        pl.when(jnp.logical_not(diag))(lambda: step(False))
        pl.when(diag)(lambda: step(True))

        @pl.when(last)
        def _():
            dk_ref[...] = dk_s[...] * (1.0 / LOG2E)
            dv_ref[...] = dv_s[...]
            dc_ref[...] = dc_s[...]

    qspec, kspec, cspec, colspec = _fox_specs(bq, bk, G)
    return _fox_call(
        body, name, tables, [qspec, kspec, kspec, cspec, colspec, colspec, qspec],
        [pl.BlockSpec((t, G * HD), lambda h, s, qi, kj, fl: (0, h)), kspec, kspec, cspec,
         pl.BlockSpec((G, t, HD), lambda h, s, qi, kj, fl: (h, 0, 0))],
        [_sds((t, D)), _sds((t, D)), _sds((t, D)), _sds((NH1, 1, t)), _sds((NH1, t, HD))],
        [pltpu.VMEM((bk, G * HD), F32), pltpu.VMEM((bk, G * HD), F32), pltpu.VMEM((G, 1, bk), F32)],
    )(*tables, qn, kn, vb, crow, lse, delta, do)


def _post1_fwd(o, z1, w, h3, name, bm=512):
    t = o.shape[0]
    bm = min(bm, t)

    def body(o_ref, g_ref, w_ref, h_ref, out_ref, og_ref):
        og_ref[...] = (o_ref[...] * _sigmoid(g_ref[...])).astype(BF16)
        out_ref[...] = h_ref[...] + _nn(og_ref[...], w_ref[...])

    row = lambda c: pl.BlockSpec((bm, D), lambda i: (i, c))
    return _pc(body, name, (t // bm,), [row(0), row(3), pl.BlockSpec((D, D), lambda i: (0, 0)), row(0)],
               [row(0), row(0)], [_sds((t, D)), _sds((t, D), BF16)])(o, z1, w, h3)


def _post1_bwd(dh4, w, o, z1, name, bm=512):
    t = o.shape[0]
    bm = min(bm, t)

    def body(dh_ref, w_ref, o_ref, g_ref, do_ref, dg_ref, dl_ref):
        d_og = _nt(dh_ref[...].astype(BF16), w_ref[...])
        o_, sg = o_ref[...], _sigmoid(g_ref[...])
        dob = (d_og * sg).astype(BF16)
        do_ref[...] = dob
        dg_ref[...] = (d_og * o_ * sg * (1.0 - sg)).astype(BF16)
        prod = dob.astype(F32) * o_
        for hd in range(NH1):
            dl_ref[hd] = jnp.sum(prod[:, hd * HD:(hd + 1) * HD], axis=1, keepdims=True)

    row = lambda c: pl.BlockSpec((bm, D), lambda i: (i, c))
    return _pc(body, name, (t // bm,), [row(0), pl.BlockSpec((D, D), lambda i: (0, 0)), row(0), row(3)],
               [row(0), row(0), pl.BlockSpec((NH1, bm, 1), lambda i: (0, i, 0))],
               [_sds((t, D), BF16), _sds((t, D), BF16), _sds((NH1, t, 1))])(dh4, w, o, z1)


def _final(h, g, tgt, name, bm=512):
    t = h.shape[0]
    bm = min(bm, t)

    def body(h_ref, g_ref, t_ref, l_ref, dh_ref, dg_ref):
        @pl.when(pl.program_id(0) == 0)
        def _():
            l_ref[...] = jnp.zeros_like(l_ref)
            dg_ref[...] = jnp.zeros_like(dg_ref)

        x, gv = h_ref[...], g_ref[...]
        r = _rstd(x)
        xh = x * r
        e = xh * gv - t_ref[...]
        l_ref[...] += 0.5 * jnp.sum(jnp.mean(e * e, axis=1, keepdims=True), axis=0, keepdims=True)
        dy = e * (1.0 / D)
        dg_ref[...] += jnp.sum(dy * xh, axis=0, keepdims=True)
        dxh = dy * gv
        dh_ref[...] = r * (dxh - xh * jnp.mean(dxh * xh, axis=1, keepdims=True))

    row = pl.BlockSpec((bm, D), lambda i: (i, 0))
    vec = pl.BlockSpec((1, D), lambda i: (0, 0))
    return _pc(body, name, (t // bm,), [row, vec, row], [pl.BlockSpec((1, HD), lambda i: (0, 0)), row, vec],
               [_sds((1, HD)), _sds((t, D)), _sds((1, D))])(h, g, tgt)


def _adam(w, g, m, v, name):
    r, c = w.shape
    br = min(r, 256)

    def body(w_ref, g_ref, m_ref, v_ref, d_ref, mo_ref, vo_ref):
        gv = g_ref[...]
        mn = ADAM_B1 * m_ref[...] + (1.0 - ADAM_B1) * gv
        vn = ADAM_B2 * v_ref[...] + (1.0 - ADAM_B2) * jnp.square(gv)
        m_hat = mn / (1.0 - ADAM_B1 ** ADAM_STEP)
        v_hat = vn / (1.0 - ADAM_B2 ** ADAM_STEP)
        d_ref[...] = -ADAM_LR * (m_hat / (jnp.sqrt(v_hat) + ADAM_EPS) + ADAM_WD * w_ref[...])
        mo_ref[...] = mn
        vo_ref[...] = vn

    blk = pl.BlockSpec((br, c), lambda i: (i, 0))
    return _pc(body, name, (r // br,), [blk] * 4, [blk] * 3, [_sds((r, c))] * 3)(w, g, m, v)


ZW = 4224
GATE0 = 4096


def _pack_w_in0(w):
    return jnp.concatenate([w[:, :2048], w[:, 2056:], w[:, 2048:2056], jnp.zeros((w.shape[0], ZW - 4104), w.dtype)], axis=1)


def _unpack_w_in0(g):
    return jnp.concatenate([g[:, :2048], g[:, GATE0:GATE0 + 8], g[:, 2048:GATE0]], axis=1)


def _pack_w_in1(w):
    return jnp.concatenate([w, jnp.zeros((w.shape[0], ZW - 4104), w.dtype)], axis=1)


def _unpack_w_in1(g):
    return g[:, :4104]


def _local_step(x, mem, tgt, W, S, late_weights=None, grads_hook=None):
    t = x.shape[0]
    row = lambda v: v.reshape(1, -1)
    G = {}

    z0, u0 = _norm_mm(x, S["norm_mix_g"][0:1], W["w_in0"], "in0_fwd")
    qk = _conv_fwd(z0, S["conv_w"], "conv_fwd")
    g8 = z0[:, GATE0:GATE0 + 8]
    gates3 = jnp.stack([g8[:, :4].T, g8[:, 4:].T], axis=-1)
    gb = S["gate_b"]
    bias3 = jnp.stack([gb[0, :4], gb[0, 4:]], axis=-1)[:, None, :]
    hm, cs, ns, ms = _mlstm_fwd(qk, z0, gates3, bias3, "mlstm_fwd")
    hh, ss = _hgrn_fwd(z0, S["lb_logits"], "hgrn_fwd")
    if late_weights is not None:
        W = {**W, **late_weights(hh)}
    kv, mn = _memkv_fwd(mem, row(S["mem_norm_g"]), W["wkv_s"], "memkv_fwd")
    h1, y0 = _post0_fwd(hm, hh, z0, S["mlstm_norm_g"], S["hgrn_norm_g"], W["w_out0"], x, "post0_fwd")

    def xattn_mlp_fwd(h, l):
        q, ux = _norm_mm(h, S["norm_xattn_g"][l:l + 1], W["wq"][l], f"xq{l}_fwd")
        h2, ox = _xattn_fwd(q, kv, W["wo"][l], h, f"xattn{l}_fwd")
        h3, a, um = _mlp_fwd(h2, S["norm_mlp_g"][l:l + 1], W["w1s"], W["w2"], l, f"mlp{l}_fwd")
        return h3, (h, q, ux, ox, h2, a, um)

    h3, sv0 = xattn_mlp_fwd(h1, 0)
    z1, u1 = _norm_mm(h3, S["norm_mix_g"][1:2], W["w_in1"], "in1_fwd")
    fbp = jnp.pad(S["c_fgate_b"], ((0, 0), (0, HD - NH1)))
    qn, kn, vb, c = _foxprep_fwd(z1, S["c_qnorm_g"], S["c_knorm_g"], fbp, "foxprep_fwd")
    crow = (c[:, :NH1] * LOG2E).T[:, None, :]
    o1, lse = _fox_fwd(qn, kn, vb, crow, _fox_rowmax(qn, kn, crow, "fox_rowmax"), "fox_fwd")
    h4, og = _post1_fwd(o1, z1, W["w_out1"], h3, "post1_fwd")
    h6, sv1 = xattn_mlp_fwd(h4, 1)
    lossp, dh, G["final_norm_g"] = _final(h6, row(S["final_norm_g"]), tgt, "final")

    grads_ready = grads_hook if grads_hook is not None else (lambda stage, grads: 0.0)
    dkv = None
    dgx, dgm, dwq, dwo, dw1, dw2 = [None, None], [None, None], [None, None], [None, None], [None, None], [None, None]

    def xattn_mlp_bwd(dh, l, sv):
        nonlocal dkv
        h, q, ux, ox, h2, a, um = sv
        dh2, da, r, dgm[l] = _mlp_bwd(dh, a, W["w1s"], W["w2"], l, h2, S["norm_mlp_g"][l:l + 1], f"mlp{l}_bwd")
        dw1[l] = _mm_tn(um, da, f"mlp{l}_dw1", col_chips=NCHIP)
        dw2[l] = _mm_tn(r, dh, f"mlp{l}_dw2")
        dq, dkv_l = _xattn_bwd(dh2, q, kv, W["wo"][l], f"xattn{l}_bwd")
        dkv = dkv_l if dkv is None else dkv + dkv_l
        dwo[l] = _mm_tn(ox, dh2, f"xattn{l}_dwo")
        dwq[l] = _mm_tn(ux, dq, f"xattn{l}_dwq")
        dh1, dgx[l] = _bwd_in(dq, W["wq"][l], h, S["norm_xattn_g"][l:l + 1], dh2, f"xq{l}_bwd")
        return dh1

    dh4 = xattn_mlp_bwd(dh, 1, sv1)
    do, dgate, delta = _post1_bwd(dh4, W["w_out1"], o1, z1, "post1_bwd")
    G["w_out1"] = _mm_tn(og, dh4, "post1_dw")
    dqn, dkn, dv1, dcrow, dcq = _fox_bwd(qn, kn, vb, crow, lse, delta, do, "fox_bwd")
    dc = jnp.pad((dcrow[:, 0, :] + jnp.sum(dcq, axis=-1)).T, ((0, 0), (0, HD - NH1)))
    dz1, G["c_qnorm_g"], G["c_knorm_g"], dfb = _foxprep_bwd(
        dqn, dkn, dv1, dgate, z1, S["c_qnorm_g"], S["c_knorm_g"], fbp, dc, "foxprep_bwd")
    G["c_fgate_b"] = dfb[:, :NH1]
    G["w_in1"] = _mm_tn(u1, dz1, "in1_dw")
    tok = grads_ready("layer1", dict(w_out=G["w_out1"], w_in=G["w_in1"], wq=dwq[1], wo=dwo[1], w1=dw1[1], w2=dw2[1]))
    dh3, dgmix1 = _bwd_in(dz1, W["w_in1"], h3, S["norm_mix_g"][1:2] + tok, dh4, "in1_bwd")
    dh1 = xattn_mlp_bwd(dh3, 0, sv0)

    G["wkv"] = _mm_tn(mn, dkv, "memkv_dw", col_chips=NCHIP)
    G["mem_norm_g"] = _memkv_bwd(dkv, W["wkv_s"], mem, row(S["mem_norm_g"]), "memkv_bwd")
    G["w_out0"] = _mm_tn(y0, dh1, "post0_dw")
    tok = grads_ready("layer0", dict(wq=dwq[0], wo=dwo[0], w1=dw1[0], w2=dw2[0], wkv=G["wkv"], w_out=G["w_out0"]))
    dhm, dhh, doa, dgb, G["mlstm_norm_g"], G["hgrn_norm_g"] = _post0_bwd(
        dh1, W["w_out0"], hm, hh, z0, S["mlstm_norm_g"] + tok, S["hgrn_norm_g"], "post0_bwd")
    dqka, dva, dgates3 = _mlstm_bwd(qk, z0, gates3, bias3, cs, ns, ms, dhm, "mlstm_bwd")
    dqb, dfb0, dib, G["lb_logits"] = _hgrn_bwd(z0, S["lb_logits"], ss, dhh, "hgrn_bwd")
    duc, G["conv_w"] = _conv_bwd(z0, S["conv_w"], dqka, "conv_bwd")
    dg8 = jnp.concatenate([dgates3[:, :, 0].T, dgates3[:, :, 1].T], axis=1)
    G["gate_b"] = jnp.sum(dg8, axis=0, keepdims=True)
    dz0 = jnp.concatenate([duc, dva.astype(BF16), doa, dqb, dfb0, dib, dgb,
                           jnp.pad(dg8, ((0, 0), (0, HD - 8))).astype(BF16)], axis=1)
    G["w_in0"] = _mm_tn(u0, dz0, "in0_dw")
    tok = grads_ready("in0", dict(w_in=G["w_in0"]))
    dx, dgmix0 = _bwd_in(dz0, W["w_in0"], x, S["norm_mix_g"][0:1] + tok, dh1, "in0_bwd")

    G["norm_mix_g"] = jnp.concatenate([dgmix0, dgmix1], axis=0)
    G["norm_xattn_g"] = jnp.concatenate(dgx, axis=0)
    G["norm_mlp_g"] = jnp.concatenate(dgm, axis=0)
    G["wq"], G["wo"], G["w1"], G["w2"] = dwq, dwo, dw1, dw2
    return lossp[0, 0], dx, G


ANY = pl.BlockSpec(memory_space=pl.ANY)
NCHIP = 4


def _place():
    x, y, c = lax.axis_index("x"), lax.axis_index("y"), lax.axis_index("c")
    return x, y, c, [(1 - x, y), (x, 1 - y), (1 - x, 1 - y)]


def _comm_call(body, name, ins, out_shapes, sems):
    return pl.pallas_call(body, name=name, in_specs=[ANY] * len(ins), out_specs=[ANY] * len(out_shapes),
                          out_shape=out_shapes, scratch_shapes=sems)(*ins)


def _gather_weights(arrs, name):
    n = len(arrs)

    def body(*refs):
        ins, outs = refs[:n], refs[n:2 * n]
        send_i, recv_i, send_d, recv_d = refs[2 * n:]
        x, y, c, chips = _place()
        me = 2 * x + y

        def half(a, cc):
            h = arrs[a].shape[0] // 2
            return pl.ds(pl.multiple_of(cc * h, h), h)

        def ici(a, k, src_chip, dst_dev):
            return pltpu.make_async_remote_copy(
                src_ref=ins[a].at[half(a, c)], dst_ref=outs[a].at[src_chip, half(a, c)], send_sem=send_i.at[a, k],
                recv_sem=recv_i.at[a, k], device_id=dst_dev, device_id_type=MESH)

        def d2d(a, k, src_chip, cc):
            reg = outs[a].at[src_chip, half(a, cc)]
            return pltpu.make_async_remote_copy(src_ref=reg, dst_ref=reg, send_sem=send_d.at[a, k], recv_sem=recv_d.at[a, k],
                                                device_id=(x, y, 1 - c), device_id_type=MESH)

        for a in range(n):
            for k, (px, py) in enumerate(chips):
                ici(a, k, me, (px, py, c)).start()
        for k, (px, py) in enumerate(chips):
            for a in range(n):
                ici(a, k, 2 * px + py, (px, py, c)).wait_recv()
                d2d(a, k, 2 * px + py, c).start()
        for k, (px, py) in enumerate(chips):
            for a in range(n):
                ici(a, k, me, (px, py, c)).wait_send()
                d2d(a, k, 2 * px + py, c).wait_send()
                d2d(a, k, 2 * px + py, 1 - c).wait_recv()

    sem = lambda: pltpu.SemaphoreType.DMA((n, 3))
    return _comm_call(body, name, arrs, [_sds((NCHIP,) + a.shape, a.dtype) for a in arrs], [sem(), sem(), sem(), sem()])


HBM = pl.BlockSpec(memory_space=pltpu.HBM)
SEM = pl.BlockSpec(memory_space=pltpu.SEMAPHORE)
DATAFLOW = pltpu.SideEffectType.DATAFLOW_SIDE_EFFECTING


def _half_rows(r, cc):
    return pl.ds(pl.multiple_of(cc * (r // 2), r // 2), r // 2)


def _gather_start(arrs, after, name):
    n = len(arrs)

    def body(*refs):
        ins, lands = refs[:n], refs[n:2 * n]
        send, recv, token = refs[2 * n + 1], refs[2 * n + 2], refs[-1]
        x, y, c, chips = _place()
        me = 2 * x + y
        for a in range(n):
            rows = _half_rows(arrs[a].shape[0], c)
            for k, (px, py) in enumerate(chips):
                pltpu.make_async_remote_copy(src_ref=ins[a].at[rows], dst_ref=lands[a].at[me, rows], send_sem=send.at[3 * a + k],
                                             recv_sem=recv.at[3 * a + k], device_id=(px, py, c), device_id_type=MESH).start()
        token[...] = jnp.zeros_like(token)

    hbm = lambda v: pltpu.with_memory_space_constraint(v, pltpu.HBM)
    land_shapes = [((NCHIP,) + a.shape, a.dtype) for a in arrs]
    out = pl.pallas_call(
        body, name=name,
        out_shape=(pltpu.SemaphoreType.DMA((3 * n,)), pltpu.SemaphoreType.DMA((3 * n,)), *[pltpu.HBM(a.shape, a.dtype) for a in arrs],
                   *[pltpu.HBM(s, d) for s, d in land_shapes], _sds((8, HD))),
        in_specs=[HBM] * (2 * n) + [ANY], out_specs=(SEM, SEM, *[HBM] * (2 * n), pl.BlockSpec(memory_space=pltpu.VMEM)),
        input_output_aliases={i: 2 + i for i in range(2 * n)},
        compiler_params=pltpu.CompilerParams(has_side_effects=DATAFLOW),
    )(*[hbm(a) for a in arrs], *[hbm(lax.empty(s, d)) for s, d in land_shapes], after)
    return out[0], out[1], list(out[2:2 + n]), list(out[2 + n:2 + 2 * n]), out[-1]


def _gather_wait(send, recv, srcs, lands, after, name):
    n = len(srcs)

    def body(*refs):
        ins, lands_ = refs[:n], refs[n:2 * n]
        send_, recv_ = refs[2 * n], refs[2 * n + 1]
        x, y, c, chips = _place()
        for a in range(n):
            rows = _half_rows(srcs[a].shape[0], c)
            for k, (px, py) in enumerate(chips):
                cp = pltpu.make_async_remote_copy(src_ref=ins[a].at[rows], dst_ref=lands_[a].at[2 * px + py, rows], send_sem=send_.at[3 * a + k],
                                                  recv_sem=recv_.at[3 * a + k], device_id=(px, py, c), device_id_type=MESH)
                cp.wait_send()
                cp.wait_recv()

    out = pl.pallas_call(
        body, name=name, out_shape=[pltpu.HBM(v.shape, v.dtype) for v in list(srcs) + list(lands)],
        in_specs=[HBM] * (2 * n) + [SEM, SEM, ANY], out_specs=[HBM] * (2 * n), input_output_aliases={i: i for i in range(2 * n)},
        compiler_params=pltpu.CompilerParams(has_side_effects=DATAFLOW),
    )(*srcs, *lands, send, recv, after)
    return list(out[n:])


def _pair_forward(lands, name):
    n = len(lands)

    def body(*refs):
        ins, outs = refs[:n], refs[n:2 * n]
        send, recv = refs[2 * n:]
        x, y, c, chips = _place()
        copies = []
        for a in range(n):
            r = lands[a].shape[1]
            for k, (px, py) in enumerate(chips):
                cp = pltpu.make_async_remote_copy(
                    src_ref=ins[a].at[2 * px + py, _half_rows(r, c)], dst_ref=outs[a].at[2 * px + py, _half_rows(r, c)],
                    send_sem=send.at[a, k], recv_sem=recv.at[a, k], device_id=(x, y, 1 - c), device_id_type=MESH)
                cp.start()
                copies.append(cp)
        for a in range(n):
            r = lands[a].shape[1]
            for k, (px, py) in enumerate(chips):
                pltpu.make_async_remote_copy(
                    src_ref=ins[a].at[2 * px + py, _half_rows(r, c)], dst_ref=outs[a].at[2 * px + py, _half_rows(r, 1 - c)],
                    send_sem=send.at[a, k], recv_sem=recv.at[a, k], device_id=(x, y, 1 - c), device_id_type=MESH).wait_recv()
        for cp in copies:
            cp.wait_send()

    return pl.pallas_call(body, name=name, in_specs=[ANY] * n, out_specs=[ANY] * n, out_shape=[_sds(v.shape, v.dtype) for v in lands],
                          scratch_shapes=[pltpu.SemaphoreType.DMA((n, 3)), pltpu.SemaphoreType.DMA((n, 3))],
                          input_output_aliases={i: i for i in range(n)})(*lands)


def _pair_exchange(arrs, name):
    n = len(arrs)

    def body(*refs):
        ins, outs = refs[:n], refs[n:2 * n]
        send, recv = refs[2 * n:]
        x, y, c, _ = _place()
        copies = []
        for a in range(n):
            h = arrs[a].shape[1] // 2
            cp = pltpu.make_async_remote_copy(src_ref=ins[a].at[:, pl.ds(pl.multiple_of((1 - c) * h, h), h)], dst_ref=outs[a],
                                              send_sem=send.at[a], recv_sem=recv.at[a], device_id=(x, y, 1 - c), device_id_type=MESH)
            cp.start()
            copies.append(cp)
        for cp in copies:
            cp.wait()

    return _comm_call(body, name, arrs, [_sds((a.shape[0], a.shape[1] // 2, a.shape[2]), a.dtype) for a in arrs],
                      [pltpu.SemaphoreType.DMA((n,)), pltpu.SemaphoreType.DMA((n,))])


def _chip_exchange_start(arrs, name):
    n = len(arrs)

    def body(*refs):
        ins, lands = refs[:n], refs[n:2 * n]
        send, recv, token = refs[2 * n], refs[2 * n + 1], refs[-1]
        x, y, c, chips = _place()
        me = 2 * x + y
        for a in range(n):
            for k, (px, py) in enumerate(chips):
                pltpu.make_async_remote_copy(src_ref=ins[a].at[2 * px + py], dst_ref=lands[a].at[me], send_sem=send.at[3 * a + k],
                                             recv_sem=recv.at[3 * a + k], device_id=(px, py, c), device_id_type=MESH).start()
        token[...] = jnp.zeros_like(token)

    hbm = lambda v: pltpu.with_memory_space_constraint(v, pltpu.HBM)
    out = pl.pallas_call(
        body, name=name,
        out_shape=(pltpu.SemaphoreType.DMA((3 * n,)), pltpu.SemaphoreType.DMA((3 * n,)), *[pltpu.HBM(a.shape, a.dtype) for a in arrs],
                   *[pltpu.HBM(a.shape, a.dtype) for a in arrs], _sds((8, HD))),
        in_specs=[HBM] * (2 * n), out_specs=(SEM, SEM, *[HBM] * (2 * n), pl.BlockSpec(memory_space=pltpu.VMEM)),
        input_output_aliases={i: 2 + i for i in range(2 * n)},
        compiler_params=pltpu.CompilerParams(has_side_effects=DATAFLOW),
    )(*[hbm(a) for a in arrs], *[hbm(lax.empty(a.shape, a.dtype)) for a in arrs])
    return out[0], out[1], list(out[2:2 + n]), list(out[2 + n:2 + 2 * n]), out[-1]


def _chip_exchange_wait(send, recv, srcs, lands, after, name):
    n = len(srcs)

    def body(*refs):
        ins, lands_ = refs[:n], refs[n:2 * n]
        send_, recv_ = refs[2 * n], refs[2 * n + 1]
        x, y, c, chips = _place()
        for a in range(n):
            for k, (px, py) in enumerate(chips):
                cp = pltpu.make_async_remote_copy(src_ref=ins[a].at[2 * px + py], dst_ref=lands_[a].at[2 * px + py], send_sem=send_.at[3 * a + k],
                                                  recv_sem=recv_.at[3 * a + k], device_id=(px, py, c), device_id_type=MESH)
                cp.wait_send()
                cp.wait_recv()

    out = pl.pallas_call(
        body, name=name, out_shape=[pltpu.HBM(v.shape, v.dtype) for v in list(srcs) + list(lands)],
        in_specs=[HBM] * (2 * n) + [SEM, SEM, ANY], out_specs=[HBM] * (2 * n), input_output_aliases={i: i for i in range(2 * n)},
        compiler_params=pltpu.CompilerParams(has_side_effects=DATAFLOW),
    )(*srcs, *lands, send, recv, after)
    return list(out[n:])


def _pair_swap(arrs, name):
    n = len(arrs)

    def body(*refs):
        ins, outs = refs[:n], refs[n:2 * n]
        send, recv = refs[2 * n:]
        x, y, c, _ = _place()
        copies = []
        for a in range(n):
            cp = pltpu.make_async_remote_copy(src_ref=ins[a], dst_ref=outs[a], send_sem=send.at[a], recv_sem=recv.at[a],
                                              device_id=(x, y, 1 - c), device_id_type=MESH)
            cp.start()
            copies.append(cp)
        for cp in copies:
            cp.wait()

    return _comm_call(body, name, arrs, [_sds(a.shape, a.dtype) for a in arrs],
                      [pltpu.SemaphoreType.DMA((n,)), pltpu.SemaphoreType.DMA((n,))])


def _all_gather_devices(v, name):
    def body(v_ref, o_ref, send, recv, loc):
        x, y, c, _ = _place()
        me = 4 * x + 2 * y + c
        own = pltpu.make_async_copy(v_ref, o_ref.at[me], loc)
        own.start()
        copies = [own]
        for k in range(1, 8):
            fx, fy, fc = (k >> 2) & 1, (k >> 1) & 1, k & 1
            peer = (x ^ fx, y ^ fy, c ^ fc)
            r = pltpu.make_async_remote_copy(src_ref=v_ref, dst_ref=o_ref.at[me], send_sem=send.at[k - 1],
                                             recv_sem=recv.at[k - 1], device_id=peer, device_id_type=MESH)
            r.start()
            copies.append(r)
        for cp in copies:
            cp.wait()

    return _comm_call(body, name, [v], [_sds((8,) + v.shape, v.dtype)],
                      [pltpu.SemaphoreType.DMA((7,)), pltpu.SemaphoreType.DMA((7,)), pltpu.SemaphoreType.DMA])[0]


def _row_tile(r):
    return next((b for b in (512, 384, 256, 128, 64, 32, 16) if r % b == 0), r)


def _add2(a, b, out_dtype, name):
    r, w = a.shape
    br = _row_tile(r)

    def body(a_ref, b_ref, o_ref):
        o_ref[...] = (a_ref[...].astype(F32) + b_ref[...].astype(F32)).astype(out_dtype)

    blk = pl.BlockSpec((br, w), lambda i: (i, 0))
    return _pc(body, name, (r // br,), [blk, blk], blk, _sds((r, w), out_dtype))(a, b)


def _sum_slots(a, out_dtype, name):
    n, r, w = a.shape
    br = _row_tile(r)

    def body(a_ref, o_ref):
        acc = a_ref[0].astype(F32)
        for s in range(1, n):
            acc = acc + a_ref[s].astype(F32)
        o_ref[...] = acc.astype(out_dtype)

    return _pc(body, name, (r // br,), [pl.BlockSpec((n, br, w), lambda i: (0, i, 0))], pl.BlockSpec((br, w), lambda i: (i, 0)),
               _sds((r, w), out_dtype))(a)


SMALL = ["norm_mix_g", "norm_xattn_g", "norm_mlp_g", "final_norm_g", "mem_norm_g", "hgrn_lb_logits", "mlstm_norm_g",
         "hgrn_norm_g", "c_qnorm_g", "c_knorm_g", "ab_gate_b", "c_fgate_b"]
SMALL_ROWS = 16


def _pack_small(parts):
    flat = jnp.concatenate([p.reshape(-1).astype(F32) for p in parts])
    return jnp.pad(flat, (0, SMALL_ROWS * D - flat.shape[0])).reshape(SMALL_ROWS, D)


def _unpack_small(buf, shapes):
    flat, out, off = buf.reshape(-1), [], 0
    for s in shapes:
        n = 1
        for d in s:
            n *= d
        out.append(flat[off:off + n].reshape(s))
        off += n
    return out


def kernel(x, mem, norm_mix_g, norm_xattn_g, norm_mlp_g, final_norm_g, ab_w_in, ab_conv_w, ab_gate_b, hgrn_lb_logits, mlstm_norm_g, hgrn_norm_g, ab_w_out, c_w_in, c_fgate_b, c_qnorm_g, c_knorm_g, c_w_out, mem_norm_g, mem_w_kv, xa_w_q, xa_w_o, mlp_w1, mlp_w2, loss_target, m_norm_mix_g, m_norm_xattn_g, m_norm_mlp_g, m_final_norm_g, m_ab_w_in, m_ab_conv_w, m_ab_gate_b, m_hgrn_lb_logits, m_mlstm_norm_g, m_hgrn_norm_g, m_ab_w_out, m_c_w_in, m_c_fgate_b, m_c_qnorm_g, m_c_knorm_g, m_c_w_out, m_mem_norm_g, m_mem_w_kv, m_xa_w_q, m_xa_w_o, m_mlp_w1, m_mlp_w2, v_norm_mix_g, v_norm_xattn_g, v_norm_mlp_g, v_final_norm_g, v_ab_w_in, v_ab_conv_w, v_ab_gate_b, v_hgrn_lb_logits, v_mlstm_norm_g, v_hgrn_norm_g, v_ab_w_out, v_c_w_in, v_c_fgate_b, v_c_qnorm_g, v_c_knorm_g, v_c_w_out, v_mem_norm_g, v_mem_w_kv, v_xa_w_q, v_xa_w_o, v_mlp_w1, v_mlp_w2):
    A = dict(locals())
    chip = 2 * lax.axis_index("x") + lax.axis_index("y")

    big = ["ab_w_in", "c_w_in", "ab_w_out", "c_w_out", "mem_w_kv", "xa_w_q", "xa_w_o", "mlp_w1", "mlp_w2"]
    shard2d = {"ab_w_in": (D, 1026), "c_w_in": (D, 1026), "ab_w_out": (256, D), "c_w_out": (256, D), "mem_w_kv": (D, 512),
               "xa_w_q": (512, D), "xa_w_o": (512, D), "mlp_w1": (2 * D, D), "mlp_w2": (2 * D, D)}
    shard16 = lambda n: A[n].reshape(shard2d[n]).astype(BF16)
    own_slot = lambda gs, os: [lax.dynamic_update_index_in_dim(g, o, chip, 0) for g, o in zip(gs, os)]
    cols = lambda g: jnp.concatenate([g[k] for k in range(NCHIP)], axis=1)
    per_layer = lambda g: g.reshape(NCHIP, 2, -1, D).transpose(1, 0, 2, 3)
    first = [shard16("ab_w_in"), jnp.pad(ab_conv_w[0], ((0, 16 - CONV_W), (0, 0)))]
    g_in0, g_conv = own_slot(_gather_weights(first, "gather_first"), first)
    W = dict(w_in0=_pack_w_in0(cols(g_in0)))
    rest_names = ["c_w_in", "ab_w_out", "c_w_out", "xa_w_q", "xa_w_o", "mlp_w1", "mlp_w2", "mem_w_kv"]
    rest = [shard16(n) for n in rest_names]
    send_s, recv_s, srcs, lands, token = _gather_start(rest, g_conv, "gather_rest_start")

    def late_weights(after):
        got = _pair_forward(_gather_wait(send_s, recv_s, srcs, lands, after, "gather_rest_wait"), "gather_rest_forward")
        gw = dict(zip(rest_names, own_slot(got, rest)))
        return dict(w_in1=_pack_w_in1(cols(gw["c_w_in"])), w_out0=gw["ab_w_out"].reshape(D, D), w_out1=gw["c_w_out"].reshape(D, D),
                    wkv_s=gw["mem_w_kv"],
                    wq=per_layer(gw["xa_w_q"]).reshape(2, D, D), wo=per_layer(gw["xa_w_o"]).reshape(2, D, D),
                    w1s=gw["mlp_w1"].reshape(NCHIP, 2, D, D), w2=gw["mlp_w2"].reshape(NCHIP, 2, D, D))

    S = dict(norm_mix_g=norm_mix_g + token[0, 0], norm_xattn_g=norm_xattn_g, norm_mlp_g=norm_mlp_g, final_norm_g=final_norm_g,
             conv_w=cols(g_conv[:, :CONV_W]), gate_b=ab_gate_b, lb_logits=hgrn_lb_logits, mlstm_norm_g=mlstm_norm_g,
             hgrn_norm_g=hgrn_norm_g, c_fgate_b=c_fgate_b, c_qnorm_g=c_qnorm_g, c_knorm_g=c_knorm_g, mem_norm_g=mem_norm_g)

    core = lax.axis_index("c")
    by_rows = lambda g: g.reshape(NCHIP, -1, D)

    def stack_cols(g):
        return jnp.stack([g[:, 1026 * k:1026 * (k + 1)] for k in range(NCHIP)])

    def pair_sums(arrs, tag):
        theirs = _pair_exchange(arrs, f"pair_exchange_{tag}")
        out = []
        for i, (a, th) in enumerate(zip(arrs, theirs)):
            h = a.shape[1] // 2
            mine = lax.dynamic_slice_in_dim(a, core * h, h, axis=1)
            out.append(_add2(mine.reshape(-1, a.shape[2]), th.reshape(-1, a.shape[2]), BF16, f"pair_sum_{tag}{i}").reshape(th.shape))
        return out

    def chip_sums(psums, from_chips, tag):
        out = []
        for i, (f, p) in enumerate(zip(from_chips, psums)):
            f = lax.dynamic_update_index_in_dim(f, lax.dynamic_index_in_dim(p, chip, 0, keepdims=False), chip, 0)
            out.append(_sum_slots(f, F32, f"chip_sum_{tag}{i}"))
        return out

    started = {}

    def grads_hook(stage, g):
        if stage == "in0":
            arrs = [stack_cols(_unpack_w_in0(g["w_in"]))]
        else:
            arrs = [jnp.concatenate([by_rows(g["w_out"]), by_rows(g["wq"]), by_rows(g["wo"]), g["w1"], by_rows(g["w2"])], axis=1),
                    stack_cols(_unpack_w_in1(g["w_in"])) if stage == "layer1" else g["wkv"]]
        psums = pair_sums(arrs, stage)
        *handles, token = _chip_exchange_start(psums, f"chip_exchange_start_{stage}")
        started[stage] = (psums, handles)
        return token[0, 0]

    lossp, dx, G = _local_step(x[0], mem[0], loss_target[0], W, S, late_weights, grads_hook)

    gsmall = {"norm_mix_g": G["norm_mix_g"], "norm_xattn_g": G["norm_xattn_g"], "norm_mlp_g": G["norm_mlp_g"],
              "final_norm_g": G["final_norm_g"], "mem_norm_g": G["mem_norm_g"], "hgrn_lb_logits": G["lb_logits"],
              "mlstm_norm_g": G["mlstm_norm_g"], "hgrn_norm_g": G["hgrn_norm_g"], "c_qnorm_g": G["c_qnorm_g"],
              "c_knorm_g": G["c_knorm_g"], "ab_gate_b": G["gate_b"], "c_fgate_b": G["c_fgate_b"]}
    packed = _pack_small([gsmall[n] for n in SMALL] + [G["conv_w"], lossp])
    red = _sum_slots(_all_gather_devices(packed, "gather_small"), F32, "sum_small")
    small_shapes = [A[n].shape for n in SMALL]
    *gs, gconv, loss = _unpack_small(red, small_shapes + [(CONV_W, D), ()])
    gs = dict(zip(SMALL, gs))
    gconv = lax.dynamic_slice_in_dim(gconv, chip * 256, 256, axis=1)[None]

    rhalf = []
    for stage in ("layer1", "layer0", "in0"):
        psums, handles = started[stage]
        rhalf += chip_sums(psums, _chip_exchange_wait(*handles, dx, f"chip_exchange_wait_{stage}"), stage)
    other = _pair_swap(rhalf, "pair_swap")
    r_l1, r_in1, r_l0, r_kv, r_in0 = [
        jnp.where(core == 0, jnp.concatenate([m_, o_], axis=0), jnp.concatenate([o_, m_], axis=0)) for m_, o_ in zip(rhalf, other)]
    both = lambda lo, hi: jnp.concatenate([r_l0[lo:hi], r_l1[lo:hi]], axis=0)
    gbig = {"ab_w_in": r_in0, "c_w_in": r_in1, "mem_w_kv": r_kv, "ab_w_out": r_l0[0:256], "c_w_out": r_l1[0:256],
            "xa_w_q": both(256, 512), "xa_w_o": both(512, 768), "mlp_w1": both(768, 1792), "mlp_w2": both(1792, 2816)}

    out_g, out_d, out_m, out_v = {}, {}, {}, {}
    for n in big:
        d_, m_, v_ = _adam(A[n].reshape(shard2d[n]), gbig[n], A["m_" + n].reshape(shard2d[n]), A["v_" + n].reshape(shard2d[n]), "adam_" + n)
        out_g[n] = gbig[n].reshape(A[n].shape)
        out_d[n], out_m[n], out_v[n] = d_.reshape(A[n].shape), m_.reshape(A[n].shape), v_.reshape(A[n].shape)
    sd, sm, sv = _adam(_pack_small([A[n] for n in SMALL]), _pack_small([gs[n] for n in SMALL]),
                       _pack_small([A["m_" + n] for n in SMALL]), _pack_small([A["v_" + n] for n in SMALL]), "adam_small")
    for n, d_, m_, v_ in zip(SMALL, _unpack_small(sd, small_shapes), _unpack_small(sm, small_shapes), _unpack_small(sv, small_shapes)):
        out_g[n], out_d[n], out_m[n], out_v[n] = gs[n], d_, m_, v_
    cd, cm_, cv = _adam(ab_conv_w[0], gconv[0], m_ab_conv_w[0], v_ab_conv_w[0], "adam_conv")
    out_g["ab_conv_w"], out_d["ab_conv_w"], out_m["ab_conv_w"], out_v["ab_conv_w"] = gconv, cd[None], cm_[None], cv[None]

    order = ["norm_mix_g", "norm_xattn_g", "norm_mlp_g", "final_norm_g", "ab_w_in", "ab_conv_w", "ab_gate_b", "hgrn_lb_logits",
             "mlstm_norm_g", "hgrn_norm_g", "ab_w_out", "c_w_in", "c_fgate_b", "c_qnorm_g", "c_knorm_g", "c_w_out", "mem_norm_g",
             "mem_w_kv", "xa_w_q", "xa_w_o", "mlp_w1", "mlp_w2"]
    return (loss, dx[None], *[out_g[n] for n in order], *[out_d[n] for n in order], *[out_m[n] for n in order],
            *[out_v[n] for n in order])
```

```python
import jax
import jax.numpy as jnp
from jax import lax
from jax.experimental import pallas as pl
from jax.experimental.pallas import tpu as pltpu

F32 = jnp.float32
BF16 = jnp.bfloat16
EPS = 1e-6
D = 1024
CHUNK = 64
REC_CHUNKS = 4
HD = 128
XD = 256
NEG = -1e30
VMEM_LIMIT_V7X = 56 * 1024 * 1024
ADAM_LR, ADAM_B1, ADAM_B2, ADAM_EPS, ADAM_WD, ADAM_STEP = 0.001, 0.9, 0.999, 1e-08, 0.01, 10
MESH = pl.DeviceIdType.MESH


def _pc(body, name, grid, in_specs, out_specs, out_shape, scratch=(), **kw):
    return pl.pallas_call(
        body, name=name, grid=grid, in_specs=in_specs, out_specs=out_specs, out_shape=out_shape,
        scratch_shapes=scratch,
        compiler_params=pltpu.CompilerParams(
            dimension_semantics=("arbitrary",) * len(grid), vmem_limit_bytes=VMEM_LIMIT_V7X), **kw)


def _sds(shape, dtype=F32):
    return jax.ShapeDtypeStruct(shape, dtype)


def _blk(n, target):
    return max(b for b in range(128, max(target, 128) + 1, 128) if n % b == 0)


def _dot(a, b, dims):
    return lax.dot_general(a, b, (dims, ((), ())), preferred_element_type=F32)


def _nn(a, b):
    return _dot(a, b, ((1,), (0,)))


def _nt(a, b):
    return _dot(a, b, ((1,), (1,)))


def _tn(a, b):
    return _dot(a, b, ((0,), (0,)))


def _sigmoid(x):
    return 1.0 / (1.0 + jnp.exp(-x))


def _log_sigmoid(x):
    return jnp.minimum(x, 0.0) - jnp.log(1.0 + jnp.exp(-jnp.abs(x)))


def _rstd(x):
    return lax.rsqrt(jnp.mean(x * x, axis=-1, keepdims=True) + EPS)


def _rms_bwd(du, x, g):
    r = _rstd(x)
    xh = x * r
    dxh = du * g
    dx = r * (dxh - xh * jnp.mean(dxh * xh, axis=-1, keepdims=True))
    return dx, du * xh


def _norm_mm(h, g, w, name, bm=1024, bn=512):
    t, n = h.shape[0], w.shape[1]
    bm, bn = min(bm, t), _blk(n, 3 * bn)

    def body(h_ref, g_ref, w_ref, z_ref, u_ref):
        @pl.when(pl.program_id(1) == 0)
        def _():
            x = h_ref[...]
            u_ref[...] = (x * _rstd(x) * g_ref[...]).astype(BF16)
        z_ref[...] = _nn(u_ref[...], w_ref[...])

    return _pc(body, name, (t // bm, n // bn),
               [pl.BlockSpec((bm, D), lambda i, j: (i, 0)), pl.BlockSpec((1, D), lambda i, j: (0, 0)),
                pl.BlockSpec((D, bn), lambda i, j: (0, j))],
               [pl.BlockSpec((bm, bn), lambda i, j: (i, j)), pl.BlockSpec((bm, D), lambda i, j: (i, 0))],
               [_sds((t, n)), _sds((t, D), BF16)])(h, g, w)


def _mm_tn(a, b, name, bm=1024, bn=1024, bt=4096, col_chips=None):
    t, m = a.shape
    n = b.shape[1]
    bm, bn, bt = _blk(m, bm), (n // col_chips if col_chips else _blk(n, bn + bn // 2)), min(bt, t)
    if (m // bm) * (n // bn) == 1 and bt >= 1024:
        bt //= 4
    nt = t // bt

    def body(a_ref, b_ref, o_ref, acc):
        k = pl.program_id(2)

        @pl.when(k == 0)
        def _():
            acc[...] = jnp.zeros_like(acc)

        acc[...] += _tn(a_ref[...].astype(BF16), b_ref[...].astype(BF16))

        @pl.when(k == nt - 1)
        def _():
            o_ref[...] = acc[...].astype(BF16)

    if col_chips:
        out_spec, out_shape = pl.BlockSpec((None, bm, bn), lambda i, j, k: (j, i, 0)), _sds((col_chips, m, bn), BF16)
    else:
        out_spec, out_shape = pl.BlockSpec((bm, bn), lambda i, j, k: (i, j)), _sds((m, n), BF16)
    return _pc(body, name, (m // bm, n // bn, nt),
               [pl.BlockSpec((bt, bm), lambda i, j, k: (k, i)), pl.BlockSpec((bt, bn), lambda i, j, k: (k, j))],
               out_spec, out_shape, scratch=[pltpu.VMEM((bm, bn), F32)])(a, b)


def _bwd_in(dz, w, h, g, dh, name, bm=1024, bk=1024):
    t, n = dz.shape
    if n > 2 * bk:
        bm, bk = min(bm // 2, t), n
    else:
        bm, bk = min(bm, t), _blk(n, bk + bk // 2)
    nk = n // bk

    def body(dz_ref, w_ref, h_ref, g_ref, dh_ref, o_ref, dg_ref, acc):
        i, k = pl.program_id(0), pl.program_id(1)

        @pl.when(k == 0)
        def _():
            acc[...] = jnp.zeros_like(acc)

        @pl.when((i == 0) & (k == 0))
        def _():
            dg_ref[...] = jnp.zeros_like(dg_ref)

        acc[...] += _nt(dz_ref[...], w_ref[...])

        @pl.when(k == nk - 1)
        def _():
            dx, dgr = _rms_bwd(acc[...], h_ref[...], g_ref[...])
            o_ref[...] = dh_ref[...] + dx
            dg_ref[...] += jnp.sum(dgr, axis=0, keepdims=True)

    return _pc(body, name, (t // bm, nk),
               [pl.BlockSpec((bm, bk), lambda i, k: (i, k)), pl.BlockSpec((D, bk), lambda i, k: (0, k)),
                pl.BlockSpec((bm, D), lambda i, k: (i, 0)), pl.BlockSpec((1, D), lambda i, k: (0, 0)),
                pl.BlockSpec((bm, D), lambda i, k: (i, 0))],
               [pl.BlockSpec((bm, D), lambda i, k: (i, 0)), pl.BlockSpec((1, D), lambda i, k: (0, 0))],
               [_sds((t, D)), _sds((1, D))], scratch=[pltpu.VMEM((bm, D), F32)])(dz, w, h, g, dh)


def _mlp_fwd(h, g, w1s, w2, l, name, bm=1024):
    t = h.shape[0]
    bm = min(bm, t)
    nk = w1s.shape[0]

    def body(h_ref, g_ref, w1_ref, w2_ref, o_ref, a_ref, u_ref, acc):
        k = pl.program_id(1)

        @pl.when(k == 0)
        def _():
            x = h_ref[...]
            u_ref[...] = (x * _rstd(x) * g_ref[...]).astype(BF16)
            acc[...] = jnp.zeros_like(acc)

        a = _nn(u_ref[...], w1_ref[...])
        a_ref[...] = a
        r = jnp.square(jnp.maximum(a, 0.0)).astype(BF16)
        acc[...] += _nn(r, w2_ref[...])

        @pl.when(k == nk - 1)
        def _():
            o_ref[...] = h_ref[...] + acc[...]

    return _pc(body, name, (t // bm, nk),
               [pl.BlockSpec((bm, D), lambda i, k: (i, 0)), pl.BlockSpec((1, D), lambda i, k: (0, 0)),
                pl.BlockSpec((None, None, D, D), lambda i, k: (k, l, 0, 0)), pl.BlockSpec((None, None, D, D), lambda i, k: (k, l, 0, 0))],
               [pl.BlockSpec((bm, D), lambda i, k: (i, 0)), pl.BlockSpec((bm, D), lambda i, k: (i, k)),
                pl.BlockSpec((bm, D), lambda i, k: (i, 0))],
               [_sds((t, D)), _sds((t, nk * D)), _sds((t, D), BF16)],
               scratch=[pltpu.VMEM((bm, D), F32)])(h, g, w1s, w2)


def _mlp_bwd(dh, a, w1s, w2, l, h, g, name, bm=512):
    t = h.shape[0]
    bm = min(bm, t)
    nk = w1s.shape[0]

    def body(dh_ref, a_ref, w1_ref, w2_ref, h_ref, g_ref, o_ref, da_ref, r_ref, dg_ref, acc):
        i, k = pl.program_id(0), pl.program_id(1)

        @pl.when(k == 0)
        def _():
            acc[...] = jnp.zeros_like(acc)

        @pl.when((i == 0) & (k == 0))
        def _():
            dg_ref[...] = jnp.zeros_like(dg_ref)

        ap = jnp.maximum(a_ref[...], 0.0)
        r_ref[...] = jnp.square(ap).astype(BF16)
        dr = _nt(dh_ref[...].astype(BF16), w2_ref[...])
        da = (dr * (2.0 * ap)).astype(BF16)
        da_ref[...] = da
        acc[...] += _nt(da, w1_ref[...])

        @pl.when(k == nk - 1)
        def _():
            dx, dgr = _rms_bwd(acc[...], h_ref[...], g_ref[...])
            o_ref[...] = dh_ref[...] + dx
            dg_ref[...] += jnp.sum(dgr, axis=0, keepdims=True)

    return _pc(body, name, (t // bm, nk),
               [pl.BlockSpec((bm, D), lambda i, k: (i, 0)), pl.BlockSpec((bm, D), lambda i, k: (i, k)),
                pl.BlockSpec((None, None, D, D), lambda i, k: (k, l, 0, 0)), pl.BlockSpec((None, None, D, D), lambda i, k: (k, l, 0, 0)),
                pl.BlockSpec((bm, D), lambda i, k: (i, 0)), pl.BlockSpec((1, D), lambda i, k: (0, 0))],
               [pl.BlockSpec((bm, D), lambda i, k: (i, 0)), pl.BlockSpec((bm, D), lambda i, k: (i, k)),
                pl.BlockSpec((bm, D), lambda i, k: (i, k)), pl.BlockSpec((1, D), lambda i, k: (0, 0))],
               [_sds((t, D)), _sds((t, nk * D), BF16), _sds((t, nk * D), BF16), _sds((1, D))],
               scratch=[pltpu.VMEM((bm, D), F32)])(dh, a, w1s, w2, h, g)


def _rows_of(x):
    return lax.broadcasted_iota(jnp.int32, x.shape, 0)


def _shift_down(x, s):
    if s == 0:
        return x
    return jnp.where(_rows_of(x) >= s, pltpu.roll(x, s, 0), 0.0)


def _shift_up(x, s):
    if s == 0:
        return x
    n = x.shape[0]
    return jnp.where(_rows_of(x) < n - s, pltpu.roll(x, n - s, 0), 0.0)


def _cumsum_rows(x):
    n, s = x.shape[0], 1
    while s < n:
        x = x + _shift_down(x, s)
        s *= 2
    return x


def _rcumsum_rows(x):
    n, s = x.shape[0], 1
    while s < n:
        x = x + _shift_up(x, s)
        s *= 2
    return x


def _silu(x):
    return x * _sigmoid(x)


def _dsilu(x):
    s = _sigmoid(x)
    return s * (1.0 + x * (1.0 - s))


CONV_W = 4


def _conv_pre(u, w):
    y = _shift_down(u, CONV_W - 1) * w[0:1, :]
    for j in range(1, CONV_W):
        y = y + _shift_down(u, CONV_W - 1 - j) * w[j:j + 1, :]
    return y


def _conv_fwd(z0, cw, name):
    t = z0.shape[0]

    def body(u_ref, w_ref, o_ref):
        o_ref[...] = _silu(_conv_pre(u_ref[...], w_ref[...]))

    return _pc(body, name, (2 * 512 // HD,),
               [pl.BlockSpec((t, HD), lambda c: (0, c)), pl.BlockSpec((CONV_W, HD), lambda c: (0, c))],
               pl.BlockSpec((t, HD), lambda c: (0, c)), _sds((t, 1024)))(z0, cw)


def _conv_bwd(z0, cw, dy, name):
    t = z0.shape[0]

    def body(u_ref, w_ref, dy_ref, du_ref, dw_ref):
        u, w = u_ref[...], w_ref[...]
        dpre = dy_ref[...] * _dsilu(_conv_pre(u, w))
        du = _shift_up(dpre, CONV_W - 1) * w[0:1, :]
        for j in range(1, CONV_W):
            du = du + _shift_up(dpre, CONV_W - 1 - j) * w[j:j + 1, :]
        du_ref[...] = du.astype(BF16)
        for j in range(CONV_W):
            dw_ref[j:j + 1, :] = jnp.sum(dpre * _shift_down(u, CONV_W - 1 - j), axis=0, keepdims=True)

    return _pc(body, name, (2 * 512 // HD,),
               [pl.BlockSpec((t, HD), lambda c: (0, c)), pl.BlockSpec((CONV_W, HD), lambda c: (0, c)),
                pl.BlockSpec((t, HD), lambda c: (0, c))],
               [pl.BlockSpec((t, HD), lambda c: (0, c)), pl.BlockSpec((CONV_W, HD), lambda c: (0, c))],
               [_sds((t, 1024), BF16), _sds((CONV_W, 1024))])(z0, cw, dy)


def _mlstm_gates(gate, bias, m_in):
    L = gate.shape[0]
    r = lax.broadcasted_iota(jnp.int32, (L, L), 0)
    c = lax.broadcasted_iota(jnp.int32, (L, L), 1)
    eye, tril = r == c, c <= r
    i_col = gate[:, 0:1] + bias[:, 0:1]
    f_col = gate[:, 1:2] + bias[:, 1:2]
    logf_col = _log_sigmoid(f_col)
    logf_row = jnp.sum(jnp.where(eye, logf_col, 0.0), axis=0, keepdims=True)
    i_row = jnp.sum(jnp.where(eye, i_col, 0.0), axis=0, keepdims=True)
    b_col = jnp.sum(jnp.where(tril, logf_row, 0.0), axis=1, keepdims=True)
    b_row = jnp.sum(jnp.where(r <= c, logf_col, 0.0), axis=0, keepdims=True)
    logd = jnp.where(tril, b_col - b_row + i_row, NEG)
    inter = b_col + m_in
    m_t = jnp.maximum(inter, jnp.max(logd, axis=1, keepdims=True))
    w_t = jnp.exp(inter - m_t)
    dm = jnp.exp(logd - m_t)
    b_last = b_col[L - 1:L, :]
    log_in = b_last - b_col + i_col
    m_new = jnp.maximum(b_last + m_in, jnp.max(log_in, axis=0, keepdims=True))
    w_col = jnp.exp(log_in - m_new)
    decay = jnp.exp(b_last + m_in - m_new)
    return dict(eye=eye, r=r, c=c, f_col=f_col, m_t=m_t, w_t=w_t, dm=dm, m_new=m_new, w_col=w_col, decay=decay)


def _mlstm_fwd(qk, z0, gates, bias, name):
    t = qk.shape[0]
    nc, nh, L = t // CHUNK, 4, CHUNK
    scale = HD ** -0.5

    def body(q_ref, k_ref, v_ref, g_ref, b_ref, h_ref, cs_ref, ns_ref, ms_ref, c_s, n_s, m_s):
        @pl.when(pl.program_id(0) == 0)
        def _():
            c_s[...] = jnp.zeros_like(c_s)
            n_s[...] = jnp.zeros_like(n_s)
            m_s[...] = jnp.zeros_like(m_s)

        for hd in range(nh):
            sl = slice(hd * HD, (hd + 1) * HD)
            cm, nv, m_in = c_s[hd], n_s[hd], m_s[hd]
            for ck in range(cps):
                rows = slice(ck * L, (ck + 1) * L)
                cs_ref[hd, ck] = cm
                ns_ref[hd, ck] = nv
                ms_ref[hd, ck] = jnp.broadcast_to(m_in, (1, HD))
                q, kh, v = q_ref[rows, sl], k_ref[rows, sl] * scale, v_ref[rows, sl]
                G = _mlstm_gates(g_ref[hd, rows, :], b_ref[hd], m_in)
                qb, kb, vb = q.astype(BF16), kh.astype(BF16), v.astype(BF16)
                sc = _nt(qb, kb) * G["dm"]
                num = _nn(sc.astype(BF16), vb) + G["w_t"] * _nn(qb, cm.astype(BF16))
                den = jnp.sum(sc, axis=1, keepdims=True) + G["w_t"] * jnp.sum(q * nv, axis=1, keepdims=True)
                h_ref[rows, sl] = num / jnp.maximum(jnp.abs(den), jnp.exp(-G["m_t"]))
                wk = G["w_col"] * kh
                cm = G["decay"] * cm + _tn(wk.astype(BF16), vb)
                nv = G["decay"] * nv + jnp.sum(wk, axis=0, keepdims=True)
                m_in = G["m_new"]
            c_s[hd], n_s[hd], m_s[hd] = cm, nv, m_in

    cps = REC_CHUNKS
    hspec = lambda blk: pl.BlockSpec((cps * L, 512), lambda j: (j, blk))
    st = lambda r: pl.BlockSpec((nh, cps, r, HD), lambda j: (0, j, 0, 0))
    return _pc(body, name, (nc // cps,),
               [hspec(0), hspec(1), hspec(2), pl.BlockSpec((nh, cps * L, 2), lambda j: (0, j, 0)),
                pl.BlockSpec((nh, 1, 2), lambda j: (0, 0, 0))],
               [hspec(0), st(HD), st(1), st(1)],
               [_sds((t, 512)), _sds((nh, nc, HD, HD)), _sds((nh, nc, 1, HD)), _sds((nh, nc, 1, HD))],
               scratch=[pltpu.VMEM((nh, HD, HD), F32), pltpu.VMEM((nh, 1, HD), F32), pltpu.VMEM((nh, 1, 1), F32)])(qk, qk, z0, gates, bias)


def _mlstm_bwd(qk, z0, gates, bias, cs, ns, ms, dh, name):
    t = qk.shape[0]
    nc, nh, L = t // CHUNK, 4, CHUNK
    scale = HD ** -0.5

    def body(q_ref, k_ref, v_ref, g_ref, b_ref, cs_ref, ns_ref, ms_ref, dh_ref, dqk_ref, dv_ref, dg_ref, dc_s, dn_s):
        @pl.when(pl.program_id(0) == 0)
        def _():
            dc_s[...] = jnp.zeros_like(dc_s)
            dn_s[...] = jnp.zeros_like(dn_s)

        for ck in reversed(range(cps)):
            for hd in range(nh):
                one_head(hd, ck, slice(hd * HD, (hd + 1) * HD), slice(ck * L, (ck + 1) * L), q_ref, k_ref, v_ref, g_ref, b_ref,
                         cs_ref, ns_ref, ms_ref, dh_ref, dqk_ref, dv_ref, dg_ref, dc_s, dn_s)

    def one_head(hd, ck, sl, rows, q_ref, k_ref, v_ref, g_ref, b_ref, cs_ref, ns_ref, ms_ref, dh_ref, dqk_ref, dv_ref, dg_ref,
                 dc_s, dn_s):
        cm, nv, m_in = cs_ref[hd, ck], ns_ref[hd, ck], ms_ref[hd, ck][:, 0:1]
        q, kh, v = q_ref[rows, sl], k_ref[rows, sl] * scale, v_ref[rows, sl]
        G = _mlstm_gates(g_ref[hd, rows, :], b_ref[hd], m_in)
        w_t, dmat, w_col, decay = G["w_t"], G["dm"], G["w_col"], G["decay"]
        qb, kb, vb, cb = q.astype(BF16), kh.astype(BF16), v.astype(BF16), cm.astype(BF16)
        s = _nt(qb, kb)
        sc = s * dmat
        scb = sc.astype(BF16)
        qc = _nn(qb, cb)
        qn = jnp.sum(q * nv, axis=1, keepdims=True)
        num = _nn(scb, vb) + w_t * qc
        den = jnp.sum(sc, axis=1, keepdims=True) + w_t * qn
        e_m = jnp.exp(-G["m_t"])
        dnm = jnp.maximum(jnp.abs(den), e_m)
        dh_ = dh_ref[rows, sl]
        dnum = dh_ / dnm
        dden = jnp.where(jnp.abs(den) > e_m, -jnp.sum(dh_ * num, axis=1, keepdims=True) / (dnm * dnm) * jnp.sign(den), 0.0)
        dnumb = dnum.astype(BF16)
        dsc = _nt(dnumb, vb) + dden
        dv = _tn(scb, dnumb)
        wd = w_t * dnum
        wdb = wd.astype(BF16)
        ds = dsc * dmat
        dsb = ds.astype(BF16)
        dq = _nt(wdb, cb) + (w_t * dden) * nv + _nn(dsb, kb)
        dc_o = _tn(qb, wdb)
        dn_o = jnp.sum(q * (w_t * dden), axis=0, keepdims=True)
        dw = jnp.sum(dnum * qc, axis=1, keepdims=True) + dden * qn
        dkh = _tn(dsb, qb)
        dlogd = ds * s
        db_col = jnp.sum(dlogd, axis=1, keepdims=True) + dw * w_t
        csum = jnp.sum(dlogd, axis=0, keepdims=True)
        dcn, dnn = dc_s[hd], dn_s[hd]
        dcnb = dcn.astype(BF16)
        kdc = _nn(kb, dcnb)
        dws = jnp.sum(kdc * v, axis=1, keepdims=True) + jnp.sum(kh * dnn, axis=1, keepdims=True)
        dv = dv + w_col * kdc
        dkh = dkh + w_col * (_nt(vb, dcnb) + dnn)
        dlin = dws * w_col
        ddecay = jnp.sum(jnp.sum(dcn * cm, axis=1, keepdims=True), axis=0, keepdims=True) + jnp.sum(dnn * nv, axis=1, keepdims=True)
        dlast = ddecay * decay + jnp.sum(dlin, axis=0, keepdims=True)
        row_id = lax.broadcasted_iota(jnp.int32, (L, 1), 0)
        db_col = db_col - dlin + jnp.where(row_id == L - 1, dlast, 0.0)
        eye, r, c = G["eye"], G["r"], G["c"]
        di = dlin + jnp.sum(jnp.where(eye, csum, 0.0), axis=1, keepdims=True)
        db_row = jnp.sum(jnp.where(eye, db_col, 0.0), axis=0, keepdims=True) - csum
        dlogf = jnp.sum(jnp.where(c >= r, db_row, 0.0), axis=1, keepdims=True)
        dg_ref[hd, rows, 0:1] = di
        dg_ref[hd, rows, 1:2] = dlogf * (1.0 - _sigmoid(G["f_col"]))
        dqk_ref[rows, sl] = dq
        dqk_ref[rows, 512 + hd * HD:512 + (hd + 1) * HD] = dkh * scale
        dv_ref[rows, sl] = dv
        dc_s[hd] = decay * dcn + dc_o
        dn_s[hd] = decay * dnn + dn_o

    cps = REC_CHUNKS
    rv = lambda j: nc // cps - 1 - j
    hspec = lambda blk: pl.BlockSpec((cps * L, 512), lambda j: (rv(j), blk))
    st = lambda r: pl.BlockSpec((nh, cps, r, HD), lambda j: (0, rv(j), 0, 0))
    gs = pl.BlockSpec((nh, cps * L, 2), lambda j: (0, rv(j), 0))
    return _pc(body, name, (nc // cps,),
               [hspec(0), hspec(1), hspec(2), gs, pl.BlockSpec((nh, 1, 2), lambda j: (0, 0, 0)),
                st(HD), st(1), st(1), hspec(0)],
               [pl.BlockSpec((cps * L, 1024), lambda j: (rv(j), 0)), hspec(0), gs],
               [_sds((t, 1024)), _sds((t, 512)), _sds((nh, t, 2))],
               scratch=[pltpu.VMEM((nh, HD, HD), F32), pltpu.VMEM((nh, 1, HD), F32)])(qk, qk, z0, gates, bias, cs, ns, ms, dh)


def _hgrn_act(qb_, fb_, ib_, lg):
    lb = _sigmoid(lg[0:1, :] - lg[1:2, :])
    sg = _sigmoid(fb_)
    f = lb + (1.0 - lb) * sg
    return lb, sg, f, _silu(qb_), (1.0 - lb) * (1.0 - sg), _silu(ib_), _cumsum_rows(jnp.log(f))


HG_SUB = 16


def _hgrn_offdiag(q, k, b, r0):
    beta = b[r0 - 1:r0, :]
    e1 = jnp.exp(b[r0:r0 + HG_SUB, :] - beta)
    e2 = jnp.where(_rows_of(b) < r0, jnp.exp(jnp.minimum(beta - b, 0.0)), 0.0)
    return q[r0:r0 + HG_SUB, :] * e1, k * e2, e1, e2


def _hgrn_fwd(z0, lbl, name):
    t = z0.shape[0]
    nc, nh, L = t // CHUNK, 4, CHUNK

    def body(q_ref, f_ref, i_ref, l_ref, o_ref, ss_ref, st_s):
        @pl.when(pl.program_id(0) == 0)
        def _():
            st_s[...] = jnp.zeros_like(st_s)

        for hd in range(nh):
            sl = slice(hd * HD, (hd + 1) * HD)
            st = st_s[hd]
            for ck in range(cps):
                rows = slice(ck * L, (ck + 1) * L)
                ss_ref[hd, ck] = st
                _, _, _, q, k, v, b = _hgrn_act(q_ref[rows, sl], f_ref[rows, sl], i_ref[rows, sl], l_ref[:, sl])
                o = _nt((q * jnp.exp(b)).astype(BF16), st.astype(BF16))
                sub = _rows_of(b) & (HG_SUB - 1)
                o = o + jnp.sum(q * k, axis=1, keepdims=True) * v
                for dl in range(1, HG_SUB):
                    e = jnp.exp(jnp.where(sub >= dl, b - pltpu.roll(b, dl, 0), NEG))
                    a = jnp.sum(q * pltpu.roll(k, dl, 0) * e, axis=1, keepdims=True)
                    o = o + a * pltpu.roll(v, dl, 0)
                o_ref[rows, sl] = o
                vb = v.astype(BF16)
                for i in range(1, L // HG_SUB):
                    r0 = i * HG_SUB
                    qt, kt, _, _ = _hgrn_offdiag(q, k, b, r0)
                    a = _nt(qt.astype(BF16), kt.astype(BF16))
                    o_ref[ck * L + r0:ck * L + r0 + HG_SUB, sl] += _nn(a.astype(BF16), vb)
                bl = b[L - 1:L, :]
                st = st * jnp.exp(bl) + _tn(v.astype(BF16), (k * jnp.exp(bl - b)).astype(BF16))
            st_s[hd] = st

    cps = REC_CHUNKS
    hspec = lambda blk: pl.BlockSpec((cps * L, 512), lambda j: (j, blk))
    return _pc(body, name, (nc // cps,),
               [hspec(4), hspec(5), hspec(6), pl.BlockSpec((2, 512), lambda j: (0, 0))],
               [hspec(0), pl.BlockSpec((nh, cps, HD, HD), lambda j: (0, j, 0, 0))],
               [_sds((t, 512)), _sds((nh, nc, HD, HD))],
               scratch=[pltpu.VMEM((nh, HD, HD), F32)])(z0, z0, z0, lbl)


def _hgrn_bwd(z0, lbl, ss, do, name):
    t = z0.shape[0]
    nc, nh, L = t // CHUNK, 4, CHUNK

    def body(q_ref, f_ref, i_ref, l_ref, ss_ref, do_ref, dq_ref, df_ref, di_ref, dl_ref, dst_s, dlb_s, dq_a, dk_a, dv_a, db_a):
        @pl.when(pl.program_id(0) == 0)
        def _():
            dst_s[...] = jnp.zeros_like(dst_s)
            dlb_s[...] = jnp.zeros_like(dlb_s)

        for ck in reversed(range(cps)):
            for hd in range(nh):
                one_head(hd, ck, slice(hd * HD, (hd + 1) * HD), slice(ck * L, (ck + 1) * L), q_ref, f_ref, i_ref, l_ref, ss_ref, do_ref,
                         dq_ref, df_ref, di_ref, dl_ref, dst_s, dlb_s, dq_a.at[hd], dk_a.at[hd], dv_a.at[hd], db_a.at[hd])

    def one_head(hd, ck, sl, rs, q_ref, f_ref, i_ref, l_ref, ss_ref, do_ref, dq_ref, df_ref, di_ref, dl_ref, dst_s, dlb_s,
                 dq_a, dk_a, dv_a, db_a):
        st = ss_ref[hd, ck]
        qp, fp, ip = q_ref[rs, sl], f_ref[rs, sl], i_ref[rs, sl]
        lb, sg, f, q, k, v, b = _hgrn_act(qp, fp, ip, l_ref[:, sl])
        do_ = do_ref[rs, sl]
        dob, stb = do_.astype(BF16), st.astype(BF16)
        eb = jnp.exp(b)
        qe = q * eb
        dqe = _nn(dob, stb)
        dst_o = _tn(dob, qe.astype(BF16))
        dq = dqe * eb
        db = dqe * qe
        rows = _rows_of(b)
        sub = rows & (HG_SUB - 1)
        p0 = jnp.sum(do_ * v, axis=1, keepdims=True)
        dq = dq + p0 * k
        dk = p0 * q
        dv = jnp.sum(q * k, axis=1, keepdims=True) * do_
        for dl in range(1, HG_SUB):
            up = L - dl
            kd, vd = pltpu.roll(k, dl, 0), pltpu.roll(v, dl, 0)
            e = jnp.exp(jnp.where(sub >= dl, b - pltpu.roll(b, dl, 0), NEG))
            a = jnp.sum(q * kd * e, axis=1, keepdims=True)
            p = jnp.sum(do_ * vd, axis=1, keepdims=True) * e
            dq = dq + p * kd
            dkd = p * q
            dbb = dkd * kd
            dv = dv + pltpu.roll(a * do_, up, 0)
            dk = dk + pltpu.roll(dkd, up, 0)
            db = db + dbb - pltpu.roll(dbb, up, 0)
        dq_a[...], dk_a[...], dv_a[...], db_a[...] = dq, dk, dv, db
        vb = v.astype(BF16)
        for i in range(1, L // HG_SUB):
            r0 = i * HG_SUB
            blk = slice(r0, r0 + HG_SUB)
            qt, kt, e1, e2 = _hgrn_offdiag(q, k, b, r0)
            qtb, ktb, dob_i = qt.astype(BF16), kt.astype(BF16), do_[blk, :].astype(BF16)
            a = _nt(qtb, ktb).astype(BF16)
            da = _nt(dob_i, vb).astype(BF16)
            dv_a[...] += _tn(a, dob_i)
            dqt = _nn(da, ktb)
            dkt = _tn(da, qtb)
            dq_a[blk, :] += dqt * e1
            t1, t2 = dqt * qt, dkt * kt
            db_a[blk, :] += t1
            dk_a[...] += dkt * e2
            db_a[...] -= t2
            db_a[r0 - 1:r0, :] += jnp.sum(t2, axis=0, keepdims=True) - jnp.sum(t1, axis=0, keepdims=True)
        dq, dk, dv, db = dq_a[...], dk_a[...], dv_a[...], db_a[...]
        dstn = dst_s[hd]
        dstnb = dstn.astype(BF16)
        bl = b[L - 1:L, :]
        ebl = jnp.exp(bl)
        kdec_e = jnp.exp(bl - b)
        kdec = k * kdec_e
        dbl = jnp.sum(dstn * st, axis=0, keepdims=True) * ebl
        dv = dv + _nt(kdec.astype(BF16), dstnb)
        dkdec = _nn(v.astype(BF16), dstnb)
        dk = dk + dkdec * kdec_e
        dx = dkdec * kdec
        dbl = dbl + jnp.sum(dx, axis=0, keepdims=True)
        db = db - dx + jnp.where(rows == L - 1, dbl, 0.0)
        dst_s[hd] = dstn * ebl + dst_o
        dg = _rcumsum_rows(db)
        dfk = dg / f - dk
        dq_ref[rs, sl] = (dq * _dsilu(qp)).astype(BF16)
        di_ref[rs, sl] = (dv * _dsilu(ip)).astype(BF16)
        df_ref[rs, sl] = (dfk * (1.0 - lb) * sg * (1.0 - sg)).astype(BF16)
        dlb_s[hd] += jnp.sum(dfk * (1.0 - sg), axis=0, keepdims=True)

        if ck == 0:
            @pl.when(pl.program_id(0) == nc // cps - 1)
            def _():
                dl0 = dlb_s[hd] * lb * (1.0 - lb)
                dl_ref[0:1, sl] = dl0
                dl_ref[1:2, sl] = -dl0

    cps = REC_CHUNKS
    rv = lambda j: nc // cps - 1 - j
    hspec = lambda blk: pl.BlockSpec((cps * L, 512), lambda j: (rv(j), blk))
    return _pc(body, name, (nc // cps,),
               [hspec(4), hspec(5), hspec(6), pl.BlockSpec((2, 512), lambda j: (0, 0)),
                pl.BlockSpec((nh, cps, HD, HD), lambda j: (0, rv(j), 0, 0)), hspec(0)],
               [hspec(0), hspec(0), hspec(0), pl.BlockSpec((2, 512), lambda j: (0, 0))],
               [_sds((t, 512), BF16), _sds((t, 512), BF16), _sds((t, 512), BF16), _sds((2, 512))],
               scratch=[pltpu.VMEM((nh, HD, HD), F32), pltpu.VMEM((nh, 1, HD), F32)] + [pltpu.VMEM((nh, L, HD), F32)] * 4)(z0, z0, z0, lbl, ss, do)


def _post0_fwd(hm, hh, z0, na, nb, w, h0, name, bm=512):
    t = h0.shape[0]
    bm = min(bm, t)

    def body(hm_ref, hh_ref, oa_ref, gb_ref, na_ref, nb_ref, w_ref, h_ref, o_ref, y_ref):
        for hd in range(4):
            sl = slice(hd * HD, (hd + 1) * HD)
            pa = _sigmoid(oa_ref[:, sl]) * hm_ref[:, sl]
            y_ref[:, sl] = (pa * _rstd(pa) * na_ref[:, sl]).astype(BF16)
            xb = hh_ref[:, sl]
            y_ref[:, 512 + hd * HD:512 + (hd + 1) * HD] = (xb * _rstd(xb) * nb_ref[:, sl] * _silu(gb_ref[:, sl])).astype(BF16)
        o_ref[...] = h_ref[...] + _nn(y_ref[...], w_ref[...])

    row = lambda wd, c: pl.BlockSpec((bm, wd), lambda i: (i, c))
    vec = lambda wd: pl.BlockSpec((1, wd), lambda i: (0, 0))
    return _pc(body, name, (t // bm,),
               [row(512, 0), row(512, 0), row(512, 3), row(512, 7), vec(512), vec(512),
                pl.BlockSpec((D, D), lambda i: (0, 0)), row(D, 0)],
               [row(D, 0), row(D, 0)], [_sds((t, D)), _sds((t, D), BF16)])(hm, hh, z0, z0, na, nb, w, h0)


def _post0_bwd(dh1, w, hm, hh, z0, na, nb, name, bm=512):
    t = dh1.shape[0]
    bm = min(bm, t)

    def body(dh_ref, w_ref, hm_ref, hh_ref, oa_ref, gb_ref, na_ref, nb_ref, dhm_ref, dhh_ref, doa_ref, dgb_ref, dna_ref, dnb_ref):
        @pl.when(pl.program_id(0) == 0)
        def _():
            dna_ref[...] = jnp.zeros_like(dna_ref)
            dnb_ref[...] = jnp.zeros_like(dnb_ref)

        dy = _nt(dh_ref[...].astype(BF16), w_ref[...])
        for hd in range(4):
            sl = slice(hd * HD, (hd + 1) * HD)
            hm_, oa = hm_ref[:, sl], oa_ref[:, sl]
            sg = _sigmoid(oa)
            dpa, dgr = _rms_bwd(dy[:, sl], sg * hm_, na_ref[:, sl])
            dna_ref[:, sl] += jnp.sum(dgr, axis=0, keepdims=True)
            doa_ref[:, sl] = (dpa * hm_ * sg * (1.0 - sg)).astype(BF16)
            dhm_ref[:, sl] = dpa * sg
            xb, gb, nbv = hh_ref[:, sl], gb_ref[:, sl], nb_ref[:, sl]
            dyb = dy[:, 512 + hd * HD:512 + (hd + 1) * HD]
            dgb_ref[:, sl] = (dyb * (xb * _rstd(xb) * nbv) * _dsilu(gb)).astype(BF16)
            dxb, dgr2 = _rms_bwd(dyb * _silu(gb), xb, nbv)
            dnb_ref[:, sl] += jnp.sum(dgr2, axis=0, keepdims=True)
            dhh_ref[:, sl] = dxb

    row = lambda wd, c: pl.BlockSpec((bm, wd), lambda i: (i, c))
    vec = lambda wd: pl.BlockSpec((1, wd), lambda i: (0, 0))
    return _pc(body, name, (t // bm,),
               [row(D, 0), pl.BlockSpec((D, D), lambda i: (0, 0)), row(512, 0), row(512, 0), row(512, 3), row(512, 7),
                vec(512), vec(512)],
               [row(512, 0), row(512, 0), row(512, 0), row(512, 0), vec(512), vec(512)],
               [_sds((t, 512)), _sds((t, 512)), _sds((t, 512), BF16), _sds((t, 512), BF16), _sds((1, 512)), _sds((1, 512))],
               )(dh1, w, hm, hh, z0, z0, na, nb)


def _memkv_fwd(mem, g, wkv_s, name):
    m = mem.shape[0]

    def body(x_ref, g_ref, w_ref, kv_ref, mn_ref):
        x = x_ref[...]
        mn = (x * _rstd(x) * g_ref[...]).astype(BF16)
        mn_ref[...] = mn
        kv_ref[...] = _nn(mn, w_ref[...])

    return _pc(body, name, (4,),
               [pl.BlockSpec((m, D), lambda k: (0, 0)), pl.BlockSpec((1, D), lambda k: (0, 0)),
                pl.BlockSpec((None, D, 512), lambda k: (k, 0, 0))],
               [pl.BlockSpec((m, 512), lambda k: (0, k)), pl.BlockSpec((m, D), lambda k: (0, 0))],
               [_sds((m, 2048)), _sds((m, D), BF16)])(mem, g, wkv_s)


def _memkv_bwd(dkv, wkv_s, mem, g, name):
    m = mem.shape[0]

    def body(d_ref, w_ref, x_ref, g_ref, dg_ref, acc):
        k = pl.program_id(0)

        @pl.when(k == 0)
        def _():
            acc[...] = jnp.zeros_like(acc)

        acc[...] += _nt(d_ref[...].astype(BF16), w_ref[...])

        @pl.when(k == 3)
        def _():
            _, dgr = _rms_bwd(acc[...], x_ref[...], g_ref[...])
            dg_ref[...] = jnp.sum(dgr, axis=0, keepdims=True)

    return _pc(body, name, (4,),
               [pl.BlockSpec((m, 512), lambda k: (0, k)), pl.BlockSpec((None, D, 512), lambda k: (k, 0, 0)),
                pl.BlockSpec((m, D), lambda k: (0, 0)), pl.BlockSpec((1, D), lambda k: (0, 0))],
               pl.BlockSpec((1, D), lambda k: (0, 0)), _sds((1, D)), scratch=[pltpu.VMEM((m, D), F32)])(dkv, wkv_s, mem, g)


def _xattn_probs(qh, kh):
    s = _nt(qh, kh) * (XD ** -0.5)
    p = jnp.exp(s - jnp.max(s, axis=1, keepdims=True))
    return p / jnp.sum(p, axis=1, keepdims=True)


def _xattn_fwd(q, kv, wo, h1, name, bm=512):
    t, m = q.shape[0], kv.shape[0]
    bm = min(bm, t)

    def body(q_ref, k_ref, v_ref, w_ref, h_ref, out_ref, o_ref):
        for hd in range(D // XD):
            sl = slice(hd * XD, (hd + 1) * XD)
            p = _xattn_probs(q_ref[:, sl].astype(BF16), k_ref[:, sl].astype(BF16))
            o_ref[:, sl] = _nn(p.astype(BF16), v_ref[:, sl].astype(BF16)).astype(BF16)
        out_ref[...] = h_ref[...] + _nn(o_ref[...], w_ref[...])

    row = pl.BlockSpec((bm, D), lambda i: (i, 0))
    return _pc(body, name, (t // bm,),
               [row, pl.BlockSpec((m, D), lambda i: (0, 0)), pl.BlockSpec((m, D), lambda i: (0, 1)),
                pl.BlockSpec((D, D), lambda i: (0, 0)), row],
               [row, row], [_sds((t, D)), _sds((t, D), BF16)])(q, kv, kv, wo, h1)


def _xattn_bwd(dh2, q, kv, wo, name, bm=512):
    t, m = q.shape[0], kv.shape[0]
    bm = min(bm, t)

    def body(dh_ref, q_ref, k_ref, v_ref, w_ref, dq_ref, dkv_ref):
        @pl.when(pl.program_id(0) == 0)
        def _():
            dkv_ref[...] = jnp.zeros_like(dkv_ref)

        d_o = _nt(dh_ref[...].astype(BF16), w_ref[...])
        for hd in range(D // XD):
            sl = slice(hd * XD, (hd + 1) * XD)
            qh, kh, vh = q_ref[:, sl].astype(BF16), k_ref[:, sl].astype(BF16), v_ref[:, sl].astype(BF16)
            p = _xattn_probs(qh, kh)
            dob = d_o[:, sl].astype(BF16)
            dp = _nt(dob, vh)
            dkv_ref[:, D + hd * XD:D + (hd + 1) * XD] += _tn(p.astype(BF16), dob)
            ds = (p * (dp - jnp.sum(dp * p, axis=1, keepdims=True)) * (XD ** -0.5)).astype(BF16)
            dq_ref[:, sl] = _nn(ds, kh).astype(BF16)
            dkv_ref[:, sl] += _tn(ds, qh)

    row = pl.BlockSpec((bm, D), lambda i: (i, 0))
    return _pc(body, name, (t // bm,),
               [row, row, pl.BlockSpec((m, D), lambda i: (0, 0)), pl.BlockSpec((m, D), lambda i: (0, 1)),
                pl.BlockSpec((D, D), lambda i: (0, 0))],
               [row, pl.BlockSpec((m, 2 * D), lambda i: (0, 0))],
               [_sds((t, D), BF16), _sds((m, 2 * D))])(dh2, q, kv, kv, wo)


NH1 = 8
FOX_BM = 512
FOX_BQ = 512
FOX_BK = 512
FOX_HEADS_PER_STEP = 4


def _foxprep_fwd(z1, qg, kg, fbp, name):
    t = z1.shape[0]
    bm = min(FOX_BM, t)

    def body(q_ref, k_ref, v_ref, f_ref, qg_ref, kg_ref, fb_ref, qn_ref, kn_ref, vb_ref, c_ref, carry):
        @pl.when(pl.program_id(0) == 0)
        def _():
            carry[...] = jnp.zeros_like(carry)

        for hd in range(NH1):
            sl = slice(hd * HD, (hd + 1) * HD)
            x = q_ref[:, sl]
            qn_ref[:, sl] = (x * _rstd(x) * qg_ref[...] * FOX_QSCALE).astype(BF16)
            x = k_ref[:, sl]
            kn_ref[:, sl] = (x * _rstd(x) * kg_ref[...]).astype(BF16)
        vb_ref[...] = v_ref[...].astype(BF16)
        c = carry[...] + _cumsum_rows(_log_sigmoid(f_ref[...] + fb_ref[...]))
        c_ref[...] = c
        carry[...] = c[bm - 1:bm, :]

    row = lambda c: pl.BlockSpec((bm, D), lambda i: (i, c))
    lane = pl.BlockSpec((bm, HD), lambda i: (i, 4 * D // HD))
    vec = pl.BlockSpec((1, HD), lambda i: (0, 0))
    return _pc(body, name, (t // bm,), [row(0), row(1), row(2), lane, vec, vec, vec],
               [row(0), row(0), row(0), pl.BlockSpec((bm, HD), lambda i: (i, 0))],
               [_sds((t, D), BF16), _sds((t, D), BF16), _sds((t, D), BF16), _sds((t, HD))],
               scratch=[pltpu.VMEM((1, HD), F32)])(z1, z1, z1, z1, qg, kg, fbp)


def _foxprep_bwd(dqn, dkn, dv, dgate, z1, qg, kg, fbp, dc, name):
    t = z1.shape[0]
    bm = min(FOX_BM, t)
    nb = t // bm

    def body(dqn_ref, dkn_ref, dv_ref, dgt_ref, q_ref, k_ref, f_ref, qg_ref, kg_ref, fb_ref, dc_ref,
             dz_ref, dqg_ref, dkg_ref, dfb_ref, carry):
        @pl.when(pl.program_id(0) == 0)
        def _():
            carry[...] = jnp.zeros_like(carry)
            dqg_ref[...] = jnp.zeros_like(dqg_ref)
            dkg_ref[...] = jnp.zeros_like(dkg_ref)
            dfb_ref[...] = jnp.zeros_like(dfb_ref)

        for hd in range(NH1):
            sl = slice(hd * HD, (hd + 1) * HD)
            dx, dgr = _rms_bwd(dqn_ref[:, sl] * (HD ** -0.5), q_ref[:, sl], qg_ref[...])
            dz_ref[:, sl] = dx.astype(BF16)
            dqg_ref[...] += jnp.sum(dgr, axis=0, keepdims=True)
            dx, dgr = _rms_bwd(dkn_ref[:, sl], k_ref[:, sl], kg_ref[...])
            dz_ref[:, D + hd * HD:D + (hd + 1) * HD] = dx.astype(BF16)
            dkg_ref[...] += jnp.sum(dgr, axis=0, keepdims=True)
        dz_ref[:, 2 * D:3 * D] = dv_ref[...].astype(BF16)
        dz_ref[:, 3 * D:4 * D] = dgt_ref[...]
        dc_ = dc_ref[...]
        dlogf = _rcumsum_rows(dc_) + carry[...]
        carry[...] += jnp.sum(dc_, axis=0, keepdims=True)
        lanes = lax.broadcasted_iota(jnp.int32, dc_.shape, 1)
        df = jnp.where(lanes < NH1, dlogf * (1.0 - _sigmoid(f_ref[...] + fb_ref[...])), 0.0)
        dz_ref[:, GATE0:GATE0 + HD] = df.astype(BF16)
        dfb_ref[...] += jnp.sum(df, axis=0, keepdims=True)

    rv = lambda i: nb - 1 - i
    row = lambda c: pl.BlockSpec((bm, D), lambda i: (rv(i), c))
    lane = lambda c: pl.BlockSpec((bm, HD), lambda i: (rv(i), c))
    vec = pl.BlockSpec((1, HD), lambda i: (0, 0))
    return _pc(body, name, (nb,), [row(0), row(0), row(0), row(0), row(0), row(1), lane(4 * D // HD), vec, vec, vec, lane(0)],
               [pl.BlockSpec((bm, ZW), lambda i: (rv(i), 0)), vec, vec, vec],
               [_sds((t, ZW), BF16), _sds((1, HD)), _sds((1, HD)), _sds((1, HD))],
               scratch=[pltpu.VMEM((1, HD), F32)])(dqn, dkn, dv, dgate, z1, z1, z1, qg, kg, fbp, dc)


LOG2E = 1.4426950408889634
FOX_QSCALE = HD ** -0.5 * LOG2E


def _fox_steps(t, bq, bk, k_major):
    nq, nk = t // bq, t // bk
    pairs = [(i, j) for i in range(nq) for j in range(nk) if j * bk < (i + 1) * bq]
    if k_major:
        pairs.sort(key=lambda p: (p[1], p[0]))
    outer = [p[1] if k_major else p[0] for p in pairs]
    n = len(pairs)
    flags = [(n_ == 0 or outer[n_] != outer[n_ - 1]) + 2 * (n_ == n - 1 or outer[n_] != outer[n_ + 1])
             + 4 * (not (j + 1) * bk <= i * bq + 1) for n_, (i, j) in enumerate(pairs)]
    as_i32 = lambda v: jnp.asarray(v, jnp.int32)
    return as_i32([p[0] for p in pairs]), as_i32([p[1] for p in pairs]), as_i32(flags)


def _fox_step_info(qi_ref, kj_ref, fl_ref):
    s = pl.program_id(1)
    fl = fl_ref[s]
    return qi_ref[s], kj_ref[s], (fl & 1) != 0, (fl & 2) != 0, (fl & 4) != 0


def _fox_call(body, name, tables, in_specs, out_specs, out_shape, scratch):
    grid_spec = pltpu.PrefetchScalarGridSpec(num_scalar_prefetch=3, grid=(NH1 // FOX_HEADS_PER_STEP, tables[0].shape[0]),
                                             in_specs=in_specs, out_specs=out_specs, scratch_shapes=scratch)
    return pl.pallas_call(body, name=name, grid_spec=grid_spec, out_shape=out_shape,
                          compiler_params=pltpu.CompilerParams(dimension_semantics=("arbitrary", "arbitrary"),
                                                               vmem_limit_bytes=VMEM_LIMIT_V7X))


def _fox_lane_tiles(x):
    return [x[:, c0:c0 + HD] for c0 in range(0, x.shape[1], HD)]


def _fox_masked_scores(q, k, ck, i, j, bq, bk, masked):
    s = _nt(q, k) - ck
    if masked:
        rows = i * bq + lax.broadcasted_iota(jnp.int32, s.shape, 0)
        cols = j * bk + lax.broadcasted_iota(jnp.int32, s.shape, 1)
        s = jnp.where(cols <= rows, s, NEG)
    return s


def _fox_specs(bq, bk, G):
    qspec = pl.BlockSpec((bq, G * HD), lambda h, s, qi, kj, fl: (qi[s], h))
    kspec = pl.BlockSpec((bk, G * HD), lambda h, s, qi, kj, fl: (kj[s], h))
    cspec = pl.BlockSpec((G, 1, bk), lambda h, s, qi, kj, fl: (h, 0, kj[s]))
    colspec = pl.BlockSpec((G, bq, 1), lambda h, s, qi, kj, fl: (h, qi[s], 0))
    return qspec, kspec, cspec, colspec


def _fox_rowmax(qn, kn, crow, name):
    t = qn.shape[0]
    bq, bk, G = min(FOX_BQ, t), min(2 * FOX_BK, t), FOX_HEADS_PER_STEP
    tables = _fox_steps(t, bq, bk, k_major=False)

    def body(qi_ref, kj_ref, fl_ref, q_ref, k_ref, ck_ref, m_ref, *mp):
        i, j, first, last, diag = _fox_step_info(qi_ref, kj_ref, fl_ref)

        @pl.when(first)
        def _():
            for g in range(G):
                mp[g][...] = jnp.full_like(mp[g], NEG)

        def step(masked):
            for g in range(G):
                sl = slice(g * HD, (g + 1) * HD)
                s = _fox_masked_scores(q_ref[:, sl], k_ref[:, sl], ck_ref[g], i, j, bq, bk, masked)
                m = mp[g][...]
                for tile in _fox_lane_tiles(s):
                    m = jnp.maximum(m, tile)
                mp[g][...] = m

        pl.when(jnp.logical_not(diag))(lambda: step(False))
        pl.when(diag)(lambda: step(True))

        @pl.when(last)
        def _():
            for g in range(G):
                m_ref[g] = jnp.max(mp[g][...], axis=1, keepdims=True)

    qspec, kspec, cspec, colspec = _fox_specs(bq, bk, G)
    return _fox_call(body, name, tables, [qspec, kspec, cspec], colspec, _sds((NH1, t, 1)),
                     [pltpu.VMEM((bq, HD), F32)] * G)(*tables, qn, kn, crow)


def _fox_fwd(qn, kn, vb, crow, m, name):
    t = qn.shape[0]
    bq, bk, G = min(FOX_BQ, t), min(FOX_BK, t), FOX_HEADS_PER_STEP
    tables = _fox_steps(t, bq, bk, k_major=False)

    def body(qi_ref, kj_ref, fl_ref, q_ref, k_ref, v_ref, ck_ref, m_ref, o_ref, lse_ref, *scr):
        i, j, first, last, diag = _fox_step_info(qi_ref, kj_ref, fl_ref)
        lp, acc = scr[:G], scr[G:]

        @pl.when(first)
        def _():
            for g in range(G):
                lp[g][...] = jnp.zeros_like(lp[g])
                acc[g][...] = jnp.zeros_like(acc[g])

        def step(masked):
            for g in range(G):
                sl = slice(g * HD, (g + 1) * HD)
                s = _fox_masked_scores(q_ref[:, sl], k_ref[:, sl], ck_ref[g], i, j, bq, bk, masked)
                p = jnp.exp2(s - m_ref[g])
                l = lp[g][...]
                for tile in _fox_lane_tiles(p):
                    l = l + tile
                lp[g][...] = l
                acc[g][...] += _nn(p.astype(BF16), v_ref[:, sl])

        pl.when(jnp.logical_not(diag))(lambda: step(False))
        pl.when(diag)(lambda: step(True))

        @pl.when(last)
        def _():
            for g in range(G):
                l = jnp.sum(lp[g][...], axis=1, keepdims=True)
                o_ref[:, g * HD:(g + 1) * HD] = acc[g][...] / l
                lse_ref[g] = m_ref[g] + jnp.log2(l)

    qspec, kspec, cspec, colspec = _fox_specs(bq, bk, G)
    return _fox_call(body, name, tables, [qspec, kspec, kspec, cspec, colspec], [qspec, colspec],
                     [_sds((t, D)), _sds((NH1, t, 1))], [pltpu.VMEM((bq, HD), F32)] * (2 * G))(*tables, qn, kn, vb, crow, m)


def _fox_bwd(qn, kn, vb, crow, lse, delta, do, name):
    t = qn.shape[0]
    bq, bk, G = min(FOX_BQ, t), min(FOX_BK, t), FOX_HEADS_PER_STEP
    tables = _fox_steps(t, bq, bk, k_major=True)

    def body(qi_ref, kj_ref, fl_ref, q_ref, k_ref, v_ref, ck_ref, lse_ref, dl_ref, do_ref, dq_ref, dk_ref, dv_ref, dc_ref, dcq_ref,
             dk_s, dv_s, dc_s):
        i, j, first, last, diag = _fox_step_info(qi_ref, kj_ref, fl_ref)

        @pl.when(first)
        def _():
            dk_s[...] = jnp.zeros_like(dk_s)
            dv_s[...] = jnp.zeros_like(dv_s)
            dc_s[...] = jnp.zeros_like(dc_s)

        @pl.when(pl.program_id(1) == 0)
        def _():
            dq_ref[...] = jnp.zeros_like(dq_ref)
            dcq_ref[...] = jnp.zeros_like(dcq_ref)

        def step(masked):
            rows = pl.ds(pl.multiple_of(i * bq, bq), bq)
            for g in range(G):
                sl = slice(g * HD, (g + 1) * HD)
                q, k = q_ref[:, sl], k_ref[:, sl]
                s = _fox_masked_scores(q, k, ck_ref[g], i, j, bq, bk, masked)
                p = jnp.exp2(s - lse_ref[g])
                dob = do_ref[:, sl]
                dv_s[:, sl] += _tn(p.astype(BF16), dob)
                ds = p * (_nt(dob, v_ref[:, sl]) - dl_ref[g])
                dsb = ds.astype(BF16)
                dq_ref[rows, sl] += _nn(dsb, k)
                dk_s[:, sl] += _tn(dsb, q)
                dc_s[g] -= jnp.sum(ds, axis=0, keepdims=True)
                part_sum = dcq_ref[g, rows, :]
                for tile in _fox_lane_tiles(ds):
                    part_sum = part_sum + tile
                dcq_ref[g, rows, :] = part_sum

        pl.when(jnp.logical_not(diag))(lambda: step(False))
        pl.when(diag)(lambda: step(True))

        @pl.when(last)
        def _():
            dk_ref[...] = dk_s[...] * (1.0 / LOG2E)
            dv_ref[...] = dv_s[...]
            dc_ref[...] = dc_s[...]

    qspec, kspec, cspec, colspec = _fox_specs(bq, bk, G)
    return _fox_call(
        body, name, tables, [qspec, kspec, kspec, cspec, colspec, colspec, qspec],
        [pl.BlockSpec((t, G * HD), lambda h, s, qi, kj, fl: (0, h)), kspec, kspec, cspec,
         pl.BlockSpec((G, t, HD), lambda h, s, qi, kj, fl: (h, 0, 0))],
        [_sds((t, D)), _sds((t, D)), _sds((t, D)), _sds((NH1, 1, t)), _sds((NH1, t, HD))],
        [pltpu.VMEM((bk, G * HD), F32), pltpu.VMEM((bk, G * HD), F32), pltpu.VMEM((G, 1, bk), F32)],
    )(*tables, qn, kn, vb, crow, lse, delta, do)


def _post1_fwd(o, z1, w, h3, name, bm=512):
    t = o.shape[0]
    bm = min(bm, t)

    def body(o_ref, g_ref, w_ref, h_ref, out_ref, og_ref):
        og_ref[...] = (o_ref[...] * _sigmoid(g_ref[...])).astype(BF16)
        out_ref[...] = h_ref[...] + _nn(og_ref[...], w_ref[...])

    row = lambda c: pl.BlockSpec((bm, D), lambda i: (i, c))
    return _pc(body, name, (t // bm,), [row(0), row(3), pl.BlockSpec((D, D), lambda i: (0, 0)), row(0)],
               [row(0), row(0)], [_sds((t, D)), _sds((t, D), BF16)])(o, z1, w, h3)


def _post1_bwd(dh4, w, o, z1, name, bm=512):
    t = o.shape[0]
    bm = min(bm, t)

    def body(dh_ref, w_ref, o_ref, g_ref, do_ref, dg_ref, dl_ref):
        d_og = _nt(dh_ref[...].astype(BF16), w_ref[...])
        o_, sg = o_ref[...], _sigmoid(g_ref[...])
        dob = (d_og * sg).astype(BF16)
        do_ref[...] = dob
        dg_ref[...] = (d_og * o_ * sg * (1.0 - sg)).astype(BF16)
        prod = dob.astype(F32) * o_
        for hd in range(NH1):
            dl_ref[hd] = jnp.sum(prod[:, hd * HD:(hd + 1) * HD], axis=1, keepdims=True)

    row = lambda c: pl.BlockSpec((bm, D), lambda i: (i, c))
    return _pc(body, name, (t // bm,), [row(0), pl.BlockSpec((D, D), lambda i: (0, 0)), row(0), row(3)],
               [row(0), row(0), pl.BlockSpec((NH1, bm, 1), lambda i: (0, i, 0))],
               [_sds((t, D), BF16), _sds((t, D), BF16), _sds((NH1, t, 1))])(dh4, w, o, z1)


def _final(h, g, tgt, name, bm=512):
    t = h.shape[0]
    bm = min(bm, t)

    def body(h_ref, g_ref, t_ref, l_ref, dh_ref, dg_ref):
        @pl.when(pl.program_id(0) == 0)
        def _():
            l_ref[...] = jnp.zeros_like(l_ref)
            dg_ref[...] = jnp.zeros_like(dg_ref)

        x, gv = h_ref[...], g_ref[...]
        r = _rstd(x)
        xh = x * r
        e = xh * gv - t_ref[...]
        l_ref[...] += 0.5 * jnp.sum(jnp.mean(e * e, axis=1, keepdims=True), axis=0, keepdims=True)
        dy = e * (1.0 / D)
        dg_ref[...] += jnp.sum(dy * xh, axis=0, keepdims=True)
        dxh = dy * gv
        dh_ref[...] = r * (dxh - xh * jnp.mean(dxh * xh, axis=1, keepdims=True))

    row = pl.BlockSpec((bm, D), lambda i: (i, 0))
    vec = pl.BlockSpec((1, D), lambda i: (0, 0))
    return _pc(body, name, (t // bm,), [row, vec, row], [pl.BlockSpec((1, HD), lambda i: (0, 0)), row, vec],
               [_sds((1, HD)), _sds((t, D)), _sds((1, D))])(h, g, tgt)


def _adam(w, g, m, v, name):
    r, c = w.shape
    br = min(r, 256)

    def body(w_ref, g_ref, m_ref, v_ref, d_ref, mo_ref, vo_ref):
        gv = g_ref[...]
        mn = ADAM_B1 * m_ref[...] + (1.0 - ADAM_B1) * gv
        vn = ADAM_B2 * v_ref[...] + (1.0 - ADAM_B2) * jnp.square(gv)
        m_hat = mn / (1.0 - ADAM_B1 ** ADAM_STEP)
        v_hat = vn / (1.0 - ADAM_B2 ** ADAM_STEP)
        d_ref[...] = -ADAM_LR * (m_hat / (jnp.sqrt(v_hat) + ADAM_EPS) + ADAM_WD * w_ref[...])
        mo_ref[...] = mn
        vo_ref[...] = vn

    blk = pl.BlockSpec((br, c), lambda i: (i, 0))
    return _pc(body, name, (r // br,), [blk] * 4, [blk] * 3, [_sds((r, c))] * 3)(w, g, m, v)


ZW = 4224
GATE0 = 4096


def _pack_w_in0(w):
    return jnp.concatenate([w[:, :2048], w[:, 2056:], w[:, 2048:2056], jnp.zeros((w.shape[0], ZW - 4104), w.dtype)], axis=1)


def _unpack_w_in0(g):
    return jnp.concatenate([g[:, :2048], g[:, GATE0:GATE0 + 8], g[:, 2048:GATE0]], axis=1)


def _pack_w_in1(w):
    return jnp.concatenate([w, jnp.zeros((w.shape[0], ZW - 4104), w.dtype)], axis=1)


def _unpack_w_in1(g):
    return g[:, :4104]


def _local_step(x, mem, tgt, W, S, late_weights=None, grads_hook=None):
    t = x.shape[0]
    row = lambda v: v.reshape(1, -1)
    G = {}

    z0, u0 = _norm_mm(x, S["norm_mix_g"][0:1], W["w_in0"], "in0_fwd")
    qk = _conv_fwd(z0, S["conv_w"], "conv_fwd")
    g8 = z0[:, GATE0:GATE0 + 8]
    gates3 = jnp.stack([g8[:, :4].T, g8[:, 4:].T], axis=-1)
    gb = S["gate_b"]
    bias3 = jnp.stack([gb[0, :4], gb[0, 4:]], axis=-1)[:, None, :]
    hm, cs, ns, ms = _mlstm_fwd(qk, z0, gates3, bias3, "mlstm_fwd")
    hh, ss = _hgrn_fwd(z0, S["lb_logits"], "hgrn_fwd")
    if late_weights is not None:
        W = {**W, **late_weights(hh)}
    kv, mn = _memkv_fwd(mem, row(S["mem_norm_g"]), W["wkv_s"], "memkv_fwd")
    h1, y0 = _post0_fwd(hm, hh, z0, S["mlstm_norm_g"], S["hgrn_norm_g"], W["w_out0"], x, "post0_fwd")

    def xattn_mlp_fwd(h, l):
        q, ux = _norm_mm(h, S["norm_xattn_g"][l:l + 1], W["wq"][l], f"xq{l}_fwd")
        h2, ox = _xattn_fwd(q, kv, W["wo"][l], h, f"xattn{l}_fwd")
        h3, a, um = _mlp_fwd(h2, S["norm_mlp_g"][l:l + 1], W["w1s"], W["w2"], l, f"mlp{l}_fwd")
        return h3, (h, q, ux, ox, h2, a, um)

    h3, sv0 = xattn_mlp_fwd(h1, 0)
    z1, u1 = _norm_mm(h3, S["norm_mix_g"][1:2], W["w_in1"], "in1_fwd")
    fbp = jnp.pad(S["c_fgate_b"], ((0, 0), (0, HD - NH1)))
    qn, kn, vb, c = _foxprep_fwd(z1, S["c_qnorm_g"], S["c_knorm_g"], fbp, "foxprep_fwd")
    crow = (c[:, :NH1] * LOG2E).T[:, None, :]
    o1, lse = _fox_fwd(qn, kn, vb, crow, _fox_rowmax(qn, kn, crow, "fox_rowmax"), "fox_fwd")
    h4, og = _post1_fwd(o1, z1, W["w_out1"], h3, "post1_fwd")
    h6, sv1 = xattn_mlp_fwd(h4, 1)
    lossp, dh, G["final_norm_g"] = _final(h6, row(S["final_norm_g"]), tgt, "final")

    grads_ready = grads_hook if grads_hook is not None else (lambda stage, grads: 0.0)
    dkv = None
    dgx, dgm, dwq, dwo, dw1, dw2 = [None, None], [None, None], [None, None], [None, None], [None, None], [None, None]

    def xattn_mlp_bwd(dh, l, sv):
        nonlocal dkv
        h, q, ux, ox, h2, a, um = sv
        dh2, da, r, dgm[l] = _mlp_bwd(dh, a, W["w1s"], W["w2"], l, h2, S["norm_mlp_g"][l:l + 1], f"mlp{l}_bwd")
        dw1[l] = _mm_tn(um, da, f"mlp{l}_dw1", col_chips=NCHIP)
        dw2[l] = _mm_tn(r, dh, f"mlp{l}_dw2")
        dq, dkv_l = _xattn_bwd(dh2, q, kv, W["wo"][l], f"xattn{l}_bwd")
        dkv = dkv_l if dkv is None else dkv + dkv_l
        dwo[l] = _mm_tn(ox, dh2, f"xattn{l}_dwo")
        dwq[l] = _mm_tn(ux, dq, f"xattn{l}_dwq")
        dh1, dgx[l] = _bwd_in(dq, W["wq"][l], h, S["norm_xattn_g"][l:l + 1], dh2, f"xq{l}_bwd")
        return dh1

    dh4 = xattn_mlp_bwd(dh, 1, sv1)
    do, dgate, delta = _post1_bwd(dh4, W["w_out1"], o1, z1, "post1_bwd")
    G["w_out1"] = _mm_tn(og, dh4, "post1_dw")
    dqn, dkn, dv1, dcrow, dcq = _fox_bwd(qn, kn, vb, crow, lse, delta, do, "fox_bwd")
    dc = jnp.pad((dcrow[:, 0, :] + jnp.sum(dcq, axis=-1)).T, ((0, 0), (0, HD - NH1)))
    dz1, G["c_qnorm_g"], G["c_knorm_g"], dfb = _foxprep_bwd(
        dqn, dkn, dv1, dgate, z1, S["c_qnorm_g"], S["c_knorm_g"], fbp, dc, "foxprep_bwd")
    G["c_fgate_b"] = dfb[:, :NH1]
    G["w_in1"] = _mm_tn(u1, dz1, "in1_dw")
    tok = grads_ready("layer1", dict(w_out=G["w_out1"], w_in=G["w_in1"], wq=dwq[1], wo=dwo[1], w1=dw1[1], w2=dw2[1]))
    dh3, dgmix1 = _bwd_in(dz1, W["w_in1"], h3, S["norm_mix_g"][1:2] + tok, dh4, "in1_bwd")
    dh1 = xattn_mlp_bwd(dh3, 0, sv0)

    G["wkv"] = _mm_tn(mn, dkv, "memkv_dw", col_chips=NCHIP)
    G["mem_norm_g"] = _memkv_bwd(dkv, W["wkv_s"], mem, row(S["mem_norm_g"]), "memkv_bwd")
    G["w_out0"] = _mm_tn(y0, dh1, "post0_dw")
    tok = grads_ready("layer0", dict(wq=dwq[0], wo=dwo[0], w1=dw1[0], w2=dw2[0], wkv=G["wkv"], w_out=G["w_out0"]))
    dhm, dhh, doa, dgb, G["mlstm_norm_g"], G["hgrn_norm_g"] = _post0_bwd(
        dh1, W["w_out0"], hm, hh, z0, S["mlstm_norm_g"] + tok, S["hgrn_norm_g"], "post0_bwd")
    dqka, dva, dgates3 = _mlstm_bwd(qk, z0, gates3, bias3, cs, ns, ms, dhm, "mlstm_bwd")
    dqb, dfb0, dib, G["lb_logits"] = _hgrn_bwd(z0, S["lb_logits"], ss, dhh, "hgrn_bwd")
    duc, G["conv_w"] = _conv_bwd(z0, S["conv_w"], dqka, "conv_bwd")
    dg8 = jnp.concatenate([dgates3[:, :, 0].T, dgates3[:, :, 1].T], axis=1)
    G["gate_b"] = jnp.sum(dg8, axis=0, keepdims=True)
    dz0 = jnp.concatenate([duc, dva.astype(BF16), doa, dqb, dfb0, dib, dgb,
                           jnp.pad(dg8, ((0, 0), (0, HD - 8))).astype(BF16)], axis=1)
    G["w_in0"] = _mm_tn(u0, dz0, "in0_dw")
    tok = grads_ready("in0", dict(w_in=G["w_in0"]))
    dx, dgmix0 = _bwd_in(dz0, W["w_in0"], x, S["norm_mix_g"][0:1] + tok, dh1, "in0_bwd")

    G["norm_mix_g"] = jnp.concatenate([dgmix0, dgmix1], axis=0)
    G["norm_xattn_g"] = jnp.concatenate(dgx, axis=0)
    G["norm_mlp_g"] = jnp.concatenate(dgm, axis=0)
    G["wq"], G["wo"], G["w1"], G["w2"] = dwq, dwo, dw1, dw2
    return lossp[0, 0], dx, G


ANY = pl.BlockSpec(memory_space=pl.ANY)
NCHIP = 4


def _place():
    x, y, c = lax.axis_index("x"), lax.axis_index("y"), lax.axis_index("c")
    return x, y, c, [(1 - x, y), (x, 1 - y), (1 - x, 1 - y)]


def _comm_call(body, name, ins, out_shapes, sems):
    return pl.pallas_call(body, name=name, in_specs=[ANY] * len(ins), out_specs=[ANY] * len(out_shapes),
                          out_shape=out_shapes, scratch_shapes=sems)(*ins)


HBM = pl.BlockSpec(memory_space=pltpu.HBM)
SEM = pl.BlockSpec(memory_space=pltpu.SEMAPHORE)
DATAFLOW = pltpu.SideEffectType.DATAFLOW_SIDE_EFFECTING


def _half_rows(r, cc):
    return pl.ds(pl.multiple_of(cc * (r // 2), r // 2), r // 2)


def _gather_start(arrs, after, name):
    n = len(arrs)

    def body(*refs):
        ins, lands = refs[:n], refs[n:2 * n]
        send, recv, token = refs[2 * n + 1], refs[2 * n + 2], refs[-1]
        x, y, c, chips = _place()
        me = 2 * x + y
        for a in range(n):
            rows = _half_rows(arrs[a].shape[0], c)
            for k, (px, py) in enumerate(chips):
                pltpu.make_async_remote_copy(src_ref=ins[a].at[rows], dst_ref=lands[a].at[me, rows], send_sem=send.at[3 * a + k],
                                             recv_sem=recv.at[3 * a + k], device_id=(px, py, c), device_id_type=MESH).start()
        token[...] = jnp.zeros_like(token)

    hbm = lambda v: pltpu.with_memory_space_constraint(v, pltpu.HBM)
    land_shapes = [((NCHIP,) + a.shape, a.dtype) for a in arrs]
    out = pl.pallas_call(
        body, name=name,
        out_shape=(pltpu.SemaphoreType.DMA((3 * n,)), pltpu.SemaphoreType.DMA((3 * n,)), *[pltpu.HBM(a.shape, a.dtype) for a in arrs],
                   *[pltpu.HBM(s, d) for s, d in land_shapes], _sds((8, HD))),
        in_specs=[HBM] * (2 * n) + [ANY], out_specs=(SEM, SEM, *[HBM] * (2 * n), pl.BlockSpec(memory_space=pltpu.VMEM)),
        input_output_aliases={i: 2 + i for i in range(2 * n)},
        compiler_params=pltpu.CompilerParams(has_side_effects=DATAFLOW),
    )(*[hbm(a) for a in arrs], *[hbm(lax.empty(s, d)) for s, d in land_shapes], after)
    return out[0], out[1], list(out[2:2 + n]), list(out[2 + n:2 + 2 * n]), out[-1]


def _gather_wait(send, recv, srcs, lands, after, name):
    n = len(srcs)

    def body(*refs):
        ins, lands_ = refs[:n], refs[n:2 * n]
        send_, recv_ = refs[2 * n], refs[2 * n + 1]
        x, y, c, chips = _place()
        for a in range(n):
            rows = _half_rows(srcs[a].shape[0], c)
            for k, (px, py) in enumerate(chips):
                cp = pltpu.make_async_remote_copy(src_ref=ins[a].at[rows], dst_ref=lands_[a].at[2 * px + py, rows], send_sem=send_.at[3 * a + k],
                                                  recv_sem=recv_.at[3 * a + k], device_id=(px, py, c), device_id_type=MESH)
                cp.wait_send()
                cp.wait_recv()

    out = pl.pallas_call(
        body, name=name, out_shape=[pltpu.HBM(v.shape, v.dtype) for v in list(srcs) + list(lands)],
        in_specs=[HBM] * (2 * n) + [SEM, SEM, ANY], out_specs=[HBM] * (2 * n), input_output_aliases={i: i for i in range(2 * n)},
        compiler_params=pltpu.CompilerParams(has_side_effects=DATAFLOW),
    )(*srcs, *lands, send, recv, after)
    return list(out[n:])


def _pair_forward(lands, name):
    n = len(lands)

    def body(*refs):
        ins, outs = refs[:n], refs[n:2 * n]
        send, recv = refs[2 * n:]
        x, y, c, chips = _place()
        copies = []
        for a in range(n):
            r = lands[a].shape[1]
            for k, (px, py) in enumerate(chips):
                cp = pltpu.make_async_remote_copy(
                    src_ref=ins[a].at[2 * px + py, _half_rows(r, c)], dst_ref=outs[a].at[2 * px + py, _half_rows(r, c)],
                    send_sem=send.at[a, k], recv_sem=recv.at[a, k], device_id=(x, y, 1 - c), device_id_type=MESH)
                cp.start()
                copies.append(cp)
        for a in range(n):
            r = lands[a].shape[1]
            for k, (px, py) in enumerate(chips):
                pltpu.make_async_remote_copy(
                    src_ref=ins[a].at[2 * px + py, _half_rows(r, c)], dst_ref=outs[a].at[2 * px + py, _half_rows(r, 1 - c)],
                    send_sem=send.at[a, k], recv_sem=recv.at[a, k], device_id=(x, y, 1 - c), device_id_type=MESH).wait_recv()
        for cp in copies:
            cp.wait_send()

    return pl.pallas_call(body, name=name, in_specs=[ANY] * n, out_specs=[ANY] * n, out_shape=[_sds(v.shape, v.dtype) for v in lands],
                          scratch_shapes=[pltpu.SemaphoreType.DMA((n, 3)), pltpu.SemaphoreType.DMA((n, 3))],
                          input_output_aliases={i: i for i in range(n)})(*lands)


def _pair_exchange(arrs, name):
    n = len(arrs)

    def body(*refs):
        ins, outs = refs[:n], refs[n:2 * n]
        send, recv = refs[2 * n:]
        x, y, c, _ = _place()
        copies = []
        for a in range(n):
            h = arrs[a].shape[1] // 2
            cp = pltpu.make_async_remote_copy(src_ref=ins[a].at[:, pl.ds(pl.multiple_of((1 - c) * h, h), h)], dst_ref=outs[a],
                                              send_sem=send.at[a], recv_sem=recv.at[a], device_id=(x, y, 1 - c), device_id_type=MESH)
            cp.start()
            copies.append(cp)
        for cp in copies:
            cp.wait()

    return _comm_call(body, name, arrs, [_sds((a.shape[0], a.shape[1] // 2, a.shape[2]), a.dtype) for a in arrs],
                      [pltpu.SemaphoreType.DMA((n,)), pltpu.SemaphoreType.DMA((n,))])


def _chip_exchange_start(arrs, name):
    n = len(arrs)

    def body(*refs):
        ins, lands = refs[:n], refs[n:2 * n]
        send, recv, token = refs[2 * n], refs[2 * n + 1], refs[-1]
        x, y, c, chips = _place()
        me = 2 * x + y
        for a in range(n):
            for k, (px, py) in enumerate(chips):
                pltpu.make_async_remote_copy(src_ref=ins[a].at[2 * px + py], dst_ref=lands[a].at[me], send_sem=send.at[3 * a + k],
                                             recv_sem=recv.at[3 * a + k], device_id=(px, py, c), device_id_type=MESH).start()
        token[...] = jnp.zeros_like(token)

    hbm = lambda v: pltpu.with_memory_space_constraint(v, pltpu.HBM)
    out = pl.pallas_call(
        body, name=name,
        out_shape=(pltpu.SemaphoreType.DMA((3 * n,)), pltpu.SemaphoreType.DMA((3 * n,)), *[pltpu.HBM(a.shape, a.dtype) for a in arrs],
                   *[pltpu.HBM(a.shape, a.dtype) for a in arrs], _sds((8, HD))),
        in_specs=[HBM] * (2 * n), out_specs=(SEM, SEM, *[HBM] * (2 * n), pl.BlockSpec(memory_space=pltpu.VMEM)),
        input_output_aliases={i: 2 + i for i in range(2 * n)},
        compiler_params=pltpu.CompilerParams(has_side_effects=DATAFLOW),
    )(*[hbm(a) for a in arrs], *[hbm(lax.empty(a.shape, a.dtype)) for a in arrs])
    return out[0], out[1], list(out[2:2 + n]), list(out[2 + n:2 + 2 * n]), out[-1]


def _chip_exchange_wait(send, recv, srcs, lands, after, name):
    n = len(srcs)

    def body(*refs):
        ins, lands_ = refs[:n], refs[n:2 * n]
        send_, recv_ = refs[2 * n], refs[2 * n + 1]
        x, y, c, chips = _place()
        for a in range(n):
            for k, (px, py) in enumerate(chips):
                cp = pltpu.make_async_remote_copy(src_ref=ins[a].at[2 * px + py], dst_ref=lands_[a].at[2 * px + py], send_sem=send_.at[3 * a + k],
                                                  recv_sem=recv_.at[3 * a + k], device_id=(px, py, c), device_id_type=MESH)
                cp.wait_send()
                cp.wait_recv()

    out = pl.pallas_call(
        body, name=name, out_shape=[pltpu.HBM(v.shape, v.dtype) for v in list(srcs) + list(lands)],
        in_specs=[HBM] * (2 * n) + [SEM, SEM, ANY], out_specs=[HBM] * (2 * n), input_output_aliases={i: i for i in range(2 * n)},
        compiler_params=pltpu.CompilerParams(has_side_effects=DATAFLOW),
    )(*srcs, *lands, send, recv, after)
    return list(out[n:])


def _pair_swap(arrs, name):
    n = len(arrs)

    def body(*refs):
        ins, outs = refs[:n], refs[n:2 * n]
        send, recv = refs[2 * n:]
        x, y, c, _ = _place()
        copies = []
        for a in range(n):
            cp = pltpu.make_async_remote_copy(src_ref=ins[a], dst_ref=outs[a], send_sem=send.at[a], recv_sem=recv.at[a],
                                              device_id=(x, y, 1 - c), device_id_type=MESH)
            cp.start()
            copies.append(cp)
        for cp in copies:
            cp.wait()

    return _comm_call(body, name, arrs, [_sds(a.shape, a.dtype) for a in arrs],
                      [pltpu.SemaphoreType.DMA((n,)), pltpu.SemaphoreType.DMA((n,))])


def _all_gather_devices(v, name):
    def body(v_ref, o_ref, send, recv, loc):
        x, y, c, _ = _place()
        me = 4 * x + 2 * y + c
        own = pltpu.make_async_copy(v_ref, o_ref.at[me], loc)
        own.start()
        copies = [own]
        for k in range(1, 8):
            fx, fy, fc = (k >> 2) & 1, (k >> 1) & 1, k & 1
            peer = (x ^ fx, y ^ fy, c ^ fc)
            r = pltpu.make_async_remote_copy(src_ref=v_ref, dst_ref=o_ref.at[me], send_sem=send.at[k - 1],
                                             recv_sem=recv.at[k - 1], device_id=peer, device_id_type=MESH)
            r.start()
            copies.append(r)
        for cp in copies:
            cp.wait()

    return _comm_call(body, name, [v], [_sds((8,) + v.shape, v.dtype)],
                      [pltpu.SemaphoreType.DMA((7,)), pltpu.SemaphoreType.DMA((7,)), pltpu.SemaphoreType.DMA])[0]


def _row_tile(r):
    return next((b for b in (512, 384, 256, 128, 64, 32, 16) if r % b == 0), r)


def _add2(a, b, out_dtype, name):
    r, w = a.shape
    br = _row_tile(r)

    def body(a_ref, b_ref, o_ref):
        o_ref[...] = (a_ref[...].astype(F32) + b_ref[...].astype(F32)).astype(out_dtype)

    blk = pl.BlockSpec((br, w), lambda i: (i, 0))
    return _pc(body, name, (r // br,), [blk, blk], blk, _sds((r, w), out_dtype))(a, b)


def _sum_slots(a, out_dtype, name):
    n, r, w = a.shape
    br = _row_tile(r)

    def body(a_ref, o_ref):
        acc = a_ref[0].astype(F32)
        for s in range(1, n):
            acc = acc + a_ref[s].astype(F32)
        o_ref[...] = acc.astype(out_dtype)

    return _pc(body, name, (r // br,), [pl.BlockSpec((n, br, w), lambda i: (0, i, 0))], pl.BlockSpec((br, w), lambda i: (i, 0)),
               _sds((r, w), out_dtype))(a)


SMALL = ["norm_mix_g", "norm_xattn_g", "norm_mlp_g", "final_norm_g", "mem_norm_g", "hgrn_lb_logits", "mlstm_norm_g",
         "hgrn_norm_g", "c_qnorm_g", "c_knorm_g", "ab_gate_b", "c_fgate_b"]
SMALL_ROWS = 16


def _pack_small(parts):
    flat = jnp.concatenate([p.reshape(-1).astype(F32) for p in parts])
    return jnp.pad(flat, (0, SMALL_ROWS * D - flat.shape[0])).reshape(SMALL_ROWS, D)


def _unpack_small(buf, shapes):
    flat, out, off = buf.reshape(-1), [], 0
    for s in shapes:
        n = 1
        for d in s:
            n *= d
        out.append(flat[off:off + n].reshape(s))
        off += n
    return out


def kernel(x, mem, norm_mix_g, norm_xattn_g, norm_mlp_g, final_norm_g, ab_w_in, ab_conv_w, ab_gate_b, hgrn_lb_logits, mlstm_norm_g, hgrn_norm_g, ab_w_out, c_w_in, c_fgate_b, c_qnorm_g, c_knorm_g, c_w_out, mem_norm_g, mem_w_kv, xa_w_q, xa_w_o, mlp_w1, mlp_w2, loss_target, m_norm_mix_g, m_norm_xattn_g, m_norm_mlp_g, m_final_norm_g, m_ab_w_in, m_ab_conv_w, m_ab_gate_b, m_hgrn_lb_logits, m_mlstm_norm_g, m_hgrn_norm_g, m_ab_w_out, m_c_w_in, m_c_fgate_b, m_c_qnorm_g, m_c_knorm_g, m_c_w_out, m_mem_norm_g, m_mem_w_kv, m_xa_w_q, m_xa_w_o, m_mlp_w1, m_mlp_w2, v_norm_mix_g, v_norm_xattn_g, v_norm_mlp_g, v_final_norm_g, v_ab_w_in, v_ab_conv_w, v_ab_gate_b, v_hgrn_lb_logits, v_mlstm_norm_g, v_hgrn_norm_g, v_ab_w_out, v_c_w_in, v_c_fgate_b, v_c_qnorm_g, v_c_knorm_g, v_c_w_out, v_mem_norm_g, v_mem_w_kv, v_xa_w_q, v_xa_w_o, v_mlp_w1, v_mlp_w2):
    A = dict(locals())
    chip = 2 * lax.axis_index("x") + lax.axis_index("y")

    big = ["ab_w_in", "c_w_in", "ab_w_out", "c_w_out", "mem_w_kv", "xa_w_q", "xa_w_o", "mlp_w1", "mlp_w2"]
    shard2d = {"ab_w_in": (D, 1026), "c_w_in": (D, 1026), "ab_w_out": (256, D), "c_w_out": (256, D), "mem_w_kv": (D, 512),
               "xa_w_q": (512, D), "xa_w_o": (512, D), "mlp_w1": (2 * D, D), "mlp_w2": (2 * D, D)}
    own_slot = lambda gs, os: [lax.dynamic_update_index_in_dim(g, o, chip, 0) for g, o in zip(gs, os)]
    cols = lambda g: jnp.concatenate([g[k] for k in range(NCHIP)], axis=1)
    per_layer = lambda g: g.reshape(NCHIP, 2, -1, D).transpose(1, 0, 2, 3)
    first = [A["ab_w_in"].reshape(shard2d["ab_w_in"]).astype(BF16), jnp.pad(ab_conv_w[0], ((0, 16 - CONV_W), (0, 0)))]
    *first_handles, tok_first = _gather_start(first, ab_conv_w, "gather_first_start")
    rest_names = ["c_w_in", "ab_w_out", "c_w_out", "xa_w_q", "xa_w_o", "mlp_w1", "mlp_w2", "mem_w_kv"]
    rest = [(A[n].reshape(shard2d[n]) + tok_first[0, 0]).astype(BF16) for n in rest_names]
    send_s, recv_s, srcs, lands, token = _gather_start(rest, tok_first, "gather_rest_start")
    g_in0, g_conv = own_slot(_pair_forward(_gather_wait(*first_handles, token, "gather_first_wait"), "gather_first_forward"), first)
    W = dict(w_in0=_pack_w_in0(cols(g_in0)))

    def late_weights(after):
        got = _pair_forward(_gather_wait(send_s, recv_s, srcs, lands, after, "gather_rest_wait"), "gather_rest_forward")
        gw = dict(zip(rest_names, own_slot(got, rest)))
        return dict(w_in1=_pack_w_in1(cols(gw["c_w_in"])), w_out0=gw["ab_w_out"].reshape(D, D), w_out1=gw["c_w_out"].reshape(D, D),
                    wkv_s=gw["mem_w_kv"],
                    wq=per_layer(gw["xa_w_q"]).reshape(2, D, D), wo=per_layer(gw["xa_w_o"]).reshape(2, D, D),
                    w1s=gw["mlp_w1"].reshape(NCHIP, 2, D, D), w2=gw["mlp_w2"].reshape(NCHIP, 2, D, D))

    S = dict(norm_mix_g=norm_mix_g + token[0, 0], norm_xattn_g=norm_xattn_g, norm_mlp_g=norm_mlp_g, final_norm_g=final_norm_g,
             conv_w=cols(g_conv[:, :CONV_W]), gate_b=ab_gate_b, lb_logits=hgrn_lb_logits, mlstm_norm_g=mlstm_norm_g,
             hgrn_norm_g=hgrn_norm_g, c_fgate_b=c_fgate_b, c_qnorm_g=c_qnorm_g, c_knorm_g=c_knorm_g, mem_norm_g=mem_norm_g)

    core = lax.axis_index("c")
    by_rows = lambda g: g.reshape(NCHIP, -1, D)

    def stack_cols(g):
        return jnp.stack([g[:, 1026 * k:1026 * (k + 1)] for k in range(NCHIP)])

    def pair_sums(arrs, tag):
        theirs = _pair_exchange(arrs, f"pair_exchange_{tag}")
        out = []
        for i, (a, th) in enumerate(zip(arrs, theirs)):
            h = a.shape[1] // 2
            mine = lax.dynamic_slice_in_dim(a, core * h, h, axis=1)
            out.append(_add2(mine.reshape(-1, a.shape[2]), th.reshape(-1, a.shape[2]), BF16, f"pair_sum_{tag}{i}").reshape(th.shape))
        return out

    def chip_sums(psums, from_chips, tag):
        out = []
        for i, (f, p) in enumerate(zip(from_chips, psums)):
            f = lax.dynamic_update_index_in_dim(f, lax.dynamic_index_in_dim(p, chip, 0, keepdims=False), chip, 0)
            out.append(_sum_slots(f, F32, f"chip_sum_{tag}{i}"))
        return out

    started = {}

    def grads_hook(stage, g):
        if stage == "in0":
            arrs = [stack_cols(_unpack_w_in0(g["w_in"]))]
        else:
            arrs = [jnp.concatenate([by_rows(g["w_out"]), by_rows(g["wq"]), by_rows(g["wo"]), g["w1"], by_rows(g["w2"])], axis=1),
                    stack_cols(_unpack_w_in1(g["w_in"])) if stage == "layer1" else g["wkv"]]
        psums = pair_sums(arrs, stage)
        *handles, token = _chip_exchange_start(psums, f"chip_exchange_start_{stage}")
        started[stage] = (psums, handles)
        return token[0, 0]

    lossp, dx, G = _local_step(x[0], mem[0], loss_target[0], W, S, late_weights, grads_hook)

    gsmall = {"norm_mix_g": G["norm_mix_g"], "norm_xattn_g": G["norm_xattn_g"], "norm_mlp_g": G["norm_mlp_g"],
              "final_norm_g": G["final_norm_g"], "mem_norm_g": G["mem_norm_g"], "hgrn_lb_logits": G["lb_logits"],
              "mlstm_norm_g": G["mlstm_norm_g"], "hgrn_norm_g": G["hgrn_norm_g"], "c_qnorm_g": G["c_qnorm_g"],
              "c_knorm_g": G["c_knorm_g"], "ab_gate_b": G["gate_b"], "c_fgate_b": G["c_fgate_b"]}
    packed = _pack_small([gsmall[n] for n in SMALL] + [G["conv_w"], lossp])
    red = _sum_slots(_all_gather_devices(packed, "gather_small"), F32, "sum_small")
    small_shapes = [A[n].shape for n in SMALL]
    *gs, gconv, loss = _unpack_small(red, small_shapes + [(CONV_W, D), ()])
    gs = dict(zip(SMALL, gs))
    gconv = lax.dynamic_slice_in_dim(gconv, chip * 256, 256, axis=1)[None]

    rhalf = []
    for stage in ("layer1", "layer0", "in0"):
        psums, handles = started[stage]
        rhalf += chip_sums(psums, _chip_exchange_wait(*handles, dx, f"chip_exchange_wait_{stage}"), stage)
    other = _pair_swap(rhalf, "pair_swap")
    r_l1, r_in1, r_l0, r_kv, r_in0 = [
        jnp.where(core == 0, jnp.concatenate([m_, o_], axis=0), jnp.concatenate([o_, m_], axis=0)) for m_, o_ in zip(rhalf, other)]
    both = lambda lo, hi: jnp.concatenate([r_l0[lo:hi], r_l1[lo:hi]], axis=0)
    gbig = {"ab_w_in": r_in0, "c_w_in": r_in1, "mem_w_kv": r_kv, "ab_w_out": r_l0[0:256], "c_w_out": r_l1[0:256],
            "xa_w_q": both(256, 512), "xa_w_o": both(512, 768), "mlp_w1": both(768, 1792), "mlp_w2": both(1792, 2816)}

    out_g, out_d, out_m, out_v = {}, {}, {}, {}
    for n in big:
        d_, m_, v_ = _adam(A[n].reshape(shard2d[n]), gbig[n], A["m_" + n].reshape(shard2d[n]), A["v_" + n].reshape(shard2d[n]), "adam_" + n)
        out_g[n] = gbig[n].reshape(A[n].shape)
        out_d[n], out_m[n], out_v[n] = d_.reshape(A[n].shape), m_.reshape(A[n].shape), v_.reshape(A[n].shape)
    sd, sm, sv = _adam(_pack_small([A[n] for n in SMALL]), _pack_small([gs[n] for n in SMALL]),
                       _pack_small([A["m_" + n] for n in SMALL]), _pack_small([A["v_" + n] for n in SMALL]), "adam_small")
    for n, d_, m_, v_ in zip(SMALL, _unpack_small(sd, small_shapes), _unpack_small(sm, small_shapes), _unpack_small(sv, small_shapes)):
        out_g[n], out_d[n], out_m[n], out_v[n] = gs[n], d_, m_, v_
    cd, cm_, cv = _adam(ab_conv_w[0], gconv[0], m_ab_conv_w[0], v_ab_conv_w[0], "adam_conv")
    out_g["ab_conv_w"], out_d["ab_conv_w"], out_m["ab_conv_w"], out_v["ab_conv_w"] = gconv, cd[None], cm_[None], cv[None]

    order = ["norm_mix_g", "norm_xattn_g", "norm_mlp_g", "final_norm_g", "ab_w_in", "ab_conv_w", "ab_gate_b", "hgrn_lb_logits",
             "mlstm_norm_g", "hgrn_norm_g", "ab_w_out", "c_w_in", "c_fgate_b", "c_qnorm_g", "c_knorm_g", "c_w_out", "mem_norm_g",
             "mem_w_kv", "xa_w_q", "xa_w_o", "mlp_w1", "mlp_w2"]
    return (loss, dx[None], *[out_g[n] for n in order], *[out_d[n] for n in order], *[out_m[n] for n in order],
            *[out_v[n] for n in order])
```

```python
import jax
import jax.numpy as jnp
from jax import lax
from jax.experimental import pallas as pl
from jax.experimental.pallas import tpu as pltpu

F32 = jnp.float32
BF16 = jnp.bfloat16
EPS = 1e-6
D = 1024
CHUNK = 64
REC_CHUNKS = 4
HD = 128
XD = 256
NEG = -1e30
VMEM_LIMIT_V7X = 56 * 1024 * 1024
ADAM_LR, ADAM_B1, ADAM_B2, ADAM_EPS, ADAM_WD, ADAM_STEP = 0.001, 0.9, 0.999, 1e-08, 0.01, 10
MESH = pl.DeviceIdType.MESH


def _pc(body, name, grid, in_specs, out_specs, out_shape, scratch=(), **kw):
    return pl.pallas_call(
        body, name=name, grid=grid, in_specs=in_specs, out_specs=out_specs, out_shape=out_shape,
        scratch_shapes=scratch,
        compiler_params=pltpu.CompilerParams(
            dimension_semantics=("arbitrary",) * len(grid), vmem_limit_bytes=VMEM_LIMIT_V7X), **kw)


def _sds(shape, dtype=F32):
    return jax.ShapeDtypeStruct(shape, dtype)


def _blk(n, target):
    return max(b for b in range(128, max(target, 128) + 1, 128) if n % b == 0)


def _dot(a, b, dims):
    return lax.dot_general(a, b, (dims, ((), ())), preferred_element_type=F32)


def _nn(a, b):
    return _dot(a, b, ((1,), (0,)))


def _nt(a, b):
    return _dot(a, b, ((1,), (1,)))


def _tn(a, b):
    return _dot(a, b, ((0,), (0,)))


def _sigmoid(x):
    return 1.0 / (1.0 + jnp.exp(-x))


def _log_sigmoid(x):
    return jnp.minimum(x, 0.0) - jnp.log(1.0 + jnp.exp(-jnp.abs(x)))


def _rstd(x):
    return lax.rsqrt(jnp.mean(x * x, axis=-1, keepdims=True) + EPS)


def _rms_bwd(du, x, g):
    r = _rstd(x)
    xh = x * r
    dxh = du * g
    dx = r * (dxh - xh * jnp.mean(dxh * xh, axis=-1, keepdims=True))
    return dx, du * xh


def _norm_mm(h, g, w, name, bm=1024, bn=512):
    t, n = h.shape[0], w.shape[1]
    bm, bn = min(bm, t), _blk(n, 3 * bn)

    def body(h_ref, g_ref, w_ref, z_ref, u_ref):
        @pl.when(pl.program_id(1) == 0)
        def _():
            x = h_ref[...]
            u_ref[...] = (x * _rstd(x) * g_ref[...]).astype(BF16)
        z_ref[...] = _nn(u_ref[...], w_ref[...])

    return _pc(body, name, (t // bm, n // bn),
               [pl.BlockSpec((bm, D), lambda i, j: (i, 0)), pl.BlockSpec((1, D), lambda i, j: (0, 0)),
                pl.BlockSpec((D, bn), lambda i, j: (0, j))],
               [pl.BlockSpec((bm, bn), lambda i, j: (i, j)), pl.BlockSpec((bm, D), lambda i, j: (i, 0))],
               [_sds((t, n)), _sds((t, D), BF16)])(h, g, w)


def _mm_tn(a, b, name, bm=1024, bn=1024, bt=4096, col_chips=None):
    t, m = a.shape
    n = b.shape[1]
    bm, bn, bt = _blk(m, bm), (n // col_chips if col_chips else _blk(n, bn + bn // 2)), min(bt, t)
    if (m // bm) * (n // bn) == 1 and bt >= 1024:
        bt //= 4
    nt = t // bt

    def body(a_ref, b_ref, o_ref, acc):
        k = pl.program_id(2)

        @pl.when(k == 0)
        def _():
            acc[...] = jnp.zeros_like(acc)

        acc[...] += _tn(a_ref[...].astype(BF16), b_ref[...].astype(BF16))

        @pl.when(k == nt - 1)
        def _():
            o_ref[...] = acc[...].astype(BF16)

    if col_chips:
        out_spec, out_shape = pl.BlockSpec((None, bm, bn), lambda i, j, k: (j, i, 0)), _sds((col_chips, m, bn), BF16)
    else:
        out_spec, out_shape = pl.BlockSpec((bm, bn), lambda i, j, k: (i, j)), _sds((m, n), BF16)
    return _pc(body, name, (m // bm, n // bn, nt),
               [pl.BlockSpec((bt, bm), lambda i, j, k: (k, i)), pl.BlockSpec((bt, bn), lambda i, j, k: (k, j))],
               out_spec, out_shape, scratch=[pltpu.VMEM((bm, bn), F32)])(a, b)


def _bwd_in(dz, w, h, g, dh, name, bm=1024, bk=1024):
    t, n = dz.shape
    if n > 2 * bk:
        bm, bk = min(bm // 2, t), n
    else:
        bm, bk = min(bm, t), _blk(n, bk + bk // 2)
    nk = n // bk

    def body(dz_ref, w_ref, h_ref, g_ref, dh_ref, o_ref, dg_ref, acc):
        i, k = pl.program_id(0), pl.program_id(1)

        @pl.when(k == 0)
        def _():
            acc[...] = jnp.zeros_like(acc)

        @pl.when((i == 0) & (k == 0))
        def _():
            dg_ref[...] = jnp.zeros_like(dg_ref)

        acc[...] += _nt(dz_ref[...], w_ref[...])

        @pl.when(k == nk - 1)
        def _():
            dx, dgr = _rms_bwd(acc[...], h_ref[...], g_ref[...])
            o_ref[...] = dh_ref[...] + dx
            dg_ref[...] += jnp.sum(dgr, axis=0, keepdims=True)

    return _pc(body, name, (t // bm, nk),
               [pl.BlockSpec((bm, bk), lambda i, k: (i, k)), pl.BlockSpec((D, bk), lambda i, k: (0, k)),
                pl.BlockSpec((bm, D), lambda i, k: (i, 0)), pl.BlockSpec((1, D), lambda i, k: (0, 0)),
                pl.BlockSpec((bm, D), lambda i, k: (i, 0))],
               [pl.BlockSpec((bm, D), lambda i, k: (i, 0)), pl.BlockSpec((1, D), lambda i, k: (0, 0))],
               [_sds((t, D)), _sds((1, D))], scratch=[pltpu.VMEM((bm, D), F32)])(dz, w, h, g, dh)


def _mlp_fwd(h, g, w1s, w2, l, name, bm=1024):
    t = h.shape[0]
    bm = min(bm, t)
    nk = w1s.shape[0]

    def body(h_ref, g_ref, w1_ref, w2_ref, o_ref, a_ref, u_ref, acc):
        k = pl.program_id(1)

        @pl.when(k == 0)
        def _():
            x = h_ref[...]
            u_ref[...] = (x * _rstd(x) * g_ref[...]).astype(BF16)
            acc[...] = jnp.zeros_like(acc)

        a = _nn(u_ref[...], w1_ref[...])
        a_ref[...] = a
        r = jnp.square(jnp.maximum(a, 0.0)).astype(BF16)
        acc[...] += _nn(r, w2_ref[...])

        @pl.when(k == nk - 1)
        def _():
            o_ref[...] = h_ref[...] + acc[...]

    return _pc(body, name, (t // bm, nk),
               [pl.BlockSpec((bm, D), lambda i, k: (i, 0)), pl.BlockSpec((1, D), lambda i, k: (0, 0)),
                pl.BlockSpec((None, None, D, D), lambda i, k: (k, l, 0, 0)), pl.BlockSpec((None, None, D, D), lambda i, k: (k, l, 0, 0))],
               [pl.BlockSpec((bm, D), lambda i, k: (i, 0)), pl.BlockSpec((bm, D), lambda i, k: (i, k)),
                pl.BlockSpec((bm, D), lambda i, k: (i, 0))],
               [_sds((t, D)), _sds((t, nk * D)), _sds((t, D), BF16)],
               scratch=[pltpu.VMEM((bm, D), F32)])(h, g, w1s, w2)


def _mlp_bwd(dh, a, w1s, w2, l, h, g, name, bm=512):
    t = h.shape[0]
    bm = min(bm, t)
    nk = w1s.shape[0]

    def body(dh_ref, a_ref, w1_ref, w2_ref, h_ref, g_ref, o_ref, da_ref, r_ref, dg_ref, acc):
        i, k = pl.program_id(0), pl.program_id(1)

        @pl.when(k == 0)
        def _():
            acc[...] = jnp.zeros_like(acc)

        @pl.when((i == 0) & (k == 0))
        def _():
            dg_ref[...] = jnp.zeros_like(dg_ref)

        ap = jnp.maximum(a_ref[...], 0.0)
        r_ref[...] = jnp.square(ap).astype(BF16)
        dr = _nt(dh_ref[...].astype(BF16), w2_ref[...])
        da = (dr * (2.0 * ap)).astype(BF16)
        da_ref[...] = da
        acc[...] += _nt(da, w1_ref[...])

        @pl.when(k == nk - 1)
        def _():
            dx, dgr = _rms_bwd(acc[...], h_ref[...], g_ref[...])
            o_ref[...] = dh_ref[...] + dx
            dg_ref[...] += jnp.sum(dgr, axis=0, keepdims=True)

    return _pc(body, name, (t // bm, nk),
               [pl.BlockSpec((bm, D), lambda i, k: (i, 0)), pl.BlockSpec((bm, D), lambda i, k: (i, k)),
                pl.BlockSpec((None, None, D, D), lambda i, k: (k, l, 0, 0)), pl.BlockSpec((None, None, D, D), lambda i, k: (k, l, 0, 0)),
                pl.BlockSpec((bm, D), lambda i, k: (i, 0)), pl.BlockSpec((1, D), lambda i, k: (0, 0))],
               [pl.BlockSpec((bm, D), lambda i, k: (i, 0)), pl.BlockSpec((bm, D), lambda i, k: (i, k)),
                pl.BlockSpec((bm, D), lambda i, k: (i, k)), pl.BlockSpec((1, D), lambda i, k: (0, 0))],
               [_sds((t, D)), _sds((t, nk * D), BF16), _sds((t, nk * D), BF16), _sds((1, D))],
               scratch=[pltpu.VMEM((bm, D), F32)])(dh, a, w1s, w2, h, g)


def _rows_of(x):
    return lax.broadcasted_iota(jnp.int32, x.shape, 0)


def _shift_down(x, s):
    if s == 0:
        return x
    return jnp.where(_rows_of(x) >= s, pltpu.roll(x, s, 0), 0.0)


def _shift_up(x, s):
    if s == 0:
        return x
    n = x.shape[0]
    return jnp.where(_rows_of(x) < n - s, pltpu.roll(x, n - s, 0), 0.0)


def _cumsum_rows(x):
    n, s = x.shape[0], 1
    while s < n:
        x = x + _shift_down(x, s)
        s *= 2
    return x


def _rcumsum_rows(x):
    n, s = x.shape[0], 1
    while s < n:
        x = x + _shift_up(x, s)
        s *= 2
    return x


def _silu(x):
    return x * _sigmoid(x)


def _dsilu(x):
    s = _sigmoid(x)
    return s * (1.0 + x * (1.0 - s))


CONV_W = 4


def _conv_pre(u, w):
    y = _shift_down(u, CONV_W - 1) * w[0:1, :]
    for j in range(1, CONV_W):
        y = y + _shift_down(u, CONV_W - 1 - j) * w[j:j + 1, :]
    return y


def _conv_fwd(z0, cw, name):
    t = z0.shape[0]

    def body(u_ref, w_ref, o_ref):
        o_ref[...] = _silu(_conv_pre(u_ref[...], w_ref[...]))

    return _pc(body, name, (2 * 512 // HD,),
               [pl.BlockSpec((t, HD), lambda c: (0, c)), pl.BlockSpec((CONV_W, HD), lambda c: (0, c))],
               pl.BlockSpec((t, HD), lambda c: (0, c)), _sds((t, 1024)))(z0, cw)


def _conv_bwd(z0, cw, dy, name):
    t = z0.shape[0]

    def body(u_ref, w_ref, dy_ref, du_ref, dw_ref):
        u, w = u_ref[...], w_ref[...]
        dpre = dy_ref[...] * _dsilu(_conv_pre(u, w))
        du = _shift_up(dpre, CONV_W - 1) * w[0:1, :]
        for j in range(1, CONV_W):
            du = du + _shift_up(dpre, CONV_W - 1 - j) * w[j:j + 1, :]
        du_ref[...] = du.astype(BF16)
        for j in range(CONV_W):
            dw_ref[j:j + 1, :] = jnp.sum(dpre * _shift_down(u, CONV_W - 1 - j), axis=0, keepdims=True)

    return _pc(body, name, (2 * 512 // HD,),
               [pl.BlockSpec((t, HD), lambda c: (0, c)), pl.BlockSpec((CONV_W, HD), lambda c: (0, c)),
                pl.BlockSpec((t, HD), lambda c: (0, c))],
               [pl.BlockSpec((t, HD), lambda c: (0, c)), pl.BlockSpec((CONV_W, HD), lambda c: (0, c))],
               [_sds((t, 1024), BF16), _sds((CONV_W, 1024))])(z0, cw, dy)


def _mlstm_gates(gate, bias, m_in):
    L = gate.shape[0]
    r = lax.broadcasted_iota(jnp.int32, (L, L), 0)
    c = lax.broadcasted_iota(jnp.int32, (L, L), 1)
    eye, tril = r == c, c <= r
    i_col = gate[:, 0:1] + bias[:, 0:1]
    f_col = gate[:, 1:2] + bias[:, 1:2]
    logf_col = _log_sigmoid(f_col)
    logf_row = jnp.sum(jnp.where(eye, logf_col, 0.0), axis=0, keepdims=True)
    i_row = jnp.sum(jnp.where(eye, i_col, 0.0), axis=0, keepdims=True)
    b_col = jnp.sum(jnp.where(tril, logf_row, 0.0), axis=1, keepdims=True)
    b_row = jnp.sum(jnp.where(r <= c, logf_col, 0.0), axis=0, keepdims=True)
    logd = jnp.where(tril, b_col - b_row + i_row, NEG)
    inter = b_col + m_in
    m_t = jnp.maximum(inter, jnp.max(logd, axis=1, keepdims=True))
    w_t = jnp.exp(inter - m_t)
    dm = jnp.exp(logd - m_t)
    b_last = b_col[L - 1:L, :]
    log_in = b_last - b_col + i_col
    m_new = jnp.maximum(b_last + m_in, jnp.max(log_in, axis=0, keepdims=True))
    w_col = jnp.exp(log_in - m_new)
    decay = jnp.exp(b_last + m_in - m_new)
    return dict(eye=eye, r=r, c=c, f_col=f_col, m_t=m_t, w_t=w_t, dm=dm, m_new=m_new, w_col=w_col, decay=decay)


def _mlstm_fwd(qk, z0, gates, bias, name):
    t = qk.shape[0]
    nc, nh, L = t // CHUNK, 4, CHUNK
    scale = HD ** -0.5

    def body(q_ref, k_ref, v_ref, g_ref, b_ref, h_ref, cs_ref, ns_ref, ms_ref, c_s, n_s, m_s):
        @pl.when(pl.program_id(0) == 0)
        def _():
            c_s[...] = jnp.zeros_like(c_s)
            n_s[...] = jnp.zeros_like(n_s)
            m_s[...] = jnp.zeros_like(m_s)

        for hd in range(nh):
            sl = slice(hd * HD, (hd + 1) * HD)
            cm, nv, m_in = c_s[hd], n_s[hd], m_s[hd]
            for ck in range(cps):
                rows = slice(ck * L, (ck + 1) * L)
                cs_ref[hd, ck] = cm
                ns_ref[hd, ck] = nv
                ms_ref[hd, ck] = jnp.broadcast_to(m_in, (1, HD))
                q, kh, v = q_ref[rows, sl], k_ref[rows, sl] * scale, v_ref[rows, sl]
                G = _mlstm_gates(g_ref[hd, rows, :], b_ref[hd], m_in)
                qb, kb, vb = q.astype(BF16), kh.astype(BF16), v.astype(BF16)
                sc = _nt(qb, kb) * G["dm"]
                num = _nn(sc.astype(BF16), vb) + G["w_t"] * _nn(qb, cm.astype(BF16))
                den = jnp.sum(sc, axis=1, keepdims=True) + G["w_t"] * jnp.sum(q * nv, axis=1, keepdims=True)
                h_ref[rows, sl] = num / jnp.maximum(jnp.abs(den), jnp.exp(-G["m_t"]))
                wk = G["w_col"] * kh
                cm = G["decay"] * cm + _tn(wk.astype(BF16), vb)
                nv = G["decay"] * nv + jnp.sum(wk, axis=0, keepdims=True)
                m_in = G["m_new"]
            c_s[hd], n_s[hd], m_s[hd] = cm, nv, m_in

    cps = REC_CHUNKS
    hspec = lambda blk: pl.BlockSpec((cps * L, 512), lambda j: (j, blk))
    st = lambda r: pl.BlockSpec((nh, cps, r, HD), lambda j: (0, j, 0, 0))
    return _pc(body, name, (nc // cps,),
               [hspec(0), hspec(1), hspec(2), pl.BlockSpec((nh, cps * L, 2), lambda j: (0, j, 0)),
                pl.BlockSpec((nh, 1, 2), lambda j: (0, 0, 0))],
               [hspec(0), st(HD), st(1), st(1)],
               [_sds((t, 512)), _sds((nh, nc, HD, HD)), _sds((nh, nc, 1, HD)), _sds((nh, nc, 1, HD))],
               scratch=[pltpu.VMEM((nh, HD, HD), F32), pltpu.VMEM((nh, 1, HD), F32), pltpu.VMEM((nh, 1, 1), F32)])(qk, qk, z0, gates, bias)


def _mlstm_bwd(qk, z0, gates, bias, cs, ns, ms, dh, name):
    t = qk.shape[0]
    nc, nh, L = t // CHUNK, 4, CHUNK
    scale = HD ** -0.5

    def body(q_ref, k_ref, v_ref, g_ref, b_ref, cs_ref, ns_ref, ms_ref, dh_ref, dqk_ref, dv_ref, dg_ref, dc_s, dn_s):
        @pl.when(pl.program_id(0) == 0)
        def _():
            dc_s[...] = jnp.zeros_like(dc_s)
            dn_s[...] = jnp.zeros_like(dn_s)

        for ck in reversed(range(cps)):
            for hd in range(nh):
                one_head(hd, ck, slice(hd * HD, (hd + 1) * HD), slice(ck * L, (ck + 1) * L), q_ref, k_ref, v_ref, g_ref, b_ref,
                         cs_ref, ns_ref, ms_ref, dh_ref, dqk_ref, dv_ref, dg_ref, dc_s, dn_s)

    def one_head(hd, ck, sl, rows, q_ref, k_ref, v_ref, g_ref, b_ref, cs_ref, ns_ref, ms_ref, dh_ref, dqk_ref, dv_ref, dg_ref,
                 dc_s, dn_s):
        cm, nv, m_in = cs_ref[hd, ck], ns_ref[hd, ck], ms_ref[hd, ck][:, 0:1]
        q, kh, v = q_ref[rows, sl], k_ref[rows, sl] * scale, v_ref[rows, sl]
        G = _mlstm_gates(g_ref[hd, rows, :], b_ref[hd], m_in)
        w_t, dmat, w_col, decay = G["w_t"], G["dm"], G["w_col"], G["decay"]
        qb, kb, vb, cb = q.astype(BF16), kh.astype(BF16), v.astype(BF16), cm.astype(BF16)
        s = _nt(qb, kb)
        sc = s * dmat
        scb = sc.astype(BF16)
        qc = _nn(qb, cb)
        qn = jnp.sum(q * nv, axis=1, keepdims=True)
        num = _nn(scb, vb) + w_t * qc
        den = jnp.sum(sc, axis=1, keepdims=True) + w_t * qn
        e_m = jnp.exp(-G["m_t"])
        dnm = jnp.maximum(jnp.abs(den), e_m)
        dh_ = dh_ref[rows, sl]
        dnum = dh_ / dnm
        dden = jnp.where(jnp.abs(den) > e_m, -jnp.sum(dh_ * num, axis=1, keepdims=True) / (dnm * dnm) * jnp.sign(den), 0.0)
        dnumb = dnum.astype(BF16)
        dsc = _nt(dnumb, vb) + dden
        dv = _tn(scb, dnumb)
        wd = w_t * dnum
        wdb = wd.astype(BF16)
        ds = dsc * dmat
        dsb = ds.astype(BF16)
        dq = _nt(wdb, cb) + (w_t * dden) * nv + _nn(dsb, kb)
        dc_o = _tn(qb, wdb)
        dn_o = jnp.sum(q * (w_t * dden), axis=0, keepdims=True)
        dw = jnp.sum(dnum * qc, axis=1, keepdims=True) + dden * qn
        dkh = _tn(dsb, qb)
        dlogd = ds * s
        db_col = jnp.sum(dlogd, axis=1, keepdims=True) + dw * w_t
        csum = jnp.sum(dlogd, axis=0, keepdims=True)
        dcn, dnn = dc_s[hd], dn_s[hd]
        dcnb = dcn.astype(BF16)
        kdc = _nn(kb, dcnb)
        dws = jnp.sum(kdc * v, axis=1, keepdims=True) + jnp.sum(kh * dnn, axis=1, keepdims=True)
        dv = dv + w_col * kdc
        dkh = dkh + w_col * (_nt(vb, dcnb) + dnn)
        dlin = dws * w_col
        ddecay = jnp.sum(jnp.sum(dcn * cm, axis=1, keepdims=True), axis=0, keepdims=True) + jnp.sum(dnn * nv, axis=1, keepdims=True)
        dlast = ddecay * decay + jnp.sum(dlin, axis=0, keepdims=True)
        row_id = lax.broadcasted_iota(jnp.int32, (L, 1), 0)
        db_col = db_col - dlin + jnp.where(row_id == L - 1, dlast, 0.0)
        eye, r, c = G["eye"], G["r"], G["c"]
        di = dlin + jnp.sum(jnp.where(eye, csum, 0.0), axis=1, keepdims=True)
        db_row = jnp.sum(jnp.where(eye, db_col, 0.0), axis=0, keepdims=True) - csum
        dlogf = jnp.sum(jnp.where(c >= r, db_row, 0.0), axis=1, keepdims=True)
        dg_ref[hd, rows, 0:1] = di
        dg_ref[hd, rows, 1:2] = dlogf * (1.0 - _sigmoid(G["f_col"]))
        dqk_ref[rows, sl] = dq
        dqk_ref[rows, 512 + hd * HD:512 + (hd + 1) * HD] = dkh * scale
        dv_ref[rows, sl] = dv
        dc_s[hd] = decay * dcn + dc_o
        dn_s[hd] = decay * dnn + dn_o

    cps = REC_CHUNKS
    rv = lambda j: nc // cps - 1 - j
    hspec = lambda blk: pl.BlockSpec((cps * L, 512), lambda j: (rv(j), blk))
    st = lambda r: pl.BlockSpec((nh, cps, r, HD), lambda j: (0, rv(j), 0, 0))
    gs = pl.BlockSpec((nh, cps * L, 2), lambda j: (0, rv(j), 0))
    return _pc(body, name, (nc // cps,),
               [hspec(0), hspec(1), hspec(2), gs, pl.BlockSpec((nh, 1, 2), lambda j: (0, 0, 0)),
                st(HD), st(1), st(1), hspec(0)],
               [pl.BlockSpec((cps * L, 1024), lambda j: (rv(j), 0)), hspec(0), gs],
               [_sds((t, 1024)), _sds((t, 512)), _sds((nh, t, 2))],
               scratch=[pltpu.VMEM((nh, HD, HD), F32), pltpu.VMEM((nh, 1, HD), F32)])(qk, qk, z0, gates, bias, cs, ns, ms, dh)


def _hgrn_act(qb_, fb_, ib_, lg):
    lb = _sigmoid(lg[0:1, :] - lg[1:2, :])
    sg = _sigmoid(fb_)
    f = lb + (1.0 - lb) * sg
    return lb, sg, f, _silu(qb_), (1.0 - lb) * (1.0 - sg), _silu(ib_), _cumsum_rows(jnp.log(f))


HG_SUB = 16


def _hgrn_offdiag(q, k, b, r0):
    beta = b[r0 - 1:r0, :]
    e1 = jnp.exp(b[r0:r0 + HG_SUB, :] - beta)
    e2 = jnp.where(_rows_of(b) < r0, jnp.exp(jnp.minimum(beta - b, 0.0)), 0.0)
    return q[r0:r0 + HG_SUB, :] * e1, k * e2, e1, e2


def _hgrn_fwd(z0, lbl, name):
    t = z0.shape[0]
    nc, nh, L = t // CHUNK, 4, CHUNK

    def body(q_ref, f_ref, i_ref, l_ref, o_ref, ss_ref, st_s):
        @pl.when(pl.program_id(0) == 0)
        def _():
            st_s[...] = jnp.zeros_like(st_s)

        for hd in range(nh):
            sl = slice(hd * HD, (hd + 1) * HD)
            st = st_s[hd]
            for ck in range(cps):
                rows = slice(ck * L, (ck + 1) * L)
                ss_ref[hd, ck] = st
                _, _, _, q, k, v, b = _hgrn_act(q_ref[rows, sl], f_ref[rows, sl], i_ref[rows, sl], l_ref[:, sl])
                o = _nt((q * jnp.exp(b)).astype(BF16), st.astype(BF16))
                sub = _rows_of(b) & (HG_SUB - 1)
                o = o + jnp.sum(q * k, axis=1, keepdims=True) * v
                for dl in range(1, HG_SUB):
                    e = jnp.exp(jnp.where(sub >= dl, b - pltpu.roll(b, dl, 0), NEG))
                    a = jnp.sum(q * pltpu.roll(k, dl, 0) * e, axis=1, keepdims=True)
                    o = o + a * pltpu.roll(v, dl, 0)
                o_ref[rows, sl] = o
                vb = v.astype(BF16)
                for i in range(1, L // HG_SUB):
                    r0 = i * HG_SUB
                    qt, kt, _, _ = _hgrn_offdiag(q, k, b, r0)
                    a = _nt(qt.astype(BF16), kt.astype(BF16))
                    o_ref[ck * L + r0:ck * L + r0 + HG_SUB, sl] += _nn(a.astype(BF16), vb)
                bl = b[L - 1:L, :]
                st = st * jnp.exp(bl) + _tn(v.astype(BF16), (k * jnp.exp(bl - b)).astype(BF16))
            st_s[hd] = st

    cps = REC_CHUNKS
    hspec = lambda blk: pl.BlockSpec((cps * L, 512), lambda j: (j, blk))
    return _pc(body, name, (nc // cps,),
               [hspec(4), hspec(5), hspec(6), pl.BlockSpec((2, 512), lambda j: (0, 0))],
               [hspec(0), pl.BlockSpec((nh, cps, HD, HD), lambda j: (0, j, 0, 0))],
               [_sds((t, 512)), _sds((nh, nc, HD, HD))],
               scratch=[pltpu.VMEM((nh, HD, HD), F32)])(z0, z0, z0, lbl)


def _hgrn_bwd(z0, lbl, ss, do, name):
    t = z0.shape[0]
    nc, nh, L = t // CHUNK, 4, CHUNK

    def body(q_ref, f_ref, i_ref, l_ref, ss_ref, do_ref, dq_ref, df_ref, di_ref, dl_ref, dst_s, dlb_s, dq_a, dk_a, dv_a, db_a):
        @pl.when(pl.program_id(0) == 0)
        def _():
            dst_s[...] = jnp.zeros_like(dst_s)
            dlb_s[...] = jnp.zeros_like(dlb_s)

        for ck in reversed(range(cps)):
            for hd in range(nh):
                one_head(hd, ck, slice(hd * HD, (hd + 1) * HD), slice(ck * L, (ck + 1) * L), q_ref, f_ref, i_ref, l_ref, ss_ref, do_ref,
                         dq_ref, df_ref, di_ref, dl_ref, dst_s, dlb_s, dq_a.at[hd], dk_a.at[hd], dv_a.at[hd], db_a.at[hd])

    def one_head(hd, ck, sl, rs, q_ref, f_ref, i_ref, l_ref, ss_ref, do_ref, dq_ref, df_ref, di_ref, dl_ref, dst_s, dlb_s,
                 dq_a, dk_a, dv_a, db_a):
        st = ss_ref[hd, ck]
        qp, fp, ip = q_ref[rs, sl], f_ref[rs, sl], i_ref[rs, sl]
        lb, sg, f, q, k, v, b = _hgrn_act(qp, fp, ip, l_ref[:, sl])
        do_ = do_ref[rs, sl]
        dob, stb = do_.astype(BF16), st.astype(BF16)
        eb = jnp.exp(b)
        qe = q * eb
        dqe = _nn(dob, stb)
        dst_o = _tn(dob, qe.astype(BF16))
        dq = dqe * eb
        db = dqe * qe
        rows = _rows_of(b)
        sub = rows & (HG_SUB - 1)
        p0 = jnp.sum(do_ * v, axis=1, keepdims=True)
        dq = dq + p0 * k
        dk = p0 * q
        dv = jnp.sum(q * k, axis=1, keepdims=True) * do_
        for dl in range(1, HG_SUB):
            up = L - dl
            kd, vd = pltpu.roll(k, dl, 0), pltpu.roll(v, dl, 0)
            e = jnp.exp(jnp.where(sub >= dl, b - pltpu.roll(b, dl, 0), NEG))
            a = jnp.sum(q * kd * e, axis=1, keepdims=True)
            p = jnp.sum(do_ * vd, axis=1, keepdims=True) * e
            dq = dq + p * kd
            dkd = p * q
            dbb = dkd * kd
            dv = dv + pltpu.roll(a * do_, up, 0)
            dk = dk + pltpu.roll(dkd, up, 0)
            db = db + dbb - pltpu.roll(dbb, up, 0)
        dq_a[...], dk_a[...], dv_a[...], db_a[...] = dq, dk, dv, db
        vb = v.astype(BF16)
        for i in range(1, L // HG_SUB):
            r0 = i * HG_SUB
            blk = slice(r0, r0 + HG_SUB)
            qt, kt, e1, e2 = _hgrn_offdiag(q, k, b, r0)
            qtb, ktb, dob_i = qt.astype(BF16), kt.astype(BF16), do_[blk, :].astype(BF16)
            a = _nt(qtb, ktb).astype(BF16)
            da = _nt(dob_i, vb).astype(BF16)
            dv_a[...] += _tn(a, dob_i)
            dqt = _nn(da, ktb)
            dkt = _tn(da, qtb)
            dq_a[blk, :] += dqt * e1
            t1, t2 = dqt * qt, dkt * kt
            db_a[blk, :] += t1
            dk_a[...] += dkt * e2
            db_a[...] -= t2
            db_a[r0 - 1:r0, :] += jnp.sum(t2, axis=0, keepdims=True) - jnp.sum(t1, axis=0, keepdims=True)
        dq, dk, dv, db = dq_a[...], dk_a[...], dv_a[...], db_a[...]
        dstn = dst_s[hd]
        dstnb = dstn.astype(BF16)
        bl = b[L - 1:L, :]
        ebl = jnp.exp(bl)
        kdec_e = jnp.exp(bl - b)
        kdec = k * kdec_e
        dbl = jnp.sum(dstn * st, axis=0, keepdims=True) * ebl
        dv = dv + _nt(kdec.astype(BF16), dstnb)
        dkdec = _nn(v.astype(BF16), dstnb)
        dk = dk + dkdec * kdec_e
        dx = dkdec * kdec
        dbl = dbl + jnp.sum(dx, axis=0, keepdims=True)
        db = db - dx + jnp.where(rows == L - 1, dbl, 0.0)
        dst_s[hd] = dstn * ebl + dst_o
        dg = _rcumsum_rows(db)
        dfk = dg / f - dk
        dq_ref[rs, sl] = (dq * _dsilu(qp)).astype(BF16)
        di_ref[rs, sl] = (dv * _dsilu(ip)).astype(BF16)
        df_ref[rs, sl] = (dfk * (1.0 - lb) * sg * (1.0 - sg)).astype(BF16)
        dlb_s[hd] += jnp.sum(dfk * (1.0 - sg), axis=0, keepdims=True)

        if ck == 0:
            @pl.when(pl.program_id(0) == nc // cps - 1)
            def _():
                dl0 = dlb_s[hd] * lb * (1.0 - lb)
                dl_ref[0:1, sl] = dl0
                dl_ref[1:2, sl] = -dl0

    cps = REC_CHUNKS
    rv = lambda j: nc // cps - 1 - j
    hspec = lambda blk: pl.BlockSpec((cps * L, 512), lambda j: (rv(j), blk))
    return _pc(body, name, (nc // cps,),
               [hspec(4), hspec(5), hspec(6), pl.BlockSpec((2, 512), lambda j: (0, 0)),
                pl.BlockSpec((nh, cps, HD, HD), lambda j: (0, rv(j), 0, 0)), hspec(0)],
               [hspec(0), hspec(0), hspec(0), pl.BlockSpec((2, 512), lambda j: (0, 0))],
               [_sds((t, 512), BF16), _sds((t, 512), BF16), _sds((t, 512), BF16), _sds((2, 512))],
               scratch=[pltpu.VMEM((nh, HD, HD), F32), pltpu.VMEM((nh, 1, HD), F32)] + [pltpu.VMEM((nh, L, HD), F32)] * 4)(z0, z0, z0, lbl, ss, do)


def _post0_fwd(hm, hh, z0, na, nb, w, h0, name, bm=512):
    t = h0.shape[0]
    bm = min(bm, t)

    def body(hm_ref, hh_ref, oa_ref, gb_ref, na_ref, nb_ref, w_ref, h_ref, o_ref, y_ref):
        for hd in range(4):
            sl = slice(hd * HD, (hd + 1) * HD)
            pa = _sigmoid(oa_ref[:, sl]) * hm_ref[:, sl]
            y_ref[:, sl] = (pa * _rstd(pa) * na_ref[:, sl]).astype(BF16)
            xb = hh_ref[:, sl]
            y_ref[:, 512 + hd * HD:512 + (hd + 1) * HD] = (xb * _rstd(xb) * nb_ref[:, sl] * _silu(gb_ref[:, sl])).astype(BF16)
        o_ref[...] = h_ref[...] + _nn(y_ref[...], w_ref[...])

    row = lambda wd, c: pl.BlockSpec((bm, wd), lambda i: (i, c))
    vec = lambda wd: pl.BlockSpec((1, wd), lambda i: (0, 0))
    return _pc(body, name, (t // bm,),
               [row(512, 0), row(512, 0), row(512, 3), row(512, 7), vec(512), vec(512),
                pl.BlockSpec((D, D), lambda i: (0, 0)), row(D, 0)],
               [row(D, 0), row(D, 0)], [_sds((t, D)), _sds((t, D), BF16)])(hm, hh, z0, z0, na, nb, w, h0)


def _post0_bwd(dh1, w, hm, hh, z0, na, nb, name, bm=512):
    t = dh1.shape[0]
    bm = min(bm, t)

    def body(dh_ref, w_ref, hm_ref, hh_ref, oa_ref, gb_ref, na_ref, nb_ref, dhm_ref, dhh_ref, doa_ref, dgb_ref, dna_ref, dnb_ref):
        @pl.when(pl.program_id(0) == 0)
        def _():
            dna_ref[...] = jnp.zeros_like(dna_ref)
            dnb_ref[...] = jnp.zeros_like(dnb_ref)

        dy = _nt(dh_ref[...].astype(BF16), w_ref[...])
        for hd in range(4):
            sl = slice(hd * HD, (hd + 1) * HD)
            hm_, oa = hm_ref[:, sl], oa_ref[:, sl]
            sg = _sigmoid(oa)
            dpa, dgr = _rms_bwd(dy[:, sl], sg * hm_, na_ref[:, sl])
            dna_ref[:, sl] += jnp.sum(dgr, axis=0, keepdims=True)
            doa_ref[:, sl] = (dpa * hm_ * sg * (1.0 - sg)).astype(BF16)
            dhm_ref[:, sl] = dpa * sg
            xb, gb, nbv = hh_ref[:, sl], gb_ref[:, sl], nb_ref[:, sl]
            dyb = dy[:, 512 + hd * HD:512 + (hd + 1) * HD]
            dgb_ref[:, sl] = (dyb * (xb * _rstd(xb) * nbv) * _dsilu(gb)).astype(BF16)
            dxb, dgr2 = _rms_bwd(dyb * _silu(gb), xb, nbv)
            dnb_ref[:, sl] += jnp.sum(dgr2, axis=0, keepdims=True)
            dhh_ref[:, sl] = dxb

    row = lambda wd, c: pl.BlockSpec((bm, wd), lambda i: (i, c))
    vec = lambda wd: pl.BlockSpec((1, wd), lambda i: (0, 0))
    return _pc(body, name, (t // bm,),
               [row(D, 0), pl.BlockSpec((D, D), lambda i: (0, 0)), row(512, 0), row(512, 0), row(512, 3), row(512, 7),
                vec(512), vec(512)],
               [row(512, 0), row(512, 0), row(512, 0), row(512, 0), vec(512), vec(512)],
               [_sds((t, 512)), _sds((t, 512)), _sds((t, 512), BF16), _sds((t, 512), BF16), _sds((1, 512)), _sds((1, 512))],
               )(dh1, w, hm, hh, z0, z0, na, nb)


def _memkv_fwd(mem, g, wkv_s, name):
    m = mem.shape[0]

    def body(x_ref, g_ref, w_ref, kv_ref, mn_ref):
        x = x_ref[...]
        mn = (x * _rstd(x) * g_ref[...]).astype(BF16)
        mn_ref[...] = mn
        kv_ref[...] = _nn(mn, w_ref[...])

    return _pc(body, name, (4,),
               [pl.BlockSpec((m, D), lambda k: (0, 0)), pl.BlockSpec((1, D), lambda k: (0, 0)),
                pl.BlockSpec((None, D, 512), lambda k: (k, 0, 0))],
               [pl.BlockSpec((m, 512), lambda k: (0, k)), pl.BlockSpec((m, D), lambda k: (0, 0))],
               [_sds((m, 2048)), _sds((m, D), BF16)])(mem, g, wkv_s)


def _memkv_bwd(dkv, wkv_s, mem, g, name):
    m = mem.shape[0]

    def body(d_ref, w_ref, x_ref, g_ref, dg_ref, acc):
        k = pl.program_id(0)

        @pl.when(k == 0)
        def _():
            acc[...] = jnp.zeros_like(acc)

        acc[...] += _nt(d_ref[...].astype(BF16), w_ref[...])

        @pl.when(k == 3)
        def _():
            _, dgr = _rms_bwd(acc[...], x_ref[...], g_ref[...])
            dg_ref[...] = jnp.sum(dgr, axis=0, keepdims=True)

    return _pc(body, name, (4,),
               [pl.BlockSpec((m, 512), lambda k: (0, k)), pl.BlockSpec((None, D, 512), lambda k: (k, 0, 0)),
                pl.BlockSpec((m, D), lambda k: (0, 0)), pl.BlockSpec((1, D), lambda k: (0, 0))],
               pl.BlockSpec((1, D), lambda k: (0, 0)), _sds((1, D)), scratch=[pltpu.VMEM((m, D), F32)])(dkv, wkv_s, mem, g)


def _xattn_probs(qh, kh):
    s = _nt(qh, kh) * (XD ** -0.5)
    p = jnp.exp(s - jnp.max(s, axis=1, keepdims=True))
    return p / jnp.sum(p, axis=1, keepdims=True)


def _xattn_fwd(q, kv, wo, h1, name, bm=512):
    t, m = q.shape[0], kv.shape[0]
    bm = min(bm, t)

    def body(q_ref, k_ref, v_ref, w_ref, h_ref, out_ref, o_ref):
        for hd in range(D // XD):
            sl = slice(hd * XD, (hd + 1) * XD)
            p = _xattn_probs(q_ref[:, sl].astype(BF16), k_ref[:, sl].astype(BF16))
            o_ref[:, sl] = _nn(p.astype(BF16), v_ref[:, sl].astype(BF16)).astype(BF16)
        out_ref[...] = h_ref[...] + _nn(o_ref[...], w_ref[...])

    row = pl.BlockSpec((bm, D), lambda i: (i, 0))
    return _pc(body, name, (t // bm,),
               [row, pl.BlockSpec((m, D), lambda i: (0, 0)), pl.BlockSpec((m, D), lambda i: (0, 1)),
                pl.BlockSpec((D, D), lambda i: (0, 0)), row],
               [row, row], [_sds((t, D)), _sds((t, D), BF16)])(q, kv, kv, wo, h1)


def _xattn_bwd(dh2, q, kv, wo, name, bm=512):
    t, m = q.shape[0], kv.shape[0]
    bm = min(bm, t)

    def body(dh_ref, q_ref, k_ref, v_ref, w_ref, dq_ref, dkv_ref):
        @pl.when(pl.program_id(0) == 0)
        def _():
            dkv_ref[...] = jnp.zeros_like(dkv_ref)

        d_o = _nt(dh_ref[...].astype(BF16), w_ref[...])
        for hd in range(D // XD):
            sl = slice(hd * XD, (hd + 1) * XD)
            qh, kh, vh = q_ref[:, sl].astype(BF16), k_ref[:, sl].astype(BF16), v_ref[:, sl].astype(BF16)
            p = _xattn_probs(qh, kh)
            dob = d_o[:, sl].astype(BF16)
            dp = _nt(dob, vh)
            dkv_ref[:, D + hd * XD:D + (hd + 1) * XD] += _tn(p.astype(BF16), dob)
            ds = (p * (dp - jnp.sum(dp * p, axis=1, keepdims=True)) * (XD ** -0.5)).astype(BF16)
            dq_ref[:, sl] = _nn(ds, kh).astype(BF16)
            dkv_ref[:, sl] += _tn(ds, qh)

    row = pl.BlockSpec((bm, D), lambda i: (i, 0))
    return _pc(body, name, (t // bm,),
               [row, row, pl.BlockSpec((m, D), lambda i: (0, 0)), pl.BlockSpec((m, D), lambda i: (0, 1)),
                pl.BlockSpec((D, D), lambda i: (0, 0))],
               [row, pl.BlockSpec((m, 2 * D), lambda i: (0, 0))],
               [_sds((t, D), BF16), _sds((m, 2 * D))])(dh2, q, kv, kv, wo)


NH1 = 8
FOX_BM = 512
FOX_BQ = 512
FOX_BK = 512
FOX_HEADS_PER_STEP = 4


def _foxprep_fwd(z1, qg, kg, fbp, name):
    t = z1.shape[0]
    bm = min(FOX_BM, t)

    def body(q_ref, k_ref, v_ref, f_ref, qg_ref, kg_ref, fb_ref, qn_ref, kn_ref, vb_ref, c_ref, carry):
        @pl.when(pl.program_id(0) == 0)
        def _():
            carry[...] = jnp.zeros_like(carry)

        for hd in range(NH1):
            sl = slice(hd * HD, (hd + 1) * HD)
            x = q_ref[:, sl]
            qn_ref[:, sl] = (x * _rstd(x) * qg_ref[...] * FOX_QSCALE).astype(BF16)
            x = k_ref[:, sl]
            kn_ref[:, sl] = (x * _rstd(x) * kg_ref[...]).astype(BF16)
        vb_ref[...] = v_ref[...].astype(BF16)
        c = carry[...] + _cumsum_rows(_log_sigmoid(f_ref[...] + fb_ref[...]))
        c_ref[...] = c
        carry[...] = c[bm - 1:bm, :]

    row = lambda c: pl.BlockSpec((bm, D), lambda i: (i, c))
    lane = pl.BlockSpec((bm, HD), lambda i: (i, 4 * D // HD))
    vec = pl.BlockSpec((1, HD), lambda i: (0, 0))
    return _pc(body, name, (t // bm,), [row(0), row(1), row(2), lane, vec, vec, vec],
               [row(0), row(0), row(0), pl.BlockSpec((bm, HD), lambda i: (i, 0))],
               [_sds((t, D), BF16), _sds((t, D), BF16), _sds((t, D), BF16), _sds((t, HD))],
               scratch=[pltpu.VMEM((1, HD), F32)])(z1, z1, z1, z1, qg, kg, fbp)


def _foxprep_bwd(dqn, dkn, dv, dgate, z1, qg, kg, fbp, dc, name):
    t = z1.shape[0]
    bm = min(FOX_BM, t)
    nb = t // bm

    def body(dqn_ref, dkn_ref, dv_ref, dgt_ref, q_ref, k_ref, f_ref, qg_ref, kg_ref, fb_ref, dc_ref,
             dz_ref, dqg_ref, dkg_ref, dfb_ref, carry):
        @pl.when(pl.program_id(0) == 0)
        def _():
            carry[...] = jnp.zeros_like(carry)
            dqg_ref[...] = jnp.zeros_like(dqg_ref)
            dkg_ref[...] = jnp.zeros_like(dkg_ref)
            dfb_ref[...] = jnp.zeros_like(dfb_ref)

        for hd in range(NH1):
            sl = slice(hd * HD, (hd + 1) * HD)
            dx, dgr = _rms_bwd(dqn_ref[:, sl] * (HD ** -0.5), q_ref[:, sl], qg_ref[...])
            dz_ref[:, sl] = dx.astype(BF16)
            dqg_ref[...] += jnp.sum(dgr, axis=0, keepdims=True)
            dx, dgr = _rms_bwd(dkn_ref[:, sl], k_ref[:, sl], kg_ref[...])
            dz_ref[:, D + hd * HD:D + (hd + 1) * HD] = dx.astype(BF16)
            dkg_ref[...] += jnp.sum(dgr, axis=0, keepdims=True)
        dz_ref[:, 2 * D:3 * D] = dv_ref[...].astype(BF16)
        dz_ref[:, 3 * D:4 * D] = dgt_ref[...]
        dc_ = dc_ref[...]
        dlogf = _rcumsum_rows(dc_) + carry[...]
        carry[...] += jnp.sum(dc_, axis=0, keepdims=True)
        lanes = lax.broadcasted_iota(jnp.int32, dc_.shape, 1)
        df = jnp.where(lanes < NH1, dlogf * (1.0 - _sigmoid(f_ref[...] + fb_ref[...])), 0.0)
        dz_ref[:, GATE0:GATE0 + HD] = df.astype(BF16)
        dfb_ref[...] += jnp.sum(df, axis=0, keepdims=True)

    rv = lambda i: nb - 1 - i
    row = lambda c: pl.BlockSpec((bm, D), lambda i: (rv(i), c))
    lane = lambda c: pl.BlockSpec((bm, HD), lambda i: (rv(i), c))
    vec = pl.BlockSpec((1, HD), lambda i: (0, 0))
    return _pc(body, name, (nb,), [row(0), row(0), row(0), row(0), row(0), row(1), lane(4 * D // HD), vec, vec, vec, lane(0)],
               [pl.BlockSpec((bm, ZW), lambda i: (rv(i), 0)), vec, vec, vec],
               [_sds((t, ZW), BF16), _sds((1, HD)), _sds((1, HD)), _sds((1, HD))],
               scratch=[pltpu.VMEM((1, HD), F32)])(dqn, dkn, dv, dgate, z1, z1, z1, qg, kg, fbp, dc)


LOG2E = 1.4426950408889634
FOX_QSCALE = HD ** -0.5 * LOG2E


def _fox_steps(t, bq, bk, k_major):
    nq, nk = t // bq, t // bk
    pairs = [(i, j) for i in range(nq) for j in range(nk) if j * bk < (i + 1) * bq]
    if k_major:
        pairs.sort(key=lambda p: (p[1], p[0]))
    outer = [p[1] if k_major else p[0] for p in pairs]
    n = len(pairs)
    flags = [(n_ == 0 or outer[n_] != outer[n_ - 1]) + 2 * (n_ == n - 1 or outer[n_] != outer[n_ + 1])
             + 4 * (not (j + 1) * bk <= i * bq + 1) for n_, (i, j) in enumerate(pairs)]
    as_i32 = lambda v: jnp.asarray(v, jnp.int32)
    return as_i32([p[0] for p in pairs]), as_i32([p[1] for p in pairs]), as_i32(flags)


def _fox_step_info(qi_ref, kj_ref, fl_ref):
    s = pl.program_id(1)
    fl = fl_ref[s]
    return qi_ref[s], kj_ref[s], (fl & 1) != 0, (fl & 2) != 0, (fl & 4) != 0


def _fox_call(body, name, tables, in_specs, out_specs, out_shape, scratch):
    grid_spec = pltpu.PrefetchScalarGridSpec(num_scalar_prefetch=3, grid=(NH1 // FOX_HEADS_PER_STEP, tables[0].shape[0]),
                                             in_specs=in_specs, out_specs=out_specs, scratch_shapes=scratch)
    return pl.pallas_call(body, name=name, grid_spec=grid_spec, out_shape=out_shape,
                          compiler_params=pltpu.CompilerParams(dimension_semantics=("arbitrary", "arbitrary"),
                                                               vmem_limit_bytes=VMEM_LIMIT_V7X))


def _fox_lane_tiles(x):
    return [x[:, c0:c0 + HD] for c0 in range(0, x.shape[1], HD)]


def _fox_masked_scores(q, k, ck, i, j, bq, bk, masked):
    s = _nt(q, k) - ck
    if masked:
        rows = i * bq + lax.broadcasted_iota(jnp.int32, s.shape, 0)
        cols = j * bk + lax.broadcasted_iota(jnp.int32, s.shape, 1)
        s = jnp.where(cols <= rows, s, NEG)
    return s


def _fox_specs(bq, bk, G):
    qspec = pl.BlockSpec((bq, G * HD), lambda h, s, qi, kj, fl: (qi[s], h))
    kspec = pl.BlockSpec((bk, G * HD), lambda h, s, qi, kj, fl: (kj[s], h))
    cspec = pl.BlockSpec((G, 1, bk), lambda h, s, qi, kj, fl: (h, 0, kj[s]))
    colspec = pl.BlockSpec((G, bq, 1), lambda h, s, qi, kj, fl: (h, qi[s], 0))
    return qspec, kspec, cspec, colspec


def _fox_rowmax(qn, kn, crow, name):
    t = qn.shape[0]
    bq, bk, G = min(FOX_BQ, t), min(2 * FOX_BK, t), FOX_HEADS_PER_STEP
    tables = _fox_steps(t, bq, bk, k_major=False)

    def body(qi_ref, kj_ref, fl_ref, q_ref, k_ref, ck_ref, m_ref, *mp):
        i, j, first, last, diag = _fox_step_info(qi_ref, kj_ref, fl_ref)

        @pl.when(first)
        def _():
            for g in range(G):
                mp[g][...] = jnp.full_like(mp[g], NEG)

        def step(masked):
            for g in range(G):
                sl = slice(g * HD, (g + 1) * HD)
                s = _fox_masked_scores(q_ref[:, sl], k_ref[:, sl], ck_ref[g], i, j, bq, bk, masked)
                m = mp[g][...]
                for tile in _fox_lane_tiles(s):
                    m = jnp.maximum(m, tile)
                mp[g][...] = m

        pl.when(jnp.logical_not(diag))(lambda: step(False))
        pl.when(diag)(lambda: step(True))

        @pl.when(last)
        def _():
            for g in range(G):
                m_ref[g] = jnp.max(mp[g][...], axis=1, keepdims=True)

    qspec, kspec, cspec, colspec = _fox_specs(bq, bk, G)
    return _fox_call(body, name, tables, [qspec, kspec, cspec], colspec, _sds((NH1, t, 1)),
                     [pltpu.VMEM((bq, HD), F32)] * G)(*tables, qn, kn, crow)


def _fox_fwd(qn, kn, vb, crow, m, name):
    t = qn.shape[0]
    bq, bk, G = min(FOX_BQ, t), min(FOX_BK, t), FOX_HEADS_PER_STEP
    tables = _fox_steps(t, bq, bk, k_major=False)

    def body(qi_ref, kj_ref, fl_ref, q_ref, k_ref, v_ref, ck_ref, m_ref, o_ref, lse_ref, *scr):
        i, j, first, last, diag = _fox_step_info(qi_ref, kj_ref, fl_ref)
        lp, acc = scr[:G], scr[G:]

        @pl.when(first)
        def _():
            for g in range(G):
                lp[g][...] = jnp.zeros_like(lp[g])
                acc[g][...] = jnp.zeros_like(acc[g])

        def step(masked):
            for g in range(G):
                sl = slice(g * HD, (g + 1) * HD)
                s = _fox_masked_scores(q_ref[:, sl], k_ref[:, sl], ck_ref[g], i, j, bq, bk, masked)
                p = jnp.exp2(s - m_ref[g])
                l = lp[g][...]
                for tile in _fox_lane_tiles(p):
                    l = l + tile
                lp[g][...] = l
                acc[g][...] += _nn(p.astype(BF16), v_ref[:, sl])

        pl.when(jnp.logical_not(diag))(lambda: step(False))
        pl.when(diag)(lambda: step(True))

        @pl.when(last)
        def _():
            for g in range(G):
                l = jnp.sum(lp[g][...], axis=1, keepdims=True)
                o_ref[:, g * HD:(g + 1) * HD] = acc[g][...] / l
                lse_ref[g] = m_ref[g] + jnp.log2(l)

    qspec, kspec, cspec, colspec = _fox_specs(bq, bk, G)
    return _fox_call(body, name, tables, [qspec, kspec, kspec, cspec, colspec], [qspec, colspec],
                     [_sds((t, D)), _sds((NH1, t, 1))], [pltpu.VMEM((bq, HD), F32)] * (2 * G))(*tables, qn, kn, vb, crow, m)


def _fox_bwd(qn, kn, vb, crow, lse, delta, do, name):
    t = qn.shape[0]
    bq, bk, G = min(FOX_BQ, t), min(FOX_BK, t), FOX_HEADS_PER_STEP
    tables = _fox_steps(t, bq, bk, k_major=True)

    def body(qi_ref, kj_ref, fl_ref, q_ref, k_ref, v_ref, ck_ref, lse_ref, dl_ref, do_ref, dq_ref, dk_ref, dv_ref, dc_ref, dcq_ref,
             dk_s, dv_s, dc_s):
        i, j, first, last, diag = _fox_step_info(qi_ref, kj_ref, fl_ref)

        @pl.when(first)
        def _():
            dk_s[...] = jnp.zeros_like(dk_s)
            dv_s[...] = jnp.zeros_like(dv_s)
            dc_s[...] = jnp.zeros_like(dc_s)

        @pl.when(pl.program_id(1) == 0)
        def _():
            dq_ref[...] = jnp.zeros_like(dq_ref)
            dcq_ref[...] = jnp.zeros_like(dcq_ref)

        def step(masked):
            rows = pl.ds(pl.multiple_of(i * bq, bq), bq)
            for g in range(G):
                sl = slice(g * HD, (g + 1) * HD)
                q, k = q_ref[:, sl], k_ref[:, sl]
                s = _fox_masked_scores(q, k, ck_ref[g], i, j, bq, bk, masked)
                p = jnp.exp2(s - lse_ref[g])
                dob = do_ref[:, sl]
                dv_s[:, sl] += _tn(p.astype(BF16), dob)
                ds = p * (_nt(dob, v_ref[:, sl]) - dl_ref[g])
                dsb = ds.astype(BF16)
                dq_ref[rows, sl] += _nn(dsb, k)
                dk_s[:, sl] += _tn(dsb, q)
                dc_s[g] -= jnp.sum(ds, axis=0, keepdims=True)
                part_sum = dcq_ref[g, rows, :]
                for tile in _fox_lane_tiles(ds):
                    part_sum = part_sum + tile
                dcq_ref[g, rows, :] = part_sum

        pl.when(jnp.logical_not(diag))(lambda: step(False))
        pl.when(diag)(lambda: step(True))

        @pl.when(last)
        def _():
            dk_ref[...] = dk_s[...] * (1.0 / LOG2E)
            dv_ref[...] = dv_s[...]
            dc_ref[...] = dc_s[...]

    qspec, kspec, cspec, colspec = _fox_specs(bq, bk, G)
    return _fox_call(
        body, name, tables, [qspec, kspec, kspec, cspec, colspec, colspec, qspec],
        [pl.BlockSpec((t, G * HD), lambda h, s, qi, kj, fl: (0, h)), kspec, kspec, cspec,
         pl.BlockSpec((G, t, HD), lambda h, s, qi, kj, fl: (h, 0, 0))],
        [_sds((t, D)), _sds((t, D)), _sds((t, D)), _sds((NH1, 1, t)), _sds((NH1, t, HD))],
        [pltpu.VMEM((bk, G * HD), F32), pltpu.VMEM((bk, G * HD), F32), pltpu.VMEM((G, 1, bk), F32)],
    )(*tables, qn, kn, vb, crow, lse, delta, do)


def _post1_fwd(o, z1, w, h3, name, bm=512):
    t = o.shape[0]
    bm = min(bm, t)

    def body(o_ref, g_ref, w_ref, h_ref, out_ref, og_ref):
        og_ref[...] = (o_ref[...] * _sigmoid(g_ref[...])).astype(BF16)
        out_ref[...] = h_ref[...] + _nn(og_ref[...], w_ref[...])

    row = lambda c: pl.BlockSpec((bm, D), lambda i: (i, c))
    return _pc(body, name, (t // bm,), [row(0), row(3), pl.BlockSpec((D, D), lambda i: (0, 0)), row(0)],
               [row(0), row(0)], [_sds((t, D)), _sds((t, D), BF16)])(o, z1, w, h3)


def _post1_bwd(dh4, w, o, z1, name, bm=512):
    t = o.shape[0]
    bm = min(bm, t)

    def body(dh_ref, w_ref, o_ref, g_ref, do_ref, dg_ref, dl_ref):
        d_og = _nt(dh_ref[...].astype(BF16), w_ref[...])
        o_, sg = o_ref[...], _sigmoid(g_ref[...])
        dob = (d_og * sg).astype(BF16)
        do_ref[...] = dob
        dg_ref[...] = (d_og * o_ * sg * (1.0 - sg)).astype(BF16)
        prod = dob.astype(F32) * o_
        for hd in range(NH1):
            dl_ref[hd] = jnp.sum(prod[:, hd * HD:(hd + 1) * HD], axis=1, keepdims=True)

    row = lambda c: pl.BlockSpec((bm, D), lambda i: (i, c))
    return _pc(body, name, (t // bm,), [row(0), pl.BlockSpec((D, D), lambda i: (0, 0)), row(0), row(3)],
               [row(0), row(0), pl.BlockSpec((NH1, bm, 1), lambda i: (0, i, 0))],
               [_sds((t, D), BF16), _sds((t, D), BF16), _sds((NH1, t, 1))])(dh4, w, o, z1)


def _final(h, g, tgt, name, bm=512):
    t = h.shape[0]
    bm = min(bm, t)

    def body(h_ref, g_ref, t_ref, l_ref, dh_ref, dg_ref):
        @pl.when(pl.program_id(0) == 0)
        def _():
            l_ref[...] = jnp.zeros_like(l_ref)
            dg_ref[...] = jnp.zeros_like(dg_ref)

        x, gv = h_ref[...], g_ref[...]
        r = _rstd(x)
        xh = x * r
        e = xh * gv - t_ref[...]
        l_ref[...] += 0.5 * jnp.sum(jnp.mean(e * e, axis=1, keepdims=True), axis=0, keepdims=True)
        dy = e * (1.0 / D)
        dg_ref[...] += jnp.sum(dy * xh, axis=0, keepdims=True)
        dxh = dy * gv
        dh_ref[...] = r * (dxh - xh * jnp.mean(dxh * xh, axis=1, keepdims=True))

    row = pl.BlockSpec((bm, D), lambda i: (i, 0))
    vec = pl.BlockSpec((1, D), lambda i: (0, 0))
    return _pc(body, name, (t // bm,), [row, vec, row], [pl.BlockSpec((1, HD), lambda i: (0, 0)), row, vec],
               [_sds((1, HD)), _sds((t, D)), _sds((1, D))])(h, g, tgt)


def _adam(w, g, m, v, name):
    r, c = w.shape
    br = min(r, 256)

    def body(w_ref, g_ref, m_ref, v_ref, d_ref, mo_ref, vo_ref):
        gv = g_ref[...]
        mn = ADAM_B1 * m_ref[...] + (1.0 - ADAM_B1) * gv
        vn = ADAM_B2 * v_ref[...] + (1.0 - ADAM_B2) * jnp.square(gv)
        m_hat = mn / (1.0 - ADAM_B1 ** ADAM_STEP)
        v_hat = vn / (1.0 - ADAM_B2 ** ADAM_STEP)
        d_ref[...] = -ADAM_LR * (m_hat / (jnp.sqrt(v_hat) + ADAM_EPS) + ADAM_WD * w_ref[...])
        mo_ref[...] = mn
        vo_ref[...] = vn

    blk = pl.BlockSpec((br, c), lambda i: (i, 0))
    return _pc(body, name, (r // br,), [blk] * 4, [blk] * 3, [_sds((r, c))] * 3)(w, g, m, v)


ZW = 4224
GATE0 = 4096


def _pack_w_in0(w):
    return jnp.concatenate([w[:, :2048], w[:, 2056:], w[:, 2048:2056], jnp.zeros((w.shape[0], ZW - 4104), w.dtype)], axis=1)


def _unpack_w_in0(g):
    return jnp.concatenate([g[:, :2048], g[:, GATE0:GATE0 + 8], g[:, 2048:GATE0]], axis=1)


def _pack_w_in1(w):
    return jnp.concatenate([w, jnp.zeros((w.shape[0], ZW - 4104), w.dtype)], axis=1)


def _unpack_w_in1(g):
    return g[:, :4104]


def _local_step(x, mem, tgt, W, S, late_weights=None, grads_hook=None):
    t = x.shape[0]
    row = lambda v: v.reshape(1, -1)
    G = {}

    z0, u0 = _norm_mm(x, S["norm_mix_g"][0:1], W["w_in0"], "in0_fwd")
    qk = _conv_fwd(z0, S["conv_w"], "conv_fwd")
    g8 = z0[:, GATE0:GATE0 + 8]
    gates3 = jnp.stack([g8[:, :4].T, g8[:, 4:].T], axis=-1)
    gb = S["gate_b"]
    bias3 = jnp.stack([gb[0, :4], gb[0, 4:]], axis=-1)[:, None, :]
    hm, cs, ns, ms = _mlstm_fwd(qk, z0, gates3, bias3, "mlstm_fwd")
    hh, ss = _hgrn_fwd(z0, S["lb_logits"], "hgrn_fwd")
    if late_weights is not None:
        W = {**W, **late_weights(hh)}
    kv, mn = _memkv_fwd(mem, row(S["mem_norm_g"]), W["wkv_s"], "memkv_fwd")
    h1, y0 = _post0_fwd(hm, hh, z0, S["mlstm_norm_g"], S["hgrn_norm_g"], W["w_out0"], x, "post0_fwd")

    def xattn_mlp_fwd(h, l):
        q, ux = _norm_mm(h, S["norm_xattn_g"][l:l + 1], W["wq"][l], f"xq{l}_fwd")
        h2, ox = _xattn_fwd(q, kv, W["wo"][l], h, f"xattn{l}_fwd")
        h3, a, um = _mlp_fwd(h2, S["norm_mlp_g"][l:l + 1], W["w1s"], W["w2"], l, f"mlp{l}_fwd")
        return h3, (h, q, ux, ox, h2, a, um)

    h3, sv0 = xattn_mlp_fwd(h1, 0)
    z1, u1 = _norm_mm(h3, S["norm_mix_g"][1:2], W["w_in1"], "in1_fwd")
    fbp = jnp.pad(S["c_fgate_b"], ((0, 0), (0, HD - NH1)))
    qn, kn, vb, c = _foxprep_fwd(z1, S["c_qnorm_g"], S["c_knorm_g"], fbp, "foxprep_fwd")
    crow = (c[:, :NH1] * LOG2E).T[:, None, :]
    o1, lse = _fox_fwd(qn, kn, vb, crow, _fox_rowmax(qn, kn, crow, "fox_rowmax"), "fox_fwd")
    h4, og = _post1_fwd(o1, z1, W["w_out1"], h3, "post1_fwd")
    h6, sv1 = xattn_mlp_fwd(h4, 1)
    lossp, dh, G["final_norm_g"] = _final(h6, row(S["final_norm_g"]), tgt, "final")

    grads_ready, grads_next = grads_hook if grads_hook is not None else ((lambda stage, grads: 0.0), (lambda stage, after: 0.0))
    dkv = None
    dgx, dgm, dwq, dwo, dw1, dw2 = [None, None], [None, None], [None, None], [None, None], [None, None], [None, None]

    def xattn_mlp_bwd(dh, l, sv, tok=0.0):
        nonlocal dkv
        h, q, ux, ox, h2, a, um = sv
        dh2, da, r, dgm[l] = _mlp_bwd(dh, a, W["w1s"], W["w2"], l, h2, S["norm_mlp_g"][l:l + 1] + tok, f"mlp{l}_bwd")
        dw1[l] = _mm_tn(um, da, f"mlp{l}_dw1", col_chips=NCHIP)
        dw2[l] = _mm_tn(r, dh, f"mlp{l}_dw2")
        dq, dkv_l = _xattn_bwd(dh2, q, kv, W["wo"][l], f"xattn{l}_bwd")
        dkv = dkv_l if dkv is None else dkv + dkv_l
        dwo[l] = _mm_tn(ox, dh2, f"xattn{l}_dwo")
        dwq[l] = _mm_tn(ux, dq, f"xattn{l}_dwq")
        dh1, dgx[l] = _bwd_in(dq, W["wq"][l], h, S["norm_xattn_g"][l:l + 1], dh2, f"xq{l}_bwd")
        return dh1

    dh4 = xattn_mlp_bwd(dh, 1, sv1)
    do, dgate, delta = _post1_bwd(dh4, W["w_out1"], o1, z1, "post1_bwd")
    G["w_out1"] = _mm_tn(og, dh4, "post1_dw")
    dqn, dkn, dv1, dcrow, dcq = _fox_bwd(qn, kn, vb, crow, lse, delta, do, "fox_bwd")
    dc = jnp.pad((dcrow[:, 0, :] + jnp.sum(dcq, axis=-1)).T, ((0, 0), (0, HD - NH1)))
    dz1, G["c_qnorm_g"], G["c_knorm_g"], dfb = _foxprep_bwd(
        dqn, dkn, dv1, dgate, z1, S["c_qnorm_g"], S["c_knorm_g"], fbp, dc, "foxprep_bwd")
    G["c_fgate_b"] = dfb[:, :NH1]
    G["w_in1"] = _mm_tn(u1, dz1, "in1_dw")
    tok = grads_ready("layer1", dict(w_out=G["w_out1"], w_in=G["w_in1"], wq=dwq[1], wo=dwo[1], w1=dw1[1], w2=dw2[1]))
    dh3, dgmix1 = _bwd_in(dz1, W["w_in1"], h3, S["norm_mix_g"][1:2] + tok, dh4, "in1_bwd")
    dh1 = xattn_mlp_bwd(dh3, 0, sv0, grads_next("layer1", dh3))

    G["wkv"] = _mm_tn(mn, dkv, "memkv_dw", col_chips=NCHIP)
    G["mem_norm_g"] = _memkv_bwd(dkv, W["wkv_s"], mem, row(S["mem_norm_g"]), "memkv_bwd")
    G["w_out0"] = _mm_tn(y0, dh1, "post0_dw")
    tok = grads_ready("layer0", dict(wq=dwq[0], wo=dwo[0], w1=dw1[0], w2=dw2[0], wkv=G["wkv"], w_out=G["w_out0"]))
    dhm, dhh, doa, dgb, G["mlstm_norm_g"], G["hgrn_norm_g"] = _post0_bwd(
        dh1, W["w_out0"], hm, hh, z0, S["mlstm_norm_g"] + tok, S["hgrn_norm_g"], "post0_bwd")
    dqka, dva, dgates3 = _mlstm_bwd(qk, z0, gates3, bias3 + grads_next("layer0", dhm), cs, ns, ms, dhm, "mlstm_bwd")
    dqb, dfb0, dib, G["lb_logits"] = _hgrn_bwd(z0, S["lb_logits"], ss, dhh, "hgrn_bwd")
    duc, G["conv_w"] = _conv_bwd(z0, S["conv_w"], dqka, "conv_bwd")
    dg8 = jnp.concatenate([dgates3[:, :, 0].T, dgates3[:, :, 1].T], axis=1)
    G["gate_b"] = jnp.sum(dg8, axis=0, keepdims=True)
    dz0 = jnp.concatenate([duc, dva.astype(BF16), doa, dqb, dfb0, dib, dgb,
                           jnp.pad(dg8, ((0, 0), (0, HD - 8))).astype(BF16)], axis=1)
    G["w_in0"] = _mm_tn(u0, dz0, "in0_dw")
    tok = grads_ready("in0", dict(w_in=G["w_in0"]))
    dx, dgmix0 = _bwd_in(dz0, W["w_in0"], x, S["norm_mix_g"][0:1] + tok, dh1, "in0_bwd")

    G["norm_mix_g"] = jnp.concatenate([dgmix0, dgmix1], axis=0)
    G["norm_xattn_g"] = jnp.concatenate(dgx, axis=0)
    G["norm_mlp_g"] = jnp.concatenate(dgm, axis=0)
    G["wq"], G["wo"], G["w1"], G["w2"] = dwq, dwo, dw1, dw2
    return lossp[0, 0], dx, G


ANY = pl.BlockSpec(memory_space=pl.ANY)
NCHIP = 4


def _place():
    x, y, c = lax.axis_index("x"), lax.axis_index("y"), lax.axis_index("c")
    return x, y, c, [(1 - x, y), (x, 1 - y), (1 - x, 1 - y)]


def _comm_call(body, name, ins, out_shapes, sems):
    return pl.pallas_call(body, name=name, in_specs=[ANY] * len(ins), out_specs=[ANY] * len(out_shapes),
                          out_shape=out_shapes, scratch_shapes=sems)(*ins)


HBM = pl.BlockSpec(memory_space=pltpu.HBM)
SEM = pl.BlockSpec(memory_space=pltpu.SEMAPHORE)
DATAFLOW = pltpu.SideEffectType.DATAFLOW_SIDE_EFFECTING


def _half_rows(r, cc):
    return pl.ds(pl.multiple_of(cc * (r // 2), r // 2), r // 2)


def _gather_start(arrs, after, name):
    n = len(arrs)

    def body(*refs):
        ins, lands = refs[:n], refs[n:2 * n]
        send, recv, token = refs[2 * n + 1], refs[2 * n + 2], refs[-1]
        x, y, c, chips = _place()
        me = 2 * x + y
        for a in range(n):
            rows = _half_rows(arrs[a].shape[0], c)
            for k, (px, py) in enumerate(chips):
                pltpu.make_async_remote_copy(src_ref=ins[a].at[rows], dst_ref=lands[a].at[me, rows], send_sem=send.at[3 * a + k],
                                             recv_sem=recv.at[3 * a + k], device_id=(px, py, c), device_id_type=MESH).start()
        token[...] = jnp.zeros_like(token)

    hbm = lambda v: pltpu.with_memory_space_constraint(v, pltpu.HBM)
    land_shapes = [((NCHIP,) + a.shape, a.dtype) for a in arrs]
    out = pl.pallas_call(
        body, name=name,
        out_shape=(pltpu.SemaphoreType.DMA((3 * n,)), pltpu.SemaphoreType.DMA((3 * n,)), *[pltpu.HBM(a.shape, a.dtype) for a in arrs],
                   *[pltpu.HBM(s, d) for s, d in land_shapes], _sds((8, HD))),
        in_specs=[HBM] * (2 * n) + [ANY], out_specs=(SEM, SEM, *[HBM] * (2 * n), pl.BlockSpec(memory_space=pltpu.VMEM)),
        input_output_aliases={i: 2 + i for i in range(2 * n)},
        compiler_params=pltpu.CompilerParams(has_side_effects=DATAFLOW),
    )(*[hbm(a) for a in arrs], *[hbm(lax.empty(s, d)) for s, d in land_shapes], after)
    return out[0], out[1], list(out[2:2 + n]), list(out[2 + n:2 + 2 * n]), out[-1]


def _gather_wait(send, recv, srcs, lands, after, name):
    n = len(srcs)

    def body(*refs):
        ins, lands_ = refs[:n], refs[n:2 * n]
        send_, recv_ = refs[2 * n], refs[2 * n + 1]
        x, y, c, chips = _place()
        for a in range(n):
            rows = _half_rows(srcs[a].shape[0], c)
            for k, (px, py) in enumerate(chips):
                cp = pltpu.make_async_remote_copy(src_ref=ins[a].at[rows], dst_ref=lands_[a].at[2 * px + py, rows], send_sem=send_.at[3 * a + k],
                                                  recv_sem=recv_.at[3 * a + k], device_id=(px, py, c), device_id_type=MESH)
                cp.wait_send()
                cp.wait_recv()

    out = pl.pallas_call(
        body, name=name, out_shape=[pltpu.HBM(v.shape, v.dtype) for v in list(srcs) + list(lands)],
        in_specs=[HBM] * (2 * n) + [SEM, SEM, ANY], out_specs=[HBM] * (2 * n), input_output_aliases={i: i for i in range(2 * n)},
        compiler_params=pltpu.CompilerParams(has_side_effects=DATAFLOW),
    )(*srcs, *lands, send, recv, after)
    return list(out[n:])


def _pair_forward(lands, name):
    n = len(lands)

    def body(*refs):
        ins, outs = refs[:n], refs[n:2 * n]
        send, recv = refs[2 * n:]
        x, y, c, chips = _place()
        copies = []
        for a in range(n):
            r = lands[a].shape[1]
            for k, (px, py) in enumerate(chips):
                cp = pltpu.make_async_remote_copy(
                    src_ref=ins[a].at[2 * px + py, _half_rows(r, c)], dst_ref=outs[a].at[2 * px + py, _half_rows(r, c)],
                    send_sem=send.at[a, k], recv_sem=recv.at[a, k], device_id=(x, y, 1 - c), device_id_type=MESH)
                cp.start()
                copies.append(cp)
        for a in range(n):
            r = lands[a].shape[1]
            for k, (px, py) in enumerate(chips):
                pltpu.make_async_remote_copy(
                    src_ref=ins[a].at[2 * px + py, _half_rows(r, c)], dst_ref=outs[a].at[2 * px + py, _half_rows(r, 1 - c)],
                    send_sem=send.at[a, k], recv_sem=recv.at[a, k], device_id=(x, y, 1 - c), device_id_type=MESH).wait_recv()
        for cp in copies:
            cp.wait_send()

    return pl.pallas_call(body, name=name, in_specs=[ANY] * n, out_specs=[ANY] * n, out_shape=[_sds(v.shape, v.dtype) for v in lands],
                          scratch_shapes=[pltpu.SemaphoreType.DMA((n, 3)), pltpu.SemaphoreType.DMA((n, 3))],
                          input_output_aliases={i: i for i in range(n)})(*lands)


def _pair_exchange(arrs, name):
    n = len(arrs)

    def body(*refs):
        ins, outs = refs[:n], refs[n:2 * n]
        send, recv = refs[2 * n:]
        x, y, c, _ = _place()
        copies = []
        for a in range(n):
            h = arrs[a].shape[1] // 2
            cp = pltpu.make_async_remote_copy(src_ref=ins[a].at[:, pl.ds(pl.multiple_of((1 - c) * h, h), h)], dst_ref=outs[a],
                                              send_sem=send.at[a], recv_sem=recv.at[a], device_id=(x, y, 1 - c), device_id_type=MESH)
            cp.start()
            copies.append(cp)
        for cp in copies:
            cp.wait()

    return _comm_call(body, name, arrs, [_sds((a.shape[0], a.shape[1] // 2, a.shape[2]), a.dtype) for a in arrs],
                      [pltpu.SemaphoreType.DMA((n,)), pltpu.SemaphoreType.DMA((n,))])


def _pair_exchange_start(arrs, name):
    n = len(arrs)
    land_shapes = [((a.shape[0], a.shape[1] // 2, a.shape[2]), a.dtype) for a in arrs]

    def body(*refs):
        ins, lands = refs[:n], refs[n:2 * n]
        send, recv, token = refs[2 * n], refs[2 * n + 1], refs[-1]
        x, y, c, _ = _place()
        for a in range(n):
            h = arrs[a].shape[1] // 2
            pltpu.make_async_remote_copy(src_ref=ins[a].at[:, pl.ds(pl.multiple_of((1 - c) * h, h), h)], dst_ref=lands[a],
                                         send_sem=send.at[a], recv_sem=recv.at[a], device_id=(x, y, 1 - c), device_id_type=MESH).start()
        token[...] = jnp.zeros_like(token)

    hbm = lambda v: pltpu.with_memory_space_constraint(v, pltpu.HBM)
    out = pl.pallas_call(
        body, name=name,
        out_shape=(pltpu.SemaphoreType.DMA((n,)), pltpu.SemaphoreType.DMA((n,)), *[pltpu.HBM(a.shape, a.dtype) for a in arrs],
                   *[pltpu.HBM(s, d) for s, d in land_shapes], _sds((8, HD))),
        in_specs=[HBM] * (2 * n), out_specs=(SEM, SEM, *[HBM] * (2 * n), pl.BlockSpec(memory_space=pltpu.VMEM)),
        input_output_aliases={i: 2 + i for i in range(2 * n)},
        compiler_params=pltpu.CompilerParams(has_side_effects=DATAFLOW),
    )(*[hbm(a) for a in arrs], *[hbm(lax.empty(s, d)) for s, d in land_shapes])
    return out[0], out[1], list(out[2:2 + n]), list(out[2 + n:2 + 2 * n]), out[-1]


def _pair_exchange_wait(send, recv, srcs, lands, after, name):
    n = len(srcs)

    def body(*refs):
        ins, lands_ = refs[:n], refs[n:2 * n]
        send_, recv_ = refs[2 * n], refs[2 * n + 1]
        x, y, c, _ = _place()
        for a in range(n):
            h = srcs[a].shape[1] // 2
            cp = pltpu.make_async_remote_copy(src_ref=ins[a].at[:, pl.ds(pl.multiple_of((1 - c) * h, h), h)], dst_ref=lands_[a],
                                              send_sem=send_.at[a], recv_sem=recv_.at[a], device_id=(x, y, 1 - c), device_id_type=MESH)
            cp.wait_send()
            cp.wait_recv()

    out = pl.pallas_call(
        body, name=name, out_shape=[pltpu.HBM(v.shape, v.dtype) for v in list(srcs) + list(lands)],
        in_specs=[HBM] * (2 * n) + [SEM, SEM, ANY], out_specs=[HBM] * (2 * n), input_output_aliases={i: i for i in range(2 * n)},
        compiler_params=pltpu.CompilerParams(has_side_effects=DATAFLOW),
    )(*srcs, *lands, send, recv, after)
    return list(out[:n]), list(out[n:])


def _chip_exchange_start(arrs, name):
    n = len(arrs)

    def body(*refs):
        ins, lands = refs[:n], refs[n:2 * n]
        send, recv, token = refs[2 * n], refs[2 * n + 1], refs[-1]
        x, y, c, chips = _place()
        me = 2 * x + y
        for a in range(n):
            for k, (px, py) in enumerate(chips):
                pltpu.make_async_remote_copy(src_ref=ins[a].at[2 * px + py], dst_ref=lands[a].at[me], send_sem=send.at[3 * a + k],
                                             recv_sem=recv.at[3 * a + k], device_id=(px, py, c), device_id_type=MESH).start()
        token[...] = jnp.zeros_like(token)

    hbm = lambda v: pltpu.with_memory_space_constraint(v, pltpu.HBM)
    out = pl.pallas_call(
        body, name=name,
        out_shape=(pltpu.SemaphoreType.DMA((3 * n,)), pltpu.SemaphoreType.DMA((3 * n,)), *[pltpu.HBM(a.shape, a.dtype) for a in arrs],
                   *[pltpu.HBM(a.shape, a.dtype) for a in arrs], _sds((8, HD))),
        in_specs=[HBM] * (2 * n), out_specs=(SEM, SEM, *[HBM] * (2 * n), pl.BlockSpec(memory_space=pltpu.VMEM)),
        input_output_aliases={i: 2 + i for i in range(2 * n)},
        compiler_params=pltpu.CompilerParams(has_side_effects=DATAFLOW),
    )(*[hbm(a) for a in arrs], *[hbm(lax.empty(a.shape, a.dtype)) for a in arrs])
    return out[0], out[1], list(out[2:2 + n]), list(out[2 + n:2 + 2 * n]), out[-1]


def _chip_exchange_wait(send, recv, srcs, lands, after, name):
    n = len(srcs)

    def body(*refs):
        ins, lands_ = refs[:n], refs[n:2 * n]
        send_, recv_ = refs[2 * n], refs[2 * n + 1]
        x, y, c, chips = _place()
        for a in range(n):
            for k, (px, py) in enumerate(chips):
                cp = pltpu.make_async_remote_copy(src_ref=ins[a].at[2 * px + py], dst_ref=lands_[a].at[2 * px + py], send_sem=send_.at[3 * a + k],
                                                  recv_sem=recv_.at[3 * a + k], device_id=(px, py, c), device_id_type=MESH)
                cp.wait_send()
                cp.wait_recv()

    out = pl.pallas_call(
        body, name=name, out_shape=[pltpu.HBM(v.shape, v.dtype) for v in list(srcs) + list(lands)],
        in_specs=[HBM] * (2 * n) + [SEM, SEM, ANY], out_specs=[HBM] * (2 * n), input_output_aliases={i: i for i in range(2 * n)},
        compiler_params=pltpu.CompilerParams(has_side_effects=DATAFLOW),
    )(*srcs, *lands, send, recv, after)
    return list(out[n:])


def _pair_swap(arrs, name):
    n = len(arrs)

    def body(*refs):
        ins, outs = refs[:n], refs[n:2 * n]
        send, recv = refs[2 * n:]
        x, y, c, _ = _place()
        copies = []
        for a in range(n):
            cp = pltpu.make_async_remote_copy(src_ref=ins[a], dst_ref=outs[a], send_sem=send.at[a], recv_sem=recv.at[a],
                                              device_id=(x, y, 1 - c), device_id_type=MESH)
            cp.start()
            copies.append(cp)
        for cp in copies:
            cp.wait()

    return _comm_call(body, name, arrs, [_sds(a.shape, a.dtype) for a in arrs],
                      [pltpu.SemaphoreType.DMA((n,)), pltpu.SemaphoreType.DMA((n,))])


def _all_gather_devices(v, name):
    def body(v_ref, o_ref, send, recv, loc):
        x, y, c, _ = _place()
        me = 4 * x + 2 * y + c
        own = pltpu.make_async_copy(v_ref, o_ref.at[me], loc)
        own.start()
        copies = [own]
        for k in range(1, 8):
            fx, fy, fc = (k >> 2) & 1, (k >> 1) & 1, k & 1
            peer = (x ^ fx, y ^ fy, c ^ fc)
            r = pltpu.make_async_remote_copy(src_ref=v_ref, dst_ref=o_ref.at[me], send_sem=send.at[k - 1],
                                             recv_sem=recv.at[k - 1], device_id=peer, device_id_type=MESH)
            r.start()
            copies.append(r)
        for cp in copies:
            cp.wait()

    return _comm_call(body, name, [v], [_sds((8,) + v.shape, v.dtype)],
                      [pltpu.SemaphoreType.DMA((7,)), pltpu.SemaphoreType.DMA((7,)), pltpu.SemaphoreType.DMA])[0]


def _row_tile(r):
    return next((b for b in (512, 384, 256, 128, 64, 32, 16) if r % b == 0), r)


def _add2(a, b, out_dtype, name):
    r, w = a.shape
    br = _row_tile(r)

    def body(a_ref, b_ref, o_ref):
        o_ref[...] = (a_ref[...].astype(F32) + b_ref[...].astype(F32)).astype(out_dtype)

    blk = pl.BlockSpec((br, w), lambda i: (i, 0))
    return _pc(body, name, (r // br,), [blk, blk], blk, _sds((r, w), out_dtype))(a, b)


def _sum_slots(a, out_dtype, name):
    n, r, w = a.shape
    br = _row_tile(r)

    def body(a_ref, o_ref):
        acc = a_ref[0].astype(F32)
        for s in range(1, n):
            acc = acc + a_ref[s].astype(F32)
        o_ref[...] = acc.astype(out_dtype)

    return _pc(body, name, (r // br,), [pl.BlockSpec((n, br, w), lambda i: (0, i, 0))], pl.BlockSpec((br, w), lambda i: (i, 0)),
               _sds((r, w), out_dtype))(a)


SMALL = ["norm_mix_g", "norm_xattn_g", "norm_mlp_g", "final_norm_g", "mem_norm_g", "hgrn_lb_logits", "mlstm_norm_g",
         "hgrn_norm_g", "c_qnorm_g", "c_knorm_g", "ab_gate_b", "c_fgate_b"]
SMALL_ROWS = 16


def _pack_small(parts):
    flat = jnp.concatenate([p.reshape(-1).astype(F32) for p in parts])
    return jnp.pad(flat, (0, SMALL_ROWS * D - flat.shape[0])).reshape(SMALL_ROWS, D)


def _unpack_small(buf, shapes):
    flat, out, off = buf.reshape(-1), [], 0
    for s in shapes:
        n = 1
        for d in s:
            n *= d
        out.append(flat[off:off + n].reshape(s))
        off += n
    return out


def kernel(x, mem, norm_mix_g, norm_xattn_g, norm_mlp_g, final_norm_g, ab_w_in, ab_conv_w, ab_gate_b, hgrn_lb_logits, mlstm_norm_g, hgrn_norm_g, ab_w_out, c_w_in, c_fgate_b, c_qnorm_g, c_knorm_g, c_w_out, mem_norm_g, mem_w_kv, xa_w_q, xa_w_o, mlp_w1, mlp_w2, loss_target, m_norm_mix_g, m_norm_xattn_g, m_norm_mlp_g, m_final_norm_g, m_ab_w_in, m_ab_conv_w, m_ab_gate_b, m_hgrn_lb_logits, m_mlstm_norm_g, m_hgrn_norm_g, m_ab_w_out, m_c_w_in, m_c_fgate_b, m_c_qnorm_g, m_c_knorm_g, m_c_w_out, m_mem_norm_g, m_mem_w_kv, m_xa_w_q, m_xa_w_o, m_mlp_w1, m_mlp_w2, v_norm_mix_g, v_norm_xattn_g, v_norm_mlp_g, v_final_norm_g, v_ab_w_in, v_ab_conv_w, v_ab_gate_b, v_hgrn_lb_logits, v_mlstm_norm_g, v_hgrn_norm_g, v_ab_w_out, v_c_w_in, v_c_fgate_b, v_c_qnorm_g, v_c_knorm_g, v_c_w_out, v_mem_norm_g, v_mem_w_kv, v_xa_w_q, v_xa_w_o, v_mlp_w1, v_mlp_w2):
    A = dict(locals())
    chip = 2 * lax.axis_index("x") + lax.axis_index("y")

    big = ["ab_w_in", "c_w_in", "ab_w_out", "c_w_out", "mem_w_kv", "xa_w_q", "xa_w_o", "mlp_w1", "mlp_w2"]
    shard2d = {"ab_w_in": (D, 1026), "c_w_in": (D, 1026), "ab_w_out": (256, D), "c_w_out": (256, D), "mem_w_kv": (D, 512),
               "xa_w_q": (512, D), "xa_w_o": (512, D), "mlp_w1": (2 * D, D), "mlp_w2": (2 * D, D)}
    own_slot = lambda gs, os: [lax.dynamic_update_index_in_dim(g, o, chip, 0) for g, o in zip(gs, os)]
    cols = lambda g: jnp.concatenate([g[k] for k in range(NCHIP)], axis=1)
    per_layer = lambda g: g.reshape(NCHIP, 2, -1, D).transpose(1, 0, 2, 3)
    first = [A["ab_w_in"].reshape(shard2d["ab_w_in"]).astype(BF16), jnp.pad(ab_conv_w[0], ((0, 16 - CONV_W), (0, 0)))]
    *first_handles, tok_first = _gather_start(first, ab_conv_w, "gather_first_start")
    rest_names = ["c_w_in", "ab_w_out", "c_w_out", "xa_w_q", "xa_w_o", "mlp_w1", "mlp_w2", "mem_w_kv"]
    rest = [(A[n].reshape(shard2d[n]) + tok_first[0, 0]).astype(BF16) for n in rest_names]
    send_s, recv_s, srcs, lands, token = _gather_start(rest, tok_first, "gather_rest_start")
    g_in0, g_conv = own_slot(_pair_forward(_gather_wait(*first_handles, token, "gather_first_wait"), "gather_first_forward"), first)
    W = dict(w_in0=_pack_w_in0(cols(g_in0)))

    def late_weights(after):
        got = _pair_forward(_gather_wait(send_s, recv_s, srcs, lands, after, "gather_rest_wait"), "gather_rest_forward")
        gw = dict(zip(rest_names, own_slot(got, rest)))
        return dict(w_in1=_pack_w_in1(cols(gw["c_w_in"])), w_out0=gw["ab_w_out"].reshape(D, D), w_out1=gw["c_w_out"].reshape(D, D),
                    wkv_s=gw["mem_w_kv"],
                    wq=per_layer(gw["xa_w_q"]).reshape(2, D, D), wo=per_layer(gw["xa_w_o"]).reshape(2, D, D),
                    w1s=gw["mlp_w1"].reshape(NCHIP, 2, D, D), w2=gw["mlp_w2"].reshape(NCHIP, 2, D, D))

    S = dict(norm_mix_g=norm_mix_g + token[0, 0], norm_xattn_g=norm_xattn_g, norm_mlp_g=norm_mlp_g, final_norm_g=final_norm_g,
             conv_w=cols(g_conv[:, :CONV_W]), gate_b=ab_gate_b, lb_logits=hgrn_lb_logits, mlstm_norm_g=mlstm_norm_g,
             hgrn_norm_g=hgrn_norm_g, c_fgate_b=c_fgate_b, c_qnorm_g=c_qnorm_g, c_knorm_g=c_knorm_g, mem_norm_g=mem_norm_g)

    core = lax.axis_index("c")
    by_rows = lambda g: g.reshape(NCHIP, -1, D)

    def stack_cols(g):
        return jnp.stack([g[:, 1026 * k:1026 * (k + 1)] for k in range(NCHIP)])

    def pair_sums(arrs, theirs, tag):
        out = []
        for i, (a, th) in enumerate(zip(arrs, theirs)):
            h = a.shape[1] // 2
            mine = lax.dynamic_slice_in_dim(a, core * h, h, axis=1)
            out.append(_add2(mine.reshape(-1, a.shape[2]), th.reshape(-1, a.shape[2]), BF16, f"pair_sum_{tag}{i}").reshape(th.shape))
        return out

    def start_chip_exchange(stage, psums):
        *handles, token = _chip_exchange_start(psums, f"chip_exchange_start_{stage}")
        started[stage] = (psums, handles)
        return token[0, 0]

    def chip_sums(psums, from_chips, tag):
        out = []
        for i, (f, p) in enumerate(zip(from_chips, psums)):
            f = lax.dynamic_update_index_in_dim(f, lax.dynamic_index_in_dim(p, chip, 0, keepdims=False), chip, 0)
            out.append(_sum_slots(f, F32, f"chip_sum_{tag}{i}"))
        return out

    started, pending = {}, {}

    def grads_ready(stage, g):
        if stage == "in0":
            arrs = [stack_cols(_unpack_w_in0(g["w_in"]))]
            return start_chip_exchange(stage, pair_sums(arrs, _pair_exchange(arrs, f"pair_exchange_{stage}"), stage))
        arrs = [jnp.concatenate([by_rows(g["w_out"]), by_rows(g["wq"]), by_rows(g["wo"]), g["w1"], by_rows(g["w2"])], axis=1),
                stack_cols(_unpack_w_in1(g["w_in"])) if stage == "layer1" else g["wkv"]]
        *pending[stage], token = _pair_exchange_start(arrs, f"pair_exchange_start_{stage}")
        return token[0, 0]

    def grads_next(stage, after):
        arrs, theirs = _pair_exchange_wait(*pending[stage], after, f"pair_exchange_wait_{stage}")
        return start_chip_exchange(stage, pair_sums(arrs, theirs, stage))

    lossp, dx, G = _local_step(x[0], mem[0], loss_target[0], W, S, late_weights, (grads_ready, grads_next))

    gsmall = {"norm_mix_g": G["norm_mix_g"], "norm_xattn_g": G["norm_xattn_g"], "norm_mlp_g": G["norm_mlp_g"],
              "final_norm_g": G["final_norm_g"], "mem_norm_g": G["mem_norm_g"], "hgrn_lb_logits": G["lb_logits"],
              "mlstm_norm_g": G["mlstm_norm_g"], "hgrn_norm_g": G["hgrn_norm_g"], "c_qnorm_g": G["c_qnorm_g"],
              "c_knorm_g": G["c_knorm_g"], "ab_gate_b": G["gate_b"], "c_fgate_b": G["c_fgate_b"]}
    packed = _pack_small([gsmall[n] for n in SMALL] + [G["conv_w"], lossp])
    red = _sum_slots(_all_gather_devices(packed, "gather_small"), F32, "sum_small")
    small_shapes = [A[n].shape for n in SMALL]
    *gs, gconv, loss = _unpack_small(red, small_shapes + [(CONV_W, D), ()])
    gs = dict(zip(SMALL, gs))
    gconv = lax.dynamic_slice_in_dim(gconv, chip * 256, 256, axis=1)[None]

    rhalf = []
    for stage in ("layer1", "layer0", "in0"):
        psums, handles = started[stage]
        rhalf += chip_sums(psums, _chip_exchange_wait(*handles, dx, f"chip_exchange_wait_{stage}"), stage)
    other = _pair_swap(rhalf, "pair_swap")
    r_l1, r_in1, r_l0, r_kv, r_in0 = [
        jnp.where(core == 0, jnp.concatenate([m_, o_], axis=0), jnp.concatenate([o_, m_], axis=0)) for m_, o_ in zip(rhalf, other)]
    both = lambda lo, hi: jnp.concatenate([r_l0[lo:hi], r_l1[lo:hi]], axis=0)
    gbig = {"ab_w_in": r_in0, "c_w_in": r_in1, "mem_w_kv": r_kv, "ab_w_out": r_l0[0:256], "c_w_out": r_l1[0:256],
            "xa_w_q": both(256, 512), "xa_w_o": both(512, 768), "mlp_w1": both(768, 1792), "mlp_w2": both(1792, 2816)}

    out_g, out_d, out_m, out_v = {}, {}, {}, {}
    for n in big:
        d_, m_, v_ = _adam(A[n].reshape(shard2d[n]), gbig[n], A["m_" + n].reshape(shard2d[n]), A["v_" + n].reshape(shard2d[n]), "adam_" + n)
        out_g[n] = gbig[n].reshape(A[n].shape)
        out_d[n], out_m[n], out_v[n] = d_.reshape(A[n].shape), m_.reshape(A[n].shape), v_.reshape(A[n].shape)
    sd, sm, sv = _adam(_pack_small([A[n] for n in SMALL]), _pack_small([gs[n] for n in SMALL]),
                       _pack_small([A["m_" + n] for n in SMALL]), _pack_small([A["v_" + n] for n in SMALL]), "adam_small")
    for n, d_, m_, v_ in zip(SMALL, _unpack_small(sd, small_shapes), _unpack_small(sm, small_shapes), _unpack_small(sv, small_shapes)):
        out_g[n], out_d[n], out_m[n], out_v[n] = gs[n], d_, m_, v_
    cd, cm_, cv = _adam(ab_conv_w[0], gconv[0], m_ab_conv_w[0], v_ab_conv_w[0], "adam_conv")
    out_g["ab_conv_w"], out_d["ab_conv_w"], out_m["ab_conv_w"], out_v["ab_conv_w"] = gconv, cd[None], cm_[None], cv[None]

    order = ["norm_mix_g", "norm_xattn_g", "norm_mlp_g", "final_norm_g", "ab_w_in", "ab_conv_w", "ab_gate_b", "hgrn_lb_logits",
             "mlstm_norm_g", "hgrn_norm_g", "ab_w_out", "c_w_in", "c_fgate_b", "c_qnorm_g", "c_knorm_g", "c_w_out", "mem_norm_g",
             "mem_w_kv", "xa_w_q", "xa_w_o", "mlp_w1", "mlp_w2"]
    return (loss, dx[None], *[out_g[n] for n in order], *[out_d[n] for n in order], *[out_m[n] for n in order],
            *[out_v[n] for n in order])
```

```python
import jax
import jax.numpy as jnp
from jax import lax
from jax.experimental import pallas as pl
from jax.experimental.pallas import tpu as pltpu

F32 = jnp.float32
BF16 = jnp.bfloat16
EPS = 1e-6
D = 1024
CHUNK = 64
REC_CHUNKS = 8
HD = 128
XD = 256
NEG = -1e30
VMEM_LIMIT_V7X = 56 * 1024 * 1024
ADAM_LR, ADAM_B1, ADAM_B2, ADAM_EPS, ADAM_WD, ADAM_STEP = 0.001, 0.9, 0.999, 1e-08, 0.01, 10
MESH = pl.DeviceIdType.MESH


def _pc(body, name, grid, in_specs, out_specs, out_shape, scratch=(), **kw):
    return pl.pallas_call(
        body, name=name, grid=grid, in_specs=in_specs, out_specs=out_specs, out_shape=out_shape,
        scratch_shapes=scratch,
        compiler_params=pltpu.CompilerParams(
            dimension_semantics=("arbitrary",) * len(grid), vmem_limit_bytes=VMEM_LIMIT_V7X), **kw)


def _sds(shape, dtype=F32):
    return jax.ShapeDtypeStruct(shape, dtype)


def _blk(n, target):
    return max(b for b in range(128, max(target, 128) + 1, 128) if n % b == 0)


def _dot(a, b, dims):
    return lax.dot_general(a, b, (dims, ((), ())), preferred_element_type=F32)


def _nn(a, b):
    return _dot(a, b, ((1,), (0,)))


def _nt(a, b):
    return _dot(a, b, ((1,), (1,)))


def _tn(a, b):
    return _dot(a, b, ((0,), (0,)))


def _sigmoid(x):
    return 1.0 / (1.0 + jnp.exp(-x))


def _log_sigmoid(x):
    return jnp.minimum(x, 0.0) - jnp.log(1.0 + jnp.exp(-jnp.abs(x)))


def _rstd(x):
    return lax.rsqrt(jnp.mean(x * x, axis=-1, keepdims=True) + EPS)


def _rms_bwd(du, x, g):
    r = _rstd(x)
    xh = x * r
    dxh = du * g
    dx = r * (dxh - xh * jnp.mean(dxh * xh, axis=-1, keepdims=True))
    return dx, du * xh


def _norm_mm(h, g, w, name, bm=1024, bn=512):
    t, n = h.shape[0], w.shape[1]
    bm, bn = min(bm, t), _blk(n, 3 * bn)

    def body(h_ref, g_ref, w_ref, z_ref, u_ref):
        @pl.when(pl.program_id(1) == 0)
        def _():
            x = h_ref[...]
            u_ref[...] = (x * _rstd(x) * g_ref[...]).astype(BF16)
        z_ref[...] = _nn(u_ref[...], w_ref[...])

    return _pc(body, name, (t // bm, n // bn),
               [pl.BlockSpec((bm, D), lambda i, j: (i, 0)), pl.BlockSpec((1, D), lambda i, j: (0, 0)),
                pl.BlockSpec((D, bn), lambda i, j: (0, j))],
               [pl.BlockSpec((bm, bn), lambda i, j: (i, j)), pl.BlockSpec((bm, D), lambda i, j: (i, 0))],
               [_sds((t, n)), _sds((t, D), BF16)])(h, g, w)


def _mm_tn(a, b, name, bm=1024, bn=1024, bt=4096, col_chips=None):
    t, m = a.shape
    n = b.shape[1]
    bm, bn, bt = _blk(m, bm), (n // col_chips if col_chips else _blk(n, bn + bn // 2)), min(bt, t)
    if (m // bm) * (n // bn) == 1 and bt >= 1024:
        bt //= 4
    nt = t // bt

    def body(a_ref, b_ref, o_ref, acc):
        k = pl.program_id(2)

        @pl.when(k == 0)
        def _():
            acc[...] = jnp.zeros_like(acc)

        acc[...] += _tn(a_ref[...].astype(BF16), b_ref[...].astype(BF16))

        @pl.when(k == nt - 1)
        def _():
            o_ref[...] = acc[...].astype(BF16)

    if col_chips:
        out_spec, out_shape = pl.BlockSpec((None, bm, bn), lambda i, j, k: (j, i, 0)), _sds((col_chips, m, bn), BF16)
    else:
        out_spec, out_shape = pl.BlockSpec((bm, bn), lambda i, j, k: (i, j)), _sds((m, n), BF16)
    return _pc(body, name, (m // bm, n // bn, nt),
               [pl.BlockSpec((bt, bm), lambda i, j, k: (k, i)), pl.BlockSpec((bt, bn), lambda i, j, k: (k, j))],
               out_spec, out_shape, scratch=[pltpu.VMEM((bm, bn), F32)])(a, b)


def _bwd_in(dz, w, h, g, dh, name, bm=1024, bk=1024):
    t, n = dz.shape
    if n > 2 * bk:
        bm, bk = min(bm // 2, t), n
    else:
        bm, bk = min(bm, t), _blk(n, bk + bk // 2)
    nk = n // bk

    def body(dz_ref, w_ref, h_ref, g_ref, dh_ref, o_ref, dg_ref, acc):
        i, k = pl.program_id(0), pl.program_id(1)

        @pl.when(k == 0)
        def _():
            acc[...] = jnp.zeros_like(acc)

        @pl.when((i == 0) & (k == 0))
        def _():
            dg_ref[...] = jnp.zeros_like(dg_ref)

        acc[...] += _nt(dz_ref[...], w_ref[...])

        @pl.when(k == nk - 1)
        def _():
            dx, dgr = _rms_bwd(acc[...], h_ref[...], g_ref[...])
            o_ref[...] = dh_ref[...] + dx
            dg_ref[...] += jnp.sum(dgr, axis=0, keepdims=True)

    return _pc(body, name, (t // bm, nk),
               [pl.BlockSpec((bm, bk), lambda i, k: (i, k)), pl.BlockSpec((D, bk), lambda i, k: (0, k)),
                pl.BlockSpec((bm, D), lambda i, k: (i, 0)), pl.BlockSpec((1, D), lambda i, k: (0, 0)),
                pl.BlockSpec((bm, D), lambda i, k: (i, 0))],
               [pl.BlockSpec((bm, D), lambda i, k: (i, 0)), pl.BlockSpec((1, D), lambda i, k: (0, 0))],
               [_sds((t, D)), _sds((1, D))], scratch=[pltpu.VMEM((bm, D), F32)])(dz, w, h, g, dh)


def _mlp_fwd(h, g, w1s, w2, l, name, bm=1024):
    t = h.shape[0]
    bm = min(bm, t)
    nk = w1s.shape[0]

    def body(h_ref, g_ref, w1_ref, w2_ref, o_ref, a_ref, u_ref, acc):
        k = pl.program_id(1)

        @pl.when(k == 0)
        def _():
            x = h_ref[...]
            u_ref[...] = (x * _rstd(x) * g_ref[...]).astype(BF16)
            acc[...] = jnp.zeros_like(acc)

        a = _nn(u_ref[...], w1_ref[...])
        a_ref[...] = a
        r = jnp.square(jnp.maximum(a, 0.0)).astype(BF16)
        acc[...] += _nn(r, w2_ref[...])

        @pl.when(k == nk - 1)
        def _():
            o_ref[...] = h_ref[...] + acc[...]

    return _pc(body, name, (t // bm, nk),
               [pl.BlockSpec((bm, D), lambda i, k: (i, 0)), pl.BlockSpec((1, D), lambda i, k: (0, 0)),
                pl.BlockSpec((None, None, D, D), lambda i, k: (k, l, 0, 0)), pl.BlockSpec((None, None, D, D), lambda i, k: (k, l, 0, 0))],
               [pl.BlockSpec((bm, D), lambda i, k: (i, 0)), pl.BlockSpec((bm, D), lambda i, k: (i, k)),
                pl.BlockSpec((bm, D), lambda i, k: (i, 0))],
               [_sds((t, D)), _sds((t, nk * D)), _sds((t, D), BF16)],
               scratch=[pltpu.VMEM((bm, D), F32)])(h, g, w1s, w2)


def _mlp_bwd(dh, a, w1s, w2, l, h, g, name, bm=512):
    t = h.shape[0]
    bm = min(bm, t)
    nk = w1s.shape[0]

    def body(dh_ref, a_ref, w1_ref, w2_ref, h_ref, g_ref, o_ref, da_ref, r_ref, dg_ref, acc):
        i, k = pl.program_id(0), pl.program_id(1)

        @pl.when(k == 0)
        def _():
            acc[...] = jnp.zeros_like(acc)

        @pl.when((i == 0) & (k == 0))
        def _():
            dg_ref[...] = jnp.zeros_like(dg_ref)

        ap = jnp.maximum(a_ref[...], 0.0)
        r_ref[...] = jnp.square(ap).astype(BF16)
        dr = _nt(dh_ref[...].astype(BF16), w2_ref[...])
        da = (dr * (2.0 * ap)).astype(BF16)
        da_ref[...] = da
        acc[...] += _nt(da, w1_ref[...])

        @pl.when(k == nk - 1)
        def _():
            dx, dgr = _rms_bwd(acc[...], h_ref[...], g_ref[...])
            o_ref[...] = dh_ref[...] + dx
            dg_ref[...] += jnp.sum(dgr, axis=0, keepdims=True)

    return _pc(body, name, (t // bm, nk),
               [pl.BlockSpec((bm, D), lambda i, k: (i, 0)), pl.BlockSpec((bm, D), lambda i, k: (i, k)),
                pl.BlockSpec((None, None, D, D), lambda i, k: (k, l, 0, 0)), pl.BlockSpec((None, None, D, D), lambda i, k: (k, l, 0, 0)),
                pl.BlockSpec((bm, D), lambda i, k: (i, 0)), pl.BlockSpec((1, D), lambda i, k: (0, 0))],
               [pl.BlockSpec((bm, D), lambda i, k: (i, 0)), pl.BlockSpec((bm, D), lambda i, k: (i, k)),
                pl.BlockSpec((bm, D), lambda i, k: (i, k)), pl.BlockSpec((1, D), lambda i, k: (0, 0))],
               [_sds((t, D)), _sds((t, nk * D), BF16), _sds((t, nk * D), BF16), _sds((1, D))],
               scratch=[pltpu.VMEM((bm, D), F32)])(dh, a, w1s, w2, h, g)


def _rows_of(x):
    return lax.broadcasted_iota(jnp.int32, x.shape, 0)


def _shift_down(x, s):
    if s == 0:
        return x
    return jnp.where(_rows_of(x) >= s, pltpu.roll(x, s, 0), 0.0)


def _shift_up(x, s):
    if s == 0:
        return x
    n = x.shape[0]
    return jnp.where(_rows_of(x) < n - s, pltpu.roll(x, n - s, 0), 0.0)


def _cumsum_rows(x):
    n, s = x.shape[0], 1
    while s < n:
        x = x + _shift_down(x, s)
        s *= 2
    return x


def _rcumsum_rows(x):
    n, s = x.shape[0], 1
    while s < n:
        x = x + _shift_up(x, s)
        s *= 2
    return x


def _silu(x):
    return x * _sigmoid(x)


def _dsilu(x):
    s = _sigmoid(x)
    return s * (1.0 + x * (1.0 - s))


CONV_W = 4


def _conv_pre(u, w):
    y = _shift_down(u, CONV_W - 1) * w[0:1, :]
    for j in range(1, CONV_W):
        y = y + _shift_down(u, CONV_W - 1 - j) * w[j:j + 1, :]
    return y


def _conv_fwd(z0, cw, name):
    t = z0.shape[0]

    def body(u_ref, w_ref, o_ref):
        o_ref[...] = _silu(_conv_pre(u_ref[...], w_ref[...]))

    return _pc(body, name, (2 * 512 // HD,),
               [pl.BlockSpec((t, HD), lambda c: (0, c)), pl.BlockSpec((CONV_W, HD), lambda c: (0, c))],
               pl.BlockSpec((t, HD), lambda c: (0, c)), _sds((t, 1024)))(z0, cw)


def _conv_bwd(z0, cw, dy, name):
    t = z0.shape[0]

    def body(u_ref, w_ref, dy_ref, du_ref, dw_ref):
        u, w = u_ref[...], w_ref[...]
        dpre = dy_ref[...] * _dsilu(_conv_pre(u, w))
        du = _shift_up(dpre, CONV_W - 1) * w[0:1, :]
        for j in range(1, CONV_W):
            du = du + _shift_up(dpre, CONV_W - 1 - j) * w[j:j + 1, :]
        du_ref[...] = du.astype(BF16)
        for j in range(CONV_W):
            dw_ref[j:j + 1, :] = jnp.sum(dpre * _shift_down(u, CONV_W - 1 - j), axis=0, keepdims=True)

    return _pc(body, name, (2 * 512 // HD,),
               [pl.BlockSpec((t, HD), lambda c: (0, c)), pl.BlockSpec((CONV_W, HD), lambda c: (0, c)),
                pl.BlockSpec((t, HD), lambda c: (0, c))],
               [pl.BlockSpec((t, HD), lambda c: (0, c)), pl.BlockSpec((CONV_W, HD), lambda c: (0, c))],
               [_sds((t, 1024), BF16), _sds((CONV_W, 1024))])(z0, cw, dy)


def _mlstm_gates(gate, bias, m_in):
    L = gate.shape[0]
    r = lax.broadcasted_iota(jnp.int32, (L, L), 0)
    c = lax.broadcasted_iota(jnp.int32, (L, L), 1)
    eye, tril = r == c, c <= r
    i_col = gate[:, 0:1] + bias[:, 0:1]
    f_col = gate[:, 1:2] + bias[:, 1:2]
    logf_col = _log_sigmoid(f_col)
    logf_row = jnp.sum(jnp.where(eye, logf_col, 0.0), axis=0, keepdims=True)
    i_row = jnp.sum(jnp.where(eye, i_col, 0.0), axis=0, keepdims=True)
    b_col = jnp.sum(jnp.where(tril, logf_row, 0.0), axis=1, keepdims=True)
    b_row = jnp.sum(jnp.where(r <= c, logf_col, 0.0), axis=0, keepdims=True)
    logd = jnp.where(tril, b_col - b_row + i_row, NEG)
    inter = b_col + m_in
    m_t = jnp.maximum(inter, jnp.max(logd, axis=1, keepdims=True))
    w_t = jnp.exp(inter - m_t)
    dm = jnp.exp(logd - m_t)
    b_last = b_col[L - 1:L, :]
    log_in = b_last - b_col + i_col
    m_new = jnp.maximum(b_last + m_in, jnp.max(log_in, axis=0, keepdims=True))
    w_col = jnp.exp(log_in - m_new)
    decay = jnp.exp(b_last + m_in - m_new)
    return dict(eye=eye, r=r, c=c, f_col=f_col, m_t=m_t, w_t=w_t, dm=dm, m_new=m_new, w_col=w_col, decay=decay)


def _mlstm_fwd(qk, z0, gates, bias, name):
    t = qk.shape[0]
    nc, nh, L = t // CHUNK, 4, CHUNK
    scale = HD ** -0.5

    def body(q_ref, k_ref, v_ref, g_ref, b_ref, h_ref, cs_ref, ns_ref, ms_ref, c_s, n_s, m_s):
        @pl.when(pl.program_id(0) == 0)
        def _():
            c_s[...] = jnp.zeros_like(c_s)
            n_s[...] = jnp.zeros_like(n_s)
            m_s[...] = jnp.zeros_like(m_s)

        for hd in range(nh):
            sl = slice(hd * HD, (hd + 1) * HD)
            cm, nv, m_in = c_s[hd], n_s[hd], m_s[hd]
            for ck in range(cps):
                rows = slice(ck * L, (ck + 1) * L)
                cs_ref[hd, ck] = cm
                ns_ref[hd, ck] = nv
                ms_ref[hd, ck] = jnp.broadcast_to(m_in, (1, HD))
                q, kh, v = q_ref[rows, sl], k_ref[rows, sl] * scale, v_ref[rows, sl]
                G = _mlstm_gates(g_ref[hd, rows, :], b_ref[hd], m_in)
                qb, kb, vb = q.astype(BF16), kh.astype(BF16), v.astype(BF16)
                sc = _nt(qb, kb) * G["dm"]
                num = _nn(sc.astype(BF16), vb) + G["w_t"] * _nn(qb, cm.astype(BF16))
                den = jnp.sum(sc, axis=1, keepdims=True) + G["w_t"] * jnp.sum(q * nv, axis=1, keepdims=True)
                h_ref[rows, sl] = num / jnp.maximum(jnp.abs(den), jnp.exp(-G["m_t"]))
                wk = G["w_col"] * kh
                cm = G["decay"] * cm + _tn(wk.astype(BF16), vb)
                nv = G["decay"] * nv + jnp.sum(wk, axis=0, keepdims=True)
                m_in = G["m_new"]
            c_s[hd], n_s[hd], m_s[hd] = cm, nv, m_in

    cps = REC_CHUNKS
    hspec = lambda blk: pl.BlockSpec((cps * L, 512), lambda j: (j, blk))
    st = lambda r: pl.BlockSpec((nh, cps, r, HD), lambda j: (0, j, 0, 0))
    return _pc(body, name, (nc // cps,),
               [hspec(0), hspec(1), hspec(2), pl.BlockSpec((nh, cps * L, 2), lambda j: (0, j, 0)),
                pl.BlockSpec((nh, 1, 2), lambda j: (0, 0, 0))],
               [hspec(0), st(HD), st(1), st(1)],
               [_sds((t, 512)), _sds((nh, nc, HD, HD)), _sds((nh, nc, 1, HD)), _sds((nh, nc, 1, HD))],
               scratch=[pltpu.VMEM((nh, HD, HD), F32), pltpu.VMEM((nh, 1, HD), F32), pltpu.VMEM((nh, 1, 1), F32)])(qk, qk, z0, gates, bias)


def _mlstm_bwd(qk, z0, gates, bias, cs, ns, ms, dh, name):
    t = qk.shape[0]
    nc, nh, L = t // CHUNK, 4, CHUNK
    scale = HD ** -0.5

    def body(q_ref, k_ref, v_ref, g_ref, b_ref, cs_ref, ns_ref, ms_ref, dh_ref, dqk_ref, dv_ref, dg_ref, dc_s, dn_s):
        @pl.when(pl.program_id(0) == 0)
        def _():
            dc_s[...] = jnp.zeros_like(dc_s)
            dn_s[...] = jnp.zeros_like(dn_s)

        for ck in reversed(range(cps)):
            for hd in range(nh):
                one_head(hd, ck, slice(hd * HD, (hd + 1) * HD), slice(ck * L, (ck + 1) * L), q_ref, k_ref, v_ref, g_ref, b_ref,
                         cs_ref, ns_ref, ms_ref, dh_ref, dqk_ref, dv_ref, dg_ref, dc_s, dn_s)

    def one_head(hd, ck, sl, rows, q_ref, k_ref, v_ref, g_ref, b_ref, cs_ref, ns_ref, ms_ref, dh_ref, dqk_ref, dv_ref, dg_ref,
                 dc_s, dn_s):
        cm, nv, m_in = cs_ref[hd, ck], ns_ref[hd, ck], ms_ref[hd, ck][:, 0:1]
        q, kh, v = q_ref[rows, sl], k_ref[rows, sl] * scale, v_ref[rows, sl]
        G = _mlstm_gates(g_ref[hd, rows, :], b_ref[hd], m_in)
        w_t, dmat, w_col, decay = G["w_t"], G["dm"], G["w_col"], G["decay"]
        qb, kb, vb, cb = q.astype(BF16), kh.astype(BF16), v.astype(BF16), cm.astype(BF16)
        s = _nt(qb, kb)
        sc = s * dmat
        scb = sc.astype(BF16)
        qc = _nn(qb, cb)
        qn = jnp.sum(q * nv, axis=1, keepdims=True)
        num = _nn(scb, vb) + w_t * qc
        den = jnp.sum(sc, axis=1, keepdims=True) + w_t * qn
        e_m = jnp.exp(-G["m_t"])
        dnm = jnp.maximum(jnp.abs(den), e_m)
        dh_ = dh_ref[rows, sl]
        dnum = dh_ / dnm
        dden = jnp.where(jnp.abs(den) > e_m, -jnp.sum(dh_ * num, axis=1, keepdims=True) / (dnm * dnm) * jnp.sign(den), 0.0)
        dnumb = dnum.astype(BF16)
        dsc = _nt(dnumb, vb) + dden
        dv = _tn(scb, dnumb)
        wd = w_t * dnum
        wdb = wd.astype(BF16)
        ds = dsc * dmat
        dsb = ds.astype(BF16)
        dq = _nt(wdb, cb) + (w_t * dden) * nv + _nn(dsb, kb)
        dc_o = _tn(qb, wdb)
        dn_o = jnp.sum(q * (w_t * dden), axis=0, keepdims=True)
        dw = jnp.sum(dnum * qc, axis=1, keepdims=True) + dden * qn
        dkh = _tn(dsb, qb)
        dlogd = ds * s
        db_col = jnp.sum(dlogd, axis=1, keepdims=True) + dw * w_t
        csum = jnp.sum(dlogd, axis=0, keepdims=True)
        dcn, dnn = dc_s[hd], dn_s[hd]
        dcnb = dcn.astype(BF16)
        kdc = _nn(kb, dcnb)
        dws = jnp.sum(kdc * v, axis=1, keepdims=True) + jnp.sum(kh * dnn, axis=1, keepdims=True)
        dv = dv + w_col * kdc
        dkh = dkh + w_col * (_nt(vb, dcnb) + dnn)
        dlin = dws * w_col
        ddecay = jnp.sum(jnp.sum(dcn * cm, axis=1, keepdims=True), axis=0, keepdims=True) + jnp.sum(dnn * nv, axis=1, keepdims=True)
        dlast = ddecay * decay + jnp.sum(dlin, axis=0, keepdims=True)
        row_id = lax.broadcasted_iota(jnp.int32, (L, 1), 0)
        db_col = db_col - dlin + jnp.where(row_id == L - 1, dlast, 0.0)
        eye, r, c = G["eye"], G["r"], G["c"]
        di = dlin + jnp.sum(jnp.where(eye, csum, 0.0), axis=1, keepdims=True)
        db_row = jnp.sum(jnp.where(eye, db_col, 0.0), axis=0, keepdims=True) - csum
        dlogf = jnp.sum(jnp.where(c >= r, db_row, 0.0), axis=1, keepdims=True)
        dg_ref[hd, rows, 0:1] = di
        dg_ref[hd, rows, 1:2] = dlogf * (1.0 - _sigmoid(G["f_col"]))
        dqk_ref[rows, sl] = dq
        dqk_ref[rows, 512 + hd * HD:512 + (hd + 1) * HD] = dkh * scale
        dv_ref[rows, sl] = dv
        dc_s[hd] = decay * dcn + dc_o
        dn_s[hd] = decay * dnn + dn_o

    cps = REC_CHUNKS
    rv = lambda j: nc // cps - 1 - j
    hspec = lambda blk: pl.BlockSpec((cps * L, 512), lambda j: (rv(j), blk))
    st = lambda r: pl.BlockSpec((nh, cps, r, HD), lambda j: (0, rv(j), 0, 0))
    gs = pl.BlockSpec((nh, cps * L, 2), lambda j: (0, rv(j), 0))
    return _pc(body, name, (nc // cps,),
               [hspec(0), hspec(1), hspec(2), gs, pl.BlockSpec((nh, 1, 2), lambda j: (0, 0, 0)),
                st(HD), st(1), st(1), hspec(0)],
               [pl.BlockSpec((cps * L, 1024), lambda j: (rv(j), 0)), hspec(0), gs],
               [_sds((t, 1024)), _sds((t, 512)), _sds((nh, t, 2))],
               scratch=[pltpu.VMEM((nh, HD, HD), F32), pltpu.VMEM((nh, 1, HD), F32)])(qk, qk, z0, gates, bias, cs, ns, ms, dh)


def _hgrn_act(qb_, fb_, ib_, lg):
    lb = _sigmoid(lg[0:1, :] - lg[1:2, :])
    sg = _sigmoid(fb_)
    f = lb + (1.0 - lb) * sg
    return lb, sg, f, _silu(qb_), (1.0 - lb) * (1.0 - sg), _silu(ib_), _cumsum_rows(jnp.log(f))


HG_SUB = 16


def _hgrn_offdiag(q, k, b, r0):
    beta = b[r0 - 1:r0, :]
    e1 = jnp.exp(b[r0:r0 + HG_SUB, :] - beta)
    e2 = jnp.where(_rows_of(b) < r0, jnp.exp(jnp.minimum(beta - b, 0.0)), 0.0)
    return q[r0:r0 + HG_SUB, :] * e1, k * e2, e1, e2


def _hgrn_fwd(z0, lbl, name):
    t = z0.shape[0]
    nc, nh, L = t // CHUNK, 4, CHUNK

    def body(q_ref, f_ref, i_ref, l_ref, o_ref, ss_ref, st_s):
        @pl.when(pl.program_id(0) == 0)
        def _():
            st_s[...] = jnp.zeros_like(st_s)

        for hd in range(nh):
            sl = slice(hd * HD, (hd + 1) * HD)
            st = st_s[hd]
            for ck in range(cps):
                rows = slice(ck * L, (ck + 1) * L)
                ss_ref[hd, ck] = st
                _, _, _, q, k, v, b = _hgrn_act(q_ref[rows, sl], f_ref[rows, sl], i_ref[rows, sl], l_ref[:, sl])
                o = _nt((q * jnp.exp(b)).astype(BF16), st.astype(BF16))
                sub = _rows_of(b) & (HG_SUB - 1)
                o = o + jnp.sum(q * k, axis=1, keepdims=True) * v
                for dl in range(1, HG_SUB):
                    e = jnp.exp(jnp.where(sub >= dl, b - pltpu.roll(b, dl, 0), NEG))
                    a = jnp.sum(q * pltpu.roll(k, dl, 0) * e, axis=1, keepdims=True)
                    o = o + a * pltpu.roll(v, dl, 0)
                o_ref[rows, sl] = o
                vb = v.astype(BF16)
                for i in range(1, L // HG_SUB):
                    r0 = i * HG_SUB
                    qt, kt, _, _ = _hgrn_offdiag(q, k, b, r0)
                    a = _nt(qt.astype(BF16), kt.astype(BF16))
                    o_ref[ck * L + r0:ck * L + r0 + HG_SUB, sl] += _nn(a.astype(BF16), vb)
                bl = b[L - 1:L, :]
                st = st * jnp.exp(bl) + _tn(v.astype(BF16), (k * jnp.exp(bl - b)).astype(BF16))
            st_s[hd] = st

    cps = REC_CHUNKS
    hspec = lambda blk: pl.BlockSpec((cps * L, 512), lambda j: (j, blk))
    return _pc(body, name, (nc // cps,),
               [hspec(4), hspec(5), hspec(6), pl.BlockSpec((2, 512), lambda j: (0, 0))],
               [hspec(0), pl.BlockSpec((nh, cps, HD, HD), lambda j: (0, j, 0, 0))],
               [_sds((t, 512)), _sds((nh, nc, HD, HD))],
               scratch=[pltpu.VMEM((nh, HD, HD), F32)])(z0, z0, z0, lbl)


def _hgrn_bwd(z0, lbl, ss, do, name):
    t = z0.shape[0]
    nc, nh, L = t // CHUNK, 4, CHUNK

    def body(q_ref, f_ref, i_ref, l_ref, ss_ref, do_ref, dq_ref, df_ref, di_ref, dl_ref, dst_s, dlb_s, dq_a, dk_a, dv_a, db_a):
        @pl.when(pl.program_id(0) == 0)
        def _():
            dst_s[...] = jnp.zeros_like(dst_s)
            dlb_s[...] = jnp.zeros_like(dlb_s)

        for ck in reversed(range(cps)):
            for hd in range(nh):
                one_head(hd, ck, slice(hd * HD, (hd + 1) * HD), slice(ck * L, (ck + 1) * L), q_ref, f_ref, i_ref, l_ref, ss_ref, do_ref,
                         dq_ref, df_ref, di_ref, dl_ref, dst_s, dlb_s, dq_a.at[hd], dk_a.at[hd], dv_a.at[hd], db_a.at[hd])

    def one_head(hd, ck, sl, rs, q_ref, f_ref, i_ref, l_ref, ss_ref, do_ref, dq_ref, df_ref, di_ref, dl_ref, dst_s, dlb_s,
                 dq_a, dk_a, dv_a, db_a):
        st = ss_ref[hd, ck]
        qp, fp, ip = q_ref[rs, sl], f_ref[rs, sl], i_ref[rs, sl]
        lb, sg, f, q, k, v, b = _hgrn_act(qp, fp, ip, l_ref[:, sl])
        do_ = do_ref[rs, sl]
        dob, stb = do_.astype(BF16), st.astype(BF16)
        eb = jnp.exp(b)
        qe = q * eb
        dqe = _nn(dob, stb)
        dst_o = _tn(dob, qe.astype(BF16))
        dq = dqe * eb
        db = dqe * qe
        rows = _rows_of(b)
        sub = rows & (HG_SUB - 1)
        p0 = jnp.sum(do_ * v, axis=1, keepdims=True)
        dq = dq + p0 * k
        dk = p0 * q
        dv = jnp.sum(q * k, axis=1, keepdims=True) * do_
        for dl in range(1, HG_SUB):
            up = L - dl
            kd, vd = pltpu.roll(k, dl, 0), pltpu.roll(v, dl, 0)
            e = jnp.exp(jnp.where(sub >= dl, b - pltpu.roll(b, dl, 0), NEG))
            a = jnp.sum(q * kd * e, axis=1, keepdims=True)
            p = jnp.sum(do_ * vd, axis=1, keepdims=True) * e
            dq = dq + p * kd
            dkd = p * q
            dbb = dkd * kd
            dv = dv + pltpu.roll(a * do_, up, 0)
            dk = dk + pltpu.roll(dkd, up, 0)
            db = db + dbb - pltpu.roll(dbb, up, 0)
        dq_a[...], dk_a[...], dv_a[...], db_a[...] = dq, dk, dv, db
        vb = v.astype(BF16)
        for i in range(1, L // HG_SUB):
            r0 = i * HG_SUB
            blk = slice(r0, r0 + HG_SUB)
            qt, kt, e1, e2 = _hgrn_offdiag(q, k, b, r0)
            qtb, ktb, dob_i = qt.astype(BF16), kt.astype(BF16), do_[blk, :].astype(BF16)
            a = _nt(qtb, ktb).astype(BF16)
            da = _nt(dob_i, vb).astype(BF16)
            dv_a[...] += _tn(a, dob_i)
            dqt = _nn(da, ktb)
            dkt = _tn(da, qtb)
            dq_a[blk, :] += dqt * e1
            t1, t2 = dqt * qt, dkt * kt
            db_a[blk, :] += t1
            dk_a[...] += dkt * e2
            db_a[...] -= t2
            db_a[r0 - 1:r0, :] += jnp.sum(t2, axis=0, keepdims=True) - jnp.sum(t1, axis=0, keepdims=True)
        dq, dk, dv, db = dq_a[...], dk_a[...], dv_a[...], db_a[...]
        dstn = dst_s[hd]
        dstnb = dstn.astype(BF16)
        bl = b[L - 1:L, :]
        ebl = jnp.exp(bl)
        kdec_e = jnp.exp(bl - b)
        kdec = k * kdec_e
        dbl = jnp.sum(dstn * st, axis=0, keepdims=True) * ebl
        dv = dv + _nt(kdec.astype(BF16), dstnb)
        dkdec = _nn(v.astype(BF16), dstnb)
        dk = dk + dkdec * kdec_e
        dx = dkdec * kdec
        dbl = dbl + jnp.sum(dx, axis=0, keepdims=True)
        db = db - dx + jnp.where(rows == L - 1, dbl, 0.0)
        dst_s[hd] = dstn * ebl + dst_o
        dg = _rcumsum_rows(db)
        dfk = dg / f - dk
        dq_ref[rs, sl] = (dq * _dsilu(qp)).astype(BF16)
        di_ref[rs, sl] = (dv * _dsilu(ip)).astype(BF16)
        df_ref[rs, sl] = (dfk * (1.0 - lb) * sg * (1.0 - sg)).astype(BF16)
        dlb_s[hd] += jnp.sum(dfk * (1.0 - sg), axis=0, keepdims=True)

        if ck == 0:
            @pl.when(pl.program_id(0) == nc // cps - 1)
            def _():
                dl0 = dlb_s[hd] * lb * (1.0 - lb)
                dl_ref[0:1, sl] = dl0
                dl_ref[1:2, sl] = -dl0

    cps = REC_CHUNKS
    rv = lambda j: nc // cps - 1 - j
    hspec = lambda blk: pl.BlockSpec((cps * L, 512), lambda j: (rv(j), blk))
    return _pc(body, name, (nc // cps,),
               [hspec(4), hspec(5), hspec(6), pl.BlockSpec((2, 512), lambda j: (0, 0)),
                pl.BlockSpec((nh, cps, HD, HD), lambda j: (0, rv(j), 0, 0)), hspec(0)],
               [hspec(0), hspec(0), hspec(0), pl.BlockSpec((2, 512), lambda j: (0, 0))],
               [_sds((t, 512), BF16), _sds((t, 512), BF16), _sds((t, 512), BF16), _sds((2, 512))],
               scratch=[pltpu.VMEM((nh, HD, HD), F32), pltpu.VMEM((nh, 1, HD), F32)] + [pltpu.VMEM((nh, L, HD), F32)] * 4)(z0, z0, z0, lbl, ss, do)


def _post0_fwd(hm, hh, z0, na, nb, w, h0, name, bm=512):
    t = h0.shape[0]
    bm = min(bm, t)

    def body(hm_ref, hh_ref, oa_ref, gb_ref, na_ref, nb_ref, w_ref, h_ref, o_ref, y_ref):
        for hd in range(4):
            sl = slice(hd * HD, (hd + 1) * HD)
            pa = _sigmoid(oa_ref[:, sl]) * hm_ref[:, sl]
            y_ref[:, sl] = (pa * _rstd(pa) * na_ref[:, sl]).astype(BF16)
            xb = hh_ref[:, sl]
            y_ref[:, 512 + hd * HD:512 + (hd + 1) * HD] = (xb * _rstd(xb) * nb_ref[:, sl] * _silu(gb_ref[:, sl])).astype(BF16)
        o_ref[...] = h_ref[...] + _nn(y_ref[...], w_ref[...])

    row = lambda wd, c: pl.BlockSpec((bm, wd), lambda i: (i, c))
    vec = lambda wd: pl.BlockSpec((1, wd), lambda i: (0, 0))
    return _pc(body, name, (t // bm,),
               [row(512, 0), row(512, 0), row(512, 3), row(512, 7), vec(512), vec(512),
                pl.BlockSpec((D, D), lambda i: (0, 0)), row(D, 0)],
               [row(D, 0), row(D, 0)], [_sds((t, D)), _sds((t, D), BF16)])(hm, hh, z0, z0, na, nb, w, h0)


def _post0_bwd(dh1, w, hm, hh, z0, na, nb, name, bm=512):
    t = dh1.shape[0]
    bm = min(bm, t)

    def body(dh_ref, w_ref, hm_ref, hh_ref, oa_ref, gb_ref, na_ref, nb_ref, dhm_ref, dhh_ref, doa_ref, dgb_ref, dna_ref, dnb_ref):
        @pl.when(pl.program_id(0) == 0)
        def _():
            dna_ref[...] = jnp.zeros_like(dna_ref)
            dnb_ref[...] = jnp.zeros_like(dnb_ref)

        dy = _nt(dh_ref[...].astype(BF16), w_ref[...])
        for hd in range(4):
            sl = slice(hd * HD, (hd + 1) * HD)
            hm_, oa = hm_ref[:, sl], oa_ref[:, sl]
            sg = _sigmoid(oa)
            dpa, dgr = _rms_bwd(dy[:, sl], sg * hm_, na_ref[:, sl])
            dna_ref[:, sl] += jnp.sum(dgr, axis=0, keepdims=True)
            doa_ref[:, sl] = (dpa * hm_ * sg * (1.0 - sg)).astype(BF16)
            dhm_ref[:, sl] = dpa * sg
            xb, gb, nbv = hh_ref[:, sl], gb_ref[:, sl], nb_ref[:, sl]
            dyb = dy[:, 512 + hd * HD:512 + (hd + 1) * HD]
            dgb_ref[:, sl] = (dyb * (xb * _rstd(xb) * nbv) * _dsilu(gb)).astype(BF16)
            dxb, dgr2 = _rms_bwd(dyb * _silu(gb), xb, nbv)
            dnb_ref[:, sl] += jnp.sum(dgr2, axis=0, keepdims=True)
            dhh_ref[:, sl] = dxb

    row = lambda wd, c: pl.BlockSpec((bm, wd), lambda i: (i, c))
    vec = lambda wd: pl.BlockSpec((1, wd), lambda i: (0, 0))
    return _pc(body, name, (t // bm,),
               [row(D, 0), pl.BlockSpec((D, D), lambda i: (0, 0)), row(512, 0), row(512, 0), row(512, 3), row(512, 7),
                vec(512), vec(512)],
               [row(512, 0), row(512, 0), row(512, 0), row(512, 0), vec(512), vec(512)],
               [_sds((t, 512)), _sds((t, 512)), _sds((t, 512), BF16), _sds((t, 512), BF16), _sds((1, 512)), _sds((1, 512))],
               )(dh1, w, hm, hh, z0, z0, na, nb)


def _memkv_fwd(mem, g, wkv_s, name):
    m = mem.shape[0]

    def body(x_ref, g_ref, w_ref, kv_ref, mn_ref):
        x = x_ref[...]
        mn = (x * _rstd(x) * g_ref[...]).astype(BF16)
        mn_ref[...] = mn
        kv_ref[...] = _nn(mn, w_ref[...])

    return _pc(body, name, (4,),
               [pl.BlockSpec((m, D), lambda k: (0, 0)), pl.BlockSpec((1, D), lambda k: (0, 0)),
                pl.BlockSpec((None, D, 512), lambda k: (k, 0, 0))],
               [pl.BlockSpec((m, 512), lambda k: (0, k)), pl.BlockSpec((m, D), lambda k: (0, 0))],
               [_sds((m, 2048)), _sds((m, D), BF16)])(mem, g, wkv_s)


def _memkv_bwd(dkv, wkv_s, mem, g, name):
    m = mem.shape[0]

    def body(d_ref, w_ref, x_ref, g_ref, dg_ref, acc):
        k = pl.program_id(0)

        @pl.when(k == 0)
        def _():
            acc[...] = jnp.zeros_like(acc)

        acc[...] += _nt(d_ref[...].astype(BF16), w_ref[...])

        @pl.when(k == 3)
        def _():
            _, dgr = _rms_bwd(acc[...], x_ref[...], g_ref[...])
            dg_ref[...] = jnp.sum(dgr, axis=0, keepdims=True)

    return _pc(body, name, (4,),
               [pl.BlockSpec((m, 512), lambda k: (0, k)), pl.BlockSpec((None, D, 512), lambda k: (k, 0, 0)),
                pl.BlockSpec((m, D), lambda k: (0, 0)), pl.BlockSpec((1, D), lambda k: (0, 0))],
               pl.BlockSpec((1, D), lambda k: (0, 0)), _sds((1, D)), scratch=[pltpu.VMEM((m, D), F32)])(dkv, wkv_s, mem, g)


def _xattn_probs(qh, kh):
    s = _nt(qh, kh) * (XD ** -0.5)
    p = jnp.exp(s - jnp.max(s, axis=1, keepdims=True))
    return p / jnp.sum(p, axis=1, keepdims=True)


def _xattn_fwd(q, kv, wo, h1, name, bm=512):
    t, m = q.shape[0], kv.shape[0]
    bm = min(bm, t)

    def body(q_ref, k_ref, v_ref, w_ref, h_ref, out_ref, o_ref):
        for hd in range(D // XD):
            sl = slice(hd * XD, (hd + 1) * XD)
            p = _xattn_probs(q_ref[:, sl].astype(BF16), k_ref[:, sl].astype(BF16))
            o_ref[:, sl] = _nn(p.astype(BF16), v_ref[:, sl].astype(BF16)).astype(BF16)
        out_ref[...] = h_ref[...] + _nn(o_ref[...], w_ref[...])

    row = pl.BlockSpec((bm, D), lambda i: (i, 0))
    return _pc(body, name, (t // bm,),
               [row, pl.BlockSpec((m, D), lambda i: (0, 0)), pl.BlockSpec((m, D), lambda i: (0, 1)),
                pl.BlockSpec((D, D), lambda i: (0, 0)), row],
               [row, row], [_sds((t, D)), _sds((t, D), BF16)])(q, kv, kv, wo, h1)


def _xattn_bwd(dh2, q, kv, wo, name, bm=512):
    t, m = q.shape[0], kv.shape[0]
    bm = min(bm, t)

    def body(dh_ref, q_ref, k_ref, v_ref, w_ref, dq_ref, dkv_ref):
        @pl.when(pl.program_id(0) == 0)
        def _():
            dkv_ref[...] = jnp.zeros_like(dkv_ref)

        d_o = _nt(dh_ref[...].astype(BF16), w_ref[...])
        for hd in range(D // XD):
            sl = slice(hd * XD, (hd + 1) * XD)
            qh, kh, vh = q_ref[:, sl].astype(BF16), k_ref[:, sl].astype(BF16), v_ref[:, sl].astype(BF16)
            p = _xattn_probs(qh, kh)
            dob = d_o[:, sl].astype(BF16)
            dp = _nt(dob, vh)
            dkv_ref[:, D + hd * XD:D + (hd + 1) * XD] += _tn(p.astype(BF16), dob)
            ds = (p * (dp - jnp.sum(dp * p, axis=1, keepdims=True)) * (XD ** -0.5)).astype(BF16)
            dq_ref[:, sl] = _nn(ds, kh).astype(BF16)
            dkv_ref[:, sl] += _tn(ds, qh)

    row = pl.BlockSpec((bm, D), lambda i: (i, 0))
    return _pc(body, name, (t // bm,),
               [row, row, pl.BlockSpec((m, D), lambda i: (0, 0)), pl.BlockSpec((m, D), lambda i: (0, 1)),
                pl.BlockSpec((D, D), lambda i: (0, 0))],
               [row, pl.BlockSpec((m, 2 * D), lambda i: (0, 0))],
               [_sds((t, D), BF16), _sds((m, 2 * D))])(dh2, q, kv, kv, wo)


NH1 = 8
FOX_BM = 512
FOX_BQ = 512
FOX_BK = 512
FOX_HEADS_PER_STEP = 4


def _foxprep_fwd(z1, qg, kg, fbp, name):
    t = z1.shape[0]
    bm = min(FOX_BM, t)

    def body(q_ref, k_ref, v_ref, f_ref, qg_ref, kg_ref, fb_ref, qn_ref, kn_ref, vb_ref, c_ref, carry):
        @pl.when(pl.program_id(0) == 0)
        def _():
            carry[...] = jnp.zeros_like(carry)

        for hd in range(NH1):
            sl = slice(hd * HD, (hd + 1) * HD)
            x = q_ref[:, sl]
            qn_ref[:, sl] = (x * _rstd(x) * qg_ref[...] * FOX_QSCALE).astype(BF16)
            x = k_ref[:, sl]
            kn_ref[:, sl] = (x * _rstd(x) * kg_ref[...]).astype(BF16)
        vb_ref[...] = v_ref[...].astype(BF16)
        c = carry[...] + _cumsum_rows(_log_sigmoid(f_ref[...] + fb_ref[...]))
        c_ref[...] = c
        carry[...] = c[bm - 1:bm, :]

    row = lambda c: pl.BlockSpec((bm, D), lambda i: (i, c))
    lane = pl.BlockSpec((bm, HD), lambda i: (i, 4 * D // HD))
    vec = pl.BlockSpec((1, HD), lambda i: (0, 0))
    return _pc(body, name, (t // bm,), [row(0), row(1), row(2), lane, vec, vec, vec],
               [row(0), row(0), row(0), pl.BlockSpec((bm, HD), lambda i: (i, 0))],
               [_sds((t, D), BF16), _sds((t, D), BF16), _sds((t, D), BF16), _sds((t, HD))],
               scratch=[pltpu.VMEM((1, HD), F32)])(z1, z1, z1, z1, qg, kg, fbp)


def _foxprep_bwd(dqn, dkn, dv, dgate, z1, qg, kg, fbp, dc, name):
    t = z1.shape[0]
    bm = min(FOX_BM, t)
    nb = t // bm

    def body(dqn_ref, dkn_ref, dv_ref, dgt_ref, q_ref, k_ref, f_ref, qg_ref, kg_ref, fb_ref, dc_ref,
             dz_ref, dqg_ref, dkg_ref, dfb_ref, carry):
        @pl.when(pl.program_id(0) == 0)
        def _():
            carry[...] = jnp.zeros_like(carry)
            dqg_ref[...] = jnp.zeros_like(dqg_ref)
            dkg_ref[...] = jnp.zeros_like(dkg_ref)
            dfb_ref[...] = jnp.zeros_like(dfb_ref)

        for hd in range(NH1):
            sl = slice(hd * HD, (hd + 1) * HD)
            dx, dgr = _rms_bwd(dqn_ref[:, sl] * (HD ** -0.5), q_ref[:, sl], qg_ref[...])
            dz_ref[:, sl] = dx.astype(BF16)
            dqg_ref[...] += jnp.sum(dgr, axis=0, keepdims=True)
            dx, dgr = _rms_bwd(dkn_ref[:, sl], k_ref[:, sl], kg_ref[...])
            dz_ref[:, D + hd * HD:D + (hd + 1) * HD] = dx.astype(BF16)
            dkg_ref[...] += jnp.sum(dgr, axis=0, keepdims=True)
        dz_ref[:, 2 * D:3 * D] = dv_ref[...].astype(BF16)
        dz_ref[:, 3 * D:4 * D] = dgt_ref[...]
        dc_ = dc_ref[...]
        dlogf = _rcumsum_rows(dc_) + carry[...]
        carry[...] += jnp.sum(dc_, axis=0, keepdims=True)
        lanes = lax.broadcasted_iota(jnp.int32, dc_.shape, 1)
        df = jnp.where(lanes < NH1, dlogf * (1.0 - _sigmoid(f_ref[...] + fb_ref[...])), 0.0)
        dz_ref[:, GATE0:GATE0 + HD] = df.astype(BF16)
        dfb_ref[...] += jnp.sum(df, axis=0, keepdims=True)

    rv = lambda i: nb - 1 - i
    row = lambda c: pl.BlockSpec((bm, D), lambda i: (rv(i), c))
    lane = lambda c: pl.BlockSpec((bm, HD), lambda i: (rv(i), c))
    vec = pl.BlockSpec((1, HD), lambda i: (0, 0))
    return _pc(body, name, (nb,), [row(0), row(0), row(0), row(0), row(0), row(1), lane(4 * D // HD), vec, vec, vec, lane(0)],
               [pl.BlockSpec((bm, ZW), lambda i: (rv(i), 0)), vec, vec, vec],
               [_sds((t, ZW), BF16), _sds((1, HD)), _sds((1, HD)), _sds((1, HD))],
               scratch=[pltpu.VMEM((1, HD), F32)])(dqn, dkn, dv, dgate, z1, z1, z1, qg, kg, fbp, dc)


LOG2E = 1.4426950408889634
FOX_QSCALE = HD ** -0.5 * LOG2E


def _fox_steps(t, bq, bk, k_major):
    nq, nk = t // bq, t // bk
    pairs = [(i, j) for i in range(nq) for j in range(nk) if j * bk < (i + 1) * bq]
    if k_major:
        pairs.sort(key=lambda p: (p[1], p[0]))
    outer = [p[1] if k_major else p[0] for p in pairs]
    n = len(pairs)
    flags = [(n_ == 0 or outer[n_] != outer[n_ - 1]) + 2 * (n_ == n - 1 or outer[n_] != outer[n_ + 1])
             + 4 * (not (j + 1) * bk <= i * bq + 1) for n_, (i, j) in enumerate(pairs)]
    as_i32 = lambda v: jnp.asarray(v, jnp.int32)
    return as_i32([p[0] for p in pairs]), as_i32([p[1] for p in pairs]), as_i32(flags)


def _fox_step_info(qi_ref, kj_ref, fl_ref):
    s = pl.program_id(1)
    fl = fl_ref[s]
    return qi_ref[s], kj_ref[s], (fl & 1) != 0, (fl & 2) != 0, (fl & 4) != 0


def _fox_call(body, name, tables, in_specs, out_specs, out_shape, scratch):
    grid_spec = pltpu.PrefetchScalarGridSpec(num_scalar_prefetch=3, grid=(NH1 // FOX_HEADS_PER_STEP, tables[0].shape[0]),
                                             in_specs=in_specs, out_specs=out_specs, scratch_shapes=scratch)
    return pl.pallas_call(body, name=name, grid_spec=grid_spec, out_shape=out_shape,
                          compiler_params=pltpu.CompilerParams(dimension_semantics=("arbitrary", "arbitrary"),
                                                               vmem_limit_bytes=VMEM_LIMIT_V7X))


def _fox_lane_tiles(x):
    return [x[:, c0:c0 + HD] for c0 in range(0, x.shape[1], HD)]


def _fox_masked_scores(q, k, ck, i, j, bq, bk, masked):
    s = _nt(q, k) - ck
    if masked:
        rows = i * bq + lax.broadcasted_iota(jnp.int32, s.shape, 0)
        cols = j * bk + lax.broadcasted_iota(jnp.int32, s.shape, 1)
        s = jnp.where(cols <= rows, s, NEG)
    return s


def _fox_specs(bq, bk, G):
    qspec = pl.BlockSpec((bq, G * HD), lambda h, s, qi, kj, fl: (qi[s], h))
    kspec = pl.BlockSpec((bk, G * HD), lambda h, s, qi, kj, fl: (kj[s], h))
    cspec = pl.BlockSpec((G, 1, bk), lambda h, s, qi, kj, fl: (h, 0, kj[s]))
    colspec = pl.BlockSpec((G, bq, 1), lambda h, s, qi, kj, fl: (h, qi[s], 0))
    return qspec, kspec, cspec, colspec


def _fox_rowmax(qn, kn, crow, name):
    t = qn.shape[0]
    bq, bk, G = min(FOX_BQ, t), min(2 * FOX_BK, t), FOX_HEADS_PER_STEP
    tables = _fox_steps(t, bq, bk, k_major=False)

    def body(qi_ref, kj_ref, fl_ref, q_ref, k_ref, ck_ref, m_ref, *mp):
        i, j, first, last, diag = _fox_step_info(qi_ref, kj_ref, fl_ref)

        @pl.when(first)
        def _():
            for g in range(G):
                mp[g][...] = jnp.full_like(mp[g], NEG)

        def step(masked):
            for g in range(G):
                sl = slice(g * HD, (g + 1) * HD)
                s = _fox_masked_scores(q_ref[:, sl], k_ref[:, sl], ck_ref[g], i, j, bq, bk, masked)
                m = mp[g][...]
                for tile in _fox_lane_tiles(s):
                    m = jnp.maximum(m, tile)
                mp[g][...] = m

        pl.when(jnp.logical_not(diag))(lambda: step(False))
        pl.when(diag)(lambda: step(True))

        @pl.when(last)
        def _():
            for g in range(G):
                m_ref[g] = jnp.max(mp[g][...], axis=1, keepdims=True)

    qspec, kspec, cspec, colspec = _fox_specs(bq, bk, G)
    return _fox_call(body, name, tables, [qspec, kspec, cspec], colspec, _sds((NH1, t, 1)),
                     [pltpu.VMEM((bq, HD), F32)] * G)(*tables, qn, kn, crow)


def _fox_fwd(qn, kn, vb, crow, m, name):
    t = qn.shape[0]
    bq, bk, G = min(FOX_BQ, t), min(FOX_BK, t), FOX_HEADS_PER_STEP
    tables = _fox_steps(t, bq, bk, k_major=False)

    def body(qi_ref, kj_ref, fl_ref, q_ref, k_ref, v_ref, ck_ref, m_ref, o_ref, lse_ref, *scr):
        i, j, first, last, diag = _fox_step_info(qi_ref, kj_ref, fl_ref)
        lp, acc = scr[:G], scr[G:]

        @pl.when(first)
        def _():
            for g in range(G):
                lp[g][...] = jnp.zeros_like(lp[g])
                acc[g][...] = jnp.zeros_like(acc[g])

        def step(masked):
            for g in range(G):
                sl = slice(g * HD, (g + 1) * HD)
                s = _fox_masked_scores(q_ref[:, sl], k_ref[:, sl], ck_ref[g], i, j, bq, bk, masked)
                p = jnp.exp2(s - m_ref[g])
                l = lp[g][...]
                for tile in _fox_lane_tiles(p):
                    l = l + tile
                lp[g][...] = l
                acc[g][...] += _nn(p.astype(BF16), v_ref[:, sl])

        pl.when(jnp.logical_not(diag))(lambda: step(False))
        pl.when(diag)(lambda: step(True))

        @pl.when(last)
        def _():
            for g in range(G):
                l = jnp.sum(lp[g][...], axis=1, keepdims=True)
                o_ref[:, g * HD:(g + 1) * HD] = acc[g][...] / l
                lse_ref[g] = m_ref[g] + jnp.log2(l)

    qspec, kspec, cspec, colspec = _fox_specs(bq, bk, G)
    return _fox_call(body, name, tables, [qspec, kspec, kspec, cspec, colspec], [qspec, colspec],
                     [_sds((t, D)), _sds((NH1, t, 1))], [pltpu.VMEM((bq, HD), F32)] * (2 * G))(*tables, qn, kn, vb, crow, m)


def _fox_bwd(qn, kn, vb, crow, lse, delta, do, name):
    t = qn.shape[0]
    bq, bk, G = min(FOX_BQ, t), min(FOX_BK, t), FOX_HEADS_PER_STEP
    tables = _fox_steps(t, bq, bk, k_major=True)

    def body(qi_ref, kj_ref, fl_ref, q_ref, k_ref, v_ref, ck_ref, lse_ref, dl_ref, do_ref, dq_ref, dk_ref, dv_ref, dc_ref, dcq_ref,
             dk_s, dv_s, dc_s):
        i, j, first, last, diag = _fox_step_info(qi_ref, kj_ref, fl_ref)

        @pl.when(first)
        def _():
            dk_s[...] = jnp.zeros_like(dk_s)
            dv_s[...] = jnp.zeros_like(dv_s)
            dc_s[...] = jnp.zeros_like(dc_s)

        @pl.when(pl.program_id(1) == 0)
        def _():
            dq_ref[...] = jnp.zeros_like(dq_ref)
            dcq_ref[...] = jnp.zeros_like(dcq_ref)

        def step(masked):
            rows = pl.ds(pl.multiple_of(i * bq, bq), bq)
            for g in range(G):
                sl = slice(g * HD, (g + 1) * HD)
                q, k = q_ref[:, sl], k_ref[:, sl]
                s = _fox_masked_scores(q, k, ck_ref[g], i, j, bq, bk, masked)
                p = jnp.exp2(s - lse_ref[g])
                dob = do_ref[:, sl]
                dv_s[:, sl] += _tn(p.astype(BF16), dob)
                ds = p * (_nt(dob, v_ref[:, sl]) - dl_ref[g])
                dsb = ds.astype(BF16)
                dq_ref[rows, sl] += _nn(dsb, k)
                dk_s[:, sl] += _tn(dsb, q)
                dc_s[g] -= jnp.sum(ds, axis=0, keepdims=True)
                part_sum = dcq_ref[g, rows, :]
                for tile in _fox_lane_tiles(ds):
                    part_sum = part_sum + tile
                dcq_ref[g, rows, :] = part_sum

        pl.when(jnp.logical_not(diag))(lambda: step(False))
        pl.when(diag)(lambda: step(True))

        @pl.when(last)
        def _():
            dk_ref[...] = dk_s[...] * (1.0 / LOG2E)
            dv_ref[...] = dv_s[...]
            dc_ref[...] = dc_s[...]

    qspec, kspec, cspec, colspec = _fox_specs(bq, bk, G)
    return _fox_call(
        body, name, tables, [qspec, kspec, kspec, cspec, colspec, colspec, qspec],
        [pl.BlockSpec((t, G * HD), lambda h, s, qi, kj, fl: (0, h)), kspec, kspec, cspec,
         pl.BlockSpec((G, t, HD), lambda h, s, qi, kj, fl: (h, 0, 0))],
        [_sds((t, D)), _sds((t, D)), _sds((t, D)), _sds((NH1, 1, t)), _sds((NH1, t, HD))],
        [pltpu.VMEM((bk, G * HD), F32), pltpu.VMEM((bk, G * HD), F32), pltpu.VMEM((G, 1, bk), F32)],
    )(*tables, qn, kn, vb, crow, lse, delta, do)


def _post1_fwd(o, z1, w, h3, name, bm=512):
    t = o.shape[0]
    bm = min(bm, t)

    def body(o_ref, g_ref, w_ref, h_ref, out_ref, og_ref):
        og_ref[...] = (o_ref[...] * _sigmoid(g_ref[...])).astype(BF16)
        out_ref[...] = h_ref[...] + _nn(og_ref[...], w_ref[...])

    row = lambda c: pl.BlockSpec((bm, D), lambda i: (i, c))
    return _pc(body, name, (t // bm,), [row(0), row(3), pl.BlockSpec((D, D), lambda i: (0, 0)), row(0)],
               [row(0), row(0)], [_sds((t, D)), _sds((t, D), BF16)])(o, z1, w, h3)


def _post1_bwd(dh4, w, o, z1, name, bm=512):
    t = o.shape[0]
    bm = min(bm, t)

    def body(dh_ref, w_ref, o_ref, g_ref, do_ref, dg_ref, dl_ref):
        d_og = _nt(dh_ref[...].astype(BF16), w_ref[...])
        o_, sg = o_ref[...], _sigmoid(g_ref[...])
        dob = (d_og * sg).astype(BF16)
        do_ref[...] = dob
        dg_ref[...] = (d_og * o_ * sg * (1.0 - sg)).astype(BF16)
        prod = dob.astype(F32) * o_
        for hd in range(NH1):
            dl_ref[hd] = jnp.sum(prod[:, hd * HD:(hd + 1) * HD], axis=1, keepdims=True)

    row = lambda c: pl.BlockSpec((bm, D), lambda i: (i, c))
    return _pc(body, name, (t // bm,), [row(0), pl.BlockSpec((D, D), lambda i: (0, 0)), row(0), row(3)],
               [row(0), row(0), pl.BlockSpec((NH1, bm, 1), lambda i: (0, i, 0))],
               [_sds((t, D), BF16), _sds((t, D), BF16), _sds((NH1, t, 1))])(dh4, w, o, z1)


def _final(h, g, tgt, name, bm=512):
    t = h.shape[0]
    bm = min(bm, t)

    def body(h_ref, g_ref, t_ref, l_ref, dh_ref, dg_ref):
        @pl.when(pl.program_id(0) == 0)
        def _():
            l_ref[...] = jnp.zeros_like(l_ref)
            dg_ref[...] = jnp.zeros_like(dg_ref)

        x, gv = h_ref[...], g_ref[...]
        r = _rstd(x)
        xh = x * r
        e = xh * gv - t_ref[...]
        l_ref[...] += 0.5 * jnp.sum(jnp.mean(e * e, axis=1, keepdims=True), axis=0, keepdims=True)
        dy = e * (1.0 / D)
        dg_ref[...] += jnp.sum(dy * xh, axis=0, keepdims=True)
        dxh = dy * gv
        dh_ref[...] = r * (dxh - xh * jnp.mean(dxh * xh, axis=1, keepdims=True))

    row = pl.BlockSpec((bm, D), lambda i: (i, 0))
    vec = pl.BlockSpec((1, D), lambda i: (0, 0))
    return _pc(body, name, (t // bm,), [row, vec, row], [pl.BlockSpec((1, HD), lambda i: (0, 0)), row, vec],
               [_sds((1, HD)), _sds((t, D)), _sds((1, D))])(h, g, tgt)


def _adam(w, g, m, v, name):
    r, c = w.shape
    br = min(r, 256)

    def body(w_ref, g_ref, m_ref, v_ref, d_ref, mo_ref, vo_ref):
        gv = g_ref[...]
        mn = ADAM_B1 * m_ref[...] + (1.0 - ADAM_B1) * gv
        vn = ADAM_B2 * v_ref[...] + (1.0 - ADAM_B2) * jnp.square(gv)
        m_hat = mn / (1.0 - ADAM_B1 ** ADAM_STEP)
        v_hat = vn / (1.0 - ADAM_B2 ** ADAM_STEP)
        d_ref[...] = -ADAM_LR * (m_hat / (jnp.sqrt(v_hat) + ADAM_EPS) + ADAM_WD * w_ref[...])
        mo_ref[...] = mn
        vo_ref[...] = vn

    blk = pl.BlockSpec((br, c), lambda i: (i, 0))
    return _pc(body, name, (r // br,), [blk] * 4, [blk] * 3, [_sds((r, c))] * 3)(w, g, m, v)


ZW = 4224
GATE0 = 4096


def _pack_w_in0(w):
    return jnp.concatenate([w[:, :2048], w[:, 2056:], w[:, 2048:2056], jnp.zeros((w.shape[0], ZW - 4104), w.dtype)], axis=1)


def _unpack_w_in0(g):
    return jnp.concatenate([g[:, :2048], g[:, GATE0:GATE0 + 8], g[:, 2048:GATE0]], axis=1)


def _pack_w_in1(w):
    return jnp.concatenate([w, jnp.zeros((w.shape[0], ZW - 4104), w.dtype)], axis=1)


def _unpack_w_in1(g):
    return g[:, :4104]


def _local_step(x, mem, tgt, W, S, late_weights=None, grads_hook=None):
    t = x.shape[0]
    row = lambda v: v.reshape(1, -1)
    G = {}

    z0, u0 = _norm_mm(x, S["norm_mix_g"][0:1], W["w_in0"], "in0_fwd")
    qk = _conv_fwd(z0, S["conv_w"], "conv_fwd")
    g8 = z0[:, GATE0:GATE0 + 8]
    gates3 = jnp.stack([g8[:, :4].T, g8[:, 4:].T], axis=-1)
    gb = S["gate_b"]
    bias3 = jnp.stack([gb[0, :4], gb[0, 4:]], axis=-1)[:, None, :]
    hm, cs, ns, ms = _mlstm_fwd(qk, z0, gates3, bias3, "mlstm_fwd")
    hh, ss = _hgrn_fwd(z0, S["lb_logits"], "hgrn_fwd")
    if late_weights is not None:
        W = {**W, **late_weights(hh)}
    kv, mn = _memkv_fwd(mem, row(S["mem_norm_g"]), W["wkv_s"], "memkv_fwd")
    h1, y0 = _post0_fwd(hm, hh, z0, S["mlstm_norm_g"], S["hgrn_norm_g"], W["w_out0"], x, "post0_fwd")

    def xattn_mlp_fwd(h, l):
        q, ux = _norm_mm(h, S["norm_xattn_g"][l:l + 1], W["wq"][l], f"xq{l}_fwd")
        h2, ox = _xattn_fwd(q, kv, W["wo"][l], h, f"xattn{l}_fwd")
        h3, a, um = _mlp_fwd(h2, S["norm_mlp_g"][l:l + 1], W["w1s"], W["w2"], l, f"mlp{l}_fwd")
        return h3, (h, q, ux, ox, h2, a, um)

    h3, sv0 = xattn_mlp_fwd(h1, 0)
    z1, u1 = _norm_mm(h3, S["norm_mix_g"][1:2], W["w_in1"], "in1_fwd")
    fbp = jnp.pad(S["c_fgate_b"], ((0, 0), (0, HD - NH1)))
    qn, kn, vb, c = _foxprep_fwd(z1, S["c_qnorm_g"], S["c_knorm_g"], fbp, "foxprep_fwd")
    crow = (c[:, :NH1] * LOG2E).T[:, None, :]
    o1, lse = _fox_fwd(qn, kn, vb, crow, _fox_rowmax(qn, kn, crow, "fox_rowmax"), "fox_fwd")
    h4, og = _post1_fwd(o1, z1, W["w_out1"], h3, "post1_fwd")
    h6, sv1 = xattn_mlp_fwd(h4, 1)
    lossp, dh, G["final_norm_g"] = _final(h6, row(S["final_norm_g"]), tgt, "final")

    grads_ready, grads_next = grads_hook if grads_hook is not None else ((lambda stage, grads: 0.0), (lambda stage, after: 0.0))
    dkv = None
    dgx, dgm, dwq, dwo, dw1, dw2 = [None, None], [None, None], [None, None], [None, None], [None, None], [None, None]

    def xattn_mlp_bwd(dh, l, sv, tok=0.0):
        nonlocal dkv
        h, q, ux, ox, h2, a, um = sv
        dh2, da, r, dgm[l] = _mlp_bwd(dh, a, W["w1s"], W["w2"], l, h2, S["norm_mlp_g"][l:l + 1] + tok, f"mlp{l}_bwd")
        dw1[l] = _mm_tn(um, da, f"mlp{l}_dw1", col_chips=NCHIP)
        dw2[l] = _mm_tn(r, dh, f"mlp{l}_dw2")
        dq, dkv_l = _xattn_bwd(dh2, q, kv, W["wo"][l], f"xattn{l}_bwd")
        dkv = dkv_l if dkv is None else dkv + dkv_l
        dwo[l] = _mm_tn(ox, dh2, f"xattn{l}_dwo")
        dwq[l] = _mm_tn(ux, dq, f"xattn{l}_dwq")
        dh1, dgx[l] = _bwd_in(dq, W["wq"][l], h, S["norm_xattn_g"][l:l + 1], dh2, f"xq{l}_bwd")
        return dh1

    dh4 = xattn_mlp_bwd(dh, 1, sv1)
    do, dgate, delta = _post1_bwd(dh4, W["w_out1"], o1, z1, "post1_bwd")
    G["w_out1"] = _mm_tn(og, dh4, "post1_dw")
    dqn, dkn, dv1, dcrow, dcq = _fox_bwd(qn, kn, vb, crow, lse, delta, do, "fox_bwd")
    dc = jnp.pad((dcrow[:, 0, :] + jnp.sum(dcq, axis=-1)).T, ((0, 0), (0, HD - NH1)))
    dz1, G["c_qnorm_g"], G["c_knorm_g"], dfb = _foxprep_bwd(
        dqn, dkn, dv1, dgate, z1, S["c_qnorm_g"], S["c_knorm_g"], fbp, dc, "foxprep_bwd")
    G["c_fgate_b"] = dfb[:, :NH1]
    G["w_in1"] = _mm_tn(u1, dz1, "in1_dw")
    tok = grads_ready("layer1", dict(w_out=G["w_out1"], w_in=G["w_in1"], wq=dwq[1], wo=dwo[1], w1=dw1[1], w2=dw2[1]))
    dh3, dgmix1 = _bwd_in(dz1, W["w_in1"], h3, S["norm_mix_g"][1:2] + tok, dh4, "in1_bwd")
    dh1 = xattn_mlp_bwd(dh3, 0, sv0, grads_next("layer1", dh3))

    G["wkv"] = _mm_tn(mn, dkv, "memkv_dw", col_chips=NCHIP)
    G["mem_norm_g"] = _memkv_bwd(dkv, W["wkv_s"], mem, row(S["mem_norm_g"]), "memkv_bwd")
    G["w_out0"] = _mm_tn(y0, dh1, "post0_dw")
    tok = grads_ready("layer0", dict(wq=dwq[0], wo=dwo[0], w1=dw1[0], w2=dw2[0], wkv=G["wkv"], w_out=G["w_out0"]))
    dhm, dhh, doa, dgb, G["mlstm_norm_g"], G["hgrn_norm_g"] = _post0_bwd(
        dh1, W["w_out0"], hm, hh, z0, S["mlstm_norm_g"] + tok, S["hgrn_norm_g"], "post0_bwd")
    dqka, dva, dgates3 = _mlstm_bwd(qk, z0, gates3, bias3 + grads_next("layer0", dhm), cs, ns, ms, dhm, "mlstm_bwd")
    dqb, dfb0, dib, G["lb_logits"] = _hgrn_bwd(z0, S["lb_logits"], ss, dhh, "hgrn_bwd")
    duc, G["conv_w"] = _conv_bwd(z0, S["conv_w"], dqka, "conv_bwd")
    dg8 = jnp.concatenate([dgates3[:, :, 0].T, dgates3[:, :, 1].T], axis=1)
    G["gate_b"] = jnp.sum(dg8, axis=0, keepdims=True)
    dz0 = jnp.concatenate([duc, dva.astype(BF16), doa, dqb, dfb0, dib, dgb,
                           jnp.pad(dg8, ((0, 0), (0, HD - 8))).astype(BF16)], axis=1)
    G["w_in0"] = _mm_tn(u0, dz0, "in0_dw")
    tok = grads_ready("in0", dict(w_in=G["w_in0"]))
    dx, dgmix0 = _bwd_in(dz0, W["w_in0"], x, S["norm_mix_g"][0:1] + tok, dh1, "in0_bwd")

    G["norm_mix_g"] = jnp.concatenate([dgmix0, dgmix1], axis=0)
    G["norm_xattn_g"] = jnp.concatenate(dgx, axis=0)
    G["norm_mlp_g"] = jnp.concatenate(dgm, axis=0)
    G["wq"], G["wo"], G["w1"], G["w2"] = dwq, dwo, dw1, dw2
    return lossp[0, 0], dx, G


ANY = pl.BlockSpec(memory_space=pl.ANY)
NCHIP = 4


def _place():
    x, y, c = lax.axis_index("x"), lax.axis_index("y"), lax.axis_index("c")
    return x, y, c, [(1 - x, y), (x, 1 - y), (1 - x, 1 - y)]


def _comm_call(body, name, ins, out_shapes, sems):
    return pl.pallas_call(body, name=name, in_specs=[ANY] * len(ins), out_specs=[ANY] * len(out_shapes),
                          out_shape=out_shapes, scratch_shapes=sems)(*ins)


HBM = pl.BlockSpec(memory_space=pltpu.HBM)
SEM = pl.BlockSpec(memory_space=pltpu.SEMAPHORE)
DATAFLOW = pltpu.SideEffectType.DATAFLOW_SIDE_EFFECTING


def _half_rows(r, cc):
    return pl.ds(pl.multiple_of(cc * (r // 2), r // 2), r // 2)


def _gather_start(arrs, after, name):
    n = len(arrs)

    def body(*refs):
        ins, lands = refs[:n], refs[n:2 * n]
        send, recv, token = refs[2 * n + 1], refs[2 * n + 2], refs[-1]
        x, y, c, chips = _place()
        me = 2 * x + y
        for a in range(n):
            rows = _half_rows(arrs[a].shape[0], c)
            for k, (px, py) in enumerate(chips):
                pltpu.make_async_remote_copy(src_ref=ins[a].at[rows], dst_ref=lands[a].at[me, rows], send_sem=send.at[3 * a + k],
                                             recv_sem=recv.at[3 * a + k], device_id=(px, py, c), device_id_type=MESH).start()
        token[...] = jnp.zeros_like(token)

    hbm = lambda v: pltpu.with_memory_space_constraint(v, pltpu.HBM)
    land_shapes = [((NCHIP,) + a.shape, a.dtype) for a in arrs]
    out = pl.pallas_call(
        body, name=name,
        out_shape=(pltpu.SemaphoreType.DMA((3 * n,)), pltpu.SemaphoreType.DMA((3 * n,)), *[pltpu.HBM(a.shape, a.dtype) for a in arrs],
                   *[pltpu.HBM(s, d) for s, d in land_shapes], _sds((8, HD))),
        in_specs=[HBM] * (2 * n) + [ANY], out_specs=(SEM, SEM, *[HBM] * (2 * n), pl.BlockSpec(memory_space=pltpu.VMEM)),
        input_output_aliases={i: 2 + i for i in range(2 * n)},
        compiler_params=pltpu.CompilerParams(has_side_effects=DATAFLOW),
    )(*[hbm(a) for a in arrs], *[hbm(lax.empty(s, d)) for s, d in land_shapes], after)
    return out[0], out[1], list(out[2:2 + n]), list(out[2 + n:2 + 2 * n]), out[-1]


def _gather_wait(send, recv, srcs, lands, after, name):
    n = len(srcs)

    def body(*refs):
        ins, lands_ = refs[:n], refs[n:2 * n]
        send_, recv_ = refs[2 * n], refs[2 * n + 1]
        x, y, c, chips = _place()
        for a in range(n):
            rows = _half_rows(srcs[a].shape[0], c)
            for k, (px, py) in enumerate(chips):
                cp = pltpu.make_async_remote_copy(src_ref=ins[a].at[rows], dst_ref=lands_[a].at[2 * px + py, rows], send_sem=send_.at[3 * a + k],
                                                  recv_sem=recv_.at[3 * a + k], device_id=(px, py, c), device_id_type=MESH)
                cp.wait_send()
                cp.wait_recv()

    out = pl.pallas_call(
        body, name=name, out_shape=[pltpu.HBM(v.shape, v.dtype) for v in list(srcs) + list(lands)],
        in_specs=[HBM] * (2 * n) + [SEM, SEM, ANY], out_specs=[HBM] * (2 * n), input_output_aliases={i: i for i in range(2 * n)},
        compiler_params=pltpu.CompilerParams(has_side_effects=DATAFLOW),
    )(*srcs, *lands, send, recv, after)
    return list(out[n:])


def _pair_forward(lands, name):
    n = len(lands)

    def body(*refs):
        ins, outs = refs[:n], refs[n:2 * n]
        send, recv = refs[2 * n:]
        x, y, c, chips = _place()
        copies = []
        for a in range(n):
            r = lands[a].shape[1]
            for k, (px, py) in enumerate(chips):
                cp = pltpu.make_async_remote_copy(
                    src_ref=ins[a].at[2 * px + py, _half_rows(r, c)], dst_ref=outs[a].at[2 * px + py, _half_rows(r, c)],
                    send_sem=send.at[a, k], recv_sem=recv.at[a, k], device_id=(x, y, 1 - c), device_id_type=MESH)
                cp.start()
                copies.append(cp)
        for a in range(n):
            r = lands[a].shape[1]
            for k, (px, py) in enumerate(chips):
                pltpu.make_async_remote_copy(
                    src_ref=ins[a].at[2 * px + py, _half_rows(r, c)], dst_ref=outs[a].at[2 * px + py, _half_rows(r, 1 - c)],
                    send_sem=send.at[a, k], recv_sem=recv.at[a, k], device_id=(x, y, 1 - c), device_id_type=MESH).wait_recv()
        for cp in copies:
            cp.wait_send()

    return pl.pallas_call(body, name=name, in_specs=[ANY] * n, out_specs=[ANY] * n, out_shape=[_sds(v.shape, v.dtype) for v in lands],
                          scratch_shapes=[pltpu.SemaphoreType.DMA((n, 3)), pltpu.SemaphoreType.DMA((n, 3))],
                          input_output_aliases={i: i for i in range(n)})(*lands)


def _pair_exchange(arrs, name):
    n = len(arrs)

    def body(*refs):
        ins, outs = refs[:n], refs[n:2 * n]
        send, recv = refs[2 * n:]
        x, y, c, _ = _place()
        copies = []
        for a in range(n):
            h = arrs[a].shape[1] // 2
            cp = pltpu.make_async_remote_copy(src_ref=ins[a].at[:, pl.ds(pl.multiple_of((1 - c) * h, h), h)], dst_ref=outs[a],
                                              send_sem=send.at[a], recv_sem=recv.at[a], device_id=(x, y, 1 - c), device_id_type=MESH)
            cp.start()
            copies.append(cp)
        for cp in copies:
            cp.wait()

    return _comm_call(body, name, arrs, [_sds((a.shape[0], a.shape[1] // 2, a.shape[2]), a.dtype) for a in arrs],
                      [pltpu.SemaphoreType.DMA((n,)), pltpu.SemaphoreType.DMA((n,))])


def _pair_exchange_start(arrs, name):
    n = len(arrs)
    land_shapes = [((a.shape[0], a.shape[1] // 2, a.shape[2]), a.dtype) for a in arrs]

    def body(*refs):
        ins, lands = refs[:n], refs[n:2 * n]
        send, recv, token = refs[2 * n], refs[2 * n + 1], refs[-1]
        x, y, c, _ = _place()
        for a in range(n):
            h = arrs[a].shape[1] // 2
            pltpu.make_async_remote_copy(src_ref=ins[a].at[:, pl.ds(pl.multiple_of((1 - c) * h, h), h)], dst_ref=lands[a],
                                         send_sem=send.at[a], recv_sem=recv.at[a], device_id=(x, y, 1 - c), device_id_type=MESH).start()
        token[...] = jnp.zeros_like(token)

    hbm = lambda v: pltpu.with_memory_space_constraint(v, pltpu.HBM)
    out = pl.pallas_call(
        body, name=name,
        out_shape=(pltpu.SemaphoreType.DMA((n,)), pltpu.SemaphoreType.DMA((n,)), *[pltpu.HBM(a.shape, a.dtype) for a in arrs],
                   *[pltpu.HBM(s, d) for s, d in land_shapes], _sds((8, HD))),
        in_specs=[HBM] * (2 * n), out_specs=(SEM, SEM, *[HBM] * (2 * n), pl.BlockSpec(memory_space=pltpu.VMEM)),
        input_output_aliases={i: 2 + i for i in range(2 * n)},
        compiler_params=pltpu.CompilerParams(has_side_effects=DATAFLOW),
    )(*[hbm(a) for a in arrs], *[hbm(lax.empty(s, d)) for s, d in land_shapes])
    return out[0], out[1], list(out[2:2 + n]), list(out[2 + n:2 + 2 * n]), out[-1]


def _pair_exchange_wait(send, recv, srcs, lands, after, name):
    n = len(srcs)

    def body(*refs):
        ins, lands_ = refs[:n], refs[n:2 * n]
        send_, recv_ = refs[2 * n], refs[2 * n + 1]
        x, y, c, _ = _place()
        for a in range(n):
            h = srcs[a].shape[1] // 2
            cp = pltpu.make_async_remote_copy(src_ref=ins[a].at[:, pl.ds(pl.multiple_of((1 - c) * h, h), h)], dst_ref=lands_[a],
                                              send_sem=send_.at[a], recv_sem=recv_.at[a], device_id=(x, y, 1 - c), device_id_type=MESH)
            cp.wait_send()
            cp.wait_recv()

    out = pl.pallas_call(
        body, name=name, out_shape=[pltpu.HBM(v.shape, v.dtype) for v in list(srcs) + list(lands)],
        in_specs=[HBM] * (2 * n) + [SEM, SEM, ANY], out_specs=[HBM] * (2 * n), input_output_aliases={i: i for i in range(2 * n)},
        compiler_params=pltpu.CompilerParams(has_side_effects=DATAFLOW),
    )(*srcs, *lands, send, recv, after)
    return list(out[:n]), list(out[n:])


def _chip_exchange_start(arrs, name):
    n = len(arrs)

    def body(*refs):
        ins, lands = refs[:n], refs[n:2 * n]
        send, recv, token = refs[2 * n], refs[2 * n + 1], refs[-1]
        x, y, c, chips = _place()
        me = 2 * x + y
        for a in range(n):
            for k, (px, py) in enumerate(chips):
                pltpu.make_async_remote_copy(src_ref=ins[a].at[2 * px + py], dst_ref=lands[a].at[me], send_sem=send.at[3 * a + k],
                                             recv_sem=recv.at[3 * a + k], device_id=(px, py, c), device_id_type=MESH).start()
        token[...] = jnp.zeros_like(token)

    hbm = lambda v: pltpu.with_memory_space_constraint(v, pltpu.HBM)
    out = pl.pallas_call(
        body, name=name,
        out_shape=(pltpu.SemaphoreType.DMA((3 * n,)), pltpu.SemaphoreType.DMA((3 * n,)), *[pltpu.HBM(a.shape, a.dtype) for a in arrs],
                   *[pltpu.HBM(a.shape, a.dtype) for a in arrs], _sds((8, HD))),
        in_specs=[HBM] * (2 * n), out_specs=(SEM, SEM, *[HBM] * (2 * n), pl.BlockSpec(memory_space=pltpu.VMEM)),
        input_output_aliases={i: 2 + i for i in range(2 * n)},
        compiler_params=pltpu.CompilerParams(has_side_effects=DATAFLOW),
    )(*[hbm(a) for a in arrs], *[hbm(lax.empty(a.shape, a.dtype)) for a in arrs])
    return out[0], out[1], list(out[2:2 + n]), list(out[2 + n:2 + 2 * n]), out[-1]


def _chip_exchange_wait(send, recv, srcs, lands, after, name):
    n = len(srcs)

    def body(*refs):
        ins, lands_ = refs[:n], refs[n:2 * n]
        send_, recv_ = refs[2 * n], refs[2 * n + 1]
        x, y, c, chips = _place()
        for a in range(n):
            for k, (px, py) in enumerate(chips):
                cp = pltpu.make_async_remote_copy(src_ref=ins[a].at[2 * px + py], dst_ref=lands_[a].at[2 * px + py], send_sem=send_.at[3 * a + k],
                                                  recv_sem=recv_.at[3 * a + k], device_id=(px, py, c), device_id_type=MESH)
                cp.wait_send()
                cp.wait_recv()

    out = pl.pallas_call(
        body, name=name, out_shape=[pltpu.HBM(v.shape, v.dtype) for v in list(srcs) + list(lands)],
        in_specs=[HBM] * (2 * n) + [SEM, SEM, ANY], out_specs=[HBM] * (2 * n), input_output_aliases={i: i for i in range(2 * n)},
        compiler_params=pltpu.CompilerParams(has_side_effects=DATAFLOW),
    )(*srcs, *lands, send, recv, after)
    return list(out[n:])


def _pair_swap(arrs, name):
    n = len(arrs)

    def body(*refs):
        ins, outs = refs[:n], refs[n:2 * n]
        send, recv = refs[2 * n:]
        x, y, c, _ = _place()
        copies = []
        for a in range(n):
            cp = pltpu.make_async_remote_copy(src_ref=ins[a], dst_ref=outs[a], send_sem=send.at[a], recv_sem=recv.at[a],
                                              device_id=(x, y, 1 - c), device_id_type=MESH)
            cp.start()
            copies.append(cp)
        for cp in copies:
            cp.wait()

    return _comm_call(body, name, arrs, [_sds(a.shape, a.dtype) for a in arrs],
                      [pltpu.SemaphoreType.DMA((n,)), pltpu.SemaphoreType.DMA((n,))])


def _all_gather_devices(v, name):
    def body(v_ref, o_ref, send, recv, loc):
        x, y, c, _ = _place()
        me = 4 * x + 2 * y + c
        own = pltpu.make_async_copy(v_ref, o_ref.at[me], loc)
        own.start()
        copies = [own]
        for k in range(1, 8):
            fx, fy, fc = (k >> 2) & 1, (k >> 1) & 1, k & 1
            peer = (x ^ fx, y ^ fy, c ^ fc)
            r = pltpu.make_async_remote_copy(src_ref=v_ref, dst_ref=o_ref.at[me], send_sem=send.at[k - 1],
                                             recv_sem=recv.at[k - 1], device_id=peer, device_id_type=MESH)
            r.start()
            copies.append(r)
        for cp in copies:
            cp.wait()

    return _comm_call(body, name, [v], [_sds((8,) + v.shape, v.dtype)],
                      [pltpu.SemaphoreType.DMA((7,)), pltpu.SemaphoreType.DMA((7,)), pltpu.SemaphoreType.DMA])[0]


def _row_tile(r):
    return next((b for b in (512, 384, 256, 128, 64, 32, 16) if r % b == 0), r)


def _add2(a, b, out_dtype, name):
    r, w = a.shape
    br = _row_tile(r)

    def body(a_ref, b_ref, o_ref):
        o_ref[...] = (a_ref[...].astype(F32) + b_ref[...].astype(F32)).astype(out_dtype)

    blk = pl.BlockSpec((br, w), lambda i: (i, 0))
    return _pc(body, name, (r // br,), [blk, blk], blk, _sds((r, w), out_dtype))(a, b)


def _sum_slots(a, out_dtype, name):
    n, r, w = a.shape
    br = _row_tile(r)

    def body(a_ref, o_ref):
        acc = a_ref[0].astype(F32)
        for s in range(1, n):
            acc = acc + a_ref[s].astype(F32)
        o_ref[...] = acc.astype(out_dtype)

    return _pc(body, name, (r // br,), [pl.BlockSpec((n, br, w), lambda i: (0, i, 0))], pl.BlockSpec((br, w), lambda i: (i, 0)),
               _sds((r, w), out_dtype))(a)


SMALL = ["norm_mix_g", "norm_xattn_g", "norm_mlp_g", "final_norm_g", "mem_norm_g", "hgrn_lb_logits", "mlstm_norm_g",
         "hgrn_norm_g", "c_qnorm_g", "c_knorm_g", "ab_gate_b", "c_fgate_b"]
SMALL_ROWS = 16


def _pack_small(parts):
    flat = jnp.concatenate([p.reshape(-1).astype(F32) for p in parts])
    return jnp.pad(flat, (0, SMALL_ROWS * D - flat.shape[0])).reshape(SMALL_ROWS, D)


def _unpack_small(buf, shapes):
    flat, out, off = buf.reshape(-1), [], 0
    for s in shapes:
        n = 1
        for d in s:
            n *= d
        out.append(flat[off:off + n].reshape(s))
        off += n
    return out


def kernel(x, mem, norm_mix_g, norm_xattn_g, norm_mlp_g, final_norm_g, ab_w_in, ab_conv_w, ab_gate_b, hgrn_lb_logits, mlstm_norm_g, hgrn_norm_g, ab_w_out, c_w_in, c_fgate_b, c_qnorm_g, c_knorm_g, c_w_out, mem_norm_g, mem_w_kv, xa_w_q, xa_w_o, mlp_w1, mlp_w2, loss_target, m_norm_mix_g, m_norm_xattn_g, m_norm_mlp_g, m_final_norm_g, m_ab_w_in, m_ab_conv_w, m_ab_gate_b, m_hgrn_lb_logits, m_mlstm_norm_g, m_hgrn_norm_g, m_ab_w_out, m_c_w_in, m_c_fgate_b, m_c_qnorm_g, m_c_knorm_g, m_c_w_out, m_mem_norm_g, m_mem_w_kv, m_xa_w_q, m_xa_w_o, m_mlp_w1, m_mlp_w2, v_norm_mix_g, v_norm_xattn_g, v_norm_mlp_g, v_final_norm_g, v_ab_w_in, v_ab_conv_w, v_ab_gate_b, v_hgrn_lb_logits, v_mlstm_norm_g, v_hgrn_norm_g, v_ab_w_out, v_c_w_in, v_c_fgate_b, v_c_qnorm_g, v_c_knorm_g, v_c_w_out, v_mem_norm_g, v_mem_w_kv, v_xa_w_q, v_xa_w_o, v_mlp_w1, v_mlp_w2):
    A = dict(locals())
    chip = 2 * lax.axis_index("x") + lax.axis_index("y")

    big = ["ab_w_in", "c_w_in", "ab_w_out", "c_w_out", "mem_w_kv", "xa_w_q", "xa_w_o", "mlp_w1", "mlp_w2"]
    shard2d = {"ab_w_in": (D, 1026), "c_w_in": (D, 1026), "ab_w_out": (256, D), "c_w_out": (256, D), "mem_w_kv": (D, 512),
               "xa_w_q": (512, D), "xa_w_o": (512, D), "mlp_w1": (2 * D, D), "mlp_w2": (2 * D, D)}
    own_slot = lambda gs, os: [lax.dynamic_update_index_in_dim(g, o, chip, 0) for g, o in zip(gs, os)]
    cols = lambda g: jnp.concatenate([g[k] for k in range(NCHIP)], axis=1)
    per_layer = lambda g: g.reshape(NCHIP, 2, -1, D).transpose(1, 0, 2, 3)
    first = [A["ab_w_in"].reshape(shard2d["ab_w_in"]).astype(BF16), jnp.pad(ab_conv_w[0], ((0, 16 - CONV_W), (0, 0)))]
    *first_handles, tok_first = _gather_start(first, ab_conv_w, "gather_first_start")
    rest_names = ["c_w_in", "ab_w_out", "c_w_out", "xa_w_q", "xa_w_o", "mlp_w1", "mlp_w2", "mem_w_kv"]
    rest = [(A[n].reshape(shard2d[n]) + tok_first[0, 0]).astype(BF16) for n in rest_names]
    send_s, recv_s, srcs, lands, token = _gather_start(rest, tok_first, "gather_rest_start")
    g_in0, g_conv = own_slot(_pair_forward(_gather_wait(*first_handles, token, "gather_first_wait"), "gather_first_forward"), first)
    W = dict(w_in0=_pack_w_in0(cols(g_in0)))

    def late_weights(after):
        got = _pair_forward(_gather_wait(send_s, recv_s, srcs, lands, after, "gather_rest_wait"), "gather_rest_forward")
        gw = dict(zip(rest_names, own_slot(got, rest)))
        return dict(w_in1=_pack_w_in1(cols(gw["c_w_in"])), w_out0=gw["ab_w_out"].reshape(D, D), w_out1=gw["c_w_out"].reshape(D, D),
                    wkv_s=gw["mem_w_kv"],
                    wq=per_layer(gw["xa_w_q"]).reshape(2, D, D), wo=per_layer(gw["xa_w_o"]).reshape(2, D, D),
                    w1s=gw["mlp_w1"].reshape(NCHIP, 2, D, D), w2=gw["mlp_w2"].reshape(NCHIP, 2, D, D))

    S = dict(norm_mix_g=norm_mix_g + token[0, 0], norm_xattn_g=norm_xattn_g, norm_mlp_g=norm_mlp_g, final_norm_g=final_norm_g,
             conv_w=cols(g_conv[:, :CONV_W]), gate_b=ab_gate_b, lb_logits=hgrn_lb_logits, mlstm_norm_g=mlstm_norm_g,
             hgrn_norm_g=hgrn_norm_g, c_fgate_b=c_fgate_b, c_qnorm_g=c_qnorm_g, c_knorm_g=c_knorm_g, mem_norm_g=mem_norm_g)

    core = lax.axis_index("c")
    by_rows = lambda g: g.reshape(NCHIP, -1, D)

    def stack_cols(g):
        return jnp.stack([g[:, 1026 * k:1026 * (k + 1)] for k in range(NCHIP)])

    def pair_sums(arrs, theirs, tag):
        out = []
        for i, (a, th) in enumerate(zip(arrs, theirs)):
            h = a.shape[1] // 2
            mine = lax.dynamic_slice_in_dim(a, core * h, h, axis=1)
            out.append(_add2(mine.reshape(-1, a.shape[2]), th.reshape(-1, a.shape[2]), BF16, f"pair_sum_{tag}{i}").reshape(th.shape))
        return out

    def start_chip_exchange(stage, psums):
        *handles, token = _chip_exchange_start(psums, f"chip_exchange_start_{stage}")
        started[stage] = (psums, handles)
        return token[0, 0]

    def chip_sums(psums, from_chips, tag):
        out = []
        for i, (f, p) in enumerate(zip(from_chips, psums)):
            f = lax.dynamic_update_index_in_dim(f, lax.dynamic_index_in_dim(p, chip, 0, keepdims=False), chip, 0)
            out.append(_sum_slots(f, F32, f"chip_sum_{tag}{i}"))
        return out

    started, pending = {}, {}

    def grads_ready(stage, g):
        if stage == "in0":
            arrs = [stack_cols(_unpack_w_in0(g["w_in"]))]
            return start_chip_exchange(stage, pair_sums(arrs, _pair_exchange(arrs, f"pair_exchange_{stage}"), stage))
        arrs = [jnp.concatenate([by_rows(g["w_out"]), by_rows(g["wq"]), by_rows(g["wo"]), g["w1"], by_rows(g["w2"])], axis=1),
                stack_cols(_unpack_w_in1(g["w_in"])) if stage == "layer1" else g["wkv"]]
        *pending[stage], token = _pair_exchange_start(arrs, f"pair_exchange_start_{stage}")
        return token[0, 0]

    def grads_next(stage, after):
        arrs, theirs = _pair_exchange_wait(*pending[stage], after, f"pair_exchange_wait_{stage}")
        return start_chip_exchange(stage, pair_sums(arrs, theirs, stage))

    lossp, dx, G = _local_step(x[0], mem[0], loss_target[0], W, S, late_weights, (grads_ready, grads_next))

    gsmall = {"norm_mix_g": G["norm_mix_g"], "norm_xattn_g": G["norm_xattn_g"], "norm_mlp_g": G["norm_mlp_g"],
              "final_norm_g": G["final_norm_g"], "mem_norm_g": G["mem_norm_g"], "hgrn_lb_logits": G["lb_logits"],
              "mlstm_norm_g": G["mlstm_norm_g"], "hgrn_norm_g": G["hgrn_norm_g"], "c_qnorm_g": G["c_qnorm_g"],
              "c_knorm_g": G["c_knorm_g"], "ab_gate_b": G["gate_b"], "c_fgate_b": G["c_fgate_b"]}
    packed = _pack_small([gsmall[n] for n in SMALL] + [G["conv_w"], lossp])
    red = _sum_slots(_all_gather_devices(packed, "gather_small"), F32, "sum_small")
    small_shapes = [A[n].shape for n in SMALL]
    *gs, gconv, loss = _unpack_small(red, small_shapes + [(CONV_W, D), ()])
    gs = dict(zip(SMALL, gs))
    gconv = lax.dynamic_slice_in_dim(gconv, chip * 256, 256, axis=1)[None]

    rhalf = []
    for stage in ("layer1", "layer0", "in0"):
        psums, handles = started[stage]
        rhalf += chip_sums(psums, _chip_exchange_wait(*handles, dx, f"chip_exchange_wait_{stage}"), stage)
    other = _pair_swap(rhalf, "pair_swap")
    r_l1, r_in1, r_l0, r_kv, r_in0 = [
        jnp.where(core == 0, jnp.concatenate([m_, o_], axis=0), jnp.concatenate([o_, m_], axis=0)) for m_, o_ in zip(rhalf, other)]
    both = lambda lo, hi: jnp.concatenate([r_l0[lo:hi], r_l1[lo:hi]], axis=0)
    gbig = {"ab_w_in": r_in0, "c_w_in": r_in1, "mem_w_kv": r_kv, "ab_w_out": r_l0[0:256], "c_w_out": r_l1[0:256],
            "xa_w_q": both(256, 512), "xa_w_o": both(512, 768), "mlp_w1": both(768, 1792), "mlp_w2": both(1792, 2816)}

    out_g, out_d, out_m, out_v = {}, {}, {}, {}
    for n in big:
        d_, m_, v_ = _adam(A[n].reshape(shard2d[n]), gbig[n], A["m_" + n].reshape(shard2d[n]), A["v_" + n].reshape(shard2d[n]), "adam_" + n)
        out_g[n] = gbig[n].reshape(A[n].shape)
        out_d[n], out_m[n], out_v[n] = d_.reshape(A[n].shape), m_.reshape(A[n].shape), v_.reshape(A[n].shape)
    sd, sm, sv = _adam(_pack_small([A[n] for n in SMALL]), _pack_small([gs[n] for n in SMALL]),
                       _pack_small([A["m_" + n] for n in SMALL]), _pack_small([A["v_" + n] for n in SMALL]), "adam_small")
    for n, d_, m_, v_ in zip(SMALL, _unpack_small(sd, small_shapes), _unpack_small(sm, small_shapes), _unpack_small(sv, small_shapes)):
        out_g[n], out_d[n], out_m[n], out_v[n] = gs[n], d_, m_, v_
    cd, cm_, cv = _adam(ab_conv_w[0], gconv[0], m_ab_conv_w[0], v_ab_conv_w[0], "adam_conv")
    out_g["ab_conv_w"], out_d["ab_conv_w"], out_m["ab_conv_w"], out_v["ab_conv_w"] = gconv, cd[None], cm_[None], cv[None]

    order = ["norm_mix_g", "norm_xattn_g", "norm_mlp_g", "final_norm_g", "ab_w_in", "ab_conv_w", "ab_gate_b", "hgrn_lb_logits",
             "mlstm_norm_g", "hgrn_norm_g", "ab_w_out", "c_w_in", "c_fgate_b", "c_qnorm_g", "c_knorm_g", "c_w_out", "mem_norm_g",
             "mem_w_kv", "xa_w_q", "xa_w_o", "mlp_w1", "mlp_w2"]
    return (loss, dx[None], *[out_g[n] for n in order], *[out_d[n] for n in order], *[out_m[n] for n in order],
            *[out_v[n] for n in order])
```

```python
import jax
import jax.numpy as jnp
from jax import lax
from jax.experimental import pallas as pl
from jax.experimental.pallas import tpu as pltpu

F32 = jnp.float32
BF16 = jnp.bfloat16
EPS = 1e-6
D = 1024
CHUNK = 64
REC_CHUNKS = 4
HD = 128
XD = 256
NEG = -1e30
VMEM_LIMIT_V7X = 56 * 1024 * 1024
ADAM_LR, ADAM_B1, ADAM_B2, ADAM_EPS, ADAM_WD, ADAM_STEP = 0.001, 0.9, 0.999, 1e-08, 0.01, 10
MESH = pl.DeviceIdType.MESH


def _pc(body, name, grid, in_specs, out_specs, out_shape, scratch=(), **kw):
    return pl.pallas_call(
        body, name=name, grid=grid, in_specs=in_specs, out_specs=out_specs, out_shape=out_shape,
        scratch_shapes=scratch,
        compiler_params=pltpu.CompilerParams(
            dimension_semantics=("arbitrary",) * len(grid), vmem_limit_bytes=VMEM_LIMIT_V7X), **kw)


def _sds(shape, dtype=F32):
    return jax.ShapeDtypeStruct(shape, dtype)


def _blk(n, target):
    return max(b for b in range(128, max(target, 128) + 1, 128) if n % b == 0)


def _dot(a, b, dims):
    return lax.dot_general(a, b, (dims, ((), ())), preferred_element_type=F32)


def _nn(a, b):
    return _dot(a, b, ((1,), (0,)))


def _nt(a, b):
    return _dot(a, b, ((1,), (1,)))


def _tn(a, b):
    return _dot(a, b, ((0,), (0,)))


def _sigmoid(x):
    return 1.0 / (1.0 + jnp.exp(-x))


def _log_sigmoid(x):
    return jnp.minimum(x, 0.0) - jnp.log(1.0 + jnp.exp(-jnp.abs(x)))


def _rstd(x):
    return lax.rsqrt(jnp.mean(x * x, axis=-1, keepdims=True) + EPS)


def _rms_bwd(du, x, g):
    r = _rstd(x)
    xh = x * r
    dxh = du * g
    dx = r * (dxh - xh * jnp.mean(dxh * xh, axis=-1, keepdims=True))
    return dx, du * xh


def _norm_mm(h, g, w, name, bm=1024, bn=512):
    t, n = h.shape[0], w.shape[1]
    bm, bn = min(bm, t), _blk(n, 3 * bn)

    def body(h_ref, g_ref, w_ref, z_ref, u_ref):
        @pl.when(pl.program_id(1) == 0)
        def _():
            x = h_ref[...]
            u_ref[...] = (x * _rstd(x) * g_ref[...]).astype(BF16)
        z_ref[...] = _nn(u_ref[...], w_ref[...])

    return _pc(body, name, (t // bm, n // bn),
               [pl.BlockSpec((bm, D), lambda i, j: (i, 0)), pl.BlockSpec((1, D), lambda i, j: (0, 0)),
                pl.BlockSpec((D, bn), lambda i, j: (0, j))],
               [pl.BlockSpec((bm, bn), lambda i, j: (i, j)), pl.BlockSpec((bm, D), lambda i, j: (i, 0))],
               [_sds((t, n)), _sds((t, D), BF16)])(h, g, w)


def _mm_tn(a, b, name, bm=1024, bn=1024, bt=4096, col_chips=None):
    t, m = a.shape
    n = b.shape[1]
    bm, bn, bt = _blk(m, bm), (n // col_chips if col_chips else _blk(n, bn + bn // 2)), min(bt, t)
    if (m // bm) * (n // bn) == 1 and bt >= 1024:
        bt //= 4
    nt = t // bt

    def body(a_ref, b_ref, o_ref, acc):
        k = pl.program_id(2)

        @pl.when(k == 0)
        def _():
            acc[...] = jnp.zeros_like(acc)

        acc[...] += _tn(a_ref[...].astype(BF16), b_ref[...].astype(BF16))

        @pl.when(k == nt - 1)
        def _():
            o_ref[...] = acc[...].astype(BF16)

    if col_chips:
        out_spec, out_shape = pl.BlockSpec((None, bm, bn), lambda i, j, k: (j, i, 0)), _sds((col_chips, m, bn), BF16)
    else:
        out_spec, out_shape = pl.BlockSpec((bm, bn), lambda i, j, k: (i, j)), _sds((m, n), BF16)
    return _pc(body, name, (m // bm, n // bn, nt),
               [pl.BlockSpec((bt, bm), lambda i, j, k: (k, i)), pl.BlockSpec((bt, bn), lambda i, j, k: (k, j))],
               out_spec, out_shape, scratch=[pltpu.VMEM((bm, bn), F32)])(a, b)


def _bwd_in(dz, w, h, g, dh, name, bm=1024, bk=1024):
    t, n = dz.shape
    if n > 2 * bk:
        bm, bk = min(bm // 2, t), n
    else:
        bm, bk = min(bm, t), _blk(n, bk + bk // 2)
    nk = n // bk

    def body(dz_ref, w_ref, h_ref, g_ref, dh_ref, o_ref, dg_ref, acc):
        i, k = pl.program_id(0), pl.program_id(1)

        @pl.when(k == 0)
        def _():
            acc[...] = jnp.zeros_like(acc)

        @pl.when((i == 0) & (k == 0))
        def _():
            dg_ref[...] = jnp.zeros_like(dg_ref)

        acc[...] += _nt(dz_ref[...], w_ref[...])

        @pl.when(k == nk - 1)
        def _():
            dx, dgr = _rms_bwd(acc[...], h_ref[...], g_ref[...])
            o_ref[...] = dh_ref[...] + dx
            dg_ref[...] += jnp.sum(dgr, axis=0, keepdims=True)

    return _pc(body, name, (t // bm, nk),
               [pl.BlockSpec((bm, bk), lambda i, k: (i, k)), pl.BlockSpec((D, bk), lambda i, k: (0, k)),
                pl.BlockSpec((bm, D), lambda i, k: (i, 0)), pl.BlockSpec((1, D), lambda i, k: (0, 0)),
                pl.BlockSpec((bm, D), lambda i, k: (i, 0))],
               [pl.BlockSpec((bm, D), lambda i, k: (i, 0)), pl.BlockSpec((1, D), lambda i, k: (0, 0))],
               [_sds((t, D)), _sds((1, D))], scratch=[pltpu.VMEM((bm, D), F32)])(dz, w, h, g, dh)


def _mlp_fwd(h, g, w1s, w2, l, name, bm=1024):
    t = h.shape[0]
    bm = min(bm, t)
    nk = w1s.shape[0]

    def body(h_ref, g_ref, w1_ref, w2_ref, o_ref, a_ref, u_ref, acc):
        k = pl.program_id(1)

        @pl.when(k == 0)
        def _():
            x = h_ref[...]
            u_ref[...] = (x * _rstd(x) * g_ref[...]).astype(BF16)
            acc[...] = jnp.zeros_like(acc)

        a = _nn(u_ref[...], w1_ref[...])
        a_ref[...] = a
        r = jnp.square(jnp.maximum(a, 0.0)).astype(BF16)
        acc[...] += _nn(r, w2_ref[...])

        @pl.when(k == nk - 1)
        def _():
            o_ref[...] = h_ref[...] + acc[...]

    return _pc(body, name, (t // bm, nk),
               [pl.BlockSpec((bm, D), lambda i, k: (i, 0)), pl.BlockSpec((1, D), lambda i, k: (0, 0)),
                pl.BlockSpec((None, None, D, D), lambda i, k: (k, l, 0, 0)), pl.BlockSpec((None, None, D, D), lambda i, k: (k, l, 0, 0))],
               [pl.BlockSpec((bm, D), lambda i, k: (i, 0)), pl.BlockSpec((bm, D), lambda i, k: (i, k)),
                pl.BlockSpec((bm, D), lambda i, k: (i, 0))],
               [_sds((t, D)), _sds((t, nk * D)), _sds((t, D), BF16)],
               scratch=[pltpu.VMEM((bm, D), F32)])(h, g, w1s, w2)


def _mlp_bwd(dh, a, w1s, w2, l, h, g, name, bm=512):
    t = h.shape[0]
    bm = min(bm, t)
    nk = w1s.shape[0]

    def body(dh_ref, a_ref, w1_ref, w2_ref, h_ref, g_ref, o_ref, da_ref, r_ref, dg_ref, acc):
        i, k = pl.program_id(0), pl.program_id(1)

        @pl.when(k == 0)
        def _():
            acc[...] = jnp.zeros_like(acc)

        @pl.when((i == 0) & (k == 0))
        def _():
            dg_ref[...] = jnp.zeros_like(dg_ref)

        ap = jnp.maximum(a_ref[...], 0.0)
        r_ref[...] = jnp.square(ap).astype(BF16)
        dr = _nt(dh_ref[...].astype(BF16), w2_ref[...])
        da = (dr * (2.0 * ap)).astype(BF16)
        da_ref[...] = da
        acc[...] += _nt(da, w1_ref[...])

        @pl.when(k == nk - 1)
        def _():
            dx, dgr = _rms_bwd(acc[...], h_ref[...], g_ref[...])
            o_ref[...] = dh_ref[...] + dx
            dg_ref[...] += jnp.sum(dgr, axis=0, keepdims=True)

    return _pc(body, name, (t // bm, nk),
               [pl.BlockSpec((bm, D), lambda i, k: (i, 0)), pl.BlockSpec((bm, D), lambda i, k: (i, k)),
                pl.BlockSpec((None, None, D, D), lambda i, k: (k, l, 0, 0)), pl.BlockSpec((None, None, D, D), lambda i, k: (k, l, 0, 0)),
                pl.BlockSpec((bm, D), lambda i, k: (i, 0)), pl.BlockSpec((1, D), lambda i, k: (0, 0))],
               [pl.BlockSpec((bm, D), lambda i, k: (i, 0)), pl.BlockSpec((bm, D), lambda i, k: (i, k)),
                pl.BlockSpec((bm, D), lambda i, k: (i, k)), pl.BlockSpec((1, D), lambda i, k: (0, 0))],
               [_sds((t, D)), _sds((t, nk * D), BF16), _sds((t, nk * D), BF16), _sds((1, D))],
               scratch=[pltpu.VMEM((bm, D), F32)])(dh, a, w1s, w2, h, g)


def _rows_of(x):
    return lax.broadcasted_iota(jnp.int32, x.shape, 0)


def _shift_down(x, s):
    if s == 0:
        return x
    return jnp.where(_rows_of(x) >= s, pltpu.roll(x, s, 0), 0.0)


def _shift_up(x, s):
    if s == 0:
        return x
    n = x.shape[0]
    return jnp.where(_rows_of(x) < n - s, pltpu.roll(x, n - s, 0), 0.0)


def _cumsum_rows(x):
    n, s = x.shape[0], 1
    while s < n:
        x = x + _shift_down(x, s)
        s *= 2
    return x


def _rcumsum_rows(x):
    n, s = x.shape[0], 1
    while s < n:
        x = x + _shift_up(x, s)
        s *= 2
    return x


def _silu(x):
    return x * _sigmoid(x)


def _dsilu(x):
    s = _sigmoid(x)
    return s * (1.0 + x * (1.0 - s))


CONV_W = 4


def _conv_pre(u, w):
    y = _shift_down(u, CONV_W - 1) * w[0:1, :]
    for j in range(1, CONV_W):
        y = y + _shift_down(u, CONV_W - 1 - j) * w[j:j + 1, :]
    return y


def _conv_fwd(z0, cw, name):
    t = z0.shape[0]

    def body(u_ref, w_ref, o_ref):
        o_ref[...] = _silu(_conv_pre(u_ref[...], w_ref[...]))

    return _pc(body, name, (2 * 512 // HD,),
               [pl.BlockSpec((t, HD), lambda c: (0, c)), pl.BlockSpec((CONV_W, HD), lambda c: (0, c))],
               pl.BlockSpec((t, HD), lambda c: (0, c)), _sds((t, 1024)))(z0, cw)


def _conv_bwd(z0, cw, dy, name):
    t = z0.shape[0]

    def body(u_ref, w_ref, dy_ref, du_ref, dw_ref):
        u, w = u_ref[...], w_ref[...]
        dpre = dy_ref[...] * _dsilu(_conv_pre(u, w))
        du = _shift_up(dpre, CONV_W - 1) * w[0:1, :]
        for j in range(1, CONV_W):
            du = du + _shift_up(dpre, CONV_W - 1 - j) * w[j:j + 1, :]
        du_ref[...] = du.astype(BF16)
        for j in range(CONV_W):
            dw_ref[j:j + 1, :] = jnp.sum(dpre * _shift_down(u, CONV_W - 1 - j), axis=0, keepdims=True)

    return _pc(body, name, (2 * 512 // HD,),
               [pl.BlockSpec((t, HD), lambda c: (0, c)), pl.BlockSpec((CONV_W, HD), lambda c: (0, c)),
                pl.BlockSpec((t, HD), lambda c: (0, c))],
               [pl.BlockSpec((t, HD), lambda c: (0, c)), pl.BlockSpec((CONV_W, HD), lambda c: (0, c))],
               [_sds((t, 1024), BF16), _sds((CONV_W, 1024))])(z0, cw, dy)


def _mlstm_gates(gate, bias, m_in):
    L = gate.shape[0]
    r = lax.broadcasted_iota(jnp.int32, (L, L), 0)
    c = lax.broadcasted_iota(jnp.int32, (L, L), 1)
    eye, tril = r == c, c <= r
    i_col = gate[:, 0:1] + bias[:, 0:1]
    f_col = gate[:, 1:2] + bias[:, 1:2]
    logf_col = _log_sigmoid(f_col)
    logf_row = jnp.sum(jnp.where(eye, logf_col, 0.0), axis=0, keepdims=True)
    i_row = jnp.sum(jnp.where(eye, i_col, 0.0), axis=0, keepdims=True)
    b_col = jnp.sum(jnp.where(tril, logf_row, 0.0), axis=1, keepdims=True)
    b_row = jnp.sum(jnp.where(r <= c, logf_col, 0.0), axis=0, keepdims=True)
    logd = jnp.where(tril, b_col - b_row + i_row, NEG)
    inter = b_col + m_in
    m_t = jnp.maximum(inter, jnp.max(logd, axis=1, keepdims=True))
    w_t = jnp.exp(inter - m_t)
    dm = jnp.exp(logd - m_t)
    b_last = b_col[L - 1:L, :]
    log_in = b_last - b_col + i_col
    m_new = jnp.maximum(b_last + m_in, jnp.max(log_in, axis=0, keepdims=True))
    w_col = jnp.exp(log_in - m_new)
    decay = jnp.exp(b_last + m_in - m_new)
    return dict(eye=eye, r=r, c=c, f_col=f_col, m_t=m_t, w_t=w_t, dm=dm, m_new=m_new, w_col=w_col, decay=decay)


def _mlstm_fwd(qk, z0, gates, bias, name):
    t = qk.shape[0]
    nc, nh, L = t // CHUNK, 4, CHUNK
    scale = HD ** -0.5

    def body(q_ref, k_ref, v_ref, g_ref, b_ref, h_ref, cs_ref, ns_ref, ms_ref, c_s, n_s, m_s):
        @pl.when(pl.program_id(0) == 0)
        def _():
            c_s[...] = jnp.zeros_like(c_s)
            n_s[...] = jnp.zeros_like(n_s)
            m_s[...] = jnp.zeros_like(m_s)

        for hd in range(nh):
            sl = slice(hd * HD, (hd + 1) * HD)
            cm, nv, m_in = c_s[hd], n_s[hd], m_s[hd]
            for ck in range(cps):
                rows = slice(ck * L, (ck + 1) * L)
                cs_ref[hd, ck] = cm
                ns_ref[hd, ck] = nv
                ms_ref[hd, ck] = jnp.broadcast_to(m_in, (1, HD))
                q, kh, v = q_ref[rows, sl], k_ref[rows, sl] * scale, v_ref[rows, sl]
                G = _mlstm_gates(g_ref[hd, rows, :], b_ref[hd], m_in)
                qb, kb, vb = q.astype(BF16), kh.astype(BF16), v.astype(BF16)
                sc = _nt(qb, kb) * G["dm"]
                num = _nn(sc.astype(BF16), vb) + G["w_t"] * _nn(qb, cm.astype(BF16))
                den = jnp.sum(sc, axis=1, keepdims=True) + G["w_t"] * jnp.sum(q * nv, axis=1, keepdims=True)
                h_ref[rows, sl] = num / jnp.maximum(jnp.abs(den), jnp.exp(-G["m_t"]))
                wk = G["w_col"] * kh
                cm = G["decay"] * cm + _tn(wk.astype(BF16), vb)
                nv = G["decay"] * nv + jnp.sum(wk, axis=0, keepdims=True)
                m_in = G["m_new"]
            c_s[hd], n_s[hd], m_s[hd] = cm, nv, m_in

    cps = REC_CHUNKS
    hspec = lambda blk: pl.BlockSpec((cps * L, 512), lambda j: (j, blk))
    st = lambda r: pl.BlockSpec((nh, cps, r, HD), lambda j: (0, j, 0, 0))
    return _pc(body, name, (nc // cps,),
               [hspec(0), hspec(1), hspec(2), pl.BlockSpec((nh, cps * L, 2), lambda j: (0, j, 0)),
                pl.BlockSpec((nh, 1, 2), lambda j: (0, 0, 0))],
               [hspec(0), st(HD), st(1), st(1)],
               [_sds((t, 512)), _sds((nh, nc, HD, HD)), _sds((nh, nc, 1, HD)), _sds((nh, nc, 1, HD))],
               scratch=[pltpu.VMEM((nh, HD, HD), F32), pltpu.VMEM((nh, 1, HD), F32), pltpu.VMEM((nh, 1, 1), F32)])(qk, qk, z0, gates, bias)


def _mlstm_bwd(qk, z0, gates, bias, cs, ns, ms, dh, name):
    t = qk.shape[0]
    nc, nh, L = t // CHUNK, 4, CHUNK
    scale = HD ** -0.5

    def body(q_ref, k_ref, v_ref, g_ref, b_ref, cs_ref, ns_ref, ms_ref, dh_ref, dqk_ref, dv_ref, dg_ref, dc_s, dn_s):
        @pl.when(pl.program_id(0) == 0)
        def _():
            dc_s[...] = jnp.zeros_like(dc_s)
            dn_s[...] = jnp.zeros_like(dn_s)

        for ck in reversed(range(cps)):
            for hd in range(nh):
                one_head(hd, ck, slice(hd * HD, (hd + 1) * HD), slice(ck * L, (ck + 1) * L), q_ref, k_ref, v_ref, g_ref, b_ref,
                         cs_ref, ns_ref, ms_ref, dh_ref, dqk_ref, dv_ref, dg_ref, dc_s, dn_s)

    def one_head(hd, ck, sl, rows, q_ref, k_ref, v_ref, g_ref, b_ref, cs_ref, ns_ref, ms_ref, dh_ref, dqk_ref, dv_ref, dg_ref,
                 dc_s, dn_s):
        cm, nv, m_in = cs_ref[hd, ck], ns_ref[hd, ck], ms_ref[hd, ck][:, 0:1]
        q, kh, v = q_ref[rows, sl], k_ref[rows, sl] * scale, v_ref[rows, sl]
        G = _mlstm_gates(g_ref[hd, rows, :], b_ref[hd], m_in)
        w_t, dmat, w_col, decay = G["w_t"], G["dm"], G["w_col"], G["decay"]
        qb, kb, vb, cb = q.astype(BF16), kh.astype(BF16), v.astype(BF16), cm.astype(BF16)
        s = _nt(qb, kb)
        sc = s * dmat
        scb = sc.astype(BF16)
        qc = _nn(qb, cb)
        qn = jnp.sum(q * nv, axis=1, keepdims=True)
        num = _nn(scb, vb) + w_t * qc
        den = jnp.sum(sc, axis=1, keepdims=True) + w_t * qn
        e_m = jnp.exp(-G["m_t"])
        dnm = jnp.maximum(jnp.abs(den), e_m)
        dh_ = dh_ref[rows, sl]
        dnum = dh_ / dnm
        dden = jnp.where(jnp.abs(den) > e_m, -jnp.sum(dh_ * num, axis=1, keepdims=True) / (dnm * dnm) * jnp.sign(den), 0.0)
        dnumb = dnum.astype(BF16)
        dsc = _nt(dnumb, vb) + dden
        dv = _tn(scb, dnumb)
        wd = w_t * dnum
        wdb = wd.astype(BF16)
        ds = dsc * dmat
        dsb = ds.astype(BF16)
        dq = _nt(wdb, cb) + (w_t * dden) * nv + _nn(dsb, kb)
        dc_o = _tn(qb, wdb)
        dn_o = jnp.sum(q * (w_t * dden), axis=0, keepdims=True)
        dw = jnp.sum(dnum * qc, axis=1, keepdims=True) + dden * qn
        dkh = _tn(dsb, qb)
        dlogd = ds * s
        db_col = jnp.sum(dlogd, axis=1, keepdims=True) + dw * w_t
        csum = jnp.sum(dlogd, axis=0, keepdims=True)
        dcn, dnn = dc_s[hd], dn_s[hd]
        dcnb = dcn.astype(BF16)
        kdc = _nn(kb, dcnb)
        dws = jnp.sum(kdc * v, axis=1, keepdims=True) + jnp.sum(kh * dnn, axis=1, keepdims=True)
        dv = dv + w_col * kdc
        dkh = dkh + w_col * (_nt(vb, dcnb) + dnn)
        dlin = dws * w_col
        ddecay = jnp.sum(jnp.sum(dcn * cm, axis=1, keepdims=True), axis=0, keepdims=True) + jnp.sum(dnn * nv, axis=1, keepdims=True)
        dlast = ddecay * decay + jnp.sum(dlin, axis=0, keepdims=True)
        row_id = lax.broadcasted_iota(jnp.int32, (L, 1), 0)
        db_col = db_col - dlin + jnp.where(row_id == L - 1, dlast, 0.0)
        eye, r, c = G["eye"], G["r"], G["c"]
        di = dlin + jnp.sum(jnp.where(eye, csum, 0.0), axis=1, keepdims=True)
        db_row = jnp.sum(jnp.where(eye, db_col, 0.0), axis=0, keepdims=True) - csum
        dlogf = jnp.sum(jnp.where(c >= r, db_row, 0.0), axis=1, keepdims=True)
        dg_ref[hd, rows, 0:1] = di
        dg_ref[hd, rows, 1:2] = dlogf * (1.0 - _sigmoid(G["f_col"]))
        dqk_ref[rows, sl] = dq
        dqk_ref[rows, 512 + hd * HD:512 + (hd + 1) * HD] = dkh * scale
        dv_ref[rows, sl] = dv
        dc_s[hd] = decay * dcn + dc_o
        dn_s[hd] = decay * dnn + dn_o

    cps = REC_CHUNKS
    rv = lambda j: nc // cps - 1 - j
    hspec = lambda blk: pl.BlockSpec((cps * L, 512), lambda j: (rv(j), blk))
    st = lambda r: pl.BlockSpec((nh, cps, r, HD), lambda j: (0, rv(j), 0, 0))
    gs = pl.BlockSpec((nh, cps * L, 2), lambda j: (0, rv(j), 0))
    return _pc(body, name, (nc // cps,),
               [hspec(0), hspec(1), hspec(2), gs, pl.BlockSpec((nh, 1, 2), lambda j: (0, 0, 0)),
                st(HD), st(1), st(1), hspec(0)],
               [pl.BlockSpec((cps * L, 1024), lambda j: (rv(j), 0)), hspec(0), gs],
               [_sds((t, 1024)), _sds((t, 512)), _sds((nh, t, 2))],
               scratch=[pltpu.VMEM((nh, HD, HD), F32), pltpu.VMEM((nh, 1, HD), F32)])(qk, qk, z0, gates, bias, cs, ns, ms, dh)


def _hgrn_act(qb_, fb_, ib_, lg):
    lb = _sigmoid(lg[0:1, :] - lg[1:2, :])
    sg = _sigmoid(fb_)
    f = lb + (1.0 - lb) * sg
    return lb, sg, f, _silu(qb_), (1.0 - lb) * (1.0 - sg), _silu(ib_), _cumsum_rows(jnp.log(f))


HG_SUB = 16


def _hgrn_offdiag(q, k, b, r0):
    beta = b[r0 - 1:r0, :]
    e1 = jnp.exp(b[r0:r0 + HG_SUB, :] - beta)
    e2 = jnp.where(_rows_of(b) < r0, jnp.exp(jnp.minimum(beta - b, 0.0)), 0.0)
    return q[r0:r0 + HG_SUB, :] * e1, k * e2, e1, e2


def _hgrn_fwd(z0, lbl, name):
    t = z0.shape[0]
    nc, nh, L = t // CHUNK, 4, CHUNK

    def body(q_ref, f_ref, i_ref, l_ref, o_ref, ss_ref, st_s):
        @pl.when(pl.program_id(0) == 0)
        def _():
            st_s[...] = jnp.zeros_like(st_s)

        for hd in range(nh):
            sl = slice(hd * HD, (hd + 1) * HD)
            st = st_s[hd]
            for ck in range(cps):
                rows = slice(ck * L, (ck + 1) * L)
                ss_ref[hd, ck] = st
                _, _, _, q, k, v, b = _hgrn_act(q_ref[rows, sl], f_ref[rows, sl], i_ref[rows, sl], l_ref[:, sl])
                o = _nt((q * jnp.exp(b)).astype(BF16), st.astype(BF16))
                sub = _rows_of(b) & (HG_SUB - 1)
                o = o + jnp.sum(q * k, axis=1, keepdims=True) * v
                for dl in range(1, HG_SUB):
                    e = jnp.exp(jnp.where(sub >= dl, b - pltpu.roll(b, dl, 0), NEG))
                    a = jnp.sum(q * pltpu.roll(k, dl, 0) * e, axis=1, keepdims=True)
                    o = o + a * pltpu.roll(v, dl, 0)
                o_ref[rows, sl] = o
                vb = v.astype(BF16)
                for i in range(1, L // HG_SUB):
                    r0 = i * HG_SUB
                    qt, kt, _, _ = _hgrn_offdiag(q, k, b, r0)
                    a = _nt(qt.astype(BF16), kt.astype(BF16))
                    o_ref[ck * L + r0:ck * L + r0 + HG_SUB, sl] += _nn(a.astype(BF16), vb)
                bl = b[L - 1:L, :]
                st = st * jnp.exp(bl) + _tn(v.astype(BF16), (k * jnp.exp(bl - b)).astype(BF16))
            st_s[hd] = st

    cps = REC_CHUNKS
    hspec = lambda blk: pl.BlockSpec((cps * L, 512), lambda j: (j, blk))
    return _pc(body, name, (nc // cps,),
               [hspec(4), hspec(5), hspec(6), pl.BlockSpec((2, 512), lambda j: (0, 0))],
               [hspec(0), pl.BlockSpec((nh, cps, HD, HD), lambda j: (0, j, 0, 0))],
               [_sds((t, 512)), _sds((nh, nc, HD, HD))],
               scratch=[pltpu.VMEM((nh, HD, HD), F32)])(z0, z0, z0, lbl)


def _hgrn_bwd(z0, lbl, ss, do, name):
    t = z0.shape[0]
    nc, nh, L = t // CHUNK, 4, CHUNK

    def body(q_ref, f_ref, i_ref, l_ref, ss_ref, do_ref, dq_ref, df_ref, di_ref, dl_ref, dst_s, dlb_s, dq_a, dk_a, dv_a, db_a):
        @pl.when(pl.program_id(0) == 0)
        def _():
            dst_s[...] = jnp.zeros_like(dst_s)
            dlb_s[...] = jnp.zeros_like(dlb_s)

        for ck in reversed(range(cps)):
            for hd in range(nh):
                one_head(hd, ck, slice(hd * HD, (hd + 1) * HD), slice(ck * L, (ck + 1) * L), q_ref, f_ref, i_ref, l_ref, ss_ref, do_ref,
                         dq_ref, df_ref, di_ref, dl_ref, dst_s, dlb_s, dq_a.at[hd], dk_a.at[hd], dv_a.at[hd], db_a.at[hd])

    def one_head(hd, ck, sl, rs, q_ref, f_ref, i_ref, l_ref, ss_ref, do_ref, dq_ref, df_ref, di_ref, dl_ref, dst_s, dlb_s,
                 dq_a, dk_a, dv_a, db_a):
        st = ss_ref[hd, ck]
        qp, fp, ip = q_ref[rs, sl], f_ref[rs, sl], i_ref[rs, sl]
        lb, sg, f, q, k, v, b = _hgrn_act(qp, fp, ip, l_ref[:, sl])
        do_ = do_ref[rs, sl]
        dob, stb = do_.astype(BF16), st.astype(BF16)
        eb = jnp.exp(b)
        qe = q * eb
        dqe = _nn(dob, stb)
        dst_o = _tn(dob, qe.astype(BF16))
        dq = dqe * eb
        db = dqe * qe
        rows = _rows_of(b)
        sub = rows & (HG_SUB - 1)
        p0 = jnp.sum(do_ * v, axis=1, keepdims=True)
        dq = dq + p0 * k
        dk = p0 * q
        dv = jnp.sum(q * k, axis=1, keepdims=True) * do_
        for dl in range(1, HG_SUB):
            up = L - dl
            kd, vd = pltpu.roll(k, dl, 0), pltpu.roll(v, dl, 0)
            e = jnp.exp(jnp.where(sub >= dl, b - pltpu.roll(b, dl, 0), NEG))
            a = jnp.sum(q * kd * e, axis=1, keepdims=True)
            p = jnp.sum(do_ * vd, axis=1, keepdims=True) * e
            dq = dq + p * kd
            dkd = p * q
            dbb = dkd * kd
            dv = dv + pltpu.roll(a * do_, up, 0)
            dk = dk + pltpu.roll(dkd, up, 0)
            db = db + dbb - pltpu.roll(dbb, up, 0)
        dq_a[...], dk_a[...], dv_a[...], db_a[...] = dq, dk, dv, db
        vb = v.astype(BF16)
        for i in range(1, L // HG_SUB):
            r0 = i * HG_SUB
            blk = slice(r0, r0 + HG_SUB)
            qt, kt, e1, e2 = _hgrn_offdiag(q, k, b, r0)
            qtb, ktb, dob_i = qt.astype(BF16), kt.astype(BF16), do_[blk, :].astype(BF16)
            a = _nt(qtb, ktb).astype(BF16)
            da = _nt(dob_i, vb).astype(BF16)
            dv_a[...] += _tn(a, dob_i)
            dqt = _nn(da, ktb)
            dkt = _tn(da, qtb)
            dq_a[blk, :] += dqt * e1
            t1, t2 = dqt * qt, dkt * kt
            db_a[blk, :] += t1
            dk_a[...] += dkt * e2
            db_a[...] -= t2
            db_a[r0 - 1:r0, :] += jnp.sum(t2, axis=0, keepdims=True) - jnp.sum(t1, axis=0, keepdims=True)
        dq, dk, dv, db = dq_a[...], dk_a[...], dv_a[...], db_a[...]
        dstn = dst_s[hd]
        dstnb = dstn.astype(BF16)
        bl = b[L - 1:L, :]
        ebl = jnp.exp(bl)
        kdec_e = jnp.exp(bl - b)
        kdec = k * kdec_e
        dbl = jnp.sum(dstn * st, axis=0, keepdims=True) * ebl
        dv = dv + _nt(kdec.astype(BF16), dstnb)
        dkdec = _nn(v.astype(BF16), dstnb)
        dk = dk + dkdec * kdec_e
        dx = dkdec * kdec
        dbl = dbl + jnp.sum(dx, axis=0, keepdims=True)
        db = db - dx + jnp.where(rows == L - 1, dbl, 0.0)
        dst_s[hd] = dstn * ebl + dst_o
        dg = _rcumsum_rows(db)
        dfk = dg / f - dk
        dq_ref[rs, sl] = (dq * _dsilu(qp)).astype(BF16)
        di_ref[rs, sl] = (dv * _dsilu(ip)).astype(BF16)
        df_ref[rs, sl] = (dfk * (1.0 - lb) * sg * (1.0 - sg)).astype(BF16)
        dlb_s[hd] += jnp.sum(dfk * (1.0 - sg), axis=0, keepdims=True)

        if ck == 0:
            @pl.when(pl.program_id(0) == nc // cps - 1)
            def _():
                dl0 = dlb_s[hd] * lb * (1.0 - lb)
                dl_ref[0:1, sl] = dl0
                dl_ref[1:2, sl] = -dl0

    cps = REC_CHUNKS
    rv = lambda j: nc // cps - 1 - j
    hspec = lambda blk: pl.BlockSpec((cps * L, 512), lambda j: (rv(j), blk))
    return _pc(body, name, (nc // cps,),
               [hspec(4), hspec(5), hspec(6), pl.BlockSpec((2, 512), lambda j: (0, 0)),
                pl.BlockSpec((nh, cps, HD, HD), lambda j: (0, rv(j), 0, 0)), hspec(0)],
               [hspec(0), hspec(0), hspec(0), pl.BlockSpec((2, 512), lambda j: (0, 0))],
               [_sds((t, 512), BF16), _sds((t, 512), BF16), _sds((t, 512), BF16), _sds((2, 512))],
               scratch=[pltpu.VMEM((nh, HD, HD), F32), pltpu.VMEM((nh, 1, HD), F32)] + [pltpu.VMEM((nh, L, HD), F32)] * 4)(z0, z0, z0, lbl, ss, do)


def _post0_fwd(hm, hh, z0, na, nb, w, h0, name, bm=512):
    t = h0.shape[0]
    bm = min(bm, t)

    def body(hm_ref, hh_ref, oa_ref, gb_ref, na_ref, nb_ref, w_ref, h_ref, o_ref, y_ref):
        for hd in range(4):
            sl = slice(hd * HD, (hd + 1) * HD)
            pa = _sigmoid(oa_ref[:, sl]) * hm_ref[:, sl]
            y_ref[:, sl] = (pa * _rstd(pa) * na_ref[:, sl]).astype(BF16)
            xb = hh_ref[:, sl]
            y_ref[:, 512 + hd * HD:512 + (hd + 1) * HD] = (xb * _rstd(xb) * nb_ref[:, sl] * _silu(gb_ref[:, sl])).astype(BF16)
        o_ref[...] = h_ref[...] + _nn(y_ref[...], w_ref[...])

    row = lambda wd, c: pl.BlockSpec((bm, wd), lambda i: (i, c))
    vec = lambda wd: pl.BlockSpec((1, wd), lambda i: (0, 0))
    return _pc(body, name, (t // bm,),
               [row(512, 0), row(512, 0), row(512, 3), row(512, 7), vec(512), vec(512),
                pl.BlockSpec((D, D), lambda i: (0, 0)), row(D, 0)],
               [row(D, 0), row(D, 0)], [_sds((t, D)), _sds((t, D), BF16)])(hm, hh, z0, z0, na, nb, w, h0)


def _post0_bwd(dh1, w, hm, hh, z0, na, nb, name, bm=512):
    t = dh1.shape[0]
    bm = min(bm, t)

    def body(dh_ref, w_ref, hm_ref, hh_ref, oa_ref, gb_ref, na_ref, nb_ref, dhm_ref, dhh_ref, doa_ref, dgb_ref, dna_ref, dnb_ref):
        @pl.when(pl.program_id(0) == 0)
        def _():
            dna_ref[...] = jnp.zeros_like(dna_ref)
            dnb_ref[...] = jnp.zeros_like(dnb_ref)

        dy = _nt(dh_ref[...].astype(BF16), w_ref[...])
        for hd in range(4):
            sl = slice(hd * HD, (hd + 1) * HD)
            hm_, oa = hm_ref[:, sl], oa_ref[:, sl]
            sg = _sigmoid(oa)
            dpa, dgr = _rms_bwd(dy[:, sl], sg * hm_, na_ref[:, sl])
            dna_ref[:, sl] += jnp.sum(dgr, axis=0, keepdims=True)
            doa_ref[:, sl] = (dpa * hm_ * sg * (1.0 - sg)).astype(BF16)
            dhm_ref[:, sl] = dpa * sg
            xb, gb, nbv = hh_ref[:, sl], gb_ref[:, sl], nb_ref[:, sl]
            dyb = dy[:, 512 + hd * HD:512 + (hd + 1) * HD]
            dgb_ref[:, sl] = (dyb * (xb * _rstd(xb) * nbv) * _dsilu(gb)).astype(BF16)
            dxb, dgr2 = _rms_bwd(dyb * _silu(gb), xb, nbv)
            dnb_ref[:, sl] += jnp.sum(dgr2, axis=0, keepdims=True)
            dhh_ref[:, sl] = dxb

    row = lambda wd, c: pl.BlockSpec((bm, wd), lambda i: (i, c))
    vec = lambda wd: pl.BlockSpec((1, wd), lambda i: (0, 0))
    return _pc(body, name, (t // bm,),
               [row(D, 0), pl.BlockSpec((D, D), lambda i: (0, 0)), row(512, 0), row(512, 0), row(512, 3), row(512, 7),
                vec(512), vec(512)],
               [row(512, 0), row(512, 0), row(512, 0), row(512, 0), vec(512), vec(512)],
               [_sds((t, 512)), _sds((t, 512)), _sds((t, 512), BF16), _sds((t, 512), BF16), _sds((1, 512)), _sds((1, 512))],
               )(dh1, w, hm, hh, z0, z0, na, nb)


def _memkv_fwd(mem, g, wkv_s, name):
    m = mem.shape[0]

    def body(x_ref, g_ref, w_ref, kv_ref, mn_ref):
        x = x_ref[...]
        mn = (x * _rstd(x) * g_ref[...]).astype(BF16)
        mn_ref[...] = mn
        kv_ref[...] = _nn(mn, w_ref[...])

    return _pc(body, name, (4,),
               [pl.BlockSpec((m, D), lambda k: (0, 0)), pl.BlockSpec((1, D), lambda k: (0, 0)),
                pl.BlockSpec((None, D, 512), lambda k: (k, 0, 0))],
               [pl.BlockSpec((m, 512), lambda k: (0, k)), pl.BlockSpec((m, D), lambda k: (0, 0))],
               [_sds((m, 2048)), _sds((m, D), BF16)])(mem, g, wkv_s)


def _memkv_bwd(dkv, wkv_s, mem, g, name):
    m = mem.shape[0]

    def body(d_ref, w_ref, x_ref, g_ref, dg_ref, acc):
        k = pl.program_id(0)

        @pl.when(k == 0)
        def _():
            acc[...] = jnp.zeros_like(acc)

        acc[...] += _nt(d_ref[...].astype(BF16), w_ref[...])

        @pl.when(k == 3)
        def _():
            _, dgr = _rms_bwd(acc[...], x_ref[...], g_ref[...])
            dg_ref[...] = jnp.sum(dgr, axis=0, keepdims=True)

    return _pc(body, name, (4,),
               [pl.BlockSpec((m, 512), lambda k: (0, k)), pl.BlockSpec((None, D, 512), lambda k: (k, 0, 0)),
                pl.BlockSpec((m, D), lambda k: (0, 0)), pl.BlockSpec((1, D), lambda k: (0, 0))],
               pl.BlockSpec((1, D), lambda k: (0, 0)), _sds((1, D)), scratch=[pltpu.VMEM((m, D), F32)])(dkv, wkv_s, mem, g)


def _xattn_probs(qh, kh):
    s = _nt(qh, kh) * (XD ** -0.5)
    p = jnp.exp(s - jnp.max(s, axis=1, keepdims=True))
    return p / jnp.sum(p, axis=1, keepdims=True)


def _xattn_fwd(q, kv, wo, h1, name, bm=512):
    t, m = q.shape[0], kv.shape[0]
    bm = min(bm, t)

    def body(q_ref, k_ref, v_ref, w_ref, h_ref, out_ref, o_ref):
        for hd in range(D // XD):
            sl = slice(hd * XD, (hd + 1) * XD)
            p = _xattn_probs(q_ref[:, sl].astype(BF16), k_ref[:, sl].astype(BF16))
            o_ref[:, sl] = _nn(p.astype(BF16), v_ref[:, sl].astype(BF16)).astype(BF16)
        out_ref[...] = h_ref[...] + _nn(o_ref[...], w_ref[...])

    row = pl.BlockSpec((bm, D), lambda i: (i, 0))
    return _pc(body, name, (t // bm,),
               [row, pl.BlockSpec((m, D), lambda i: (0, 0)), pl.BlockSpec((m, D), lambda i: (0, 1)),
                pl.BlockSpec((D, D), lambda i: (0, 0)), row],
               [row, row], [_sds((t, D)), _sds((t, D), BF16)])(q, kv, kv, wo, h1)


def _xattn_bwd(dh2, q, kv, wo, name, bm=512):
    t, m = q.shape[0], kv.shape[0]
    bm = min(bm, t)

    def body(dh_ref, q_ref, k_ref, v_ref, w_ref, dq_ref, dkv_ref):
        @pl.when(pl.program_id(0) == 0)
        def _():
            dkv_ref[...] = jnp.zeros_like(dkv_ref)

        d_o = _nt(dh_ref[...].astype(BF16), w_ref[...])
        for hd in range(D // XD):
            sl = slice(hd * XD, (hd + 1) * XD)
            qh, kh, vh = q_ref[:, sl].astype(BF16), k_ref[:, sl].astype(BF16), v_ref[:, sl].astype(BF16)
            p = _xattn_probs(qh, kh)
            dob = d_o[:, sl].astype(BF16)
            dp = _nt(dob, vh)
            dkv_ref[:, D + hd * XD:D + (hd + 1) * XD] += _tn(p.astype(BF16), dob)
            ds = (p * (dp - jnp.sum(dp * p, axis=1, keepdims=True)) * (XD ** -0.5)).astype(BF16)
            dq_ref[:, sl] = _nn(ds, kh).astype(BF16)
            dkv_ref[:, sl] += _tn(ds, qh)

    row = pl.BlockSpec((bm, D), lambda i: (i, 0))
    return _pc(body, name, (t // bm,),
               [row, row, pl.BlockSpec((m, D), lambda i: (0, 0)), pl.BlockSpec((m, D), lambda i: (0, 1)),
                pl.BlockSpec((D, D), lambda i: (0, 0))],
               [row, pl.BlockSpec((m, 2 * D), lambda i: (0, 0))],
               [_sds((t, D), BF16), _sds((m, 2 * D))])(dh2, q, kv, kv, wo)


NH1 = 8
FOX_BM = 512
FOX_BQ = 512
FOX_BK = 512
FOX_HEADS_PER_STEP = 4


def _foxprep_fwd(z1, qg, kg, fbp, name):
    t = z1.shape[0]
    bm = min(FOX_BM, t)

    def body(q_ref, k_ref, v_ref, f_ref, qg_ref, kg_ref, fb_ref, qn_ref, kn_ref, vb_ref, c_ref, carry):
        @pl.when(pl.program_id(0) == 0)
        def _():
            carry[...] = jnp.zeros_like(carry)

        for hd in range(NH1):
            sl = slice(hd * HD, (hd + 1) * HD)
            x = q_ref[:, sl]
            qn_ref[:, sl] = (x * _rstd(x) * qg_ref[...] * FOX_QSCALE).astype(BF16)
            x = k_ref[:, sl]
            kn_ref[:, sl] = (x * _rstd(x) * kg_ref[...]).astype(BF16)
        vb_ref[...] = v_ref[...].astype(BF16)
        c = carry[...] + _cumsum_rows(_log_sigmoid(f_ref[...] + fb_ref[...]))
        c_ref[...] = c
        carry[...] = c[bm - 1:bm, :]

    row = lambda c: pl.BlockSpec((bm, D), lambda i: (i, c))
    lane = pl.BlockSpec((bm, HD), lambda i: (i, 4 * D // HD))
    vec = pl.BlockSpec((1, HD), lambda i: (0, 0))
    return _pc(body, name, (t // bm,), [row(0), row(1), row(2), lane, vec, vec, vec],
               [row(0), row(0), row(0), pl.BlockSpec((bm, HD), lambda i: (i, 0))],
               [_sds((t, D), BF16), _sds((t, D), BF16), _sds((t, D), BF16), _sds((t, HD))],
               scratch=[pltpu.VMEM((1, HD), F32)])(z1, z1, z1, z1, qg, kg, fbp)


def _foxprep_bwd(dqn, dkn, dv, dgate, z1, qg, kg, fbp, dc, name):
    t = z1.shape[0]
    bm = min(FOX_BM, t)
    nb = t // bm

    def body(dqn_ref, dkn_ref, dv_ref, dgt_ref, q_ref, k_ref, f_ref, qg_ref, kg_ref, fb_ref, dc_ref,
             dz_ref, dqg_ref, dkg_ref, dfb_ref, carry):
        @pl.when(pl.program_id(0) == 0)
        def _():
            carry[...] = jnp.zeros_like(carry)
            dqg_ref[...] = jnp.zeros_like(dqg_ref)
            dkg_ref[...] = jnp.zeros_like(dkg_ref)
            dfb_ref[...] = jnp.zeros_like(dfb_ref)

        for hd in range(NH1):
            sl = slice(hd * HD, (hd + 1) * HD)
            dx, dgr = _rms_bwd(dqn_ref[:, sl] * (HD ** -0.5), q_ref[:, sl], qg_ref[...])
            dz_ref[:, sl] = dx.astype(BF16)
            dqg_ref[...] += jnp.sum(dgr, axis=0, keepdims=True)
            dx, dgr = _rms_bwd(dkn_ref[:, sl], k_ref[:, sl], kg_ref[...])
            dz_ref[:, D + hd * HD:D + (hd + 1) * HD] = dx.astype(BF16)
            dkg_ref[...] += jnp.sum(dgr, axis=0, keepdims=True)
        dz_ref[:, 2 * D:3 * D] = dv_ref[...].astype(BF16)
        dz_ref[:, 3 * D:4 * D] = dgt_ref[...]
        dc_ = dc_ref[...]
        dlogf = _rcumsum_rows(dc_) + carry[...]
        carry[...] += jnp.sum(dc_, axis=0, keepdims=True)
        lanes = lax.broadcasted_iota(jnp.int32, dc_.shape, 1)
        df = jnp.where(lanes < NH1, dlogf * (1.0 - _sigmoid(f_ref[...] + fb_ref[...])), 0.0)
        dz_ref[:, GATE0:GATE0 + HD] = df.astype(BF16)
        dfb_ref[...] += jnp.sum(df, axis=0, keepdims=True)

    rv = lambda i: nb - 1 - i
    row = lambda c: pl.BlockSpec((bm, D), lambda i: (rv(i), c))
    lane = lambda c: pl.BlockSpec((bm, HD), lambda i: (rv(i), c))
    vec = pl.BlockSpec((1, HD), lambda i: (0, 0))
    return _pc(body, name, (nb,), [row(0), row(0), row(0), row(0), row(0), row(1), lane(4 * D // HD), vec, vec, vec, lane(0)],
               [pl.BlockSpec((bm, ZW), lambda i: (rv(i), 0)), vec, vec, vec],
               [_sds((t, ZW), BF16), _sds((1, HD)), _sds((1, HD)), _sds((1, HD))],
               scratch=[pltpu.VMEM((1, HD), F32)])(dqn, dkn, dv, dgate, z1, z1, z1, qg, kg, fbp, dc)


LOG2E = 1.4426950408889634
FOX_QSCALE = HD ** -0.5 * LOG2E


def _fox_steps(t, bq, bk, k_major):
    nq, nk = t // bq, t // bk
    pairs = [(i, j) for i in range(nq) for j in range(nk) if j * bk < (i + 1) * bq]
    if k_major:
        pairs.sort(key=lambda p: (p[1], p[0]))
    outer = [p[1] if k_major else p[0] for p in pairs]
    n = len(pairs)
    flags = [(n_ == 0 or outer[n_] != outer[n_ - 1]) + 2 * (n_ == n - 1 or outer[n_] != outer[n_ + 1])
             + 4 * (not (j + 1) * bk <= i * bq + 1) for n_, (i, j) in enumerate(pairs)]
    as_i32 = lambda v: jnp.asarray(v, jnp.int32)
    return as_i32([p[0] for p in pairs]), as_i32([p[1] for p in pairs]), as_i32(flags)


def _fox_step_info(qi_ref, kj_ref, fl_ref):
    s = pl.program_id(1)
    fl = fl_ref[s]
    return qi_ref[s], kj_ref[s], (fl & 1) != 0, (fl & 2) != 0, (fl & 4) != 0


def _fox_call(body, name, tables, in_specs, out_specs, out_shape, scratch):
    grid_spec = pltpu.PrefetchScalarGridSpec(num_scalar_prefetch=3, grid=(NH1 // FOX_HEADS_PER_STEP, tables[0].shape[0]),
                                             in_specs=in_specs, out_specs=out_specs, scratch_shapes=scratch)
    return pl.pallas_call(body, name=name, grid_spec=grid_spec, out_shape=out_shape,
                          compiler_params=pltpu.CompilerParams(dimension_semantics=("arbitrary", "arbitrary"),
                                                               vmem_limit_bytes=VMEM_LIMIT_V7X))


def _fox_lane_tiles(x):
    return [x[:, c0:c0 + HD] for c0 in range(0, x.shape[1], HD)]


def _fox_masked_scores(q, k, ck, i, j, bq, bk, masked):
    s = _nt(q, k) - ck
    if masked:
        rows = i * bq + lax.broadcasted_iota(jnp.int32, s.shape, 0)
        cols = j * bk + lax.broadcasted_iota(jnp.int32, s.shape, 1)
        s = jnp.where(cols <= rows, s, NEG)
    return s


def _fox_specs(bq, bk, G):
    qspec = pl.BlockSpec((bq, G * HD), lambda h, s, qi, kj, fl: (qi[s], h))
    kspec = pl.BlockSpec((bk, G * HD), lambda h, s, qi, kj, fl: (kj[s], h))
    cspec = pl.BlockSpec((G, 1, bk), lambda h, s, qi, kj, fl: (h, 0, kj[s]))
    colspec = pl.BlockSpec((G, bq, 1), lambda h, s, qi, kj, fl: (h, qi[s], 0))
    return qspec, kspec, cspec, colspec


def _fox_rowmax(qn, kn, crow, name):
    t = qn.shape[0]
    bq, bk, G = min(FOX_BQ, t), min(2 * FOX_BK, t), FOX_HEADS_PER_STEP
    tables = _fox_steps(t, bq, bk, k_major=False)

    def body(qi_ref, kj_ref, fl_ref, q_ref, k_ref, ck_ref, m_ref, *mp):
        i, j, first, last, diag = _fox_step_info(qi_ref, kj_ref, fl_ref)

        @pl.when(first)
        def _():
            for g in range(G):
                mp[g][...] = jnp.full_like(mp[g], NEG)

        def step(masked):
            for g in range(G):
                sl = slice(g * HD, (g + 1) * HD)
                s = _fox_masked_scores(q_ref[:, sl], k_ref[:, sl], ck_ref[g], i, j, bq, bk, masked)
                m = mp[g][...]
                for tile in _fox_lane_tiles(s):
                    m = jnp.maximum(m, tile)
                mp[g][...] = m

        pl.when(jnp.logical_not(diag))(lambda: step(False))
        pl.when(diag)(lambda: step(True))

        @pl.when(last)
        def _():
            for g in range(G):
                m_ref[g] = jnp.max(mp[g][...], axis=1, keepdims=True)

    qspec, kspec, cspec, colspec = _fox_specs(bq, bk, G)
    return _fox_call(body, name, tables, [qspec, kspec, cspec], colspec, _sds((NH1, t, 1)),
                     [pltpu.VMEM((bq, HD), F32)] * G)(*tables, qn, kn, crow)


def _fox_fwd(qn, kn, vb, crow, m, name):
    t = qn.shape[0]
    bq, bk, G = min(FOX_BQ, t), min(FOX_BK, t), FOX_HEADS_PER_STEP
    tables = _fox_steps(t, bq, bk, k_major=False)

    def body(qi_ref, kj_ref, fl_ref, q_ref, k_ref, v_ref, ck_ref, m_ref, o_ref, lse_ref, *scr):
        i, j, first, last, diag = _fox_step_info(qi_ref, kj_ref, fl_ref)
        lp, acc = scr[:G], scr[G:]

        @pl.when(first)
        def _():
            for g in range(G):
                lp[g][...] = jnp.zeros_like(lp[g])
                acc[g][...] = jnp.zeros_like(acc[g])

        def step(masked):
            for g in range(G):
                sl = slice(g * HD, (g + 1) * HD)
                s = _fox_masked_scores(q_ref[:, sl], k_ref[:, sl], ck_ref[g], i, j, bq, bk, masked)
                p = jnp.exp2(s - m_ref[g])
                l = lp[g][...]
                for tile in _fox_lane_tiles(p):
                    l = l + tile
                lp[g][...] = l
                acc[g][...] += _nn(p.astype(BF16), v_ref[:, sl])

        pl.when(jnp.logical_not(diag))(lambda: step(False))
        pl.when(diag)(lambda: step(True))

        @pl.when(last)
        def _():
            for g in range(G):
                l = jnp.sum(lp[g][...], axis=1, keepdims=True)
                o_ref[:, g * HD:(g + 1) * HD] = acc[g][...] / l
                lse_ref[g] = m_ref[g] + jnp.log2(l)

    qspec, kspec, cspec, colspec = _fox_specs(bq, bk, G)
    return _fox_call(body, name, tables, [qspec, kspec, kspec, cspec, colspec], [qspec, colspec],
                     [_sds((t, D)), _sds((NH1, t, 1))], [pltpu.VMEM((bq, HD), F32)] * (2 * G))(*tables, qn, kn, vb, crow, m)


def _fox_bwd(qn, kn, vb, crow, lse, delta, do, name):
    t = qn.shape[0]
    bq, bk, G = min(FOX_BQ, t), min(FOX_BK, t), FOX_HEADS_PER_STEP
    tables = _fox_steps(t, bq, bk, k_major=True)

    def body(qi_ref, kj_ref, fl_ref, q_ref, k_ref, v_ref, ck_ref, lse_ref, dl_ref, do_ref, dq_ref, dk_ref, dv_ref, dc_ref, dcq_ref,
             dk_s, dv_s, dc_s):
        i, j, first, last, diag = _fox_step_info(qi_ref, kj_ref, fl_ref)

        @pl.when(first)
        def _():
            dk_s[...] = jnp.zeros_like(dk_s)
            dv_s[...] = jnp.zeros_like(dv_s)
            dc_s[...] = jnp.zeros_like(dc_s)

        @pl.when(pl.program_id(1) == 0)
        def _():
            dq_ref[...] = jnp.zeros_like(dq_ref)
            dcq_ref[...] = jnp.zeros_like(dcq_ref)

        def step(masked):
            rows = pl.ds(pl.multiple_of(i * bq, bq), bq)
            for g in range(G):
                sl = slice(g * HD, (g + 1) * HD)
                q, k = q_ref[:, sl], k_ref[:, sl]
                s = _fox_masked_scores(q, k, ck_ref[g], i, j, bq, bk, masked)
                p = jnp.exp2(s - lse_ref[g])
                dob = do_ref[:, sl]
                dv_s[:, sl] += _tn(p.astype(BF16), dob)
                ds = p * (_nt(dob, v_ref[:, sl]) - dl_ref[g])
                dsb = ds.astype(BF16)
                dq_ref[rows, sl] += _nn(dsb, k)
                dk_s[:, sl] += _tn(dsb, q)
                dc_s[g] -= jnp.sum(ds, axis=0, keepdims=True)
                part_sum = dcq_ref[g, rows, :]
                for tile in _fox_lane_tiles(ds):
                    part_sum = part_sum + tile
                dcq_ref[g, rows, :] = part_sum

        pl.when(jnp.logical_not(diag))(lambda: step(False))
        pl.when(diag)(lambda: step(True))

        @pl.when(last)
        def _():
            dk_ref[...] = dk_s[...] * (1.0 / LOG2E)
            dv_ref[...] = dv_s[...]
            dc_ref[...] = dc_s[...]

    qspec, kspec, cspec, colspec = _fox_specs(bq, bk, G)
    return _fox_call(
        body, name, tables, [qspec, kspec, kspec, cspec, colspec, colspec, qspec],
        [pl.BlockSpec((t, G * HD), lambda h, s, qi, kj, fl: (0, h)), kspec, kspec, cspec,
         pl.BlockSpec((G, t, HD), lambda h, s, qi, kj, fl: (h, 0, 0))],
        [_sds((t, D)), _sds((t, D)), _sds((t, D)), _sds((NH1, 1, t)), _sds((NH1, t, HD))],
        [pltpu.VMEM((bk, G * HD), F32), pltpu.VMEM((bk, G * HD), F32), pltpu.VMEM((G, 1, bk), F32)],
    )(*tables, qn, kn, vb, crow, lse, delta, do)


def _post1_fwd(o, z1, w, h3, name, bm=512):
    t = o.shape[0]
    bm = min(bm, t)

    def body(o_ref, g_ref, w_ref, h_ref, out_ref, og_ref):
        og_ref[...] = (o_ref[...] * _sigmoid(g_ref[...])).astype(BF16)
        out_ref[...] = h_ref[...] + _nn(og_ref[...], w_ref[...])

    row = lambda c: pl.BlockSpec((bm, D), lambda i: (i, c))
    return _pc(body, name, (t // bm,), [row(0), row(3), pl.BlockSpec((D, D), lambda i: (0, 0)), row(0)],
               [row(0), row(0)], [_sds((t, D)), _sds((t, D), BF16)])(o, z1, w, h3)


def _post1_bwd(dh4, w, o, z1, name, bm=512):
    t = o.shape[0]
    bm = min(bm, t)

    def body(dh_ref, w_ref, o_ref, g_ref, do_ref, dg_ref, dl_ref):
        d_og = _nt(dh_ref[...].astype(BF16), w_ref[...])
        o_, sg = o_ref[...], _sigmoid(g_ref[...])
        dob = (d_og * sg).astype(BF16)
        do_ref[...] = dob
        dg_ref[...] = (d_og * o_ * sg * (1.0 - sg)).astype(BF16)
        prod = dob.astype(F32) * o_
        for hd in range(NH1):
            dl_ref[hd] = jnp.sum(prod[:, hd * HD:(hd + 1) * HD], axis=1, keepdims=True)

    row = lambda c: pl.BlockSpec((bm, D), lambda i: (i, c))
    return _pc(body, name, (t // bm,), [row(0), pl.BlockSpec((D, D), lambda i: (0, 0)), row(0), row(3)],
               [row(0), row(0), pl.BlockSpec((NH1, bm, 1), lambda i: (0, i, 0))],
               [_sds((t, D), BF16), _sds((t, D), BF16), _sds((NH1, t, 1))])(dh4, w, o, z1)


def _final(h, g, tgt, name, bm=512):
    t = h.shape[0]
    bm = min(bm, t)

    def body(h_ref, g_ref, t_ref, l_ref, dh_ref, dg_ref):
        @pl.when(pl.program_id(0) == 0)
        def _():
            l_ref[...] = jnp.zeros_like(l_ref)
            dg_ref[...] = jnp.zeros_like(dg_ref)

        x, gv = h_ref[...], g_ref[...]
        r = _rstd(x)
        xh = x * r
        e = xh * gv - t_ref[...]
        l_ref[...] += 0.5 * jnp.sum(jnp.mean(e * e, axis=1, keepdims=True), axis=0, keepdims=True)
        dy = e * (1.0 / D)
        dg_ref[...] += jnp.sum(dy * xh, axis=0, keepdims=True)
        dxh = dy * gv
        dh_ref[...] = r * (dxh - xh * jnp.mean(dxh * xh, axis=1, keepdims=True))

    row = pl.BlockSpec((bm, D), lambda i: (i, 0))
    vec = pl.BlockSpec((1, D), lambda i: (0, 0))
    return _pc(body, name, (t // bm,), [row, vec, row], [pl.BlockSpec((1, HD), lambda i: (0, 0)), row, vec],
               [_sds((1, HD)), _sds((t, D)), _sds((1, D))])(h, g, tgt)


def _adam(w, g, m, v, name):
    r, c = w.shape
    br = min(r, 256)

    def body(w_ref, g_ref, m_ref, v_ref, d_ref, mo_ref, vo_ref):
        gv = g_ref[...]
        mn = ADAM_B1 * m_ref[...] + (1.0 - ADAM_B1) * gv
        vn = ADAM_B2 * v_ref[...] + (1.0 - ADAM_B2) * jnp.square(gv)
        m_hat = mn / (1.0 - ADAM_B1 ** ADAM_STEP)
        v_hat = vn / (1.0 - ADAM_B2 ** ADAM_STEP)
        d_ref[...] = -ADAM_LR * (m_hat / (jnp.sqrt(v_hat) + ADAM_EPS) + ADAM_WD * w_ref[...])
        mo_ref[...] = mn
        vo_ref[...] = vn

    blk = pl.BlockSpec((br, c), lambda i: (i, 0))
    return _pc(body, name, (r // br,), [blk] * 4, [blk] * 3, [_sds((r, c))] * 3)(w, g, m, v)


ZW = 4224
GATE0 = 4096


def _pack_w_in0(w):
    return jnp.concatenate([w[:, :2048], w[:, 2056:], w[:, 2048:2056], jnp.zeros((w.shape[0], ZW - 4104), w.dtype)], axis=1)


def _unpack_w_in0(g):
    return jnp.concatenate([g[:, :2048], g[:, GATE0:GATE0 + 8], g[:, 2048:GATE0]], axis=1)


def _pack_w_in1(w):
    return jnp.concatenate([w, jnp.zeros((w.shape[0], ZW - 4104), w.dtype)], axis=1)


def _unpack_w_in1(g):
    return g[:, :4104]


def _local_step(x, mem, tgt, W, S, late_weights=None, grads_hook=None):
    t = x.shape[0]
    row = lambda v: v.reshape(1, -1)
    G = {}

    z0, u0 = _norm_mm(x, S["norm_mix_g"][0:1], W["w_in0"], "in0_fwd")
    qk = _conv_fwd(z0, S["conv_w"], "conv_fwd")
    g8 = z0[:, GATE0:GATE0 + 8]
    gates3 = jnp.stack([g8[:, :4].T, g8[:, 4:].T], axis=-1)
    gb = S["gate_b"]
    bias3 = jnp.stack([gb[0, :4], gb[0, 4:]], axis=-1)[:, None, :]
    hm, cs, ns, ms = _mlstm_fwd(qk, z0, gates3, bias3, "mlstm_fwd")
    hh, ss = _hgrn_fwd(z0, S["lb_logits"], "hgrn_fwd")
    if late_weights is not None:
        W = {**W, **late_weights(hh)}
    kv, mn = _memkv_fwd(mem, row(S["mem_norm_g"]), W["wkv_s"], "memkv_fwd")
    h1, y0 = _post0_fwd(hm, hh, z0, S["mlstm_norm_g"], S["hgrn_norm_g"], W["w_out0"], x, "post0_fwd")

    def xattn_mlp_fwd(h, l):
        q, ux = _norm_mm(h, S["norm_xattn_g"][l:l + 1], W["wq"][l], f"xq{l}_fwd")
        h2, ox = _xattn_fwd(q, kv, W["wo"][l], h, f"xattn{l}_fwd")
        h3, a, um = _mlp_fwd(h2, S["norm_mlp_g"][l:l + 1], W["w1s"], W["w2"], l, f"mlp{l}_fwd")
        return h3, (h, q, ux, ox, h2, a, um)

    h3, sv0 = xattn_mlp_fwd(h1, 0)
    z1, u1 = _norm_mm(h3, S["norm_mix_g"][1:2], W["w_in1"], "in1_fwd")
    fbp = jnp.pad(S["c_fgate_b"], ((0, 0), (0, HD - NH1)))
    qn, kn, vb, c = _foxprep_fwd(z1, S["c_qnorm_g"], S["c_knorm_g"], fbp, "foxprep_fwd")
    crow = (c[:, :NH1] * LOG2E).T[:, None, :]
    o1, lse = _fox_fwd(qn, kn, vb, crow, _fox_rowmax(qn, kn, crow, "fox_rowmax"), "fox_fwd")
    h4, og = _post1_fwd(o1, z1, W["w_out1"], h3, "post1_fwd")
    h6, sv1 = xattn_mlp_fwd(h4, 1)
    lossp, dh, G["final_norm_g"] = _final(h6, row(S["final_norm_g"]), tgt, "final")

    grads_ready, grads_next = grads_hook if grads_hook is not None else ((lambda stage, grads: 0.0), (lambda stage, after: 0.0))
    dkv = None
    dgx, dgm, dwq, dwo, dw1, dw2 = [None, None], [None, None], [None, None], [None, None], [None, None], [None, None]

    def xattn_mlp_bwd(dh, l, sv, tok=0.0):
        nonlocal dkv
        h, q, ux, ox, h2, a, um = sv
        dh2, da, r, dgm[l] = _mlp_bwd(dh, a, W["w1s"], W["w2"], l, h2, S["norm_mlp_g"][l:l + 1] + tok, f"mlp{l}_bwd")
        dw1[l] = _mm_tn(um, da, f"mlp{l}_dw1", col_chips=NCHIP)
        dw2[l] = _mm_tn(r, dh, f"mlp{l}_dw2")
        dq, dkv_l = _xattn_bwd(dh2, q, kv, W["wo"][l], f"xattn{l}_bwd")
        dkv = dkv_l if dkv is None else dkv + dkv_l
        dwo[l] = _mm_tn(ox, dh2, f"xattn{l}_dwo")
        dwq[l] = _mm_tn(ux, dq, f"xattn{l}_dwq")
        dh1, dgx[l] = _bwd_in(dq, W["wq"][l], h, S["norm_xattn_g"][l:l + 1], dh2, f"xq{l}_bwd")
        return dh1

    dh4 = xattn_mlp_bwd(dh, 1, sv1)
    do, dgate, delta = _post1_bwd(dh4, W["w_out1"], o1, z1, "post1_bwd")
    G["w_out1"] = _mm_tn(og, dh4, "post1_dw")
    dqn, dkn, dv1, dcrow, dcq = _fox_bwd(qn, kn, vb, crow, lse, delta, do, "fox_bwd")
    dc = jnp.pad((dcrow[:, 0, :] + jnp.sum(dcq, axis=-1)).T, ((0, 0), (0, HD - NH1)))
    dz1, G["c_qnorm_g"], G["c_knorm_g"], dfb = _foxprep_bwd(
        dqn, dkn, dv1, dgate, z1, S["c_qnorm_g"], S["c_knorm_g"], fbp, dc, "foxprep_bwd")
    G["c_fgate_b"] = dfb[:, :NH1]
    G["w_in1"] = _mm_tn(u1, dz1, "in1_dw")
    tok = grads_ready("layer1", dict(w_out=G["w_out1"], w_in=G["w_in1"], wq=dwq[1], wo=dwo[1], w1=dw1[1], w2=dw2[1]))
    dh3, dgmix1 = _bwd_in(dz1, W["w_in1"], h3, S["norm_mix_g"][1:2] + tok, dh4, "in1_bwd")
    dh1 = xattn_mlp_bwd(dh3, 0, sv0, grads_next("layer1", dh3))

    G["wkv"] = _mm_tn(mn, dkv, "memkv_dw", col_chips=NCHIP)
    G["mem_norm_g"] = _memkv_bwd(dkv, W["wkv_s"], mem, row(S["mem_norm_g"]), "memkv_bwd")
    G["w_out0"] = _mm_tn(y0, dh1, "post0_dw")
    tok = grads_ready("layer0", dict(wq=dwq[0], wo=dwo[0], w1=dw1[0], w2=dw2[0], wkv=G["wkv"], w_out=G["w_out0"]))
    dhm, dhh, doa, dgb, G["mlstm_norm_g"], G["hgrn_norm_g"] = _post0_bwd(
        dh1, W["w_out0"], hm, hh, z0, S["mlstm_norm_g"] + tok, S["hgrn_norm_g"], "post0_bwd")
    dqka, dva, dgates3 = _mlstm_bwd(qk, z0, gates3, bias3 + grads_next("layer0", dhm), cs, ns, ms, dhm, "mlstm_bwd")
    dqb, dfb0, dib, G["lb_logits"] = _hgrn_bwd(z0, S["lb_logits"], ss, dhh, "hgrn_bwd")
    duc, G["conv_w"] = _conv_bwd(z0, S["conv_w"], dqka, "conv_bwd")
    dg8 = jnp.concatenate([dgates3[:, :, 0].T, dgates3[:, :, 1].T], axis=1)
    G["gate_b"] = jnp.sum(dg8, axis=0, keepdims=True)
    dz0 = jnp.concatenate([duc, dva.astype(BF16), doa, dqb, dfb0, dib, dgb,
                           jnp.pad(dg8, ((0, 0), (0, HD - 8))).astype(BF16)], axis=1)
    G["w_in0"] = _mm_tn(u0, dz0, "in0_dw")
    tok = grads_ready("in0", dict(w_in=G["w_in0"]))
    dx, dgmix0 = _bwd_in(dz0, W["w_in0"], x, S["norm_mix_g"][0:1] + tok, dh1, "in0_bwd")

    G["norm_mix_g"] = jnp.concatenate([dgmix0, dgmix1], axis=0)
    G["norm_xattn_g"] = jnp.concatenate(dgx, axis=0)
    G["norm_mlp_g"] = jnp.concatenate(dgm, axis=0)
    G["wq"], G["wo"], G["w1"], G["w2"] = dwq, dwo, dw1, dw2
    return lossp[0, 0], dx, G


ANY = pl.BlockSpec(memory_space=pl.ANY)
NCHIP = 4


def _place():
    x, y, c = lax.axis_index("x"), lax.axis_index("y"), lax.axis_index("c")
    return x, y, c, [(1 - x, y), (x, 1 - y), (1 - x, 1 - y)]


def _comm_call(body, name, ins, out_shapes, sems):
    return pl.pallas_call(body, name=name, in_specs=[ANY] * len(ins), out_specs=[ANY] * len(out_shapes),
                          out_shape=out_shapes, scratch_shapes=sems)(*ins)


HBM = pl.BlockSpec(memory_space=pltpu.HBM)
SEM = pl.BlockSpec(memory_space=pltpu.SEMAPHORE)
DATAFLOW = pltpu.SideEffectType.DATAFLOW_SIDE_EFFECTING


def _half_rows(r, cc):
    return pl.ds(pl.multiple_of(cc * (r // 2), r // 2), r // 2)


def _gather_start(arrs, after, name):
    n = len(arrs)

    def body(*refs):
        ins, lands = refs[:n], refs[n:2 * n]
        send, recv, token = refs[2 * n + 1], refs[2 * n + 2], refs[-1]
        x, y, c, chips = _place()
        me = 2 * x + y
        for a in range(n):
            rows = _half_rows(arrs[a].shape[0], c)
            for k, (px, py) in enumerate(chips):
                pltpu.make_async_remote_copy(src_ref=ins[a].at[rows], dst_ref=lands[a].at[me, rows], send_sem=send.at[3 * a + k],
                                             recv_sem=recv.at[3 * a + k], device_id=(px, py, c), device_id_type=MESH).start()
        token[...] = jnp.zeros_like(token)

    hbm = lambda v: pltpu.with_memory_space_constraint(v, pltpu.HBM)
    land_shapes = [((NCHIP,) + a.shape, a.dtype) for a in arrs]
    out = pl.pallas_call(
        body, name=name,
        out_shape=(pltpu.SemaphoreType.DMA((3 * n,)), pltpu.SemaphoreType.DMA((3 * n,)), *[pltpu.HBM(a.shape, a.dtype) for a in arrs],
                   *[pltpu.HBM(s, d) for s, d in land_shapes], _sds((8, HD))),
        in_specs=[HBM] * (2 * n) + [ANY], out_specs=(SEM, SEM, *[HBM] * (2 * n), pl.BlockSpec(memory_space=pltpu.VMEM)),
        input_output_aliases={i: 2 + i for i in range(2 * n)},
        compiler_params=pltpu.CompilerParams(has_side_effects=DATAFLOW),
    )(*[hbm(a) for a in arrs], *[hbm(lax.empty(s, d)) for s, d in land_shapes], after)
    return out[0], out[1], list(out[2:2 + n]), list(out[2 + n:2 + 2 * n]), out[-1]


def _gather_wait(send, recv, srcs, lands, after, name):
    n = len(srcs)

    def body(*refs):
        ins, lands_ = refs[:n], refs[n:2 * n]
        send_, recv_ = refs[2 * n], refs[2 * n + 1]
        x, y, c, chips = _place()
        for a in range(n):
            rows = _half_rows(srcs[a].shape[0], c)
            for k, (px, py) in enumerate(chips):
                cp = pltpu.make_async_remote_copy(src_ref=ins[a].at[rows], dst_ref=lands_[a].at[2 * px + py, rows], send_sem=send_.at[3 * a + k],
                                                  recv_sem=recv_.at[3 * a + k], device_id=(px, py, c), device_id_type=MESH)
                cp.wait_send()
                cp.wait_recv()

    out = pl.pallas_call(
        body, name=name, out_shape=[pltpu.HBM(v.shape, v.dtype) for v in list(srcs) + list(lands)],
        in_specs=[HBM] * (2 * n) + [SEM, SEM, ANY], out_specs=[HBM] * (2 * n), input_output_aliases={i: i for i in range(2 * n)},
        compiler_params=pltpu.CompilerParams(has_side_effects=DATAFLOW),
    )(*srcs, *lands, send, recv, after)
    return list(out[n:])


def _pair_forward(lands, name):
    n = len(lands)

    def body(*refs):
        ins, outs = refs[:n], refs[n:2 * n]
        send, recv = refs[2 * n:]
        x, y, c, chips = _place()
        copies = []
        for a in range(n):
            r = lands[a].shape[1]
            for k, (px, py) in enumerate(chips):
                cp = pltpu.make_async_remote_copy(
                    src_ref=ins[a].at[2 * px + py, _half_rows(r, c)], dst_ref=outs[a].at[2 * px + py, _half_rows(r, c)],
                    send_sem=send.at[a, k], recv_sem=recv.at[a, k], device_id=(x, y, 1 - c), device_id_type=MESH)
                cp.start()
                copies.append(cp)
        for a in range(n):
            r = lands[a].shape[1]
            for k, (px, py) in enumerate(chips):
                pltpu.make_async_remote_copy(
                    src_ref=ins[a].at[2 * px + py, _half_rows(r, c)], dst_ref=outs[a].at[2 * px + py, _half_rows(r, 1 - c)],
                    send_sem=send.at[a, k], recv_sem=recv.at[a, k], device_id=(x, y, 1 - c), device_id_type=MESH).wait_recv()
        for cp in copies:
            cp.wait_send()

    return pl.pallas_call(body, name=name, in_specs=[ANY] * n, out_specs=[ANY] * n, out_shape=[_sds(v.shape, v.dtype) for v in lands],
                          scratch_shapes=[pltpu.SemaphoreType.DMA((n, 3)), pltpu.SemaphoreType.DMA((n, 3))],
                          input_output_aliases={i: i for i in range(n)})(*lands)


def _pair_exchange(arrs, name):
    n = len(arrs)

    def body(*refs):
        ins, outs = refs[:n], refs[n:2 * n]
        send, recv = refs[2 * n:]
        x, y, c, _ = _place()
        copies = []
        for a in range(n):
            h = arrs[a].shape[1] // 2
            cp = pltpu.make_async_remote_copy(src_ref=ins[a].at[:, pl.ds(pl.multiple_of((1 - c) * h, h), h)], dst_ref=outs[a],
                                              send_sem=send.at[a], recv_sem=recv.at[a], device_id=(x, y, 1 - c), device_id_type=MESH)
            cp.start()
            copies.append(cp)
        for cp in copies:
            cp.wait()

    return _comm_call(body, name, arrs, [_sds((a.shape[0], a.shape[1] // 2, a.shape[2]), a.dtype) for a in arrs],
                      [pltpu.SemaphoreType.DMA((n,)), pltpu.SemaphoreType.DMA((n,))])


def _pair_exchange_start(arrs, name):
    n = len(arrs)
    land_shapes = [((a.shape[0], a.shape[1] // 2, a.shape[2]), a.dtype) for a in arrs]

    def body(*refs):
        ins, lands = refs[:n], refs[n:2 * n]
        send, recv, token = refs[2 * n], refs[2 * n + 1], refs[-1]
        x, y, c, _ = _place()
        for a in range(n):
            h = arrs[a].shape[1] // 2
            pltpu.make_async_remote_copy(src_ref=ins[a].at[:, pl.ds(pl.multiple_of((1 - c) * h, h), h)], dst_ref=lands[a],
                                         send_sem=send.at[a], recv_sem=recv.at[a], device_id=(x, y, 1 - c), device_id_type=MESH).start()
        token[...] = jnp.zeros_like(token)

    hbm = lambda v: pltpu.with_memory_space_constraint(v, pltpu.HBM)
    out = pl.pallas_call(
        body, name=name,
        out_shape=(pltpu.SemaphoreType.DMA((n,)), pltpu.SemaphoreType.DMA((n,)), *[pltpu.HBM(a.shape, a.dtype) for a in arrs],
                   *[pltpu.HBM(s, d) for s, d in land_shapes], _sds((8, HD))),
        in_specs=[HBM] * (2 * n), out_specs=(SEM, SEM, *[HBM] * (2 * n), pl.BlockSpec(memory_space=pltpu.VMEM)),
        input_output_aliases={i: 2 + i for i in range(2 * n)},
        compiler_params=pltpu.CompilerParams(has_side_effects=DATAFLOW),
    )(*[hbm(a) for a in arrs], *[hbm(lax.empty(s, d)) for s, d in land_shapes])
    return out[0], out[1], list(out[2:2 + n]), list(out[2 + n:2 + 2 * n]), out[-1]


def _pair_exchange_wait(send, recv, srcs, lands, after, name):
    n = len(srcs)

    def body(*refs):
        ins, lands_ = refs[:n], refs[n:2 * n]
        send_, recv_ = refs[2 * n], refs[2 * n + 1]
        x, y, c, _ = _place()
        for a in range(n):
            h = srcs[a].shape[1] // 2
            cp = pltpu.make_async_remote_copy(src_ref=ins[a].at[:, pl.ds(pl.multiple_of((1 - c) * h, h), h)], dst_ref=lands_[a],
                                              send_sem=send_.at[a], recv_sem=recv_.at[a], device_id=(x, y, 1 - c), device_id_type=MESH)
            cp.wait_send()
            cp.wait_recv()

    out = pl.pallas_call(
        body, name=name, out_shape=[pltpu.HBM(v.shape, v.dtype) for v in list(srcs) + list(lands)],
        in_specs=[HBM] * (2 * n) + [SEM, SEM, ANY], out_specs=[HBM] * (2 * n), input_output_aliases={i: i for i in range(2 * n)},
        compiler_params=pltpu.CompilerParams(has_side_effects=DATAFLOW),
    )(*srcs, *lands, send, recv, after)
    return list(out[:n]), list(out[n:])


def _chip_exchange_start(arrs, name):
    n = len(arrs)

    def body(*refs):
        ins, lands = refs[:n], refs[n:2 * n]
        send, recv, token = refs[2 * n], refs[2 * n + 1], refs[-1]
        x, y, c, chips = _place()
        me = 2 * x + y
        for a in range(n):
            for k, (px, py) in enumerate(chips):
                pltpu.make_async_remote_copy(src_ref=ins[a].at[2 * px + py], dst_ref=lands[a].at[me], send_sem=send.at[3 * a + k],
                                             recv_sem=recv.at[3 * a + k], device_id=(px, py, c), device_id_type=MESH).start()
        token[...] = jnp.zeros_like(token)

    hbm = lambda v: pltpu.with_memory_space_constraint(v, pltpu.HBM)
    out = pl.pallas_call(
        body, name=name,
        out_shape=(pltpu.SemaphoreType.DMA((3 * n,)), pltpu.SemaphoreType.DMA((3 * n,)), *[pltpu.HBM(a.shape, a.dtype) for a in arrs],
                   *[pltpu.HBM(a.shape, a.dtype) for a in arrs], _sds((8, HD))),
        in_specs=[HBM] * (2 * n), out_specs=(SEM, SEM, *[HBM] * (2 * n), pl.BlockSpec(memory_space=pltpu.VMEM)),
        input_output_aliases={i: 2 + i for i in range(2 * n)},
        compiler_params=pltpu.CompilerParams(has_side_effects=DATAFLOW),
    )(*[hbm(a) for a in arrs], *[hbm(lax.empty(a.shape, a.dtype)) for a in arrs])
    return out[0], out[1], list(out[2:2 + n]), list(out[2 + n:2 + 2 * n]), out[-1]


def _chip_exchange_wait(send, recv, srcs, lands, after, name):
    n = len(srcs)

    def body(*refs):
        ins, lands_ = refs[:n], refs[n:2 * n]
        send_, recv_ = refs[2 * n], refs[2 * n + 1]
        x, y, c, chips = _place()
        for a in range(n):
            for k, (px, py) in enumerate(chips):
                cp = pltpu.make_async_remote_copy(src_ref=ins[a].at[2 * px + py], dst_ref=lands_[a].at[2 * px + py], send_sem=send_.at[3 * a + k],
                                                  recv_sem=recv_.at[3 * a + k], device_id=(px, py, c), device_id_type=MESH)
                cp.wait_send()
                cp.wait_recv()

    out = pl.pallas_call(
        body, name=name, out_shape=[pltpu.HBM(v.shape, v.dtype) for v in list(srcs) + list(lands)],
        in_specs=[HBM] * (2 * n) + [SEM, SEM, ANY], out_specs=[HBM] * (2 * n), input_output_aliases={i: i for i in range(2 * n)},
        compiler_params=pltpu.CompilerParams(has_side_effects=DATAFLOW),
    )(*srcs, *lands, send, recv, after)
    return list(out[n:])


def _pair_swap_start(arrs, name):
    n = len(arrs)

    def body(*refs):
        ins, lands = refs[:n], refs[n:2 * n]
        send, recv, token = refs[2 * n], refs[2 * n + 1], refs[-1]
        x, y, c, _ = _place()
        for a in range(n):
            pltpu.make_async_remote_copy(src_ref=ins[a], dst_ref=lands[a], send_sem=send.at[a], recv_sem=recv.at[a],
                                         device_id=(x, y, 1 - c), device_id_type=MESH).start()
        token[...] = jnp.zeros_like(token)

    hbm = lambda v: pltpu.with_memory_space_constraint(v, pltpu.HBM)
    out = pl.pallas_call(
        body, name=name,
        out_shape=(pltpu.SemaphoreType.DMA((n,)), pltpu.SemaphoreType.DMA((n,)), *[pltpu.HBM(a.shape, a.dtype) for a in arrs],
                   *[pltpu.HBM(a.shape, a.dtype) for a in arrs], _sds((8, HD))),
        in_specs=[HBM] * (2 * n), out_specs=(SEM, SEM, *[HBM] * (2 * n), pl.BlockSpec(memory_space=pltpu.VMEM)),
        input_output_aliases={i: 2 + i for i in range(2 * n)},
        compiler_params=pltpu.CompilerParams(has_side_effects=DATAFLOW),
    )(*[hbm(a) for a in arrs], *[hbm(lax.empty(a.shape, a.dtype)) for a in arrs])
    return out[0], out[1], list(out[2:2 + n]), list(out[2 + n:2 + 2 * n]), out[-1]


def _pair_swap_wait(send, recv, srcs, lands, after, name):
    n = len(srcs)

    def body(*refs):
        ins, lands_ = refs[:n], refs[n:2 * n]
        send_, recv_ = refs[2 * n], refs[2 * n + 1]
        x, y, c, _ = _place()
        for a in range(n):
            cp = pltpu.make_async_remote_copy(src_ref=ins[a], dst_ref=lands_[a], send_sem=send_.at[a], recv_sem=recv_.at[a],
                                              device_id=(x, y, 1 - c), device_id_type=MESH)
            cp.wait_send()
            cp.wait_recv()

    out = pl.pallas_call(
        body, name=name, out_shape=[pltpu.HBM(v.shape, v.dtype) for v in list(srcs) + list(lands)],
        in_specs=[HBM] * (2 * n) + [SEM, SEM, ANY], out_specs=[HBM] * (2 * n), input_output_aliases={i: i for i in range(2 * n)},
        compiler_params=pltpu.CompilerParams(has_side_effects=DATAFLOW),
    )(*srcs, *lands, send, recv, after)
    return list(out[:n]), list(out[n:])


def _all_gather_devices(v, name):
    def body(v_ref, o_ref, send, recv, loc):
        x, y, c, _ = _place()
        me = 4 * x + 2 * y + c
        own = pltpu.make_async_copy(v_ref, o_ref.at[me], loc)
        own.start()
        copies = [own]
        for k in range(1, 8):
            fx, fy, fc = (k >> 2) & 1, (k >> 1) & 1, k & 1
            peer = (x ^ fx, y ^ fy, c ^ fc)
            r = pltpu.make_async_remote_copy(src_ref=v_ref, dst_ref=o_ref.at[me], send_sem=send.at[k - 1],
                                             recv_sem=recv.at[k - 1], device_id=peer, device_id_type=MESH)
            r.start()
            copies.append(r)
        for cp in copies:
            cp.wait()

    return _comm_call(body, name, [v], [_sds((8,) + v.shape, v.dtype)],
                      [pltpu.SemaphoreType.DMA((7,)), pltpu.SemaphoreType.DMA((7,)), pltpu.SemaphoreType.DMA])[0]


def _row_tile(r):
    return next((b for b in (512, 384, 256, 128, 64, 32, 16) if r % b == 0), r)


def _add2(a, b, out_dtype, name):
    r, w = a.shape
    br = _row_tile(r)

    def body(a_ref, b_ref, o_ref):
        o_ref[...] = (a_ref[...].astype(F32) + b_ref[...].astype(F32)).astype(out_dtype)

    blk = pl.BlockSpec((br, w), lambda i: (i, 0))
    return _pc(body, name, (r // br,), [blk, blk], blk, _sds((r, w), out_dtype))(a, b)


def _sum_slots(a, out_dtype, name):
    n, r, w = a.shape
    br = _row_tile(r)

    def body(a_ref, o_ref):
        acc = a_ref[0].astype(F32)
        for s in range(1, n):
            acc = acc + a_ref[s].astype(F32)
        o_ref[...] = acc.astype(out_dtype)

    return _pc(body, name, (r // br,), [pl.BlockSpec((n, br, w), lambda i: (0, i, 0))], pl.BlockSpec((br, w), lambda i: (i, 0)),
               _sds((r, w), out_dtype))(a)


SMALL = ["norm_mix_g", "norm_xattn_g", "norm_mlp_g", "final_norm_g", "mem_norm_g", "hgrn_lb_logits", "mlstm_norm_g",
         "hgrn_norm_g", "c_qnorm_g", "c_knorm_g", "ab_gate_b", "c_fgate_b"]
SMALL_ROWS = 16


def _pack_small(parts):
    flat = jnp.concatenate([p.reshape(-1).astype(F32) for p in parts])
    return jnp.pad(flat, (0, SMALL_ROWS * D - flat.shape[0])).reshape(SMALL_ROWS, D)


def _unpack_small(buf, shapes):
    flat, out, off = buf.reshape(-1), [], 0
    for s in shapes:
        n = 1
        for d in s:
            n *= d
        out.append(flat[off:off + n].reshape(s))
        off += n
    return out


def kernel(x, mem, norm_mix_g, norm_xattn_g, norm_mlp_g, final_norm_g, ab_w_in, ab_conv_w, ab_gate_b, hgrn_lb_logits, mlstm_norm_g, hgrn_norm_g, ab_w_out, c_w_in, c_fgate_b, c_qnorm_g, c_knorm_g, c_w_out, mem_norm_g, mem_w_kv, xa_w_q, xa_w_o, mlp_w1, mlp_w2, loss_target, m_norm_mix_g, m_norm_xattn_g, m_norm_mlp_g, m_final_norm_g, m_ab_w_in, m_ab_conv_w, m_ab_gate_b, m_hgrn_lb_logits, m_mlstm_norm_g, m_hgrn_norm_g, m_ab_w_out, m_c_w_in, m_c_fgate_b, m_c_qnorm_g, m_c_knorm_g, m_c_w_out, m_mem_norm_g, m_mem_w_kv, m_xa_w_q, m_xa_w_o, m_mlp_w1, m_mlp_w2, v_norm_mix_g, v_norm_xattn_g, v_norm_mlp_g, v_final_norm_g, v_ab_w_in, v_ab_conv_w, v_ab_gate_b, v_hgrn_lb_logits, v_mlstm_norm_g, v_hgrn_norm_g, v_ab_w_out, v_c_w_in, v_c_fgate_b, v_c_qnorm_g, v_c_knorm_g, v_c_w_out, v_mem_norm_g, v_mem_w_kv, v_xa_w_q, v_xa_w_o, v_mlp_w1, v_mlp_w2):
    A = dict(locals())
    chip = 2 * lax.axis_index("x") + lax.axis_index("y")

    big = ["ab_w_in", "c_w_in", "ab_w_out", "c_w_out", "mem_w_kv", "xa_w_q", "xa_w_o", "mlp_w1", "mlp_w2"]
    shard2d = {"ab_w_in": (D, 1026), "c_w_in": (D, 1026), "ab_w_out": (256, D), "c_w_out": (256, D), "mem_w_kv": (D, 512),
               "xa_w_q": (512, D), "xa_w_o": (512, D), "mlp_w1": (2 * D, D), "mlp_w2": (2 * D, D)}
    own_slot = lambda gs, os: [lax.dynamic_update_index_in_dim(g, o, chip, 0) for g, o in zip(gs, os)]
    cols = lambda g: jnp.concatenate([g[k] for k in range(NCHIP)], axis=1)
    per_layer = lambda g: g.reshape(NCHIP, 2, -1, D).transpose(1, 0, 2, 3)
    first = [A["ab_w_in"].reshape(shard2d["ab_w_in"]).astype(BF16), jnp.pad(ab_conv_w[0], ((0, 16 - CONV_W), (0, 0)))]
    *first_handles, tok_first = _gather_start(first, ab_conv_w, "gather_first_start")
    rest_names = ["c_w_in", "ab_w_out", "c_w_out", "xa_w_q", "xa_w_o", "mlp_w1", "mlp_w2", "mem_w_kv"]
    rest = [(A[n].reshape(shard2d[n]) + tok_first[0, 0]).astype(BF16) for n in rest_names]
    send_s, recv_s, srcs, lands, token = _gather_start(rest, tok_first, "gather_rest_start")
    g_in0, g_conv = own_slot(_pair_forward(_gather_wait(*first_handles, token, "gather_first_wait"), "gather_first_forward"), first)
    W = dict(w_in0=_pack_w_in0(cols(g_in0)))

    def late_weights(after):
        got = _pair_forward(_gather_wait(send_s, recv_s, srcs, lands, after, "gather_rest_wait"), "gather_rest_forward")
        gw = dict(zip(rest_names, own_slot(got, rest)))
        return dict(w_in1=_pack_w_in1(cols(gw["c_w_in"])), w_out0=gw["ab_w_out"].reshape(D, D), w_out1=gw["c_w_out"].reshape(D, D),
                    wkv_s=gw["mem_w_kv"],
                    wq=per_layer(gw["xa_w_q"]).reshape(2, D, D), wo=per_layer(gw["xa_w_o"]).reshape(2, D, D),
                    w1s=gw["mlp_w1"].reshape(NCHIP, 2, D, D), w2=gw["mlp_w2"].reshape(NCHIP, 2, D, D))

    S = dict(norm_mix_g=norm_mix_g + token[0, 0], norm_xattn_g=norm_xattn_g, norm_mlp_g=norm_mlp_g, final_norm_g=final_norm_g,
             conv_w=cols(g_conv[:, :CONV_W]), gate_b=ab_gate_b, lb_logits=hgrn_lb_logits, mlstm_norm_g=mlstm_norm_g,
             hgrn_norm_g=hgrn_norm_g, c_fgate_b=c_fgate_b, c_qnorm_g=c_qnorm_g, c_knorm_g=c_knorm_g, mem_norm_g=mem_norm_g)

    core = lax.axis_index("c")
    by_rows = lambda g: g.reshape(NCHIP, -1, D)

    def stack_cols(g):
        return jnp.stack([g[:, 1026 * k:1026 * (k + 1)] for k in range(NCHIP)])

    def pair_sums(arrs, theirs, tag):
        out = []
        for i, (a, th) in enumerate(zip(arrs, theirs)):
            h = a.shape[1] // 2
            mine = lax.dynamic_slice_in_dim(a, core * h, h, axis=1)
            out.append(_add2(mine.reshape(-1, a.shape[2]), th.reshape(-1, a.shape[2]), BF16, f"pair_sum_{tag}{i}").reshape(th.shape))
        return out

    def start_chip_exchange(stage, psums):
        *handles, token = _chip_exchange_start(psums, f"chip_exchange_start_{stage}")
        started[stage] = (psums, handles)
        return token[0, 0]

    def chip_sums(psums, from_chips, tag):
        out = []
        for i, (f, p) in enumerate(zip(from_chips, psums)):
            f = lax.dynamic_update_index_in_dim(f, lax.dynamic_index_in_dim(p, chip, 0, keepdims=False), chip, 0)
            out.append(_sum_slots(f, F32, f"chip_sum_{tag}{i}"))
        return out

    started, pending = {}, {}

    def grads_ready(stage, g):
        if stage == "in0":
            arrs = [stack_cols(_unpack_w_in0(g["w_in"]))]
            return start_chip_exchange(stage, pair_sums(arrs, _pair_exchange(arrs, f"pair_exchange_{stage}"), stage))
        arrs = [jnp.concatenate([by_rows(g["w_out"]), by_rows(g["wq"]), by_rows(g["wo"]), g["w1"], by_rows(g["w2"])], axis=1),
                stack_cols(_unpack_w_in1(g["w_in"])) if stage == "layer1" else g["wkv"]]
        *pending[stage], token = _pair_exchange_start(arrs, f"pair_exchange_start_{stage}")
        return token[0, 0]

    def grads_next(stage, after):
        arrs, theirs = _pair_exchange_wait(*pending[stage], after, f"pair_exchange_wait_{stage}")
        return start_chip_exchange(stage, pair_sums(arrs, theirs, stage))

    lossp, dx, G = _local_step(x[0], mem[0], loss_target[0], W, S, late_weights, (grads_ready, grads_next))

    gsmall = {"norm_mix_g": G["norm_mix_g"], "norm_xattn_g": G["norm_xattn_g"], "norm_mlp_g": G["norm_mlp_g"],
              "final_norm_g": G["final_norm_g"], "mem_norm_g": G["mem_norm_g"], "hgrn_lb_logits": G["lb_logits"],
              "mlstm_norm_g": G["mlstm_norm_g"], "hgrn_norm_g": G["hgrn_norm_g"], "c_qnorm_g": G["c_qnorm_g"],
              "c_knorm_g": G["c_knorm_g"], "ab_gate_b": G["gate_b"], "c_fgate_b": G["c_fgate_b"]}
    rhalf = []
    for stage in ("layer1", "layer0", "in0"):
        psums, handles = started[stage]
        rhalf += chip_sums(psums, _chip_exchange_wait(*handles, dx, f"chip_exchange_wait_{stage}"), stage)
    *swap_handles, tok_swap = _pair_swap_start(rhalf, "pair_swap_start")

    packed = _pack_small([gsmall[n] for n in SMALL] + [G["conv_w"], lossp]) + tok_swap[0, 0]
    red = _sum_slots(_all_gather_devices(packed, "gather_small"), F32, "sum_small")
    small_shapes = [A[n].shape for n in SMALL]
    *gs, gconv, loss = _unpack_small(red, small_shapes + [(CONV_W, D), ()])
    gs = dict(zip(SMALL, gs))
    gconv = lax.dynamic_slice_in_dim(gconv, chip * 256, 256, axis=1)[None]
    sd, sm, sv = _adam(_pack_small([A[n] for n in SMALL]), _pack_small([gs[n] for n in SMALL]),
                       _pack_small([A["m_" + n] for n in SMALL]), _pack_small([A["v_" + n] for n in SMALL]), "adam_small")
    cd, cm_, cv = _adam(ab_conv_w[0], gconv[0], m_ab_conv_w[0], v_ab_conv_w[0], "adam_conv")

    rhalf, other = _pair_swap_wait(*swap_handles, cd, "pair_swap_wait")
    r_l1, r_in1, r_l0, r_kv, r_in0 = [
        jnp.where(core == 0, jnp.concatenate([m_, o_], axis=0), jnp.concatenate([o_, m_], axis=0)) for m_, o_ in zip(rhalf, other)]
    both = lambda lo, hi: jnp.concatenate([r_l0[lo:hi], r_l1[lo:hi]], axis=0)
    gbig = {"ab_w_in": r_in0, "c_w_in": r_in1, "mem_w_kv": r_kv, "ab_w_out": r_l0[0:256], "c_w_out": r_l1[0:256],
            "xa_w_q": both(256, 512), "xa_w_o": both(512, 768), "mlp_w1": both(768, 1792), "mlp_w2": both(1792, 2816)}

    out_g, out_d, out_m, out_v = {}, {}, {}, {}
    for n in big:
        d_, m_, v_ = _adam(A[n].reshape(shard2d[n]), gbig[n], A["m_" + n].reshape(shard2d[n]), A["v_" + n].reshape(shard2d[n]), "adam_" + n)
        out_g[n] = gbig[n].reshape(A[n].shape)
        out_d[n], out_m[n], out_v[n] = d_.reshape(A[n].shape), m_.reshape(A[n].shape), v_.reshape(A[n].shape)
    for n, d_, m_, v_ in zip(SMALL, _unpack_small(sd, small_shapes), _unpack_small(sm, small_shapes), _unpack_small(sv, small_shapes)):
        out_g[n], out_d[n], out_m[n], out_v[n] = gs[n], d_, m_, v_
    out_g["ab_conv_w"], out_d["ab_conv_w"], out_m["ab_conv_w"], out_v["ab_conv_w"] = gconv, cd[None], cm_[None], cv[None]

    order = ["norm_mix_g", "norm_xattn_g", "norm_mlp_g", "final_norm_g", "ab_w_in", "ab_conv_w", "ab_gate_b", "hgrn_lb_logits",
             "mlstm_norm_g", "hgrn_norm_g", "ab_w_out", "c_w_in", "c_fgate_b", "c_qnorm_g", "c_knorm_g", "c_w_out", "mem_norm_g",
             "mem_w_kv", "xa_w_q", "xa_w_o", "mlp_w1", "mlp_w2"]
    return (loss, dx[None], *[out_g[n] for n in order], *[out_d[n] for n in order], *[out_m[n] for n in order],
            *[out_v[n] for n in order])
```
